```python
import jax
import jax.numpy as jnp
from jax import lax
import numpy as np

D_MODEL = 1024
BATCH = 8
SEQ = 4096
DEPTH = 1

MIX_WIDTH = D_MODEL
HG_WIDTH = MIX_WIDTH // 2
HG_HEAD_DIM = 128
HG_HEADS = HG_WIDTH // HG_HEAD_DIM
HG_EXPAND = 128
HG_FDIM = HG_HEADS * HG_EXPAND
HG_CHUNK = 64
ATT_WIDTH = MIX_WIDTH - HG_WIDTH
ATT_HEAD_DIM = 64
ATT_HEADS = ATT_WIDTH // ATT_HEAD_DIM
DILATED_PAIRS = ((128, 1), (512, 4), (2048, 16))
ATT_BLOCK = 128
D_FF = ((8 * D_MODEL + 3 * 256 - 1) // (3 * 256)) * 256
RMS_EPS = 1e-6
IN_SIZES = (HG_FDIM, HG_FDIM, HG_WIDTH, HG_WIDTH, ATT_WIDTH, ATT_WIDTH, ATT_WIDTH)
IN_WIDTH = HG_FDIM * 2 + HG_WIDTH * 2 + ATT_WIDTH * 3

kernel_name = 'hybrid_hgrn2_dilated_attn_adaln_block'


def rms_norm(x, g):
    xf = x.astype(jnp.float32)
    y = xf * lax.rsqrt(jnp.mean(xf * xf, axis=-1, keepdims=True) + RMS_EPS)
    return (y * g.astype(jnp.float32)).astype(x.dtype)


def modulate(h, shift, scale):
    return h * (1 + scale[:, None, :]) + shift[:, None, :]


def hgrn2_mixer(q, f_raw, i, g, lb, onorm_g):
    B, S = q.shape[0], q.shape[1]
    nc = S // HG_CHUNK
    lb = lb.reshape(HG_HEADS, HG_EXPAND)
    f = lb + (1.0 - lb) * jax.nn.sigmoid(f_raw.astype(jnp.float32))
    k = 1.0 - f
    log_f = jnp.log(f)
    qf = jax.nn.silu(q.astype(jnp.float32))
    vf = i.astype(jnp.float32)

    def chunks(t):
        return t.reshape(B, nc, HG_CHUNK, HG_HEADS, t.shape[-1]).transpose(1, 0, 3, 2, 4)

    qc, kc, vc = chunks(qf), chunks(k), chunks(vf)
    bc = jnp.cumsum(chunks(log_f), axis=3)
    causal = jnp.tril(jnp.ones((HG_CHUNK, HG_CHUNK), dtype=bool))

    def step(state, inp):
        q_c, k_c, v_c, b_c = inp
        o_inter = jnp.einsum('bhtk,bhkv->bhtv', q_c * jnp.exp(b_c), state)
        diff = b_c[:, :, :, None, :] - b_c[:, :, None, :, :]
        decay = jnp.where(causal[:, :, None], jnp.exp(jnp.minimum(diff, 0.0)), 0.0)
        scores = jnp.einsum('bhtk,bhsk,bhtsk->bhts', q_c, k_c, decay)
        o_intra = jnp.einsum('bhts,bhsv->bhtv', scores, v_c)
        b_last = b_c[:, :, -1, :]
        k_dec = k_c * jnp.exp(b_last[:, :, None, :] - b_c)
        state = jnp.exp(b_last)[..., None] * state + jnp.einsum('bhsk,bhsv->bhkv', k_dec, v_c)
        return state, o_inter + o_intra

    state0 = jnp.zeros((B, HG_HEADS, HG_EXPAND, HG_HEAD_DIM), jnp.float32)
    _, o = lax.scan(step, state0, (qc, kc, vc, bc))
    o = o.transpose(1, 0, 3, 2, 4).reshape(B, S, HG_HEADS, HG_HEAD_DIM)
    o = rms_norm(o, onorm_g) * jax.nn.silu(g.astype(jnp.float32))
    return o.reshape(B, S, HG_WIDTH).astype(q.dtype)


def dilated_branch(q, k, v, window, dil):
    B, H, S, E = q.shape
    span = window // dil
    seg = dil * ATT_BLOCK
    s_pad = -(-S // seg) * seg
    m = s_pad // dil
    nb = m // ATT_BLOCK

    def to_blocks(t):
        t = jnp.pad(t, ((0, 0), (0, 0), (0, s_pad - S), (0, 0)))
        t = t.reshape(B, H, m, dil, E).transpose(0, 1, 3, 2, 4)
        return t.reshape(B, H, dil, nb, ATT_BLOCK, E)

    def with_prev(t):
        prev = jnp.pad(t[:, :, :, :-1], ((0, 0), (0, 0), (0, 0), (1, 0), (0, 0), (0, 0)))
        return jnp.concatenate([prev, t], axis=4)

    qb = to_blocks(q)
    kb = with_prev(to_blocks(k))
    vb = with_prev(to_blocks(v))
    s = jnp.einsum('bhrnqe,bhrnke->bhrnqk', qb, kb).astype(jnp.float32)
    qi = jnp.arange(ATT_BLOCK)[:, None]
    kj = jnp.arange(2 * ATT_BLOCK)[None, :]
    dist = ATT_BLOCK + qi - kj
    band = (dist >= 0) & (dist <= span)
    real = (jnp.arange(nb) > 0)[:, None, None] | (kj >= ATT_BLOCK)[None]
    mask = band[None] & real
    s = jnp.where(mask, s, -jnp.inf)
    s_max = jnp.max(s, axis=-1, keepdims=True)
    p = jnp.exp(s - s_max)
    l = jnp.sum(p, axis=-1, keepdims=True)
    o = jnp.einsum('bhrnqk,bhrnke->bhrnqe', p, vb.astype(jnp.float32)) / l
    lse = (s_max + jnp.log(l))[..., 0]
    o = o.reshape(B, H, dil, m, E).transpose(0, 1, 3, 2, 4).reshape(B, H, s_pad, E)[:, :, :S]
    lse = lse.reshape(B, H, dil, m).transpose(0, 1, 3, 2).reshape(B, H, s_pad)[:, :, :S]
    return o, lse


def dilated_attention(q, k, v):
    B, S = q.shape[0], q.shape[1]
    qh = q.transpose(0, 2, 1, 3) * (ATT_HEAD_DIM ** -0.5)
    kh = k.transpose(0, 2, 1, 3)
    vh = v.transpose(0, 2, 1, 3)
    outs, lses = zip(*[dilated_branch(qh, kh, vh, w, d) for (w, d) in DILATED_PAIRS])
    weights = jax.nn.softmax(jnp.stack(lses), axis=0)
    o = jnp.sum(weights[..., None] * jnp.stack(outs), axis=0)
    return o.transpose(0, 2, 1, 3).reshape(B, S, ATT_WIDTH).astype(q.dtype)


def _fwd_setup_inputs(seed: int = 0) -> dict:
    key = jax.random.key(seed)
    ks = jax.random.split(key, 14)

    def nrm(k, shape, scale):
        return jax.random.normal(k, shape, jnp.float32) * scale

    return {
        'x': nrm(ks[0], (BATCH, SEQ, D_MODEL), 1.0),
        'c': nrm(ks[1], (BATCH, D_MODEL), 1.0),
        'w_ada': nrm(ks[2], (DEPTH, D_MODEL, 6 * D_MODEL), D_MODEL ** -0.5),
        'b_ada': nrm(ks[3], (DEPTH, 6 * D_MODEL), 0.01),
        'norm1_g': 1.0 + nrm(ks[4], (DEPTH, D_MODEL), 0.01),
        'w_in': nrm(ks[5], (DEPTH, D_MODEL, IN_WIDTH), D_MODEL ** -0.5),
        'hg_lb_logits': nrm(ks[6], (DEPTH + 1, HG_FDIM), 0.1),
        'hg_onorm_g': 1.0 + nrm(ks[7], (DEPTH, HG_HEAD_DIM), 0.01),
        'att_onorm_g': 1.0 + nrm(ks[8], (DEPTH, ATT_WIDTH), 0.01),
        'w_out': nrm(ks[9], (DEPTH, MIX_WIDTH, D_MODEL), MIX_WIDTH ** -0.5),
        'norm2_g': 1.0 + nrm(ks[10], (DEPTH, D_MODEL), 0.01),
        'w_gate_up': nrm(ks[11], (DEPTH, D_MODEL, 2 * D_FF), D_MODEL ** -0.5),
        'w_down': nrm(ks[12], (DEPTH, D_FF, D_MODEL), D_FF ** -0.5),
        'final_g': 1.0 + nrm(ks[13], (D_MODEL,), 0.01),
    }


def _fwd_reference(x, c, w_ada, b_ada, norm1_g, w_in, hg_lb_logits, hg_onorm_g, att_onorm_g,
              w_out, norm2_g, w_gate_up, w_down, final_g):
    B, S = x.shape[0], x.shape[1]
    lower_bounds = jnp.cumsum(jax.nn.softmax(hg_lb_logits.astype(jnp.float32), axis=0), axis=0)
    c_act = jax.nn.silu(c)
    split_at = np.cumsum(IN_SIZES)[:-1].tolist()
    for l in range(DEPTH):
        mod = c_act @ w_ada[l] + b_ada[l]
        shift1, scale1, gate1, shift2, scale2, gate2 = jnp.split(mod, 6, axis=-1)
        h = modulate(rms_norm(x, norm1_g[l]), shift1, scale1)
        hq, hf, hi, hgt, aq, ak, av = jnp.split(h @ w_in[l], split_at, axis=-1)
        hg_out = hgrn2_mixer(
            hq.reshape(B, S, HG_HEADS, HG_EXPAND),
            hf.reshape(B, S, HG_HEADS, HG_EXPAND),
            hi.reshape(B, S, HG_HEADS, HG_HEAD_DIM),
            hgt.reshape(B, S, HG_HEADS, HG_HEAD_DIM),
            lower_bounds[l], hg_onorm_g[l])
        att = dilated_attention(
            aq.reshape(B, S, ATT_HEADS, ATT_HEAD_DIM),
            ak.reshape(B, S, ATT_HEADS, ATT_HEAD_DIM),
            av.reshape(B, S, ATT_HEADS, ATT_HEAD_DIM))
        att_out = rms_norm(att, att_onorm_g[l])
        mix = jnp.concatenate([hg_out, att_out], axis=-1) @ w_out[l]
        x = x + gate1[:, None, :] * mix
        h = modulate(rms_norm(x, norm2_g[l]), shift2, scale2)
        a, u = jnp.split(h @ w_gate_up[l], 2, axis=-1)
        x = x + gate2[:, None, :] * ((jax.nn.silu(a) * u) @ w_down[l])
    return rms_norm(x, final_g)


import jax as _jax
import jax.numpy as _jnp

TWIN_FORMAT = 'train_step'
FWD_PARAMS = ['x', 'c', 'w_ada', 'b_ada', 'norm1_g', 'w_in', 'hg_lb_logits', 'hg_onorm_g', 'att_onorm_g', 'w_out', 'norm2_g', 'w_gate_up', 'w_down', 'final_g']
TWIN_WEIGHTS = ['w_ada', 'b_ada', 'norm1_g', 'w_in', 'hg_lb_logits', 'hg_onorm_g', 'att_onorm_g', 'w_out', 'norm2_g', 'w_gate_up', 'w_down', 'final_g']
TWIN_DIFF_INPUT = 'x'
TWIN_INPUTS = ['x', 'c', 'w_ada', 'b_ada', 'norm1_g', 'w_in', 'hg_lb_logits', 'hg_onorm_g', 'att_onorm_g', 'w_out', 'norm2_g', 'w_gate_up', 'w_down', 'final_g', 'loss_target', 'm_w_ada', 'm_b_ada', 'm_norm1_g', 'm_w_in', 'm_hg_lb_logits', 'm_hg_onorm_g', 'm_att_onorm_g', 'm_w_out', 'm_norm2_g', 'm_w_gate_up', 'm_w_down', 'm_final_g', 'v_w_ada', 'v_b_ada', 'v_norm1_g', 'v_w_in', 'v_hg_lb_logits', 'v_hg_onorm_g', 'v_att_onorm_g', 'v_w_out', 'v_norm2_g', 'v_w_gate_up', 'v_w_down', 'v_final_g']
TWIN_OUTPUTS = ['loss', 'grad_x', 'grad_w_ada', 'grad_b_ada', 'grad_norm1_g', 'grad_w_in', 'grad_hg_lb_logits', 'grad_hg_onorm_g', 'grad_att_onorm_g', 'grad_w_out', 'grad_norm2_g', 'grad_w_gate_up', 'grad_w_down', 'grad_final_g', 'delta_w_ada', 'delta_b_ada', 'delta_norm1_g', 'delta_w_in', 'delta_hg_lb_logits', 'delta_hg_onorm_g', 'delta_att_onorm_g', 'delta_w_out', 'delta_norm2_g', 'delta_w_gate_up', 'delta_w_down', 'delta_final_g', 'new_m_w_ada', 'new_m_b_ada', 'new_m_norm1_g', 'new_m_w_in', 'new_m_hg_lb_logits', 'new_m_hg_onorm_g', 'new_m_att_onorm_g', 'new_m_w_out', 'new_m_norm2_g', 'new_m_w_gate_up', 'new_m_w_down', 'new_m_final_g', 'new_v_w_ada', 'new_v_b_ada', 'new_v_norm1_g', 'new_v_w_in', 'new_v_hg_lb_logits', 'new_v_hg_onorm_g', 'new_v_att_onorm_g', 'new_v_w_out', 'new_v_norm2_g', 'new_v_w_gate_up', 'new_v_w_down', 'new_v_final_g']
TWIN_LEAF_KINDS = {'loss': 'loss', 'grad_x': 'grad_x', 'grad_w_ada': 'grad_w', 'grad_b_ada': 'grad_w', 'grad_norm1_g': 'grad_w', 'grad_w_in': 'grad_w', 'grad_hg_lb_logits': 'grad_w', 'grad_hg_onorm_g': 'grad_w', 'grad_att_onorm_g': 'grad_w', 'grad_w_out': 'grad_w', 'grad_norm2_g': 'grad_w', 'grad_w_gate_up': 'grad_w', 'grad_w_down': 'grad_w', 'grad_final_g': 'grad_w', 'delta_w_ada': 'delta_w', 'delta_b_ada': 'delta_w', 'delta_norm1_g': 'delta_w', 'delta_w_in': 'delta_w', 'delta_hg_lb_logits': 'delta_w', 'delta_hg_onorm_g': 'delta_w', 'delta_att_onorm_g': 'delta_w', 'delta_w_out': 'delta_w', 'delta_norm2_g': 'delta_w', 'delta_w_gate_up': 'delta_w', 'delta_w_down': 'delta_w', 'delta_final_g': 'delta_w', 'new_m_w_ada': 'new_m', 'new_m_b_ada': 'new_m', 'new_m_norm1_g': 'new_m', 'new_m_w_in': 'new_m', 'new_m_hg_lb_logits': 'new_m', 'new_m_hg_onorm_g': 'new_m', 'new_m_att_onorm_g': 'new_m', 'new_m_w_out': 'new_m', 'new_m_norm2_g': 'new_m', 'new_m_w_gate_up': 'new_m', 'new_m_w_down': 'new_m', 'new_m_final_g': 'new_m', 'new_v_w_ada': 'new_v', 'new_v_b_ada': 'new_v', 'new_v_norm1_g': 'new_v', 'new_v_w_in': 'new_v', 'new_v_hg_lb_logits': 'new_v', 'new_v_hg_onorm_g': 'new_v', 'new_v_att_onorm_g': 'new_v', 'new_v_w_out': 'new_v', 'new_v_norm2_g': 'new_v', 'new_v_w_gate_up': 'new_v', 'new_v_w_down': 'new_v', 'new_v_final_g': 'new_v'}


def _forward(args):
    return _fwd_reference(*[args[k] for k in FWD_PARAMS])


def _output_shape():
    out = _jax.eval_shape(lambda: _forward(_fwd_setup_inputs(0)))
    return out.shape, out.dtype

N_MICROBATCH = 1
ADAM_LR = 0.001
ADAM_B1 = 0.9
ADAM_B2 = 0.999
ADAM_EPS = 1e-08
ADAM_WD = 0.01
ADAM_STEP = 10
PER_EXAMPLE_BATCH_AXIS = {'x': 0, 'c': 0, 'loss_target': 0}
SHARED_INPUTS = []
_WEIGHT_DTYPES = {'w_ada': _jnp.float32, 'b_ada': _jnp.float32, 'norm1_g': _jnp.float32, 'w_in': _jnp.float32, 'hg_lb_logits': _jnp.float32, 'hg_onorm_g': _jnp.float32, 'att_onorm_g': _jnp.float32, 'w_out': _jnp.float32, 'norm2_g': _jnp.float32, 'w_gate_up': _jnp.float32, 'w_down': _jnp.float32, 'final_g': _jnp.float32}
MOMENT_SCALE = {'w_ada': 8.253882e-02, 'b_ada': 1.359285e-01, 'norm1_g': 9.552995e-02, 'w_in': 6.761494e-02, 'hg_lb_logits': 4.675611e-03, 'hg_onorm_g': 2.104411e-01, 'att_onorm_g': 1.091114e-01, 'w_out': 9.935644e-02, 'norm2_g': 1.148301e-01, 'w_gate_up': 5.569348e-02, 'w_down': 9.040245e-02, 'final_g': 3.317550e+01}


def _to_microbatches(a, axis):
    t = _jnp.moveaxis(a, axis, 0)
    t = t.reshape((N_MICROBATCH, t.shape[0] // N_MICROBATCH) + t.shape[1:])
    return _jnp.moveaxis(t, 1, axis + 1)


def setup_inputs(seed: int = 0) -> dict:
    inp = _fwd_setup_inputs(seed)
    key = _jax.random.fold_in(_jax.random.key(seed), 7919)
    shape, _ = _output_shape()
    out = dict(inp)
    out["loss_target"] = _jax.random.normal(_jax.random.fold_in(key, 0), shape, _jnp.float32)
    for i, name in enumerate(TWIN_WEIGHTS):
        w = inp[name].astype(_jnp.float32)
        if MOMENT_SCALE is None:
            s = _jnp.sqrt(_jnp.mean(_jnp.square(w)) + 1e-30)
        else:
            s = MOMENT_SCALE[name]
        km, kv = _jax.random.split(_jax.random.fold_in(key, i + 1))
        out[name] = w
        out["m_" + name] = s * _jax.random.normal(km, w.shape, _jnp.float32)
        out["v_" + name] = (s * s) * _jax.random.uniform(kv, w.shape, _jnp.float32, 0.5, 1.5)
    if N_MICROBATCH > 1:
        for name, axis in PER_EXAMPLE_BATCH_AXIS.items():
            out[name] = _to_microbatches(out[name], axis)
    return {'x': out['x'], 'c': out['c'], 'w_ada': out['w_ada'], 'b_ada': out['b_ada'], 'norm1_g': out['norm1_g'], 'w_in': out['w_in'], 'hg_lb_logits': out['hg_lb_logits'], 'hg_onorm_g': out['hg_onorm_g'], 'att_onorm_g': out['att_onorm_g'], 'w_out': out['w_out'], 'norm2_g': out['norm2_g'], 'w_gate_up': out['w_gate_up'], 'w_down': out['w_down'], 'final_g': out['final_g'], 'loss_target': out['loss_target'], 'm_w_ada': out['m_w_ada'], 'm_b_ada': out['m_b_ada'], 'm_norm1_g': out['m_norm1_g'], 'm_w_in': out['m_w_in'], 'm_hg_lb_logits': out['m_hg_lb_logits'], 'm_hg_onorm_g': out['m_hg_onorm_g'], 'm_att_onorm_g': out['m_att_onorm_g'], 'm_w_out': out['m_w_out'], 'm_norm2_g': out['m_norm2_g'], 'm_w_gate_up': out['m_w_gate_up'], 'm_w_down': out['m_w_down'], 'm_final_g': out['m_final_g'], 'v_w_ada': out['v_w_ada'], 'v_b_ada': out['v_b_ada'], 'v_norm1_g': out['v_norm1_g'], 'v_w_in': out['v_w_in'], 'v_hg_lb_logits': out['v_hg_lb_logits'], 'v_hg_onorm_g': out['v_hg_onorm_g'], 'v_att_onorm_g': out['v_att_onorm_g'], 'v_w_out': out['v_w_out'], 'v_norm2_g': out['v_norm2_g'], 'v_w_gate_up': out['v_w_gate_up'], 'v_w_down': out['v_w_down'], 'v_final_g': out['v_final_g']}


def _loss(weights, diff, rest, loss_target):
    with _jax.named_scope("forward"):
        args = {**rest, TWIN_DIFF_INPUT: diff, **{k: w.astype(_WEIGHT_DTYPES[k]) for k, w in weights.items()}}
        y = _forward(args)
    with _jax.named_scope("loss_head"):
        err = _jnp.square(y.astype(_jnp.float32) - loss_target)
        return 0.5 * _jnp.sum(_jnp.mean(err, axis=-1)) if err.ndim else 0.5 * err


def _adamw(w, g, m, v):
    m = ADAM_B1 * m + (1.0 - ADAM_B1) * g
    v = ADAM_B2 * v + (1.0 - ADAM_B2) * _jnp.square(g)
    m_hat = m / (1.0 - ADAM_B1 ** ADAM_STEP)
    v_hat = v / (1.0 - ADAM_B2 ** ADAM_STEP)
    delta = -ADAM_LR * (m_hat / (_jnp.sqrt(v_hat) + ADAM_EPS) + ADAM_WD * w)
    return delta, m, v


def reference(x, c, w_ada, b_ada, norm1_g, w_in, hg_lb_logits, hg_onorm_g, att_onorm_g, w_out, norm2_g, w_gate_up, w_down, final_g, loss_target, m_w_ada, m_b_ada, m_norm1_g, m_w_in, m_hg_lb_logits, m_hg_onorm_g, m_att_onorm_g, m_w_out, m_norm2_g, m_w_gate_up, m_w_down, m_final_g, v_w_ada, v_b_ada, v_norm1_g, v_w_in, v_hg_lb_logits, v_hg_onorm_g, v_att_onorm_g, v_w_out, v_norm2_g, v_w_gate_up, v_w_down, v_final_g):
    given = dict(x=x, c=c, w_ada=w_ada, b_ada=b_ada, norm1_g=norm1_g, w_in=w_in, hg_lb_logits=hg_lb_logits, hg_onorm_g=hg_onorm_g, att_onorm_g=att_onorm_g, w_out=w_out, norm2_g=norm2_g, w_gate_up=w_gate_up, w_down=w_down, final_g=final_g, loss_target=loss_target, m_w_ada=m_w_ada, m_b_ada=m_b_ada, m_norm1_g=m_norm1_g, m_w_in=m_w_in, m_hg_lb_logits=m_hg_lb_logits, m_hg_onorm_g=m_hg_onorm_g, m_att_onorm_g=m_att_onorm_g, m_w_out=m_w_out, m_norm2_g=m_norm2_g, m_w_gate_up=m_w_gate_up, m_w_down=m_w_down, m_final_g=m_final_g, v_w_ada=v_w_ada, v_b_ada=v_b_ada, v_norm1_g=v_norm1_g, v_w_in=v_w_in, v_hg_lb_logits=v_hg_lb_logits, v_hg_onorm_g=v_hg_onorm_g, v_att_onorm_g=v_att_onorm_g, v_w_out=v_w_out, v_norm2_g=v_norm2_g, v_w_gate_up=v_w_gate_up, v_w_down=v_w_down, v_final_g=v_final_g)
    weights = {n: given[n] for n in TWIN_WEIGHTS}
    shared = {n: given[n] for n in SHARED_INPUTS}
    per_example = {n: given[n] for n in ['x', 'c']}
    grad_fn = _jax.value_and_grad(_loss, argnums=(0, 1))

    def one_microbatch(ex, loss_target):
        ex = dict(ex)
        diff = ex.pop(TWIN_DIFF_INPUT)
        return grad_fn(weights, diff, {**shared, **ex}, loss_target)

    if N_MICROBATCH == 1:
        loss, (grad_w, grad_x) = one_microbatch(per_example, given["loss_target"])
    else:
        def body(carry, xs):
            loss_sum, grad_sum = carry
            l_k, (gw_k, gx_k) = one_microbatch(xs[0], xs[1])
            with _jax.named_scope("update"):
                return (loss_sum + l_k, _jax.tree.map(_jnp.add, grad_sum, gw_k)), gx_k

        init = (_jnp.zeros((), _jnp.float32), _jax.tree.map(_jnp.zeros_like, weights))
        (loss, grad_w), grad_x = _jax.lax.scan(body, init, (per_example, given["loss_target"]))
    with _jax.named_scope("update"):
        delta_w, new_m, new_v = {}, {}, {}
        for n in TWIN_WEIGHTS:
            delta_w[n], new_m[n], new_v[n] = _adamw(weights[n], grad_w[n], given["m_" + n], given["v_" + n])
    return (loss, grad_x, *[grad_w[n] for n in TWIN_WEIGHTS], *[delta_w[n] for n in TWIN_WEIGHTS],
            *[new_m[n] for n in TWIN_WEIGHTS], *[new_v[n] for n in TWIN_WEIGHTS])
```

```python
import functools

import jax
import jax.numpy as jnp
from jax import lax
from jax.experimental import pallas as pl
from jax.experimental.pallas import tpu as pltpu

F32 = jnp.float32
BF16 = jnp.bfloat16
HIGHEST = lax.Precision.HIGHEST
MESH = pl.DeviceIdType.MESH

D_MODEL = 1024
N_DEV = 8
HG_HEADS = 4
HG_DIM = 128
HG_WIDTH = HG_HEADS * HG_DIM
HG_CHUNK = 64
ATT_WIDTH = 512
ATT_HEAD_DIM = 64
ATT_BLOCK = 128
DILATIONS = (1, 4, 16)
ATT_SCALE = ATT_HEAD_DIM ** -0.5
D_FF = 2816
IN_WIDTH = 7 * 512
RMS_EPS = 1e-6
NEG = -1e30

ADAM_LR = 0.001
ADAM_B1 = 0.9
ADAM_B2 = 0.999
ADAM_EPS = 1e-08
ADAM_WD = 0.01
ADAM_STEP = 10

V7X_VMEM_LIMIT = 56 * 1024 * 1024

SM_MOD = 0
SM_G1 = 6 * D_MODEL
SM_G2 = 7 * D_MODEL
SM_GF = 8 * D_MODEL
SM_ATT = 9 * D_MODEL
SM_LB = 9 * D_MODEL + 512
SM_HG = 10 * D_MODEL
SM_LOSS = 10 * D_MODEL + 128
SM_WIDTH = 10 * D_MODEL + 256


def _params(*sem, vmem=V7X_VMEM_LIMIT):
    return pltpu.CompilerParams(dimension_semantics=sem, vmem_limit_bytes=vmem)


def _dot(a, b):
    return jnp.dot(a, b, preferred_element_type=F32)


def _dot_nt(a, b):
    return lax.dot_general(a, b, (((1,), (1,)), ((), ())), preferred_element_type=F32)


def _dot_tn(a, b):
    return lax.dot_general(a, b, (((0,), (0,)), ((), ())), preferred_element_type=F32)


def _dot_f32(a, b):
    return jnp.dot(a, b, preferred_element_type=F32, precision=HIGHEST)


def _sigmoid(x):
    return 1.0 / (1.0 + jnp.exp(-x))


def _silu(x):
    return x * _sigmoid(x)


def _dsilu(x):
    s = _sigmoid(x)
    return s * (1.0 + x * (1.0 - s))


def _rms(x):
    rstd = lax.rsqrt(jnp.mean(x * x, axis=-1, keepdims=True) + RMS_EPS)
    return x * rstd, rstd


def _rms_bwd(dn, xhat, rstd):
    return rstd * (dn - xhat * jnp.mean(dn * xhat, axis=-1, keepdims=True))


def _rowsum(x):
    return jnp.sum(x, axis=0, keepdims=True)


def _rows(tm, n):
    return pl.BlockSpec((tm, n), lambda i: (i, 0))


def _whole(shape):
    return pl.BlockSpec(shape, lambda i: (0,) * len(shape))


def _mesh_pos():
    return lax.axis_index("x"), lax.axis_index("y"), lax.axis_index("c")


def _flip(k):
    x, y, c = _mesh_pos()
    px = 1 - x if k & 4 else x
    py = 1 - y if k & 2 else y
    pc = 1 - c if k & 1 else c
    return (px, py, pc), 4 * px + 2 * py + pc


def _exchange_small(x, rows_per_peer, name):
    r_all, cols = x.shape
    r_out = r_all if rows_per_peer is None else rows_per_peer

    def body(x_ref, out_ref, send_sems, recv_sems):
        _, me = _flip(0)

        def src(pid):
            if rows_per_peer is None:
                return x_ref
            return x_ref.at[pl.ds(pl.multiple_of(pid * r_out, r_out), r_out), :]

        if rows_per_peer is None:
            out_ref[me] = x_ref[...]
        else:
            out_ref[me] = x_ref[pl.ds(pl.multiple_of(me * r_out, r_out), r_out), :]
        sends = []
        for k in range(1, N_DEV):
            dev, pid = _flip(k)
            cp = pltpu.make_async_remote_copy(src_ref=src(pid), dst_ref=out_ref.at[me], send_sem=send_sems.at[k - 1],
                                              recv_sem=recv_sems.at[k - 1], device_id=dev, device_id_type=MESH)
            cp.start()
            sends.append(cp)
        for k in range(1, N_DEV):
            dev, pid = _flip(k)
            pltpu.make_async_remote_copy(src_ref=src(pid), dst_ref=out_ref.at[pid], send_sem=send_sems.at[k - 1],
                                         recv_sem=recv_sems.at[k - 1], device_id=dev, device_id_type=MESH).wait_recv()
        for cp in sends:
            cp.wait_send()

    return pl.pallas_call(
        body, name=name,
        out_shape=jax.ShapeDtypeStruct((N_DEV, r_out, cols), x.dtype),
        in_specs=[pl.BlockSpec(memory_space=pltpu.VMEM)],
        out_specs=pl.BlockSpec(memory_space=pltpu.VMEM),
        scratch_shapes=[pltpu.SemaphoreType.DMA((N_DEV - 1,)), pltpu.SemaphoreType.DMA((N_DEV - 1,))],
    )(x)


def _gather_weights(shards):
    n = len(shards)

    def body(*refs):
        xs, outs = refs[:n], refs[n:2 * n]
        send_sems, recv_sems, local_sems = refs[2 * n:]
        x, y, c = _mesh_pos()
        me, sibling = (x, y, c), (x, y, 1 - c)
        chips = [(1 - x, y), (x, 1 - y), (1 - x, 1 - y)]

        def blk(a, px, py, pc):
            return outs[a].at[4 * px + 2 * py + pc]

        def copy(a, k, block, to, src=None):
            return pltpu.make_async_remote_copy(
                src_ref=blk(a, *block) if src is None else src, dst_ref=blk(a, *block),
                send_sem=send_sems.at[a * 7 + k], recv_sem=recv_sems.at[a * 7 + k], device_id=to, device_id_type=MESH)

        mine = [pltpu.make_async_copy(xs[a], blk(a, *me), local_sems.at[a]) for a in range(n)]
        for cp in mine:
            cp.start()
        first = []
        for a in range(n):
            first.append(copy(a, 0, me, sibling, src=xs[a]))
            first += [copy(a, 1 + j, me, (*chip, c), src=xs[a]) for j, chip in enumerate(chips)]
        for cp in first:
            cp.start()
        passed = []
        for j, chip in enumerate(chips):
            for a in range(n):
                copy(a, 1 + j, (*chip, c), me).wait_recv()
                cp = copy(a, 4 + j, (*chip, c), sibling)
                cp.start()
                passed.append(cp)
        for a in range(n):
            copy(a, 0, sibling, me).wait_recv()
            for j, chip in enumerate(chips):
                copy(a, 4 + j, (*chip, 1 - c), me).wait_recv()
        for cp in first + passed:
            cp.wait_send()
        for cp in mine:
            cp.wait()

    hbm = pl.BlockSpec(memory_space=pl.ANY)
    return pl.pallas_call(
        body, name="gather_weights",
        out_shape=[jax.ShapeDtypeStruct((N_DEV,) + s.shape, s.dtype) for s in shards],
        in_specs=[hbm] * n, out_specs=[hbm] * n,
        scratch_shapes=[pltpu.SemaphoreType.DMA((7 * n,)), pltpu.SemaphoreType.DMA((7 * n,)), pltpu.SemaphoreType.DMA((n,))],
    )(*shards)


def _reduce_pairs(grads):
    n = len(grads)

    def body(*refs):
        gs, mine, got = refs[:n], refs[n:2 * n], refs[2 * n:3 * n]
        send_sems, recv_sems, local_sems = refs[3 * n:]
        x, y, c = _mesh_pos()
        sibling = (x, y, 1 - c)
        local, sends = [], []
        for a in range(n):
            for chip in range(4):
                i = a * 4 + chip
                lc = pltpu.make_async_copy(gs[a].at[2 * chip + c], mine[a].at[chip], local_sems.at[i])
                lc.start()
                local.append(lc)
                cp = pltpu.make_async_remote_copy(src_ref=gs[a].at[2 * chip + 1 - c], dst_ref=got[a].at[chip],
                                                  send_sem=send_sems.at[i], recv_sem=recv_sems.at[i],
                                                  device_id=sibling, device_id_type=MESH)
                cp.start()
                sends.append(cp)
        for cp in sends:
            cp.wait_recv()
        for cp in sends:
            cp.wait_send()
        for lc in local:
            lc.wait()

    hbm = pl.BlockSpec(memory_space=pl.ANY)
    quarter = [jax.ShapeDtypeStruct((4,) + g.shape[1:], g.dtype) for g in grads]
    outs = pl.pallas_call(
        body, name="reduce_pairs", out_shape=quarter + quarter, in_specs=[hbm] * n, out_specs=[hbm] * (2 * n),
        scratch_shapes=[pltpu.SemaphoreType.DMA((4 * n,)), pltpu.SemaphoreType.DMA((4 * n,)), pltpu.SemaphoreType.DMA((4 * n,))],
    )(*grads)
    return outs[:n], outs[n:]


def _reduce_chips(partials):
    n = len(partials)
    flips = (4, 2, 6)

    def body(*refs):
        ps, own, got = refs[:n], refs[n:2 * n], refs[2 * n:3 * n]
        send_sems, recv_sems, local_sems = refs[3 * n:]
        x, y, _ = _mesh_pos()
        local, sends = [], []
        for a in range(n):
            lc = pltpu.make_async_copy(ps[a].at[2 * x + y], own[a], local_sems.at[a])
            lc.start()
            local.append(lc)
            for j, k in enumerate(flips):
                dev, _ = _flip(k)
                cp = pltpu.make_async_remote_copy(src_ref=ps[a].at[2 * dev[0] + dev[1]], dst_ref=got[a].at[j],
                                                  send_sem=send_sems.at[a * 3 + j], recv_sem=recv_sems.at[a * 3 + j],
                                                  device_id=dev, device_id_type=MESH)
                cp.start()
                sends.append(cp)
        for cp in sends:
            cp.wait_recv()
        for cp in sends:
            cp.wait_send()
        for lc in local:
            lc.wait()

    hbm = pl.BlockSpec(memory_space=pl.ANY)
    outs = pl.pallas_call(
        body, name="reduce_chips",
        out_shape=[jax.ShapeDtypeStruct(p.shape[1:], p.dtype) for p in partials]
        + [jax.ShapeDtypeStruct((3,) + p.shape[1:], p.dtype) for p in partials],
        in_specs=[hbm] * n, out_specs=[hbm] * (2 * n),
        scratch_shapes=[pltpu.SemaphoreType.DMA((3 * n,)), pltpu.SemaphoreType.DMA((3 * n,)), pltpu.SemaphoreType.DMA((n,))],
    )(*partials)
    return outs[:n], outs[n:]


def _pair_sum(mine, got, name):
    _, r, c = mine.shape
    tr = r // 2 if r % 16 == 0 else r

    def body(a_ref, b_ref, o_ref):
        o_ref[...] = a_ref[...] + b_ref[...]

    spec = pl.BlockSpec((1, tr, c), lambda i, j: (i, j, 0))
    return pl.pallas_call(
        body, name=name, grid=(4, r // tr), out_shape=jax.ShapeDtypeStruct(mine.shape, F32),
        in_specs=[spec, spec], out_specs=spec, compiler_params=_params("parallel", "parallel"),
    )(mine, got)


def _ada_rows(c_all, w_ada, b_ada):
    n_cols = w_ada.shape[1]

    def body(c_ref, w_ref, b_ref, o_ref):
        _, me = _flip(0)
        bias = b_ref[:, pl.ds(pl.multiple_of(me * n_cols, 128), n_cols)]
        o_ref[...] = _dot_f32(_silu(c_ref[...]), w_ref[...]) + bias

    return pl.pallas_call(
        body, name="ada_rows", out_shape=jax.ShapeDtypeStruct((N_DEV, n_cols), F32),
        in_specs=[pl.BlockSpec(memory_space=pltpu.VMEM)] * 3, out_specs=pl.BlockSpec(memory_space=pltpu.VMEM),
    )(c_all, w_ada, b_ada)


def _in_fwd(x, mod, g1, w_in):
    s = x.shape[0]
    tm = 256

    def body(x_ref, mod_ref, g_ref, w_ref, h_ref, *outs):
        xhat, _ = _rms(x_ref[...])
        h = (xhat * g_ref[...]) * (1.0 + mod_ref[:, D_MODEL:2 * D_MODEL]) + mod_ref[:, 0:D_MODEL]
        hb = h.astype(BF16)
        h_ref[...] = hb
        for j, o_ref in enumerate(outs):
            o_ref[...] = _dot(hb, w_ref[:, j * 512:(j + 1) * 512])

    return pl.pallas_call(
        body, name="in_fwd", grid=(s // tm,),
        out_shape=[jax.ShapeDtypeStruct((s, D_MODEL), BF16)] + [jax.ShapeDtypeStruct((s, 512), F32)] * 7,
        in_specs=[_rows(tm, D_MODEL), _whole((1, 6 * D_MODEL)), _whole((1, D_MODEL)), _whole((D_MODEL, IN_WIDTH))],
        out_specs=[_rows(tm, D_MODEL)] + [_rows(tm, 512)] * 7,
        compiler_params=_params("parallel"),
    )(x, mod, g1, w_in)


def _in_bwd(x, dx1, mod, g1, w_in, dps):
    s = x.shape[0]
    tm = 256

    def body(x_ref, dx_ref, mod_ref, g_ref, w_ref, *rest):
        dp_refs, (gx_ref, dpb_ref, dsh_ref, dsc_ref, dg_ref) = rest[:13], rest[13:]
        pieces = [dp_refs[j][...] for j in range(4)]
        pieces += [dp_refs[4 + 3 * j][...] + dp_refs[5 + 3 * j][...] + dp_refs[6 + 3 * j][...] for j in range(3)]
        dh = jnp.zeros((tm, D_MODEL), F32)
        for j, p in enumerate(pieces):
            pb = p.astype(BF16)
            dpb_ref[:, j * 512:(j + 1) * 512] = pb
            dh += _dot_nt(pb, w_ref[:, j * 512:(j + 1) * 512])
        xhat, rstd = _rms(x_ref[...])
        g = g_ref[...]
        scale1 = 1.0 + mod_ref[:, D_MODEL:2 * D_MODEL]
        n1 = xhat * g

        @pl.when(pl.program_id(0) == 0)
        def _():
            dsh_ref[...] = jnp.zeros_like(dsh_ref)
            dsc_ref[...] = jnp.zeros_like(dsc_ref)
            dg_ref[...] = jnp.zeros_like(dg_ref)

        dsh_ref[...] += _rowsum(dh)
        dsc_ref[...] += _rowsum(dh * n1)
        dn = dh * scale1
        dg_ref[...] += _rowsum(dn * xhat)
        gx_ref[...] = dx_ref[...] + _rms_bwd(dn * g, xhat, rstd)

    vec = _whole((1, D_MODEL))
    return pl.pallas_call(
        body, name="in_bwd", grid=(s // tm,),
        out_shape=[jax.ShapeDtypeStruct((s, D_MODEL), F32), jax.ShapeDtypeStruct((s, IN_WIDTH), BF16)]
        + [jax.ShapeDtypeStruct((1, D_MODEL), F32)] * 3,
        in_specs=[_rows(tm, D_MODEL), _rows(tm, D_MODEL), _whole((1, 6 * D_MODEL)), vec, _whole((D_MODEL, IN_WIDTH))]
        + [_rows(tm, 512)] * 13,
        out_specs=[_rows(tm, D_MODEL), _rows(tm, IN_WIDTH), vec, vec, vec],
        compiler_params=_params("arbitrary"),
    )(x, dx1, mod, g1, w_in, *dps)


HG_TILE = 512
HG_TILE_CHUNKS = HG_TILE // HG_CHUNK


def _lower_bound(lg_ref):
    return 1.0 / (1.0 + jnp.exp(lg_ref[1:2, :] - lg_ref[0:1, :]))


def _chunk_masks():
    r = lax.broadcasted_iota(jnp.int32, (HG_CHUNK, HG_CHUNK), 0)
    c = lax.broadcasted_iota(jnp.int32, (HG_CHUNK, HG_CHUNK), 1)
    return r >= c, c >= r, (r >= c).astype(F32), (c >= r).astype(F32)


def _hg_fwd(hq, hf, hi, hgt, logits, onorm_g):
    s = hq.shape[0]
    n_tiles = s // HG_TILE

    def body(q_ref, f_ref, i_ref, g_ref, lg_ref, og_ref, out_ref, o_ref, st_ref, state, qf_s, kk_s, lf_s):
        @pl.when(pl.program_id(0) == 0)
        def _():
            state[...] = jnp.zeros_like(state)

        lb = _lower_bound(lg_ref)
        f = lb + (1.0 - lb) * _sigmoid(f_ref[...])
        kk_s[...] = 1.0 - f
        lf_s[...] = jnp.log(f)
        qf_s[...] = _silu(q_ref[...])
        causal, _, tri, _ = _chunk_masks()

        def chunk(ci, carry):
            rows = pl.ds(pl.multiple_of(ci * HG_CHUNK, HG_CHUNK), HG_CHUNK)
            srows = pl.ds(pl.multiple_of(ci * HG_DIM, HG_DIM), HG_DIM)
            lf = lf_s[rows, :]
            b = _dot_f32(tri, lf)
            bl = _rowsum(lf)
            ref = 0.5 * bl
            qf, kk, v = qf_s[rows, :], kk_s[rows, :], i_ref[rows, :]
            a_in = (qf * jnp.exp(b)).astype(BF16)
            a_t = (qf * jnp.exp(b - ref)).astype(BF16)
            b_t = (kk * jnp.exp(ref - b)).astype(BF16)
            kd = kk * jnp.exp(bl - b)
            ebl = jnp.exp(bl)
            vb = v.astype(BF16)
            for h in range(HG_HEADS):
                c = slice(h * HG_DIM, (h + 1) * HG_DIM)
                st = state[h]
                st_ref[srows, c] = st
                p = jnp.where(causal, _dot_nt(a_t[:, c], b_t[:, c]), 0.0)
                o_ref[rows, c] = _dot(p.astype(BF16), vb[:, c]) + _dot_nt(a_in[:, c], st.astype(BF16))
                state[h] = st * ebl[:, c] + _dot_tn(vb[:, c], kd[:, c].astype(BF16))
            return carry

        lax.fori_loop(0, HG_TILE_CHUNKS, chunk, 0)
        for h in range(HG_HEADS):
            c = slice(h * HG_DIM, (h + 1) * HG_DIM)
            ohat, _ = _rms(o_ref[:, c])
            out_ref[:, c] = (ohat * og_ref[...] * _silu(g_ref[:, c])).astype(BF16)

    tile = _rows(HG_TILE, HG_WIDTH)
    return pl.pallas_call(
        body, name="hg_fwd", grid=(n_tiles,),
        out_shape=[jax.ShapeDtypeStruct((s, HG_WIDTH), BF16), jax.ShapeDtypeStruct((s, HG_WIDTH), F32),
                   jax.ShapeDtypeStruct((s // HG_CHUNK * HG_DIM, HG_WIDTH), F32)],
        in_specs=[tile] * 4 + [_whole((2, HG_WIDTH)), _whole((1, HG_DIM))],
        out_specs=[tile, tile, _rows(HG_TILE_CHUNKS * HG_DIM, HG_WIDTH)],
        scratch_shapes=[pltpu.VMEM((HG_HEADS, HG_DIM, HG_DIM), F32)] + [pltpu.VMEM((HG_TILE, HG_WIDTH), F32)] * 3,
        compiler_params=_params("arbitrary"),
    )(hq, hf, hi, hgt, logits, onorm_g)


def _hg_bwd(hq, hf, hi, hgt, logits, onorm_g, o, states, dout):
    s = hq.shape[0]
    n_tiles = s // HG_TILE

    def body(q_ref, f_ref, i_ref, g_ref, lg_ref, og_ref, o_ref, st_ref, d_ref,
             dq_ref, df_ref, di_ref, dg_ref, dog_ref, dlb_ref, dstate, qf_s, kk_s, lf_s, do_s):
        @pl.when(pl.program_id(0) == 0)
        def _():
            dstate[...] = jnp.zeros_like(dstate)
            dog_ref[...] = jnp.zeros_like(dog_ref)
            dlb_ref[...] = jnp.zeros_like(dlb_ref)

        og = og_ref[...]
        dog = jnp.zeros((1, HG_DIM), F32)
        for h in range(HG_HEADS):
            c = slice(h * HG_DIM, (h + 1) * HG_DIM)
            ohat, rstd = _rms(o_ref[:, c])
            gate = g_ref[:, c]
            d = d_ref[:, c]
            dg_ref[:, c] = d * (ohat * og) * _dsilu(gate)
            dnormed = d * _silu(gate)
            dog += _rowsum(dnormed * ohat)
            do_s[:, c] = _rms_bwd(dnormed * og, ohat, rstd)
        dog_ref[...] += dog

        lb = _lower_bound(lg_ref)
        f = lb + (1.0 - lb) * _sigmoid(f_ref[...])
        kk_s[...] = 1.0 - f
        lf_s[...] = jnp.log(f)
        qf_s[...] = _silu(q_ref[...])
        causal, upper, tri, tri_t = _chunk_masks()

        def chunk(step, carry):
            ci = HG_TILE_CHUNKS - 1 - step
            rows = pl.ds(pl.multiple_of(ci * HG_CHUNK, HG_CHUNK), HG_CHUNK)
            srows = pl.ds(pl.multiple_of(ci * HG_DIM, HG_DIM), HG_DIM)
            lf = lf_s[rows, :]
            b = _dot_f32(tri, lf)
            bl = _rowsum(lf)
            ref = 0.5 * bl
            qf, kk, v, do = qf_s[rows, :], kk_s[rows, :], i_ref[rows, :], do_s[rows, :]
            eb, ebr, erb, ekd, ebl = jnp.exp(b), jnp.exp(b - ref), jnp.exp(ref - b), jnp.exp(bl - b), jnp.exp(bl)
            a_in, a_t, b_t, kd = qf * eb, qf * ebr, kk * erb, kk * ekd
            for h in range(HG_HEADS):
                c = slice(h * HG_DIM, (h + 1) * HG_DIM)
                st, dst = st_ref[srows, c], dstate[h]
                stb, dstb = st.astype(BF16), dst.astype(BF16)
                doh, vh = do[:, c], v[:, c]
                dob, vb = doh.astype(BF16), vh.astype(BF16)
                ain_h, at_h, bt_h, kd_h = a_in[:, c], a_t[:, c], b_t[:, c], kd[:, c]
                atb, btb = at_h.astype(BF16), bt_h.astype(BF16)
                d_ain = _dot(dob, stb)
                p_t = jnp.where(upper, _dot_nt(btb, atb), 0.0).astype(BF16)
                dp = jnp.where(causal, _dot_nt(dob, vb), 0.0).astype(BF16)
                dp_t = jnp.where(upper, _dot_nt(vb, dob), 0.0).astype(BF16)
                di_ref[rows, c] = _dot(p_t, dob) + _dot_nt(kd_h.astype(BF16), dstb)
                d_at = _dot(dp, btb)
                d_bt = _dot(dp_t, atb)
                d_kd = _dot(vb, dstb)
                dqf = d_ain * eb[:, c] + d_at * ebr[:, c]
                dkk = d_bt * erb[:, c] + d_kd * ekd[:, c]
                db = d_ain * ain_h + d_at * atb.astype(F32) - d_bt * btb.astype(F32) - d_kd * kd_h
                dbl = _rowsum(d_kd * kd_h) + _rowsum(dst * st) * ebl[:, c]
                dstate[h] = _dot_tn(dob, ain_h.astype(BF16)) + dst * ebl[:, c]
                dlf = _dot_f32(tri_t, db) + dbl
                qv, fr = q_ref[rows, c], f_ref[rows, c]
                lbh = lb[:, c]
                sg = _sigmoid(fr)
                dfv = dlf / (lbh + (1.0 - lbh) * sg) - dkk
                df_ref[rows, c] = dfv * (1.0 - lbh) * sg * (1.0 - sg)
                dlb_ref[:, c] += _rowsum(dfv * (1.0 - sg))
                dq_ref[rows, c] = dqf * _dsilu(qv)
            return carry

        lax.fori_loop(0, HG_TILE_CHUNKS, chunk, 0)

    rev = pl.BlockSpec((HG_TILE, HG_WIDTH), lambda i: (n_tiles - 1 - i, 0))
    return pl.pallas_call(
        body, name="hg_bwd", grid=(n_tiles,),
        out_shape=[jax.ShapeDtypeStruct((s, HG_WIDTH), F32)] * 4
        + [jax.ShapeDtypeStruct((1, HG_DIM), F32), jax.ShapeDtypeStruct((1, HG_WIDTH), F32)],
        in_specs=[rev] * 4 + [_whole((2, HG_WIDTH)), _whole((1, HG_DIM)), rev,
                              pl.BlockSpec((HG_TILE_CHUNKS * HG_DIM, HG_WIDTH), lambda i: (n_tiles - 1 - i, 0)), rev],
        out_specs=[rev] * 4 + [_whole((1, HG_DIM)), _whole((1, HG_WIDTH))],
        scratch_shapes=[pltpu.VMEM((HG_HEADS, HG_DIM, HG_DIM), F32)] + [pltpu.VMEM((HG_TILE, HG_WIDTH), F32)] * 4,
        compiler_params=_params("arbitrary"),
    )(hq, hf, hi, hgt, logits, onorm_g, o, states, dout)


def _att_consts():
    lane = lax.broadcasted_iota(jnp.int32, (ATT_BLOCK, 128), 1)
    qi = lax.broadcasted_iota(jnp.int32, (2 * ATT_BLOCK, ATT_BLOCK), 0) % ATT_BLOCK
    kj = lax.broadcasted_iota(jnp.int32, (2 * ATT_BLOCK, ATT_BLOCK), 1)
    return lane < ATT_HEAD_DIM, kj <= qi, lambda off: kj >= qi + off


def _stack_heads(x2, first):
    return jnp.concatenate([jnp.where(first, x2, 0.0), jnp.where(first, 0.0, x2)], axis=0)


def _stack_bcast(x2, first):
    other = pltpu.roll(x2, ATT_HEAD_DIM, axis=1)
    return jnp.concatenate([jnp.where(first, x2, other), jnp.where(first, other, x2)], axis=0)


def _unstack_heads(st, first):
    return jnp.where(first, st[:ATT_BLOCK], st[ATT_BLOCK:])


def _att_fwd(q, k, v, dil):
    m, width = q.shape
    nb = m // ATT_BLOCK

    def body(q_ref, kc_ref, kp_ref, vc_ref, vp_ref, o_ref, lse_ref):
        first, cur_ok, _band = _att_consts()
        prev_ok = _band(jnp.where(pl.program_id(1) > 0, 0, ATT_BLOCK))
        for j in range(ATT_WIDTH // 128):
            c = slice(j * 128, (j + 1) * 128)
            qst = _stack_heads(q_ref[:, c] * ATT_SCALE, first).astype(BF16)
            kc, kp = kc_ref[:, c].astype(BF16), kp_ref[:, c].astype(BF16)
            vc, vp = vc_ref[:, c].astype(BF16), vp_ref[:, c].astype(BF16)
            sc = jnp.where(cur_ok, _dot_nt(qst, kc), NEG)
            sp = jnp.where(prev_ok, _dot_nt(qst, kp), NEG)
            mx = jnp.maximum(jnp.max(sc, axis=-1, keepdims=True), jnp.max(sp, axis=-1, keepdims=True))
            pc, pp = jnp.exp(sc - mx), jnp.exp(sp - mx)
            den = jnp.sum(pc, axis=-1, keepdims=True) + jnp.sum(pp, axis=-1, keepdims=True)
            ost = (_dot(pc.astype(BF16), vc) + _dot(pp.astype(BF16), vp)) / den
            lse = jnp.broadcast_to(mx + jnp.log(den), (2 * ATT_BLOCK, 128))
            o_ref[:, c] = _unstack_heads(ost, first)
            lse_ref[:, c] = _unstack_heads(lse, first)

    cur = pl.BlockSpec((ATT_BLOCK, ATT_WIDTH), lambda r, n: (n, r))
    prev = pl.BlockSpec((ATT_BLOCK, ATT_WIDTH), lambda r, n: (jnp.maximum(n - 1, 0), r))
    return pl.pallas_call(
        body, name=f"att_fwd_d{dil}", grid=(dil, nb),
        out_shape=[jax.ShapeDtypeStruct((m, width), F32)] * 2,
        in_specs=[cur, cur, prev, cur, prev], out_specs=[cur, cur],
        compiler_params=_params("parallel", "arbitrary"),
    )(q, k, k, v, v)


def _att_bwd(q, k, v, do, cc, lse, dil):
    m, width = q.shape
    nb = m // ATT_BLOCK

    def body(q_ref, qx_ref, kc_ref, kp_ref, vc_ref, vp_ref, do_ref, dox_ref, cc_ref, ccx_ref, lse_ref, lsex_ref,
             dq_ref, dk_ref, dv_ref):
        first, cur_ok, _band = _att_consts()
        n = pl.program_id(1)
        prev_ok = _band(jnp.where(n > 0, 0, ATT_BLOCK))
        next_ok = _band(jnp.where(n < nb - 1, 0, ATT_BLOCK))
        for j in range(ATT_WIDTH // 128):
            c = slice(j * 128, (j + 1) * 128)
            qst = _stack_heads(q_ref[:, c] * ATT_SCALE, first).astype(BF16)
            qxst = _stack_heads(qx_ref[:, c] * ATT_SCALE, first).astype(BF16)
            dost = _stack_heads(do_ref[:, c], first).astype(BF16)
            doxst = _stack_heads(dox_ref[:, c], first).astype(BF16)
            lse_n, lse_x = _stack_bcast(lse_ref[:, c], first), _stack_bcast(lsex_ref[:, c], first)
            cc_n, cc_x = _stack_bcast(cc_ref[:, c], first), _stack_bcast(ccx_ref[:, c], first)
            kc, kp = kc_ref[:, c].astype(BF16), kp_ref[:, c].astype(BF16)
            vc, vp = vc_ref[:, c].astype(BF16), vp_ref[:, c].astype(BF16)
            p_cur = jnp.exp(jnp.where(cur_ok, _dot_nt(qst, kc), NEG) - lse_n)
            p_prev = jnp.exp(jnp.where(prev_ok, _dot_nt(qst, kp), NEG) - lse_n)
            p_next = jnp.exp(jnp.where(next_ok, _dot_nt(qxst, kc), NEG) - lse_x)
            ds_cur = (p_cur * (_dot_nt(dost, vc) + cc_n)).astype(BF16)
            ds_prev = (p_prev * (_dot_nt(dost, vp) + cc_n)).astype(BF16)
            ds_next = (p_next * (_dot_nt(doxst, vc) + cc_x)).astype(BF16)
            dq_ref[:, c] = _unstack_heads(_dot(ds_cur, kc) + _dot(ds_prev, kp), first) * ATT_SCALE
            dk_ref[:, c] = _dot_tn(ds_cur, qst) + _dot_tn(ds_next, qxst)
            dv_ref[:, c] = _dot_tn(p_cur.astype(BF16), dost) + _dot_tn(p_next.astype(BF16), doxst)

    cur = pl.BlockSpec((ATT_BLOCK, ATT_WIDTH), lambda r, n: (n, r))
    prev = pl.BlockSpec((ATT_BLOCK, ATT_WIDTH), lambda r, n: (jnp.maximum(n - 1, 0), r))
    nxt = pl.BlockSpec((ATT_BLOCK, ATT_WIDTH), lambda r, n: (jnp.minimum(n + 1, nb - 1), r))
    return pl.pallas_call(
        body, name=f"att_bwd_d{dil}", grid=(dil, nb),
        out_shape=[jax.ShapeDtypeStruct((m, width), F32)] * 3,
        in_specs=[cur, nxt, cur, prev, cur, prev, cur, nxt, cur, nxt, cur, nxt], out_specs=[cur, cur, cur],
        compiler_params=_params("parallel", "arbitrary"),
    )(q, q, k, k, v, v, do, do, cc, cc, lse, lse)


def _branch_weights(lses):
    mx = jnp.maximum(jnp.maximum(lses[0], lses[1]), lses[2])
    es = [jnp.exp(l - mx) for l in lses]
    inv = 1.0 / (es[0] + es[1] + es[2])
    return [e * inv for e in es]


def _att_combine(outs, lses, att_g):
    s = outs[0].shape[0]
    tm = 512

    def body(o0, o1, o2, l0, l1, l2, g_ref, att_ref, out_ref):
        ws = _branch_weights([l0[...], l1[...], l2[...]])
        att = ws[0] * o0[...] + ws[1] * o1[...] + ws[2] * o2[...]
        att_ref[...] = att
        ahat, _ = _rms(att)
        out_ref[...] = (ahat * g_ref[...]).astype(BF16)

    tile = _rows(tm, ATT_WIDTH)
    return pl.pallas_call(
        body, name="att_combine", grid=(s // tm,),
        out_shape=[jax.ShapeDtypeStruct((s, ATT_WIDTH), F32), jax.ShapeDtypeStruct((s, ATT_WIDTH), BF16)],
        in_specs=[tile] * 6 + [_whole((1, ATT_WIDTH))], out_specs=[tile, tile],
        compiler_params=_params("parallel"),
    )(*outs, *lses, att_g)


def _att_combine_bwd(datt_out, att, lses, att_g):
    s = att.shape[0]
    tm = 256

    def body(d_ref, att_ref, l0, l1, l2, g_ref, do0, do1, do2, cc0, cc1, cc2, dg_ref):
        @pl.when(pl.program_id(0) == 0)
        def _():
            dg_ref[...] = jnp.zeros_like(dg_ref)

        att = att_ref[...]
        ahat, rstd = _rms(att)
        d = d_ref[...]
        dg_ref[...] += _rowsum(d * ahat)
        datt = _rms_bwd(d * g_ref[...], ahat, rstd)
        hi = lax.broadcasted_iota(jnp.int32, (ATT_WIDTH, ATT_WIDTH), 0) // ATT_HEAD_DIM
        hj = lax.broadcasted_iota(jnp.int32, (ATT_WIDTH, ATT_WIDTH), 1) // ATT_HEAD_DIM
        head_sum = _dot_f32(datt * att, (hi == hj).astype(F32))
        ws = _branch_weights([l0[...], l1[...], l2[...]])
        for w, do_ref, cc_ref in zip(ws, (do0, do1, do2), (cc0, cc1, cc2)):
            do_ref[...] = w * datt
            cc_ref[...] = -w * head_sum

    tile = _rows(tm, ATT_WIDTH)
    return pl.pallas_call(
        body, name="att_combine_bwd", grid=(s // tm,),
        out_shape=[jax.ShapeDtypeStruct((s, ATT_WIDTH), F32)] * 6 + [jax.ShapeDtypeStruct((1, ATT_WIDTH), F32)],
        in_specs=[tile] * 5 + [_whole((1, ATT_WIDTH))], out_specs=[tile] * 6 + [_whole((1, ATT_WIDTH))],
        compiler_params=_params("arbitrary"),
    )(datt_out, att, *lses, att_g)


def _out_fwd(x, hg, at, mod, w_out):
    s = x.shape[0]
    tm = 512

    def body(x_ref, hg_ref, at_ref, mod_ref, w_ref, x1_ref):
        mix = _dot(hg_ref[...], w_ref[0:512, :]) + _dot(at_ref[...], w_ref[512:1024, :])
        x1_ref[...] = x_ref[...] + mod_ref[:, 2 * D_MODEL:3 * D_MODEL] * mix

    return pl.pallas_call(
        body, name="out_fwd", grid=(s // tm,), out_shape=jax.ShapeDtypeStruct((s, D_MODEL), F32),
        in_specs=[_rows(tm, D_MODEL), _rows(tm, 512), _rows(tm, 512), _whole((1, 6 * D_MODEL)), _whole((D_MODEL, D_MODEL))],
        out_specs=_rows(tm, D_MODEL), compiler_params=_params("parallel"),
    )(x, hg, at, mod, w_out)


def _out_bwd(dx1, hg, at, mod, w_out):
    s = dx1.shape[0]
    tm = 512

    def body(dx_ref, hg_ref, at_ref, mod_ref, w_ref, dhg_ref, dat_ref, dw_ref, dgate_ref):
        @pl.when(pl.program_id(0) == 0)
        def _():
            dw_ref[...] = jnp.zeros_like(dw_ref)
            dgate_ref[...] = jnp.zeros_like(dgate_ref)

        hg, at, dx = hg_ref[...], at_ref[...], dx_ref[...]
        mix = _dot(hg, w_ref[0:512, :]) + _dot(at, w_ref[512:1024, :])
        dgate_ref[...] += _rowsum(dx * mix)
        dmix = (mod_ref[:, 2 * D_MODEL:3 * D_MODEL] * dx).astype(BF16)
        dhg_ref[...] = _dot_nt(dmix, w_ref[0:512, :])
        dat_ref[...] = _dot_nt(dmix, w_ref[512:1024, :])
        dw_ref[0:512, :] += _dot_tn(hg, dmix)
        dw_ref[512:1024, :] += _dot_tn(at, dmix)

    return pl.pallas_call(
        body, name="out_bwd", grid=(s // tm,),
        out_shape=[jax.ShapeDtypeStruct((s, 512), F32)] * 2
        + [jax.ShapeDtypeStruct((D_MODEL, D_MODEL), F32), jax.ShapeDtypeStruct((1, D_MODEL), F32)],
        in_specs=[_rows(tm, D_MODEL), _rows(tm, 512), _rows(tm, 512), _whole((1, 6 * D_MODEL)), _whole((D_MODEL, D_MODEL))],
        out_specs=[_rows(tm, 512), _rows(tm, 512), _whole((D_MODEL, D_MODEL)), _whole((1, D_MODEL))],
        compiler_params=_params("arbitrary"),
    )(dx1, hg, at, mod, w_out)


FFN_CHUNK = 256


def _ffn(x1, target, mod, g2, gf, w_gu, w_down):
    s = x1.shape[0]
    tm = 256
    n_chunks = D_FF // FFN_CHUNK

    def body(x_ref, t_ref, mod_ref, g2_ref, gf_ref, wgu_hbm, wd_hbm,
             dx_ref, h2_ref, act_ref, dau_ref, dff_ref, sums_ref, loss_ref, wgu, wd, a_s, u_s, sem):
        @pl.when(pl.program_id(0) == 0)
        def _():
            c1 = pltpu.make_async_copy(wgu_hbm, wgu, sem.at[0])
            c2 = pltpu.make_async_copy(wd_hbm, wd, sem.at[1])
            c1.start()
            c2.start()
            c1.wait()
            c2.wait()
            sums_ref[...] = jnp.zeros_like(sums_ref)
            loss_ref[...] = jnp.zeros_like(loss_ref)

        x1v = x_ref[...]
        xhat, rstd = _rms(x1v)
        g2 = g2_ref[...]
        n2 = xhat * g2
        scale2 = 1.0 + mod_ref[:, 4 * D_MODEL:5 * D_MODEL]
        gate2 = mod_ref[:, 5 * D_MODEL:6 * D_MODEL]
        hb = (n2 * scale2 + mod_ref[:, 3 * D_MODEL:4 * D_MODEL]).astype(BF16)
        h2_ref[...] = hb
        ff = jnp.zeros((tm, D_MODEL), F32)
        for j in range(n_chunks):
            c = slice(j * FFN_CHUNK, (j + 1) * FFN_CHUNK)
            cu = slice(D_FF + j * FFN_CHUNK, D_FF + (j + 1) * FFN_CHUNK)
            a = _dot(hb, wgu[:, c])
            u = _dot(hb, wgu[:, cu])
            a_s[:, c] = a
            u_s[:, c] = u
            act = (_silu(a) * u).astype(BF16)
            act_ref[:, c] = act
            ff += _dot(act, wd[c, :])
        x2 = x1v + gate2 * ff
        nf, rstd_f = _rms(x2)
        gfv = gf_ref[...]
        err = nf * gfv - t_ref[...]
        loss_ref[...] += 0.5 * jnp.sum(_rowsum(err * err), axis=-1, keepdims=True) * (1.0 / D_MODEL)
        dy = err * (1.0 / D_MODEL)
        dx2 = _rms_bwd(dy * gfv, nf, rstd_f)
        dffb = (gate2 * dx2).astype(BF16)
        dff_ref[...] = dffb
        dh = jnp.zeros((tm, D_MODEL), F32)
        for j in range(n_chunks):
            c = slice(j * FFN_CHUNK, (j + 1) * FFN_CHUNK)
            cu = slice(D_FF + j * FFN_CHUNK, D_FF + (j + 1) * FFN_CHUNK)
            dact = _dot_nt(dffb, wd[c, :])
            a, u = a_s[:, c], u_s[:, c]
            da = (dact * u * _dsilu(a)).astype(BF16)
            du = (dact * _silu(a)).astype(BF16)
            dau_ref[:, c] = da
            dau_ref[:, cu] = du
            dh += _dot_nt(da, wgu[:, c]) + _dot_nt(du, wgu[:, cu])
        dn = dh * scale2
        sums_ref[0:1, :] += _rowsum(dh)
        sums_ref[1:2, :] += _rowsum(dh * n2)
        sums_ref[2:3, :] += _rowsum(dx2 * ff)
        sums_ref[3:4, :] += _rowsum(dn * xhat)
        sums_ref[4:5, :] += _rowsum(dy * nf)
        dx_ref[...] = dx2 + _rms_bwd(dn * g2, xhat, rstd)

    vec = _whole((1, D_MODEL))
    hbm = pl.BlockSpec(memory_space=pl.ANY)
    return pl.pallas_call(
        body, name="ffn", grid=(s // tm,),
        out_shape=[jax.ShapeDtypeStruct((s, D_MODEL), F32), jax.ShapeDtypeStruct((s, D_MODEL), BF16),
                   jax.ShapeDtypeStruct((s, D_FF), BF16), jax.ShapeDtypeStruct((s, 2 * D_FF), BF16),
                   jax.ShapeDtypeStruct((s, D_MODEL), BF16), jax.ShapeDtypeStruct((8, D_MODEL), F32),
                   jax.ShapeDtypeStruct((1, 128), F32)],
        in_specs=[_rows(tm, D_MODEL), _rows(tm, D_MODEL), _whole((1, 6 * D_MODEL)), vec, vec, hbm, hbm],
        out_specs=[_rows(tm, D_MODEL), _rows(tm, D_MODEL), _rows(tm, D_FF), _rows(tm, 2 * D_FF), _rows(tm, D_MODEL),
                   _whole((8, D_MODEL)), _whole((1, 128))],
        scratch_shapes=[pltpu.VMEM((D_MODEL, 2 * D_FF), BF16), pltpu.VMEM((D_FF, D_MODEL), BF16),
                        pltpu.VMEM((tm, D_FF), F32), pltpu.VMEM((tm, D_FF), F32), pltpu.SemaphoreType.DMA((2,))],
        compiler_params=_params("arbitrary"),
    )(x1, target, mod, g2, gf, w_gu, w_down)


def _weight_grad(a, b, name):
    s, m = a.shape
    n = b.shape[1]
    ts, tn = 512, 512

    def body(a_ref, b_ref, o_ref):
        @pl.when(pl.program_id(1) == 0)
        def _():
            o_ref[...] = jnp.zeros_like(o_ref)

        o_ref[...] += _dot_tn(a_ref[...], b_ref[...])

    return pl.pallas_call(
        body, name=name, grid=(n // tn, s // ts), out_shape=jax.ShapeDtypeStruct((m, n), F32),
        in_specs=[pl.BlockSpec((ts, m), lambda j, i: (i, 0)), pl.BlockSpec((ts, tn), lambda j, i: (i, j))],
        out_specs=pl.BlockSpec((m, tn), lambda j, i: (0, j)),
        compiler_params=_params("parallel", "arbitrary"),
    )(a, b)


def _adamw_math(w, g, m, v):
    m = ADAM_B1 * m + (1.0 - ADAM_B1) * g
    v = ADAM_B2 * v + (1.0 - ADAM_B2) * (g * g)
    m_hat = m / (1.0 - ADAM_B1 ** ADAM_STEP)
    v_hat = v / (1.0 - ADAM_B2 ** ADAM_STEP)
    delta = -ADAM_LR * (m_hat / (jnp.sqrt(v_hat) + ADAM_EPS) + ADAM_WD * w)
    return delta, m, v


def _adamw_shard(w, m, v, own, got, name):
    r, c = w.shape
    tr = r // 2 if r % 16 == 0 else r

    def body(w_ref, m_ref, v_ref, own_ref, g0, g1, g2, grad_ref, d_ref, nm_ref, nv_ref):
        g = ((own_ref[...] + g0[0]) + g1[0]) + g2[0]
        grad_ref[...] = g
        d_ref[...], nm_ref[...], nv_ref[...] = _adamw_math(w_ref[...], g, m_ref[...], v_ref[...])

    tile = _rows(tr, c)
    part = [pl.BlockSpec((1, tr, c), functools.partial(lambda j, i: (j, i, 0), j)) for j in range(3)]
    return pl.pallas_call(
        body, name=name, grid=(r // tr,), out_shape=[jax.ShapeDtypeStruct((r, c), F32)] * 4,
        in_specs=[tile] * 4 + part, out_specs=[tile] * 4, compiler_params=_params("parallel"),
    )(w, m, v, own, got, got, got)


def _small_update(small_all, dmod_blocks, c_all, logits, w_ada, m_ada, v_ada, smalls):
    def body(sm_ref, dm_ref, c_ref, lg_ref, wa_ref, ma_ref, va_ref, *rest):
        ins, outs = rest[:21], rest[21:]
        _, me = _flip(0)
        tot = sm_ref[0:1, :]
        for i in range(1, N_DEV):
            tot = tot + sm_ref[i:i + 1, :]
        loss_ref = outs[0]
        loss_ref[...] = tot[:, SM_LOSS:SM_LOSS + 128]
        g_ada = lax.dot_general(_silu(c_ref[...]), dm_ref[me], (((0,), (0,)), ((), ())),
                                preferred_element_type=F32, precision=HIGHEST)
        outs[1][...] = g_ada
        outs[2][...], outs[3][...], outs[4][...] = _adamw_math(wa_ref[...], g_ada, ma_ref[...], va_ref[...])
        p0 = _lower_bound(lg_ref)
        dl0 = tot[:, SM_LB:SM_LB + 512] * p0 * (1.0 - p0)
        grads = [tot[:, SM_MOD:SM_MOD + 6 * D_MODEL], tot[:, SM_G1:SM_G1 + D_MODEL], tot[:, SM_G2:SM_G2 + D_MODEL],
                 tot[:, SM_GF:SM_GF + D_MODEL], tot[:, SM_ATT:SM_ATT + 512], tot[:, SM_HG:SM_HG + 128],
                 jnp.where(lax.broadcasted_iota(jnp.int32, (2, 512), 0) == 0, dl0, -dl0)]
        for i, g in enumerate(grads):
            w_ref, m_ref, v_ref = ins[3 * i:3 * i + 3]
            o = outs[5 + 4 * i:9 + 4 * i]
            o[0][...] = g
            o[1][...], o[2][...], o[3][...] = _adamw_math(w_ref[...], g, m_ref[...], v_ref[...])

    flat = [t for trio in smalls for t in trio]
    vm = pl.BlockSpec(memory_space=pltpu.VMEM)
    out_shape = [jax.ShapeDtypeStruct((1, 128), F32)] + [jax.ShapeDtypeStruct(w_ada.shape, F32)] * 4
    for trio in smalls:
        out_shape += [jax.ShapeDtypeStruct(trio[0].shape, F32)] * 4
    return pl.pallas_call(
        body, name="small_update", out_shape=out_shape,
        in_specs=[vm] * (7 + len(flat)), out_specs=[vm] * len(out_shape),
        compiler_params=pltpu.CompilerParams(vmem_limit_bytes=V7X_VMEM_LIMIT),
    )(small_all, dmod_blocks, c_all, logits, w_ada, m_ada, v_ada, *flat)


def _dilate(t, d):
    return t if d == 1 else t.reshape(t.shape[0] // d, d * t.shape[1])


def _undilate(t, d):
    return t if d == 1 else t.reshape(t.shape[0] * d, t.shape[1] // d)


def kernel(x, c, w_ada, b_ada, norm1_g, w_in, hg_lb_logits, hg_onorm_g, att_onorm_g, w_out, norm2_g, w_gate_up, w_down, final_g, loss_target, m_w_ada, m_b_ada, m_norm1_g, m_w_in, m_hg_lb_logits, m_hg_onorm_g, m_att_onorm_g, m_w_out, m_norm2_g, m_w_gate_up, m_w_down, m_final_g, v_w_ada, v_b_ada, v_norm1_g, v_w_in, v_hg_lb_logits, v_hg_onorm_g, v_att_onorm_g, v_w_out, v_norm2_g, v_w_gate_up, v_w_down, v_final_g):
    x2d, target = x[0], loss_target[0]
    seq = x2d.shape[0]
    assert seq % (ATT_BLOCK * max(DILATIONS)) == 0 and seq % HG_TILE == 0
    gf = final_g.reshape(1, D_MODEL)

    c_all = _exchange_small(jnp.broadcast_to(c, (8, D_MODEL)), None, "gather_c")[:, 0, :]
    ada = _ada_rows(c_all, w_ada[0], b_ada)
    mod = _exchange_small(ada, 1, "scatter_mod").reshape(1, 6 * D_MODEL)

    shards = [w_in[0].astype(BF16), w_out[0].astype(BF16), w_gate_up[0].astype(BF16), w_down[0].astype(BF16)]
    g_in, g_out, g_gu, g_down = _gather_weights(shards)
    w_in_b = jnp.transpose(g_in, (1, 0, 2)).reshape(D_MODEL, IN_WIDTH)
    w_out_b = g_out.reshape(D_MODEL, D_MODEL)
    w_gu_b = jnp.transpose(g_gu, (1, 0, 2)).reshape(D_MODEL, 2 * D_FF)
    w_down_b = g_down.reshape(D_FF, D_MODEL)

    grad_x, dw_in, dw_out, dw_gu, dw_down, small = _block_step(
        x2d, target, mod, norm1_g, hg_lb_logits, hg_onorm_g, att_onorm_g, norm2_g, gf, w_in_b, w_out_b, w_gu_b, w_down_b)

    grads8 = [dw_in.reshape(D_MODEL, N_DEV, IN_WIDTH // N_DEV).transpose(1, 0, 2),
              dw_out.reshape(N_DEV, D_MODEL // N_DEV, D_MODEL),
              dw_gu.reshape(D_MODEL, N_DEV, 2 * D_FF // N_DEV).transpose(1, 0, 2),
              dw_down.reshape(N_DEV, D_FF // N_DEV, D_MODEL)]
    mine, got = _reduce_pairs(grads8)
    partials = [_pair_sum(a, b, f"pair_sum_{i}") for i, (a, b) in enumerate(zip(mine, got))]
    own, recv = _reduce_chips(partials)
    big = []
    for i, (w, m, v) in enumerate([(w_in, m_w_in, v_w_in), (w_out, m_w_out, v_w_out),
                                   (w_gate_up, m_w_gate_up, v_w_gate_up), (w_down, m_w_down, v_w_down)]):
        big.append([t[None] for t in _adamw_shard(w[0], m[0], v[0], own[i], recv[i], f"adamw_{i}")])

    small_all = _exchange_small(jnp.broadcast_to(small, (8, SM_WIDTH)), None, "gather_small")[:, 0, :]
    smalls = [(b_ada, m_b_ada, v_b_ada), (norm1_g, m_norm1_g, v_norm1_g), (norm2_g, m_norm2_g, v_norm2_g),
              (gf, m_final_g.reshape(1, D_MODEL), v_final_g.reshape(1, D_MODEL)),
              (att_onorm_g, m_att_onorm_g, v_att_onorm_g), (hg_onorm_g, m_hg_onorm_g, v_hg_onorm_g),
              (hg_lb_logits, m_hg_lb_logits, v_hg_lb_logits)]
    dmod_blocks = small_all[:, :6 * D_MODEL].reshape(N_DEV, N_DEV, 6 * D_MODEL // N_DEV).transpose(1, 0, 2)
    res = _small_update(small_all, dmod_blocks, c_all, hg_lb_logits, w_ada[0], m_w_ada[0], v_w_ada[0], smalls)
    loss = res[0][0, 0]
    ada4 = [t[None] for t in res[1:5]]
    sm4 = {n: list(res[5 + 4 * i:9 + 4 * i]) for i, n in enumerate(["b_ada", "norm1_g", "norm2_g", "final_g", "att", "hg", "lb"])}
    sm4["final_g"] = [t.reshape(D_MODEL) for t in sm4["final_g"]]

    order = [ada4, sm4["b_ada"], sm4["norm1_g"], big[0], sm4["lb"], sm4["hg"], sm4["att"], big[1], sm4["norm2_g"], big[2], big[3],
             sm4["final_g"]]
    return (loss, grad_x[None], *[o[0] for o in order], *[o[1] for o in order], *[o[2] for o in order], *[o[3] for o in order])


def _block_step(x2d, target, mod, norm1_g, hg_lb_logits, hg_onorm_g, att_onorm_g, norm2_g, gf, w_in_b, w_out_b, w_gu_b, w_down_b):
    h1, hq, hf, hi, hgt, aq, ak, av = _in_fwd(x2d, mod, norm1_g, w_in_b)
    hg_out, hg_o, hg_states = _hg_fwd(hq, hf, hi, hgt, hg_lb_logits, hg_onorm_g)
    views = [(_dilate(aq, d), _dilate(ak, d), _dilate(av, d)) for d in DILATIONS]
    branch = [_att_fwd(*views[i], d) for i, d in enumerate(DILATIONS)]
    outs = [_undilate(branch[i][0], d) for i, d in enumerate(DILATIONS)]
    lses = [_undilate(branch[i][1], d) for i, d in enumerate(DILATIONS)]
    att, att_out = _att_combine(outs, lses, att_onorm_g)
    x1 = _out_fwd(x2d, hg_out, att_out, mod, w_out_b)

    dx1, h2, act, dau, dff, ffn_sums, loss_part = _ffn(x1, target, mod, norm2_g, gf, w_gu_b, w_down_b)
    dw_gu = _weight_grad(h2, dau, "dw_gate_up")
    dw_down = _weight_grad(act, dff, "dw_down")

    dhg, dat, dw_out, dgate1 = _out_bwd(dx1, hg_out, att_out, mod, w_out_b)
    comb = _att_combine_bwd(dat, att, lses, att_onorm_g)
    dos, ccs, d_att_g = comb[0:3], comb[3:6], comb[6]
    datt = []
    for i, d in enumerate(DILATIONS):
        grads = _att_bwd(*views[i], _dilate(dos[i], d), _dilate(ccs[i], d), branch[i][1], d)
        datt.append([_undilate(g, d) for g in grads])
    dhq, dhf, dhi, dhgt, d_hg_g, d_lb = _hg_bwd(hq, hf, hi, hgt, hg_lb_logits, hg_onorm_g, hg_o, hg_states, dhg)
    dps = [dhq, dhf, dhi, dhgt] + [datt[i][j] for j in range(3) for i in range(3)]
    grad_x, dp_b, dshift1, dscale1, d_g1 = _in_bwd(x2d, dx1, mod, norm1_g, w_in_b, dps)
    dw_in = _weight_grad(h1, dp_b, "dw_in")
    small = jnp.concatenate([dshift1, dscale1, dgate1, ffn_sums[0:1], ffn_sums[1:2], ffn_sums[2:3], d_g1, ffn_sums[3:4],
                             ffn_sums[4:5], d_att_g, d_lb, d_hg_g, loss_part], axis=1)
    return grad_x, dw_in, dw_out, dw_gu, dw_down, small
```

```python
import functools

import jax
import jax.numpy as jnp
from jax import lax
from jax.experimental import pallas as pl
from jax.experimental.pallas import tpu as pltpu

F32 = jnp.float32
BF16 = jnp.bfloat16
HIGHEST = lax.Precision.HIGHEST
MESH = pl.DeviceIdType.MESH

D_MODEL = 1024
N_DEV = 8
HG_HEADS = 4
HG_DIM = 128
HG_WIDTH = HG_HEADS * HG_DIM
HG_CHUNK = 64
ATT_WIDTH = 512
ATT_HEAD_DIM = 64
ATT_BLOCK = 128
DILATIONS = (1, 4, 16)
ATT_SCALE = ATT_HEAD_DIM ** -0.5
D_FF = 2816
IN_WIDTH = 7 * 512
RMS_EPS = 1e-6
NEG = -1e30

ADAM_LR = 0.001
ADAM_B1 = 0.9
ADAM_B2 = 0.999
ADAM_EPS = 1e-08
ADAM_WD = 0.01
ADAM_STEP = 10

V7X_VMEM_LIMIT = 56 * 1024 * 1024

SM_MOD = 0
SM_G1 = 6 * D_MODEL
SM_G2 = 7 * D_MODEL
SM_GF = 8 * D_MODEL
SM_ATT = 9 * D_MODEL
SM_LB = 9 * D_MODEL + 512
SM_HG = 10 * D_MODEL
SM_LOSS = 10 * D_MODEL + 128
SM_WIDTH = 10 * D_MODEL + 256


def _params(*sem, vmem=V7X_VMEM_LIMIT):
    return pltpu.CompilerParams(dimension_semantics=sem, vmem_limit_bytes=vmem)


def _dot(a, b):
    return jnp.dot(a, b, preferred_element_type=F32)


def _dot_nt(a, b):
    return lax.dot_general(a, b, (((1,), (1,)), ((), ())), preferred_element_type=F32)


def _dot_tn(a, b):
    return lax.dot_general(a, b, (((0,), (0,)), ((), ())), preferred_element_type=F32)


def _dot_f32(a, b):
    return jnp.dot(a, b, preferred_element_type=F32, precision=HIGHEST)


def _sigmoid(x):
    return 1.0 / (1.0 + jnp.exp(-x))


def _silu(x):
    return x * _sigmoid(x)


def _dsilu(x):
    s = _sigmoid(x)
    return s * (1.0 + x * (1.0 - s))


def _rms(x):
    rstd = lax.rsqrt(jnp.mean(x * x, axis=-1, keepdims=True) + RMS_EPS)
    return x * rstd, rstd


def _rms_bwd(dn, xhat, rstd):
    return rstd * (dn - xhat * jnp.mean(dn * xhat, axis=-1, keepdims=True))


def _rowsum(x):
    return jnp.sum(x, axis=0, keepdims=True)


def _rows(tm, n):
    return pl.BlockSpec((tm, n), lambda i: (i, 0))


def _whole(shape):
    return pl.BlockSpec(shape, lambda i: (0,) * len(shape))


def _mesh_pos():
    return lax.axis_index("x"), lax.axis_index("y"), lax.axis_index("c")


def _flip(k):
    x, y, c = _mesh_pos()
    px = 1 - x if k & 4 else x
    py = 1 - y if k & 2 else y
    pc = 1 - c if k & 1 else c
    return (px, py, pc), 4 * px + 2 * py + pc


def _exchange_small(x, rows_per_peer, name):
    r_all, cols = x.shape
    r_out = r_all if rows_per_peer is None else rows_per_peer

    def body(x_ref, out_ref, send_sems, recv_sems):
        _, me = _flip(0)

        def src(pid):
            if rows_per_peer is None:
                return x_ref
            return x_ref.at[pl.ds(pl.multiple_of(pid * r_out, r_out), r_out), :]

        if rows_per_peer is None:
            out_ref[me] = x_ref[...]
        else:
            out_ref[me] = x_ref[pl.ds(pl.multiple_of(me * r_out, r_out), r_out), :]
        sends = []
        for k in range(1, N_DEV):
            dev, pid = _flip(k)
            cp = pltpu.make_async_remote_copy(src_ref=src(pid), dst_ref=out_ref.at[me], send_sem=send_sems.at[k - 1],
                                              recv_sem=recv_sems.at[k - 1], device_id=dev, device_id_type=MESH)
            cp.start()
            sends.append(cp)
        for k in range(1, N_DEV):
            dev, pid = _flip(k)
            pltpu.make_async_remote_copy(src_ref=src(pid), dst_ref=out_ref.at[pid], send_sem=send_sems.at[k - 1],
                                         recv_sem=recv_sems.at[k - 1], device_id=dev, device_id_type=MESH).wait_recv()
        for cp in sends:
            cp.wait_send()

    return pl.pallas_call(
        body, name=name,
        out_shape=jax.ShapeDtypeStruct((N_DEV, r_out, cols), x.dtype),
        in_specs=[pl.BlockSpec(memory_space=pltpu.VMEM)],
        out_specs=pl.BlockSpec(memory_space=pltpu.VMEM),
        scratch_shapes=[pltpu.SemaphoreType.DMA((N_DEV - 1,)), pltpu.SemaphoreType.DMA((N_DEV - 1,))],
    )(x)


def _gather_weights(shards):
    n = len(shards)

    def body(*refs):
        xs, outs = refs[:n], refs[n:2 * n]
        send_sems, recv_sems, local_sems = refs[2 * n:]
        x, y, c = _mesh_pos()
        me, sibling = (x, y, c), (x, y, 1 - c)
        chips = [(1 - x, y), (x, 1 - y), (1 - x, 1 - y)]

        def blk(a, px, py, pc):
            return outs[a].at[4 * px + 2 * py + pc]

        def copy(a, k, block, to, src=None):
            return pltpu.make_async_remote_copy(
                src_ref=blk(a, *block) if src is None else src, dst_ref=blk(a, *block),
                send_sem=send_sems.at[a * 7 + k], recv_sem=recv_sems.at[a * 7 + k], device_id=to, device_id_type=MESH)

        mine = [pltpu.make_async_copy(xs[a], blk(a, *me), local_sems.at[a]) for a in range(n)]
        for cp in mine:
            cp.start()
        first = []
        for a in range(n):
            first.append(copy(a, 0, me, sibling, src=xs[a]))
            first += [copy(a, 1 + j, me, (*chip, c), src=xs[a]) for j, chip in enumerate(chips)]
        for cp in first:
            cp.start()
        passed = []
        for j, chip in enumerate(chips):
            for a in range(n):
                copy(a, 1 + j, (*chip, c), me).wait_recv()
                cp = copy(a, 4 + j, (*chip, c), sibling)
                cp.start()
                passed.append(cp)
        for a in range(n):
            copy(a, 0, sibling, me).wait_recv()
            for j, chip in enumerate(chips):
                copy(a, 4 + j, (*chip, 1 - c), me).wait_recv()
        for cp in first + passed:
            cp.wait_send()
        for cp in mine:
            cp.wait()

    hbm = pl.BlockSpec(memory_space=pl.ANY)
    return pl.pallas_call(
        body, name="gather_weights",
        out_shape=[jax.ShapeDtypeStruct((N_DEV,) + s.shape, s.dtype) for s in shards],
        in_specs=[hbm] * n, out_specs=[hbm] * n,
        scratch_shapes=[pltpu.SemaphoreType.DMA((7 * n,)), pltpu.SemaphoreType.DMA((7 * n,)), pltpu.SemaphoreType.DMA((n,))],
    )(*shards)


def _reduce_pairs(grads):
    n = len(grads)

    def body(*refs):
        gs, got = refs[:n], refs[n:2 * n]
        send_sems, recv_sems = refs[2 * n:]
        x, y, c = _mesh_pos()
        sends = []
        for a in range(n):
            for chip in range(4):
                i = a * 4 + chip
                cp = pltpu.make_async_remote_copy(src_ref=gs[a].at[chip, 1 - c], dst_ref=got[a].at[chip],
                                                  send_sem=send_sems.at[i], recv_sem=recv_sems.at[i],
                                                  device_id=(x, y, 1 - c), device_id_type=MESH)
                cp.start()
                sends.append(cp)
        for cp in sends:
            cp.wait_recv()
        for cp in sends:
            cp.wait_send()

    hbm = pl.BlockSpec(memory_space=pl.ANY)
    return pl.pallas_call(
        body, name="reduce_pairs", out_shape=[jax.ShapeDtypeStruct((4,) + g.shape[2:], g.dtype) for g in grads],
        in_specs=[hbm] * n, out_specs=[hbm] * n,
        scratch_shapes=[pltpu.SemaphoreType.DMA((4 * n,)), pltpu.SemaphoreType.DMA((4 * n,))],
    )(*grads)


def _reduce_chips(partials):
    n = len(partials)
    flips = (4, 2, 6)

    def body(*refs):
        ps, got = refs[:n], refs[n:2 * n]
        send_sems, recv_sems = refs[2 * n:]
        sends = []
        for a in range(n):
            for j, k in enumerate(flips):
                dev, _ = _flip(k)
                cp = pltpu.make_async_remote_copy(src_ref=ps[a].at[2 * dev[0] + dev[1]], dst_ref=got[a].at[j],
                                                  send_sem=send_sems.at[a * 3 + j], recv_sem=recv_sems.at[a * 3 + j],
                                                  device_id=dev, device_id_type=MESH)
                cp.start()
                sends.append(cp)
        for cp in sends:
            cp.wait_recv()
        for cp in sends:
            cp.wait_send()

    hbm = pl.BlockSpec(memory_space=pl.ANY)
    return pl.pallas_call(
        body, name="reduce_chips", out_shape=[jax.ShapeDtypeStruct((3,) + p.shape[1:], p.dtype) for p in partials],
        in_specs=[hbm] * n, out_specs=[hbm] * n,
        scratch_shapes=[pltpu.SemaphoreType.DMA((3 * n,)), pltpu.SemaphoreType.DMA((3 * n,))],
    )(*partials)


def _shard_rows(r):
    return r // 2 if r % 32 == 0 else r


def _pair_sum(core, grads, got, name):
    _, _, r, c = grads.shape
    tr = _shard_rows(r)

    def body(core_ref, a_ref, b_ref, o_ref, ob_ref):
        s = a_ref[...] + b_ref[...]
        o_ref[...] = s
        ob_ref[...] = s.astype(BF16)

    spec = pl.BlockSpec((None, tr, c), lambda i, j, core_ref: (i, j, 0))
    return pl.pallas_call(
        body, name=name,
        grid_spec=pltpu.PrefetchScalarGridSpec(
            num_scalar_prefetch=1, grid=(4, r // tr),
            in_specs=[pl.BlockSpec((None, None, tr, c), lambda i, j, core_ref: (i, core_ref[0], j, 0)), spec],
            out_specs=[spec, spec]),
        out_shape=[jax.ShapeDtypeStruct((4, r, c), F32), jax.ShapeDtypeStruct((4, r, c), BF16)],
        compiler_params=_params("parallel", "parallel"),
    )(core, grads, got)


def _ada_rows(c_all, w_ada, b_ada):
    n_cols = w_ada.shape[1]

    def body(c_ref, w_ref, b_ref, o_ref):
        _, me = _flip(0)
        bias = b_ref[:, pl.ds(pl.multiple_of(me * n_cols, 128), n_cols)]
        o_ref[...] = _dot_f32(_silu(c_ref[...]), w_ref[...]) + bias

    return pl.pallas_call(
        body, name="ada_rows", out_shape=jax.ShapeDtypeStruct((N_DEV, n_cols), F32),
        in_specs=[pl.BlockSpec(memory_space=pltpu.VMEM)] * 3, out_specs=pl.BlockSpec(memory_space=pltpu.VMEM),
    )(c_all, w_ada, b_ada)


def _in_fwd(x, mod, g1, w_in):
    s = x.shape[0]
    tm = 256

    def body(x_ref, mod_ref, g_ref, w_ref, h_ref, *outs):
        xhat, _ = _rms(x_ref[...])
        h = (xhat * g_ref[...]) * (1.0 + mod_ref[:, D_MODEL:2 * D_MODEL]) + mod_ref[:, 0:D_MODEL]
        hb = h.astype(BF16)
        h_ref[...] = hb
        for j, o_ref in enumerate(outs):
            o_ref[...] = _dot(hb, w_ref[:, j * 512:(j + 1) * 512])

    return pl.pallas_call(
        body, name="in_fwd", grid=(s // tm,),
        out_shape=[jax.ShapeDtypeStruct((s, D_MODEL), BF16)] + [jax.ShapeDtypeStruct((s, 512), F32)] * 7,
        in_specs=[_rows(tm, D_MODEL), _whole((1, 6 * D_MODEL)), _whole((1, D_MODEL)), _whole((D_MODEL, IN_WIDTH))],
        out_specs=[_rows(tm, D_MODEL)] + [_rows(tm, 512)] * 7,
        compiler_params=_params("parallel"),
    )(x, mod, g1, w_in)


def _in_bwd(x, dx1, mod, g1, w_in, dps):
    s = x.shape[0]
    tm = 256

    def body(x_ref, dx_ref, mod_ref, g_ref, w_ref, *rest):
        dp_refs, (gx_ref, dpb_ref, dsh_ref, dsc_ref, dg_ref) = rest[:13], rest[13:]
        pieces = [dp_refs[j][...] for j in range(4)]
        pieces += [dp_refs[4 + 3 * j][...] + dp_refs[5 + 3 * j][...] + dp_refs[6 + 3 * j][...] for j in range(3)]
        dh = jnp.zeros((tm, D_MODEL), F32)
        for j, p in enumerate(pieces):
            pb = p.astype(BF16)
            dpb_ref[:, j * 512:(j + 1) * 512] = pb
            dh += _dot_nt(pb, w_ref[:, j * 512:(j + 1) * 512])
        xhat, rstd = _rms(x_ref[...])
        g = g_ref[...]
        scale1 = 1.0 + mod_ref[:, D_MODEL:2 * D_MODEL]
        n1 = xhat * g

        @pl.when(pl.program_id(0) == 0)
        def _():
            dsh_ref[...] = jnp.zeros_like(dsh_ref)
            dsc_ref[...] = jnp.zeros_like(dsc_ref)
            dg_ref[...] = jnp.zeros_like(dg_ref)

        dsh_ref[...] += _rowsum(dh)
        dsc_ref[...] += _rowsum(dh * n1)
        dn = dh * scale1
        dg_ref[...] += _rowsum(dn * xhat)
        gx_ref[...] = dx_ref[...] + _rms_bwd(dn * g, xhat, rstd)

    vec = _whole((1, D_MODEL))
    return pl.pallas_call(
        body, name="in_bwd", grid=(s // tm,),
        out_shape=[jax.ShapeDtypeStruct((s, D_MODEL), F32), jax.ShapeDtypeStruct((s, IN_WIDTH), BF16)]
        + [jax.ShapeDtypeStruct((1, D_MODEL), F32)] * 3,
        in_specs=[_rows(tm, D_MODEL), _rows(tm, D_MODEL), _whole((1, 6 * D_MODEL)), vec, _whole((D_MODEL, IN_WIDTH))]
        + [_rows(tm, 512)] * 13,
        out_specs=[_rows(tm, D_MODEL), _rows(tm, IN_WIDTH), vec, vec, vec],
        compiler_params=_params("arbitrary"),
    )(x, dx1, mod, g1, w_in, *dps)


HG_TILE = 512
HG_TILE_CHUNKS = HG_TILE // HG_CHUNK


def _lower_bound(lg_ref):
    return 1.0 / (1.0 + jnp.exp(lg_ref[1:2, :] - lg_ref[0:1, :]))


def _chunk_masks():
    r = lax.broadcasted_iota(jnp.int32, (HG_CHUNK, HG_CHUNK), 0)
    c = lax.broadcasted_iota(jnp.int32, (HG_CHUNK, HG_CHUNK), 1)
    return r >= c, c >= r, (r >= c).astype(F32), (c >= r).astype(F32)


def _hg_fwd(hq, hf, hi, hgt, logits, onorm_g):
    s = hq.shape[0]
    n_tiles = s // HG_TILE

    def body(q_ref, f_ref, i_ref, g_ref, lg_ref, og_ref, out_ref, o_ref, st_ref, state, qf_s, kk_s, lf_s):
        @pl.when(pl.program_id(0) == 0)
        def _():
            state[...] = jnp.zeros_like(state)

        lb = _lower_bound(lg_ref)
        f = lb + (1.0 - lb) * _sigmoid(f_ref[...])
        kk_s[...] = 1.0 - f
        lf_s[...] = jnp.log(f)
        qf_s[...] = _silu(q_ref[...])
        causal, _, tri, _ = _chunk_masks()

        def chunk(ci, carry):
            rows = pl.ds(pl.multiple_of(ci * HG_CHUNK, HG_CHUNK), HG_CHUNK)
            srows = pl.ds(pl.multiple_of(ci * HG_DIM, HG_DIM), HG_DIM)
            lf = lf_s[rows, :]
            b = _dot_f32(tri, lf)
            bl = _rowsum(lf)
            ref = 0.5 * bl
            qf, kk, v = qf_s[rows, :], kk_s[rows, :], i_ref[rows, :]
            a_in = (qf * jnp.exp(b)).astype(BF16)
            a_t = (qf * jnp.exp(b - ref)).astype(BF16)
            b_t = (kk * jnp.exp(ref - b)).astype(BF16)
            kd = kk * jnp.exp(bl - b)
            ebl = jnp.exp(bl)
            vb = v.astype(BF16)
            for h in range(HG_HEADS):
                c = slice(h * HG_DIM, (h + 1) * HG_DIM)
                st = state[h]
                st_ref[srows, c] = st
                p = jnp.where(causal, _dot_nt(a_t[:, c], b_t[:, c]), 0.0)
                o_ref[rows, c] = _dot(p.astype(BF16), vb[:, c]) + _dot_nt(a_in[:, c], st.astype(BF16))
                state[h] = st * ebl[:, c] + _dot_tn(vb[:, c], kd[:, c].astype(BF16))
            return carry

        lax.fori_loop(0, HG_TILE_CHUNKS, chunk, 0)
        for h in range(HG_HEADS):
            c = slice(h * HG_DIM, (h + 1) * HG_DIM)
            ohat, _ = _rms(o_ref[:, c])
            out_ref[:, c] = (ohat * og_ref[...] * _silu(g_ref[:, c])).astype(BF16)

    tile = _rows(HG_TILE, HG_WIDTH)
    return pl.pallas_call(
        body, name="hg_fwd", grid=(n_tiles,),
        out_shape=[jax.ShapeDtypeStruct((s, HG_WIDTH), BF16), jax.ShapeDtypeStruct((s, HG_WIDTH), F32),
                   jax.ShapeDtypeStruct((s // HG_CHUNK * HG_DIM, HG_WIDTH), F32)],
        in_specs=[tile] * 4 + [_whole((2, HG_WIDTH)), _whole((1, HG_DIM))],
        out_specs=[tile, tile, _rows(HG_TILE_CHUNKS * HG_DIM, HG_WIDTH)],
        scratch_shapes=[pltpu.VMEM((HG_HEADS, HG_DIM, HG_DIM), F32)] + [pltpu.VMEM((HG_TILE, HG_WIDTH), F32)] * 3,
        compiler_params=_params("arbitrary"),
    )(hq, hf, hi, hgt, logits, onorm_g)


def _hg_bwd(hq, hf, hi, hgt, logits, onorm_g, o, states, dout):
    s = hq.shape[0]
    n_tiles = s // HG_TILE

    def body(q_ref, f_ref, i_ref, g_ref, lg_ref, og_ref, o_ref, st_ref, d_ref,
             dq_ref, df_ref, di_ref, dg_ref, dog_ref, dlb_ref, dstate, qf_s, kk_s, lf_s, do_s):
        @pl.when(pl.program_id(0) == 0)
        def _():
            dstate[...] = jnp.zeros_like(dstate)
            dog_ref[...] = jnp.zeros_like(dog_ref)
            dlb_ref[...] = jnp.zeros_like(dlb_ref)

        og = og_ref[...]
        dog = jnp.zeros((1, HG_DIM), F32)
        for h in range(HG_HEADS):
            c = slice(h * HG_DIM, (h + 1) * HG_DIM)
            ohat, rstd = _rms(o_ref[:, c])
            gate = g_ref[:, c]
            d = d_ref[:, c]
            dg_ref[:, c] = d * (ohat * og) * _dsilu(gate)
            dnormed = d * _silu(gate)
            dog += _rowsum(dnormed * ohat)
            do_s[:, c] = _rms_bwd(dnormed * og, ohat, rstd)
        dog_ref[...] += dog

        lb = _lower_bound(lg_ref)
        f = lb + (1.0 - lb) * _sigmoid(f_ref[...])
        kk_s[...] = 1.0 - f
        lf_s[...] = jnp.log(f)
        qf_s[...] = _silu(q_ref[...])
        causal, upper, tri, tri_t = _chunk_masks()

        def chunk(step, carry):
            ci = HG_TILE_CHUNKS - 1 - step
            rows = pl.ds(pl.multiple_of(ci * HG_CHUNK, HG_CHUNK), HG_CHUNK)
            srows = pl.ds(pl.multiple_of(ci * HG_DIM, HG_DIM), HG_DIM)
            lf = lf_s[rows, :]
            b = _dot_f32(tri, lf)
            bl = _rowsum(lf)
            ref = 0.5 * bl
            qf, kk, v, do = qf_s[rows, :], kk_s[rows, :], i_ref[rows, :], do_s[rows, :]
            eb, ebr, erb, ekd, ebl = jnp.exp(b), jnp.exp(b - ref), jnp.exp(ref - b), jnp.exp(bl - b), jnp.exp(bl)
            a_in, a_t, b_t, kd = qf * eb, qf * ebr, kk * erb, kk * ekd
            for h in range(HG_HEADS):
                c = slice(h * HG_DIM, (h + 1) * HG_DIM)
                st, dst = st_ref[srows, c], dstate[h]
                stb, dstb = st.astype(BF16), dst.astype(BF16)
                doh, vh = do[:, c], v[:, c]
                dob, vb = doh.astype(BF16), vh.astype(BF16)
                ain_h, at_h, bt_h, kd_h = a_in[:, c], a_t[:, c], b_t[:, c], kd[:, c]
                atb, btb = at_h.astype(BF16), bt_h.astype(BF16)
                d_ain = _dot(dob, stb)
                p_t = jnp.where(upper, _dot_nt(btb, atb), 0.0).astype(BF16)
                dp = jnp.where(causal, _dot_nt(dob, vb), 0.0).astype(BF16)
                dp_t = jnp.where(upper, _dot_nt(vb, dob), 0.0).astype(BF16)
                di_ref[rows, c] = _dot(p_t, dob) + _dot_nt(kd_h.astype(BF16), dstb)
                d_at = _dot(dp, btb)
                d_bt = _dot(dp_t, atb)
                d_kd = _dot(vb, dstb)
                dqf = d_ain * eb[:, c] + d_at * ebr[:, c]
                dkk = d_bt * erb[:, c] + d_kd * ekd[:, c]
                db = d_ain * ain_h + d_at * atb.astype(F32) - d_bt * btb.astype(F32) - d_kd * kd_h
                dbl = _rowsum(d_kd * kd_h) + _rowsum(dst * st) * ebl[:, c]
                dstate[h] = _dot_tn(dob, ain_h.astype(BF16)) + dst * ebl[:, c]
                dlf = _dot_f32(tri_t, db) + dbl
                qv, fr = q_ref[rows, c], f_ref[rows, c]
                lbh = lb[:, c]
                sg = _sigmoid(fr)
                dfv = dlf / (lbh + (1.0 - lbh) * sg) - dkk
                df_ref[rows, c] = dfv * (1.0 - lbh) * sg * (1.0 - sg)
                dlb_ref[:, c] += _rowsum(dfv * (1.0 - sg))
                dq_ref[rows, c] = dqf * _dsilu(qv)
            return carry

        lax.fori_loop(0, HG_TILE_CHUNKS, chunk, 0)

    rev = pl.BlockSpec((HG_TILE, HG_WIDTH), lambda i: (n_tiles - 1 - i, 0))
    return pl.pallas_call(
        body, name="hg_bwd", grid=(n_tiles,),
        out_shape=[jax.ShapeDtypeStruct((s, HG_WIDTH), F32)] * 4
        + [jax.ShapeDtypeStruct((1, HG_DIM), F32), jax.ShapeDtypeStruct((1, HG_WIDTH), F32)],
        in_specs=[rev] * 4 + [_whole((2, HG_WIDTH)), _whole((1, HG_DIM)), rev,
                              pl.BlockSpec((HG_TILE_CHUNKS * HG_DIM, HG_WIDTH), lambda i: (n_tiles - 1 - i, 0)), rev],
        out_specs=[rev] * 4 + [_whole((1, HG_DIM)), _whole((1, HG_WIDTH))],
        scratch_shapes=[pltpu.VMEM((HG_HEADS, HG_DIM, HG_DIM), F32)] + [pltpu.VMEM((HG_TILE, HG_WIDTH), F32)] * 4,
        compiler_params=_params("arbitrary"),
    )(hq, hf, hi, hgt, logits, onorm_g, o, states, dout)


def _att_consts():
    lane = lax.broadcasted_iota(jnp.int32, (ATT_BLOCK, 128), 1)
    qi = lax.broadcasted_iota(jnp.int32, (2 * ATT_BLOCK, ATT_BLOCK), 0) % ATT_BLOCK
    kj = lax.broadcasted_iota(jnp.int32, (2 * ATT_BLOCK, ATT_BLOCK), 1)
    return lane < ATT_HEAD_DIM, kj <= qi, lambda off: kj >= qi + off


def _stack_heads(x2, first):
    return jnp.concatenate([jnp.where(first, x2, 0.0), jnp.where(first, 0.0, x2)], axis=0)


def _stack_bcast(x2, first):
    other = pltpu.roll(x2, ATT_HEAD_DIM, axis=1)
    return jnp.concatenate([jnp.where(first, x2, other), jnp.where(first, other, x2)], axis=0)


def _unstack_heads(st, first):
    return jnp.where(first, st[:ATT_BLOCK], st[ATT_BLOCK:])


def _att_fwd(q, k, v, dil):
    m, width = q.shape
    nb = m // ATT_BLOCK

    def body(q_ref, kc_ref, kp_ref, vc_ref, vp_ref, o_ref, lse_ref):
        first, cur_ok, _band = _att_consts()
        prev_ok = _band(jnp.where(pl.program_id(1) > 0, 0, ATT_BLOCK))
        for j in range(ATT_WIDTH // 128):
            c = slice(j * 128, (j + 1) * 128)
            qst = _stack_heads(q_ref[:, c] * ATT_SCALE, first).astype(BF16)
            kc, kp = kc_ref[:, c].astype(BF16), kp_ref[:, c].astype(BF16)
            vc, vp = vc_ref[:, c].astype(BF16), vp_ref[:, c].astype(BF16)
            sc = jnp.where(cur_ok, _dot_nt(qst, kc), NEG)
            sp = jnp.where(prev_ok, _dot_nt(qst, kp), NEG)
            mx = jnp.maximum(jnp.max(sc, axis=-1, keepdims=True), jnp.max(sp, axis=-1, keepdims=True))
            pc, pp = jnp.exp(sc - mx), jnp.exp(sp - mx)
            den = jnp.sum(pc, axis=-1, keepdims=True) + jnp.sum(pp, axis=-1, keepdims=True)
            ost = (_dot(pc.astype(BF16), vc) + _dot(pp.astype(BF16), vp)) / den
            lse = jnp.broadcast_to(mx + jnp.log(den), (2 * ATT_BLOCK, 128))
            o_ref[:, c] = _unstack_heads(ost, first)
            lse_ref[:, c] = _unstack_heads(lse, first)

    cur = pl.BlockSpec((ATT_BLOCK, ATT_WIDTH), lambda r, n: (n, r))
    prev = pl.BlockSpec((ATT_BLOCK, ATT_WIDTH), lambda r, n: (jnp.maximum(n - 1, 0), r))
    return pl.pallas_call(
        body, name=f"att_fwd_d{dil}", grid=(dil, nb),
        out_shape=[jax.ShapeDtypeStruct((m, width), F32)] * 2,
        in_specs=[cur, cur, prev, cur, prev], out_specs=[cur, cur],
        compiler_params=_params("parallel", "arbitrary"),
    )(q, k, k, v, v)


def _att_bwd(q, k, v, do, cc, lse, dil):
    m, width = q.shape
    nb = m // ATT_BLOCK

    def body(q_ref, qx_ref, kc_ref, kp_ref, vc_ref, vp_ref, do_ref, dox_ref, cc_ref, ccx_ref, lse_ref, lsex_ref,
             dq_ref, dk_ref, dv_ref):
        first, cur_ok, _band = _att_consts()
        n = pl.program_id(1)
        prev_ok = _band(jnp.where(n > 0, 0, ATT_BLOCK))
        next_ok = _band(jnp.where(n < nb - 1, 0, ATT_BLOCK))
        for j in range(ATT_WIDTH // 128):
            c = slice(j * 128, (j + 1) * 128)
            qst = _stack_heads(q_ref[:, c] * ATT_SCALE, first).astype(BF16)
            qxst = _stack_heads(qx_ref[:, c] * ATT_SCALE, first).astype(BF16)
            dost = _stack_heads(do_ref[:, c], first).astype(BF16)
            doxst = _stack_heads(dox_ref[:, c], first).astype(BF16)
            lse_n, lse_x = _stack_bcast(lse_ref[:, c], first), _stack_bcast(lsex_ref[:, c], first)
            cc_n, cc_x = _stack_bcast(cc_ref[:, c], first), _stack_bcast(ccx_ref[:, c], first)
            kc, kp = kc_ref[:, c].astype(BF16), kp_ref[:, c].astype(BF16)
            vc, vp = vc_ref[:, c].astype(BF16), vp_ref[:, c].astype(BF16)
            p_cur = jnp.exp(jnp.where(cur_ok, _dot_nt(qst, kc), NEG) - lse_n)
            p_prev = jnp.exp(jnp.where(prev_ok, _dot_nt(qst, kp), NEG) - lse_n)
            p_next = jnp.exp(jnp.where(next_ok, _dot_nt(qxst, kc), NEG) - lse_x)
            ds_cur = (p_cur * (_dot_nt(dost, vc) + cc_n)).astype(BF16)
            ds_prev = (p_prev * (_dot_nt(dost, vp) + cc_n)).astype(BF16)
            ds_next = (p_next * (_dot_nt(doxst, vc) + cc_x)).astype(BF16)
            dq_ref[:, c] = _unstack_heads(_dot(ds_cur, kc) + _dot(ds_prev, kp), first) * ATT_SCALE
            dk_ref[:, c] = _dot_tn(ds_cur, qst) + _dot_tn(ds_next, qxst)
            dv_ref[:, c] = _dot_tn(p_cur.astype(BF16), dost) + _dot_tn(p_next.astype(BF16), doxst)

    cur = pl.BlockSpec((ATT_BLOCK, ATT_WIDTH), lambda r, n: (n, r))
    prev = pl.BlockSpec((ATT_BLOCK, ATT_WIDTH), lambda r, n: (jnp.maximum(n - 1, 0), r))
    nxt = pl.BlockSpec((ATT_BLOCK, ATT_WIDTH), lambda r, n: (jnp.minimum(n + 1, nb - 1), r))
    return pl.pallas_call(
        body, name=f"att_bwd_d{dil}", grid=(dil, nb),
        out_shape=[jax.ShapeDtypeStruct((m, width), F32)] * 3,
        in_specs=[cur, nxt, cur, prev, cur, prev, cur, nxt, cur, nxt, cur, nxt], out_specs=[cur, cur, cur],
        compiler_params=_params("parallel", "arbitrary"),
    )(q, q, k, k, v, v, do, do, cc, cc, lse, lse)


def _branch_weights(lses):
    mx = jnp.maximum(jnp.maximum(lses[0], lses[1]), lses[2])
    es = [jnp.exp(l - mx) for l in lses]
    inv = 1.0 / (es[0] + es[1] + es[2])
    return [e * inv for e in es]


def _att_combine(outs, lses, att_g):
    s = outs[0].shape[0]
    tm = 512

    def body(o0, o1, o2, l0, l1, l2, g_ref, att_ref, out_ref):
        ws = _branch_weights([l0[...], l1[...], l2[...]])
        att = ws[0] * o0[...] + ws[1] * o1[...] + ws[2] * o2[...]
        att_ref[...] = att
        ahat, _ = _rms(att)
        out_ref[...] = (ahat * g_ref[...]).astype(BF16)

    tile = _rows(tm, ATT_WIDTH)
    return pl.pallas_call(
        body, name="att_combine", grid=(s // tm,),
        out_shape=[jax.ShapeDtypeStruct((s, ATT_WIDTH), F32), jax.ShapeDtypeStruct((s, ATT_WIDTH), BF16)],
        in_specs=[tile] * 6 + [_whole((1, ATT_WIDTH))], out_specs=[tile, tile],
        compiler_params=_params("parallel"),
    )(*outs, *lses, att_g)


def _att_combine_bwd(datt_out, att, lses, att_g):
    s = att.shape[0]
    tm = 256

    def body(d_ref, att_ref, l0, l1, l2, g_ref, do0, do1, do2, cc0, cc1, cc2, dg_ref):
        @pl.when(pl.program_id(0) == 0)
        def _():
            dg_ref[...] = jnp.zeros_like(dg_ref)

        att = att_ref[...]
        ahat, rstd = _rms(att)
        d = d_ref[...]
        dg_ref[...] += _rowsum(d * ahat)
        datt = _rms_bwd(d * g_ref[...], ahat, rstd)
        hi = lax.broadcasted_iota(jnp.int32, (ATT_WIDTH, ATT_WIDTH), 0) // ATT_HEAD_DIM
        hj = lax.broadcasted_iota(jnp.int32, (ATT_WIDTH, ATT_WIDTH), 1) // ATT_HEAD_DIM
        head_sum = _dot_f32(datt * att, (hi == hj).astype(F32))
        ws = _branch_weights([l0[...], l1[...], l2[...]])
        for w, do_ref, cc_ref in zip(ws, (do0, do1, do2), (cc0, cc1, cc2)):
            do_ref[...] = w * datt
            cc_ref[...] = -w * head_sum

    tile = _rows(tm, ATT_WIDTH)
    return pl.pallas_call(
        body, name="att_combine_bwd", grid=(s // tm,),
        out_shape=[jax.ShapeDtypeStruct((s, ATT_WIDTH), F32)] * 6 + [jax.ShapeDtypeStruct((1, ATT_WIDTH), F32)],
        in_specs=[tile] * 5 + [_whole((1, ATT_WIDTH))], out_specs=[tile] * 6 + [_whole((1, ATT_WIDTH))],
        compiler_params=_params("arbitrary"),
    )(datt_out, att, *lses, att_g)


def _out_fwd(x, hg, at, mod, w_out):
    s = x.shape[0]
    tm = 512

    def body(x_ref, hg_ref, at_ref, mod_ref, w_ref, x1_ref):
        mix = _dot(hg_ref[...], w_ref[0:512, :]) + _dot(at_ref[...], w_ref[512:1024, :])
        x1_ref[...] = x_ref[...] + mod_ref[:, 2 * D_MODEL:3 * D_MODEL] * mix

    return pl.pallas_call(
        body, name="out_fwd", grid=(s // tm,), out_shape=jax.ShapeDtypeStruct((s, D_MODEL), F32),
        in_specs=[_rows(tm, D_MODEL), _rows(tm, 512), _rows(tm, 512), _whole((1, 6 * D_MODEL)), _whole((D_MODEL, D_MODEL))],
        out_specs=_rows(tm, D_MODEL), compiler_params=_params("parallel"),
    )(x, hg, at, mod, w_out)


def _out_bwd(dx1, hg, at, mod, w_out):
    s = dx1.shape[0]
    tm = 512

    def body(dx_ref, hg_ref, at_ref, mod_ref, w_ref, dhg_ref, dat_ref, dw_ref, dgate_ref):
        @pl.when(pl.program_id(0) == 0)
        def _():
            dw_ref[...] = jnp.zeros_like(dw_ref)
            dgate_ref[...] = jnp.zeros_like(dgate_ref)

        hg, at, dx = hg_ref[...], at_ref[...], dx_ref[...]
        mix = _dot(hg, w_ref[0:512, :]) + _dot(at, w_ref[512:1024, :])
        dgate_ref[...] += _rowsum(dx * mix)
        dmix = (mod_ref[:, 2 * D_MODEL:3 * D_MODEL] * dx).astype(BF16)
        dhg_ref[...] = _dot_nt(dmix, w_ref[0:512, :])
        dat_ref[...] = _dot_nt(dmix, w_ref[512:1024, :])
        dw_ref[0:512, :] += _dot_tn(hg, dmix)
        dw_ref[512:1024, :] += _dot_tn(at, dmix)

    return pl.pallas_call(
        body, name="out_bwd", grid=(s // tm,),
        out_shape=[jax.ShapeDtypeStruct((s, 512), F32)] * 2
        + [jax.ShapeDtypeStruct((D_MODEL, D_MODEL), F32), jax.ShapeDtypeStruct((1, D_MODEL), F32)],
        in_specs=[_rows(tm, D_MODEL), _rows(tm, 512), _rows(tm, 512), _whole((1, 6 * D_MODEL)), _whole((D_MODEL, D_MODEL))],
        out_specs=[_rows(tm, 512), _rows(tm, 512), _whole((D_MODEL, D_MODEL)), _whole((1, D_MODEL))],
        compiler_params=_params("arbitrary"),
    )(dx1, hg, at, mod, w_out)


FFN_CHUNK = 256


def _ffn(x1, target, mod, g2, gf, w_gu, w_down):
    s = x1.shape[0]
    tm = 256
    n_chunks = D_FF // FFN_CHUNK

    def body(x_ref, t_ref, mod_ref, g2_ref, gf_ref, wgu_hbm, wd_hbm,
             dx_ref, h2_ref, act_ref, dau_ref, dff_ref, sums_ref, loss_ref, wgu, wd, a_s, u_s, sem):
        @pl.when(pl.program_id(0) == 0)
        def _():
            c1 = pltpu.make_async_copy(wgu_hbm, wgu, sem.at[0])
            c2 = pltpu.make_async_copy(wd_hbm, wd, sem.at[1])
            c1.start()
            c2.start()
            c1.wait()
            c2.wait()
            sums_ref[...] = jnp.zeros_like(sums_ref)
            loss_ref[...] = jnp.zeros_like(loss_ref)

        x1v = x_ref[...]
        xhat, rstd = _rms(x1v)
        g2 = g2_ref[...]
        n2 = xhat * g2
        scale2 = 1.0 + mod_ref[:, 4 * D_MODEL:5 * D_MODEL]
        gate2 = mod_ref[:, 5 * D_MODEL:6 * D_MODEL]
        hb = (n2 * scale2 + mod_ref[:, 3 * D_MODEL:4 * D_MODEL]).astype(BF16)
        h2_ref[...] = hb
        ff = jnp.zeros((tm, D_MODEL), F32)
        for j in range(n_chunks):
            c = slice(j * FFN_CHUNK, (j + 1) * FFN_CHUNK)
            cu = slice(D_FF + j * FFN_CHUNK, D_FF + (j + 1) * FFN_CHUNK)
            a = _dot(hb, wgu[:, c])
            u = _dot(hb, wgu[:, cu])
            a_s[:, c] = a
            u_s[:, c] = u
            act = (_silu(a) * u).astype(BF16)
            act_ref[:, c] = act
            ff += _dot(act, wd[c, :])
        x2 = x1v + gate2 * ff
        nf, rstd_f = _rms(x2)
        gfv = gf_ref[...]
        err = nf * gfv - t_ref[...]
        loss_ref[...] += 0.5 * jnp.sum(_rowsum(err * err), axis=-1, keepdims=True) * (1.0 / D_MODEL)
        dy = err * (1.0 / D_MODEL)
        dx2 = _rms_bwd(dy * gfv, nf, rstd_f)
        dffb = (gate2 * dx2).astype(BF16)
        dff_ref[...] = dffb
        dh = jnp.zeros((tm, D_MODEL), F32)
        for j in range(n_chunks):
            c = slice(j * FFN_CHUNK, (j + 1) * FFN_CHUNK)
            cu = slice(D_FF + j * FFN_CHUNK, D_FF + (j + 1) * FFN_CHUNK)
            dact = _dot_nt(dffb, wd[c, :])
            a, u = a_s[:, c], u_s[:, c]
            da = (dact * u * _dsilu(a)).astype(BF16)
            du = (dact * _silu(a)).astype(BF16)
            dau_ref[:, c] = da
            dau_ref[:, cu] = du
            dh += _dot_nt(da, wgu[:, c]) + _dot_nt(du, wgu[:, cu])
        dn = dh * scale2
        sums_ref[0:1, :] += _rowsum(dh)
        sums_ref[1:2, :] += _rowsum(dh * n2)
        sums_ref[2:3, :] += _rowsum(dx2 * ff)
        sums_ref[3:4, :] += _rowsum(dn * xhat)
        sums_ref[4:5, :] += _rowsum(dy * nf)
        dx_ref[...] = dx2 + _rms_bwd(dn * g2, xhat, rstd)

    vec = _whole((1, D_MODEL))
    hbm = pl.BlockSpec(memory_space=pl.ANY)
    return pl.pallas_call(
        body, name="ffn", grid=(s // tm,),
        out_shape=[jax.ShapeDtypeStruct((s, D_MODEL), F32), jax.ShapeDtypeStruct((s, D_MODEL), BF16),
                   jax.ShapeDtypeStruct((s, D_FF), BF16), jax.ShapeDtypeStruct((s, 2 * D_FF), BF16),
                   jax.ShapeDtypeStruct((s, D_MODEL), BF16), jax.ShapeDtypeStruct((8, D_MODEL), F32),
                   jax.ShapeDtypeStruct((1, 128), F32)],
        in_specs=[_rows(tm, D_MODEL), _rows(tm, D_MODEL), _whole((1, 6 * D_MODEL)), vec, vec, hbm, hbm],
        out_specs=[_rows(tm, D_MODEL), _rows(tm, D_MODEL), _rows(tm, D_FF), _rows(tm, 2 * D_FF), _rows(tm, D_MODEL),
                   _whole((8, D_MODEL)), _whole((1, 128))],
        scratch_shapes=[pltpu.VMEM((D_MODEL, 2 * D_FF), BF16), pltpu.VMEM((D_FF, D_MODEL), BF16),
                        pltpu.VMEM((tm, D_FF), F32), pltpu.VMEM((tm, D_FF), F32), pltpu.SemaphoreType.DMA((2,))],
        compiler_params=_params("arbitrary"),
    )(x1, target, mod, g2, gf, w_gu, w_down)


def _weight_grad(a, b, name):
    s, m = a.shape
    n = b.shape[1]
    ts, tn = 512, 512

    def body(a_ref, b_ref, o_ref):
        @pl.when(pl.program_id(1) == 0)
        def _():
            o_ref[...] = jnp.zeros_like(o_ref)

        o_ref[...] += _dot_tn(a_ref[...], b_ref[...])

    return pl.pallas_call(
        body, name=name, grid=(n // tn, s // ts), out_shape=jax.ShapeDtypeStruct((m, n), F32),
        in_specs=[pl.BlockSpec((ts, m), lambda j, i: (i, 0)), pl.BlockSpec((ts, tn), lambda j, i: (i, j))],
        out_specs=pl.BlockSpec((m, tn), lambda j, i: (0, j)),
        compiler_params=_params("parallel", "arbitrary"),
    )(a, b)


def _adamw_math(w, g, m, v):
    m = ADAM_B1 * m + (1.0 - ADAM_B1) * g
    v = ADAM_B2 * v + (1.0 - ADAM_B2) * (g * g)
    m_hat = m / (1.0 - ADAM_B1 ** ADAM_STEP)
    v_hat = v / (1.0 - ADAM_B2 ** ADAM_STEP)
    delta = -ADAM_LR * (m_hat / (jnp.sqrt(v_hat) + ADAM_EPS) + ADAM_WD * w)
    return delta, m, v


def _adamw_shard(chip, w, m, v, partial, got, name):
    r, c = w.shape
    tr = _shard_rows(r)

    def body(chip_ref, w_ref, m_ref, v_ref, own_ref, g0, g1, g2, grad_ref, d_ref, nm_ref, nv_ref):
        g = ((own_ref[...] + g0[...].astype(F32)) + g1[...].astype(F32)) + g2[...].astype(F32)
        grad_ref[...] = g
        d_ref[...], nm_ref[...], nv_ref[...] = _adamw_math(w_ref[...], g, m_ref[...], v_ref[...])

    tile = pl.BlockSpec((tr, c), lambda i, chip_ref: (i, 0))
    own = pl.BlockSpec((None, tr, c), lambda i, chip_ref: (chip_ref[0], i, 0))
    part = [pl.BlockSpec((None, tr, c), functools.partial(lambda j, i, chip_ref: (j, i, 0), j)) for j in range(3)]
    return pl.pallas_call(
        body, name=name,
        grid_spec=pltpu.PrefetchScalarGridSpec(num_scalar_prefetch=1, grid=(r // tr,), in_specs=[tile] * 3 + [own] + part,
                                               out_specs=[tile] * 4),
        out_shape=[jax.ShapeDtypeStruct((r, c), F32)] * 4, compiler_params=_params("parallel"),
    )(chip, w, m, v, partial, got, got, got)


def _small_update(small_all, dmod_blocks, c_all, logits, w_ada, m_ada, v_ada, smalls):
    def body(sm_ref, dm_ref, c_ref, lg_ref, wa_ref, ma_ref, va_ref, *rest):
        ins, outs = rest[:21], rest[21:]
        _, me = _flip(0)
        tot = sm_ref[0:1, :]
        for i in range(1, N_DEV):
            tot = tot + sm_ref[i:i + 1, :]
        loss_ref = outs[0]
        loss_ref[...] = tot[:, SM_LOSS:SM_LOSS + 128]
        g_ada = lax.dot_general(_silu(c_ref[...]), dm_ref[me], (((0,), (0,)), ((), ())),
                                preferred_element_type=F32, precision=HIGHEST)
        outs[1][...] = g_ada
        outs[2][...], outs[3][...], outs[4][...] = _adamw_math(wa_ref[...], g_ada, ma_ref[...], va_ref[...])
        p0 = _lower_bound(lg_ref)
        dl0 = tot[:, SM_LB:SM_LB + 512] * p0 * (1.0 - p0)
        grads = [tot[:, SM_MOD:SM_MOD + 6 * D_MODEL], tot[:, SM_G1:SM_G1 + D_MODEL], tot[:, SM_G2:SM_G2 + D_MODEL],
                 tot[:, SM_GF:SM_GF + D_MODEL], tot[:, SM_ATT:SM_ATT + 512], tot[:, SM_HG:SM_HG + 128],
                 jnp.where(lax.broadcasted_iota(jnp.int32, (2, 512), 0) == 0, dl0, -dl0)]
        for i, g in enumerate(grads):
            w_ref, m_ref, v_ref = ins[3 * i:3 * i + 3]
            o = outs[5 + 4 * i:9 + 4 * i]
            o[0][...] = g
            o[1][...], o[2][...], o[3][...] = _adamw_math(w_ref[...], g, m_ref[...], v_ref[...])

    flat = [t for trio in smalls for t in trio]
    vm = pl.BlockSpec(memory_space=pltpu.VMEM)
    out_shape = [jax.ShapeDtypeStruct((1, 128), F32)] + [jax.ShapeDtypeStruct(w_ada.shape, F32)] * 4
    for trio in smalls:
        out_shape += [jax.ShapeDtypeStruct(trio[0].shape, F32)] * 4
    return pl.pallas_call(
        body, name="small_update", out_shape=out_shape,
        in_specs=[vm] * (7 + len(flat)), out_specs=[vm] * len(out_shape),
        compiler_params=pltpu.CompilerParams(vmem_limit_bytes=V7X_VMEM_LIMIT),
    )(small_all, dmod_blocks, c_all, logits, w_ada, m_ada, v_ada, *flat)


def _dilate(t, d):
    return t if d == 1 else t.reshape(t.shape[0] // d, d * t.shape[1])


def _undilate(t, d):
    return t if d == 1 else t.reshape(t.shape[0] * d, t.shape[1] // d)


def kernel(x, c, w_ada, b_ada, norm1_g, w_in, hg_lb_logits, hg_onorm_g, att_onorm_g, w_out, norm2_g, w_gate_up, w_down, final_g, loss_target, m_w_ada, m_b_ada, m_norm1_g, m_w_in, m_hg_lb_logits, m_hg_onorm_g, m_att_onorm_g, m_w_out, m_norm2_g, m_w_gate_up, m_w_down, m_final_g, v_w_ada, v_b_ada, v_norm1_g, v_w_in, v_hg_lb_logits, v_hg_onorm_g, v_att_onorm_g, v_w_out, v_norm2_g, v_w_gate_up, v_w_down, v_final_g):
    x2d, target = x[0], loss_target[0]
    seq = x2d.shape[0]
    assert seq % (ATT_BLOCK * max(DILATIONS)) == 0 and seq % HG_TILE == 0
    gf = final_g.reshape(1, D_MODEL)

    c_all = _exchange_small(jnp.broadcast_to(c, (8, D_MODEL)), None, "gather_c")[:, 0, :]
    ada = _ada_rows(c_all, w_ada[0], b_ada)
    mod = _exchange_small(ada, 1, "scatter_mod").reshape(1, 6 * D_MODEL)

    shards = [w_in[0].astype(BF16), w_out[0].astype(BF16), w_gate_up[0].astype(BF16), w_down[0].astype(BF16)]
    g_in, g_out, g_gu, g_down = _gather_weights(shards)
    w_in_b = jnp.transpose(g_in, (1, 0, 2)).reshape(D_MODEL, IN_WIDTH)
    w_out_b = g_out.reshape(D_MODEL, D_MODEL)
    w_gu_b = jnp.transpose(g_gu, (1, 0, 2)).reshape(D_MODEL, 2 * D_FF)
    w_down_b = g_down.reshape(D_FF, D_MODEL)

    grad_x, dw_in, dw_out, dw_gu, dw_down, small = _block_step(
        x2d, target, mod, norm1_g, hg_lb_logits, hg_onorm_g, att_onorm_g, norm2_g, gf, w_in_b, w_out_b, w_gu_b, w_down_b)

    grads8 = [dw_in.reshape(D_MODEL, 4, 2, IN_WIDTH // N_DEV).transpose(1, 2, 0, 3),
              dw_out.reshape(4, 2, D_MODEL // N_DEV, D_MODEL),
              dw_gu.reshape(D_MODEL, 4, 2, 2 * D_FF // N_DEV).transpose(1, 2, 0, 3),
              dw_down.reshape(4, 2, D_FF // N_DEV, D_MODEL)]
    core = lax.axis_index("c").astype(jnp.int32).reshape(1)
    chip = (2 * lax.axis_index("x") + lax.axis_index("y")).astype(jnp.int32).reshape(1)
    got = _reduce_pairs(grads8)
    sums = [_pair_sum(core, g, b, f"pair_sum_{i}") for i, (g, b) in enumerate(zip(grads8, got))]
    recv = _reduce_chips([s16 for _, s16 in sums])
    big = []
    for i, (w, m, v) in enumerate([(w_in, m_w_in, v_w_in), (w_out, m_w_out, v_w_out),
                                   (w_gate_up, m_w_gate_up, v_w_gate_up), (w_down, m_w_down, v_w_down)]):
        big.append([t[None] for t in _adamw_shard(chip, w[0], m[0], v[0], sums[i][0], recv[i], f"adamw_{i}")])

    small_all = _exchange_small(jnp.broadcast_to(small, (8, SM_WIDTH)), None, "gather_small")[:, 0, :]
    smalls = [(b_ada, m_b_ada, v_b_ada), (norm1_g, m_norm1_g, v_norm1_g), (norm2_g, m_norm2_g, v_norm2_g),
              (gf, m_final_g.reshape(1, D_MODEL), v_final_g.reshape(1, D_MODEL)),
              (att_onorm_g, m_att_onorm_g, v_att_onorm_g), (hg_onorm_g, m_hg_onorm_g, v_hg_onorm_g),
              (hg_lb_logits, m_hg_lb_logits, v_hg_lb_logits)]
    dmod_blocks = small_all[:, :6 * D_MODEL].reshape(N_DEV, N_DEV, 6 * D_MODEL // N_DEV).transpose(1, 0, 2)
    res = _small_update(small_all, dmod_blocks, c_all, hg_lb_logits, w_ada[0], m_w_ada[0], v_w_ada[0], smalls)
    loss = res[0][0, 0]
    ada4 = [t[None] for t in res[1:5]]
    sm4 = {n: list(res[5 + 4 * i:9 + 4 * i]) for i, n in enumerate(["b_ada", "norm1_g", "norm2_g", "final_g", "att", "hg", "lb"])}
    sm4["final_g"] = [t.reshape(D_MODEL) for t in sm4["final_g"]]

    order = [ada4, sm4["b_ada"], sm4["norm1_g"], big[0], sm4["lb"], sm4["hg"], sm4["att"], big[1], sm4["norm2_g"], big[2], big[3],
             sm4["final_g"]]
    return (loss, grad_x[None], *[o[0] for o in order], *[o[1] for o in order], *[o[2] for o in order], *[o[3] for o in order])


def _block_step(x2d, target, mod, norm1_g, hg_lb_logits, hg_onorm_g, att_onorm_g, norm2_g, gf, w_in_b, w_out_b, w_gu_b, w_down_b):
    h1, hq, hf, hi, hgt, aq, ak, av = _in_fwd(x2d, mod, norm1_g, w_in_b)
    hg_out, hg_o, hg_states = _hg_fwd(hq, hf, hi, hgt, hg_lb_logits, hg_onorm_g)
    views = [(_dilate(aq, d), _dilate(ak, d), _dilate(av, d)) for d in DILATIONS]
    branch = [_att_fwd(*views[i], d) for i, d in enumerate(DILATIONS)]
    outs = [_undilate(branch[i][0], d) for i, d in enumerate(DILATIONS)]
    lses = [_undilate(branch[i][1], d) for i, d in enumerate(DILATIONS)]
    att, att_out = _att_combine(outs, lses, att_onorm_g)
    x1 = _out_fwd(x2d, hg_out, att_out, mod, w_out_b)

    dx1, h2, act, dau, dff, ffn_sums, loss_part = _ffn(x1, target, mod, norm2_g, gf, w_gu_b, w_down_b)
    dw_gu = _weight_grad(h2, dau, "dw_gate_up")
    dw_down = _weight_grad(act, dff, "dw_down")

    dhg, dat, dw_out, dgate1 = _out_bwd(dx1, hg_out, att_out, mod, w_out_b)
    comb = _att_combine_bwd(dat, att, lses, att_onorm_g)
    dos, ccs, d_att_g = comb[0:3], comb[3:6], comb[6]
    datt = []
    for i, d in enumerate(DILATIONS):
        grads = _att_bwd(*views[i], _dilate(dos[i], d), _dilate(ccs[i], d), branch[i][1], d)
        datt.append([_undilate(g, d) for g in grads])
    dhq, dhf, dhi, dhgt, d_hg_g, d_lb = _hg_bwd(hq, hf, hi, hgt, hg_lb_logits, hg_onorm_g, hg_o, hg_states, dhg)
    dps = [dhq, dhf, dhi, dhgt] + [datt[i][j] for j in range(3) for i in range(3)]
    grad_x, dp_b, dshift1, dscale1, d_g1 = _in_bwd(x2d, dx1, mod, norm1_g, w_in_b, dps)
    dw_in = _weight_grad(h1, dp_b, "dw_in")
    small = jnp.concatenate([dshift1, dscale1, dgate1, ffn_sums[0:1], ffn_sums[1:2], ffn_sums[2:3], d_g1, ffn_sums[3:4],
                             ffn_sums[4:5], d_att_g, d_lb, d_hg_g, loss_part], axis=1)
    return grad_x, dw_in, dw_out, dw_gu, dw_down, small
```

```python
import functools

import jax
import jax.numpy as jnp
from jax import lax
from jax.experimental import pallas as pl
from jax.experimental.pallas import tpu as pltpu

F32 = jnp.float32
BF16 = jnp.bfloat16
HIGHEST = lax.Precision.HIGHEST
MESH = pl.DeviceIdType.MESH

D_MODEL = 1024
N_DEV = 8
HG_HEADS = 4
HG_DIM = 128
HG_WIDTH = HG_HEADS * HG_DIM
HG_CHUNK = 64
ATT_WIDTH = 512
ATT_HEAD_DIM = 64
ATT_BLOCK = 128
DILATIONS = (1, 4, 16)
ATT_SCALE = ATT_HEAD_DIM ** -0.5
D_FF = 2816
IN_WIDTH = 7 * 512
RMS_EPS = 1e-6
NEG = -1e30

ADAM_LR = 0.001
ADAM_B1 = 0.9
ADAM_B2 = 0.999
ADAM_EPS = 1e-08
ADAM_WD = 0.01
ADAM_STEP = 10

V7X_VMEM_LIMIT = 56 * 1024 * 1024

SM_MOD = 0
SM_G1 = 6 * D_MODEL
SM_G2 = 7 * D_MODEL
SM_GF = 8 * D_MODEL
SM_ATT = 9 * D_MODEL
SM_LB = 9 * D_MODEL + 512
SM_HG = 10 * D_MODEL
SM_LOSS = 10 * D_MODEL + 128
SM_WIDTH = 10 * D_MODEL + 256


def _params(*sem, vmem=V7X_VMEM_LIMIT):
    return pltpu.CompilerParams(dimension_semantics=sem, vmem_limit_bytes=vmem)


def _dot(a, b):
    return jnp.dot(a, b, preferred_element_type=F32)


def _dot_nt(a, b):
    return lax.dot_general(a, b, (((1,), (1,)), ((), ())), preferred_element_type=F32)


def _dot_tn(a, b):
    return lax.dot_general(a, b, (((0,), (0,)), ((), ())), preferred_element_type=F32)


def _dot_f32(a, b):
    return jnp.dot(a, b, preferred_element_type=F32, precision=HIGHEST)


def _sigmoid(x):
    return 1.0 / (1.0 + jnp.exp(-x))


def _silu(x):
    return x * _sigmoid(x)


def _dsilu(x):
    s = _sigmoid(x)
    return s * (1.0 + x * (1.0 - s))


def _rms(x):
    rstd = lax.rsqrt(jnp.mean(x * x, axis=-1, keepdims=True) + RMS_EPS)
    return x * rstd, rstd


def _rms_bwd(dn, xhat, rstd):
    return rstd * (dn - xhat * jnp.mean(dn * xhat, axis=-1, keepdims=True))


def _rowsum(x):
    return jnp.sum(x, axis=0, keepdims=True)


def _rows(tm, n):
    return pl.BlockSpec((tm, n), lambda i: (i, 0))


def _whole(shape):
    return pl.BlockSpec(shape, lambda i: (0,) * len(shape))


def _mesh_pos():
    return lax.axis_index("x"), lax.axis_index("y"), lax.axis_index("c")


def _flip(k):
    x, y, c = _mesh_pos()
    px = 1 - x if k & 4 else x
    py = 1 - y if k & 2 else y
    pc = 1 - c if k & 1 else c
    return (px, py, pc), 4 * px + 2 * py + pc


def _exchange_small(x, rows_per_peer, name):
    r_all, cols = x.shape
    r_out = r_all if rows_per_peer is None else rows_per_peer

    def body(x_ref, out_ref, send_sems, recv_sems):
        _, me = _flip(0)

        def src(pid):
            if rows_per_peer is None:
                return x_ref
            return x_ref.at[pl.ds(pl.multiple_of(pid * r_out, r_out), r_out), :]

        if rows_per_peer is None:
            out_ref[me] = x_ref[...]
        else:
            out_ref[me] = x_ref[pl.ds(pl.multiple_of(me * r_out, r_out), r_out), :]
        sends = []
        for k in range(1, N_DEV):
            dev, pid = _flip(k)
            cp = pltpu.make_async_remote_copy(src_ref=src(pid), dst_ref=out_ref.at[me], send_sem=send_sems.at[k - 1],
                                              recv_sem=recv_sems.at[k - 1], device_id=dev, device_id_type=MESH)
            cp.start()
            sends.append(cp)
        for k in range(1, N_DEV):
            dev, pid = _flip(k)
            pltpu.make_async_remote_copy(src_ref=src(pid), dst_ref=out_ref.at[pid], send_sem=send_sems.at[k - 1],
                                         recv_sem=recv_sems.at[k - 1], device_id=dev, device_id_type=MESH).wait_recv()
        for cp in sends:
            cp.wait_send()

    return pl.pallas_call(
        body, name=name,
        out_shape=jax.ShapeDtypeStruct((N_DEV, r_out, cols), x.dtype),
        in_specs=[pl.BlockSpec(memory_space=pltpu.VMEM)],
        out_specs=pl.BlockSpec(memory_space=pltpu.VMEM),
        scratch_shapes=[pltpu.SemaphoreType.DMA((N_DEV - 1,)), pltpu.SemaphoreType.DMA((N_DEV - 1,))],
    )(x)


def _gather_weights(shards):
    n = len(shards)

    def body(*refs):
        xs, outs = refs[:n], refs[n:2 * n]
        send_sems, recv_sems, local_sems = refs[2 * n:]
        x, y, c = _mesh_pos()
        me, sibling = (x, y, c), (x, y, 1 - c)
        chips = [(1 - x, y), (x, 1 - y), (1 - x, 1 - y)]

        def blk(a, px, py, pc):
            return outs[a].at[4 * px + 2 * py + pc]

        def copy(a, k, block, to, src=None):
            return pltpu.make_async_remote_copy(
                src_ref=blk(a, *block) if src is None else src, dst_ref=blk(a, *block),
                send_sem=send_sems.at[a * 7 + k], recv_sem=recv_sems.at[a * 7 + k], device_id=to, device_id_type=MESH)

        mine = [pltpu.make_async_copy(xs[a], blk(a, *me), local_sems.at[a]) for a in range(n)]
        for cp in mine:
            cp.start()
        first = []
        for a in range(n):
            first.append(copy(a, 0, me, sibling, src=xs[a]))
            first += [copy(a, 1 + j, me, (*chip, c), src=xs[a]) for j, chip in enumerate(chips)]
        for cp in first:
            cp.start()
        passed = []
        for j, chip in enumerate(chips):
            for a in range(n):
                copy(a, 1 + j, (*chip, c), me).wait_recv()
                cp = copy(a, 4 + j, (*chip, c), sibling)
                cp.start()
                passed.append(cp)
        for a in range(n):
            copy(a, 0, sibling, me).wait_recv()
            for j, chip in enumerate(chips):
                copy(a, 4 + j, (*chip, 1 - c), me).wait_recv()
        for cp in first + passed:
            cp.wait_send()
        for cp in mine:
            cp.wait()

    hbm = pl.BlockSpec(memory_space=pl.ANY)
    return pl.pallas_call(
        body, name="gather_weights",
        out_shape=[jax.ShapeDtypeStruct((N_DEV,) + s.shape, s.dtype) for s in shards],
        in_specs=[hbm] * n, out_specs=[hbm] * n,
        scratch_shapes=[pltpu.SemaphoreType.DMA((7 * n,)), pltpu.SemaphoreType.DMA((7 * n,)), pltpu.SemaphoreType.DMA((n,))],
    )(*shards)


def _reduce_pairs(grads):
    n = len(grads)

    def body(*refs):
        gs, got = refs[:n], refs[n:2 * n]
        send_sems, recv_sems = refs[2 * n:]
        x, y, c = _mesh_pos()
        sends = []
        for a in range(n):
            for chip in range(4):
                i = a * 4 + chip
                cp = pltpu.make_async_remote_copy(src_ref=gs[a].at[chip, 1 - c], dst_ref=got[a].at[chip],
                                                  send_sem=send_sems.at[i], recv_sem=recv_sems.at[i],
                                                  device_id=(x, y, 1 - c), device_id_type=MESH)
                cp.start()
                sends.append(cp)
        for cp in sends:
            cp.wait_recv()
        for cp in sends:
            cp.wait_send()

    hbm = pl.BlockSpec(memory_space=pl.ANY)
    return pl.pallas_call(
        body, name="reduce_pairs", out_shape=[jax.ShapeDtypeStruct((4,) + g.shape[2:], g.dtype) for g in grads],
        in_specs=[hbm] * n, out_specs=[hbm] * n,
        scratch_shapes=[pltpu.SemaphoreType.DMA((4 * n,)), pltpu.SemaphoreType.DMA((4 * n,))],
    )(*grads)


def _reduce_chips(partials):
    n = len(partials)
    flips = (4, 2, 6)

    def body(*refs):
        ps, got = refs[:n], refs[n:2 * n]
        send_sems, recv_sems = refs[2 * n:]
        sends = []
        for a in range(n):
            for j, k in enumerate(flips):
                dev, _ = _flip(k)
                cp = pltpu.make_async_remote_copy(src_ref=ps[a].at[2 * dev[0] + dev[1]], dst_ref=got[a].at[j],
                                                  send_sem=send_sems.at[a * 3 + j], recv_sem=recv_sems.at[a * 3 + j],
                                                  device_id=dev, device_id_type=MESH)
                cp.start()
                sends.append(cp)
        for cp in sends:
            cp.wait_recv()
        for cp in sends:
            cp.wait_send()

    hbm = pl.BlockSpec(memory_space=pl.ANY)
    return pl.pallas_call(
        body, name="reduce_chips", out_shape=[jax.ShapeDtypeStruct((3,) + p.shape[1:], p.dtype) for p in partials],
        in_specs=[hbm] * n, out_specs=[hbm] * n,
        scratch_shapes=[pltpu.SemaphoreType.DMA((3 * n,)), pltpu.SemaphoreType.DMA((3 * n,))],
    )(*partials)


def _shard_rows(r):
    return r // 2 if r % 32 == 0 else r


def _pair_sum(core, grads, got, name):
    _, _, r, c = grads.shape
    tr = _shard_rows(r)

    def body(core_ref, a_ref, b_ref, o_ref, ob_ref):
        s = a_ref[...] + b_ref[...]
        o_ref[...] = s
        ob_ref[...] = s.astype(BF16)

    spec = pl.BlockSpec((None, tr, c), lambda i, j, core_ref: (i, j, 0))
    return pl.pallas_call(
        body, name=name,
        grid_spec=pltpu.PrefetchScalarGridSpec(
            num_scalar_prefetch=1, grid=(4, r // tr),
            in_specs=[pl.BlockSpec((None, None, tr, c), lambda i, j, core_ref: (i, core_ref[0], j, 0)), spec],
            out_specs=[spec, spec]),
        out_shape=[jax.ShapeDtypeStruct((4, r, c), F32), jax.ShapeDtypeStruct((4, r, c), BF16)],
        compiler_params=_params("parallel", "parallel"),
    )(core, grads, got)


def _ada_rows(c_all, w_ada, b_ada):
    n_cols = w_ada.shape[1]

    def body(c_ref, w_ref, b_ref, o_ref):
        _, me = _flip(0)
        bias = b_ref[:, pl.ds(pl.multiple_of(me * n_cols, 128), n_cols)]
        o_ref[...] = _dot_f32(_silu(c_ref[...]), w_ref[...]) + bias

    return pl.pallas_call(
        body, name="ada_rows", out_shape=jax.ShapeDtypeStruct((N_DEV, n_cols), F32),
        in_specs=[pl.BlockSpec(memory_space=pltpu.VMEM)] * 3, out_specs=pl.BlockSpec(memory_space=pltpu.VMEM),
    )(c_all, w_ada, b_ada)


def _in_fwd(x, mod, g1, w_in):
    s = x.shape[0]
    tm = 256

    def body(x_ref, mod_ref, g_ref, w_ref, h_ref, *outs):
        xhat, _ = _rms(x_ref[...])
        h = (xhat * g_ref[...]) * (1.0 + mod_ref[:, D_MODEL:2 * D_MODEL]) + mod_ref[:, 0:D_MODEL]
        hb = h.astype(BF16)
        h_ref[...] = hb
        for j, o_ref in enumerate(outs):
            o_ref[...] = _dot(hb, w_ref[:, j * 512:(j + 1) * 512]).astype(o_ref.dtype)

    return pl.pallas_call(
        body, name="in_fwd", grid=(s // tm,),
        out_shape=[jax.ShapeDtypeStruct((s, D_MODEL), BF16)] + [jax.ShapeDtypeStruct((s, 512), F32)] * 4
        + [jax.ShapeDtypeStruct((s, 512), BF16)] * 3,
        in_specs=[_rows(tm, D_MODEL), _whole((1, 6 * D_MODEL)), _whole((1, D_MODEL)), _whole((D_MODEL, IN_WIDTH))],
        out_specs=[_rows(tm, D_MODEL)] + [_rows(tm, 512)] * 7,
        compiler_params=_params("parallel"),
    )(x, mod, g1, w_in)


def _in_bwd(x, dx1, mod, g1, w_in, dps):
    s = x.shape[0]
    tm = 256

    def body(x_ref, dx_ref, mod_ref, g_ref, w_ref, *rest):
        dp_refs, (gx_ref, dpb_ref, dsh_ref, dsc_ref, dg_ref) = rest[:13], rest[13:]
        pieces = [dp_refs[j][...] for j in range(4)]
        pieces += [dp_refs[4 + 3 * j][...] + dp_refs[5 + 3 * j][...] + dp_refs[6 + 3 * j][...] for j in range(3)]
        dh = jnp.zeros((tm, D_MODEL), F32)
        for j, p in enumerate(pieces):
            pb = p.astype(BF16)
            dpb_ref[:, j * 512:(j + 1) * 512] = pb
            dh += _dot_nt(pb, w_ref[:, j * 512:(j + 1) * 512])
        xhat, rstd = _rms(x_ref[...])
        g = g_ref[...]
        scale1 = 1.0 + mod_ref[:, D_MODEL:2 * D_MODEL]
        n1 = xhat * g

        @pl.when(pl.program_id(0) == 0)
        def _():
            dsh_ref[...] = jnp.zeros_like(dsh_ref)
            dsc_ref[...] = jnp.zeros_like(dsc_ref)
            dg_ref[...] = jnp.zeros_like(dg_ref)

        dsh_ref[...] += _rowsum(dh)
        dsc_ref[...] += _rowsum(dh * n1)
        dn = dh * scale1
        dg_ref[...] += _rowsum(dn * xhat)
        gx_ref[...] = dx_ref[...] + _rms_bwd(dn * g, xhat, rstd)

    vec = _whole((1, D_MODEL))
    return pl.pallas_call(
        body, name="in_bwd", grid=(s // tm,),
        out_shape=[jax.ShapeDtypeStruct((s, D_MODEL), F32), jax.ShapeDtypeStruct((s, IN_WIDTH), BF16)]
        + [jax.ShapeDtypeStruct((1, D_MODEL), F32)] * 3,
        in_specs=[_rows(tm, D_MODEL), _rows(tm, D_MODEL), _whole((1, 6 * D_MODEL)), vec, _whole((D_MODEL, IN_WIDTH))]
        + [_rows(tm, 512)] * 13,
        out_specs=[_rows(tm, D_MODEL), _rows(tm, IN_WIDTH), vec, vec, vec],
        compiler_params=_params("arbitrary"),
    )(x, dx1, mod, g1, w_in, *dps)


HG_TILE = 512
HG_TILE_CHUNKS = HG_TILE // HG_CHUNK


def _lower_bound(lg_ref):
    return 1.0 / (1.0 + jnp.exp(lg_ref[1:2, :] - lg_ref[0:1, :]))


def _chunk_masks():
    r = lax.broadcasted_iota(jnp.int32, (HG_CHUNK, HG_CHUNK), 0)
    c = lax.broadcasted_iota(jnp.int32, (HG_CHUNK, HG_CHUNK), 1)
    return r >= c, c >= r, (r >= c).astype(F32), (c >= r).astype(F32)


def _hg_fwd(hq, hf, hi, hgt, logits, onorm_g):
    s = hq.shape[0]
    n_tiles = s // HG_TILE

    def body(q_ref, f_ref, i_ref, g_ref, lg_ref, og_ref, out_ref, o_ref, st_ref, state, qf_s, kk_s, lf_s):
        @pl.when(pl.program_id(0) == 0)
        def _():
            state[...] = jnp.zeros_like(state)

        lb = _lower_bound(lg_ref)
        f = lb + (1.0 - lb) * _sigmoid(f_ref[...])
        kk_s[...] = 1.0 - f
        lf_s[...] = jnp.log(f)
        qf_s[...] = _silu(q_ref[...])
        causal, _, tri, _ = _chunk_masks()

        def chunk(ci, carry):
            rows = pl.ds(pl.multiple_of(ci * HG_CHUNK, HG_CHUNK), HG_CHUNK)
            srows = pl.ds(pl.multiple_of(ci * HG_DIM, HG_DIM), HG_DIM)
            lf = lf_s[rows, :]
            b = _dot_f32(tri, lf)
            bl = _rowsum(lf)
            ref = 0.5 * bl
            qf, kk, v = qf_s[rows, :], kk_s[rows, :], i_ref[rows, :]
            a_in = (qf * jnp.exp(b)).astype(BF16)
            a_t = (qf * jnp.exp(b - ref)).astype(BF16)
            b_t = (kk * jnp.exp(ref - b)).astype(BF16)
            kd = kk * jnp.exp(bl - b)
            ebl = jnp.exp(bl)
            vb = v.astype(BF16)
            for h in range(HG_HEADS):
                c = slice(h * HG_DIM, (h + 1) * HG_DIM)
                st = state[h]
                st_ref[srows, c] = st
                p = jnp.where(causal, _dot_nt(a_t[:, c], b_t[:, c]), 0.0)
                o_ref[rows, c] = _dot(p.astype(BF16), vb[:, c]) + _dot_nt(a_in[:, c], st.astype(BF16))
                state[h] = st * ebl[:, c] + _dot_tn(vb[:, c], kd[:, c].astype(BF16))
            return carry

        lax.fori_loop(0, HG_TILE_CHUNKS, chunk, 0)
        for h in range(HG_HEADS):
            c = slice(h * HG_DIM, (h + 1) * HG_DIM)
            ohat, _ = _rms(o_ref[:, c])
            out_ref[:, c] = (ohat * og_ref[...] * _silu(g_ref[:, c])).astype(BF16)

    tile = _rows(HG_TILE, HG_WIDTH)
    return pl.pallas_call(
        body, name="hg_fwd", grid=(n_tiles,),
        out_shape=[jax.ShapeDtypeStruct((s, HG_WIDTH), BF16), jax.ShapeDtypeStruct((s, HG_WIDTH), F32),
                   jax.ShapeDtypeStruct((s // HG_CHUNK * HG_DIM, HG_WIDTH), F32)],
        in_specs=[tile] * 4 + [_whole((2, HG_WIDTH)), _whole((1, HG_DIM))],
        out_specs=[tile, tile, _rows(HG_TILE_CHUNKS * HG_DIM, HG_WIDTH)],
        scratch_shapes=[pltpu.VMEM((HG_HEADS, HG_DIM, HG_DIM), F32)] + [pltpu.VMEM((HG_TILE, HG_WIDTH), F32)] * 3,
        compiler_params=_params("arbitrary"),
    )(hq, hf, hi, hgt, logits, onorm_g)


def _hg_bwd(hq, hf, hi, hgt, logits, onorm_g, o, states, dout):
    s = hq.shape[0]
    n_tiles = s // HG_TILE

    def body(q_ref, f_ref, i_ref, g_ref, lg_ref, og_ref, o_ref, st_ref, d_ref,
             dq_ref, df_ref, di_ref, dg_ref, dog_ref, dlb_ref, dstate, qf_s, kk_s, lf_s, do_s):
        @pl.when(pl.program_id(0) == 0)
        def _():
            dstate[...] = jnp.zeros_like(dstate)
            dog_ref[...] = jnp.zeros_like(dog_ref)
            dlb_ref[...] = jnp.zeros_like(dlb_ref)

        og = og_ref[...]
        dog = jnp.zeros((1, HG_DIM), F32)
        for h in range(HG_HEADS):
            c = slice(h * HG_DIM, (h + 1) * HG_DIM)
            ohat, rstd = _rms(o_ref[:, c])
            gate = g_ref[:, c]
            d = d_ref[:, c]
            dg_ref[:, c] = d * (ohat * og) * _dsilu(gate)
            dnormed = d * _silu(gate)
            dog += _rowsum(dnormed * ohat)
            do_s[:, c] = _rms_bwd(dnormed * og, ohat, rstd)
        dog_ref[...] += dog

        lb = _lower_bound(lg_ref)
        f = lb + (1.0 - lb) * _sigmoid(f_ref[...])
        kk_s[...] = 1.0 - f
        lf_s[...] = jnp.log(f)
        qf_s[...] = _silu(q_ref[...])
        causal, upper, tri, tri_t = _chunk_masks()

        def chunk(step, carry):
            ci = HG_TILE_CHUNKS - 1 - step
            rows = pl.ds(pl.multiple_of(ci * HG_CHUNK, HG_CHUNK), HG_CHUNK)
            srows = pl.ds(pl.multiple_of(ci * HG_DIM, HG_DIM), HG_DIM)
            lf = lf_s[rows, :]
            b = _dot_f32(tri, lf)
            bl = _rowsum(lf)
            ref = 0.5 * bl
            qf, kk, v, do = qf_s[rows, :], kk_s[rows, :], i_ref[rows, :], do_s[rows, :]
            eb, ebr, erb, ekd, ebl = jnp.exp(b), jnp.exp(b - ref), jnp.exp(ref - b), jnp.exp(bl - b), jnp.exp(bl)
            a_in, a_t, b_t, kd = qf * eb, qf * ebr, kk * erb, kk * ekd
            for h in range(HG_HEADS):
                c = slice(h * HG_DIM, (h + 1) * HG_DIM)
                st, dst = st_ref[srows, c], dstate[h]
                stb, dstb = st.astype(BF16), dst.astype(BF16)
                doh, vh = do[:, c], v[:, c]
                dob, vb = doh.astype(BF16), vh.astype(BF16)
                ain_h, at_h, bt_h, kd_h = a_in[:, c], a_t[:, c], b_t[:, c], kd[:, c]
                atb, btb = at_h.astype(BF16), bt_h.astype(BF16)
                d_ain = _dot(dob, stb)
                p_t = jnp.where(upper, _dot_nt(btb, atb), 0.0).astype(BF16)
                dp = jnp.where(causal, _dot_nt(dob, vb), 0.0).astype(BF16)
                dp_t = jnp.where(upper, _dot_nt(vb, dob), 0.0).astype(BF16)
                di_ref[rows, c] = _dot(p_t, dob) + _dot_nt(kd_h.astype(BF16), dstb)
                d_at = _dot(dp, btb)
                d_bt = _dot(dp_t, atb)
                d_kd = _dot(vb, dstb)
                dqf = d_ain * eb[:, c] + d_at * ebr[:, c]
                dkk = d_bt * erb[:, c] + d_kd * ekd[:, c]
                db = d_ain * ain_h + d_at * atb.astype(F32) - d_bt * btb.astype(F32) - d_kd * kd_h
                dbl = _rowsum(d_kd * kd_h) + _rowsum(dst * st) * ebl[:, c]
                dstate[h] = _dot_tn(dob, ain_h.astype(BF16)) + dst * ebl[:, c]
                dlf = _dot_f32(tri_t, db) + dbl
                qv, fr = q_ref[rows, c], f_ref[rows, c]
                lbh = lb[:, c]
                sg = _sigmoid(fr)
                dfv = dlf / (lbh + (1.0 - lbh) * sg) - dkk
                df_ref[rows, c] = dfv * (1.0 - lbh) * sg * (1.0 - sg)
                dlb_ref[:, c] += _rowsum(dfv * (1.0 - sg))
                dq_ref[rows, c] = dqf * _dsilu(qv)
            return carry

        lax.fori_loop(0, HG_TILE_CHUNKS, chunk, 0)

    rev = pl.BlockSpec((HG_TILE, HG_WIDTH), lambda i: (n_tiles - 1 - i, 0))
    return pl.pallas_call(
        body, name="hg_bwd", grid=(n_tiles,),
        out_shape=[jax.ShapeDtypeStruct((s, HG_WIDTH), F32)] * 4
        + [jax.ShapeDtypeStruct((1, HG_DIM), F32), jax.ShapeDtypeStruct((1, HG_WIDTH), F32)],
        in_specs=[rev] * 4 + [_whole((2, HG_WIDTH)), _whole((1, HG_DIM)), rev,
                              pl.BlockSpec((HG_TILE_CHUNKS * HG_DIM, HG_WIDTH), lambda i: (n_tiles - 1 - i, 0)), rev],
        out_specs=[rev] * 4 + [_whole((1, HG_DIM)), _whole((1, HG_WIDTH))],
        scratch_shapes=[pltpu.VMEM((HG_HEADS, HG_DIM, HG_DIM), F32)] + [pltpu.VMEM((HG_TILE, HG_WIDTH), F32)] * 4,
        compiler_params=_params("arbitrary"),
    )(hq, hf, hi, hgt, logits, onorm_g, o, states, dout)


def _att_consts():
    lane = lax.broadcasted_iota(jnp.int32, (ATT_BLOCK, 128), 1)
    qi = lax.broadcasted_iota(jnp.int32, (2 * ATT_BLOCK, ATT_BLOCK), 0) % ATT_BLOCK
    kj = lax.broadcasted_iota(jnp.int32, (2 * ATT_BLOCK, ATT_BLOCK), 1)
    return lane < ATT_HEAD_DIM, kj <= qi, lambda off: kj >= qi + off


def _stack_heads(x2, first):
    return jnp.concatenate([jnp.where(first, x2, 0.0), jnp.where(first, 0.0, x2)], axis=0)


def _stack_bcast(x2, first):
    other = pltpu.roll(x2, ATT_HEAD_DIM, axis=1)
    return jnp.concatenate([jnp.where(first, x2, other), jnp.where(first, other, x2)], axis=0)


def _unstack_heads(st, first):
    return jnp.where(first, st[:ATT_BLOCK], st[ATT_BLOCK:])


def _att_fwd(q, k, v, dil):
    m, width = q.shape
    nb = m // ATT_BLOCK

    def body(q_ref, kc_ref, kp_ref, vc_ref, vp_ref, o_ref, lse_ref):
        first, cur_ok, _band = _att_consts()
        prev_ok = _band(jnp.where(pl.program_id(1) > 0, 0, ATT_BLOCK))
        for j in range(ATT_WIDTH // 128):
            c = slice(j * 128, (j + 1) * 128)
            qst = _stack_heads(q_ref[:, c] * ATT_SCALE, first).astype(BF16)
            kc, kp = kc_ref[:, c].astype(BF16), kp_ref[:, c].astype(BF16)
            vc, vp = vc_ref[:, c].astype(BF16), vp_ref[:, c].astype(BF16)
            sc = jnp.where(cur_ok, _dot_nt(qst, kc), NEG)
            sp = jnp.where(prev_ok, _dot_nt(qst, kp), NEG)
            mx = jnp.maximum(jnp.max(sc, axis=-1, keepdims=True), jnp.max(sp, axis=-1, keepdims=True))
            pc, pp = jnp.exp(sc - mx), jnp.exp(sp - mx)
            den = jnp.sum(pc, axis=-1, keepdims=True) + jnp.sum(pp, axis=-1, keepdims=True)
            ost = (_dot(pc.astype(BF16), vc) + _dot(pp.astype(BF16), vp)) / den
            lse = jnp.broadcast_to(mx + jnp.log(den), (2 * ATT_BLOCK, 128))
            o_ref[:, c] = _unstack_heads(ost, first)
            lse_ref[:, c] = _unstack_heads(lse, first)

    cur = pl.BlockSpec((ATT_BLOCK, ATT_WIDTH), lambda r, n: (n, r))
    prev = pl.BlockSpec((ATT_BLOCK, ATT_WIDTH), lambda r, n: (jnp.maximum(n - 1, 0), r))
    return pl.pallas_call(
        body, name=f"att_fwd_d{dil}", grid=(dil, nb),
        out_shape=[jax.ShapeDtypeStruct((m, width), F32)] * 2,
        in_specs=[cur, cur, prev, cur, prev], out_specs=[cur, cur],
        compiler_params=_params("parallel", "arbitrary"),
    )(q, k, k, v, v)


def _att_bwd(q, k, v, do, cc, lse, dil):
    m, width = q.shape
    nb = m // ATT_BLOCK

    def body(q_ref, qx_ref, kc_ref, kp_ref, vc_ref, vp_ref, do_ref, dox_ref, cc_ref, ccx_ref, lse_ref, lsex_ref,
             dq_ref, dk_ref, dv_ref):
        first, cur_ok, _band = _att_consts()
        n = pl.program_id(1)
        prev_ok = _band(jnp.where(n > 0, 0, ATT_BLOCK))
        next_ok = _band(jnp.where(n < nb - 1, 0, ATT_BLOCK))
        for j in range(ATT_WIDTH // 128):
            c = slice(j * 128, (j + 1) * 128)
            qst = _stack_heads(q_ref[:, c] * ATT_SCALE, first).astype(BF16)
            qxst = _stack_heads(qx_ref[:, c] * ATT_SCALE, first).astype(BF16)
            dost = _stack_heads(do_ref[:, c], first).astype(BF16)
            doxst = _stack_heads(dox_ref[:, c], first).astype(BF16)
            lse_n, lse_x = _stack_bcast(lse_ref[:, c], first), _stack_bcast(lsex_ref[:, c], first)
            cc_n, cc_x = _stack_bcast(cc_ref[:, c], first), _stack_bcast(ccx_ref[:, c], first)
            kc, kp = kc_ref[:, c].astype(BF16), kp_ref[:, c].astype(BF16)
            vc, vp = vc_ref[:, c].astype(BF16), vp_ref[:, c].astype(BF16)
            p_cur = jnp.exp(jnp.where(cur_ok, _dot_nt(qst, kc), NEG) - lse_n)
            p_prev = jnp.exp(jnp.where(prev_ok, _dot_nt(qst, kp), NEG) - lse_n)
            p_next = jnp.exp(jnp.where(next_ok, _dot_nt(qxst, kc), NEG) - lse_x)
            ds_cur = (p_cur * (_dot_nt(dost, vc) + cc_n)).astype(BF16)
            ds_prev = (p_prev * (_dot_nt(dost, vp) + cc_n)).astype(BF16)
            ds_next = (p_next * (_dot_nt(doxst, vc) + cc_x)).astype(BF16)
            dq_ref[:, c] = _unstack_heads(_dot(ds_cur, kc) + _dot(ds_prev, kp), first) * ATT_SCALE
            dk_ref[:, c] = _dot_tn(ds_cur, qst) + _dot_tn(ds_next, qxst)
            dv_ref[:, c] = _dot_tn(p_cur.astype(BF16), dost) + _dot_tn(p_next.astype(BF16), doxst)

    cur = pl.BlockSpec((ATT_BLOCK, ATT_WIDTH), lambda r, n: (n, r))
    prev = pl.BlockSpec((ATT_BLOCK, ATT_WIDTH), lambda r, n: (jnp.maximum(n - 1, 0), r))
    nxt = pl.BlockSpec((ATT_BLOCK, ATT_WIDTH), lambda r, n: (jnp.minimum(n + 1, nb - 1), r))
    return pl.pallas_call(
        body, name=f"att_bwd_d{dil}", grid=(dil, nb),
        out_shape=[jax.ShapeDtypeStruct((m, width), F32)] * 3,
        in_specs=[cur, nxt, cur, prev, cur, prev, cur, nxt, cur, nxt, cur, nxt], out_specs=[cur, cur, cur],
        compiler_params=_params("parallel", "arbitrary"),
    )(q, q, k, k, v, v, do, do, cc, cc, lse, lse)


def _branch_weights(lses):
    mx = jnp.maximum(jnp.maximum(lses[0], lses[1]), lses[2])
    es = [jnp.exp(l - mx) for l in lses]
    inv = 1.0 / (es[0] + es[1] + es[2])
    return [e * inv for e in es]


def _att_combine(outs, lses, att_g):
    s = outs[0].shape[0]
    tm = 512

    def body(o0, o1, o2, l0, l1, l2, g_ref, att_ref, out_ref):
        ws = _branch_weights([l0[...], l1[...], l2[...]])
        att = ws[0] * o0[...] + ws[1] * o1[...] + ws[2] * o2[...]
        att_ref[...] = att
        ahat, _ = _rms(att)
        out_ref[...] = (ahat * g_ref[...]).astype(BF16)

    tile = _rows(tm, ATT_WIDTH)
    return pl.pallas_call(
        body, name="att_combine", grid=(s // tm,),
        out_shape=[jax.ShapeDtypeStruct((s, ATT_WIDTH), F32), jax.ShapeDtypeStruct((s, ATT_WIDTH), BF16)],
        in_specs=[tile] * 6 + [_whole((1, ATT_WIDTH))], out_specs=[tile, tile],
        compiler_params=_params("parallel"),
    )(*outs, *lses, att_g)


def _att_combine_bwd(datt_out, att, lses, att_g):
    s = att.shape[0]
    tm = 256

    def body(d_ref, att_ref, l0, l1, l2, g_ref, do0, do1, do2, cc0, cc1, cc2, dg_ref):
        @pl.when(pl.program_id(0) == 0)
        def _():
            dg_ref[...] = jnp.zeros_like(dg_ref)

        att = att_ref[...]
        ahat, rstd = _rms(att)
        d = d_ref[...]
        dg_ref[...] += _rowsum(d * ahat)
        datt = _rms_bwd(d * g_ref[...], ahat, rstd)
        hi = lax.broadcasted_iota(jnp.int32, (ATT_WIDTH, ATT_WIDTH), 0) // ATT_HEAD_DIM
        hj = lax.broadcasted_iota(jnp.int32, (ATT_WIDTH, ATT_WIDTH), 1) // ATT_HEAD_DIM
        head_sum = _dot_f32(datt * att, (hi == hj).astype(F32))
        ws = _branch_weights([l0[...], l1[...], l2[...]])
        for w, do_ref, cc_ref in zip(ws, (do0, do1, do2), (cc0, cc1, cc2)):
            do_ref[...] = (w * datt).astype(BF16)
            cc_ref[...] = -w * head_sum

    tile = _rows(tm, ATT_WIDTH)
    return pl.pallas_call(
        body, name="att_combine_bwd", grid=(s // tm,),
        out_shape=[jax.ShapeDtypeStruct((s, ATT_WIDTH), BF16)] * 3 + [jax.ShapeDtypeStruct((s, ATT_WIDTH), F32)] * 3
        + [jax.ShapeDtypeStruct((1, ATT_WIDTH), F32)],
        in_specs=[tile] * 5 + [_whole((1, ATT_WIDTH))], out_specs=[tile] * 6 + [_whole((1, ATT_WIDTH))],
        compiler_params=_params("arbitrary"),
    )(datt_out, att, *lses, att_g)


def _out_fwd(x, hg, at, mod, w_out):
    s = x.shape[0]
    tm = 512

    def body(x_ref, hg_ref, at_ref, mod_ref, w_ref, x1_ref):
        mix = _dot(hg_ref[...], w_ref[0:512, :]) + _dot(at_ref[...], w_ref[512:1024, :])
        x1_ref[...] = x_ref[...] + mod_ref[:, 2 * D_MODEL:3 * D_MODEL] * mix

    return pl.pallas_call(
        body, name="out_fwd", grid=(s // tm,), out_shape=jax.ShapeDtypeStruct((s, D_MODEL), F32),
        in_specs=[_rows(tm, D_MODEL), _rows(tm, 512), _rows(tm, 512), _whole((1, 6 * D_MODEL)), _whole((D_MODEL, D_MODEL))],
        out_specs=_rows(tm, D_MODEL), compiler_params=_params("parallel"),
    )(x, hg, at, mod, w_out)


def _out_bwd(dx1, hg, at, mod, w_out):
    s = dx1.shape[0]
    tm = 512

    def body(dx_ref, hg_ref, at_ref, mod_ref, w_ref, dhg_ref, dat_ref, dw_ref, dgate_ref):
        @pl.when(pl.program_id(0) == 0)
        def _():
            dw_ref[...] = jnp.zeros_like(dw_ref)
            dgate_ref[...] = jnp.zeros_like(dgate_ref)

        hg, at, dx = hg_ref[...], at_ref[...], dx_ref[...]
        mix = _dot(hg, w_ref[0:512, :]) + _dot(at, w_ref[512:1024, :])
        dgate_ref[...] += _rowsum(dx * mix)
        dmix = (mod_ref[:, 2 * D_MODEL:3 * D_MODEL] * dx).astype(BF16)
        dhg_ref[...] = _dot_nt(dmix, w_ref[0:512, :])
        dat_ref[...] = _dot_nt(dmix, w_ref[512:1024, :])
        dw_ref[0:512, :] += _dot_tn(hg, dmix)
        dw_ref[512:1024, :] += _dot_tn(at, dmix)

    return pl.pallas_call(
        body, name="out_bwd", grid=(s // tm,),
        out_shape=[jax.ShapeDtypeStruct((s, 512), F32)] * 2
        + [jax.ShapeDtypeStruct((D_MODEL, D_MODEL), F32), jax.ShapeDtypeStruct((1, D_MODEL), F32)],
        in_specs=[_rows(tm, D_MODEL), _rows(tm, 512), _rows(tm, 512), _whole((1, 6 * D_MODEL)), _whole((D_MODEL, D_MODEL))],
        out_specs=[_rows(tm, 512), _rows(tm, 512), _whole((D_MODEL, D_MODEL)), _whole((1, D_MODEL))],
        compiler_params=_params("arbitrary"),
    )(dx1, hg, at, mod, w_out)


FFN_CHUNK = 256


def _ffn(x1, target, mod, g2, gf, w_gu, w_down):
    s = x1.shape[0]
    tm = 256
    n_chunks = D_FF // FFN_CHUNK

    def body(x_ref, t_ref, mod_ref, g2_ref, gf_ref, wgu_hbm, wd_hbm,
             dx_ref, h2_ref, act_ref, dau_ref, dff_ref, sums_ref, loss_ref, wgu, wd, a_s, u_s, sem):
        @pl.when(pl.program_id(0) == 0)
        def _():
            c1 = pltpu.make_async_copy(wgu_hbm, wgu, sem.at[0])
            c2 = pltpu.make_async_copy(wd_hbm, wd, sem.at[1])
            c1.start()
            c2.start()
            c1.wait()
            c2.wait()
            sums_ref[...] = jnp.zeros_like(sums_ref)
            loss_ref[...] = jnp.zeros_like(loss_ref)

        x1v = x_ref[...]
        xhat, rstd = _rms(x1v)
        g2 = g2_ref[...]
        n2 = xhat * g2
        scale2 = 1.0 + mod_ref[:, 4 * D_MODEL:5 * D_MODEL]
        gate2 = mod_ref[:, 5 * D_MODEL:6 * D_MODEL]
        hb = (n2 * scale2 + mod_ref[:, 3 * D_MODEL:4 * D_MODEL]).astype(BF16)
        h2_ref[...] = hb
        ff = jnp.zeros((tm, D_MODEL), F32)
        for j in range(n_chunks):
            c = slice(j * FFN_CHUNK, (j + 1) * FFN_CHUNK)
            cu = slice(D_FF + j * FFN_CHUNK, D_FF + (j + 1) * FFN_CHUNK)
            a = _dot(hb, wgu[:, c])
            u = _dot(hb, wgu[:, cu])
            a_s[:, c] = a
            u_s[:, c] = u
            act = (_silu(a) * u).astype(BF16)
            act_ref[:, c] = act
            ff += _dot(act, wd[c, :])
        x2 = x1v + gate2 * ff
        nf, rstd_f = _rms(x2)
        gfv = gf_ref[...]
        err = nf * gfv - t_ref[...]
        loss_ref[...] += 0.5 * jnp.sum(_rowsum(err * err), axis=-1, keepdims=True) * (1.0 / D_MODEL)
        dy = err * (1.0 / D_MODEL)
        dx2 = _rms_bwd(dy * gfv, nf, rstd_f)
        dffb = (gate2 * dx2).astype(BF16)
        dff_ref[...] = dffb
        dh = jnp.zeros((tm, D_MODEL), F32)
        for j in range(n_chunks):
            c = slice(j * FFN_CHUNK, (j + 1) * FFN_CHUNK)
            cu = slice(D_FF + j * FFN_CHUNK, D_FF + (j + 1) * FFN_CHUNK)
            dact = _dot_nt(dffb, wd[c, :])
            a, u = a_s[:, c], u_s[:, c]
            da = (dact * u * _dsilu(a)).astype(BF16)
            du = (dact * _silu(a)).astype(BF16)
            dau_ref[:, c] = da
            dau_ref[:, cu] = du
            dh += _dot_nt(da, wgu[:, c]) + _dot_nt(du, wgu[:, cu])
        dn = dh * scale2
        sums_ref[0:1, :] += _rowsum(dh)
        sums_ref[1:2, :] += _rowsum(dh * n2)
        sums_ref[2:3, :] += _rowsum(dx2 * ff)
        sums_ref[3:4, :] += _rowsum(dn * xhat)
        sums_ref[4:5, :] += _rowsum(dy * nf)
        dx_ref[...] = dx2 + _rms_bwd(dn * g2, xhat, rstd)

    vec = _whole((1, D_MODEL))
    hbm = pl.BlockSpec(memory_space=pl.ANY)
    return pl.pallas_call(
        body, name="ffn", grid=(s // tm,),
        out_shape=[jax.ShapeDtypeStruct((s, D_MODEL), F32), jax.ShapeDtypeStruct((s, D_MODEL), BF16),
                   jax.ShapeDtypeStruct((s, D_FF), BF16), jax.ShapeDtypeStruct((s, 2 * D_FF), BF16),
                   jax.ShapeDtypeStruct((s, D_MODEL), BF16), jax.ShapeDtypeStruct((8, D_MODEL), F32),
                   jax.ShapeDtypeStruct((1, 128), F32)],
        in_specs=[_rows(tm, D_MODEL), _rows(tm, D_MODEL), _whole((1, 6 * D_MODEL)), vec, vec, hbm, hbm],
        out_specs=[_rows(tm, D_MODEL), _rows(tm, D_MODEL), _rows(tm, D_FF), _rows(tm, 2 * D_FF), _rows(tm, D_MODEL),
                   _whole((8, D_MODEL)), _whole((1, 128))],
        scratch_shapes=[pltpu.VMEM((D_MODEL, 2 * D_FF), BF16), pltpu.VMEM((D_FF, D_MODEL), BF16),
                        pltpu.VMEM((tm, D_FF), F32), pltpu.VMEM((tm, D_FF), F32), pltpu.SemaphoreType.DMA((2,))],
        compiler_params=_params("arbitrary"),
    )(x1, target, mod, g2, gf, w_gu, w_down)


def _weight_grad(a, b, name):
    s, m = a.shape
    n = b.shape[1]
    ts, tn = min(s, 2048), 512

    def body(a_ref, b_ref, o_ref):
        @pl.when(pl.program_id(1) == 0)
        def _():
            o_ref[...] = jnp.zeros_like(o_ref)

        o_ref[...] += _dot_tn(a_ref[...], b_ref[...])

    return pl.pallas_call(
        body, name=name, grid=(n // tn, s // ts), out_shape=jax.ShapeDtypeStruct((m, n), F32),
        in_specs=[pl.BlockSpec((ts, m), lambda j, i: (i, 0)), pl.BlockSpec((ts, tn), lambda j, i: (i, j))],
        out_specs=pl.BlockSpec((m, tn), lambda j, i: (0, j)),
        compiler_params=_params("parallel", "arbitrary"),
    )(a, b)


def _adamw_math(w, g, m, v):
    m = ADAM_B1 * m + (1.0 - ADAM_B1) * g
    v = ADAM_B2 * v + (1.0 - ADAM_B2) * (g * g)
    m_hat = m / (1.0 - ADAM_B1 ** ADAM_STEP)
    v_hat = v / (1.0 - ADAM_B2 ** ADAM_STEP)
    delta = -ADAM_LR * (m_hat / (jnp.sqrt(v_hat) + ADAM_EPS) + ADAM_WD * w)
    return delta, m, v


def _adamw_shard(chip, w, m, v, partial, got, name):
    r, c = w.shape
    tr = _shard_rows(r)

    def body(chip_ref, w_ref, m_ref, v_ref, own_ref, g0, g1, g2, grad_ref, d_ref, nm_ref, nv_ref):
        g = ((own_ref[...] + g0[...].astype(F32)) + g1[...].astype(F32)) + g2[...].astype(F32)
        grad_ref[...] = g
        d_ref[...], nm_ref[...], nv_ref[...] = _adamw_math(w_ref[...], g, m_ref[...], v_ref[...])

    tile = pl.BlockSpec((tr, c), lambda i, chip_ref: (i, 0))
    own = pl.BlockSpec((None, tr, c), lambda i, chip_ref: (chip_ref[0], i, 0))
    part = [pl.BlockSpec((None, tr, c), functools.partial(lambda j, i, chip_ref: (j, i, 0), j)) for j in range(3)]
    return pl.pallas_call(
        body, name=name,
        grid_spec=pltpu.PrefetchScalarGridSpec(num_scalar_prefetch=1, grid=(r // tr,), in_specs=[tile] * 3 + [own] + part,
                                               out_specs=[tile] * 4),
        out_shape=[jax.ShapeDtypeStruct((r, c), F32)] * 4, compiler_params=_params("parallel"),
    )(chip, w, m, v, partial, got, got, got)


def _small_update(small_all, dmod_blocks, c_all, logits, w_ada, m_ada, v_ada, smalls):
    def body(sm_ref, dm_ref, c_ref, lg_ref, wa_ref, ma_ref, va_ref, *rest):
        ins, outs = rest[:21], rest[21:]
        _, me = _flip(0)
        tot = sm_ref[0:1, :]
        for i in range(1, N_DEV):
            tot = tot + sm_ref[i:i + 1, :]
        loss_ref = outs[0]
        loss_ref[...] = tot[:, SM_LOSS:SM_LOSS + 128]
        g_ada = lax.dot_general(_silu(c_ref[...]), dm_ref[me], (((0,), (0,)), ((), ())),
                                preferred_element_type=F32, precision=HIGHEST)
        outs[1][...] = g_ada
        outs[2][...], outs[3][...], outs[4][...] = _adamw_math(wa_ref[...], g_ada, ma_ref[...], va_ref[...])
        p0 = _lower_bound(lg_ref)
        dl0 = tot[:, SM_LB:SM_LB + 512] * p0 * (1.0 - p0)
        grads = [tot[:, SM_MOD:SM_MOD + 6 * D_MODEL], tot[:, SM_G1:SM_G1 + D_MODEL], tot[:, SM_G2:SM_G2 + D_MODEL],
                 tot[:, SM_GF:SM_GF + D_MODEL], tot[:, SM_ATT:SM_ATT + 512], tot[:, SM_HG:SM_HG + 128],
                 jnp.where(lax.broadcasted_iota(jnp.int32, (2, 512), 0) == 0, dl0, -dl0)]
        for i, g in enumerate(grads):
            w_ref, m_ref, v_ref = ins[3 * i:3 * i + 3]
            o = outs[5 + 4 * i:9 + 4 * i]
            o[0][...] = g
            o[1][...], o[2][...], o[3][...] = _adamw_math(w_ref[...], g, m_ref[...], v_ref[...])

    flat = [t for trio in smalls for t in trio]
    vm = pl.BlockSpec(memory_space=pltpu.VMEM)
    out_shape = [jax.ShapeDtypeStruct((1, 128), F32)] + [jax.ShapeDtypeStruct(w_ada.shape, F32)] * 4
    for trio in smalls:
        out_shape += [jax.ShapeDtypeStruct(trio[0].shape, F32)] * 4
    return pl.pallas_call(
        body, name="small_update", out_shape=out_shape,
        in_specs=[vm] * (7 + len(flat)), out_specs=[vm] * len(out_shape),
        compiler_params=pltpu.CompilerParams(vmem_limit_bytes=V7X_VMEM_LIMIT),
    )(small_all, dmod_blocks, c_all, logits, w_ada, m_ada, v_ada, *flat)


def _dilate(t, d):
    return t if d == 1 else t.reshape(t.shape[0] // d, d * t.shape[1])


def _undilate(t, d):
    return t if d == 1 else t.reshape(t.shape[0] * d, t.shape[1] // d)


def kernel(x, c, w_ada, b_ada, norm1_g, w_in, hg_lb_logits, hg_onorm_g, att_onorm_g, w_out, norm2_g, w_gate_up, w_down, final_g, loss_target, m_w_ada, m_b_ada, m_norm1_g, m_w_in, m_hg_lb_logits, m_hg_onorm_g, m_att_onorm_g, m_w_out, m_norm2_g, m_w_gate_up, m_w_down, m_final_g, v_w_ada, v_b_ada, v_norm1_g, v_w_in, v_hg_lb_logits, v_hg_onorm_g, v_att_onorm_g, v_w_out, v_norm2_g, v_w_gate_up, v_w_down, v_final_g):
    x2d, target = x[0], loss_target[0]
    seq = x2d.shape[0]
    assert seq % (ATT_BLOCK * max(DILATIONS)) == 0 and seq % HG_TILE == 0
    gf = final_g.reshape(1, D_MODEL)

    c_all = _exchange_small(jnp.broadcast_to(c, (8, D_MODEL)), None, "gather_c")[:, 0, :]
    ada = _ada_rows(c_all, w_ada[0], b_ada)
    mod = _exchange_small(ada, 1, "scatter_mod").reshape(1, 6 * D_MODEL)

    shards = [w_in[0].astype(BF16), w_out[0].astype(BF16), w_gate_up[0].astype(BF16), w_down[0].astype(BF16)]
    g_in, g_out, g_gu, g_down = _gather_weights(shards)
    w_in_b = jnp.transpose(g_in, (1, 0, 2)).reshape(D_MODEL, IN_WIDTH)
    w_out_b = g_out.reshape(D_MODEL, D_MODEL)
    w_gu_b = jnp.transpose(g_gu, (1, 0, 2)).reshape(D_MODEL, 2 * D_FF)
    w_down_b = g_down.reshape(D_FF, D_MODEL)

    grad_x, dw_in, dw_out, dw_gu, dw_down, small = _block_step(
        x2d, target, mod, norm1_g, hg_lb_logits, hg_onorm_g, att_onorm_g, norm2_g, gf, w_in_b, w_out_b, w_gu_b, w_down_b)

    grads8 = [dw_in.reshape(D_MODEL, 4, 2, IN_WIDTH // N_DEV).transpose(1, 2, 0, 3),
              dw_out.reshape(4, 2, D_MODEL // N_DEV, D_MODEL),
              dw_gu.reshape(D_MODEL, 4, 2, 2 * D_FF // N_DEV).transpose(1, 2, 0, 3),
              dw_down.reshape(4, 2, D_FF // N_DEV, D_MODEL)]
    core = lax.axis_index("c").astype(jnp.int32).reshape(1)
    chip = (2 * lax.axis_index("x") + lax.axis_index("y")).astype(jnp.int32).reshape(1)
    got = _reduce_pairs(grads8)
    sums = [_pair_sum(core, g, b, f"pair_sum_{i}") for i, (g, b) in enumerate(zip(grads8, got))]
    recv = _reduce_chips([s16 for _, s16 in sums])
    big = []
    for i, (w, m, v) in enumerate([(w_in, m_w_in, v_w_in), (w_out, m_w_out, v_w_out),
                                   (w_gate_up, m_w_gate_up, v_w_gate_up), (w_down, m_w_down, v_w_down)]):
        big.append([t[None] for t in _adamw_shard(chip, w[0], m[0], v[0], sums[i][0], recv[i], f"adamw_{i}")])

    small_all = _exchange_small(jnp.broadcast_to(small, (8, SM_WIDTH)), None, "gather_small")[:, 0, :]
    smalls = [(b_ada, m_b_ada, v_b_ada), (norm1_g, m_norm1_g, v_norm1_g), (norm2_g, m_norm2_g, v_norm2_g),
              (gf, m_final_g.reshape(1, D_MODEL), v_final_g.reshape(1, D_MODEL)),
              (att_onorm_g, m_att_onorm_g, v_att_onorm_g), (hg_onorm_g, m_hg_onorm_g, v_hg_onorm_g),
              (hg_lb_logits, m_hg_lb_logits, v_hg_lb_logits)]
    dmod_blocks = small_all[:, :6 * D_MODEL].reshape(N_DEV, N_DEV, 6 * D_MODEL // N_DEV).transpose(1, 0, 2)
    res = _small_update(small_all, dmod_blocks, c_all, hg_lb_logits, w_ada[0], m_w_ada[0], v_w_ada[0], smalls)
    loss = res[0][0, 0]
    ada4 = [t[None] for t in res[1:5]]
    sm4 = {n: list(res[5 + 4 * i:9 + 4 * i]) for i, n in enumerate(["b_ada", "norm1_g", "norm2_g", "final_g", "att", "hg", "lb"])}
    sm4["final_g"] = [t.reshape(D_MODEL) for t in sm4["final_g"]]

    order = [ada4, sm4["b_ada"], sm4["norm1_g"], big[0], sm4["lb"], sm4["hg"], sm4["att"], big[1], sm4["norm2_g"], big[2], big[3],
             sm4["final_g"]]
    return (loss, grad_x[None], *[o[0] for o in order], *[o[1] for o in order], *[o[2] for o in order], *[o[3] for o in order])


def _block_step(x2d, target, mod, norm1_g, hg_lb_logits, hg_onorm_g, att_onorm_g, norm2_g, gf, w_in_b, w_out_b, w_gu_b, w_down_b):
    h1, hq, hf, hi, hgt, aq, ak, av = _in_fwd(x2d, mod, norm1_g, w_in_b)
    hg_out, hg_o, hg_states = _hg_fwd(hq, hf, hi, hgt, hg_lb_logits, hg_onorm_g)
    views = [(_dilate(aq, d), _dilate(ak, d), _dilate(av, d)) for d in DILATIONS]
    branch = [_att_fwd(*views[i], d) for i, d in enumerate(DILATIONS)]
    outs = [_undilate(branch[i][0], d) for i, d in enumerate(DILATIONS)]
    lses = [_undilate(branch[i][1], d) for i, d in enumerate(DILATIONS)]
    att, att_out = _att_combine(outs, lses, att_onorm_g)
    x1 = _out_fwd(x2d, hg_out, att_out, mod, w_out_b)

    dx1, h2, act, dau, dff, ffn_sums, loss_part = _ffn(x1, target, mod, norm2_g, gf, w_gu_b, w_down_b)
    dw_gu = _weight_grad(h2, dau, "dw_gate_up")
    dw_down = _weight_grad(act, dff, "dw_down")

    dhg, dat, dw_out, dgate1 = _out_bwd(dx1, hg_out, att_out, mod, w_out_b)
    comb = _att_combine_bwd(dat, att, lses, att_onorm_g)
    dos, ccs, d_att_g = comb[0:3], comb[3:6], comb[6]
    datt = []
    for i, d in enumerate(DILATIONS):
        grads = _att_bwd(*views[i], _dilate(dos[i], d), _dilate(ccs[i], d), branch[i][1], d)
        datt.append([_undilate(g, d) for g in grads])
    dhq, dhf, dhi, dhgt, d_hg_g, d_lb = _hg_bwd(hq, hf, hi, hgt, hg_lb_logits, hg_onorm_g, hg_o, hg_states, dhg)
    dps = [dhq, dhf, dhi, dhgt] + [datt[i][j] for j in range(3) for i in range(3)]
    grad_x, dp_b, dshift1, dscale1, d_g1 = _in_bwd(x2d, dx1, mod, norm1_g, w_in_b, dps)
    dw_in = _weight_grad(h1, dp_b, "dw_in")
    small = jnp.concatenate([dshift1, dscale1, dgate1, ffn_sums[0:1], ffn_sums[1:2], ffn_sums[2:3], d_g1, ffn_sums[3:4],
                             ffn_sums[4:5], d_att_g, d_lb, d_hg_g, loss_part], axis=1)
    return grad_x, dw_in, dw_out, dw_gu, dw_down, small
```

```python
import functools

import jax
import jax.numpy as jnp
from jax import lax
from jax.experimental import pallas as pl
from jax.experimental.pallas import tpu as pltpu

F32 = jnp.float32
BF16 = jnp.bfloat16
HIGHEST = lax.Precision.HIGHEST
MESH = pl.DeviceIdType.MESH

D_MODEL = 1024
N_DEV = 8
HG_HEADS = 4
HG_DIM = 128
HG_WIDTH = HG_HEADS * HG_DIM
HG_CHUNK = 64
ATT_WIDTH = 512
ATT_HEAD_DIM = 64
ATT_BLOCK = 128
DILATIONS = (1, 4, 16)
ATT_SCALE = ATT_HEAD_DIM ** -0.5
D_FF = 2816
IN_WIDTH = 7 * 512
RMS_EPS = 1e-6
NEG = -1e30

ADAM_LR = 0.001
ADAM_B1 = 0.9
ADAM_B2 = 0.999
ADAM_EPS = 1e-08
ADAM_WD = 0.01
ADAM_STEP = 10

V7X_VMEM_LIMIT = 56 * 1024 * 1024

SM_MOD = 0
SM_G1 = 6 * D_MODEL
SM_G2 = 7 * D_MODEL
SM_GF = 8 * D_MODEL
SM_ATT = 9 * D_MODEL
SM_LB = 9 * D_MODEL + 512
SM_HG = 10 * D_MODEL
SM_LOSS = 10 * D_MODEL + 128
SM_WIDTH = 10 * D_MODEL + 256


def _params(*sem, vmem=V7X_VMEM_LIMIT):
    return pltpu.CompilerParams(dimension_semantics=sem, vmem_limit_bytes=vmem)


def _dot(a, b):
    return jnp.dot(a, b, preferred_element_type=F32)


def _dot_nt(a, b):
    return lax.dot_general(a, b, (((1,), (1,)), ((), ())), preferred_element_type=F32)


def _dot_tn(a, b):
    return lax.dot_general(a, b, (((0,), (0,)), ((), ())), preferred_element_type=F32)


def _dot_f32(a, b):
    return jnp.dot(a, b, preferred_element_type=F32, precision=HIGHEST)


def _sigmoid(x):
    return 1.0 / (1.0 + jnp.exp(-x))


def _silu(x):
    return x * _sigmoid(x)


def _dsilu(x):
    s = _sigmoid(x)
    return s * (1.0 + x * (1.0 - s))


def _rms(x):
    rstd = lax.rsqrt(jnp.mean(x * x, axis=-1, keepdims=True) + RMS_EPS)
    return x * rstd, rstd


def _rms_bwd(dn, xhat, rstd):
    return rstd * (dn - xhat * jnp.mean(dn * xhat, axis=-1, keepdims=True))


def _rowsum(x):
    return jnp.sum(x, axis=0, keepdims=True)


def _rows(tm, n):
    return pl.BlockSpec((tm, n), lambda i: (i, 0))


def _whole(shape):
    return pl.BlockSpec(shape, lambda i: (0,) * len(shape))


def _mesh_pos():
    return lax.axis_index("x"), lax.axis_index("y"), lax.axis_index("c")


def _flip(k):
    x, y, c = _mesh_pos()
    px = 1 - x if k & 4 else x
    py = 1 - y if k & 2 else y
    pc = 1 - c if k & 1 else c
    return (px, py, pc), 4 * px + 2 * py + pc


def _exchange_small(x, rows_per_peer, name):
    r_all, cols = x.shape
    r_out = r_all if rows_per_peer is None else rows_per_peer

    def body(x_ref, out_ref, send_sems, recv_sems):
        _, me = _flip(0)

        def src(pid):
            if rows_per_peer is None:
                return x_ref
            return x_ref.at[pl.ds(pl.multiple_of(pid * r_out, r_out), r_out), :]

        if rows_per_peer is None:
            out_ref[me] = x_ref[...]
        else:
            out_ref[me] = x_ref[pl.ds(pl.multiple_of(me * r_out, r_out), r_out), :]
        sends = []
        for k in range(1, N_DEV):
            dev, pid = _flip(k)
            cp = pltpu.make_async_remote_copy(src_ref=src(pid), dst_ref=out_ref.at[me], send_sem=send_sems.at[k - 1],
                                              recv_sem=recv_sems.at[k - 1], device_id=dev, device_id_type=MESH)
            cp.start()
            sends.append(cp)
        for k in range(1, N_DEV):
            dev, pid = _flip(k)
            pltpu.make_async_remote_copy(src_ref=src(pid), dst_ref=out_ref.at[pid], send_sem=send_sems.at[k - 1],
                                         recv_sem=recv_sems.at[k - 1], device_id=dev, device_id_type=MESH).wait_recv()
        for cp in sends:
            cp.wait_send()

    return pl.pallas_call(
        body, name=name,
        out_shape=jax.ShapeDtypeStruct((N_DEV, r_out, cols), x.dtype),
        in_specs=[pl.BlockSpec(memory_space=pltpu.VMEM)],
        out_specs=pl.BlockSpec(memory_space=pltpu.VMEM),
        scratch_shapes=[pltpu.SemaphoreType.DMA((N_DEV - 1,)), pltpu.SemaphoreType.DMA((N_DEV - 1,))],
    )(x)


def _gather_weights(shards):
    n = len(shards)

    def body(*refs):
        xs, outs = refs[:n], refs[n:2 * n]
        send_sems, recv_sems, local_sems = refs[2 * n:]
        x, y, c = _mesh_pos()
        me, sibling = (x, y, c), (x, y, 1 - c)
        chips = [(1 - x, y), (x, 1 - y), (1 - x, 1 - y)]

        def blk(a, px, py, pc):
            return outs[a].at[4 * px + 2 * py + pc]

        def copy(a, k, block, to, src=None):
            return pltpu.make_async_remote_copy(
                src_ref=blk(a, *block) if src is None else src, dst_ref=blk(a, *block),
                send_sem=send_sems.at[a * 7 + k], recv_sem=recv_sems.at[a * 7 + k], device_id=to, device_id_type=MESH)

        mine = [pltpu.make_async_copy(xs[a], blk(a, *me), local_sems.at[a]) for a in range(n)]
        for cp in mine:
            cp.start()
        first = []
        for a in range(n):
            first.append(copy(a, 0, me, sibling, src=xs[a]))
            first += [copy(a, 1 + j, me, (*chip, c), src=xs[a]) for j, chip in enumerate(chips)]
        for cp in first:
            cp.start()
        passed = []
        for j, chip in enumerate(chips):
            for a in range(n):
                copy(a, 1 + j, (*chip, c), me).wait_recv()
                cp = copy(a, 4 + j, (*chip, c), sibling)
                cp.start()
                passed.append(cp)
        for a in range(n):
            copy(a, 0, sibling, me).wait_recv()
            for j, chip in enumerate(chips):
                copy(a, 4 + j, (*chip, 1 - c), me).wait_recv()
        for cp in first + passed:
            cp.wait_send()
        for cp in mine:
            cp.wait()

    hbm = pl.BlockSpec(memory_space=pl.ANY)
    return pl.pallas_call(
        body, name="gather_weights",
        out_shape=[jax.ShapeDtypeStruct((N_DEV,) + s.shape, s.dtype) for s in shards],
        in_specs=[hbm] * n, out_specs=[hbm] * n,
        scratch_shapes=[pltpu.SemaphoreType.DMA((7 * n,)), pltpu.SemaphoreType.DMA((7 * n,)), pltpu.SemaphoreType.DMA((n,))],
    )(*shards)


def _reduce_pairs(grads):
    n = len(grads)

    def body(*refs):
        gs, got = refs[:n], refs[n:2 * n]
        send_sems, recv_sems = refs[2 * n:]
        x, y, c = _mesh_pos()
        sends = []
        for a in range(n):
            for chip in range(4):
                i = a * 4 + chip
                cp = pltpu.make_async_remote_copy(src_ref=gs[a].at[chip, 1 - c], dst_ref=got[a].at[chip],
                                                  send_sem=send_sems.at[i], recv_sem=recv_sems.at[i],
                                                  device_id=(x, y, 1 - c), device_id_type=MESH)
                cp.start()
                sends.append(cp)
        for cp in sends:
            cp.wait_recv()
        for cp in sends:
            cp.wait_send()

    hbm = pl.BlockSpec(memory_space=pl.ANY)
    return pl.pallas_call(
        body, name="reduce_pairs", out_shape=[jax.ShapeDtypeStruct((4,) + g.shape[2:], g.dtype) for g in grads],
        in_specs=[hbm] * n, out_specs=[hbm] * n,
        scratch_shapes=[pltpu.SemaphoreType.DMA((4 * n,)), pltpu.SemaphoreType.DMA((4 * n,))],
    )(*grads)


def _reduce_chips(partials):
    n = len(partials)
    flips = (4, 2, 6)

    def body(*refs):
        ps, got = refs[:n], refs[n:2 * n]
        send_sems, recv_sems = refs[2 * n:]
        sends = []
        for a in range(n):
            for j, k in enumerate(flips):
                dev, _ = _flip(k)
                cp = pltpu.make_async_remote_copy(src_ref=ps[a].at[2 * dev[0] + dev[1]], dst_ref=got[a].at[j],
                                                  send_sem=send_sems.at[a * 3 + j], recv_sem=recv_sems.at[a * 3 + j],
                                                  device_id=dev, device_id_type=MESH)
                cp.start()
                sends.append(cp)
        for cp in sends:
            cp.wait_recv()
        for cp in sends:
            cp.wait_send()

    hbm = pl.BlockSpec(memory_space=pl.ANY)
    return pl.pallas_call(
        body, name="reduce_chips", out_shape=[jax.ShapeDtypeStruct((3,) + p.shape[1:], p.dtype) for p in partials],
        in_specs=[hbm] * n, out_specs=[hbm] * n,
        scratch_shapes=[pltpu.SemaphoreType.DMA((3 * n,)), pltpu.SemaphoreType.DMA((3 * n,))],
    )(*partials)


_HBM = pl.BlockSpec(memory_space=pltpu.HBM)
_SEM = pl.BlockSpec(memory_space=pltpu.SEMAPHORE)
_DATAFLOW = pltpu.SideEffectType.DATAFLOW_SIDE_EFFECTING


def _copies_start(name, plan, n_copies, srcs, lands):
    bufs = list(srcs) + list(lands)
    nb = len(bufs)

    def body(*refs):
        ins, send_sems, recv_sems, token = refs[:nb], refs[nb], refs[nb + 1], refs[-1]
        for i, (src, dst, dev) in enumerate(plan(ins[:len(srcs)], ins[len(srcs):])):
            pltpu.make_async_remote_copy(src_ref=src, dst_ref=dst, send_sem=send_sems.at[i], recv_sem=recv_sems.at[i],
                                         device_id=dev, device_id_type=MESH).start()
        token[...] = jnp.zeros_like(token)

    outs = pl.pallas_call(
        body, name=name,
        out_shape=(pltpu.SemaphoreType.DMA((n_copies,)), pltpu.SemaphoreType.DMA((n_copies,)),
                   *[pltpu.HBM(b.shape, b.dtype) for b in bufs], jax.ShapeDtypeStruct((8, 128), F32)),
        in_specs=[_HBM] * nb, out_specs=(_SEM, _SEM, *[_HBM] * nb, pl.BlockSpec(memory_space=pltpu.VMEM)),
        input_output_aliases={i: 2 + i for i in range(nb)},
        compiler_params=pltpu.CompilerParams(has_side_effects=_DATAFLOW),
    )(*[pltpu.with_memory_space_constraint(b, pltpu.HBM) for b in bufs])
    return outs[0], outs[1], list(outs[2:2 + len(srcs)]), list(outs[2 + len(srcs):2 + nb]), outs[-1]


def _copies_wait(name, plan, send_sems, recv_sems, srcs, lands, after):
    bufs = list(srcs) + list(lands)
    nb = len(bufs)

    def body(*refs):
        ins, send_ref, recv_ref = refs[:nb], refs[nb], refs[nb + 1]
        for i, (src, dst, dev) in enumerate(plan(ins[:len(srcs)], ins[len(srcs):])):
            cp = pltpu.make_async_remote_copy(src_ref=src, dst_ref=dst, send_sem=send_ref.at[i], recv_sem=recv_ref.at[i],
                                              device_id=dev, device_id_type=MESH)
            cp.wait_send()
            cp.wait_recv()

    outs = pl.pallas_call(
        body, name=name, out_shape=[pltpu.HBM(b.shape, b.dtype) for b in bufs],
        in_specs=[_HBM] * nb + [_SEM, _SEM, pl.BlockSpec(memory_space=pl.ANY)], out_specs=[_HBM] * nb,
        input_output_aliases={i: i for i in range(nb)},
        compiler_params=pltpu.CompilerParams(has_side_effects=_DATAFLOW),
    )(*bufs, send_sems, recv_sems, after)
    return list(outs[len(srcs):])


def _plan_gather_own(srcs, lands):
    _, me = _flip(0)
    return [(srcs[a], lands[a].at[me], _flip(k)[0]) for a in range(len(srcs)) for k in (1, 4, 2, 6)]


def _plan_gather_pass(srcs, lands):
    sibling = _flip(1)[0]
    plan = []
    for land in lands:
        for k in (4, 2, 6):
            block = land.at[_flip(k)[1]]
            plan.append((block, block, sibling))
    return plan


def _plan_reduce_pairs(srcs, lands):
    x, y, c = _mesh_pos()
    return [(srcs[a].at[chip, 1 - c], lands[a].at[chip], (x, y, 1 - c)) for a in range(len(srcs)) for chip in range(4)]


def _plan_reduce_chips(srcs, lands):
    plan = []
    for a in range(len(srcs)):
        for j, k in enumerate((4, 2, 6)):
            dev = _flip(k)[0]
            plan.append((srcs[a].at[2 * dev[0] + dev[1]], lands[a].at[j], dev))
    return plan


def _shard_rows(r):
    return r // 2 if r % 32 == 0 else r


def _pair_sum(core, grads, got, name):
    _, _, r, c = grads.shape
    tr = _shard_rows(r)

    def body(core_ref, a_ref, b_ref, o_ref, ob_ref):
        s = a_ref[...] + b_ref[...]
        o_ref[...] = s
        ob_ref[...] = s.astype(BF16)

    spec = pl.BlockSpec((None, tr, c), lambda i, j, core_ref: (i, j, 0))
    return pl.pallas_call(
        body, name=name,
        grid_spec=pltpu.PrefetchScalarGridSpec(
            num_scalar_prefetch=1, grid=(4, r // tr),
            in_specs=[pl.BlockSpec((None, None, tr, c), lambda i, j, core_ref: (i, core_ref[0], j, 0)), spec],
            out_specs=[spec, spec]),
        out_shape=[jax.ShapeDtypeStruct((4, r, c), F32), jax.ShapeDtypeStruct((4, r, c), BF16)],
        compiler_params=_params("parallel", "parallel"),
    )(core, grads, got)


def _ada_rows(c_all, w_ada, b_ada):
    n_cols = w_ada.shape[1]

    def body(c_ref, w_ref, b_ref, o_ref):
        _, me = _flip(0)
        bias = b_ref[:, pl.ds(pl.multiple_of(me * n_cols, 128), n_cols)]
        o_ref[...] = _dot_f32(_silu(c_ref[...]), w_ref[...]) + bias

    return pl.pallas_call(
        body, name="ada_rows", out_shape=jax.ShapeDtypeStruct((N_DEV, n_cols), F32),
        in_specs=[pl.BlockSpec(memory_space=pltpu.VMEM)] * 3, out_specs=pl.BlockSpec(memory_space=pltpu.VMEM),
    )(c_all, w_ada, b_ada)


def _in_fwd(x, mod, g1, w_in):
    s = x.shape[0]
    tm = 256

    def body(x_ref, mod_ref, g_ref, w_ref, h_ref, *outs):
        xhat, _ = _rms(x_ref[...])
        h = (xhat * g_ref[...]) * (1.0 + mod_ref[:, D_MODEL:2 * D_MODEL]) + mod_ref[:, 0:D_MODEL]
        hb = h.astype(BF16)
        h_ref[...] = hb
        for j, o_ref in enumerate(outs):
            o_ref[...] = _dot(hb, w_ref[:, j * 512:(j + 1) * 512]).astype(o_ref.dtype)

    return pl.pallas_call(
        body, name="in_fwd", grid=(s // tm,),
        out_shape=[jax.ShapeDtypeStruct((s, D_MODEL), BF16)] + [jax.ShapeDtypeStruct((s, 512), F32)] * 4
        + [jax.ShapeDtypeStruct((s, 512), BF16)] * 3,
        in_specs=[_rows(tm, D_MODEL), _whole((1, 6 * D_MODEL)), _whole((1, D_MODEL)), _whole((D_MODEL, IN_WIDTH))],
        out_specs=[_rows(tm, D_MODEL)] + [_rows(tm, 512)] * 7,
        compiler_params=_params("parallel"),
    )(x, mod, g1, w_in)


def _in_bwd(x, dx1, mod, g1, w_in, dps):
    s = x.shape[0]
    tm = 256

    def body(x_ref, dx_ref, mod_ref, g_ref, w_ref, *rest):
        dp_refs, (gx_ref, dpb_ref, dsh_ref, dsc_ref, dg_ref) = rest[:13], rest[13:]
        pieces = [dp_refs[j][...] for j in range(4)]
        pieces += [dp_refs[4 + 3 * j][...] + dp_refs[5 + 3 * j][...] + dp_refs[6 + 3 * j][...] for j in range(3)]
        dh = jnp.zeros((tm, D_MODEL), F32)
        for j, p in enumerate(pieces):
            pb = p.astype(BF16)
            dpb_ref[:, j * 512:(j + 1) * 512] = pb
            dh += _dot_nt(pb, w_ref[:, j * 512:(j + 1) * 512])
        xhat, rstd = _rms(x_ref[...])
        g = g_ref[...]
        scale1 = 1.0 + mod_ref[:, D_MODEL:2 * D_MODEL]
        n1 = xhat * g

        @pl.when(pl.program_id(0) == 0)
        def _():
            dsh_ref[...] = jnp.zeros_like(dsh_ref)
            dsc_ref[...] = jnp.zeros_like(dsc_ref)
            dg_ref[...] = jnp.zeros_like(dg_ref)

        dsh_ref[...] += _rowsum(dh)
        dsc_ref[...] += _rowsum(dh * n1)
        dn = dh * scale1
        dg_ref[...] += _rowsum(dn * xhat)
        gx_ref[...] = dx_ref[...] + _rms_bwd(dn * g, xhat, rstd)

    vec = _whole((1, D_MODEL))
    return pl.pallas_call(
        body, name="in_bwd", grid=(s // tm,),
        out_shape=[jax.ShapeDtypeStruct((s, D_MODEL), F32), jax.ShapeDtypeStruct((s, IN_WIDTH), BF16)]
        + [jax.ShapeDtypeStruct((1, D_MODEL), F32)] * 3,
        in_specs=[_rows(tm, D_MODEL), _rows(tm, D_MODEL), _whole((1, 6 * D_MODEL)), vec, _whole((D_MODEL, IN_WIDTH))]
        + [_rows(tm, 512)] * 13,
        out_specs=[_rows(tm, D_MODEL), _rows(tm, IN_WIDTH), vec, vec, vec],
        compiler_params=_params("arbitrary"),
    )(x, dx1, mod, g1, w_in, *dps)


HG_TILE = 512
HG_TILE_CHUNKS = HG_TILE // HG_CHUNK


def _lower_bound(lg_ref):
    return 1.0 / (1.0 + jnp.exp(lg_ref[1:2, :] - lg_ref[0:1, :]))


def _chunk_masks():
    r = lax.broadcasted_iota(jnp.int32, (HG_CHUNK, HG_CHUNK), 0)
    c = lax.broadcasted_iota(jnp.int32, (HG_CHUNK, HG_CHUNK), 1)
    return r >= c, c >= r, (r >= c).astype(F32), (c >= r).astype(F32)


def _hg_fwd(hq, hf, hi, hgt, logits, onorm_g):
    s = hq.shape[0]
    n_tiles = s // HG_TILE

    def body(q_ref, f_ref, i_ref, g_ref, lg_ref, og_ref, out_ref, o_ref, st_ref, state, qf_s, kk_s, lf_s):
        @pl.when(pl.program_id(0) == 0)
        def _():
            state[...] = jnp.zeros_like(state)

        lb = _lower_bound(lg_ref)
        f = lb + (1.0 - lb) * _sigmoid(f_ref[...])
        kk_s[...] = 1.0 - f
        lf_s[...] = jnp.log(f)
        qf_s[...] = _silu(q_ref[...])
        causal, _, tri, _ = _chunk_masks()

        def chunk(ci, carry):
            rows = pl.ds(pl.multiple_of(ci * HG_CHUNK, HG_CHUNK), HG_CHUNK)
            srows = pl.ds(pl.multiple_of(ci * HG_DIM, HG_DIM), HG_DIM)
            lf = lf_s[rows, :]
            b = _dot_f32(tri, lf)
            bl = _rowsum(lf)
            ref = 0.5 * bl
            qf, kk, v = qf_s[rows, :], kk_s[rows, :], i_ref[rows, :]
            a_in = (qf * jnp.exp(b)).astype(BF16)
            a_t = (qf * jnp.exp(b - ref)).astype(BF16)
            b_t = (kk * jnp.exp(ref - b)).astype(BF16)
            kd = kk * jnp.exp(bl - b)
            ebl = jnp.exp(bl)
            vb = v.astype(BF16)
            for h in range(HG_HEADS):
                c = slice(h * HG_DIM, (h + 1) * HG_DIM)
                st = state[h]
                st_ref[srows, c] = st
                p = jnp.where(causal, _dot_nt(a_t[:, c], b_t[:, c]), 0.0)
                o_ref[rows, c] = _dot(p.astype(BF16), vb[:, c]) + _dot_nt(a_in[:, c], st.astype(BF16))
                state[h] = st * ebl[:, c] + _dot_tn(vb[:, c], kd[:, c].astype(BF16))
            return carry

        lax.fori_loop(0, HG_TILE_CHUNKS, chunk, 0)
        for h in range(HG_HEADS):
            c = slice(h * HG_DIM, (h + 1) * HG_DIM)
            ohat, _ = _rms(o_ref[:, c])
            out_ref[:, c] = (ohat * og_ref[...] * _silu(g_ref[:, c])).astype(BF16)

    tile = _rows(HG_TILE, HG_WIDTH)
    return pl.pallas_call(
        body, name="hg_fwd", grid=(n_tiles,),
        out_shape=[jax.ShapeDtypeStruct((s, HG_WIDTH), BF16), jax.ShapeDtypeStruct((s, HG_WIDTH), F32),
                   jax.ShapeDtypeStruct((s // HG_CHUNK * HG_DIM, HG_WIDTH), F32)],
        in_specs=[tile] * 4 + [_whole((2, HG_WIDTH)), _whole((1, HG_DIM))],
        out_specs=[tile, tile, _rows(HG_TILE_CHUNKS * HG_DIM, HG_WIDTH)],
        scratch_shapes=[pltpu.VMEM((HG_HEADS, HG_DIM, HG_DIM), F32)] + [pltpu.VMEM((HG_TILE, HG_WIDTH), F32)] * 3,
        compiler_params=_params("arbitrary"),
    )(hq, hf, hi, hgt, logits, onorm_g)


def _hg_bwd(hq, hf, hi, hgt, logits, onorm_g, o, states, dout):
    s = hq.shape[0]
    n_tiles = s // HG_TILE

    def body(q_ref, f_ref, i_ref, g_ref, lg_ref, og_ref, o_ref, st_ref, d_ref,
             dq_ref, df_ref, di_ref, dg_ref, dog_ref, dlb_ref, dstate, qf_s, kk_s, lf_s, do_s):
        @pl.when(pl.program_id(0) == 0)
        def _():
            dstate[...] = jnp.zeros_like(dstate)
            dog_ref[...] = jnp.zeros_like(dog_ref)
            dlb_ref[...] = jnp.zeros_like(dlb_ref)

        og = og_ref[...]
        dog = jnp.zeros((1, HG_DIM), F32)
        for h in range(HG_HEADS):
            c = slice(h * HG_DIM, (h + 1) * HG_DIM)
            ohat, rstd = _rms(o_ref[:, c])
            gate = g_ref[:, c]
            d = d_ref[:, c]
            dg_ref[:, c] = d * (ohat * og) * _dsilu(gate)
            dnormed = d * _silu(gate)
            dog += _rowsum(dnormed * ohat)
            do_s[:, c] = _rms_bwd(dnormed * og, ohat, rstd)
        dog_ref[...] += dog

        lb = _lower_bound(lg_ref)
        f = lb + (1.0 - lb) * _sigmoid(f_ref[...])
        kk_s[...] = 1.0 - f
        lf_s[...] = jnp.log(f)
        qf_s[...] = _silu(q_ref[...])
        causal, upper, tri, tri_t = _chunk_masks()

        def chunk(step, carry):
            ci = HG_TILE_CHUNKS - 1 - step
            rows = pl.ds(pl.multiple_of(ci * HG_CHUNK, HG_CHUNK), HG_CHUNK)
            srows = pl.ds(pl.multiple_of(ci * HG_DIM, HG_DIM), HG_DIM)
            lf = lf_s[rows, :]
            b = _dot_f32(tri, lf)
            bl = _rowsum(lf)
            ref = 0.5 * bl
            qf, kk, v, do = qf_s[rows, :], kk_s[rows, :], i_ref[rows, :], do_s[rows, :]
            eb, ebr, erb, ekd, ebl = jnp.exp(b), jnp.exp(b - ref), jnp.exp(ref - b), jnp.exp(bl - b), jnp.exp(bl)
            a_in, a_t, b_t, kd = qf * eb, qf * ebr, kk * erb, kk * ekd
            for h in range(HG_HEADS):
                c = slice(h * HG_DIM, (h + 1) * HG_DIM)
                st, dst = st_ref[srows, c], dstate[h]
                stb, dstb = st.astype(BF16), dst.astype(BF16)
                doh, vh = do[:, c], v[:, c]
                dob, vb = doh.astype(BF16), vh.astype(BF16)
                ain_h, at_h, bt_h, kd_h = a_in[:, c], a_t[:, c], b_t[:, c], kd[:, c]
                atb, btb = at_h.astype(BF16), bt_h.astype(BF16)
                d_ain = _dot(dob, stb)
                p_t = jnp.where(upper, _dot_nt(btb, atb), 0.0).astype(BF16)
                dp = jnp.where(causal, _dot_nt(dob, vb), 0.0).astype(BF16)
                dp_t = jnp.where(upper, _dot_nt(vb, dob), 0.0).astype(BF16)
                di_ref[rows, c] = _dot(p_t, dob) + _dot_nt(kd_h.astype(BF16), dstb)
                d_at = _dot(dp, btb)
                d_bt = _dot(dp_t, atb)
                d_kd = _dot(vb, dstb)
                dqf = d_ain * eb[:, c] + d_at * ebr[:, c]
                dkk = d_bt * erb[:, c] + d_kd * ekd[:, c]
                db = d_ain * ain_h + d_at * atb.astype(F32) - d_bt * btb.astype(F32) - d_kd * kd_h
                dbl = _rowsum(d_kd * kd_h) + _rowsum(dst * st) * ebl[:, c]
                dstate[h] = _dot_tn(dob, ain_h.astype(BF16)) + dst * ebl[:, c]
                dlf = _dot_f32(tri_t, db) + dbl
                qv, fr = q_ref[rows, c], f_ref[rows, c]
                lbh = lb[:, c]
                sg = _sigmoid(fr)
                dfv = dlf / (lbh + (1.0 - lbh) * sg) - dkk
                df_ref[rows, c] = dfv * (1.0 - lbh) * sg * (1.0 - sg)
                dlb_ref[:, c] += _rowsum(dfv * (1.0 - sg))
                dq_ref[rows, c] = dqf * _dsilu(qv)
            return carry

        lax.fori_loop(0, HG_TILE_CHUNKS, chunk, 0)

    rev = pl.BlockSpec((HG_TILE, HG_WIDTH), lambda i: (n_tiles - 1 - i, 0))
    return pl.pallas_call(
        body, name="hg_bwd", grid=(n_tiles,),
        out_shape=[jax.ShapeDtypeStruct((s, HG_WIDTH), F32)] * 4
        + [jax.ShapeDtypeStruct((1, HG_DIM), F32), jax.ShapeDtypeStruct((1, HG_WIDTH), F32)],
        in_specs=[rev] * 4 + [_whole((2, HG_WIDTH)), _whole((1, HG_DIM)), rev,
                              pl.BlockSpec((HG_TILE_CHUNKS * HG_DIM, HG_WIDTH), lambda i: (n_tiles - 1 - i, 0)), rev],
        out_specs=[rev] * 4 + [_whole((1, HG_DIM)), _whole((1, HG_WIDTH))],
        scratch_shapes=[pltpu.VMEM((HG_HEADS, HG_DIM, HG_DIM), F32)] + [pltpu.VMEM((HG_TILE, HG_WIDTH), F32)] * 4,
        compiler_params=_params("arbitrary"),
    )(hq, hf, hi, hgt, logits, onorm_g, o, states, dout)


def _att_consts():
    lane = lax.broadcasted_iota(jnp.int32, (ATT_BLOCK, 128), 1)
    qi = lax.broadcasted_iota(jnp.int32, (2 * ATT_BLOCK, ATT_BLOCK), 0) % ATT_BLOCK
    kj = lax.broadcasted_iota(jnp.int32, (2 * ATT_BLOCK, ATT_BLOCK), 1)
    return lane < ATT_HEAD_DIM, kj <= qi, lambda off: kj >= qi + off


def _stack_heads(x2, first):
    return jnp.concatenate([jnp.where(first, x2, 0.0), jnp.where(first, 0.0, x2)], axis=0)


def _stack_bcast(x2, first):
    other = pltpu.roll(x2, ATT_HEAD_DIM, axis=1)
    return jnp.concatenate([jnp.where(first, x2, other), jnp.where(first, other, x2)], axis=0)


def _unstack_heads(st, first):
    return jnp.where(first, st[:ATT_BLOCK], st[ATT_BLOCK:])


def _att_fwd(q, k, v, dil):
    m, width = q.shape
    nb = m // ATT_BLOCK

    def body(q_ref, kc_ref, kp_ref, vc_ref, vp_ref, o_ref, lse_ref):
        first, cur_ok, _band = _att_consts()
        prev_ok = _band(jnp.where(pl.program_id(1) > 0, 0, ATT_BLOCK))
        for j in range(ATT_WIDTH // 128):
            c = slice(j * 128, (j + 1) * 128)
            qst = _stack_heads(q_ref[:, c] * ATT_SCALE, first).astype(BF16)
            kc, kp = kc_ref[:, c].astype(BF16), kp_ref[:, c].astype(BF16)
            vc, vp = vc_ref[:, c].astype(BF16), vp_ref[:, c].astype(BF16)
            sc = jnp.where(cur_ok, _dot_nt(qst, kc), NEG)
            sp = jnp.where(prev_ok, _dot_nt(qst, kp), NEG)
            mx = jnp.maximum(jnp.max(sc, axis=-1, keepdims=True), jnp.max(sp, axis=-1, keepdims=True))
            pc, pp = jnp.exp(sc - mx), jnp.exp(sp - mx)
            den = jnp.sum(pc, axis=-1, keepdims=True) + jnp.sum(pp, axis=-1, keepdims=True)
            ost = (_dot(pc.astype(BF16), vc) + _dot(pp.astype(BF16), vp)) / den
            lse = jnp.broadcast_to(mx + jnp.log(den), (2 * ATT_BLOCK, 128))
            o_ref[:, c] = _unstack_heads(ost, first)
            lse_ref[:, c] = _unstack_heads(lse, first)

    cur = pl.BlockSpec((ATT_BLOCK, ATT_WIDTH), lambda r, n: (n, r))
    prev = pl.BlockSpec((ATT_BLOCK, ATT_WIDTH), lambda r, n: (jnp.maximum(n - 1, 0), r))
    return pl.pallas_call(
        body, name=f"att_fwd_d{dil}", grid=(dil, nb),
        out_shape=[jax.ShapeDtypeStruct((m, width), F32)] * 2,
        in_specs=[cur, cur, prev, cur, prev], out_specs=[cur, cur],
        compiler_params=_params("parallel", "arbitrary"),
    )(q, k, k, v, v)


def _att_bwd(q, k, v, do, cc, lse, dil):
    m, width = q.shape
    nb = m // ATT_BLOCK

    def body(q_ref, qx_ref, kc_ref, kp_ref, vc_ref, vp_ref, do_ref, dox_ref, cc_ref, ccx_ref, lse_ref, lsex_ref,
             dq_ref, dk_ref, dv_ref):
        first, cur_ok, _band = _att_consts()
        n = pl.program_id(1)
        prev_ok = _band(jnp.where(n > 0, 0, ATT_BLOCK))
        next_ok = _band(jnp.where(n < nb - 1, 0, ATT_BLOCK))
        for j in range(ATT_WIDTH // 128):
            c = slice(j * 128, (j + 1) * 128)
            qst = _stack_heads(q_ref[:, c] * ATT_SCALE, first).astype(BF16)
            qxst = _stack_heads(qx_ref[:, c] * ATT_SCALE, first).astype(BF16)
            dost = _stack_heads(do_ref[:, c], first).astype(BF16)
            doxst = _stack_heads(dox_ref[:, c], first).astype(BF16)
            lse_n, lse_x = _stack_bcast(lse_ref[:, c], first), _stack_bcast(lsex_ref[:, c], first)
            cc_n, cc_x = _stack_bcast(cc_ref[:, c], first), _stack_bcast(ccx_ref[:, c], first)
            kc, kp = kc_ref[:, c].astype(BF16), kp_ref[:, c].astype(BF16)
            vc, vp = vc_ref[:, c].astype(BF16), vp_ref[:, c].astype(BF16)
            p_cur = jnp.exp(jnp.where(cur_ok, _dot_nt(qst, kc), NEG) - lse_n)
            p_prev = jnp.exp(jnp.where(prev_ok, _dot_nt(qst, kp), NEG) - lse_n)
            p_next = jnp.exp(jnp.where(next_ok, _dot_nt(qxst, kc), NEG) - lse_x)
            ds_cur = (p_cur * (_dot_nt(dost, vc) + cc_n)).astype(BF16)
            ds_prev = (p_prev * (_dot_nt(dost, vp) + cc_n)).astype(BF16)
            ds_next = (p_next * (_dot_nt(doxst, vc) + cc_x)).astype(BF16)
            dq_ref[:, c] = _unstack_heads(_dot(ds_cur, kc) + _dot(ds_prev, kp), first) * ATT_SCALE
            dk_ref[:, c] = _dot_tn(ds_cur, qst) + _dot_tn(ds_next, qxst)
            dv_ref[:, c] = _dot_tn(p_cur.astype(BF16), dost) + _dot_tn(p_next.astype(BF16), doxst)

    cur = pl.BlockSpec((ATT_BLOCK, ATT_WIDTH), lambda r, n: (n, r))
    prev = pl.BlockSpec((ATT_BLOCK, ATT_WIDTH), lambda r, n: (jnp.maximum(n - 1, 0), r))
    nxt = pl.BlockSpec((ATT_BLOCK, ATT_WIDTH), lambda r, n: (jnp.minimum(n + 1, nb - 1), r))
    return pl.pallas_call(
        body, name=f"att_bwd_d{dil}", grid=(dil, nb),
        out_shape=[jax.ShapeDtypeStruct((m, width), F32)] * 3,
        in_specs=[cur, nxt, cur, prev, cur, prev, cur, nxt, cur, nxt, cur, nxt], out_specs=[cur, cur, cur],
        compiler_params=_params("parallel", "arbitrary"),
    )(q, q, k, k, v, v, do, do, cc, cc, lse, lse)


def _branch_weights(lses):
    mx = jnp.maximum(jnp.maximum(lses[0], lses[1]), lses[2])
    es = [jnp.exp(l - mx) for l in lses]
    inv = 1.0 / (es[0] + es[1] + es[2])
    return [e * inv for e in es]


def _att_combine(outs, lses, att_g):
    s = outs[0].shape[0]
    tm = 512

    def body(o0, o1, o2, l0, l1, l2, g_ref, att_ref, out_ref):
        ws = _branch_weights([l0[...], l1[...], l2[...]])
        att = ws[0] * o0[...] + ws[1] * o1[...] + ws[2] * o2[...]
        att_ref[...] = att
        ahat, _ = _rms(att)
        out_ref[...] = (ahat * g_ref[...]).astype(BF16)

    tile = _rows(tm, ATT_WIDTH)
    return pl.pallas_call(
        body, name="att_combine", grid=(s // tm,),
        out_shape=[jax.ShapeDtypeStruct((s, ATT_WIDTH), F32), jax.ShapeDtypeStruct((s, ATT_WIDTH), BF16)],
        in_specs=[tile] * 6 + [_whole((1, ATT_WIDTH))], out_specs=[tile, tile],
        compiler_params=_params("parallel"),
    )(*outs, *lses, att_g)


def _att_combine_bwd(datt_out, att, lses, att_g):
    s = att.shape[0]
    tm = 256

    def body(d_ref, att_ref, l0, l1, l2, g_ref, do0, do1, do2, cc0, cc1, cc2, dg_ref):
        @pl.when(pl.program_id(0) == 0)
        def _():
            dg_ref[...] = jnp.zeros_like(dg_ref)

        att = att_ref[...]
        ahat, rstd = _rms(att)
        d = d_ref[...]
        dg_ref[...] += _rowsum(d * ahat)
        datt = _rms_bwd(d * g_ref[...], ahat, rstd)
        hi = lax.broadcasted_iota(jnp.int32, (ATT_WIDTH, ATT_WIDTH), 0) // ATT_HEAD_DIM
        hj = lax.broadcasted_iota(jnp.int32, (ATT_WIDTH, ATT_WIDTH), 1) // ATT_HEAD_DIM
        head_sum = _dot_f32(datt * att, (hi == hj).astype(F32))
        ws = _branch_weights([l0[...], l1[...], l2[...]])
        for w, do_ref, cc_ref in zip(ws, (do0, do1, do2), (cc0, cc1, cc2)):
            do_ref[...] = (w * datt).astype(BF16)
            cc_ref[...] = -w * head_sum

    tile = _rows(tm, ATT_WIDTH)
    return pl.pallas_call(
        body, name="att_combine_bwd", grid=(s // tm,),
        out_shape=[jax.ShapeDtypeStruct((s, ATT_WIDTH), BF16)] * 3 + [jax.ShapeDtypeStruct((s, ATT_WIDTH), F32)] * 3
        + [jax.ShapeDtypeStruct((1, ATT_WIDTH), F32)],
        in_specs=[tile] * 5 + [_whole((1, ATT_WIDTH))], out_specs=[tile] * 6 + [_whole((1, ATT_WIDTH))],
        compiler_params=_params("arbitrary"),
    )(datt_out, att, *lses, att_g)


def _out_fwd(x, hg, at, mod, w_out):
    s = x.shape[0]
    tm = 512

    def body(x_ref, hg_ref, at_ref, mod_ref, w_ref, x1_ref):
        mix = _dot(hg_ref[...], w_ref[0:512, :]) + _dot(at_ref[...], w_ref[512:1024, :])
        x1_ref[...] = x_ref[...] + mod_ref[:, 2 * D_MODEL:3 * D_MODEL] * mix

    return pl.pallas_call(
        body, name="out_fwd", grid=(s // tm,), out_shape=jax.ShapeDtypeStruct((s, D_MODEL), F32),
        in_specs=[_rows(tm, D_MODEL), _rows(tm, 512), _rows(tm, 512), _whole((1, 6 * D_MODEL)), _whole((D_MODEL, D_MODEL))],
        out_specs=_rows(tm, D_MODEL), compiler_params=_params("parallel"),
    )(x, hg, at, mod, w_out)


def _out_bwd(dx1, hg, at, mod, w_out):
    s = dx1.shape[0]
    tm = 512

    def body(dx_ref, hg_ref, at_ref, mod_ref, w_ref, dhg_ref, dat_ref, dw_ref, dgate_ref):
        @pl.when(pl.program_id(0) == 0)
        def _():
            dw_ref[...] = jnp.zeros_like(dw_ref)
            dgate_ref[...] = jnp.zeros_like(dgate_ref)

        hg, at, dx = hg_ref[...], at_ref[...], dx_ref[...]
        mix = _dot(hg, w_ref[0:512, :]) + _dot(at, w_ref[512:1024, :])
        dgate_ref[...] += _rowsum(dx * mix)
        dmix = (mod_ref[:, 2 * D_MODEL:3 * D_MODEL] * dx).astype(BF16)
        dhg_ref[...] = _dot_nt(dmix, w_ref[0:512, :])
        dat_ref[...] = _dot_nt(dmix, w_ref[512:1024, :])
        dw_ref[0:512, :] += _dot_tn(hg, dmix)
        dw_ref[512:1024, :] += _dot_tn(at, dmix)

    return pl.pallas_call(
        body, name="out_bwd", grid=(s // tm,),
        out_shape=[jax.ShapeDtypeStruct((s, 512), F32)] * 2
        + [jax.ShapeDtypeStruct((D_MODEL, D_MODEL), F32), jax.ShapeDtypeStruct((1, D_MODEL), F32)],
        in_specs=[_rows(tm, D_MODEL), _rows(tm, 512), _rows(tm, 512), _whole((1, 6 * D_MODEL)), _whole((D_MODEL, D_MODEL))],
        out_specs=[_rows(tm, 512), _rows(tm, 512), _whole((D_MODEL, D_MODEL)), _whole((1, D_MODEL))],
        compiler_params=_params("arbitrary"),
    )(dx1, hg, at, mod, w_out)


FFN_CHUNK = 256


def _ffn(x1, target, mod, g2, gf, w_gu, w_down):
    s = x1.shape[0]
    tm = 256
    n_chunks = D_FF // FFN_CHUNK

    def body(x_ref, t_ref, mod_ref, g2_ref, gf_ref, wgu_hbm, wd_hbm,
             dx_ref, h2_ref, act_ref, dau_ref, dff_ref, sums_ref, loss_ref, wgu, wd, a_s, u_s, sem):
        @pl.when(pl.program_id(0) == 0)
        def _():
            c1 = pltpu.make_async_copy(wgu_hbm, wgu, sem.at[0])
            c2 = pltpu.make_async_copy(wd_hbm, wd, sem.at[1])
            c1.start()
            c2.start()
            c1.wait()
            c2.wait()
            sums_ref[...] = jnp.zeros_like(sums_ref)
            loss_ref[...] = jnp.zeros_like(loss_ref)

        x1v = x_ref[...]
        xhat, rstd = _rms(x1v)
        g2 = g2_ref[...]
        n2 = xhat * g2
        scale2 = 1.0 + mod_ref[:, 4 * D_MODEL:5 * D_MODEL]
        gate2 = mod_ref[:, 5 * D_MODEL:6 * D_MODEL]
        hb = (n2 * scale2 + mod_ref[:, 3 * D_MODEL:4 * D_MODEL]).astype(BF16)
        h2_ref[...] = hb
        ff = jnp.zeros((tm, D_MODEL), F32)
        for j in range(n_chunks):
            c = slice(j * FFN_CHUNK, (j + 1) * FFN_CHUNK)
            cu = slice(D_FF + j * FFN_CHUNK, D_FF + (j + 1) * FFN_CHUNK)
            a = _dot(hb, wgu[:, c])
            u = _dot(hb, wgu[:, cu])
            a_s[:, c] = a
            u_s[:, c] = u
            act = (_silu(a) * u).astype(BF16)
            act_ref[:, c] = act
            ff += _dot(act, wd[c, :])
        x2 = x1v + gate2 * ff
        nf, rstd_f = _rms(x2)
        gfv = gf_ref[...]
        err = nf * gfv - t_ref[...]
        loss_ref[...] += 0.5 * jnp.sum(_rowsum(err * err), axis=-1, keepdims=True) * (1.0 / D_MODEL)
        dy = err * (1.0 / D_MODEL)
        dx2 = _rms_bwd(dy * gfv, nf, rstd_f)
        dffb = (gate2 * dx2).astype(BF16)
        dff_ref[...] = dffb
        dh = jnp.zeros((tm, D_MODEL), F32)
        for j in range(n_chunks):
            c = slice(j * FFN_CHUNK, (j + 1) * FFN_CHUNK)
            cu = slice(D_FF + j * FFN_CHUNK, D_FF + (j + 1) * FFN_CHUNK)
            dact = _dot_nt(dffb, wd[c, :])
            a, u = a_s[:, c], u_s[:, c]
            da = (dact * u * _dsilu(a)).astype(BF16)
            du = (dact * _silu(a)).astype(BF16)
            dau_ref[:, c] = da
            dau_ref[:, cu] = du
            dh += _dot_nt(da, wgu[:, c]) + _dot_nt(du, wgu[:, cu])
        dn = dh * scale2
        sums_ref[0:1, :] += _rowsum(dh)
        sums_ref[1:2, :] += _rowsum(dh * n2)
        sums_ref[2:3, :] += _rowsum(dx2 * ff)
        sums_ref[3:4, :] += _rowsum(dn * xhat)
        sums_ref[4:5, :] += _rowsum(dy * nf)
        dx_ref[...] = dx2 + _rms_bwd(dn * g2, xhat, rstd)

    vec = _whole((1, D_MODEL))
    hbm = pl.BlockSpec(memory_space=pl.ANY)
    return pl.pallas_call(
        body, name="ffn", grid=(s // tm,),
        out_shape=[jax.ShapeDtypeStruct((s, D_MODEL), F32), jax.ShapeDtypeStruct((s, D_MODEL), BF16),
                   jax.ShapeDtypeStruct((s, D_FF), BF16), jax.ShapeDtypeStruct((s, 2 * D_FF), BF16),
                   jax.ShapeDtypeStruct((s, D_MODEL), BF16), jax.ShapeDtypeStruct((8, D_MODEL), F32),
                   jax.ShapeDtypeStruct((1, 128), F32)],
        in_specs=[_rows(tm, D_MODEL), _rows(tm, D_MODEL), _whole((1, 6 * D_MODEL)), vec, vec, hbm, hbm],
        out_specs=[_rows(tm, D_MODEL), _rows(tm, D_MODEL), _rows(tm, D_FF), _rows(tm, 2 * D_FF), _rows(tm, D_MODEL),
                   _whole((8, D_MODEL)), _whole((1, 128))],
        scratch_shapes=[pltpu.VMEM((D_MODEL, 2 * D_FF), BF16), pltpu.VMEM((D_FF, D_MODEL), BF16),
                        pltpu.VMEM((tm, D_FF), F32), pltpu.VMEM((tm, D_FF), F32), pltpu.SemaphoreType.DMA((2,))],
        compiler_params=_params("arbitrary"),
    )(x1, target, mod, g2, gf, w_gu, w_down)


def _weight_grad(a, b, name):
    s, m = a.shape
    n = b.shape[1]
    ts, tn = min(s, 2048), 512

    def body(a_ref, b_ref, o_ref):
        @pl.when(pl.program_id(1) == 0)
        def _():
            o_ref[...] = jnp.zeros_like(o_ref)

        o_ref[...] += _dot_tn(a_ref[...], b_ref[...])

    return pl.pallas_call(
        body, name=name, grid=(n // tn, s // ts), out_shape=jax.ShapeDtypeStruct((m, n), F32),
        in_specs=[pl.BlockSpec((ts, m), lambda j, i: (i, 0)), pl.BlockSpec((ts, tn), lambda j, i: (i, j))],
        out_specs=pl.BlockSpec((m, tn), lambda j, i: (0, j)),
        compiler_params=_params("parallel", "arbitrary"),
    )(a, b)


def _adamw_math(w, g, m, v):
    m = ADAM_B1 * m + (1.0 - ADAM_B1) * g
    v = ADAM_B2 * v + (1.0 - ADAM_B2) * (g * g)
    m_hat = m / (1.0 - ADAM_B1 ** ADAM_STEP)
    v_hat = v / (1.0 - ADAM_B2 ** ADAM_STEP)
    delta = -ADAM_LR * (m_hat / (jnp.sqrt(v_hat) + ADAM_EPS) + ADAM_WD * w)
    return delta, m, v


def _adamw_shard(chip, w, m, v, partial, got, name):
    r, c = w.shape
    tr = _shard_rows(r)

    def body(chip_ref, w_ref, m_ref, v_ref, own_ref, g0, g1, g2, grad_ref, d_ref, nm_ref, nv_ref):
        g = ((own_ref[...] + g0[...].astype(F32)) + g1[...].astype(F32)) + g2[...].astype(F32)
        grad_ref[...] = g
        d_ref[...], nm_ref[...], nv_ref[...] = _adamw_math(w_ref[...], g, m_ref[...], v_ref[...])

    tile = pl.BlockSpec((tr, c), lambda i, chip_ref: (i, 0))
    own = pl.BlockSpec((None, tr, c), lambda i, chip_ref: (chip_ref[0], i, 0))
    part = [pl.BlockSpec((None, tr, c), functools.partial(lambda j, i, chip_ref: (j, i, 0), j)) for j in range(3)]
    return pl.pallas_call(
        body, name=name,
        grid_spec=pltpu.PrefetchScalarGridSpec(num_scalar_prefetch=1, grid=(r // tr,), in_specs=[tile] * 3 + [own] + part,
                                               out_specs=[tile] * 4),
        out_shape=[jax.ShapeDtypeStruct((r, c), F32)] * 4, compiler_params=_params("parallel"),
    )(chip, w, m, v, partial, got, got, got)


def _small_update(small_all, dmod_blocks, c_all, logits, w_ada, m_ada, v_ada, smalls):
    def body(sm_ref, dm_ref, c_ref, lg_ref, wa_ref, ma_ref, va_ref, *rest):
        ins, outs = rest[:21], rest[21:]
        _, me = _flip(0)
        tot = sm_ref[0:1, :]
        for i in range(1, N_DEV):
            tot = tot + sm_ref[i:i + 1, :]
        loss_ref = outs[0]
        loss_ref[...] = tot[:, SM_LOSS:SM_LOSS + 128]
        g_ada = lax.dot_general(_silu(c_ref[...]), dm_ref[me], (((0,), (0,)), ((), ())),
                                preferred_element_type=F32, precision=HIGHEST)
        outs[1][...] = g_ada
        outs[2][...], outs[3][...], outs[4][...] = _adamw_math(wa_ref[...], g_ada, ma_ref[...], va_ref[...])
        p0 = _lower_bound(lg_ref)
        dl0 = tot[:, SM_LB:SM_LB + 512] * p0 * (1.0 - p0)
        grads = [tot[:, SM_MOD:SM_MOD + 6 * D_MODEL], tot[:, SM_G1:SM_G1 + D_MODEL], tot[:, SM_G2:SM_G2 + D_MODEL],
                 tot[:, SM_GF:SM_GF + D_MODEL], tot[:, SM_ATT:SM_ATT + 512], tot[:, SM_HG:SM_HG + 128],
                 jnp.where(lax.broadcasted_iota(jnp.int32, (2, 512), 0) == 0, dl0, -dl0)]
        for i, g in enumerate(grads):
            w_ref, m_ref, v_ref = ins[3 * i:3 * i + 3]
            o = outs[5 + 4 * i:9 + 4 * i]
            o[0][...] = g
            o[1][...], o[2][...], o[3][...] = _adamw_math(w_ref[...], g, m_ref[...], v_ref[...])

    flat = [t for trio in smalls for t in trio]
    vm = pl.BlockSpec(memory_space=pltpu.VMEM)
    out_shape = [jax.ShapeDtypeStruct((1, 128), F32)] + [jax.ShapeDtypeStruct(w_ada.shape, F32)] * 4
    for trio in smalls:
        out_shape += [jax.ShapeDtypeStruct(trio[0].shape, F32)] * 4
    return pl.pallas_call(
        body, name="small_update", out_shape=out_shape,
        in_specs=[vm] * (7 + len(flat)), out_specs=[vm] * len(out_shape),
        compiler_params=pltpu.CompilerParams(vmem_limit_bytes=V7X_VMEM_LIMIT),
    )(small_all, dmod_blocks, c_all, logits, w_ada, m_ada, v_ada, *flat)


def _dilate(t, d):
    return t if d == 1 else t.reshape(t.shape[0] // d, d * t.shape[1])


def _undilate(t, d):
    return t if d == 1 else t.reshape(t.shape[0] * d, t.shape[1] // d)


def kernel(x, c, w_ada, b_ada, norm1_g, w_in, hg_lb_logits, hg_onorm_g, att_onorm_g, w_out, norm2_g, w_gate_up, w_down, final_g, loss_target, m_w_ada, m_b_ada, m_norm1_g, m_w_in, m_hg_lb_logits, m_hg_onorm_g, m_att_onorm_g, m_w_out, m_norm2_g, m_w_gate_up, m_w_down, m_final_g, v_w_ada, v_b_ada, v_norm1_g, v_w_in, v_hg_lb_logits, v_hg_onorm_g, v_att_onorm_g, v_w_out, v_norm2_g, v_w_gate_up, v_w_down, v_final_g):
    x2d, target = x[0], loss_target[0]
    seq = x2d.shape[0]
    assert seq % (ATT_BLOCK * max(DILATIONS)) == 0 and seq % HG_TILE == 0
    gf = final_g.reshape(1, D_MODEL)

    c_all = _exchange_small(jnp.broadcast_to(c, (8, D_MODEL)), None, "gather_c")[:, 0, :]
    ada = _ada_rows(c_all, w_ada[0], b_ada)
    mod = _exchange_small(ada, 1, "scatter_mod").reshape(1, 6 * D_MODEL)

    core = lax.axis_index("c").astype(jnp.int32).reshape(1)
    chip = (2 * lax.axis_index("x") + lax.axis_index("y")).astype(jnp.int32).reshape(1)
    me = 4 * lax.axis_index("x") + 2 * lax.axis_index("y") + lax.axis_index("c")

    g_in, = _gather_weights([w_in[0].astype(BF16)])
    w_in_b = jnp.transpose(g_in, (1, 0, 2)).reshape(D_MODEL, IN_WIDTH)
    rest_shards = [w_out[0].astype(BF16), w_gate_up[0].astype(BF16), w_down[0].astype(BF16)]
    lands = [lax.empty((N_DEV,) + s.shape, BF16) for s in rest_shards]
    g_send, g_recv, g_srcs, g_lands, tok = _copies_start("gather_rest_start", _plan_gather_own, 12, rest_shards, lands)
    flight = {}

    def stage(name, *vals):
        if name == "attention_begun":
            got = _copies_wait("gather_rest_wait", _plan_gather_own, g_send, g_recv, g_srcs, g_lands, vals[0])
            flight["pass"] = _copies_start("gather_pass_start", _plan_gather_pass, 9, [], got)
            return flight["pass"][4][0:1, 0:1]
        if name == "mixer_weights_done":
            dw_gu, dw_down, dw_out = vals
            flight["grads"] = [dw_out.reshape(4, 2, D_MODEL // N_DEV, D_MODEL),
                               dw_gu.reshape(D_MODEL, 4, 2, 2 * D_FF // N_DEV).transpose(1, 2, 0, 3),
                               dw_down.reshape(4, 2, D_FF // N_DEV, D_MODEL)]
            pair_lands = [lax.empty((4,) + g.shape[2:], F32) for g in flight["grads"]]
            flight["pairs"] = _copies_start("reduce_pairs_start", _plan_reduce_pairs, 12, flight["grads"], pair_lands)
            return flight["pairs"][4][0:1, 0:1]
        if name == "attention_backward_begun":
            s, r, srcs, pl_lands, _ = flight["pairs"]
            got = _copies_wait("reduce_pairs_wait", _plan_reduce_pairs, s, r, srcs, pl_lands, vals[0])
            flight["sums"] = [_pair_sum(core, g, b, f"pair_sum_{i}") for i, (g, b) in enumerate(zip(flight["grads"], got))]
            chip_lands = [lax.empty((3,) + s16.shape[1:], BF16) for _, s16 in flight["sums"]]
            flight["chips"] = _copies_start("reduce_chips_start", _plan_reduce_chips, 9, [s16 for _, s16 in flight["sums"]], chip_lands)
            return flight["chips"][4][0:1, 0:1]
        raise ValueError(name)

    def rest_weights(after):
        s, r, _, p_lands, _ = flight["pass"]
        got = _copies_wait("gather_pass_wait", _plan_gather_pass, s, r, [], p_lands, after)
        full = [lax.dynamic_update_index_in_dim(g, shard, me, 0) for g, shard in zip(got, rest_shards)]
        return (full[0].reshape(D_MODEL, D_MODEL), jnp.transpose(full[1], (1, 0, 2)).reshape(D_MODEL, 2 * D_FF),
                full[2].reshape(D_FF, D_MODEL))

    grad_x, dw_in, small = _block_step(x2d, target, mod + tok[0:1, 0:1], norm1_g, hg_lb_logits, hg_onorm_g, att_onorm_g, norm2_g, gf,
                                       w_in_b, rest_weights, stage)

    s, r, srcs, c_lands, _ = flight["chips"]
    recv_rest = _copies_wait("reduce_chips_wait", _plan_reduce_chips, s, r, srcs, c_lands, dw_in)
    g_in8 = [dw_in.reshape(D_MODEL, 4, 2, IN_WIDTH // N_DEV).transpose(1, 2, 0, 3)]
    in_sums = [_pair_sum(core, g_in8[0], _reduce_pairs(g_in8)[0], "pair_sum_in")]
    recv_in = _reduce_chips([in_sums[0][1]])
    sums = [in_sums[0]] + flight["sums"]
    recv = [recv_in[0]] + recv_rest
    big = {}
    for i, (n, w, m, v) in enumerate([("w_in", w_in, m_w_in, v_w_in), ("w_out", w_out, m_w_out, v_w_out),
                                      ("w_gate_up", w_gate_up, m_w_gate_up, v_w_gate_up), ("w_down", w_down, m_w_down, v_w_down)]):
        big[n] = [t[None] for t in _adamw_shard(chip, w[0], m[0], v[0], sums[i][0], recv[i], f"adamw_{n}")]

    small_all = _exchange_small(jnp.broadcast_to(small, (8, SM_WIDTH)), None, "gather_small")[:, 0, :]
    smalls = [(b_ada, m_b_ada, v_b_ada), (norm1_g, m_norm1_g, v_norm1_g), (norm2_g, m_norm2_g, v_norm2_g),
              (gf, m_final_g.reshape(1, D_MODEL), v_final_g.reshape(1, D_MODEL)),
              (att_onorm_g, m_att_onorm_g, v_att_onorm_g), (hg_onorm_g, m_hg_onorm_g, v_hg_onorm_g),
              (hg_lb_logits, m_hg_lb_logits, v_hg_lb_logits)]
    dmod_blocks = small_all[:, :6 * D_MODEL].reshape(N_DEV, N_DEV, 6 * D_MODEL // N_DEV).transpose(1, 0, 2)
    res = _small_update(small_all, dmod_blocks, c_all, hg_lb_logits, w_ada[0], m_w_ada[0], v_w_ada[0], smalls)
    loss = res[0][0, 0]
    ada4 = [t[None] for t in res[1:5]]
    sm4 = {n: list(res[5 + 4 * i:9 + 4 * i]) for i, n in enumerate(["b_ada", "norm1_g", "norm2_g", "final_g", "att", "hg", "lb"])}
    sm4["final_g"] = [t.reshape(D_MODEL) for t in sm4["final_g"]]

    order = [ada4, sm4["b_ada"], sm4["norm1_g"], big["w_in"], sm4["lb"], sm4["hg"], sm4["att"], big["w_out"], sm4["norm2_g"],
             big["w_gate_up"], big["w_down"], sm4["final_g"]]
    return (loss, grad_x[None], *[o[0] for o in order], *[o[1] for o in order], *[o[2] for o in order], *[o[3] for o in order])


def _block_step(x2d, target, mod, norm1_g, hg_lb_logits, hg_onorm_g, att_onorm_g, norm2_g, gf, w_in_b, rest_weights, stage):
    h1, hq, hf, hi, hgt, aq, ak, av = _in_fwd(x2d, mod, norm1_g, w_in_b)
    hg_out, hg_o, hg_states = _hg_fwd(hq, hf, hi, hgt, hg_lb_logits, hg_onorm_g)
    views = [(_dilate(aq, d), _dilate(ak, d), _dilate(av, d)) for d in DILATIONS]
    branch = [_att_fwd(*views[0], DILATIONS[0])]
    att_g = att_onorm_g + stage("attention_begun", branch[0][0])
    branch += [_att_fwd(*views[i], d) for i, d in enumerate(DILATIONS) if i > 0]
    outs = [_undilate(branch[i][0], d) for i, d in enumerate(DILATIONS)]
    lses = [_undilate(branch[i][1], d) for i, d in enumerate(DILATIONS)]
    att, att_out = _att_combine(outs, lses, att_g)
    w_out_b, w_gu_b, w_down_b = rest_weights(att_out)
    x1 = _out_fwd(x2d, hg_out, att_out, mod, w_out_b)

    dx1, h2, act, dau, dff, ffn_sums, loss_part = _ffn(x1, target, mod, norm2_g, gf, w_gu_b, w_down_b)
    dw_gu = _weight_grad(h2, dau, "dw_gate_up")
    dw_down = _weight_grad(act, dff, "dw_down")

    dhg, dat, dw_out, dgate1 = _out_bwd(dx1, hg_out, att_out, mod, w_out_b)
    att_g = att_onorm_g + stage("mixer_weights_done", dw_gu, dw_down, dw_out)
    comb = _att_combine_bwd(dat, att, lses, att_g)
    dos, ccs, d_att_g = comb[0:3], comb[3:6], comb[6]
    hg_g = hg_onorm_g + stage("attention_backward_begun", comb[3])
    datt = []
    for i, d in enumerate(DILATIONS):
        grads = _att_bwd(*views[i], _dilate(dos[i], d), _dilate(ccs[i], d), branch[i][1], d)
        datt.append([_undilate(g, d) for g in grads])
    dhq, dhf, dhi, dhgt, d_hg_g, d_lb = _hg_bwd(hq, hf, hi, hgt, hg_lb_logits, hg_g, hg_o, hg_states, dhg)
    dps = [dhq, dhf, dhi, dhgt] + [datt[i][j] for j in range(3) for i in range(3)]
    grad_x, dp_b, dshift1, dscale1, d_g1 = _in_bwd(x2d, dx1, mod, norm1_g, w_in_b, dps)
    dw_in = _weight_grad(h1, dp_b, "dw_in")
    small = jnp.concatenate([dshift1, dscale1, dgate1, ffn_sums[0:1], ffn_sums[1:2], ffn_sums[2:3], d_g1, ffn_sums[3:4],
                             ffn_sums[4:5], d_att_g, d_lb, d_hg_g, loss_part], axis=1)
    return grad_x, dw_in, small
```

```python
import functools

import jax
import jax.numpy as jnp
from jax import lax
from jax.experimental import pallas as pl
from jax.experimental.pallas import tpu as pltpu

F32 = jnp.float32
BF16 = jnp.bfloat16
HIGHEST = lax.Precision.HIGHEST
MESH = pl.DeviceIdType.MESH

D_MODEL = 1024
N_DEV = 8
HG_HEADS = 4
HG_DIM = 128
HG_WIDTH = HG_HEADS * HG_DIM
HG_CHUNK = 64
ATT_WIDTH = 512
ATT_HEAD_DIM = 64
ATT_BLOCK = 128
DILATIONS = (1, 4, 16)
ATT_SCALE = ATT_HEAD_DIM ** -0.5
D_FF = 2816
IN_WIDTH = 7 * 512
RMS_EPS = 1e-6
NEG = -1e30

ADAM_LR = 0.001
ADAM_B1 = 0.9
ADAM_B2 = 0.999
ADAM_EPS = 1e-08
ADAM_WD = 0.01
ADAM_STEP = 10

V7X_VMEM_LIMIT = 56 * 1024 * 1024

SM_MOD = 0
SM_G1 = 6 * D_MODEL
SM_G2 = 7 * D_MODEL
SM_GF = 8 * D_MODEL
SM_ATT = 9 * D_MODEL
SM_LB = 9 * D_MODEL + 512
SM_HG = 10 * D_MODEL
SM_LOSS = 10 * D_MODEL + 128
SM_WIDTH = 10 * D_MODEL + 256
SM_PADDED = 88 * 128


def _params(*sem, vmem=V7X_VMEM_LIMIT):
    return pltpu.CompilerParams(dimension_semantics=sem, vmem_limit_bytes=vmem)


def _dot(a, b):
    return jnp.dot(a, b, preferred_element_type=F32)


def _dot_nt(a, b):
    return lax.dot_general(a, b, (((1,), (1,)), ((), ())), preferred_element_type=F32)


def _dot_tn(a, b):
    return lax.dot_general(a, b, (((0,), (0,)), ((), ())), preferred_element_type=F32)


def _dot_f32(a, b):
    return jnp.dot(a, b, preferred_element_type=F32, precision=HIGHEST)


def _sigmoid(x):
    return 1.0 / (1.0 + jnp.exp(-x))


def _silu(x):
    return x * _sigmoid(x)


def _dsilu(x):
    s = _sigmoid(x)
    return s * (1.0 + x * (1.0 - s))


def _rms(x):
    rstd = lax.rsqrt(jnp.mean(x * x, axis=-1, keepdims=True) + RMS_EPS)
    return x * rstd, rstd


def _rms_bwd(dn, xhat, rstd):
    return rstd * (dn - xhat * jnp.mean(dn * xhat, axis=-1, keepdims=True))


def _rowsum(x):
    return jnp.sum(x, axis=0, keepdims=True)


def _rows(tm, n):
    return pl.BlockSpec((tm, n), lambda i: (i, 0))


def _whole(shape):
    return pl.BlockSpec(shape, lambda i: (0,) * len(shape))


def _mesh_pos():
    return lax.axis_index("x"), lax.axis_index("y"), lax.axis_index("c")


def _flip(k):
    x, y, c = _mesh_pos()
    px = 1 - x if k & 4 else x
    py = 1 - y if k & 2 else y
    pc = 1 - c if k & 1 else c
    return (px, py, pc), 4 * px + 2 * py + pc


def _exchange_small(x, rows_per_peer, name):
    r_all, cols = x.shape
    r_out = r_all if rows_per_peer is None else rows_per_peer

    def body(x_ref, out_ref, send_sems, recv_sems):
        _, me = _flip(0)

        def src(pid):
            if rows_per_peer is None:
                return x_ref
            return x_ref.at[pl.ds(pl.multiple_of(pid * r_out, r_out), r_out), :]

        if rows_per_peer is None:
            out_ref[me] = x_ref[...]
        else:
            out_ref[me] = x_ref[pl.ds(pl.multiple_of(me * r_out, r_out), r_out), :]
        sends = []
        for k in range(1, N_DEV):
            dev, pid = _flip(k)
            cp = pltpu.make_async_remote_copy(src_ref=src(pid), dst_ref=out_ref.at[me], send_sem=send_sems.at[k - 1],
                                              recv_sem=recv_sems.at[k - 1], device_id=dev, device_id_type=MESH)
            cp.start()
            sends.append(cp)
        for k in range(1, N_DEV):
            dev, pid = _flip(k)
            pltpu.make_async_remote_copy(src_ref=src(pid), dst_ref=out_ref.at[pid], send_sem=send_sems.at[k - 1],
                                         recv_sem=recv_sems.at[k - 1], device_id=dev, device_id_type=MESH).wait_recv()
        for cp in sends:
            cp.wait_send()

    return pl.pallas_call(
        body, name=name,
        out_shape=jax.ShapeDtypeStruct((N_DEV, r_out, cols), x.dtype),
        in_specs=[pl.BlockSpec(memory_space=pltpu.VMEM)],
        out_specs=pl.BlockSpec(memory_space=pltpu.VMEM),
        scratch_shapes=[pltpu.SemaphoreType.DMA((N_DEV - 1,)), pltpu.SemaphoreType.DMA((N_DEV - 1,))],
    )(x)


def _gather_weights(shards):
    n = len(shards)

    def body(*refs):
        xs, outs = refs[:n], refs[n:2 * n]
        send_sems, recv_sems, local_sems = refs[2 * n:]
        x, y, c = _mesh_pos()
        me, sibling = (x, y, c), (x, y, 1 - c)
        chips = [(1 - x, y), (x, 1 - y), (1 - x, 1 - y)]

        def blk(a, px, py, pc):
            return outs[a].at[4 * px + 2 * py + pc]

        def copy(a, k, block, to, src=None):
            return pltpu.make_async_remote_copy(
                src_ref=blk(a, *block) if src is None else src, dst_ref=blk(a, *block),
                send_sem=send_sems.at[a * 7 + k], recv_sem=recv_sems.at[a * 7 + k], device_id=to, device_id_type=MESH)

        mine = [pltpu.make_async_copy(xs[a], blk(a, *me), local_sems.at[a]) for a in range(n)]
        for cp in mine:
            cp.start()
        first = []
        for a in range(n):
            first.append(copy(a, 0, me, sibling, src=xs[a]))
            first += [copy(a, 1 + j, me, (*chip, c), src=xs[a]) for j, chip in enumerate(chips)]
        for cp in first:
            cp.start()
        passed = []
        for j, chip in enumerate(chips):
            for a in range(n):
                copy(a, 1 + j, (*chip, c), me).wait_recv()
                cp = copy(a, 4 + j, (*chip, c), sibling)
                cp.start()
                passed.append(cp)
        for a in range(n):
            copy(a, 0, sibling, me).wait_recv()
            for j, chip in enumerate(chips):
                copy(a, 4 + j, (*chip, 1 - c), me).wait_recv()
        for cp in first + passed:
            cp.wait_send()
        for cp in mine:
            cp.wait()

    hbm = pl.BlockSpec(memory_space=pl.ANY)
    return pl.pallas_call(
        body, name="gather_weights",
        out_shape=[jax.ShapeDtypeStruct((N_DEV,) + s.shape, s.dtype) for s in shards],
        in_specs=[hbm] * n, out_specs=[hbm] * n,
        scratch_shapes=[pltpu.SemaphoreType.DMA((7 * n,)), pltpu.SemaphoreType.DMA((7 * n,)), pltpu.SemaphoreType.DMA((n,))],
    )(*shards)


_HBM = pl.BlockSpec(memory_space=pltpu.HBM)
_SEM = pl.BlockSpec(memory_space=pltpu.SEMAPHORE)
_DATAFLOW = pltpu.SideEffectType.DATAFLOW_SIDE_EFFECTING


def _copies_start(name, plan, n_copies, srcs, lands, after):
    bufs = list(srcs) + list(lands)
    nb = len(bufs)

    def body(*refs):
        ins, send_sems, recv_sems, token = refs[:nb], refs[nb + len(after)], refs[nb + len(after) + 1], refs[-1]
        for i, (src, dst, dev) in enumerate(plan(ins[:len(srcs)], ins[len(srcs):])):
            pltpu.make_async_remote_copy(src_ref=src, dst_ref=dst, send_sem=send_sems.at[i], recv_sem=recv_sems.at[i],
                                         device_id=dev, device_id_type=MESH).start()
        token[...] = jnp.zeros_like(token)

    outs = pl.pallas_call(
        body, name=name,
        out_shape=(pltpu.SemaphoreType.DMA((n_copies,)), pltpu.SemaphoreType.DMA((n_copies,)),
                   *[pltpu.HBM(b.shape, b.dtype) for b in bufs], jax.ShapeDtypeStruct((8, 128), F32)),
        in_specs=[_HBM] * nb + [pl.BlockSpec(memory_space=pl.ANY)] * len(after),
        out_specs=(_SEM, _SEM, *[_HBM] * nb, pl.BlockSpec(memory_space=pltpu.VMEM)),
        input_output_aliases={i: 2 + i for i in range(nb)},
        compiler_params=pltpu.CompilerParams(has_side_effects=_DATAFLOW),
    )(*[pltpu.with_memory_space_constraint(b, pltpu.HBM) for b in bufs], *after)
    return outs[0], outs[1], list(outs[2:2 + len(srcs)]), list(outs[2 + len(srcs):2 + nb]), outs[-1]


def _copies_wait(name, plan, send_sems, recv_sems, srcs, lands, after):
    bufs = list(srcs) + list(lands)
    nb = len(bufs)

    def body(*refs):
        ins, send_ref, recv_ref = refs[:nb], refs[nb], refs[nb + 1]
        for i, (src, dst, dev) in enumerate(plan(ins[:len(srcs)], ins[len(srcs):])):
            cp = pltpu.make_async_remote_copy(src_ref=src, dst_ref=dst, send_sem=send_ref.at[i], recv_sem=recv_ref.at[i],
                                              device_id=dev, device_id_type=MESH)
            cp.wait_send()
            cp.wait_recv()

    outs = pl.pallas_call(
        body, name=name, out_shape=[pltpu.HBM(b.shape, b.dtype) for b in bufs],
        in_specs=[_HBM] * nb + [_SEM, _SEM] + [pl.BlockSpec(memory_space=pl.ANY)] * len(after), out_specs=[_HBM] * nb,
        input_output_aliases={i: i for i in range(nb)},
        compiler_params=pltpu.CompilerParams(has_side_effects=_DATAFLOW),
    )(*bufs, send_sems, recv_sems, *after)
    return list(outs[len(srcs):])


def _plan_gather_own(srcs, lands):
    _, me = _flip(0)
    return [(srcs[a], lands[a].at[me], _flip(k)[0]) for a in range(len(srcs)) for k in (1, 4, 2, 6)]


def _plan_gather_pass(srcs, lands):
    sibling = _flip(1)[0]
    plan = []
    for land in lands:
        for k in (4, 2, 6):
            block = land.at[_flip(k)[1]]
            plan.append((block, block, sibling))
    return plan


def _plan_reduce_pairs(srcs, lands):
    x, y, c = _mesh_pos()
    return [(srcs[a].at[chip, 1 - c], lands[a].at[chip], (x, y, 1 - c)) for a in range(len(srcs)) for chip in range(4)]


def _plan_reduce_chips(srcs, lands):
    plan = []
    for a in range(len(srcs)):
        for j, k in enumerate((4, 2, 6)):
            dev = _flip(k)[0]
            plan.append((srcs[a].at[2 * dev[0] + dev[1]], lands[a].at[j], dev))
    return plan


def _shard_rows(r):
    return r // 2 if r % 32 == 0 else r


def _pair_sum(core, grads, got, name):
    _, _, r, c = grads.shape
    tr = _shard_rows(r)

    def body(core_ref, a_ref, b_ref, o_ref, ob_ref):
        s = a_ref[...] + b_ref[...]
        o_ref[...] = s
        ob_ref[...] = s.astype(BF16)

    spec = pl.BlockSpec((None, tr, c), lambda i, j, core_ref: (i, j, 0))
    return pl.pallas_call(
        body, name=name,
        grid_spec=pltpu.PrefetchScalarGridSpec(
            num_scalar_prefetch=1, grid=(4, r // tr),
            in_specs=[pl.BlockSpec((None, None, tr, c), lambda i, j, core_ref: (i, core_ref[0], j, 0)), spec],
            out_specs=[spec, spec]),
        out_shape=[jax.ShapeDtypeStruct((4, r, c), F32), jax.ShapeDtypeStruct((4, r, c), BF16)],
        compiler_params=_params("parallel", "parallel"),
    )(core, grads, got)


def _ada_rows(c_all, w_ada, b_ada):
    n_cols = w_ada.shape[1]

    def body(c_ref, w_ref, b_ref, o_ref):
        _, me = _flip(0)
        bias = b_ref[:, pl.ds(pl.multiple_of(me * n_cols, 128), n_cols)]
        o_ref[...] = _dot_f32(_silu(c_ref[...]), w_ref[...]) + bias

    return pl.pallas_call(
        body, name="ada_rows", out_shape=jax.ShapeDtypeStruct((N_DEV, n_cols), F32),
        in_specs=[pl.BlockSpec(memory_space=pltpu.VMEM)] * 3, out_specs=pl.BlockSpec(memory_space=pltpu.VMEM),
    )(c_all, w_ada, b_ada)


def _in_fwd(x, mod, g1, w_in):
    s = x.shape[0]
    tm = 256

    def body(x_ref, mod_ref, g_ref, w_ref, h_ref, *outs):
        xhat, _ = _rms(x_ref[...])
        h = (xhat * g_ref[...]) * (1.0 + mod_ref[:, D_MODEL:2 * D_MODEL]) + mod_ref[:, 0:D_MODEL]
        hb = h.astype(BF16)
        h_ref[...] = hb
        for j, o_ref in enumerate(outs):
            o_ref[...] = _dot(hb, w_ref[:, j * 512:(j + 1) * 512]).astype(o_ref.dtype)

    return pl.pallas_call(
        body, name="in_fwd", grid=(s // tm,),
        out_shape=[jax.ShapeDtypeStruct((s, D_MODEL), BF16)] + [jax.ShapeDtypeStruct((s, 512), F32)] * 4
        + [jax.ShapeDtypeStruct((s, 512), BF16)] * 3,
        in_specs=[_rows(tm, D_MODEL), _whole((1, 6 * D_MODEL)), _whole((1, D_MODEL)), _whole((D_MODEL, IN_WIDTH))],
        out_specs=[_rows(tm, D_MODEL)] + [_rows(tm, 512)] * 7,
        compiler_params=_params("parallel"),
    )(x, mod, g1, w_in)


def _in_bwd(x, dx1, mod, g1, w_in, dps):
    s = x.shape[0]
    tm = 256

    def body(x_ref, dx_ref, mod_ref, g_ref, w_ref, *rest):
        dp_refs, (gx_ref, dpb_ref, dsh_ref, dsc_ref, dg_ref) = rest[:13], rest[13:]
        pieces = [dp_refs[j][...] for j in range(4)]
        pieces += [dp_refs[4 + 3 * j][...] + dp_refs[5 + 3 * j][...] + dp_refs[6 + 3 * j][...] for j in range(3)]
        dh = jnp.zeros((tm, D_MODEL), F32)
        for j, p in enumerate(pieces):
            pb = p.astype(BF16)
            dpb_ref[:, j * 512:(j + 1) * 512] = pb
            dh += _dot_nt(pb, w_ref[:, j * 512:(j + 1) * 512])
        xhat, rstd = _rms(x_ref[...])
        g = g_ref[...]
        scale1 = 1.0 + mod_ref[:, D_MODEL:2 * D_MODEL]
        n1 = xhat * g

        @pl.when(pl.program_id(0) == 0)
        def _():
            dsh_ref[...] = jnp.zeros_like(dsh_ref)
            dsc_ref[...] = jnp.zeros_like(dsc_ref)
            dg_ref[...] = jnp.zeros_like(dg_ref)

        dsh_ref[...] += _rowsum(dh)
        dsc_ref[...] += _rowsum(dh * n1)
        dn = dh * scale1
        dg_ref[...] += _rowsum(dn * xhat)
        gx_ref[...] = dx_ref[...] + _rms_bwd(dn * g, xhat, rstd)

    vec = _whole((1, D_MODEL))
    return pl.pallas_call(
        body, name="in_bwd", grid=(s // tm,),
        out_shape=[jax.ShapeDtypeStruct((s, D_MODEL), F32), jax.ShapeDtypeStruct((s, IN_WIDTH), BF16)]
        + [jax.ShapeDtypeStruct((1, D_MODEL), F32)] * 3,
        in_specs=[_rows(tm, D_MODEL), _rows(tm, D_MODEL), _whole((1, 6 * D_MODEL)), vec, _whole((D_MODEL, IN_WIDTH))]
        + [_rows(tm, 512)] * 13,
        out_specs=[_rows(tm, D_MODEL), _rows(tm, IN_WIDTH), vec, vec, vec],
        compiler_params=_params("arbitrary"),
    )(x, dx1, mod, g1, w_in, *dps)


HG_TILE = 512
HG_TILE_CHUNKS = HG_TILE // HG_CHUNK


def _lower_bound(lg_ref):
    return 1.0 / (1.0 + jnp.exp(lg_ref[1:2, :] - lg_ref[0:1, :]))


def _chunk_masks():
    r = lax.broadcasted_iota(jnp.int32, (HG_CHUNK, HG_CHUNK), 0)
    c = lax.broadcasted_iota(jnp.int32, (HG_CHUNK, HG_CHUNK), 1)
    return r >= c, c >= r, (r >= c).astype(F32), (c >= r).astype(F32)


def _hg_fwd(hq, hf, hi, hgt, logits, onorm_g):
    s = hq.shape[0]
    n_tiles = s // HG_TILE

    def body(q_ref, f_ref, i_ref, g_ref, lg_ref, og_ref, out_ref, o_ref, st_ref, state, qf_s, kk_s, lf_s):
        @pl.when(pl.program_id(0) == 0)
        def _():
            state[...] = jnp.zeros_like(state)

        lb = _lower_bound(lg_ref)
        f = lb + (1.0 - lb) * _sigmoid(f_ref[...])
        kk_s[...] = 1.0 - f
        lf_s[...] = jnp.log(f)
        qf_s[...] = _silu(q_ref[...])
        causal, _, tri, _ = _chunk_masks()

        def chunk(ci, carry):
            rows = pl.ds(pl.multiple_of(ci * HG_CHUNK, HG_CHUNK), HG_CHUNK)
            srows = pl.ds(pl.multiple_of(ci * HG_DIM, HG_DIM), HG_DIM)
            lf = lf_s[rows, :]
            b = _dot_f32(tri, lf)
            bl = _rowsum(lf)
            ref = 0.5 * bl
            qf, kk, v = qf_s[rows, :], kk_s[rows, :], i_ref[rows, :]
            a_in = (qf * jnp.exp(b)).astype(BF16)
            a_t = (qf * jnp.exp(b - ref)).astype(BF16)
            b_t = (kk * jnp.exp(ref - b)).astype(BF16)
            kd = kk * jnp.exp(bl - b)
            ebl = jnp.exp(bl)
            vb = v.astype(BF16)
            for h in range(HG_HEADS):
                c = slice(h * HG_DIM, (h + 1) * HG_DIM)
                st = state[h]
                st_ref[srows, c] = st
                p = jnp.where(causal, _dot_nt(a_t[:, c], b_t[:, c]), 0.0)
                o_ref[rows, c] = _dot(p.astype(BF16), vb[:, c]) + _dot_nt(a_in[:, c], st.astype(BF16))
                state[h] = st * ebl[:, c] + _dot_tn(vb[:, c], kd[:, c].astype(BF16))
            return carry

        lax.fori_loop(0, HG_TILE_CHUNKS, chunk, 0)
        for h in range(HG_HEADS):
            c = slice(h * HG_DIM, (h + 1) * HG_DIM)
            ohat, _ = _rms(o_ref[:, c])
            out_ref[:, c] = (ohat * og_ref[...] * _silu(g_ref[:, c])).astype(BF16)

    tile = _rows(HG_TILE, HG_WIDTH)
    return pl.pallas_call(
        body, name="hg_fwd", grid=(n_tiles,),
        out_shape=[jax.ShapeDtypeStruct((s, HG_WIDTH), BF16), jax.ShapeDtypeStruct((s, HG_WIDTH), F32),
                   jax.ShapeDtypeStruct((s // HG_CHUNK * HG_DIM, HG_WIDTH), F32)],
        in_specs=[tile] * 4 + [_whole((2, HG_WIDTH)), _whole((1, HG_DIM))],
        out_specs=[tile, tile, _rows(HG_TILE_CHUNKS * HG_DIM, HG_WIDTH)],
        scratch_shapes=[pltpu.VMEM((HG_HEADS, HG_DIM, HG_DIM), F32)] + [pltpu.VMEM((HG_TILE, HG_WIDTH), F32)] * 3,
        compiler_params=_params("arbitrary"),
    )(hq, hf, hi, hgt, logits, onorm_g)


def _hg_bwd(hq, hf, hi, hgt, logits, onorm_g, o, states, dout):
    s = hq.shape[0]
    n_tiles = s // HG_TILE

    def body(q_ref, f_ref, i_ref, g_ref, lg_ref, og_ref, o_ref, st_ref, d_ref,
             dq_ref, df_ref, di_ref, dg_ref, dog_ref, dlb_ref, dstate, qf_s, kk_s, lf_s, do_s):
        @pl.when(pl.program_id(0) == 0)
        def _():
            dstate[...] = jnp.zeros_like(dstate)
            dog_ref[...] = jnp.zeros_like(dog_ref)
            dlb_ref[...] = jnp.zeros_like(dlb_ref)

        og = og_ref[...]
        dog = jnp.zeros((1, HG_DIM), F32)
        for h in range(HG_HEADS):
            c = slice(h * HG_DIM, (h + 1) * HG_DIM)
            ohat, rstd = _rms(o_ref[:, c])
            gate = g_ref[:, c]
            d = d_ref[:, c]
            dg_ref[:, c] = d * (ohat * og) * _dsilu(gate)
            dnormed = d * _silu(gate)
            dog += _rowsum(dnormed * ohat)
            do_s[:, c] = _rms_bwd(dnormed * og, ohat, rstd)
        dog_ref[...] += dog

        lb = _lower_bound(lg_ref)
        f = lb + (1.0 - lb) * _sigmoid(f_ref[...])
        kk_s[...] = 1.0 - f
        lf_s[...] = jnp.log(f)
        qf_s[...] = _silu(q_ref[...])
        causal, upper, tri, tri_t = _chunk_masks()

        def chunk(step, carry):
            ci = HG_TILE_CHUNKS - 1 - step
            rows = pl.ds(pl.multiple_of(ci * HG_CHUNK, HG_CHUNK), HG_CHUNK)
            srows = pl.ds(pl.multiple_of(ci * HG_DIM, HG_DIM), HG_DIM)
            lf = lf_s[rows, :]
            b = _dot_f32(tri, lf)
            bl = _rowsum(lf)
            ref = 0.5 * bl
            qf, kk, v, do = qf_s[rows, :], kk_s[rows, :], i_ref[rows, :], do_s[rows, :]
            eb, ebr, erb, ekd, ebl = jnp.exp(b), jnp.exp(b - ref), jnp.exp(ref - b), jnp.exp(bl - b), jnp.exp(bl)
            a_in, a_t, b_t, kd = qf * eb, qf * ebr, kk * erb, kk * ekd
            for h in range(HG_HEADS):
                c = slice(h * HG_DIM, (h + 1) * HG_DIM)
                st, dst = st_ref[srows, c], dstate[h]
                stb, dstb = st.astype(BF16), dst.astype(BF16)
                doh, vh = do[:, c], v[:, c]
                dob, vb = doh.astype(BF16), vh.astype(BF16)
                ain_h, at_h, bt_h, kd_h = a_in[:, c], a_t[:, c], b_t[:, c], kd[:, c]
                atb, btb = at_h.astype(BF16), bt_h.astype(BF16)
                d_ain = _dot(dob, stb)
                p_t = jnp.where(upper, _dot_nt(btb, atb), 0.0).astype(BF16)
                dp = jnp.where(causal, _dot_nt(dob, vb), 0.0).astype(BF16)
                dp_t = jnp.where(upper, _dot_nt(vb, dob), 0.0).astype(BF16)
                di_ref[rows, c] = _dot(p_t, dob) + _dot_nt(kd_h.astype(BF16), dstb)
                d_at = _dot(dp, btb)
                d_bt = _dot(dp_t, atb)
                d_kd = _dot(vb, dstb)
                dqf = d_ain * eb[:, c] + d_at * ebr[:, c]
                dkk = d_bt * erb[:, c] + d_kd * ekd[:, c]
                db = d_ain * ain_h + d_at * atb.astype(F32) - d_bt * btb.astype(F32) - d_kd * kd_h
                dbl = _rowsum(d_kd * kd_h) + _rowsum(dst * st) * ebl[:, c]
                dstate[h] = _dot_tn(dob, ain_h.astype(BF16)) + dst * ebl[:, c]
                dlf = _dot_f32(tri_t, db) + dbl
                qv, fr = q_ref[rows, c], f_ref[rows, c]
                lbh = lb[:, c]
                sg = _sigmoid(fr)
                dfv = dlf / (lbh + (1.0 - lbh) * sg) - dkk
                df_ref[rows, c] = dfv * (1.0 - lbh) * sg * (1.0 - sg)
                dlb_ref[:, c] += _rowsum(dfv * (1.0 - sg))
                dq_ref[rows, c] = dqf * _dsilu(qv)
            return carry

        lax.fori_loop(0, HG_TILE_CHUNKS, chunk, 0)

    rev = pl.BlockSpec((HG_TILE, HG_WIDTH), lambda i: (n_tiles - 1 - i, 0))
    return pl.pallas_call(
        body, name="hg_bwd", grid=(n_tiles,),
        out_shape=[jax.ShapeDtypeStruct((s, HG_WIDTH), F32)] * 4
        + [jax.ShapeDtypeStruct((1, HG_DIM), F32), jax.ShapeDtypeStruct((1, HG_WIDTH), F32)],
        in_specs=[rev] * 4 + [_whole((2, HG_WIDTH)), _whole((1, HG_DIM)), rev,
                              pl.BlockSpec((HG_TILE_CHUNKS * HG_DIM, HG_WIDTH), lambda i: (n_tiles - 1 - i, 0)), rev],
        out_specs=[rev] * 4 + [_whole((1, HG_DIM)), _whole((1, HG_WIDTH))],
        scratch_shapes=[pltpu.VMEM((HG_HEADS, HG_DIM, HG_DIM), F32)] + [pltpu.VMEM((HG_TILE, HG_WIDTH), F32)] * 4,
        compiler_params=_params("arbitrary"),
    )(hq, hf, hi, hgt, logits, onorm_g, o, states, dout)


def _att_consts():
    lane = lax.broadcasted_iota(jnp.int32, (ATT_BLOCK, 128), 1)
    qi = lax.broadcasted_iota(jnp.int32, (2 * ATT_BLOCK, ATT_BLOCK), 0) % ATT_BLOCK
    kj = lax.broadcasted_iota(jnp.int32, (2 * ATT_BLOCK, ATT_BLOCK), 1)
    return lane < ATT_HEAD_DIM, kj <= qi, lambda off: kj >= qi + off


def _stack_heads(x2, first):
    return jnp.concatenate([jnp.where(first, x2, 0.0), jnp.where(first, 0.0, x2)], axis=0)


def _stack_bcast(x2, first):
    other = pltpu.roll(x2, ATT_HEAD_DIM, axis=1)
    return jnp.concatenate([jnp.where(first, x2, other), jnp.where(first, other, x2)], axis=0)


def _unstack_heads(st, first):
    return jnp.where(first, st[:ATT_BLOCK], st[ATT_BLOCK:])


def _att_fwd(q, k, v, dil):
    m, width = q.shape
    nb = m // ATT_BLOCK

    def body(q_ref, kc_ref, kp_ref, vc_ref, vp_ref, o_ref, lse_ref):
        first, cur_ok, _band = _att_consts()
        prev_ok = _band(jnp.where(pl.program_id(1) > 0, 0, ATT_BLOCK))
        for j in range(ATT_WIDTH // 128):
            c = slice(j * 128, (j + 1) * 128)
            qst = _stack_heads(q_ref[:, c] * ATT_SCALE, first).astype(BF16)
            kc, kp = kc_ref[:, c].astype(BF16), kp_ref[:, c].astype(BF16)
            vc, vp = vc_ref[:, c].astype(BF16), vp_ref[:, c].astype(BF16)
            sc = jnp.where(cur_ok, _dot_nt(qst, kc), NEG)
            sp = jnp.where(prev_ok, _dot_nt(qst, kp), NEG)
            mx = jnp.maximum(jnp.max(sc, axis=-1, keepdims=True), jnp.max(sp, axis=-1, keepdims=True))
            pc, pp = jnp.exp(sc - mx), jnp.exp(sp - mx)
            den = jnp.sum(pc, axis=-1, keepdims=True) + jnp.sum(pp, axis=-1, keepdims=True)
            ost = (_dot(pc.astype(BF16), vc) + _dot(pp.astype(BF16), vp)) / den
            lse = jnp.broadcast_to(mx + jnp.log(den), (2 * ATT_BLOCK, 128))
            o_ref[:, c] = _unstack_heads(ost, first)
            lse_ref[:, c] = _unstack_heads(lse, first)

    cur = pl.BlockSpec((ATT_BLOCK, ATT_WIDTH), lambda r, n: (n, r))
    prev = pl.BlockSpec((ATT_BLOCK, ATT_WIDTH), lambda r, n: (jnp.maximum(n - 1, 0), r))
    return pl.pallas_call(
        body, name=f"att_fwd_d{dil}", grid=(dil, nb),
        out_shape=[jax.ShapeDtypeStruct((m, width), F32)] * 2,
        in_specs=[cur, cur, prev, cur, prev], out_specs=[cur, cur],
        compiler_params=_params("parallel", "arbitrary"),
    )(q, k, k, v, v)


def _att_bwd(q, k, v, do, cc, lse, dil):
    m, width = q.shape
    nb = m // ATT_BLOCK

    def body(q_ref, qx_ref, kc_ref, kp_ref, vc_ref, vp_ref, do_ref, dox_ref, cc_ref, ccx_ref, lse_ref, lsex_ref,
             dq_ref, dk_ref, dv_ref):
        first, cur_ok, _band = _att_consts()
        n = pl.program_id(1)
        prev_ok = _band(jnp.where(n > 0, 0, ATT_BLOCK))
        next_ok = _band(jnp.where(n < nb - 1, 0, ATT_BLOCK))
        for j in range(ATT_WIDTH // 128):
            c = slice(j * 128, (j + 1) * 128)
            qst = _stack_heads(q_ref[:, c] * ATT_SCALE, first).astype(BF16)
            qxst = _stack_heads(qx_ref[:, c] * ATT_SCALE, first).astype(BF16)
            dost = _stack_heads(do_ref[:, c], first).astype(BF16)
            doxst = _stack_heads(dox_ref[:, c], first).astype(BF16)
            lse_n, lse_x = _stack_bcast(lse_ref[:, c], first), _stack_bcast(lsex_ref[:, c], first)
            cc_n, cc_x = _stack_bcast(cc_ref[:, c], first), _stack_bcast(ccx_ref[:, c], first)
            kc, kp = kc_ref[:, c].astype(BF16), kp_ref[:, c].astype(BF16)
            vc, vp = vc_ref[:, c].astype(BF16), vp_ref[:, c].astype(BF16)
            p_cur = jnp.exp(jnp.where(cur_ok, _dot_nt(qst, kc), NEG) - lse_n)
            p_prev = jnp.exp(jnp.where(prev_ok, _dot_nt(qst, kp), NEG) - lse_n)
            p_next = jnp.exp(jnp.where(next_ok, _dot_nt(qxst, kc), NEG) - lse_x)
            ds_cur = (p_cur * (_dot_nt(dost, vc) + cc_n)).astype(BF16)
            ds_prev = (p_prev * (_dot_nt(dost, vp) + cc_n)).astype(BF16)
            ds_next = (p_next * (_dot_nt(doxst, vc) + cc_x)).astype(BF16)
            dq_ref[:, c] = _unstack_heads(_dot(ds_cur, kc) + _dot(ds_prev, kp), first) * ATT_SCALE
            dk_ref[:, c] = _dot_tn(ds_cur, qst) + _dot_tn(ds_next, qxst)
            dv_ref[:, c] = _dot_tn(p_cur.astype(BF16), dost) + _dot_tn(p_next.astype(BF16), doxst)

    cur = pl.BlockSpec((ATT_BLOCK, ATT_WIDTH), lambda r, n: (n, r))
    prev = pl.BlockSpec((ATT_BLOCK, ATT_WIDTH), lambda r, n: (jnp.maximum(n - 1, 0), r))
    nxt = pl.BlockSpec((ATT_BLOCK, ATT_WIDTH), lambda r, n: (jnp.minimum(n + 1, nb - 1), r))
    return pl.pallas_call(
        body, name=f"att_bwd_d{dil}", grid=(dil, nb),
        out_shape=[jax.ShapeDtypeStruct((m, width), F32)] * 3,
        in_specs=[cur, nxt, cur, prev, cur, prev, cur, nxt, cur, nxt, cur, nxt], out_specs=[cur, cur, cur],
        compiler_params=_params("parallel", "arbitrary"),
    )(q, q, k, k, v, v, do, do, cc, cc, lse, lse)


def _branch_weights(lses):
    mx = jnp.maximum(jnp.maximum(lses[0], lses[1]), lses[2])
    es = [jnp.exp(l - mx) for l in lses]
    inv = 1.0 / (es[0] + es[1] + es[2])
    return [e * inv for e in es]


def _att_combine(outs, lses, att_g):
    s = outs[0].shape[0]
    tm = 512

    def body(o0, o1, o2, l0, l1, l2, g_ref, att_ref, out_ref):
        ws = _branch_weights([l0[...], l1[...], l2[...]])
        att = ws[0] * o0[...] + ws[1] * o1[...] + ws[2] * o2[...]
        att_ref[...] = att
        ahat, _ = _rms(att)
        out_ref[...] = (ahat * g_ref[...]).astype(BF16)

    tile = _rows(tm, ATT_WIDTH)
    return pl.pallas_call(
        body, name="att_combine", grid=(s // tm,),
        out_shape=[jax.ShapeDtypeStruct((s, ATT_WIDTH), F32), jax.ShapeDtypeStruct((s, ATT_WIDTH), BF16)],
        in_specs=[tile] * 6 + [_whole((1, ATT_WIDTH))], out_specs=[tile, tile],
        compiler_params=_params("parallel"),
    )(*outs, *lses, att_g)


def _att_combine_bwd(datt_out, att, lses, att_g):
    s = att.shape[0]
    tm = 256

    def body(d_ref, att_ref, l0, l1, l2, g_ref, do0, do1, do2, cc0, cc1, cc2, dg_ref):
        @pl.when(pl.program_id(0) == 0)
        def _():
            dg_ref[...] = jnp.zeros_like(dg_ref)

        att = att_ref[...]
        ahat, rstd = _rms(att)
        d = d_ref[...]
        dg_ref[...] += _rowsum(d * ahat)
        datt = _rms_bwd(d * g_ref[...], ahat, rstd)
        hi = lax.broadcasted_iota(jnp.int32, (ATT_WIDTH, ATT_WIDTH), 0) // ATT_HEAD_DIM
        hj = lax.broadcasted_iota(jnp.int32, (ATT_WIDTH, ATT_WIDTH), 1) // ATT_HEAD_DIM
        head_sum = _dot_f32(datt * att, (hi == hj).astype(F32))
        ws = _branch_weights([l0[...], l1[...], l2[...]])
        for w, do_ref, cc_ref in zip(ws, (do0, do1, do2), (cc0, cc1, cc2)):
            do_ref[...] = (w * datt).astype(BF16)
            cc_ref[...] = -w * head_sum

    tile = _rows(tm, ATT_WIDTH)
    return pl.pallas_call(
        body, name="att_combine_bwd", grid=(s // tm,),
        out_shape=[jax.ShapeDtypeStruct((s, ATT_WIDTH), BF16)] * 3 + [jax.ShapeDtypeStruct((s, ATT_WIDTH), F32)] * 3
        + [jax.ShapeDtypeStruct((1, ATT_WIDTH), F32)],
        in_specs=[tile] * 5 + [_whole((1, ATT_WIDTH))], out_specs=[tile] * 6 + [_whole((1, ATT_WIDTH))],
        compiler_params=_params("arbitrary"),
    )(datt_out, att, *lses, att_g)


def _out_fwd(x, hg, at, mod, w_out):
    s = x.shape[0]
    tm = 512

    def body(x_ref, hg_ref, at_ref, mod_ref, w_ref, x1_ref):
        mix = _dot(hg_ref[...], w_ref[0:512, :]) + _dot(at_ref[...], w_ref[512:1024, :])
        x1_ref[...] = x_ref[...] + mod_ref[:, 2 * D_MODEL:3 * D_MODEL] * mix

    return pl.pallas_call(
        body, name="out_fwd", grid=(s // tm,), out_shape=jax.ShapeDtypeStruct((s, D_MODEL), F32),
        in_specs=[_rows(tm, D_MODEL), _rows(tm, 512), _rows(tm, 512), _whole((1, 6 * D_MODEL)), _whole((D_MODEL, D_MODEL))],
        out_specs=_rows(tm, D_MODEL), compiler_params=_params("parallel"),
    )(x, hg, at, mod, w_out)


def _out_bwd(dx1, hg, at, mod, w_out):
    s = dx1.shape[0]
    tm = 512

    def body(dx_ref, hg_ref, at_ref, mod_ref, w_ref, dhg_ref, dat_ref, dw_ref, dgate_ref):
        @pl.when(pl.program_id(0) == 0)
        def _():
            dw_ref[...] = jnp.zeros_like(dw_ref)
            dgate_ref[...] = jnp.zeros_like(dgate_ref)

        hg, at, dx = hg_ref[...], at_ref[...], dx_ref[...]
        mix = _dot(hg, w_ref[0:512, :]) + _dot(at, w_ref[512:1024, :])
        dgate_ref[...] += _rowsum(dx * mix)
        dmix = (mod_ref[:, 2 * D_MODEL:3 * D_MODEL] * dx).astype(BF16)
        dhg_ref[...] = _dot_nt(dmix, w_ref[0:512, :])
        dat_ref[...] = _dot_nt(dmix, w_ref[512:1024, :])
        dw_ref[0:512, :] += _dot_tn(hg, dmix)
        dw_ref[512:1024, :] += _dot_tn(at, dmix)

    return pl.pallas_call(
        body, name="out_bwd", grid=(s // tm,),
        out_shape=[jax.ShapeDtypeStruct((s, 512), F32)] * 2
        + [jax.ShapeDtypeStruct((D_MODEL, D_MODEL), F32), jax.ShapeDtypeStruct((1, D_MODEL), F32)],
        in_specs=[_rows(tm, D_MODEL), _rows(tm, 512), _rows(tm, 512), _whole((1, 6 * D_MODEL)), _whole((D_MODEL, D_MODEL))],
        out_specs=[_rows(tm, 512), _rows(tm, 512), _whole((D_MODEL, D_MODEL)), _whole((1, D_MODEL))],
        compiler_params=_params("arbitrary"),
    )(dx1, hg, at, mod, w_out)


FFN_CHUNK = 256


def _ffn(x1, target, mod, g2, gf, w_gu, w_down):
    s = x1.shape[0]
    tm = 256
    n_chunks = D_FF // FFN_CHUNK

    def body(x_ref, t_ref, mod_ref, g2_ref, gf_ref, wgu_hbm, wd_hbm,
             dx_ref, h2_ref, act_ref, dau_ref, dff_ref, sums_ref, loss_ref, wgu, wd, a_s, u_s, sem):
        @pl.when(pl.program_id(0) == 0)
        def _():
            c1 = pltpu.make_async_copy(wgu_hbm, wgu, sem.at[0])
            c2 = pltpu.make_async_copy(wd_hbm, wd, sem.at[1])
            c1.start()
            c2.start()
            c1.wait()
            c2.wait()
            sums_ref[...] = jnp.zeros_like(sums_ref)
            loss_ref[...] = jnp.zeros_like(loss_ref)

        x1v = x_ref[...]
        xhat, rstd = _rms(x1v)
        g2 = g2_ref[...]
        n2 = xhat * g2
        scale2 = 1.0 + mod_ref[:, 4 * D_MODEL:5 * D_MODEL]
        gate2 = mod_ref[:, 5 * D_MODEL:6 * D_MODEL]
        hb = (n2 * scale2 + mod_ref[:, 3 * D_MODEL:4 * D_MODEL]).astype(BF16)
        h2_ref[...] = hb
        ff = jnp.zeros((tm, D_MODEL), F32)
        for j in range(n_chunks):
            c = slice(j * FFN_CHUNK, (j + 1) * FFN_CHUNK)
            cu = slice(D_FF + j * FFN_CHUNK, D_FF + (j + 1) * FFN_CHUNK)
            a = _dot(hb, wgu[:, c])
            u = _dot(hb, wgu[:, cu])
            a_s[:, c] = a
            u_s[:, c] = u
            act = (_silu(a) * u).astype(BF16)
            act_ref[:, c] = act
            ff += _dot(act, wd[c, :])
        x2 = x1v + gate2 * ff
        nf, rstd_f = _rms(x2)
        gfv = gf_ref[...]
        err = nf * gfv - t_ref[...]
        loss_ref[...] += 0.5 * jnp.sum(_rowsum(err * err), axis=-1, keepdims=True) * (1.0 / D_MODEL)
        dy = err * (1.0 / D_MODEL)
        dx2 = _rms_bwd(dy * gfv, nf, rstd_f)
        dffb = (gate2 * dx2).astype(BF16)
        dff_ref[...] = dffb
        dh = jnp.zeros((tm, D_MODEL), F32)
        for j in range(n_chunks):
            c = slice(j * FFN_CHUNK, (j + 1) * FFN_CHUNK)
            cu = slice(D_FF + j * FFN_CHUNK, D_FF + (j + 1) * FFN_CHUNK)
            dact = _dot_nt(dffb, wd[c, :])
            a, u = a_s[:, c], u_s[:, c]
            da = (dact * u * _dsilu(a)).astype(BF16)
            du = (dact * _silu(a)).astype(BF16)
            dau_ref[:, c] = da
            dau_ref[:, cu] = du
            dh += _dot_nt(da, wgu[:, c]) + _dot_nt(du, wgu[:, cu])
        dn = dh * scale2
        sums_ref[0:1, :] += _rowsum(dh)
        sums_ref[1:2, :] += _rowsum(dh * n2)
        sums_ref[2:3, :] += _rowsum(dx2 * ff)
        sums_ref[3:4, :] += _rowsum(dn * xhat)
        sums_ref[4:5, :] += _rowsum(dy * nf)
        dx_ref[...] = dx2 + _rms_bwd(dn * g2, xhat, rstd)

    vec = _whole((1, D_MODEL))
    hbm = pl.BlockSpec(memory_space=pl.ANY)
    return pl.pallas_call(
        body, name="ffn", grid=(s // tm,),
        out_shape=[jax.ShapeDtypeStruct((s, D_MODEL), F32), jax.ShapeDtypeStruct((s, D_MODEL), BF16),
                   jax.ShapeDtypeStruct((s, D_FF), BF16), jax.ShapeDtypeStruct((s, 2 * D_FF), BF16),
                   jax.ShapeDtypeStruct((s, D_MODEL), BF16), jax.ShapeDtypeStruct((8, D_MODEL), F32),
                   jax.ShapeDtypeStruct((1, 128), F32)],
        in_specs=[_rows(tm, D_MODEL), _rows(tm, D_MODEL), _whole((1, 6 * D_MODEL)), vec, vec, hbm, hbm],
        out_specs=[_rows(tm, D_MODEL), _rows(tm, D_MODEL), _rows(tm, D_FF), _rows(tm, 2 * D_FF), _rows(tm, D_MODEL),
                   _whole((8, D_MODEL)), _whole((1, 128))],
        scratch_shapes=[pltpu.VMEM((D_MODEL, 2 * D_FF), BF16), pltpu.VMEM((D_FF, D_MODEL), BF16),
                        pltpu.VMEM((tm, D_FF), F32), pltpu.VMEM((tm, D_FF), F32), pltpu.SemaphoreType.DMA((2,))],
        compiler_params=_params("arbitrary"),
    )(x1, target, mod, g2, gf, w_gu, w_down)


def _weight_grad(a, b, name):
    s, m = a.shape
    n = b.shape[1]
    ts, tn = min(s, 2048), 512

    def body(a_ref, b_ref, o_ref):
        @pl.when(pl.program_id(1) == 0)
        def _():
            o_ref[...] = jnp.zeros_like(o_ref)

        o_ref[...] += _dot_tn(a_ref[...], b_ref[...])

    return pl.pallas_call(
        body, name=name, grid=(n // tn, s // ts), out_shape=jax.ShapeDtypeStruct((m, n), F32),
        in_specs=[pl.BlockSpec((ts, m), lambda j, i: (i, 0)), pl.BlockSpec((ts, tn), lambda j, i: (i, j))],
        out_specs=pl.BlockSpec((m, tn), lambda j, i: (0, j)),
        compiler_params=_params("parallel", "arbitrary"),
    )(a, b)


def _adamw_math(w, g, m, v):
    m = ADAM_B1 * m + (1.0 - ADAM_B1) * g
    v = ADAM_B2 * v + (1.0 - ADAM_B2) * (g * g)
    m_hat = m / (1.0 - ADAM_B1 ** ADAM_STEP)
    v_hat = v / (1.0 - ADAM_B2 ** ADAM_STEP)
    delta = -ADAM_LR * (m_hat / (jnp.sqrt(v_hat) + ADAM_EPS) + ADAM_WD * w)
    return delta, m, v


def _adamw_shard(chip, w, m, v, partial, got, name):
    r, c = w.shape
    tr = _shard_rows(r)

    def body(chip_ref, w_ref, m_ref, v_ref, own_ref, g0, g1, g2, grad_ref, d_ref, nm_ref, nv_ref):
        g = ((own_ref[...] + g0[...].astype(F32)) + g1[...].astype(F32)) + g2[...].astype(F32)
        grad_ref[...] = g
        d_ref[...], nm_ref[...], nv_ref[...] = _adamw_math(w_ref[...], g, m_ref[...], v_ref[...])

    tile = pl.BlockSpec((tr, c), lambda i, chip_ref: (i, 0))
    own = pl.BlockSpec((None, tr, c), lambda i, chip_ref: (chip_ref[0], i, 0))
    part = [pl.BlockSpec((None, tr, c), functools.partial(lambda j, i, chip_ref: (j, i, 0), j)) for j in range(3)]
    return pl.pallas_call(
        body, name=name,
        grid_spec=pltpu.PrefetchScalarGridSpec(num_scalar_prefetch=1, grid=(r // tr,), in_specs=[tile] * 3 + [own] + part,
                                               out_specs=[tile] * 4),
        out_shape=[jax.ShapeDtypeStruct((r, c), F32)] * 4, compiler_params=_params("parallel"),
    )(chip, w, m, v, partial, got, got, got)


def _small_update(small_all, dmod_blocks, c_all, logits, w_ada, m_ada, v_ada, smalls):
    def body(sm_ref, dm_ref, c_ref, lg_ref, wa_ref, ma_ref, va_ref, *rest):
        ins, outs = rest[:21], rest[21:]
        _, me = _flip(0)
        tot = sm_ref[0:1, :]
        for i in range(1, N_DEV):
            tot = tot + sm_ref[i:i + 1, :]
        loss_ref = outs[0]
        loss_ref[...] = tot[:, SM_LOSS:SM_LOSS + 128]
        g_ada = lax.dot_general(_silu(c_ref[...]), dm_ref[me], (((0,), (0,)), ((), ())),
                                preferred_element_type=F32, precision=HIGHEST)
        outs[1][...] = g_ada
        outs[2][...], outs[3][...], outs[4][...] = _adamw_math(wa_ref[...], g_ada, ma_ref[...], va_ref[...])
        p0 = _lower_bound(lg_ref)
        dl0 = tot[:, SM_LB:SM_LB + 512] * p0 * (1.0 - p0)
        grads = [tot[:, SM_MOD:SM_MOD + 6 * D_MODEL], tot[:, SM_G1:SM_G1 + D_MODEL], tot[:, SM_G2:SM_G2 + D_MODEL],
                 tot[:, SM_GF:SM_GF + D_MODEL], tot[:, SM_ATT:SM_ATT + 512], tot[:, SM_HG:SM_HG + 128],
                 jnp.where(lax.broadcasted_iota(jnp.int32, (2, 512), 0) == 0, dl0, -dl0)]
        for i, g in enumerate(grads):
            w_ref, m_ref, v_ref = ins[3 * i:3 * i + 3]
            o = outs[5 + 4 * i:9 + 4 * i]
            o[0][...] = g
            o[1][...], o[2][...], o[3][...] = _adamw_math(w_ref[...], g, m_ref[...], v_ref[...])

    flat = [t for trio in smalls for t in trio]
    vm = pl.BlockSpec(memory_space=pltpu.VMEM)
    out_shape = [jax.ShapeDtypeStruct((1, 128), F32)] + [jax.ShapeDtypeStruct(w_ada.shape, F32)] * 4
    for trio in smalls:
        out_shape += [jax.ShapeDtypeStruct(trio[0].shape, F32)] * 4
    return pl.pallas_call(
        body, name="small_update", out_shape=out_shape,
        in_specs=[vm] * (7 + len(flat)), out_specs=[vm] * len(out_shape),
        compiler_params=pltpu.CompilerParams(vmem_limit_bytes=V7X_VMEM_LIMIT),
    )(small_all, dmod_blocks, c_all, logits, w_ada, m_ada, v_ada, *flat)


def _dilate(t, d):
    return t if d == 1 else t.reshape(t.shape[0] // d, d * t.shape[1])


def _undilate(t, d):
    return t if d == 1 else t.reshape(t.shape[0] * d, t.shape[1] // d)


def kernel(x, c, w_ada, b_ada, norm1_g, w_in, hg_lb_logits, hg_onorm_g, att_onorm_g, w_out, norm2_g, w_gate_up, w_down, final_g, loss_target, m_w_ada, m_b_ada, m_norm1_g, m_w_in, m_hg_lb_logits, m_hg_onorm_g, m_att_onorm_g, m_w_out, m_norm2_g, m_w_gate_up, m_w_down, m_final_g, v_w_ada, v_b_ada, v_norm1_g, v_w_in, v_hg_lb_logits, v_hg_onorm_g, v_att_onorm_g, v_w_out, v_norm2_g, v_w_gate_up, v_w_down, v_final_g):
    x2d, target = x[0], loss_target[0]
    seq = x2d.shape[0]
    assert seq % (ATT_BLOCK * max(DILATIONS)) == 0 and seq % HG_TILE == 0
    gf = final_g.reshape(1, D_MODEL)

    c_all = _exchange_small(c.reshape(8, D_MODEL // 8), None, "gather_c").reshape(N_DEV, D_MODEL)
    ada = _ada_rows(c_all, w_ada[0], b_ada)
    mod = _exchange_small(ada, 1, "scatter_mod").reshape(1, 6 * D_MODEL)

    core = lax.axis_index("c").astype(jnp.int32).reshape(1)
    chip = (2 * lax.axis_index("x") + lax.axis_index("y")).astype(jnp.int32).reshape(1)
    me = 4 * lax.axis_index("x") + 2 * lax.axis_index("y") + lax.axis_index("c")

    g_in, = _gather_weights([w_in[0].astype(BF16)])
    w_in_b = jnp.transpose(g_in, (1, 0, 2)).reshape(D_MODEL, IN_WIDTH)
    rest_shards = [w_out[0].astype(BF16), w_gate_up[0].astype(BF16), w_down[0].astype(BF16)]
    lands = [lax.empty((N_DEV,) + s.shape, BF16) for s in rest_shards]
    g_send, g_recv, g_srcs, g_lands, tok = _copies_start("gather_rest_start", _plan_gather_own, 12, rest_shards, lands, [w_in_b, mod])
    flight = {}

    def stage(name, *vals):
        if name == "attention_begun":
            got = _copies_wait("gather_rest_wait", _plan_gather_own, g_send, g_recv, g_srcs, g_lands, [vals[0]])
            flight["pass"] = _copies_start("gather_pass_start", _plan_gather_pass, 9, [], got, [])
            return flight["pass"][4][0:1, 0:1]
        if name == "mixer_weights_done":
            dw_gu, dw_down, dw_out = vals
            flight["grads"] = [dw_out.reshape(4, 2, D_MODEL // N_DEV, D_MODEL),
                               dw_gu.reshape(D_MODEL, 4, 2, 2 * D_FF // N_DEV).transpose(1, 2, 0, 3),
                               dw_down.reshape(4, 2, D_FF // N_DEV, D_MODEL)]
            pair_lands = [lax.empty((4,) + g.shape[2:], F32) for g in flight["grads"]]
            flight["pairs"] = _copies_start("reduce_pairs_start", _plan_reduce_pairs, 12, flight["grads"], pair_lands, [])
            return flight["pairs"][4][0:1, 0:1]
        if name == "attention_backward_begun":
            s, r, srcs, pl_lands, _ = flight["pairs"]
            got = _copies_wait("reduce_pairs_wait", _plan_reduce_pairs, s, r, srcs, pl_lands, [vals[0]])
            flight["sums"] = [_pair_sum(core, g, b, f"pair_sum_{i}") for i, (g, b) in enumerate(zip(flight["grads"], got))]
            chip_lands = [lax.empty((3,) + s16.shape[1:], BF16) for _, s16 in flight["sums"]]
            flight["chips"] = _copies_start("reduce_chips_start", _plan_reduce_chips, 9, [s16 for _, s16 in flight["sums"]], chip_lands, [])
            return flight["chips"][4][0:1, 0:1]
        raise ValueError(name)

    def rest_weights(after):
        s, r, _, p_lands, _ = flight["pass"]
        got = _copies_wait("gather_pass_wait", _plan_gather_pass, s, r, [], p_lands, [after])
        full = [lax.dynamic_update_index_in_dim(g, shard, me, 0) for g, shard in zip(got, rest_shards)]
        return (full[0].reshape(D_MODEL, D_MODEL), jnp.transpose(full[1], (1, 0, 2)).reshape(D_MODEL, 2 * D_FF),
                full[2].reshape(D_FF, D_MODEL))

    grad_x, dw_in, small = _block_step(x2d, target, mod + tok[0:1, 0:1], norm1_g, hg_lb_logits, hg_onorm_g, att_onorm_g, norm2_g, gf,
                                       w_in_b, rest_weights, stage)

    g_in8 = dw_in.reshape(D_MODEL, 4, 2, IN_WIDTH // N_DEV).transpose(1, 2, 0, 3)
    in_pairs = _copies_start("reduce_pairs_in_start", _plan_reduce_pairs, 4, [g_in8], [lax.empty((4,) + g_in8.shape[2:], F32)], [])
    s, r, srcs, c_lands, _ = flight["chips"]
    recv_rest = _copies_wait("reduce_chips_wait", _plan_reduce_chips, s, r, srcs, c_lands, [in_pairs[4]])
    big = {}
    rest_params = [("w_out", w_out, m_w_out, v_w_out), ("w_gate_up", w_gate_up, m_w_gate_up, v_w_gate_up), ("w_down", w_down, m_w_down, v_w_down)]
    for (n, w, m, v), (s32, _), got in zip(rest_params, flight["sums"], recv_rest):
        big[n] = [t[None] for t in _adamw_shard(chip, w[0], m[0], v[0], s32, got, f"adamw_{n}")]
    got_in = _copies_wait("reduce_pairs_in_wait", _plan_reduce_pairs, in_pairs[0], in_pairs[1], in_pairs[2], in_pairs[3],
                          [big[n][3] for n, _, _, _ in rest_params])
    in_s32, in_s16 = _pair_sum(core, g_in8, got_in[0], "pair_sum_in")
    in_chips = _copies_start("reduce_chips_in_start", _plan_reduce_chips, 3, [in_s16], [lax.empty((3,) + in_s16.shape[1:], BF16)], [])

    small_rows = jnp.pad(small + in_chips[4][0:1, 0:1], ((0, 0), (0, SM_PADDED - SM_WIDTH))).reshape(SM_PADDED // 128, 128)
    small_all = _exchange_small(small_rows, None, "gather_small").reshape(N_DEV, SM_PADDED)[:, :SM_WIDTH]
    smalls = [(b_ada, m_b_ada, v_b_ada), (norm1_g, m_norm1_g, v_norm1_g), (norm2_g, m_norm2_g, v_norm2_g),
              (gf, m_final_g.reshape(1, D_MODEL), v_final_g.reshape(1, D_MODEL)),
              (att_onorm_g, m_att_onorm_g, v_att_onorm_g), (hg_onorm_g, m_hg_onorm_g, v_hg_onorm_g),
              (hg_lb_logits, m_hg_lb_logits, v_hg_lb_logits)]
    dmod_blocks = small_all[:, :6 * D_MODEL].reshape(N_DEV, N_DEV, 6 * D_MODEL // N_DEV).transpose(1, 0, 2)
    res = _small_update(small_all, dmod_blocks, c_all, hg_lb_logits, w_ada[0], m_w_ada[0], v_w_ada[0], smalls)
    recv_in = _copies_wait("reduce_chips_in_wait", _plan_reduce_chips, in_chips[0], in_chips[1], in_chips[2], in_chips[3], [res[0]])
    big["w_in"] = [t[None] for t in _adamw_shard(chip, w_in[0], m_w_in[0], v_w_in[0], in_s32, recv_in[0], "adamw_w_in")]
    loss = res[0][0, 0]
    ada4 = [t[None] for t in res[1:5]]
    sm4 = {n: list(res[5 + 4 * i:9 + 4 * i]) for i, n in enumerate(["b_ada", "norm1_g", "norm2_g", "final_g", "att", "hg", "lb"])}
    sm4["final_g"] = [t.reshape(D_MODEL) for t in sm4["final_g"]]

    order = [ada4, sm4["b_ada"], sm4["norm1_g"], big["w_in"], sm4["lb"], sm4["hg"], sm4["att"], big["w_out"], sm4["norm2_g"],
             big["w_gate_up"], big["w_down"], sm4["final_g"]]
    return (loss, grad_x[None], *[o[0] for o in order], *[o[1] for o in order], *[o[2] for o in order], *[o[3] for o in order])


def _block_step(x2d, target, mod, norm1_g, hg_lb_logits, hg_onorm_g, att_onorm_g, norm2_g, gf, w_in_b, rest_weights, stage):
    h1, hq, hf, hi, hgt, aq, ak, av = _in_fwd(x2d, mod, norm1_g, w_in_b)
    hg_out, hg_o, hg_states = _hg_fwd(hq, hf, hi, hgt, hg_lb_logits, hg_onorm_g)
    views = [(_dilate(aq, d), _dilate(ak, d), _dilate(av, d)) for d in DILATIONS]
    branch = [_att_fwd(*views[0], DILATIONS[0])]
    att_g = att_onorm_g + stage("attention_begun", branch[0][0])
    branch += [_att_fwd(*views[i], d) for i, d in enumerate(DILATIONS) if i > 0]
    outs = [_undilate(branch[i][0], d) for i, d in enumerate(DILATIONS)]
    lses = [_undilate(branch[i][1], d) for i, d in enumerate(DILATIONS)]
    att, att_out = _att_combine(outs, lses, att_g)
    w_out_b, w_gu_b, w_down_b = rest_weights(att_out)
    x1 = _out_fwd(x2d, hg_out, att_out, mod, w_out_b)

    dx1, h2, act, dau, dff, ffn_sums, loss_part = _ffn(x1, target, mod, norm2_g, gf, w_gu_b, w_down_b)
    dw_gu = _weight_grad(h2, dau, "dw_gate_up")
    dw_down = _weight_grad(act, dff, "dw_down")

    dhg, dat, dw_out, dgate1 = _out_bwd(dx1, hg_out, att_out, mod, w_out_b)
    att_g = att_onorm_g + stage("mixer_weights_done", dw_gu, dw_down, dw_out)
    comb = _att_combine_bwd(dat, att, lses, att_g)
    dos, ccs, d_att_g = comb[0:3], comb[3:6], comb[6]
    hg_g = hg_onorm_g + stage("attention_backward_begun", comb[3])
    datt = []
    for i, d in enumerate(DILATIONS):
        grads = _att_bwd(*views[i], _dilate(dos[i], d), _dilate(ccs[i], d), branch[i][1], d)
        datt.append([_undilate(g, d) for g in grads])
    dhq, dhf, dhi, dhgt, d_hg_g, d_lb = _hg_bwd(hq, hf, hi, hgt, hg_lb_logits, hg_g, hg_o, hg_states, dhg)
    dps = [dhq, dhf, dhi, dhgt] + [datt[i][j] for j in range(3) for i in range(3)]
    grad_x, dp_b, dshift1, dscale1, d_g1 = _in_bwd(x2d, dx1, mod, norm1_g, w_in_b, dps)
    dw_in = _weight_grad(h1, dp_b, "dw_in")
    small = jnp.concatenate([dshift1, dscale1, dgate1, ffn_sums[0:1], ffn_sums[1:2], ffn_sums[2:3], d_g1, ffn_sums[3:4],
                             ffn_sums[4:5], d_att_g, d_lb, d_hg_g, loss_part], axis=1)
    return grad_x, dw_in, small
```

```python
import functools

import jax
import jax.numpy as jnp
from jax import lax
from jax.experimental import pallas as pl
from jax.experimental.pallas import tpu as pltpu

F32 = jnp.float32
BF16 = jnp.bfloat16
HIGHEST = lax.Precision.HIGHEST
MESH = pl.DeviceIdType.MESH

D_MODEL = 1024
N_DEV = 8
HG_HEADS = 4
HG_DIM = 128
HG_WIDTH = HG_HEADS * HG_DIM
HG_CHUNK = 64
ATT_WIDTH = 512
ATT_HEAD_DIM = 64
ATT_BLOCK = 128
DILATIONS = (1, 4, 16)
ATT_SCALE = ATT_HEAD_DIM ** -0.5
D_FF = 2816
IN_WIDTH = 7 * 512
RMS_EPS = 1e-6
NEG = -1e30

ADAM_LR = 0.001
ADAM_B1 = 0.9
ADAM_B2 = 0.999
ADAM_EPS = 1e-08
ADAM_WD = 0.01
ADAM_STEP = 10

V7X_VMEM_LIMIT = 56 * 1024 * 1024

SM_MOD = 0
SM_G1 = 6 * D_MODEL
SM_G2 = 7 * D_MODEL
SM_GF = 8 * D_MODEL
SM_ATT = 9 * D_MODEL
SM_LB = 9 * D_MODEL + 512
SM_HG = 10 * D_MODEL
SM_LOSS = 10 * D_MODEL + 128
SM_WIDTH = 10 * D_MODEL + 256
SM_PADDED = 88 * 128


def _params(*sem, vmem=V7X_VMEM_LIMIT):
    return pltpu.CompilerParams(dimension_semantics=sem, vmem_limit_bytes=vmem)


def _dot(a, b):
    return jnp.dot(a, b, preferred_element_type=F32)


def _dot_nt(a, b):
    return lax.dot_general(a, b, (((1,), (1,)), ((), ())), preferred_element_type=F32)


def _dot_tn(a, b):
    return lax.dot_general(a, b, (((0,), (0,)), ((), ())), preferred_element_type=F32)


def _dot_f32(a, b):
    return jnp.dot(a, b, preferred_element_type=F32, precision=HIGHEST)


def _sigmoid(x):
    return 1.0 / (1.0 + jnp.exp(-x))


def _silu(x):
    return x * _sigmoid(x)


def _dsilu(x):
    s = _sigmoid(x)
    return s * (1.0 + x * (1.0 - s))


def _rms(x):
    rstd = lax.rsqrt(jnp.mean(x * x, axis=-1, keepdims=True) + RMS_EPS)
    return x * rstd, rstd


def _rms_bwd(dn, xhat, rstd):
    return rstd * (dn - xhat * jnp.mean(dn * xhat, axis=-1, keepdims=True))


def _rowsum(x):
    return jnp.sum(x, axis=0, keepdims=True)


def _rows(tm, n):
    return pl.BlockSpec((tm, n), lambda i: (i, 0))


def _whole(shape):
    return pl.BlockSpec(shape, lambda i: (0,) * len(shape))


def _mesh_pos():
    return lax.axis_index("x"), lax.axis_index("y"), lax.axis_index("c")


def _flip(k):
    x, y, c = _mesh_pos()
    px = 1 - x if k & 4 else x
    py = 1 - y if k & 2 else y
    pc = 1 - c if k & 1 else c
    return (px, py, pc), 4 * px + 2 * py + pc


def _exchange_small(x, rows_per_peer, name):
    r_all, cols = x.shape
    r_out = r_all if rows_per_peer is None else rows_per_peer

    def body(x_ref, out_ref, send_sems, recv_sems):
        _, me = _flip(0)

        def src(pid):
            if rows_per_peer is None:
                return x_ref
            return x_ref.at[pl.ds(pl.multiple_of(pid * r_out, r_out), r_out), :]

        if rows_per_peer is None:
            out_ref[me] = x_ref[...]
        else:
            out_ref[me] = x_ref[pl.ds(pl.multiple_of(me * r_out, r_out), r_out), :]
        sends = []
        for k in range(1, N_DEV):
            dev, pid = _flip(k)
            cp = pltpu.make_async_remote_copy(src_ref=src(pid), dst_ref=out_ref.at[me], send_sem=send_sems.at[k - 1],
                                              recv_sem=recv_sems.at[k - 1], device_id=dev, device_id_type=MESH)
            cp.start()
            sends.append(cp)
        for k in range(1, N_DEV):
            dev, pid = _flip(k)
            pltpu.make_async_remote_copy(src_ref=src(pid), dst_ref=out_ref.at[pid], send_sem=send_sems.at[k - 1],
                                         recv_sem=recv_sems.at[k - 1], device_id=dev, device_id_type=MESH).wait_recv()
        for cp in sends:
            cp.wait_send()

    return pl.pallas_call(
        body, name=name,
        out_shape=jax.ShapeDtypeStruct((N_DEV, r_out, cols), x.dtype),
        in_specs=[pl.BlockSpec(memory_space=pltpu.VMEM)],
        out_specs=pl.BlockSpec(memory_space=pltpu.VMEM),
        scratch_shapes=[pltpu.SemaphoreType.DMA((N_DEV - 1,)), pltpu.SemaphoreType.DMA((N_DEV - 1,))],
    )(x)


def _gather_weights(shards):
    n = len(shards)

    def body(*refs):
        xs, outs = refs[:n], refs[n:2 * n]
        send_sems, recv_sems, local_sems = refs[2 * n:]
        x, y, c = _mesh_pos()
        me, sibling = (x, y, c), (x, y, 1 - c)
        chips = [(1 - x, y), (x, 1 - y), (1 - x, 1 - y)]

        def blk(a, px, py, pc):
            return outs[a].at[4 * px + 2 * py + pc]

        def copy(a, k, block, to, src=None):
            return pltpu.make_async_remote_copy(
                src_ref=blk(a, *block) if src is None else src, dst_ref=blk(a, *block),
                send_sem=send_sems.at[a * 7 + k], recv_sem=recv_sems.at[a * 7 + k], device_id=to, device_id_type=MESH)

        mine = [pltpu.make_async_copy(xs[a], blk(a, *me), local_sems.at[a]) for a in range(n)]
        for cp in mine:
            cp.start()
        first = []
        for a in range(n):
            first.append(copy(a, 0, me, sibling, src=xs[a]))
            first += [copy(a, 1 + j, me, (*chip, c), src=xs[a]) for j, chip in enumerate(chips)]
        for cp in first:
            cp.start()
        passed = []
        for j, chip in enumerate(chips):
            for a in range(n):
                copy(a, 1 + j, (*chip, c), me).wait_recv()
                cp = copy(a, 4 + j, (*chip, c), sibling)
                cp.start()
                passed.append(cp)
        for a in range(n):
            copy(a, 0, sibling, me).wait_recv()
            for j, chip in enumerate(chips):
                copy(a, 4 + j, (*chip, 1 - c), me).wait_recv()
        for cp in first + passed:
            cp.wait_send()
        for cp in mine:
            cp.wait()

    hbm = pl.BlockSpec(memory_space=pl.ANY)
    return pl.pallas_call(
        body, name="gather_weights",
        out_shape=[jax.ShapeDtypeStruct((N_DEV,) + s.shape, s.dtype) for s in shards],
        in_specs=[hbm] * n, out_specs=[hbm] * n,
        scratch_shapes=[pltpu.SemaphoreType.DMA((7 * n,)), pltpu.SemaphoreType.DMA((7 * n,)), pltpu.SemaphoreType.DMA((n,))],
    )(*shards)


_HBM = pl.BlockSpec(memory_space=pltpu.HBM)
_SEM = pl.BlockSpec(memory_space=pltpu.SEMAPHORE)
_DATAFLOW = pltpu.SideEffectType.DATAFLOW_SIDE_EFFECTING


def _copies_start(name, plan, n_copies, srcs, lands, after):
    bufs = list(srcs) + list(lands)
    nb = len(bufs)

    def body(*refs):
        ins, send_sems, recv_sems, token = refs[:nb], refs[nb + len(after)], refs[nb + len(after) + 1], refs[-1]
        for i, (src, dst, dev) in enumerate(plan(ins[:len(srcs)], ins[len(srcs):])):
            pltpu.make_async_remote_copy(src_ref=src, dst_ref=dst, send_sem=send_sems.at[i], recv_sem=recv_sems.at[i],
                                         device_id=dev, device_id_type=MESH).start()
        token[...] = jnp.zeros_like(token)

    outs = pl.pallas_call(
        body, name=name,
        out_shape=(pltpu.SemaphoreType.DMA((n_copies,)), pltpu.SemaphoreType.DMA((n_copies,)),
                   *[pltpu.HBM(b.shape, b.dtype) for b in bufs], jax.ShapeDtypeStruct((8, 128), F32)),
        in_specs=[_HBM] * nb + [pl.BlockSpec(memory_space=pl.ANY)] * len(after),
        out_specs=(_SEM, _SEM, *[_HBM] * nb, pl.BlockSpec(memory_space=pltpu.VMEM)),
        input_output_aliases={i: 2 + i for i in range(nb)},
        compiler_params=pltpu.CompilerParams(has_side_effects=_DATAFLOW),
    )(*[pltpu.with_memory_space_constraint(b, pltpu.HBM) for b in bufs], *after)
    return outs[0], outs[1], list(outs[2:2 + len(srcs)]), list(outs[2 + len(srcs):2 + nb]), outs[-1]


def _copies_wait(name, plan, send_sems, recv_sems, srcs, lands, after):
    bufs = list(srcs) + list(lands)
    nb = len(bufs)

    def body(*refs):
        ins, send_ref, recv_ref = refs[:nb], refs[nb], refs[nb + 1]
        for i, (src, dst, dev) in enumerate(plan(ins[:len(srcs)], ins[len(srcs):])):
            cp = pltpu.make_async_remote_copy(src_ref=src, dst_ref=dst, send_sem=send_ref.at[i], recv_sem=recv_ref.at[i],
                                              device_id=dev, device_id_type=MESH)
            cp.wait_send()
            cp.wait_recv()

    outs = pl.pallas_call(
        body, name=name, out_shape=[pltpu.HBM(b.shape, b.dtype) for b in bufs],
        in_specs=[_HBM] * nb + [_SEM, _SEM] + [pl.BlockSpec(memory_space=pl.ANY)] * len(after), out_specs=[_HBM] * nb,
        input_output_aliases={i: i for i in range(nb)},
        compiler_params=pltpu.CompilerParams(has_side_effects=_DATAFLOW),
    )(*bufs, send_sems, recv_sems, *after)
    return list(outs[:len(srcs)]), list(outs[len(srcs):])


def _plan_gather_own(srcs, lands):
    _, me = _flip(0)
    return [(srcs[a], lands[a].at[me], _flip(k)[0]) for a in range(len(srcs)) for k in (1, 4, 2, 6)]


def _plan_gather_pass(srcs, lands):
    sibling = _flip(1)[0]
    plan = []
    for land in lands:
        for k in (4, 2, 6):
            block = land.at[_flip(k)[1]]
            plan.append((block, block, sibling))
    return plan


def _plan_reduce_pairs(srcs, lands):
    x, y, c = _mesh_pos()
    return [(srcs[a].at[chip, 1 - c], lands[a].at[chip], (x, y, 1 - c)) for a in range(len(srcs)) for chip in range(4)]


def _plan_reduce_chips(srcs, lands):
    plan = []
    for a in range(len(srcs)):
        for j, k in enumerate((4, 2, 6)):
            dev = _flip(k)[0]
            plan.append((srcs[a].at[2 * dev[0] + dev[1]], lands[a].at[j], dev))
    return plan


def _shard_rows(r):
    return r // 2 if r % 32 == 0 else r


def _pair_sum(core, grads, got, name):
    _, _, r, c = grads.shape
    tr = _shard_rows(r)

    def body(core_ref, a_ref, b_ref, o_ref, ob_ref):
        s = a_ref[...] + b_ref[...]
        o_ref[...] = s
        ob_ref[...] = s.astype(BF16)

    spec = pl.BlockSpec((None, tr, c), lambda i, j, core_ref: (i, j, 0))
    return pl.pallas_call(
        body, name=name,
        grid_spec=pltpu.PrefetchScalarGridSpec(
            num_scalar_prefetch=1, grid=(4, r // tr),
            in_specs=[pl.BlockSpec((None, None, tr, c), lambda i, j, core_ref: (i, core_ref[0], j, 0)), spec],
            out_specs=[spec, spec]),
        out_shape=[jax.ShapeDtypeStruct((4, r, c), F32), jax.ShapeDtypeStruct((4, r, c), BF16)],
        compiler_params=_params("parallel", "parallel"),
    )(core, grads, got)


def _ada_rows(c_all, w_ada, b_ada):
    n_cols = w_ada.shape[1]

    def body(c_ref, w_ref, b_ref, o_ref):
        _, me = _flip(0)
        bias = b_ref[:, pl.ds(pl.multiple_of(me * n_cols, 128), n_cols)]
        o_ref[...] = _dot_f32(_silu(c_ref[...]), w_ref[...]) + bias

    return pl.pallas_call(
        body, name="ada_rows", out_shape=jax.ShapeDtypeStruct((N_DEV, n_cols), F32),
        in_specs=[pl.BlockSpec(memory_space=pltpu.VMEM)] * 3, out_specs=pl.BlockSpec(memory_space=pltpu.VMEM),
    )(c_all, w_ada, b_ada)


def _in_fwd(x, mod, g1, w_in):
    s = x.shape[0]
    tm = 256

    def body(x_ref, mod_ref, g_ref, w_ref, h_ref, *outs):
        xhat, _ = _rms(x_ref[...])
        h = (xhat * g_ref[...]) * (1.0 + mod_ref[:, D_MODEL:2 * D_MODEL]) + mod_ref[:, 0:D_MODEL]
        hb = h.astype(BF16)
        h_ref[...] = hb
        for j, o_ref in enumerate(outs):
            o_ref[...] = _dot(hb, w_ref[:, j * 512:(j + 1) * 512])

    return pl.pallas_call(
        body, name="in_fwd", grid=(s // tm,),
        out_shape=[jax.ShapeDtypeStruct((s, D_MODEL), BF16)] + [jax.ShapeDtypeStruct((s, 512), F32)] * 7,
        in_specs=[_rows(tm, D_MODEL), _whole((1, 6 * D_MODEL)), _whole((1, D_MODEL)), _whole((D_MODEL, IN_WIDTH))],
        out_specs=[_rows(tm, D_MODEL)] + [_rows(tm, 512)] * 7,
        compiler_params=_params("parallel"),
    )(x, mod, g1, w_in)


def _in_bwd(x, dx1, mod, g1, w_in, dps):
    s = x.shape[0]
    tm = 256

    def body(x_ref, dx_ref, mod_ref, g_ref, w_ref, *rest):
        dp_refs, (gx_ref, dpb_ref, dsh_ref, dsc_ref, dg_ref) = rest[:13], rest[13:]
        pieces = [dp_refs[j][...] for j in range(4)]
        pieces += [dp_refs[4 + 3 * j][...] + dp_refs[5 + 3 * j][...] + dp_refs[6 + 3 * j][...] for j in range(3)]
        dh = jnp.zeros((tm, D_MODEL), F32)
        for j, p in enumerate(pieces):
            pb = p.astype(BF16)
            dpb_ref[:, j * 512:(j + 1) * 512] = pb
            dh += _dot_nt(pb, w_ref[:, j * 512:(j + 1) * 512])
        xhat, rstd = _rms(x_ref[...])
        g = g_ref[...]
        scale1 = 1.0 + mod_ref[:, D_MODEL:2 * D_MODEL]
        n1 = xhat * g

        @pl.when(pl.program_id(0) == 0)
        def _():
            dsh_ref[...] = jnp.zeros_like(dsh_ref)
            dsc_ref[...] = jnp.zeros_like(dsc_ref)
            dg_ref[...] = jnp.zeros_like(dg_ref)

        dsh_ref[...] += _rowsum(dh)
        dsc_ref[...] += _rowsum(dh * n1)
        dn = dh * scale1
        dg_ref[...] += _rowsum(dn * xhat)
        gx_ref[...] = dx_ref[...] + _rms_bwd(dn * g, xhat, rstd)

    vec = _whole((1, D_MODEL))
    return pl.pallas_call(
        body, name="in_bwd", grid=(s // tm,),
        out_shape=[jax.ShapeDtypeStruct((s, D_MODEL), F32), jax.ShapeDtypeStruct((s, IN_WIDTH), BF16)]
        + [jax.ShapeDtypeStruct((1, D_MODEL), F32)] * 3,
        in_specs=[_rows(tm, D_MODEL), _rows(tm, D_MODEL), _whole((1, 6 * D_MODEL)), vec, _whole((D_MODEL, IN_WIDTH))]
        + [_rows(tm, 512)] * 13,
        out_specs=[_rows(tm, D_MODEL), _rows(tm, IN_WIDTH), vec, vec, vec],
        compiler_params=_params("arbitrary"),
    )(x, dx1, mod, g1, w_in, *dps)


HG_TILE = 512
HG_TILE_CHUNKS = HG_TILE // HG_CHUNK


def _lower_bound(lg_ref):
    return 1.0 / (1.0 + jnp.exp(lg_ref[1:2, :] - lg_ref[0:1, :]))


def _chunk_masks():
    r = lax.broadcasted_iota(jnp.int32, (HG_CHUNK, HG_CHUNK), 0)
    c = lax.broadcasted_iota(jnp.int32, (HG_CHUNK, HG_CHUNK), 1)
    return r >= c, c >= r, (r >= c).astype(F32), (c >= r).astype(F32)


def _hg_fwd(hq, hf, hi, hgt, logits, onorm_g):
    s = hq.shape[0]
    n_tiles = s // HG_TILE

    def body(q_ref, f_ref, i_ref, g_ref, lg_ref, og_ref, out_ref, o_ref, st_ref, state, qf_s, kk_s, lf_s):
        @pl.when(pl.program_id(0) == 0)
        def _():
            state[...] = jnp.zeros_like(state)

        lb = _lower_bound(lg_ref)
        f = lb + (1.0 - lb) * _sigmoid(f_ref[...])
        kk_s[...] = 1.0 - f
        lf_s[...] = jnp.log(f)
        qf_s[...] = _silu(q_ref[...])
        causal, _, tri, _ = _chunk_masks()

        def chunk(ci, carry):
            rows = pl.ds(pl.multiple_of(ci * HG_CHUNK, HG_CHUNK), HG_CHUNK)
            srows = pl.ds(pl.multiple_of(ci * HG_DIM, HG_DIM), HG_DIM)
            lf = lf_s[rows, :]
            b = _dot_f32(tri, lf)
            bl = _rowsum(lf)
            ref = 0.5 * bl
            qf, kk, v = qf_s[rows, :], kk_s[rows, :], i_ref[rows, :]
            a_in = (qf * jnp.exp(b)).astype(BF16)
            a_t = (qf * jnp.exp(b - ref)).astype(BF16)
            b_t = (kk * jnp.exp(ref - b)).astype(BF16)
            kd = kk * jnp.exp(bl - b)
            ebl = jnp.exp(bl)
            vb = v.astype(BF16)
            for h in range(HG_HEADS):
                c = slice(h * HG_DIM, (h + 1) * HG_DIM)
                st = state[h]
                st_ref[srows, c] = st
                p = jnp.where(causal, _dot_nt(a_t[:, c], b_t[:, c]), 0.0)
                o_ref[rows, c] = _dot(p.astype(BF16), vb[:, c]) + _dot_nt(a_in[:, c], st.astype(BF16))
                state[h] = st * ebl[:, c] + _dot_tn(vb[:, c], kd[:, c].astype(BF16))
            return carry

        lax.fori_loop(0, HG_TILE_CHUNKS, chunk, 0)
        for h in range(HG_HEADS):
            c = slice(h * HG_DIM, (h + 1) * HG_DIM)
            ohat, _ = _rms(o_ref[:, c])
            out_ref[:, c] = (ohat * og_ref[...] * _silu(g_ref[:, c])).astype(BF16)

    tile = _rows(HG_TILE, HG_WIDTH)
    return pl.pallas_call(
        body, name="hg_fwd", grid=(n_tiles,),
        out_shape=[jax.ShapeDtypeStruct((s, HG_WIDTH), BF16), jax.ShapeDtypeStruct((s, HG_WIDTH), F32),
                   jax.ShapeDtypeStruct((s // HG_CHUNK * HG_DIM, HG_WIDTH), F32)],
        in_specs=[tile] * 4 + [_whole((2, HG_WIDTH)), _whole((1, HG_DIM))],
        out_specs=[tile, tile, _rows(HG_TILE_CHUNKS * HG_DIM, HG_WIDTH)],
        scratch_shapes=[pltpu.VMEM((HG_HEADS, HG_DIM, HG_DIM), F32)] + [pltpu.VMEM((HG_TILE, HG_WIDTH), F32)] * 3,
        compiler_params=_params("arbitrary"),
    )(hq, hf, hi, hgt, logits, onorm_g)


def _hg_bwd(hq, hf, hi, hgt, logits, onorm_g, o, states, dout):
    s = hq.shape[0]
    n_tiles = s // HG_TILE

    def body(q_ref, f_ref, i_ref, g_ref, lg_ref, og_ref, o_ref, st_ref, d_ref,
             dq_ref, df_ref, di_ref, dg_ref, dog_ref, dlb_ref, dstate, qf_s, kk_s, lf_s, do_s):
        @pl.when(pl.program_id(0) == 0)
        def _():
            dstate[...] = jnp.zeros_like(dstate)
            dog_ref[...] = jnp.zeros_like(dog_ref)
            dlb_ref[...] = jnp.zeros_like(dlb_ref)

        og = og_ref[...]
        dog = jnp.zeros((1, HG_DIM), F32)
        for h in range(HG_HEADS):
            c = slice(h * HG_DIM, (h + 1) * HG_DIM)
            ohat, rstd = _rms(o_ref[:, c])
            gate = g_ref[:, c]
            d = d_ref[:, c]
            dg_ref[:, c] = d * (ohat * og) * _dsilu(gate)
            dnormed = d * _silu(gate)
            dog += _rowsum(dnormed * ohat)
            do_s[:, c] = _rms_bwd(dnormed * og, ohat, rstd)
        dog_ref[...] += dog

        lb = _lower_bound(lg_ref)
        f = lb + (1.0 - lb) * _sigmoid(f_ref[...])
        kk_s[...] = 1.0 - f
        lf_s[...] = jnp.log(f)
        qf_s[...] = _silu(q_ref[...])
        causal, upper, tri, tri_t = _chunk_masks()

        def chunk(step, carry):
            ci = HG_TILE_CHUNKS - 1 - step
            rows = pl.ds(pl.multiple_of(ci * HG_CHUNK, HG_CHUNK), HG_CHUNK)
            srows = pl.ds(pl.multiple_of(ci * HG_DIM, HG_DIM), HG_DIM)
            lf = lf_s[rows, :]
            b = _dot_f32(tri, lf)
            bl = _rowsum(lf)
            ref = 0.5 * bl
            qf, kk, v, do = qf_s[rows, :], kk_s[rows, :], i_ref[rows, :], do_s[rows, :]
            eb, ebr, erb, ekd, ebl = jnp.exp(b), jnp.exp(b - ref), jnp.exp(ref - b), jnp.exp(bl - b), jnp.exp(bl)
            a_in, a_t, b_t, kd = qf * eb, qf * ebr, kk * erb, kk * ekd
            for h in range(HG_HEADS):
                c = slice(h * HG_DIM, (h + 1) * HG_DIM)
                st, dst = st_ref[srows, c], dstate[h]
                stb, dstb = st.astype(BF16), dst.astype(BF16)
                doh, vh = do[:, c], v[:, c]
                dob, vb = doh.astype(BF16), vh.astype(BF16)
                ain_h, at_h, bt_h, kd_h = a_in[:, c], a_t[:, c], b_t[:, c], kd[:, c]
                atb, btb = at_h.astype(BF16), bt_h.astype(BF16)
                d_ain = _dot(dob, stb)
                p_t = jnp.where(upper, _dot_nt(btb, atb), 0.0).astype(BF16)
                dp = jnp.where(causal, _dot_nt(dob, vb), 0.0).astype(BF16)
                dp_t = jnp.where(upper, _dot_nt(vb, dob), 0.0).astype(BF16)
                di_ref[rows, c] = _dot(p_t, dob) + _dot_nt(kd_h.astype(BF16), dstb)
                d_at = _dot(dp, btb)
                d_bt = _dot(dp_t, atb)
                d_kd = _dot(vb, dstb)
                dqf = d_ain * eb[:, c] + d_at * ebr[:, c]
                dkk = d_bt * erb[:, c] + d_kd * ekd[:, c]
                db = d_ain * ain_h + d_at * atb.astype(F32) - d_bt * btb.astype(F32) - d_kd * kd_h
                dbl = _rowsum(d_kd * kd_h) + _rowsum(dst * st) * ebl[:, c]
                dstate[h] = _dot_tn(dob, ain_h.astype(BF16)) + dst * ebl[:, c]
                dlf = _dot_f32(tri_t, db) + dbl
                qv, fr = q_ref[rows, c], f_ref[rows, c]
                lbh = lb[:, c]
                sg = _sigmoid(fr)
                dfv = dlf / (lbh + (1.0 - lbh) * sg) - dkk
                df_ref[rows, c] = dfv * (1.0 - lbh) * sg * (1.0 - sg)
                dlb_ref[:, c] += _rowsum(dfv * (1.0 - sg))
                dq_ref[rows, c] = dqf * _dsilu(qv)
            return carry

        lax.fori_loop(0, HG_TILE_CHUNKS, chunk, 0)

    rev = pl.BlockSpec((HG_TILE, HG_WIDTH), lambda i: (n_tiles - 1 - i, 0))
    return pl.pallas_call(
        body, name="hg_bwd", grid=(n_tiles,),
        out_shape=[jax.ShapeDtypeStruct((s, HG_WIDTH), F32)] * 4
        + [jax.ShapeDtypeStruct((1, HG_DIM), F32), jax.ShapeDtypeStruct((1, HG_WIDTH), F32)],
        in_specs=[rev] * 4 + [_whole((2, HG_WIDTH)), _whole((1, HG_DIM)), rev,
                              pl.BlockSpec((HG_TILE_CHUNKS * HG_DIM, HG_WIDTH), lambda i: (n_tiles - 1 - i, 0)), rev],
        out_specs=[rev] * 4 + [_whole((1, HG_DIM)), _whole((1, HG_WIDTH))],
        scratch_shapes=[pltpu.VMEM((HG_HEADS, HG_DIM, HG_DIM), F32)] + [pltpu.VMEM((HG_TILE, HG_WIDTH), F32)] * 4,
        compiler_params=_params("arbitrary"),
    )(hq, hf, hi, hgt, logits, onorm_g, o, states, dout)


TOKEN_GROUP = 16


def _att_geometry(dil):
    per_group = TOKEN_GROUP // dil
    return per_group, ATT_BLOCK // per_group, ATT_WIDTH if dil == 1 else 128


def _att_consts(dil):
    per_group, ub, _ = _att_geometry(dil)

    def pos(i):
        return i if dil == 1 else (i % ub) * per_group + i // ub

    lane = lax.broadcasted_iota(jnp.int32, (ATT_BLOCK, 128), 1)
    qi = pos(lax.broadcasted_iota(jnp.int32, (2 * ATT_BLOCK, ATT_BLOCK), 0) % ATT_BLOCK)
    kj = pos(lax.broadcasted_iota(jnp.int32, (2 * ATT_BLOCK, ATT_BLOCK), 1))
    return lane < ATT_HEAD_DIM, kj <= qi, lambda off: kj >= qi + off


def _load_tile(ref, dil, r, c):
    if dil == 1:
        return ref[:, c]
    per_group, ub, _ = _att_geometry(dil)
    return jnp.concatenate([ref[pl.ds(dil * w + r, ub, stride=TOKEN_GROUP), c] for w in range(per_group)], axis=0)


def _store_tile(ref, dil, r, c, val):
    if dil == 1:
        ref[:, c] = val
        return
    per_group, ub, _ = _att_geometry(dil)
    for w in range(per_group):
        ref[pl.ds(dil * w + r, ub, stride=TOKEN_GROUP), c] = val[w * ub:(w + 1) * ub]


def _att_specs(seq, dil):
    _, ub, lanes = _att_geometry(dil)
    rows = ub * TOKEN_GROUP
    nb = seq // rows
    cur = pl.BlockSpec((rows, lanes), lambda n, j: (n, j))
    prev = pl.BlockSpec((rows, lanes), lambda n, j: (jnp.maximum(n - 1, 0), j))
    nxt = pl.BlockSpec((rows, lanes), lambda n, j: (jnp.minimum(n + 1, nb - 1), j))
    return nb, lanes, cur, prev, nxt


def _stack_heads(x2, first):
    return jnp.concatenate([jnp.where(first, x2, 0.0), jnp.where(first, 0.0, x2)], axis=0)


def _stack_bcast(x2, first):
    other = pltpu.roll(x2, ATT_HEAD_DIM, axis=1)
    return jnp.concatenate([jnp.where(first, x2, other), jnp.where(first, other, x2)], axis=0)


def _unstack_heads(st, first):
    return jnp.where(first, st[:ATT_BLOCK], st[ATT_BLOCK:])


def _att_fwd(q, k, v, dil):
    seq, width = q.shape
    nb, lanes, cur, prev, _ = _att_specs(seq, dil)

    def body(q_ref, kc_ref, kp_ref, vc_ref, vp_ref, o_ref, lse_ref):
        first, cur_ok, _band = _att_consts(dil)
        prev_ok = _band(jnp.where(pl.program_id(0) > 0, 0, ATT_BLOCK))
        for r in range(dil):
            for j in range(lanes // 128):
                c = slice(j * 128, (j + 1) * 128)
                qst = _stack_heads(_load_tile(q_ref, dil, r, c) * ATT_SCALE, first).astype(BF16)
                kc, kp = _load_tile(kc_ref, dil, r, c).astype(BF16), _load_tile(kp_ref, dil, r, c).astype(BF16)
                vc, vp = _load_tile(vc_ref, dil, r, c).astype(BF16), _load_tile(vp_ref, dil, r, c).astype(BF16)
                sc = jnp.where(cur_ok, _dot_nt(qst, kc), NEG)
                sp = jnp.where(prev_ok, _dot_nt(qst, kp), NEG)
                mx = jnp.maximum(jnp.max(sc, axis=-1, keepdims=True), jnp.max(sp, axis=-1, keepdims=True))
                pc, pp = jnp.exp(sc - mx), jnp.exp(sp - mx)
                den = jnp.sum(pc, axis=-1, keepdims=True) + jnp.sum(pp, axis=-1, keepdims=True)
                ost = (_dot(pc.astype(BF16), vc) + _dot(pp.astype(BF16), vp)) / den
                lse = jnp.broadcast_to(mx + jnp.log(den), (2 * ATT_BLOCK, 128))
                _store_tile(o_ref, dil, r, c, _unstack_heads(ost, first))
                _store_tile(lse_ref, dil, r, c, _unstack_heads(lse, first))

    return pl.pallas_call(
        body, name=f"att_fwd_d{dil}", grid=(nb, width // lanes),
        out_shape=[jax.ShapeDtypeStruct((seq, width), F32)] * 2,
        in_specs=[cur, cur, prev, cur, prev], out_specs=[cur, cur],
        compiler_params=_params("arbitrary", "arbitrary"),
    )(q, k, k, v, v)


def _att_bwd(q, k, v, do, cc, lse, dil):
    seq, width = q.shape
    nb, lanes, cur, prev, nxt = _att_specs(seq, dil)

    def body(q_ref, qx_ref, kc_ref, kp_ref, vc_ref, vp_ref, do_ref, dox_ref, cc_ref, ccx_ref, lse_ref, lsex_ref,
             dq_ref, dk_ref, dv_ref):
        first, cur_ok, _band = _att_consts(dil)
        n = pl.program_id(0)
        prev_ok = _band(jnp.where(n > 0, 0, ATT_BLOCK))
        next_ok = _band(jnp.where(n < nb - 1, 0, ATT_BLOCK))
        for r in range(dil):
            for j in range(lanes // 128):
                c = slice(j * 128, (j + 1) * 128)
                qst = _stack_heads(_load_tile(q_ref, dil, r, c) * ATT_SCALE, first).astype(BF16)
                qxst = _stack_heads(_load_tile(qx_ref, dil, r, c) * ATT_SCALE, first).astype(BF16)
                dost = _stack_heads(_load_tile(do_ref, dil, r, c), first).astype(BF16)
                doxst = _stack_heads(_load_tile(dox_ref, dil, r, c), first).astype(BF16)
                lse_n = _stack_bcast(_load_tile(lse_ref, dil, r, c), first)
                lse_x = _stack_bcast(_load_tile(lsex_ref, dil, r, c), first)
                cc_n = _stack_bcast(_load_tile(cc_ref, dil, r, c), first)
                cc_x = _stack_bcast(_load_tile(ccx_ref, dil, r, c), first)
                kc, kp = _load_tile(kc_ref, dil, r, c).astype(BF16), _load_tile(kp_ref, dil, r, c).astype(BF16)
                vc, vp = _load_tile(vc_ref, dil, r, c).astype(BF16), _load_tile(vp_ref, dil, r, c).astype(BF16)
                p_cur = jnp.exp(jnp.where(cur_ok, _dot_nt(qst, kc), NEG) - lse_n)
                p_prev = jnp.exp(jnp.where(prev_ok, _dot_nt(qst, kp), NEG) - lse_n)
                p_next = jnp.exp(jnp.where(next_ok, _dot_nt(qxst, kc), NEG) - lse_x)
                ds_cur = (p_cur * (_dot_nt(dost, vc) + cc_n)).astype(BF16)
                ds_prev = (p_prev * (_dot_nt(dost, vp) + cc_n)).astype(BF16)
                ds_next = (p_next * (_dot_nt(doxst, vc) + cc_x)).astype(BF16)
                _store_tile(dq_ref, dil, r, c, _unstack_heads(_dot(ds_cur, kc) + _dot(ds_prev, kp), first) * ATT_SCALE)
                _store_tile(dk_ref, dil, r, c, _dot_tn(ds_cur, qst) + _dot_tn(ds_next, qxst))
                _store_tile(dv_ref, dil, r, c, _dot_tn(p_cur.astype(BF16), dost) + _dot_tn(p_next.astype(BF16), doxst))

    return pl.pallas_call(
        body, name=f"att_bwd_d{dil}", grid=(nb, width // lanes),
        out_shape=[jax.ShapeDtypeStruct((seq, width), F32)] * 3,
        in_specs=[cur, nxt, cur, prev, cur, prev, cur, nxt, cur, nxt, cur, nxt], out_specs=[cur, cur, cur],
        compiler_params=_params("arbitrary", "arbitrary"),
    )(q, q, k, k, v, v, do, do, cc, cc, lse, lse)


def _branch_weights(lses):
    mx = jnp.maximum(jnp.maximum(lses[0], lses[1]), lses[2])
    es = [jnp.exp(l - mx) for l in lses]
    inv = 1.0 / (es[0] + es[1] + es[2])
    return [e * inv for e in es]


def _att_combine(outs, lses, att_g):
    s = outs[0].shape[0]
    tm = 512

    def body(o0, o1, o2, l0, l1, l2, g_ref, att_ref, out_ref):
        ws = _branch_weights([l0[...], l1[...], l2[...]])
        att = ws[0] * o0[...] + ws[1] * o1[...] + ws[2] * o2[...]
        att_ref[...] = att
        ahat, _ = _rms(att)
        out_ref[...] = (ahat * g_ref[...]).astype(BF16)

    tile = _rows(tm, ATT_WIDTH)
    return pl.pallas_call(
        body, name="att_combine", grid=(s // tm,),
        out_shape=[jax.ShapeDtypeStruct((s, ATT_WIDTH), F32), jax.ShapeDtypeStruct((s, ATT_WIDTH), BF16)],
        in_specs=[tile] * 6 + [_whole((1, ATT_WIDTH))], out_specs=[tile, tile],
        compiler_params=_params("parallel"),
    )(*outs, *lses, att_g)


def _att_combine_bwd(datt_out, att, lses, att_g):
    s = att.shape[0]
    tm = 256

    def body(d_ref, att_ref, l0, l1, l2, g_ref, do0, do1, do2, cc0, cc1, cc2, dg_ref):
        @pl.when(pl.program_id(0) == 0)
        def _():
            dg_ref[...] = jnp.zeros_like(dg_ref)

        att = att_ref[...]
        ahat, rstd = _rms(att)
        d = d_ref[...]
        dg_ref[...] += _rowsum(d * ahat)
        datt = _rms_bwd(d * g_ref[...], ahat, rstd)
        hi = lax.broadcasted_iota(jnp.int32, (ATT_WIDTH, ATT_WIDTH), 0) // ATT_HEAD_DIM
        hj = lax.broadcasted_iota(jnp.int32, (ATT_WIDTH, ATT_WIDTH), 1) // ATT_HEAD_DIM
        head_sum = _dot_f32(datt * att, (hi == hj).astype(F32))
        ws = _branch_weights([l0[...], l1[...], l2[...]])
        for w, do_ref, cc_ref in zip(ws, (do0, do1, do2), (cc0, cc1, cc2)):
            do_ref[...] = w * datt
            cc_ref[...] = -w * head_sum

    tile = _rows(tm, ATT_WIDTH)
    return pl.pallas_call(
        body, name="att_combine_bwd", grid=(s // tm,),
        out_shape=[jax.ShapeDtypeStruct((s, ATT_WIDTH), F32)] * 6 + [jax.ShapeDtypeStruct((1, ATT_WIDTH), F32)],
        in_specs=[tile] * 5 + [_whole((1, ATT_WIDTH))], out_specs=[tile] * 6 + [_whole((1, ATT_WIDTH))],
        compiler_params=_params("arbitrary"),
    )(datt_out, att, *lses, att_g)


def _out_fwd(x, hg, at, mod, w_out):
    s = x.shape[0]
    tm = 512

    def body(x_ref, hg_ref, at_ref, mod_ref, w_ref, x1_ref):
        mix = _dot(hg_ref[...], w_ref[0:512, :]) + _dot(at_ref[...], w_ref[512:1024, :])
        x1_ref[...] = x_ref[...] + mod_ref[:, 2 * D_MODEL:3 * D_MODEL] * mix

    return pl.pallas_call(
        body, name="out_fwd", grid=(s // tm,), out_shape=jax.ShapeDtypeStruct((s, D_MODEL), F32),
        in_specs=[_rows(tm, D_MODEL), _rows(tm, 512), _rows(tm, 512), _whole((1, 6 * D_MODEL)), _whole((D_MODEL, D_MODEL))],
        out_specs=_rows(tm, D_MODEL), compiler_params=_params("parallel"),
    )(x, hg, at, mod, w_out)


def _out_bwd(dx1, hg, at, mod, w_out):
    s = dx1.shape[0]
    tm = 512

    def body(dx_ref, hg_ref, at_ref, mod_ref, w_ref, dhg_ref, dat_ref, dw_ref, dgate_ref):
        @pl.when(pl.program_id(0) == 0)
        def _():
            dw_ref[...] = jnp.zeros_like(dw_ref)
            dgate_ref[...] = jnp.zeros_like(dgate_ref)

        hg, at, dx = hg_ref[...], at_ref[...], dx_ref[...]
        mix = _dot(hg, w_ref[0:512, :]) + _dot(at, w_ref[512:1024, :])
        dgate_ref[...] += _rowsum(dx * mix)
        dmix = (mod_ref[:, 2 * D_MODEL:3 * D_MODEL] * dx).astype(BF16)
        dhg_ref[...] = _dot_nt(dmix, w_ref[0:512, :])
        dat_ref[...] = _dot_nt(dmix, w_ref[512:1024, :])
        dw_ref[0:512, :] += _dot_tn(hg, dmix)
        dw_ref[512:1024, :] += _dot_tn(at, dmix)

    return pl.pallas_call(
        body, name="out_bwd", grid=(s // tm,),
        out_shape=[jax.ShapeDtypeStruct((s, 512), F32)] * 2
        + [jax.ShapeDtypeStruct((D_MODEL, D_MODEL), F32), jax.ShapeDtypeStruct((1, D_MODEL), F32)],
        in_specs=[_rows(tm, D_MODEL), _rows(tm, 512), _rows(tm, 512), _whole((1, 6 * D_MODEL)), _whole((D_MODEL, D_MODEL))],
        out_specs=[_rows(tm, 512), _rows(tm, 512), _whole((D_MODEL, D_MODEL)), _whole((1, D_MODEL))],
        compiler_params=_params("arbitrary"),
    )(dx1, hg, at, mod, w_out)


FFN_CHUNK = 256


def _ffn(x1, target, mod, g2, gf, w_gu, w_down):
    s = x1.shape[0]
    tm = 256
    n_chunks = D_FF // FFN_CHUNK

    def body(x_ref, t_ref, mod_ref, g2_ref, gf_ref, wgu_hbm, wd_hbm,
             dx_ref, h2_ref, act_ref, dau_ref, dff_ref, sums_ref, loss_ref, wgu, wd, a_s, u_s, sem):
        @pl.when(pl.program_id(0) == 0)
        def _():
            c1 = pltpu.make_async_copy(wgu_hbm, wgu, sem.at[0])
            c2 = pltpu.make_async_copy(wd_hbm, wd, sem.at[1])
            c1.start()
            c2.start()
            c1.wait()
            c2.wait()
            sums_ref[...] = jnp.zeros_like(sums_ref)
            loss_ref[...] = jnp.zeros_like(loss_ref)

        x1v = x_ref[...]
        xhat, rstd = _rms(x1v)
        g2 = g2_ref[...]
        n2 = xhat * g2
        scale2 = 1.0 + mod_ref[:, 4 * D_MODEL:5 * D_MODEL]
        gate2 = mod_ref[:, 5 * D_MODEL:6 * D_MODEL]
        hb = (n2 * scale2 + mod_ref[:, 3 * D_MODEL:4 * D_MODEL]).astype(BF16)
        h2_ref[...] = hb
        ff = jnp.zeros((tm, D_MODEL), F32)
        for j in range(n_chunks):
            c = slice(j * FFN_CHUNK, (j + 1) * FFN_CHUNK)
            cu = slice(D_FF + j * FFN_CHUNK, D_FF + (j + 1) * FFN_CHUNK)
            a = _dot(hb, wgu[:, c])
            u = _dot(hb, wgu[:, cu])
            a_s[:, c] = a
            u_s[:, c] = u
            act = (_silu(a) * u).astype(BF16)
            act_ref[:, c] = act
            ff += _dot(act, wd[c, :])
        x2 = x1v + gate2 * ff
        nf, rstd_f = _rms(x2)
        gfv = gf_ref[...]
        err = nf * gfv - t_ref[...]
        loss_ref[...] += 0.5 * jnp.sum(_rowsum(err * err), axis=-1, keepdims=True) * (1.0 / D_MODEL)
        dy = err * (1.0 / D_MODEL)
        dx2 = _rms_bwd(dy * gfv, nf, rstd_f)
        dffb = (gate2 * dx2).astype(BF16)
        dff_ref[...] = dffb
        dh = jnp.zeros((tm, D_MODEL), F32)
        for j in range(n_chunks):
            c = slice(j * FFN_CHUNK, (j + 1) * FFN_CHUNK)
            cu = slice(D_FF + j * FFN_CHUNK, D_FF + (j + 1) * FFN_CHUNK)
            dact = _dot_nt(dffb, wd[c, :])
            a, u = a_s[:, c], u_s[:, c]
            da = (dact * u * _dsilu(a)).astype(BF16)
            du = (dact * _silu(a)).astype(BF16)
            dau_ref[:, c] = da
            dau_ref[:, cu] = du
            dh += _dot_nt(da, wgu[:, c]) + _dot_nt(du, wgu[:, cu])
        dn = dh * scale2
        sums_ref[0:1, :] += _rowsum(dh)
        sums_ref[1:2, :] += _rowsum(dh * n2)
        sums_ref[2:3, :] += _rowsum(dx2 * ff)
        sums_ref[3:4, :] += _rowsum(dn * xhat)
        sums_ref[4:5, :] += _rowsum(dy * nf)
        dx_ref[...] = dx2 + _rms_bwd(dn * g2, xhat, rstd)

    vec = _whole((1, D_MODEL))
    hbm = pl.BlockSpec(memory_space=pl.ANY)
    return pl.pallas_call(
        body, name="ffn", grid=(s // tm,),
        out_shape=[jax.ShapeDtypeStruct((s, D_MODEL), F32), jax.ShapeDtypeStruct((s, D_MODEL), BF16),
                   jax.ShapeDtypeStruct((s, D_FF), BF16), jax.ShapeDtypeStruct((s, 2 * D_FF), BF16),
                   jax.ShapeDtypeStruct((s, D_MODEL), BF16), jax.ShapeDtypeStruct((8, D_MODEL), F32),
                   jax.ShapeDtypeStruct((1, 128), F32)],
        in_specs=[_rows(tm, D_MODEL), _rows(tm, D_MODEL), _whole((1, 6 * D_MODEL)), vec, vec, hbm, hbm],
        out_specs=[_rows(tm, D_MODEL), _rows(tm, D_MODEL), _rows(tm, D_FF), _rows(tm, 2 * D_FF), _rows(tm, D_MODEL),
                   _whole((8, D_MODEL)), _whole((1, 128))],
        scratch_shapes=[pltpu.VMEM((D_MODEL, 2 * D_FF), BF16), pltpu.VMEM((D_FF, D_MODEL), BF16),
                        pltpu.VMEM((tm, D_FF), F32), pltpu.VMEM((tm, D_FF), F32), pltpu.SemaphoreType.DMA((2,))],
        compiler_params=_params("arbitrary"),
    )(x1, target, mod, g2, gf, w_gu, w_down)


def _weight_grad(a, b, name):
    s, m = a.shape
    n = b.shape[1]
    ts, tn = min(s, 2048), 512

    def body(a_ref, b_ref, o_ref):
        @pl.when(pl.program_id(1) == 0)
        def _():
            o_ref[...] = jnp.zeros_like(o_ref)

        o_ref[...] += _dot_tn(a_ref[...], b_ref[...])

    return pl.pallas_call(
        body, name=name, grid=(n // tn, s // ts), out_shape=jax.ShapeDtypeStruct((m, n), F32),
        in_specs=[pl.BlockSpec((ts, m), lambda j, i: (i, 0)), pl.BlockSpec((ts, tn), lambda j, i: (i, j))],
        out_specs=pl.BlockSpec((m, tn), lambda j, i: (0, j)),
        compiler_params=_params("parallel", "arbitrary"),
    )(a, b)


def _adamw_math(w, g, m, v):
    m = ADAM_B1 * m + (1.0 - ADAM_B1) * g
    v = ADAM_B2 * v + (1.0 - ADAM_B2) * (g * g)
    m_hat = m / (1.0 - ADAM_B1 ** ADAM_STEP)
    v_hat = v / (1.0 - ADAM_B2 ** ADAM_STEP)
    delta = -ADAM_LR * (m_hat / (jnp.sqrt(v_hat) + ADAM_EPS) + ADAM_WD * w)
    return delta, m, v


def _adamw_shard(chip, w, m, v, partial, got, name):
    r, c = w.shape
    tr = _shard_rows(r)

    def body(chip_ref, w_ref, m_ref, v_ref, own_ref, g0, g1, g2, grad_ref, d_ref, nm_ref, nv_ref):
        g = ((own_ref[...] + g0[...].astype(F32)) + g1[...].astype(F32)) + g2[...].astype(F32)
        grad_ref[...] = g
        d_ref[...], nm_ref[...], nv_ref[...] = _adamw_math(w_ref[...], g, m_ref[...], v_ref[...])

    tile = pl.BlockSpec((tr, c), lambda i, chip_ref: (i, 0))
    own = pl.BlockSpec((None, tr, c), lambda i, chip_ref: (chip_ref[0], i, 0))
    part = [pl.BlockSpec((None, tr, c), functools.partial(lambda j, i, chip_ref: (j, i, 0), j)) for j in range(3)]
    return pl.pallas_call(
        body, name=name,
        grid_spec=pltpu.PrefetchScalarGridSpec(num_scalar_prefetch=1, grid=(r // tr,), in_specs=[tile] * 3 + [own] + part,
                                               out_specs=[tile] * 4),
        out_shape=[jax.ShapeDtypeStruct((r, c), F32)] * 4, compiler_params=_params("parallel"),
    )(chip, w, m, v, partial, got, got, got)


def _small_update(small_all, dmod_blocks, c_all, logits, w_ada, m_ada, v_ada, smalls):
    def body(sm_ref, dm_ref, c_ref, lg_ref, wa_ref, ma_ref, va_ref, *rest):
        ins, outs = rest[:21], rest[21:]
        _, me = _flip(0)
        tot = sm_ref[0:1, :]
        for i in range(1, N_DEV):
            tot = tot + sm_ref[i:i + 1, :]
        loss_ref = outs[0]
        loss_ref[...] = tot[:, SM_LOSS:SM_LOSS + 128]
        g_ada = lax.dot_general(_silu(c_ref[...]), dm_ref[me], (((0,), (0,)), ((), ())),
                                preferred_element_type=F32, precision=HIGHEST)
        outs[1][...] = g_ada
        outs[2][...], outs[3][...], outs[4][...] = _adamw_math(wa_ref[...], g_ada, ma_ref[...], va_ref[...])
        p0 = _lower_bound(lg_ref)
        dl0 = tot[:, SM_LB:SM_LB + 512] * p0 * (1.0 - p0)
        grads = [tot[:, SM_MOD:SM_MOD + 6 * D_MODEL], tot[:, SM_G1:SM_G1 + D_MODEL], tot[:, SM_G2:SM_G2 + D_MODEL],
                 tot[:, SM_GF:SM_GF + D_MODEL], tot[:, SM_ATT:SM_ATT + 512], tot[:, SM_HG:SM_HG + 128],
                 jnp.where(lax.broadcasted_iota(jnp.int32, (2, 512), 0) == 0, dl0, -dl0)]
        for i, g in enumerate(grads):
            w_ref, m_ref, v_ref = ins[3 * i:3 * i + 3]
            o = outs[5 + 4 * i:9 + 4 * i]
            o[0][...] = g
            o[1][...], o[2][...], o[3][...] = _adamw_math(w_ref[...], g, m_ref[...], v_ref[...])

    flat = [t for trio in smalls for t in trio]
    vm = pl.BlockSpec(memory_space=pltpu.VMEM)
    out_shape = [jax.ShapeDtypeStruct((1, 128), F32)] + [jax.ShapeDtypeStruct(w_ada.shape, F32)] * 4
    for trio in smalls:
        out_shape += [jax.ShapeDtypeStruct(trio[0].shape, F32)] * 4
    return pl.pallas_call(
        body, name="small_update", out_shape=out_shape,
        in_specs=[vm] * (7 + len(flat)), out_specs=[vm] * len(out_shape),
        compiler_params=pltpu.CompilerParams(vmem_limit_bytes=V7X_VMEM_LIMIT),
    )(small_all, dmod_blocks, c_all, logits, w_ada, m_ada, v_ada, *flat)


def kernel(x, c, w_ada, b_ada, norm1_g, w_in, hg_lb_logits, hg_onorm_g, att_onorm_g, w_out, norm2_g, w_gate_up, w_down, final_g, loss_target, m_w_ada, m_b_ada, m_norm1_g, m_w_in, m_hg_lb_logits, m_hg_onorm_g, m_att_onorm_g, m_w_out, m_norm2_g, m_w_gate_up, m_w_down, m_final_g, v_w_ada, v_b_ada, v_norm1_g, v_w_in, v_hg_lb_logits, v_hg_onorm_g, v_att_onorm_g, v_w_out, v_norm2_g, v_w_gate_up, v_w_down, v_final_g):
    x2d, target = x[0], loss_target[0]
    seq = x2d.shape[0]
    assert seq % (ATT_BLOCK * max(DILATIONS)) == 0 and seq % HG_TILE == 0
    gf = final_g.reshape(1, D_MODEL)

    c_all = _exchange_small(c.reshape(8, D_MODEL // 8), None, "gather_c").reshape(N_DEV, D_MODEL)
    ada = _ada_rows(c_all, w_ada[0], b_ada)
    mod = _exchange_small(ada, 1, "scatter_mod").reshape(1, 6 * D_MODEL)

    core = lax.axis_index("c").astype(jnp.int32).reshape(1)
    chip = (2 * lax.axis_index("x") + lax.axis_index("y")).astype(jnp.int32).reshape(1)
    me = 4 * lax.axis_index("x") + 2 * lax.axis_index("y") + lax.axis_index("c")

    g_in, = _gather_weights([w_in[0].astype(BF16)])
    w_in_b = jnp.transpose(g_in, (1, 0, 2)).reshape(D_MODEL, IN_WIDTH)
    rest_shards = [w_out[0].astype(BF16), w_gate_up[0].astype(BF16), w_down[0].astype(BF16)]
    lands = [lax.empty((N_DEV,) + s.shape, BF16) for s in rest_shards]
    g_send, g_recv, g_srcs, g_lands, tok = _copies_start("gather_rest_start", _plan_gather_own, 12, rest_shards, lands, [w_in_b, mod])
    flight = {}

    def stage(name, *vals):
        if name == "attention_begun":
            flight["shards"], got = _copies_wait("gather_rest_wait", _plan_gather_own, g_send, g_recv, g_srcs, g_lands, [vals[0]])
            flight["pass"] = _copies_start("gather_pass_start", _plan_gather_pass, 9, [], got, [])
            return flight["pass"][4][0:1, 0:1]
        if name == "mixer_weights_done":
            dw_gu, dw_down, dw_out = vals
            flight["grads"] = [dw_out.reshape(4, 2, D_MODEL // N_DEV, D_MODEL),
                               dw_gu.reshape(D_MODEL, 4, 2, 2 * D_FF // N_DEV).transpose(1, 2, 0, 3),
                               dw_down.reshape(4, 2, D_FF // N_DEV, D_MODEL)]
            pair_lands = [lax.empty((4,) + g.shape[2:], F32) for g in flight["grads"]]
            flight["pairs"] = _copies_start("reduce_pairs_start", _plan_reduce_pairs, 12, flight["grads"], pair_lands, [])
            return flight["pairs"][4][0:1, 0:1]
        if name == "attention_backward_begun":
            s, r, srcs, pl_lands, _ = flight["pairs"]
            grads, got = _copies_wait("reduce_pairs_wait", _plan_reduce_pairs, s, r, srcs, pl_lands, [vals[0]])
            flight["sums"] = [_pair_sum(core, g, b, f"pair_sum_{i}") for i, (g, b) in enumerate(zip(grads, got))]
            chip_lands = [lax.empty((3,) + s16.shape[1:], BF16) for _, s16 in flight["sums"]]
            flight["chips"] = _copies_start("reduce_chips_start", _plan_reduce_chips, 9, [s16 for _, s16 in flight["sums"]], chip_lands, [])
            return flight["chips"][4][0:1, 0:1]
        raise ValueError(name)

    def rest_weights(after):
        s, r, _, p_lands, _ = flight["pass"]
        _, got = _copies_wait("gather_pass_wait", _plan_gather_pass, s, r, [], p_lands, [after])
        full = [lax.dynamic_update_index_in_dim(g, shard, me, 0) for g, shard in zip(got, flight["shards"])]
        return (full[0].reshape(D_MODEL, D_MODEL), jnp.transpose(full[1], (1, 0, 2)).reshape(D_MODEL, 2 * D_FF),
                full[2].reshape(D_FF, D_MODEL))

    grad_x, dw_in, small = _block_step(x2d, target, mod + tok[0:1, 0:1], norm1_g, hg_lb_logits, hg_onorm_g, att_onorm_g, norm2_g, gf,
                                       w_in_b, rest_weights, stage)

    g_in8 = dw_in.reshape(D_MODEL, 4, 2, IN_WIDTH // N_DEV).transpose(1, 2, 0, 3)
    in_pairs = _copies_start("reduce_pairs_in_start", _plan_reduce_pairs, 4, [g_in8], [lax.empty((4,) + g_in8.shape[2:], F32)], [])
    s, r, srcs, c_lands, _ = flight["chips"]
    _, recv_rest = _copies_wait("reduce_chips_wait", _plan_reduce_chips, s, r, srcs, c_lands, [in_pairs[4]])
    small_rows = jnp.pad(small + in_pairs[4][0:1, 0:1], ((0, 0), (0, SM_PADDED - SM_WIDTH))).reshape(SM_PADDED // 128, 128)
    small_all = _exchange_small(small_rows, None, "gather_small").reshape(N_DEV, SM_PADDED)[:, :SM_WIDTH]
    big = {}
    rest_params = [("w_out", w_out, m_w_out, v_w_out), ("w_gate_up", w_gate_up, m_w_gate_up, v_w_gate_up), ("w_down", w_down, m_w_down, v_w_down)]
    for (n, w, m, v), (s32, _), got in zip(rest_params, flight["sums"], recv_rest):
        big[n] = [t[None] for t in _adamw_shard(chip, w[0], m[0], v[0], s32, got, f"adamw_{n}")]
    in_grads, got_in = _copies_wait("reduce_pairs_in_wait", _plan_reduce_pairs, in_pairs[0], in_pairs[1], in_pairs[2], in_pairs[3],
                                    [big[n][3] for n, _, _, _ in rest_params] + [small_all])
    in_s32, in_s16 = _pair_sum(core, in_grads[0], got_in[0], "pair_sum_in")
    in_chips = _copies_start("reduce_chips_in_start", _plan_reduce_chips, 3, [in_s16], [lax.empty((3,) + in_s16.shape[1:], BF16)], [])
    c_all = c_all + in_chips[4][0:1, 0:1]
    smalls = [(b_ada, m_b_ada, v_b_ada), (norm1_g, m_norm1_g, v_norm1_g), (norm2_g, m_norm2_g, v_norm2_g),
              (gf, m_final_g.reshape(1, D_MODEL), v_final_g.reshape(1, D_MODEL)),
              (att_onorm_g, m_att_onorm_g, v_att_onorm_g), (hg_onorm_g, m_hg_onorm_g, v_hg_onorm_g),
              (hg_lb_logits, m_hg_lb_logits, v_hg_lb_logits)]
    dmod_blocks = small_all[:, :6 * D_MODEL].reshape(N_DEV, N_DEV, 6 * D_MODEL // N_DEV).transpose(1, 0, 2)
    res = _small_update(small_all, dmod_blocks, c_all, hg_lb_logits, w_ada[0], m_w_ada[0], v_w_ada[0], smalls)
    _, recv_in = _copies_wait("reduce_chips_in_wait", _plan_reduce_chips, in_chips[0], in_chips[1], in_chips[2], in_chips[3], [res[0]])
    big["w_in"] = [t[None] for t in _adamw_shard(chip, w_in[0], m_w_in[0], v_w_in[0], in_s32, recv_in[0], "adamw_w_in")]
    loss = res[0][0, 0]
    ada4 = [t[None] for t in res[1:5]]
    sm4 = {n: list(res[5 + 4 * i:9 + 4 * i]) for i, n in enumerate(["b_ada", "norm1_g", "norm2_g", "final_g", "att", "hg", "lb"])}
    sm4["final_g"] = [t.reshape(D_MODEL) for t in sm4["final_g"]]

    order = [ada4, sm4["b_ada"], sm4["norm1_g"], big["w_in"], sm4["lb"], sm4["hg"], sm4["att"], big["w_out"], sm4["norm2_g"],
             big["w_gate_up"], big["w_down"], sm4["final_g"]]
    return (loss, grad_x[None], *[o[0] for o in order], *[o[1] for o in order], *[o[2] for o in order], *[o[3] for o in order])


def _block_step(x2d, target, mod, norm1_g, hg_lb_logits, hg_onorm_g, att_onorm_g, norm2_g, gf, w_in_b, rest_weights, stage):
    h1, hq, hf, hi, hgt, aq, ak, av = _in_fwd(x2d, mod, norm1_g, w_in_b)
    hg_out, hg_o, hg_states = _hg_fwd(hq, hf, hi, hgt, hg_lb_logits, hg_onorm_g)
    branch = [_att_fwd(aq, ak, av, DILATIONS[0])]
    att_g = att_onorm_g + stage("attention_begun", branch[0][0])
    branch += [_att_fwd(aq, ak, av, d) for d in DILATIONS[1:]]
    outs = [b[0] for b in branch]
    lses = [b[1] for b in branch]
    att, att_out = _att_combine(outs, lses, att_g)
    w_out_b, w_gu_b, w_down_b = rest_weights(att_out)
    x1 = _out_fwd(x2d, hg_out, att_out, mod, w_out_b)

    dx1, h2, act, dau, dff, ffn_sums, loss_part = _ffn(x1, target, mod, norm2_g, gf, w_gu_b, w_down_b)
    dw_gu = _weight_grad(h2, dau, "dw_gate_up")
    dw_down = _weight_grad(act, dff, "dw_down")

    dhg, dat, dw_out, dgate1 = _out_bwd(dx1, hg_out, att_out, mod, w_out_b)
    att_g = att_onorm_g + stage("mixer_weights_done", dw_gu, dw_down, dw_out)
    comb = _att_combine_bwd(dat, att, lses, att_g)
    dos, ccs, d_att_g = comb[0:3], comb[3:6], comb[6]
    hg_g = hg_onorm_g + stage("attention_backward_begun", comb[3])
    datt = []
    for i, d in enumerate(DILATIONS):
        datt.append(_att_bwd(aq, ak, av, dos[i], ccs[i], lses[i], d))
    dhq, dhf, dhi, dhgt, d_hg_g, d_lb = _hg_bwd(hq, hf, hi, hgt, hg_lb_logits, hg_g, hg_o, hg_states, dhg)
    dps = [dhq, dhf, dhi, dhgt] + [datt[i][j] for j in range(3) for i in range(3)]
    grad_x, dp_b, dshift1, dscale1, d_g1 = _in_bwd(x2d, dx1, mod, norm1_g, w_in_b, dps)
    dw_in = _weight_grad(h1, dp_b, "dw_in")
    small = jnp.concatenate([dshift1, dscale1, dgate1, ffn_sums[0:1], ffn_sums[1:2], ffn_sums[2:3], d_g1, ffn_sums[3:4],
                             ffn_sums[4:5], d_att_g, d_lb, d_hg_g, loss_part], axis=1)
    return grad_x, dw_in, small
```

```python
import functools

import jax
import jax.numpy as jnp
from jax import lax
from jax.experimental import pallas as pl
from jax.experimental.pallas import tpu as pltpu

F32 = jnp.float32
BF16 = jnp.bfloat16
HIGHEST = lax.Precision.HIGHEST
MESH = pl.DeviceIdType.MESH

D_MODEL = 1024
N_DEV = 8
HG_HEADS = 4
HG_DIM = 128
HG_WIDTH = HG_HEADS * HG_DIM
HG_CHUNK = 64
ATT_WIDTH = 512
ATT_HEAD_DIM = 64
ATT_BLOCK = 128
DILATIONS = (1, 4, 16)
ATT_SCALE = ATT_HEAD_DIM ** -0.5
D_FF = 2816
IN_WIDTH = 7 * 512
RMS_EPS = 1e-6
NEG = -1e30

ADAM_LR = 0.001
ADAM_B1 = 0.9
ADAM_B2 = 0.999
ADAM_EPS = 1e-08
ADAM_WD = 0.01
ADAM_STEP = 10

V7X_VMEM_LIMIT = 56 * 1024 * 1024

SM_MOD = 0
SM_G1 = 6 * D_MODEL
SM_G2 = 7 * D_MODEL
SM_GF = 8 * D_MODEL
SM_ATT = 9 * D_MODEL
SM_LB = 9 * D_MODEL + 512
SM_HG = 10 * D_MODEL
SM_LOSS = 10 * D_MODEL + 128
SM_WIDTH = 10 * D_MODEL + 256
SM_PADDED = 88 * 128


def _params(*sem, vmem=V7X_VMEM_LIMIT):
    return pltpu.CompilerParams(dimension_semantics=sem, vmem_limit_bytes=vmem)


def _dot(a, b):
    return jnp.dot(a, b, preferred_element_type=F32)


def _dot_nt(a, b):
    return lax.dot_general(a, b, (((1,), (1,)), ((), ())), preferred_element_type=F32)


def _dot_tn(a, b):
    return lax.dot_general(a, b, (((0,), (0,)), ((), ())), preferred_element_type=F32)


def _dot_f32(a, b):
    return jnp.dot(a, b, preferred_element_type=F32, precision=HIGHEST)


def _sigmoid(x):
    return 1.0 / (1.0 + jnp.exp(-x))


def _silu(x):
    return x * _sigmoid(x)


def _dsilu(x):
    s = _sigmoid(x)
    return s * (1.0 + x * (1.0 - s))


def _rms(x):
    rstd = lax.rsqrt(jnp.mean(x * x, axis=-1, keepdims=True) + RMS_EPS)
    return x * rstd, rstd


def _rms_bwd(dn, xhat, rstd):
    return rstd * (dn - xhat * jnp.mean(dn * xhat, axis=-1, keepdims=True))


def _rowsum(x):
    return jnp.sum(x, axis=0, keepdims=True)


def _rows(tm, n):
    return pl.BlockSpec((tm, n), lambda i: (i, 0))


def _whole(shape):
    return pl.BlockSpec(shape, lambda i: (0,) * len(shape))


def _mesh_pos():
    return lax.axis_index("x"), lax.axis_index("y"), lax.axis_index("c")


def _flip(k):
    x, y, c = _mesh_pos()
    px = 1 - x if k & 4 else x
    py = 1 - y if k & 2 else y
    pc = 1 - c if k & 1 else c
    return (px, py, pc), 4 * px + 2 * py + pc


def _exchange_small(x, rows_per_peer, name):
    r_all, cols = x.shape
    r_out = r_all if rows_per_peer is None else rows_per_peer

    def body(x_ref, out_ref, send_sems, recv_sems):
        _, me = _flip(0)

        def src(pid):
            if rows_per_peer is None:
                return x_ref
            return x_ref.at[pl.ds(pl.multiple_of(pid * r_out, r_out), r_out), :]

        if rows_per_peer is None:
            out_ref[me] = x_ref[...]
        else:
            out_ref[me] = x_ref[pl.ds(pl.multiple_of(me * r_out, r_out), r_out), :]
        sends = []
        for k in range(1, N_DEV):
            dev, pid = _flip(k)
            cp = pltpu.make_async_remote_copy(src_ref=src(pid), dst_ref=out_ref.at[me], send_sem=send_sems.at[k - 1],
                                              recv_sem=recv_sems.at[k - 1], device_id=dev, device_id_type=MESH)
            cp.start()
            sends.append(cp)
        for k in range(1, N_DEV):
            dev, pid = _flip(k)
            pltpu.make_async_remote_copy(src_ref=src(pid), dst_ref=out_ref.at[pid], send_sem=send_sems.at[k - 1],
                                         recv_sem=recv_sems.at[k - 1], device_id=dev, device_id_type=MESH).wait_recv()
        for cp in sends:
            cp.wait_send()

    return pl.pallas_call(
        body, name=name,
        out_shape=jax.ShapeDtypeStruct((N_DEV, r_out, cols), x.dtype),
        in_specs=[pl.BlockSpec(memory_space=pltpu.VMEM)],
        out_specs=pl.BlockSpec(memory_space=pltpu.VMEM),
        scratch_shapes=[pltpu.SemaphoreType.DMA((N_DEV - 1,)), pltpu.SemaphoreType.DMA((N_DEV - 1,))],
    )(x)


def _gather_weights(shards):
    n = len(shards)

    def body(*refs):
        xs, outs = refs[:n], refs[n:2 * n]
        send_sems, recv_sems, local_sems = refs[2 * n:]
        x, y, c = _mesh_pos()
        me, sibling = (x, y, c), (x, y, 1 - c)
        chips = [(1 - x, y), (x, 1 - y), (1 - x, 1 - y)]

        def blk(a, px, py, pc):
            return outs[a].at[4 * px + 2 * py + pc]

        def copy(a, k, block, to, src=None):
            return pltpu.make_async_remote_copy(
                src_ref=blk(a, *block) if src is None else src, dst_ref=blk(a, *block),
                send_sem=send_sems.at[a * 7 + k], recv_sem=recv_sems.at[a * 7 + k], device_id=to, device_id_type=MESH)

        mine = [pltpu.make_async_copy(xs[a], blk(a, *me), local_sems.at[a]) for a in range(n)]
        for cp in mine:
            cp.start()
        first = []
        for a in range(n):
            first.append(copy(a, 0, me, sibling, src=xs[a]))
            first += [copy(a, 1 + j, me, (*chip, c), src=xs[a]) for j, chip in enumerate(chips)]
        for cp in first:
            cp.start()
        passed = []
        for j, chip in enumerate(chips):
            for a in range(n):
                copy(a, 1 + j, (*chip, c), me).wait_recv()
                cp = copy(a, 4 + j, (*chip, c), sibling)
                cp.start()
                passed.append(cp)
        for a in range(n):
            copy(a, 0, sibling, me).wait_recv()
            for j, chip in enumerate(chips):
                copy(a, 4 + j, (*chip, 1 - c), me).wait_recv()
        for cp in first + passed:
            cp.wait_send()
        for cp in mine:
            cp.wait()

    hbm = pl.BlockSpec(memory_space=pl.ANY)
    return pl.pallas_call(
        body, name="gather_weights",
        out_shape=[jax.ShapeDtypeStruct((N_DEV,) + s.shape, s.dtype) for s in shards],
        in_specs=[hbm] * n, out_specs=[hbm] * n,
        scratch_shapes=[pltpu.SemaphoreType.DMA((7 * n,)), pltpu.SemaphoreType.DMA((7 * n,)), pltpu.SemaphoreType.DMA((n,))],
    )(*shards)


_HBM = pl.BlockSpec(memory_space=pltpu.HBM)
_SEM = pl.BlockSpec(memory_space=pltpu.SEMAPHORE)
_DATAFLOW = pltpu.SideEffectType.DATAFLOW_SIDE_EFFECTING


def _copies_start(name, plan, n_copies, srcs, lands, after):
    bufs = list(srcs) + list(lands)
    nb = len(bufs)

    def body(*refs):
        ins, send_sems, recv_sems, token = refs[:nb], refs[nb + len(after)], refs[nb + len(after) + 1], refs[-1]
        for i, (src, dst, dev) in enumerate(plan(ins[:len(srcs)], ins[len(srcs):])):
            pltpu.make_async_remote_copy(src_ref=src, dst_ref=dst, send_sem=send_sems.at[i], recv_sem=recv_sems.at[i],
                                         device_id=dev, device_id_type=MESH).start()
        token[...] = jnp.zeros_like(token)

    outs = pl.pallas_call(
        body, name=name,
        out_shape=(pltpu.SemaphoreType.DMA((n_copies,)), pltpu.SemaphoreType.DMA((n_copies,)),
                   *[pltpu.HBM(b.shape, b.dtype) for b in bufs], jax.ShapeDtypeStruct((8, 128), F32)),
        in_specs=[_HBM] * nb + [pl.BlockSpec(memory_space=pl.ANY)] * len(after),
        out_specs=(_SEM, _SEM, *[_HBM] * nb, pl.BlockSpec(memory_space=pltpu.VMEM)),
        input_output_aliases={i: 2 + i for i in range(nb)},
        compiler_params=pltpu.CompilerParams(has_side_effects=_DATAFLOW),
    )(*[pltpu.with_memory_space_constraint(b, pltpu.HBM) for b in bufs], *after)
    return outs[0], outs[1], list(outs[2:2 + len(srcs)]), list(outs[2 + len(srcs):2 + nb]), outs[-1]


def _copies_wait(name, plan, send_sems, recv_sems, srcs, lands, after):
    bufs = list(srcs) + list(lands)
    nb = len(bufs)

    def body(*refs):
        ins, send_ref, recv_ref = refs[:nb], refs[nb], refs[nb + 1]
        for i, (src, dst, dev) in enumerate(plan(ins[:len(srcs)], ins[len(srcs):])):
            cp = pltpu.make_async_remote_copy(src_ref=src, dst_ref=dst, send_sem=send_ref.at[i], recv_sem=recv_ref.at[i],
                                              device_id=dev, device_id_type=MESH)
            cp.wait_send()
            cp.wait_recv()

    outs = pl.pallas_call(
        body, name=name, out_shape=[pltpu.HBM(b.shape, b.dtype) for b in bufs],
        in_specs=[_HBM] * nb + [_SEM, _SEM] + [pl.BlockSpec(memory_space=pl.ANY)] * len(after), out_specs=[_HBM] * nb,
        input_output_aliases={i: i for i in range(nb)},
        compiler_params=pltpu.CompilerParams(has_side_effects=_DATAFLOW),
    )(*bufs, send_sems, recv_sems, *after)
    return list(outs[:len(srcs)]), list(outs[len(srcs):])


def _plan_gather_own(srcs, lands):
    _, me = _flip(0)
    return [(srcs[a], lands[a].at[me], _flip(k)[0]) for a in range(len(srcs)) for k in (1, 4, 2, 6)]


def _plan_gather_pass(srcs, lands):
    sibling = _flip(1)[0]
    plan = []
    for land in lands:
        for k in (4, 2, 6):
            block = land.at[_flip(k)[1]]
            plan.append((block, block, sibling))
    return plan


def _plan_reduce_pairs(srcs, lands):
    x, y, c = _mesh_pos()
    return [(srcs[a].at[chip, 1 - c], lands[a].at[chip], (x, y, 1 - c)) for a in range(len(srcs)) for chip in range(4)]


def _plan_reduce_chips(srcs, lands):
    plan = []
    for a in range(len(srcs)):
        for j, k in enumerate((4, 2, 6)):
            dev = _flip(k)[0]
            plan.append((srcs[a].at[2 * dev[0] + dev[1]], lands[a].at[j], dev))
    return plan


def _shard_rows(r):
    return r // 2 if r % 32 == 0 else r


def _pair_sum(core, grads, got, name):
    _, _, r, c = grads.shape
    tr = _shard_rows(r)

    def body(core_ref, a_ref, b_ref, o_ref, ob_ref):
        s = a_ref[...] + b_ref[...]
        o_ref[...] = s
        ob_ref[...] = s.astype(BF16)

    spec = pl.BlockSpec((None, tr, c), lambda i, j, core_ref: (i, j, 0))
    return pl.pallas_call(
        body, name=name,
        grid_spec=pltpu.PrefetchScalarGridSpec(
            num_scalar_prefetch=1, grid=(4, r // tr),
            in_specs=[pl.BlockSpec((None, None, tr, c), lambda i, j, core_ref: (i, core_ref[0], j, 0)), spec],
            out_specs=[spec, spec]),
        out_shape=[jax.ShapeDtypeStruct((4, r, c), F32), jax.ShapeDtypeStruct((4, r, c), BF16)],
        compiler_params=_params("parallel", "parallel"),
    )(core, grads, got)


def _ada_rows(c_all, w_ada, b_ada):
    n_cols = w_ada.shape[1]

    def body(c_ref, w_ref, b_ref, o_ref):
        _, me = _flip(0)
        bias = b_ref[:, pl.ds(pl.multiple_of(me * n_cols, 128), n_cols)]
        o_ref[...] = _dot_f32(_silu(c_ref[...]), w_ref[...]) + bias

    return pl.pallas_call(
        body, name="ada_rows", out_shape=jax.ShapeDtypeStruct((N_DEV, n_cols), F32),
        in_specs=[pl.BlockSpec(memory_space=pltpu.VMEM)] * 3, out_specs=pl.BlockSpec(memory_space=pltpu.VMEM),
    )(c_all, w_ada, b_ada)


def _in_fwd(x, mod, g1, w_in):
    s = x.shape[0]
    tm = 256

    def body(x_ref, mod_ref, g_ref, w_ref, h_ref, *outs):
        xhat, _ = _rms(x_ref[...])
        h = (xhat * g_ref[...]) * (1.0 + mod_ref[:, D_MODEL:2 * D_MODEL]) + mod_ref[:, 0:D_MODEL]
        hb = h.astype(BF16)
        h_ref[...] = hb
        for j, o_ref in enumerate(outs):
            o_ref[...] = _dot(hb, w_ref[:, j * 512:(j + 1) * 512])

    return pl.pallas_call(
        body, name="in_fwd", grid=(s // tm,),
        out_shape=[jax.ShapeDtypeStruct((s, D_MODEL), BF16)] + [jax.ShapeDtypeStruct((s, 512), F32)] * 7,
        in_specs=[_rows(tm, D_MODEL), _whole((1, 6 * D_MODEL)), _whole((1, D_MODEL)), _whole((D_MODEL, IN_WIDTH))],
        out_specs=[_rows(tm, D_MODEL)] + [_rows(tm, 512)] * 7,
        compiler_params=_params("parallel"),
    )(x, mod, g1, w_in)


def _in_bwd(x, dx1, mod, g1, w_in, dps):
    s = x.shape[0]
    tm = 256

    def body(x_ref, dx_ref, mod_ref, g_ref, w_ref, *rest):
        dp_refs, (gx_ref, dpb_ref, dsh_ref, dsc_ref, dg_ref) = rest[:13], rest[13:]
        pieces = [dp_refs[j][...] for j in range(4)]
        pieces += [dp_refs[4 + 3 * j][...] + dp_refs[5 + 3 * j][...] + dp_refs[6 + 3 * j][...] for j in range(3)]
        for j, p in enumerate(pieces):
            dpb_ref[:, j * 512:(j + 1) * 512] = p.astype(BF16)
        dh = _dot_nt(dpb_ref[...], w_ref[...])
        xhat, rstd = _rms(x_ref[...])
        g = g_ref[...]
        scale1 = 1.0 + mod_ref[:, D_MODEL:2 * D_MODEL]
        n1 = xhat * g

        @pl.when(pl.program_id(0) == 0)
        def _():
            dsh_ref[...] = jnp.zeros_like(dsh_ref)
            dsc_ref[...] = jnp.zeros_like(dsc_ref)
            dg_ref[...] = jnp.zeros_like(dg_ref)

        dsh_ref[...] += _rowsum(dh)
        dsc_ref[...] += _rowsum(dh * n1)
        dn = dh * scale1
        dg_ref[...] += _rowsum(dn * xhat)
        gx_ref[...] = dx_ref[...] + _rms_bwd(dn * g, xhat, rstd)

    vec = _whole((1, D_MODEL))
    return pl.pallas_call(
        body, name="in_bwd", grid=(s // tm,),
        out_shape=[jax.ShapeDtypeStruct((s, D_MODEL), F32), jax.ShapeDtypeStruct((s, IN_WIDTH), BF16)]
        + [jax.ShapeDtypeStruct((1, D_MODEL), F32)] * 3,
        in_specs=[_rows(tm, D_MODEL), _rows(tm, D_MODEL), _whole((1, 6 * D_MODEL)), vec, _whole((D_MODEL, IN_WIDTH))]
        + [_rows(tm, 512)] * 13,
        out_specs=[_rows(tm, D_MODEL), _rows(tm, IN_WIDTH), vec, vec, vec],
        compiler_params=_params("arbitrary"),
    )(x, dx1, mod, g1, w_in, *dps)


HG_TILE = 512
HG_TILE_CHUNKS = HG_TILE // HG_CHUNK


def _lower_bound(lg_ref):
    return 1.0 / (1.0 + jnp.exp(lg_ref[1:2, :] - lg_ref[0:1, :]))


def _chunk_masks():
    r = lax.broadcasted_iota(jnp.int32, (HG_CHUNK, HG_CHUNK), 0)
    c = lax.broadcasted_iota(jnp.int32, (HG_CHUNK, HG_CHUNK), 1)
    return r >= c, c >= r, (r >= c).astype(F32), (c >= r).astype(F32)


def _hg_fwd(hq, hf, hi, hgt, logits, onorm_g):
    s = hq.shape[0]
    n_tiles = s // HG_TILE

    def body(q_ref, f_ref, i_ref, g_ref, lg_ref, og_ref, out_ref, o_ref, st_ref, state, qf_s, kk_s, lf_s):
        @pl.when(pl.program_id(0) == 0)
        def _():
            state[...] = jnp.zeros_like(state)

        lb = _lower_bound(lg_ref)
        f = lb + (1.0 - lb) * _sigmoid(f_ref[...])
        kk_s[...] = 1.0 - f
        lf_s[...] = jnp.log(f)
        qf_s[...] = _silu(q_ref[...])
        causal, _, tri, _ = _chunk_masks()

        def chunk(ci, carry):
            rows = pl.ds(pl.multiple_of(ci * HG_CHUNK, HG_CHUNK), HG_CHUNK)
            srows = pl.ds(pl.multiple_of(ci * HG_DIM, HG_DIM), HG_DIM)
            lf = lf_s[rows, :]
            b = _dot_f32(tri, lf)
            bl = _rowsum(lf)
            ref = 0.5 * bl
            qf, kk, v = qf_s[rows, :], kk_s[rows, :], i_ref[rows, :]
            a_in = (qf * jnp.exp(b)).astype(BF16)
            a_t = (qf * jnp.exp(b - ref)).astype(BF16)
            b_t = (kk * jnp.exp(ref - b)).astype(BF16)
            kd = kk * jnp.exp(bl - b)
            ebl = jnp.exp(bl)
            vb = v.astype(BF16)
            for h in range(HG_HEADS):
                c = slice(h * HG_DIM, (h + 1) * HG_DIM)
                st = state[h]
                st_ref[srows, c] = st
                p = jnp.where(causal, _dot_nt(a_t[:, c], b_t[:, c]), 0.0)
                o_ref[rows, c] = _dot(p.astype(BF16), vb[:, c]) + _dot_nt(a_in[:, c], st.astype(BF16))
                state[h] = st * ebl[:, c] + _dot_tn(vb[:, c], kd[:, c].astype(BF16))
            return carry

        lax.fori_loop(0, HG_TILE_CHUNKS, chunk, 0, unroll=2)
        for h in range(HG_HEADS):
            c = slice(h * HG_DIM, (h + 1) * HG_DIM)
            ohat, _ = _rms(o_ref[:, c])
            out_ref[:, c] = (ohat * og_ref[...] * _silu(g_ref[:, c])).astype(BF16)

    tile = _rows(HG_TILE, HG_WIDTH)
    return pl.pallas_call(
        body, name="hg_fwd", grid=(n_tiles,),
        out_shape=[jax.ShapeDtypeStruct((s, HG_WIDTH), BF16), jax.ShapeDtypeStruct((s, HG_WIDTH), F32),
                   jax.ShapeDtypeStruct((s // HG_CHUNK * HG_DIM, HG_WIDTH), F32)],
        in_specs=[tile] * 4 + [_whole((2, HG_WIDTH)), _whole((1, HG_DIM))],
        out_specs=[tile, tile, _rows(HG_TILE_CHUNKS * HG_DIM, HG_WIDTH)],
        scratch_shapes=[pltpu.VMEM((HG_HEADS, HG_DIM, HG_DIM), F32)] + [pltpu.VMEM((HG_TILE, HG_WIDTH), F32)] * 3,
        compiler_params=_params("arbitrary"),
    )(hq, hf, hi, hgt, logits, onorm_g)


def _hg_bwd(hq, hf, hi, hgt, logits, onorm_g, o, states, dout):
    s = hq.shape[0]
    n_tiles = s // HG_TILE

    def body(q_ref, f_ref, i_ref, g_ref, lg_ref, og_ref, o_ref, st_ref, d_ref,
             dq_ref, df_ref, di_ref, dg_ref, dog_ref, dlb_ref, dstate, qf_s, kk_s, lf_s, do_s):
        @pl.when(pl.program_id(0) == 0)
        def _():
            dstate[...] = jnp.zeros_like(dstate)
            dog_ref[...] = jnp.zeros_like(dog_ref)
            dlb_ref[...] = jnp.zeros_like(dlb_ref)

        og = og_ref[...]
        dog = jnp.zeros((1, HG_DIM), F32)
        for h in range(HG_HEADS):
            c = slice(h * HG_DIM, (h + 1) * HG_DIM)
            ohat, rstd = _rms(o_ref[:, c])
            gate = g_ref[:, c]
            d = d_ref[:, c]
            dg_ref[:, c] = d * (ohat * og) * _dsilu(gate)
            dnormed = d * _silu(gate)
            dog += _rowsum(dnormed * ohat)
            do_s[:, c] = _rms_bwd(dnormed * og, ohat, rstd)
        dog_ref[...] += dog

        lb = _lower_bound(lg_ref)
        f = lb + (1.0 - lb) * _sigmoid(f_ref[...])
        kk_s[...] = 1.0 - f
        lf_s[...] = jnp.log(f)
        qf_s[...] = _silu(q_ref[...])
        causal, upper, tri, tri_t = _chunk_masks()

        def chunk(step, carry):
            ci = HG_TILE_CHUNKS - 1 - step
            rows = pl.ds(pl.multiple_of(ci * HG_CHUNK, HG_CHUNK), HG_CHUNK)
            srows = pl.ds(pl.multiple_of(ci * HG_DIM, HG_DIM), HG_DIM)
            lf = lf_s[rows, :]
            b = _dot_f32(tri, lf)
            bl = _rowsum(lf)
            ref = 0.5 * bl
            qf, kk, v, do = qf_s[rows, :], kk_s[rows, :], i_ref[rows, :], do_s[rows, :]
            eb, ebr, erb, ekd, ebl = jnp.exp(b), jnp.exp(b - ref), jnp.exp(ref - b), jnp.exp(bl - b), jnp.exp(bl)
            a_in, a_t, b_t, kd = qf * eb, qf * ebr, kk * erb, kk * ekd
            for h in range(HG_HEADS):
                c = slice(h * HG_DIM, (h + 1) * HG_DIM)
                st, dst = st_ref[srows, c], dstate[h]
                stb, dstb = st.astype(BF16), dst.astype(BF16)
                doh, vh = do[:, c], v[:, c]
                dob, vb = doh.astype(BF16), vh.astype(BF16)
                ain_h, at_h, bt_h, kd_h = a_in[:, c], a_t[:, c], b_t[:, c], kd[:, c]
                atb, btb = at_h.astype(BF16), bt_h.astype(BF16)
                d_ain = _dot(dob, stb)
                p_t = jnp.where(upper, _dot_nt(btb, atb), 0.0).astype(BF16)
                dp = jnp.where(causal, _dot_nt(dob, vb), 0.0).astype(BF16)
                dp_t = jnp.where(upper, _dot_nt(vb, dob), 0.0).astype(BF16)
                di_ref[rows, c] = _dot(p_t, dob) + _dot_nt(kd_h.astype(BF16), dstb)
                d_at = _dot(dp, btb)
                d_bt = _dot(dp_t, atb)
                d_kd = _dot(vb, dstb)
                dqf = d_ain * eb[:, c] + d_at * ebr[:, c]
                dkk = d_bt * erb[:, c] + d_kd * ekd[:, c]
                db = d_ain * ain_h + d_at * atb.astype(F32) - d_bt * btb.astype(F32) - d_kd * kd_h
                dbl = _rowsum(d_kd * kd_h) + _rowsum(dst * st) * ebl[:, c]
                dstate[h] = _dot_tn(dob, ain_h.astype(BF16)) + dst * ebl[:, c]
                dlf = _dot_f32(tri_t, db) + dbl
                qv, fr = q_ref[rows, c], f_ref[rows, c]
                lbh = lb[:, c]
                sg = _sigmoid(fr)
                dfv = dlf / (lbh + (1.0 - lbh) * sg) - dkk
                df_ref[rows, c] = dfv * (1.0 - lbh) * sg * (1.0 - sg)
                dlb_ref[:, c] += _rowsum(dfv * (1.0 - sg))
                dq_ref[rows, c] = dqf * _dsilu(qv)
            return carry

        lax.fori_loop(0, HG_TILE_CHUNKS, chunk, 0, unroll=2)

    rev = pl.BlockSpec((HG_TILE, HG_WIDTH), lambda i: (n_tiles - 1 - i, 0))
    return pl.pallas_call(
        body, name="hg_bwd", grid=(n_tiles,),
        out_shape=[jax.ShapeDtypeStruct((s, HG_WIDTH), F32)] * 4
        + [jax.ShapeDtypeStruct((1, HG_DIM), F32), jax.ShapeDtypeStruct((1, HG_WIDTH), F32)],
        in_specs=[rev] * 4 + [_whole((2, HG_WIDTH)), _whole((1, HG_DIM)), rev,
                              pl.BlockSpec((HG_TILE_CHUNKS * HG_DIM, HG_WIDTH), lambda i: (n_tiles - 1 - i, 0)), rev],
        out_specs=[rev] * 4 + [_whole((1, HG_DIM)), _whole((1, HG_WIDTH))],
        scratch_shapes=[pltpu.VMEM((HG_HEADS, HG_DIM, HG_DIM), F32)] + [pltpu.VMEM((HG_TILE, HG_WIDTH), F32)] * 4,
        compiler_params=_params("arbitrary"),
    )(hq, hf, hi, hgt, logits, onorm_g, o, states, dout)


TOKEN_GROUP = 16


def _att_geometry(dil):
    per_group = TOKEN_GROUP // dil
    return per_group, ATT_BLOCK // per_group, ATT_WIDTH if dil == 1 else 128


def _att_consts(dil):
    per_group, ub, _ = _att_geometry(dil)

    def pos(i):
        return i if dil == 1 else (i % ub) * per_group + i // ub

    lane = lax.broadcasted_iota(jnp.int32, (ATT_BLOCK, 128), 1)
    qi = pos(lax.broadcasted_iota(jnp.int32, (2 * ATT_BLOCK, ATT_BLOCK), 0) % ATT_BLOCK)
    kj = pos(lax.broadcasted_iota(jnp.int32, (2 * ATT_BLOCK, ATT_BLOCK), 1))
    return lane < ATT_HEAD_DIM, kj <= qi, lambda off: kj >= qi + off


def _load_tile(ref, dil, r, c):
    if dil == 1:
        return ref[:, c]
    per_group, ub, _ = _att_geometry(dil)
    return jnp.concatenate([ref[pl.ds(dil * w + r, ub, stride=TOKEN_GROUP), c] for w in range(per_group)], axis=0)


def _store_tile(ref, dil, r, c, val):
    if dil == 1:
        ref[:, c] = val
        return
    per_group, ub, _ = _att_geometry(dil)
    for w in range(per_group):
        ref[pl.ds(dil * w + r, ub, stride=TOKEN_GROUP), c] = val[w * ub:(w + 1) * ub]


def _att_specs(seq, dil):
    _, ub, lanes = _att_geometry(dil)
    rows = ub * TOKEN_GROUP
    nb = seq // rows
    cur = pl.BlockSpec((rows, lanes), lambda n, j: (n, j))
    prev = pl.BlockSpec((rows, lanes), lambda n, j: (jnp.maximum(n - 1, 0), j))
    nxt = pl.BlockSpec((rows, lanes), lambda n, j: (jnp.minimum(n + 1, nb - 1), j))
    return nb, lanes, cur, prev, nxt


def _stack_heads(x2, first):
    return jnp.concatenate([jnp.where(first, x2, 0.0), jnp.where(first, 0.0, x2)], axis=0)


def _stack_bcast(x2, first):
    other = pltpu.roll(x2, ATT_HEAD_DIM, axis=1)
    return jnp.concatenate([jnp.where(first, x2, other), jnp.where(first, other, x2)], axis=0)


def _unstack_heads(st, first):
    return jnp.where(first, st[:ATT_BLOCK], st[ATT_BLOCK:])


def _att_fwd(q, k, v, dil):
    seq, width = q.shape
    nb, lanes, cur, prev, _ = _att_specs(seq, dil)

    def body(q_ref, kc_ref, kp_ref, vc_ref, vp_ref, o_ref, lse_ref):
        first, cur_ok, _band = _att_consts(dil)
        prev_ok = _band(jnp.where(pl.program_id(0) > 0, 0, ATT_BLOCK))
        for r in range(dil):
            for j in range(lanes // 128):
                c = slice(j * 128, (j + 1) * 128)
                qst = _stack_heads(_load_tile(q_ref, dil, r, c) * ATT_SCALE, first).astype(BF16)
                kc, kp = _load_tile(kc_ref, dil, r, c).astype(BF16), _load_tile(kp_ref, dil, r, c).astype(BF16)
                vc, vp = _load_tile(vc_ref, dil, r, c).astype(BF16), _load_tile(vp_ref, dil, r, c).astype(BF16)
                sc = jnp.where(cur_ok, _dot_nt(qst, kc), NEG)
                sp = jnp.where(prev_ok, _dot_nt(qst, kp), NEG)
                mx = jnp.maximum(jnp.max(sc, axis=-1, keepdims=True), jnp.max(sp, axis=-1, keepdims=True))
                pc, pp = jnp.exp(sc - mx), jnp.exp(sp - mx)
                den = jnp.sum(pc, axis=-1, keepdims=True) + jnp.sum(pp, axis=-1, keepdims=True)
                ost = (_dot(pc.astype(BF16), vc) + _dot(pp.astype(BF16), vp)) / den
                lse = jnp.broadcast_to(mx + jnp.log(den), (2 * ATT_BLOCK, 128))
                _store_tile(o_ref, dil, r, c, _unstack_heads(ost, first))
                _store_tile(lse_ref, dil, r, c, _unstack_heads(lse, first))

    return pl.pallas_call(
        body, name=f"att_fwd_d{dil}", grid=(nb, width // lanes),
        out_shape=[jax.ShapeDtypeStruct((seq, width), F32)] * 2,
        in_specs=[cur, cur, prev, cur, prev], out_specs=[cur, cur],
        compiler_params=_params("arbitrary", "arbitrary"),
    )(q, k, k, v, v)


def _att_bwd(q, k, v, do, cc, lse, dil):
    seq, width = q.shape
    nb, lanes, cur, prev, nxt = _att_specs(seq, dil)

    def body(q_ref, qx_ref, kc_ref, kp_ref, vc_ref, vp_ref, do_ref, dox_ref, cc_ref, ccx_ref, lse_ref, lsex_ref,
             dq_ref, dk_ref, dv_ref):
        first, cur_ok, _band = _att_consts(dil)
        n = pl.program_id(0)
        prev_ok = _band(jnp.where(n > 0, 0, ATT_BLOCK))
        next_ok = _band(jnp.where(n < nb - 1, 0, ATT_BLOCK))
        for r in range(dil):
            for j in range(lanes // 128):
                c = slice(j * 128, (j + 1) * 128)
                qst = _stack_heads(_load_tile(q_ref, dil, r, c) * ATT_SCALE, first).astype(BF16)
                qxst = _stack_heads(_load_tile(qx_ref, dil, r, c) * ATT_SCALE, first).astype(BF16)
                dost = _stack_heads(_load_tile(do_ref, dil, r, c), first).astype(BF16)
                doxst = _stack_heads(_load_tile(dox_ref, dil, r, c), first).astype(BF16)
                lse_n = _stack_bcast(_load_tile(lse_ref, dil, r, c), first)
                lse_x = _stack_bcast(_load_tile(lsex_ref, dil, r, c), first)
                cc_n = _stack_bcast(_load_tile(cc_ref, dil, r, c), first)
                cc_x = _stack_bcast(_load_tile(ccx_ref, dil, r, c), first)
                kc, kp = _load_tile(kc_ref, dil, r, c).astype(BF16), _load_tile(kp_ref, dil, r, c).astype(BF16)
                vc, vp = _load_tile(vc_ref, dil, r, c).astype(BF16), _load_tile(vp_ref, dil, r, c).astype(BF16)
                p_cur = jnp.exp(jnp.where(cur_ok, _dot_nt(qst, kc), NEG) - lse_n)
                p_prev = jnp.exp(jnp.where(prev_ok, _dot_nt(qst, kp), NEG) - lse_n)
                p_next = jnp.exp(jnp.where(next_ok, _dot_nt(qxst, kc), NEG) - lse_x)
                ds_cur = (p_cur * (_dot_nt(dost, vc) + cc_n)).astype(BF16)
                ds_prev = (p_prev * (_dot_nt(dost, vp) + cc_n)).astype(BF16)
                ds_next = (p_next * (_dot_nt(doxst, vc) + cc_x)).astype(BF16)
                _store_tile(dq_ref, dil, r, c, _unstack_heads(_dot(ds_cur, kc) + _dot(ds_prev, kp), first) * ATT_SCALE)
                _store_tile(dk_ref, dil, r, c, _dot_tn(ds_cur, qst) + _dot_tn(ds_next, qxst))
                _store_tile(dv_ref, dil, r, c, _dot_tn(p_cur.astype(BF16), dost) + _dot_tn(p_next.astype(BF16), doxst))

    return pl.pallas_call(
        body, name=f"att_bwd_d{dil}", grid=(nb, width // lanes),
        out_shape=[jax.ShapeDtypeStruct((seq, width), F32)] * 3,
        in_specs=[cur, nxt, cur, prev, cur, prev, cur, nxt, cur, nxt, cur, nxt], out_specs=[cur, cur, cur],
        compiler_params=_params("arbitrary", "arbitrary"),
    )(q, q, k, k, v, v, do, do, cc, cc, lse, lse)


def _branch_weights(lses):
    mx = jnp.maximum(jnp.maximum(lses[0], lses[1]), lses[2])
    es = [jnp.exp(l - mx) for l in lses]
    inv = 1.0 / (es[0] + es[1] + es[2])
    return [e * inv for e in es]


def _att_combine(outs, lses, att_g):
    s = outs[0].shape[0]
    tm = 512

    def body(o0, o1, o2, l0, l1, l2, g_ref, att_ref, out_ref):
        ws = _branch_weights([l0[...], l1[...], l2[...]])
        att = ws[0] * o0[...] + ws[1] * o1[...] + ws[2] * o2[...]
        att_ref[...] = att
        ahat, _ = _rms(att)
        out_ref[...] = (ahat * g_ref[...]).astype(BF16)

    tile = _rows(tm, ATT_WIDTH)
    return pl.pallas_call(
        body, name="att_combine", grid=(s // tm,),
        out_shape=[jax.ShapeDtypeStruct((s, ATT_WIDTH), F32), jax.ShapeDtypeStruct((s, ATT_WIDTH), BF16)],
        in_specs=[tile] * 6 + [_whole((1, ATT_WIDTH))], out_specs=[tile, tile],
        compiler_params=_params("parallel"),
    )(*outs, *lses, att_g)


def _att_combine_bwd(datt_out, att, lses, att_g):
    s = att.shape[0]
    tm = 256

    def body(d_ref, att_ref, l0, l1, l2, g_ref, do0, do1, do2, cc0, cc1, cc2, dg_ref):
        @pl.when(pl.program_id(0) == 0)
        def _():
            dg_ref[...] = jnp.zeros_like(dg_ref)

        att = att_ref[...]
        ahat, rstd = _rms(att)
        d = d_ref[...]
        dg_ref[...] += _rowsum(d * ahat)
        datt = _rms_bwd(d * g_ref[...], ahat, rstd)
        hi = lax.broadcasted_iota(jnp.int32, (ATT_WIDTH, ATT_WIDTH), 0) // ATT_HEAD_DIM
        hj = lax.broadcasted_iota(jnp.int32, (ATT_WIDTH, ATT_WIDTH), 1) // ATT_HEAD_DIM
        head_sum = _dot_f32(datt * att, (hi == hj).astype(F32))
        ws = _branch_weights([l0[...], l1[...], l2[...]])
        for w, do_ref, cc_ref in zip(ws, (do0, do1, do2), (cc0, cc1, cc2)):
            do_ref[...] = w * datt
            cc_ref[...] = -w * head_sum

    tile = _rows(tm, ATT_WIDTH)
    return pl.pallas_call(
        body, name="att_combine_bwd", grid=(s // tm,),
        out_shape=[jax.ShapeDtypeStruct((s, ATT_WIDTH), F32)] * 6 + [jax.ShapeDtypeStruct((1, ATT_WIDTH), F32)],
        in_specs=[tile] * 5 + [_whole((1, ATT_WIDTH))], out_specs=[tile] * 6 + [_whole((1, ATT_WIDTH))],
        compiler_params=_params("arbitrary"),
    )(datt_out, att, *lses, att_g)


def _out_fwd(x, hg, at, mod, w_out):
    s = x.shape[0]
    tm = 512

    def body(x_ref, hg_ref, at_ref, mod_ref, w_ref, x1_ref):
        mix = _dot(hg_ref[...], w_ref[0:512, :]) + _dot(at_ref[...], w_ref[512:1024, :])
        x1_ref[...] = x_ref[...] + mod_ref[:, 2 * D_MODEL:3 * D_MODEL] * mix

    return pl.pallas_call(
        body, name="out_fwd", grid=(s // tm,), out_shape=jax.ShapeDtypeStruct((s, D_MODEL), F32),
        in_specs=[_rows(tm, D_MODEL), _rows(tm, 512), _rows(tm, 512), _whole((1, 6 * D_MODEL)), _whole((D_MODEL, D_MODEL))],
        out_specs=_rows(tm, D_MODEL), compiler_params=_params("parallel"),
    )(x, hg, at, mod, w_out)


def _out_bwd(dx1, hg, at, mod, w_out):
    s = dx1.shape[0]
    tm = 512

    def body(dx_ref, hg_ref, at_ref, mod_ref, w_ref, dhg_ref, dat_ref, dw_ref, dgate_ref):
        @pl.when(pl.program_id(0) == 0)
        def _():
            dw_ref[...] = jnp.zeros_like(dw_ref)
            dgate_ref[...] = jnp.zeros_like(dgate_ref)

        hg, at, dx = hg_ref[...], at_ref[...], dx_ref[...]
        mix = _dot(hg, w_ref[0:512, :]) + _dot(at, w_ref[512:1024, :])
        dgate_ref[...] += _rowsum(dx * mix)
        dmix = (mod_ref[:, 2 * D_MODEL:3 * D_MODEL] * dx).astype(BF16)
        dhg_ref[...] = _dot_nt(dmix, w_ref[0:512, :])
        dat_ref[...] = _dot_nt(dmix, w_ref[512:1024, :])
        dw_ref[0:512, :] += _dot_tn(hg, dmix)
        dw_ref[512:1024, :] += _dot_tn(at, dmix)

    return pl.pallas_call(
        body, name="out_bwd", grid=(s // tm,),
        out_shape=[jax.ShapeDtypeStruct((s, 512), F32)] * 2
        + [jax.ShapeDtypeStruct((D_MODEL, D_MODEL), F32), jax.ShapeDtypeStruct((1, D_MODEL), F32)],
        in_specs=[_rows(tm, D_MODEL), _rows(tm, 512), _rows(tm, 512), _whole((1, 6 * D_MODEL)), _whole((D_MODEL, D_MODEL))],
        out_specs=[_rows(tm, 512), _rows(tm, 512), _whole((D_MODEL, D_MODEL)), _whole((1, D_MODEL))],
        compiler_params=_params("arbitrary"),
    )(dx1, hg, at, mod, w_out)


FFN_CHUNK = 2816


def _ffn(x1, target, mod, g2, gf, w_gu, w_down):
    s = x1.shape[0]
    tm = 256
    n_chunks = D_FF // FFN_CHUNK

    def body(x_ref, t_ref, mod_ref, g2_ref, gf_ref, wgu_hbm, wd_hbm,
             dx_ref, h2_ref, act_ref, dau_ref, dff_ref, sums_ref, loss_ref, wgu, wd, a_s, u_s, sem):
        @pl.when(pl.program_id(0) == 0)
        def _():
            c1 = pltpu.make_async_copy(wgu_hbm, wgu, sem.at[0])
            c2 = pltpu.make_async_copy(wd_hbm, wd, sem.at[1])
            c1.start()
            c2.start()
            c1.wait()
            c2.wait()
            sums_ref[...] = jnp.zeros_like(sums_ref)
            loss_ref[...] = jnp.zeros_like(loss_ref)

        x1v = x_ref[...]
        xhat, rstd = _rms(x1v)
        g2 = g2_ref[...]
        n2 = xhat * g2
        scale2 = 1.0 + mod_ref[:, 4 * D_MODEL:5 * D_MODEL]
        gate2 = mod_ref[:, 5 * D_MODEL:6 * D_MODEL]
        hb = (n2 * scale2 + mod_ref[:, 3 * D_MODEL:4 * D_MODEL]).astype(BF16)
        h2_ref[...] = hb
        ff = jnp.zeros((tm, D_MODEL), F32)
        for j in range(n_chunks):
            c = slice(j * FFN_CHUNK, (j + 1) * FFN_CHUNK)
            cu = slice(D_FF + j * FFN_CHUNK, D_FF + (j + 1) * FFN_CHUNK)
            a = _dot(hb, wgu[:, c])
            u = _dot(hb, wgu[:, cu])
            a_s[:, c] = a
            u_s[:, c] = u
            act = (_silu(a) * u).astype(BF16)
            act_ref[:, c] = act
            ff += _dot(act, wd[c, :])
        x2 = x1v + gate2 * ff
        nf, rstd_f = _rms(x2)
        gfv = gf_ref[...]
        err = nf * gfv - t_ref[...]
        loss_ref[...] += 0.5 * jnp.sum(_rowsum(err * err), axis=-1, keepdims=True) * (1.0 / D_MODEL)
        dy = err * (1.0 / D_MODEL)
        dx2 = _rms_bwd(dy * gfv, nf, rstd_f)
        dffb = (gate2 * dx2).astype(BF16)
        dff_ref[...] = dffb
        dh = jnp.zeros((tm, D_MODEL), F32)
        for j in range(n_chunks):
            c = slice(j * FFN_CHUNK, (j + 1) * FFN_CHUNK)
            cu = slice(D_FF + j * FFN_CHUNK, D_FF + (j + 1) * FFN_CHUNK)
            dact = _dot_nt(dffb, wd[c, :])
            a, u = a_s[:, c], u_s[:, c]
            da = (dact * u * _dsilu(a)).astype(BF16)
            du = (dact * _silu(a)).astype(BF16)
            dau_ref[:, c] = da
            dau_ref[:, cu] = du
            dh += _dot_nt(da, wgu[:, c]) + _dot_nt(du, wgu[:, cu])
        dn = dh * scale2
        sums_ref[0:1, :] += _rowsum(dh)
        sums_ref[1:2, :] += _rowsum(dh * n2)
        sums_ref[2:3, :] += _rowsum(dx2 * ff)
        sums_ref[3:4, :] += _rowsum(dn * xhat)
        sums_ref[4:5, :] += _rowsum(dy * nf)
        dx_ref[...] = dx2 + _rms_bwd(dn * g2, xhat, rstd)

    vec = _whole((1, D_MODEL))
    hbm = pl.BlockSpec(memory_space=pl.ANY)
    return pl.pallas_call(
        body, name="ffn", grid=(s // tm,),
        out_shape=[jax.ShapeDtypeStruct((s, D_MODEL), F32), jax.ShapeDtypeStruct((s, D_MODEL), BF16),
                   jax.ShapeDtypeStruct((s, D_FF), BF16), jax.ShapeDtypeStruct((s, 2 * D_FF), BF16),
                   jax.ShapeDtypeStruct((s, D_MODEL), BF16), jax.ShapeDtypeStruct((8, D_MODEL), F32),
                   jax.ShapeDtypeStruct((1, 128), F32)],
        in_specs=[_rows(tm, D_MODEL), _rows(tm, D_MODEL), _whole((1, 6 * D_MODEL)), vec, vec, hbm, hbm],
        out_specs=[_rows(tm, D_MODEL), _rows(tm, D_MODEL), _rows(tm, D_FF), _rows(tm, 2 * D_FF), _rows(tm, D_MODEL),
                   _whole((8, D_MODEL)), _whole((1, 128))],
        scratch_shapes=[pltpu.VMEM((D_MODEL, 2 * D_FF), BF16), pltpu.VMEM((D_FF, D_MODEL), BF16),
                        pltpu.VMEM((tm, D_FF), F32), pltpu.VMEM((tm, D_FF), F32), pltpu.SemaphoreType.DMA((2,))],
        compiler_params=_params("arbitrary"),
    )(x1, target, mod, g2, gf, w_gu, w_down)


def _weight_grad(a, b, name):
    s, m = a.shape
    n = b.shape[1]
    ts, tn = min(s, 2048), 512

    def body(a_ref, b_ref, o_ref):
        @pl.when(pl.program_id(1) == 0)
        def _():
            o_ref[...] = jnp.zeros_like(o_ref)

        o_ref[...] += _dot_tn(a_ref[...], b_ref[...])

    return pl.pallas_call(
        body, name=name, grid=(n // tn, s // ts), out_shape=jax.ShapeDtypeStruct((m, n), F32),
        in_specs=[pl.BlockSpec((ts, m), lambda j, i: (i, 0)), pl.BlockSpec((ts, tn), lambda j, i: (i, j))],
        out_specs=pl.BlockSpec((m, tn), lambda j, i: (0, j)),
        compiler_params=_params("parallel", "arbitrary"),
    )(a, b)


def _adamw_math(w, g, m, v):
    m = ADAM_B1 * m + (1.0 - ADAM_B1) * g
    v = ADAM_B2 * v + (1.0 - ADAM_B2) * (g * g)
    m_hat = m / (1.0 - ADAM_B1 ** ADAM_STEP)
    v_hat = v / (1.0 - ADAM_B2 ** ADAM_STEP)
    delta = -ADAM_LR * (m_hat / (jnp.sqrt(v_hat) + ADAM_EPS) + ADAM_WD * w)
    return delta, m, v


def _adamw_shard(chip, w, m, v, partial, got, name):
    r, c = w.shape
    tr = _shard_rows(r)

    def body(chip_ref, w_ref, m_ref, v_ref, own_ref, g0, g1, g2, grad_ref, d_ref, nm_ref, nv_ref):
        g = ((own_ref[...] + g0[...].astype(F32)) + g1[...].astype(F32)) + g2[...].astype(F32)
        grad_ref[...] = g
        d_ref[...], nm_ref[...], nv_ref[...] = _adamw_math(w_ref[...], g, m_ref[...], v_ref[...])

    tile = pl.BlockSpec((tr, c), lambda i, chip_ref: (i, 0))
    own = pl.BlockSpec((None, tr, c), lambda i, chip_ref: (chip_ref[0], i, 0))
    part = [pl.BlockSpec((None, tr, c), functools.partial(lambda j, i, chip_ref: (j, i, 0), j)) for j in range(3)]
    return pl.pallas_call(
        body, name=name,
        grid_spec=pltpu.PrefetchScalarGridSpec(num_scalar_prefetch=1, grid=(r // tr,), in_specs=[tile] * 3 + [own] + part,
                                               out_specs=[tile] * 4),
        out_shape=[jax.ShapeDtypeStruct((r, c), F32)] * 4, compiler_params=_params("parallel"),
    )(chip, w, m, v, partial, got, got, got)


def _small_update(small_all, dmod_blocks, c_all, logits, w_ada, m_ada, v_ada, smalls):
    def body(sm_ref, dm_ref, c_ref, lg_ref, wa_ref, ma_ref, va_ref, *rest):
        ins, outs = rest[:21], rest[21:]
        _, me = _flip(0)
        tot = sm_ref[0:1, :]
        for i in range(1, N_DEV):
            tot = tot + sm_ref[i:i + 1, :]
        loss_ref = outs[0]
        loss_ref[...] = tot[:, SM_LOSS:SM_LOSS + 128]
        g_ada = lax.dot_general(_silu(c_ref[...]), dm_ref[me], (((0,), (0,)), ((), ())),
                                preferred_element_type=F32, precision=HIGHEST)
        outs[1][...] = g_ada
        outs[2][...], outs[3][...], outs[4][...] = _adamw_math(wa_ref[...], g_ada, ma_ref[...], va_ref[...])
        p0 = _lower_bound(lg_ref)
        dl0 = tot[:, SM_LB:SM_LB + 512] * p0 * (1.0 - p0)
        grads = [tot[:, SM_MOD:SM_MOD + 6 * D_MODEL], tot[:, SM_G1:SM_G1 + D_MODEL], tot[:, SM_G2:SM_G2 + D_MODEL],
                 tot[:, SM_GF:SM_GF + D_MODEL], tot[:, SM_ATT:SM_ATT + 512], tot[:, SM_HG:SM_HG + 128],
                 jnp.where(lax.broadcasted_iota(jnp.int32, (2, 512), 0) == 0, dl0, -dl0)]
        for i, g in enumerate(grads):
            w_ref, m_ref, v_ref = ins[3 * i:3 * i + 3]
            o = outs[5 + 4 * i:9 + 4 * i]
            o[0][...] = g
            o[1][...], o[2][...], o[3][...] = _adamw_math(w_ref[...], g, m_ref[...], v_ref[...])

    flat = [t for trio in smalls for t in trio]
    vm = pl.BlockSpec(memory_space=pltpu.VMEM)
    out_shape = [jax.ShapeDtypeStruct((1, 128), F32)] + [jax.ShapeDtypeStruct(w_ada.shape, F32)] * 4
    for trio in smalls:
        out_shape += [jax.ShapeDtypeStruct(trio[0].shape, F32)] * 4
    return pl.pallas_call(
        body, name="small_update", out_shape=out_shape,
        in_specs=[vm] * (7 + len(flat)), out_specs=[vm] * len(out_shape),
        compiler_params=pltpu.CompilerParams(vmem_limit_bytes=V7X_VMEM_LIMIT),
    )(small_all, dmod_blocks, c_all, logits, w_ada, m_ada, v_ada, *flat)


def kernel(x, c, w_ada, b_ada, norm1_g, w_in, hg_lb_logits, hg_onorm_g, att_onorm_g, w_out, norm2_g, w_gate_up, w_down, final_g, loss_target, m_w_ada, m_b_ada, m_norm1_g, m_w_in, m_hg_lb_logits, m_hg_onorm_g, m_att_onorm_g, m_w_out, m_norm2_g, m_w_gate_up, m_w_down, m_final_g, v_w_ada, v_b_ada, v_norm1_g, v_w_in, v_hg_lb_logits, v_hg_onorm_g, v_att_onorm_g, v_w_out, v_norm2_g, v_w_gate_up, v_w_down, v_final_g):
    x2d, target = x[0], loss_target[0]
    seq = x2d.shape[0]
    assert seq % (ATT_BLOCK * max(DILATIONS)) == 0 and seq % HG_TILE == 0
    gf = final_g.reshape(1, D_MODEL)

    c_all = _exchange_small(c.reshape(8, D_MODEL // 8), None, "gather_c").reshape(N_DEV, D_MODEL)
    ada = _ada_rows(c_all, w_ada[0], b_ada)
    mod = _exchange_small(ada, 1, "scatter_mod").reshape(1, 6 * D_MODEL)

    core = lax.axis_index("c").astype(jnp.int32).reshape(1)
    chip = (2 * lax.axis_index("x") + lax.axis_index("y")).astype(jnp.int32).reshape(1)
    me = 4 * lax.axis_index("x") + 2 * lax.axis_index("y") + lax.axis_index("c")

    g_in, = _gather_weights([w_in[0].astype(BF16)])
    w_in_b = jnp.transpose(g_in, (1, 0, 2)).reshape(D_MODEL, IN_WIDTH)
    rest_shards = [w_out[0].astype(BF16), w_gate_up[0].astype(BF16), w_down[0].astype(BF16)]
    lands = [lax.empty((N_DEV,) + s.shape, BF16) for s in rest_shards]
    g_send, g_recv, g_srcs, g_lands, tok = _copies_start("gather_rest_start", _plan_gather_own, 12, rest_shards, lands, [w_in_b, mod])
    flight = {}

    def stage(name, *vals):
        if name == "attention_begun":
            flight["shards"], got = _copies_wait("gather_rest_wait", _plan_gather_own, g_send, g_recv, g_srcs, g_lands, [vals[0]])
            flight["pass"] = _copies_start("gather_pass_start", _plan_gather_pass, 9, [], got, [])
            return flight["pass"][4][0:1, 0:1]
        if name == "mixer_weights_done":
            dw_gu, dw_down, dw_out = vals
            flight["grads"] = [dw_out.reshape(4, 2, D_MODEL // N_DEV, D_MODEL),
                               dw_gu.reshape(D_MODEL, 4, 2, 2 * D_FF // N_DEV).transpose(1, 2, 0, 3),
                               dw_down.reshape(4, 2, D_FF // N_DEV, D_MODEL)]
            pair_lands = [lax.empty((4,) + g.shape[2:], F32) for g in flight["grads"]]
            flight["pairs"] = _copies_start("reduce_pairs_start", _plan_reduce_pairs, 12, flight["grads"], pair_lands, [])
            return flight["pairs"][4][0:1, 0:1]
        if name == "attention_backward_begun":
            s, r, srcs, pl_lands, _ = flight["pairs"]
            grads, got = _copies_wait("reduce_pairs_wait", _plan_reduce_pairs, s, r, srcs, pl_lands, [vals[0]])
            flight["sums"] = [_pair_sum(core, g, b, f"pair_sum_{i}") for i, (g, b) in enumerate(zip(grads, got))]
            chip_lands = [lax.empty((3,) + s16.shape[1:], BF16) for _, s16 in flight["sums"]]
            flight["chips"] = _copies_start("reduce_chips_start", _plan_reduce_chips, 9, [s16 for _, s16 in flight["sums"]], chip_lands, [])
            return flight["chips"][4][0:1, 0:1]
        raise ValueError(name)

    def rest_weights(after):
        s, r, _, p_lands, _ = flight["pass"]
        _, got = _copies_wait("gather_pass_wait", _plan_gather_pass, s, r, [], p_lands, [after])
        full = [lax.dynamic_update_index_in_dim(g, shard, me, 0) for g, shard in zip(got, flight["shards"])]
        return (full[0].reshape(D_MODEL, D_MODEL), jnp.transpose(full[1], (1, 0, 2)).reshape(D_MODEL, 2 * D_FF),
                full[2].reshape(D_FF, D_MODEL))

    grad_x, dw_in, small = _block_step(x2d, target, mod + tok[0:1, 0:1], norm1_g, hg_lb_logits, hg_onorm_g, att_onorm_g, norm2_g, gf,
                                       w_in_b, rest_weights, stage)

    g_in8 = dw_in.reshape(D_MODEL, 4, 2, IN_WIDTH // N_DEV).transpose(1, 2, 0, 3)
    in_pairs = _copies_start("reduce_pairs_in_start", _plan_reduce_pairs, 4, [g_in8], [lax.empty((4,) + g_in8.shape[2:], F32)], [])
    s, r, srcs, c_lands, _ = flight["chips"]
    _, recv_rest = _copies_wait("reduce_chips_wait", _plan_reduce_chips, s, r, srcs, c_lands, [in_pairs[4]])
    small_rows = jnp.pad(small + in_pairs[4][0:1, 0:1], ((0, 0), (0, SM_PADDED - SM_WIDTH))).reshape(SM_PADDED // 128, 128)
    small_all = _exchange_small(small_rows, None, "gather_small").reshape(N_DEV, SM_PADDED)[:, :SM_WIDTH]
    big = {}
    rest_params = [("w_out", w_out, m_w_out, v_w_out), ("w_gate_up", w_gate_up, m_w_gate_up, v_w_gate_up), ("w_down", w_down, m_w_down, v_w_down)]
    for (n, w, m, v), (s32, _), got in zip(rest_params, flight["sums"], recv_rest):
        big[n] = [t[None] for t in _adamw_shard(chip, w[0], m[0], v[0], s32, got, f"adamw_{n}")]
    in_grads, got_in = _copies_wait("reduce_pairs_in_wait", _plan_reduce_pairs, in_pairs[0], in_pairs[1], in_pairs[2], in_pairs[3],
                                    [big[n][3] for n, _, _, _ in rest_params] + [small_all])
    in_s32, in_s16 = _pair_sum(core, in_grads[0], got_in[0], "pair_sum_in")
    in_chips = _copies_start("reduce_chips_in_start", _plan_reduce_chips, 3, [in_s16], [lax.empty((3,) + in_s16.shape[1:], BF16)], [])
    c_all = c_all + in_chips[4][0:1, 0:1]
    smalls = [(b_ada, m_b_ada, v_b_ada), (norm1_g, m_norm1_g, v_norm1_g), (norm2_g, m_norm2_g, v_norm2_g),
              (gf, m_final_g.reshape(1, D_MODEL), v_final_g.reshape(1, D_MODEL)),
              (att_onorm_g, m_att_onorm_g, v_att_onorm_g), (hg_onorm_g, m_hg_onorm_g, v_hg_onorm_g),
              (hg_lb_logits, m_hg_lb_logits, v_hg_lb_logits)]
    dmod_blocks = small_all[:, :6 * D_MODEL].reshape(N_DEV, N_DEV, 6 * D_MODEL // N_DEV).transpose(1, 0, 2)
    res = _small_update(small_all, dmod_blocks, c_all, hg_lb_logits, w_ada[0], m_w_ada[0], v_w_ada[0], smalls)
    _, recv_in = _copies_wait("reduce_chips_in_wait", _plan_reduce_chips, in_chips[0], in_chips[1], in_chips[2], in_chips[3], [res[0]])
    big["w_in"] = [t[None] for t in _adamw_shard(chip, w_in[0], m_w_in[0], v_w_in[0], in_s32, recv_in[0], "adamw_w_in")]
    loss = res[0][0, 0]
    ada4 = [t[None] for t in res[1:5]]
    sm4 = {n: list(res[5 + 4 * i:9 + 4 * i]) for i, n in enumerate(["b_ada", "norm1_g", "norm2_g", "final_g", "att", "hg", "lb"])}
    sm4["final_g"] = [t.reshape(D_MODEL) for t in sm4["final_g"]]

    order = [ada4, sm4["b_ada"], sm4["norm1_g"], big["w_in"], sm4["lb"], sm4["hg"], sm4["att"], big["w_out"], sm4["norm2_g"],
             big["w_gate_up"], big["w_down"], sm4["final_g"]]
    return (loss, grad_x[None], *[o[0] for o in order], *[o[1] for o in order], *[o[2] for o in order], *[o[3] for o in order])


def _block_step(x2d, target, mod, norm1_g, hg_lb_logits, hg_onorm_g, att_onorm_g, norm2_g, gf, w_in_b, rest_weights, stage):
    h1, hq, hf, hi, hgt, aq, ak, av = _in_fwd(x2d, mod, norm1_g, w_in_b)
    hg_out, hg_o, hg_states = _hg_fwd(hq, hf, hi, hgt, hg_lb_logits, hg_onorm_g)
    branch = [_att_fwd(aq, ak, av, DILATIONS[0])]
    att_g = att_onorm_g + stage("attention_begun", branch[0][0])
    branch += [_att_fwd(aq, ak, av, d) for d in DILATIONS[1:]]
    outs = [b[0] for b in branch]
    lses = [b[1] for b in branch]
    att, att_out = _att_combine(outs, lses, att_g)
    w_out_b, w_gu_b, w_down_b = rest_weights(att_out)
    x1 = _out_fwd(x2d, hg_out, att_out, mod, w_out_b)

    dx1, h2, act, dau, dff, ffn_sums, loss_part = _ffn(x1, target, mod, norm2_g, gf, w_gu_b, w_down_b)
    dw_gu = _weight_grad(h2, dau, "dw_gate_up")
    dw_down = _weight_grad(act, dff, "dw_down")

    dhg, dat, dw_out, dgate1 = _out_bwd(dx1, hg_out, att_out, mod, w_out_b)
    att_g = att_onorm_g + stage("mixer_weights_done", dw_gu, dw_down, dw_out)
    comb = _att_combine_bwd(dat, att, lses, att_g)
    dos, ccs, d_att_g = comb[0:3], comb[3:6], comb[6]
    hg_g = hg_onorm_g + stage("attention_backward_begun", comb[3])
    datt = []
    for i, d in enumerate(DILATIONS):
        datt.append(_att_bwd(aq, ak, av, dos[i], ccs[i], lses[i], d))
    dhq, dhf, dhi, dhgt, d_hg_g, d_lb = _hg_bwd(hq, hf, hi, hgt, hg_lb_logits, hg_g, hg_o, hg_states, dhg)
    dps = [dhq, dhf, dhi, dhgt] + [datt[i][j] for j in range(3) for i in range(3)]
    grad_x, dp_b, dshift1, dscale1, d_g1 = _in_bwd(x2d, dx1, mod, norm1_g, w_in_b, dps)
    dw_in = _weight_grad(h1, dp_b, "dw_in")
    small = jnp.concatenate([dshift1, dscale1, dgate1, ffn_sums[0:1], ffn_sums[1:2], ffn_sums[2:3], d_g1, ffn_sums[3:4],
                             ffn_sums[4:5], d_att_g, d_lb, d_hg_g, loss_part], axis=1)
    return grad_x, dw_in, small
```

```python
import functools

import jax
import jax.numpy as jnp
from jax import lax
from jax.experimental import pallas as pl
from jax.experimental.pallas import tpu as pltpu

F32 = jnp.float32
BF16 = jnp.bfloat16
HIGHEST = lax.Precision.HIGHEST
MESH = pl.DeviceIdType.MESH

D_MODEL = 1024
N_DEV = 8
HG_HEADS = 4
HG_DIM = 128
HG_WIDTH = HG_HEADS * HG_DIM
HG_CHUNK = 128
ATT_WIDTH = 512
ATT_HEAD_DIM = 64
ATT_BLOCK = 128
DILATIONS = (1, 4, 16)
ATT_SCALE = ATT_HEAD_DIM ** -0.5
D_FF = 2816
IN_WIDTH = 7 * 512
RMS_EPS = 1e-6
NEG = -1e30

ADAM_LR = 0.001
ADAM_B1 = 0.9
ADAM_B2 = 0.999
ADAM_EPS = 1e-08
ADAM_WD = 0.01
ADAM_STEP = 10

V7X_VMEM_LIMIT = 56 * 1024 * 1024

SM_MOD = 0
SM_G1 = 6 * D_MODEL
SM_G2 = 7 * D_MODEL
SM_GF = 8 * D_MODEL
SM_ATT = 9 * D_MODEL
SM_LB = 9 * D_MODEL + 512
SM_HG = 10 * D_MODEL
SM_LOSS = 10 * D_MODEL + 128
SM_WIDTH = 10 * D_MODEL + 256
SM_PADDED = 88 * 128


def _params(*sem, vmem=V7X_VMEM_LIMIT):
    return pltpu.CompilerParams(dimension_semantics=sem, vmem_limit_bytes=vmem)


def _dot(a, b):
    return jnp.dot(a, b, preferred_element_type=F32)


def _dot_nt(a, b):
    return lax.dot_general(a, b, (((1,), (1,)), ((), ())), preferred_element_type=F32)


def _dot_tn(a, b):
    return lax.dot_general(a, b, (((0,), (0,)), ((), ())), preferred_element_type=F32)


def _dot_f32(a, b):
    return jnp.dot(a, b, preferred_element_type=F32, precision=HIGHEST)


def _sigmoid(x):
    return 1.0 / (1.0 + jnp.exp(-x))


def _silu(x):
    return x * _sigmoid(x)


def _dsilu(x):
    s = _sigmoid(x)
    return s * (1.0 + x * (1.0 - s))


def _rms(x):
    rstd = lax.rsqrt(jnp.mean(x * x, axis=-1, keepdims=True) + RMS_EPS)
    return x * rstd, rstd


def _rms_bwd(dn, xhat, rstd):
    return rstd * (dn - xhat * jnp.mean(dn * xhat, axis=-1, keepdims=True))


def _rowsum(x):
    return jnp.sum(x, axis=0, keepdims=True)


def _rows(tm, n):
    return pl.BlockSpec((tm, n), lambda i: (i, 0))


def _whole(shape):
    return pl.BlockSpec(shape, lambda i: (0,) * len(shape))


def _mesh_pos():
    return lax.axis_index("x"), lax.axis_index("y"), lax.axis_index("c")


def _flip(k):
    x, y, c = _mesh_pos()
    px = 1 - x if k & 4 else x
    py = 1 - y if k & 2 else y
    pc = 1 - c if k & 1 else c
    return (px, py, pc), 4 * px + 2 * py + pc


def _exchange_small(x, rows_per_peer, name):
    r_all, cols = x.shape
    r_out = r_all if rows_per_peer is None else rows_per_peer

    def body(x_ref, out_ref, send_sems, recv_sems):
        _, me = _flip(0)

        def src(pid):
            if rows_per_peer is None:
                return x_ref
            return x_ref.at[pl.ds(pl.multiple_of(pid * r_out, r_out), r_out), :]

        if rows_per_peer is None:
            out_ref[me] = x_ref[...]
        else:
            out_ref[me] = x_ref[pl.ds(pl.multiple_of(me * r_out, r_out), r_out), :]
        sends = []
        for k in range(1, N_DEV):
            dev, pid = _flip(k)
            cp = pltpu.make_async_remote_copy(src_ref=src(pid), dst_ref=out_ref.at[me], send_sem=send_sems.at[k - 1],
                                              recv_sem=recv_sems.at[k - 1], device_id=dev, device_id_type=MESH)
            cp.start()
            sends.append(cp)
        for k in range(1, N_DEV):
            dev, pid = _flip(k)
            pltpu.make_async_remote_copy(src_ref=src(pid), dst_ref=out_ref.at[pid], send_sem=send_sems.at[k - 1],
                                         recv_sem=recv_sems.at[k - 1], device_id=dev, device_id_type=MESH).wait_recv()
        for cp in sends:
            cp.wait_send()

    return pl.pallas_call(
        body, name=name,
        out_shape=jax.ShapeDtypeStruct((N_DEV, r_out, cols), x.dtype),
        in_specs=[pl.BlockSpec(memory_space=pltpu.VMEM)],
        out_specs=pl.BlockSpec(memory_space=pltpu.VMEM),
        scratch_shapes=[pltpu.SemaphoreType.DMA((N_DEV - 1,)), pltpu.SemaphoreType.DMA((N_DEV - 1,))],
    )(x)


def _gather_weights(shards):
    n = len(shards)

    def body(*refs):
        xs, outs = refs[:n], refs[n:2 * n]
        send_sems, recv_sems, local_sems = refs[2 * n:]
        x, y, c = _mesh_pos()
        me, sibling = (x, y, c), (x, y, 1 - c)
        chips = [(1 - x, y), (x, 1 - y), (1 - x, 1 - y)]

        def blk(a, px, py, pc):
            return outs[a].at[4 * px + 2 * py + pc]

        def copy(a, k, block, to, src=None):
            return pltpu.make_async_remote_copy(
                src_ref=blk(a, *block) if src is None else src, dst_ref=blk(a, *block),
                send_sem=send_sems.at[a * 7 + k], recv_sem=recv_sems.at[a * 7 + k], device_id=to, device_id_type=MESH)

        mine = [pltpu.make_async_copy(xs[a], blk(a, *me), local_sems.at[a]) for a in range(n)]
        for cp in mine:
            cp.start()
        first = []
        for a in range(n):
            first.append(copy(a, 0, me, sibling, src=xs[a]))
            first += [copy(a, 1 + j, me, (*chip, c), src=xs[a]) for j, chip in enumerate(chips)]
        for cp in first:
            cp.start()
        passed = []
        for j, chip in enumerate(chips):
            for a in range(n):
                copy(a, 1 + j, (*chip, c), me).wait_recv()
                cp = copy(a, 4 + j, (*chip, c), sibling)
                cp.start()
                passed.append(cp)
        for a in range(n):
            copy(a, 0, sibling, me).wait_recv()
            for j, chip in enumerate(chips):
                copy(a, 4 + j, (*chip, 1 - c), me).wait_recv()
        for cp in first + passed:
            cp.wait_send()
        for cp in mine:
            cp.wait()

    hbm = pl.BlockSpec(memory_space=pl.ANY)
    return pl.pallas_call(
        body, name="gather_weights",
        out_shape=[jax.ShapeDtypeStruct((N_DEV,) + s.shape, s.dtype) for s in shards],
        in_specs=[hbm] * n, out_specs=[hbm] * n,
        scratch_shapes=[pltpu.SemaphoreType.DMA((7 * n,)), pltpu.SemaphoreType.DMA((7 * n,)), pltpu.SemaphoreType.DMA((n,))],
    )(*shards)


_HBM = pl.BlockSpec(memory_space=pltpu.HBM)
_SEM = pl.BlockSpec(memory_space=pltpu.SEMAPHORE)
_DATAFLOW = pltpu.SideEffectType.DATAFLOW_SIDE_EFFECTING


def _copies_start(name, plan, n_copies, srcs, lands, after):
    bufs = list(srcs) + list(lands)
    nb = len(bufs)

    def body(*refs):
        ins, send_sems, recv_sems, token = refs[:nb], refs[nb + len(after)], refs[nb + len(after) + 1], refs[-1]
        for i, (src, dst, dev) in enumerate(plan(ins[:len(srcs)], ins[len(srcs):])):
            pltpu.make_async_remote_copy(src_ref=src, dst_ref=dst, send_sem=send_sems.at[i], recv_sem=recv_sems.at[i],
                                         device_id=dev, device_id_type=MESH).start()
        token[...] = jnp.zeros_like(token)

    outs = pl.pallas_call(
        body, name=name,
        out_shape=(pltpu.SemaphoreType.DMA((n_copies,)), pltpu.SemaphoreType.DMA((n_copies,)),
                   *[pltpu.HBM(b.shape, b.dtype) for b in bufs], jax.ShapeDtypeStruct((8, 128), F32)),
        in_specs=[_HBM] * nb + [pl.BlockSpec(memory_space=pl.ANY)] * len(after),
        out_specs=(_SEM, _SEM, *[_HBM] * nb, pl.BlockSpec(memory_space=pltpu.VMEM)),
        input_output_aliases={i: 2 + i for i in range(nb)},
        compiler_params=pltpu.CompilerParams(has_side_effects=_DATAFLOW),
    )(*[pltpu.with_memory_space_constraint(b, pltpu.HBM) for b in bufs], *after)
    return outs[0], outs[1], list(outs[2:2 + len(srcs)]), list(outs[2 + len(srcs):2 + nb]), outs[-1]


def _copies_wait(name, plan, send_sems, recv_sems, srcs, lands, after):
    bufs = list(srcs) + list(lands)
    nb = len(bufs)

    def body(*refs):
        ins, send_ref, recv_ref = refs[:nb], refs[nb], refs[nb + 1]
        for i, (src, dst, dev) in enumerate(plan(ins[:len(srcs)], ins[len(srcs):])):
            cp = pltpu.make_async_remote_copy(src_ref=src, dst_ref=dst, send_sem=send_ref.at[i], recv_sem=recv_ref.at[i],
                                              device_id=dev, device_id_type=MESH)
            cp.wait_send()
            cp.wait_recv()

    outs = pl.pallas_call(
        body, name=name, out_shape=[pltpu.HBM(b.shape, b.dtype) for b in bufs],
        in_specs=[_HBM] * nb + [_SEM, _SEM] + [pl.BlockSpec(memory_space=pl.ANY)] * len(after), out_specs=[_HBM] * nb,
        input_output_aliases={i: i for i in range(nb)},
        compiler_params=pltpu.CompilerParams(has_side_effects=_DATAFLOW),
    )(*bufs, send_sems, recv_sems, *after)
    return list(outs[:len(srcs)]), list(outs[len(srcs):])


def _plan_gather_own(srcs, lands):
    _, me = _flip(0)
    return [(srcs[a], lands[a].at[me], _flip(k)[0]) for a in range(len(srcs)) for k in (1, 4, 2, 6)]


def _plan_gather_pass(srcs, lands):
    sibling = _flip(1)[0]
    plan = []
    for land in lands:
        for k in (4, 2, 6):
            block = land.at[_flip(k)[1]]
            plan.append((block, block, sibling))
    return plan


def _plan_reduce_pairs(srcs, lands):
    x, y, c = _mesh_pos()
    return [(srcs[a].at[chip, 1 - c], lands[a].at[chip], (x, y, 1 - c)) for a in range(len(srcs)) for chip in range(4)]


def _plan_reduce_chips(srcs, lands):
    plan = []
    for a in range(len(srcs)):
        for j, k in enumerate((4, 2, 6)):
            dev = _flip(k)[0]
            plan.append((srcs[a].at[2 * dev[0] + dev[1]], lands[a].at[j], dev))
    return plan


def _shard_rows(r):
    return r // 2 if r % 32 == 0 else r


def _pair_sum(core, grads, got, name):
    _, _, r, c = grads.shape
    tr = _shard_rows(r)

    def body(core_ref, a_ref, b_ref, o_ref, ob_ref):
        s = a_ref[...] + b_ref[...]
        o_ref[...] = s
        ob_ref[...] = s.astype(BF16)

    spec = pl.BlockSpec((None, tr, c), lambda i, j, core_ref: (i, j, 0))
    return pl.pallas_call(
        body, name=name,
        grid_spec=pltpu.PrefetchScalarGridSpec(
            num_scalar_prefetch=1, grid=(4, r // tr),
            in_specs=[pl.BlockSpec((None, None, tr, c), lambda i, j, core_ref: (i, core_ref[0], j, 0)), spec],
            out_specs=[spec, spec]),
        out_shape=[jax.ShapeDtypeStruct((4, r, c), F32), jax.ShapeDtypeStruct((4, r, c), BF16)],
        compiler_params=_params("parallel", "parallel"),
    )(core, grads, got)


def _ada_rows(c_all, w_ada, b_ada):
    n_cols = w_ada.shape[1]

    def body(c_ref, w_ref, b_ref, o_ref):
        _, me = _flip(0)
        bias = b_ref[:, pl.ds(pl.multiple_of(me * n_cols, 128), n_cols)]
        o_ref[...] = _dot_f32(_silu(c_ref[...]), w_ref[...]) + bias

    return pl.pallas_call(
        body, name="ada_rows", out_shape=jax.ShapeDtypeStruct((N_DEV, n_cols), F32),
        in_specs=[pl.BlockSpec(memory_space=pltpu.VMEM)] * 3, out_specs=pl.BlockSpec(memory_space=pltpu.VMEM),
    )(c_all, w_ada, b_ada)


def _in_fwd(x, mod, g1, w_in):
    s = x.shape[0]
    tm = 256

    def body(x_ref, mod_ref, g_ref, w_ref, h_ref, *outs):
        xhat, _ = _rms(x_ref[...])
        h = (xhat * g_ref[...]) * (1.0 + mod_ref[:, D_MODEL:2 * D_MODEL]) + mod_ref[:, 0:D_MODEL]
        hb = h.astype(BF16)
        h_ref[...] = hb
        for j, o_ref in enumerate(outs):
            o_ref[...] = _dot(hb, w_ref[:, j * 512:(j + 1) * 512])

    return pl.pallas_call(
        body, name="in_fwd", grid=(s // tm,),
        out_shape=[jax.ShapeDtypeStruct((s, D_MODEL), BF16)] + [jax.ShapeDtypeStruct((s, 512), F32)] * 7,
        in_specs=[_rows(tm, D_MODEL), _whole((1, 6 * D_MODEL)), _whole((1, D_MODEL)), _whole((D_MODEL, IN_WIDTH))],
        out_specs=[_rows(tm, D_MODEL)] + [_rows(tm, 512)] * 7,
        compiler_params=_params("parallel"),
    )(x, mod, g1, w_in)


def _in_bwd(x, dx1, mod, g1, w_in, dps):
    s = x.shape[0]
    tm = 256

    def body(x_ref, dx_ref, mod_ref, g_ref, w_ref, *rest):
        dp_refs, (gx_ref, dpb_ref, dsh_ref, dsc_ref, dg_ref) = rest[:13], rest[13:]
        pieces = [dp_refs[j][...] for j in range(4)]
        pieces += [dp_refs[4 + 3 * j][...] + dp_refs[5 + 3 * j][...] + dp_refs[6 + 3 * j][...] for j in range(3)]
        for j, p in enumerate(pieces):
            dpb_ref[:, j * 512:(j + 1) * 512] = p.astype(BF16)
        dh = _dot_nt(dpb_ref[...], w_ref[...])
        xhat, rstd = _rms(x_ref[...])
        g = g_ref[...]
        scale1 = 1.0 + mod_ref[:, D_MODEL:2 * D_MODEL]
        n1 = xhat * g

        @pl.when(pl.program_id(0) == 0)
        def _():
            dsh_ref[...] = jnp.zeros_like(dsh_ref)
            dsc_ref[...] = jnp.zeros_like(dsc_ref)
            dg_ref[...] = jnp.zeros_like(dg_ref)

        dsh_ref[...] += _rowsum(dh)
        dsc_ref[...] += _rowsum(dh * n1)
        dn = dh * scale1
        dg_ref[...] += _rowsum(dn * xhat)
        gx_ref[...] = dx_ref[...] + _rms_bwd(dn * g, xhat, rstd)

    vec = _whole((1, D_MODEL))
    return pl.pallas_call(
        body, name="in_bwd", grid=(s // tm,),
        out_shape=[jax.ShapeDtypeStruct((s, D_MODEL), F32), jax.ShapeDtypeStruct((s, IN_WIDTH), BF16)]
        + [jax.ShapeDtypeStruct((1, D_MODEL), F32)] * 3,
        in_specs=[_rows(tm, D_MODEL), _rows(tm, D_MODEL), _whole((1, 6 * D_MODEL)), vec, _whole((D_MODEL, IN_WIDTH))]
        + [_rows(tm, 512)] * 13,
        out_specs=[_rows(tm, D_MODEL), _rows(tm, IN_WIDTH), vec, vec, vec],
        compiler_params=_params("arbitrary"),
    )(x, dx1, mod, g1, w_in, *dps)


HG_TILE = 512
HG_TILE_CHUNKS = HG_TILE // HG_CHUNK


def _lower_bound(lg_ref):
    return 1.0 / (1.0 + jnp.exp(lg_ref[1:2, :] - lg_ref[0:1, :]))


def _chunk_masks():
    r = lax.broadcasted_iota(jnp.int32, (HG_CHUNK, HG_CHUNK), 0)
    c = lax.broadcasted_iota(jnp.int32, (HG_CHUNK, HG_CHUNK), 1)
    return r >= c, c >= r, (r >= c).astype(F32), (c >= r).astype(F32)


def _hg_fwd(hq, hf, hi, hgt, logits, onorm_g):
    s = hq.shape[0]
    n_tiles = s // HG_TILE

    def body(q_ref, f_ref, i_ref, g_ref, lg_ref, og_ref, out_ref, o_ref, st_ref, state, qf_s, kk_s, lf_s):
        @pl.when(pl.program_id(0) == 0)
        def _():
            state[...] = jnp.zeros_like(state)

        lb = _lower_bound(lg_ref)
        f = lb + (1.0 - lb) * _sigmoid(f_ref[...])
        kk_s[...] = 1.0 - f
        lf_s[...] = jnp.log(f)
        qf_s[...] = _silu(q_ref[...])
        causal, _, tri, _ = _chunk_masks()

        def chunk(ci, carry):
            rows = pl.ds(pl.multiple_of(ci * HG_CHUNK, HG_CHUNK), HG_CHUNK)
            srows = pl.ds(pl.multiple_of(ci * HG_DIM, HG_DIM), HG_DIM)
            lf = lf_s[rows, :]
            b = _dot_f32(tri, lf)
            bl = _rowsum(lf)
            ref = 0.5 * bl
            qf, kk, v = qf_s[rows, :], kk_s[rows, :], i_ref[rows, :]
            a_in = (qf * jnp.exp(b)).astype(BF16)
            a_t = (qf * jnp.exp(b - ref)).astype(BF16)
            b_t = (kk * jnp.exp(ref - b)).astype(BF16)
            kd = kk * jnp.exp(bl - b)
            ebl = jnp.exp(bl)
            vb = v.astype(BF16)
            for h in range(HG_HEADS):
                c = slice(h * HG_DIM, (h + 1) * HG_DIM)
                st = state[h]
                st_ref[srows, c] = st
                p = jnp.where(causal, _dot_nt(a_t[:, c], b_t[:, c]), 0.0)
                o_ref[rows, c] = _dot(p.astype(BF16), vb[:, c]) + _dot_nt(a_in[:, c], st.astype(BF16))
                state[h] = st * ebl[:, c] + _dot_tn(vb[:, c], kd[:, c].astype(BF16))
            return carry

        lax.fori_loop(0, HG_TILE_CHUNKS, chunk, 0, unroll=2)
        for h in range(HG_HEADS):
            c = slice(h * HG_DIM, (h + 1) * HG_DIM)
            ohat, _ = _rms(o_ref[:, c])
            out_ref[:, c] = (ohat * og_ref[...] * _silu(g_ref[:, c])).astype(BF16)

    tile = _rows(HG_TILE, HG_WIDTH)
    return pl.pallas_call(
        body, name="hg_fwd", grid=(n_tiles,),
        out_shape=[jax.ShapeDtypeStruct((s, HG_WIDTH), BF16), jax.ShapeDtypeStruct((s, HG_WIDTH), F32),
                   jax.ShapeDtypeStruct((s // HG_CHUNK * HG_DIM, HG_WIDTH), F32)],
        in_specs=[tile] * 4 + [_whole((2, HG_WIDTH)), _whole((1, HG_DIM))],
        out_specs=[tile, tile, _rows(HG_TILE_CHUNKS * HG_DIM, HG_WIDTH)],
        scratch_shapes=[pltpu.VMEM((HG_HEADS, HG_DIM, HG_DIM), F32)] + [pltpu.VMEM((HG_TILE, HG_WIDTH), F32)] * 3,
        compiler_params=_params("arbitrary"),
    )(hq, hf, hi, hgt, logits, onorm_g)


def _hg_bwd(hq, hf, hi, hgt, logits, onorm_g, o, states, dout):
    s = hq.shape[0]
    n_tiles = s // HG_TILE

    def body(q_ref, f_ref, i_ref, g_ref, lg_ref, og_ref, o_ref, st_ref, d_ref,
             dq_ref, df_ref, di_ref, dg_ref, dog_ref, dlb_ref, dstate, qf_s, kk_s, lf_s, do_s):
        @pl.when(pl.program_id(0) == 0)
        def _():
            dstate[...] = jnp.zeros_like(dstate)
            dog_ref[...] = jnp.zeros_like(dog_ref)
            dlb_ref[...] = jnp.zeros_like(dlb_ref)

        og = og_ref[...]
        dog = jnp.zeros((1, HG_DIM), F32)
        for h in range(HG_HEADS):
            c = slice(h * HG_DIM, (h + 1) * HG_DIM)
            ohat, rstd = _rms(o_ref[:, c])
            gate = g_ref[:, c]
            d = d_ref[:, c]
            dg_ref[:, c] = d * (ohat * og) * _dsilu(gate)
            dnormed = d * _silu(gate)
            dog += _rowsum(dnormed * ohat)
            do_s[:, c] = _rms_bwd(dnormed * og, ohat, rstd)
        dog_ref[...] += dog

        lb = _lower_bound(lg_ref)
        f = lb + (1.0 - lb) * _sigmoid(f_ref[...])
        kk_s[...] = 1.0 - f
        lf_s[...] = jnp.log(f)
        qf_s[...] = _silu(q_ref[...])
        causal, upper, tri, tri_t = _chunk_masks()

        def chunk(step, carry):
            ci = HG_TILE_CHUNKS - 1 - step
            rows = pl.ds(pl.multiple_of(ci * HG_CHUNK, HG_CHUNK), HG_CHUNK)
            srows = pl.ds(pl.multiple_of(ci * HG_DIM, HG_DIM), HG_DIM)
            lf = lf_s[rows, :]
            b = _dot_f32(tri, lf)
            bl = _rowsum(lf)
            ref = 0.5 * bl
            qf, kk, v, do = qf_s[rows, :], kk_s[rows, :], i_ref[rows, :], do_s[rows, :]
            eb, ebr, erb, ekd, ebl = jnp.exp(b), jnp.exp(b - ref), jnp.exp(ref - b), jnp.exp(bl - b), jnp.exp(bl)
            a_in, a_t, b_t, kd = qf * eb, qf * ebr, kk * erb, kk * ekd
            for h in range(HG_HEADS):
                c = slice(h * HG_DIM, (h + 1) * HG_DIM)
                st, dst = st_ref[srows, c], dstate[h]
                stb, dstb = st.astype(BF16), dst.astype(BF16)
                doh, vh = do[:, c], v[:, c]
                dob, vb = doh.astype(BF16), vh.astype(BF16)
                ain_h, at_h, bt_h, kd_h = a_in[:, c], a_t[:, c], b_t[:, c], kd[:, c]
                atb, btb = at_h.astype(BF16), bt_h.astype(BF16)
                d_ain = _dot(dob, stb)
                p_t = jnp.where(upper, _dot_nt(btb, atb), 0.0).astype(BF16)
                dp = jnp.where(causal, _dot_nt(dob, vb), 0.0).astype(BF16)
                dp_t = jnp.where(upper, _dot_nt(vb, dob), 0.0).astype(BF16)
                di_ref[rows, c] = _dot(p_t, dob) + _dot_nt(kd_h.astype(BF16), dstb)
                d_at = _dot(dp, btb)
                d_bt = _dot(dp_t, atb)
                d_kd = _dot(vb, dstb)
                dqf = d_ain * eb[:, c] + d_at * ebr[:, c]
                dkk = d_bt * erb[:, c] + d_kd * ekd[:, c]
                db = d_ain * ain_h + d_at * atb.astype(F32) - d_bt * btb.astype(F32) - d_kd * kd_h
                dbl = _rowsum(d_kd * kd_h) + _rowsum(dst * st) * ebl[:, c]
                dstate[h] = _dot_tn(dob, ain_h.astype(BF16)) + dst * ebl[:, c]
                dlf = _dot_f32(tri_t, db) + dbl
                qv, fr = q_ref[rows, c], f_ref[rows, c]
                lbh = lb[:, c]
                sg = _sigmoid(fr)
                dfv = dlf / (lbh + (1.0 - lbh) * sg) - dkk
                df_ref[rows, c] = dfv * (1.0 - lbh) * sg * (1.0 - sg)
                dlb_ref[:, c] += _rowsum(dfv * (1.0 - sg))
                dq_ref[rows, c] = dqf * _dsilu(qv)
            return carry

        lax.fori_loop(0, HG_TILE_CHUNKS, chunk, 0, unroll=2)

    rev = pl.BlockSpec((HG_TILE, HG_WIDTH), lambda i: (n_tiles - 1 - i, 0))
    return pl.pallas_call(
        body, name="hg_bwd", grid=(n_tiles,),
        out_shape=[jax.ShapeDtypeStruct((s, HG_WIDTH), F32)] * 4
        + [jax.ShapeDtypeStruct((1, HG_DIM), F32), jax.ShapeDtypeStruct((1, HG_WIDTH), F32)],
        in_specs=[rev] * 4 + [_whole((2, HG_WIDTH)), _whole((1, HG_DIM)), rev,
                              pl.BlockSpec((HG_TILE_CHUNKS * HG_DIM, HG_WIDTH), lambda i: (n_tiles - 1 - i, 0)), rev],
        out_specs=[rev] * 4 + [_whole((1, HG_DIM)), _whole((1, HG_WIDTH))],
        scratch_shapes=[pltpu.VMEM((HG_HEADS, HG_DIM, HG_DIM), F32)] + [pltpu.VMEM((HG_TILE, HG_WIDTH), F32)] * 4,
        compiler_params=_params("arbitrary"),
    )(hq, hf, hi, hgt, logits, onorm_g, o, states, dout)


TOKEN_GROUP = 16


def _att_geometry(dil):
    per_group = TOKEN_GROUP // dil
    return per_group, ATT_BLOCK // per_group, ATT_WIDTH if dil == 1 else 128


def _att_consts(dil):
    per_group, ub, _ = _att_geometry(dil)

    def pos(i):
        return i if dil == 1 else (i % ub) * per_group + i // ub

    lane = lax.broadcasted_iota(jnp.int32, (ATT_BLOCK, 128), 1)
    qi = pos(lax.broadcasted_iota(jnp.int32, (2 * ATT_BLOCK, ATT_BLOCK), 0) % ATT_BLOCK)
    kj = pos(lax.broadcasted_iota(jnp.int32, (2 * ATT_BLOCK, ATT_BLOCK), 1))
    return lane < ATT_HEAD_DIM, kj <= qi, lambda off: kj >= qi + off


def _load_tile(ref, dil, r, c):
    if dil == 1:
        return ref[:, c]
    per_group, ub, _ = _att_geometry(dil)
    return jnp.concatenate([ref[pl.ds(dil * w + r, ub, stride=TOKEN_GROUP), c] for w in range(per_group)], axis=0)


def _store_tile(ref, dil, r, c, val):
    if dil == 1:
        ref[:, c] = val
        return
    per_group, ub, _ = _att_geometry(dil)
    for w in range(per_group):
        ref[pl.ds(dil * w + r, ub, stride=TOKEN_GROUP), c] = val[w * ub:(w + 1) * ub]


def _att_specs(seq, dil):
    _, ub, lanes = _att_geometry(dil)
    rows = ub * TOKEN_GROUP
    nb = seq // rows
    cur = pl.BlockSpec((rows, lanes), lambda n, j: (n, j))
    prev = pl.BlockSpec((rows, lanes), lambda n, j: (jnp.maximum(n - 1, 0), j))
    nxt = pl.BlockSpec((rows, lanes), lambda n, j: (jnp.minimum(n + 1, nb - 1), j))
    return nb, lanes, cur, prev, nxt


def _stack_heads(x2, first):
    return jnp.concatenate([jnp.where(first, x2, 0.0), jnp.where(first, 0.0, x2)], axis=0)


def _stack_bcast(x2, first):
    other = pltpu.roll(x2, ATT_HEAD_DIM, axis=1)
    return jnp.concatenate([jnp.where(first, x2, other), jnp.where(first, other, x2)], axis=0)


def _unstack_heads(st, first):
    return jnp.where(first, st[:ATT_BLOCK], st[ATT_BLOCK:])


def _att_fwd(q, k, v, dil):
    seq, width = q.shape
    nb, lanes, cur, prev, _ = _att_specs(seq, dil)

    def body(q_ref, kc_ref, kp_ref, vc_ref, vp_ref, o_ref, lse_ref):
        first, cur_ok, _band = _att_consts(dil)
        prev_ok = _band(jnp.where(pl.program_id(0) > 0, 0, ATT_BLOCK))
        for r in range(dil):
            for j in range(lanes // 128):
                c = slice(j * 128, (j + 1) * 128)
                qst = _stack_heads(_load_tile(q_ref, dil, r, c) * ATT_SCALE, first).astype(BF16)
                kc, kp = _load_tile(kc_ref, dil, r, c).astype(BF16), _load_tile(kp_ref, dil, r, c).astype(BF16)
                vc, vp = _load_tile(vc_ref, dil, r, c).astype(BF16), _load_tile(vp_ref, dil, r, c).astype(BF16)
                sc = jnp.where(cur_ok, _dot_nt(qst, kc), NEG)
                sp = jnp.where(prev_ok, _dot_nt(qst, kp), NEG)
                mx = jnp.maximum(jnp.max(sc, axis=-1, keepdims=True), jnp.max(sp, axis=-1, keepdims=True))
                pc, pp = jnp.exp(sc - mx), jnp.exp(sp - mx)
                den = jnp.sum(pc, axis=-1, keepdims=True) + jnp.sum(pp, axis=-1, keepdims=True)
                ost = (_dot(pc.astype(BF16), vc) + _dot(pp.astype(BF16), vp)) / den
                lse = jnp.broadcast_to(mx + jnp.log(den), (2 * ATT_BLOCK, 128))
                _store_tile(o_ref, dil, r, c, _unstack_heads(ost, first))
                _store_tile(lse_ref, dil, r, c, _unstack_heads(lse, first))

    return pl.pallas_call(
        body, name=f"att_fwd_d{dil}", grid=(nb, width // lanes),
        out_shape=[jax.ShapeDtypeStruct((seq, width), F32)] * 2,
        in_specs=[cur, cur, prev, cur, prev], out_specs=[cur, cur],
        compiler_params=_params("arbitrary", "arbitrary"),
    )(q, k, k, v, v)


def _att_bwd(q, k, v, do, cc, lse, dil):
    seq, width = q.shape
    nb, lanes, cur, prev, nxt = _att_specs(seq, dil)

    def body(q_ref, qx_ref, kc_ref, kp_ref, vc_ref, vp_ref, do_ref, dox_ref, cc_ref, ccx_ref, lse_ref, lsex_ref,
             dq_ref, dk_ref, dv_ref):
        first, cur_ok, _band = _att_consts(dil)
        n = pl.program_id(0)
        prev_ok = _band(jnp.where(n > 0, 0, ATT_BLOCK))
        next_ok = _band(jnp.where(n < nb - 1, 0, ATT_BLOCK))
        for r in range(dil):
            for j in range(lanes // 128):
                c = slice(j * 128, (j + 1) * 128)
                qst = _stack_heads(_load_tile(q_ref, dil, r, c) * ATT_SCALE, first).astype(BF16)
                qxst = _stack_heads(_load_tile(qx_ref, dil, r, c) * ATT_SCALE, first).astype(BF16)
                dost = _stack_heads(_load_tile(do_ref, dil, r, c), first).astype(BF16)
                doxst = _stack_heads(_load_tile(dox_ref, dil, r, c), first).astype(BF16)
                lse_n = _stack_bcast(_load_tile(lse_ref, dil, r, c), first)
                lse_x = _stack_bcast(_load_tile(lsex_ref, dil, r, c), first)
                cc_n = _stack_bcast(_load_tile(cc_ref, dil, r, c), first)
                cc_x = _stack_bcast(_load_tile(ccx_ref, dil, r, c), first)
                kc, kp = _load_tile(kc_ref, dil, r, c).astype(BF16), _load_tile(kp_ref, dil, r, c).astype(BF16)
                vc, vp = _load_tile(vc_ref, dil, r, c).astype(BF16), _load_tile(vp_ref, dil, r, c).astype(BF16)
                p_cur = jnp.exp(jnp.where(cur_ok, _dot_nt(qst, kc), NEG) - lse_n)
                p_prev = jnp.exp(jnp.where(prev_ok, _dot_nt(qst, kp), NEG) - lse_n)
                p_next = jnp.exp(jnp.where(next_ok, _dot_nt(qxst, kc), NEG) - lse_x)
                ds_cur = (p_cur * (_dot_nt(dost, vc) + cc_n)).astype(BF16)
                ds_prev = (p_prev * (_dot_nt(dost, vp) + cc_n)).astype(BF16)
                ds_next = (p_next * (_dot_nt(doxst, vc) + cc_x)).astype(BF16)
                _store_tile(dq_ref, dil, r, c, _unstack_heads(_dot(ds_cur, kc) + _dot(ds_prev, kp), first) * ATT_SCALE)
                _store_tile(dk_ref, dil, r, c, _dot_tn(ds_cur, qst) + _dot_tn(ds_next, qxst))
                _store_tile(dv_ref, dil, r, c, _dot_tn(p_cur.astype(BF16), dost) + _dot_tn(p_next.astype(BF16), doxst))

    return pl.pallas_call(
        body, name=f"att_bwd_d{dil}", grid=(nb, width // lanes),
        out_shape=[jax.ShapeDtypeStruct((seq, width), F32)] * 3,
        in_specs=[cur, nxt, cur, prev, cur, prev, cur, nxt, cur, nxt, cur, nxt], out_specs=[cur, cur, cur],
        compiler_params=_params("arbitrary", "arbitrary"),
    )(q, q, k, k, v, v, do, do, cc, cc, lse, lse)


def _branch_weights(lses):
    mx = jnp.maximum(jnp.maximum(lses[0], lses[1]), lses[2])
    es = [jnp.exp(l - mx) for l in lses]
    inv = 1.0 / (es[0] + es[1] + es[2])
    return [e * inv for e in es]


def _att_combine(outs, lses, att_g):
    s = outs[0].shape[0]
    tm = 512

    def body(o0, o1, o2, l0, l1, l2, g_ref, att_ref, out_ref):
        ws = _branch_weights([l0[...], l1[...], l2[...]])
        att = ws[0] * o0[...] + ws[1] * o1[...] + ws[2] * o2[...]
        att_ref[...] = att
        ahat, _ = _rms(att)
        out_ref[...] = (ahat * g_ref[...]).astype(BF16)

    tile = _rows(tm, ATT_WIDTH)
    return pl.pallas_call(
        body, name="att_combine", grid=(s // tm,),
        out_shape=[jax.ShapeDtypeStruct((s, ATT_WIDTH), F32), jax.ShapeDtypeStruct((s, ATT_WIDTH), BF16)],
        in_specs=[tile] * 6 + [_whole((1, ATT_WIDTH))], out_specs=[tile, tile],
        compiler_params=_params("parallel"),
    )(*outs, *lses, att_g)


def _att_combine_bwd(datt_out, att, lses, att_g):
    s = att.shape[0]
    tm = 256

    def body(d_ref, att_ref, l0, l1, l2, g_ref, do0, do1, do2, cc0, cc1, cc2, dg_ref):
        @pl.when(pl.program_id(0) == 0)
        def _():
            dg_ref[...] = jnp.zeros_like(dg_ref)

        att = att_ref[...]
        ahat, rstd = _rms(att)
        d = d_ref[...]
        dg_ref[...] += _rowsum(d * ahat)
        datt = _rms_bwd(d * g_ref[...], ahat, rstd)
        hi = lax.broadcasted_iota(jnp.int32, (ATT_WIDTH, ATT_WIDTH), 0) // ATT_HEAD_DIM
        hj = lax.broadcasted_iota(jnp.int32, (ATT_WIDTH, ATT_WIDTH), 1) // ATT_HEAD_DIM
        head_sum = _dot_f32(datt * att, (hi == hj).astype(F32))
        ws = _branch_weights([l0[...], l1[...], l2[...]])
        for w, do_ref, cc_ref in zip(ws, (do0, do1, do2), (cc0, cc1, cc2)):
            do_ref[...] = w * datt
            cc_ref[...] = -w * head_sum

    tile = _rows(tm, ATT_WIDTH)
    return pl.pallas_call(
        body, name="att_combine_bwd", grid=(s // tm,),
        out_shape=[jax.ShapeDtypeStruct((s, ATT_WIDTH), F32)] * 6 + [jax.ShapeDtypeStruct((1, ATT_WIDTH), F32)],
        in_specs=[tile] * 5 + [_whole((1, ATT_WIDTH))], out_specs=[tile] * 6 + [_whole((1, ATT_WIDTH))],
        compiler_params=_params("arbitrary"),
    )(datt_out, att, *lses, att_g)


def _out_fwd(x, hg, at, mod, w_out):
    s = x.shape[0]
    tm = 512

    def body(x_ref, hg_ref, at_ref, mod_ref, w_ref, x1_ref):
        mix = _dot(hg_ref[...], w_ref[0:512, :]) + _dot(at_ref[...], w_ref[512:1024, :])
        x1_ref[...] = x_ref[...] + mod_ref[:, 2 * D_MODEL:3 * D_MODEL] * mix

    return pl.pallas_call(
        body, name="out_fwd", grid=(s // tm,), out_shape=jax.ShapeDtypeStruct((s, D_MODEL), F32),
        in_specs=[_rows(tm, D_MODEL), _rows(tm, 512), _rows(tm, 512), _whole((1, 6 * D_MODEL)), _whole((D_MODEL, D_MODEL))],
        out_specs=_rows(tm, D_MODEL), compiler_params=_params("parallel"),
    )(x, hg, at, mod, w_out)


def _out_bwd(dx1, hg, at, mod, w_out):
    s = dx1.shape[0]
    tm = 512

    def body(dx_ref, hg_ref, at_ref, mod_ref, w_ref, dhg_ref, dat_ref, dw_ref, dgate_ref):
        @pl.when(pl.program_id(0) == 0)
        def _():
            dw_ref[...] = jnp.zeros_like(dw_ref)
            dgate_ref[...] = jnp.zeros_like(dgate_ref)

        hg, at, dx = hg_ref[...], at_ref[...], dx_ref[...]
        mix = _dot(hg, w_ref[0:512, :]) + _dot(at, w_ref[512:1024, :])
        dgate_ref[...] += _rowsum(dx * mix)
        dmix = (mod_ref[:, 2 * D_MODEL:3 * D_MODEL] * dx).astype(BF16)
        dhg_ref[...] = _dot_nt(dmix, w_ref[0:512, :])
        dat_ref[...] = _dot_nt(dmix, w_ref[512:1024, :])
        dw_ref[0:512, :] += _dot_tn(hg, dmix)
        dw_ref[512:1024, :] += _dot_tn(at, dmix)

    return pl.pallas_call(
        body, name="out_bwd", grid=(s // tm,),
        out_shape=[jax.ShapeDtypeStruct((s, 512), F32)] * 2
        + [jax.ShapeDtypeStruct((D_MODEL, D_MODEL), F32), jax.ShapeDtypeStruct((1, D_MODEL), F32)],
        in_specs=[_rows(tm, D_MODEL), _rows(tm, 512), _rows(tm, 512), _whole((1, 6 * D_MODEL)), _whole((D_MODEL, D_MODEL))],
        out_specs=[_rows(tm, 512), _rows(tm, 512), _whole((D_MODEL, D_MODEL)), _whole((1, D_MODEL))],
        compiler_params=_params("arbitrary"),
    )(dx1, hg, at, mod, w_out)


FFN_CHUNK = 2816


def _ffn(x1, target, mod, g2, gf, w_gu, w_down):
    s = x1.shape[0]
    tm = 256
    n_chunks = D_FF // FFN_CHUNK

    def body(x_ref, t_ref, mod_ref, g2_ref, gf_ref, wgu_hbm, wd_hbm,
             dx_ref, h2_ref, act_ref, dau_ref, dff_ref, sums_ref, loss_ref, wgu, wd, a_s, u_s, sem):
        @pl.when(pl.program_id(0) == 0)
        def _():
            c1 = pltpu.make_async_copy(wgu_hbm, wgu, sem.at[0])
            c2 = pltpu.make_async_copy(wd_hbm, wd, sem.at[1])
            c1.start()
            c2.start()
            c1.wait()
            c2.wait()
            sums_ref[...] = jnp.zeros_like(sums_ref)
            loss_ref[...] = jnp.zeros_like(loss_ref)

        x1v = x_ref[...]
        xhat, rstd = _rms(x1v)
        g2 = g2_ref[...]
        n2 = xhat * g2
        scale2 = 1.0 + mod_ref[:, 4 * D_MODEL:5 * D_MODEL]
        gate2 = mod_ref[:, 5 * D_MODEL:6 * D_MODEL]
        hb = (n2 * scale2 + mod_ref[:, 3 * D_MODEL:4 * D_MODEL]).astype(BF16)
        h2_ref[...] = hb
        ff = jnp.zeros((tm, D_MODEL), F32)
        for j in range(n_chunks):
            c = slice(j * FFN_CHUNK, (j + 1) * FFN_CHUNK)
            cu = slice(D_FF + j * FFN_CHUNK, D_FF + (j + 1) * FFN_CHUNK)
            a = _dot(hb, wgu[:, c])
            u = _dot(hb, wgu[:, cu])
            a_s[:, c] = a
            u_s[:, c] = u
            act = (_silu(a) * u).astype(BF16)
            act_ref[:, c] = act
            ff += _dot(act, wd[c, :])
        x2 = x1v + gate2 * ff
        nf, rstd_f = _rms(x2)
        gfv = gf_ref[...]
        err = nf * gfv - t_ref[...]
        loss_ref[...] += 0.5 * jnp.sum(_rowsum(err * err), axis=-1, keepdims=True) * (1.0 / D_MODEL)
        dy = err * (1.0 / D_MODEL)
        dx2 = _rms_bwd(dy * gfv, nf, rstd_f)
        dffb = (gate2 * dx2).astype(BF16)
        dff_ref[...] = dffb
        dh = jnp.zeros((tm, D_MODEL), F32)
        for j in range(n_chunks):
            c = slice(j * FFN_CHUNK, (j + 1) * FFN_CHUNK)
            cu = slice(D_FF + j * FFN_CHUNK, D_FF + (j + 1) * FFN_CHUNK)
            dact = _dot_nt(dffb, wd[c, :])
            a, u = a_s[:, c], u_s[:, c]
            da = (dact * u * _dsilu(a)).astype(BF16)
            du = (dact * _silu(a)).astype(BF16)
            dau_ref[:, c] = da
            dau_ref[:, cu] = du
            dh += _dot_nt(da, wgu[:, c]) + _dot_nt(du, wgu[:, cu])
        dn = dh * scale2
        sums_ref[0:1, :] += _rowsum(dh)
        sums_ref[1:2, :] += _rowsum(dh * n2)
        sums_ref[2:3, :] += _rowsum(dx2 * ff)
        sums_ref[3:4, :] += _rowsum(dn * xhat)
        sums_ref[4:5, :] += _rowsum(dy * nf)
        dx_ref[...] = dx2 + _rms_bwd(dn * g2, xhat, rstd)

    vec = _whole((1, D_MODEL))
    hbm = pl.BlockSpec(memory_space=pl.ANY)
    return pl.pallas_call(
        body, name="ffn", grid=(s // tm,),
        out_shape=[jax.ShapeDtypeStruct((s, D_MODEL), F32), jax.ShapeDtypeStruct((s, D_MODEL), BF16),
                   jax.ShapeDtypeStruct((s, D_FF), BF16), jax.ShapeDtypeStruct((s, 2 * D_FF), BF16),
                   jax.ShapeDtypeStruct((s, D_MODEL), BF16), jax.ShapeDtypeStruct((8, D_MODEL), F32),
                   jax.ShapeDtypeStruct((1, 128), F32)],
        in_specs=[_rows(tm, D_MODEL), _rows(tm, D_MODEL), _whole((1, 6 * D_MODEL)), vec, vec, hbm, hbm],
        out_specs=[_rows(tm, D_MODEL), _rows(tm, D_MODEL), _rows(tm, D_FF), _rows(tm, 2 * D_FF), _rows(tm, D_MODEL),
                   _whole((8, D_MODEL)), _whole((1, 128))],
        scratch_shapes=[pltpu.VMEM((D_MODEL, 2 * D_FF), BF16), pltpu.VMEM((D_FF, D_MODEL), BF16),
                        pltpu.VMEM((tm, D_FF), F32), pltpu.VMEM((tm, D_FF), F32), pltpu.SemaphoreType.DMA((2,))],
        compiler_params=_params("arbitrary"),
    )(x1, target, mod, g2, gf, w_gu, w_down)


def _weight_grad(a, b, name):
    s, m = a.shape
    n = b.shape[1]
    ts, tn = min(s, 2048), 512

    def body(a_ref, b_ref, o_ref):
        @pl.when(pl.program_id(1) == 0)
        def _():
            o_ref[...] = jnp.zeros_like(o_ref)

        o_ref[...] += _dot_tn(a_ref[...], b_ref[...])

    return pl.pallas_call(
        body, name=name, grid=(n // tn, s // ts), out_shape=jax.ShapeDtypeStruct((m, n), F32),
        in_specs=[pl.BlockSpec((ts, m), lambda j, i: (i, 0)), pl.BlockSpec((ts, tn), lambda j, i: (i, j))],
        out_specs=pl.BlockSpec((m, tn), lambda j, i: (0, j)),
        compiler_params=_params("parallel", "arbitrary"),
    )(a, b)


def _weight_grad_by_owner(a, b, name):
    s, m = a.shape
    n = b.shape[1]
    cb = n // N_DEV
    ts = min(s, 2048)
    n_steps = s // ts

    def body(a_ref, b_ref, o_ref, acc, low):
        @pl.when(pl.program_id(1) == 0)
        def _():
            acc[...] = jnp.zeros_like(acc)

        acc[...] += _dot_tn(a_ref[...], b_ref[...])

        @pl.when(pl.program_id(1) == n_steps - 1)
        def _():
            low[...] = pltpu.roll(acc[...], cb, axis=1)
            o_ref[0] = acc[:, 0:cb]
            o_ref[1] = low[:, 0:cb]

    return pl.pallas_call(
        body, name=name, grid=(4, n_steps), out_shape=jax.ShapeDtypeStruct((4, 2, m, cb), F32),
        in_specs=[pl.BlockSpec((ts, m), lambda j, i: (i, 0)), pl.BlockSpec((ts, 2 * cb), lambda j, i: (i, j))],
        out_specs=pl.BlockSpec((None, 2, m, cb), lambda j, i: (j, 0, 0, 0)),
        scratch_shapes=[pltpu.VMEM((m, 2 * cb), F32), pltpu.VMEM((m, 2 * cb), F32)],
        compiler_params=_params("parallel", "arbitrary"),
    )(a, b)


def _adamw_math(w, g, m, v):
    m = ADAM_B1 * m + (1.0 - ADAM_B1) * g
    v = ADAM_B2 * v + (1.0 - ADAM_B2) * (g * g)
    m_hat = m / (1.0 - ADAM_B1 ** ADAM_STEP)
    v_hat = v / (1.0 - ADAM_B2 ** ADAM_STEP)
    delta = -ADAM_LR * (m_hat / (jnp.sqrt(v_hat) + ADAM_EPS) + ADAM_WD * w)
    return delta, m, v


def _adamw_shard(chip, w, m, v, partial, got, name):
    r, c = w.shape
    tr = _shard_rows(r)

    def body(chip_ref, w_ref, m_ref, v_ref, own_ref, g0, g1, g2, grad_ref, d_ref, nm_ref, nv_ref):
        g = ((own_ref[...] + g0[...].astype(F32)) + g1[...].astype(F32)) + g2[...].astype(F32)
        grad_ref[...] = g
        d_ref[...], nm_ref[...], nv_ref[...] = _adamw_math(w_ref[...], g, m_ref[...], v_ref[...])

    tile = pl.BlockSpec((tr, c), lambda i, chip_ref: (i, 0))
    own = pl.BlockSpec((None, tr, c), lambda i, chip_ref: (chip_ref[0], i, 0))
    part = [pl.BlockSpec((None, tr, c), functools.partial(lambda j, i, chip_ref: (j, i, 0), j)) for j in range(3)]
    return pl.pallas_call(
        body, name=name,
        grid_spec=pltpu.PrefetchScalarGridSpec(num_scalar_prefetch=1, grid=(r // tr,), in_specs=[tile] * 3 + [own] + part,
                                               out_specs=[tile] * 4),
        out_shape=[jax.ShapeDtypeStruct((r, c), F32)] * 4, compiler_params=_params("parallel"),
    )(chip, w, m, v, partial, got, got, got)


def _small_update(small_all, dmod_blocks, c_all, logits, w_ada, m_ada, v_ada, smalls):
    def body(sm_ref, dm_ref, c_ref, lg_ref, wa_ref, ma_ref, va_ref, *rest):
        ins, outs = rest[:21], rest[21:]
        _, me = _flip(0)
        tot = sm_ref[0:1, :]
        for i in range(1, N_DEV):
            tot = tot + sm_ref[i:i + 1, :]
        loss_ref = outs[0]
        loss_ref[...] = tot[:, SM_LOSS:SM_LOSS + 128]
        g_ada = lax.dot_general(_silu(c_ref[...]), dm_ref[me], (((0,), (0,)), ((), ())),
                                preferred_element_type=F32, precision=HIGHEST)
        outs[1][...] = g_ada
        outs[2][...], outs[3][...], outs[4][...] = _adamw_math(wa_ref[...], g_ada, ma_ref[...], va_ref[...])
        p0 = _lower_bound(lg_ref)
        dl0 = tot[:, SM_LB:SM_LB + 512] * p0 * (1.0 - p0)
        grads = [tot[:, SM_MOD:SM_MOD + 6 * D_MODEL], tot[:, SM_G1:SM_G1 + D_MODEL], tot[:, SM_G2:SM_G2 + D_MODEL],
                 tot[:, SM_GF:SM_GF + D_MODEL], tot[:, SM_ATT:SM_ATT + 512], tot[:, SM_HG:SM_HG + 128],
                 jnp.where(lax.broadcasted_iota(jnp.int32, (2, 512), 0) == 0, dl0, -dl0)]
        for i, g in enumerate(grads):
            w_ref, m_ref, v_ref = ins[3 * i:3 * i + 3]
            o = outs[5 + 4 * i:9 + 4 * i]
            o[0][...] = g
            o[1][...], o[2][...], o[3][...] = _adamw_math(w_ref[...], g, m_ref[...], v_ref[...])

    flat = [t for trio in smalls for t in trio]
    vm = pl.BlockSpec(memory_space=pltpu.VMEM)
    out_shape = [jax.ShapeDtypeStruct((1, 128), F32)] + [jax.ShapeDtypeStruct(w_ada.shape, F32)] * 4
    for trio in smalls:
        out_shape += [jax.ShapeDtypeStruct(trio[0].shape, F32)] * 4
    return pl.pallas_call(
        body, name="small_update", out_shape=out_shape,
        in_specs=[vm] * (7 + len(flat)), out_specs=[vm] * len(out_shape),
        compiler_params=pltpu.CompilerParams(vmem_limit_bytes=V7X_VMEM_LIMIT),
    )(small_all, dmod_blocks, c_all, logits, w_ada, m_ada, v_ada, *flat)


def kernel(x, c, w_ada, b_ada, norm1_g, w_in, hg_lb_logits, hg_onorm_g, att_onorm_g, w_out, norm2_g, w_gate_up, w_down, final_g, loss_target, m_w_ada, m_b_ada, m_norm1_g, m_w_in, m_hg_lb_logits, m_hg_onorm_g, m_att_onorm_g, m_w_out, m_norm2_g, m_w_gate_up, m_w_down, m_final_g, v_w_ada, v_b_ada, v_norm1_g, v_w_in, v_hg_lb_logits, v_hg_onorm_g, v_att_onorm_g, v_w_out, v_norm2_g, v_w_gate_up, v_w_down, v_final_g):
    x2d, target = x[0], loss_target[0]
    seq = x2d.shape[0]
    assert seq % (ATT_BLOCK * max(DILATIONS)) == 0 and seq % HG_TILE == 0
    gf = final_g.reshape(1, D_MODEL)

    c_all = _exchange_small(c.reshape(8, D_MODEL // 8), None, "gather_c").reshape(N_DEV, D_MODEL)
    ada = _ada_rows(c_all, w_ada[0], b_ada)
    mod = _exchange_small(ada, 1, "scatter_mod").reshape(1, 6 * D_MODEL)

    core = lax.axis_index("c").astype(jnp.int32).reshape(1)
    chip = (2 * lax.axis_index("x") + lax.axis_index("y")).astype(jnp.int32).reshape(1)
    me = 4 * lax.axis_index("x") + 2 * lax.axis_index("y") + lax.axis_index("c")

    g_in, = _gather_weights([w_in[0].astype(BF16)])
    w_in_b = jnp.transpose(g_in, (1, 0, 2)).reshape(D_MODEL, IN_WIDTH)
    rest_shards = [w_out[0].astype(BF16), w_gate_up[0].astype(BF16), w_down[0].astype(BF16)]
    lands = [lax.empty((N_DEV,) + s.shape, BF16) for s in rest_shards]
    g_send, g_recv, g_srcs, g_lands, tok = _copies_start("gather_rest_start", _plan_gather_own, 12, rest_shards, lands, [w_in_b, mod])
    flight = {}

    def stage(name, *vals):
        if name == "attention_begun":
            flight["shards"], got = _copies_wait("gather_rest_wait", _plan_gather_own, g_send, g_recv, g_srcs, g_lands, [vals[0]])
            flight["pass"] = _copies_start("gather_pass_start", _plan_gather_pass, 9, [], got, [])
            return flight["pass"][4][0:1, 0:1]
        if name == "mixer_weights_done":
            dw_gu, dw_down, dw_out = vals
            flight["grads"] = [dw_out.reshape(4, 2, D_MODEL // N_DEV, D_MODEL),
                               dw_gu,
                               dw_down.reshape(4, 2, D_FF // N_DEV, D_MODEL)]
            pair_lands = [lax.empty((4,) + g.shape[2:], F32) for g in flight["grads"]]
            flight["pairs"] = _copies_start("reduce_pairs_start", _plan_reduce_pairs, 12, flight["grads"], pair_lands, [])
            return flight["pairs"][4][0:1, 0:1]
        if name == "attention_backward_begun":
            s, r, srcs, pl_lands, _ = flight["pairs"]
            grads, got = _copies_wait("reduce_pairs_wait", _plan_reduce_pairs, s, r, srcs, pl_lands, [vals[0]])
            flight["sums"] = [_pair_sum(core, g, b, f"pair_sum_{i}") for i, (g, b) in enumerate(zip(grads, got))]
            chip_lands = [lax.empty((3,) + s16.shape[1:], BF16) for _, s16 in flight["sums"]]
            flight["chips"] = _copies_start("reduce_chips_start", _plan_reduce_chips, 9, [s16 for _, s16 in flight["sums"]], chip_lands, [])
            return flight["chips"][4][0:1, 0:1]
        raise ValueError(name)

    def rest_weights(after):
        s, r, _, p_lands, _ = flight["pass"]
        _, got = _copies_wait("gather_pass_wait", _plan_gather_pass, s, r, [], p_lands, [after])
        full = [lax.dynamic_update_index_in_dim(g, shard, me, 0) for g, shard in zip(got, flight["shards"])]
        return (full[0].reshape(D_MODEL, D_MODEL), jnp.transpose(full[1], (1, 0, 2)).reshape(D_MODEL, 2 * D_FF),
                full[2].reshape(D_FF, D_MODEL))

    grad_x, dw_in, small = _block_step(x2d, target, mod + tok[0:1, 0:1], norm1_g, hg_lb_logits, hg_onorm_g, att_onorm_g, norm2_g, gf,
                                       w_in_b, rest_weights, stage)

    g_in8 = dw_in
    in_pairs = _copies_start("reduce_pairs_in_start", _plan_reduce_pairs, 4, [g_in8], [lax.empty((4,) + g_in8.shape[2:], F32)], [])
    s, r, srcs, c_lands, _ = flight["chips"]
    _, recv_rest = _copies_wait("reduce_chips_wait", _plan_reduce_chips, s, r, srcs, c_lands, [in_pairs[4]])
    small_rows = jnp.pad(small + in_pairs[4][0:1, 0:1], ((0, 0), (0, SM_PADDED - SM_WIDTH))).reshape(SM_PADDED // 128, 128)
    small_all = _exchange_small(small_rows, None, "gather_small").reshape(N_DEV, SM_PADDED)[:, :SM_WIDTH]
    big = {}
    rest_params = [("w_out", w_out, m_w_out, v_w_out), ("w_gate_up", w_gate_up, m_w_gate_up, v_w_gate_up), ("w_down", w_down, m_w_down, v_w_down)]
    for (n, w, m, v), (s32, _), got in zip(rest_params, flight["sums"], recv_rest):
        big[n] = [t[None] for t in _adamw_shard(chip, w[0], m[0], v[0], s32, got, f"adamw_{n}")]
    in_grads, got_in = _copies_wait("reduce_pairs_in_wait", _plan_reduce_pairs, in_pairs[0], in_pairs[1], in_pairs[2], in_pairs[3],
                                    [big[n][3] for n, _, _, _ in rest_params] + [small_all])
    in_s32, in_s16 = _pair_sum(core, in_grads[0], got_in[0], "pair_sum_in")
    in_chips = _copies_start("reduce_chips_in_start", _plan_reduce_chips, 3, [in_s16], [lax.empty((3,) + in_s16.shape[1:], BF16)], [])
    c_all = c_all + in_chips[4][0:1, 0:1]
    smalls = [(b_ada, m_b_ada, v_b_ada), (norm1_g, m_norm1_g, v_norm1_g), (norm2_g, m_norm2_g, v_norm2_g),
              (gf, m_final_g.reshape(1, D_MODEL), v_final_g.reshape(1, D_MODEL)),
              (att_onorm_g, m_att_onorm_g, v_att_onorm_g), (hg_onorm_g, m_hg_onorm_g, v_hg_onorm_g),
              (hg_lb_logits, m_hg_lb_logits, v_hg_lb_logits)]
    dmod_blocks = small_all[:, :6 * D_MODEL].reshape(N_DEV, N_DEV, 6 * D_MODEL // N_DEV).transpose(1, 0, 2)
    res = _small_update(small_all, dmod_blocks, c_all, hg_lb_logits, w_ada[0], m_w_ada[0], v_w_ada[0], smalls)
    _, recv_in = _copies_wait("reduce_chips_in_wait", _plan_reduce_chips, in_chips[0], in_chips[1], in_chips[2], in_chips[3], [res[0]])
    big["w_in"] = [t[None] for t in _adamw_shard(chip, w_in[0], m_w_in[0], v_w_in[0], in_s32, recv_in[0], "adamw_w_in")]
    loss = res[0][0, 0]
    ada4 = [t[None] for t in res[1:5]]
    sm4 = {n: list(res[5 + 4 * i:9 + 4 * i]) for i, n in enumerate(["b_ada", "norm1_g", "norm2_g", "final_g", "att", "hg", "lb"])}
    sm4["final_g"] = [t.reshape(D_MODEL) for t in sm4["final_g"]]

    order = [ada4, sm4["b_ada"], sm4["norm1_g"], big["w_in"], sm4["lb"], sm4["hg"], sm4["att"], big["w_out"], sm4["norm2_g"],
             big["w_gate_up"], big["w_down"], sm4["final_g"]]
    return (loss, grad_x[None], *[o[0] for o in order], *[o[1] for o in order], *[o[2] for o in order], *[o[3] for o in order])


def _block_step(x2d, target, mod, norm1_g, hg_lb_logits, hg_onorm_g, att_onorm_g, norm2_g, gf, w_in_b, rest_weights, stage):
    h1, hq, hf, hi, hgt, aq, ak, av = _in_fwd(x2d, mod, norm1_g, w_in_b)
    hg_out, hg_o, hg_states = _hg_fwd(hq, hf, hi, hgt, hg_lb_logits, hg_onorm_g)
    branch = [_att_fwd(aq, ak, av, DILATIONS[0])]
    att_g = att_onorm_g + stage("attention_begun", branch[0][0])
    branch += [_att_fwd(aq, ak, av, d) for d in DILATIONS[1:]]
    outs = [b[0] for b in branch]
    lses = [b[1] for b in branch]
    att, att_out = _att_combine(outs, lses, att_g)
    w_out_b, w_gu_b, w_down_b = rest_weights(att_out)
    x1 = _out_fwd(x2d, hg_out, att_out, mod, w_out_b)

    dx1, h2, act, dau, dff, ffn_sums, loss_part = _ffn(x1, target, mod, norm2_g, gf, w_gu_b, w_down_b)
    dw_gu = _weight_grad_by_owner(h2, dau, "dw_gate_up")
    dw_down = _weight_grad(act, dff, "dw_down")

    dhg, dat, dw_out, dgate1 = _out_bwd(dx1, hg_out, att_out, mod, w_out_b)
    att_g = att_onorm_g + stage("mixer_weights_done", dw_gu, dw_down, dw_out)
    comb = _att_combine_bwd(dat, att, lses, att_g)
    dos, ccs, d_att_g = comb[0:3], comb[3:6], comb[6]
    hg_g = hg_onorm_g + stage("attention_backward_begun", comb[3])
    datt = []
    for i, d in enumerate(DILATIONS):
        datt.append(_att_bwd(aq, ak, av, dos[i], ccs[i], lses[i], d))
    dhq, dhf, dhi, dhgt, d_hg_g, d_lb = _hg_bwd(hq, hf, hi, hgt, hg_lb_logits, hg_g, hg_o, hg_states, dhg)
    dps = [dhq, dhf, dhi, dhgt] + [datt[i][j] for j in range(3) for i in range(3)]
    grad_x, dp_b, dshift1, dscale1, d_g1 = _in_bwd(x2d, dx1, mod, norm1_g, w_in_b, dps)
    dw_in = _weight_grad_by_owner(h1, dp_b, "dw_in")
    small = jnp.concatenate([dshift1, dscale1, dgate1, ffn_sums[0:1], ffn_sums[1:2], ffn_sums[2:3], d_g1, ffn_sums[3:4],
                             ffn_sums[4:5], d_att_g, d_lb, d_hg_g, loss_part], axis=1)
    return grad_x, dw_in, small
```

```python
import functools

import jax
import jax.numpy as jnp
from jax import lax
from jax.experimental import pallas as pl
from jax.experimental.pallas import tpu as pltpu

F32 = jnp.float32
BF16 = jnp.bfloat16
HIGHEST = lax.Precision.HIGHEST
MESH = pl.DeviceIdType.MESH

D_MODEL = 1024
N_DEV = 8
HG_HEADS = 4
HG_DIM = 128
HG_WIDTH = HG_HEADS * HG_DIM
HG_CHUNK = 128
ATT_WIDTH = 512
ATT_HEAD_DIM = 64
ATT_BLOCK = 128
DILATIONS = (1, 4, 16)
ATT_SCALE = ATT_HEAD_DIM ** -0.5
D_FF = 2816
IN_WIDTH = 7 * 512
RMS_EPS = 1e-6
NEG = -1e30

ADAM_LR = 0.001
ADAM_B1 = 0.9
ADAM_B2 = 0.999
ADAM_EPS = 1e-08
ADAM_WD = 0.01
ADAM_STEP = 10

V7X_VMEM_LIMIT = 56 * 1024 * 1024

SM_MOD = 0
SM_G1 = 6 * D_MODEL
SM_G2 = 7 * D_MODEL
SM_GF = 8 * D_MODEL
SM_ATT = 9 * D_MODEL
SM_LB = 9 * D_MODEL + 512
SM_HG = 10 * D_MODEL
SM_LOSS = 10 * D_MODEL + 128
SM_WIDTH = 10 * D_MODEL + 256
SM_PADDED = 88 * 128


def _params(*sem, vmem=V7X_VMEM_LIMIT):
    return pltpu.CompilerParams(dimension_semantics=sem, vmem_limit_bytes=vmem)


def _dot(a, b):
    return jnp.dot(a, b, preferred_element_type=F32)


def _dot_nt(a, b):
    return lax.dot_general(a, b, (((1,), (1,)), ((), ())), preferred_element_type=F32)


def _dot_tn(a, b):
    return lax.dot_general(a, b, (((0,), (0,)), ((), ())), preferred_element_type=F32)


def _dot_f32(a, b):
    return jnp.dot(a, b, preferred_element_type=F32, precision=HIGHEST)


def _sigmoid(x):
    return 1.0 / (1.0 + jnp.exp(-x))


def _silu(x):
    return x * _sigmoid(x)


def _dsilu(x):
    s = _sigmoid(x)
    return s * (1.0 + x * (1.0 - s))


def _rms(x):
    rstd = lax.rsqrt(jnp.mean(x * x, axis=-1, keepdims=True) + RMS_EPS)
    return x * rstd, rstd


def _rms_bwd(dn, xhat, rstd):
    return rstd * (dn - xhat * jnp.mean(dn * xhat, axis=-1, keepdims=True))


def _rowsum(x):
    return jnp.sum(x, axis=0, keepdims=True)


def _rows(tm, n):
    return pl.BlockSpec((tm, n), lambda i: (i, 0))


def _whole(shape):
    return pl.BlockSpec(shape, lambda i: (0,) * len(shape))


def _mesh_pos():
    return lax.axis_index("x"), lax.axis_index("y"), lax.axis_index("c")


def _flip(k):
    x, y, c = _mesh_pos()
    px = 1 - x if k & 4 else x
    py = 1 - y if k & 2 else y
    pc = 1 - c if k & 1 else c
    return (px, py, pc), 4 * px + 2 * py + pc


def _exchange_small(x, rows_per_peer, name):
    r_all, cols = x.shape
    r_out = r_all if rows_per_peer is None else rows_per_peer

    def body(x_ref, out_ref, send_sems, recv_sems):
        _, me = _flip(0)

        def src(pid):
            if rows_per_peer is None:
                return x_ref
            return x_ref.at[pl.ds(pl.multiple_of(pid * r_out, r_out), r_out), :]

        if rows_per_peer is None:
            out_ref[me] = x_ref[...]
        else:
            out_ref[me] = x_ref[pl.ds(pl.multiple_of(me * r_out, r_out), r_out), :]
        sends = []
        for k in range(1, N_DEV):
            dev, pid = _flip(k)
            cp = pltpu.make_async_remote_copy(src_ref=src(pid), dst_ref=out_ref.at[me], send_sem=send_sems.at[k - 1],
                                              recv_sem=recv_sems.at[k - 1], device_id=dev, device_id_type=MESH)
            cp.start()
            sends.append(cp)
        for k in range(1, N_DEV):
            dev, pid = _flip(k)
            pltpu.make_async_remote_copy(src_ref=src(pid), dst_ref=out_ref.at[pid], send_sem=send_sems.at[k - 1],
                                         recv_sem=recv_sems.at[k - 1], device_id=dev, device_id_type=MESH).wait_recv()
        for cp in sends:
            cp.wait_send()

    return pl.pallas_call(
        body, name=name,
        out_shape=jax.ShapeDtypeStruct((N_DEV, r_out, cols), x.dtype),
        in_specs=[pl.BlockSpec(memory_space=pltpu.VMEM)],
        out_specs=pl.BlockSpec(memory_space=pltpu.VMEM),
        scratch_shapes=[pltpu.SemaphoreType.DMA((N_DEV - 1,)), pltpu.SemaphoreType.DMA((N_DEV - 1,))],
    )(x)


def _gather_weights(shards):
    n = len(shards)

    def body(*refs):
        xs, outs = refs[:n], refs[n:2 * n]
        send_sems, recv_sems, local_sems = refs[2 * n:]
        x, y, c = _mesh_pos()
        me, sibling = (x, y, c), (x, y, 1 - c)
        chips = [(1 - x, y), (x, 1 - y), (1 - x, 1 - y)]

        def blk(a, px, py, pc):
            return outs[a].at[4 * px + 2 * py + pc]

        def copy(a, k, block, to, src=None):
            return pltpu.make_async_remote_copy(
                src_ref=blk(a, *block) if src is None else src, dst_ref=blk(a, *block),
                send_sem=send_sems.at[a * 7 + k], recv_sem=recv_sems.at[a * 7 + k], device_id=to, device_id_type=MESH)

        mine = [pltpu.make_async_copy(xs[a], blk(a, *me), local_sems.at[a]) for a in range(n)]
        for cp in mine:
            cp.start()
        first = []
        for a in range(n):
            first.append(copy(a, 0, me, sibling, src=xs[a]))
            first += [copy(a, 1 + j, me, (*chip, c), src=xs[a]) for j, chip in enumerate(chips)]
        for cp in first:
            cp.start()
        passed = []
        for j, chip in enumerate(chips):
            for a in range(n):
                copy(a, 1 + j, (*chip, c), me).wait_recv()
                cp = copy(a, 4 + j, (*chip, c), sibling)
                cp.start()
                passed.append(cp)
        for a in range(n):
            copy(a, 0, sibling, me).wait_recv()
            for j, chip in enumerate(chips):
                copy(a, 4 + j, (*chip, 1 - c), me).wait_recv()
        for cp in first + passed:
            cp.wait_send()
        for cp in mine:
            cp.wait()

    hbm = pl.BlockSpec(memory_space=pl.ANY)
    return pl.pallas_call(
        body, name="gather_weights",
        out_shape=[jax.ShapeDtypeStruct((N_DEV,) + s.shape, s.dtype) for s in shards],
        in_specs=[hbm] * n, out_specs=[hbm] * n,
        scratch_shapes=[pltpu.SemaphoreType.DMA((7 * n,)), pltpu.SemaphoreType.DMA((7 * n,)), pltpu.SemaphoreType.DMA((n,))],
    )(*shards)


_HBM = pl.BlockSpec(memory_space=pltpu.HBM)
_SEM = pl.BlockSpec(memory_space=pltpu.SEMAPHORE)
_DATAFLOW = pltpu.SideEffectType.DATAFLOW_SIDE_EFFECTING


def _copies_start(name, plan, n_copies, srcs, lands, after):
    bufs = list(srcs) + list(lands)
    nb = len(bufs)

    def body(*refs):
        ins, send_sems, recv_sems, token = refs[:nb], refs[nb + len(after)], refs[nb + len(after) + 1], refs[-1]
        for i, (src, dst, dev) in enumerate(plan(ins[:len(srcs)], ins[len(srcs):])):
            pltpu.make_async_remote_copy(src_ref=src, dst_ref=dst, send_sem=send_sems.at[i], recv_sem=recv_sems.at[i],
                                         device_id=dev, device_id_type=MESH).start()
        token[...] = jnp.zeros_like(token)

    outs = pl.pallas_call(
        body, name=name,
        out_shape=(pltpu.SemaphoreType.DMA((n_copies,)), pltpu.SemaphoreType.DMA((n_copies,)),
                   *[pltpu.HBM(b.shape, b.dtype) for b in bufs], jax.ShapeDtypeStruct((8, 128), F32)),
        in_specs=[_HBM] * nb + [pl.BlockSpec(memory_space=pl.ANY)] * len(after),
        out_specs=(_SEM, _SEM, *[_HBM] * nb, pl.BlockSpec(memory_space=pltpu.VMEM)),
        input_output_aliases={i: 2 + i for i in range(nb)},
        compiler_params=pltpu.CompilerParams(has_side_effects=_DATAFLOW),
    )(*[pltpu.with_memory_space_constraint(b, pltpu.HBM) for b in bufs], *after)
    return outs[0], outs[1], list(outs[2:2 + len(srcs)]), list(outs[2 + len(srcs):2 + nb]), outs[-1]


def _copies_wait(name, plan, send_sems, recv_sems, srcs, lands, after):
    bufs = list(srcs) + list(lands)
    nb = len(bufs)

    def body(*refs):
        ins, send_ref, recv_ref = refs[:nb], refs[nb], refs[nb + 1]
        for i, (src, dst, dev) in enumerate(plan(ins[:len(srcs)], ins[len(srcs):])):
            cp = pltpu.make_async_remote_copy(src_ref=src, dst_ref=dst, send_sem=send_ref.at[i], recv_sem=recv_ref.at[i],
                                              device_id=dev, device_id_type=MESH)
            cp.wait_send()
            cp.wait_recv()

    outs = pl.pallas_call(
        body, name=name, out_shape=[pltpu.HBM(b.shape, b.dtype) for b in bufs],
        in_specs=[_HBM] * nb + [_SEM, _SEM] + [pl.BlockSpec(memory_space=pl.ANY)] * len(after), out_specs=[_HBM] * nb,
        input_output_aliases={i: i for i in range(nb)},
        compiler_params=pltpu.CompilerParams(has_side_effects=_DATAFLOW),
    )(*bufs, send_sems, recv_sems, *after)
    return list(outs[:len(srcs)]), list(outs[len(srcs):])


def _plan_gather_own(srcs, lands):
    _, me = _flip(0)
    return [(srcs[a], lands[a].at[me], _flip(k)[0]) for a in range(len(srcs)) for k in (1, 4, 2, 6)]


def _plan_gather_pass(srcs, lands):
    sibling = _flip(1)[0]
    plan = []
    for land in lands:
        for k in (4, 2, 6):
            block = land.at[_flip(k)[1]]
            plan.append((block, block, sibling))
    return plan


def _plan_reduce_pairs(srcs, lands):
    x, y, c = _mesh_pos()
    return [(srcs[a].at[chip, 1 - c], lands[a].at[chip], (x, y, 1 - c)) for a in range(len(srcs)) for chip in range(4)]


def _plan_reduce_chips(srcs, lands):
    plan = []
    for a in range(len(srcs)):
        for j, k in enumerate((4, 2, 6)):
            dev = _flip(k)[0]
            plan.append((srcs[a].at[2 * dev[0] + dev[1]], lands[a].at[j], dev))
    return plan


def _shard_rows(r):
    return r // 2 if r % 32 == 0 else r


def _pair_sum(core, grads, got, name):
    _, _, r, c = grads.shape
    tr = _shard_rows(r)

    def body(core_ref, a_ref, b_ref, o_ref, ob_ref):
        s = a_ref[...] + b_ref[...]
        o_ref[...] = s
        ob_ref[...] = s.astype(BF16)

    spec = pl.BlockSpec((None, tr, c), lambda i, j, core_ref: (i, j, 0))
    return pl.pallas_call(
        body, name=name,
        grid_spec=pltpu.PrefetchScalarGridSpec(
            num_scalar_prefetch=1, grid=(4, r // tr),
            in_specs=[pl.BlockSpec((None, None, tr, c), lambda i, j, core_ref: (i, core_ref[0], j, 0)), spec],
            out_specs=[spec, spec]),
        out_shape=[jax.ShapeDtypeStruct((4, r, c), F32), jax.ShapeDtypeStruct((4, r, c), BF16)],
        compiler_params=_params("parallel", "parallel"),
    )(core, grads, got)


def _ada_rows(c_all, w_ada, b_ada):
    n_cols = w_ada.shape[1]

    def body(c_ref, w_ref, b_ref, o_ref):
        _, me = _flip(0)
        bias = b_ref[:, pl.ds(pl.multiple_of(me * n_cols, 128), n_cols)]
        o_ref[...] = _dot_f32(_silu(c_ref[...]), w_ref[...]) + bias

    return pl.pallas_call(
        body, name="ada_rows", out_shape=jax.ShapeDtypeStruct((N_DEV, n_cols), F32),
        in_specs=[pl.BlockSpec(memory_space=pltpu.VMEM)] * 3, out_specs=pl.BlockSpec(memory_space=pltpu.VMEM),
    )(c_all, w_ada, b_ada)


def _in_fwd(x, mod, g1, w_in):
    s = x.shape[0]
    tm = 256

    def body(x_ref, mod_ref, g_ref, w_ref, h_ref, *outs):
        xhat, _ = _rms(x_ref[...])
        h = (xhat * g_ref[...]) * (1.0 + mod_ref[:, D_MODEL:2 * D_MODEL]) + mod_ref[:, 0:D_MODEL]
        hb = h.astype(BF16)
        h_ref[...] = hb
        for j, o_ref in enumerate(outs):
            o_ref[...] = _dot(hb, w_ref[:, j * 512:(j + 1) * 512])

    return pl.pallas_call(
        body, name="in_fwd", grid=(s // tm,),
        out_shape=[jax.ShapeDtypeStruct((s, D_MODEL), BF16)] + [jax.ShapeDtypeStruct((s, 512), F32)] * 7,
        in_specs=[_rows(tm, D_MODEL), _whole((1, 6 * D_MODEL)), _whole((1, D_MODEL)), _whole((D_MODEL, IN_WIDTH))],
        out_specs=[_rows(tm, D_MODEL)] + [_rows(tm, 512)] * 7,
        compiler_params=_params("parallel"),
    )(x, mod, g1, w_in)


def _in_bwd(x, dx1, mod, g1, w_in, dps):
    s = x.shape[0]
    tm = 256

    def body(x_ref, dx_ref, mod_ref, g_ref, w_ref, *rest):
        dp_refs, (gx_ref, dpb_ref, dsh_ref, dsc_ref, dg_ref) = rest[:13], rest[13:]
        pieces = [dp_refs[j][...] for j in range(4)]
        pieces += [dp_refs[4 + 3 * j][...] + dp_refs[5 + 3 * j][...] + dp_refs[6 + 3 * j][...] for j in range(3)]
        for j, p in enumerate(pieces):
            dpb_ref[:, j * 512:(j + 1) * 512] = p.astype(BF16)
        dh = _dot_nt(dpb_ref[...], w_ref[...])
        xhat, rstd = _rms(x_ref[...])
        g = g_ref[...]
        scale1 = 1.0 + mod_ref[:, D_MODEL:2 * D_MODEL]
        n1 = xhat * g

        @pl.when(pl.program_id(0) == 0)
        def _():
            dsh_ref[...] = jnp.zeros_like(dsh_ref)
            dsc_ref[...] = jnp.zeros_like(dsc_ref)
            dg_ref[...] = jnp.zeros_like(dg_ref)

        dsh_ref[...] += _rowsum(dh)
        dsc_ref[...] += _rowsum(dh * n1)
        dn = dh * scale1
        dg_ref[...] += _rowsum(dn * xhat)
        gx_ref[...] = dx_ref[...] + _rms_bwd(dn * g, xhat, rstd)

    vec = _whole((1, D_MODEL))
    return pl.pallas_call(
        body, name="in_bwd", grid=(s // tm,),
        out_shape=[jax.ShapeDtypeStruct((s, D_MODEL), F32), jax.ShapeDtypeStruct((s, IN_WIDTH), BF16)]
        + [jax.ShapeDtypeStruct((1, D_MODEL), F32)] * 3,
        in_specs=[_rows(tm, D_MODEL), _rows(tm, D_MODEL), _whole((1, 6 * D_MODEL)), vec, _whole((D_MODEL, IN_WIDTH))]
        + [_rows(tm, 512)] * 13,
        out_specs=[_rows(tm, D_MODEL), _rows(tm, IN_WIDTH), vec, vec, vec],
        compiler_params=_params("arbitrary"),
    )(x, dx1, mod, g1, w_in, *dps)


HG_TILE = 512
HG_TILE_CHUNKS = HG_TILE // HG_CHUNK


def _lower_bound(lg_ref):
    return 1.0 / (1.0 + jnp.exp(lg_ref[1:2, :] - lg_ref[0:1, :]))


def _chunk_masks():
    r = lax.broadcasted_iota(jnp.int32, (HG_CHUNK, HG_CHUNK), 0)
    c = lax.broadcasted_iota(jnp.int32, (HG_CHUNK, HG_CHUNK), 1)
    return r >= c, c >= r, (r >= c).astype(F32), (c >= r).astype(F32)


def _hg_fwd(hq, hf, hi, hgt, logits, onorm_g):
    s = hq.shape[0]
    n_tiles = s // HG_TILE

    def body(q_ref, f_ref, i_ref, g_ref, lg_ref, og_ref, out_ref, o_ref, st_ref, state, qf_s, kk_s, lf_s):
        @pl.when(pl.program_id(0) == 0)
        def _():
            state[...] = jnp.zeros_like(state)

        lb = _lower_bound(lg_ref)
        f = lb + (1.0 - lb) * _sigmoid(f_ref[...])
        kk_s[...] = 1.0 - f
        lf_s[...] = jnp.log(f)
        qf_s[...] = _silu(q_ref[...])
        causal, _, tri, _ = _chunk_masks()

        def chunk(ci, carry):
            rows = pl.ds(pl.multiple_of(ci * HG_CHUNK, HG_CHUNK), HG_CHUNK)
            srows = pl.ds(pl.multiple_of(ci * HG_DIM, HG_DIM), HG_DIM)
            lf = lf_s[rows, :]
            b = _dot_f32(tri, lf)
            bl = _rowsum(lf)
            ref = 0.5 * bl
            qf, kk, v = qf_s[rows, :], kk_s[rows, :], i_ref[rows, :]
            a_in = (qf * jnp.exp(b)).astype(BF16)
            a_t = (qf * jnp.exp(b - ref)).astype(BF16)
            b_t = (kk * jnp.exp(ref - b)).astype(BF16)
            kd = kk * jnp.exp(bl - b)
            ebl = jnp.exp(bl)
            vb = v.astype(BF16)
            for h in range(HG_HEADS):
                c = slice(h * HG_DIM, (h + 1) * HG_DIM)
                st = state[h]
                st_ref[srows, c] = st
                p = jnp.where(causal, _dot_nt(a_t[:, c], b_t[:, c]), 0.0)
                o_ref[rows, c] = _dot(p.astype(BF16), vb[:, c]) + _dot_nt(a_in[:, c], st.astype(BF16))
                state[h] = st * ebl[:, c] + _dot_tn(vb[:, c], kd[:, c].astype(BF16))
            return carry

        lax.fori_loop(0, HG_TILE_CHUNKS, chunk, 0, unroll=2)
        for h in range(HG_HEADS):
            c = slice(h * HG_DIM, (h + 1) * HG_DIM)
            ohat, _ = _rms(o_ref[:, c])
            out_ref[:, c] = (ohat * og_ref[...] * _silu(g_ref[:, c])).astype(BF16)

    tile = _rows(HG_TILE, HG_WIDTH)
    return pl.pallas_call(
        body, name="hg_fwd", grid=(n_tiles,),
        out_shape=[jax.ShapeDtypeStruct((s, HG_WIDTH), BF16), jax.ShapeDtypeStruct((s, HG_WIDTH), F32),
                   jax.ShapeDtypeStruct((s // HG_CHUNK * HG_DIM, HG_WIDTH), F32)],
        in_specs=[tile] * 4 + [_whole((2, HG_WIDTH)), _whole((1, HG_DIM))],
        out_specs=[tile, tile, _rows(HG_TILE_CHUNKS * HG_DIM, HG_WIDTH)],
        scratch_shapes=[pltpu.VMEM((HG_HEADS, HG_DIM, HG_DIM), F32)] + [pltpu.VMEM((HG_TILE, HG_WIDTH), F32)] * 3,
        compiler_params=_params("arbitrary"),
    )(hq, hf, hi, hgt, logits, onorm_g)


def _hg_bwd(hq, hf, hi, hgt, logits, onorm_g, o, states, dout):
    s = hq.shape[0]
    n_tiles = s // HG_TILE

    def body(q_ref, f_ref, i_ref, g_ref, lg_ref, og_ref, o_ref, st_ref, d_ref,
             dq_ref, df_ref, di_ref, dg_ref, dog_ref, dlb_ref, dstate, qf_s, kk_s, lf_s, do_s):
        @pl.when(pl.program_id(0) == 0)
        def _():
            dstate[...] = jnp.zeros_like(dstate)
            dog_ref[...] = jnp.zeros_like(dog_ref)
            dlb_ref[...] = jnp.zeros_like(dlb_ref)

        og = og_ref[...]
        dog = jnp.zeros((1, HG_DIM), F32)
        for h in range(HG_HEADS):
            c = slice(h * HG_DIM, (h + 1) * HG_DIM)
            ohat, rstd = _rms(o_ref[:, c])
            gate = g_ref[:, c]
            d = d_ref[:, c]
            dg_ref[:, c] = (d * (ohat * og) * _dsilu(gate)).astype(BF16)
            dnormed = d * _silu(gate)
            dog += _rowsum(dnormed * ohat)
            do_s[:, c] = _rms_bwd(dnormed * og, ohat, rstd)
        dog_ref[...] += dog

        lb = _lower_bound(lg_ref)
        f = lb + (1.0 - lb) * _sigmoid(f_ref[...])
        kk_s[...] = 1.0 - f
        lf_s[...] = jnp.log(f)
        qf_s[...] = _silu(q_ref[...])
        causal, upper, tri, tri_t = _chunk_masks()

        def chunk(step, carry):
            ci = HG_TILE_CHUNKS - 1 - step
            rows = pl.ds(pl.multiple_of(ci * HG_CHUNK, HG_CHUNK), HG_CHUNK)
            srows = pl.ds(pl.multiple_of(ci * HG_DIM, HG_DIM), HG_DIM)
            lf = lf_s[rows, :]
            b = _dot_f32(tri, lf)
            bl = _rowsum(lf)
            ref = 0.5 * bl
            qf, kk, v, do = qf_s[rows, :], kk_s[rows, :], i_ref[rows, :], do_s[rows, :]
            eb, ebr, erb, ekd, ebl = jnp.exp(b), jnp.exp(b - ref), jnp.exp(ref - b), jnp.exp(bl - b), jnp.exp(bl)
            a_in, a_t, b_t, kd = qf * eb, qf * ebr, kk * erb, kk * ekd
            for h in range(HG_HEADS):
                c = slice(h * HG_DIM, (h + 1) * HG_DIM)
                st, dst = st_ref[srows, c], dstate[h]
                stb, dstb = st.astype(BF16), dst.astype(BF16)
                doh, vh = do[:, c], v[:, c]
                dob, vb = doh.astype(BF16), vh.astype(BF16)
                ain_h, at_h, bt_h, kd_h = a_in[:, c], a_t[:, c], b_t[:, c], kd[:, c]
                atb, btb = at_h.astype(BF16), bt_h.astype(BF16)
                d_ain = _dot(dob, stb)
                p_t = jnp.where(upper, _dot_nt(btb, atb), 0.0).astype(BF16)
                dp = jnp.where(causal, _dot_nt(dob, vb), 0.0).astype(BF16)
                dp_t = jnp.where(upper, _dot_nt(vb, dob), 0.0).astype(BF16)
                di_ref[rows, c] = (_dot(p_t, dob) + _dot_nt(kd_h.astype(BF16), dstb)).astype(BF16)
                d_at = _dot(dp, btb)
                d_bt = _dot(dp_t, atb)
                d_kd = _dot(vb, dstb)
                dqf = d_ain * eb[:, c] + d_at * ebr[:, c]
                dkk = d_bt * erb[:, c] + d_kd * ekd[:, c]
                db = d_ain * ain_h + d_at * atb.astype(F32) - d_bt * btb.astype(F32) - d_kd * kd_h
                dbl = _rowsum(d_kd * kd_h) + _rowsum(dst * st) * ebl[:, c]
                dstate[h] = _dot_tn(dob, ain_h.astype(BF16)) + dst * ebl[:, c]
                dlf = _dot_f32(tri_t, db) + dbl
                qv, fr = q_ref[rows, c], f_ref[rows, c]
                lbh = lb[:, c]
                sg = _sigmoid(fr)
                dfv = dlf / (lbh + (1.0 - lbh) * sg) - dkk
                df_ref[rows, c] = (dfv * (1.0 - lbh) * sg * (1.0 - sg)).astype(BF16)
                dlb_ref[:, c] += _rowsum(dfv * (1.0 - sg))
                dq_ref[rows, c] = (dqf * _dsilu(qv)).astype(BF16)
            return carry

        lax.fori_loop(0, HG_TILE_CHUNKS, chunk, 0, unroll=2)

    rev = pl.BlockSpec((HG_TILE, HG_WIDTH), lambda i: (n_tiles - 1 - i, 0))
    return pl.pallas_call(
        body, name="hg_bwd", grid=(n_tiles,),
        out_shape=[jax.ShapeDtypeStruct((s, HG_WIDTH), BF16)] * 4
        + [jax.ShapeDtypeStruct((1, HG_DIM), F32), jax.ShapeDtypeStruct((1, HG_WIDTH), F32)],
        in_specs=[rev] * 4 + [_whole((2, HG_WIDTH)), _whole((1, HG_DIM)), rev,
                              pl.BlockSpec((HG_TILE_CHUNKS * HG_DIM, HG_WIDTH), lambda i: (n_tiles - 1 - i, 0)), rev],
        out_specs=[rev] * 4 + [_whole((1, HG_DIM)), _whole((1, HG_WIDTH))],
        scratch_shapes=[pltpu.VMEM((HG_HEADS, HG_DIM, HG_DIM), F32)] + [pltpu.VMEM((HG_TILE, HG_WIDTH), F32)] * 4,
        compiler_params=_params("arbitrary"),
    )(hq, hf, hi, hgt, logits, onorm_g, o, states, dout)


TOKEN_GROUP = 16


def _att_geometry(dil):
    per_group = TOKEN_GROUP // dil
    return per_group, ATT_BLOCK // per_group, ATT_WIDTH if dil == 1 else 128


def _att_consts(dil):
    per_group, ub, _ = _att_geometry(dil)

    def pos(i):
        return i if dil == 1 else (i % ub) * per_group + i // ub

    lane = lax.broadcasted_iota(jnp.int32, (ATT_BLOCK, 128), 1)
    qi = pos(lax.broadcasted_iota(jnp.int32, (2 * ATT_BLOCK, ATT_BLOCK), 0) % ATT_BLOCK)
    kj = pos(lax.broadcasted_iota(jnp.int32, (2 * ATT_BLOCK, ATT_BLOCK), 1))
    return lane < ATT_HEAD_DIM, kj <= qi, lambda off: kj >= qi + off


def _load_tile(ref, dil, r, c):
    if dil == 1:
        return ref[:, c]
    per_group, ub, _ = _att_geometry(dil)
    return jnp.concatenate([ref[pl.ds(dil * w + r, ub, stride=TOKEN_GROUP), c] for w in range(per_group)], axis=0)


def _store_tile(ref, dil, r, c, val):
    if dil == 1:
        ref[:, c] = val
        return
    per_group, ub, _ = _att_geometry(dil)
    for w in range(per_group):
        ref[pl.ds(dil * w + r, ub, stride=TOKEN_GROUP), c] = val[w * ub:(w + 1) * ub]


def _att_specs(seq, dil):
    _, ub, lanes = _att_geometry(dil)
    rows = ub * TOKEN_GROUP
    nb = seq // rows
    cur = pl.BlockSpec((rows, lanes), lambda n, j: (n, j))
    prev = pl.BlockSpec((rows, lanes), lambda n, j: (jnp.maximum(n - 1, 0), j))
    nxt = pl.BlockSpec((rows, lanes), lambda n, j: (jnp.minimum(n + 1, nb - 1), j))
    return nb, lanes, cur, prev, nxt


def _stack_heads(x2, first):
    return jnp.concatenate([jnp.where(first, x2, 0.0), jnp.where(first, 0.0, x2)], axis=0)


def _stack_bcast(x2, first):
    other = pltpu.roll(x2, ATT_HEAD_DIM, axis=1)
    return jnp.concatenate([jnp.where(first, x2, other), jnp.where(first, other, x2)], axis=0)


def _unstack_heads(st, first):
    return jnp.where(first, st[:ATT_BLOCK], st[ATT_BLOCK:])


def _att_fwd(q, k, v, dil):
    seq, width = q.shape
    nb, lanes, cur, prev, _ = _att_specs(seq, dil)

    def body(q_ref, kc_ref, kp_ref, vc_ref, vp_ref, o_ref, lse_ref):
        first, cur_ok, _band = _att_consts(dil)
        prev_ok = _band(jnp.where(pl.program_id(0) > 0, 0, ATT_BLOCK))
        for r in range(dil):
            for j in range(lanes // 128):
                c = slice(j * 128, (j + 1) * 128)
                qst = _stack_heads(_load_tile(q_ref, dil, r, c) * ATT_SCALE, first).astype(BF16)
                kc, kp = _load_tile(kc_ref, dil, r, c).astype(BF16), _load_tile(kp_ref, dil, r, c).astype(BF16)
                vc, vp = _load_tile(vc_ref, dil, r, c).astype(BF16), _load_tile(vp_ref, dil, r, c).astype(BF16)
                sc = jnp.where(cur_ok, _dot_nt(qst, kc), NEG)
                sp = jnp.where(prev_ok, _dot_nt(qst, kp), NEG)
                mx = jnp.maximum(jnp.max(sc, axis=-1, keepdims=True), jnp.max(sp, axis=-1, keepdims=True))
                pc, pp = jnp.exp(sc - mx), jnp.exp(sp - mx)
                den = jnp.sum(pc, axis=-1, keepdims=True) + jnp.sum(pp, axis=-1, keepdims=True)
                ost = (_dot(pc.astype(BF16), vc) + _dot(pp.astype(BF16), vp)) / den
                lse = jnp.broadcast_to(mx + jnp.log(den), (2 * ATT_BLOCK, 128))
                _store_tile(o_ref, dil, r, c, _unstack_heads(ost, first))
                _store_tile(lse_ref, dil, r, c, _unstack_heads(lse, first))

    return pl.pallas_call(
        body, name=f"att_fwd_d{dil}", grid=(nb, width // lanes),
        out_shape=[jax.ShapeDtypeStruct((seq, width), F32)] * 2,
        in_specs=[cur, cur, prev, cur, prev], out_specs=[cur, cur],
        compiler_params=_params("arbitrary", "arbitrary"),
    )(q, k, k, v, v)


def _att_bwd(q, k, v, do, cc, lse, dil):
    seq, width = q.shape
    _, ub, lanes = _att_geometry(dil)
    rows = ub * TOKEN_GROUP
    nb = seq // rows

    def body(q_ref, qx_ref, k_ref, v_ref, do_ref, dox_ref, cc_ref, ccx_ref, lse_ref, lsex_ref,
             dq_ref, dk_ref, dv_ref, carry):
        first, cur_ok, _band = _att_consts(dil)
        n = pl.program_id(1)
        next_ok = _band(jnp.where(n < nb - 1, 0, ATT_BLOCK))

        @pl.when(n == 0)
        def _():
            carry[...] = jnp.zeros_like(carry)

        for r in range(dil):
            for j in range(lanes // 128):
                c = slice(j * 128, (j + 1) * 128)
                qst = _stack_heads(_load_tile(q_ref, dil, r, c) * ATT_SCALE, first).astype(BF16)
                qxst = _stack_heads(_load_tile(qx_ref, dil, r, c) * ATT_SCALE, first).astype(BF16)
                dost = _stack_heads(_load_tile(do_ref, dil, r, c), first).astype(BF16)
                doxst = _stack_heads(_load_tile(dox_ref, dil, r, c), first).astype(BF16)
                lse_n = _stack_bcast(_load_tile(lse_ref, dil, r, c), first)
                lse_x = _stack_bcast(_load_tile(lsex_ref, dil, r, c), first)
                cc_n = _stack_bcast(_load_tile(cc_ref, dil, r, c), first)
                cc_x = _stack_bcast(_load_tile(ccx_ref, dil, r, c), first)
                kb, vb = _load_tile(k_ref, dil, r, c).astype(BF16), _load_tile(v_ref, dil, r, c).astype(BF16)
                p_cur = jnp.exp(jnp.where(cur_ok, _dot_nt(qst, kb), NEG) - lse_n)
                p_next = jnp.exp(jnp.where(next_ok, _dot_nt(qxst, kb), NEG) - lse_x)
                ds_cur = (p_cur * (_dot_nt(dost, vb) + cc_n)).astype(BF16)
                ds_next = (p_next * (_dot_nt(doxst, vb) + cc_x)).astype(BF16)
                dq_own = _load_tile(carry, dil, r, c) + _unstack_heads(_dot(ds_cur, kb), first)
                _store_tile(dq_ref, dil, r, c, dq_own * ATT_SCALE)
                _store_tile(carry, dil, r, c, _unstack_heads(_dot(ds_next, kb), first))
                _store_tile(dk_ref, dil, r, c, _dot_tn(ds_cur, qst) + _dot_tn(ds_next, qxst))
                _store_tile(dv_ref, dil, r, c, _dot_tn(p_cur.astype(BF16), dost) + _dot_tn(p_next.astype(BF16), doxst))

    cur = pl.BlockSpec((rows, lanes), lambda j, n: (n, j))
    nxt = pl.BlockSpec((rows, lanes), lambda j, n: (jnp.minimum(n + 1, nb - 1), j))
    return pl.pallas_call(
        body, name=f"att_bwd_d{dil}", grid=(width // lanes, nb),
        out_shape=[jax.ShapeDtypeStruct((seq, width), F32)] * 3,
        in_specs=[cur, nxt, cur, cur, cur, nxt, cur, nxt, cur, nxt], out_specs=[cur, cur, cur],
        scratch_shapes=[pltpu.VMEM((rows, lanes), F32)],
        compiler_params=_params("arbitrary", "arbitrary"),
    )(q, q, k, v, do, do, cc, cc, lse, lse)


def _branch_weights(lses):
    mx = jnp.maximum(jnp.maximum(lses[0], lses[1]), lses[2])
    es = [jnp.exp(l - mx) for l in lses]
    inv = 1.0 / (es[0] + es[1] + es[2])
    return [e * inv for e in es]


def _att_combine(outs, lses, att_g):
    s = outs[0].shape[0]
    tm = 512

    def body(o0, o1, o2, l0, l1, l2, g_ref, att_ref, out_ref):
        ws = _branch_weights([l0[...], l1[...], l2[...]])
        att = ws[0] * o0[...] + ws[1] * o1[...] + ws[2] * o2[...]
        att_ref[...] = att
        ahat, _ = _rms(att)
        out_ref[...] = (ahat * g_ref[...]).astype(BF16)

    tile = _rows(tm, ATT_WIDTH)
    return pl.pallas_call(
        body, name="att_combine", grid=(s // tm,),
        out_shape=[jax.ShapeDtypeStruct((s, ATT_WIDTH), F32), jax.ShapeDtypeStruct((s, ATT_WIDTH), BF16)],
        in_specs=[tile] * 6 + [_whole((1, ATT_WIDTH))], out_specs=[tile, tile],
        compiler_params=_params("parallel"),
    )(*outs, *lses, att_g)


def _att_combine_bwd(datt_out, att, lses, att_g):
    s = att.shape[0]
    tm = 256

    def body(d_ref, att_ref, l0, l1, l2, g_ref, do0, do1, do2, cc0, cc1, cc2, dg_ref):
        @pl.when(pl.program_id(0) == 0)
        def _():
            dg_ref[...] = jnp.zeros_like(dg_ref)

        att = att_ref[...]
        ahat, rstd = _rms(att)
        d = d_ref[...]
        dg_ref[...] += _rowsum(d * ahat)
        datt = _rms_bwd(d * g_ref[...], ahat, rstd)
        hi = lax.broadcasted_iota(jnp.int32, (ATT_WIDTH, ATT_WIDTH), 0) // ATT_HEAD_DIM
        hj = lax.broadcasted_iota(jnp.int32, (ATT_WIDTH, ATT_WIDTH), 1) // ATT_HEAD_DIM
        head_sum = _dot_f32(datt * att, (hi == hj).astype(F32))
        ws = _branch_weights([l0[...], l1[...], l2[...]])
        for w, do_ref, cc_ref in zip(ws, (do0, do1, do2), (cc0, cc1, cc2)):
            do_ref[...] = w * datt
            cc_ref[...] = -w * head_sum

    tile = _rows(tm, ATT_WIDTH)
    return pl.pallas_call(
        body, name="att_combine_bwd", grid=(s // tm,),
        out_shape=[jax.ShapeDtypeStruct((s, ATT_WIDTH), F32)] * 6 + [jax.ShapeDtypeStruct((1, ATT_WIDTH), F32)],
        in_specs=[tile] * 5 + [_whole((1, ATT_WIDTH))], out_specs=[tile] * 6 + [_whole((1, ATT_WIDTH))],
        compiler_params=_params("arbitrary"),
    )(datt_out, att, *lses, att_g)


def _out_fwd(x, hg, at, mod, w_out):
    s = x.shape[0]
    tm = 512

    def body(x_ref, hg_ref, at_ref, mod_ref, w_ref, x1_ref):
        mix = _dot(hg_ref[...], w_ref[0:512, :]) + _dot(at_ref[...], w_ref[512:1024, :])
        x1_ref[...] = x_ref[...] + mod_ref[:, 2 * D_MODEL:3 * D_MODEL] * mix

    return pl.pallas_call(
        body, name="out_fwd", grid=(s // tm,), out_shape=jax.ShapeDtypeStruct((s, D_MODEL), F32),
        in_specs=[_rows(tm, D_MODEL), _rows(tm, 512), _rows(tm, 512), _whole((1, 6 * D_MODEL)), _whole((D_MODEL, D_MODEL))],
        out_specs=_rows(tm, D_MODEL), compiler_params=_params("parallel"),
    )(x, hg, at, mod, w_out)


def _out_bwd(dx1, hg, at, mod, w_out):
    s = dx1.shape[0]
    tm = 512

    def body(dx_ref, hg_ref, at_ref, mod_ref, w_ref, dhg_ref, dat_ref, dw_ref, dgate_ref):
        @pl.when(pl.program_id(0) == 0)
        def _():
            dw_ref[...] = jnp.zeros_like(dw_ref)
            dgate_ref[...] = jnp.zeros_like(dgate_ref)

        hg, at, dx = hg_ref[...], at_ref[...], dx_ref[...]
        mix = _dot(hg, w_ref[0:512, :]) + _dot(at, w_ref[512:1024, :])
        dgate_ref[...] += _rowsum(dx * mix)
        dmix = (mod_ref[:, 2 * D_MODEL:3 * D_MODEL] * dx).astype(BF16)
        dhg_ref[...] = _dot_nt(dmix, w_ref[0:512, :])
        dat_ref[...] = _dot_nt(dmix, w_ref[512:1024, :])
        dw_ref[0:512, :] += _dot_tn(hg, dmix)
        dw_ref[512:1024, :] += _dot_tn(at, dmix)

    return pl.pallas_call(
        body, name="out_bwd", grid=(s // tm,),
        out_shape=[jax.ShapeDtypeStruct((s, 512), F32)] * 2
        + [jax.ShapeDtypeStruct((D_MODEL, D_MODEL), F32), jax.ShapeDtypeStruct((1, D_MODEL), F32)],
        in_specs=[_rows(tm, D_MODEL), _rows(tm, 512), _rows(tm, 512), _whole((1, 6 * D_MODEL)), _whole((D_MODEL, D_MODEL))],
        out_specs=[_rows(tm, 512), _rows(tm, 512), _whole((D_MODEL, D_MODEL)), _whole((1, D_MODEL))],
        compiler_params=_params("arbitrary"),
    )(dx1, hg, at, mod, w_out)


FFN_CHUNK = 2816


def _ffn(x1, target, mod, g2, gf, w_gu, w_down):
    s = x1.shape[0]
    tm = 256
    n_chunks = D_FF // FFN_CHUNK

    def body(x_ref, t_ref, mod_ref, g2_ref, gf_ref, wgu_hbm, wd_hbm,
             dx_ref, h2_ref, act_ref, dau_ref, dff_ref, sums_ref, loss_ref, wgu, wd, a_s, u_s, sem):
        @pl.when(pl.program_id(0) == 0)
        def _():
            c1 = pltpu.make_async_copy(wgu_hbm, wgu, sem.at[0])
            c2 = pltpu.make_async_copy(wd_hbm, wd, sem.at[1])
            c1.start()
            c2.start()
            c1.wait()
            c2.wait()
            sums_ref[...] = jnp.zeros_like(sums_ref)
            loss_ref[...] = jnp.zeros_like(loss_ref)

        x1v = x_ref[...]
        xhat, rstd = _rms(x1v)
        g2 = g2_ref[...]
        n2 = xhat * g2
        scale2 = 1.0 + mod_ref[:, 4 * D_MODEL:5 * D_MODEL]
        gate2 = mod_ref[:, 5 * D_MODEL:6 * D_MODEL]
        hb = (n2 * scale2 + mod_ref[:, 3 * D_MODEL:4 * D_MODEL]).astype(BF16)
        h2_ref[...] = hb
        ff = jnp.zeros((tm, D_MODEL), F32)
        for j in range(n_chunks):
            c = slice(j * FFN_CHUNK, (j + 1) * FFN_CHUNK)
            cu = slice(D_FF + j * FFN_CHUNK, D_FF + (j + 1) * FFN_CHUNK)
            a = _dot(hb, wgu[:, c])
            u = _dot(hb, wgu[:, cu])
            a_s[:, c] = a
            u_s[:, c] = u
            act = (_silu(a) * u).astype(BF16)
            act_ref[:, c] = act
            ff += _dot(act, wd[c, :])
        x2 = x1v + gate2 * ff
        nf, rstd_f = _rms(x2)
        gfv = gf_ref[...]
        err = nf * gfv - t_ref[...]
        loss_ref[...] += 0.5 * jnp.sum(_rowsum(err * err), axis=-1, keepdims=True) * (1.0 / D_MODEL)
        dy = err * (1.0 / D_MODEL)
        dx2 = _rms_bwd(dy * gfv, nf, rstd_f)
        dffb = (gate2 * dx2).astype(BF16)
        dff_ref[...] = dffb
        dh = jnp.zeros((tm, D_MODEL), F32)
        for j in range(n_chunks):
            c = slice(j * FFN_CHUNK, (j + 1) * FFN_CHUNK)
            cu = slice(D_FF + j * FFN_CHUNK, D_FF + (j + 1) * FFN_CHUNK)
            dact = _dot_nt(dffb, wd[c, :])
            a, u = a_s[:, c], u_s[:, c]
            da = (dact * u * _dsilu(a)).astype(BF16)
            du = (dact * _silu(a)).astype(BF16)
            dau_ref[:, c] = da
            dau_ref[:, cu] = du
            dh += _dot_nt(da, wgu[:, c]) + _dot_nt(du, wgu[:, cu])
        dn = dh * scale2
        sums_ref[0:1, :] += _rowsum(dh)
        sums_ref[1:2, :] += _rowsum(dh * n2)
        sums_ref[2:3, :] += _rowsum(dx2 * ff)
        sums_ref[3:4, :] += _rowsum(dn * xhat)
        sums_ref[4:5, :] += _rowsum(dy * nf)
        dx_ref[...] = dx2 + _rms_bwd(dn * g2, xhat, rstd)

    vec = _whole((1, D_MODEL))
    hbm = pl.BlockSpec(memory_space=pl.ANY)
    return pl.pallas_call(
        body, name="ffn", grid=(s // tm,),
        out_shape=[jax.ShapeDtypeStruct((s, D_MODEL), F32), jax.ShapeDtypeStruct((s, D_MODEL), BF16),
                   jax.ShapeDtypeStruct((s, D_FF), BF16), jax.ShapeDtypeStruct((s, 2 * D_FF), BF16),
                   jax.ShapeDtypeStruct((s, D_MODEL), BF16), jax.ShapeDtypeStruct((8, D_MODEL), F32),
                   jax.ShapeDtypeStruct((1, 128), F32)],
        in_specs=[_rows(tm, D_MODEL), _rows(tm, D_MODEL), _whole((1, 6 * D_MODEL)), vec, vec, hbm, hbm],
        out_specs=[_rows(tm, D_MODEL), _rows(tm, D_MODEL), _rows(tm, D_FF), _rows(tm, 2 * D_FF), _rows(tm, D_MODEL),
                   _whole((8, D_MODEL)), _whole((1, 128))],
        scratch_shapes=[pltpu.VMEM((D_MODEL, 2 * D_FF), BF16), pltpu.VMEM((D_FF, D_MODEL), BF16),
                        pltpu.VMEM((tm, D_FF), F32), pltpu.VMEM((tm, D_FF), F32), pltpu.SemaphoreType.DMA((2,))],
        compiler_params=_params("arbitrary"),
    )(x1, target, mod, g2, gf, w_gu, w_down)


def _weight_grad(a, b, name):
    s, m = a.shape
    n = b.shape[1]
    ts, tn = min(s, 2048), 512

    def body(a_ref, b_ref, o_ref):
        @pl.when(pl.program_id(1) == 0)
        def _():
            o_ref[...] = jnp.zeros_like(o_ref)

        o_ref[...] += _dot_tn(a_ref[...], b_ref[...])

    return pl.pallas_call(
        body, name=name, grid=(n // tn, s // ts), out_shape=jax.ShapeDtypeStruct((m, n), F32),
        in_specs=[pl.BlockSpec((ts, m), lambda j, i: (i, 0)), pl.BlockSpec((ts, tn), lambda j, i: (i, j))],
        out_specs=pl.BlockSpec((m, tn), lambda j, i: (0, j)),
        compiler_params=_params("parallel", "arbitrary"),
    )(a, b)


def _weight_grad_by_owner(a, b, name):
    s, m = a.shape
    n = b.shape[1]
    cb = n // N_DEV
    ts = min(s, 2048)
    n_steps = s // ts

    def body(a_ref, b_ref, o_ref, acc, low):
        @pl.when(pl.program_id(1) == 0)
        def _():
            acc[...] = jnp.zeros_like(acc)

        acc[...] += _dot_tn(a_ref[...], b_ref[...])

        @pl.when(pl.program_id(1) == n_steps - 1)
        def _():
            low[...] = pltpu.roll(acc[...], cb, axis=1)
            o_ref[0] = acc[:, 0:cb]
            o_ref[1] = low[:, 0:cb]

    return pl.pallas_call(
        body, name=name, grid=(4, n_steps), out_shape=jax.ShapeDtypeStruct((4, 2, m, cb), F32),
        in_specs=[pl.BlockSpec((ts, m), lambda j, i: (i, 0)), pl.BlockSpec((ts, 2 * cb), lambda j, i: (i, j))],
        out_specs=pl.BlockSpec((None, 2, m, cb), lambda j, i: (j, 0, 0, 0)),
        scratch_shapes=[pltpu.VMEM((m, 2 * cb), F32), pltpu.VMEM((m, 2 * cb), F32)],
        compiler_params=_params("parallel", "arbitrary"),
    )(a, b)


def _adamw_math(w, g, m, v):
    m = ADAM_B1 * m + (1.0 - ADAM_B1) * g
    v = ADAM_B2 * v + (1.0 - ADAM_B2) * (g * g)
    m_hat = m / (1.0 - ADAM_B1 ** ADAM_STEP)
    v_hat = v / (1.0 - ADAM_B2 ** ADAM_STEP)
    delta = -ADAM_LR * (m_hat / (jnp.sqrt(v_hat) + ADAM_EPS) + ADAM_WD * w)
    return delta, m, v


def _adamw_shard(chip, w, m, v, partial, got, name):
    r, c = w.shape
    tr = _shard_rows(r)

    def body(chip_ref, w_ref, m_ref, v_ref, own_ref, g0, g1, g2, grad_ref, d_ref, nm_ref, nv_ref):
        g = ((own_ref[...] + g0[...].astype(F32)) + g1[...].astype(F32)) + g2[...].astype(F32)
        grad_ref[...] = g
        d_ref[...], nm_ref[...], nv_ref[...] = _adamw_math(w_ref[...], g, m_ref[...], v_ref[...])

    tile = pl.BlockSpec((tr, c), lambda i, chip_ref: (i, 0))
    own = pl.BlockSpec((None, tr, c), lambda i, chip_ref: (chip_ref[0], i, 0))
    part = [pl.BlockSpec((None, tr, c), functools.partial(lambda j, i, chip_ref: (j, i, 0), j)) for j in range(3)]
    return pl.pallas_call(
        body, name=name,
        grid_spec=pltpu.PrefetchScalarGridSpec(num_scalar_prefetch=1, grid=(r // tr,), in_specs=[tile] * 3 + [own] + part,
                                               out_specs=[tile] * 4),
        out_shape=[jax.ShapeDtypeStruct((r, c), F32)] * 4, compiler_params=_params("parallel"),
    )(chip, w, m, v, partial, got, got, got)


def _small_update(small_all, dmod_blocks, c_all, logits, w_ada, m_ada, v_ada, smalls):
    def body(sm_ref, dm_ref, c_ref, lg_ref, wa_ref, ma_ref, va_ref, *rest):
        ins, outs = rest[:21], rest[21:]
        _, me = _flip(0)
        tot = sm_ref[0:1, :]
        for i in range(1, N_DEV):
            tot = tot + sm_ref[i:i + 1, :]
        loss_ref = outs[0]
        loss_ref[...] = tot[:, SM_LOSS:SM_LOSS + 128]
        g_ada = lax.dot_general(_silu(c_ref[...]), dm_ref[me], (((0,), (0,)), ((), ())),
                                preferred_element_type=F32, precision=HIGHEST)
        outs[1][...] = g_ada
        outs[2][...], outs[3][...], outs[4][...] = _adamw_math(wa_ref[...], g_ada, ma_ref[...], va_ref[...])
        p0 = _lower_bound(lg_ref)
        dl0 = tot[:, SM_LB:SM_LB + 512] * p0 * (1.0 - p0)
        grads = [tot[:, SM_MOD:SM_MOD + 6 * D_MODEL], tot[:, SM_G1:SM_G1 + D_MODEL], tot[:, SM_G2:SM_G2 + D_MODEL],
                 tot[:, SM_GF:SM_GF + D_MODEL], tot[:, SM_ATT:SM_ATT + 512], tot[:, SM_HG:SM_HG + 128],
                 jnp.where(lax.broadcasted_iota(jnp.int32, (2, 512), 0) == 0, dl0, -dl0)]
        for i, g in enumerate(grads):
            w_ref, m_ref, v_ref = ins[3 * i:3 * i + 3]
            o = outs[5 + 4 * i:9 + 4 * i]
            o[0][...] = g
            o[1][...], o[2][...], o[3][...] = _adamw_math(w_ref[...], g, m_ref[...], v_ref[...])

    flat = [t for trio in smalls for t in trio]
    vm = pl.BlockSpec(memory_space=pltpu.VMEM)
    out_shape = [jax.ShapeDtypeStruct((1, 128), F32)] + [jax.ShapeDtypeStruct(w_ada.shape, F32)] * 4
    for trio in smalls:
        out_shape += [jax.ShapeDtypeStruct(trio[0].shape, F32)] * 4
    return pl.pallas_call(
        body, name="small_update", out_shape=out_shape,
        in_specs=[vm] * (7 + len(flat)), out_specs=[vm] * len(out_shape),
        compiler_params=pltpu.CompilerParams(vmem_limit_bytes=V7X_VMEM_LIMIT),
    )(small_all, dmod_blocks, c_all, logits, w_ada, m_ada, v_ada, *flat)


def kernel(x, c, w_ada, b_ada, norm1_g, w_in, hg_lb_logits, hg_onorm_g, att_onorm_g, w_out, norm2_g, w_gate_up, w_down, final_g, loss_target, m_w_ada, m_b_ada, m_norm1_g, m_w_in, m_hg_lb_logits, m_hg_onorm_g, m_att_onorm_g, m_w_out, m_norm2_g, m_w_gate_up, m_w_down, m_final_g, v_w_ada, v_b_ada, v_norm1_g, v_w_in, v_hg_lb_logits, v_hg_onorm_g, v_att_onorm_g, v_w_out, v_norm2_g, v_w_gate_up, v_w_down, v_final_g):
    x2d, target = x[0], loss_target[0]
    seq = x2d.shape[0]
    assert seq % (ATT_BLOCK * max(DILATIONS)) == 0 and seq % HG_TILE == 0
    gf = final_g.reshape(1, D_MODEL)

    c_all = _exchange_small(c.reshape(8, D_MODEL // 8), None, "gather_c").reshape(N_DEV, D_MODEL)
    ada = _ada_rows(c_all, w_ada[0], b_ada)
    mod = _exchange_small(ada, 1, "scatter_mod").reshape(1, 6 * D_MODEL)

    core = lax.axis_index("c").astype(jnp.int32).reshape(1)
    chip = (2 * lax.axis_index("x") + lax.axis_index("y")).astype(jnp.int32).reshape(1)
    me = 4 * lax.axis_index("x") + 2 * lax.axis_index("y") + lax.axis_index("c")

    g_in, = _gather_weights([w_in[0].astype(BF16)])
    w_in_b = jnp.transpose(g_in, (1, 0, 2)).reshape(D_MODEL, IN_WIDTH)
    rest_shards = [w_out[0].astype(BF16), w_gate_up[0].astype(BF16), w_down[0].astype(BF16)]
    lands = [lax.empty((N_DEV,) + s.shape, BF16) for s in rest_shards]
    g_send, g_recv, g_srcs, g_lands, tok = _copies_start("gather_rest_start", _plan_gather_own, 12, rest_shards, lands, [w_in_b, mod])
    flight = {}

    def stage(name, *vals):
        if name == "attention_begun":
            flight["shards"], got = _copies_wait("gather_rest_wait", _plan_gather_own, g_send, g_recv, g_srcs, g_lands, [vals[0]])
            flight["pass"] = _copies_start("gather_pass_start", _plan_gather_pass, 9, [], got, [])
            return flight["pass"][4][0:1, 0:1]
        if name == "mixer_weights_done":
            dw_gu, dw_down, dw_out = vals
            flight["grads"] = [dw_out.reshape(4, 2, D_MODEL // N_DEV, D_MODEL),
                               dw_gu,
                               dw_down.reshape(4, 2, D_FF // N_DEV, D_MODEL)]
            pair_lands = [lax.empty((4,) + g.shape[2:], F32) for g in flight["grads"]]
            flight["pairs"] = _copies_start("reduce_pairs_start", _plan_reduce_pairs, 12, flight["grads"], pair_lands, [])
            return flight["pairs"][4][0:1, 0:1]
        if name == "attention_backward_begun":
            s, r, srcs, pl_lands, _ = flight["pairs"]
            grads, got = _copies_wait("reduce_pairs_wait", _plan_reduce_pairs, s, r, srcs, pl_lands, [vals[0]])
            flight["sums"] = [_pair_sum(core, g, b, f"pair_sum_{i}") for i, (g, b) in enumerate(zip(grads, got))]
            chip_lands = [lax.empty((3,) + s16.shape[1:], BF16) for _, s16 in flight["sums"]]
            flight["chips"] = _copies_start("reduce_chips_start", _plan_reduce_chips, 9, [s16 for _, s16 in flight["sums"]], chip_lands, [])
            return flight["chips"][4][0:1, 0:1]
        raise ValueError(name)

    def rest_weights(after):
        s, r, _, p_lands, _ = flight["pass"]
        _, got = _copies_wait("gather_pass_wait", _plan_gather_pass, s, r, [], p_lands, [after])
        full = [lax.dynamic_update_index_in_dim(g, shard, me, 0) for g, shard in zip(got, flight["shards"])]
        return (full[0].reshape(D_MODEL, D_MODEL), jnp.transpose(full[1], (1, 0, 2)).reshape(D_MODEL, 2 * D_FF),
                full[2].reshape(D_FF, D_MODEL))

    grad_x, dw_in, small = _block_step(x2d, target, mod + tok[0:1, 0:1], norm1_g, hg_lb_logits, hg_onorm_g, att_onorm_g, norm2_g, gf,
                                       w_in_b, rest_weights, stage)

    g_in8 = dw_in
    in_pairs = _copies_start("reduce_pairs_in_start", _plan_reduce_pairs, 4, [g_in8], [lax.empty((4,) + g_in8.shape[2:], F32)], [])
    s, r, srcs, c_lands, _ = flight["chips"]
    _, recv_rest = _copies_wait("reduce_chips_wait", _plan_reduce_chips, s, r, srcs, c_lands, [in_pairs[4]])
    small_rows = jnp.pad(small + in_pairs[4][0:1, 0:1], ((0, 0), (0, SM_PADDED - SM_WIDTH))).reshape(SM_PADDED // 128, 128)
    small_all = _exchange_small(small_rows, None, "gather_small").reshape(N_DEV, SM_PADDED)[:, :SM_WIDTH]
    big = {}
    rest_params = [("w_out", w_out, m_w_out, v_w_out), ("w_gate_up", w_gate_up, m_w_gate_up, v_w_gate_up), ("w_down", w_down, m_w_down, v_w_down)]
    for (n, w, m, v), (s32, _), got in zip(rest_params, flight["sums"], recv_rest):
        big[n] = [t[None] for t in _adamw_shard(chip, w[0], m[0], v[0], s32, got, f"adamw_{n}")]
    in_grads, got_in = _copies_wait("reduce_pairs_in_wait", _plan_reduce_pairs, in_pairs[0], in_pairs[1], in_pairs[2], in_pairs[3],
                                    [big[n][3] for n, _, _, _ in rest_params] + [small_all])
    in_s32, in_s16 = _pair_sum(core, in_grads[0], got_in[0], "pair_sum_in")
    in_chips = _copies_start("reduce_chips_in_start", _plan_reduce_chips, 3, [in_s16], [lax.empty((3,) + in_s16.shape[1:], BF16)], [])
    c_all = c_all + in_chips[4][0:1, 0:1]
    smalls = [(b_ada, m_b_ada, v_b_ada), (norm1_g, m_norm1_g, v_norm1_g), (norm2_g, m_norm2_g, v_norm2_g),
              (gf, m_final_g.reshape(1, D_MODEL), v_final_g.reshape(1, D_MODEL)),
              (att_onorm_g, m_att_onorm_g, v_att_onorm_g), (hg_onorm_g, m_hg_onorm_g, v_hg_onorm_g),
              (hg_lb_logits, m_hg_lb_logits, v_hg_lb_logits)]
    dmod_blocks = small_all[:, :6 * D_MODEL].reshape(N_DEV, N_DEV, 6 * D_MODEL // N_DEV).transpose(1, 0, 2)
    res = _small_update(small_all, dmod_blocks, c_all, hg_lb_logits, w_ada[0], m_w_ada[0], v_w_ada[0], smalls)
    _, recv_in = _copies_wait("reduce_chips_in_wait", _plan_reduce_chips, in_chips[0], in_chips[1], in_chips[2], in_chips[3], [res[0]])
    big["w_in"] = [t[None] for t in _adamw_shard(chip, w_in[0], m_w_in[0], v_w_in[0], in_s32, recv_in[0], "adamw_w_in")]
    loss = res[0][0, 0]
    ada4 = [t[None] for t in res[1:5]]
    sm4 = {n: list(res[5 + 4 * i:9 + 4 * i]) for i, n in enumerate(["b_ada", "norm1_g", "norm2_g", "final_g", "att", "hg", "lb"])}
    sm4["final_g"] = [t.reshape(D_MODEL) for t in sm4["final_g"]]

    order = [ada4, sm4["b_ada"], sm4["norm1_g"], big["w_in"], sm4["lb"], sm4["hg"], sm4["att"], big["w_out"], sm4["norm2_g"],
             big["w_gate_up"], big["w_down"], sm4["final_g"]]
    return (loss, grad_x[None], *[o[0] for o in order], *[o[1] for o in order], *[o[2] for o in order], *[o[3] for o in order])


def _block_step(x2d, target, mod, norm1_g, hg_lb_logits, hg_onorm_g, att_onorm_g, norm2_g, gf, w_in_b, rest_weights, stage):
    h1, hq, hf, hi, hgt, aq, ak, av = _in_fwd(x2d, mod, norm1_g, w_in_b)
    hg_out, hg_o, hg_states = _hg_fwd(hq, hf, hi, hgt, hg_lb_logits, hg_onorm_g)
    branch = [_att_fwd(aq, ak, av, DILATIONS[0])]
    att_g = att_onorm_g + stage("attention_begun", branch[0][0])
    branch += [_att_fwd(aq, ak, av, d) for d in DILATIONS[1:]]
    outs = [b[0] for b in branch]
    lses = [b[1] for b in branch]
    att, att_out = _att_combine(outs, lses, att_g)
    w_out_b, w_gu_b, w_down_b = rest_weights(att_out)
    x1 = _out_fwd(x2d, hg_out, att_out, mod, w_out_b)

    dx1, h2, act, dau, dff, ffn_sums, loss_part = _ffn(x1, target, mod, norm2_g, gf, w_gu_b, w_down_b)
    dw_gu = _weight_grad_by_owner(h2, dau, "dw_gate_up")
    dw_down = _weight_grad(act, dff, "dw_down")

    dhg, dat, dw_out, dgate1 = _out_bwd(dx1, hg_out, att_out, mod, w_out_b)
    att_g = att_onorm_g + stage("mixer_weights_done", dw_gu, dw_down, dw_out)
    comb = _att_combine_bwd(dat, att, lses, att_g)
    dos, ccs, d_att_g = comb[0:3], comb[3:6], comb[6]
    hg_g = hg_onorm_g + stage("attention_backward_begun", comb[3])
    datt = []
    for i, d in enumerate(DILATIONS):
        datt.append(_att_bwd(aq, ak, av, dos[i], ccs[i], lses[i], d))
    dhq, dhf, dhi, dhgt, d_hg_g, d_lb = _hg_bwd(hq, hf, hi, hgt, hg_lb_logits, hg_g, hg_o, hg_states, dhg)
    dps = [dhq, dhf, dhi, dhgt] + [datt[i][j] for j in range(3) for i in range(3)]
    grad_x, dp_b, dshift1, dscale1, d_g1 = _in_bwd(x2d, dx1, mod, norm1_g, w_in_b, dps)
    dw_in = _weight_grad_by_owner(h1, dp_b, "dw_in")
    small = jnp.concatenate([dshift1, dscale1, dgate1, ffn_sums[0:1], ffn_sums[1:2], ffn_sums[2:3], d_g1, ffn_sums[3:4],
                             ffn_sums[4:5], d_att_g, d_lb, d_hg_g, loss_part], axis=1)
    return grad_x, dw_in, small
```

```python
import functools

import jax
import jax.numpy as jnp
from jax import lax
from jax.experimental import pallas as pl
from jax.experimental.pallas import tpu as pltpu

F32 = jnp.float32
BF16 = jnp.bfloat16
HIGHEST = lax.Precision.HIGHEST
MESH = pl.DeviceIdType.MESH

D_MODEL = 1024
N_DEV = 8
HG_HEADS = 4
HG_DIM = 128
HG_WIDTH = HG_HEADS * HG_DIM
HG_CHUNK = 128
ATT_WIDTH = 512
ATT_HEAD_DIM = 64
ATT_BLOCK = 128
DILATIONS = (1, 4, 16)
ATT_SCALE = ATT_HEAD_DIM ** -0.5
D_FF = 2816
IN_WIDTH = 7 * 512
RMS_EPS = 1e-6
NEG = -1e30

ADAM_LR = 0.001
ADAM_B1 = 0.9
ADAM_B2 = 0.999
ADAM_EPS = 1e-08
ADAM_WD = 0.01
ADAM_STEP = 10

V7X_VMEM_LIMIT = 56 * 1024 * 1024

SM_MOD = 0
SM_G1 = 6 * D_MODEL
SM_G2 = 7 * D_MODEL
SM_GF = 8 * D_MODEL
SM_ATT = 9 * D_MODEL
SM_LB = 9 * D_MODEL + 512
SM_HG = 10 * D_MODEL
SM_LOSS = 10 * D_MODEL + 128
SM_WIDTH = 10 * D_MODEL + 256
SM_PADDED = 88 * 128


def _params(*sem, vmem=V7X_VMEM_LIMIT):
    return pltpu.CompilerParams(dimension_semantics=sem, vmem_limit_bytes=vmem)


def _dot(a, b):
    return jnp.dot(a, b, preferred_element_type=F32)


def _dot_nt(a, b):
    return lax.dot_general(a, b, (((1,), (1,)), ((), ())), preferred_element_type=F32)


def _dot_tn(a, b):
    return lax.dot_general(a, b, (((0,), (0,)), ((), ())), preferred_element_type=F32)


def _dot_f32(a, b):
    return jnp.dot(a, b, preferred_element_type=F32, precision=HIGHEST)


def _sigmoid(x):
    return 1.0 / (1.0 + jnp.exp(-x))


def _silu(x):
    return x * _sigmoid(x)


def _dsilu(x):
    s = _sigmoid(x)
    return s * (1.0 + x * (1.0 - s))


def _rms(x):
    rstd = lax.rsqrt(jnp.mean(x * x, axis=-1, keepdims=True) + RMS_EPS)
    return x * rstd, rstd


def _rms_bwd(dn, xhat, rstd):
    return rstd * (dn - xhat * jnp.mean(dn * xhat, axis=-1, keepdims=True))


def _rowsum(x):
    return jnp.sum(x, axis=0, keepdims=True)


def _rows(tm, n):
    return pl.BlockSpec((tm, n), lambda i: (i, 0))


def _whole(shape):
    return pl.BlockSpec(shape, lambda i: (0,) * len(shape))


def _mesh_pos():
    return lax.axis_index("x"), lax.axis_index("y"), lax.axis_index("c")


def _flip(k):
    x, y, c = _mesh_pos()
    px = 1 - x if k & 4 else x
    py = 1 - y if k & 2 else y
    pc = 1 - c if k & 1 else c
    return (px, py, pc), 4 * px + 2 * py + pc


def _exchange_small(x, rows_per_peer, name):
    r_all, cols = x.shape
    r_out = r_all if rows_per_peer is None else rows_per_peer

    def body(x_ref, out_ref, send_sems, recv_sems):
        _, me = _flip(0)

        def src(pid):
            if rows_per_peer is None:
                return x_ref
            return x_ref.at[pl.ds(pl.multiple_of(pid * r_out, r_out), r_out), :]

        if rows_per_peer is None:
            out_ref[me] = x_ref[...]
        else:
            out_ref[me] = x_ref[pl.ds(pl.multiple_of(me * r_out, r_out), r_out), :]
        sends = []
        for k in range(1, N_DEV):
            dev, pid = _flip(k)
            cp = pltpu.make_async_remote_copy(src_ref=src(pid), dst_ref=out_ref.at[me], send_sem=send_sems.at[k - 1],
                                              recv_sem=recv_sems.at[k - 1], device_id=dev, device_id_type=MESH)
            cp.start()
            sends.append(cp)
        for k in range(1, N_DEV):
            dev, pid = _flip(k)
            pltpu.make_async_remote_copy(src_ref=src(pid), dst_ref=out_ref.at[pid], send_sem=send_sems.at[k - 1],
                                         recv_sem=recv_sems.at[k - 1], device_id=dev, device_id_type=MESH).wait_recv()
        for cp in sends:
            cp.wait_send()

    return pl.pallas_call(
        body, name=name,
        out_shape=jax.ShapeDtypeStruct((N_DEV, r_out, cols), x.dtype),
        in_specs=[pl.BlockSpec(memory_space=pltpu.VMEM)],
        out_specs=pl.BlockSpec(memory_space=pltpu.VMEM),
        scratch_shapes=[pltpu.SemaphoreType.DMA((N_DEV - 1,)), pltpu.SemaphoreType.DMA((N_DEV - 1,))],
    )(x)


def _gather_weights(shards):
    n = len(shards)

    def body(*refs):
        xs, outs = refs[:n], refs[n:2 * n]
        send_sems, recv_sems, local_sems = refs[2 * n:]
        x, y, c = _mesh_pos()
        me, sibling = (x, y, c), (x, y, 1 - c)
        chips = [(1 - x, y), (x, 1 - y), (1 - x, 1 - y)]

        def blk(a, px, py, pc):
            return outs[a].at[4 * px + 2 * py + pc]

        def copy(a, k, block, to, src=None):
            return pltpu.make_async_remote_copy(
                src_ref=blk(a, *block) if src is None else src, dst_ref=blk(a, *block),
                send_sem=send_sems.at[a * 7 + k], recv_sem=recv_sems.at[a * 7 + k], device_id=to, device_id_type=MESH)

        mine = [pltpu.make_async_copy(xs[a], blk(a, *me), local_sems.at[a]) for a in range(n)]
        for cp in mine:
            cp.start()
        first = []
        for a in range(n):
            first.append(copy(a, 0, me, sibling, src=xs[a]))
            first += [copy(a, 1 + j, me, (*chip, c), src=xs[a]) for j, chip in enumerate(chips)]
        for cp in first:
            cp.start()
        passed = []
        for j, chip in enumerate(chips):
            for a in range(n):
                copy(a, 1 + j, (*chip, c), me).wait_recv()
                cp = copy(a, 4 + j, (*chip, c), sibling)
                cp.start()
                passed.append(cp)
        for a in range(n):
            copy(a, 0, sibling, me).wait_recv()
            for j, chip in enumerate(chips):
                copy(a, 4 + j, (*chip, 1 - c), me).wait_recv()
        for cp in first + passed:
            cp.wait_send()
        for cp in mine:
            cp.wait()

    hbm = pl.BlockSpec(memory_space=pl.ANY)
    return pl.pallas_call(
        body, name="gather_weights",
        out_shape=[jax.ShapeDtypeStruct((N_DEV,) + s.shape, s.dtype) for s in shards],
        in_specs=[hbm] * n, out_specs=[hbm] * n,
        scratch_shapes=[pltpu.SemaphoreType.DMA((7 * n,)), pltpu.SemaphoreType.DMA((7 * n,)), pltpu.SemaphoreType.DMA((n,))],
    )(*shards)


_HBM = pl.BlockSpec(memory_space=pltpu.HBM)
_SEM = pl.BlockSpec(memory_space=pltpu.SEMAPHORE)
_DATAFLOW = pltpu.SideEffectType.DATAFLOW_SIDE_EFFECTING


def _copies_start(name, plan, n_copies, srcs, lands, after):
    bufs = list(srcs) + list(lands)
    nb = len(bufs)

    def body(*refs):
        ins, send_sems, recv_sems, token = refs[:nb], refs[nb + len(after)], refs[nb + len(after) + 1], refs[-1]
        for i, (src, dst, dev) in enumerate(plan(ins[:len(srcs)], ins[len(srcs):])):
            pltpu.make_async_remote_copy(src_ref=src, dst_ref=dst, send_sem=send_sems.at[i], recv_sem=recv_sems.at[i],
                                         device_id=dev, device_id_type=MESH).start()
        token[...] = jnp.zeros_like(token)

    outs = pl.pallas_call(
        body, name=name,
        out_shape=(pltpu.SemaphoreType.DMA((n_copies,)), pltpu.SemaphoreType.DMA((n_copies,)),
                   *[pltpu.HBM(b.shape, b.dtype) for b in bufs], jax.ShapeDtypeStruct((8, 128), F32)),
        in_specs=[_HBM] * nb + [pl.BlockSpec(memory_space=pl.ANY)] * len(after),
        out_specs=(_SEM, _SEM, *[_HBM] * nb, pl.BlockSpec(memory_space=pltpu.VMEM)),
        input_output_aliases={i: 2 + i for i in range(nb)},
        compiler_params=pltpu.CompilerParams(has_side_effects=_DATAFLOW),
    )(*[pltpu.with_memory_space_constraint(b, pltpu.HBM) for b in bufs], *after)
    return outs[0], outs[1], list(outs[2:2 + len(srcs)]), list(outs[2 + len(srcs):2 + nb]), outs[-1]


def _copies_wait(name, plan, send_sems, recv_sems, srcs, lands, after):
    bufs = list(srcs) + list(lands)
    nb = len(bufs)

    def body(*refs):
        ins, send_ref, recv_ref = refs[:nb], refs[nb], refs[nb + 1]
        for i, (src, dst, dev) in enumerate(plan(ins[:len(srcs)], ins[len(srcs):])):
            cp = pltpu.make_async_remote_copy(src_ref=src, dst_ref=dst, send_sem=send_ref.at[i], recv_sem=recv_ref.at[i],
                                              device_id=dev, device_id_type=MESH)
            cp.wait_send()
            cp.wait_recv()

    outs = pl.pallas_call(
        body, name=name, out_shape=[pltpu.HBM(b.shape, b.dtype) for b in bufs],
        in_specs=[_HBM] * nb + [_SEM, _SEM] + [pl.BlockSpec(memory_space=pl.ANY)] * len(after), out_specs=[_HBM] * nb,
        input_output_aliases={i: i for i in range(nb)},
        compiler_params=pltpu.CompilerParams(has_side_effects=_DATAFLOW),
    )(*bufs, send_sems, recv_sems, *after)
    return list(outs[:len(srcs)]), list(outs[len(srcs):])


def _plan_gather_own(srcs, lands):
    _, me = _flip(0)
    return [(srcs[a], lands[a].at[me], _flip(k)[0]) for a in range(len(srcs)) for k in (1, 4, 2, 6)]


def _plan_gather_pass(srcs, lands):
    sibling = _flip(1)[0]
    plan = []
    for land in lands:
        for k in (4, 2, 6):
            block = land.at[_flip(k)[1]]
            plan.append((block, block, sibling))
    return plan


def _plan_reduce_pairs(srcs, lands):
    x, y, c = _mesh_pos()
    return [(srcs[a].at[chip, 1 - c], lands[a].at[chip], (x, y, 1 - c)) for a in range(len(srcs)) for chip in range(4)]


def _plan_reduce_chips(srcs, lands):
    plan = []
    for a in range(len(srcs)):
        for j, k in enumerate((4, 2, 6)):
            dev = _flip(k)[0]
            plan.append((srcs[a].at[2 * dev[0] + dev[1]], lands[a].at[j], dev))
    return plan


def _shard_rows(r):
    return r // 2 if r % 32 == 0 else r


def _pair_sum(core, grads, got, name):
    _, _, r, c = grads.shape
    tr = _shard_rows(r)

    def body(core_ref, a_ref, b_ref, o_ref, ob_ref):
        s = a_ref[...] + b_ref[...]
        o_ref[...] = s
        ob_ref[...] = s.astype(BF16)

    spec = pl.BlockSpec((None, tr, c), lambda i, j, core_ref: (i, j, 0))
    return pl.pallas_call(
        body, name=name,
        grid_spec=pltpu.PrefetchScalarGridSpec(
            num_scalar_prefetch=1, grid=(4, r // tr),
            in_specs=[pl.BlockSpec((None, None, tr, c), lambda i, j, core_ref: (i, core_ref[0], j, 0)), spec],
            out_specs=[spec, spec]),
        out_shape=[jax.ShapeDtypeStruct((4, r, c), F32), jax.ShapeDtypeStruct((4, r, c), BF16)],
        compiler_params=_params("parallel", "parallel"),
    )(core, grads, got)


def _ada_rows(c_all, w_ada, b_ada):
    n_cols = w_ada.shape[1]

    def body(c_ref, w_ref, b_ref, o_ref):
        _, me = _flip(0)
        bias = b_ref[:, pl.ds(pl.multiple_of(me * n_cols, 128), n_cols)]
        o_ref[...] = _dot_f32(_silu(c_ref[...]), w_ref[...]) + bias

    return pl.pallas_call(
        body, name="ada_rows", out_shape=jax.ShapeDtypeStruct((N_DEV, n_cols), F32),
        in_specs=[pl.BlockSpec(memory_space=pltpu.VMEM)] * 3, out_specs=pl.BlockSpec(memory_space=pltpu.VMEM),
    )(c_all, w_ada, b_ada)


def _in_fwd(x, mod, g1, w_in):
    s = x.shape[0]
    tm = 256

    def body(x_ref, mod_ref, g_ref, w_ref, h_ref, *outs):
        xhat, _ = _rms(x_ref[...])
        h = (xhat * g_ref[...]) * (1.0 + mod_ref[:, D_MODEL:2 * D_MODEL]) + mod_ref[:, 0:D_MODEL]
        hb = h.astype(BF16)
        h_ref[...] = hb
        for j, o_ref in enumerate(outs):
            o_ref[...] = _dot_nt(hb, w_ref[j * 512:(j + 1) * 512, :])

    return pl.pallas_call(
        body, name="in_fwd", grid=(s // tm,),
        out_shape=[jax.ShapeDtypeStruct((s, D_MODEL), BF16)] + [jax.ShapeDtypeStruct((s, 512), F32)] * 7,
        in_specs=[_rows(tm, D_MODEL), _whole((1, 6 * D_MODEL)), _whole((1, D_MODEL)), _whole((IN_WIDTH, D_MODEL))],
        out_specs=[_rows(tm, D_MODEL)] + [_rows(tm, 512)] * 7,
        compiler_params=_params("parallel"),
    )(x, mod, g1, w_in)


def _in_bwd(x, dx1, mod, g1, w_in, dps):
    s = x.shape[0]
    tm = 256

    def body(x_ref, dx_ref, mod_ref, g_ref, w_ref, *rest):
        dp_refs, (gx_ref, dpb_ref, dsh_ref, dsc_ref, dg_ref) = rest[:13], rest[13:]
        pieces = [dp_refs[j][...] for j in range(4)]
        pieces += [dp_refs[4 + 3 * j][...] + dp_refs[5 + 3 * j][...] + dp_refs[6 + 3 * j][...] for j in range(3)]
        for j, p in enumerate(pieces):
            dpb_ref[:, j * 512:(j + 1) * 512] = p.astype(BF16)
        dh = _dot(dpb_ref[...], w_ref[...])
        xhat, rstd = _rms(x_ref[...])
        g = g_ref[...]
        scale1 = 1.0 + mod_ref[:, D_MODEL:2 * D_MODEL]
        n1 = xhat * g

        @pl.when(pl.program_id(0) == 0)
        def _():
            dsh_ref[...] = jnp.zeros_like(dsh_ref)
            dsc_ref[...] = jnp.zeros_like(dsc_ref)
            dg_ref[...] = jnp.zeros_like(dg_ref)

        dsh_ref[...] += _rowsum(dh)
        dsc_ref[...] += _rowsum(dh * n1)
        dn = dh * scale1
        dg_ref[...] += _rowsum(dn * xhat)
        gx_ref[...] = dx_ref[...] + _rms_bwd(dn * g, xhat, rstd)

    vec = _whole((1, D_MODEL))
    return pl.pallas_call(
        body, name="in_bwd", grid=(s // tm,),
        out_shape=[jax.ShapeDtypeStruct((s, D_MODEL), F32), jax.ShapeDtypeStruct((s, IN_WIDTH), BF16)]
        + [jax.ShapeDtypeStruct((1, D_MODEL), F32)] * 3,
        in_specs=[_rows(tm, D_MODEL), _rows(tm, D_MODEL), _whole((1, 6 * D_MODEL)), vec, _whole((IN_WIDTH, D_MODEL))]
        + [_rows(tm, 512)] * 13,
        out_specs=[_rows(tm, D_MODEL), _rows(tm, IN_WIDTH), vec, vec, vec],
        compiler_params=_params("arbitrary"),
    )(x, dx1, mod, g1, w_in, *dps)


HG_TILE = 512
HG_TILE_CHUNKS = HG_TILE // HG_CHUNK


def _lower_bound(lg_ref):
    return 1.0 / (1.0 + jnp.exp(lg_ref[1:2, :] - lg_ref[0:1, :]))


def _chunk_masks():
    r = lax.broadcasted_iota(jnp.int32, (HG_CHUNK, HG_CHUNK), 0)
    c = lax.broadcasted_iota(jnp.int32, (HG_CHUNK, HG_CHUNK), 1)
    return r >= c, c >= r, (r >= c).astype(F32), (c >= r).astype(F32)


def _hg_fwd(hq, hf, hi, hgt, logits, onorm_g):
    s = hq.shape[0]
    n_tiles = s // HG_TILE

    def body(q_ref, f_ref, i_ref, g_ref, lg_ref, og_ref, out_ref, o_ref, st_ref, state, qf_s, kk_s, lf_s):
        @pl.when(pl.program_id(0) == 0)
        def _():
            state[...] = jnp.zeros_like(state)

        lb = _lower_bound(lg_ref)
        f = lb + (1.0 - lb) * _sigmoid(f_ref[...])
        kk_s[...] = 1.0 - f
        lf_s[...] = jnp.log(f)
        qf_s[...] = _silu(q_ref[...])
        causal, _, tri, _ = _chunk_masks()

        def chunk(ci, carry):
            rows = pl.ds(pl.multiple_of(ci * HG_CHUNK, HG_CHUNK), HG_CHUNK)
            srows = pl.ds(pl.multiple_of(ci * HG_DIM, HG_DIM), HG_DIM)
            lf = lf_s[rows, :]
            b = _dot_f32(tri, lf)
            bl = _rowsum(lf)
            ref = 0.5 * bl
            qf, kk, v = qf_s[rows, :], kk_s[rows, :], i_ref[rows, :]
            a_in = (qf * jnp.exp(b)).astype(BF16)
            a_t = (qf * jnp.exp(b - ref)).astype(BF16)
            b_t = (kk * jnp.exp(ref - b)).astype(BF16)
            kd = kk * jnp.exp(bl - b)
            ebl = jnp.exp(bl)
            vb = v.astype(BF16)
            for h in range(HG_HEADS):
                c = slice(h * HG_DIM, (h + 1) * HG_DIM)
                st = state[h]
                st_ref[srows, c] = st
                p = jnp.where(causal, _dot_nt(a_t[:, c], b_t[:, c]), 0.0)
                o_ref[rows, c] = _dot(p.astype(BF16), vb[:, c]) + _dot_nt(a_in[:, c], st.astype(BF16))
                state[h] = st * ebl[:, c] + _dot_tn(vb[:, c], kd[:, c].astype(BF16))
            return carry

        lax.fori_loop(0, HG_TILE_CHUNKS, chunk, 0, unroll=2)
        for h in range(HG_HEADS):
            c = slice(h * HG_DIM, (h + 1) * HG_DIM)
            ohat, _ = _rms(o_ref[:, c])
            out_ref[:, c] = (ohat * og_ref[...] * _silu(g_ref[:, c])).astype(BF16)

    tile = _rows(HG_TILE, HG_WIDTH)
    return pl.pallas_call(
        body, name="hg_fwd", grid=(n_tiles,),
        out_shape=[jax.ShapeDtypeStruct((s, HG_WIDTH), BF16), jax.ShapeDtypeStruct((s, HG_WIDTH), F32),
                   jax.ShapeDtypeStruct((s // HG_CHUNK * HG_DIM, HG_WIDTH), F32)],
        in_specs=[tile] * 4 + [_whole((2, HG_WIDTH)), _whole((1, HG_DIM))],
        out_specs=[tile, tile, _rows(HG_TILE_CHUNKS * HG_DIM, HG_WIDTH)],
        scratch_shapes=[pltpu.VMEM((HG_HEADS, HG_DIM, HG_DIM), F32)] + [pltpu.VMEM((HG_TILE, HG_WIDTH), F32)] * 3,
        compiler_params=_params("arbitrary"),
    )(hq, hf, hi, hgt, logits, onorm_g)


def _hg_bwd(hq, hf, hi, hgt, logits, onorm_g, o, states, dout):
    s = hq.shape[0]
    n_tiles = s // HG_TILE

    def body(q_ref, f_ref, i_ref, g_ref, lg_ref, og_ref, o_ref, st_ref, d_ref,
             dq_ref, df_ref, di_ref, dg_ref, dog_ref, dlb_ref, dstate, qf_s, kk_s, lf_s, do_s):
        @pl.when(pl.program_id(0) == 0)
        def _():
            dstate[...] = jnp.zeros_like(dstate)
            dog_ref[...] = jnp.zeros_like(dog_ref)
            dlb_ref[...] = jnp.zeros_like(dlb_ref)

        og = og_ref[...]
        dog = jnp.zeros((1, HG_DIM), F32)
        for h in range(HG_HEADS):
            c = slice(h * HG_DIM, (h + 1) * HG_DIM)
            ohat, rstd = _rms(o_ref[:, c])
            gate = g_ref[:, c]
            d = d_ref[:, c]
            dg_ref[:, c] = (d * (ohat * og) * _dsilu(gate)).astype(BF16)
            dnormed = d * _silu(gate)
            dog += _rowsum(dnormed * ohat)
            do_s[:, c] = _rms_bwd(dnormed * og, ohat, rstd)
        dog_ref[...] += dog

        lb = _lower_bound(lg_ref)
        f = lb + (1.0 - lb) * _sigmoid(f_ref[...])
        kk_s[...] = 1.0 - f
        lf_s[...] = jnp.log(f)
        qf_s[...] = _silu(q_ref[...])
        causal, upper, tri, tri_t = _chunk_masks()

        def chunk(step, carry):
            ci = HG_TILE_CHUNKS - 1 - step
            rows = pl.ds(pl.multiple_of(ci * HG_CHUNK, HG_CHUNK), HG_CHUNK)
            srows = pl.ds(pl.multiple_of(ci * HG_DIM, HG_DIM), HG_DIM)
            lf = lf_s[rows, :]
            b = _dot_f32(tri, lf)
            bl = _rowsum(lf)
            ref = 0.5 * bl
            qf, kk, v, do = qf_s[rows, :], kk_s[rows, :], i_ref[rows, :], do_s[rows, :]
            eb, ebr, erb, ekd, ebl = jnp.exp(b), jnp.exp(b - ref), jnp.exp(ref - b), jnp.exp(bl - b), jnp.exp(bl)
            a_in, a_t, b_t, kd = qf * eb, qf * ebr, kk * erb, kk * ekd
            for h in range(HG_HEADS):
                c = slice(h * HG_DIM, (h + 1) * HG_DIM)
                st, dst = st_ref[srows, c], dstate[h]
                stb, dstb = st.astype(BF16), dst.astype(BF16)
                doh, vh = do[:, c], v[:, c]
                dob, vb = doh.astype(BF16), vh.astype(BF16)
                ain_h, at_h, bt_h, kd_h = a_in[:, c], a_t[:, c], b_t[:, c], kd[:, c]
                atb, btb = at_h.astype(BF16), bt_h.astype(BF16)
                d_ain = _dot(dob, stb)
                p_t = jnp.where(upper, _dot_nt(btb, atb), 0.0).astype(BF16)
                dp = jnp.where(causal, _dot_nt(dob, vb), 0.0).astype(BF16)
                dp_t = jnp.where(upper, _dot_nt(vb, dob), 0.0).astype(BF16)
                di_ref[rows, c] = (_dot(p_t, dob) + _dot_nt(kd_h.astype(BF16), dstb)).astype(BF16)
                d_at = _dot(dp, btb)
                d_bt = _dot(dp_t, atb)
                d_kd = _dot(vb, dstb)
                dqf = d_ain * eb[:, c] + d_at * ebr[:, c]
                dkk = d_bt * erb[:, c] + d_kd * ekd[:, c]
                db = d_ain * ain_h + d_at * atb.astype(F32) - d_bt * btb.astype(F32) - d_kd * kd_h
                dbl = _rowsum(d_kd * kd_h) + _rowsum(dst * st) * ebl[:, c]
                dstate[h] = _dot_tn(dob, ain_h.astype(BF16)) + dst * ebl[:, c]
                dlf = _dot_f32(tri_t, db) + dbl
                qv, fr = q_ref[rows, c], f_ref[rows, c]
                lbh = lb[:, c]
                sg = _sigmoid(fr)
                dfv = dlf / (lbh + (1.0 - lbh) * sg) - dkk
                df_ref[rows, c] = (dfv * (1.0 - lbh) * sg * (1.0 - sg)).astype(BF16)
                dlb_ref[:, c] += _rowsum(dfv * (1.0 - sg))
                dq_ref[rows, c] = (dqf * _dsilu(qv)).astype(BF16)
            return carry

        lax.fori_loop(0, HG_TILE_CHUNKS, chunk, 0, unroll=2)

    rev = pl.BlockSpec((HG_TILE, HG_WIDTH), lambda i: (n_tiles - 1 - i, 0))
    return pl.pallas_call(
        body, name="hg_bwd", grid=(n_tiles,),
        out_shape=[jax.ShapeDtypeStruct((s, HG_WIDTH), BF16)] * 4
        + [jax.ShapeDtypeStruct((1, HG_DIM), F32), jax.ShapeDtypeStruct((1, HG_WIDTH), F32)],
        in_specs=[rev] * 4 + [_whole((2, HG_WIDTH)), _whole((1, HG_DIM)), rev,
                              pl.BlockSpec((HG_TILE_CHUNKS * HG_DIM, HG_WIDTH), lambda i: (n_tiles - 1 - i, 0)), rev],
        out_specs=[rev] * 4 + [_whole((1, HG_DIM)), _whole((1, HG_WIDTH))],
        scratch_shapes=[pltpu.VMEM((HG_HEADS, HG_DIM, HG_DIM), F32)] + [pltpu.VMEM((HG_TILE, HG_WIDTH), F32)] * 4,
        compiler_params=_params("arbitrary"),
    )(hq, hf, hi, hgt, logits, onorm_g, o, states, dout)


TOKEN_GROUP = 16


def _att_geometry(dil):
    per_group = TOKEN_GROUP // dil
    return per_group, ATT_BLOCK // per_group, ATT_WIDTH if dil == 1 else 128


def _att_consts(dil):
    per_group, ub, _ = _att_geometry(dil)

    def pos(i):
        return i if dil == 1 else (i % ub) * per_group + i // ub

    lane = lax.broadcasted_iota(jnp.int32, (ATT_BLOCK, 128), 1)
    qi = pos(lax.broadcasted_iota(jnp.int32, (2 * ATT_BLOCK, ATT_BLOCK), 0) % ATT_BLOCK)
    kj = pos(lax.broadcasted_iota(jnp.int32, (2 * ATT_BLOCK, ATT_BLOCK), 1))
    return lane < ATT_HEAD_DIM, kj <= qi, lambda off: kj >= qi + off


def _load_tile(ref, dil, r, c):
    if dil == 1:
        return ref[:, c]
    per_group, ub, _ = _att_geometry(dil)
    return jnp.concatenate([ref[pl.ds(dil * w + r, ub, stride=TOKEN_GROUP), c] for w in range(per_group)], axis=0)


def _store_tile(ref, dil, r, c, val):
    if dil == 1:
        ref[:, c] = val
        return
    per_group, ub, _ = _att_geometry(dil)
    for w in range(per_group):
        ref[pl.ds(dil * w + r, ub, stride=TOKEN_GROUP), c] = val[w * ub:(w + 1) * ub]


def _att_specs(seq, dil):
    _, ub, lanes = _att_geometry(dil)
    rows = ub * TOKEN_GROUP
    nb = seq // rows
    cur = pl.BlockSpec((rows, lanes), lambda n, j: (n, j))
    prev = pl.BlockSpec((rows, lanes), lambda n, j: (jnp.maximum(n - 1, 0), j))
    nxt = pl.BlockSpec((rows, lanes), lambda n, j: (jnp.minimum(n + 1, nb - 1), j))
    return nb, lanes, cur, prev, nxt


def _stack_heads(x2, first):
    return jnp.concatenate([jnp.where(first, x2, 0.0), jnp.where(first, 0.0, x2)], axis=0)


def _stack_bcast(x2, first):
    other = pltpu.roll(x2, ATT_HEAD_DIM, axis=1)
    return jnp.concatenate([jnp.where(first, x2, other), jnp.where(first, other, x2)], axis=0)


def _unstack_heads(st, first):
    return jnp.where(first, st[:ATT_BLOCK], st[ATT_BLOCK:])


def _att_fwd(q, k, v, dil):
    seq, width = q.shape
    nb, lanes, cur, prev, _ = _att_specs(seq, dil)

    def body(q_ref, kc_ref, kp_ref, vc_ref, vp_ref, o_ref, lse_ref):
        first, cur_ok, _band = _att_consts(dil)
        prev_ok = _band(jnp.where(pl.program_id(0) > 0, 0, ATT_BLOCK))
        for r in range(dil):
            for j in range(lanes // 128):
                c = slice(j * 128, (j + 1) * 128)
                qst = _stack_heads(_load_tile(q_ref, dil, r, c) * ATT_SCALE, first).astype(BF16)
                kc, kp = _load_tile(kc_ref, dil, r, c).astype(BF16), _load_tile(kp_ref, dil, r, c).astype(BF16)
                vc, vp = _load_tile(vc_ref, dil, r, c).astype(BF16), _load_tile(vp_ref, dil, r, c).astype(BF16)
                sc = jnp.where(cur_ok, _dot_nt(qst, kc), NEG)
                sp = jnp.where(prev_ok, _dot_nt(qst, kp), NEG)
                mx = jnp.maximum(jnp.max(sc, axis=-1, keepdims=True), jnp.max(sp, axis=-1, keepdims=True))
                pc, pp = jnp.exp(sc - mx), jnp.exp(sp - mx)
                den = jnp.sum(pc, axis=-1, keepdims=True) + jnp.sum(pp, axis=-1, keepdims=True)
                ost = (_dot(pc.astype(BF16), vc) + _dot(pp.astype(BF16), vp)) / den
                lse = jnp.broadcast_to(mx + jnp.log(den), (2 * ATT_BLOCK, 128))
                _store_tile(o_ref, dil, r, c, _unstack_heads(ost, first))
                _store_tile(lse_ref, dil, r, c, _unstack_heads(lse, first))

    return pl.pallas_call(
        body, name=f"att_fwd_d{dil}", grid=(nb, width // lanes),
        out_shape=[jax.ShapeDtypeStruct((seq, width), F32)] * 2,
        in_specs=[cur, cur, prev, cur, prev], out_specs=[cur, cur],
        compiler_params=_params("arbitrary", "arbitrary"),
    )(q, k, k, v, v)


def _att_bwd(q, k, v, do, cc, lse, dil):
    seq, width = q.shape
    _, ub, lanes = _att_geometry(dil)
    rows = ub * TOKEN_GROUP
    nb = seq // rows

    def body(q_ref, qx_ref, k_ref, v_ref, do_ref, dox_ref, cc_ref, ccx_ref, lse_ref, lsex_ref,
             dq_ref, dk_ref, dv_ref, carry):
        first, cur_ok, _band = _att_consts(dil)
        n = pl.program_id(1)
        next_ok = _band(jnp.where(n < nb - 1, 0, ATT_BLOCK))

        @pl.when(n == 0)
        def _():
            carry[...] = jnp.zeros_like(carry)

        for r in range(dil):
            for j in range(lanes // 128):
                c = slice(j * 128, (j + 1) * 128)
                qst = _stack_heads(_load_tile(q_ref, dil, r, c) * ATT_SCALE, first).astype(BF16)
                qxst = _stack_heads(_load_tile(qx_ref, dil, r, c) * ATT_SCALE, first).astype(BF16)
                dost = _stack_heads(_load_tile(do_ref, dil, r, c), first).astype(BF16)
                doxst = _stack_heads(_load_tile(dox_ref, dil, r, c), first).astype(BF16)
                lse_n = _stack_bcast(_load_tile(lse_ref, dil, r, c), first)
                lse_x = _stack_bcast(_load_tile(lsex_ref, dil, r, c), first)
                cc_n = _stack_bcast(_load_tile(cc_ref, dil, r, c), first)
                cc_x = _stack_bcast(_load_tile(ccx_ref, dil, r, c), first)
                kb, vb = _load_tile(k_ref, dil, r, c).astype(BF16), _load_tile(v_ref, dil, r, c).astype(BF16)
                p_cur = jnp.exp(jnp.where(cur_ok, _dot_nt(qst, kb), NEG) - lse_n)
                p_next = jnp.exp(jnp.where(next_ok, _dot_nt(qxst, kb), NEG) - lse_x)
                ds_cur = (p_cur * (_dot_nt(dost, vb) + cc_n)).astype(BF16)
                ds_next = (p_next * (_dot_nt(doxst, vb) + cc_x)).astype(BF16)
                dq_own = _load_tile(carry, dil, r, c) + _unstack_heads(_dot(ds_cur, kb), first)
                _store_tile(dq_ref, dil, r, c, dq_own * ATT_SCALE)
                _store_tile(carry, dil, r, c, _unstack_heads(_dot(ds_next, kb), first))
                _store_tile(dk_ref, dil, r, c, _dot_tn(ds_cur, qst) + _dot_tn(ds_next, qxst))
                _store_tile(dv_ref, dil, r, c, _dot_tn(p_cur.astype(BF16), dost) + _dot_tn(p_next.astype(BF16), doxst))

    cur = pl.BlockSpec((rows, lanes), lambda j, n: (n, j))
    nxt = pl.BlockSpec((rows, lanes), lambda j, n: (jnp.minimum(n + 1, nb - 1), j))
    return pl.pallas_call(
        body, name=f"att_bwd_d{dil}", grid=(width // lanes, nb),
        out_shape=[jax.ShapeDtypeStruct((seq, width), F32)] * 3,
        in_specs=[cur, nxt, cur, cur, cur, nxt, cur, nxt, cur, nxt], out_specs=[cur, cur, cur],
        scratch_shapes=[pltpu.VMEM((rows, lanes), F32)],
        compiler_params=_params("arbitrary", "arbitrary"),
    )(q, q, k, v, do, do, cc, cc, lse, lse)


def _branch_weights(lses):
    mx = jnp.maximum(jnp.maximum(lses[0], lses[1]), lses[2])
    es = [jnp.exp(l - mx) for l in lses]
    inv = 1.0 / (es[0] + es[1] + es[2])
    return [e * inv for e in es]


def _att_combine(outs, lses, att_g):
    s = outs[0].shape[0]
    tm = 512

    def body(o0, o1, o2, l0, l1, l2, g_ref, att_ref, out_ref):
        ws = _branch_weights([l0[...], l1[...], l2[...]])
        att = ws[0] * o0[...] + ws[1] * o1[...] + ws[2] * o2[...]
        att_ref[...] = att
        ahat, _ = _rms(att)
        out_ref[...] = (ahat * g_ref[...]).astype(BF16)

    tile = _rows(tm, ATT_WIDTH)
    return pl.pallas_call(
        body, name="att_combine", grid=(s // tm,),
        out_shape=[jax.ShapeDtypeStruct((s, ATT_WIDTH), F32), jax.ShapeDtypeStruct((s, ATT_WIDTH), BF16)],
        in_specs=[tile] * 6 + [_whole((1, ATT_WIDTH))], out_specs=[tile, tile],
        compiler_params=_params("parallel"),
    )(*outs, *lses, att_g)


def _att_combine_bwd(datt_out, att, lses, att_g):
    s = att.shape[0]
    tm = 256

    def body(d_ref, att_ref, l0, l1, l2, g_ref, do0, do1, do2, cc0, cc1, cc2, dg_ref):
        @pl.when(pl.program_id(0) == 0)
        def _():
            dg_ref[...] = jnp.zeros_like(dg_ref)

        att = att_ref[...]
        ahat, rstd = _rms(att)
        d = d_ref[...]
        dg_ref[...] += _rowsum(d * ahat)
        datt = _rms_bwd(d * g_ref[...], ahat, rstd)
        hi = lax.broadcasted_iota(jnp.int32, (ATT_WIDTH, ATT_WIDTH), 0) // ATT_HEAD_DIM
        hj = lax.broadcasted_iota(jnp.int32, (ATT_WIDTH, ATT_WIDTH), 1) // ATT_HEAD_DIM
        head_sum = _dot_f32(datt * att, (hi == hj).astype(F32))
        ws = _branch_weights([l0[...], l1[...], l2[...]])
        for w, do_ref, cc_ref in zip(ws, (do0, do1, do2), (cc0, cc1, cc2)):
            do_ref[...] = w * datt
            cc_ref[...] = -w * head_sum

    tile = _rows(tm, ATT_WIDTH)
    return pl.pallas_call(
        body, name="att_combine_bwd", grid=(s // tm,),
        out_shape=[jax.ShapeDtypeStruct((s, ATT_WIDTH), F32)] * 6 + [jax.ShapeDtypeStruct((1, ATT_WIDTH), F32)],
        in_specs=[tile] * 5 + [_whole((1, ATT_WIDTH))], out_specs=[tile] * 6 + [_whole((1, ATT_WIDTH))],
        compiler_params=_params("arbitrary"),
    )(datt_out, att, *lses, att_g)


def _out_fwd(x, hg, at, mod, w_out):
    s = x.shape[0]
    tm = 512

    def body(x_ref, hg_ref, at_ref, mod_ref, w_ref, x1_ref):
        mix = _dot(hg_ref[...], w_ref[0:512, :]) + _dot(at_ref[...], w_ref[512:1024, :])
        x1_ref[...] = x_ref[...] + mod_ref[:, 2 * D_MODEL:3 * D_MODEL] * mix

    return pl.pallas_call(
        body, name="out_fwd", grid=(s // tm,), out_shape=jax.ShapeDtypeStruct((s, D_MODEL), F32),
        in_specs=[_rows(tm, D_MODEL), _rows(tm, 512), _rows(tm, 512), _whole((1, 6 * D_MODEL)), _whole((D_MODEL, D_MODEL))],
        out_specs=_rows(tm, D_MODEL), compiler_params=_params("parallel"),
    )(x, hg, at, mod, w_out)


def _out_bwd(dx1, hg, at, mod, w_out):
    s = dx1.shape[0]
    tm = 512

    def body(dx_ref, hg_ref, at_ref, mod_ref, w_ref, dhg_ref, dat_ref, dw_ref, dgate_ref):
        @pl.when(pl.program_id(0) == 0)
        def _():
            dw_ref[...] = jnp.zeros_like(dw_ref)
            dgate_ref[...] = jnp.zeros_like(dgate_ref)

        hg, at, dx = hg_ref[...], at_ref[...], dx_ref[...]
        mix = _dot(hg, w_ref[0:512, :]) + _dot(at, w_ref[512:1024, :])
        dgate_ref[...] += _rowsum(dx * mix)
        dmix = (mod_ref[:, 2 * D_MODEL:3 * D_MODEL] * dx).astype(BF16)
        dhg_ref[...] = _dot_nt(dmix, w_ref[0:512, :])
        dat_ref[...] = _dot_nt(dmix, w_ref[512:1024, :])
        dw_ref[0:512, :] += _dot_tn(hg, dmix)
        dw_ref[512:1024, :] += _dot_tn(at, dmix)

    return pl.pallas_call(
        body, name="out_bwd", grid=(s // tm,),
        out_shape=[jax.ShapeDtypeStruct((s, 512), F32)] * 2
        + [jax.ShapeDtypeStruct((D_MODEL, D_MODEL), F32), jax.ShapeDtypeStruct((1, D_MODEL), F32)],
        in_specs=[_rows(tm, D_MODEL), _rows(tm, 512), _rows(tm, 512), _whole((1, 6 * D_MODEL)), _whole((D_MODEL, D_MODEL))],
        out_specs=[_rows(tm, 512), _rows(tm, 512), _whole((D_MODEL, D_MODEL)), _whole((1, D_MODEL))],
        compiler_params=_params("arbitrary"),
    )(dx1, hg, at, mod, w_out)


FFN_CHUNK = 2816


def _ffn(x1, target, mod, g2, gf, w_gu, w_down):
    s = x1.shape[0]
    tm = 256
    n_chunks = D_FF // FFN_CHUNK

    def body(x_ref, t_ref, mod_ref, g2_ref, gf_ref, wgu_hbm, wd_hbm,
             dx_ref, h2_ref, act_ref, dau_ref, dff_ref, sums_ref, loss_ref, wgu, wd, a_s, u_s, sem):
        @pl.when(pl.program_id(0) == 0)
        def _():
            c1 = pltpu.make_async_copy(wgu_hbm, wgu, sem.at[0])
            c2 = pltpu.make_async_copy(wd_hbm, wd, sem.at[1])
            c1.start()
            c2.start()
            c1.wait()
            c2.wait()
            sums_ref[...] = jnp.zeros_like(sums_ref)
            loss_ref[...] = jnp.zeros_like(loss_ref)

        x1v = x_ref[...]
        xhat, rstd = _rms(x1v)
        g2 = g2_ref[...]
        n2 = xhat * g2
        scale2 = 1.0 + mod_ref[:, 4 * D_MODEL:5 * D_MODEL]
        gate2 = mod_ref[:, 5 * D_MODEL:6 * D_MODEL]
        hb = (n2 * scale2 + mod_ref[:, 3 * D_MODEL:4 * D_MODEL]).astype(BF16)
        h2_ref[...] = hb
        ff = jnp.zeros((tm, D_MODEL), F32)
        for j in range(n_chunks):
            c = slice(j * FFN_CHUNK, (j + 1) * FFN_CHUNK)
            cu = slice(D_FF + j * FFN_CHUNK, D_FF + (j + 1) * FFN_CHUNK)
            a = _dot_nt(hb, wgu[c, :])
            u = _dot_nt(hb, wgu[cu, :])
            a_s[:, c] = a
            u_s[:, c] = u
            act = (_silu(a) * u).astype(BF16)
            act_ref[:, c] = act
            ff += _dot(act, wd[c, :])
        x2 = x1v + gate2 * ff
        nf, rstd_f = _rms(x2)
        gfv = gf_ref[...]
        err = nf * gfv - t_ref[...]
        loss_ref[...] += 0.5 * jnp.sum(_rowsum(err * err), axis=-1, keepdims=True) * (1.0 / D_MODEL)
        dy = err * (1.0 / D_MODEL)
        dx2 = _rms_bwd(dy * gfv, nf, rstd_f)
        dffb = (gate2 * dx2).astype(BF16)
        dff_ref[...] = dffb
        dh = jnp.zeros((tm, D_MODEL), F32)
        for j in range(n_chunks):
            c = slice(j * FFN_CHUNK, (j + 1) * FFN_CHUNK)
            cu = slice(D_FF + j * FFN_CHUNK, D_FF + (j + 1) * FFN_CHUNK)
            dact = _dot_nt(dffb, wd[c, :])
            a, u = a_s[:, c], u_s[:, c]
            da = (dact * u * _dsilu(a)).astype(BF16)
            du = (dact * _silu(a)).astype(BF16)
            dau_ref[:, c] = da
            dau_ref[:, cu] = du
            dh += _dot(da, wgu[c, :]) + _dot(du, wgu[cu, :])
        dn = dh * scale2
        sums_ref[0:1, :] += _rowsum(dh)
        sums_ref[1:2, :] += _rowsum(dh * n2)
        sums_ref[2:3, :] += _rowsum(dx2 * ff)
        sums_ref[3:4, :] += _rowsum(dn * xhat)
        sums_ref[4:5, :] += _rowsum(dy * nf)
        dx_ref[...] = dx2 + _rms_bwd(dn * g2, xhat, rstd)

    vec = _whole((1, D_MODEL))
    hbm = pl.BlockSpec(memory_space=pl.ANY)
    return pl.pallas_call(
        body, name="ffn", grid=(s // tm,),
        out_shape=[jax.ShapeDtypeStruct((s, D_MODEL), F32), jax.ShapeDtypeStruct((s, D_MODEL), BF16),
                   jax.ShapeDtypeStruct((s, D_FF), BF16), jax.ShapeDtypeStruct((s, 2 * D_FF), BF16),
                   jax.ShapeDtypeStruct((s, D_MODEL), BF16), jax.ShapeDtypeStruct((8, D_MODEL), F32),
                   jax.ShapeDtypeStruct((1, 128), F32)],
        in_specs=[_rows(tm, D_MODEL), _rows(tm, D_MODEL), _whole((1, 6 * D_MODEL)), vec, vec, hbm, hbm],
        out_specs=[_rows(tm, D_MODEL), _rows(tm, D_MODEL), _rows(tm, D_FF), _rows(tm, 2 * D_FF), _rows(tm, D_MODEL),
                   _whole((8, D_MODEL)), _whole((1, 128))],
        scratch_shapes=[pltpu.VMEM((2 * D_FF, D_MODEL), BF16), pltpu.VMEM((D_FF, D_MODEL), BF16),
                        pltpu.VMEM((tm, D_FF), F32), pltpu.VMEM((tm, D_FF), F32), pltpu.SemaphoreType.DMA((2,))],
        compiler_params=_params("arbitrary"),
    )(x1, target, mod, g2, gf, w_gu, w_down)


def _weight_grad(a, b, name):
    s, m = a.shape
    n = b.shape[1]
    ts = min(s, 2048)
    tm = 512 if m % 512 == 0 else 256

    def body(a_ref, b_ref, o_ref):
        @pl.when(pl.program_id(1) == 0)
        def _():
            o_ref[...] = jnp.zeros_like(o_ref)

        o_ref[...] += _dot_tn(a_ref[...], b_ref[...])

    return pl.pallas_call(
        body, name=name, grid=(m // tm, s // ts), out_shape=jax.ShapeDtypeStruct((m, n), F32),
        in_specs=[pl.BlockSpec((ts, tm), lambda j, i: (i, j)), pl.BlockSpec((ts, n), lambda j, i: (i, 0))],
        out_specs=pl.BlockSpec((tm, n), lambda j, i: (j, 0)),
        compiler_params=_params("parallel", "arbitrary"),
    )(a, b)


def _adamw_math(w, g, m, v):
    m = ADAM_B1 * m + (1.0 - ADAM_B1) * g
    v = ADAM_B2 * v + (1.0 - ADAM_B2) * (g * g)
    m_hat = m / (1.0 - ADAM_B1 ** ADAM_STEP)
    v_hat = v / (1.0 - ADAM_B2 ** ADAM_STEP)
    delta = -ADAM_LR * (m_hat / (jnp.sqrt(v_hat) + ADAM_EPS) + ADAM_WD * w)
    return delta, m, v


def _adamw_shard(chip, w, m, v, partial, got, name):
    r, c = w.shape
    tr = _shard_rows(r)

    def body(chip_ref, w_ref, m_ref, v_ref, own_ref, g0, g1, g2, grad_ref, d_ref, nm_ref, nv_ref):
        g = ((own_ref[...] + g0[...].astype(F32)) + g1[...].astype(F32)) + g2[...].astype(F32)
        grad_ref[...] = g
        d_ref[...], nm_ref[...], nv_ref[...] = _adamw_math(w_ref[...], g, m_ref[...], v_ref[...])

    tile = pl.BlockSpec((tr, c), lambda i, chip_ref: (i, 0))
    own = pl.BlockSpec((None, tr, c), lambda i, chip_ref: (chip_ref[0], i, 0))
    part = [pl.BlockSpec((None, tr, c), functools.partial(lambda j, i, chip_ref: (j, i, 0), j)) for j in range(3)]
    return pl.pallas_call(
        body, name=name,
        grid_spec=pltpu.PrefetchScalarGridSpec(num_scalar_prefetch=1, grid=(r // tr,), in_specs=[tile] * 3 + [own] + part,
                                               out_specs=[tile] * 4),
        out_shape=[jax.ShapeDtypeStruct((r, c), F32)] * 4, compiler_params=_params("parallel"),
    )(chip, w, m, v, partial, got, got, got)


def _small_update(small_all, dmod_blocks, c_all, logits, w_ada, m_ada, v_ada, smalls):
    def body(sm_ref, dm_ref, c_ref, lg_ref, wa_ref, ma_ref, va_ref, *rest):
        ins, outs = rest[:21], rest[21:]
        _, me = _flip(0)
        tot = sm_ref[0:1, :]
        for i in range(1, N_DEV):
            tot = tot + sm_ref[i:i + 1, :]
        loss_ref = outs[0]
        loss_ref[...] = tot[:, SM_LOSS:SM_LOSS + 128]
        g_ada = lax.dot_general(_silu(c_ref[...]), dm_ref[me], (((0,), (0,)), ((), ())),
                                preferred_element_type=F32, precision=HIGHEST)
        outs[1][...] = g_ada
        outs[2][...], outs[3][...], outs[4][...] = _adamw_math(wa_ref[...], g_ada, ma_ref[...], va_ref[...])
        p0 = _lower_bound(lg_ref)
        dl0 = tot[:, SM_LB:SM_LB + 512] * p0 * (1.0 - p0)
        grads = [tot[:, SM_MOD:SM_MOD + 6 * D_MODEL], tot[:, SM_G1:SM_G1 + D_MODEL], tot[:, SM_G2:SM_G2 + D_MODEL],
                 tot[:, SM_GF:SM_GF + D_MODEL], tot[:, SM_ATT:SM_ATT + 512], tot[:, SM_HG:SM_HG + 128],
                 jnp.where(lax.broadcasted_iota(jnp.int32, (2, 512), 0) == 0, dl0, -dl0)]
        for i, g in enumerate(grads):
            w_ref, m_ref, v_ref = ins[3 * i:3 * i + 3]
            o = outs[5 + 4 * i:9 + 4 * i]
            o[0][...] = g
            o[1][...], o[2][...], o[3][...] = _adamw_math(w_ref[...], g, m_ref[...], v_ref[...])

    flat = [t for trio in smalls for t in trio]
    vm = pl.BlockSpec(memory_space=pltpu.VMEM)
    out_shape = [jax.ShapeDtypeStruct((1, 128), F32)] + [jax.ShapeDtypeStruct(w_ada.shape, F32)] * 4
    for trio in smalls:
        out_shape += [jax.ShapeDtypeStruct(trio[0].shape, F32)] * 4
    return pl.pallas_call(
        body, name="small_update", out_shape=out_shape,
        in_specs=[vm] * (7 + len(flat)), out_specs=[vm] * len(out_shape),
        compiler_params=pltpu.CompilerParams(vmem_limit_bytes=V7X_VMEM_LIMIT),
    )(small_all, dmod_blocks, c_all, logits, w_ada, m_ada, v_ada, *flat)


def kernel(x, c, w_ada, b_ada, norm1_g, w_in, hg_lb_logits, hg_onorm_g, att_onorm_g, w_out, norm2_g, w_gate_up, w_down, final_g, loss_target, m_w_ada, m_b_ada, m_norm1_g, m_w_in, m_hg_lb_logits, m_hg_onorm_g, m_att_onorm_g, m_w_out, m_norm2_g, m_w_gate_up, m_w_down, m_final_g, v_w_ada, v_b_ada, v_norm1_g, v_w_in, v_hg_lb_logits, v_hg_onorm_g, v_att_onorm_g, v_w_out, v_norm2_g, v_w_gate_up, v_w_down, v_final_g):
    x2d, target = x[0], loss_target[0]
    seq = x2d.shape[0]
    assert seq % (ATT_BLOCK * max(DILATIONS)) == 0 and seq % HG_TILE == 0
    gf = final_g.reshape(1, D_MODEL)

    c_all = _exchange_small(c.reshape(8, D_MODEL // 8), None, "gather_c").reshape(N_DEV, D_MODEL)
    ada = _ada_rows(c_all, w_ada[0], b_ada)
    mod = _exchange_small(ada, 1, "scatter_mod").reshape(1, 6 * D_MODEL)

    core = lax.axis_index("c").astype(jnp.int32).reshape(1)
    chip = (2 * lax.axis_index("x") + lax.axis_index("y")).astype(jnp.int32).reshape(1)
    me = 4 * lax.axis_index("x") + 2 * lax.axis_index("y") + lax.axis_index("c")

    g_in, = _gather_weights([w_in[0].T.astype(BF16)])
    w_in_b = g_in.reshape(IN_WIDTH, D_MODEL)
    rest_shards = [w_out[0].astype(BF16), w_gate_up[0].T.astype(BF16), w_down[0].astype(BF16)]
    lands = [lax.empty((N_DEV,) + s.shape, BF16) for s in rest_shards]
    g_send, g_recv, g_srcs, g_lands, tok = _copies_start("gather_rest_start", _plan_gather_own, 12, rest_shards, lands, [w_in_b, mod])
    flight = {}

    def stage(name, *vals):
        if name == "attention_begun":
            flight["shards"], got = _copies_wait("gather_rest_wait", _plan_gather_own, g_send, g_recv, g_srcs, g_lands, [vals[0]])
            flight["pass"] = _copies_start("gather_pass_start", _plan_gather_pass, 9, [], got, [])
            return flight["pass"][4][0:1, 0:1]
        if name == "mixer_weights_done":
            dw_gu, dw_down, dw_out = vals
            flight["grads"] = [dw_out.reshape(4, 2, D_MODEL // N_DEV, D_MODEL),
                               dw_gu.reshape(4, 2, 2 * D_FF // N_DEV, D_MODEL),
                               dw_down.reshape(4, 2, D_FF // N_DEV, D_MODEL)]
            pair_lands = [lax.empty((4,) + g.shape[2:], F32) for g in flight["grads"]]
            flight["pairs"] = _copies_start("reduce_pairs_start", _plan_reduce_pairs, 12, flight["grads"], pair_lands, [])
            return flight["pairs"][4][0:1, 0:1]
        if name == "attention_backward_begun":
            s, r, srcs, pl_lands, _ = flight["pairs"]
            grads, got = _copies_wait("reduce_pairs_wait", _plan_reduce_pairs, s, r, srcs, pl_lands, [vals[0]])
            flight["sums"] = [_pair_sum(core, g, b, f"pair_sum_{i}") for i, (g, b) in enumerate(zip(grads, got))]
            chip_lands = [lax.empty((3,) + s16.shape[1:], BF16) for _, s16 in flight["sums"]]
            flight["chips"] = _copies_start("reduce_chips_start", _plan_reduce_chips, 9, [s16 for _, s16 in flight["sums"]], chip_lands, [])
            return flight["chips"][4][0:1, 0:1]
        raise ValueError(name)

    def rest_weights(after):
        s, r, _, p_lands, _ = flight["pass"]
        _, got = _copies_wait("gather_pass_wait", _plan_gather_pass, s, r, [], p_lands, [after])
        full = [lax.dynamic_update_index_in_dim(g, shard, me, 0) for g, shard in zip(got, flight["shards"])]
        return full[0].reshape(D_MODEL, D_MODEL), full[1].reshape(2 * D_FF, D_MODEL), full[2].reshape(D_FF, D_MODEL)

    grad_x, dw_in, small = _block_step(x2d, target, mod + tok[0:1, 0:1], norm1_g, hg_lb_logits, hg_onorm_g, att_onorm_g, norm2_g, gf,
                                       w_in_b, rest_weights, stage)

    g_in8 = dw_in.reshape(4, 2, IN_WIDTH // N_DEV, D_MODEL)
    in_pairs = _copies_start("reduce_pairs_in_start", _plan_reduce_pairs, 4, [g_in8], [lax.empty((4,) + g_in8.shape[2:], F32)], [])
    s, r, srcs, c_lands, _ = flight["chips"]
    _, recv_rest = _copies_wait("reduce_chips_wait", _plan_reduce_chips, s, r, srcs, c_lands, [in_pairs[4]])
    small_rows = jnp.pad(small + in_pairs[4][0:1, 0:1], ((0, 0), (0, SM_PADDED - SM_WIDTH))).reshape(SM_PADDED // 128, 128)
    small_all = _exchange_small(small_rows, None, "gather_small").reshape(N_DEV, SM_PADDED)[:, :SM_WIDTH]
    in_grads, got_in = _copies_wait("reduce_pairs_in_wait", _plan_reduce_pairs, in_pairs[0], in_pairs[1], in_pairs[2], in_pairs[3],
                                    [small_all])
    in_s32, in_s16 = _pair_sum(core, in_grads[0], got_in[0], "pair_sum_in")
    in_chips = _copies_start("reduce_chips_in_start", _plan_reduce_chips, 3, [in_s16], [lax.empty((3,) + in_s16.shape[1:], BF16)], [])
    big, updated = {}, []
    rest_params = [("w_out", w_out, m_w_out, v_w_out), ("w_gate_up", w_gate_up, m_w_gate_up, v_w_gate_up), ("w_down", w_down, m_w_down, v_w_down)]
    for (n, w, m, v), (s32, _), got in zip(rest_params, flight["sums"], recv_rest):
        if n == "w_gate_up":
            outs4 = _adamw_shard(chip, w[0].T, m[0].T, v[0].T, s32, got, f"adamw_{n}")
            big[n] = [t.T[None] for t in outs4]
        else:
            outs4 = _adamw_shard(chip, w[0], m[0], v[0], s32, got, f"adamw_{n}")
            big[n] = [t[None] for t in outs4]
        updated.append(outs4[3])
    c_all = c_all + in_chips[4][0:1, 0:1]
    smalls = [(b_ada, m_b_ada, v_b_ada), (norm1_g, m_norm1_g, v_norm1_g), (norm2_g, m_norm2_g, v_norm2_g),
              (gf, m_final_g.reshape(1, D_MODEL), v_final_g.reshape(1, D_MODEL)),
              (att_onorm_g, m_att_onorm_g, v_att_onorm_g), (hg_onorm_g, m_hg_onorm_g, v_hg_onorm_g),
              (hg_lb_logits, m_hg_lb_logits, v_hg_lb_logits)]
    dmod_blocks = small_all[:, :6 * D_MODEL].reshape(N_DEV, N_DEV, 6 * D_MODEL // N_DEV).transpose(1, 0, 2)
    res = _small_update(small_all, dmod_blocks, c_all, hg_lb_logits, w_ada[0], m_w_ada[0], v_w_ada[0], smalls)
    _, recv_in = _copies_wait("reduce_chips_in_wait", _plan_reduce_chips, in_chips[0], in_chips[1], in_chips[2], in_chips[3],
                              [res[0]] + updated)
    big["w_in"] = [t.T[None] for t in _adamw_shard(chip, w_in[0].T, m_w_in[0].T, v_w_in[0].T, in_s32, recv_in[0], "adamw_w_in")]
    loss = res[0][0, 0]
    ada4 = [t[None] for t in res[1:5]]
    sm4 = {n: list(res[5 + 4 * i:9 + 4 * i]) for i, n in enumerate(["b_ada", "norm1_g", "norm2_g", "final_g", "att", "hg", "lb"])}
    sm4["final_g"] = [t.reshape(D_MODEL) for t in sm4["final_g"]]

    order = [ada4, sm4["b_ada"], sm4["norm1_g"], big["w_in"], sm4["lb"], sm4["hg"], sm4["att"], big["w_out"], sm4["norm2_g"],
             big["w_gate_up"], big["w_down"], sm4["final_g"]]
    return (loss, grad_x[None], *[o[0] for o in order], *[o[1] for o in order], *[o[2] for o in order], *[o[3] for o in order])


def _block_step(x2d, target, mod, norm1_g, hg_lb_logits, hg_onorm_g, att_onorm_g, norm2_g, gf, w_in_b, rest_weights, stage):
    h1, hq, hf, hi, hgt, aq, ak, av = _in_fwd(x2d, mod, norm1_g, w_in_b)
    hg_out, hg_o, hg_states = _hg_fwd(hq, hf, hi, hgt, hg_lb_logits, hg_onorm_g)
    branch = [_att_fwd(aq, ak, av, DILATIONS[0])]
    att_g = att_onorm_g + stage("attention_begun", branch[0][0])
    branch += [_att_fwd(aq, ak, av, d) for d in DILATIONS[1:]]
    outs = [b[0] for b in branch]
    lses = [b[1] for b in branch]
    att, att_out = _att_combine(outs, lses, att_g)
    w_out_b, w_gu_b, w_down_b = rest_weights(att_out)
    x1 = _out_fwd(x2d, hg_out, att_out, mod, w_out_b)

    dx1, h2, act, dau, dff, ffn_sums, loss_part = _ffn(x1, target, mod, norm2_g, gf, w_gu_b, w_down_b)
    dw_gu = _weight_grad(dau, h2, "dw_gate_up")
    dw_down = _weight_grad(act, dff, "dw_down")

    dhg, dat, dw_out, dgate1 = _out_bwd(dx1, hg_out, att_out, mod, w_out_b)
    att_g = att_onorm_g + stage("mixer_weights_done", dw_gu, dw_down, dw_out)
    comb = _att_combine_bwd(dat, att, lses, att_g)
    dos, ccs, d_att_g = comb[0:3], comb[3:6], comb[6]
    hg_g = hg_onorm_g + stage("attention_backward_begun", comb[3])
    datt = []
    for i, d in enumerate(DILATIONS):
        datt.append(_att_bwd(aq, ak, av, dos[i], ccs[i], lses[i], d))
    dhq, dhf, dhi, dhgt, d_hg_g, d_lb = _hg_bwd(hq, hf, hi, hgt, hg_lb_logits, hg_g, hg_o, hg_states, dhg)
    dps = [dhq, dhf, dhi, dhgt] + [datt[i][j] for j in range(3) for i in range(3)]
    grad_x, dp_b, dshift1, dscale1, d_g1 = _in_bwd(x2d, dx1, mod, norm1_g, w_in_b, dps)
    dw_in = _weight_grad(dp_b, h1, "dw_in")
    small = jnp.concatenate([dshift1, dscale1, dgate1, ffn_sums[0:1], ffn_sums[1:2], ffn_sums[2:3], d_g1, ffn_sums[3:4],
                             ffn_sums[4:5], d_att_g, d_lb, d_hg_g, loss_part], axis=1)
    return grad_x, dw_in, small
```

```python
import functools

import jax
import jax.numpy as jnp
from jax import lax
from jax.experimental import pallas as pl
from jax.experimental.pallas import tpu as pltpu

F32 = jnp.float32
BF16 = jnp.bfloat16
HIGHEST = lax.Precision.HIGHEST
MESH = pl.DeviceIdType.MESH

D_MODEL = 1024
N_DEV = 8
HG_HEADS = 4
HG_DIM = 128
HG_WIDTH = HG_HEADS * HG_DIM
HG_CHUNK = 128
ATT_WIDTH = 512
ATT_HEAD_DIM = 64
ATT_BLOCK = 128
DILATIONS = (1, 4, 16)
ATT_SCALE = ATT_HEAD_DIM ** -0.5
D_FF = 2816
IN_WIDTH = 7 * 512
RMS_EPS = 1e-6
NEG = -1e30

ADAM_LR = 0.001
ADAM_B1 = 0.9
ADAM_B2 = 0.999
ADAM_EPS = 1e-08
ADAM_WD = 0.01
ADAM_STEP = 10

V7X_VMEM_LIMIT = 56 * 1024 * 1024

SM_MOD = 0
SM_G1 = 6 * D_MODEL
SM_G2 = 7 * D_MODEL
SM_GF = 8 * D_MODEL
SM_ATT = 9 * D_MODEL
SM_LB = 9 * D_MODEL + 512
SM_HG = 10 * D_MODEL
SM_LOSS = 10 * D_MODEL + 128
SM_WIDTH = 10 * D_MODEL + 256
SM_PADDED = 88 * 128


def _params(*sem, vmem=V7X_VMEM_LIMIT):
    return pltpu.CompilerParams(dimension_semantics=sem, vmem_limit_bytes=vmem)


def _dot(a, b):
    return jnp.dot(a, b, preferred_element_type=F32)


def _dot_nt(a, b):
    return lax.dot_general(a, b, (((1,), (1,)), ((), ())), preferred_element_type=F32)


def _dot_tn(a, b):
    return lax.dot_general(a, b, (((0,), (0,)), ((), ())), preferred_element_type=F32)


def _dot_f32(a, b):
    return jnp.dot(a, b, preferred_element_type=F32, precision=HIGHEST)


def _sigmoid(x):
    return 1.0 / (1.0 + jnp.exp(-x))


def _silu(x):
    return x * _sigmoid(x)


def _dsilu(x):
    s = _sigmoid(x)
    return s * (1.0 + x * (1.0 - s))


def _rms(x):
    rstd = lax.rsqrt(jnp.mean(x * x, axis=-1, keepdims=True) + RMS_EPS)
    return x * rstd, rstd


def _rms_bwd(dn, xhat, rstd):
    return rstd * (dn - xhat * jnp.mean(dn * xhat, axis=-1, keepdims=True))


def _rowsum(x):
    return jnp.sum(x, axis=0, keepdims=True)


def _rows(tm, n):
    return pl.BlockSpec((tm, n), lambda i: (i, 0))


def _whole(shape):
    return pl.BlockSpec(shape, lambda i: (0,) * len(shape))


def _mesh_pos():
    return lax.axis_index("x"), lax.axis_index("y"), lax.axis_index("c")


def _flip(k):
    x, y, c = _mesh_pos()
    px = 1 - x if k & 4 else x
    py = 1 - y if k & 2 else y
    pc = 1 - c if k & 1 else c
    return (px, py, pc), 4 * px + 2 * py + pc


def _exchange_small(x, rows_per_peer, name):
    r_all, cols = x.shape
    r_out = r_all if rows_per_peer is None else rows_per_peer

    def body(x_ref, out_ref, send_sems, recv_sems):
        _, me = _flip(0)

        def src(pid):
            if rows_per_peer is None:
                return x_ref
            return x_ref.at[pl.ds(pl.multiple_of(pid * r_out, r_out), r_out), :]

        if rows_per_peer is None:
            out_ref[me] = x_ref[...]
        else:
            out_ref[me] = x_ref[pl.ds(pl.multiple_of(me * r_out, r_out), r_out), :]
        sends = []
        for k in range(1, N_DEV):
            dev, pid = _flip(k)
            cp = pltpu.make_async_remote_copy(src_ref=src(pid), dst_ref=out_ref.at[me], send_sem=send_sems.at[k - 1],
                                              recv_sem=recv_sems.at[k - 1], device_id=dev, device_id_type=MESH)
            cp.start()
            sends.append(cp)
        for k in range(1, N_DEV):
            dev, pid = _flip(k)
            pltpu.make_async_remote_copy(src_ref=src(pid), dst_ref=out_ref.at[pid], send_sem=send_sems.at[k - 1],
                                         recv_sem=recv_sems.at[k - 1], device_id=dev, device_id_type=MESH).wait_recv()
        for cp in sends:
            cp.wait_send()

    return pl.pallas_call(
        body, name=name,
        out_shape=jax.ShapeDtypeStruct((N_DEV, r_out, cols), x.dtype),
        in_specs=[pl.BlockSpec(memory_space=pltpu.VMEM)],
        out_specs=pl.BlockSpec(memory_space=pltpu.VMEM),
        scratch_shapes=[pltpu.SemaphoreType.DMA((N_DEV - 1,)), pltpu.SemaphoreType.DMA((N_DEV - 1,))],
    )(x)


def _gather_weights(shards):
    n = len(shards)

    def body(*refs):
        xs, outs = refs[:n], refs[n:2 * n]
        send_sems, recv_sems, local_sems = refs[2 * n:]
        x, y, c = _mesh_pos()
        me, sibling = (x, y, c), (x, y, 1 - c)
        chips = [(1 - x, y), (x, 1 - y), (1 - x, 1 - y)]

        def blk(a, px, py, pc):
            return outs[a].at[4 * px + 2 * py + pc]

        def copy(a, k, block, to, src=None):
            return pltpu.make_async_remote_copy(
                src_ref=blk(a, *block) if src is None else src, dst_ref=blk(a, *block),
                send_sem=send_sems.at[a * 7 + k], recv_sem=recv_sems.at[a * 7 + k], device_id=to, device_id_type=MESH)

        mine = [pltpu.make_async_copy(xs[a], blk(a, *me), local_sems.at[a]) for a in range(n)]
        for cp in mine:
            cp.start()
        first = []
        for a in range(n):
            first.append(copy(a, 0, me, sibling, src=xs[a]))
            first += [copy(a, 1 + j, me, (*chip, c), src=xs[a]) for j, chip in enumerate(chips)]
        for cp in first:
            cp.start()
        passed = []
        for j, chip in enumerate(chips):
            for a in range(n):
                copy(a, 1 + j, (*chip, c), me).wait_recv()
                cp = copy(a, 4 + j, (*chip, c), sibling)
                cp.start()
                passed.append(cp)
        for a in range(n):
            copy(a, 0, sibling, me).wait_recv()
            for j, chip in enumerate(chips):
                copy(a, 4 + j, (*chip, 1 - c), me).wait_recv()
        for cp in first + passed:
            cp.wait_send()
        for cp in mine:
            cp.wait()

    hbm = pl.BlockSpec(memory_space=pl.ANY)
    return pl.pallas_call(
        body, name="gather_weights",
        out_shape=[jax.ShapeDtypeStruct((N_DEV,) + s.shape, s.dtype) for s in shards],
        in_specs=[hbm] * n, out_specs=[hbm] * n,
        scratch_shapes=[pltpu.SemaphoreType.DMA((7 * n,)), pltpu.SemaphoreType.DMA((7 * n,)), pltpu.SemaphoreType.DMA((n,))],
    )(*shards)


_HBM = pl.BlockSpec(memory_space=pltpu.HBM)
_SEM = pl.BlockSpec(memory_space=pltpu.SEMAPHORE)
_DATAFLOW = pltpu.SideEffectType.DATAFLOW_SIDE_EFFECTING


def _copies_start(name, plan, n_copies, srcs, lands, after):
    bufs = list(srcs) + list(lands)
    nb = len(bufs)

    def body(*refs):
        ins, send_sems, recv_sems, token = refs[:nb], refs[nb + len(after)], refs[nb + len(after) + 1], refs[-1]
        for i, (src, dst, dev) in enumerate(plan(ins[:len(srcs)], ins[len(srcs):])):
            pltpu.make_async_remote_copy(src_ref=src, dst_ref=dst, send_sem=send_sems.at[i], recv_sem=recv_sems.at[i],
                                         device_id=dev, device_id_type=MESH).start()
        token[...] = jnp.zeros_like(token)

    outs = pl.pallas_call(
        body, name=name,
        out_shape=(pltpu.SemaphoreType.DMA((n_copies,)), pltpu.SemaphoreType.DMA((n_copies,)),
                   *[pltpu.HBM(b.shape, b.dtype) for b in bufs], jax.ShapeDtypeStruct((8, 128), F32)),
        in_specs=[_HBM] * nb + [pl.BlockSpec(memory_space=pl.ANY)] * len(after),
        out_specs=(_SEM, _SEM, *[_HBM] * nb, pl.BlockSpec(memory_space=pltpu.VMEM)),
        input_output_aliases={i: 2 + i for i in range(nb)},
        compiler_params=pltpu.CompilerParams(has_side_effects=_DATAFLOW),
    )(*[pltpu.with_memory_space_constraint(b, pltpu.HBM) for b in bufs], *after)
    return outs[0], outs[1], list(outs[2:2 + len(srcs)]), list(outs[2 + len(srcs):2 + nb]), outs[-1]


def _copies_wait(name, plan, send_sems, recv_sems, srcs, lands, after):
    bufs = list(srcs) + list(lands)
    nb = len(bufs)

    def body(*refs):
        ins, send_ref, recv_ref = refs[:nb], refs[nb], refs[nb + 1]
        for i, (src, dst, dev) in enumerate(plan(ins[:len(srcs)], ins[len(srcs):])):
            cp = pltpu.make_async_remote_copy(src_ref=src, dst_ref=dst, send_sem=send_ref.at[i], recv_sem=recv_ref.at[i],
                                              device_id=dev, device_id_type=MESH)
            cp.wait_send()
            cp.wait_recv()

    outs = pl.pallas_call(
        body, name=name, out_shape=[pltpu.HBM(b.shape, b.dtype) for b in bufs],
        in_specs=[_HBM] * nb + [_SEM, _SEM] + [pl.BlockSpec(memory_space=pl.ANY)] * len(after), out_specs=[_HBM] * nb,
        input_output_aliases={i: i for i in range(nb)},
        compiler_params=pltpu.CompilerParams(has_side_effects=_DATAFLOW),
    )(*bufs, send_sems, recv_sems, *after)
    return list(outs[:len(srcs)]), list(outs[len(srcs):])


def _plan_gather_own(srcs, lands):
    _, me = _flip(0)
    return [(srcs[a], lands[a].at[me], _flip(k)[0]) for a in range(len(srcs)) for k in (1, 4, 2, 6)]


def _plan_gather_pass(srcs, lands):
    sibling = _flip(1)[0]
    plan = []
    for land in lands:
        for k in (4, 2, 6):
            block = land.at[_flip(k)[1]]
            plan.append((block, block, sibling))
    return plan


def _plan_reduce_pairs(srcs, lands):
    x, y, c = _mesh_pos()
    return [(srcs[a].at[chip, 1 - c], lands[a].at[chip], (x, y, 1 - c)) for a in range(len(srcs)) for chip in range(4)]


def _plan_reduce_chips(srcs, lands):
    plan = []
    for a in range(len(srcs)):
        for j, k in enumerate((4, 2, 6)):
            dev = _flip(k)[0]
            plan.append((srcs[a].at[2 * dev[0] + dev[1]], lands[a].at[j], dev))
    return plan


def _shard_rows(r):
    return r // 2 if r % 32 == 0 else r


def _pair_sum(core, grads, got, name):
    _, _, r, c = grads.shape
    tr = _shard_rows(r)

    def body(core_ref, a_ref, b_ref, o_ref, ob_ref):
        s = a_ref[...] + b_ref[...]
        o_ref[...] = s
        ob_ref[...] = s.astype(BF16)

    spec = pl.BlockSpec((None, tr, c), lambda i, j, core_ref: (i, j, 0))
    return pl.pallas_call(
        body, name=name,
        grid_spec=pltpu.PrefetchScalarGridSpec(
            num_scalar_prefetch=1, grid=(4, r // tr),
            in_specs=[pl.BlockSpec((None, None, tr, c), lambda i, j, core_ref: (i, core_ref[0], j, 0)), spec],
            out_specs=[spec, spec]),
        out_shape=[jax.ShapeDtypeStruct((4, r, c), F32), jax.ShapeDtypeStruct((4, r, c), BF16)],
        compiler_params=_params("parallel", "parallel"),
    )(core, grads, got)


def _ada_rows(c_all, w_ada, b_ada):
    n_cols = w_ada.shape[1]

    def body(c_ref, w_ref, b_ref, o_ref):
        _, me = _flip(0)
        bias = b_ref[:, pl.ds(pl.multiple_of(me * n_cols, 128), n_cols)]
        o_ref[...] = _dot_f32(_silu(c_ref[...]), w_ref[...]) + bias

    return pl.pallas_call(
        body, name="ada_rows", out_shape=jax.ShapeDtypeStruct((N_DEV, n_cols), F32),
        in_specs=[pl.BlockSpec(memory_space=pltpu.VMEM)] * 3, out_specs=pl.BlockSpec(memory_space=pltpu.VMEM),
    )(c_all, w_ada, b_ada)


def _in_fwd(x, mod, g1, w_in):
    s = x.shape[0]
    tm = 256

    def body(x_ref, mod_ref, g_ref, w_ref, h_ref, *outs):
        xhat, _ = _rms(x_ref[...])
        h = (xhat * g_ref[...]) * (1.0 + mod_ref[:, D_MODEL:2 * D_MODEL]) + mod_ref[:, 0:D_MODEL]
        hb = h.astype(BF16)
        h_ref[...] = hb
        for j, o_ref in enumerate(outs):
            o_ref[...] = _dot_nt(hb, w_ref[j * 512:(j + 1) * 512, :])

    return pl.pallas_call(
        body, name="in_fwd", grid=(s // tm,),
        out_shape=[jax.ShapeDtypeStruct((s, D_MODEL), BF16)] + [jax.ShapeDtypeStruct((s, 512), F32)] * 7,
        in_specs=[_rows(tm, D_MODEL), _whole((1, 6 * D_MODEL)), _whole((1, D_MODEL)), _whole((IN_WIDTH, D_MODEL))],
        out_specs=[_rows(tm, D_MODEL)] + [_rows(tm, 512)] * 7,
        compiler_params=_params("parallel"),
    )(x, mod, g1, w_in)


def _in_bwd(x, dx1, mod, g1, w_in, dps):
    s = x.shape[0]
    tm = 256

    def body(x_ref, dx_ref, mod_ref, g_ref, w_ref, *rest):
        dp_refs, (gx_ref, dpb_ref, dsh_ref, dsc_ref, dg_ref) = rest[:13], rest[13:]
        pieces = [dp_refs[j][...] for j in range(4)]
        pieces += [dp_refs[4 + 3 * j][...] + dp_refs[5 + 3 * j][...] + dp_refs[6 + 3 * j][...] for j in range(3)]
        for j, p in enumerate(pieces):
            dpb_ref[:, j * 512:(j + 1) * 512] = p.astype(BF16)
        dh = _dot(dpb_ref[...], w_ref[...])
        xhat, rstd = _rms(x_ref[...])
        g = g_ref[...]
        scale1 = 1.0 + mod_ref[:, D_MODEL:2 * D_MODEL]
        n1 = xhat * g

        @pl.when(pl.program_id(0) == 0)
        def _():
            dsh_ref[...] = jnp.zeros_like(dsh_ref)
            dsc_ref[...] = jnp.zeros_like(dsc_ref)
            dg_ref[...] = jnp.zeros_like(dg_ref)

        dsh_ref[...] += _rowsum(dh)
        dsc_ref[...] += _rowsum(dh * n1)
        dn = dh * scale1
        dg_ref[...] += _rowsum(dn * xhat)
        gx_ref[...] = dx_ref[...] + _rms_bwd(dn * g, xhat, rstd)

    vec = _whole((1, D_MODEL))
    return pl.pallas_call(
        body, name="in_bwd", grid=(s // tm,),
        out_shape=[jax.ShapeDtypeStruct((s, D_MODEL), F32), jax.ShapeDtypeStruct((s, IN_WIDTH), BF16)]
        + [jax.ShapeDtypeStruct((1, D_MODEL), F32)] * 3,
        in_specs=[_rows(tm, D_MODEL), _rows(tm, D_MODEL), _whole((1, 6 * D_MODEL)), vec, _whole((IN_WIDTH, D_MODEL))]
        + [_rows(tm, 512)] * 13,
        out_specs=[_rows(tm, D_MODEL), _rows(tm, IN_WIDTH), vec, vec, vec],
        compiler_params=_params("arbitrary"),
    )(x, dx1, mod, g1, w_in, *dps)


HG_TILE = 512
HG_TILE_CHUNKS = HG_TILE // HG_CHUNK


def _lower_bound(lg_ref):
    return 1.0 / (1.0 + jnp.exp(lg_ref[1:2, :] - lg_ref[0:1, :]))


def _chunk_masks():
    r = lax.broadcasted_iota(jnp.int32, (HG_CHUNK, HG_CHUNK), 0)
    c = lax.broadcasted_iota(jnp.int32, (HG_CHUNK, HG_CHUNK), 1)
    return r >= c, c >= r, (r >= c).astype(F32), (c >= r).astype(F32)


def _hg_fwd(hq, hf, hi, hgt, logits, onorm_g):
    s = hq.shape[0]
    n_tiles = s // HG_TILE

    def body(q_ref, f_ref, i_ref, g_ref, lg_ref, og_ref, out_ref, o_ref, st_ref, state, qf_s, kk_s, lf_s):
        @pl.when(pl.program_id(0) == 0)
        def _():
            state[...] = jnp.zeros_like(state)

        lb = _lower_bound(lg_ref)
        f = lb + (1.0 - lb) * _sigmoid(f_ref[...])
        kk_s[...] = 1.0 - f
        lf_s[...] = jnp.log(f)
        qf_s[...] = _silu(q_ref[...])
        causal, _, tri, _ = _chunk_masks()

        def chunk(ci, carry):
            rows = pl.ds(pl.multiple_of(ci * HG_CHUNK, HG_CHUNK), HG_CHUNK)
            srows = pl.ds(pl.multiple_of(ci * HG_DIM, HG_DIM), HG_DIM)
            lf = lf_s[rows, :]
            b = _dot_f32(tri, lf)
            bl = _rowsum(lf)
            ref = 0.5 * bl
            qf, kk, v = qf_s[rows, :], kk_s[rows, :], i_ref[rows, :]
            a_in = (qf * jnp.exp(b)).astype(BF16)
            a_t = (qf * jnp.exp(b - ref)).astype(BF16)
            b_t = (kk * jnp.exp(ref - b)).astype(BF16)
            kd = kk * jnp.exp(bl - b)
            ebl = jnp.exp(bl)
            vb = v.astype(BF16)
            for h in range(HG_HEADS):
                c = slice(h * HG_DIM, (h + 1) * HG_DIM)
                st = state[h]
                st_ref[srows, c] = st
                p = jnp.where(causal, _dot_nt(a_t[:, c], b_t[:, c]), 0.0)
                o_ref[rows, c] = _dot(p.astype(BF16), vb[:, c]) + _dot_nt(a_in[:, c], st.astype(BF16))
                state[h] = st * ebl[:, c] + _dot_tn(vb[:, c], kd[:, c].astype(BF16))
            return carry

        lax.fori_loop(0, HG_TILE_CHUNKS, chunk, 0, unroll=2)
        for h in range(HG_HEADS):
            c = slice(h * HG_DIM, (h + 1) * HG_DIM)
            ohat, _ = _rms(o_ref[:, c])
            out_ref[:, c] = (ohat * og_ref[...] * _silu(g_ref[:, c])).astype(BF16)

    tile = _rows(HG_TILE, HG_WIDTH)
    return pl.pallas_call(
        body, name="hg_fwd", grid=(n_tiles,),
        out_shape=[jax.ShapeDtypeStruct((s, HG_WIDTH), BF16), jax.ShapeDtypeStruct((s, HG_WIDTH), F32),
                   jax.ShapeDtypeStruct((s // HG_CHUNK * HG_DIM, HG_WIDTH), F32)],
        in_specs=[tile] * 4 + [_whole((2, HG_WIDTH)), _whole((1, HG_DIM))],
        out_specs=[tile, tile, _rows(HG_TILE_CHUNKS * HG_DIM, HG_WIDTH)],
        scratch_shapes=[pltpu.VMEM((HG_HEADS, HG_DIM, HG_DIM), F32)] + [pltpu.VMEM((HG_TILE, HG_WIDTH), F32)] * 3,
        compiler_params=_params("arbitrary"),
    )(hq, hf, hi, hgt, logits, onorm_g)


def _hg_bwd(hq, hf, hi, hgt, logits, onorm_g, o, states, dout):
    s = hq.shape[0]
    n_tiles = s // HG_TILE

    def body(q_ref, f_ref, i_ref, g_ref, lg_ref, og_ref, o_ref, st_ref, d_ref,
             dq_ref, df_ref, di_ref, dg_ref, dog_ref, dlb_ref, dstate, qf_s, kk_s, lf_s, do_s):
        @pl.when(pl.program_id(0) == 0)
        def _():
            dstate[...] = jnp.zeros_like(dstate)
            dog_ref[...] = jnp.zeros_like(dog_ref)
            dlb_ref[...] = jnp.zeros_like(dlb_ref)

        og = og_ref[...]
        dog = jnp.zeros((1, HG_DIM), F32)
        for h in range(HG_HEADS):
            c = slice(h * HG_DIM, (h + 1) * HG_DIM)
            ohat, rstd = _rms(o_ref[:, c])
            gate = g_ref[:, c]
            d = d_ref[:, c]
            dg_ref[:, c] = (d * (ohat * og) * _dsilu(gate)).astype(BF16)
            dnormed = d * _silu(gate)
            dog += _rowsum(dnormed * ohat)
            do_s[:, c] = _rms_bwd(dnormed * og, ohat, rstd)
        dog_ref[...] += dog

        lb = _lower_bound(lg_ref)
        f = lb + (1.0 - lb) * _sigmoid(f_ref[...])
        kk_s[...] = 1.0 - f
        lf_s[...] = jnp.log(f)
        qf_s[...] = _silu(q_ref[...])
        causal, upper, tri, tri_t = _chunk_masks()

        def chunk(step, carry):
            ci = HG_TILE_CHUNKS - 1 - step
            rows = pl.ds(pl.multiple_of(ci * HG_CHUNK, HG_CHUNK), HG_CHUNK)
            srows = pl.ds(pl.multiple_of(ci * HG_DIM, HG_DIM), HG_DIM)
            lf = lf_s[rows, :]
            b = _dot_f32(tri, lf)
            bl = _rowsum(lf)
            ref = 0.5 * bl
            qf, kk, v, do = qf_s[rows, :], kk_s[rows, :], i_ref[rows, :], do_s[rows, :]
            eb, ebr, erb, ekd, ebl = jnp.exp(b), jnp.exp(b - ref), jnp.exp(ref - b), jnp.exp(bl - b), jnp.exp(bl)
            a_in, a_t, b_t, kd = qf * eb, qf * ebr, kk * erb, kk * ekd
            for h in range(HG_HEADS):
                c = slice(h * HG_DIM, (h + 1) * HG_DIM)
                st, dst = st_ref[srows, c], dstate[h]
                stb, dstb = st.astype(BF16), dst.astype(BF16)
                doh, vh = do[:, c], v[:, c]
                dob, vb = doh.astype(BF16), vh.astype(BF16)
                ain_h, at_h, bt_h, kd_h = a_in[:, c], a_t[:, c], b_t[:, c], kd[:, c]
                atb, btb = at_h.astype(BF16), bt_h.astype(BF16)
                d_ain = _dot(dob, stb)
                p_t = jnp.where(upper, _dot_nt(btb, atb), 0.0).astype(BF16)
                dp = jnp.where(causal, _dot_nt(dob, vb), 0.0).astype(BF16)
                dp_t = jnp.where(upper, _dot_nt(vb, dob), 0.0).astype(BF16)
                di_ref[rows, c] = (_dot(p_t, dob) + _dot_nt(kd_h.astype(BF16), dstb)).astype(BF16)
                d_at = _dot(dp, btb)
                d_bt = _dot(dp_t, atb)
                d_kd = _dot(vb, dstb)
                dqf = d_ain * eb[:, c] + d_at * ebr[:, c]
                dkk = d_bt * erb[:, c] + d_kd * ekd[:, c]
                db = d_ain * ain_h + d_at * atb.astype(F32) - d_bt * btb.astype(F32) - d_kd * kd_h
                dbl = _rowsum(d_kd * kd_h) + _rowsum(dst * st) * ebl[:, c]
                dstate[h] = _dot_tn(dob, ain_h.astype(BF16)) + dst * ebl[:, c]
                dlf = _dot_f32(tri_t, db) + dbl
                qv, fr = q_ref[rows, c], f_ref[rows, c]
                lbh = lb[:, c]
                sg = _sigmoid(fr)
                dfv = dlf / (lbh + (1.0 - lbh) * sg) - dkk
                df_ref[rows, c] = (dfv * (1.0 - lbh) * sg * (1.0 - sg)).astype(BF16)
                dlb_ref[:, c] += _rowsum(dfv * (1.0 - sg))
                dq_ref[rows, c] = (dqf * _dsilu(qv)).astype(BF16)
            return carry

        lax.fori_loop(0, HG_TILE_CHUNKS, chunk, 0, unroll=2)

    rev = pl.BlockSpec((HG_TILE, HG_WIDTH), lambda i: (n_tiles - 1 - i, 0))
    return pl.pallas_call(
        body, name="hg_bwd", grid=(n_tiles,),
        out_shape=[jax.ShapeDtypeStruct((s, HG_WIDTH), BF16)] * 4
        + [jax.ShapeDtypeStruct((1, HG_DIM), F32), jax.ShapeDtypeStruct((1, HG_WIDTH), F32)],
        in_specs=[rev] * 4 + [_whole((2, HG_WIDTH)), _whole((1, HG_DIM)), rev,
                              pl.BlockSpec((HG_TILE_CHUNKS * HG_DIM, HG_WIDTH), lambda i: (n_tiles - 1 - i, 0)), rev],
        out_specs=[rev] * 4 + [_whole((1, HG_DIM)), _whole((1, HG_WIDTH))],
        scratch_shapes=[pltpu.VMEM((HG_HEADS, HG_DIM, HG_DIM), F32)] + [pltpu.VMEM((HG_TILE, HG_WIDTH), F32)] * 4,
        compiler_params=_params("arbitrary"),
    )(hq, hf, hi, hgt, logits, onorm_g, o, states, dout)


TOKEN_GROUP = 16


def _att_geometry(dil):
    per_group = TOKEN_GROUP // dil
    return per_group, ATT_BLOCK // per_group, ATT_WIDTH if dil == 1 else 128, 1 if dil == TOKEN_GROUP else 4


def _att_consts(dil):
    per_group, ub = _att_geometry(dil)[:2]

    def pos(i):
        return i if dil == 1 else (i % ub) * per_group + i // ub

    lane = lax.broadcasted_iota(jnp.int32, (ATT_BLOCK, 128), 1)
    qi = pos(lax.broadcasted_iota(jnp.int32, (2 * ATT_BLOCK, ATT_BLOCK), 0) % ATT_BLOCK)
    kj = pos(lax.broadcasted_iota(jnp.int32, (2 * ATT_BLOCK, ATT_BLOCK), 1))
    return lane < ATT_HEAD_DIM, kj <= qi, lambda off: kj >= qi + off


def _load_tile(ref, dil, r, c, base=0):
    per_group, ub = _att_geometry(dil)[:2]
    if dil == 1:
        return ref[base:base + ATT_BLOCK, c]
    return jnp.concatenate([ref[pl.ds(base + dil * w + r, ub, stride=TOKEN_GROUP), c] for w in range(per_group)], axis=0)


def _store_tile(ref, dil, r, c, val, base=0):
    per_group, ub = _att_geometry(dil)[:2]
    if dil == 1:
        ref[base:base + ATT_BLOCK, c] = val
        return
    for w in range(per_group):
        ref[pl.ds(base + dil * w + r, ub, stride=TOKEN_GROUP), c] = val[w * ub:(w + 1) * ub]


def _stack_heads(x2, first):
    return jnp.concatenate([jnp.where(first, x2, 0.0), jnp.where(first, 0.0, x2)], axis=0)


def _stack_bcast(x2, first):
    other = pltpu.roll(x2, ATT_HEAD_DIM, axis=1)
    return jnp.concatenate([jnp.where(first, x2, other), jnp.where(first, other, x2)], axis=0)


def _unstack_heads(st, first):
    return jnp.where(first, st[:ATT_BLOCK], st[ATT_BLOCK:])


def _att_fwd(q, k, v, dil):
    seq, width = q.shape
    _, ub, lanes, nbs = _att_geometry(dil)
    rows = ub * TOKEN_GROUP
    n_steps = seq // (nbs * rows)

    def body(q_ref, k_ref, v_ref, kp_ref, vp_ref, o_ref, lse_ref):
        first, cur_ok, _band = _att_consts(dil)
        inner_ok = _band(0)
        edge_ok = _band(jnp.where(pl.program_id(0) > 0, 0, ATT_BLOCK))
        for b in range(nbs):
            base = b * rows
            prev_ok = edge_ok if b == 0 else inner_ok
            for r in range(dil):
                for j in range(lanes // 128):
                    c = slice(j * 128, (j + 1) * 128)
                    qst = _stack_heads(_load_tile(q_ref, dil, r, c, base) * ATT_SCALE, first).astype(BF16)
                    kc = _load_tile(k_ref, dil, r, c, base).astype(BF16)
                    vc = _load_tile(v_ref, dil, r, c, base).astype(BF16)
                    if b == 0:
                        kp, vp = _load_tile(kp_ref, dil, r, c).astype(BF16), _load_tile(vp_ref, dil, r, c).astype(BF16)
                    else:
                        kp = _load_tile(k_ref, dil, r, c, base - rows).astype(BF16)
                        vp = _load_tile(v_ref, dil, r, c, base - rows).astype(BF16)
                    sc = jnp.where(cur_ok, _dot_nt(qst, kc), NEG)
                    sp = jnp.where(prev_ok, _dot_nt(qst, kp), NEG)
                    mx = jnp.maximum(jnp.max(sc, axis=-1, keepdims=True), jnp.max(sp, axis=-1, keepdims=True))
                    pc, pp = jnp.exp(sc - mx), jnp.exp(sp - mx)
                    den = jnp.sum(pc, axis=-1, keepdims=True) + jnp.sum(pp, axis=-1, keepdims=True)
                    ost = (_dot(pc.astype(BF16), vc) + _dot(pp.astype(BF16), vp)) / den
                    lse = jnp.broadcast_to(mx + jnp.log(den), (2 * ATT_BLOCK, 128))
                    _store_tile(o_ref, dil, r, c, _unstack_heads(ost, first), base)
                    _store_tile(lse_ref, dil, r, c, _unstack_heads(lse, first), base)

    slab = pl.BlockSpec((nbs * rows, lanes), lambda n, j: (n, j))
    before = pl.BlockSpec((rows, lanes), lambda n, j: (jnp.maximum(n * nbs - 1, 0), j))
    return pl.pallas_call(
        body, name=f"att_fwd_d{dil}", grid=(n_steps, width // lanes),
        out_shape=[jax.ShapeDtypeStruct((seq, width), F32)] * 2,
        in_specs=[slab, slab, slab, before, before], out_specs=[slab, slab],
        compiler_params=_params("arbitrary", "arbitrary"),
    )(q, k, v, k, v)


def _att_bwd(q, k, v, do, cc, lse, dil):
    seq, width = q.shape
    _, ub, lanes, nbs = _att_geometry(dil)
    rows = ub * TOKEN_GROUP
    n_blocks = seq // rows
    n_steps = n_blocks // nbs

    def body(q_ref, k_ref, v_ref, do_ref, cc_ref, lse_ref, qx_ref, dox_ref, ccx_ref, lsex_ref,
             dq_ref, dk_ref, dv_ref, carry):
        first, cur_ok, _band = _att_consts(dil)
        step = pl.program_id(1)
        inner_ok = _band(0)
        edge_ok = _band(jnp.where(step < n_steps - 1, 0, ATT_BLOCK))

        @pl.when(step == 0)
        def _():
            carry[...] = jnp.zeros_like(carry)

        for b in range(nbs):
            base = b * rows
            last = b == nbs - 1
            next_ok = edge_ok if last else inner_ok
            for r in range(dil):
                for j in range(lanes // 128):
                    c = slice(j * 128, (j + 1) * 128)

                    def following(inner_ref, edge_ref):
                        return _load_tile(edge_ref, dil, r, c) if last else _load_tile(inner_ref, dil, r, c, base + rows)

                    qst = _stack_heads(_load_tile(q_ref, dil, r, c, base) * ATT_SCALE, first).astype(BF16)
                    qxst = _stack_heads(following(q_ref, qx_ref) * ATT_SCALE, first).astype(BF16)
                    dost = _stack_heads(_load_tile(do_ref, dil, r, c, base), first).astype(BF16)
                    doxst = _stack_heads(following(do_ref, dox_ref), first).astype(BF16)
                    lse_n = _stack_bcast(_load_tile(lse_ref, dil, r, c, base), first)
                    lse_x = _stack_bcast(following(lse_ref, lsex_ref), first)
                    cc_n = _stack_bcast(_load_tile(cc_ref, dil, r, c, base), first)
                    cc_x = _stack_bcast(following(cc_ref, ccx_ref), first)
                    kb = _load_tile(k_ref, dil, r, c, base).astype(BF16)
                    vb = _load_tile(v_ref, dil, r, c, base).astype(BF16)
                    p_cur = jnp.exp(jnp.where(cur_ok, _dot_nt(qst, kb), NEG) - lse_n)
                    p_next = jnp.exp(jnp.where(next_ok, _dot_nt(qxst, kb), NEG) - lse_x)
                    ds_cur = (p_cur * (_dot_nt(dost, vb) + cc_n)).astype(BF16)
                    ds_next = (p_next * (_dot_nt(doxst, vb) + cc_x)).astype(BF16)
                    dq_own = _load_tile(carry, dil, r, c) + _unstack_heads(_dot(ds_cur, kb), first)
                    _store_tile(dq_ref, dil, r, c, dq_own * ATT_SCALE, base)
                    _store_tile(carry, dil, r, c, _unstack_heads(_dot(ds_next, kb), first))
                    _store_tile(dk_ref, dil, r, c, _dot_tn(ds_cur, qst) + _dot_tn(ds_next, qxst), base)
                    _store_tile(dv_ref, dil, r, c, _dot_tn(p_cur.astype(BF16), dost) + _dot_tn(p_next.astype(BF16), doxst), base)

    slab = pl.BlockSpec((nbs * rows, lanes), lambda j, n: (n, j))
    after = pl.BlockSpec((rows, lanes), lambda j, n: (jnp.minimum((n + 1) * nbs, n_blocks - 1), j))
    return pl.pallas_call(
        body, name=f"att_bwd_d{dil}", grid=(width // lanes, n_steps),
        out_shape=[jax.ShapeDtypeStruct((seq, width), F32)] * 3,
        in_specs=[slab] * 6 + [after] * 4, out_specs=[slab] * 3,
        scratch_shapes=[pltpu.VMEM((rows, lanes), F32)],
        compiler_params=_params("arbitrary", "arbitrary"),
    )(q, k, v, do, cc, lse, q, do, cc, lse)


def _branch_weights(lses):
    mx = jnp.maximum(jnp.maximum(lses[0], lses[1]), lses[2])
    es = [jnp.exp(l - mx) for l in lses]
    inv = 1.0 / (es[0] + es[1] + es[2])
    return [e * inv for e in es]


def _att_combine(outs, lses, att_g):
    s = outs[0].shape[0]
    tm = 512

    def body(o0, o1, o2, l0, l1, l2, g_ref, att_ref, out_ref):
        ws = _branch_weights([l0[...], l1[...], l2[...]])
        att = ws[0] * o0[...] + ws[1] * o1[...] + ws[2] * o2[...]
        att_ref[...] = att
        ahat, _ = _rms(att)
        out_ref[...] = (ahat * g_ref[...]).astype(BF16)

    tile = _rows(tm, ATT_WIDTH)
    return pl.pallas_call(
        body, name="att_combine", grid=(s // tm,),
        out_shape=[jax.ShapeDtypeStruct((s, ATT_WIDTH), F32), jax.ShapeDtypeStruct((s, ATT_WIDTH), BF16)],
        in_specs=[tile] * 6 + [_whole((1, ATT_WIDTH))], out_specs=[tile, tile],
        compiler_params=_params("parallel"),
    )(*outs, *lses, att_g)


def _att_combine_bwd(datt_out, att, lses, att_g):
    s = att.shape[0]
    tm = 256

    def body(d_ref, att_ref, l0, l1, l2, g_ref, do0, do1, do2, cc0, cc1, cc2, dg_ref):
        @pl.when(pl.program_id(0) == 0)
        def _():
            dg_ref[...] = jnp.zeros_like(dg_ref)

        att = att_ref[...]
        ahat, rstd = _rms(att)
        d = d_ref[...]
        dg_ref[...] += _rowsum(d * ahat)
        datt = _rms_bwd(d * g_ref[...], ahat, rstd)
        hi = lax.broadcasted_iota(jnp.int32, (ATT_WIDTH, ATT_WIDTH), 0) // ATT_HEAD_DIM
        hj = lax.broadcasted_iota(jnp.int32, (ATT_WIDTH, ATT_WIDTH), 1) // ATT_HEAD_DIM
        head_sum = _dot_f32(datt * att, (hi == hj).astype(F32))
        ws = _branch_weights([l0[...], l1[...], l2[...]])
        for w, do_ref, cc_ref in zip(ws, (do0, do1, do2), (cc0, cc1, cc2)):
            do_ref[...] = w * datt
            cc_ref[...] = -w * head_sum

    tile = _rows(tm, ATT_WIDTH)
    return pl.pallas_call(
        body, name="att_combine_bwd", grid=(s // tm,),
        out_shape=[jax.ShapeDtypeStruct((s, ATT_WIDTH), F32)] * 6 + [jax.ShapeDtypeStruct((1, ATT_WIDTH), F32)],
        in_specs=[tile] * 5 + [_whole((1, ATT_WIDTH))], out_specs=[tile] * 6 + [_whole((1, ATT_WIDTH))],
        compiler_params=_params("arbitrary"),
    )(datt_out, att, *lses, att_g)


def _out_fwd(x, hg, at, mod, w_out):
    s = x.shape[0]
    tm = 512

    def body(x_ref, hg_ref, at_ref, mod_ref, w_ref, x1_ref):
        mix = _dot(hg_ref[...], w_ref[0:512, :]) + _dot(at_ref[...], w_ref[512:1024, :])
        x1_ref[...] = x_ref[...] + mod_ref[:, 2 * D_MODEL:3 * D_MODEL] * mix

    return pl.pallas_call(
        body, name="out_fwd", grid=(s // tm,), out_shape=jax.ShapeDtypeStruct((s, D_MODEL), F32),
        in_specs=[_rows(tm, D_MODEL), _rows(tm, 512), _rows(tm, 512), _whole((1, 6 * D_MODEL)), _whole((D_MODEL, D_MODEL))],
        out_specs=_rows(tm, D_MODEL), compiler_params=_params("parallel"),
    )(x, hg, at, mod, w_out)


def _out_bwd(dx1, hg, at, mod, w_out):
    s = dx1.shape[0]
    tm = 512

    def body(dx_ref, hg_ref, at_ref, mod_ref, w_ref, dhg_ref, dat_ref, dw_ref, dgate_ref):
        @pl.when(pl.program_id(0) == 0)
        def _():
            dw_ref[...] = jnp.zeros_like(dw_ref)
            dgate_ref[...] = jnp.zeros_like(dgate_ref)

        hg, at, dx = hg_ref[...], at_ref[...], dx_ref[...]
        mix = _dot(hg, w_ref[0:512, :]) + _dot(at, w_ref[512:1024, :])
        dgate_ref[...] += _rowsum(dx * mix)
        dmix = (mod_ref[:, 2 * D_MODEL:3 * D_MODEL] * dx).astype(BF16)
        dhg_ref[...] = _dot_nt(dmix, w_ref[0:512, :])
        dat_ref[...] = _dot_nt(dmix, w_ref[512:1024, :])
        dw_ref[0:512, :] += _dot_tn(hg, dmix)
        dw_ref[512:1024, :] += _dot_tn(at, dmix)

    return pl.pallas_call(
        body, name="out_bwd", grid=(s // tm,),
        out_shape=[jax.ShapeDtypeStruct((s, 512), F32)] * 2
        + [jax.ShapeDtypeStruct((D_MODEL, D_MODEL), F32), jax.ShapeDtypeStruct((1, D_MODEL), F32)],
        in_specs=[_rows(tm, D_MODEL), _rows(tm, 512), _rows(tm, 512), _whole((1, 6 * D_MODEL)), _whole((D_MODEL, D_MODEL))],
        out_specs=[_rows(tm, 512), _rows(tm, 512), _whole((D_MODEL, D_MODEL)), _whole((1, D_MODEL))],
        compiler_params=_params("arbitrary"),
    )(dx1, hg, at, mod, w_out)


FFN_CHUNK = 2816


def _ffn(x1, target, mod, g2, gf, w_gu, w_down):
    s = x1.shape[0]
    tm = 256
    n_chunks = D_FF // FFN_CHUNK

    def body(x_ref, t_ref, mod_ref, g2_ref, gf_ref, wgu_hbm, wd_hbm,
             dx_ref, h2_ref, act_ref, dau_ref, dff_ref, sums_ref, loss_ref, wgu, wd, a_s, u_s, sem):
        @pl.when(pl.program_id(0) == 0)
        def _():
            c1 = pltpu.make_async_copy(wgu_hbm, wgu, sem.at[0])
            c2 = pltpu.make_async_copy(wd_hbm, wd, sem.at[1])
            c1.start()
            c2.start()
            c1.wait()
            c2.wait()
            sums_ref[...] = jnp.zeros_like(sums_ref)
            loss_ref[...] = jnp.zeros_like(loss_ref)

        x1v = x_ref[...]
        xhat, rstd = _rms(x1v)
        g2 = g2_ref[...]
        n2 = xhat * g2
        scale2 = 1.0 + mod_ref[:, 4 * D_MODEL:5 * D_MODEL]
        gate2 = mod_ref[:, 5 * D_MODEL:6 * D_MODEL]
        hb = (n2 * scale2 + mod_ref[:, 3 * D_MODEL:4 * D_MODEL]).astype(BF16)
        h2_ref[...] = hb
        ff = jnp.zeros((tm, D_MODEL), F32)
        for j in range(n_chunks):
            c = slice(j * FFN_CHUNK, (j + 1) * FFN_CHUNK)
            cu = slice(D_FF + j * FFN_CHUNK, D_FF + (j + 1) * FFN_CHUNK)
            a = _dot_nt(hb, wgu[c, :])
            u = _dot_nt(hb, wgu[cu, :])
            a_s[:, c] = a
            u_s[:, c] = u
            act = (_silu(a) * u).astype(BF16)
            act_ref[:, c] = act
            ff += _dot(act, wd[c, :])
        x2 = x1v + gate2 * ff
        nf, rstd_f = _rms(x2)
        gfv = gf_ref[...]
        err = nf * gfv - t_ref[...]
        loss_ref[...] += 0.5 * jnp.sum(_rowsum(err * err), axis=-1, keepdims=True) * (1.0 / D_MODEL)
        dy = err * (1.0 / D_MODEL)
        dx2 = _rms_bwd(dy * gfv, nf, rstd_f)
        dffb = (gate2 * dx2).astype(BF16)
        dff_ref[...] = dffb
        dh = jnp.zeros((tm, D_MODEL), F32)
        for j in range(n_chunks):
            c = slice(j * FFN_CHUNK, (j + 1) * FFN_CHUNK)
            cu = slice(D_FF + j * FFN_CHUNK, D_FF + (j + 1) * FFN_CHUNK)
            dact = _dot_nt(dffb, wd[c, :])
            a, u = a_s[:, c], u_s[:, c]
            da = (dact * u * _dsilu(a)).astype(BF16)
            du = (dact * _silu(a)).astype(BF16)
            dau_ref[:, c] = da
            dau_ref[:, cu] = du
            dh += _dot(da, wgu[c, :]) + _dot(du, wgu[cu, :])
        dn = dh * scale2
        sums_ref[0:1, :] += _rowsum(dh)
        sums_ref[1:2, :] += _rowsum(dh * n2)
        sums_ref[2:3, :] += _rowsum(dx2 * ff)
        sums_ref[3:4, :] += _rowsum(dn * xhat)
        sums_ref[4:5, :] += _rowsum(dy * nf)
        dx_ref[...] = dx2 + _rms_bwd(dn * g2, xhat, rstd)

    vec = _whole((1, D_MODEL))
    hbm = pl.BlockSpec(memory_space=pl.ANY)
    return pl.pallas_call(
        body, name="ffn", grid=(s // tm,),
        out_shape=[jax.ShapeDtypeStruct((s, D_MODEL), F32), jax.ShapeDtypeStruct((s, D_MODEL), BF16),
                   jax.ShapeDtypeStruct((s, D_FF), BF16), jax.ShapeDtypeStruct((s, 2 * D_FF), BF16),
                   jax.ShapeDtypeStruct((s, D_MODEL), BF16), jax.ShapeDtypeStruct((8, D_MODEL), F32),
                   jax.ShapeDtypeStruct((1, 128), F32)],
        in_specs=[_rows(tm, D_MODEL), _rows(tm, D_MODEL), _whole((1, 6 * D_MODEL)), vec, vec, hbm, hbm],
        out_specs=[_rows(tm, D_MODEL), _rows(tm, D_MODEL), _rows(tm, D_FF), _rows(tm, 2 * D_FF), _rows(tm, D_MODEL),
                   _whole((8, D_MODEL)), _whole((1, 128))],
        scratch_shapes=[pltpu.VMEM((2 * D_FF, D_MODEL), BF16), pltpu.VMEM((D_FF, D_MODEL), BF16),
                        pltpu.VMEM((tm, D_FF), F32), pltpu.VMEM((tm, D_FF), F32), pltpu.SemaphoreType.DMA((2,))],
        compiler_params=_params("arbitrary"),
    )(x1, target, mod, g2, gf, w_gu, w_down)


def _weight_grad(a, b, name):
    s, m = a.shape
    n = b.shape[1]
    ts = min(s, 2048)
    tm = max(t for t in range(128, m + 1, 128) if m % t == 0 and t * n * 4 <= 6 * 1024 * 1024)

    def body(a_ref, b_ref, o_ref):
        @pl.when(pl.program_id(1) == 0)
        def _():
            o_ref[...] = jnp.zeros_like(o_ref)

        o_ref[...] += _dot_tn(a_ref[...], b_ref[...])

    return pl.pallas_call(
        body, name=name, grid=(m // tm, s // ts), out_shape=jax.ShapeDtypeStruct((m, n), F32),
        in_specs=[pl.BlockSpec((ts, tm), lambda j, i: (i, j)), pl.BlockSpec((ts, n), lambda j, i: (i, 0))],
        out_specs=pl.BlockSpec((tm, n), lambda j, i: (j, 0)),
        compiler_params=_params("parallel", "arbitrary"),
    )(a, b)


def _adamw_math(w, g, m, v):
    m = ADAM_B1 * m + (1.0 - ADAM_B1) * g
    v = ADAM_B2 * v + (1.0 - ADAM_B2) * (g * g)
    m_hat = m / (1.0 - ADAM_B1 ** ADAM_STEP)
    v_hat = v / (1.0 - ADAM_B2 ** ADAM_STEP)
    delta = -ADAM_LR * (m_hat / (jnp.sqrt(v_hat) + ADAM_EPS) + ADAM_WD * w)
    return delta, m, v


def _adamw_shard(chip, w, m, v, partial, got, name):
    r, c = w.shape
    tr = _shard_rows(r)

    def body(chip_ref, w_ref, m_ref, v_ref, own_ref, g0, g1, g2, grad_ref, d_ref, nm_ref, nv_ref):
        g = ((own_ref[...] + g0[...].astype(F32)) + g1[...].astype(F32)) + g2[...].astype(F32)
        grad_ref[...] = g
        d_ref[...], nm_ref[...], nv_ref[...] = _adamw_math(w_ref[...], g, m_ref[...], v_ref[...])

    tile = pl.BlockSpec((tr, c), lambda i, chip_ref: (i, 0))
    own = pl.BlockSpec((None, tr, c), lambda i, chip_ref: (chip_ref[0], i, 0))
    part = [pl.BlockSpec((None, tr, c), functools.partial(lambda j, i, chip_ref: (j, i, 0), j)) for j in range(3)]
    return pl.pallas_call(
        body, name=name,
        grid_spec=pltpu.PrefetchScalarGridSpec(num_scalar_prefetch=1, grid=(r // tr,), in_specs=[tile] * 3 + [own] + part,
                                               out_specs=[tile] * 4),
        out_shape=[jax.ShapeDtypeStruct((r, c), F32)] * 4, compiler_params=_params("parallel"),
    )(chip, w, m, v, partial, got, got, got)


def _small_update(small_all, dmod_blocks, c_all, logits, w_ada, m_ada, v_ada, smalls):
    def body(sm_ref, dm_ref, c_ref, lg_ref, wa_ref, ma_ref, va_ref, *rest):
        ins, outs = rest[:21], rest[21:]
        _, me = _flip(0)
        tot = sm_ref[0:1, :]
        for i in range(1, N_DEV):
            tot = tot + sm_ref[i:i + 1, :]
        loss_ref = outs[0]
        loss_ref[...] = tot[:, SM_LOSS:SM_LOSS + 128]
        g_ada = lax.dot_general(_silu(c_ref[...]), dm_ref[me], (((0,), (0,)), ((), ())),
                                preferred_element_type=F32, precision=HIGHEST)
        outs[1][...] = g_ada
        outs[2][...], outs[3][...], outs[4][...] = _adamw_math(wa_ref[...], g_ada, ma_ref[...], va_ref[...])
        p0 = _lower_bound(lg_ref)
        dl0 = tot[:, SM_LB:SM_LB + 512] * p0 * (1.0 - p0)
        grads = [tot[:, SM_MOD:SM_MOD + 6 * D_MODEL], tot[:, SM_G1:SM_G1 + D_MODEL], tot[:, SM_G2:SM_G2 + D_MODEL],
                 tot[:, SM_GF:SM_GF + D_MODEL], tot[:, SM_ATT:SM_ATT + 512], tot[:, SM_HG:SM_HG + 128],
                 jnp.where(lax.broadcasted_iota(jnp.int32, (2, 512), 0) == 0, dl0, -dl0)]
        for i, g in enumerate(grads):
            w_ref, m_ref, v_ref = ins[3 * i:3 * i + 3]
            o = outs[5 + 4 * i:9 + 4 * i]
            o[0][...] = g
            o[1][...], o[2][...], o[3][...] = _adamw_math(w_ref[...], g, m_ref[...], v_ref[...])

    flat = [t for trio in smalls for t in trio]
    vm = pl.BlockSpec(memory_space=pltpu.VMEM)
    out_shape = [jax.ShapeDtypeStruct((1, 128), F32)] + [jax.ShapeDtypeStruct(w_ada.shape, F32)] * 4
    for trio in smalls:
        out_shape += [jax.ShapeDtypeStruct(trio[0].shape, F32)] * 4
    return pl.pallas_call(
        body, name="small_update", out_shape=out_shape,
        in_specs=[vm] * (7 + len(flat)), out_specs=[vm] * len(out_shape),
        compiler_params=pltpu.CompilerParams(vmem_limit_bytes=V7X_VMEM_LIMIT),
    )(small_all, dmod_blocks, c_all, logits, w_ada, m_ada, v_ada, *flat)


def kernel(x, c, w_ada, b_ada, norm1_g, w_in, hg_lb_logits, hg_onorm_g, att_onorm_g, w_out, norm2_g, w_gate_up, w_down, final_g, loss_target, m_w_ada, m_b_ada, m_norm1_g, m_w_in, m_hg_lb_logits, m_hg_onorm_g, m_att_onorm_g, m_w_out, m_norm2_g, m_w_gate_up, m_w_down, m_final_g, v_w_ada, v_b_ada, v_norm1_g, v_w_in, v_hg_lb_logits, v_hg_onorm_g, v_att_onorm_g, v_w_out, v_norm2_g, v_w_gate_up, v_w_down, v_final_g):
    x2d, target = x[0], loss_target[0]
    seq = x2d.shape[0]
    assert seq % (ATT_BLOCK * max(DILATIONS)) == 0 and seq % HG_TILE == 0
    gf = final_g.reshape(1, D_MODEL)

    c_all = _exchange_small(c.reshape(8, D_MODEL // 8), None, "gather_c").reshape(N_DEV, D_MODEL)
    ada = _ada_rows(c_all, w_ada[0], b_ada)
    mod = _exchange_small(ada, 1, "scatter_mod").reshape(1, 6 * D_MODEL)

    core = lax.axis_index("c").astype(jnp.int32).reshape(1)
    chip = (2 * lax.axis_index("x") + lax.axis_index("y")).astype(jnp.int32).reshape(1)
    me = 4 * lax.axis_index("x") + 2 * lax.axis_index("y") + lax.axis_index("c")

    g_in, = _gather_weights([w_in[0].T.astype(BF16)])
    w_in_b = g_in.reshape(IN_WIDTH, D_MODEL)
    rest_shards = [w_out[0].astype(BF16), w_gate_up[0].T.astype(BF16), w_down[0].astype(BF16)]
    lands = [lax.empty((N_DEV,) + s.shape, BF16) for s in rest_shards]
    g_send, g_recv, g_srcs, g_lands, tok = _copies_start("gather_rest_start", _plan_gather_own, 12, rest_shards, lands, [w_in_b, mod])
    flight = {}

    def stage(name, *vals):
        if name == "attention_begun":
            flight["shards"], got = _copies_wait("gather_rest_wait", _plan_gather_own, g_send, g_recv, g_srcs, g_lands, list(vals))
            flight["pass"] = _copies_start("gather_pass_start", _plan_gather_pass, 9, [], got, [])
            return flight["pass"][4][0:1, 0:1]
        if name == "mixer_weights_done":
            dw_gu, dw_down, dw_out = vals
            flight["grads"] = [dw_out.reshape(4, 2, D_MODEL // N_DEV, D_MODEL),
                               dw_gu.reshape(4, 2, 2 * D_FF // N_DEV, D_MODEL),
                               dw_down.reshape(4, 2, D_FF // N_DEV, D_MODEL)]
            pair_lands = [lax.empty((4,) + g.shape[2:], F32) for g in flight["grads"]]
            flight["pairs"] = _copies_start("reduce_pairs_start", _plan_reduce_pairs, 12, flight["grads"], pair_lands, [])
            return flight["pairs"][4][0:1, 0:1]
        if name == "attention_backward_begun":
            s, r, srcs, pl_lands, _ = flight["pairs"]
            grads, got = _copies_wait("reduce_pairs_wait", _plan_reduce_pairs, s, r, srcs, pl_lands, [vals[0]])
            flight["sums"] = [_pair_sum(core, g, b, f"pair_sum_{i}") for i, (g, b) in enumerate(zip(grads, got))]
            chip_lands = [lax.empty((3,) + s16.shape[1:], BF16) for _, s16 in flight["sums"]]
            flight["chips"] = _copies_start("reduce_chips_start", _plan_reduce_chips, 9, [s16 for _, s16 in flight["sums"]], chip_lands, [])
            return flight["chips"][4][0:1, 0:1]
        raise ValueError(name)

    def rest_weights(after):
        s, r, _, p_lands, _ = flight["pass"]
        _, got = _copies_wait("gather_pass_wait", _plan_gather_pass, s, r, [], p_lands, [after])
        full = [lax.dynamic_update_index_in_dim(g, shard, me, 0) for g, shard in zip(got, flight["shards"])]
        return full[0].reshape(D_MODEL, D_MODEL), full[1].reshape(2 * D_FF, D_MODEL), full[2].reshape(D_FF, D_MODEL)

    grad_x, dw_in, small = _block_step(x2d, target, mod + tok[0:1, 0:1], norm1_g, hg_lb_logits, hg_onorm_g, att_onorm_g, norm2_g, gf,
                                       w_in_b, rest_weights, stage)

    g_in8 = dw_in.reshape(4, 2, IN_WIDTH // N_DEV, D_MODEL)
    in_pairs = _copies_start("reduce_pairs_in_start", _plan_reduce_pairs, 4, [g_in8], [lax.empty((4,) + g_in8.shape[2:], F32)], [])
    s, r, srcs, c_lands, _ = flight["chips"]
    _, recv_rest = _copies_wait("reduce_chips_wait", _plan_reduce_chips, s, r, srcs, c_lands, [in_pairs[4]])
    small_rows = jnp.pad(small + in_pairs[4][0:1, 0:1], ((0, 0), (0, SM_PADDED - SM_WIDTH))).reshape(SM_PADDED // 128, 128)
    small_all = _exchange_small(small_rows, None, "gather_small").reshape(N_DEV, SM_PADDED)[:, :SM_WIDTH]
    in_grads, got_in = _copies_wait("reduce_pairs_in_wait", _plan_reduce_pairs, in_pairs[0], in_pairs[1], in_pairs[2], in_pairs[3],
                                    [small_all])
    in_s32, in_s16 = _pair_sum(core, in_grads[0], got_in[0], "pair_sum_in")
    in_chips = _copies_start("reduce_chips_in_start", _plan_reduce_chips, 3, [in_s16], [lax.empty((3,) + in_s16.shape[1:], BF16)], [])
    big, updated = {}, []
    rest_params = [("w_out", w_out, m_w_out, v_w_out), ("w_gate_up", w_gate_up, m_w_gate_up, v_w_gate_up), ("w_down", w_down, m_w_down, v_w_down)]
    for (n, w, m, v), (s32, _), got in zip(rest_params, flight["sums"], recv_rest):
        if n == "w_gate_up":
            outs4 = _adamw_shard(chip, w[0].T, m[0].T, v[0].T, s32, got, f"adamw_{n}")
            big[n] = [t.T[None] for t in outs4]
        else:
            outs4 = _adamw_shard(chip, w[0], m[0], v[0], s32, got, f"adamw_{n}")
            big[n] = [t[None] for t in outs4]
        updated.append(outs4[3])
    c_all = c_all + in_chips[4][0:1, 0:1]
    smalls = [(b_ada, m_b_ada, v_b_ada), (norm1_g, m_norm1_g, v_norm1_g), (norm2_g, m_norm2_g, v_norm2_g),
              (gf, m_final_g.reshape(1, D_MODEL), v_final_g.reshape(1, D_MODEL)),
              (att_onorm_g, m_att_onorm_g, v_att_onorm_g), (hg_onorm_g, m_hg_onorm_g, v_hg_onorm_g),
              (hg_lb_logits, m_hg_lb_logits, v_hg_lb_logits)]
    dmod_blocks = small_all[:, :6 * D_MODEL].reshape(N_DEV, N_DEV, 6 * D_MODEL // N_DEV).transpose(1, 0, 2)
    res = _small_update(small_all, dmod_blocks, c_all, hg_lb_logits, w_ada[0], m_w_ada[0], v_w_ada[0], smalls)
    _, recv_in = _copies_wait("reduce_chips_in_wait", _plan_reduce_chips, in_chips[0], in_chips[1], in_chips[2], in_chips[3],
                              [res[0]] + updated)
    big["w_in"] = [t.T[None] for t in _adamw_shard(chip, w_in[0].T, m_w_in[0].T, v_w_in[0].T, in_s32, recv_in[0], "adamw_w_in")]
    loss = res[0][0, 0]
    ada4 = [t[None] for t in res[1:5]]
    sm4 = {n: list(res[5 + 4 * i:9 + 4 * i]) for i, n in enumerate(["b_ada", "norm1_g", "norm2_g", "final_g", "att", "hg", "lb"])}
    sm4["final_g"] = [t.reshape(D_MODEL) for t in sm4["final_g"]]

    order = [ada4, sm4["b_ada"], sm4["norm1_g"], big["w_in"], sm4["lb"], sm4["hg"], sm4["att"], big["w_out"], sm4["norm2_g"],
             big["w_gate_up"], big["w_down"], sm4["final_g"]]
    return (loss, grad_x[None], *[o[0] for o in order], *[o[1] for o in order], *[o[2] for o in order], *[o[3] for o in order])


def _block_step(x2d, target, mod, norm1_g, hg_lb_logits, hg_onorm_g, att_onorm_g, norm2_g, gf, w_in_b, rest_weights, stage):
    h1, hq, hf, hi, hgt, aq, ak, av = _in_fwd(x2d, mod, norm1_g, w_in_b)
    hg_out, hg_o, hg_states = _hg_fwd(hq, hf, hi, hgt, hg_lb_logits, hg_onorm_g)
    branch = [_att_fwd(aq, ak, av, d) for d in DILATIONS[:2]]
    att_g = att_onorm_g + stage("attention_begun", branch[0][0], branch[1][0])
    branch += [_att_fwd(aq, ak, av, d) for d in DILATIONS[2:]]
    outs = [b[0] for b in branch]
    lses = [b[1] for b in branch]
    att, att_out = _att_combine(outs, lses, att_g)
    w_out_b, w_gu_b, w_down_b = rest_weights(att_out)
    x1 = _out_fwd(x2d, hg_out, att_out, mod, w_out_b)

    dx1, h2, act, dau, dff, ffn_sums, loss_part = _ffn(x1, target, mod, norm2_g, gf, w_gu_b, w_down_b)
    dw_gu = _weight_grad(dau, h2, "dw_gate_up")
    dw_down = _weight_grad(act, dff, "dw_down")

    dhg, dat, dw_out, dgate1 = _out_bwd(dx1, hg_out, att_out, mod, w_out_b)
    att_g = att_onorm_g + stage("mixer_weights_done", dw_gu, dw_down, dw_out)
    comb = _att_combine_bwd(dat, att, lses, att_g)
    dos, ccs, d_att_g = comb[0:3], comb[3:6], comb[6]
    hg_g = hg_onorm_g + stage("attention_backward_begun", comb[3])
    datt = []
    for i, d in enumerate(DILATIONS):
        datt.append(_att_bwd(aq, ak, av, dos[i], ccs[i], lses[i], d))
    dhq, dhf, dhi, dhgt, d_hg_g, d_lb = _hg_bwd(hq, hf, hi, hgt, hg_lb_logits, hg_g, hg_o, hg_states, dhg)
    dps = [dhq, dhf, dhi, dhgt] + [datt[i][j] for j in range(3) for i in range(3)]
    grad_x, dp_b, dshift1, dscale1, d_g1 = _in_bwd(x2d, dx1, mod, norm1_g, w_in_b, dps)
    dw_in = _weight_grad(dp_b, h1, "dw_in")
    small = jnp.concatenate([dshift1, dscale1, dgate1, ffn_sums[0:1], ffn_sums[1:2], ffn_sums[2:3], d_g1, ffn_sums[3:4],
                             ffn_sums[4:5], d_att_g, d_lb, d_hg_g, loss_part], axis=1)
    return grad_x, dw_in, small
```

```python
import functools

import jax
import jax.numpy as jnp
from jax import lax
from jax.experimental import pallas as pl
from jax.experimental.pallas import tpu as pltpu

F32 = jnp.float32
BF16 = jnp.bfloat16
HIGHEST = lax.Precision.HIGHEST
MESH = pl.DeviceIdType.MESH

D_MODEL = 1024
N_DEV = 8
HG_HEADS = 4
HG_DIM = 128
HG_WIDTH = HG_HEADS * HG_DIM
HG_CHUNK = 128
ATT_WIDTH = 512
ATT_HEAD_DIM = 64
ATT_BLOCK = 128
DILATIONS = (1, 4, 16)
ATT_SCALE = ATT_HEAD_DIM ** -0.5
D_FF = 2816
IN_WIDTH = 7 * 512
RMS_EPS = 1e-6
NEG = -1e30

ADAM_LR = 0.001
ADAM_B1 = 0.9
ADAM_B2 = 0.999
ADAM_EPS = 1e-08
ADAM_WD = 0.01
ADAM_STEP = 10

V7X_VMEM_LIMIT = 56 * 1024 * 1024

SM_MOD = 0
SM_G1 = 6 * D_MODEL
SM_G2 = 7 * D_MODEL
SM_GF = 8 * D_MODEL
SM_ATT = 9 * D_MODEL
SM_LB = 9 * D_MODEL + 512
SM_HG = 10 * D_MODEL
SM_LOSS = 10 * D_MODEL + 128
SM_WIDTH = 10 * D_MODEL + 256
SM_PADDED = 88 * 128


def _params(*sem, vmem=V7X_VMEM_LIMIT):
    return pltpu.CompilerParams(dimension_semantics=sem, vmem_limit_bytes=vmem)


def _dot(a, b):
    return jnp.dot(a, b, preferred_element_type=F32)


def _dot_nt(a, b):
    return lax.dot_general(a, b, (((1,), (1,)), ((), ())), preferred_element_type=F32)


def _dot_tn(a, b):
    return lax.dot_general(a, b, (((0,), (0,)), ((), ())), preferred_element_type=F32)


def _dot_f32(a, b):
    return jnp.dot(a, b, preferred_element_type=F32, precision=HIGHEST)


def _sigmoid(x):
    return 1.0 / (1.0 + jnp.exp(-x))


def _silu(x):
    return x * _sigmoid(x)


def _dsilu(x):
    s = _sigmoid(x)
    return s * (1.0 + x * (1.0 - s))


def _rms(x):
    rstd = lax.rsqrt(jnp.mean(x * x, axis=-1, keepdims=True) + RMS_EPS)
    return x * rstd, rstd


def _rms_bwd(dn, xhat, rstd):
    return rstd * (dn - xhat * jnp.mean(dn * xhat, axis=-1, keepdims=True))


def _rowsum(x):
    return jnp.sum(x, axis=0, keepdims=True)


def _rows(tm, n):
    return pl.BlockSpec((tm, n), lambda i: (i, 0))


def _whole(shape):
    return pl.BlockSpec(shape, lambda i: (0,) * len(shape))


def _mesh_pos():
    return lax.axis_index("x"), lax.axis_index("y"), lax.axis_index("c")


def _flip(k):
    x, y, c = _mesh_pos()
    px = 1 - x if k & 4 else x
    py = 1 - y if k & 2 else y
    pc = 1 - c if k & 1 else c
    return (px, py, pc), 4 * px + 2 * py + pc


def _exchange_small(x, rows_per_peer, name):
    r_all, cols = x.shape
    r_out = r_all if rows_per_peer is None else rows_per_peer

    def body(x_ref, out_ref, send_sems, recv_sems):
        _, me = _flip(0)

        def src(pid):
            if rows_per_peer is None:
                return x_ref
            return x_ref.at[pl.ds(pl.multiple_of(pid * r_out, r_out), r_out), :]

        if rows_per_peer is None:
            out_ref[me] = x_ref[...]
        else:
            out_ref[me] = x_ref[pl.ds(pl.multiple_of(me * r_out, r_out), r_out), :]
        sends = []
        for k in range(1, N_DEV):
            dev, pid = _flip(k)
            cp = pltpu.make_async_remote_copy(src_ref=src(pid), dst_ref=out_ref.at[me], send_sem=send_sems.at[k - 1],
                                              recv_sem=recv_sems.at[k - 1], device_id=dev, device_id_type=MESH)
            cp.start()
            sends.append(cp)
        for k in range(1, N_DEV):
            dev, pid = _flip(k)
            pltpu.make_async_remote_copy(src_ref=src(pid), dst_ref=out_ref.at[pid], send_sem=send_sems.at[k - 1],
                                         recv_sem=recv_sems.at[k - 1], device_id=dev, device_id_type=MESH).wait_recv()
        for cp in sends:
            cp.wait_send()

    return pl.pallas_call(
        body, name=name,
        out_shape=jax.ShapeDtypeStruct((N_DEV, r_out, cols), x.dtype),
        in_specs=[pl.BlockSpec(memory_space=pltpu.VMEM)],
        out_specs=pl.BlockSpec(memory_space=pltpu.VMEM),
        scratch_shapes=[pltpu.SemaphoreType.DMA((N_DEV - 1,)), pltpu.SemaphoreType.DMA((N_DEV - 1,))],
    )(x)


def _gather_weights(shards):
    n = len(shards)

    def body(*refs):
        xs, outs = refs[:n], refs[n:2 * n]
        send_sems, recv_sems, local_sems = refs[2 * n:]
        x, y, c = _mesh_pos()
        me, sibling = (x, y, c), (x, y, 1 - c)
        chips = [(1 - x, y), (x, 1 - y), (1 - x, 1 - y)]

        def blk(a, px, py, pc):
            return outs[a].at[4 * px + 2 * py + pc]

        def copy(a, k, block, to, src=None):
            return pltpu.make_async_remote_copy(
                src_ref=blk(a, *block) if src is None else src, dst_ref=blk(a, *block),
                send_sem=send_sems.at[a * 7 + k], recv_sem=recv_sems.at[a * 7 + k], device_id=to, device_id_type=MESH)

        mine = [pltpu.make_async_copy(xs[a], blk(a, *me), local_sems.at[a]) for a in range(n)]
        for cp in mine:
            cp.start()
        first = []
        for a in range(n):
            first.append(copy(a, 0, me, sibling, src=xs[a]))
            first += [copy(a, 1 + j, me, (*chip, c), src=xs[a]) for j, chip in enumerate(chips)]
        for cp in first:
            cp.start()
        passed = []
        for j, chip in enumerate(chips):
            for a in range(n):
                copy(a, 1 + j, (*chip, c), me).wait_recv()
                cp = copy(a, 4 + j, (*chip, c), sibling)
                cp.start()
                passed.append(cp)
        for a in range(n):
            copy(a, 0, sibling, me).wait_recv()
            for j, chip in enumerate(chips):
                copy(a, 4 + j, (*chip, 1 - c), me).wait_recv()
        for cp in first + passed:
            cp.wait_send()
        for cp in mine:
            cp.wait()

    hbm = pl.BlockSpec(memory_space=pl.ANY)
    return pl.pallas_call(
        body, name="gather_weights",
        out_shape=[jax.ShapeDtypeStruct((N_DEV,) + s.shape, s.dtype) for s in shards],
        in_specs=[hbm] * n, out_specs=[hbm] * n,
        scratch_shapes=[pltpu.SemaphoreType.DMA((7 * n,)), pltpu.SemaphoreType.DMA((7 * n,)), pltpu.SemaphoreType.DMA((n,))],
    )(*shards)


_HBM = pl.BlockSpec(memory_space=pltpu.HBM)
_SEM = pl.BlockSpec(memory_space=pltpu.SEMAPHORE)
_DATAFLOW = pltpu.SideEffectType.DATAFLOW_SIDE_EFFECTING


def _copies_start(name, plan, n_copies, srcs, lands, after):
    bufs = list(srcs) + list(lands)
    nb = len(bufs)

    def body(*refs):
        ins, send_sems, recv_sems, token = refs[:nb], refs[nb + len(after)], refs[nb + len(after) + 1], refs[-1]
        for i, (src, dst, dev) in enumerate(plan(ins[:len(srcs)], ins[len(srcs):])):
            pltpu.make_async_remote_copy(src_ref=src, dst_ref=dst, send_sem=send_sems.at[i], recv_sem=recv_sems.at[i],
                                         device_id=dev, device_id_type=MESH).start()
        token[...] = jnp.zeros_like(token)

    outs = pl.pallas_call(
        body, name=name,
        out_shape=(pltpu.SemaphoreType.DMA((n_copies,)), pltpu.SemaphoreType.DMA((n_copies,)),
                   *[pltpu.HBM(b.shape, b.dtype) for b in bufs], jax.ShapeDtypeStruct((8, 128), F32)),
        in_specs=[_HBM] * nb + [pl.BlockSpec(memory_space=pl.ANY)] * len(after),
        out_specs=(_SEM, _SEM, *[_HBM] * nb, pl.BlockSpec(memory_space=pltpu.VMEM)),
        input_output_aliases={i: 2 + i for i in range(nb)},
        compiler_params=pltpu.CompilerParams(has_side_effects=_DATAFLOW),
    )(*[pltpu.with_memory_space_constraint(b, pltpu.HBM) for b in bufs], *after)
    return outs[0], outs[1], list(outs[2:2 + len(srcs)]), list(outs[2 + len(srcs):2 + nb]), outs[-1]


def _copies_wait(name, plan, send_sems, recv_sems, srcs, lands, after):
    bufs = list(srcs) + list(lands)
    nb = len(bufs)

    def body(*refs):
        ins, send_ref, recv_ref = refs[:nb], refs[nb], refs[nb + 1]
        for i, (src, dst, dev) in enumerate(plan(ins[:len(srcs)], ins[len(srcs):])):
            cp = pltpu.make_async_remote_copy(src_ref=src, dst_ref=dst, send_sem=send_ref.at[i], recv_sem=recv_ref.at[i],
                                              device_id=dev, device_id_type=MESH)
            cp.wait_send()
            cp.wait_recv()

    outs = pl.pallas_call(
        body, name=name, out_shape=[pltpu.HBM(b.shape, b.dtype) for b in bufs],
        in_specs=[_HBM] * nb + [_SEM, _SEM] + [pl.BlockSpec(memory_space=pl.ANY)] * len(after), out_specs=[_HBM] * nb,
        input_output_aliases={i: i for i in range(nb)},
        compiler_params=pltpu.CompilerParams(has_side_effects=_DATAFLOW),
    )(*bufs, send_sems, recv_sems, *after)
    return list(outs[:len(srcs)]), list(outs[len(srcs):])


def _plan_gather_own(srcs, lands):
    _, me = _flip(0)
    return [(srcs[a], lands[a].at[me], _flip(k)[0]) for a in range(len(srcs)) for k in (1, 4, 2, 6)]


def _plan_gather_pass(srcs, lands):
    sibling = _flip(1)[0]
    plan = []
    for land in lands:
        for k in (4, 2, 6):
            block = land.at[_flip(k)[1]]
            plan.append((block, block, sibling))
    return plan


def _plan_reduce_pairs(srcs, lands):
    x, y, c = _mesh_pos()
    return [(srcs[a].at[chip, 1 - c], lands[a].at[chip], (x, y, 1 - c)) for a in range(len(srcs)) for chip in range(4)]


def _plan_reduce_chips(srcs, lands):
    plan = []
    for a in range(len(srcs)):
        for j, k in enumerate((4, 2, 6)):
            dev = _flip(k)[0]
            plan.append((srcs[a].at[2 * dev[0] + dev[1]], lands[a].at[j], dev))
    return plan


def _plan_reduce_direct(srcs, lands):
    plan = []
    for a in range(len(srcs)):
        for k in range(1, N_DEV):
            dev = _flip(k)[0]
            plan.append((srcs[a].at[2 * dev[0] + dev[1], dev[2]], lands[a].at[k - 1], dev))
    return plan


def _shard_rows(r):
    return r // 2 if r % 32 == 0 else r


def _pair_sum(core, grads, got, name):
    _, _, r, c = grads.shape
    tr = _shard_rows(r)

    def body(core_ref, a_ref, b_ref, o_ref, ob_ref):
        s = a_ref[...] + b_ref[...]
        o_ref[...] = s
        ob_ref[...] = s.astype(BF16)

    spec = pl.BlockSpec((None, tr, c), lambda i, j, core_ref: (i, j, 0))
    return pl.pallas_call(
        body, name=name,
        grid_spec=pltpu.PrefetchScalarGridSpec(
            num_scalar_prefetch=1, grid=(4, r // tr),
            in_specs=[pl.BlockSpec((None, None, tr, c), lambda i, j, core_ref: (i, core_ref[0], j, 0)), spec],
            out_specs=[spec, spec]),
        out_shape=[jax.ShapeDtypeStruct((4, r, c), F32), jax.ShapeDtypeStruct((4, r, c), BF16)],
        compiler_params=_params("parallel", "parallel"),
    )(core, grads, got)


def _ada_rows(c_all, w_ada, b_ada):
    n_cols = w_ada.shape[1]

    def body(c_ref, w_ref, b_ref, o_ref):
        _, me = _flip(0)
        bias = b_ref[:, pl.ds(pl.multiple_of(me * n_cols, 128), n_cols)]
        o_ref[...] = _dot_f32(_silu(c_ref[...]), w_ref[...]) + bias

    return pl.pallas_call(
        body, name="ada_rows", out_shape=jax.ShapeDtypeStruct((N_DEV, n_cols), F32),
        in_specs=[pl.BlockSpec(memory_space=pltpu.VMEM)] * 3, out_specs=pl.BlockSpec(memory_space=pltpu.VMEM),
    )(c_all, w_ada, b_ada)


def _in_fwd(x, mod, g1, w_in):
    s = x.shape[0]
    tm = 256

    def body(x_ref, mod_ref, g_ref, w_ref, h_ref, *outs):
        xhat, _ = _rms(x_ref[...])
        h = (xhat * g_ref[...]) * (1.0 + mod_ref[:, D_MODEL:2 * D_MODEL]) + mod_ref[:, 0:D_MODEL]
        hb = h.astype(BF16)
        h_ref[...] = hb
        for j, o_ref in enumerate(outs):
            o_ref[...] = _dot_nt(hb, w_ref[j * 512:(j + 1) * 512, :])

    return pl.pallas_call(
        body, name="in_fwd", grid=(s // tm,),
        out_shape=[jax.ShapeDtypeStruct((s, D_MODEL), BF16)] + [jax.ShapeDtypeStruct((s, 512), F32)] * 7,
        in_specs=[_rows(tm, D_MODEL), _whole((1, 6 * D_MODEL)), _whole((1, D_MODEL)), _whole((IN_WIDTH, D_MODEL))],
        out_specs=[_rows(tm, D_MODEL)] + [_rows(tm, 512)] * 7,
        compiler_params=_params("parallel"),
    )(x, mod, g1, w_in)


def _in_bwd(x, dx1, mod, g1, w_in, dps):
    s = x.shape[0]
    tm = 256

    def body(x_ref, dx_ref, mod_ref, g_ref, w_ref, *rest):
        dp_refs, (gx_ref, dpb_ref, dsh_ref, dsc_ref, dg_ref) = rest[:13], rest[13:]
        pieces = [dp_refs[j][...] for j in range(4)]
        pieces += [dp_refs[4 + 3 * j][...] + dp_refs[5 + 3 * j][...] + dp_refs[6 + 3 * j][...] for j in range(3)]
        for j, p in enumerate(pieces):
            dpb_ref[:, j * 512:(j + 1) * 512] = p.astype(BF16)
        dh = _dot(dpb_ref[...], w_ref[...])
        xhat, rstd = _rms(x_ref[...])
        g = g_ref[...]
        scale1 = 1.0 + mod_ref[:, D_MODEL:2 * D_MODEL]
        n1 = xhat * g

        @pl.when(pl.program_id(0) == 0)
        def _():
            dsh_ref[...] = jnp.zeros_like(dsh_ref)
            dsc_ref[...] = jnp.zeros_like(dsc_ref)
            dg_ref[...] = jnp.zeros_like(dg_ref)

        dsh_ref[...] += _rowsum(dh)
        dsc_ref[...] += _rowsum(dh * n1)
        dn = dh * scale1
        dg_ref[...] += _rowsum(dn * xhat)
        gx_ref[...] = dx_ref[...] + _rms_bwd(dn * g, xhat, rstd)

    vec = _whole((1, D_MODEL))
    return pl.pallas_call(
        body, name="in_bwd", grid=(s // tm,),
        out_shape=[jax.ShapeDtypeStruct((s, D_MODEL), F32), jax.ShapeDtypeStruct((s, IN_WIDTH), BF16)]
        + [jax.ShapeDtypeStruct((1, D_MODEL), F32)] * 3,
        in_specs=[_rows(tm, D_MODEL), _rows(tm, D_MODEL), _whole((1, 6 * D_MODEL)), vec, _whole((IN_WIDTH, D_MODEL))]
        + [_rows(tm, 512)] * 13,
        out_specs=[_rows(tm, D_MODEL), _rows(tm, IN_WIDTH), vec, vec, vec],
        compiler_params=_params("arbitrary"),
    )(x, dx1, mod, g1, w_in, *dps)


HG_TILE = 512
HG_TILE_CHUNKS = HG_TILE // HG_CHUNK


def _lower_bound(lg_ref):
    return 1.0 / (1.0 + jnp.exp(lg_ref[1:2, :] - lg_ref[0:1, :]))


def _chunk_masks():
    r = lax.broadcasted_iota(jnp.int32, (HG_CHUNK, HG_CHUNK), 0)
    c = lax.broadcasted_iota(jnp.int32, (HG_CHUNK, HG_CHUNK), 1)
    return r >= c, c >= r, (r >= c).astype(F32), (c >= r).astype(F32)


def _hg_fwd(hq, hf, hi, hgt, logits, onorm_g):
    s = hq.shape[0]
    n_tiles = s // HG_TILE

    def body(q_ref, f_ref, i_ref, g_ref, lg_ref, og_ref, out_ref, o_ref, st_ref, state, qf_s, kk_s, lf_s):
        @pl.when(pl.program_id(0) == 0)
        def _():
            state[...] = jnp.zeros_like(state)

        lb = _lower_bound(lg_ref)
        f = lb + (1.0 - lb) * _sigmoid(f_ref[...])
        kk_s[...] = 1.0 - f
        lf_s[...] = jnp.log(f)
        qf_s[...] = _silu(q_ref[...])
        causal, _, tri, _ = _chunk_masks()

        def chunk(ci, carry):
            rows = pl.ds(pl.multiple_of(ci * HG_CHUNK, HG_CHUNK), HG_CHUNK)
            srows = pl.ds(pl.multiple_of(ci * HG_DIM, HG_DIM), HG_DIM)
            lf = lf_s[rows, :]
            b = _dot_f32(tri, lf)
            bl = _rowsum(lf)
            ref = 0.5 * bl
            qf, kk, v = qf_s[rows, :], kk_s[rows, :], i_ref[rows, :]
            a_in = (qf * jnp.exp(b)).astype(BF16)
            a_t = (qf * jnp.exp(b - ref)).astype(BF16)
            b_t = (kk * jnp.exp(ref - b)).astype(BF16)
            kd = kk * jnp.exp(bl - b)
            ebl = jnp.exp(bl)
            vb = v.astype(BF16)
            for h in range(HG_HEADS):
                c = slice(h * HG_DIM, (h + 1) * HG_DIM)
                st = state[h]
                st_ref[srows, c] = st
                p = jnp.where(causal, _dot_nt(a_t[:, c], b_t[:, c]), 0.0)
                o_ref[rows, c] = _dot(p.astype(BF16), vb[:, c]) + _dot_nt(a_in[:, c], st.astype(BF16))
                state[h] = st * ebl[:, c] + _dot_tn(vb[:, c], kd[:, c].astype(BF16))
            return carry

        lax.fori_loop(0, HG_TILE_CHUNKS, chunk, 0, unroll=2)
        for h in range(HG_HEADS):
            c = slice(h * HG_DIM, (h + 1) * HG_DIM)
            ohat, _ = _rms(o_ref[:, c])
            out_ref[:, c] = (ohat * og_ref[...] * _silu(g_ref[:, c])).astype(BF16)

    tile = _rows(HG_TILE, HG_WIDTH)
    return pl.pallas_call(
        body, name="hg_fwd", grid=(n_tiles,),
        out_shape=[jax.ShapeDtypeStruct((s, HG_WIDTH), BF16), jax.ShapeDtypeStruct((s, HG_WIDTH), F32),
                   jax.ShapeDtypeStruct((s // HG_CHUNK * HG_DIM, HG_WIDTH), F32)],
        in_specs=[tile] * 4 + [_whole((2, HG_WIDTH)), _whole((1, HG_DIM))],
        out_specs=[tile, tile, _rows(HG_TILE_CHUNKS * HG_DIM, HG_WIDTH)],
        scratch_shapes=[pltpu.VMEM((HG_HEADS, HG_DIM, HG_DIM), F32)] + [pltpu.VMEM((HG_TILE, HG_WIDTH), F32)] * 3,
        compiler_params=_params("arbitrary"),
    )(hq, hf, hi, hgt, logits, onorm_g)


def _hg_bwd(hq, hf, hi, hgt, logits, onorm_g, o, states, dout):
    s = hq.shape[0]
    n_tiles = s // HG_TILE

    def body(q_ref, f_ref, i_ref, g_ref, lg_ref, og_ref, o_ref, st_ref, d_ref,
             dq_ref, df_ref, di_ref, dg_ref, dog_ref, dlb_ref, dstate, qf_s, kk_s, lf_s, do_s):
        @pl.when(pl.program_id(0) == 0)
        def _():
            dstate[...] = jnp.zeros_like(dstate)
            dog_ref[...] = jnp.zeros_like(dog_ref)
            dlb_ref[...] = jnp.zeros_like(dlb_ref)

        og = og_ref[...]
        dog = jnp.zeros((1, HG_DIM), F32)
        for h in range(HG_HEADS):
            c = slice(h * HG_DIM, (h + 1) * HG_DIM)
            ohat, rstd = _rms(o_ref[:, c])
            gate = g_ref[:, c]
            d = d_ref[:, c]
            dg_ref[:, c] = (d * (ohat * og) * _dsilu(gate)).astype(BF16)
            dnormed = d * _silu(gate)
            dog += _rowsum(dnormed * ohat)
            do_s[:, c] = _rms_bwd(dnormed * og, ohat, rstd)
        dog_ref[...] += dog

        lb = _lower_bound(lg_ref)
        f = lb + (1.0 - lb) * _sigmoid(f_ref[...])
        kk_s[...] = 1.0 - f
        lf_s[...] = jnp.log(f)
        qf_s[...] = _silu(q_ref[...])
        causal, upper, tri, tri_t = _chunk_masks()

        def chunk(step, carry):
            ci = HG_TILE_CHUNKS - 1 - step
            rows = pl.ds(pl.multiple_of(ci * HG_CHUNK, HG_CHUNK), HG_CHUNK)
            srows = pl.ds(pl.multiple_of(ci * HG_DIM, HG_DIM), HG_DIM)
            lf = lf_s[rows, :]
            b = _dot_f32(tri, lf)
            bl = _rowsum(lf)
            ref = 0.5 * bl
            qf, kk, v, do = qf_s[rows, :], kk_s[rows, :], i_ref[rows, :], do_s[rows, :]
            eb, ebr, erb, ekd, ebl = jnp.exp(b), jnp.exp(b - ref), jnp.exp(ref - b), jnp.exp(bl - b), jnp.exp(bl)
            a_in, a_t, b_t, kd = qf * eb, qf * ebr, kk * erb, kk * ekd
            for h in range(HG_HEADS):
                c = slice(h * HG_DIM, (h + 1) * HG_DIM)
                st, dst = st_ref[srows, c], dstate[h]
                stb, dstb = st.astype(BF16), dst.astype(BF16)
                doh, vh = do[:, c], v[:, c]
                dob, vb = doh.astype(BF16), vh.astype(BF16)
                ain_h, at_h, bt_h, kd_h = a_in[:, c], a_t[:, c], b_t[:, c], kd[:, c]
                atb, btb = at_h.astype(BF16), bt_h.astype(BF16)
                d_ain = _dot(dob, stb)
                p_t = jnp.where(upper, _dot_nt(btb, atb), 0.0).astype(BF16)
                dp = jnp.where(causal, _dot_nt(dob, vb), 0.0).astype(BF16)
                dp_t = jnp.where(upper, _dot_nt(vb, dob), 0.0).astype(BF16)
                di_ref[rows, c] = (_dot(p_t, dob) + _dot_nt(kd_h.astype(BF16), dstb)).astype(BF16)
                d_at = _dot(dp, btb)
                d_bt = _dot(dp_t, atb)
                d_kd = _dot(vb, dstb)
                dqf = d_ain * eb[:, c] + d_at * ebr[:, c]
                dkk = d_bt * erb[:, c] + d_kd * ekd[:, c]
                db = d_ain * ain_h + d_at * atb.astype(F32) - d_bt * btb.astype(F32) - d_kd * kd_h
                dbl = _rowsum(d_kd * kd_h) + _rowsum(dst * st) * ebl[:, c]
                dstate[h] = _dot_tn(dob, ain_h.astype(BF16)) + dst * ebl[:, c]
                dlf = _dot_f32(tri_t, db) + dbl
                qv, fr = q_ref[rows, c], f_ref[rows, c]
                lbh = lb[:, c]
                sg = _sigmoid(fr)
                dfv = dlf / (lbh + (1.0 - lbh) * sg) - dkk
                df_ref[rows, c] = (dfv * (1.0 - lbh) * sg * (1.0 - sg)).astype(BF16)
                dlb_ref[:, c] += _rowsum(dfv * (1.0 - sg))
                dq_ref[rows, c] = (dqf * _dsilu(qv)).astype(BF16)
            return carry

        lax.fori_loop(0, HG_TILE_CHUNKS, chunk, 0, unroll=2)

    rev = pl.BlockSpec((HG_TILE, HG_WIDTH), lambda i: (n_tiles - 1 - i, 0))
    return pl.pallas_call(
        body, name="hg_bwd", grid=(n_tiles,),
        out_shape=[jax.ShapeDtypeStruct((s, HG_WIDTH), BF16)] * 4
        + [jax.ShapeDtypeStruct((1, HG_DIM), F32), jax.ShapeDtypeStruct((1, HG_WIDTH), F32)],
        in_specs=[rev] * 4 + [_whole((2, HG_WIDTH)), _whole((1, HG_DIM)), rev,
                              pl.BlockSpec((HG_TILE_CHUNKS * HG_DIM, HG_WIDTH), lambda i: (n_tiles - 1 - i, 0)), rev],
        out_specs=[rev] * 4 + [_whole((1, HG_DIM)), _whole((1, HG_WIDTH))],
        scratch_shapes=[pltpu.VMEM((HG_HEADS, HG_DIM, HG_DIM), F32)] + [pltpu.VMEM((HG_TILE, HG_WIDTH), F32)] * 4,
        compiler_params=_params("arbitrary"),
    )(hq, hf, hi, hgt, logits, onorm_g, o, states, dout)


TOKEN_GROUP = 16


def _att_geometry(dil):
    per_group = TOKEN_GROUP // dil
    return per_group, ATT_BLOCK // per_group, ATT_WIDTH if dil == 1 else 128, 1 if dil == TOKEN_GROUP else 4


def _att_consts(dil):
    per_group, ub = _att_geometry(dil)[:2]

    def pos(i):
        return i if dil == 1 else (i % ub) * per_group + i // ub

    lane = lax.broadcasted_iota(jnp.int32, (ATT_BLOCK, 128), 1)
    qi = pos(lax.broadcasted_iota(jnp.int32, (2 * ATT_BLOCK, ATT_BLOCK), 0) % ATT_BLOCK)
    kj = pos(lax.broadcasted_iota(jnp.int32, (2 * ATT_BLOCK, ATT_BLOCK), 1))
    return lane < ATT_HEAD_DIM, kj <= qi, lambda off: kj >= qi + off


def _load_tile(ref, dil, r, c, base=0):
    per_group, ub = _att_geometry(dil)[:2]
    if dil == 1:
        return ref[base:base + ATT_BLOCK, c]
    return jnp.concatenate([ref[pl.ds(base + dil * w + r, ub, stride=TOKEN_GROUP), c] for w in range(per_group)], axis=0)


def _store_tile(ref, dil, r, c, val, base=0):
    per_group, ub = _att_geometry(dil)[:2]
    if dil == 1:
        ref[base:base + ATT_BLOCK, c] = val
        return
    for w in range(per_group):
        ref[pl.ds(base + dil * w + r, ub, stride=TOKEN_GROUP), c] = val[w * ub:(w + 1) * ub]


def _stack_heads(x2, first):
    return jnp.concatenate([jnp.where(first, x2, 0.0), jnp.where(first, 0.0, x2)], axis=0)


def _stack_bcast(x2, first):
    other = pltpu.roll(x2, ATT_HEAD_DIM, axis=1)
    return jnp.concatenate([jnp.where(first, x2, other), jnp.where(first, other, x2)], axis=0)


def _unstack_heads(st, first):
    return jnp.where(first, st[:ATT_BLOCK], st[ATT_BLOCK:])


def _att_fwd(q, k, v, dil):
    seq, width = q.shape
    _, ub, lanes, nbs = _att_geometry(dil)
    rows = ub * TOKEN_GROUP
    n_steps = seq // (nbs * rows)

    def body(q_ref, k_ref, v_ref, kp_ref, vp_ref, o_ref, lse_ref):
        first, cur_ok, _band = _att_consts(dil)
        inner_ok = _band(0)
        edge_ok = _band(jnp.where(pl.program_id(0) > 0, 0, ATT_BLOCK))
        for b in range(nbs):
            base = b * rows
            prev_ok = edge_ok if b == 0 else inner_ok
            for r in range(dil):
                for j in range(lanes // 128):
                    c = slice(j * 128, (j + 1) * 128)
                    qst = _stack_heads(_load_tile(q_ref, dil, r, c, base) * ATT_SCALE, first).astype(BF16)
                    kc = _load_tile(k_ref, dil, r, c, base).astype(BF16)
                    vc = _load_tile(v_ref, dil, r, c, base).astype(BF16)
                    if b == 0:
                        kp, vp = _load_tile(kp_ref, dil, r, c).astype(BF16), _load_tile(vp_ref, dil, r, c).astype(BF16)
                    else:
                        kp = _load_tile(k_ref, dil, r, c, base - rows).astype(BF16)
                        vp = _load_tile(v_ref, dil, r, c, base - rows).astype(BF16)
                    sc = jnp.where(cur_ok, _dot_nt(qst, kc), NEG)
                    sp = jnp.where(prev_ok, _dot_nt(qst, kp), NEG)
                    mx = jnp.maximum(jnp.max(sc, axis=-1, keepdims=True), jnp.max(sp, axis=-1, keepdims=True))
                    pc, pp = jnp.exp(sc - mx), jnp.exp(sp - mx)
                    den = jnp.sum(pc, axis=-1, keepdims=True) + jnp.sum(pp, axis=-1, keepdims=True)
                    ost = (_dot(pc.astype(BF16), vc) + _dot(pp.astype(BF16), vp)) / den
                    lse = jnp.broadcast_to(mx + jnp.log(den), (2 * ATT_BLOCK, 128))
                    _store_tile(o_ref, dil, r, c, _unstack_heads(ost, first), base)
                    _store_tile(lse_ref, dil, r, c, _unstack_heads(lse, first), base)

    slab = pl.BlockSpec((nbs * rows, lanes), lambda n, j: (n, j))
    before = pl.BlockSpec((rows, lanes), lambda n, j: (jnp.maximum(n * nbs - 1, 0), j))
    return pl.pallas_call(
        body, name=f"att_fwd_d{dil}", grid=(n_steps, width // lanes),
        out_shape=[jax.ShapeDtypeStruct((seq, width), F32)] * 2,
        in_specs=[slab, slab, slab, before, before], out_specs=[slab, slab],
        compiler_params=_params("arbitrary", "arbitrary"),
    )(q, k, v, k, v)


def _att_bwd(q, k, v, do, cc, lse, dil):
    seq, width = q.shape
    _, ub, lanes, nbs = _att_geometry(dil)
    rows = ub * TOKEN_GROUP
    n_blocks = seq // rows
    n_steps = n_blocks // nbs

    def body(q_ref, k_ref, v_ref, do_ref, cc_ref, lse_ref, qx_ref, dox_ref, ccx_ref, lsex_ref,
             dq_ref, dk_ref, dv_ref, carry):
        first, cur_ok, _band = _att_consts(dil)
        step = pl.program_id(1)
        inner_ok = _band(0)
        edge_ok = _band(jnp.where(step < n_steps - 1, 0, ATT_BLOCK))

        @pl.when(step == 0)
        def _():
            carry[...] = jnp.zeros_like(carry)

        for b in range(nbs):
            base = b * rows
            last = b == nbs - 1
            next_ok = edge_ok if last else inner_ok
            for r in range(dil):
                for j in range(lanes // 128):
                    c = slice(j * 128, (j + 1) * 128)

                    def following(inner_ref, edge_ref):
                        return _load_tile(edge_ref, dil, r, c) if last else _load_tile(inner_ref, dil, r, c, base + rows)

                    qst = _stack_heads(_load_tile(q_ref, dil, r, c, base) * ATT_SCALE, first).astype(BF16)
                    qxst = _stack_heads(following(q_ref, qx_ref) * ATT_SCALE, first).astype(BF16)
                    dost = _stack_heads(_load_tile(do_ref, dil, r, c, base), first).astype(BF16)
                    doxst = _stack_heads(following(do_ref, dox_ref), first).astype(BF16)
                    lse_n = _stack_bcast(_load_tile(lse_ref, dil, r, c, base), first)
                    lse_x = _stack_bcast(following(lse_ref, lsex_ref), first)
                    cc_n = _stack_bcast(_load_tile(cc_ref, dil, r, c, base), first)
                    cc_x = _stack_bcast(following(cc_ref, ccx_ref), first)
                    kb = _load_tile(k_ref, dil, r, c, base).astype(BF16)
                    vb = _load_tile(v_ref, dil, r, c, base).astype(BF16)
                    p_cur = jnp.exp(jnp.where(cur_ok, _dot_nt(qst, kb), NEG) - lse_n)
                    p_next = jnp.exp(jnp.where(next_ok, _dot_nt(qxst, kb), NEG) - lse_x)
                    ds_cur = (p_cur * (_dot_nt(dost, vb) + cc_n)).astype(BF16)
                    ds_next = (p_next * (_dot_nt(doxst, vb) + cc_x)).astype(BF16)
                    dq_own = _load_tile(carry, dil, r, c) + _unstack_heads(_dot(ds_cur, kb), first)
                    _store_tile(dq_ref, dil, r, c, dq_own * ATT_SCALE, base)
                    _store_tile(carry, dil, r, c, _unstack_heads(_dot(ds_next, kb), first))
                    _store_tile(dk_ref, dil, r, c, _dot_tn(ds_cur, qst) + _dot_tn(ds_next, qxst), base)
                    _store_tile(dv_ref, dil, r, c, _dot_tn(p_cur.astype(BF16), dost) + _dot_tn(p_next.astype(BF16), doxst), base)

    slab = pl.BlockSpec((nbs * rows, lanes), lambda j, n: (n, j))
    after = pl.BlockSpec((rows, lanes), lambda j, n: (jnp.minimum((n + 1) * nbs, n_blocks - 1), j))
    return pl.pallas_call(
        body, name=f"att_bwd_d{dil}", grid=(width // lanes, n_steps),
        out_shape=[jax.ShapeDtypeStruct((seq, width), F32)] * 3,
        in_specs=[slab] * 6 + [after] * 4, out_specs=[slab] * 3,
        scratch_shapes=[pltpu.VMEM((rows, lanes), F32)],
        compiler_params=_params("arbitrary", "arbitrary"),
    )(q, k, v, do, cc, lse, q, do, cc, lse)


def _branch_weights(lses):
    mx = jnp.maximum(jnp.maximum(lses[0], lses[1]), lses[2])
    es = [jnp.exp(l - mx) for l in lses]
    inv = 1.0 / (es[0] + es[1] + es[2])
    return [e * inv for e in es]


def _att_combine(outs, lses, att_g):
    s = outs[0].shape[0]
    tm = 512

    def body(o0, o1, o2, l0, l1, l2, g_ref, att_ref, out_ref):
        ws = _branch_weights([l0[...], l1[...], l2[...]])
        att = ws[0] * o0[...] + ws[1] * o1[...] + ws[2] * o2[...]
        att_ref[...] = att
        ahat, _ = _rms(att)
        out_ref[...] = (ahat * g_ref[...]).astype(BF16)

    tile = _rows(tm, ATT_WIDTH)
    return pl.pallas_call(
        body, name="att_combine", grid=(s // tm,),
        out_shape=[jax.ShapeDtypeStruct((s, ATT_WIDTH), F32), jax.ShapeDtypeStruct((s, ATT_WIDTH), BF16)],
        in_specs=[tile] * 6 + [_whole((1, ATT_WIDTH))], out_specs=[tile, tile],
        compiler_params=_params("parallel"),
    )(*outs, *lses, att_g)


def _att_combine_bwd(datt_out, att, lses, att_g):
    s = att.shape[0]
    tm = 256

    def body(d_ref, att_ref, l0, l1, l2, g_ref, do0, do1, do2, cc0, cc1, cc2, dg_ref):
        @pl.when(pl.program_id(0) == 0)
        def _():
            dg_ref[...] = jnp.zeros_like(dg_ref)

        att = att_ref[...]
        ahat, rstd = _rms(att)
        d = d_ref[...]
        dg_ref[...] += _rowsum(d * ahat)
        datt = _rms_bwd(d * g_ref[...], ahat, rstd)
        hi = lax.broadcasted_iota(jnp.int32, (ATT_WIDTH, ATT_WIDTH), 0) // ATT_HEAD_DIM
        hj = lax.broadcasted_iota(jnp.int32, (ATT_WIDTH, ATT_WIDTH), 1) // ATT_HEAD_DIM
        head_sum = _dot_f32(datt * att, (hi == hj).astype(F32))
        ws = _branch_weights([l0[...], l1[...], l2[...]])
        for w, do_ref, cc_ref in zip(ws, (do0, do1, do2), (cc0, cc1, cc2)):
            do_ref[...] = w * datt
            cc_ref[...] = -w * head_sum

    tile = _rows(tm, ATT_WIDTH)
    return pl.pallas_call(
        body, name="att_combine_bwd", grid=(s // tm,),
        out_shape=[jax.ShapeDtypeStruct((s, ATT_WIDTH), F32)] * 6 + [jax.ShapeDtypeStruct((1, ATT_WIDTH), F32)],
        in_specs=[tile] * 5 + [_whole((1, ATT_WIDTH))], out_specs=[tile] * 6 + [_whole((1, ATT_WIDTH))],
        compiler_params=_params("arbitrary"),
    )(datt_out, att, *lses, att_g)


def _out_fwd(x, hg, at, mod, w_out):
    s = x.shape[0]
    tm = 512

    def body(x_ref, hg_ref, at_ref, mod_ref, w_ref, x1_ref):
        mix = _dot(hg_ref[...], w_ref[0:512, :]) + _dot(at_ref[...], w_ref[512:1024, :])
        x1_ref[...] = x_ref[...] + mod_ref[:, 2 * D_MODEL:3 * D_MODEL] * mix

    return pl.pallas_call(
        body, name="out_fwd", grid=(s // tm,), out_shape=jax.ShapeDtypeStruct((s, D_MODEL), F32),
        in_specs=[_rows(tm, D_MODEL), _rows(tm, 512), _rows(tm, 512), _whole((1, 6 * D_MODEL)), _whole((D_MODEL, D_MODEL))],
        out_specs=_rows(tm, D_MODEL), compiler_params=_params("parallel"),
    )(x, hg, at, mod, w_out)


def _out_bwd(dx1, hg, at, mod, w_out):
    s = dx1.shape[0]
    tm = 512
    n_steps = s // tm

    def body(dx_ref, hg_ref, at_ref, mod_ref, w_ref, dhg_ref, dat_ref, dw_ref, dwb_ref, dgate_ref):
        @pl.when(pl.program_id(0) == 0)
        def _():
            dw_ref[...] = jnp.zeros_like(dw_ref)
            dgate_ref[...] = jnp.zeros_like(dgate_ref)

        hg, at, dx = hg_ref[...], at_ref[...], dx_ref[...]
        mix = _dot(hg, w_ref[0:512, :]) + _dot(at, w_ref[512:1024, :])
        dgate_ref[...] += _rowsum(dx * mix)
        dmix = (mod_ref[:, 2 * D_MODEL:3 * D_MODEL] * dx).astype(BF16)
        dhg_ref[...] = _dot_nt(dmix, w_ref[0:512, :])
        dat_ref[...] = _dot_nt(dmix, w_ref[512:1024, :])
        dw_ref[0:512, :] += _dot_tn(hg, dmix)
        dw_ref[512:1024, :] += _dot_tn(at, dmix)

        @pl.when(pl.program_id(0) == n_steps - 1)
        def _():
            dwb_ref[...] = dw_ref[...].astype(BF16)

    return pl.pallas_call(
        body, name="out_bwd", grid=(n_steps,),
        out_shape=[jax.ShapeDtypeStruct((s, 512), F32)] * 2
        + [jax.ShapeDtypeStruct((D_MODEL, D_MODEL), F32), jax.ShapeDtypeStruct((D_MODEL, D_MODEL), BF16),
           jax.ShapeDtypeStruct((1, D_MODEL), F32)],
        in_specs=[_rows(tm, D_MODEL), _rows(tm, 512), _rows(tm, 512), _whole((1, 6 * D_MODEL)), _whole((D_MODEL, D_MODEL))],
        out_specs=[_rows(tm, 512), _rows(tm, 512), _whole((D_MODEL, D_MODEL)), _whole((D_MODEL, D_MODEL)), _whole((1, D_MODEL))],
        compiler_params=_params("arbitrary"),
    )(dx1, hg, at, mod, w_out)


FFN_CHUNK = 2816


def _ffn(x1, target, mod, g2, gf, w_gu, w_down):
    s = x1.shape[0]
    tm = 256
    n_chunks = D_FF // FFN_CHUNK

    def body(x_ref, t_ref, mod_ref, g2_ref, gf_ref, wgu_hbm, wd_hbm,
             dx_ref, h2_ref, act_ref, dau_ref, dff_ref, sums_ref, loss_ref, wgu, wd, a_s, u_s, sem):
        @pl.when(pl.program_id(0) == 0)
        def _():
            c1 = pltpu.make_async_copy(wgu_hbm, wgu, sem.at[0])
            c2 = pltpu.make_async_copy(wd_hbm, wd, sem.at[1])
            c1.start()
            c2.start()
            c1.wait()
            c2.wait()
            sums_ref[...] = jnp.zeros_like(sums_ref)
            loss_ref[...] = jnp.zeros_like(loss_ref)

        x1v = x_ref[...]
        xhat, rstd = _rms(x1v)
        g2 = g2_ref[...]
        n2 = xhat * g2
        scale2 = 1.0 + mod_ref[:, 4 * D_MODEL:5 * D_MODEL]
        gate2 = mod_ref[:, 5 * D_MODEL:6 * D_MODEL]
        hb = (n2 * scale2 + mod_ref[:, 3 * D_MODEL:4 * D_MODEL]).astype(BF16)
        h2_ref[...] = hb
        ff = jnp.zeros((tm, D_MODEL), F32)
        for j in range(n_chunks):
            c = slice(j * FFN_CHUNK, (j + 1) * FFN_CHUNK)
            cu = slice(D_FF + j * FFN_CHUNK, D_FF + (j + 1) * FFN_CHUNK)
            a = _dot_nt(hb, wgu[c, :])
            u = _dot_nt(hb, wgu[cu, :])
            a_s[:, c] = a
            u_s[:, c] = u
            act = (_silu(a) * u).astype(BF16)
            act_ref[:, c] = act
            ff += _dot(act, wd[c, :])
        x2 = x1v + gate2 * ff
        nf, rstd_f = _rms(x2)
        gfv = gf_ref[...]
        err = nf * gfv - t_ref[...]
        loss_ref[...] += 0.5 * jnp.sum(_rowsum(err * err), axis=-1, keepdims=True) * (1.0 / D_MODEL)
        dy = err * (1.0 / D_MODEL)
        dx2 = _rms_bwd(dy * gfv, nf, rstd_f)
        dffb = (gate2 * dx2).astype(BF16)
        dff_ref[...] = dffb
        dh = jnp.zeros((tm, D_MODEL), F32)
        for j in range(n_chunks):
            c = slice(j * FFN_CHUNK, (j + 1) * FFN_CHUNK)
            cu = slice(D_FF + j * FFN_CHUNK, D_FF + (j + 1) * FFN_CHUNK)
            dact = _dot_nt(dffb, wd[c, :])
            a, u = a_s[:, c], u_s[:, c]
            da = (dact * u * _dsilu(a)).astype(BF16)
            du = (dact * _silu(a)).astype(BF16)
            dau_ref[:, c] = da
            dau_ref[:, cu] = du
            dh += _dot(da, wgu[c, :]) + _dot(du, wgu[cu, :])
        dn = dh * scale2
        sums_ref[0:1, :] += _rowsum(dh)
        sums_ref[1:2, :] += _rowsum(dh * n2)
        sums_ref[2:3, :] += _rowsum(dx2 * ff)
        sums_ref[3:4, :] += _rowsum(dn * xhat)
        sums_ref[4:5, :] += _rowsum(dy * nf)
        dx_ref[...] = dx2 + _rms_bwd(dn * g2, xhat, rstd)

    vec = _whole((1, D_MODEL))
    hbm = pl.BlockSpec(memory_space=pl.ANY)
    return pl.pallas_call(
        body, name="ffn", grid=(s // tm,),
        out_shape=[jax.ShapeDtypeStruct((s, D_MODEL), F32), jax.ShapeDtypeStruct((s, D_MODEL), BF16),
                   jax.ShapeDtypeStruct((s, D_FF), BF16), jax.ShapeDtypeStruct((s, 2 * D_FF), BF16),
                   jax.ShapeDtypeStruct((s, D_MODEL), BF16), jax.ShapeDtypeStruct((8, D_MODEL), F32),
                   jax.ShapeDtypeStruct((1, 128), F32)],
        in_specs=[_rows(tm, D_MODEL), _rows(tm, D_MODEL), _whole((1, 6 * D_MODEL)), vec, vec, hbm, hbm],
        out_specs=[_rows(tm, D_MODEL), _rows(tm, D_MODEL), _rows(tm, D_FF), _rows(tm, 2 * D_FF), _rows(tm, D_MODEL),
                   _whole((8, D_MODEL)), _whole((1, 128))],
        scratch_shapes=[pltpu.VMEM((2 * D_FF, D_MODEL), BF16), pltpu.VMEM((D_FF, D_MODEL), BF16),
                        pltpu.VMEM((tm, D_FF), F32), pltpu.VMEM((tm, D_FF), F32), pltpu.SemaphoreType.DMA((2,))],
        compiler_params=_params("arbitrary"),
    )(x1, target, mod, g2, gf, w_gu, w_down)


def _weight_grad(a, b, name, rounded=False):
    s, m = a.shape
    n = b.shape[1]
    ts = min(s, 2048)
    n_steps = s // ts
    tm = max(t for t in range(128, m + 1, 128) if m % t == 0 and t * n * 4 <= 6 * 1024 * 1024)

    def body(a_ref, b_ref, o_ref, *ob_ref):
        @pl.when(pl.program_id(1) == 0)
        def _():
            o_ref[...] = jnp.zeros_like(o_ref)

        o_ref[...] += _dot_tn(a_ref[...], b_ref[...])
        if rounded:
            @pl.when(pl.program_id(1) == n_steps - 1)
            def _():
                ob_ref[0][...] = o_ref[...].astype(BF16)

    tile = pl.BlockSpec((tm, n), lambda j, i: (j, 0))
    return pl.pallas_call(
        body, name=name, grid=(m // tm, n_steps),
        out_shape=[jax.ShapeDtypeStruct((m, n), F32)] + [jax.ShapeDtypeStruct((m, n), BF16)] * rounded,
        in_specs=[pl.BlockSpec((ts, tm), lambda j, i: (i, j)), pl.BlockSpec((ts, n), lambda j, i: (i, 0))],
        out_specs=[tile] + [tile] * rounded,
        compiler_params=_params("parallel", "arbitrary"),
    )(a, b)


def _adamw_math(w, g, m, v):
    m = ADAM_B1 * m + (1.0 - ADAM_B1) * g
    v = ADAM_B2 * v + (1.0 - ADAM_B2) * (g * g)
    m_hat = m / (1.0 - ADAM_B1 ** ADAM_STEP)
    v_hat = v / (1.0 - ADAM_B2 ** ADAM_STEP)
    delta = -ADAM_LR * (m_hat / (jnp.sqrt(v_hat) + ADAM_EPS) + ADAM_WD * w)
    return delta, m, v


def _adamw_shard(where, w, m, v, partial, got, name):
    r, c = w.shape
    tr = _shard_rows(r)
    lead = partial.ndim - 2
    n_got = got.shape[0]

    def body(where_ref, w_ref, m_ref, v_ref, own_ref, *rest):
        got_refs, (grad_ref, d_ref, nm_ref, nv_ref) = rest[:n_got], rest[n_got:]
        g = own_ref[...]
        for g_ref in got_refs:
            g = g + g_ref[...].astype(F32)
        grad_ref[...] = g
        d_ref[...], nm_ref[...], nv_ref[...] = _adamw_math(w_ref[...], g, m_ref[...], v_ref[...])

    tile = pl.BlockSpec((tr, c), lambda i, where_ref: (i, 0))
    own = pl.BlockSpec((None,) * lead + (tr, c), lambda i, where_ref: (*[where_ref[d] for d in range(lead)], i, 0))
    part = [pl.BlockSpec((None, tr, c), functools.partial(lambda j, i, where_ref: (j, i, 0), j)) for j in range(n_got)]
    return pl.pallas_call(
        body, name=name,
        grid_spec=pltpu.PrefetchScalarGridSpec(num_scalar_prefetch=1, grid=(r // tr,), in_specs=[tile] * 3 + [own] + part,
                                               out_specs=[tile] * 4),
        out_shape=[jax.ShapeDtypeStruct((r, c), F32)] * 4, compiler_params=_params("parallel"),
    )(where, w, m, v, partial, *[got] * n_got)


def _small_update(small_all, dmod_blocks, c_all, logits, w_ada, m_ada, v_ada, smalls):
    def body(sm_ref, dm_ref, c_ref, lg_ref, wa_ref, ma_ref, va_ref, *rest):
        ins, outs = rest[:21], rest[21:]
        _, me = _flip(0)
        tot = sm_ref[0:1, :]
        for i in range(1, N_DEV):
            tot = tot + sm_ref[i:i + 1, :]
        loss_ref = outs[0]
        loss_ref[...] = tot[:, SM_LOSS:SM_LOSS + 128]
        g_ada = lax.dot_general(_silu(c_ref[...]), dm_ref[me], (((0,), (0,)), ((), ())),
                                preferred_element_type=F32, precision=HIGHEST)
        outs[1][...] = g_ada
        outs[2][...], outs[3][...], outs[4][...] = _adamw_math(wa_ref[...], g_ada, ma_ref[...], va_ref[...])
        p0 = _lower_bound(lg_ref)
        dl0 = tot[:, SM_LB:SM_LB + 512] * p0 * (1.0 - p0)
        grads = [tot[:, SM_MOD:SM_MOD + 6 * D_MODEL], tot[:, SM_G1:SM_G1 + D_MODEL], tot[:, SM_G2:SM_G2 + D_MODEL],
                 tot[:, SM_GF:SM_GF + D_MODEL], tot[:, SM_ATT:SM_ATT + 512], tot[:, SM_HG:SM_HG + 128],
                 jnp.where(lax.broadcasted_iota(jnp.int32, (2, 512), 0) == 0, dl0, -dl0)]
        for i, g in enumerate(grads):
            w_ref, m_ref, v_ref = ins[3 * i:3 * i + 3]
            o = outs[5 + 4 * i:9 + 4 * i]
            o[0][...] = g
            o[1][...], o[2][...], o[3][...] = _adamw_math(w_ref[...], g, m_ref[...], v_ref[...])

    flat = [t for trio in smalls for t in trio]
    vm = pl.BlockSpec(memory_space=pltpu.VMEM)
    out_shape = [jax.ShapeDtypeStruct((1, 128), F32)] + [jax.ShapeDtypeStruct(w_ada.shape, F32)] * 4
    for trio in smalls:
        out_shape += [jax.ShapeDtypeStruct(trio[0].shape, F32)] * 4
    return pl.pallas_call(
        body, name="small_update", out_shape=out_shape,
        in_specs=[vm] * (7 + len(flat)), out_specs=[vm] * len(out_shape),
        compiler_params=pltpu.CompilerParams(vmem_limit_bytes=V7X_VMEM_LIMIT),
    )(small_all, dmod_blocks, c_all, logits, w_ada, m_ada, v_ada, *flat)


def kernel(x, c, w_ada, b_ada, norm1_g, w_in, hg_lb_logits, hg_onorm_g, att_onorm_g, w_out, norm2_g, w_gate_up, w_down, final_g, loss_target, m_w_ada, m_b_ada, m_norm1_g, m_w_in, m_hg_lb_logits, m_hg_onorm_g, m_att_onorm_g, m_w_out, m_norm2_g, m_w_gate_up, m_w_down, m_final_g, v_w_ada, v_b_ada, v_norm1_g, v_w_in, v_hg_lb_logits, v_hg_onorm_g, v_att_onorm_g, v_w_out, v_norm2_g, v_w_gate_up, v_w_down, v_final_g):
    x2d, target = x[0], loss_target[0]
    seq = x2d.shape[0]
    assert seq % (ATT_BLOCK * max(DILATIONS)) == 0 and seq % HG_TILE == 0
    gf = final_g.reshape(1, D_MODEL)

    c_all = _exchange_small(c.reshape(8, D_MODEL // 8), None, "gather_c").reshape(N_DEV, D_MODEL)
    ada = _ada_rows(c_all, w_ada[0], b_ada)
    mod = _exchange_small(ada, 1, "scatter_mod").reshape(1, 6 * D_MODEL)

    core = lax.axis_index("c").astype(jnp.int32).reshape(1)
    chip = (2 * lax.axis_index("x") + lax.axis_index("y")).astype(jnp.int32).reshape(1)
    me = 4 * lax.axis_index("x") + 2 * lax.axis_index("y") + lax.axis_index("c")

    g_in, = _gather_weights([w_in[0].T.astype(BF16)])
    w_in_b = g_in.reshape(IN_WIDTH, D_MODEL)
    rest_shards = [w_out[0].astype(BF16), w_gate_up[0].T.astype(BF16), w_down[0].astype(BF16)]
    lands = [lax.empty((N_DEV,) + s.shape, BF16) for s in rest_shards]
    g_send, g_recv, g_srcs, g_lands, tok = _copies_start("gather_rest_start", _plan_gather_own, 12, rest_shards, lands, [w_in_b, mod])
    flight = {}

    def stage(name, *vals):
        if name == "attention_begun":
            flight["shards"], got = _copies_wait("gather_rest_wait", _plan_gather_own, g_send, g_recv, g_srcs, g_lands, list(vals))
            flight["pass"] = _copies_start("gather_pass_start", _plan_gather_pass, 9, [], got, [])
            return flight["pass"][4][0:1, 0:1]
        if name == "mixer_weights_done":
            shapes = [(4, 2, D_MODEL // N_DEV, D_MODEL), (4, 2, 2 * D_FF // N_DEV, D_MODEL), (4, 2, D_FF // N_DEV, D_MODEL)]
            flight["grads"] = [g32.reshape(sh) for (g32, _), sh in zip(vals, shapes)]
            rounded = [g16.reshape(sh) for (_, g16), sh in zip(vals, shapes)]
            direct_lands = [lax.empty((N_DEV - 1,) + sh[2:], BF16) for sh in shapes]
            flight["direct"] = _copies_start("reduce_rest_start", _plan_reduce_direct, 21, rounded, direct_lands, [])
            return flight["direct"][4][0:1, 0:1]
        raise ValueError(name)

    def rest_weights(after):
        s, r, _, p_lands, _ = flight["pass"]
        _, got = _copies_wait("gather_pass_wait", _plan_gather_pass, s, r, [], p_lands, [after])
        full = [lax.dynamic_update_index_in_dim(g, shard, me, 0) for g, shard in zip(got, flight["shards"])]
        return full[0].reshape(D_MODEL, D_MODEL), full[1].reshape(2 * D_FF, D_MODEL), full[2].reshape(D_FF, D_MODEL)

    grad_x, dw_in, small = _block_step(x2d, target, mod + tok[0:1, 0:1], norm1_g, hg_lb_logits, hg_onorm_g, att_onorm_g, norm2_g, gf,
                                       w_in_b, rest_weights, stage)

    g_in8 = dw_in.reshape(4, 2, IN_WIDTH // N_DEV, D_MODEL)
    in_pairs = _copies_start("reduce_pairs_in_start", _plan_reduce_pairs, 4, [g_in8], [lax.empty((4,) + g_in8.shape[2:], F32)], [])
    s, r, srcs, d_lands, _ = flight["direct"]
    _, recv_rest = _copies_wait("reduce_rest_wait", _plan_reduce_direct, s, r, srcs, d_lands, [in_pairs[4]])
    small_rows = jnp.pad(small + in_pairs[4][0:1, 0:1], ((0, 0), (0, SM_PADDED - SM_WIDTH))).reshape(SM_PADDED // 128, 128)
    small_all = _exchange_small(small_rows, None, "gather_small").reshape(N_DEV, SM_PADDED)[:, :SM_WIDTH]
    in_grads, got_in = _copies_wait("reduce_pairs_in_wait", _plan_reduce_pairs, in_pairs[0], in_pairs[1], in_pairs[2], in_pairs[3],
                                    [small_all])
    in_s32, in_s16 = _pair_sum(core, in_grads[0], got_in[0], "pair_sum_in")
    in_chips = _copies_start("reduce_chips_in_start", _plan_reduce_chips, 3, [in_s16], [lax.empty((3,) + in_s16.shape[1:], BF16)], [])
    big, updated = {}, []
    rest_params = [("w_out", w_out, m_w_out, v_w_out), ("w_gate_up", w_gate_up, m_w_gate_up, v_w_gate_up), ("w_down", w_down, m_w_down, v_w_down)]
    mine = jnp.concatenate([chip, core])
    for (n, w, m, v), g32, got in zip(rest_params, flight["grads"], recv_rest):
        if n == "w_gate_up":
            outs4 = _adamw_shard(mine, w[0].T, m[0].T, v[0].T, g32, got, f"adamw_{n}")
            big[n] = [t.T[None] for t in outs4]
        else:
            outs4 = _adamw_shard(mine, w[0], m[0], v[0], g32, got, f"adamw_{n}")
            big[n] = [t[None] for t in outs4]
        updated.append(outs4[3])
    c_all = c_all + in_chips[4][0:1, 0:1]
    smalls = [(b_ada, m_b_ada, v_b_ada), (norm1_g, m_norm1_g, v_norm1_g), (norm2_g, m_norm2_g, v_norm2_g),
              (gf, m_final_g.reshape(1, D_MODEL), v_final_g.reshape(1, D_MODEL)),
              (att_onorm_g, m_att_onorm_g, v_att_onorm_g), (hg_onorm_g, m_hg_onorm_g, v_hg_onorm_g),
              (hg_lb_logits, m_hg_lb_logits, v_hg_lb_logits)]
    dmod_blocks = small_all[:, :6 * D_MODEL].reshape(N_DEV, N_DEV, 6 * D_MODEL // N_DEV).transpose(1, 0, 2)
    res = _small_update(small_all, dmod_blocks, c_all, hg_lb_logits, w_ada[0], m_w_ada[0], v_w_ada[0], smalls)
    _, recv_in = _copies_wait("reduce_chips_in_wait", _plan_reduce_chips, in_chips[0], in_chips[1], in_chips[2], in_chips[3],
                              [res[0]] + updated)
    big["w_in"] = [t.T[None] for t in _adamw_shard(chip, w_in[0].T, m_w_in[0].T, v_w_in[0].T, in_s32, recv_in[0], "adamw_w_in")]
    loss = res[0][0, 0]
    ada4 = [t[None] for t in res[1:5]]
    sm4 = {n: list(res[5 + 4 * i:9 + 4 * i]) for i, n in enumerate(["b_ada", "norm1_g", "norm2_g", "final_g", "att", "hg", "lb"])}
    sm4["final_g"] = [t.reshape(D_MODEL) for t in sm4["final_g"]]

    order = [ada4, sm4["b_ada"], sm4["norm1_g"], big["w_in"], sm4["lb"], sm4["hg"], sm4["att"], big["w_out"], sm4["norm2_g"],
             big["w_gate_up"], big["w_down"], sm4["final_g"]]
    return (loss, grad_x[None], *[o[0] for o in order], *[o[1] for o in order], *[o[2] for o in order], *[o[3] for o in order])


def _block_step(x2d, target, mod, norm1_g, hg_lb_logits, hg_onorm_g, att_onorm_g, norm2_g, gf, w_in_b, rest_weights, stage):
    h1, hq, hf, hi, hgt, aq, ak, av = _in_fwd(x2d, mod, norm1_g, w_in_b)
    hg_out, hg_o, hg_states = _hg_fwd(hq, hf, hi, hgt, hg_lb_logits, hg_onorm_g)
    branch = [_att_fwd(aq, ak, av, d) for d in DILATIONS[:2]]
    att_g = att_onorm_g + stage("attention_begun", branch[0][0], branch[1][0])
    branch += [_att_fwd(aq, ak, av, d) for d in DILATIONS[2:]]
    outs = [b[0] for b in branch]
    lses = [b[1] for b in branch]
    att, att_out = _att_combine(outs, lses, att_g)
    w_out_b, w_gu_b, w_down_b = rest_weights(att_out)
    x1 = _out_fwd(x2d, hg_out, att_out, mod, w_out_b)

    dx1, h2, act, dau, dff, ffn_sums, loss_part = _ffn(x1, target, mod, norm2_g, gf, w_gu_b, w_down_b)
    dw_gu = _weight_grad(dau, h2, "dw_gate_up", rounded=True)
    dw_down = _weight_grad(act, dff, "dw_down", rounded=True)

    dhg, dat, dw_out, dw_out_b, dgate1 = _out_bwd(dx1, hg_out, att_out, mod, w_out_b)
    att_g = att_onorm_g + stage("mixer_weights_done", (dw_out, dw_out_b), dw_gu, dw_down)
    comb = _att_combine_bwd(dat, att, lses, att_g)
    dos, ccs, d_att_g = comb[0:3], comb[3:6], comb[6]
    datt = []
    for i, d in enumerate(DILATIONS):
        datt.append(_att_bwd(aq, ak, av, dos[i], ccs[i], lses[i], d))
    dhq, dhf, dhi, dhgt, d_hg_g, d_lb = _hg_bwd(hq, hf, hi, hgt, hg_lb_logits, hg_onorm_g, hg_o, hg_states, dhg)
    dps = [dhq, dhf, dhi, dhgt] + [datt[i][j] for j in range(3) for i in range(3)]
    grad_x, dp_b, dshift1, dscale1, d_g1 = _in_bwd(x2d, dx1, mod, norm1_g, w_in_b, dps)
    dw_in, = _weight_grad(dp_b, h1, "dw_in")
    small = jnp.concatenate([dshift1, dscale1, dgate1, ffn_sums[0:1], ffn_sums[1:2], ffn_sums[2:3], d_g1, ffn_sums[3:4],
                             ffn_sums[4:5], d_att_g, d_lb, d_hg_g, loss_part], axis=1)
    return grad_x, dw_in, small
```

```python
import functools

import jax
import jax.numpy as jnp
from jax import lax
from jax.experimental import pallas as pl
from jax.experimental.pallas import tpu as pltpu

F32 = jnp.float32
BF16 = jnp.bfloat16
HIGHEST = lax.Precision.HIGHEST
MESH = pl.DeviceIdType.MESH

D_MODEL = 1024
N_DEV = 8
HG_HEADS = 4
HG_DIM = 128
HG_WIDTH = HG_HEADS * HG_DIM
HG_CHUNK = 128
ATT_WIDTH = 512
ATT_HEAD_DIM = 64
ATT_BLOCK = 128
DILATIONS = (1, 4, 16)
ATT_SCALE = ATT_HEAD_DIM ** -0.5
D_FF = 2816
IN_WIDTH = 7 * 512
RMS_EPS = 1e-6
NEG = -1e30

ADAM_LR = 0.001
ADAM_B1 = 0.9
ADAM_B2 = 0.999
ADAM_EPS = 1e-08
ADAM_WD = 0.01
ADAM_STEP = 10

V7X_VMEM_LIMIT = 56 * 1024 * 1024

SM_MOD = 0
SM_G1 = 6 * D_MODEL
SM_G2 = 7 * D_MODEL
SM_GF = 8 * D_MODEL
SM_ATT = 9 * D_MODEL
SM_LB = 9 * D_MODEL + 512
SM_HG = 10 * D_MODEL
SM_LOSS = 10 * D_MODEL + 128
SM_WIDTH = 10 * D_MODEL + 256
SM_PADDED = 88 * 128


def _params(*sem, vmem=V7X_VMEM_LIMIT):
    return pltpu.CompilerParams(dimension_semantics=sem, vmem_limit_bytes=vmem)


def _dot(a, b):
    return jnp.dot(a, b, preferred_element_type=F32)


def _dot_nt(a, b):
    return lax.dot_general(a, b, (((1,), (1,)), ((), ())), preferred_element_type=F32)


def _dot_tn(a, b):
    return lax.dot_general(a, b, (((0,), (0,)), ((), ())), preferred_element_type=F32)


def _dot_f32(a, b):
    return jnp.dot(a, b, preferred_element_type=F32, precision=HIGHEST)


def _sigmoid(x):
    return 1.0 / (1.0 + jnp.exp(-x))


def _silu(x):
    return x * _sigmoid(x)


def _dsilu(x):
    s = _sigmoid(x)
    return s * (1.0 + x * (1.0 - s))


def _rms(x):
    rstd = lax.rsqrt(jnp.mean(x * x, axis=-1, keepdims=True) + RMS_EPS)
    return x * rstd, rstd


def _rms_bwd(dn, xhat, rstd):
    return rstd * (dn - xhat * jnp.mean(dn * xhat, axis=-1, keepdims=True))


def _rowsum(x):
    return jnp.sum(x, axis=0, keepdims=True)


def _rows(tm, n):
    return pl.BlockSpec((tm, n), lambda i: (i, 0))


def _whole(shape):
    return pl.BlockSpec(shape, lambda i: (0,) * len(shape))


def _mesh_pos():
    return lax.axis_index("x"), lax.axis_index("y"), lax.axis_index("c")


def _flip(k):
    x, y, c = _mesh_pos()
    px = 1 - x if k & 4 else x
    py = 1 - y if k & 2 else y
    pc = 1 - c if k & 1 else c
    return (px, py, pc), 4 * px + 2 * py + pc


def _exchange_small(x, rows_per_peer, name):
    r_all, cols = x.shape
    r_out = r_all if rows_per_peer is None else rows_per_peer

    def body(x_ref, out_ref, send_sems, recv_sems):
        _, me = _flip(0)

        def src(pid):
            if rows_per_peer is None:
                return x_ref
            return x_ref.at[pl.ds(pl.multiple_of(pid * r_out, r_out), r_out), :]

        if rows_per_peer is None:
            out_ref[me] = x_ref[...]
        else:
            out_ref[me] = x_ref[pl.ds(pl.multiple_of(me * r_out, r_out), r_out), :]
        sends = []
        for k in range(1, N_DEV):
            dev, pid = _flip(k)
            cp = pltpu.make_async_remote_copy(src_ref=src(pid), dst_ref=out_ref.at[me], send_sem=send_sems.at[k - 1],
                                              recv_sem=recv_sems.at[k - 1], device_id=dev, device_id_type=MESH)
            cp.start()
            sends.append(cp)
        for k in range(1, N_DEV):
            dev, pid = _flip(k)
            pltpu.make_async_remote_copy(src_ref=src(pid), dst_ref=out_ref.at[pid], send_sem=send_sems.at[k - 1],
                                         recv_sem=recv_sems.at[k - 1], device_id=dev, device_id_type=MESH).wait_recv()
        for cp in sends:
            cp.wait_send()

    return pl.pallas_call(
        body, name=name,
        out_shape=jax.ShapeDtypeStruct((N_DEV, r_out, cols), x.dtype),
        in_specs=[pl.BlockSpec(memory_space=pltpu.VMEM)],
        out_specs=pl.BlockSpec(memory_space=pltpu.VMEM),
        scratch_shapes=[pltpu.SemaphoreType.DMA((N_DEV - 1,)), pltpu.SemaphoreType.DMA((N_DEV - 1,))],
    )(x)


def _gather_weights(shards):
    n = len(shards)

    def body(*refs):
        xs, outs = refs[:n], refs[n:2 * n]
        send_sems, recv_sems, local_sems = refs[2 * n:]
        x, y, c = _mesh_pos()
        me, sibling = (x, y, c), (x, y, 1 - c)
        chips = [(1 - x, y), (x, 1 - y), (1 - x, 1 - y)]

        def blk(a, px, py, pc):
            return outs[a].at[4 * px + 2 * py + pc]

        def copy(a, k, block, to, src=None):
            return pltpu.make_async_remote_copy(
                src_ref=blk(a, *block) if src is None else src, dst_ref=blk(a, *block),
                send_sem=send_sems.at[a * 7 + k], recv_sem=recv_sems.at[a * 7 + k], device_id=to, device_id_type=MESH)

        mine = [pltpu.make_async_copy(xs[a], blk(a, *me), local_sems.at[a]) for a in range(n)]
        for cp in mine:
            cp.start()
        first = []
        for a in range(n):
            first.append(copy(a, 0, me, sibling, src=xs[a]))
            first += [copy(a, 1 + j, me, (*chip, c), src=xs[a]) for j, chip in enumerate(chips)]
        for cp in first:
            cp.start()
        passed = []
        for j, chip in enumerate(chips):
            for a in range(n):
                copy(a, 1 + j, (*chip, c), me).wait_recv()
                cp = copy(a, 4 + j, (*chip, c), sibling)
                cp.start()
                passed.append(cp)
        for a in range(n):
            copy(a, 0, sibling, me).wait_recv()
            for j, chip in enumerate(chips):
                copy(a, 4 + j, (*chip, 1 - c), me).wait_recv()
        for cp in first + passed:
            cp.wait_send()
        for cp in mine:
            cp.wait()

    hbm = pl.BlockSpec(memory_space=pl.ANY)
    return pl.pallas_call(
        body, name="gather_weights",
        out_shape=[jax.ShapeDtypeStruct((N_DEV,) + s.shape, s.dtype) for s in shards],
        in_specs=[hbm] * n, out_specs=[hbm] * n,
        scratch_shapes=[pltpu.SemaphoreType.DMA((7 * n,)), pltpu.SemaphoreType.DMA((7 * n,)), pltpu.SemaphoreType.DMA((n,))],
    )(*shards)


_HBM = pl.BlockSpec(memory_space=pltpu.HBM)
_SEM = pl.BlockSpec(memory_space=pltpu.SEMAPHORE)
_DATAFLOW = pltpu.SideEffectType.DATAFLOW_SIDE_EFFECTING


def _copies_start(name, plan, n_copies, srcs, lands, after):
    bufs = list(srcs) + list(lands)
    nb = len(bufs)

    def body(*refs):
        ins, send_sems, recv_sems, token = refs[:nb], refs[nb + len(after)], refs[nb + len(after) + 1], refs[-1]
        for i, (src, dst, dev) in enumerate(plan(ins[:len(srcs)], ins[len(srcs):])):
            pltpu.make_async_remote_copy(src_ref=src, dst_ref=dst, send_sem=send_sems.at[i], recv_sem=recv_sems.at[i],
                                         device_id=dev, device_id_type=MESH).start()
        token[...] = jnp.zeros_like(token)

    outs = pl.pallas_call(
        body, name=name,
        out_shape=(pltpu.SemaphoreType.DMA((n_copies,)), pltpu.SemaphoreType.DMA((n_copies,)),
                   *[pltpu.HBM(b.shape, b.dtype) for b in bufs], jax.ShapeDtypeStruct((8, 128), F32)),
        in_specs=[_HBM] * nb + [pl.BlockSpec(memory_space=pl.ANY)] * len(after),
        out_specs=(_SEM, _SEM, *[_HBM] * nb, pl.BlockSpec(memory_space=pltpu.VMEM)),
        input_output_aliases={i: 2 + i for i in range(nb)},
        compiler_params=pltpu.CompilerParams(has_side_effects=_DATAFLOW),
    )(*[pltpu.with_memory_space_constraint(b, pltpu.HBM) for b in bufs], *after)
    return outs[0], outs[1], list(outs[2:2 + len(srcs)]), list(outs[2 + len(srcs):2 + nb]), outs[-1]


def _copies_wait(name, plan, send_sems, recv_sems, srcs, lands, after):
    bufs = list(srcs) + list(lands)
    nb = len(bufs)

    def body(*refs):
        ins, send_ref, recv_ref = refs[:nb], refs[nb], refs[nb + 1]
        for i, (src, dst, dev) in enumerate(plan(ins[:len(srcs)], ins[len(srcs):])):
            cp = pltpu.make_async_remote_copy(src_ref=src, dst_ref=dst, send_sem=send_ref.at[i], recv_sem=recv_ref.at[i],
                                              device_id=dev, device_id_type=MESH)
            cp.wait_send()
            cp.wait_recv()

    outs = pl.pallas_call(
        body, name=name, out_shape=[pltpu.HBM(b.shape, b.dtype) for b in bufs],
        in_specs=[_HBM] * nb + [_SEM, _SEM] + [pl.BlockSpec(memory_space=pl.ANY)] * len(after), out_specs=[_HBM] * nb,
        input_output_aliases={i: i for i in range(nb)},
        compiler_params=pltpu.CompilerParams(has_side_effects=_DATAFLOW),
    )(*bufs, send_sems, recv_sems, *after)
    return list(outs[:len(srcs)]), list(outs[len(srcs):])


def _plan_gather_own(srcs, lands):
    _, me = _flip(0)
    return [(srcs[a], lands[a].at[me], _flip(k)[0]) for a in range(len(srcs)) for k in (1, 4, 2, 6)]


def _plan_gather_pass(srcs, lands):
    sibling = _flip(1)[0]
    plan = []
    for land in lands:
        for k in (4, 2, 6):
            block = land.at[_flip(k)[1]]
            plan.append((block, block, sibling))
    return plan


def _plan_reduce_pairs(srcs, lands):
    x, y, c = _mesh_pos()
    return [(srcs[a].at[chip, 1 - c], lands[a].at[chip], (x, y, 1 - c)) for a in range(len(srcs)) for chip in range(4)]


def _plan_reduce_chips(srcs, lands):
    plan = []
    for a in range(len(srcs)):
        for j, k in enumerate((4, 2, 6)):
            dev = _flip(k)[0]
            plan.append((srcs[a].at[2 * dev[0] + dev[1]], lands[a].at[j], dev))
    return plan


def _plan_reduce_direct(srcs, lands):
    plan = []
    for a in range(len(srcs)):
        for k in range(1, N_DEV):
            dev = _flip(k)[0]
            plan.append((srcs[a].at[2 * dev[0] + dev[1], dev[2]], lands[a].at[k - 1], dev))
    return plan


def _shard_rows(r):
    return r // 2 if r % 32 == 0 else r


def _pair_sum(core, grads, got, name):
    _, _, r, c = grads.shape
    tr = _shard_rows(r)

    def body(core_ref, a_ref, b_ref, o_ref, ob_ref):
        s = a_ref[...] + b_ref[...]
        o_ref[...] = s
        ob_ref[...] = s.astype(BF16)

    spec = pl.BlockSpec((None, tr, c), lambda i, j, core_ref: (i, j, 0))
    return pl.pallas_call(
        body, name=name,
        grid_spec=pltpu.PrefetchScalarGridSpec(
            num_scalar_prefetch=1, grid=(4, r // tr),
            in_specs=[pl.BlockSpec((None, None, tr, c), lambda i, j, core_ref: (i, core_ref[0], j, 0)), spec],
            out_specs=[spec, spec]),
        out_shape=[jax.ShapeDtypeStruct((4, r, c), F32), jax.ShapeDtypeStruct((4, r, c), BF16)],
        compiler_params=_params("parallel", "parallel"),
    )(core, grads, got)


def _ada_rows(c_all, w_ada, b_ada):
    n_cols = w_ada.shape[1]

    def body(c_ref, w_ref, b_ref, o_ref):
        _, me = _flip(0)
        bias = b_ref[:, pl.ds(pl.multiple_of(me * n_cols, 128), n_cols)]
        o_ref[...] = _dot_f32(_silu(c_ref[...]), w_ref[...]) + bias

    return pl.pallas_call(
        body, name="ada_rows", out_shape=jax.ShapeDtypeStruct((N_DEV, n_cols), F32),
        in_specs=[pl.BlockSpec(memory_space=pltpu.VMEM)] * 3, out_specs=pl.BlockSpec(memory_space=pltpu.VMEM),
    )(c_all, w_ada, b_ada)


def _in_fwd(x, mod, g1, w_in):
    s = x.shape[0]
    tm = 256

    def body(x_ref, mod_ref, g_ref, w_ref, h_ref, *outs):
        xhat, _ = _rms(x_ref[...])
        h = (xhat * g_ref[...]) * (1.0 + mod_ref[:, D_MODEL:2 * D_MODEL]) + mod_ref[:, 0:D_MODEL]
        hb = h.astype(BF16)
        h_ref[...] = hb
        for j, o_ref in enumerate(outs):
            o_ref[...] = _dot_nt(hb, w_ref[j * 512:(j + 1) * 512, :])

    return pl.pallas_call(
        body, name="in_fwd", grid=(s // tm,),
        out_shape=[jax.ShapeDtypeStruct((s, D_MODEL), BF16)] + [jax.ShapeDtypeStruct((s, 512), F32)] * 7,
        in_specs=[_rows(tm, D_MODEL), _whole((1, 6 * D_MODEL)), _whole((1, D_MODEL)), _whole((IN_WIDTH, D_MODEL))],
        out_specs=[_rows(tm, D_MODEL)] + [_rows(tm, 512)] * 7,
        compiler_params=_params("parallel"),
    )(x, mod, g1, w_in)


def _in_bwd(x, dx1, mod, g1, w_in, dps):
    s = x.shape[0]
    tm = 256

    def body(x_ref, dx_ref, mod_ref, g_ref, w_ref, *rest):
        dp_refs, (gx_ref, dpb_ref, dsh_ref, dsc_ref, dg_ref) = rest[:13], rest[13:]
        pieces = [dp_refs[j][...] for j in range(4)]
        pieces += [dp_refs[4 + 3 * j][...] + dp_refs[5 + 3 * j][...] + dp_refs[6 + 3 * j][...] for j in range(3)]
        for j, p in enumerate(pieces):
            dpb_ref[:, j * 512:(j + 1) * 512] = p.astype(BF16)
        dh = _dot(dpb_ref[...], w_ref[...])
        xhat, rstd = _rms(x_ref[...])
        g = g_ref[...]
        scale1 = 1.0 + mod_ref[:, D_MODEL:2 * D_MODEL]
        n1 = xhat * g

        @pl.when(pl.program_id(0) == 0)
        def _():
            dsh_ref[...] = jnp.zeros_like(dsh_ref)
            dsc_ref[...] = jnp.zeros_like(dsc_ref)
            dg_ref[...] = jnp.zeros_like(dg_ref)

        dsh_ref[...] += _rowsum(dh)
        dsc_ref[...] += _rowsum(dh * n1)
        dn = dh * scale1
        dg_ref[...] += _rowsum(dn * xhat)
        gx_ref[...] = dx_ref[...] + _rms_bwd(dn * g, xhat, rstd)

    vec = _whole((1, D_MODEL))
    return pl.pallas_call(
        body, name="in_bwd", grid=(s // tm,),
        out_shape=[jax.ShapeDtypeStruct((s, D_MODEL), F32), jax.ShapeDtypeStruct((s, IN_WIDTH), BF16)]
        + [jax.ShapeDtypeStruct((1, D_MODEL), F32)] * 3,
        in_specs=[_rows(tm, D_MODEL), _rows(tm, D_MODEL), _whole((1, 6 * D_MODEL)), vec, _whole((IN_WIDTH, D_MODEL))]
        + [_rows(tm, 512)] * 13,
        out_specs=[_rows(tm, D_MODEL), _rows(tm, IN_WIDTH), vec, vec, vec],
        compiler_params=_params("arbitrary"),
    )(x, dx1, mod, g1, w_in, *dps)


HG_TILE = 512
HG_TILE_CHUNKS = HG_TILE // HG_CHUNK


def _lower_bound(lg_ref):
    return 1.0 / (1.0 + jnp.exp(lg_ref[1:2, :] - lg_ref[0:1, :]))


def _chunk_masks():
    r = lax.broadcasted_iota(jnp.int32, (HG_CHUNK, HG_CHUNK), 0)
    c = lax.broadcasted_iota(jnp.int32, (HG_CHUNK, HG_CHUNK), 1)
    return r >= c, c >= r, (r >= c).astype(F32), (c >= r).astype(F32)


def _hg_fwd(hq, hf, hi, hgt, logits, onorm_g):
    s = hq.shape[0]
    n_tiles = s // HG_TILE

    def body(q_ref, f_ref, i_ref, g_ref, lg_ref, og_ref, out_ref, o_ref, st_ref, state, qf_s, kk_s, lf_s):
        @pl.when(pl.program_id(0) == 0)
        def _():
            state[...] = jnp.zeros_like(state)

        lb = _lower_bound(lg_ref)
        f = lb + (1.0 - lb) * _sigmoid(f_ref[...])
        kk_s[...] = 1.0 - f
        lf_s[...] = jnp.log(f)
        qf_s[...] = _silu(q_ref[...])
        causal, _, tri, _ = _chunk_masks()

        def chunk(ci, carry):
            rows = pl.ds(pl.multiple_of(ci * HG_CHUNK, HG_CHUNK), HG_CHUNK)
            srows = pl.ds(pl.multiple_of(ci * HG_DIM, HG_DIM), HG_DIM)
            lf = lf_s[rows, :]
            b = _dot_f32(tri, lf)
            bl = _rowsum(lf)
            ref = 0.5 * bl
            qf, kk, v = qf_s[rows, :], kk_s[rows, :], i_ref[rows, :]
            a_in = (qf * jnp.exp(b)).astype(BF16)
            a_t = (qf * jnp.exp(b - ref)).astype(BF16)
            b_t = (kk * jnp.exp(ref - b)).astype(BF16)
            kd = kk * jnp.exp(bl - b)
            ebl = jnp.exp(bl)
            vb = v.astype(BF16)
            for h in range(HG_HEADS):
                c = slice(h * HG_DIM, (h + 1) * HG_DIM)
                st = state[h]
                st_ref[srows, c] = st
                p = jnp.where(causal, _dot_nt(a_t[:, c], b_t[:, c]), 0.0)
                o_ref[rows, c] = _dot(p.astype(BF16), vb[:, c]) + _dot_nt(a_in[:, c], st.astype(BF16))
                state[h] = st * ebl[:, c] + _dot_tn(vb[:, c], kd[:, c].astype(BF16))
            return carry

        lax.fori_loop(0, HG_TILE_CHUNKS, chunk, 0, unroll=True)
        for h in range(HG_HEADS):
            c = slice(h * HG_DIM, (h + 1) * HG_DIM)
            ohat, _ = _rms(o_ref[:, c])
            out_ref[:, c] = (ohat * og_ref[...] * _silu(g_ref[:, c])).astype(BF16)

    tile = _rows(HG_TILE, HG_WIDTH)
    return pl.pallas_call(
        body, name="hg_fwd", grid=(n_tiles,),
        out_shape=[jax.ShapeDtypeStruct((s, HG_WIDTH), BF16), jax.ShapeDtypeStruct((s, HG_WIDTH), F32),
                   jax.ShapeDtypeStruct((s // HG_CHUNK * HG_DIM, HG_WIDTH), F32)],
        in_specs=[tile] * 4 + [_whole((2, HG_WIDTH)), _whole((1, HG_DIM))],
        out_specs=[tile, tile, _rows(HG_TILE_CHUNKS * HG_DIM, HG_WIDTH)],
        scratch_shapes=[pltpu.VMEM((HG_HEADS, HG_DIM, HG_DIM), F32)] + [pltpu.VMEM((HG_TILE, HG_WIDTH), F32)] * 3,
        compiler_params=_params("arbitrary"),
    )(hq, hf, hi, hgt, logits, onorm_g)


def _hg_bwd(hq, hf, hi, hgt, logits, onorm_g, o, states, dout):
    s = hq.shape[0]
    n_tiles = s // HG_TILE

    def body(q_ref, f_ref, i_ref, g_ref, lg_ref, og_ref, o_ref, st_ref, d_ref,
             dq_ref, df_ref, di_ref, dg_ref, dog_ref, dlb_ref, dstate, qf_s, kk_s, lf_s, do_s):
        @pl.when(pl.program_id(0) == 0)
        def _():
            dstate[...] = jnp.zeros_like(dstate)
            dog_ref[...] = jnp.zeros_like(dog_ref)
            dlb_ref[...] = jnp.zeros_like(dlb_ref)

        og = og_ref[...]
        dog = jnp.zeros((1, HG_DIM), F32)
        for h in range(HG_HEADS):
            c = slice(h * HG_DIM, (h + 1) * HG_DIM)
            ohat, rstd = _rms(o_ref[:, c])
            gate = g_ref[:, c]
            d = d_ref[:, c]
            dg_ref[:, c] = (d * (ohat * og) * _dsilu(gate)).astype(BF16)
            dnormed = d * _silu(gate)
            dog += _rowsum(dnormed * ohat)
            do_s[:, c] = _rms_bwd(dnormed * og, ohat, rstd)
        dog_ref[...] += dog

        lb = _lower_bound(lg_ref)
        f = lb + (1.0 - lb) * _sigmoid(f_ref[...])
        kk_s[...] = 1.0 - f
        lf_s[...] = jnp.log(f)
        qf_s[...] = _silu(q_ref[...])
        causal, upper, tri, tri_t = _chunk_masks()

        def chunk(step, carry):
            ci = HG_TILE_CHUNKS - 1 - step
            rows = pl.ds(pl.multiple_of(ci * HG_CHUNK, HG_CHUNK), HG_CHUNK)
            srows = pl.ds(pl.multiple_of(ci * HG_DIM, HG_DIM), HG_DIM)
            lf = lf_s[rows, :]
            b = _dot_f32(tri, lf)
            bl = _rowsum(lf)
            ref = 0.5 * bl
            qf, kk, v, do = qf_s[rows, :], kk_s[rows, :], i_ref[rows, :], do_s[rows, :]
            eb, ebr, erb, ekd, ebl = jnp.exp(b), jnp.exp(b - ref), jnp.exp(ref - b), jnp.exp(bl - b), jnp.exp(bl)
            a_in, a_t, b_t, kd = qf * eb, qf * ebr, kk * erb, kk * ekd
            for h in range(HG_HEADS):
                c = slice(h * HG_DIM, (h + 1) * HG_DIM)
                st, dst = st_ref[srows, c], dstate[h]
                stb, dstb = st.astype(BF16), dst.astype(BF16)
                doh, vh = do[:, c], v[:, c]
                dob, vb = doh.astype(BF16), vh.astype(BF16)
                ain_h, at_h, bt_h, kd_h = a_in[:, c], a_t[:, c], b_t[:, c], kd[:, c]
                atb, btb = at_h.astype(BF16), bt_h.astype(BF16)
                d_ain = _dot(dob, stb)
                p_t = jnp.where(upper, _dot_nt(btb, atb), 0.0).astype(BF16)
                dp = jnp.where(causal, _dot_nt(dob, vb), 0.0).astype(BF16)
                dp_t = jnp.where(upper, _dot_nt(vb, dob), 0.0).astype(BF16)
                di_ref[rows, c] = (_dot(p_t, dob) + _dot_nt(kd_h.astype(BF16), dstb)).astype(BF16)
                d_at = _dot(dp, btb)
                d_bt = _dot(dp_t, atb)
                d_kd = _dot(vb, dstb)
                dqf = d_ain * eb[:, c] + d_at * ebr[:, c]
                dkk = d_bt * erb[:, c] + d_kd * ekd[:, c]
                db = d_ain * ain_h + d_at * atb.astype(F32) - d_bt * btb.astype(F32) - d_kd * kd_h
                dbl = _rowsum(d_kd * kd_h) + _rowsum(dst * st) * ebl[:, c]
                dstate[h] = _dot_tn(dob, ain_h.astype(BF16)) + dst * ebl[:, c]
                dlf = _dot_f32(tri_t, db) + dbl
                qv, fr = q_ref[rows, c], f_ref[rows, c]
                lbh = lb[:, c]
                sg = _sigmoid(fr)
                dfv = dlf / (lbh + (1.0 - lbh) * sg) - dkk
                df_ref[rows, c] = (dfv * (1.0 - lbh) * sg * (1.0 - sg)).astype(BF16)
                dlb_ref[:, c] += _rowsum(dfv * (1.0 - sg))
                dq_ref[rows, c] = (dqf * _dsilu(qv)).astype(BF16)
            return carry

        lax.fori_loop(0, HG_TILE_CHUNKS, chunk, 0, unroll=True)

    rev = pl.BlockSpec((HG_TILE, HG_WIDTH), lambda i: (n_tiles - 1 - i, 0))
    return pl.pallas_call(
        body, name="hg_bwd", grid=(n_tiles,),
        out_shape=[jax.ShapeDtypeStruct((s, HG_WIDTH), BF16)] * 4
        + [jax.ShapeDtypeStruct((1, HG_DIM), F32), jax.ShapeDtypeStruct((1, HG_WIDTH), F32)],
        in_specs=[rev] * 4 + [_whole((2, HG_WIDTH)), _whole((1, HG_DIM)), rev,
                              pl.BlockSpec((HG_TILE_CHUNKS * HG_DIM, HG_WIDTH), lambda i: (n_tiles - 1 - i, 0)), rev],
        out_specs=[rev] * 4 + [_whole((1, HG_DIM)), _whole((1, HG_WIDTH))],
        scratch_shapes=[pltpu.VMEM((HG_HEADS, HG_DIM, HG_DIM), F32)] + [pltpu.VMEM((HG_TILE, HG_WIDTH), F32)] * 4,
        compiler_params=_params("arbitrary"),
    )(hq, hf, hi, hgt, logits, onorm_g, o, states, dout)


TOKEN_GROUP = 16


def _att_geometry(dil):
    per_group = TOKEN_GROUP // dil
    return per_group, ATT_BLOCK // per_group, ATT_WIDTH if dil == 1 else 128, 1 if dil == TOKEN_GROUP else 4


def _att_consts(dil):
    per_group, ub = _att_geometry(dil)[:2]

    def pos(i):
        return i if dil == 1 else (i % ub) * per_group + i // ub

    lane = lax.broadcasted_iota(jnp.int32, (ATT_BLOCK, 128), 1)
    qi = pos(lax.broadcasted_iota(jnp.int32, (2 * ATT_BLOCK, ATT_BLOCK), 0) % ATT_BLOCK)
    kj = pos(lax.broadcasted_iota(jnp.int32, (2 * ATT_BLOCK, ATT_BLOCK), 1))
    return lane < ATT_HEAD_DIM, kj <= qi, lambda off: kj >= qi + off


def _load_tile(ref, dil, r, c, base=0):
    per_group, ub = _att_geometry(dil)[:2]
    if dil == 1:
        return ref[base:base + ATT_BLOCK, c]
    return jnp.concatenate([ref[pl.ds(base + dil * w + r, ub, stride=TOKEN_GROUP), c] for w in range(per_group)], axis=0)


def _store_tile(ref, dil, r, c, val, base=0):
    per_group, ub = _att_geometry(dil)[:2]
    if dil == 1:
        ref[base:base + ATT_BLOCK, c] = val
        return
    for w in range(per_group):
        ref[pl.ds(base + dil * w + r, ub, stride=TOKEN_GROUP), c] = val[w * ub:(w + 1) * ub]


def _stack_heads(x2, first):
    return jnp.concatenate([jnp.where(first, x2, 0.0), jnp.where(first, 0.0, x2)], axis=0)


def _stack_bcast(x2, first):
    other = pltpu.roll(x2, ATT_HEAD_DIM, axis=1)
    return jnp.concatenate([jnp.where(first, x2, other), jnp.where(first, other, x2)], axis=0)


def _unstack_heads(st, first):
    return jnp.where(first, st[:ATT_BLOCK], st[ATT_BLOCK:])


def _att_fwd(q, k, v, dil):
    seq, width = q.shape
    _, ub, lanes, nbs = _att_geometry(dil)
    rows = ub * TOKEN_GROUP
    n_steps = seq // (nbs * rows)

    def body(q_ref, k_ref, v_ref, kp_ref, vp_ref, o_ref, lse_ref):
        first, cur_ok, _band = _att_consts(dil)
        inner_ok = _band(0)
        edge_ok = _band(jnp.where(pl.program_id(0) > 0, 0, ATT_BLOCK))
        for b in range(nbs):
            base = b * rows
            prev_ok = edge_ok if b == 0 else inner_ok
            for r in range(dil):
                for j in range(lanes // 128):
                    c = slice(j * 128, (j + 1) * 128)
                    qst = _stack_heads(_load_tile(q_ref, dil, r, c, base) * ATT_SCALE, first).astype(BF16)
                    kc = _load_tile(k_ref, dil, r, c, base).astype(BF16)
                    vc = _load_tile(v_ref, dil, r, c, base).astype(BF16)
                    if b == 0:
                        kp, vp = _load_tile(kp_ref, dil, r, c).astype(BF16), _load_tile(vp_ref, dil, r, c).astype(BF16)
                    else:
                        kp = _load_tile(k_ref, dil, r, c, base - rows).astype(BF16)
                        vp = _load_tile(v_ref, dil, r, c, base - rows).astype(BF16)
                    sc = jnp.where(cur_ok, _dot_nt(qst, kc), NEG)
                    sp = jnp.where(prev_ok, _dot_nt(qst, kp), NEG)
                    mx = jnp.maximum(jnp.max(sc, axis=-1, keepdims=True), jnp.max(sp, axis=-1, keepdims=True))
                    pc, pp = jnp.exp(sc - mx), jnp.exp(sp - mx)
                    den = jnp.sum(pc, axis=-1, keepdims=True) + jnp.sum(pp, axis=-1, keepdims=True)
                    ost = (_dot(pc.astype(BF16), vc) + _dot(pp.astype(BF16), vp)) / den
                    lse = jnp.broadcast_to(mx + jnp.log(den), (2 * ATT_BLOCK, 128))
                    _store_tile(o_ref, dil, r, c, _unstack_heads(ost, first), base)
                    _store_tile(lse_ref, dil, r, c, _unstack_heads(lse, first), base)

    slab = pl.BlockSpec((nbs * rows, lanes), lambda n, j: (n, j))
    before = pl.BlockSpec((rows, lanes), lambda n, j: (jnp.maximum(n * nbs - 1, 0), j))
    return pl.pallas_call(
        body, name=f"att_fwd_d{dil}", grid=(n_steps, width // lanes),
        out_shape=[jax.ShapeDtypeStruct((seq, width), F32)] * 2,
        in_specs=[slab, slab, slab, before, before], out_specs=[slab, slab],
        compiler_params=_params("arbitrary", "arbitrary"),
    )(q, k, v, k, v)


def _att_bwd(q, k, v, do, cc, lse, dil):
    seq, width = q.shape
    _, ub, lanes, nbs = _att_geometry(dil)
    rows = ub * TOKEN_GROUP
    n_blocks = seq // rows
    n_steps = n_blocks // nbs

    def body(q_ref, k_ref, v_ref, do_ref, cc_ref, lse_ref, qx_ref, dox_ref, ccx_ref, lsex_ref,
             dq_ref, dk_ref, dv_ref, carry):
        first, cur_ok, _band = _att_consts(dil)
        step = pl.program_id(1)
        inner_ok = _band(0)
        edge_ok = _band(jnp.where(step < n_steps - 1, 0, ATT_BLOCK))

        @pl.when(step == 0)
        def _():
            carry[...] = jnp.zeros_like(carry)

        for b in range(nbs):
            base = b * rows
            last = b == nbs - 1
            next_ok = edge_ok if last else inner_ok
            for r in range(dil):
                for j in range(lanes // 128):
                    c = slice(j * 128, (j + 1) * 128)

                    def following(inner_ref, edge_ref):
                        return _load_tile(edge_ref, dil, r, c) if last else _load_tile(inner_ref, dil, r, c, base + rows)

                    qst = _stack_heads(_load_tile(q_ref, dil, r, c, base) * ATT_SCALE, first).astype(BF16)
                    qxst = _stack_heads(following(q_ref, qx_ref) * ATT_SCALE, first).astype(BF16)
                    dost = _stack_heads(_load_tile(do_ref, dil, r, c, base), first).astype(BF16)
                    doxst = _stack_heads(following(do_ref, dox_ref), first).astype(BF16)
                    lse_n = _stack_bcast(_load_tile(lse_ref, dil, r, c, base), first)
                    lse_x = _stack_bcast(following(lse_ref, lsex_ref), first)
                    cc_n = _stack_bcast(_load_tile(cc_ref, dil, r, c, base), first)
                    cc_x = _stack_bcast(following(cc_ref, ccx_ref), first)
                    kb = _load_tile(k_ref, dil, r, c, base).astype(BF16)
                    vb = _load_tile(v_ref, dil, r, c, base).astype(BF16)
                    p_cur = jnp.exp(jnp.where(cur_ok, _dot_nt(qst, kb), NEG) - lse_n)
                    p_next = jnp.exp(jnp.where(next_ok, _dot_nt(qxst, kb), NEG) - lse_x)
                    ds_cur = (p_cur * (_dot_nt(dost, vb) + cc_n)).astype(BF16)
                    ds_next = (p_next * (_dot_nt(doxst, vb) + cc_x)).astype(BF16)
                    dq_own = _load_tile(carry, dil, r, c) + _unstack_heads(_dot(ds_cur, kb), first)
                    _store_tile(dq_ref, dil, r, c, dq_own * ATT_SCALE, base)
                    _store_tile(carry, dil, r, c, _unstack_heads(_dot(ds_next, kb), first))
                    _store_tile(dk_ref, dil, r, c, _dot_tn(ds_cur, qst) + _dot_tn(ds_next, qxst), base)
                    _store_tile(dv_ref, dil, r, c, _dot_tn(p_cur.astype(BF16), dost) + _dot_tn(p_next.astype(BF16), doxst), base)

    slab = pl.BlockSpec((nbs * rows, lanes), lambda j, n: (n, j))
    after = pl.BlockSpec((rows, lanes), lambda j, n: (jnp.minimum((n + 1) * nbs, n_blocks - 1), j))
    return pl.pallas_call(
        body, name=f"att_bwd_d{dil}", grid=(width // lanes, n_steps),
        out_shape=[jax.ShapeDtypeStruct((seq, width), F32)] * 3,
        in_specs=[slab] * 6 + [after] * 4, out_specs=[slab] * 3,
        scratch_shapes=[pltpu.VMEM((rows, lanes), F32)],
        compiler_params=_params("arbitrary", "arbitrary"),
    )(q, k, v, do, cc, lse, q, do, cc, lse)


def _branch_weights(lses):
    mx = jnp.maximum(jnp.maximum(lses[0], lses[1]), lses[2])
    es = [jnp.exp(l - mx) for l in lses]
    inv = 1.0 / (es[0] + es[1] + es[2])
    return [e * inv for e in es]


def _att_combine(outs, lses, att_g):
    s = outs[0].shape[0]
    tm = 512

    def body(o0, o1, o2, l0, l1, l2, g_ref, att_ref, out_ref):
        ws = _branch_weights([l0[...], l1[...], l2[...]])
        att = ws[0] * o0[...] + ws[1] * o1[...] + ws[2] * o2[...]
        att_ref[...] = att
        ahat, _ = _rms(att)
        out_ref[...] = (ahat * g_ref[...]).astype(BF16)

    tile = _rows(tm, ATT_WIDTH)
    return pl.pallas_call(
        body, name="att_combine", grid=(s // tm,),
        out_shape=[jax.ShapeDtypeStruct((s, ATT_WIDTH), F32), jax.ShapeDtypeStruct((s, ATT_WIDTH), BF16)],
        in_specs=[tile] * 6 + [_whole((1, ATT_WIDTH))], out_specs=[tile, tile],
        compiler_params=_params("parallel"),
    )(*outs, *lses, att_g)


def _att_combine_bwd(datt_out, att, lses, att_g):
    s = att.shape[0]
    tm = 256

    def body(d_ref, att_ref, l0, l1, l2, g_ref, do0, do1, do2, cc0, cc1, cc2, dg_ref):
        @pl.when(pl.program_id(0) == 0)
        def _():
            dg_ref[...] = jnp.zeros_like(dg_ref)

        att = att_ref[...]
        ahat, rstd = _rms(att)
        d = d_ref[...]
        dg_ref[...] += _rowsum(d * ahat)
        datt = _rms_bwd(d * g_ref[...], ahat, rstd)
        hi = lax.broadcasted_iota(jnp.int32, (ATT_WIDTH, ATT_WIDTH), 0) // ATT_HEAD_DIM
        hj = lax.broadcasted_iota(jnp.int32, (ATT_WIDTH, ATT_WIDTH), 1) // ATT_HEAD_DIM
        head_sum = _dot_f32(datt * att, (hi == hj).astype(F32))
        ws = _branch_weights([l0[...], l1[...], l2[...]])
        for w, do_ref, cc_ref in zip(ws, (do0, do1, do2), (cc0, cc1, cc2)):
            do_ref[...] = w * datt
            cc_ref[...] = -w * head_sum

    tile = _rows(tm, ATT_WIDTH)
    return pl.pallas_call(
        body, name="att_combine_bwd", grid=(s // tm,),
        out_shape=[jax.ShapeDtypeStruct((s, ATT_WIDTH), F32)] * 6 + [jax.ShapeDtypeStruct((1, ATT_WIDTH), F32)],
        in_specs=[tile] * 5 + [_whole((1, ATT_WIDTH))], out_specs=[tile] * 6 + [_whole((1, ATT_WIDTH))],
        compiler_params=_params("arbitrary"),
    )(datt_out, att, *lses, att_g)


def _out_fwd(x, hg, at, mod, w_out):
    s = x.shape[0]
    tm = 512

    def body(x_ref, hg_ref, at_ref, mod_ref, w_ref, x1_ref):
        mix = _dot(hg_ref[...], w_ref[0:512, :]) + _dot(at_ref[...], w_ref[512:1024, :])
        x1_ref[...] = x_ref[...] + mod_ref[:, 2 * D_MODEL:3 * D_MODEL] * mix

    return pl.pallas_call(
        body, name="out_fwd", grid=(s // tm,), out_shape=jax.ShapeDtypeStruct((s, D_MODEL), F32),
        in_specs=[_rows(tm, D_MODEL), _rows(tm, 512), _rows(tm, 512), _whole((1, 6 * D_MODEL)), _whole((D_MODEL, D_MODEL))],
        out_specs=_rows(tm, D_MODEL), compiler_params=_params("parallel"),
    )(x, hg, at, mod, w_out)


def _out_bwd(dx1, hg, at, mod, w_out):
    s = dx1.shape[0]
    tm = 512
    n_steps = s // tm

    def body(dx_ref, hg_ref, at_ref, mod_ref, w_ref, dhg_ref, dat_ref, dw_ref, dwb_ref, dgate_ref):
        @pl.when(pl.program_id(0) == 0)
        def _():
            dw_ref[...] = jnp.zeros_like(dw_ref)
            dgate_ref[...] = jnp.zeros_like(dgate_ref)

        hg, at, dx = hg_ref[...], at_ref[...], dx_ref[...]
        mix = _dot(hg, w_ref[0:512, :]) + _dot(at, w_ref[512:1024, :])
        dgate_ref[...] += _rowsum(dx * mix)
        dmix = (mod_ref[:, 2 * D_MODEL:3 * D_MODEL] * dx).astype(BF16)
        dhg_ref[...] = _dot_nt(dmix, w_ref[0:512, :])
        dat_ref[...] = _dot_nt(dmix, w_ref[512:1024, :])
        dw_ref[0:512, :] += _dot_tn(hg, dmix)
        dw_ref[512:1024, :] += _dot_tn(at, dmix)

        @pl.when(pl.program_id(0) == n_steps - 1)
        def _():
            dwb_ref[...] = dw_ref[...].astype(BF16)

    return pl.pallas_call(
        body, name="out_bwd", grid=(n_steps,),
        out_shape=[jax.ShapeDtypeStruct((s, 512), F32)] * 2
        + [jax.ShapeDtypeStruct((D_MODEL, D_MODEL), F32), jax.ShapeDtypeStruct((D_MODEL, D_MODEL), BF16),
           jax.ShapeDtypeStruct((1, D_MODEL), F32)],
        in_specs=[_rows(tm, D_MODEL), _rows(tm, 512), _rows(tm, 512), _whole((1, 6 * D_MODEL)), _whole((D_MODEL, D_MODEL))],
        out_specs=[_rows(tm, 512), _rows(tm, 512), _whole((D_MODEL, D_MODEL)), _whole((D_MODEL, D_MODEL)), _whole((1, D_MODEL))],
        compiler_params=_params("arbitrary"),
    )(dx1, hg, at, mod, w_out)


FFN_CHUNK = 2816


def _ffn(x1, target, mod, g2, gf, w_gu, w_down):
    s = x1.shape[0]
    tm = 256
    n_chunks = D_FF // FFN_CHUNK

    def body(x_ref, t_ref, mod_ref, g2_ref, gf_ref, wgu_hbm, wd_hbm,
             dx_ref, h2_ref, act_ref, dau_ref, dff_ref, sums_ref, loss_ref, wgu, wd, a_s, u_s, sem):
        @pl.when(pl.program_id(0) == 0)
        def _():
            c1 = pltpu.make_async_copy(wgu_hbm, wgu, sem.at[0])
            c2 = pltpu.make_async_copy(wd_hbm, wd, sem.at[1])
            c1.start()
            c2.start()
            c1.wait()
            c2.wait()
            sums_ref[...] = jnp.zeros_like(sums_ref)
            loss_ref[...] = jnp.zeros_like(loss_ref)

        x1v = x_ref[...]
        xhat, rstd = _rms(x1v)
        g2 = g2_ref[...]
        n2 = xhat * g2
        scale2 = 1.0 + mod_ref[:, 4 * D_MODEL:5 * D_MODEL]
        gate2 = mod_ref[:, 5 * D_MODEL:6 * D_MODEL]
        hb = (n2 * scale2 + mod_ref[:, 3 * D_MODEL:4 * D_MODEL]).astype(BF16)
        h2_ref[...] = hb
        ff = jnp.zeros((tm, D_MODEL), F32)
        for j in range(n_chunks):
            c = slice(j * FFN_CHUNK, (j + 1) * FFN_CHUNK)
            cu = slice(D_FF + j * FFN_CHUNK, D_FF + (j + 1) * FFN_CHUNK)
            a = _dot_nt(hb, wgu[c, :])
            u = _dot_nt(hb, wgu[cu, :])
            a_s[:, c] = a
            u_s[:, c] = u
            act = (_silu(a) * u).astype(BF16)
            act_ref[:, c] = act
            ff += _dot(act, wd[c, :])
        x2 = x1v + gate2 * ff
        nf, rstd_f = _rms(x2)
        gfv = gf_ref[...]
        err = nf * gfv - t_ref[...]
        loss_ref[...] += 0.5 * jnp.sum(_rowsum(err * err), axis=-1, keepdims=True) * (1.0 / D_MODEL)
        dy = err * (1.0 / D_MODEL)
        dx2 = _rms_bwd(dy * gfv, nf, rstd_f)
        dffb = (gate2 * dx2).astype(BF16)
        dff_ref[...] = dffb
        dh = jnp.zeros((tm, D_MODEL), F32)
        for j in range(n_chunks):
            c = slice(j * FFN_CHUNK, (j + 1) * FFN_CHUNK)
            cu = slice(D_FF + j * FFN_CHUNK, D_FF + (j + 1) * FFN_CHUNK)
            dact = _dot_nt(dffb, wd[c, :])
            a, u = a_s[:, c], u_s[:, c]
            da = (dact * u * _dsilu(a)).astype(BF16)
            du = (dact * _silu(a)).astype(BF16)
            dau_ref[:, c] = da
            dau_ref[:, cu] = du
            dh += _dot(da, wgu[c, :]) + _dot(du, wgu[cu, :])
        dn = dh * scale2
        sums_ref[0:1, :] += _rowsum(dh)
        sums_ref[1:2, :] += _rowsum(dh * n2)
        sums_ref[2:3, :] += _rowsum(dx2 * ff)
        sums_ref[3:4, :] += _rowsum(dn * xhat)
        sums_ref[4:5, :] += _rowsum(dy * nf)
        dx_ref[...] = dx2 + _rms_bwd(dn * g2, xhat, rstd)

    vec = _whole((1, D_MODEL))
    hbm = pl.BlockSpec(memory_space=pl.ANY)
    return pl.pallas_call(
        body, name="ffn", grid=(s // tm,),
        out_shape=[jax.ShapeDtypeStruct((s, D_MODEL), F32), jax.ShapeDtypeStruct((s, D_MODEL), BF16),
                   jax.ShapeDtypeStruct((s, D_FF), BF16), jax.ShapeDtypeStruct((s, 2 * D_FF), BF16),
                   jax.ShapeDtypeStruct((s, D_MODEL), BF16), jax.ShapeDtypeStruct((8, D_MODEL), F32),
                   jax.ShapeDtypeStruct((1, 128), F32)],
        in_specs=[_rows(tm, D_MODEL), _rows(tm, D_MODEL), _whole((1, 6 * D_MODEL)), vec, vec, hbm, hbm],
        out_specs=[_rows(tm, D_MODEL), _rows(tm, D_MODEL), _rows(tm, D_FF), _rows(tm, 2 * D_FF), _rows(tm, D_MODEL),
                   _whole((8, D_MODEL)), _whole((1, 128))],
        scratch_shapes=[pltpu.VMEM((2 * D_FF, D_MODEL), BF16), pltpu.VMEM((D_FF, D_MODEL), BF16),
                        pltpu.VMEM((tm, D_FF), F32), pltpu.VMEM((tm, D_FF), F32), pltpu.SemaphoreType.DMA((2,))],
        compiler_params=_params("arbitrary"),
    )(x1, target, mod, g2, gf, w_gu, w_down)


def _weight_grad(a, b, name, rounded=False):
    s, m = a.shape
    n = b.shape[1]
    ts = min(s, 2048)
    n_steps = s // ts
    tm = max(t for t in range(128, m + 1, 128) if m % t == 0 and t * n * 4 <= 6 * 1024 * 1024)

    def body(a_ref, b_ref, o_ref, *ob_ref):
        @pl.when(pl.program_id(1) == 0)
        def _():
            o_ref[...] = jnp.zeros_like(o_ref)

        o_ref[...] += _dot_tn(a_ref[...], b_ref[...])
        if rounded:
            @pl.when(pl.program_id(1) == n_steps - 1)
            def _():
                ob_ref[0][...] = o_ref[...].astype(BF16)

    tile = pl.BlockSpec((tm, n), lambda j, i: (j, 0))
    return pl.pallas_call(
        body, name=name, grid=(m // tm, n_steps),
        out_shape=[jax.ShapeDtypeStruct((m, n), F32)] + [jax.ShapeDtypeStruct((m, n), BF16)] * rounded,
        in_specs=[pl.BlockSpec((ts, tm), lambda j, i: (i, j)), pl.BlockSpec((ts, n), lambda j, i: (i, 0))],
        out_specs=[tile] + [tile] * rounded,
        compiler_params=_params("parallel", "arbitrary"),
    )(a, b)


def _adamw_math(w, g, m, v):
    m = ADAM_B1 * m + (1.0 - ADAM_B1) * g
    v = ADAM_B2 * v + (1.0 - ADAM_B2) * (g * g)
    m_hat = m / (1.0 - ADAM_B1 ** ADAM_STEP)
    v_hat = v / (1.0 - ADAM_B2 ** ADAM_STEP)
    delta = -ADAM_LR * (m_hat / (jnp.sqrt(v_hat) + ADAM_EPS) + ADAM_WD * w)
    return delta, m, v


def _adamw_shard(where, w, m, v, partial, got, name):
    r, c = w.shape
    tr = _shard_rows(r)
    lead = partial.ndim - 2
    n_got = got.shape[0]

    def body(where_ref, w_ref, m_ref, v_ref, own_ref, *rest):
        got_refs, (grad_ref, d_ref, nm_ref, nv_ref) = rest[:n_got], rest[n_got:]
        g = own_ref[...]
        for g_ref in got_refs:
            g = g + g_ref[...].astype(F32)
        grad_ref[...] = g
        d_ref[...], nm_ref[...], nv_ref[...] = _adamw_math(w_ref[...], g, m_ref[...], v_ref[...])

    tile = pl.BlockSpec((tr, c), lambda i, where_ref: (i, 0))
    own = pl.BlockSpec((None,) * lead + (tr, c), lambda i, where_ref: (*[where_ref[d] for d in range(lead)], i, 0))
    part = [pl.BlockSpec((None, tr, c), functools.partial(lambda j, i, where_ref: (j, i, 0), j)) for j in range(n_got)]
    return pl.pallas_call(
        body, name=name,
        grid_spec=pltpu.PrefetchScalarGridSpec(num_scalar_prefetch=1, grid=(r // tr,), in_specs=[tile] * 3 + [own] + part,
                                               out_specs=[tile] * 4),
        out_shape=[jax.ShapeDtypeStruct((r, c), F32)] * 4, compiler_params=_params("parallel"),
    )(where, w, m, v, partial, *[got] * n_got)


def _small_update(small_all, dmod_blocks, c_all, logits, w_ada, m_ada, v_ada, smalls):
    def body(sm_ref, dm_ref, c_ref, lg_ref, wa_ref, ma_ref, va_ref, *rest):
        ins, outs = rest[:21], rest[21:]
        _, me = _flip(0)
        tot = sm_ref[0:1, :]
        for i in range(1, N_DEV):
            tot = tot + sm_ref[i:i + 1, :]
        loss_ref = outs[0]
        loss_ref[...] = tot[:, SM_LOSS:SM_LOSS + 128]
        g_ada = lax.dot_general(_silu(c_ref[...]), dm_ref[me], (((0,), (0,)), ((), ())),
                                preferred_element_type=F32, precision=HIGHEST)
        outs[1][...] = g_ada
        outs[2][...], outs[3][...], outs[4][...] = _adamw_math(wa_ref[...], g_ada, ma_ref[...], va_ref[...])
        p0 = _lower_bound(lg_ref)
        dl0 = tot[:, SM_LB:SM_LB + 512] * p0 * (1.0 - p0)
        grads = [tot[:, SM_MOD:SM_MOD + 6 * D_MODEL], tot[:, SM_G1:SM_G1 + D_MODEL], tot[:, SM_G2:SM_G2 + D_MODEL],
                 tot[:, SM_GF:SM_GF + D_MODEL], tot[:, SM_ATT:SM_ATT + 512], tot[:, SM_HG:SM_HG + 128],
                 jnp.where(lax.broadcasted_iota(jnp.int32, (2, 512), 0) == 0, dl0, -dl0)]
        for i, g in enumerate(grads):
            w_ref, m_ref, v_ref = ins[3 * i:3 * i + 3]
            o = outs[5 + 4 * i:9 + 4 * i]
            o[0][...] = g
            o[1][...], o[2][...], o[3][...] = _adamw_math(w_ref[...], g, m_ref[...], v_ref[...])

    flat = [t for trio in smalls for t in trio]
    vm = pl.BlockSpec(memory_space=pltpu.VMEM)
    out_shape = [jax.ShapeDtypeStruct((1, 128), F32)] + [jax.ShapeDtypeStruct(w_ada.shape, F32)] * 4
    for trio in smalls:
        out_shape += [jax.ShapeDtypeStruct(trio[0].shape, F32)] * 4
    return pl.pallas_call(
        body, name="small_update", out_shape=out_shape,
        in_specs=[vm] * (7 + len(flat)), out_specs=[vm] * len(out_shape),
        compiler_params=pltpu.CompilerParams(vmem_limit_bytes=V7X_VMEM_LIMIT),
    )(small_all, dmod_blocks, c_all, logits, w_ada, m_ada, v_ada, *flat)


def kernel(x, c, w_ada, b_ada, norm1_g, w_in, hg_lb_logits, hg_onorm_g, att_onorm_g, w_out, norm2_g, w_gate_up, w_down, final_g, loss_target, m_w_ada, m_b_ada, m_norm1_g, m_w_in, m_hg_lb_logits, m_hg_onorm_g, m_att_onorm_g, m_w_out, m_norm2_g, m_w_gate_up, m_w_down, m_final_g, v_w_ada, v_b_ada, v_norm1_g, v_w_in, v_hg_lb_logits, v_hg_onorm_g, v_att_onorm_g, v_w_out, v_norm2_g, v_w_gate_up, v_w_down, v_final_g):
    x2d, target = x[0], loss_target[0]
    seq = x2d.shape[0]
    assert seq % (ATT_BLOCK * max(DILATIONS)) == 0 and seq % HG_TILE == 0
    gf = final_g.reshape(1, D_MODEL)

    c_all = _exchange_small(c.reshape(8, D_MODEL // 8), None, "gather_c").reshape(N_DEV, D_MODEL)
    ada = _ada_rows(c_all, w_ada[0], b_ada)
    mod = _exchange_small(ada, 1, "scatter_mod").reshape(1, 6 * D_MODEL)

    core = lax.axis_index("c").astype(jnp.int32).reshape(1)
    chip = (2 * lax.axis_index("x") + lax.axis_index("y")).astype(jnp.int32).reshape(1)
    me = 4 * lax.axis_index("x") + 2 * lax.axis_index("y") + lax.axis_index("c")

    g_in, = _gather_weights([w_in[0].T.astype(BF16)])
    w_in_b = g_in.reshape(IN_WIDTH, D_MODEL)
    rest_shards = [w_out[0].astype(BF16), w_gate_up[0].T.astype(BF16), w_down[0].astype(BF16)]
    lands = [lax.empty((N_DEV,) + s.shape, BF16) for s in rest_shards]
    g_send, g_recv, g_srcs, g_lands, tok = _copies_start("gather_rest_start", _plan_gather_own, 12, rest_shards, lands, [w_in_b, mod])
    flight = {}

    def stage(name, *vals):
        if name == "attention_begun":
            flight["shards"], got = _copies_wait("gather_rest_wait", _plan_gather_own, g_send, g_recv, g_srcs, g_lands, list(vals))
            flight["pass"] = _copies_start("gather_pass_start", _plan_gather_pass, 9, [], got, [])
            return flight["pass"][4][0:1, 0:1]
        if name == "mixer_weights_done":
            shapes = [(4, 2, D_MODEL // N_DEV, D_MODEL), (4, 2, 2 * D_FF // N_DEV, D_MODEL), (4, 2, D_FF // N_DEV, D_MODEL)]
            flight["grads"] = [g32.reshape(sh) for (g32, _), sh in zip(vals, shapes)]
            rounded = [g16.reshape(sh) for (_, g16), sh in zip(vals, shapes)]
            direct_lands = [lax.empty((N_DEV - 1,) + sh[2:], BF16) for sh in shapes]
            flight["direct"] = _copies_start("reduce_rest_start", _plan_reduce_direct, 21, rounded, direct_lands, [])
            return flight["direct"][4][0:1, 0:1]
        raise ValueError(name)

    def rest_weights(after):
        s, r, _, p_lands, _ = flight["pass"]
        _, got = _copies_wait("gather_pass_wait", _plan_gather_pass, s, r, [], p_lands, [after])
        full = [lax.dynamic_update_index_in_dim(g, shard, me, 0) for g, shard in zip(got, flight["shards"])]
        return full[0].reshape(D_MODEL, D_MODEL), full[1].reshape(2 * D_FF, D_MODEL), full[2].reshape(D_FF, D_MODEL)

    grad_x, dw_in, small = _block_step(x2d, target, mod + tok[0:1, 0:1], norm1_g, hg_lb_logits, hg_onorm_g, att_onorm_g, norm2_g, gf,
                                       w_in_b, rest_weights, stage)

    g_in8 = dw_in.reshape(4, 2, IN_WIDTH // N_DEV, D_MODEL)
    in_pairs = _copies_start("reduce_pairs_in_start", _plan_reduce_pairs, 4, [g_in8], [lax.empty((4,) + g_in8.shape[2:], F32)], [])
    s, r, srcs, d_lands, _ = flight["direct"]
    _, recv_rest = _copies_wait("reduce_rest_wait", _plan_reduce_direct, s, r, srcs, d_lands, [in_pairs[4]])
    small_rows = jnp.pad(small + in_pairs[4][0:1, 0:1], ((0, 0), (0, SM_PADDED - SM_WIDTH))).reshape(SM_PADDED // 128, 128)
    small_all = _exchange_small(small_rows, None, "gather_small").reshape(N_DEV, SM_PADDED)[:, :SM_WIDTH]
    in_grads, got_in = _copies_wait("reduce_pairs_in_wait", _plan_reduce_pairs, in_pairs[0], in_pairs[1], in_pairs[2], in_pairs[3],
                                    [small_all])
    in_s32, in_s16 = _pair_sum(core, in_grads[0], got_in[0], "pair_sum_in")
    in_chips = _copies_start("reduce_chips_in_start", _plan_reduce_chips, 3, [in_s16], [lax.empty((3,) + in_s16.shape[1:], BF16)], [])
    big, updated = {}, []
    rest_params = [("w_out", w_out, m_w_out, v_w_out), ("w_gate_up", w_gate_up, m_w_gate_up, v_w_gate_up), ("w_down", w_down, m_w_down, v_w_down)]
    mine = jnp.concatenate([chip, core])
    for (n, w, m, v), g32, got in zip(rest_params, flight["grads"], recv_rest):
        if n == "w_gate_up":
            outs4 = _adamw_shard(mine, w[0].T, m[0].T, v[0].T, g32, got, f"adamw_{n}")
            big[n] = [t.T[None] for t in outs4]
        else:
            outs4 = _adamw_shard(mine, w[0], m[0], v[0], g32, got, f"adamw_{n}")
            big[n] = [t[None] for t in outs4]
        updated.append(outs4[3])
    c_all = c_all + in_chips[4][0:1, 0:1]
    smalls = [(b_ada, m_b_ada, v_b_ada), (norm1_g, m_norm1_g, v_norm1_g), (norm2_g, m_norm2_g, v_norm2_g),
              (gf, m_final_g.reshape(1, D_MODEL), v_final_g.reshape(1, D_MODEL)),
              (att_onorm_g, m_att_onorm_g, v_att_onorm_g), (hg_onorm_g, m_hg_onorm_g, v_hg_onorm_g),
              (hg_lb_logits, m_hg_lb_logits, v_hg_lb_logits)]
    dmod_blocks = small_all[:, :6 * D_MODEL].reshape(N_DEV, N_DEV, 6 * D_MODEL // N_DEV).transpose(1, 0, 2)
    res = _small_update(small_all, dmod_blocks, c_all, hg_lb_logits, w_ada[0], m_w_ada[0], v_w_ada[0], smalls)
    _, recv_in = _copies_wait("reduce_chips_in_wait", _plan_reduce_chips, in_chips[0], in_chips[1], in_chips[2], in_chips[3],
                              [res[0]] + updated)
    big["w_in"] = [t.T[None] for t in _adamw_shard(chip, w_in[0].T, m_w_in[0].T, v_w_in[0].T, in_s32, recv_in[0], "adamw_w_in")]
    loss = res[0][0, 0]
    ada4 = [t[None] for t in res[1:5]]
    sm4 = {n: list(res[5 + 4 * i:9 + 4 * i]) for i, n in enumerate(["b_ada", "norm1_g", "norm2_g", "final_g", "att", "hg", "lb"])}
    sm4["final_g"] = [t.reshape(D_MODEL) for t in sm4["final_g"]]

    order = [ada4, sm4["b_ada"], sm4["norm1_g"], big["w_in"], sm4["lb"], sm4["hg"], sm4["att"], big["w_out"], sm4["norm2_g"],
             big["w_gate_up"], big["w_down"], sm4["final_g"]]
    return (loss, grad_x[None], *[o[0] for o in order], *[o[1] for o in order], *[o[2] for o in order], *[o[3] for o in order])


def _block_step(x2d, target, mod, norm1_g, hg_lb_logits, hg_onorm_g, att_onorm_g, norm2_g, gf, w_in_b, rest_weights, stage):
    h1, hq, hf, hi, hgt, aq, ak, av = _in_fwd(x2d, mod, norm1_g, w_in_b)
    hg_out, hg_o, hg_states = _hg_fwd(hq, hf, hi, hgt, hg_lb_logits, hg_onorm_g)
    branch = [_att_fwd(aq, ak, av, d) for d in DILATIONS[:2]]
    att_g = att_onorm_g + stage("attention_begun", branch[0][0], branch[1][0])
    branch += [_att_fwd(aq, ak, av, d) for d in DILATIONS[2:]]
    outs = [b[0] for b in branch]
    lses = [b[1] for b in branch]
    att, att_out = _att_combine(outs, lses, att_g)
    w_out_b, w_gu_b, w_down_b = rest_weights(att_out)
    x1 = _out_fwd(x2d, hg_out, att_out, mod, w_out_b)

    dx1, h2, act, dau, dff, ffn_sums, loss_part = _ffn(x1, target, mod, norm2_g, gf, w_gu_b, w_down_b)
    dw_gu = _weight_grad(dau, h2, "dw_gate_up", rounded=True)
    dw_down = _weight_grad(act, dff, "dw_down", rounded=True)

    dhg, dat, dw_out, dw_out_b, dgate1 = _out_bwd(dx1, hg_out, att_out, mod, w_out_b)
    att_g = att_onorm_g + stage("mixer_weights_done", (dw_out, dw_out_b), dw_gu, dw_down)
    comb = _att_combine_bwd(dat, att, lses, att_g)
    dos, ccs, d_att_g = comb[0:3], comb[3:6], comb[6]
    datt = []
    for i, d in enumerate(DILATIONS):
        datt.append(_att_bwd(aq, ak, av, dos[i], ccs[i], lses[i], d))
    dhq, dhf, dhi, dhgt, d_hg_g, d_lb = _hg_bwd(hq, hf, hi, hgt, hg_lb_logits, hg_onorm_g, hg_o, hg_states, dhg)
    dps = [dhq, dhf, dhi, dhgt] + [datt[i][j] for j in range(3) for i in range(3)]
    grad_x, dp_b, dshift1, dscale1, d_g1 = _in_bwd(x2d, dx1, mod, norm1_g, w_in_b, dps)
    dw_in, = _weight_grad(dp_b, h1, "dw_in")
    small = jnp.concatenate([dshift1, dscale1, dgate1, ffn_sums[0:1], ffn_sums[1:2], ffn_sums[2:3], d_g1, ffn_sums[3:4],
                             ffn_sums[4:5], d_att_g, d_lb, d_hg_g, loss_part], axis=1)
    return grad_x, dw_in, small
```

```python
import functools

import jax
import jax.numpy as jnp
from jax import lax
from jax.experimental import pallas as pl
from jax.experimental.pallas import tpu as pltpu

F32 = jnp.float32
BF16 = jnp.bfloat16
HIGHEST = lax.Precision.HIGHEST
MESH = pl.DeviceIdType.MESH

D_MODEL = 1024
N_DEV = 8
HG_HEADS = 4
HG_DIM = 128
HG_WIDTH = HG_HEADS * HG_DIM
HG_CHUNK = 128
ATT_WIDTH = 512
ATT_HEAD_DIM = 64
ATT_BLOCK = 128
DILATIONS = (1, 4, 16)
ATT_SCALE = ATT_HEAD_DIM ** -0.5
D_FF = 2816
IN_WIDTH = 7 * 512
RMS_EPS = 1e-6
NEG = -1e30

ADAM_LR = 0.001
ADAM_B1 = 0.9
ADAM_B2 = 0.999
ADAM_EPS = 1e-08
ADAM_WD = 0.01
ADAM_STEP = 10

V7X_VMEM_LIMIT = 56 * 1024 * 1024

SM_MOD = 0
SM_G1 = 6 * D_MODEL
SM_G2 = 7 * D_MODEL
SM_GF = 8 * D_MODEL
SM_ATT = 9 * D_MODEL
SM_LB = 9 * D_MODEL + 512
SM_HG = 10 * D_MODEL
SM_LOSS = 10 * D_MODEL + 128
SM_WIDTH = 10 * D_MODEL + 256
SM_PADDED = 88 * 128


def _params(*sem, vmem=V7X_VMEM_LIMIT):
    return pltpu.CompilerParams(dimension_semantics=sem, vmem_limit_bytes=vmem)


def _dot(a, b):
    return jnp.dot(a, b, preferred_element_type=F32)


def _dot_nt(a, b):
    return lax.dot_general(a, b, (((1,), (1,)), ((), ())), preferred_element_type=F32)


def _dot_tn(a, b):
    return lax.dot_general(a, b, (((0,), (0,)), ((), ())), preferred_element_type=F32)


def _dot_f32(a, b):
    return jnp.dot(a, b, preferred_element_type=F32, precision=HIGHEST)


def _sigmoid(x):
    return 1.0 / (1.0 + jnp.exp(-x))


def _silu(x):
    return x * _sigmoid(x)


def _dsilu(x):
    s = _sigmoid(x)
    return s * (1.0 + x * (1.0 - s))


def _rms(x):
    rstd = lax.rsqrt(jnp.mean(x * x, axis=-1, keepdims=True) + RMS_EPS)
    return x * rstd, rstd


def _rms_bwd(dn, xhat, rstd):
    return rstd * (dn - xhat * jnp.mean(dn * xhat, axis=-1, keepdims=True))


def _rowsum(x):
    return jnp.sum(x, axis=0, keepdims=True)


def _rows(tm, n):
    return pl.BlockSpec((tm, n), lambda i: (i, 0))


def _whole(shape):
    return pl.BlockSpec(shape, lambda i: (0,) * len(shape))


def _mesh_pos():
    return lax.axis_index("x"), lax.axis_index("y"), lax.axis_index("c")


def _flip(k):
    x, y, c = _mesh_pos()
    px = 1 - x if k & 4 else x
    py = 1 - y if k & 2 else y
    pc = 1 - c if k & 1 else c
    return (px, py, pc), 4 * px + 2 * py + pc


def _exchange_small(x, rows_per_peer, name):
    r_all, cols = x.shape
    r_out = r_all if rows_per_peer is None else rows_per_peer

    def body(x_ref, out_ref, send_sems, recv_sems):
        _, me = _flip(0)

        def src(pid):
            if rows_per_peer is None:
                return x_ref
            return x_ref.at[pl.ds(pl.multiple_of(pid * r_out, r_out), r_out), :]

        if rows_per_peer is None:
            out_ref[me] = x_ref[...]
        else:
            out_ref[me] = x_ref[pl.ds(pl.multiple_of(me * r_out, r_out), r_out), :]
        sends = []
        for k in range(1, N_DEV):
            dev, pid = _flip(k)
            cp = pltpu.make_async_remote_copy(src_ref=src(pid), dst_ref=out_ref.at[me], send_sem=send_sems.at[k - 1],
                                              recv_sem=recv_sems.at[k - 1], device_id=dev, device_id_type=MESH)
            cp.start()
            sends.append(cp)
        for k in range(1, N_DEV):
            dev, pid = _flip(k)
            pltpu.make_async_remote_copy(src_ref=src(pid), dst_ref=out_ref.at[pid], send_sem=send_sems.at[k - 1],
                                         recv_sem=recv_sems.at[k - 1], device_id=dev, device_id_type=MESH).wait_recv()
        for cp in sends:
            cp.wait_send()

    return pl.pallas_call(
        body, name=name,
        out_shape=jax.ShapeDtypeStruct((N_DEV, r_out, cols), x.dtype),
        in_specs=[pl.BlockSpec(memory_space=pltpu.VMEM)],
        out_specs=pl.BlockSpec(memory_space=pltpu.VMEM),
        scratch_shapes=[pltpu.SemaphoreType.DMA((N_DEV - 1,)), pltpu.SemaphoreType.DMA((N_DEV - 1,))],
    )(x)


def _gather_weights(shards):
    n = len(shards)

    def body(*refs):
        xs, outs = refs[:n], refs[n:2 * n]
        send_sems, recv_sems, local_sems = refs[2 * n:]
        x, y, c = _mesh_pos()
        me, sibling = (x, y, c), (x, y, 1 - c)
        chips = [(1 - x, y), (x, 1 - y), (1 - x, 1 - y)]

        def blk(a, px, py, pc):
            return outs[a].at[4 * px + 2 * py + pc]

        def copy(a, k, block, to, src=None):
            return pltpu.make_async_remote_copy(
                src_ref=blk(a, *block) if src is None else src, dst_ref=blk(a, *block),
                send_sem=send_sems.at[a * 7 + k], recv_sem=recv_sems.at[a * 7 + k], device_id=to, device_id_type=MESH)

        mine = [pltpu.make_async_copy(xs[a], blk(a, *me), local_sems.at[a]) for a in range(n)]
        for cp in mine:
            cp.start()
        first = []
        for a in range(n):
            first.append(copy(a, 0, me, sibling, src=xs[a]))
            first += [copy(a, 1 + j, me, (*chip, c), src=xs[a]) for j, chip in enumerate(chips)]
        for cp in first:
            cp.start()
        passed = []
        for j, chip in enumerate(chips):
            for a in range(n):
                copy(a, 1 + j, (*chip, c), me).wait_recv()
                cp = copy(a, 4 + j, (*chip, c), sibling)
                cp.start()
                passed.append(cp)
        for a in range(n):
            copy(a, 0, sibling, me).wait_recv()
            for j, chip in enumerate(chips):
                copy(a, 4 + j, (*chip, 1 - c), me).wait_recv()
        for cp in first + passed:
            cp.wait_send()
        for cp in mine:
            cp.wait()

    hbm = pl.BlockSpec(memory_space=pl.ANY)
    return pl.pallas_call(
        body, name="gather_weights",
        out_shape=[jax.ShapeDtypeStruct((N_DEV,) + s.shape, s.dtype) for s in shards],
        in_specs=[hbm] * n, out_specs=[hbm] * n,
        scratch_shapes=[pltpu.SemaphoreType.DMA((7 * n,)), pltpu.SemaphoreType.DMA((7 * n,)), pltpu.SemaphoreType.DMA((n,))],
    )(*shards)


_HBM = pl.BlockSpec(memory_space=pltpu.HBM)
_SEM = pl.BlockSpec(memory_space=pltpu.SEMAPHORE)
_DATAFLOW = pltpu.SideEffectType.DATAFLOW_SIDE_EFFECTING


def _copies_start(name, plan, n_copies, srcs, lands, after):
    bufs = list(srcs) + list(lands)
    nb = len(bufs)

    def body(*refs):
        ins, send_sems, recv_sems, token = refs[:nb], refs[nb + len(after)], refs[nb + len(after) + 1], refs[-1]
        for i, (src, dst, dev) in enumerate(plan(ins[:len(srcs)], ins[len(srcs):])):
            pltpu.make_async_remote_copy(src_ref=src, dst_ref=dst, send_sem=send_sems.at[i], recv_sem=recv_sems.at[i],
                                         device_id=dev, device_id_type=MESH).start()
        token[...] = jnp.zeros_like(token)

    outs = pl.pallas_call(
        body, name=name,
        out_shape=(pltpu.SemaphoreType.DMA((n_copies,)), pltpu.SemaphoreType.DMA((n_copies,)),
                   *[pltpu.HBM(b.shape, b.dtype) for b in bufs], jax.ShapeDtypeStruct((8, 128), F32)),
        in_specs=[_HBM] * nb + [pl.BlockSpec(memory_space=pl.ANY)] * len(after),
        out_specs=(_SEM, _SEM, *[_HBM] * nb, pl.BlockSpec(memory_space=pltpu.VMEM)),
        input_output_aliases={i: 2 + i for i in range(nb)},
        compiler_params=pltpu.CompilerParams(has_side_effects=_DATAFLOW),
    )(*[pltpu.with_memory_space_constraint(b, pltpu.HBM) for b in bufs], *after)
    return outs[0], outs[1], list(outs[2:2 + len(srcs)]), list(outs[2 + len(srcs):2 + nb]), outs[-1]


def _copies_wait(name, plan, send_sems, recv_sems, srcs, lands, after):
    bufs = list(srcs) + list(lands)
    nb = len(bufs)

    def body(*refs):
        ins, send_ref, recv_ref = refs[:nb], refs[nb], refs[nb + 1]
        for i, (src, dst, dev) in enumerate(plan(ins[:len(srcs)], ins[len(srcs):])):
            cp = pltpu.make_async_remote_copy(src_ref=src, dst_ref=dst, send_sem=send_ref.at[i], recv_sem=recv_ref.at[i],
                                              device_id=dev, device_id_type=MESH)
            cp.wait_send()
            cp.wait_recv()

    outs = pl.pallas_call(
        body, name=name, out_shape=[pltpu.HBM(b.shape, b.dtype) for b in bufs],
        in_specs=[_HBM] * nb + [_SEM, _SEM] + [pl.BlockSpec(memory_space=pl.ANY)] * len(after), out_specs=[_HBM] * nb,
        input_output_aliases={i: i for i in range(nb)},
        compiler_params=pltpu.CompilerParams(has_side_effects=_DATAFLOW),
    )(*bufs, send_sems, recv_sems, *after)
    return list(outs[:len(srcs)]), list(outs[len(srcs):])


def _plan_gather_own(srcs, lands):
    _, me = _flip(0)
    return [(srcs[a], lands[a].at[me], _flip(k)[0]) for a in range(len(srcs)) for k in (1, 4, 2, 6)]


def _plan_gather_pass(srcs, lands):
    sibling = _flip(1)[0]
    plan = []
    for land in lands:
        for k in (4, 2, 6):
            block = land.at[_flip(k)[1]]
            plan.append((block, block, sibling))
    return plan


def _plan_reduce_pairs(srcs, lands):
    x, y, c = _mesh_pos()
    return [(srcs[a].at[chip, 1 - c], lands[a].at[chip], (x, y, 1 - c)) for a in range(len(srcs)) for chip in range(4)]


def _plan_reduce_chips(srcs, lands):
    plan = []
    for a in range(len(srcs)):
        for j, k in enumerate((4, 2, 6)):
            dev = _flip(k)[0]
            plan.append((srcs[a].at[2 * dev[0] + dev[1]], lands[a].at[j], dev))
    return plan


def _plan_reduce_direct(srcs, lands):
    plan = []
    for a in range(len(srcs)):
        for k in range(1, N_DEV):
            dev = _flip(k)[0]
            plan.append((srcs[a].at[2 * dev[0] + dev[1], dev[2]], lands[a].at[k - 1], dev))
    return plan


def _shard_rows(r):
    return r // 2 if r % 32 == 0 else r


def _pair_sum(core, grads, got, name):
    _, _, r, c = grads.shape
    tr = _shard_rows(r)

    def body(core_ref, a_ref, b_ref, o_ref, ob_ref):
        s = a_ref[...] + b_ref[...]
        o_ref[...] = s
        ob_ref[...] = s.astype(BF16)

    spec = pl.BlockSpec((None, tr, c), lambda i, j, core_ref: (i, j, 0))
    return pl.pallas_call(
        body, name=name,
        grid_spec=pltpu.PrefetchScalarGridSpec(
            num_scalar_prefetch=1, grid=(4, r // tr),
            in_specs=[pl.BlockSpec((None, None, tr, c), lambda i, j, core_ref: (i, core_ref[0], j, 0)), spec],
            out_specs=[spec, spec]),
        out_shape=[jax.ShapeDtypeStruct((4, r, c), F32), jax.ShapeDtypeStruct((4, r, c), BF16)],
        compiler_params=_params("parallel", "parallel"),
    )(core, grads, got)


def _ada_rows(c_all, w_ada, b_ada):
    n_cols = w_ada.shape[1]

    def body(c_ref, w_ref, b_ref, o_ref):
        _, me = _flip(0)
        bias = b_ref[:, pl.ds(pl.multiple_of(me * n_cols, 128), n_cols)]
        o_ref[...] = _dot_f32(_silu(c_ref[...]), w_ref[...]) + bias

    return pl.pallas_call(
        body, name="ada_rows", out_shape=jax.ShapeDtypeStruct((N_DEV, n_cols), F32),
        in_specs=[pl.BlockSpec(memory_space=pltpu.VMEM)] * 3, out_specs=pl.BlockSpec(memory_space=pltpu.VMEM),
    )(c_all, w_ada, b_ada)


def _in_fwd(x, mod, g1, w_in):
    s = x.shape[0]
    tm = 256

    def body(x_ref, mod_ref, g_ref, w_ref, h_ref, *outs):
        xhat, _ = _rms(x_ref[...])
        h = (xhat * g_ref[...]) * (1.0 + mod_ref[:, D_MODEL:2 * D_MODEL]) + mod_ref[:, 0:D_MODEL]
        hb = h.astype(BF16)
        h_ref[...] = hb
        for j, o_ref in enumerate(outs):
            o_ref[...] = _dot_nt(hb, w_ref[j * 512:(j + 1) * 512, :])

    return pl.pallas_call(
        body, name="in_fwd", grid=(s // tm,),
        out_shape=[jax.ShapeDtypeStruct((s, D_MODEL), BF16)] + [jax.ShapeDtypeStruct((s, 512), F32)] * 7,
        in_specs=[_rows(tm, D_MODEL), _whole((1, 6 * D_MODEL)), _whole((1, D_MODEL)), _whole((IN_WIDTH, D_MODEL))],
        out_specs=[_rows(tm, D_MODEL)] + [_rows(tm, 512)] * 7,
        compiler_params=_params("parallel"),
    )(x, mod, g1, w_in)


def _in_bwd(x, dx1, mod, g1, w_in, dps):
    s = x.shape[0]
    tm = 256

    def body(x_ref, dx_ref, mod_ref, g_ref, w_ref, *rest):
        dp_refs, (gx_ref, dpb_ref, dsh_ref, dsc_ref, dg_ref) = rest[:13], rest[13:]
        pieces = [dp_refs[j][...] for j in range(4)]
        pieces += [dp_refs[4 + 3 * j][...] + dp_refs[5 + 3 * j][...] + dp_refs[6 + 3 * j][...] for j in range(3)]
        for j, p in enumerate(pieces):
            dpb_ref[:, j * 512:(j + 1) * 512] = p.astype(BF16)
        dh = _dot(dpb_ref[...], w_ref[...])
        xhat, rstd = _rms(x_ref[...])
        g = g_ref[...]
        scale1 = 1.0 + mod_ref[:, D_MODEL:2 * D_MODEL]
        n1 = xhat * g

        @pl.when(pl.program_id(0) == 0)
        def _():
            dsh_ref[...] = jnp.zeros_like(dsh_ref)
            dsc_ref[...] = jnp.zeros_like(dsc_ref)
            dg_ref[...] = jnp.zeros_like(dg_ref)

        dsh_ref[...] += _rowsum(dh)
        dsc_ref[...] += _rowsum(dh * n1)
        dn = dh * scale1
        dg_ref[...] += _rowsum(dn * xhat)
        gx_ref[...] = dx_ref[...] + _rms_bwd(dn * g, xhat, rstd)

    vec = _whole((1, D_MODEL))
    return pl.pallas_call(
        body, name="in_bwd", grid=(s // tm,),
        out_shape=[jax.ShapeDtypeStruct((s, D_MODEL), F32), jax.ShapeDtypeStruct((s, IN_WIDTH), BF16)]
        + [jax.ShapeDtypeStruct((1, D_MODEL), F32)] * 3,
        in_specs=[_rows(tm, D_MODEL), _rows(tm, D_MODEL), _whole((1, 6 * D_MODEL)), vec, _whole((IN_WIDTH, D_MODEL))]
        + [_rows(tm, 512)] * 13,
        out_specs=[_rows(tm, D_MODEL), _rows(tm, IN_WIDTH), vec, vec, vec],
        compiler_params=_params("arbitrary"),
    )(x, dx1, mod, g1, w_in, *dps)


HG_TILE = 512
HG_TILE_CHUNKS = HG_TILE // HG_CHUNK


def _lower_bound(lg_ref):
    return 1.0 / (1.0 + jnp.exp(lg_ref[1:2, :] - lg_ref[0:1, :]))


def _chunk_masks():
    r = lax.broadcasted_iota(jnp.int32, (HG_CHUNK, HG_CHUNK), 0)
    c = lax.broadcasted_iota(jnp.int32, (HG_CHUNK, HG_CHUNK), 1)
    return r >= c, c >= r, (r >= c).astype(F32), (c >= r).astype(F32)


def _hg_fwd(hq, hf, hi, hgt, logits, onorm_g):
    s = hq.shape[0]
    n_tiles = s // HG_TILE

    def body(q_ref, f_ref, i_ref, g_ref, lg_ref, og_ref, out_ref, o_ref, st_ref, state, qf_s, kk_s, lf_s):
        @pl.when(pl.program_id(0) == 0)
        def _():
            state[...] = jnp.zeros_like(state)

        lb = _lower_bound(lg_ref)
        f = lb + (1.0 - lb) * _sigmoid(f_ref[...])
        kk_s[...] = 1.0 - f
        lf_s[...] = jnp.log(f)
        qf_s[...] = _silu(q_ref[...])
        causal, _, tri, _ = _chunk_masks()

        def chunk(ci, carry):
            rows = pl.ds(pl.multiple_of(ci * HG_CHUNK, HG_CHUNK), HG_CHUNK)
            srows = pl.ds(pl.multiple_of(ci * HG_DIM, HG_DIM), HG_DIM)
            lf = lf_s[rows, :]
            b = _dot_f32(tri, lf)
            bl = _rowsum(lf)
            ref = 0.5 * bl
            qf, kk, v = qf_s[rows, :], kk_s[rows, :], i_ref[rows, :]
            a_in = (qf * jnp.exp(b)).astype(BF16)
            a_t = (qf * jnp.exp(b - ref)).astype(BF16)
            b_t = (kk * jnp.exp(ref - b)).astype(BF16)
            kd = kk * jnp.exp(bl - b)
            ebl = jnp.exp(bl)
            vb = v.astype(BF16)
            for h in range(HG_HEADS):
                c = slice(h * HG_DIM, (h + 1) * HG_DIM)
                st = state[h]
                st_ref[srows, c] = st
                p = jnp.where(causal, _dot_nt(a_t[:, c], b_t[:, c]), 0.0)
                o_ref[rows, c] = _dot(p.astype(BF16), vb[:, c]) + _dot_nt(a_in[:, c], st.astype(BF16))
                state[h] = st * ebl[:, c] + _dot_tn(vb[:, c], kd[:, c].astype(BF16))
            return carry

        lax.fori_loop(0, HG_TILE_CHUNKS, chunk, 0, unroll=True)
        for h in range(HG_HEADS):
            c = slice(h * HG_DIM, (h + 1) * HG_DIM)
            ohat, _ = _rms(o_ref[:, c])
            out_ref[:, c] = (ohat * og_ref[...] * _silu(g_ref[:, c])).astype(BF16)

    tile = _rows(HG_TILE, HG_WIDTH)
    return pl.pallas_call(
        body, name="hg_fwd", grid=(n_tiles,),
        out_shape=[jax.ShapeDtypeStruct((s, HG_WIDTH), BF16), jax.ShapeDtypeStruct((s, HG_WIDTH), F32),
                   jax.ShapeDtypeStruct((s // HG_CHUNK * HG_DIM, HG_WIDTH), F32)],
        in_specs=[tile] * 4 + [_whole((2, HG_WIDTH)), _whole((1, HG_DIM))],
        out_specs=[tile, tile, _rows(HG_TILE_CHUNKS * HG_DIM, HG_WIDTH)],
        scratch_shapes=[pltpu.VMEM((HG_HEADS, HG_DIM, HG_DIM), F32)] + [pltpu.VMEM((HG_TILE, HG_WIDTH), F32)] * 3,
        compiler_params=_params("arbitrary"),
    )(hq, hf, hi, hgt, logits, onorm_g)


def _hg_bwd(hq, hf, hi, hgt, logits, onorm_g, o, states, dout):
    s = hq.shape[0]
    n_tiles = s // HG_TILE

    def body(q_ref, f_ref, i_ref, g_ref, lg_ref, og_ref, o_ref, st_ref, d_ref,
             dq_ref, df_ref, di_ref, dg_ref, dog_ref, dlb_ref, dstate, qf_s, kk_s, lf_s, do_s):
        @pl.when(pl.program_id(0) == 0)
        def _():
            dstate[...] = jnp.zeros_like(dstate)
            dog_ref[...] = jnp.zeros_like(dog_ref)
            dlb_ref[...] = jnp.zeros_like(dlb_ref)

        og = og_ref[...]
        dog = jnp.zeros((1, HG_DIM), F32)
        for h in range(HG_HEADS):
            c = slice(h * HG_DIM, (h + 1) * HG_DIM)
            ohat, rstd = _rms(o_ref[:, c])
            gate = g_ref[:, c]
            d = d_ref[:, c]
            dg_ref[:, c] = (d * (ohat * og) * _dsilu(gate)).astype(BF16)
            dnormed = d * _silu(gate)
            dog += _rowsum(dnormed * ohat)
            do_s[:, c] = _rms_bwd(dnormed * og, ohat, rstd)
        dog_ref[...] += dog

        lb = _lower_bound(lg_ref)
        f = lb + (1.0 - lb) * _sigmoid(f_ref[...])
        kk_s[...] = 1.0 - f
        lf_s[...] = jnp.log(f)
        qf_s[...] = _silu(q_ref[...])
        causal, upper, tri, tri_t = _chunk_masks()

        def chunk(step, carry):
            ci = HG_TILE_CHUNKS - 1 - step
            rows = pl.ds(pl.multiple_of(ci * HG_CHUNK, HG_CHUNK), HG_CHUNK)
            srows = pl.ds(pl.multiple_of(ci * HG_DIM, HG_DIM), HG_DIM)
            lf = lf_s[rows, :]
            b = _dot_f32(tri, lf)
            bl = _rowsum(lf)
            ref = 0.5 * bl
            qf, kk, v, do = qf_s[rows, :], kk_s[rows, :], i_ref[rows, :], do_s[rows, :]
            eb, ebr, erb, ekd, ebl = jnp.exp(b), jnp.exp(b - ref), jnp.exp(ref - b), jnp.exp(bl - b), jnp.exp(bl)
            a_in, a_t, b_t, kd = qf * eb, qf * ebr, kk * erb, kk * ekd
            for h in range(HG_HEADS):
                c = slice(h * HG_DIM, (h + 1) * HG_DIM)
                st, dst = st_ref[srows, c], dstate[h]
                stb, dstb = st.astype(BF16), dst.astype(BF16)
                doh, vh = do[:, c], v[:, c]
                dob, vb = doh.astype(BF16), vh.astype(BF16)
                ain_h, at_h, bt_h, kd_h = a_in[:, c], a_t[:, c], b_t[:, c], kd[:, c]
                atb, btb = at_h.astype(BF16), bt_h.astype(BF16)
                d_ain = _dot(dob, stb)
                p_t = jnp.where(upper, _dot_nt(btb, atb), 0.0).astype(BF16)
                dp = jnp.where(causal, _dot_nt(dob, vb), 0.0).astype(BF16)
                dp_t = jnp.where(upper, _dot_nt(vb, dob), 0.0).astype(BF16)
                di_ref[rows, c] = (_dot(p_t, dob) + _dot_nt(kd_h.astype(BF16), dstb)).astype(BF16)
                d_at = _dot(dp, btb)
                d_bt = _dot(dp_t, atb)
                d_kd = _dot(vb, dstb)
                dqf = d_ain * eb[:, c] + d_at * ebr[:, c]
                dkk = d_bt * erb[:, c] + d_kd * ekd[:, c]
                db = d_ain * ain_h + d_at * atb.astype(F32) - d_bt * btb.astype(F32) - d_kd * kd_h
                dbl = _rowsum(d_kd * kd_h) + _rowsum(dst * st) * ebl[:, c]
                dstate[h] = _dot_tn(dob, ain_h.astype(BF16)) + dst * ebl[:, c]
                dlf = _dot_f32(tri_t, db) + dbl
                qv, fr = q_ref[rows, c], f_ref[rows, c]
                lbh = lb[:, c]
                sg = _sigmoid(fr)
                dfv = dlf / (lbh + (1.0 - lbh) * sg) - dkk
                df_ref[rows, c] = (dfv * (1.0 - lbh) * sg * (1.0 - sg)).astype(BF16)
                dlb_ref[:, c] += _rowsum(dfv * (1.0 - sg))
                dq_ref[rows, c] = (dqf * _dsilu(qv)).astype(BF16)
            return carry

        lax.fori_loop(0, HG_TILE_CHUNKS, chunk, 0, unroll=True)

    rev = pl.BlockSpec((HG_TILE, HG_WIDTH), lambda i: (n_tiles - 1 - i, 0))
    return pl.pallas_call(
        body, name="hg_bwd", grid=(n_tiles,),
        out_shape=[jax.ShapeDtypeStruct((s, HG_WIDTH), BF16)] * 4
        + [jax.ShapeDtypeStruct((1, HG_DIM), F32), jax.ShapeDtypeStruct((1, HG_WIDTH), F32)],
        in_specs=[rev] * 4 + [_whole((2, HG_WIDTH)), _whole((1, HG_DIM)), rev,
                              pl.BlockSpec((HG_TILE_CHUNKS * HG_DIM, HG_WIDTH), lambda i: (n_tiles - 1 - i, 0)), rev],
        out_specs=[rev] * 4 + [_whole((1, HG_DIM)), _whole((1, HG_WIDTH))],
        scratch_shapes=[pltpu.VMEM((HG_HEADS, HG_DIM, HG_DIM), F32)] + [pltpu.VMEM((HG_TILE, HG_WIDTH), F32)] * 4,
        compiler_params=_params("arbitrary"),
    )(hq, hf, hi, hgt, logits, onorm_g, o, states, dout)


TOKEN_GROUP = 16


def _att_geometry(dil):
    per_group = TOKEN_GROUP // dil
    return per_group, ATT_BLOCK // per_group, ATT_WIDTH if dil == 1 else 128, 1 if dil == TOKEN_GROUP else 4


def _att_consts(dil):
    per_group, ub = _att_geometry(dil)[:2]

    def pos(i):
        return i if dil == 1 else (i % ub) * per_group + i // ub

    lane = lax.broadcasted_iota(jnp.int32, (ATT_BLOCK, 128), 1)
    qi = pos(lax.broadcasted_iota(jnp.int32, (2 * ATT_BLOCK, ATT_BLOCK), 0) % ATT_BLOCK)
    kj = pos(lax.broadcasted_iota(jnp.int32, (2 * ATT_BLOCK, ATT_BLOCK), 1))
    return lane < ATT_HEAD_DIM, kj <= qi, lambda off: kj >= qi + off


def _load_tile(ref, dil, r, c, base=0):
    per_group, ub = _att_geometry(dil)[:2]
    if dil == 1:
        return ref[base:base + ATT_BLOCK, c]
    return jnp.concatenate([ref[pl.ds(base + dil * w + r, ub, stride=TOKEN_GROUP), c] for w in range(per_group)], axis=0)


def _store_tile(ref, dil, r, c, val, base=0):
    per_group, ub = _att_geometry(dil)[:2]
    if dil == 1:
        ref[base:base + ATT_BLOCK, c] = val
        return
    for w in range(per_group):
        ref[pl.ds(base + dil * w + r, ub, stride=TOKEN_GROUP), c] = val[w * ub:(w + 1) * ub]


def _stack_heads(x2, first):
    return jnp.concatenate([jnp.where(first, x2, 0.0), jnp.where(first, 0.0, x2)], axis=0)


def _stack_bcast(x2, first):
    other = pltpu.roll(x2, ATT_HEAD_DIM, axis=1)
    return jnp.concatenate([jnp.where(first, x2, other), jnp.where(first, other, x2)], axis=0)


def _unstack_heads(st, first):
    return jnp.where(first, st[:ATT_BLOCK], st[ATT_BLOCK:])


def _att_fwd(q, k, v, dil):
    seq, width = q.shape
    _, ub, lanes, nbs = _att_geometry(dil)
    rows = ub * TOKEN_GROUP
    n_steps = seq // (nbs * rows)

    def body(q_ref, k_ref, v_ref, kp_ref, vp_ref, o_ref, lse_ref):
        first, cur_ok, _band = _att_consts(dil)
        inner_ok = _band(0)
        edge_ok = _band(jnp.where(pl.program_id(0) > 0, 0, ATT_BLOCK))
        for b in range(nbs):
            base = b * rows
            prev_ok = edge_ok if b == 0 else inner_ok
            for r in range(dil):
                for j in range(lanes // 128):
                    c = slice(j * 128, (j + 1) * 128)
                    qst = _stack_heads(_load_tile(q_ref, dil, r, c, base) * ATT_SCALE, first).astype(BF16)
                    kc = _load_tile(k_ref, dil, r, c, base).astype(BF16)
                    vc = _load_tile(v_ref, dil, r, c, base).astype(BF16)
                    if b == 0:
                        kp, vp = _load_tile(kp_ref, dil, r, c).astype(BF16), _load_tile(vp_ref, dil, r, c).astype(BF16)
                    else:
                        kp = _load_tile(k_ref, dil, r, c, base - rows).astype(BF16)
                        vp = _load_tile(v_ref, dil, r, c, base - rows).astype(BF16)
                    sc = jnp.where(cur_ok, _dot_nt(qst, kc), NEG)
                    sp = jnp.where(prev_ok, _dot_nt(qst, kp), NEG)
                    mx = jnp.max(jnp.maximum(sc, sp), axis=-1, keepdims=True)
                    pc, pp = jnp.exp(sc - mx), jnp.exp(sp - mx)
                    den = jnp.sum(pc + pp, axis=-1, keepdims=True)
                    ost = (_dot(pc.astype(BF16), vc) + _dot(pp.astype(BF16), vp)) / den
                    lse = jnp.broadcast_to(mx + jnp.log(den), (2 * ATT_BLOCK, 128))
                    _store_tile(o_ref, dil, r, c, _unstack_heads(ost, first), base)
                    _store_tile(lse_ref, dil, r, c, _unstack_heads(lse, first), base)

    slab = pl.BlockSpec((nbs * rows, lanes), lambda n, j: (n, j))
    before = pl.BlockSpec((rows, lanes), lambda n, j: (jnp.maximum(n * nbs - 1, 0), j))
    return pl.pallas_call(
        body, name=f"att_fwd_d{dil}", grid=(n_steps, width // lanes),
        out_shape=[jax.ShapeDtypeStruct((seq, width), F32)] * 2,
        in_specs=[slab, slab, slab, before, before], out_specs=[slab, slab],
        compiler_params=_params("arbitrary", "arbitrary"),
    )(q, k, v, k, v)


def _att_bwd(q, k, v, do, cc, lse, dil):
    seq, width = q.shape
    _, ub, lanes, nbs = _att_geometry(dil)
    rows = ub * TOKEN_GROUP
    n_blocks = seq // rows
    n_steps = n_blocks // nbs

    def body(q_ref, k_ref, v_ref, do_ref, cc_ref, lse_ref, qx_ref, dox_ref, ccx_ref, lsex_ref,
             dq_ref, dk_ref, dv_ref, carry):
        first, cur_ok, _band = _att_consts(dil)
        step = pl.program_id(1)
        inner_ok = _band(0)
        edge_ok = _band(jnp.where(step < n_steps - 1, 0, ATT_BLOCK))

        @pl.when(step == 0)
        def _():
            carry[...] = jnp.zeros_like(carry)

        for b in range(nbs):
            base = b * rows
            last = b == nbs - 1
            next_ok = edge_ok if last else inner_ok
            for r in range(dil):
                for j in range(lanes // 128):
                    c = slice(j * 128, (j + 1) * 128)

                    def following(inner_ref, edge_ref):
                        return _load_tile(edge_ref, dil, r, c) if last else _load_tile(inner_ref, dil, r, c, base + rows)

                    qst = _stack_heads(_load_tile(q_ref, dil, r, c, base) * ATT_SCALE, first).astype(BF16)
                    qxst = _stack_heads(following(q_ref, qx_ref) * ATT_SCALE, first).astype(BF16)
                    dost = _stack_heads(_load_tile(do_ref, dil, r, c, base), first).astype(BF16)
                    doxst = _stack_heads(following(do_ref, dox_ref), first).astype(BF16)
                    lse_n = _stack_bcast(_load_tile(lse_ref, dil, r, c, base), first)
                    lse_x = _stack_bcast(following(lse_ref, lsex_ref), first)
                    cc_n = _stack_bcast(_load_tile(cc_ref, dil, r, c, base), first)
                    cc_x = _stack_bcast(following(cc_ref, ccx_ref), first)
                    kb = _load_tile(k_ref, dil, r, c, base).astype(BF16)
                    vb = _load_tile(v_ref, dil, r, c, base).astype(BF16)
                    p_cur = jnp.exp(jnp.where(cur_ok, _dot_nt(qst, kb), NEG) - lse_n)
                    p_next = jnp.exp(jnp.where(next_ok, _dot_nt(qxst, kb), NEG) - lse_x)
                    ds_cur = (p_cur * (_dot_nt(dost, vb) + cc_n)).astype(BF16)
                    ds_next = (p_next * (_dot_nt(doxst, vb) + cc_x)).astype(BF16)
                    dq_own = _load_tile(carry, dil, r, c) + _unstack_heads(_dot(ds_cur, kb), first)
                    _store_tile(dq_ref, dil, r, c, dq_own * ATT_SCALE, base)
                    _store_tile(carry, dil, r, c, _unstack_heads(_dot(ds_next, kb), first))
                    _store_tile(dk_ref, dil, r, c, _dot_tn(ds_cur, qst) + _dot_tn(ds_next, qxst), base)
                    _store_tile(dv_ref, dil, r, c, _dot_tn(p_cur.astype(BF16), dost) + _dot_tn(p_next.astype(BF16), doxst), base)

    slab = pl.BlockSpec((nbs * rows, lanes), lambda j, n: (n, j))
    after = pl.BlockSpec((rows, lanes), lambda j, n: (jnp.minimum((n + 1) * nbs, n_blocks - 1), j))
    return pl.pallas_call(
        body, name=f"att_bwd_d{dil}", grid=(width // lanes, n_steps),
        out_shape=[jax.ShapeDtypeStruct((seq, width), F32)] * 3,
        in_specs=[slab] * 6 + [after] * 4, out_specs=[slab] * 3,
        scratch_shapes=[pltpu.VMEM((rows, lanes), F32)],
        compiler_params=_params("arbitrary", "arbitrary"),
    )(q, k, v, do, cc, lse, q, do, cc, lse)


def _branch_weights(lses):
    mx = jnp.maximum(jnp.maximum(lses[0], lses[1]), lses[2])
    es = [jnp.exp(l - mx) for l in lses]
    inv = 1.0 / (es[0] + es[1] + es[2])
    return [e * inv for e in es]


def _att_combine(outs, lses, att_g):
    s = outs[0].shape[0]
    tm = 512

    def body(o0, o1, o2, l0, l1, l2, g_ref, att_ref, out_ref):
        ws = _branch_weights([l0[...], l1[...], l2[...]])
        att = ws[0] * o0[...] + ws[1] * o1[...] + ws[2] * o2[...]
        att_ref[...] = att
        ahat, _ = _rms(att)
        out_ref[...] = (ahat * g_ref[...]).astype(BF16)

    tile = _rows(tm, ATT_WIDTH)
    return pl.pallas_call(
        body, name="att_combine", grid=(s // tm,),
        out_shape=[jax.ShapeDtypeStruct((s, ATT_WIDTH), F32), jax.ShapeDtypeStruct((s, ATT_WIDTH), BF16)],
        in_specs=[tile] * 6 + [_whole((1, ATT_WIDTH))], out_specs=[tile, tile],
        compiler_params=_params("parallel"),
    )(*outs, *lses, att_g)


def _att_combine_bwd(datt_out, att, lses, att_g):
    s = att.shape[0]
    tm = 256

    def body(d_ref, att_ref, l0, l1, l2, g_ref, do0, do1, do2, cc0, cc1, cc2, dg_ref):
        @pl.when(pl.program_id(0) == 0)
        def _():
            dg_ref[...] = jnp.zeros_like(dg_ref)

        att = att_ref[...]
        ahat, rstd = _rms(att)
        d = d_ref[...]
        dg_ref[...] += _rowsum(d * ahat)
        datt = _rms_bwd(d * g_ref[...], ahat, rstd)
        hi = lax.broadcasted_iota(jnp.int32, (ATT_WIDTH, ATT_WIDTH), 0) // ATT_HEAD_DIM
        hj = lax.broadcasted_iota(jnp.int32, (ATT_WIDTH, ATT_WIDTH), 1) // ATT_HEAD_DIM
        head_sum = _dot_f32(datt * att, (hi == hj).astype(F32))
        ws = _branch_weights([l0[...], l1[...], l2[...]])
        for w, do_ref, cc_ref in zip(ws, (do0, do1, do2), (cc0, cc1, cc2)):
            do_ref[...] = w * datt
            cc_ref[...] = -w * head_sum

    tile = _rows(tm, ATT_WIDTH)
    return pl.pallas_call(
        body, name="att_combine_bwd", grid=(s // tm,),
        out_shape=[jax.ShapeDtypeStruct((s, ATT_WIDTH), F32)] * 6 + [jax.ShapeDtypeStruct((1, ATT_WIDTH), F32)],
        in_specs=[tile] * 5 + [_whole((1, ATT_WIDTH))], out_specs=[tile] * 6 + [_whole((1, ATT_WIDTH))],
        compiler_params=_params("arbitrary"),
    )(datt_out, att, *lses, att_g)


def _out_fwd(x, hg, at, mod, w_out):
    s = x.shape[0]
    tm = 512

    def body(x_ref, hg_ref, at_ref, mod_ref, w_ref, x1_ref):
        mix = _dot(hg_ref[...], w_ref[0:512, :]) + _dot(at_ref[...], w_ref[512:1024, :])
        x1_ref[...] = x_ref[...] + mod_ref[:, 2 * D_MODEL:3 * D_MODEL] * mix

    return pl.pallas_call(
        body, name="out_fwd", grid=(s // tm,), out_shape=jax.ShapeDtypeStruct((s, D_MODEL), F32),
        in_specs=[_rows(tm, D_MODEL), _rows(tm, 512), _rows(tm, 512), _whole((1, 6 * D_MODEL)), _whole((D_MODEL, D_MODEL))],
        out_specs=_rows(tm, D_MODEL), compiler_params=_params("parallel"),
    )(x, hg, at, mod, w_out)


def _out_bwd(dx1, hg, at, mod, w_out):
    s = dx1.shape[0]
    tm = 512
    n_steps = s // tm

    def body(dx_ref, hg_ref, at_ref, mod_ref, w_ref, dhg_ref, dat_ref, dw_ref, dwb_ref, dgate_ref):
        @pl.when(pl.program_id(0) == 0)
        def _():
            dw_ref[...] = jnp.zeros_like(dw_ref)
            dgate_ref[...] = jnp.zeros_like(dgate_ref)

        hg, at, dx = hg_ref[...], at_ref[...], dx_ref[...]
        mix = _dot(hg, w_ref[0:512, :]) + _dot(at, w_ref[512:1024, :])
        dgate_ref[...] += _rowsum(dx * mix)
        dmix = (mod_ref[:, 2 * D_MODEL:3 * D_MODEL] * dx).astype(BF16)
        dhg_ref[...] = _dot_nt(dmix, w_ref[0:512, :])
        dat_ref[...] = _dot_nt(dmix, w_ref[512:1024, :])
        dw_ref[0:512, :] += _dot_tn(hg, dmix)
        dw_ref[512:1024, :] += _dot_tn(at, dmix)

        @pl.when(pl.program_id(0) == n_steps - 1)
        def _():
            dwb_ref[...] = dw_ref[...].astype(BF16)

    return pl.pallas_call(
        body, name="out_bwd", grid=(n_steps,),
        out_shape=[jax.ShapeDtypeStruct((s, 512), F32)] * 2
        + [jax.ShapeDtypeStruct((D_MODEL, D_MODEL), F32), jax.ShapeDtypeStruct((D_MODEL, D_MODEL), BF16),
           jax.ShapeDtypeStruct((1, D_MODEL), F32)],
        in_specs=[_rows(tm, D_MODEL), _rows(tm, 512), _rows(tm, 512), _whole((1, 6 * D_MODEL)), _whole((D_MODEL, D_MODEL))],
        out_specs=[_rows(tm, 512), _rows(tm, 512), _whole((D_MODEL, D_MODEL)), _whole((D_MODEL, D_MODEL)), _whole((1, D_MODEL))],
        compiler_params=_params("arbitrary"),
    )(dx1, hg, at, mod, w_out)


FFN_CHUNK = 2816


def _ffn(x1, target, mod, g2, gf, w_gu, w_down):
    s = x1.shape[0]
    tm = 256
    n_chunks = D_FF // FFN_CHUNK

    def body(x_ref, t_ref, mod_ref, g2_ref, gf_ref, wgu_hbm, wd_hbm,
             dx_ref, h2_ref, act_ref, dau_ref, dff_ref, sums_ref, loss_ref, wgu, wd, a_s, u_s, sem):
        @pl.when(pl.program_id(0) == 0)
        def _():
            c1 = pltpu.make_async_copy(wgu_hbm, wgu, sem.at[0])
            c2 = pltpu.make_async_copy(wd_hbm, wd, sem.at[1])
            c1.start()
            c2.start()
            c1.wait()
            c2.wait()
            sums_ref[...] = jnp.zeros_like(sums_ref)
            loss_ref[...] = jnp.zeros_like(loss_ref)

        x1v = x_ref[...]
        xhat, rstd = _rms(x1v)
        g2 = g2_ref[...]
        n2 = xhat * g2
        scale2 = 1.0 + mod_ref[:, 4 * D_MODEL:5 * D_MODEL]
        gate2 = mod_ref[:, 5 * D_MODEL:6 * D_MODEL]
        hb = (n2 * scale2 + mod_ref[:, 3 * D_MODEL:4 * D_MODEL]).astype(BF16)
        h2_ref[...] = hb
        ff = jnp.zeros((tm, D_MODEL), F32)
        for j in range(n_chunks):
            c = slice(j * FFN_CHUNK, (j + 1) * FFN_CHUNK)
            cu = slice(D_FF + j * FFN_CHUNK, D_FF + (j + 1) * FFN_CHUNK)
            a = _dot_nt(hb, wgu[c, :])
            u = _dot_nt(hb, wgu[cu, :])
            a_s[:, c] = a
            u_s[:, c] = u
            act = (_silu(a) * u).astype(BF16)
            act_ref[:, c] = act
            ff += _dot(act, wd[c, :])
        x2 = x1v + gate2 * ff
        nf, rstd_f = _rms(x2)
        gfv = gf_ref[...]
        err = nf * gfv - t_ref[...]
        loss_ref[...] += 0.5 * jnp.sum(_rowsum(err * err), axis=-1, keepdims=True) * (1.0 / D_MODEL)
        dy = err * (1.0 / D_MODEL)
        dx2 = _rms_bwd(dy * gfv, nf, rstd_f)
        dffb = (gate2 * dx2).astype(BF16)
        dff_ref[...] = dffb
        dh = jnp.zeros((tm, D_MODEL), F32)
        for j in range(n_chunks):
            c = slice(j * FFN_CHUNK, (j + 1) * FFN_CHUNK)
            cu = slice(D_FF + j * FFN_CHUNK, D_FF + (j + 1) * FFN_CHUNK)
            dact = _dot_nt(dffb, wd[c, :])
            a, u = a_s[:, c], u_s[:, c]
            da = (dact * u * _dsilu(a)).astype(BF16)
            du = (dact * _silu(a)).astype(BF16)
            dau_ref[:, c] = da
            dau_ref[:, cu] = du
            dh += _dot(da, wgu[c, :]) + _dot(du, wgu[cu, :])
        dn = dh * scale2
        sums_ref[0:1, :] += _rowsum(dh)
        sums_ref[1:2, :] += _rowsum(dh * n2)
        sums_ref[2:3, :] += _rowsum(dx2 * ff)
        sums_ref[3:4, :] += _rowsum(dn * xhat)
        sums_ref[4:5, :] += _rowsum(dy * nf)
        dx_ref[...] = dx2 + _rms_bwd(dn * g2, xhat, rstd)

    vec = _whole((1, D_MODEL))
    hbm = pl.BlockSpec(memory_space=pl.ANY)
    return pl.pallas_call(
        body, name="ffn", grid=(s // tm,),
        out_shape=[jax.ShapeDtypeStruct((s, D_MODEL), F32), jax.ShapeDtypeStruct((s, D_MODEL), BF16),
                   jax.ShapeDtypeStruct((s, D_FF), BF16), jax.ShapeDtypeStruct((s, 2 * D_FF), BF16),
                   jax.ShapeDtypeStruct((s, D_MODEL), BF16), jax.ShapeDtypeStruct((8, D_MODEL), F32),
                   jax.ShapeDtypeStruct((1, 128), F32)],
        in_specs=[_rows(tm, D_MODEL), _rows(tm, D_MODEL), _whole((1, 6 * D_MODEL)), vec, vec, hbm, hbm],
        out_specs=[_rows(tm, D_MODEL), _rows(tm, D_MODEL), _rows(tm, D_FF), _rows(tm, 2 * D_FF), _rows(tm, D_MODEL),
                   _whole((8, D_MODEL)), _whole((1, 128))],
        scratch_shapes=[pltpu.VMEM((2 * D_FF, D_MODEL), BF16), pltpu.VMEM((D_FF, D_MODEL), BF16),
                        pltpu.VMEM((tm, D_FF), F32), pltpu.VMEM((tm, D_FF), F32), pltpu.SemaphoreType.DMA((2,))],
        compiler_params=_params("arbitrary"),
    )(x1, target, mod, g2, gf, w_gu, w_down)


def _weight_grad(a, b, name, rounded=False):
    s, m = a.shape
    n = b.shape[1]
    ts = min(s, 2048)
    n_steps = s // ts
    tm = max(t for t in range(128, m + 1, 128) if m % t == 0 and t * n * 4 <= 6 * 1024 * 1024)

    def body(a_ref, b_ref, o_ref, *ob_ref):
        @pl.when(pl.program_id(1) == 0)
        def _():
            o_ref[...] = jnp.zeros_like(o_ref)

        o_ref[...] += _dot_tn(a_ref[...], b_ref[...])
        if rounded:
            @pl.when(pl.program_id(1) == n_steps - 1)
            def _():
                ob_ref[0][...] = o_ref[...].astype(BF16)

    tile = pl.BlockSpec((tm, n), lambda j, i: (j, 0))
    return pl.pallas_call(
        body, name=name, grid=(m // tm, n_steps),
        out_shape=[jax.ShapeDtypeStruct((m, n), F32)] + [jax.ShapeDtypeStruct((m, n), BF16)] * rounded,
        in_specs=[pl.BlockSpec((ts, tm), lambda j, i: (i, j)), pl.BlockSpec((ts, n), lambda j, i: (i, 0))],
        out_specs=[tile] + [tile] * rounded,
        compiler_params=_params("parallel", "arbitrary"),
    )(a, b)


def _adamw_math(w, g, m, v):
    m = ADAM_B1 * m + (1.0 - ADAM_B1) * g
    v = ADAM_B2 * v + (1.0 - ADAM_B2) * (g * g)
    m_hat = m / (1.0 - ADAM_B1 ** ADAM_STEP)
    v_hat = v / (1.0 - ADAM_B2 ** ADAM_STEP)
    delta = -ADAM_LR * (m_hat / (jnp.sqrt(v_hat) + ADAM_EPS) + ADAM_WD * w)
    return delta, m, v


def _adamw_shard(where, w, m, v, partial, got, name):
    r, c = w.shape
    tr = _shard_rows(r)
    lead = partial.ndim - 2
    n_got = got.shape[0]

    def body(where_ref, w_ref, m_ref, v_ref, own_ref, *rest):
        got_refs, (grad_ref, d_ref, nm_ref, nv_ref) = rest[:n_got], rest[n_got:]
        g = own_ref[...]
        for g_ref in got_refs:
            g = g + g_ref[...].astype(F32)
        grad_ref[...] = g
        d_ref[...], nm_ref[...], nv_ref[...] = _adamw_math(w_ref[...], g, m_ref[...], v_ref[...])

    tile = pl.BlockSpec((tr, c), lambda i, where_ref: (i, 0))
    own = pl.BlockSpec((None,) * lead + (tr, c), lambda i, where_ref: (*[where_ref[d] for d in range(lead)], i, 0))
    part = [pl.BlockSpec((None, tr, c), functools.partial(lambda j, i, where_ref: (j, i, 0), j)) for j in range(n_got)]
    return pl.pallas_call(
        body, name=name,
        grid_spec=pltpu.PrefetchScalarGridSpec(num_scalar_prefetch=1, grid=(r // tr,), in_specs=[tile] * 3 + [own] + part,
                                               out_specs=[tile] * 4),
        out_shape=[jax.ShapeDtypeStruct((r, c), F32)] * 4, compiler_params=_params("parallel"),
    )(where, w, m, v, partial, *[got] * n_got)


def _small_update(small_all, dmod_blocks, c_all, logits, w_ada, m_ada, v_ada, smalls):
    def body(sm_ref, dm_ref, c_ref, lg_ref, wa_ref, ma_ref, va_ref, *rest):
        ins, outs = rest[:21], rest[21:]
        _, me = _flip(0)
        tot = sm_ref[0:1, :]
        for i in range(1, N_DEV):
            tot = tot + sm_ref[i:i + 1, :]
        loss_ref = outs[0]
        loss_ref[...] = tot[:, SM_LOSS:SM_LOSS + 128]
        g_ada = lax.dot_general(_silu(c_ref[...]), dm_ref[me], (((0,), (0,)), ((), ())),
                                preferred_element_type=F32, precision=HIGHEST)
        outs[1][...] = g_ada
        outs[2][...], outs[3][...], outs[4][...] = _adamw_math(wa_ref[...], g_ada, ma_ref[...], va_ref[...])
        p0 = _lower_bound(lg_ref)
        dl0 = tot[:, SM_LB:SM_LB + 512] * p0 * (1.0 - p0)
        grads = [tot[:, SM_MOD:SM_MOD + 6 * D_MODEL], tot[:, SM_G1:SM_G1 + D_MODEL], tot[:, SM_G2:SM_G2 + D_MODEL],
                 tot[:, SM_GF:SM_GF + D_MODEL], tot[:, SM_ATT:SM_ATT + 512], tot[:, SM_HG:SM_HG + 128],
                 jnp.where(lax.broadcasted_iota(jnp.int32, (2, 512), 0) == 0, dl0, -dl0)]
        for i, g in enumerate(grads):
            w_ref, m_ref, v_ref = ins[3 * i:3 * i + 3]
            o = outs[5 + 4 * i:9 + 4 * i]
            o[0][...] = g
            o[1][...], o[2][...], o[3][...] = _adamw_math(w_ref[...], g, m_ref[...], v_ref[...])

    flat = [t for trio in smalls for t in trio]
    vm = pl.BlockSpec(memory_space=pltpu.VMEM)
    out_shape = [jax.ShapeDtypeStruct((1, 128), F32)] + [jax.ShapeDtypeStruct(w_ada.shape, F32)] * 4
    for trio in smalls:
        out_shape += [jax.ShapeDtypeStruct(trio[0].shape, F32)] * 4
    return pl.pallas_call(
        body, name="small_update", out_shape=out_shape,
        in_specs=[vm] * (7 + len(flat)), out_specs=[vm] * len(out_shape),
        compiler_params=pltpu.CompilerParams(vmem_limit_bytes=V7X_VMEM_LIMIT),
    )(small_all, dmod_blocks, c_all, logits, w_ada, m_ada, v_ada, *flat)


def kernel(x, c, w_ada, b_ada, norm1_g, w_in, hg_lb_logits, hg_onorm_g, att_onorm_g, w_out, norm2_g, w_gate_up, w_down, final_g, loss_target, m_w_ada, m_b_ada, m_norm1_g, m_w_in, m_hg_lb_logits, m_hg_onorm_g, m_att_onorm_g, m_w_out, m_norm2_g, m_w_gate_up, m_w_down, m_final_g, v_w_ada, v_b_ada, v_norm1_g, v_w_in, v_hg_lb_logits, v_hg_onorm_g, v_att_onorm_g, v_w_out, v_norm2_g, v_w_gate_up, v_w_down, v_final_g):
    x2d, target = x[0], loss_target[0]
    seq = x2d.shape[0]
    assert seq % (ATT_BLOCK * max(DILATIONS)) == 0 and seq % HG_TILE == 0
    gf = final_g.reshape(1, D_MODEL)

    c_all = _exchange_small(c.reshape(8, D_MODEL // 8), None, "gather_c").reshape(N_DEV, D_MODEL)
    ada = _ada_rows(c_all, w_ada[0], b_ada)
    mod = _exchange_small(ada, 1, "scatter_mod").reshape(1, 6 * D_MODEL)

    core = lax.axis_index("c").astype(jnp.int32).reshape(1)
    chip = (2 * lax.axis_index("x") + lax.axis_index("y")).astype(jnp.int32).reshape(1)
    me = 4 * lax.axis_index("x") + 2 * lax.axis_index("y") + lax.axis_index("c")

    g_in, = _gather_weights([w_in[0].T.astype(BF16)])
    w_in_b = g_in.reshape(IN_WIDTH, D_MODEL)
    rest_shards = [w_out[0].astype(BF16), w_gate_up[0].T.astype(BF16), w_down[0].astype(BF16)]
    lands = [lax.empty((N_DEV,) + s.shape, BF16) for s in rest_shards]
    g_send, g_recv, g_srcs, g_lands, tok = _copies_start("gather_rest_start", _plan_gather_own, 12, rest_shards, lands, [w_in_b, mod])
    flight = {}

    def stage(name, *vals):
        if name == "attention_begun":
            flight["shards"], got = _copies_wait("gather_rest_wait", _plan_gather_own, g_send, g_recv, g_srcs, g_lands, list(vals))
            flight["pass"] = _copies_start("gather_pass_start", _plan_gather_pass, 9, [], got, [])
            return flight["pass"][4][0:1, 0:1]
        if name == "mixer_weights_done":
            shapes = [(4, 2, D_MODEL // N_DEV, D_MODEL), (4, 2, 2 * D_FF // N_DEV, D_MODEL), (4, 2, D_FF // N_DEV, D_MODEL)]
            flight["grads"] = [g32.reshape(sh) for (g32, _), sh in zip(vals, shapes)]
            rounded = [g16.reshape(sh) for (_, g16), sh in zip(vals, shapes)]
            direct_lands = [lax.empty((N_DEV - 1,) + sh[2:], BF16) for sh in shapes]
            flight["direct"] = _copies_start("reduce_rest_start", _plan_reduce_direct, 21, rounded, direct_lands, [])
            return flight["direct"][4][0:1, 0:1]
        raise ValueError(name)

    def rest_weights(after):
        s, r, _, p_lands, _ = flight["pass"]
        _, got = _copies_wait("gather_pass_wait", _plan_gather_pass, s, r, [], p_lands, [after])
        full = [lax.dynamic_update_index_in_dim(g, shard, me, 0) for g, shard in zip(got, flight["shards"])]
        return full[0].reshape(D_MODEL, D_MODEL), full[1].reshape(2 * D_FF, D_MODEL), full[2].reshape(D_FF, D_MODEL)

    grad_x, dw_in, small = _block_step(x2d, target, mod + tok[0:1, 0:1], norm1_g, hg_lb_logits, hg_onorm_g, att_onorm_g, norm2_g, gf,
                                       w_in_b, rest_weights, stage)

    g_in8 = dw_in.reshape(4, 2, IN_WIDTH // N_DEV, D_MODEL)
    in_pairs = _copies_start("reduce_pairs_in_start", _plan_reduce_pairs, 4, [g_in8], [lax.empty((4,) + g_in8.shape[2:], F32)], [])
    s, r, srcs, d_lands, _ = flight["direct"]
    _, recv_rest = _copies_wait("reduce_rest_wait", _plan_reduce_direct, s, r, srcs, d_lands, [in_pairs[4]])
    small_rows = jnp.pad(small + in_pairs[4][0:1, 0:1], ((0, 0), (0, SM_PADDED - SM_WIDTH))).reshape(SM_PADDED // 128, 128)
    small_all = _exchange_small(small_rows, None, "gather_small").reshape(N_DEV, SM_PADDED)[:, :SM_WIDTH]
    in_grads, got_in = _copies_wait("reduce_pairs_in_wait", _plan_reduce_pairs, in_pairs[0], in_pairs[1], in_pairs[2], in_pairs[3],
                                    [small_all])
    in_s32, in_s16 = _pair_sum(core, in_grads[0], got_in[0], "pair_sum_in")
    in_chips = _copies_start("reduce_chips_in_start", _plan_reduce_chips, 3, [in_s16], [lax.empty((3,) + in_s16.shape[1:], BF16)], [])
    big, updated = {}, []
    rest_params = [("w_out", w_out, m_w_out, v_w_out), ("w_gate_up", w_gate_up, m_w_gate_up, v_w_gate_up), ("w_down", w_down, m_w_down, v_w_down)]
    mine = jnp.concatenate([chip, core])
    for (n, w, m, v), g32, got in zip(rest_params, flight["grads"], recv_rest):
        if n == "w_gate_up":
            outs4 = _adamw_shard(mine, w[0].T, m[0].T, v[0].T, g32, got, f"adamw_{n}")
            big[n] = [t.T[None] for t in outs4]
        else:
            outs4 = _adamw_shard(mine, w[0], m[0], v[0], g32, got, f"adamw_{n}")
            big[n] = [t[None] for t in outs4]
        updated.append(outs4[3])
    c_all = c_all + in_chips[4][0:1, 0:1]
    smalls = [(b_ada, m_b_ada, v_b_ada), (norm1_g, m_norm1_g, v_norm1_g), (norm2_g, m_norm2_g, v_norm2_g),
              (gf, m_final_g.reshape(1, D_MODEL), v_final_g.reshape(1, D_MODEL)),
              (att_onorm_g, m_att_onorm_g, v_att_onorm_g), (hg_onorm_g, m_hg_onorm_g, v_hg_onorm_g),
              (hg_lb_logits, m_hg_lb_logits, v_hg_lb_logits)]
    dmod_blocks = small_all[:, :6 * D_MODEL].reshape(N_DEV, N_DEV, 6 * D_MODEL // N_DEV).transpose(1, 0, 2)
    res = _small_update(small_all, dmod_blocks, c_all, hg_lb_logits, w_ada[0], m_w_ada[0], v_w_ada[0], smalls)
    _, recv_in = _copies_wait("reduce_chips_in_wait", _plan_reduce_chips, in_chips[0], in_chips[1], in_chips[2], in_chips[3],
                              [res[0]] + updated)
    big["w_in"] = [t.T[None] for t in _adamw_shard(chip, w_in[0].T, m_w_in[0].T, v_w_in[0].T, in_s32, recv_in[0], "adamw_w_in")]
    loss = res[0][0, 0]
    ada4 = [t[None] for t in res[1:5]]
    sm4 = {n: list(res[5 + 4 * i:9 + 4 * i]) for i, n in enumerate(["b_ada", "norm1_g", "norm2_g", "final_g", "att", "hg", "lb"])}
    sm4["final_g"] = [t.reshape(D_MODEL) for t in sm4["final_g"]]

    order = [ada4, sm4["b_ada"], sm4["norm1_g"], big["w_in"], sm4["lb"], sm4["hg"], sm4["att"], big["w_out"], sm4["norm2_g"],
             big["w_gate_up"], big["w_down"], sm4["final_g"]]
    return (loss, grad_x[None], *[o[0] for o in order], *[o[1] for o in order], *[o[2] for o in order], *[o[3] for o in order])


def _block_step(x2d, target, mod, norm1_g, hg_lb_logits, hg_onorm_g, att_onorm_g, norm2_g, gf, w_in_b, rest_weights, stage):
    h1, hq, hf, hi, hgt, aq, ak, av = _in_fwd(x2d, mod, norm1_g, w_in_b)
    hg_out, hg_o, hg_states = _hg_fwd(hq, hf, hi, hgt, hg_lb_logits, hg_onorm_g)
    branch = [_att_fwd(aq, ak, av, d) for d in DILATIONS[:2]]
    att_g = att_onorm_g + stage("attention_begun", branch[0][0], branch[1][0])
    branch += [_att_fwd(aq, ak, av, d) for d in DILATIONS[2:]]
    outs = [b[0] for b in branch]
    lses = [b[1] for b in branch]
    att, att_out = _att_combine(outs, lses, att_g)
    w_out_b, w_gu_b, w_down_b = rest_weights(att_out)
    x1 = _out_fwd(x2d, hg_out, att_out, mod, w_out_b)

    dx1, h2, act, dau, dff, ffn_sums, loss_part = _ffn(x1, target, mod, norm2_g, gf, w_gu_b, w_down_b)
    dw_gu = _weight_grad(dau, h2, "dw_gate_up", rounded=True)
    dw_down = _weight_grad(act, dff, "dw_down", rounded=True)

    dhg, dat, dw_out, dw_out_b, dgate1 = _out_bwd(dx1, hg_out, att_out, mod, w_out_b)
    att_g = att_onorm_g + stage("mixer_weights_done", (dw_out, dw_out_b), dw_gu, dw_down)
    comb = _att_combine_bwd(dat, att, lses, att_g)
    dos, ccs, d_att_g = comb[0:3], comb[3:6], comb[6]
    datt = []
    for i, d in enumerate(DILATIONS):
        datt.append(_att_bwd(aq, ak, av, dos[i], ccs[i], lses[i], d))
    dhq, dhf, dhi, dhgt, d_hg_g, d_lb = _hg_bwd(hq, hf, hi, hgt, hg_lb_logits, hg_onorm_g, hg_o, hg_states, dhg)
    dps = [dhq, dhf, dhi, dhgt] + [datt[i][j] for j in range(3) for i in range(3)]
    grad_x, dp_b, dshift1, dscale1, d_g1 = _in_bwd(x2d, dx1, mod, norm1_g, w_in_b, dps)
    dw_in, = _weight_grad(dp_b, h1, "dw_in")
    small = jnp.concatenate([dshift1, dscale1, dgate1, ffn_sums[0:1], ffn_sums[1:2], ffn_sums[2:3], d_g1, ffn_sums[3:4],
                             ffn_sums[4:5], d_att_g, d_lb, d_hg_g, loss_part], axis=1)
    return grad_x, dw_in, small
```

```python
import functools

import jax
import jax.numpy as jnp
from jax import lax
from jax.experimental import pallas as pl
from jax.experimental.pallas import tpu as pltpu

F32 = jnp.float32
BF16 = jnp.bfloat16
HIGHEST = lax.Precision.HIGHEST
MESH = pl.DeviceIdType.MESH

D_MODEL = 1024
N_DEV = 8
HG_HEADS = 4
HG_DIM = 128
HG_WIDTH = HG_HEADS * HG_DIM
HG_CHUNK = 128
ATT_WIDTH = 512
ATT_HEAD_DIM = 64
ATT_BLOCK = 128
DILATIONS = (1, 4, 16)
ATT_SCALE = ATT_HEAD_DIM ** -0.5
D_FF = 2816
IN_WIDTH = 7 * 512
RMS_EPS = 1e-6
NEG = -1e30

ADAM_LR = 0.001
ADAM_B1 = 0.9
ADAM_B2 = 0.999
ADAM_EPS = 1e-08
ADAM_WD = 0.01
ADAM_STEP = 10

V7X_VMEM_LIMIT = 56 * 1024 * 1024

SM_MOD = 0
SM_G1 = 6 * D_MODEL
SM_G2 = 7 * D_MODEL
SM_GF = 8 * D_MODEL
SM_ATT = 9 * D_MODEL
SM_LB = 9 * D_MODEL + 512
SM_HG = 10 * D_MODEL
SM_LOSS = 10 * D_MODEL + 128
SM_WIDTH = 10 * D_MODEL + 256
SM_PADDED = 88 * 128


def _params(*sem, vmem=V7X_VMEM_LIMIT):
    return pltpu.CompilerParams(dimension_semantics=sem, vmem_limit_bytes=vmem)


def _dot(a, b):
    return jnp.dot(a, b, preferred_element_type=F32)


def _dot_nt(a, b):
    return lax.dot_general(a, b, (((1,), (1,)), ((), ())), preferred_element_type=F32)


def _dot_tn(a, b):
    return lax.dot_general(a, b, (((0,), (0,)), ((), ())), preferred_element_type=F32)


def _dot_f32(a, b):
    return jnp.dot(a, b, preferred_element_type=F32, precision=HIGHEST)


def _sigmoid(x):
    return 1.0 / (1.0 + jnp.exp(-x))


def _silu(x):
    return x * _sigmoid(x)


def _dsilu(x):
    s = _sigmoid(x)
    return s * (1.0 + x * (1.0 - s))


def _rms(x):
    rstd = lax.rsqrt(jnp.mean(x * x, axis=-1, keepdims=True) + RMS_EPS)
    return x * rstd, rstd


def _rms_bwd(dn, xhat, rstd):
    return rstd * (dn - xhat * jnp.mean(dn * xhat, axis=-1, keepdims=True))


def _rowsum(x):
    return jnp.sum(x, axis=0, keepdims=True)


def _rows(tm, n):
    return pl.BlockSpec((tm, n), lambda i: (i, 0))


def _whole(shape):
    return pl.BlockSpec(shape, lambda i: (0,) * len(shape))


def _mesh_pos():
    return lax.axis_index("x"), lax.axis_index("y"), lax.axis_index("c")


def _flip(k):
    x, y, c = _mesh_pos()
    px = 1 - x if k & 4 else x
    py = 1 - y if k & 2 else y
    pc = 1 - c if k & 1 else c
    return (px, py, pc), 4 * px + 2 * py + pc


def _exchange_small(x, rows_per_peer, name):
    r_all, cols = x.shape
    r_out = r_all if rows_per_peer is None else rows_per_peer

    def body(x_ref, out_ref, send_sems, recv_sems):
        _, me = _flip(0)

        def src(pid):
            if rows_per_peer is None:
                return x_ref
            return x_ref.at[pl.ds(pl.multiple_of(pid * r_out, r_out), r_out), :]

        if rows_per_peer is None:
            out_ref[me] = x_ref[...]
        else:
            out_ref[me] = x_ref[pl.ds(pl.multiple_of(me * r_out, r_out), r_out), :]
        sends = []
        for k in range(1, N_DEV):
            dev, pid = _flip(k)
            cp = pltpu.make_async_remote_copy(src_ref=src(pid), dst_ref=out_ref.at[me], send_sem=send_sems.at[k - 1],
                                              recv_sem=recv_sems.at[k - 1], device_id=dev, device_id_type=MESH)
            cp.start()
            sends.append(cp)
        for k in range(1, N_DEV):
            dev, pid = _flip(k)
            pltpu.make_async_remote_copy(src_ref=src(pid), dst_ref=out_ref.at[pid], send_sem=send_sems.at[k - 1],
                                         recv_sem=recv_sems.at[k - 1], device_id=dev, device_id_type=MESH).wait_recv()
        for cp in sends:
            cp.wait_send()

    return pl.pallas_call(
        body, name=name,
        out_shape=jax.ShapeDtypeStruct((N_DEV, r_out, cols), x.dtype),
        in_specs=[pl.BlockSpec(memory_space=pltpu.VMEM)],
        out_specs=pl.BlockSpec(memory_space=pltpu.VMEM),
        scratch_shapes=[pltpu.SemaphoreType.DMA((N_DEV - 1,)), pltpu.SemaphoreType.DMA((N_DEV - 1,))],
    )(x)


def _gather_weights(shards):
    n = len(shards)

    def body(*refs):
        xs, outs = refs[:n], refs[n:2 * n]
        send_sems, recv_sems, local_sems = refs[2 * n:]
        x, y, c = _mesh_pos()
        me, sibling = (x, y, c), (x, y, 1 - c)
        chips = [(1 - x, y), (x, 1 - y), (1 - x, 1 - y)]

        def blk(a, px, py, pc):
            return outs[a].at[4 * px + 2 * py + pc]

        def copy(a, k, block, to, src=None):
            return pltpu.make_async_remote_copy(
                src_ref=blk(a, *block) if src is None else src, dst_ref=blk(a, *block),
                send_sem=send_sems.at[a * 7 + k], recv_sem=recv_sems.at[a * 7 + k], device_id=to, device_id_type=MESH)

        mine = [pltpu.make_async_copy(xs[a], blk(a, *me), local_sems.at[a]) for a in range(n)]
        for cp in mine:
            cp.start()
        first = []
        for a in range(n):
            first.append(copy(a, 0, me, sibling, src=xs[a]))
            first += [copy(a, 1 + j, me, (*chip, c), src=xs[a]) for j, chip in enumerate(chips)]
        for cp in first:
            cp.start()
        passed = []
        for j, chip in enumerate(chips):
            for a in range(n):
                copy(a, 1 + j, (*chip, c), me).wait_recv()
                cp = copy(a, 4 + j, (*chip, c), sibling)
                cp.start()
                passed.append(cp)
        for a in range(n):
            copy(a, 0, sibling, me).wait_recv()
            for j, chip in enumerate(chips):
                copy(a, 4 + j, (*chip, 1 - c), me).wait_recv()
        for cp in first + passed:
            cp.wait_send()
        for cp in mine:
            cp.wait()

    hbm = pl.BlockSpec(memory_space=pl.ANY)
    return pl.pallas_call(
        body, name="gather_weights",
        out_shape=[jax.ShapeDtypeStruct((N_DEV,) + s.shape, s.dtype) for s in shards],
        in_specs=[hbm] * n, out_specs=[hbm] * n,
        scratch_shapes=[pltpu.SemaphoreType.DMA((7 * n,)), pltpu.SemaphoreType.DMA((7 * n,)), pltpu.SemaphoreType.DMA((n,))],
    )(*shards)


_HBM = pl.BlockSpec(memory_space=pltpu.HBM)
_SEM = pl.BlockSpec(memory_space=pltpu.SEMAPHORE)
_DATAFLOW = pltpu.SideEffectType.DATAFLOW_SIDE_EFFECTING


def _copies_start(name, plan, n_copies, srcs, lands, after):
    bufs = list(srcs) + list(lands)
    nb = len(bufs)

    def body(*refs):
        ins, send_sems, recv_sems, token = refs[:nb], refs[nb + len(after)], refs[nb + len(after) + 1], refs[-1]
        for i, (src, dst, dev) in enumerate(plan(ins[:len(srcs)], ins[len(srcs):])):
            pltpu.make_async_remote_copy(src_ref=src, dst_ref=dst, send_sem=send_sems.at[i], recv_sem=recv_sems.at[i],
                                         device_id=dev, device_id_type=MESH).start()
        token[...] = jnp.zeros_like(token)

    outs = pl.pallas_call(
        body, name=name,
        out_shape=(pltpu.SemaphoreType.DMA((n_copies,)), pltpu.SemaphoreType.DMA((n_copies,)),
                   *[pltpu.HBM(b.shape, b.dtype) for b in bufs], jax.ShapeDtypeStruct((8, 128), F32)),
        in_specs=[_HBM] * nb + [pl.BlockSpec(memory_space=pl.ANY)] * len(after),
        out_specs=(_SEM, _SEM, *[_HBM] * nb, pl.BlockSpec(memory_space=pltpu.VMEM)),
        input_output_aliases={i: 2 + i for i in range(nb)},
        compiler_params=pltpu.CompilerParams(has_side_effects=_DATAFLOW),
    )(*[pltpu.with_memory_space_constraint(b, pltpu.HBM) for b in bufs], *after)
    return outs[0], outs[1], list(outs[2:2 + len(srcs)]), list(outs[2 + len(srcs):2 + nb]), outs[-1]


def _copies_wait(name, plan, send_sems, recv_sems, srcs, lands, after):
    bufs = list(srcs) + list(lands)
    nb = len(bufs)

    def body(*refs):
        ins, send_ref, recv_ref = refs[:nb], refs[nb], refs[nb + 1]
        for i, (src, dst, dev) in enumerate(plan(ins[:len(srcs)], ins[len(srcs):])):
            cp = pltpu.make_async_remote_copy(src_ref=src, dst_ref=dst, send_sem=send_ref.at[i], recv_sem=recv_ref.at[i],
                                              device_id=dev, device_id_type=MESH)
            cp.wait_send()
            cp.wait_recv()

    outs = pl.pallas_call(
        body, name=name, out_shape=[pltpu.HBM(b.shape, b.dtype) for b in bufs],
        in_specs=[_HBM] * nb + [_SEM, _SEM] + [pl.BlockSpec(memory_space=pl.ANY)] * len(after), out_specs=[_HBM] * nb,
        input_output_aliases={i: i for i in range(nb)},
        compiler_params=pltpu.CompilerParams(has_side_effects=_DATAFLOW),
    )(*bufs, send_sems, recv_sems, *after)
    return list(outs[:len(srcs)]), list(outs[len(srcs):])


def _plan_gather_own(srcs, lands):
    _, me = _flip(0)
    return [(srcs[a], lands[a].at[me], _flip(k)[0]) for a in range(len(srcs)) for k in (1, 4, 2, 6)]


def _plan_gather_pass(srcs, lands):
    sibling = _flip(1)[0]
    plan = []
    for land in lands:
        for k in (4, 2, 6):
            block = land.at[_flip(k)[1]]
            plan.append((block, block, sibling))
    return plan


def _plan_reduce_pairs(srcs, lands):
    x, y, c = _mesh_pos()
    return [(srcs[a].at[chip, 1 - c], lands[a].at[chip], (x, y, 1 - c)) for a in range(len(srcs)) for chip in range(4)]


def _plan_reduce_chips(srcs, lands):
    plan = []
    for a in range(len(srcs)):
        for j, k in enumerate((4, 2, 6)):
            dev = _flip(k)[0]
            plan.append((srcs[a].at[2 * dev[0] + dev[1]], lands[a].at[j], dev))
    return plan


def _plan_reduce_direct(srcs, lands):
    plan = []
    for a in range(len(srcs)):
        for k in range(1, N_DEV):
            dev = _flip(k)[0]
            plan.append((srcs[a].at[2 * dev[0] + dev[1], dev[2]], lands[a].at[k - 1], dev))
    return plan


def _shard_rows(r):
    return r // 2 if r % 32 == 0 else r


def _pair_sum(core, grads, got, name):
    _, _, r, c = grads.shape
    tr = _shard_rows(r)

    def body(core_ref, a_ref, b_ref, o_ref, ob_ref):
        s = a_ref[...] + b_ref[...]
        o_ref[...] = s
        ob_ref[...] = s.astype(BF16)

    spec = pl.BlockSpec((None, tr, c), lambda i, j, core_ref: (i, j, 0))
    return pl.pallas_call(
        body, name=name,
        grid_spec=pltpu.PrefetchScalarGridSpec(
            num_scalar_prefetch=1, grid=(4, r // tr),
            in_specs=[pl.BlockSpec((None, None, tr, c), lambda i, j, core_ref: (i, core_ref[0], j, 0)), spec],
            out_specs=[spec, spec]),
        out_shape=[jax.ShapeDtypeStruct((4, r, c), F32), jax.ShapeDtypeStruct((4, r, c), BF16)],
        compiler_params=_params("parallel", "parallel"),
    )(core, grads, got)


def _ada_rows(c_all, w_ada, b_ada):
    n_cols = w_ada.shape[1]

    def body(c_ref, w_ref, b_ref, o_ref):
        _, me = _flip(0)
        bias = b_ref[:, pl.ds(pl.multiple_of(me * n_cols, 128), n_cols)]
        o_ref[...] = _dot_f32(_silu(c_ref[...]), w_ref[...]) + bias

    return pl.pallas_call(
        body, name="ada_rows", out_shape=jax.ShapeDtypeStruct((N_DEV, n_cols), F32),
        in_specs=[pl.BlockSpec(memory_space=pltpu.VMEM)] * 3, out_specs=pl.BlockSpec(memory_space=pltpu.VMEM),
    )(c_all, w_ada, b_ada)


def _in_fwd(x, mod, g1, w_in):
    s = x.shape[0]
    tm = 256

    def body(x_ref, mod_ref, g_ref, w_ref, h_ref, *outs):
        xhat, _ = _rms(x_ref[...])
        h = (xhat * g_ref[...]) * (1.0 + mod_ref[:, D_MODEL:2 * D_MODEL]) + mod_ref[:, 0:D_MODEL]
        hb = h.astype(BF16)
        h_ref[...] = hb
        for j, o_ref in enumerate(outs):
            o_ref[...] = _dot_nt(hb, w_ref[j * 512:(j + 1) * 512, :])

    return pl.pallas_call(
        body, name="in_fwd", grid=(s // tm,),
        out_shape=[jax.ShapeDtypeStruct((s, D_MODEL), BF16)] + [jax.ShapeDtypeStruct((s, 512), F32)] * 7,
        in_specs=[_rows(tm, D_MODEL), _whole((1, 6 * D_MODEL)), _whole((1, D_MODEL)), _whole((IN_WIDTH, D_MODEL))],
        out_specs=[_rows(tm, D_MODEL)] + [_rows(tm, 512)] * 7,
        compiler_params=_params("parallel"),
    )(x, mod, g1, w_in)


def _in_bwd(x, dx1, mod, g1, w_in, dps):
    s = x.shape[0]
    tm = 256

    def body(x_ref, dx_ref, mod_ref, g_ref, w_ref, *rest):
        dp_refs, (gx_ref, dpb_ref, dsh_ref, dsc_ref, dg_ref) = rest[:13], rest[13:]
        pieces = [dp_refs[j][...] for j in range(4)]
        pieces += [dp_refs[4 + 3 * j][...] + dp_refs[5 + 3 * j][...] + dp_refs[6 + 3 * j][...] for j in range(3)]
        for j, p in enumerate(pieces):
            dpb_ref[:, j * 512:(j + 1) * 512] = p.astype(BF16)
        dh = _dot(dpb_ref[...], w_ref[...])
        xhat, rstd = _rms(x_ref[...])
        g = g_ref[...]
        scale1 = 1.0 + mod_ref[:, D_MODEL:2 * D_MODEL]
        n1 = xhat * g

        @pl.when(pl.program_id(0) == 0)
        def _():
            dsh_ref[...] = jnp.zeros_like(dsh_ref)
            dsc_ref[...] = jnp.zeros_like(dsc_ref)
            dg_ref[...] = jnp.zeros_like(dg_ref)

        dsh_ref[...] += _rowsum(dh)
        dsc_ref[...] += _rowsum(dh * n1)
        dn = dh * scale1
        dg_ref[...] += _rowsum(dn * xhat)
        gx_ref[...] = dx_ref[...] + _rms_bwd(dn * g, xhat, rstd)

    vec = _whole((1, D_MODEL))
    return pl.pallas_call(
        body, name="in_bwd", grid=(s // tm,),
        out_shape=[jax.ShapeDtypeStruct((s, D_MODEL), F32), jax.ShapeDtypeStruct((s, IN_WIDTH), BF16)]
        + [jax.ShapeDtypeStruct((1, D_MODEL), F32)] * 3,
        in_specs=[_rows(tm, D_MODEL), _rows(tm, D_MODEL), _whole((1, 6 * D_MODEL)), vec, _whole((IN_WIDTH, D_MODEL))]
        + [_rows(tm, 512)] * 13,
        out_specs=[_rows(tm, D_MODEL), _rows(tm, IN_WIDTH), vec, vec, vec],
        compiler_params=_params("arbitrary"),
    )(x, dx1, mod, g1, w_in, *dps)


HG_TILE = 512
HG_TILE_CHUNKS = HG_TILE // HG_CHUNK


def _lower_bound(lg_ref):
    return 1.0 / (1.0 + jnp.exp(lg_ref[1:2, :] - lg_ref[0:1, :]))


def _chunk_masks():
    r = lax.broadcasted_iota(jnp.int32, (HG_CHUNK, HG_CHUNK), 0)
    c = lax.broadcasted_iota(jnp.int32, (HG_CHUNK, HG_CHUNK), 1)
    return r >= c, c >= r, (r >= c).astype(F32), (c >= r).astype(F32)


def _hg_fwd(hq, hf, hi, hgt, logits, onorm_g):
    s = hq.shape[0]
    n_tiles = s // HG_TILE

    def body(q_ref, f_ref, i_ref, g_ref, lg_ref, og_ref, out_ref, o_ref, st_ref, state, qf_s, kk_s, lf_s):
        @pl.when(pl.program_id(0) == 0)
        def _():
            state[...] = jnp.zeros_like(state)

        lb = _lower_bound(lg_ref)
        f = lb + (1.0 - lb) * _sigmoid(f_ref[...])
        kk_s[...] = 1.0 - f
        lf_s[...] = jnp.log(f)
        qf_s[...] = _silu(q_ref[...])
        causal, _, tri, _ = _chunk_masks()

        def chunk(ci, carry):
            rows = pl.ds(pl.multiple_of(ci * HG_CHUNK, HG_CHUNK), HG_CHUNK)
            srows = pl.ds(pl.multiple_of(ci * HG_DIM, HG_DIM), HG_DIM)
            lf = lf_s[rows, :]
            b = _dot_f32(tri, lf)
            bl = _rowsum(lf)
            ref = 0.5 * bl
            qf, kk, v = qf_s[rows, :], kk_s[rows, :], i_ref[rows, :]
            a_in = (qf * jnp.exp(b)).astype(BF16)
            a_t = (qf * jnp.exp(b - ref)).astype(BF16)
            b_t = (kk * jnp.exp(ref - b)).astype(BF16)
            kd = kk * jnp.exp(bl - b)
            ebl = jnp.exp(bl)
            vb = v.astype(BF16)
            for h in range(HG_HEADS):
                c = slice(h * HG_DIM, (h + 1) * HG_DIM)
                st = state[h]
                st_ref[srows, c] = st
                p = jnp.where(causal, _dot_nt(a_t[:, c], b_t[:, c]), 0.0)
                o_ref[rows, c] = _dot(p.astype(BF16), vb[:, c]) + _dot_nt(a_in[:, c], st.astype(BF16))
                state[h] = st * ebl[:, c] + _dot_tn(vb[:, c], kd[:, c].astype(BF16))
            return carry

        lax.fori_loop(0, HG_TILE_CHUNKS, chunk, 0, unroll=True)
        for h in range(HG_HEADS):
            c = slice(h * HG_DIM, (h + 1) * HG_DIM)
            ohat, _ = _rms(o_ref[:, c])
            out_ref[:, c] = (ohat * og_ref[...] * _silu(g_ref[:, c])).astype(BF16)

    tile = _rows(HG_TILE, HG_WIDTH)
    return pl.pallas_call(
        body, name="hg_fwd", grid=(n_tiles,),
        out_shape=[jax.ShapeDtypeStruct((s, HG_WIDTH), BF16), jax.ShapeDtypeStruct((s, HG_WIDTH), F32),
                   jax.ShapeDtypeStruct((s // HG_CHUNK * HG_DIM, HG_WIDTH), F32)],
        in_specs=[tile] * 4 + [_whole((2, HG_WIDTH)), _whole((1, HG_DIM))],
        out_specs=[tile, tile, _rows(HG_TILE_CHUNKS * HG_DIM, HG_WIDTH)],
        scratch_shapes=[pltpu.VMEM((HG_HEADS, HG_DIM, HG_DIM), F32)] + [pltpu.VMEM((HG_TILE, HG_WIDTH), F32)] * 3,
        compiler_params=_params("arbitrary"),
    )(hq, hf, hi, hgt, logits, onorm_g)


def _hg_bwd(hq, hf, hi, hgt, logits, onorm_g, o, states, dout):
    s = hq.shape[0]
    n_tiles = s // HG_TILE

    def body(q_ref, f_ref, i_ref, g_ref, lg_ref, og_ref, o_ref, st_ref, d_ref,
             dq_ref, df_ref, di_ref, dg_ref, dog_ref, dlb_ref, dstate, qf_s, kk_s, lf_s, do_s):
        @pl.when(pl.program_id(0) == 0)
        def _():
            dstate[...] = jnp.zeros_like(dstate)
            dog_ref[...] = jnp.zeros_like(dog_ref)
            dlb_ref[...] = jnp.zeros_like(dlb_ref)

        og = og_ref[...]
        dog = jnp.zeros((1, HG_DIM), F32)
        for h in range(HG_HEADS):
            c = slice(h * HG_DIM, (h + 1) * HG_DIM)
            ohat, rstd = _rms(o_ref[:, c])
            gate = g_ref[:, c]
            d = d_ref[:, c]
            dg_ref[:, c] = (d * (ohat * og) * _dsilu(gate)).astype(BF16)
            dnormed = d * _silu(gate)
            dog += _rowsum(dnormed * ohat)
            do_s[:, c] = _rms_bwd(dnormed * og, ohat, rstd)
        dog_ref[...] += dog

        lb = _lower_bound(lg_ref)
        f = lb + (1.0 - lb) * _sigmoid(f_ref[...])
        kk_s[...] = 1.0 - f
        lf_s[...] = jnp.log(f)
        qf_s[...] = _silu(q_ref[...])
        causal, upper, tri, tri_t = _chunk_masks()

        def chunk(step, carry):
            ci = HG_TILE_CHUNKS - 1 - step
            rows = pl.ds(pl.multiple_of(ci * HG_CHUNK, HG_CHUNK), HG_CHUNK)
            srows = pl.ds(pl.multiple_of(ci * HG_DIM, HG_DIM), HG_DIM)
            lf = lf_s[rows, :]
            b = _dot_f32(tri, lf)
            bl = _rowsum(lf)
            ref = 0.5 * bl
            qf, kk, v, do = qf_s[rows, :], kk_s[rows, :], i_ref[rows, :], do_s[rows, :]
            eb, ebr, erb, ekd, ebl = jnp.exp(b), jnp.exp(b - ref), jnp.exp(ref - b), jnp.exp(bl - b), jnp.exp(bl)
            a_in, a_t, b_t, kd = qf * eb, qf * ebr, kk * erb, kk * ekd
            for h in range(HG_HEADS):
                c = slice(h * HG_DIM, (h + 1) * HG_DIM)
                st, dst = st_ref[srows, c], dstate[h]
                stb, dstb = st.astype(BF16), dst.astype(BF16)
                doh, vh = do[:, c], v[:, c]
                dob, vb = doh.astype(BF16), vh.astype(BF16)
                ain_h, at_h, bt_h, kd_h = a_in[:, c], a_t[:, c], b_t[:, c], kd[:, c]
                atb, btb = at_h.astype(BF16), bt_h.astype(BF16)
                d_ain = _dot(dob, stb)
                p_t = jnp.where(upper, _dot_nt(btb, atb), 0.0).astype(BF16)
                dp = jnp.where(causal, _dot_nt(dob, vb), 0.0).astype(BF16)
                dp_t = jnp.where(upper, _dot_nt(vb, dob), 0.0).astype(BF16)
                di_ref[rows, c] = (_dot(p_t, dob) + _dot_nt(kd_h.astype(BF16), dstb)).astype(BF16)
                d_at = _dot(dp, btb)
                d_bt = _dot(dp_t, atb)
                d_kd = _dot(vb, dstb)
                dqf = d_ain * eb[:, c] + d_at * ebr[:, c]
                dkk = d_bt * erb[:, c] + d_kd * ekd[:, c]
                db = d_ain * ain_h + d_at * atb.astype(F32) - d_bt * btb.astype(F32) - d_kd * kd_h
                dbl = _rowsum(d_kd * kd_h) + _rowsum(dst * st) * ebl[:, c]
                dstate[h] = _dot_tn(dob, ain_h.astype(BF16)) + dst * ebl[:, c]
                dlf = _dot_f32(tri_t, db) + dbl
                qv, fr = q_ref[rows, c], f_ref[rows, c]
                lbh = lb[:, c]
                sg = _sigmoid(fr)
                dfv = dlf / (lbh + (1.0 - lbh) * sg) - dkk
                df_ref[rows, c] = (dfv * (1.0 - lbh) * sg * (1.0 - sg)).astype(BF16)
                dlb_ref[:, c] += _rowsum(dfv * (1.0 - sg))
                dq_ref[rows, c] = (dqf * _dsilu(qv)).astype(BF16)
            return carry

        lax.fori_loop(0, HG_TILE_CHUNKS, chunk, 0, unroll=True)

    rev = pl.BlockSpec((HG_TILE, HG_WIDTH), lambda i: (n_tiles - 1 - i, 0))
    return pl.pallas_call(
        body, name="hg_bwd", grid=(n_tiles,),
        out_shape=[jax.ShapeDtypeStruct((s, HG_WIDTH), BF16)] * 4
        + [jax.ShapeDtypeStruct((1, HG_DIM), F32), jax.ShapeDtypeStruct((1, HG_WIDTH), F32)],
        in_specs=[rev] * 4 + [_whole((2, HG_WIDTH)), _whole((1, HG_DIM)), rev,
                              pl.BlockSpec((HG_TILE_CHUNKS * HG_DIM, HG_WIDTH), lambda i: (n_tiles - 1 - i, 0)), rev],
        out_specs=[rev] * 4 + [_whole((1, HG_DIM)), _whole((1, HG_WIDTH))],
        scratch_shapes=[pltpu.VMEM((HG_HEADS, HG_DIM, HG_DIM), F32)] + [pltpu.VMEM((HG_TILE, HG_WIDTH), F32)] * 4,
        compiler_params=_params("arbitrary"),
    )(hq, hf, hi, hgt, logits, onorm_g, o, states, dout)


TOKEN_GROUP = 16


def _att_geometry(dil):
    per_group = TOKEN_GROUP // dil
    return per_group, ATT_BLOCK // per_group, ATT_WIDTH if dil == 1 else 128, 1 if dil == TOKEN_GROUP else 4


def _att_consts(dil):
    per_group, ub = _att_geometry(dil)[:2]

    def pos(i):
        return i if dil == 1 else (i % ub) * per_group + i // ub

    lane = lax.broadcasted_iota(jnp.int32, (ATT_BLOCK, 128), 1)
    qi = pos(lax.broadcasted_iota(jnp.int32, (2 * ATT_BLOCK, ATT_BLOCK), 0) % ATT_BLOCK)
    kj = pos(lax.broadcasted_iota(jnp.int32, (2 * ATT_BLOCK, ATT_BLOCK), 1))
    return lane < ATT_HEAD_DIM, kj <= qi, lambda off: kj >= qi + off


def _load_tile(ref, dil, r, c, base=0):
    per_group, ub = _att_geometry(dil)[:2]
    if dil == 1:
        return ref[base:base + ATT_BLOCK, c]
    return jnp.concatenate([ref[pl.ds(base + dil * w + r, ub, stride=TOKEN_GROUP), c] for w in range(per_group)], axis=0)


def _store_tile(ref, dil, r, c, val, base=0):
    per_group, ub = _att_geometry(dil)[:2]
    if dil == 1:
        ref[base:base + ATT_BLOCK, c] = val
        return
    for w in range(per_group):
        ref[pl.ds(base + dil * w + r, ub, stride=TOKEN_GROUP), c] = val[w * ub:(w + 1) * ub]


def _stack_heads(x2, first):
    return jnp.concatenate([jnp.where(first, x2, 0.0), jnp.where(first, 0.0, x2)], axis=0)


def _stack_bcast(x2, first):
    other = pltpu.roll(x2, ATT_HEAD_DIM, axis=1)
    return jnp.concatenate([jnp.where(first, x2, other), jnp.where(first, other, x2)], axis=0)


def _unstack_heads(st, first):
    return jnp.where(first, st[:ATT_BLOCK], st[ATT_BLOCK:])


def _att_fwd(q, k, v, dil):
    seq, width = q.shape
    _, ub, lanes, nbs = _att_geometry(dil)
    rows = ub * TOKEN_GROUP
    n_steps = seq // (nbs * rows)

    def body(q_ref, k_ref, v_ref, kp_ref, vp_ref, o_ref, lse_ref):
        first, cur_ok, _band = _att_consts(dil)
        inner_ok = _band(0)
        edge_ok = _band(jnp.where(pl.program_id(0) > 0, 0, ATT_BLOCK))
        for b in range(nbs):
            base = b * rows
            prev_ok = edge_ok if b == 0 else inner_ok
            for r in range(dil):
                for j in range(lanes // 128):
                    c = slice(j * 128, (j + 1) * 128)
                    qst = _stack_heads(_load_tile(q_ref, dil, r, c, base) * ATT_SCALE, first).astype(BF16)
                    kc = _load_tile(k_ref, dil, r, c, base).astype(BF16)
                    vc = _load_tile(v_ref, dil, r, c, base).astype(BF16)
                    if b == 0:
                        kp, vp = _load_tile(kp_ref, dil, r, c).astype(BF16), _load_tile(vp_ref, dil, r, c).astype(BF16)
                    else:
                        kp = _load_tile(k_ref, dil, r, c, base - rows).astype(BF16)
                        vp = _load_tile(v_ref, dil, r, c, base - rows).astype(BF16)
                    sc = jnp.where(cur_ok, _dot_nt(qst, kc), NEG)
                    sp = jnp.where(prev_ok, _dot_nt(qst, kp), NEG)
                    mx = jnp.max(jnp.maximum(sc, sp), axis=-1, keepdims=True)
                    pc, pp = jnp.exp(sc - mx), jnp.exp(sp - mx)
                    den = jnp.sum(pc + pp, axis=-1, keepdims=True)
                    ost = (_dot(pc.astype(BF16), vc) + _dot(pp.astype(BF16), vp)) / den
                    lse = jnp.broadcast_to(mx + jnp.log(den), (2 * ATT_BLOCK, 128))
                    _store_tile(o_ref, dil, r, c, _unstack_heads(ost, first), base)
                    _store_tile(lse_ref, dil, r, c, _unstack_heads(lse, first), base)

    slab = pl.BlockSpec((nbs * rows, lanes), lambda n, j: (n, j))
    before = pl.BlockSpec((rows, lanes), lambda n, j: (jnp.maximum(n * nbs - 1, 0), j))
    return pl.pallas_call(
        body, name=f"att_fwd_d{dil}", grid=(n_steps, width // lanes),
        out_shape=[jax.ShapeDtypeStruct((seq, width), F32)] * 2,
        in_specs=[slab, slab, slab, before, before], out_specs=[slab, slab],
        compiler_params=_params("arbitrary", "arbitrary"),
    )(q, k, v, k, v)


def _att_bwd(q, k, v, do, cc, lse, dil):
    seq, width = q.shape
    _, ub, lanes, nbs = _att_geometry(dil)
    rows = ub * TOKEN_GROUP
    n_blocks = seq // rows
    n_steps = n_blocks // nbs

    def body(q_ref, k_ref, v_ref, do_ref, cc_ref, lse_ref, qx_ref, dox_ref, ccx_ref, lsex_ref,
             dq_ref, dk_ref, dv_ref, carry):
        first, cur_ok, _band = _att_consts(dil)
        step = pl.program_id(1)
        inner_ok = _band(0)
        edge_ok = _band(jnp.where(step < n_steps - 1, 0, ATT_BLOCK))

        @pl.when(step == 0)
        def _():
            carry[...] = jnp.zeros_like(carry)

        for b in range(nbs):
            base = b * rows
            last = b == nbs - 1
            next_ok = edge_ok if last else inner_ok
            for r in range(dil):
                for j in range(lanes // 128):
                    c = slice(j * 128, (j + 1) * 128)

                    def following(inner_ref, edge_ref):
                        return _load_tile(edge_ref, dil, r, c) if last else _load_tile(inner_ref, dil, r, c, base + rows)

                    qst = _stack_heads(_load_tile(q_ref, dil, r, c, base) * ATT_SCALE, first).astype(BF16)
                    qxst = _stack_heads(following(q_ref, qx_ref) * ATT_SCALE, first).astype(BF16)
                    dost = _stack_heads(_load_tile(do_ref, dil, r, c, base), first).astype(BF16)
                    doxst = _stack_heads(following(do_ref, dox_ref), first).astype(BF16)
                    lse_n = _stack_bcast(_load_tile(lse_ref, dil, r, c, base), first)
                    lse_x = _stack_bcast(following(lse_ref, lsex_ref), first)
                    cc_n = _stack_bcast(_load_tile(cc_ref, dil, r, c, base), first)
                    cc_x = _stack_bcast(following(cc_ref, ccx_ref), first)
                    kb = _load_tile(k_ref, dil, r, c, base).astype(BF16)
                    vb = _load_tile(v_ref, dil, r, c, base).astype(BF16)
                    p_cur = jnp.exp(jnp.where(cur_ok, _dot_nt(qst, kb), NEG) - lse_n)
                    p_next = jnp.exp(jnp.where(next_ok, _dot_nt(qxst, kb), NEG) - lse_x)
                    ds_cur = (p_cur * (_dot_nt(dost, vb) + cc_n)).astype(BF16)
                    ds_next = (p_next * (_dot_nt(doxst, vb) + cc_x)).astype(BF16)
                    dq_own = _load_tile(carry, dil, r, c) + _unstack_heads(_dot(ds_cur, kb), first)
                    _store_tile(dq_ref, dil, r, c, dq_own * ATT_SCALE, base)
                    _store_tile(carry, dil, r, c, _unstack_heads(_dot(ds_next, kb), first))
                    _store_tile(dk_ref, dil, r, c, _dot_tn(ds_cur, qst) + _dot_tn(ds_next, qxst), base)
                    _store_tile(dv_ref, dil, r, c, _dot_tn(p_cur.astype(BF16), dost) + _dot_tn(p_next.astype(BF16), doxst), base)

    slab = pl.BlockSpec((nbs * rows, lanes), lambda j, n: (n, j))
    after = pl.BlockSpec((rows, lanes), lambda j, n: (jnp.minimum((n + 1) * nbs, n_blocks - 1), j))
    return pl.pallas_call(
        body, name=f"att_bwd_d{dil}", grid=(width // lanes, n_steps),
        out_shape=[jax.ShapeDtypeStruct((seq, width), F32)] * 3,
        in_specs=[slab] * 6 + [after] * 4, out_specs=[slab] * 3,
        scratch_shapes=[pltpu.VMEM((rows, lanes), F32)],
        compiler_params=_params("arbitrary", "arbitrary"),
    )(q, k, v, do, cc, lse, q, do, cc, lse)


def _branch_weights(lses):
    mx = jnp.maximum(jnp.maximum(lses[0], lses[1]), lses[2])
    es = [jnp.exp(l - mx) for l in lses]
    inv = 1.0 / (es[0] + es[1] + es[2])
    return [e * inv for e in es]


def _att_combine(outs, lses, att_g):
    s = outs[0].shape[0]
    tm = 512

    def body(o0, o1, o2, l0, l1, l2, g_ref, att_ref, out_ref):
        ws = _branch_weights([l0[...], l1[...], l2[...]])
        att = ws[0] * o0[...] + ws[1] * o1[...] + ws[2] * o2[...]
        att_ref[...] = att
        ahat, _ = _rms(att)
        out_ref[...] = (ahat * g_ref[...]).astype(BF16)

    tile = _rows(tm, ATT_WIDTH)
    return pl.pallas_call(
        body, name="att_combine", grid=(s // tm,),
        out_shape=[jax.ShapeDtypeStruct((s, ATT_WIDTH), F32), jax.ShapeDtypeStruct((s, ATT_WIDTH), BF16)],
        in_specs=[tile] * 6 + [_whole((1, ATT_WIDTH))], out_specs=[tile, tile],
        compiler_params=_params("parallel"),
    )(*outs, *lses, att_g)


def _att_combine_bwd(datt_out, att, lses, att_g):
    s = att.shape[0]
    tm = 256

    def body(d_ref, att_ref, l0, l1, l2, g_ref, do0, do1, do2, cc0, cc1, cc2, dg_ref):
        @pl.when(pl.program_id(0) == 0)
        def _():
            dg_ref[...] = jnp.zeros_like(dg_ref)

        att = att_ref[...]
        ahat, rstd = _rms(att)
        d = d_ref[...]
        dg_ref[...] += _rowsum(d * ahat)
        datt = _rms_bwd(d * g_ref[...], ahat, rstd)
        hi = lax.broadcasted_iota(jnp.int32, (ATT_WIDTH, ATT_WIDTH), 0) // ATT_HEAD_DIM
        hj = lax.broadcasted_iota(jnp.int32, (ATT_WIDTH, ATT_WIDTH), 1) // ATT_HEAD_DIM
        same_head = (hi == hj).astype(BF16)
        prod = datt * att
        prod_hi = prod.astype(BF16)
        prod_lo = (prod - prod_hi.astype(F32)).astype(BF16)
        head_sum = _dot(prod_hi, same_head) + _dot(prod_lo, same_head)
        ws = _branch_weights([l0[...], l1[...], l2[...]])
        for w, do_ref, cc_ref in zip(ws, (do0, do1, do2), (cc0, cc1, cc2)):
            do_ref[...] = w * datt
            cc_ref[...] = -w * head_sum

    tile = _rows(tm, ATT_WIDTH)
    return pl.pallas_call(
        body, name="att_combine_bwd", grid=(s // tm,),
        out_shape=[jax.ShapeDtypeStruct((s, ATT_WIDTH), F32)] * 6 + [jax.ShapeDtypeStruct((1, ATT_WIDTH), F32)],
        in_specs=[tile] * 5 + [_whole((1, ATT_WIDTH))], out_specs=[tile] * 6 + [_whole((1, ATT_WIDTH))],
        compiler_params=_params("arbitrary"),
    )(datt_out, att, *lses, att_g)


def _out_fwd(x, hg, at, mod, w_out):
    s = x.shape[0]
    tm = 512

    def body(x_ref, hg_ref, at_ref, mod_ref, w_ref, x1_ref):
        mix = _dot(hg_ref[...], w_ref[0:512, :]) + _dot(at_ref[...], w_ref[512:1024, :])
        x1_ref[...] = x_ref[...] + mod_ref[:, 2 * D_MODEL:3 * D_MODEL] * mix

    return pl.pallas_call(
        body, name="out_fwd", grid=(s // tm,), out_shape=jax.ShapeDtypeStruct((s, D_MODEL), F32),
        in_specs=[_rows(tm, D_MODEL), _rows(tm, 512), _rows(tm, 512), _whole((1, 6 * D_MODEL)), _whole((D_MODEL, D_MODEL))],
        out_specs=_rows(tm, D_MODEL), compiler_params=_params("parallel"),
    )(x, hg, at, mod, w_out)


def _out_bwd(dx1, hg, at, mod, w_out):
    s = dx1.shape[0]
    tm = 512
    n_steps = s // tm

    def body(dx_ref, hg_ref, at_ref, mod_ref, w_ref, dhg_ref, dat_ref, dw_ref, dwb_ref, dgate_ref):
        @pl.when(pl.program_id(0) == 0)
        def _():
            dw_ref[...] = jnp.zeros_like(dw_ref)
            dgate_ref[...] = jnp.zeros_like(dgate_ref)

        hg, at, dx = hg_ref[...], at_ref[...], dx_ref[...]
        mix = _dot(hg, w_ref[0:512, :]) + _dot(at, w_ref[512:1024, :])
        dgate_ref[...] += _rowsum(dx * mix)
        dmix = (mod_ref[:, 2 * D_MODEL:3 * D_MODEL] * dx).astype(BF16)
        dhg_ref[...] = _dot_nt(dmix, w_ref[0:512, :])
        dat_ref[...] = _dot_nt(dmix, w_ref[512:1024, :])
        dw_ref[0:512, :] += _dot_tn(hg, dmix)
        dw_ref[512:1024, :] += _dot_tn(at, dmix)

        @pl.when(pl.program_id(0) == n_steps - 1)
        def _():
            dwb_ref[...] = dw_ref[...].astype(BF16)

    return pl.pallas_call(
        body, name="out_bwd", grid=(n_steps,),
        out_shape=[jax.ShapeDtypeStruct((s, 512), F32)] * 2
        + [jax.ShapeDtypeStruct((D_MODEL, D_MODEL), F32), jax.ShapeDtypeStruct((D_MODEL, D_MODEL), BF16),
           jax.ShapeDtypeStruct((1, D_MODEL), F32)],
        in_specs=[_rows(tm, D_MODEL), _rows(tm, 512), _rows(tm, 512), _whole((1, 6 * D_MODEL)), _whole((D_MODEL, D_MODEL))],
        out_specs=[_rows(tm, 512), _rows(tm, 512), _whole((D_MODEL, D_MODEL)), _whole((D_MODEL, D_MODEL)), _whole((1, D_MODEL))],
        compiler_params=_params("arbitrary"),
    )(dx1, hg, at, mod, w_out)


def _ffn(x1, target, mod, g2, gf, w_gu, w_down):
    s = x1.shape[0]
    tm = 256

    def body(x_ref, t_ref, mod_ref, g2_ref, gf_ref, wgu_hbm, wd_hbm,
             dx_ref, h2_ref, act_ref, dau_ref, dff_ref, sums_ref, loss_ref, wgu, wd, au_s, sem):
        @pl.when(pl.program_id(0) == 0)
        def _():
            c1 = pltpu.make_async_copy(wgu_hbm, wgu, sem.at[0])
            c2 = pltpu.make_async_copy(wd_hbm, wd, sem.at[1])
            c1.start()
            c2.start()
            c1.wait()
            c2.wait()
            sums_ref[...] = jnp.zeros_like(sums_ref)
            loss_ref[...] = jnp.zeros_like(loss_ref)

        x1v = x_ref[...]
        xhat, rstd = _rms(x1v)
        g2 = g2_ref[...]
        n2 = xhat * g2
        scale2 = 1.0 + mod_ref[:, 4 * D_MODEL:5 * D_MODEL]
        gate2 = mod_ref[:, 5 * D_MODEL:6 * D_MODEL]
        hb = (n2 * scale2 + mod_ref[:, 3 * D_MODEL:4 * D_MODEL]).astype(BF16)
        h2_ref[...] = hb
        au_s[...] = _dot_nt(hb, wgu[...])
        a = au_s[:, 0:D_FF]
        act = (_silu(a) * au_s[:, D_FF:2 * D_FF]).astype(BF16)
        act_ref[...] = act
        ff = _dot(act, wd[...])
        x2 = x1v + gate2 * ff
        nf, rstd_f = _rms(x2)
        gfv = gf_ref[...]
        err = nf * gfv - t_ref[...]
        loss_ref[...] += 0.5 * jnp.sum(_rowsum(err * err), axis=-1, keepdims=True) * (1.0 / D_MODEL)
        dy = err * (1.0 / D_MODEL)
        dx2 = _rms_bwd(dy * gfv, nf, rstd_f)
        dffb = (gate2 * dx2).astype(BF16)
        dff_ref[...] = dffb
        dact = _dot_nt(dffb, wd[...])
        a = au_s[:, 0:D_FF]
        dau_ref[:, 0:D_FF] = (dact * au_s[:, D_FF:2 * D_FF] * _dsilu(a)).astype(BF16)
        dau_ref[:, D_FF:2 * D_FF] = (dact * _silu(a)).astype(BF16)
        dh = _dot(dau_ref[...], wgu[...])
        dn = dh * scale2
        sums_ref[0:1, :] += _rowsum(dh)
        sums_ref[1:2, :] += _rowsum(dh * n2)
        sums_ref[2:3, :] += _rowsum(dx2 * ff)
        sums_ref[3:4, :] += _rowsum(dn * xhat)
        sums_ref[4:5, :] += _rowsum(dy * nf)
        dx_ref[...] = dx2 + _rms_bwd(dn * g2, xhat, rstd)

    vec = _whole((1, D_MODEL))
    hbm = pl.BlockSpec(memory_space=pl.ANY)
    return pl.pallas_call(
        body, name="ffn", grid=(s // tm,),
        out_shape=[jax.ShapeDtypeStruct((s, D_MODEL), F32), jax.ShapeDtypeStruct((s, D_MODEL), BF16),
                   jax.ShapeDtypeStruct((s, D_FF), BF16), jax.ShapeDtypeStruct((s, 2 * D_FF), BF16),
                   jax.ShapeDtypeStruct((s, D_MODEL), BF16), jax.ShapeDtypeStruct((8, D_MODEL), F32),
                   jax.ShapeDtypeStruct((1, 128), F32)],
        in_specs=[_rows(tm, D_MODEL), _rows(tm, D_MODEL), _whole((1, 6 * D_MODEL)), vec, vec, hbm, hbm],
        out_specs=[_rows(tm, D_MODEL), _rows(tm, D_MODEL), _rows(tm, D_FF), _rows(tm, 2 * D_FF), _rows(tm, D_MODEL),
                   _whole((8, D_MODEL)), _whole((1, 128))],
        scratch_shapes=[pltpu.VMEM((2 * D_FF, D_MODEL), BF16), pltpu.VMEM((D_FF, D_MODEL), BF16),
                        pltpu.VMEM((tm, 2 * D_FF), F32), pltpu.SemaphoreType.DMA((2,))],
        compiler_params=_params("arbitrary"),
    )(x1, target, mod, g2, gf, w_gu, w_down)


def _weight_grad(a, b, name, rounded=False):
    s, m = a.shape
    n = b.shape[1]
    ts = min(s, 2048)
    n_steps = s // ts
    tm = max(t for t in range(128, m + 1, 128) if m % t == 0 and t * n * 4 <= 6 * 1024 * 1024)

    def body(a_ref, b_ref, o_ref, *ob_ref):
        @pl.when(pl.program_id(1) == 0)
        def _():
            o_ref[...] = jnp.zeros_like(o_ref)

        o_ref[...] += _dot_tn(a_ref[...], b_ref[...])
        if rounded:
            @pl.when(pl.program_id(1) == n_steps - 1)
            def _():
                ob_ref[0][...] = o_ref[...].astype(BF16)

    tile = pl.BlockSpec((tm, n), lambda j, i: (j, 0))
    return pl.pallas_call(
        body, name=name, grid=(m // tm, n_steps),
        out_shape=[jax.ShapeDtypeStruct((m, n), F32)] + [jax.ShapeDtypeStruct((m, n), BF16)] * rounded,
        in_specs=[pl.BlockSpec((ts, tm), lambda j, i: (i, j)), pl.BlockSpec((ts, n), lambda j, i: (i, 0))],
        out_specs=[tile] + [tile] * rounded,
        compiler_params=_params("parallel", "arbitrary"),
    )(a, b)


def _adamw_math(w, g, m, v):
    m = ADAM_B1 * m + (1.0 - ADAM_B1) * g
    v = ADAM_B2 * v + (1.0 - ADAM_B2) * (g * g)
    m_hat = m / (1.0 - ADAM_B1 ** ADAM_STEP)
    v_hat = v / (1.0 - ADAM_B2 ** ADAM_STEP)
    delta = -ADAM_LR * (m_hat / (jnp.sqrt(v_hat) + ADAM_EPS) + ADAM_WD * w)
    return delta, m, v


def _adamw_shard(where, w, m, v, partial, got, name):
    r, c = w.shape
    tr = _shard_rows(r)
    lead = partial.ndim - 2
    n_got = got.shape[0]

    def body(where_ref, w_ref, m_ref, v_ref, own_ref, *rest):
        got_refs, (grad_ref, d_ref, nm_ref, nv_ref) = rest[:n_got], rest[n_got:]
        g = own_ref[...]
        for g_ref in got_refs:
            g = g + g_ref[...].astype(F32)
        grad_ref[...] = g
        d_ref[...], nm_ref[...], nv_ref[...] = _adamw_math(w_ref[...], g, m_ref[...], v_ref[...])

    tile = pl.BlockSpec((tr, c), lambda i, where_ref: (i, 0))
    own = pl.BlockSpec((None,) * lead + (tr, c), lambda i, where_ref: (*[where_ref[d] for d in range(lead)], i, 0))
    part = [pl.BlockSpec((None, tr, c), functools.partial(lambda j, i, where_ref: (j, i, 0), j)) for j in range(n_got)]
    return pl.pallas_call(
        body, name=name,
        grid_spec=pltpu.PrefetchScalarGridSpec(num_scalar_prefetch=1, grid=(r // tr,), in_specs=[tile] * 3 + [own] + part,
                                               out_specs=[tile] * 4),
        out_shape=[jax.ShapeDtypeStruct((r, c), F32)] * 4, compiler_params=_params("parallel"),
    )(where, w, m, v, partial, *[got] * n_got)


def _small_update(small_all, dmod_blocks, c_all, logits, w_ada, m_ada, v_ada, smalls):
    def body(sm_ref, dm_ref, c_ref, lg_ref, wa_ref, ma_ref, va_ref, *rest):
        ins, outs = rest[:21], rest[21:]
        _, me = _flip(0)
        tot = sm_ref[0:1, :]
        for i in range(1, N_DEV):
            tot = tot + sm_ref[i:i + 1, :]
        loss_ref = outs[0]
        loss_ref[...] = tot[:, SM_LOSS:SM_LOSS + 128]
        g_ada = lax.dot_general(_silu(c_ref[...]), dm_ref[me], (((0,), (0,)), ((), ())),
                                preferred_element_type=F32, precision=HIGHEST)
        outs[1][...] = g_ada
        outs[2][...], outs[3][...], outs[4][...] = _adamw_math(wa_ref[...], g_ada, ma_ref[...], va_ref[...])
        p0 = _lower_bound(lg_ref)
        dl0 = tot[:, SM_LB:SM_LB + 512] * p0 * (1.0 - p0)
        grads = [tot[:, SM_MOD:SM_MOD + 6 * D_MODEL], tot[:, SM_G1:SM_G1 + D_MODEL], tot[:, SM_G2:SM_G2 + D_MODEL],
                 tot[:, SM_GF:SM_GF + D_MODEL], tot[:, SM_ATT:SM_ATT + 512], tot[:, SM_HG:SM_HG + 128],
                 jnp.where(lax.broadcasted_iota(jnp.int32, (2, 512), 0) == 0, dl0, -dl0)]
        for i, g in enumerate(grads):
            w_ref, m_ref, v_ref = ins[3 * i:3 * i + 3]
            o = outs[5 + 4 * i:9 + 4 * i]
            o[0][...] = g
            o[1][...], o[2][...], o[3][...] = _adamw_math(w_ref[...], g, m_ref[...], v_ref[...])

    flat = [t for trio in smalls for t in trio]
    vm = pl.BlockSpec(memory_space=pltpu.VMEM)
    out_shape = [jax.ShapeDtypeStruct((1, 128), F32)] + [jax.ShapeDtypeStruct(w_ada.shape, F32)] * 4
    for trio in smalls:
        out_shape += [jax.ShapeDtypeStruct(trio[0].shape, F32)] * 4
    return pl.pallas_call(
        body, name="small_update", out_shape=out_shape,
        in_specs=[vm] * (7 + len(flat)), out_specs=[vm] * len(out_shape),
        compiler_params=pltpu.CompilerParams(vmem_limit_bytes=V7X_VMEM_LIMIT),
    )(small_all, dmod_blocks, c_all, logits, w_ada, m_ada, v_ada, *flat)


def kernel(x, c, w_ada, b_ada, norm1_g, w_in, hg_lb_logits, hg_onorm_g, att_onorm_g, w_out, norm2_g, w_gate_up, w_down, final_g, loss_target, m_w_ada, m_b_ada, m_norm1_g, m_w_in, m_hg_lb_logits, m_hg_onorm_g, m_att_onorm_g, m_w_out, m_norm2_g, m_w_gate_up, m_w_down, m_final_g, v_w_ada, v_b_ada, v_norm1_g, v_w_in, v_hg_lb_logits, v_hg_onorm_g, v_att_onorm_g, v_w_out, v_norm2_g, v_w_gate_up, v_w_down, v_final_g):
    x2d, target = x[0], loss_target[0]
    seq = x2d.shape[0]
    assert seq % (ATT_BLOCK * max(DILATIONS)) == 0 and seq % HG_TILE == 0
    gf = final_g.reshape(1, D_MODEL)

    c_all = _exchange_small(c.reshape(8, D_MODEL // 8), None, "gather_c").reshape(N_DEV, D_MODEL)
    ada = _ada_rows(c_all, w_ada[0], b_ada)
    mod = _exchange_small(ada, 1, "scatter_mod").reshape(1, 6 * D_MODEL)

    core = lax.axis_index("c").astype(jnp.int32).reshape(1)
    chip = (2 * lax.axis_index("x") + lax.axis_index("y")).astype(jnp.int32).reshape(1)
    me = 4 * lax.axis_index("x") + 2 * lax.axis_index("y") + lax.axis_index("c")

    g_in, = _gather_weights([w_in[0].T.astype(BF16)])
    w_in_b = g_in.reshape(IN_WIDTH, D_MODEL)
    rest_shards = [w_out[0].astype(BF16), w_gate_up[0].T.astype(BF16), w_down[0].astype(BF16)]
    lands = [lax.empty((N_DEV,) + s.shape, BF16) for s in rest_shards]
    g_send, g_recv, g_srcs, g_lands, tok = _copies_start("gather_rest_start", _plan_gather_own, 12, rest_shards, lands, [w_in_b, mod])
    flight = {}

    def stage(name, *vals):
        if name == "attention_begun":
            flight["shards"], got = _copies_wait("gather_rest_wait", _plan_gather_own, g_send, g_recv, g_srcs, g_lands, list(vals))
            flight["pass"] = _copies_start("gather_pass_start", _plan_gather_pass, 9, [], got, [])
            return flight["pass"][4][0:1, 0:1]
        if name == "mixer_weights_done":
            shapes = [(4, 2, D_MODEL // N_DEV, D_MODEL), (4, 2, 2 * D_FF // N_DEV, D_MODEL), (4, 2, D_FF // N_DEV, D_MODEL)]
            flight["grads"] = [g32.reshape(sh) for (g32, _), sh in zip(vals, shapes)]
            rounded = [g16.reshape(sh) for (_, g16), sh in zip(vals, shapes)]
            direct_lands = [lax.empty((N_DEV - 1,) + sh[2:], BF16) for sh in shapes]
            flight["direct"] = _copies_start("reduce_rest_start", _plan_reduce_direct, 21, rounded, direct_lands, [])
            return flight["direct"][4][0:1, 0:1]
        raise ValueError(name)

    def rest_weights(after):
        s, r, _, p_lands, _ = flight["pass"]
        _, got = _copies_wait("gather_pass_wait", _plan_gather_pass, s, r, [], p_lands, [after])
        full = [lax.dynamic_update_index_in_dim(g, shard, me, 0) for g, shard in zip(got, flight["shards"])]
        return full[0].reshape(D_MODEL, D_MODEL), full[1].reshape(2 * D_FF, D_MODEL), full[2].reshape(D_FF, D_MODEL)

    grad_x, dw_in, small = _block_step(x2d, target, mod + tok[0:1, 0:1], norm1_g, hg_lb_logits, hg_onorm_g, att_onorm_g, norm2_g, gf,
                                       w_in_b, rest_weights, stage)

    g_in8 = dw_in.reshape(4, 2, IN_WIDTH // N_DEV, D_MODEL)
    in_pairs = _copies_start("reduce_pairs_in_start", _plan_reduce_pairs, 4, [g_in8], [lax.empty((4,) + g_in8.shape[2:], F32)], [])
    s, r, srcs, d_lands, _ = flight["direct"]
    _, recv_rest = _copies_wait("reduce_rest_wait", _plan_reduce_direct, s, r, srcs, d_lands, [in_pairs[4]])
    small_rows = jnp.pad(small + in_pairs[4][0:1, 0:1], ((0, 0), (0, SM_PADDED - SM_WIDTH))).reshape(SM_PADDED // 128, 128)
    small_all = _exchange_small(small_rows, None, "gather_small").reshape(N_DEV, SM_PADDED)[:, :SM_WIDTH]
    in_grads, got_in = _copies_wait("reduce_pairs_in_wait", _plan_reduce_pairs, in_pairs[0], in_pairs[1], in_pairs[2], in_pairs[3],
                                    [small_all])
    in_s32, in_s16 = _pair_sum(core, in_grads[0], got_in[0], "pair_sum_in")
    in_chips = _copies_start("reduce_chips_in_start", _plan_reduce_chips, 3, [in_s16], [lax.empty((3,) + in_s16.shape[1:], BF16)], [])
    big, updated = {}, []
    rest_params = [("w_out", w_out, m_w_out, v_w_out), ("w_gate_up", w_gate_up, m_w_gate_up, v_w_gate_up), ("w_down", w_down, m_w_down, v_w_down)]
    mine = jnp.concatenate([chip, core])
    for (n, w, m, v), g32, got in zip(rest_params, flight["grads"], recv_rest):
        if n == "w_gate_up":
            outs4 = _adamw_shard(mine, w[0].T, m[0].T, v[0].T, g32, got, f"adamw_{n}")
            big[n] = [t.T[None] for t in outs4]
        else:
            outs4 = _adamw_shard(mine, w[0], m[0], v[0], g32, got, f"adamw_{n}")
            big[n] = [t[None] for t in outs4]
        updated.append(outs4[3])
    c_all = c_all + in_chips[4][0:1, 0:1]
    smalls = [(b_ada, m_b_ada, v_b_ada), (norm1_g, m_norm1_g, v_norm1_g), (norm2_g, m_norm2_g, v_norm2_g),
              (gf, m_final_g.reshape(1, D_MODEL), v_final_g.reshape(1, D_MODEL)),
              (att_onorm_g, m_att_onorm_g, v_att_onorm_g), (hg_onorm_g, m_hg_onorm_g, v_hg_onorm_g),
              (hg_lb_logits, m_hg_lb_logits, v_hg_lb_logits)]
    dmod_blocks = small_all[:, :6 * D_MODEL].reshape(N_DEV, N_DEV, 6 * D_MODEL // N_DEV).transpose(1, 0, 2)
    res = _small_update(small_all, dmod_blocks, c_all, hg_lb_logits, w_ada[0], m_w_ada[0], v_w_ada[0], smalls)
    _, recv_in = _copies_wait("reduce_chips_in_wait", _plan_reduce_chips, in_chips[0], in_chips[1], in_chips[2], in_chips[3],
                              [res[0]] + updated)
    big["w_in"] = [t.T[None] for t in _adamw_shard(chip, w_in[0].T, m_w_in[0].T, v_w_in[0].T, in_s32, recv_in[0], "adamw_w_in")]
    loss = res[0][0, 0]
    ada4 = [t[None] for t in res[1:5]]
    sm4 = {n: list(res[5 + 4 * i:9 + 4 * i]) for i, n in enumerate(["b_ada", "norm1_g", "norm2_g", "final_g", "att", "hg", "lb"])}
    sm4["final_g"] = [t.reshape(D_MODEL) for t in sm4["final_g"]]

    order = [ada4, sm4["b_ada"], sm4["norm1_g"], big["w_in"], sm4["lb"], sm4["hg"], sm4["att"], big["w_out"], sm4["norm2_g"],
             big["w_gate_up"], big["w_down"], sm4["final_g"]]
    return (loss, grad_x[None], *[o[0] for o in order], *[o[1] for o in order], *[o[2] for o in order], *[o[3] for o in order])


def _block_step(x2d, target, mod, norm1_g, hg_lb_logits, hg_onorm_g, att_onorm_g, norm2_g, gf, w_in_b, rest_weights, stage):
    h1, hq, hf, hi, hgt, aq, ak, av = _in_fwd(x2d, mod, norm1_g, w_in_b)
    hg_out, hg_o, hg_states = _hg_fwd(hq, hf, hi, hgt, hg_lb_logits, hg_onorm_g)
    branch = [_att_fwd(aq, ak, av, d) for d in DILATIONS[:2]]
    att_g = att_onorm_g + stage("attention_begun", branch[0][0], branch[1][0])
    branch += [_att_fwd(aq, ak, av, d) for d in DILATIONS[2:]]
    outs = [b[0] for b in branch]
    lses = [b[1] for b in branch]
    att, att_out = _att_combine(outs, lses, att_g)
    w_out_b, w_gu_b, w_down_b = rest_weights(att_out)
    x1 = _out_fwd(x2d, hg_out, att_out, mod, w_out_b)

    dx1, h2, act, dau, dff, ffn_sums, loss_part = _ffn(x1, target, mod, norm2_g, gf, w_gu_b, w_down_b)
    dw_gu = _weight_grad(dau, h2, "dw_gate_up", rounded=True)
    dw_down = _weight_grad(act, dff, "dw_down", rounded=True)

    dhg, dat, dw_out, dw_out_b, dgate1 = _out_bwd(dx1, hg_out, att_out, mod, w_out_b)
    att_g = att_onorm_g + stage("mixer_weights_done", (dw_out, dw_out_b), dw_gu, dw_down)
    comb = _att_combine_bwd(dat, att, lses, att_g)
    dos, ccs, d_att_g = comb[0:3], comb[3:6], comb[6]
    datt = []
    for i, d in enumerate(DILATIONS):
        datt.append(_att_bwd(aq, ak, av, dos[i], ccs[i], lses[i], d))
    dhq, dhf, dhi, dhgt, d_hg_g, d_lb = _hg_bwd(hq, hf, hi, hgt, hg_lb_logits, hg_onorm_g, hg_o, hg_states, dhg)
    dps = [dhq, dhf, dhi, dhgt] + [datt[i][j] for j in range(3) for i in range(3)]
    grad_x, dp_b, dshift1, dscale1, d_g1 = _in_bwd(x2d, dx1, mod, norm1_g, w_in_b, dps)
    dw_in, = _weight_grad(dp_b, h1, "dw_in")
    small = jnp.concatenate([dshift1, dscale1, dgate1, ffn_sums[0:1], ffn_sums[1:2], ffn_sums[2:3], d_g1, ffn_sums[3:4],
                             ffn_sums[4:5], d_att_g, d_lb, d_hg_g, loss_part], axis=1)
    return grad_x, dw_in, small
```

```python
import functools

import jax
import jax.numpy as jnp
from jax import lax
from jax.experimental import pallas as pl
from jax.experimental.pallas import tpu as pltpu

F32 = jnp.float32
BF16 = jnp.bfloat16
HIGHEST = lax.Precision.HIGHEST
MESH = pl.DeviceIdType.MESH

D_MODEL = 1024
N_DEV = 8
HG_HEADS = 4
HG_DIM = 128
HG_WIDTH = HG_HEADS * HG_DIM
HG_CHUNK = 128
ATT_WIDTH = 512
ATT_HEAD_DIM = 64
ATT_BLOCK = 128
DILATIONS = (1, 4, 16)
ATT_SCALE = ATT_HEAD_DIM ** -0.5
D_FF = 2816
IN_WIDTH = 7 * 512
RMS_EPS = 1e-6
NEG = -1e30

ADAM_LR = 0.001
ADAM_B1 = 0.9
ADAM_B2 = 0.999
ADAM_EPS = 1e-08
ADAM_WD = 0.01
ADAM_STEP = 10

V7X_VMEM_LIMIT = 56 * 1024 * 1024

SM_MOD = 0
SM_G1 = 6 * D_MODEL
SM_G2 = 7 * D_MODEL
SM_GF = 8 * D_MODEL
SM_ATT = 9 * D_MODEL
SM_LB = 9 * D_MODEL + 512
SM_HG = 10 * D_MODEL
SM_LOSS = 10 * D_MODEL + 128
SM_WIDTH = 10 * D_MODEL + 256
SM_PADDED = 88 * 128


def _params(*sem, vmem=V7X_VMEM_LIMIT):
    return pltpu.CompilerParams(dimension_semantics=sem, vmem_limit_bytes=vmem)


def _dot(a, b):
    return jnp.dot(a, b, preferred_element_type=F32)


def _dot_nt(a, b):
    return lax.dot_general(a, b, (((1,), (1,)), ((), ())), preferred_element_type=F32)


def _dot_tn(a, b):
    return lax.dot_general(a, b, (((0,), (0,)), ((), ())), preferred_element_type=F32)


def _dot_f32(a, b):
    return jnp.dot(a, b, preferred_element_type=F32, precision=HIGHEST)


def _sigmoid(x):
    return 1.0 / (1.0 + jnp.exp(-x))


def _silu(x):
    return x * _sigmoid(x)


def _dsilu(x):
    s = _sigmoid(x)
    return s * (1.0 + x * (1.0 - s))


def _rms(x):
    rstd = lax.rsqrt(jnp.mean(x * x, axis=-1, keepdims=True) + RMS_EPS)
    return x * rstd, rstd


def _rms_bwd(dn, xhat, rstd):
    return rstd * (dn - xhat * jnp.mean(dn * xhat, axis=-1, keepdims=True))


def _rowsum(x):
    return jnp.sum(x, axis=0, keepdims=True)


def _rows(tm, n):
    return pl.BlockSpec((tm, n), lambda i: (i, 0))


def _whole(shape):
    return pl.BlockSpec(shape, lambda i: (0,) * len(shape))


def _mesh_pos():
    return lax.axis_index("x"), lax.axis_index("y"), lax.axis_index("c")


def _flip(k):
    x, y, c = _mesh_pos()
    px = 1 - x if k & 4 else x
    py = 1 - y if k & 2 else y
    pc = 1 - c if k & 1 else c
    return (px, py, pc), 4 * px + 2 * py + pc


def _exchange_small(x, rows_per_peer, name):
    r_all, cols = x.shape
    r_out = r_all if rows_per_peer is None else rows_per_peer

    def body(x_ref, out_ref, send_sems, recv_sems):
        _, me = _flip(0)

        def src(pid):
            if rows_per_peer is None:
                return x_ref
            return x_ref.at[pl.ds(pl.multiple_of(pid * r_out, r_out), r_out), :]

        if rows_per_peer is None:
            out_ref[me] = x_ref[...]
        else:
            out_ref[me] = x_ref[pl.ds(pl.multiple_of(me * r_out, r_out), r_out), :]
        sends = []
        for k in range(1, N_DEV):
            dev, pid = _flip(k)
            cp = pltpu.make_async_remote_copy(src_ref=src(pid), dst_ref=out_ref.at[me], send_sem=send_sems.at[k - 1],
                                              recv_sem=recv_sems.at[k - 1], device_id=dev, device_id_type=MESH)
            cp.start()
            sends.append(cp)
        for k in range(1, N_DEV):
            dev, pid = _flip(k)
            pltpu.make_async_remote_copy(src_ref=src(pid), dst_ref=out_ref.at[pid], send_sem=send_sems.at[k - 1],
                                         recv_sem=recv_sems.at[k - 1], device_id=dev, device_id_type=MESH).wait_recv()
        for cp in sends:
            cp.wait_send()

    return pl.pallas_call(
        body, name=name,
        out_shape=jax.ShapeDtypeStruct((N_DEV, r_out, cols), x.dtype),
        in_specs=[pl.BlockSpec(memory_space=pltpu.VMEM)],
        out_specs=pl.BlockSpec(memory_space=pltpu.VMEM),
        scratch_shapes=[pltpu.SemaphoreType.DMA((N_DEV - 1,)), pltpu.SemaphoreType.DMA((N_DEV - 1,))],
    )(x)


def _gather_weights(shards):
    n = len(shards)

    def body(*refs):
        xs, outs = refs[:n], refs[n:2 * n]
        send_sems, recv_sems, local_sems = refs[2 * n:]
        x, y, c = _mesh_pos()
        me, sibling = (x, y, c), (x, y, 1 - c)
        chips = [(1 - x, y), (x, 1 - y), (1 - x, 1 - y)]

        def blk(a, px, py, pc):
            return outs[a].at[4 * px + 2 * py + pc]

        def copy(a, k, block, to, src=None):
            return pltpu.make_async_remote_copy(
                src_ref=blk(a, *block) if src is None else src, dst_ref=blk(a, *block),
                send_sem=send_sems.at[a * 7 + k], recv_sem=recv_sems.at[a * 7 + k], device_id=to, device_id_type=MESH)

        mine = [pltpu.make_async_copy(xs[a], blk(a, *me), local_sems.at[a]) for a in range(n)]
        for cp in mine:
            cp.start()
        first = []
        for a in range(n):
            first.append(copy(a, 0, me, sibling, src=xs[a]))
            first += [copy(a, 1 + j, me, (*chip, c), src=xs[a]) for j, chip in enumerate(chips)]
        for cp in first:
            cp.start()
        passed = []
        for j, chip in enumerate(chips):
            for a in range(n):
                copy(a, 1 + j, (*chip, c), me).wait_recv()
                cp = copy(a, 4 + j, (*chip, c), sibling)
                cp.start()
                passed.append(cp)
        for a in range(n):
            copy(a, 0, sibling, me).wait_recv()
            for j, chip in enumerate(chips):
                copy(a, 4 + j, (*chip, 1 - c), me).wait_recv()
        for cp in first + passed:
            cp.wait_send()
        for cp in mine:
            cp.wait()

    hbm = pl.BlockSpec(memory_space=pl.ANY)
    return pl.pallas_call(
        body, name="gather_weights",
        out_shape=[jax.ShapeDtypeStruct((N_DEV,) + s.shape, s.dtype) for s in shards],
        in_specs=[hbm] * n, out_specs=[hbm] * n,
        scratch_shapes=[pltpu.SemaphoreType.DMA((7 * n,)), pltpu.SemaphoreType.DMA((7 * n,)), pltpu.SemaphoreType.DMA((n,))],
    )(*shards)


_HBM = pl.BlockSpec(memory_space=pltpu.HBM)
_SEM = pl.BlockSpec(memory_space=pltpu.SEMAPHORE)
_DATAFLOW = pltpu.SideEffectType.DATAFLOW_SIDE_EFFECTING


def _copies_start(name, plan, n_copies, srcs, lands, after):
    bufs = list(srcs) + list(lands)
    nb = len(bufs)

    def body(*refs):
        ins, send_sems, recv_sems, token = refs[:nb], refs[nb + len(after)], refs[nb + len(after) + 1], refs[-1]
        for i, (src, dst, dev) in enumerate(plan(ins[:len(srcs)], ins[len(srcs):])):
            pltpu.make_async_remote_copy(src_ref=src, dst_ref=dst, send_sem=send_sems.at[i], recv_sem=recv_sems.at[i],
                                         device_id=dev, device_id_type=MESH).start()
        token[...] = jnp.zeros_like(token)

    outs = pl.pallas_call(
        body, name=name,
        out_shape=(pltpu.SemaphoreType.DMA((n_copies,)), pltpu.SemaphoreType.DMA((n_copies,)),
                   *[pltpu.HBM(b.shape, b.dtype) for b in bufs], jax.ShapeDtypeStruct((8, 128), F32)),
        in_specs=[_HBM] * nb + [pl.BlockSpec(memory_space=pl.ANY)] * len(after),
        out_specs=(_SEM, _SEM, *[_HBM] * nb, pl.BlockSpec(memory_space=pltpu.VMEM)),
        input_output_aliases={i: 2 + i for i in range(nb)},
        compiler_params=pltpu.CompilerParams(has_side_effects=_DATAFLOW),
    )(*[pltpu.with_memory_space_constraint(b, pltpu.HBM) for b in bufs], *after)
    return outs[0], outs[1], list(outs[2:2 + len(srcs)]), list(outs[2 + len(srcs):2 + nb]), outs[-1]


def _copies_wait(name, plan, send_sems, recv_sems, srcs, lands, after):
    bufs = list(srcs) + list(lands)
    nb = len(bufs)

    def body(*refs):
        ins, send_ref, recv_ref = refs[:nb], refs[nb], refs[nb + 1]
        for i, (src, dst, dev) in enumerate(plan(ins[:len(srcs)], ins[len(srcs):])):
            cp = pltpu.make_async_remote_copy(src_ref=src, dst_ref=dst, send_sem=send_ref.at[i], recv_sem=recv_ref.at[i],
                                              device_id=dev, device_id_type=MESH)
            cp.wait_send()
            cp.wait_recv()

    outs = pl.pallas_call(
        body, name=name, out_shape=[pltpu.HBM(b.shape, b.dtype) for b in bufs],
        in_specs=[_HBM] * nb + [_SEM, _SEM] + [pl.BlockSpec(memory_space=pl.ANY)] * len(after), out_specs=[_HBM] * nb,
        input_output_aliases={i: i for i in range(nb)},
        compiler_params=pltpu.CompilerParams(has_side_effects=_DATAFLOW),
    )(*bufs, send_sems, recv_sems, *after)
    return list(outs[:len(srcs)]), list(outs[len(srcs):])


def _plan_gather_own(srcs, lands):
    _, me = _flip(0)
    return [(srcs[a], lands[a].at[me], _flip(k)[0]) for a in range(len(srcs)) for k in (1, 4, 2, 6)]


def _plan_gather_pass(srcs, lands):
    sibling = _flip(1)[0]
    plan = []
    for land in lands:
        for k in (4, 2, 6):
            block = land.at[_flip(k)[1]]
            plan.append((block, block, sibling))
    return plan


def _plan_reduce_pairs(srcs, lands):
    x, y, c = _mesh_pos()
    return [(srcs[a].at[chip, 1 - c], lands[a].at[chip], (x, y, 1 - c)) for a in range(len(srcs)) for chip in range(4)]


def _plan_reduce_chips(srcs, lands):
    plan = []
    for a in range(len(srcs)):
        for j, k in enumerate((4, 2, 6)):
            dev = _flip(k)[0]
            plan.append((srcs[a].at[2 * dev[0] + dev[1]], lands[a].at[j], dev))
    return plan


def _plan_reduce_direct(srcs, lands):
    plan = []
    for a in range(len(srcs)):
        for k in range(1, N_DEV):
            dev = _flip(k)[0]
            plan.append((srcs[a].at[2 * dev[0] + dev[1], dev[2]], lands[a].at[k - 1], dev))
    return plan


def _shard_rows(r):
    return r // 2 if r % 32 == 0 else r


def _pair_sum(core, grads, got, name):
    _, _, r, c = grads.shape
    tr = _shard_rows(r)

    def body(core_ref, a_ref, b_ref, o_ref, ob_ref):
        s = a_ref[...] + b_ref[...]
        o_ref[...] = s
        ob_ref[...] = s.astype(BF16)

    spec = pl.BlockSpec((None, tr, c), lambda i, j, core_ref: (i, j, 0))
    return pl.pallas_call(
        body, name=name,
        grid_spec=pltpu.PrefetchScalarGridSpec(
            num_scalar_prefetch=1, grid=(4, r // tr),
            in_specs=[pl.BlockSpec((None, None, tr, c), lambda i, j, core_ref: (i, core_ref[0], j, 0)), spec],
            out_specs=[spec, spec]),
        out_shape=[jax.ShapeDtypeStruct((4, r, c), F32), jax.ShapeDtypeStruct((4, r, c), BF16)],
        compiler_params=_params("parallel", "parallel"),
    )(core, grads, got)


def _ada_rows(c_all, w_ada, b_ada):
    n_cols = w_ada.shape[1]

    def body(c_ref, w_ref, b_ref, o_ref):
        _, me = _flip(0)
        bias = b_ref[:, pl.ds(pl.multiple_of(me * n_cols, 128), n_cols)]
        o_ref[...] = _dot_f32(_silu(c_ref[...]), w_ref[...]) + bias

    return pl.pallas_call(
        body, name="ada_rows", out_shape=jax.ShapeDtypeStruct((N_DEV, n_cols), F32),
        in_specs=[pl.BlockSpec(memory_space=pltpu.VMEM)] * 3, out_specs=pl.BlockSpec(memory_space=pltpu.VMEM),
    )(c_all, w_ada, b_ada)


def _in_fwd(x, mod, g1, w_in):
    s = x.shape[0]
    tm = 256

    def body(x_ref, mod_ref, g_ref, w_ref, h_ref, *outs):
        xhat, _ = _rms(x_ref[...])
        h = (xhat * g_ref[...]) * (1.0 + mod_ref[:, D_MODEL:2 * D_MODEL]) + mod_ref[:, 0:D_MODEL]
        hb = h.astype(BF16)
        h_ref[...] = hb
        for j, o_ref in enumerate(outs):
            o_ref[...] = _dot_nt(hb, w_ref[j * 512:(j + 1) * 512, :])

    return pl.pallas_call(
        body, name="in_fwd", grid=(s // tm,),
        out_shape=[jax.ShapeDtypeStruct((s, D_MODEL), BF16)] + [jax.ShapeDtypeStruct((s, 512), F32)] * 7,
        in_specs=[_rows(tm, D_MODEL), _whole((1, 6 * D_MODEL)), _whole((1, D_MODEL)), _whole((IN_WIDTH, D_MODEL))],
        out_specs=[_rows(tm, D_MODEL)] + [_rows(tm, 512)] * 7,
        compiler_params=_params("parallel"),
    )(x, mod, g1, w_in)


def _in_bwd(x, dx1, mod, g1, w_in, dps):
    s = x.shape[0]
    tm = 256

    def body(x_ref, dx_ref, mod_ref, g_ref, w_ref, *rest):
        dp_refs, (gx_ref, dpb_ref, dsh_ref, dsc_ref, dg_ref) = rest[:13], rest[13:]
        pieces = [dp_refs[j][...] for j in range(4)]
        pieces += [dp_refs[4 + 3 * j][...] + dp_refs[5 + 3 * j][...] + dp_refs[6 + 3 * j][...] for j in range(3)]
        for j, p in enumerate(pieces):
            dpb_ref[:, j * 512:(j + 1) * 512] = p.astype(BF16)
        dh = _dot(dpb_ref[...], w_ref[...])
        xhat, rstd = _rms(x_ref[...])
        g = g_ref[...]
        scale1 = 1.0 + mod_ref[:, D_MODEL:2 * D_MODEL]
        n1 = xhat * g

        @pl.when(pl.program_id(0) == 0)
        def _():
            dsh_ref[...] = jnp.zeros_like(dsh_ref)
            dsc_ref[...] = jnp.zeros_like(dsc_ref)
            dg_ref[...] = jnp.zeros_like(dg_ref)

        dsh_ref[...] += _rowsum(dh)
        dsc_ref[...] += _rowsum(dh * n1)
        dn = dh * scale1
        dg_ref[...] += _rowsum(dn * xhat)
        gx_ref[...] = dx_ref[...] + _rms_bwd(dn * g, xhat, rstd)

    vec = _whole((1, D_MODEL))
    return pl.pallas_call(
        body, name="in_bwd", grid=(s // tm,),
        out_shape=[jax.ShapeDtypeStruct((s, D_MODEL), F32), jax.ShapeDtypeStruct((s, IN_WIDTH), BF16)]
        + [jax.ShapeDtypeStruct((1, D_MODEL), F32)] * 3,
        in_specs=[_rows(tm, D_MODEL), _rows(tm, D_MODEL), _whole((1, 6 * D_MODEL)), vec, _whole((IN_WIDTH, D_MODEL))]
        + [_rows(tm, 512)] * 13,
        out_specs=[_rows(tm, D_MODEL), _rows(tm, IN_WIDTH), vec, vec, vec],
        compiler_params=_params("arbitrary"),
    )(x, dx1, mod, g1, w_in, *dps)


HG_TILE = 512
HG_TILE_CHUNKS = HG_TILE // HG_CHUNK


def _lower_bound(lg_ref):
    return 1.0 / (1.0 + jnp.exp(lg_ref[1:2, :] - lg_ref[0:1, :]))


def _chunk_masks():
    r = lax.broadcasted_iota(jnp.int32, (HG_CHUNK, HG_CHUNK), 0)
    c = lax.broadcasted_iota(jnp.int32, (HG_CHUNK, HG_CHUNK), 1)
    return r >= c, c >= r, (r >= c).astype(F32), (c >= r).astype(F32)


def _hg_fwd(hq, hf, hi, hgt, logits, onorm_g):
    s = hq.shape[0]
    n_tiles = s // HG_TILE

    def body(q_ref, f_ref, i_ref, g_ref, lg_ref, og_ref, out_ref, o_ref, st_ref, state, qf_s, kk_s, lf_s):
        @pl.when(pl.program_id(0) == 0)
        def _():
            state[...] = jnp.zeros_like(state)

        lb = _lower_bound(lg_ref)
        f = lb + (1.0 - lb) * _sigmoid(f_ref[...])
        kk_s[...] = 1.0 - f
        lf_s[...] = jnp.log(f)
        qf_s[...] = _silu(q_ref[...])
        causal, _, tri, _ = _chunk_masks()

        def chunk(ci, carry):
            rows = pl.ds(pl.multiple_of(ci * HG_CHUNK, HG_CHUNK), HG_CHUNK)
            srows = pl.ds(pl.multiple_of(ci * HG_DIM, HG_DIM), HG_DIM)
            lf = lf_s[rows, :]
            b = _dot_f32(tri, lf)
            bl = _rowsum(lf)
            ref = 0.5 * bl
            qf, kk, v = qf_s[rows, :], kk_s[rows, :], i_ref[rows, :]
            a_in = (qf * jnp.exp(b)).astype(BF16)
            a_t = (qf * jnp.exp(b - ref)).astype(BF16)
            b_t = (kk * jnp.exp(ref - b)).astype(BF16)
            kd = kk * jnp.exp(bl - b)
            ebl = jnp.exp(bl)
            vb = v.astype(BF16)
            for h in range(HG_HEADS):
                c = slice(h * HG_DIM, (h + 1) * HG_DIM)
                st = state[h]
                st_ref[srows, c] = st
                p = jnp.where(causal, _dot_nt(a_t[:, c], b_t[:, c]), 0.0)
                o_ref[rows, c] = _dot(p.astype(BF16), vb[:, c]) + _dot_nt(a_in[:, c], st.astype(BF16))
                state[h] = st * ebl[:, c] + _dot_tn(vb[:, c], kd[:, c].astype(BF16))
            return carry

        lax.fori_loop(0, HG_TILE_CHUNKS, chunk, 0, unroll=True)
        for h in range(HG_HEADS):
            c = slice(h * HG_DIM, (h + 1) * HG_DIM)
            ohat, _ = _rms(o_ref[:, c])
            out_ref[:, c] = (ohat * og_ref[...] * _silu(g_ref[:, c])).astype(BF16)

    tile = _rows(HG_TILE, HG_WIDTH)
    return pl.pallas_call(
        body, name="hg_fwd", grid=(n_tiles,),
        out_shape=[jax.ShapeDtypeStruct((s, HG_WIDTH), BF16), jax.ShapeDtypeStruct((s, HG_WIDTH), F32),
                   jax.ShapeDtypeStruct((s // HG_CHUNK * HG_DIM, HG_WIDTH), F32)],
        in_specs=[tile] * 4 + [_whole((2, HG_WIDTH)), _whole((1, HG_DIM))],
        out_specs=[tile, tile, _rows(HG_TILE_CHUNKS * HG_DIM, HG_WIDTH)],
        scratch_shapes=[pltpu.VMEM((HG_HEADS, HG_DIM, HG_DIM), F32)] + [pltpu.VMEM((HG_TILE, HG_WIDTH), F32)] * 3,
        compiler_params=_params("arbitrary"),
    )(hq, hf, hi, hgt, logits, onorm_g)


def _hg_bwd(hq, hf, hi, hgt, logits, onorm_g, o, states, dout):
    s = hq.shape[0]
    n_tiles = s // HG_TILE

    def body(q_ref, f_ref, i_ref, g_ref, lg_ref, og_ref, o_ref, st_ref, d_ref,
             dq_ref, df_ref, di_ref, dg_ref, dog_ref, dlb_ref, dstate, qf_s, kk_s, lf_s, do_s):
        @pl.when(pl.program_id(0) == 0)
        def _():
            dstate[...] = jnp.zeros_like(dstate)
            dog_ref[...] = jnp.zeros_like(dog_ref)
            dlb_ref[...] = jnp.zeros_like(dlb_ref)

        og = og_ref[...]
        dog = jnp.zeros((1, HG_DIM), F32)
        for h in range(HG_HEADS):
            c = slice(h * HG_DIM, (h + 1) * HG_DIM)
            ohat, rstd = _rms(o_ref[:, c])
            gate = g_ref[:, c]
            d = d_ref[:, c]
            dg_ref[:, c] = (d * (ohat * og) * _dsilu(gate)).astype(BF16)
            dnormed = d * _silu(gate)
            dog += _rowsum(dnormed * ohat)
            do_s[:, c] = _rms_bwd(dnormed * og, ohat, rstd)
        dog_ref[...] += dog

        lb = _lower_bound(lg_ref)
        f = lb + (1.0 - lb) * _sigmoid(f_ref[...])
        kk_s[...] = 1.0 - f
        lf_s[...] = jnp.log(f)
        qf_s[...] = _silu(q_ref[...])
        causal, upper, tri, tri_t = _chunk_masks()

        def chunk(step, carry):
            ci = HG_TILE_CHUNKS - 1 - step
            rows = pl.ds(pl.multiple_of(ci * HG_CHUNK, HG_CHUNK), HG_CHUNK)
            srows = pl.ds(pl.multiple_of(ci * HG_DIM, HG_DIM), HG_DIM)
            lf = lf_s[rows, :]
            b = _dot_f32(tri, lf)
            bl = _rowsum(lf)
            ref = 0.5 * bl
            qf, kk, v, do = qf_s[rows, :], kk_s[rows, :], i_ref[rows, :], do_s[rows, :]
            eb, ebr, erb, ekd, ebl = jnp.exp(b), jnp.exp(b - ref), jnp.exp(ref - b), jnp.exp(bl - b), jnp.exp(bl)
            a_in, a_t, b_t, kd = qf * eb, qf * ebr, kk * erb, kk * ekd
            for h in range(HG_HEADS):
                c = slice(h * HG_DIM, (h + 1) * HG_DIM)
                st, dst = st_ref[srows, c], dstate[h]
                stb, dstb = st.astype(BF16), dst.astype(BF16)
                doh, vh = do[:, c], v[:, c]
                dob, vb = doh.astype(BF16), vh.astype(BF16)
                ain_h, at_h, bt_h, kd_h = a_in[:, c], a_t[:, c], b_t[:, c], kd[:, c]
                atb, btb = at_h.astype(BF16), bt_h.astype(BF16)
                d_ain = _dot(dob, stb)
                p_t = jnp.where(upper, _dot_nt(btb, atb), 0.0).astype(BF16)
                dp = jnp.where(causal, _dot_nt(dob, vb), 0.0).astype(BF16)
                dp_t = jnp.where(upper, _dot_nt(vb, dob), 0.0).astype(BF16)
                di_ref[rows, c] = (_dot(p_t, dob) + _dot_nt(kd_h.astype(BF16), dstb)).astype(BF16)
                d_at = _dot(dp, btb)
                d_bt = _dot(dp_t, atb)
                d_kd = _dot(vb, dstb)
                dqf = d_ain * eb[:, c] + d_at * ebr[:, c]
                dkk = d_bt * erb[:, c] + d_kd * ekd[:, c]
                db = d_ain * ain_h + d_at * atb.astype(F32) - d_bt * btb.astype(F32) - d_kd * kd_h
                dbl = _rowsum(d_kd * kd_h) + _rowsum(dst * st) * ebl[:, c]
                dstate[h] = _dot_tn(dob, ain_h.astype(BF16)) + dst * ebl[:, c]
                dlf = _dot_f32(tri_t, db) + dbl
                qv, fr = q_ref[rows, c], f_ref[rows, c]
                lbh = lb[:, c]
                sg = _sigmoid(fr)
                dfv = dlf / (lbh + (1.0 - lbh) * sg) - dkk
                df_ref[rows, c] = (dfv * (1.0 - lbh) * sg * (1.0 - sg)).astype(BF16)
                dlb_ref[:, c] += _rowsum(dfv * (1.0 - sg))
                dq_ref[rows, c] = (dqf * _dsilu(qv)).astype(BF16)
            return carry

        lax.fori_loop(0, HG_TILE_CHUNKS, chunk, 0, unroll=True)

    rev = pl.BlockSpec((HG_TILE, HG_WIDTH), lambda i: (n_tiles - 1 - i, 0))
    return pl.pallas_call(
        body, name="hg_bwd", grid=(n_tiles,),
        out_shape=[jax.ShapeDtypeStruct((s, HG_WIDTH), BF16)] * 4
        + [jax.ShapeDtypeStruct((1, HG_DIM), F32), jax.ShapeDtypeStruct((1, HG_WIDTH), F32)],
        in_specs=[rev] * 4 + [_whole((2, HG_WIDTH)), _whole((1, HG_DIM)), rev,
                              pl.BlockSpec((HG_TILE_CHUNKS * HG_DIM, HG_WIDTH), lambda i: (n_tiles - 1 - i, 0)), rev],
        out_specs=[rev] * 4 + [_whole((1, HG_DIM)), _whole((1, HG_WIDTH))],
        scratch_shapes=[pltpu.VMEM((HG_HEADS, HG_DIM, HG_DIM), F32)] + [pltpu.VMEM((HG_TILE, HG_WIDTH), F32)] * 4,
        compiler_params=_params("arbitrary"),
    )(hq, hf, hi, hgt, logits, onorm_g, o, states, dout)


TOKEN_GROUP = 16


def _att_geometry(dil, seq=0):
    per_group = TOKEN_GROUP // dil
    ub = ATT_BLOCK // per_group
    if dil == TOKEN_GROUP:
        n_blocks = 2 if seq % (2 * ub * TOKEN_GROUP) == 0 and seq > 0 else 1
    else:
        n_blocks = 4
    return per_group, ub, ATT_WIDTH if dil == 1 else 128, n_blocks


def _att_consts(dil):
    per_group, ub = _att_geometry(dil)[:2]

    def pos(i):
        return i if dil == 1 else (i % ub) * per_group + i // ub

    lane = lax.broadcasted_iota(jnp.int32, (ATT_BLOCK, 128), 1)
    qi = pos(lax.broadcasted_iota(jnp.int32, (2 * ATT_BLOCK, ATT_BLOCK), 0) % ATT_BLOCK)
    kj = pos(lax.broadcasted_iota(jnp.int32, (2 * ATT_BLOCK, ATT_BLOCK), 1))
    return lane < ATT_HEAD_DIM, kj <= qi, lambda off: kj >= qi + off


def _load_tile(ref, dil, r, c, base=0):
    per_group, ub = _att_geometry(dil)[:2]
    if dil == 1:
        return ref[base:base + ATT_BLOCK, c]
    return jnp.concatenate([ref[pl.ds(base + dil * w + r, ub, stride=TOKEN_GROUP), c] for w in range(per_group)], axis=0)


def _store_tile(ref, dil, r, c, val, base=0):
    per_group, ub = _att_geometry(dil)[:2]
    if dil == 1:
        ref[base:base + ATT_BLOCK, c] = val
        return
    for w in range(per_group):
        ref[pl.ds(base + dil * w + r, ub, stride=TOKEN_GROUP), c] = val[w * ub:(w + 1) * ub]


def _stack_heads(x2, first):
    return jnp.concatenate([jnp.where(first, x2, 0.0), jnp.where(first, 0.0, x2)], axis=0)


def _stack_bcast(x2, first):
    other = pltpu.roll(x2, ATT_HEAD_DIM, axis=1)
    return jnp.concatenate([jnp.where(first, x2, other), jnp.where(first, other, x2)], axis=0)


def _unstack_heads(st, first):
    return jnp.where(first, st[:ATT_BLOCK], st[ATT_BLOCK:])


def _att_fwd(q, k, v, dil):
    seq, width = q.shape
    _, ub, lanes, nbs = _att_geometry(dil, seq)
    rows = ub * TOKEN_GROUP
    n_steps = seq // (nbs * rows)

    def body(q_ref, k_ref, v_ref, kp_ref, vp_ref, o_ref, lse_ref):
        first, cur_ok, _band = _att_consts(dil)
        inner_ok = _band(0)
        edge_ok = _band(jnp.where(pl.program_id(0) > 0, 0, ATT_BLOCK))
        for r in range(dil):
            for j in range(lanes // 128):
                c = slice(j * 128, (j + 1) * 128)
                kc = vc = None
                for b in range(nbs):
                    base = b * rows
                    prev_ok = edge_ok if b == 0 else inner_ok
                    if b == 0:
                        kp, vp = _load_tile(kp_ref, dil, r, c).astype(BF16), _load_tile(vp_ref, dil, r, c).astype(BF16)
                    else:
                        kp, vp = kc, vc
                    qst = _stack_heads(_load_tile(q_ref, dil, r, c, base) * ATT_SCALE, first).astype(BF16)
                    kc = _load_tile(k_ref, dil, r, c, base).astype(BF16)
                    vc = _load_tile(v_ref, dil, r, c, base).astype(BF16)
                    sc = jnp.where(cur_ok, _dot_nt(qst, kc), NEG)
                    sp = jnp.where(prev_ok, _dot_nt(qst, kp), NEG)
                    mx = jnp.max(jnp.maximum(sc, sp), axis=-1, keepdims=True)
                    pc, pp = jnp.exp(sc - mx), jnp.exp(sp - mx)
                    den = jnp.sum(pc + pp, axis=-1, keepdims=True)
                    ost = (_dot(pc.astype(BF16), vc) + _dot(pp.astype(BF16), vp)) / den
                    lse = jnp.broadcast_to(mx + jnp.log(den), (2 * ATT_BLOCK, 128))
                    _store_tile(o_ref, dil, r, c, _unstack_heads(ost, first), base)
                    _store_tile(lse_ref, dil, r, c, _unstack_heads(lse, first), base)

    slab = pl.BlockSpec((nbs * rows, lanes), lambda n, j: (n, j))
    before = pl.BlockSpec((rows, lanes), lambda n, j: (jnp.maximum(n * nbs - 1, 0), j))
    return pl.pallas_call(
        body, name=f"att_fwd_d{dil}", grid=(n_steps, width // lanes),
        out_shape=[jax.ShapeDtypeStruct((seq, width), F32)] * 2,
        in_specs=[slab, slab, slab, before, before], out_specs=[slab, slab],
        compiler_params=_params("arbitrary", "arbitrary"),
    )(q, k, v, k, v)


def _att_bwd(q, k, v, do, cc, lse, dil):
    seq, width = q.shape
    _, ub, lanes, nbs = _att_geometry(dil, seq)
    rows = ub * TOKEN_GROUP
    n_blocks = seq // rows
    n_steps = n_blocks // nbs

    def body(q_ref, k_ref, v_ref, do_ref, cc_ref, lse_ref, qx_ref, dox_ref, ccx_ref, lsex_ref,
             dq_ref, dk_ref, dv_ref, carry):
        first, cur_ok, _band = _att_consts(dil)
        step = pl.program_id(1)
        inner_ok = _band(0)
        edge_ok = _band(jnp.where(step < n_steps - 1, 0, ATT_BLOCK))

        @pl.when(step == 0)
        def _():
            carry[...] = jnp.zeros_like(carry)

        def queries(refs, r, c, base):
            q_r, do_r, lse_r, cc_r = refs
            return (_stack_heads(_load_tile(q_r, dil, r, c, base) * ATT_SCALE, first).astype(BF16),
                    _stack_heads(_load_tile(do_r, dil, r, c, base), first).astype(BF16),
                    _stack_bcast(_load_tile(lse_r, dil, r, c, base), first),
                    _stack_bcast(_load_tile(cc_r, dil, r, c, base), first))

        for r in range(dil):
            for j in range(lanes // 128):
                c = slice(j * 128, (j + 1) * 128)
                own = queries((q_ref, do_ref, lse_ref, cc_ref), r, c, 0)
                left = _load_tile(carry, dil, r, c)
                for b in range(nbs):
                    base = b * rows
                    last = b == nbs - 1
                    next_ok = edge_ok if last else inner_ok
                    if last:
                        following = queries((qx_ref, dox_ref, lsex_ref, ccx_ref), r, c, 0)
                    else:
                        following = queries((q_ref, do_ref, lse_ref, cc_ref), r, c, base + rows)
                    (qst, dost, lse_n, cc_n), (qxst, doxst, lse_x, cc_x) = own, following
                    kb = _load_tile(k_ref, dil, r, c, base).astype(BF16)
                    vb = _load_tile(v_ref, dil, r, c, base).astype(BF16)
                    p_cur = jnp.exp(jnp.where(cur_ok, _dot_nt(qst, kb), NEG) - lse_n)
                    p_next = jnp.exp(jnp.where(next_ok, _dot_nt(qxst, kb), NEG) - lse_x)
                    ds_cur = (p_cur * (_dot_nt(dost, vb) + cc_n)).astype(BF16)
                    ds_next = (p_next * (_dot_nt(doxst, vb) + cc_x)).astype(BF16)
                    dq_own = left + _unstack_heads(_dot(ds_cur, kb), first)
                    _store_tile(dq_ref, dil, r, c, dq_own * ATT_SCALE, base)
                    _store_tile(dk_ref, dil, r, c, _dot_tn(ds_cur, qst) + _dot_tn(ds_next, qxst), base)
                    _store_tile(dv_ref, dil, r, c, _dot_tn(p_cur.astype(BF16), dost) + _dot_tn(p_next.astype(BF16), doxst), base)
                    left = _unstack_heads(_dot(ds_next, kb), first)
                    own = following
                _store_tile(carry, dil, r, c, left)

    slab = pl.BlockSpec((nbs * rows, lanes), lambda j, n: (n, j))
    after = pl.BlockSpec((rows, lanes), lambda j, n: (jnp.minimum((n + 1) * nbs, n_blocks - 1), j))
    return pl.pallas_call(
        body, name=f"att_bwd_d{dil}", grid=(width // lanes, n_steps),
        out_shape=[jax.ShapeDtypeStruct((seq, width), F32)] * 3,
        in_specs=[slab] * 6 + [after] * 4, out_specs=[slab] * 3,
        scratch_shapes=[pltpu.VMEM((rows, lanes), F32)],
        compiler_params=_params("arbitrary", "arbitrary"),
    )(q, k, v, do, cc, lse, q, do, cc, lse)


def _branch_weights(lses):
    mx = jnp.maximum(jnp.maximum(lses[0], lses[1]), lses[2])
    es = [jnp.exp(l - mx) for l in lses]
    inv = 1.0 / (es[0] + es[1] + es[2])
    return [e * inv for e in es]


def _att_combine(outs, lses, att_g):
    s = outs[0].shape[0]
    tm = 512

    def body(o0, o1, o2, l0, l1, l2, g_ref, att_ref, out_ref):
        ws = _branch_weights([l0[...], l1[...], l2[...]])
        att = ws[0] * o0[...] + ws[1] * o1[...] + ws[2] * o2[...]
        att_ref[...] = att
        ahat, _ = _rms(att)
        out_ref[...] = (ahat * g_ref[...]).astype(BF16)

    tile = _rows(tm, ATT_WIDTH)
    return pl.pallas_call(
        body, name="att_combine", grid=(s // tm,),
        out_shape=[jax.ShapeDtypeStruct((s, ATT_WIDTH), F32), jax.ShapeDtypeStruct((s, ATT_WIDTH), BF16)],
        in_specs=[tile] * 6 + [_whole((1, ATT_WIDTH))], out_specs=[tile, tile],
        compiler_params=_params("parallel"),
    )(*outs, *lses, att_g)


def _att_combine_bwd(datt_out, att, lses, att_g):
    s = att.shape[0]
    tm = 256

    def body(d_ref, att_ref, l0, l1, l2, g_ref, do0, do1, do2, cc0, cc1, cc2, dg_ref):
        @pl.when(pl.program_id(0) == 0)
        def _():
            dg_ref[...] = jnp.zeros_like(dg_ref)

        att = att_ref[...]
        ahat, rstd = _rms(att)
        d = d_ref[...]
        dg_ref[...] += _rowsum(d * ahat)
        datt = _rms_bwd(d * g_ref[...], ahat, rstd)
        hi = lax.broadcasted_iota(jnp.int32, (ATT_WIDTH, ATT_WIDTH), 0) // ATT_HEAD_DIM
        hj = lax.broadcasted_iota(jnp.int32, (ATT_WIDTH, ATT_WIDTH), 1) // ATT_HEAD_DIM
        same_head = (hi == hj).astype(BF16)
        prod = datt * att
        prod_hi = prod.astype(BF16)
        prod_lo = (prod - prod_hi.astype(F32)).astype(BF16)
        head_sum = _dot(prod_hi, same_head) + _dot(prod_lo, same_head)
        ws = _branch_weights([l0[...], l1[...], l2[...]])
        for w, do_ref, cc_ref in zip(ws, (do0, do1, do2), (cc0, cc1, cc2)):
            do_ref[...] = w * datt
            cc_ref[...] = -w * head_sum

    tile = _rows(tm, ATT_WIDTH)
    return pl.pallas_call(
        body, name="att_combine_bwd", grid=(s // tm,),
        out_shape=[jax.ShapeDtypeStruct((s, ATT_WIDTH), F32)] * 6 + [jax.ShapeDtypeStruct((1, ATT_WIDTH), F32)],
        in_specs=[tile] * 5 + [_whole((1, ATT_WIDTH))], out_specs=[tile] * 6 + [_whole((1, ATT_WIDTH))],
        compiler_params=_params("arbitrary"),
    )(datt_out, att, *lses, att_g)


def _out_fwd(x, hg, at, mod, w_out):
    s = x.shape[0]
    tm = 512

    def body(x_ref, hg_ref, at_ref, mod_ref, w_ref, x1_ref):
        mix = _dot(hg_ref[...], w_ref[0:512, :]) + _dot(at_ref[...], w_ref[512:1024, :])
        x1_ref[...] = x_ref[...] + mod_ref[:, 2 * D_MODEL:3 * D_MODEL] * mix

    return pl.pallas_call(
        body, name="out_fwd", grid=(s // tm,), out_shape=jax.ShapeDtypeStruct((s, D_MODEL), F32),
        in_specs=[_rows(tm, D_MODEL), _rows(tm, 512), _rows(tm, 512), _whole((1, 6 * D_MODEL)), _whole((D_MODEL, D_MODEL))],
        out_specs=_rows(tm, D_MODEL), compiler_params=_params("parallel"),
    )(x, hg, at, mod, w_out)


def _out_bwd(dx1, hg, at, mod, w_out):
    s = dx1.shape[0]
    tm = 512
    n_steps = s // tm

    def body(dx_ref, hg_ref, at_ref, mod_ref, w_ref, dhg_ref, dat_ref, dw_ref, dwb_ref, dgate_ref):
        @pl.when(pl.program_id(0) == 0)
        def _():
            dw_ref[...] = jnp.zeros_like(dw_ref)
            dgate_ref[...] = jnp.zeros_like(dgate_ref)

        hg, at, dx = hg_ref[...], at_ref[...], dx_ref[...]
        mix = _dot(hg, w_ref[0:512, :]) + _dot(at, w_ref[512:1024, :])
        dgate_ref[...] += _rowsum(dx * mix)
        dmix = (mod_ref[:, 2 * D_MODEL:3 * D_MODEL] * dx).astype(BF16)
        dhg_ref[...] = _dot_nt(dmix, w_ref[0:512, :])
        dat_ref[...] = _dot_nt(dmix, w_ref[512:1024, :])
        dw_ref[0:512, :] += _dot_tn(hg, dmix)
        dw_ref[512:1024, :] += _dot_tn(at, dmix)

        @pl.when(pl.program_id(0) == n_steps - 1)
        def _():
            dwb_ref[...] = dw_ref[...].astype(BF16)

    return pl.pallas_call(
        body, name="out_bwd", grid=(n_steps,),
        out_shape=[jax.ShapeDtypeStruct((s, 512), F32)] * 2
        + [jax.ShapeDtypeStruct((D_MODEL, D_MODEL), F32), jax.ShapeDtypeStruct((D_MODEL, D_MODEL), BF16),
           jax.ShapeDtypeStruct((1, D_MODEL), F32)],
        in_specs=[_rows(tm, D_MODEL), _rows(tm, 512), _rows(tm, 512), _whole((1, 6 * D_MODEL)), _whole((D_MODEL, D_MODEL))],
        out_specs=[_rows(tm, 512), _rows(tm, 512), _whole((D_MODEL, D_MODEL)), _whole((D_MODEL, D_MODEL)), _whole((1, D_MODEL))],
        compiler_params=_params("arbitrary"),
    )(dx1, hg, at, mod, w_out)


def _ffn(x1, target, mod, g2, gf, w_gu, w_down):
    s = x1.shape[0]
    tm = 256

    def body(x_ref, t_ref, mod_ref, g2_ref, gf_ref, wgu_hbm, wd_hbm,
             dx_ref, h2_ref, act_ref, dau_ref, dff_ref, sums_ref, loss_ref, wgu, wd, au_s, sem):
        @pl.when(pl.program_id(0) == 0)
        def _():
            c1 = pltpu.make_async_copy(wgu_hbm, wgu, sem.at[0])
            c2 = pltpu.make_async_copy(wd_hbm, wd, sem.at[1])
            c1.start()
            c2.start()
            c1.wait()
            c2.wait()
            sums_ref[...] = jnp.zeros_like(sums_ref)
            loss_ref[...] = jnp.zeros_like(loss_ref)

        x1v = x_ref[...]
        xhat, rstd = _rms(x1v)
        g2 = g2_ref[...]
        n2 = xhat * g2
        scale2 = 1.0 + mod_ref[:, 4 * D_MODEL:5 * D_MODEL]
        gate2 = mod_ref[:, 5 * D_MODEL:6 * D_MODEL]
        hb = (n2 * scale2 + mod_ref[:, 3 * D_MODEL:4 * D_MODEL]).astype(BF16)
        h2_ref[...] = hb
        au_s[...] = _dot_nt(hb, wgu[...])
        a = au_s[:, 0:D_FF]
        act = (_silu(a) * au_s[:, D_FF:2 * D_FF]).astype(BF16)
        act_ref[...] = act
        ff = _dot(act, wd[...])
        x2 = x1v + gate2 * ff
        nf, rstd_f = _rms(x2)
        gfv = gf_ref[...]
        err = nf * gfv - t_ref[...]
        loss_ref[...] += 0.5 * jnp.sum(_rowsum(err * err), axis=-1, keepdims=True) * (1.0 / D_MODEL)
        dy = err * (1.0 / D_MODEL)
        dx2 = _rms_bwd(dy * gfv, nf, rstd_f)
        dffb = (gate2 * dx2).astype(BF16)
        dff_ref[...] = dffb
        dact = _dot_nt(dffb, wd[...])
        a = au_s[:, 0:D_FF]
        dau_ref[:, 0:D_FF] = (dact * au_s[:, D_FF:2 * D_FF] * _dsilu(a)).astype(BF16)
        dau_ref[:, D_FF:2 * D_FF] = (dact * _silu(a)).astype(BF16)
        dh = _dot(dau_ref[...], wgu[...])
        dn = dh * scale2
        sums_ref[0:1, :] += _rowsum(dh)
        sums_ref[1:2, :] += _rowsum(dh * n2)
        sums_ref[2:3, :] += _rowsum(dx2 * ff)
        sums_ref[3:4, :] += _rowsum(dn * xhat)
        sums_ref[4:5, :] += _rowsum(dy * nf)
        dx_ref[...] = dx2 + _rms_bwd(dn * g2, xhat, rstd)

    vec = _whole((1, D_MODEL))
    hbm = pl.BlockSpec(memory_space=pl.ANY)
    return pl.pallas_call(
        body, name="ffn", grid=(s // tm,),
        out_shape=[jax.ShapeDtypeStruct((s, D_MODEL), F32), jax.ShapeDtypeStruct((s, D_MODEL), BF16),
                   jax.ShapeDtypeStruct((s, D_FF), BF16), jax.ShapeDtypeStruct((s, 2 * D_FF), BF16),
                   jax.ShapeDtypeStruct((s, D_MODEL), BF16), jax.ShapeDtypeStruct((8, D_MODEL), F32),
                   jax.ShapeDtypeStruct((1, 128), F32)],
        in_specs=[_rows(tm, D_MODEL), _rows(tm, D_MODEL), _whole((1, 6 * D_MODEL)), vec, vec, hbm, hbm],
        out_specs=[_rows(tm, D_MODEL), _rows(tm, D_MODEL), _rows(tm, D_FF), _rows(tm, 2 * D_FF), _rows(tm, D_MODEL),
                   _whole((8, D_MODEL)), _whole((1, 128))],
        scratch_shapes=[pltpu.VMEM((2 * D_FF, D_MODEL), BF16), pltpu.VMEM((D_FF, D_MODEL), BF16),
                        pltpu.VMEM((tm, 2 * D_FF), F32), pltpu.SemaphoreType.DMA((2,))],
        compiler_params=_params("arbitrary"),
    )(x1, target, mod, g2, gf, w_gu, w_down)


def _weight_grad(a, b, name, rounded=False):
    s, m = a.shape
    n = b.shape[1]
    ts = min(s, 2048)
    n_steps = s // ts
    tm = max(t for t in range(128, m + 1, 128) if m % t == 0 and t * n * 4 <= 6 * 1024 * 1024)

    def body(a_ref, b_ref, o_ref, *ob_ref):
        @pl.when(pl.program_id(1) == 0)
        def _():
            o_ref[...] = jnp.zeros_like(o_ref)

        o_ref[...] += _dot_tn(a_ref[...], b_ref[...])
        if rounded:
            @pl.when(pl.program_id(1) == n_steps - 1)
            def _():
                ob_ref[0][...] = o_ref[...].astype(BF16)

    tile = pl.BlockSpec((tm, n), lambda j, i: (j, 0))
    return pl.pallas_call(
        body, name=name, grid=(m // tm, n_steps),
        out_shape=[jax.ShapeDtypeStruct((m, n), F32)] + [jax.ShapeDtypeStruct((m, n), BF16)] * rounded,
        in_specs=[pl.BlockSpec((ts, tm), lambda j, i: (i, j)), pl.BlockSpec((ts, n), lambda j, i: (i, 0))],
        out_specs=[tile] + [tile] * rounded,
        compiler_params=_params("parallel", "arbitrary"),
    )(a, b)


def _adamw_math(w, g, m, v):
    m = ADAM_B1 * m + (1.0 - ADAM_B1) * g
    v = ADAM_B2 * v + (1.0 - ADAM_B2) * (g * g)
    m_hat = m / (1.0 - ADAM_B1 ** ADAM_STEP)
    v_hat = v / (1.0 - ADAM_B2 ** ADAM_STEP)
    delta = -ADAM_LR * (m_hat / (jnp.sqrt(v_hat) + ADAM_EPS) + ADAM_WD * w)
    return delta, m, v


def _adamw_shard(where, w, m, v, partial, got, name):
    r, c = w.shape
    tr = _shard_rows(r)
    lead = partial.ndim - 2
    n_got = got.shape[0]

    def body(where_ref, w_ref, m_ref, v_ref, own_ref, *rest):
        got_refs, (grad_ref, d_ref, nm_ref, nv_ref) = rest[:n_got], rest[n_got:]
        g = own_ref[...]
        for g_ref in got_refs:
            g = g + g_ref[...].astype(F32)
        grad_ref[...] = g
        d_ref[...], nm_ref[...], nv_ref[...] = _adamw_math(w_ref[...], g, m_ref[...], v_ref[...])

    tile = pl.BlockSpec((tr, c), lambda i, where_ref: (i, 0))
    own = pl.BlockSpec((None,) * lead + (tr, c), lambda i, where_ref: (*[where_ref[d] for d in range(lead)], i, 0))
    part = [pl.BlockSpec((None, tr, c), functools.partial(lambda j, i, where_ref: (j, i, 0), j)) for j in range(n_got)]
    return pl.pallas_call(
        body, name=name,
        grid_spec=pltpu.PrefetchScalarGridSpec(num_scalar_prefetch=1, grid=(r // tr,), in_specs=[tile] * 3 + [own] + part,
                                               out_specs=[tile] * 4),
        out_shape=[jax.ShapeDtypeStruct((r, c), F32)] * 4, compiler_params=_params("parallel"),
    )(where, w, m, v, partial, *[got] * n_got)


def _small_update(small_all, dmod_blocks, c_all, logits, w_ada, m_ada, v_ada, smalls):
    def body(sm_ref, dm_ref, c_ref, lg_ref, wa_ref, ma_ref, va_ref, *rest):
        ins, outs = rest[:21], rest[21:]
        _, me = _flip(0)
        tot = sm_ref[0:1, :]
        for i in range(1, N_DEV):
            tot = tot + sm_ref[i:i + 1, :]
        loss_ref = outs[0]
        loss_ref[...] = tot[:, SM_LOSS:SM_LOSS + 128]
        g_ada = lax.dot_general(_silu(c_ref[...]), dm_ref[me], (((0,), (0,)), ((), ())),
                                preferred_element_type=F32, precision=HIGHEST)
        outs[1][...] = g_ada
        outs[2][...], outs[3][...], outs[4][...] = _adamw_math(wa_ref[...], g_ada, ma_ref[...], va_ref[...])
        p0 = _lower_bound(lg_ref)
        dl0 = tot[:, SM_LB:SM_LB + 512] * p0 * (1.0 - p0)
        grads = [tot[:, SM_MOD:SM_MOD + 6 * D_MODEL], tot[:, SM_G1:SM_G1 + D_MODEL], tot[:, SM_G2:SM_G2 + D_MODEL],
                 tot[:, SM_GF:SM_GF + D_MODEL], tot[:, SM_ATT:SM_ATT + 512], tot[:, SM_HG:SM_HG + 128],
                 jnp.where(lax.broadcasted_iota(jnp.int32, (2, 512), 0) == 0, dl0, -dl0)]
        for i, g in enumerate(grads):
            w_ref, m_ref, v_ref = ins[3 * i:3 * i + 3]
            o = outs[5 + 4 * i:9 + 4 * i]
            o[0][...] = g
            o[1][...], o[2][...], o[3][...] = _adamw_math(w_ref[...], g, m_ref[...], v_ref[...])

    flat = [t for trio in smalls for t in trio]
    vm = pl.BlockSpec(memory_space=pltpu.VMEM)
    out_shape = [jax.ShapeDtypeStruct((1, 128), F32)] + [jax.ShapeDtypeStruct(w_ada.shape, F32)] * 4
    for trio in smalls:
        out_shape += [jax.ShapeDtypeStruct(trio[0].shape, F32)] * 4
    return pl.pallas_call(
        body, name="small_update", out_shape=out_shape,
        in_specs=[vm] * (7 + len(flat)), out_specs=[vm] * len(out_shape),
        compiler_params=pltpu.CompilerParams(vmem_limit_bytes=V7X_VMEM_LIMIT),
    )(small_all, dmod_blocks, c_all, logits, w_ada, m_ada, v_ada, *flat)


def kernel(x, c, w_ada, b_ada, norm1_g, w_in, hg_lb_logits, hg_onorm_g, att_onorm_g, w_out, norm2_g, w_gate_up, w_down, final_g, loss_target, m_w_ada, m_b_ada, m_norm1_g, m_w_in, m_hg_lb_logits, m_hg_onorm_g, m_att_onorm_g, m_w_out, m_norm2_g, m_w_gate_up, m_w_down, m_final_g, v_w_ada, v_b_ada, v_norm1_g, v_w_in, v_hg_lb_logits, v_hg_onorm_g, v_att_onorm_g, v_w_out, v_norm2_g, v_w_gate_up, v_w_down, v_final_g):
    x2d, target = x[0], loss_target[0]
    seq = x2d.shape[0]
    assert seq % (ATT_BLOCK * max(DILATIONS)) == 0 and seq % HG_TILE == 0
    gf = final_g.reshape(1, D_MODEL)

    c_all = _exchange_small(c.reshape(8, D_MODEL // 8), None, "gather_c").reshape(N_DEV, D_MODEL)
    ada = _ada_rows(c_all, w_ada[0], b_ada)
    mod = _exchange_small(ada, 1, "scatter_mod").reshape(1, 6 * D_MODEL)

    core = lax.axis_index("c").astype(jnp.int32).reshape(1)
    chip = (2 * lax.axis_index("x") + lax.axis_index("y")).astype(jnp.int32).reshape(1)
    me = 4 * lax.axis_index("x") + 2 * lax.axis_index("y") + lax.axis_index("c")

    g_in, = _gather_weights([w_in[0].T.astype(BF16)])
    w_in_b = g_in.reshape(IN_WIDTH, D_MODEL)
    rest_shards = [w_out[0].astype(BF16), w_gate_up[0].T.astype(BF16), w_down[0].astype(BF16)]
    lands = [lax.empty((N_DEV,) + s.shape, BF16) for s in rest_shards]
    g_send, g_recv, g_srcs, g_lands, tok = _copies_start("gather_rest_start", _plan_gather_own, 12, rest_shards, lands, [w_in_b, mod])
    flight = {}

    def stage(name, *vals):
        if name == "attention_begun":
            flight["shards"], got = _copies_wait("gather_rest_wait", _plan_gather_own, g_send, g_recv, g_srcs, g_lands, list(vals))
            flight["pass"] = _copies_start("gather_pass_start", _plan_gather_pass, 9, [], got, [])
            return flight["pass"][4][0:1, 0:1]
        if name == "mixer_weights_done":
            shapes = [(4, 2, D_MODEL // N_DEV, D_MODEL), (4, 2, 2 * D_FF // N_DEV, D_MODEL), (4, 2, D_FF // N_DEV, D_MODEL)]
            flight["grads"] = [g32.reshape(sh) for (g32, _), sh in zip(vals, shapes)]
            rounded = [g16.reshape(sh) for (_, g16), sh in zip(vals, shapes)]
            direct_lands = [lax.empty((N_DEV - 1,) + sh[2:], BF16) for sh in shapes]
            flight["direct"] = _copies_start("reduce_rest_start", _plan_reduce_direct, 21, rounded, direct_lands, [])
            return flight["direct"][4][0:1, 0:1]
        raise ValueError(name)

    def rest_weights(after):
        s, r, _, p_lands, _ = flight["pass"]
        _, got = _copies_wait("gather_pass_wait", _plan_gather_pass, s, r, [], p_lands, [after])
        full = [lax.dynamic_update_index_in_dim(g, shard, me, 0) for g, shard in zip(got, flight["shards"])]
        return full[0].reshape(D_MODEL, D_MODEL), full[1].reshape(2 * D_FF, D_MODEL), full[2].reshape(D_FF, D_MODEL)

    grad_x, dw_in, small = _block_step(x2d, target, mod + tok[0:1, 0:1], norm1_g, hg_lb_logits, hg_onorm_g, att_onorm_g, norm2_g, gf,
                                       w_in_b, rest_weights, stage)

    g_in8 = dw_in.reshape(4, 2, IN_WIDTH // N_DEV, D_MODEL)
    in_pairs = _copies_start("reduce_pairs_in_start", _plan_reduce_pairs, 4, [g_in8], [lax.empty((4,) + g_in8.shape[2:], F32)], [])
    s, r, srcs, d_lands, _ = flight["direct"]
    _, recv_rest = _copies_wait("reduce_rest_wait", _plan_reduce_direct, s, r, srcs, d_lands, [in_pairs[4]])
    small_rows = jnp.pad(small + in_pairs[4][0:1, 0:1], ((0, 0), (0, SM_PADDED - SM_WIDTH))).reshape(SM_PADDED // 128, 128)
    small_all = _exchange_small(small_rows, None, "gather_small").reshape(N_DEV, SM_PADDED)[:, :SM_WIDTH]
    in_grads, got_in = _copies_wait("reduce_pairs_in_wait", _plan_reduce_pairs, in_pairs[0], in_pairs[1], in_pairs[2], in_pairs[3],
                                    [small_all])
    in_s32, in_s16 = _pair_sum(core, in_grads[0], got_in[0], "pair_sum_in")
    in_chips = _copies_start("reduce_chips_in_start", _plan_reduce_chips, 3, [in_s16], [lax.empty((3,) + in_s16.shape[1:], BF16)], [])
    big, updated = {}, []
    rest_params = [("w_out", w_out, m_w_out, v_w_out), ("w_gate_up", w_gate_up, m_w_gate_up, v_w_gate_up), ("w_down", w_down, m_w_down, v_w_down)]
    mine = jnp.concatenate([chip, core])
    for (n, w, m, v), g32, got in zip(rest_params, flight["grads"], recv_rest):
        if n == "w_gate_up":
            outs4 = _adamw_shard(mine, w[0].T, m[0].T, v[0].T, g32, got, f"adamw_{n}")
            big[n] = [t.T[None] for t in outs4]
        else:
            outs4 = _adamw_shard(mine, w[0], m[0], v[0], g32, got, f"adamw_{n}")
            big[n] = [t[None] for t in outs4]
        updated.append(outs4[3])
    c_all = c_all + in_chips[4][0:1, 0:1]
    smalls = [(b_ada, m_b_ada, v_b_ada), (norm1_g, m_norm1_g, v_norm1_g), (norm2_g, m_norm2_g, v_norm2_g),
              (gf, m_final_g.reshape(1, D_MODEL), v_final_g.reshape(1, D_MODEL)),
              (att_onorm_g, m_att_onorm_g, v_att_onorm_g), (hg_onorm_g, m_hg_onorm_g, v_hg_onorm_g),
              (hg_lb_logits, m_hg_lb_logits, v_hg_lb_logits)]
    dmod_blocks = small_all[:, :6 * D_MODEL].reshape(N_DEV, N_DEV, 6 * D_MODEL // N_DEV).transpose(1, 0, 2)
    res = _small_update(small_all, dmod_blocks, c_all, hg_lb_logits, w_ada[0], m_w_ada[0], v_w_ada[0], smalls)
    _, recv_in = _copies_wait("reduce_chips_in_wait", _plan_reduce_chips, in_chips[0], in_chips[1], in_chips[2], in_chips[3],
                              [res[0]] + updated)
    big["w_in"] = [t.T[None] for t in _adamw_shard(chip, w_in[0].T, m_w_in[0].T, v_w_in[0].T, in_s32, recv_in[0], "adamw_w_in")]
    loss = res[0][0, 0]
    ada4 = [t[None] for t in res[1:5]]
    sm4 = {n: list(res[5 + 4 * i:9 + 4 * i]) for i, n in enumerate(["b_ada", "norm1_g", "norm2_g", "final_g", "att", "hg", "lb"])}
    sm4["final_g"] = [t.reshape(D_MODEL) for t in sm4["final_g"]]

    order = [ada4, sm4["b_ada"], sm4["norm1_g"], big["w_in"], sm4["lb"], sm4["hg"], sm4["att"], big["w_out"], sm4["norm2_g"],
             big["w_gate_up"], big["w_down"], sm4["final_g"]]
    return (loss, grad_x[None], *[o[0] for o in order], *[o[1] for o in order], *[o[2] for o in order], *[o[3] for o in order])


def _block_step(x2d, target, mod, norm1_g, hg_lb_logits, hg_onorm_g, att_onorm_g, norm2_g, gf, w_in_b, rest_weights, stage):
    h1, hq, hf, hi, hgt, aq, ak, av = _in_fwd(x2d, mod, norm1_g, w_in_b)
    hg_out, hg_o, hg_states = _hg_fwd(hq, hf, hi, hgt, hg_lb_logits, hg_onorm_g)
    branch = [_att_fwd(aq, ak, av, d) for d in DILATIONS[:2]]
    att_g = att_onorm_g + stage("attention_begun", branch[0][0], branch[1][0])
    branch += [_att_fwd(aq, ak, av, d) for d in DILATIONS[2:]]
    outs = [b[0] for b in branch]
    lses = [b[1] for b in branch]
    att, att_out = _att_combine(outs, lses, att_g)
    w_out_b, w_gu_b, w_down_b = rest_weights(att_out)
    x1 = _out_fwd(x2d, hg_out, att_out, mod, w_out_b)

    dx1, h2, act, dau, dff, ffn_sums, loss_part = _ffn(x1, target, mod, norm2_g, gf, w_gu_b, w_down_b)
    dw_gu = _weight_grad(dau, h2, "dw_gate_up", rounded=True)
    dw_down = _weight_grad(act, dff, "dw_down", rounded=True)

    dhg, dat, dw_out, dw_out_b, dgate1 = _out_bwd(dx1, hg_out, att_out, mod, w_out_b)
    att_g = att_onorm_g + stage("mixer_weights_done", (dw_out, dw_out_b), dw_gu, dw_down)
    comb = _att_combine_bwd(dat, att, lses, att_g)
    dos, ccs, d_att_g = comb[0:3], comb[3:6], comb[6]
    datt = []
    for i, d in enumerate(DILATIONS):
        datt.append(_att_bwd(aq, ak, av, dos[i], ccs[i], lses[i], d))
    dhq, dhf, dhi, dhgt, d_hg_g, d_lb = _hg_bwd(hq, hf, hi, hgt, hg_lb_logits, hg_onorm_g, hg_o, hg_states, dhg)
    dps = [dhq, dhf, dhi, dhgt] + [datt[i][j] for j in range(3) for i in range(3)]
    grad_x, dp_b, dshift1, dscale1, d_g1 = _in_bwd(x2d, dx1, mod, norm1_g, w_in_b, dps)
    dw_in, = _weight_grad(dp_b, h1, "dw_in")
    small = jnp.concatenate([dshift1, dscale1, dgate1, ffn_sums[0:1], ffn_sums[1:2], ffn_sums[2:3], d_g1, ffn_sums[3:4],
                             ffn_sums[4:5], d_att_g, d_lb, d_hg_g, loss_part], axis=1)
    return grad_x, dw_in, small
```

```python
import functools

import jax
import jax.numpy as jnp
from jax import lax
from jax.experimental import pallas as pl
from jax.experimental.pallas import tpu as pltpu

F32 = jnp.float32
BF16 = jnp.bfloat16
HIGHEST = lax.Precision.HIGHEST
MESH = pl.DeviceIdType.MESH

D_MODEL = 1024
N_DEV = 8
HG_HEADS = 4
HG_DIM = 128
HG_WIDTH = HG_HEADS * HG_DIM
HG_CHUNK = 128
ATT_WIDTH = 512
ATT_HEAD_DIM = 64
ATT_BLOCK = 128
DILATIONS = (1, 4, 16)
ATT_SCALE = ATT_HEAD_DIM ** -0.5
D_FF = 2816
IN_WIDTH = 7 * 512
RMS_EPS = 1e-6
NEG = -1e30

ADAM_LR = 0.001
ADAM_B1 = 0.9
ADAM_B2 = 0.999
ADAM_EPS = 1e-08
ADAM_WD = 0.01
ADAM_STEP = 10

V7X_VMEM_LIMIT = 56 * 1024 * 1024

SM_MOD = 0
SM_G1 = 6 * D_MODEL
SM_G2 = 7 * D_MODEL
SM_GF = 8 * D_MODEL
SM_ATT = 9 * D_MODEL
SM_LB = 9 * D_MODEL + 512
SM_HG = 10 * D_MODEL
SM_LOSS = 10 * D_MODEL + 128
SM_WIDTH = 10 * D_MODEL + 256
SM_PADDED = 88 * 128


def _params(*sem, vmem=V7X_VMEM_LIMIT):
    return pltpu.CompilerParams(dimension_semantics=sem, vmem_limit_bytes=vmem)


def _dot(a, b):
    return jnp.dot(a, b, preferred_element_type=F32)


def _dot_nt(a, b):
    return lax.dot_general(a, b, (((1,), (1,)), ((), ())), preferred_element_type=F32)


def _dot_tn(a, b):
    return lax.dot_general(a, b, (((0,), (0,)), ((), ())), preferred_element_type=F32)


def _dot_f32(a, b):
    return jnp.dot(a, b, preferred_element_type=F32, precision=HIGHEST)


def _sigmoid(x):
    return 1.0 / (1.0 + jnp.exp(-x))


def _silu(x):
    return x * _sigmoid(x)


def _dsilu(x):
    s = _sigmoid(x)
    return s * (1.0 + x * (1.0 - s))


def _rms(x):
    rstd = lax.rsqrt(jnp.mean(x * x, axis=-1, keepdims=True) + RMS_EPS)
    return x * rstd, rstd


def _rms_bwd(dn, xhat, rstd):
    return rstd * (dn - xhat * jnp.mean(dn * xhat, axis=-1, keepdims=True))


def _rowsum(x):
    return jnp.sum(x, axis=0, keepdims=True)


def _rows(tm, n):
    return pl.BlockSpec((tm, n), lambda i: (i, 0))


def _whole(shape):
    return pl.BlockSpec(shape, lambda i: (0,) * len(shape))


def _mesh_pos():
    return lax.axis_index("x"), lax.axis_index("y"), lax.axis_index("c")


def _flip(k):
    x, y, c = _mesh_pos()
    px = 1 - x if k & 4 else x
    py = 1 - y if k & 2 else y
    pc = 1 - c if k & 1 else c
    return (px, py, pc), 4 * px + 2 * py + pc


def _exchange_small(x, rows_per_peer, name):
    r_all, cols = x.shape
    r_out = r_all if rows_per_peer is None else rows_per_peer

    def body(x_ref, out_ref, send_sems, recv_sems):
        _, me = _flip(0)

        def src(pid):
            if rows_per_peer is None:
                return x_ref
            return x_ref.at[pl.ds(pl.multiple_of(pid * r_out, r_out), r_out), :]

        if rows_per_peer is None:
            out_ref[me] = x_ref[...]
        else:
            out_ref[me] = x_ref[pl.ds(pl.multiple_of(me * r_out, r_out), r_out), :]
        sends = []
        for k in range(1, N_DEV):
            dev, pid = _flip(k)
            cp = pltpu.make_async_remote_copy(src_ref=src(pid), dst_ref=out_ref.at[me], send_sem=send_sems.at[k - 1],
                                              recv_sem=recv_sems.at[k - 1], device_id=dev, device_id_type=MESH)
            cp.start()
            sends.append(cp)
        for k in range(1, N_DEV):
            dev, pid = _flip(k)
            pltpu.make_async_remote_copy(src_ref=src(pid), dst_ref=out_ref.at[pid], send_sem=send_sems.at[k - 1],
                                         recv_sem=recv_sems.at[k - 1], device_id=dev, device_id_type=MESH).wait_recv()
        for cp in sends:
            cp.wait_send()

    return pl.pallas_call(
        body, name=name,
        out_shape=jax.ShapeDtypeStruct((N_DEV, r_out, cols), x.dtype),
        in_specs=[pl.BlockSpec(memory_space=pltpu.VMEM)],
        out_specs=pl.BlockSpec(memory_space=pltpu.VMEM),
        scratch_shapes=[pltpu.SemaphoreType.DMA((N_DEV - 1,)), pltpu.SemaphoreType.DMA((N_DEV - 1,))],
    )(x)


def _gather_weights(shards):
    n = len(shards)

    def body(*refs):
        xs, outs = refs[:n], refs[n:2 * n]
        send_sems, recv_sems, local_sems = refs[2 * n:]
        x, y, c = _mesh_pos()
        me, sibling = (x, y, c), (x, y, 1 - c)
        chips = [(1 - x, y), (x, 1 - y), (1 - x, 1 - y)]

        def blk(a, px, py, pc):
            return outs[a].at[4 * px + 2 * py + pc]

        def copy(a, k, block, to, src=None):
            return pltpu.make_async_remote_copy(
                src_ref=blk(a, *block) if src is None else src, dst_ref=blk(a, *block),
                send_sem=send_sems.at[a * 7 + k], recv_sem=recv_sems.at[a * 7 + k], device_id=to, device_id_type=MESH)

        mine = [pltpu.make_async_copy(xs[a], blk(a, *me), local_sems.at[a]) for a in range(n)]
        for cp in mine:
            cp.start()
        first = []
        for a in range(n):
            first.append(copy(a, 0, me, sibling, src=xs[a]))
            first += [copy(a, 1 + j, me, (*chip, c), src=xs[a]) for j, chip in enumerate(chips)]
        for cp in first:
            cp.start()
        passed = []
        for j, chip in enumerate(chips):
            for a in range(n):
                copy(a, 1 + j, (*chip, c), me).wait_recv()
                cp = copy(a, 4 + j, (*chip, c), sibling)
                cp.start()
                passed.append(cp)
        for a in range(n):
            copy(a, 0, sibling, me).wait_recv()
            for j, chip in enumerate(chips):
                copy(a, 4 + j, (*chip, 1 - c), me).wait_recv()
        for cp in first + passed:
            cp.wait_send()
        for cp in mine:
            cp.wait()

    hbm = pl.BlockSpec(memory_space=pl.ANY)
    return pl.pallas_call(
        body, name="gather_weights",
        out_shape=[jax.ShapeDtypeStruct((N_DEV,) + s.shape, s.dtype) for s in shards],
        in_specs=[hbm] * n, out_specs=[hbm] * n,
        scratch_shapes=[pltpu.SemaphoreType.DMA((7 * n,)), pltpu.SemaphoreType.DMA((7 * n,)), pltpu.SemaphoreType.DMA((n,))],
    )(*shards)


_HBM = pl.BlockSpec(memory_space=pltpu.HBM)
_SEM = pl.BlockSpec(memory_space=pltpu.SEMAPHORE)
_DATAFLOW = pltpu.SideEffectType.DATAFLOW_SIDE_EFFECTING


def _copies_start(name, plan, n_copies, srcs, lands, after):
    bufs = list(srcs) + list(lands)
    nb = len(bufs)

    def body(*refs):
        ins, send_sems, recv_sems, token = refs[:nb], refs[nb + len(after)], refs[nb + len(after) + 1], refs[-1]
        for i, (src, dst, dev) in enumerate(plan(ins[:len(srcs)], ins[len(srcs):])):
            pltpu.make_async_remote_copy(src_ref=src, dst_ref=dst, send_sem=send_sems.at[i], recv_sem=recv_sems.at[i],
                                         device_id=dev, device_id_type=MESH).start()
        token[...] = jnp.zeros_like(token)

    outs = pl.pallas_call(
        body, name=name,
        out_shape=(pltpu.SemaphoreType.DMA((n_copies,)), pltpu.SemaphoreType.DMA((n_copies,)),
                   *[pltpu.HBM(b.shape, b.dtype) for b in bufs], jax.ShapeDtypeStruct((8, 128), F32)),
        in_specs=[_HBM] * nb + [pl.BlockSpec(memory_space=pl.ANY)] * len(after),
        out_specs=(_SEM, _SEM, *[_HBM] * nb, pl.BlockSpec(memory_space=pltpu.VMEM)),
        input_output_aliases={i: 2 + i for i in range(nb)},
        compiler_params=pltpu.CompilerParams(has_side_effects=_DATAFLOW),
    )(*[pltpu.with_memory_space_constraint(b, pltpu.HBM) for b in bufs], *after)
    return outs[0], outs[1], list(outs[2:2 + len(srcs)]), list(outs[2 + len(srcs):2 + nb]), outs[-1]


def _copies_wait(name, plan, send_sems, recv_sems, srcs, lands, after):
    bufs = list(srcs) + list(lands)
    nb = len(bufs)

    def body(*refs):
        ins, send_ref, recv_ref = refs[:nb], refs[nb], refs[nb + 1]
        for i, (src, dst, dev) in enumerate(plan(ins[:len(srcs)], ins[len(srcs):])):
            cp = pltpu.make_async_remote_copy(src_ref=src, dst_ref=dst, send_sem=send_ref.at[i], recv_sem=recv_ref.at[i],
                                              device_id=dev, device_id_type=MESH)
            cp.wait_send()
            cp.wait_recv()

    outs = pl.pallas_call(
        body, name=name, out_shape=[pltpu.HBM(b.shape, b.dtype) for b in bufs],
        in_specs=[_HBM] * nb + [_SEM, _SEM] + [pl.BlockSpec(memory_space=pl.ANY)] * len(after), out_specs=[_HBM] * nb,
        input_output_aliases={i: i for i in range(nb)},
        compiler_params=pltpu.CompilerParams(has_side_effects=_DATAFLOW),
    )(*bufs, send_sems, recv_sems, *after)
    return list(outs[:len(srcs)]), list(outs[len(srcs):])


def _plan_gather_own(srcs, lands):
    _, me = _flip(0)
    return [(srcs[a], lands[a].at[me], _flip(k)[0]) for a in range(len(srcs)) for k in (1, 4, 2, 6)]


def _plan_gather_pass(srcs, lands):
    sibling = _flip(1)[0]
    plan = []
    for land in lands:
        for k in (4, 2, 6):
            block = land.at[_flip(k)[1]]
            plan.append((block, block, sibling))
    return plan


def _plan_reduce_pairs(srcs, lands):
    x, y, c = _mesh_pos()
    return [(srcs[a].at[chip, 1 - c], lands[a].at[chip], (x, y, 1 - c)) for a in range(len(srcs)) for chip in range(4)]


def _plan_reduce_chips(srcs, lands):
    plan = []
    for a in range(len(srcs)):
        for j, k in enumerate((4, 2, 6)):
            dev = _flip(k)[0]
            plan.append((srcs[a].at[2 * dev[0] + dev[1]], lands[a].at[j], dev))
    return plan


def _plan_reduce_direct(srcs, lands):
    plan = []
    for a in range(len(srcs)):
        for k in range(1, N_DEV):
            dev = _flip(k)[0]
            plan.append((srcs[a].at[2 * dev[0] + dev[1], dev[2]], lands[a].at[k - 1], dev))
    return plan


def _shard_rows(r):
    return r // 2 if r % 32 == 0 else r


def _pair_sum(core, grads, got, name):
    _, _, r, c = grads.shape
    tr = _shard_rows(r)

    def body(core_ref, a_ref, b_ref, o_ref, ob_ref):
        s = a_ref[...] + b_ref[...]
        o_ref[...] = s
        ob_ref[...] = s.astype(BF16)

    spec = pl.BlockSpec((None, tr, c), lambda i, j, core_ref: (i, j, 0))
    return pl.pallas_call(
        body, name=name,
        grid_spec=pltpu.PrefetchScalarGridSpec(
            num_scalar_prefetch=1, grid=(4, r // tr),
            in_specs=[pl.BlockSpec((None, None, tr, c), lambda i, j, core_ref: (i, core_ref[0], j, 0)), spec],
            out_specs=[spec, spec]),
        out_shape=[jax.ShapeDtypeStruct((4, r, c), F32), jax.ShapeDtypeStruct((4, r, c), BF16)],
        compiler_params=_params("parallel", "parallel"),
    )(core, grads, got)


def _ada_rows(c_all, w_ada, b_ada):
    n_cols = w_ada.shape[1]

    def body(c_ref, w_ref, b_ref, o_ref):
        _, me = _flip(0)
        bias = b_ref[:, pl.ds(pl.multiple_of(me * n_cols, 128), n_cols)]
        o_ref[...] = _dot_f32(_silu(c_ref[...]), w_ref[...]) + bias

    return pl.pallas_call(
        body, name="ada_rows", out_shape=jax.ShapeDtypeStruct((N_DEV, n_cols), F32),
        in_specs=[pl.BlockSpec(memory_space=pltpu.VMEM)] * 3, out_specs=pl.BlockSpec(memory_space=pltpu.VMEM),
    )(c_all, w_ada, b_ada)


def _in_fwd(x, mod, g1, w_in):
    s = x.shape[0]
    tm = 256

    def body(x_ref, mod_ref, g_ref, w_ref, h_ref, *outs):
        xhat, _ = _rms(x_ref[...])
        h = (xhat * g_ref[...]) * (1.0 + mod_ref[:, D_MODEL:2 * D_MODEL]) + mod_ref[:, 0:D_MODEL]
        hb = h.astype(BF16)
        h_ref[...] = hb
        for j, o_ref in enumerate(outs):
            o_ref[...] = _dot_nt(hb, w_ref[j * 512:(j + 1) * 512, :])

    return pl.pallas_call(
        body, name="in_fwd", grid=(s // tm,),
        out_shape=[jax.ShapeDtypeStruct((s, D_MODEL), BF16)] + [jax.ShapeDtypeStruct((s, 512), F32)] * 7,
        in_specs=[_rows(tm, D_MODEL), _whole((1, 6 * D_MODEL)), _whole((1, D_MODEL)), _whole((IN_WIDTH, D_MODEL))],
        out_specs=[_rows(tm, D_MODEL)] + [_rows(tm, 512)] * 7,
        compiler_params=_params("parallel"),
    )(x, mod, g1, w_in)


def _in_bwd(x, dx1, mod, g1, w_in, dps):
    s = x.shape[0]
    tm = 256

    def body(x_ref, dx_ref, mod_ref, g_ref, w_ref, *rest):
        dp_refs, (gx_ref, dpb_ref, dsh_ref, dsc_ref, dg_ref) = rest[:13], rest[13:]
        pieces = [dp_refs[j][...] for j in range(4)]
        pieces += [dp_refs[4 + 3 * j][...] + dp_refs[5 + 3 * j][...] + dp_refs[6 + 3 * j][...] for j in range(3)]
        for j, p in enumerate(pieces):
            dpb_ref[:, j * 512:(j + 1) * 512] = p.astype(BF16)
        dh = _dot(dpb_ref[...], w_ref[...])
        xhat, rstd = _rms(x_ref[...])
        g = g_ref[...]
        scale1 = 1.0 + mod_ref[:, D_MODEL:2 * D_MODEL]
        n1 = xhat * g

        @pl.when(pl.program_id(0) == 0)
        def _():
            dsh_ref[...] = jnp.zeros_like(dsh_ref)
            dsc_ref[...] = jnp.zeros_like(dsc_ref)
            dg_ref[...] = jnp.zeros_like(dg_ref)

        dsh_ref[...] += _rowsum(dh)
        dsc_ref[...] += _rowsum(dh * n1)
        dn = dh * scale1
        dg_ref[...] += _rowsum(dn * xhat)
        gx_ref[...] = dx_ref[...] + _rms_bwd(dn * g, xhat, rstd)

    vec = _whole((1, D_MODEL))
    return pl.pallas_call(
        body, name="in_bwd", grid=(s // tm,),
        out_shape=[jax.ShapeDtypeStruct((s, D_MODEL), F32), jax.ShapeDtypeStruct((s, IN_WIDTH), BF16)]
        + [jax.ShapeDtypeStruct((1, D_MODEL), F32)] * 3,
        in_specs=[_rows(tm, D_MODEL), _rows(tm, D_MODEL), _whole((1, 6 * D_MODEL)), vec, _whole((IN_WIDTH, D_MODEL))]
        + [_rows(tm, 512)] * 13,
        out_specs=[_rows(tm, D_MODEL), _rows(tm, IN_WIDTH), vec, vec, vec],
        compiler_params=_params("arbitrary"),
    )(x, dx1, mod, g1, w_in, *dps)


HG_TILE = 512
HG_TILE_CHUNKS = HG_TILE // HG_CHUNK


def _lower_bound(lg_ref):
    return 1.0 / (1.0 + jnp.exp(lg_ref[1:2, :] - lg_ref[0:1, :]))


def _chunk_masks():
    r = lax.broadcasted_iota(jnp.int32, (HG_CHUNK, HG_CHUNK), 0)
    c = lax.broadcasted_iota(jnp.int32, (HG_CHUNK, HG_CHUNK), 1)
    return r >= c, c >= r, (r >= c).astype(F32), (c >= r).astype(F32)


def _hg_fwd(hq, hf, hi, hgt, logits, onorm_g):
    s = hq.shape[0]
    n_tiles = s // HG_TILE

    def body(q_ref, f_ref, i_ref, g_ref, lg_ref, og_ref, out_ref, o_ref, st_ref, state, qf_s, kk_s, lf_s):
        @pl.when(pl.program_id(0) == 0)
        def _():
            state[...] = jnp.zeros_like(state)

        lb = _lower_bound(lg_ref)
        f = lb + (1.0 - lb) * _sigmoid(f_ref[...])
        kk_s[...] = 1.0 - f
        lf_s[...] = jnp.log(f)
        qf_s[...] = _silu(q_ref[...])
        causal, _, tri, _ = _chunk_masks()

        def chunk(ci, carry):
            rows = pl.ds(pl.multiple_of(ci * HG_CHUNK, HG_CHUNK), HG_CHUNK)
            srows = pl.ds(pl.multiple_of(ci * HG_DIM, HG_DIM), HG_DIM)
            lf = lf_s[rows, :]
            b = _dot_f32(tri, lf)
            bl = _rowsum(lf)
            ref = 0.5 * bl
            qf, kk, v = qf_s[rows, :], kk_s[rows, :], i_ref[rows, :]
            a_in = (qf * jnp.exp(b)).astype(BF16)
            a_t = (qf * jnp.exp(b - ref)).astype(BF16)
            b_t = (kk * jnp.exp(ref - b)).astype(BF16)
            kd = kk * jnp.exp(bl - b)
            ebl = jnp.exp(bl)
            vb = v.astype(BF16)
            for h in range(HG_HEADS):
                c = slice(h * HG_DIM, (h + 1) * HG_DIM)
                st = state[h]
                st_ref[srows, c] = st
                p = jnp.where(causal, _dot_nt(a_t[:, c], b_t[:, c]), 0.0)
                o_ref[rows, c] = _dot(p.astype(BF16), vb[:, c]) + _dot_nt(a_in[:, c], st.astype(BF16))
                state[h] = st * ebl[:, c] + _dot_tn(vb[:, c], kd[:, c].astype(BF16))
            return carry

        lax.fori_loop(0, HG_TILE_CHUNKS, chunk, 0, unroll=True)
        for h in range(HG_HEADS):
            c = slice(h * HG_DIM, (h + 1) * HG_DIM)
            ohat, _ = _rms(o_ref[:, c])
            out_ref[:, c] = (ohat * og_ref[...] * _silu(g_ref[:, c])).astype(BF16)

    tile = _rows(HG_TILE, HG_WIDTH)
    return pl.pallas_call(
        body, name="hg_fwd", grid=(n_tiles,),
        out_shape=[jax.ShapeDtypeStruct((s, HG_WIDTH), BF16), jax.ShapeDtypeStruct((s, HG_WIDTH), F32),
                   jax.ShapeDtypeStruct((s // HG_CHUNK * HG_DIM, HG_WIDTH), F32)],
        in_specs=[tile] * 4 + [_whole((2, HG_WIDTH)), _whole((1, HG_DIM))],
        out_specs=[tile, tile, _rows(HG_TILE_CHUNKS * HG_DIM, HG_WIDTH)],
        scratch_shapes=[pltpu.VMEM((HG_HEADS, HG_DIM, HG_DIM), F32)] + [pltpu.VMEM((HG_TILE, HG_WIDTH), F32)] * 3,
        compiler_params=_params("arbitrary"),
    )(hq, hf, hi, hgt, logits, onorm_g)


def _hg_bwd(hq, hf, hi, hgt, logits, onorm_g, o, states, dout):
    s = hq.shape[0]
    n_tiles = s // HG_TILE

    def body(q_ref, f_ref, i_ref, g_ref, lg_ref, og_ref, o_ref, st_ref, d_ref,
             dq_ref, df_ref, di_ref, dg_ref, dog_ref, dlb_ref, dstate, qf_s, kk_s, lf_s, do_s):
        @pl.when(pl.program_id(0) == 0)
        def _():
            dstate[...] = jnp.zeros_like(dstate)
            dog_ref[...] = jnp.zeros_like(dog_ref)
            dlb_ref[...] = jnp.zeros_like(dlb_ref)

        og = og_ref[...]
        dog = jnp.zeros((1, HG_DIM), F32)
        for h in range(HG_HEADS):
            c = slice(h * HG_DIM, (h + 1) * HG_DIM)
            ohat, rstd = _rms(o_ref[:, c])
            gate = g_ref[:, c]
            d = d_ref[:, c]
            dg_ref[:, c] = (d * (ohat * og) * _dsilu(gate)).astype(BF16)
            dnormed = d * _silu(gate)
            dog += _rowsum(dnormed * ohat)
            do_s[:, c] = _rms_bwd(dnormed * og, ohat, rstd)
        dog_ref[...] += dog

        lb = _lower_bound(lg_ref)
        f = lb + (1.0 - lb) * _sigmoid(f_ref[...])
        kk_s[...] = 1.0 - f
        lf_s[...] = jnp.log(f)
        qf_s[...] = _silu(q_ref[...])
        causal, upper, tri, tri_t = _chunk_masks()

        def chunk(step, carry):
            ci = HG_TILE_CHUNKS - 1 - step
            rows = pl.ds(pl.multiple_of(ci * HG_CHUNK, HG_CHUNK), HG_CHUNK)
            srows = pl.ds(pl.multiple_of(ci * HG_DIM, HG_DIM), HG_DIM)
            lf = lf_s[rows, :]
            b = _dot_f32(tri, lf)
            bl = _rowsum(lf)
            ref = 0.5 * bl
            qf, kk, v, do = qf_s[rows, :], kk_s[rows, :], i_ref[rows, :], do_s[rows, :]
            eb, ebr, erb, ekd, ebl = jnp.exp(b), jnp.exp(b - ref), jnp.exp(ref - b), jnp.exp(bl - b), jnp.exp(bl)
            a_in, a_t, b_t, kd = qf * eb, qf * ebr, kk * erb, kk * ekd
            for h in range(HG_HEADS):
                c = slice(h * HG_DIM, (h + 1) * HG_DIM)
                st, dst = st_ref[srows, c], dstate[h]
                stb, dstb = st.astype(BF16), dst.astype(BF16)
                doh, vh = do[:, c], v[:, c]
                dob, vb = doh.astype(BF16), vh.astype(BF16)
                ain_h, at_h, bt_h, kd_h = a_in[:, c], a_t[:, c], b_t[:, c], kd[:, c]
                atb, btb = at_h.astype(BF16), bt_h.astype(BF16)
                d_ain = _dot(dob, stb)
                p_t = jnp.where(upper, _dot_nt(btb, atb), 0.0).astype(BF16)
                dp = jnp.where(causal, _dot_nt(dob, vb), 0.0).astype(BF16)
                dp_t = jnp.where(upper, _dot_nt(vb, dob), 0.0).astype(BF16)
                di_ref[rows, c] = (_dot(p_t, dob) + _dot_nt(kd_h.astype(BF16), dstb)).astype(BF16)
                d_at = _dot(dp, btb)
                d_bt = _dot(dp_t, atb)
                d_kd = _dot(vb, dstb)
                dqf = d_ain * eb[:, c] + d_at * ebr[:, c]
                dkk = d_bt * erb[:, c] + d_kd * ekd[:, c]
                db = d_ain * ain_h + d_at * atb.astype(F32) - d_bt * btb.astype(F32) - d_kd * kd_h
                dbl = _rowsum(d_kd * kd_h) + _rowsum(dst * st) * ebl[:, c]
                dstate[h] = _dot_tn(dob, ain_h.astype(BF16)) + dst * ebl[:, c]
                dlf = _dot_f32(tri_t, db) + dbl
                qv, fr = q_ref[rows, c], f_ref[rows, c]
                lbh = lb[:, c]
                sg = _sigmoid(fr)
                dfv = dlf / (lbh + (1.0 - lbh) * sg) - dkk
                df_ref[rows, c] = (dfv * (1.0 - lbh) * sg * (1.0 - sg)).astype(BF16)
                dlb_ref[:, c] += _rowsum(dfv * (1.0 - sg))
                dq_ref[rows, c] = (dqf * _dsilu(qv)).astype(BF16)
            return carry

        lax.fori_loop(0, HG_TILE_CHUNKS, chunk, 0, unroll=True)

    rev = pl.BlockSpec((HG_TILE, HG_WIDTH), lambda i: (n_tiles - 1 - i, 0))
    return pl.pallas_call(
        body, name="hg_bwd", grid=(n_tiles,),
        out_shape=[jax.ShapeDtypeStruct((s, HG_WIDTH), BF16)] * 4
        + [jax.ShapeDtypeStruct((1, HG_DIM), F32), jax.ShapeDtypeStruct((1, HG_WIDTH), F32)],
        in_specs=[rev] * 4 + [_whole((2, HG_WIDTH)), _whole((1, HG_DIM)), rev,
                              pl.BlockSpec((HG_TILE_CHUNKS * HG_DIM, HG_WIDTH), lambda i: (n_tiles - 1 - i, 0)), rev],
        out_specs=[rev] * 4 + [_whole((1, HG_DIM)), _whole((1, HG_WIDTH))],
        scratch_shapes=[pltpu.VMEM((HG_HEADS, HG_DIM, HG_DIM), F32)] + [pltpu.VMEM((HG_TILE, HG_WIDTH), F32)] * 4,
        compiler_params=_params("arbitrary"),
    )(hq, hf, hi, hgt, logits, onorm_g, o, states, dout)


TOKEN_GROUP = 16


def _att_geometry(dil, seq=0):
    per_group = TOKEN_GROUP // dil
    ub = ATT_BLOCK // per_group
    if dil == TOKEN_GROUP:
        n_blocks = 2 if seq % (2 * ub * TOKEN_GROUP) == 0 and seq > 0 else 1
    else:
        n_blocks = 4
    return per_group, ub, ATT_WIDTH if dil == 1 else 128, n_blocks


def _att_consts(dil):
    per_group, ub = _att_geometry(dil)[:2]

    def pos(i):
        return i if dil == 1 else (i % ub) * per_group + i // ub

    lane = lax.broadcasted_iota(jnp.int32, (ATT_BLOCK, 128), 1)
    qi = pos(lax.broadcasted_iota(jnp.int32, (2 * ATT_BLOCK, ATT_BLOCK), 0) % ATT_BLOCK)
    kj = pos(lax.broadcasted_iota(jnp.int32, (2 * ATT_BLOCK, ATT_BLOCK), 1))
    return lane < ATT_HEAD_DIM, kj <= qi, lambda off: kj >= qi + off


def _load_tile(ref, dil, r, c, base=0):
    per_group, ub = _att_geometry(dil)[:2]
    if dil == 1:
        return ref[base:base + ATT_BLOCK, c]
    return jnp.concatenate([ref[pl.ds(base + dil * w + r, ub, stride=TOKEN_GROUP), c] for w in range(per_group)], axis=0)


def _store_tile(ref, dil, r, c, val, base=0):
    per_group, ub = _att_geometry(dil)[:2]
    if dil == 1:
        ref[base:base + ATT_BLOCK, c] = val
        return
    for w in range(per_group):
        ref[pl.ds(base + dil * w + r, ub, stride=TOKEN_GROUP), c] = val[w * ub:(w + 1) * ub]


def _stack_heads(x2, first):
    return jnp.concatenate([jnp.where(first, x2, 0.0), jnp.where(first, 0.0, x2)], axis=0)


def _stack_bcast(x2, first):
    other = pltpu.roll(x2, ATT_HEAD_DIM, axis=1)
    return jnp.concatenate([jnp.where(first, x2, other), jnp.where(first, other, x2)], axis=0)


def _unstack_heads(st, first):
    return jnp.where(first, st[:ATT_BLOCK], st[ATT_BLOCK:])


def _att_fwd(q, k, v, dil):
    seq, width = q.shape
    _, ub, lanes, nbs = _att_geometry(dil, seq)
    rows = ub * TOKEN_GROUP
    n_steps = seq // (nbs * rows)

    def body(q_ref, k_ref, v_ref, kp_ref, vp_ref, o_ref, lse_ref):
        first, cur_ok, _band = _att_consts(dil)
        inner_ok = _band(0)
        edge_ok = _band(jnp.where(pl.program_id(0) > 0, 0, ATT_BLOCK))
        for r in range(dil):
            for j in range(lanes // 128):
                c = slice(j * 128, (j + 1) * 128)
                kc = vc = None
                for b in range(nbs):
                    base = b * rows
                    prev_ok = edge_ok if b == 0 else inner_ok
                    if b == 0:
                        kp, vp = _load_tile(kp_ref, dil, r, c).astype(BF16), _load_tile(vp_ref, dil, r, c).astype(BF16)
                    else:
                        kp, vp = kc, vc
                    qst = _stack_heads(_load_tile(q_ref, dil, r, c, base) * ATT_SCALE, first).astype(BF16)
                    kc = _load_tile(k_ref, dil, r, c, base).astype(BF16)
                    vc = _load_tile(v_ref, dil, r, c, base).astype(BF16)
                    sc = jnp.where(cur_ok, _dot_nt(qst, kc), NEG)
                    sp = jnp.where(prev_ok, _dot_nt(qst, kp), NEG)
                    mx = jnp.max(jnp.maximum(sc, sp), axis=-1, keepdims=True)
                    pc, pp = jnp.exp(sc - mx), jnp.exp(sp - mx)
                    den = jnp.sum(pc + pp, axis=-1, keepdims=True)
                    ost = (_dot(pc.astype(BF16), vc) + _dot(pp.astype(BF16), vp)) / den
                    lse = jnp.broadcast_to(mx + jnp.log(den), (2 * ATT_BLOCK, 128))
                    _store_tile(o_ref, dil, r, c, _unstack_heads(ost, first), base)
                    _store_tile(lse_ref, dil, r, c, _unstack_heads(lse, first), base)

    slab = pl.BlockSpec((nbs * rows, lanes), lambda n, j: (n, j))
    before = pl.BlockSpec((rows, lanes), lambda n, j: (jnp.maximum(n * nbs - 1, 0), j))
    return pl.pallas_call(
        body, name=f"att_fwd_d{dil}", grid=(n_steps, width // lanes),
        out_shape=[jax.ShapeDtypeStruct((seq, width), F32)] * 2,
        in_specs=[slab, slab, slab, before, before], out_specs=[slab, slab],
        compiler_params=_params("arbitrary", "arbitrary"),
    )(q, k, v, k, v)


def _att_bwd(q, k, v, do, cc, lse, dil):
    seq, width = q.shape
    _, ub, lanes, nbs = _att_geometry(dil, seq)
    rows = ub * TOKEN_GROUP
    n_blocks = seq // rows
    n_steps = n_blocks // nbs

    def body(q_ref, k_ref, v_ref, do_ref, cc_ref, lse_ref, qx_ref, dox_ref, ccx_ref, lsex_ref,
             dq_ref, dk_ref, dv_ref, carry):
        first, cur_ok, _band = _att_consts(dil)
        step = pl.program_id(1)
        inner_ok = _band(0)
        edge_ok = _band(jnp.where(step < n_steps - 1, 0, ATT_BLOCK))

        @pl.when(step == 0)
        def _():
            carry[...] = jnp.zeros_like(carry)

        def queries(refs, r, c, base):
            q_r, do_r, lse_r, cc_r = refs
            return (_stack_heads(_load_tile(q_r, dil, r, c, base) * ATT_SCALE, first).astype(BF16),
                    _stack_heads(_load_tile(do_r, dil, r, c, base), first).astype(BF16),
                    _stack_bcast(_load_tile(lse_r, dil, r, c, base), first),
                    _stack_bcast(_load_tile(cc_r, dil, r, c, base), first))

        for r in range(dil):
            for j in range(lanes // 128):
                c = slice(j * 128, (j + 1) * 128)
                own = queries((q_ref, do_ref, lse_ref, cc_ref), r, c, 0)
                left = _load_tile(carry, dil, r, c)
                for b in range(nbs):
                    base = b * rows
                    last = b == nbs - 1
                    next_ok = edge_ok if last else inner_ok
                    if last:
                        following = queries((qx_ref, dox_ref, lsex_ref, ccx_ref), r, c, 0)
                    else:
                        following = queries((q_ref, do_ref, lse_ref, cc_ref), r, c, base + rows)
                    (qst, dost, lse_n, cc_n), (qxst, doxst, lse_x, cc_x) = own, following
                    kb = _load_tile(k_ref, dil, r, c, base).astype(BF16)
                    vb = _load_tile(v_ref, dil, r, c, base).astype(BF16)
                    p_cur = jnp.exp(jnp.where(cur_ok, _dot_nt(qst, kb), NEG) - lse_n)
                    p_next = jnp.exp(jnp.where(next_ok, _dot_nt(qxst, kb), NEG) - lse_x)
                    ds_cur = (p_cur * (_dot_nt(dost, vb) + cc_n)).astype(BF16)
                    ds_next = (p_next * (_dot_nt(doxst, vb) + cc_x)).astype(BF16)
                    dq_own = left + _unstack_heads(_dot(ds_cur, kb), first)
                    _store_tile(dq_ref, dil, r, c, dq_own * ATT_SCALE, base)
                    _store_tile(dk_ref, dil, r, c, _dot_tn(ds_cur, qst) + _dot_tn(ds_next, qxst), base)
                    _store_tile(dv_ref, dil, r, c, _dot_tn(p_cur.astype(BF16), dost) + _dot_tn(p_next.astype(BF16), doxst), base)
                    left = _unstack_heads(_dot(ds_next, kb), first)
                    own = following
                _store_tile(carry, dil, r, c, left)

    slab = pl.BlockSpec((nbs * rows, lanes), lambda j, n: (n, j))
    after = pl.BlockSpec((rows, lanes), lambda j, n: (jnp.minimum((n + 1) * nbs, n_blocks - 1), j))
    return pl.pallas_call(
        body, name=f"att_bwd_d{dil}", grid=(width // lanes, n_steps),
        out_shape=[jax.ShapeDtypeStruct((seq, width), F32)] * 3,
        in_specs=[slab] * 6 + [after] * 4, out_specs=[slab] * 3,
        scratch_shapes=[pltpu.VMEM((rows, lanes), F32)],
        compiler_params=_params("arbitrary", "arbitrary"),
    )(q, k, v, do, cc, lse, q, do, cc, lse)


def _branch_weights(lses):
    mx = jnp.maximum(jnp.maximum(lses[0], lses[1]), lses[2])
    es = [jnp.exp(l - mx) for l in lses]
    inv = 1.0 / (es[0] + es[1] + es[2])
    return [e * inv for e in es]


def _att_combine(outs, lses, att_g):
    s = outs[0].shape[0]
    tm = 512

    def body(o0, o1, o2, l0, l1, l2, g_ref, att_ref, out_ref):
        ws = _branch_weights([l0[...], l1[...], l2[...]])
        att = ws[0] * o0[...] + ws[1] * o1[...] + ws[2] * o2[...]
        att_ref[...] = att
        ahat, _ = _rms(att)
        out_ref[...] = (ahat * g_ref[...]).astype(BF16)

    tile = _rows(tm, ATT_WIDTH)
    return pl.pallas_call(
        body, name="att_combine", grid=(s // tm,),
        out_shape=[jax.ShapeDtypeStruct((s, ATT_WIDTH), F32), jax.ShapeDtypeStruct((s, ATT_WIDTH), BF16)],
        in_specs=[tile] * 6 + [_whole((1, ATT_WIDTH))], out_specs=[tile, tile],
        compiler_params=_params("parallel"),
    )(*outs, *lses, att_g)


def _att_combine_bwd(datt_out, att, lses, att_g):
    s = att.shape[0]
    tm = 256

    def body(d_ref, att_ref, l0, l1, l2, g_ref, do0, do1, do2, cc0, cc1, cc2, dg_ref):
        @pl.when(pl.program_id(0) == 0)
        def _():
            dg_ref[...] = jnp.zeros_like(dg_ref)

        att = att_ref[...]
        ahat, rstd = _rms(att)
        d = d_ref[...]
        dg_ref[...] += _rowsum(d * ahat)
        datt = _rms_bwd(d * g_ref[...], ahat, rstd)
        hi = lax.broadcasted_iota(jnp.int32, (ATT_WIDTH, ATT_WIDTH), 0) // ATT_HEAD_DIM
        hj = lax.broadcasted_iota(jnp.int32, (ATT_WIDTH, ATT_WIDTH), 1) // ATT_HEAD_DIM
        same_head = (hi == hj).astype(BF16)
        prod = datt * att
        prod_hi = prod.astype(BF16)
        prod_lo = (prod - prod_hi.astype(F32)).astype(BF16)
        head_sum = _dot(prod_hi, same_head) + _dot(prod_lo, same_head)
        ws = _branch_weights([l0[...], l1[...], l2[...]])
        for w, do_ref, cc_ref in zip(ws, (do0, do1, do2), (cc0, cc1, cc2)):
            do_ref[...] = w * datt
            cc_ref[...] = -w * head_sum

    tile = _rows(tm, ATT_WIDTH)
    return pl.pallas_call(
        body, name="att_combine_bwd", grid=(s // tm,),
        out_shape=[jax.ShapeDtypeStruct((s, ATT_WIDTH), F32)] * 6 + [jax.ShapeDtypeStruct((1, ATT_WIDTH), F32)],
        in_specs=[tile] * 5 + [_whole((1, ATT_WIDTH))], out_specs=[tile] * 6 + [_whole((1, ATT_WIDTH))],
        compiler_params=_params("arbitrary"),
    )(datt_out, att, *lses, att_g)


def _out_fwd(x, hg, at, mod, w_out):
    s = x.shape[0]
    tm = 512

    def body(x_ref, hg_ref, at_ref, mod_ref, w_ref, x1_ref):
        mix = _dot(hg_ref[...], w_ref[0:512, :]) + _dot(at_ref[...], w_ref[512:1024, :])
        x1_ref[...] = x_ref[...] + mod_ref[:, 2 * D_MODEL:3 * D_MODEL] * mix

    return pl.pallas_call(
        body, name="out_fwd", grid=(s // tm,), out_shape=jax.ShapeDtypeStruct((s, D_MODEL), F32),
        in_specs=[_rows(tm, D_MODEL), _rows(tm, 512), _rows(tm, 512), _whole((1, 6 * D_MODEL)), _whole((D_MODEL, D_MODEL))],
        out_specs=_rows(tm, D_MODEL), compiler_params=_params("parallel"),
    )(x, hg, at, mod, w_out)


def _out_bwd(dx1, hg, at, mod, w_out):
    s = dx1.shape[0]
    tm = 512
    n_steps = s // tm

    def body(dx_ref, hg_ref, at_ref, mod_ref, w_ref, dhg_ref, dat_ref, dw_ref, dwb_ref, dgate_ref):
        @pl.when(pl.program_id(0) == 0)
        def _():
            dw_ref[...] = jnp.zeros_like(dw_ref)
            dgate_ref[...] = jnp.zeros_like(dgate_ref)

        hg, at, dx = hg_ref[...], at_ref[...], dx_ref[...]
        mix = _dot(hg, w_ref[0:512, :]) + _dot(at, w_ref[512:1024, :])
        dgate_ref[...] += _rowsum(dx * mix)
        dmix = (mod_ref[:, 2 * D_MODEL:3 * D_MODEL] * dx).astype(BF16)
        dhg_ref[...] = _dot_nt(dmix, w_ref[0:512, :])
        dat_ref[...] = _dot_nt(dmix, w_ref[512:1024, :])
        dw_ref[0:512, :] += _dot_tn(hg, dmix)
        dw_ref[512:1024, :] += _dot_tn(at, dmix)

        @pl.when(pl.program_id(0) == n_steps - 1)
        def _():
            dwb_ref[...] = dw_ref[...].astype(BF16)

    return pl.pallas_call(
        body, name="out_bwd", grid=(n_steps,),
        out_shape=[jax.ShapeDtypeStruct((s, 512), F32)] * 2
        + [jax.ShapeDtypeStruct((D_MODEL, D_MODEL), F32), jax.ShapeDtypeStruct((D_MODEL, D_MODEL), BF16),
           jax.ShapeDtypeStruct((1, D_MODEL), F32)],
        in_specs=[_rows(tm, D_MODEL), _rows(tm, 512), _rows(tm, 512), _whole((1, 6 * D_MODEL)), _whole((D_MODEL, D_MODEL))],
        out_specs=[_rows(tm, 512), _rows(tm, 512), _whole((D_MODEL, D_MODEL)), _whole((D_MODEL, D_MODEL)), _whole((1, D_MODEL))],
        compiler_params=_params("arbitrary"),
    )(dx1, hg, at, mod, w_out)


def _ffn(x1, target, mod, g2, gf, w_gu, w_down):
    s = x1.shape[0]
    tm = 256

    def body(x_ref, t_ref, mod_ref, g2_ref, gf_ref, wgu_hbm, wd_hbm,
             dx_ref, h2_ref, act_ref, dau_ref, dff_ref, sums_ref, loss_ref, wgu, wd, au_s, sem):
        @pl.when(pl.program_id(0) == 0)
        def _():
            c1 = pltpu.make_async_copy(wgu_hbm, wgu, sem.at[0])
            c2 = pltpu.make_async_copy(wd_hbm, wd, sem.at[1])
            c1.start()
            c2.start()
            c1.wait()
            c2.wait()
            sums_ref[...] = jnp.zeros_like(sums_ref)
            loss_ref[...] = jnp.zeros_like(loss_ref)

        x1v = x_ref[...]
        xhat, rstd = _rms(x1v)
        g2 = g2_ref[...]
        n2 = xhat * g2
        scale2 = 1.0 + mod_ref[:, 4 * D_MODEL:5 * D_MODEL]
        gate2 = mod_ref[:, 5 * D_MODEL:6 * D_MODEL]
        hb = (n2 * scale2 + mod_ref[:, 3 * D_MODEL:4 * D_MODEL]).astype(BF16)
        h2_ref[...] = hb
        au_s[...] = _dot_nt(hb, wgu[...])
        a = au_s[:, 0:D_FF]
        act = (_silu(a) * au_s[:, D_FF:2 * D_FF]).astype(BF16)
        act_ref[...] = act
        ff = _dot(act, wd[...])
        x2 = x1v + gate2 * ff
        nf, rstd_f = _rms(x2)
        gfv = gf_ref[...]
        err = nf * gfv - t_ref[...]
        loss_ref[...] += 0.5 * jnp.sum(_rowsum(err * err), axis=-1, keepdims=True) * (1.0 / D_MODEL)
        dy = err * (1.0 / D_MODEL)
        dx2 = _rms_bwd(dy * gfv, nf, rstd_f)
        dffb = (gate2 * dx2).astype(BF16)
        dff_ref[...] = dffb
        dact = _dot_nt(dffb, wd[...])
        a = au_s[:, 0:D_FF]
        dau_ref[:, 0:D_FF] = (dact * au_s[:, D_FF:2 * D_FF] * _dsilu(a)).astype(BF16)
        dau_ref[:, D_FF:2 * D_FF] = (dact * _silu(a)).astype(BF16)
        dh = _dot(dau_ref[...], wgu[...])
        dn = dh * scale2
        sums_ref[0:1, :] += _rowsum(dh)
        sums_ref[1:2, :] += _rowsum(dh * n2)
        sums_ref[2:3, :] += _rowsum(dx2 * ff)
        sums_ref[3:4, :] += _rowsum(dn * xhat)
        sums_ref[4:5, :] += _rowsum(dy * nf)
        dx_ref[...] = dx2 + _rms_bwd(dn * g2, xhat, rstd)

    vec = _whole((1, D_MODEL))
    hbm = pl.BlockSpec(memory_space=pl.ANY)
    return pl.pallas_call(
        body, name="ffn", grid=(s // tm,),
        out_shape=[jax.ShapeDtypeStruct((s, D_MODEL), F32), jax.ShapeDtypeStruct((s, D_MODEL), BF16),
                   jax.ShapeDtypeStruct((s, D_FF), BF16), jax.ShapeDtypeStruct((s, 2 * D_FF), BF16),
                   jax.ShapeDtypeStruct((s, D_MODEL), BF16), jax.ShapeDtypeStruct((8, D_MODEL), F32),
                   jax.ShapeDtypeStruct((1, 128), F32)],
        in_specs=[_rows(tm, D_MODEL), _rows(tm, D_MODEL), _whole((1, 6 * D_MODEL)), vec, vec, hbm, hbm],
        out_specs=[_rows(tm, D_MODEL), _rows(tm, D_MODEL), _rows(tm, D_FF), _rows(tm, 2 * D_FF), _rows(tm, D_MODEL),
                   _whole((8, D_MODEL)), _whole((1, 128))],
        scratch_shapes=[pltpu.VMEM((2 * D_FF, D_MODEL), BF16), pltpu.VMEM((D_FF, D_MODEL), BF16),
                        pltpu.VMEM((tm, 2 * D_FF), F32), pltpu.SemaphoreType.DMA((2,))],
        compiler_params=_params("arbitrary"),
    )(x1, target, mod, g2, gf, w_gu, w_down)


def _weight_grad(a, b, name, rounded=False):
    s, m = a.shape
    n = b.shape[1]
    ts = min(s, 2048)
    n_steps = s // ts
    tm = max(t for t in range(128, m + 1, 128) if m % t == 0 and t * n * 4 <= 6 * 1024 * 1024)

    def body(a_ref, b_ref, o_ref, *ob_ref):
        @pl.when(pl.program_id(1) == 0)
        def _():
            o_ref[...] = jnp.zeros_like(o_ref)

        o_ref[...] += _dot_tn(a_ref[...], b_ref[...])
        if rounded:
            @pl.when(pl.program_id(1) == n_steps - 1)
            def _():
                ob_ref[0][...] = o_ref[...].astype(BF16)

    tile = pl.BlockSpec((tm, n), lambda j, i: (j, 0))
    return pl.pallas_call(
        body, name=name, grid=(m // tm, n_steps),
        out_shape=[jax.ShapeDtypeStruct((m, n), F32)] + [jax.ShapeDtypeStruct((m, n), BF16)] * rounded,
        in_specs=[pl.BlockSpec((ts, tm), lambda j, i: (i, j)), pl.BlockSpec((ts, n), lambda j, i: (i, 0))],
        out_specs=[tile] + [tile] * rounded,
        compiler_params=_params("parallel", "arbitrary"),
    )(a, b)


def _adamw_math(w, g, m, v):
    m = ADAM_B1 * m + (1.0 - ADAM_B1) * g
    v = ADAM_B2 * v + (1.0 - ADAM_B2) * (g * g)
    m_hat = m / (1.0 - ADAM_B1 ** ADAM_STEP)
    v_hat = v / (1.0 - ADAM_B2 ** ADAM_STEP)
    delta = -ADAM_LR * (m_hat / (jnp.sqrt(v_hat) + ADAM_EPS) + ADAM_WD * w)
    return delta, m, v


def _adamw_shard(where, w, m, v, partial, got, name):
    r, c = w.shape
    tr = _shard_rows(r)
    lead = partial.ndim - 2
    n_got = got.shape[0]

    def body(where_ref, w_ref, m_ref, v_ref, own_ref, *rest):
        got_refs, (grad_ref, d_ref, nm_ref, nv_ref) = rest[:n_got], rest[n_got:]
        g = own_ref[...]
        for g_ref in got_refs:
            g = g + g_ref[...].astype(F32)
        grad_ref[...] = g
        d_ref[...], nm_ref[...], nv_ref[...] = _adamw_math(w_ref[...], g, m_ref[...], v_ref[...])

    tile = pl.BlockSpec((tr, c), lambda i, where_ref: (i, 0))
    own = pl.BlockSpec((None,) * lead + (tr, c), lambda i, where_ref: (*[where_ref[d] for d in range(lead)], i, 0))
    part = [pl.BlockSpec((None, tr, c), functools.partial(lambda j, i, where_ref: (j, i, 0), j)) for j in range(n_got)]
    return pl.pallas_call(
        body, name=name,
        grid_spec=pltpu.PrefetchScalarGridSpec(num_scalar_prefetch=1, grid=(r // tr,), in_specs=[tile] * 3 + [own] + part,
                                               out_specs=[tile] * 4),
        out_shape=[jax.ShapeDtypeStruct((r, c), F32)] * 4, compiler_params=_params("parallel"),
    )(where, w, m, v, partial, *[got] * n_got)


def _small_update(small_all, dmod_blocks, c_all, logits, w_ada, m_ada, v_ada, smalls):
    def body(sm_ref, dm_ref, c_ref, lg_ref, wa_ref, ma_ref, va_ref, *rest):
        ins, outs = rest[:21], rest[21:]
        _, me = _flip(0)
        tot = sm_ref[0:1, :]
        for i in range(1, N_DEV):
            tot = tot + sm_ref[i:i + 1, :]
        loss_ref = outs[0]
        loss_ref[...] = tot[:, SM_LOSS:SM_LOSS + 128]
        g_ada = lax.dot_general(_silu(c_ref[...]), dm_ref[me], (((0,), (0,)), ((), ())),
                                preferred_element_type=F32, precision=HIGHEST)
        outs[1][...] = g_ada
        outs[2][...], outs[3][...], outs[4][...] = _adamw_math(wa_ref[...], g_ada, ma_ref[...], va_ref[...])
        p0 = _lower_bound(lg_ref)
        dl0 = tot[:, SM_LB:SM_LB + 512] * p0 * (1.0 - p0)
        grads = [tot[:, SM_MOD:SM_MOD + 6 * D_MODEL], tot[:, SM_G1:SM_G1 + D_MODEL], tot[:, SM_G2:SM_G2 + D_MODEL],
                 tot[:, SM_GF:SM_GF + D_MODEL], tot[:, SM_ATT:SM_ATT + 512], tot[:, SM_HG:SM_HG + 128],
                 jnp.where(lax.broadcasted_iota(jnp.int32, (2, 512), 0) == 0, dl0, -dl0)]
        for i, g in enumerate(grads):
            w_ref, m_ref, v_ref = ins[3 * i:3 * i + 3]
            o = outs[5 + 4 * i:9 + 4 * i]
            o[0][...] = g
            o[1][...], o[2][...], o[3][...] = _adamw_math(w_ref[...], g, m_ref[...], v_ref[...])

    flat = [t for trio in smalls for t in trio]
    vm = pl.BlockSpec(memory_space=pltpu.VMEM)
    out_shape = [jax.ShapeDtypeStruct((1, 128), F32)] + [jax.ShapeDtypeStruct(w_ada.shape, F32)] * 4
    for trio in smalls:
        out_shape += [jax.ShapeDtypeStruct(trio[0].shape, F32)] * 4
    return pl.pallas_call(
        body, name="small_update", out_shape=out_shape,
        in_specs=[vm] * (7 + len(flat)), out_specs=[vm] * len(out_shape),
        compiler_params=pltpu.CompilerParams(vmem_limit_bytes=V7X_VMEM_LIMIT),
    )(small_all, dmod_blocks, c_all, logits, w_ada, m_ada, v_ada, *flat)


def kernel(x, c, w_ada, b_ada, norm1_g, w_in, hg_lb_logits, hg_onorm_g, att_onorm_g, w_out, norm2_g, w_gate_up, w_down, final_g, loss_target, m_w_ada, m_b_ada, m_norm1_g, m_w_in, m_hg_lb_logits, m_hg_onorm_g, m_att_onorm_g, m_w_out, m_norm2_g, m_w_gate_up, m_w_down, m_final_g, v_w_ada, v_b_ada, v_norm1_g, v_w_in, v_hg_lb_logits, v_hg_onorm_g, v_att_onorm_g, v_w_out, v_norm2_g, v_w_gate_up, v_w_down, v_final_g):
    x2d, target = x[0], loss_target[0]
    seq = x2d.shape[0]
    assert seq % (ATT_BLOCK * max(DILATIONS)) == 0 and seq % HG_TILE == 0
    gf = final_g.reshape(1, D_MODEL)

    core = lax.axis_index("c").astype(jnp.int32).reshape(1)
    chip = (2 * lax.axis_index("x") + lax.axis_index("y")).astype(jnp.int32).reshape(1)
    me = 4 * lax.axis_index("x") + 2 * lax.axis_index("y") + lax.axis_index("c")

    in_shard = [w_in[0].T.astype(BF16)]
    i_send, i_recv, i_srcs, i_lands, tok_in = _copies_start(
        "gather_in_start", _plan_gather_own, 4, in_shard, [lax.empty((N_DEV,) + in_shard[0].shape, BF16)], [])
    c_all = _exchange_small(c.reshape(8, D_MODEL // 8) + tok_in[0:1, 0:1], None, "gather_c").reshape(N_DEV, D_MODEL)
    ada = _ada_rows(c_all, w_ada[0], b_ada)
    mod = _exchange_small(ada, 1, "scatter_mod").reshape(1, 6 * D_MODEL)
    i_srcs, i_lands = _copies_wait("gather_in_wait", _plan_gather_own, i_send, i_recv, i_srcs, i_lands, [mod])
    p_send, p_recv, _, i_lands, _ = _copies_start("gather_in_pass_start", _plan_gather_pass, 3, [], i_lands, [])
    _, i_lands = _copies_wait("gather_in_pass_wait", _plan_gather_pass, p_send, p_recv, [], i_lands, [])
    w_in_b = lax.dynamic_update_index_in_dim(i_lands[0], i_srcs[0], me, 0).reshape(IN_WIDTH, D_MODEL)
    rest_shards = [w_out[0].astype(BF16), w_gate_up[0].T.astype(BF16), w_down[0].astype(BF16)]
    lands = [lax.empty((N_DEV,) + s.shape, BF16) for s in rest_shards]
    g_send, g_recv, g_srcs, g_lands, tok = _copies_start("gather_rest_start", _plan_gather_own, 12, rest_shards, lands, [w_in_b, mod])
    flight = {}

    def stage(name, *vals):
        if name == "attention_begun":
            flight["shards"], got = _copies_wait("gather_rest_wait", _plan_gather_own, g_send, g_recv, g_srcs, g_lands, list(vals))
            flight["pass"] = _copies_start("gather_pass_start", _plan_gather_pass, 9, [], got, [])
            return flight["pass"][4][0:1, 0:1]
        if name == "mixer_weights_done":
            shapes = [(4, 2, D_MODEL // N_DEV, D_MODEL), (4, 2, 2 * D_FF // N_DEV, D_MODEL), (4, 2, D_FF // N_DEV, D_MODEL)]
            flight["grads"] = [g32.reshape(sh) for (g32, _), sh in zip(vals, shapes)]
            rounded = [g16.reshape(sh) for (_, g16), sh in zip(vals, shapes)]
            direct_lands = [lax.empty((N_DEV - 1,) + sh[2:], BF16) for sh in shapes]
            flight["direct"] = _copies_start("reduce_rest_start", _plan_reduce_direct, 21, rounded, direct_lands, [])
            return flight["direct"][4][0:1, 0:1]
        raise ValueError(name)

    def rest_weights(after):
        s, r, _, p_lands, _ = flight["pass"]
        _, got = _copies_wait("gather_pass_wait", _plan_gather_pass, s, r, [], p_lands, [after])
        full = [lax.dynamic_update_index_in_dim(g, shard, me, 0) for g, shard in zip(got, flight["shards"])]
        return full[0].reshape(D_MODEL, D_MODEL), full[1].reshape(2 * D_FF, D_MODEL), full[2].reshape(D_FF, D_MODEL)

    grad_x, dw_in, small = _block_step(x2d, target, mod + tok[0:1, 0:1], norm1_g, hg_lb_logits, hg_onorm_g, att_onorm_g, norm2_g, gf,
                                       w_in_b, rest_weights, stage)

    g_in8 = dw_in.reshape(4, 2, IN_WIDTH // N_DEV, D_MODEL)
    in_pairs = _copies_start("reduce_pairs_in_start", _plan_reduce_pairs, 4, [g_in8], [lax.empty((4,) + g_in8.shape[2:], F32)], [])
    s, r, srcs, d_lands, _ = flight["direct"]
    _, recv_rest = _copies_wait("reduce_rest_wait", _plan_reduce_direct, s, r, srcs, d_lands, [in_pairs[4]])
    small_rows = jnp.pad(small + in_pairs[4][0:1, 0:1], ((0, 0), (0, SM_PADDED - SM_WIDTH))).reshape(SM_PADDED // 128, 128)
    small_all = _exchange_small(small_rows, None, "gather_small").reshape(N_DEV, SM_PADDED)[:, :SM_WIDTH]
    in_grads, got_in = _copies_wait("reduce_pairs_in_wait", _plan_reduce_pairs, in_pairs[0], in_pairs[1], in_pairs[2], in_pairs[3],
                                    [small_all])
    in_s32, in_s16 = _pair_sum(core, in_grads[0], got_in[0], "pair_sum_in")
    in_chips = _copies_start("reduce_chips_in_start", _plan_reduce_chips, 3, [in_s16], [lax.empty((3,) + in_s16.shape[1:], BF16)], [])
    big, updated = {}, []
    rest_params = [("w_out", w_out, m_w_out, v_w_out), ("w_gate_up", w_gate_up, m_w_gate_up, v_w_gate_up), ("w_down", w_down, m_w_down, v_w_down)]
    mine = jnp.concatenate([chip, core])
    for (n, w, m, v), g32, got in zip(rest_params, flight["grads"], recv_rest):
        if n == "w_gate_up":
            outs4 = _adamw_shard(mine, w[0].T, m[0].T, v[0].T, g32, got, f"adamw_{n}")
            big[n] = [t.T[None] for t in outs4]
        else:
            outs4 = _adamw_shard(mine, w[0], m[0], v[0], g32, got, f"adamw_{n}")
            big[n] = [t[None] for t in outs4]
        updated.append(outs4[3])
    c_all = c_all + in_chips[4][0:1, 0:1]
    smalls = [(b_ada, m_b_ada, v_b_ada), (norm1_g, m_norm1_g, v_norm1_g), (norm2_g, m_norm2_g, v_norm2_g),
              (gf, m_final_g.reshape(1, D_MODEL), v_final_g.reshape(1, D_MODEL)),
              (att_onorm_g, m_att_onorm_g, v_att_onorm_g), (hg_onorm_g, m_hg_onorm_g, v_hg_onorm_g),
              (hg_lb_logits, m_hg_lb_logits, v_hg_lb_logits)]
    dmod_blocks = small_all[:, :6 * D_MODEL].reshape(N_DEV, N_DEV, 6 * D_MODEL // N_DEV).transpose(1, 0, 2)
    res = _small_update(small_all, dmod_blocks, c_all, hg_lb_logits, w_ada[0], m_w_ada[0], v_w_ada[0], smalls)
    _, recv_in = _copies_wait("reduce_chips_in_wait", _plan_reduce_chips, in_chips[0], in_chips[1], in_chips[2], in_chips[3],
                              [res[0]] + updated)
    big["w_in"] = [t.T[None] for t in _adamw_shard(chip, w_in[0].T, m_w_in[0].T, v_w_in[0].T, in_s32, recv_in[0], "adamw_w_in")]
    loss = res[0][0, 0]
    ada4 = [t[None] for t in res[1:5]]
    sm4 = {n: list(res[5 + 4 * i:9 + 4 * i]) for i, n in enumerate(["b_ada", "norm1_g", "norm2_g", "final_g", "att", "hg", "lb"])}
    sm4["final_g"] = [t.reshape(D_MODEL) for t in sm4["final_g"]]

    order = [ada4, sm4["b_ada"], sm4["norm1_g"], big["w_in"], sm4["lb"], sm4["hg"], sm4["att"], big["w_out"], sm4["norm2_g"],
             big["w_gate_up"], big["w_down"], sm4["final_g"]]
    return (loss, grad_x[None], *[o[0] for o in order], *[o[1] for o in order], *[o[2] for o in order], *[o[3] for o in order])


def _block_step(x2d, target, mod, norm1_g, hg_lb_logits, hg_onorm_g, att_onorm_g, norm2_g, gf, w_in_b, rest_weights, stage):
    h1, hq, hf, hi, hgt, aq, ak, av = _in_fwd(x2d, mod, norm1_g, w_in_b)
    hg_out, hg_o, hg_states = _hg_fwd(hq, hf, hi, hgt, hg_lb_logits, hg_onorm_g)
    branch = [_att_fwd(aq, ak, av, d) for d in DILATIONS[:2]]
    att_g = att_onorm_g + stage("attention_begun", branch[0][0], branch[1][0])
    branch += [_att_fwd(aq, ak, av, d) for d in DILATIONS[2:]]
    outs = [b[0] for b in branch]
    lses = [b[1] for b in branch]
    att, att_out = _att_combine(outs, lses, att_g)
    w_out_b, w_gu_b, w_down_b = rest_weights(att_out)
    x1 = _out_fwd(x2d, hg_out, att_out, mod, w_out_b)

    dx1, h2, act, dau, dff, ffn_sums, loss_part = _ffn(x1, target, mod, norm2_g, gf, w_gu_b, w_down_b)
    dw_gu = _weight_grad(dau, h2, "dw_gate_up", rounded=True)
    dw_down = _weight_grad(act, dff, "dw_down", rounded=True)

    dhg, dat, dw_out, dw_out_b, dgate1 = _out_bwd(dx1, hg_out, att_out, mod, w_out_b)
    att_g = att_onorm_g + stage("mixer_weights_done", (dw_out, dw_out_b), dw_gu, dw_down)
    comb = _att_combine_bwd(dat, att, lses, att_g)
    dos, ccs, d_att_g = comb[0:3], comb[3:6], comb[6]
    datt = []
    for i, d in enumerate(DILATIONS):
        datt.append(_att_bwd(aq, ak, av, dos[i], ccs[i], lses[i], d))
    dhq, dhf, dhi, dhgt, d_hg_g, d_lb = _hg_bwd(hq, hf, hi, hgt, hg_lb_logits, hg_onorm_g, hg_o, hg_states, dhg)
    dps = [dhq, dhf, dhi, dhgt] + [datt[i][j] for j in range(3) for i in range(3)]
    grad_x, dp_b, dshift1, dscale1, d_g1 = _in_bwd(x2d, dx1, mod, norm1_g, w_in_b, dps)
    dw_in, = _weight_grad(dp_b, h1, "dw_in")
    small = jnp.concatenate([dshift1, dscale1, dgate1, ffn_sums[0:1], ffn_sums[1:2], ffn_sums[2:3], d_g1, ffn_sums[3:4],
                             ffn_sums[4:5], d_att_g, d_lb, d_hg_g, loss_part], axis=1)
    return grad_x, dw_in, small
```

```python
import functools

import jax
import jax.numpy as jnp
from jax import lax
from jax.experimental import pallas as pl
from jax.experimental.pallas import tpu as pltpu

F32 = jnp.float32
BF16 = jnp.bfloat16
HIGHEST = lax.Precision.HIGHEST
MESH = pl.DeviceIdType.MESH

D_MODEL = 1024
N_DEV = 8
HG_HEADS = 4
HG_DIM = 128
HG_WIDTH = HG_HEADS * HG_DIM
HG_CHUNK = 128
ATT_WIDTH = 512
ATT_HEAD_DIM = 64
ATT_BLOCK = 128
DILATIONS = (1, 4, 16)
ATT_SCALE = ATT_HEAD_DIM ** -0.5
D_FF = 2816
IN_WIDTH = 7 * 512
RMS_EPS = 1e-6
NEG = -1e30

ADAM_LR = 0.001
ADAM_B1 = 0.9
ADAM_B2 = 0.999
ADAM_EPS = 1e-08
ADAM_WD = 0.01
ADAM_STEP = 10

V7X_VMEM_LIMIT = 56 * 1024 * 1024

SM_MOD = 0
SM_G1 = 6 * D_MODEL
SM_G2 = 7 * D_MODEL
SM_GF = 8 * D_MODEL
SM_ATT = 9 * D_MODEL
SM_LB = 9 * D_MODEL + 512
SM_HG = 10 * D_MODEL
SM_LOSS = 10 * D_MODEL + 128
SM_WIDTH = 10 * D_MODEL + 256
SM_PADDED = 88 * 128


def _params(*sem, vmem=V7X_VMEM_LIMIT):
    return pltpu.CompilerParams(dimension_semantics=sem, vmem_limit_bytes=vmem)


def _dot(a, b):
    return jnp.dot(a, b, preferred_element_type=F32)


def _dot_nt(a, b):
    return lax.dot_general(a, b, (((1,), (1,)), ((), ())), preferred_element_type=F32)


def _dot_tn(a, b):
    return lax.dot_general(a, b, (((0,), (0,)), ((), ())), preferred_element_type=F32)


def _dot_f32(a, b):
    return jnp.dot(a, b, preferred_element_type=F32, precision=HIGHEST)


def _sigmoid(x):
    return 1.0 / (1.0 + jnp.exp(-x))


def _silu(x):
    return x * _sigmoid(x)


def _dsilu(x):
    s = _sigmoid(x)
    return s * (1.0 + x * (1.0 - s))


def _rms(x):
    rstd = lax.rsqrt(jnp.mean(x * x, axis=-1, keepdims=True) + RMS_EPS)
    return x * rstd, rstd


def _rms_bwd(dn, xhat, rstd):
    return rstd * (dn - xhat * jnp.mean(dn * xhat, axis=-1, keepdims=True))


def _rowsum(x):
    return jnp.sum(x, axis=0, keepdims=True)


def _rows(tm, n):
    return pl.BlockSpec((tm, n), lambda i: (i, 0))


def _whole(shape):
    return pl.BlockSpec(shape, lambda i: (0,) * len(shape))


def _mesh_pos():
    return lax.axis_index("x"), lax.axis_index("y"), lax.axis_index("c")


def _flip(k):
    x, y, c = _mesh_pos()
    px = 1 - x if k & 4 else x
    py = 1 - y if k & 2 else y
    pc = 1 - c if k & 1 else c
    return (px, py, pc), 4 * px + 2 * py + pc


_ANY = [pl.BlockSpec(memory_space=pl.ANY)]


def _behind(body, after):
    return lambda *refs: body(*refs[len(after):])


def _exchange_small(x, rows_per_peer, name, after=()):
    r_all, cols = x.shape
    r_out = r_all if rows_per_peer is None else rows_per_peer

    def body(x_ref, out_ref, send_sems, recv_sems):
        _, me = _flip(0)

        def src(pid):
            if rows_per_peer is None:
                return x_ref
            return x_ref.at[pl.ds(pl.multiple_of(pid * r_out, r_out), r_out), :]

        if rows_per_peer is None:
            out_ref[me] = x_ref[...]
        else:
            out_ref[me] = x_ref[pl.ds(pl.multiple_of(me * r_out, r_out), r_out), :]
        sends = []
        for k in range(1, N_DEV):
            dev, pid = _flip(k)
            cp = pltpu.make_async_remote_copy(src_ref=src(pid), dst_ref=out_ref.at[me], send_sem=send_sems.at[k - 1],
                                              recv_sem=recv_sems.at[k - 1], device_id=dev, device_id_type=MESH)
            cp.start()
            sends.append(cp)
        for k in range(1, N_DEV):
            dev, pid = _flip(k)
            pltpu.make_async_remote_copy(src_ref=src(pid), dst_ref=out_ref.at[pid], send_sem=send_sems.at[k - 1],
                                         recv_sem=recv_sems.at[k - 1], device_id=dev, device_id_type=MESH).wait_recv()
        for cp in sends:
            cp.wait_send()

    return pl.pallas_call(
        _behind(body, after), name=name,
        out_shape=jax.ShapeDtypeStruct((N_DEV, r_out, cols), x.dtype),
        in_specs=_ANY * len(after) + [pl.BlockSpec(memory_space=pltpu.VMEM)],
        out_specs=pl.BlockSpec(memory_space=pltpu.VMEM),
        scratch_shapes=[pltpu.SemaphoreType.DMA((N_DEV - 1,)), pltpu.SemaphoreType.DMA((N_DEV - 1,))],
    )(*after, x)


def _gather_weights(shards):
    n = len(shards)

    def body(*refs):
        xs, outs = refs[:n], refs[n:2 * n]
        send_sems, recv_sems, local_sems = refs[2 * n:]
        x, y, c = _mesh_pos()
        me, sibling = (x, y, c), (x, y, 1 - c)
        chips = [(1 - x, y), (x, 1 - y), (1 - x, 1 - y)]

        def blk(a, px, py, pc):
            return outs[a].at[4 * px + 2 * py + pc]

        def copy(a, k, block, to, src=None):
            return pltpu.make_async_remote_copy(
                src_ref=blk(a, *block) if src is None else src, dst_ref=blk(a, *block),
                send_sem=send_sems.at[a * 7 + k], recv_sem=recv_sems.at[a * 7 + k], device_id=to, device_id_type=MESH)

        mine = [pltpu.make_async_copy(xs[a], blk(a, *me), local_sems.at[a]) for a in range(n)]
        for cp in mine:
            cp.start()
        first = []
        for a in range(n):
            first.append(copy(a, 0, me, sibling, src=xs[a]))
            first += [copy(a, 1 + j, me, (*chip, c), src=xs[a]) for j, chip in enumerate(chips)]
        for cp in first:
            cp.start()
        passed = []
        for j, chip in enumerate(chips):
            for a in range(n):
                copy(a, 1 + j, (*chip, c), me).wait_recv()
                cp = copy(a, 4 + j, (*chip, c), sibling)
                cp.start()
                passed.append(cp)
        for a in range(n):
            copy(a, 0, sibling, me).wait_recv()
            for j, chip in enumerate(chips):
                copy(a, 4 + j, (*chip, 1 - c), me).wait_recv()
        for cp in first + passed:
            cp.wait_send()
        for cp in mine:
            cp.wait()

    hbm = pl.BlockSpec(memory_space=pl.ANY)
    return pl.pallas_call(
        body, name="gather_weights",
        out_shape=[jax.ShapeDtypeStruct((N_DEV,) + s.shape, s.dtype) for s in shards],
        in_specs=[hbm] * n, out_specs=[hbm] * n,
        scratch_shapes=[pltpu.SemaphoreType.DMA((7 * n,)), pltpu.SemaphoreType.DMA((7 * n,)), pltpu.SemaphoreType.DMA((n,))],
    )(*shards)


_HBM = pl.BlockSpec(memory_space=pltpu.HBM)
_SEM = pl.BlockSpec(memory_space=pltpu.SEMAPHORE)
_DATAFLOW = pltpu.SideEffectType.DATAFLOW_SIDE_EFFECTING


def _copies_start(name, plan, n_copies, srcs, lands, after):
    bufs = list(srcs) + list(lands)
    nb = len(bufs)

    def body(*refs):
        ins, send_sems, recv_sems, token = refs[:nb], refs[nb + len(after)], refs[nb + len(after) + 1], refs[-1]
        for i, (src, dst, dev) in enumerate(plan(ins[:len(srcs)], ins[len(srcs):])):
            pltpu.make_async_remote_copy(src_ref=src, dst_ref=dst, send_sem=send_sems.at[i], recv_sem=recv_sems.at[i],
                                         device_id=dev, device_id_type=MESH).start()
        token[...] = jnp.zeros_like(token)

    outs = pl.pallas_call(
        body, name=name,
        out_shape=(pltpu.SemaphoreType.DMA((n_copies,)), pltpu.SemaphoreType.DMA((n_copies,)),
                   *[pltpu.HBM(b.shape, b.dtype) for b in bufs], jax.ShapeDtypeStruct((8, 128), F32)),
        in_specs=[_HBM] * nb + [pl.BlockSpec(memory_space=pl.ANY)] * len(after),
        out_specs=(_SEM, _SEM, *[_HBM] * nb, pl.BlockSpec(memory_space=pltpu.VMEM)),
        input_output_aliases={i: 2 + i for i in range(nb)},
        compiler_params=pltpu.CompilerParams(has_side_effects=_DATAFLOW),
    )(*[pltpu.with_memory_space_constraint(b, pltpu.HBM) for b in bufs], *after)
    return outs[0], outs[1], list(outs[2:2 + len(srcs)]), list(outs[2 + len(srcs):2 + nb]), outs[-1]


def _copies_wait(name, plan, send_sems, recv_sems, srcs, lands, after):
    bufs = list(srcs) + list(lands)
    nb = len(bufs)

    def body(*refs):
        ins, send_ref, recv_ref = refs[:nb], refs[nb], refs[nb + 1]
        for i, (src, dst, dev) in enumerate(plan(ins[:len(srcs)], ins[len(srcs):])):
            cp = pltpu.make_async_remote_copy(src_ref=src, dst_ref=dst, send_sem=send_ref.at[i], recv_sem=recv_ref.at[i],
                                              device_id=dev, device_id_type=MESH)
            cp.wait_send()
            cp.wait_recv()

    outs = pl.pallas_call(
        body, name=name, out_shape=[pltpu.HBM(b.shape, b.dtype) for b in bufs],
        in_specs=[_HBM] * nb + [_SEM, _SEM] + [pl.BlockSpec(memory_space=pl.ANY)] * len(after), out_specs=[_HBM] * nb,
        input_output_aliases={i: i for i in range(nb)},
        compiler_params=pltpu.CompilerParams(has_side_effects=_DATAFLOW),
    )(*bufs, send_sems, recv_sems, *after)
    return list(outs[:len(srcs)]), list(outs[len(srcs):])


def _plan_gather_own(srcs, lands):
    _, me = _flip(0)
    return [(srcs[a], lands[a].at[me], _flip(k)[0]) for a in range(len(srcs)) for k in (1, 4, 2, 6)]


def _plan_gather_pass(srcs, lands):
    sibling = _flip(1)[0]
    plan = []
    for land in lands:
        for k in (4, 2, 6):
            block = land.at[_flip(k)[1]]
            plan.append((block, block, sibling))
    return plan


def _plan_reduce_pairs(srcs, lands):
    x, y, c = _mesh_pos()
    return [(srcs[a].at[chip, 1 - c], lands[a].at[chip], (x, y, 1 - c)) for a in range(len(srcs)) for chip in range(4)]


def _plan_reduce_chips(srcs, lands):
    plan = []
    for a in range(len(srcs)):
        for j, k in enumerate((4, 2, 6)):
            dev = _flip(k)[0]
            plan.append((srcs[a].at[2 * dev[0] + dev[1]], lands[a].at[j], dev))
    return plan


def _plan_reduce_direct(srcs, lands):
    plan = []
    for a in range(len(srcs)):
        for k in range(1, N_DEV):
            dev = _flip(k)[0]
            plan.append((srcs[a].at[2 * dev[0] + dev[1], dev[2]], lands[a].at[k - 1], dev))
    return plan


def _shard_rows(r):
    return r // 2 if r % 32 == 0 else r


def _pair_sum(core, grads, got, name):
    _, _, r, c = grads.shape
    tr = _shard_rows(r)

    def body(core_ref, a_ref, b_ref, o_ref, ob_ref):
        s = a_ref[...] + b_ref[...]
        o_ref[...] = s
        ob_ref[...] = s.astype(BF16)

    spec = pl.BlockSpec((None, tr, c), lambda i, j, core_ref: (i, j, 0))
    return pl.pallas_call(
        body, name=name,
        grid_spec=pltpu.PrefetchScalarGridSpec(
            num_scalar_prefetch=1, grid=(4, r // tr),
            in_specs=[pl.BlockSpec((None, None, tr, c), lambda i, j, core_ref: (i, core_ref[0], j, 0)), spec],
            out_specs=[spec, spec]),
        out_shape=[jax.ShapeDtypeStruct((4, r, c), F32), jax.ShapeDtypeStruct((4, r, c), BF16)],
        compiler_params=_params("parallel", "parallel"),
    )(core, grads, got)


def _ada_rows(c_all, w_ada, b_ada):
    n_cols = w_ada.shape[1]

    def body(c_ref, w_ref, b_ref, o_ref):
        _, me = _flip(0)
        bias = b_ref[:, pl.ds(pl.multiple_of(me * n_cols, 128), n_cols)]
        o_ref[...] = _dot_f32(_silu(c_ref[...]), w_ref[...]) + bias

    return pl.pallas_call(
        body, name="ada_rows", out_shape=jax.ShapeDtypeStruct((N_DEV, n_cols), F32),
        in_specs=[pl.BlockSpec(memory_space=pltpu.VMEM)] * 3, out_specs=pl.BlockSpec(memory_space=pltpu.VMEM),
    )(c_all, w_ada, b_ada)


def _in_fwd(x, mod, g1, w_in, after=()):
    s = x.shape[0]
    tm = 256

    def body(x_ref, mod_ref, g_ref, w_ref, h_ref, *outs):
        xhat, _ = _rms(x_ref[...])
        h = (xhat * g_ref[...]) * (1.0 + mod_ref[:, D_MODEL:2 * D_MODEL]) + mod_ref[:, 0:D_MODEL]
        hb = h.astype(BF16)
        h_ref[...] = hb
        for j, o_ref in enumerate(outs):
            o_ref[...] = _dot_nt(hb, w_ref[j * 512:(j + 1) * 512, :])

    return pl.pallas_call(
        _behind(body, after), name="in_fwd", grid=(s // tm,),
        out_shape=[jax.ShapeDtypeStruct((s, D_MODEL), BF16)] + [jax.ShapeDtypeStruct((s, 512), F32)] * 7,
        in_specs=_ANY * len(after)
        + [_rows(tm, D_MODEL), _whole((1, 6 * D_MODEL)), _whole((1, D_MODEL)), _whole((IN_WIDTH, D_MODEL))],
        out_specs=[_rows(tm, D_MODEL)] + [_rows(tm, 512)] * 7,
        compiler_params=_params("parallel"),
    )(*after, x, mod, g1, w_in)


def _in_bwd(x, dx1, mod, g1, w_in, dps):
    s = x.shape[0]
    tm = 256

    def body(x_ref, dx_ref, mod_ref, g_ref, w_ref, *rest):
        dp_refs, (gx_ref, dpb_ref, dsh_ref, dsc_ref, dg_ref) = rest[:13], rest[13:]
        pieces = [dp_refs[j][...] for j in range(4)]
        pieces += [dp_refs[4 + 3 * j][...] + dp_refs[5 + 3 * j][...] + dp_refs[6 + 3 * j][...] for j in range(3)]
        for j, p in enumerate(pieces):
            dpb_ref[:, j * 512:(j + 1) * 512] = p.astype(BF16)
        dh = _dot(dpb_ref[...], w_ref[...])
        xhat, rstd = _rms(x_ref[...])
        g = g_ref[...]
        scale1 = 1.0 + mod_ref[:, D_MODEL:2 * D_MODEL]
        n1 = xhat * g

        @pl.when(pl.program_id(0) == 0)
        def _():
            dsh_ref[...] = jnp.zeros_like(dsh_ref)
            dsc_ref[...] = jnp.zeros_like(dsc_ref)
            dg_ref[...] = jnp.zeros_like(dg_ref)

        dsh_ref[...] += _rowsum(dh)
        dsc_ref[...] += _rowsum(dh * n1)
        dn = dh * scale1
        dg_ref[...] += _rowsum(dn * xhat)
        gx_ref[...] = dx_ref[...] + _rms_bwd(dn * g, xhat, rstd)

    vec = _whole((1, D_MODEL))
    return pl.pallas_call(
        body, name="in_bwd", grid=(s // tm,),
        out_shape=[jax.ShapeDtypeStruct((s, D_MODEL), F32), jax.ShapeDtypeStruct((s, IN_WIDTH), BF16)]
        + [jax.ShapeDtypeStruct((1, D_MODEL), F32)] * 3,
        in_specs=[_rows(tm, D_MODEL), _rows(tm, D_MODEL), _whole((1, 6 * D_MODEL)), vec, _whole((IN_WIDTH, D_MODEL))]
        + [_rows(tm, 512)] * 13,
        out_specs=[_rows(tm, D_MODEL), _rows(tm, IN_WIDTH), vec, vec, vec],
        compiler_params=_params("arbitrary"),
    )(x, dx1, mod, g1, w_in, *dps)


HG_TILE = 512
HG_TILE_CHUNKS = HG_TILE // HG_CHUNK


def _lower_bound(lg_ref):
    return 1.0 / (1.0 + jnp.exp(lg_ref[1:2, :] - lg_ref[0:1, :]))


def _chunk_masks():
    r = lax.broadcasted_iota(jnp.int32, (HG_CHUNK, HG_CHUNK), 0)
    c = lax.broadcasted_iota(jnp.int32, (HG_CHUNK, HG_CHUNK), 1)
    return r >= c, c >= r, (r >= c).astype(F32), (c >= r).astype(F32)


def _hg_fwd(hq, hf, hi, hgt, logits, onorm_g):
    s = hq.shape[0]
    n_tiles = s // HG_TILE

    def body(q_ref, f_ref, i_ref, g_ref, lg_ref, og_ref, out_ref, o_ref, st_ref, state, qf_s, kk_s, lf_s):
        @pl.when(pl.program_id(0) == 0)
        def _():
            state[...] = jnp.zeros_like(state)

        lb = _lower_bound(lg_ref)
        f = lb + (1.0 - lb) * _sigmoid(f_ref[...])
        kk_s[...] = 1.0 - f
        lf_s[...] = jnp.log(f)
        qf_s[...] = _silu(q_ref[...])
        causal, _, tri, _ = _chunk_masks()

        def chunk(ci, carry):
            rows = pl.ds(pl.multiple_of(ci * HG_CHUNK, HG_CHUNK), HG_CHUNK)
            srows = pl.ds(pl.multiple_of(ci * HG_DIM, HG_DIM), HG_DIM)
            lf = lf_s[rows, :]
            b = _dot_f32(tri, lf)
            bl = _rowsum(lf)
            ref = 0.5 * bl
            qf, kk, v = qf_s[rows, :], kk_s[rows, :], i_ref[rows, :]
            a_in = (qf * jnp.exp(b)).astype(BF16)
            a_t = (qf * jnp.exp(b - ref)).astype(BF16)
            b_t = (kk * jnp.exp(ref - b)).astype(BF16)
            kd = kk * jnp.exp(bl - b)
            ebl = jnp.exp(bl)
            vb = v.astype(BF16)
            for h in range(HG_HEADS):
                c = slice(h * HG_DIM, (h + 1) * HG_DIM)
                st = state[h]
                st_ref[srows, c] = st
                p = jnp.where(causal, _dot_nt(a_t[:, c], b_t[:, c]), 0.0)
                o_ref[rows, c] = _dot(p.astype(BF16), vb[:, c]) + _dot_nt(a_in[:, c], st.astype(BF16))
                state[h] = st * ebl[:, c] + _dot_tn(vb[:, c], kd[:, c].astype(BF16))
            return carry

        lax.fori_loop(0, HG_TILE_CHUNKS, chunk, 0, unroll=True)
        for h in range(HG_HEADS):
            c = slice(h * HG_DIM, (h + 1) * HG_DIM)
            ohat, _ = _rms(o_ref[:, c])
            out_ref[:, c] = (ohat * og_ref[...] * _silu(g_ref[:, c])).astype(BF16)

    tile = _rows(HG_TILE, HG_WIDTH)
    return pl.pallas_call(
        body, name="hg_fwd", grid=(n_tiles,),
        out_shape=[jax.ShapeDtypeStruct((s, HG_WIDTH), BF16), jax.ShapeDtypeStruct((s, HG_WIDTH), F32),
                   jax.ShapeDtypeStruct((s // HG_CHUNK * HG_DIM, HG_WIDTH), F32)],
        in_specs=[tile] * 4 + [_whole((2, HG_WIDTH)), _whole((1, HG_DIM))],
        out_specs=[tile, tile, _rows(HG_TILE_CHUNKS * HG_DIM, HG_WIDTH)],
        scratch_shapes=[pltpu.VMEM((HG_HEADS, HG_DIM, HG_DIM), F32)] + [pltpu.VMEM((HG_TILE, HG_WIDTH), F32)] * 3,
        compiler_params=_params("arbitrary"),
    )(hq, hf, hi, hgt, logits, onorm_g)


def _hg_bwd(hq, hf, hi, hgt, logits, onorm_g, o, states, dout):
    s = hq.shape[0]
    n_tiles = s // HG_TILE

    def body(q_ref, f_ref, i_ref, g_ref, lg_ref, og_ref, o_ref, st_ref, d_ref,
             dq_ref, df_ref, di_ref, dg_ref, dog_ref, dlb_ref, dstate, qf_s, kk_s, lf_s, do_s):
        @pl.when(pl.program_id(0) == 0)
        def _():
            dstate[...] = jnp.zeros_like(dstate)
            dog_ref[...] = jnp.zeros_like(dog_ref)
            dlb_ref[...] = jnp.zeros_like(dlb_ref)

        og = og_ref[...]
        dog = jnp.zeros((1, HG_DIM), F32)
        for h in range(HG_HEADS):
            c = slice(h * HG_DIM, (h + 1) * HG_DIM)
            ohat, rstd = _rms(o_ref[:, c])
            gate = g_ref[:, c]
            d = d_ref[:, c]
            dg_ref[:, c] = (d * (ohat * og) * _dsilu(gate)).astype(BF16)
            dnormed = d * _silu(gate)
            dog += _rowsum(dnormed * ohat)
            do_s[:, c] = _rms_bwd(dnormed * og, ohat, rstd)
        dog_ref[...] += dog

        lb = _lower_bound(lg_ref)
        f = lb + (1.0 - lb) * _sigmoid(f_ref[...])
        kk_s[...] = 1.0 - f
        lf_s[...] = jnp.log(f)
        qf_s[...] = _silu(q_ref[...])
        causal, upper, tri, tri_t = _chunk_masks()

        def chunk(step, carry):
            ci = HG_TILE_CHUNKS - 1 - step
            rows = pl.ds(pl.multiple_of(ci * HG_CHUNK, HG_CHUNK), HG_CHUNK)
            srows = pl.ds(pl.multiple_of(ci * HG_DIM, HG_DIM), HG_DIM)
            lf = lf_s[rows, :]
            b = _dot_f32(tri, lf)
            bl = _rowsum(lf)
            ref = 0.5 * bl
            qf, kk, v, do = qf_s[rows, :], kk_s[rows, :], i_ref[rows, :], do_s[rows, :]
            eb, ebr, erb, ekd, ebl = jnp.exp(b), jnp.exp(b - ref), jnp.exp(ref - b), jnp.exp(bl - b), jnp.exp(bl)
            a_in, a_t, b_t, kd = qf * eb, qf * ebr, kk * erb, kk * ekd
            for h in range(HG_HEADS):
                c = slice(h * HG_DIM, (h + 1) * HG_DIM)
                st, dst = st_ref[srows, c], dstate[h]
                stb, dstb = st.astype(BF16), dst.astype(BF16)
                doh, vh = do[:, c], v[:, c]
                dob, vb = doh.astype(BF16), vh.astype(BF16)
                ain_h, at_h, bt_h, kd_h = a_in[:, c], a_t[:, c], b_t[:, c], kd[:, c]
                atb, btb = at_h.astype(BF16), bt_h.astype(BF16)
                d_ain = _dot(dob, stb)
                p_t = jnp.where(upper, _dot_nt(btb, atb), 0.0).astype(BF16)
                dp = jnp.where(causal, _dot_nt(dob, vb), 0.0).astype(BF16)
                dp_t = jnp.where(upper, _dot_nt(vb, dob), 0.0).astype(BF16)
                di_ref[rows, c] = (_dot(p_t, dob) + _dot_nt(kd_h.astype(BF16), dstb)).astype(BF16)
                d_at = _dot(dp, btb)
                d_bt = _dot(dp_t, atb)
                d_kd = _dot(vb, dstb)
                dqf = d_ain * eb[:, c] + d_at * ebr[:, c]
                dkk = d_bt * erb[:, c] + d_kd * ekd[:, c]
                db = d_ain * ain_h + d_at * atb.astype(F32) - d_bt * btb.astype(F32) - d_kd * kd_h
                dbl = _rowsum(d_kd * kd_h) + _rowsum(dst * st) * ebl[:, c]
                dstate[h] = _dot_tn(dob, ain_h.astype(BF16)) + dst * ebl[:, c]
                dlf = _dot_f32(tri_t, db) + dbl
                qv, fr = q_ref[rows, c], f_ref[rows, c]
                lbh = lb[:, c]
                sg = _sigmoid(fr)
                dfv = dlf / (lbh + (1.0 - lbh) * sg) - dkk
                df_ref[rows, c] = (dfv * (1.0 - lbh) * sg * (1.0 - sg)).astype(BF16)
                dlb_ref[:, c] += _rowsum(dfv * (1.0 - sg))
                dq_ref[rows, c] = (dqf * _dsilu(qv)).astype(BF16)
            return carry

        lax.fori_loop(0, HG_TILE_CHUNKS, chunk, 0, unroll=True)

    rev = pl.BlockSpec((HG_TILE, HG_WIDTH), lambda i: (n_tiles - 1 - i, 0))
    return pl.pallas_call(
        body, name="hg_bwd", grid=(n_tiles,),
        out_shape=[jax.ShapeDtypeStruct((s, HG_WIDTH), BF16)] * 4
        + [jax.ShapeDtypeStruct((1, HG_DIM), F32), jax.ShapeDtypeStruct((1, HG_WIDTH), F32)],
        in_specs=[rev] * 4 + [_whole((2, HG_WIDTH)), _whole((1, HG_DIM)), rev,
                              pl.BlockSpec((HG_TILE_CHUNKS * HG_DIM, HG_WIDTH), lambda i: (n_tiles - 1 - i, 0)), rev],
        out_specs=[rev] * 4 + [_whole((1, HG_DIM)), _whole((1, HG_WIDTH))],
        scratch_shapes=[pltpu.VMEM((HG_HEADS, HG_DIM, HG_DIM), F32)] + [pltpu.VMEM((HG_TILE, HG_WIDTH), F32)] * 4,
        compiler_params=_params("arbitrary"),
    )(hq, hf, hi, hgt, logits, onorm_g, o, states, dout)


TOKEN_GROUP = 16


def _att_geometry(dil, seq=0):
    per_group = TOKEN_GROUP // dil
    ub = ATT_BLOCK // per_group
    if dil == TOKEN_GROUP:
        n_blocks = 2 if seq % (2 * ub * TOKEN_GROUP) == 0 and seq > 0 else 1
    else:
        n_blocks = 4
    return per_group, ub, ATT_WIDTH if dil == 1 else 128, n_blocks


def _att_consts(dil):
    per_group, ub = _att_geometry(dil)[:2]

    def pos(i):
        return i if dil == 1 else (i % ub) * per_group + i // ub

    lane = lax.broadcasted_iota(jnp.int32, (ATT_BLOCK, 128), 1)
    qi = pos(lax.broadcasted_iota(jnp.int32, (2 * ATT_BLOCK, ATT_BLOCK), 0) % ATT_BLOCK)
    kj = pos(lax.broadcasted_iota(jnp.int32, (2 * ATT_BLOCK, ATT_BLOCK), 1))
    return lane < ATT_HEAD_DIM, kj <= qi, lambda off: kj >= qi + off


def _load_tile(ref, dil, r, c, base=0):
    per_group, ub = _att_geometry(dil)[:2]
    if dil == 1:
        return ref[base:base + ATT_BLOCK, c]
    return jnp.concatenate([ref[pl.ds(base + dil * w + r, ub, stride=TOKEN_GROUP), c] for w in range(per_group)], axis=0)


def _store_tile(ref, dil, r, c, val, base=0):
    per_group, ub = _att_geometry(dil)[:2]
    if dil == 1:
        ref[base:base + ATT_BLOCK, c] = val
        return
    for w in range(per_group):
        ref[pl.ds(base + dil * w + r, ub, stride=TOKEN_GROUP), c] = val[w * ub:(w + 1) * ub]


def _stack_heads(x2, first):
    return jnp.concatenate([jnp.where(first, x2, 0.0), jnp.where(first, 0.0, x2)], axis=0)


def _stack_bcast(x2, first):
    other = pltpu.roll(x2, ATT_HEAD_DIM, axis=1)
    return jnp.concatenate([jnp.where(first, x2, other), jnp.where(first, other, x2)], axis=0)


def _unstack_heads(st, first):
    return jnp.where(first, st[:ATT_BLOCK], st[ATT_BLOCK:])


def _att_fwd(q, k, v, dil):
    seq, width = q.shape
    _, ub, lanes, nbs = _att_geometry(dil, seq)
    rows = ub * TOKEN_GROUP
    n_steps = seq // (nbs * rows)

    def body(q_ref, k_ref, v_ref, kp_ref, vp_ref, o_ref, lse_ref):
        first, cur_ok, _band = _att_consts(dil)
        inner_ok = _band(0)
        edge_ok = _band(jnp.where(pl.program_id(0) > 0, 0, ATT_BLOCK))
        for r in range(dil):
            for j in range(lanes // 128):
                c = slice(j * 128, (j + 1) * 128)
                kc = vc = None
                for b in range(nbs):
                    base = b * rows
                    prev_ok = edge_ok if b == 0 else inner_ok
                    if b == 0:
                        kp, vp = _load_tile(kp_ref, dil, r, c).astype(BF16), _load_tile(vp_ref, dil, r, c).astype(BF16)
                    else:
                        kp, vp = kc, vc
                    qst = _stack_heads(_load_tile(q_ref, dil, r, c, base) * ATT_SCALE, first).astype(BF16)
                    kc = _load_tile(k_ref, dil, r, c, base).astype(BF16)
                    vc = _load_tile(v_ref, dil, r, c, base).astype(BF16)
                    sc = jnp.where(cur_ok, _dot_nt(qst, kc), NEG)
                    sp = jnp.where(prev_ok, _dot_nt(qst, kp), NEG)
                    mx = jnp.max(jnp.maximum(sc, sp), axis=-1, keepdims=True)
                    pc, pp = jnp.exp(sc - mx), jnp.exp(sp - mx)
                    den = jnp.sum(pc + pp, axis=-1, keepdims=True)
                    ost = (_dot(pc.astype(BF16), vc) + _dot(pp.astype(BF16), vp)) / den
                    lse = jnp.broadcast_to(mx + jnp.log(den), (2 * ATT_BLOCK, 128))
                    _store_tile(o_ref, dil, r, c, _unstack_heads(ost, first), base)
                    _store_tile(lse_ref, dil, r, c, _unstack_heads(lse, first), base)

    slab = pl.BlockSpec((nbs * rows, lanes), lambda n, j: (n, j))
    before = pl.BlockSpec((rows, lanes), lambda n, j: (jnp.maximum(n * nbs - 1, 0), j))
    return pl.pallas_call(
        body, name=f"att_fwd_d{dil}", grid=(n_steps, width // lanes),
        out_shape=[jax.ShapeDtypeStruct((seq, width), F32)] * 2,
        in_specs=[slab, slab, slab, before, before], out_specs=[slab, slab],
        compiler_params=_params("arbitrary", "arbitrary"),
    )(q, k, v, k, v)


def _att_bwd(q, k, v, do, cc, lse, dil):
    seq, width = q.shape
    _, ub, lanes, nbs = _att_geometry(dil, seq)
    rows = ub * TOKEN_GROUP
    n_blocks = seq // rows
    n_steps = n_blocks // nbs

    def body(q_ref, k_ref, v_ref, do_ref, cc_ref, lse_ref, qx_ref, dox_ref, ccx_ref, lsex_ref,
             dq_ref, dk_ref, dv_ref, carry):
        first, cur_ok, _band = _att_consts(dil)
        step = pl.program_id(1)
        inner_ok = _band(0)
        edge_ok = _band(jnp.where(step < n_steps - 1, 0, ATT_BLOCK))

        @pl.when(step == 0)
        def _():
            carry[...] = jnp.zeros_like(carry)

        def queries(refs, r, c, base):
            q_r, do_r, lse_r, cc_r = refs
            return (_stack_heads(_load_tile(q_r, dil, r, c, base) * ATT_SCALE, first).astype(BF16),
                    _stack_heads(_load_tile(do_r, dil, r, c, base), first).astype(BF16),
                    _stack_bcast(_load_tile(lse_r, dil, r, c, base), first),
                    _stack_bcast(_load_tile(cc_r, dil, r, c, base), first))

        for r in range(dil):
            for j in range(lanes // 128):
                c = slice(j * 128, (j + 1) * 128)
                own = queries((q_ref, do_ref, lse_ref, cc_ref), r, c, 0)
                left = _load_tile(carry, dil, r, c)
                for b in range(nbs):
                    base = b * rows
                    last = b == nbs - 1
                    next_ok = edge_ok if last else inner_ok
                    if last:
                        following = queries((qx_ref, dox_ref, lsex_ref, ccx_ref), r, c, 0)
                    else:
                        following = queries((q_ref, do_ref, lse_ref, cc_ref), r, c, base + rows)
                    (qst, dost, lse_n, cc_n), (qxst, doxst, lse_x, cc_x) = own, following
                    kb = _load_tile(k_ref, dil, r, c, base).astype(BF16)
                    vb = _load_tile(v_ref, dil, r, c, base).astype(BF16)
                    p_cur = jnp.exp(jnp.where(cur_ok, _dot_nt(qst, kb), NEG) - lse_n)
                    p_next = jnp.exp(jnp.where(next_ok, _dot_nt(qxst, kb), NEG) - lse_x)
                    ds_cur = (p_cur * (_dot_nt(dost, vb) + cc_n)).astype(BF16)
                    ds_next = (p_next * (_dot_nt(doxst, vb) + cc_x)).astype(BF16)
                    dq_own = left + _unstack_heads(_dot(ds_cur, kb), first)
                    _store_tile(dq_ref, dil, r, c, dq_own * ATT_SCALE, base)
                    _store_tile(dk_ref, dil, r, c, _dot_tn(ds_cur, qst) + _dot_tn(ds_next, qxst), base)
                    _store_tile(dv_ref, dil, r, c, _dot_tn(p_cur.astype(BF16), dost) + _dot_tn(p_next.astype(BF16), doxst), base)
                    left = _unstack_heads(_dot(ds_next, kb), first)
                    own = following
                _store_tile(carry, dil, r, c, left)

    slab = pl.BlockSpec((nbs * rows, lanes), lambda j, n: (n, j))
    after = pl.BlockSpec((rows, lanes), lambda j, n: (jnp.minimum((n + 1) * nbs, n_blocks - 1), j))
    return pl.pallas_call(
        body, name=f"att_bwd_d{dil}", grid=(width // lanes, n_steps),
        out_shape=[jax.ShapeDtypeStruct((seq, width), F32)] * 3,
        in_specs=[slab] * 6 + [after] * 4, out_specs=[slab] * 3,
        scratch_shapes=[pltpu.VMEM((rows, lanes), F32)],
        compiler_params=_params("arbitrary", "arbitrary"),
    )(q, k, v, do, cc, lse, q, do, cc, lse)


def _branch_weights(lses):
    mx = jnp.maximum(jnp.maximum(lses[0], lses[1]), lses[2])
    es = [jnp.exp(l - mx) for l in lses]
    inv = 1.0 / (es[0] + es[1] + es[2])
    return [e * inv for e in es]


def _att_combine(outs, lses, att_g, after=()):
    s = outs[0].shape[0]
    tm = 512

    def body(o0, o1, o2, l0, l1, l2, g_ref, att_ref, out_ref):
        ws = _branch_weights([l0[...], l1[...], l2[...]])
        att = ws[0] * o0[...] + ws[1] * o1[...] + ws[2] * o2[...]
        att_ref[...] = att
        ahat, _ = _rms(att)
        out_ref[...] = (ahat * g_ref[...]).astype(BF16)

    tile = _rows(tm, ATT_WIDTH)
    return pl.pallas_call(
        _behind(body, after), name="att_combine", grid=(s // tm,),
        out_shape=[jax.ShapeDtypeStruct((s, ATT_WIDTH), F32), jax.ShapeDtypeStruct((s, ATT_WIDTH), BF16)],
        in_specs=_ANY * len(after) + [tile] * 6 + [_whole((1, ATT_WIDTH))], out_specs=[tile, tile],
        compiler_params=_params("parallel"),
    )(*after, *outs, *lses, att_g)


def _att_combine_bwd(datt_out, att, lses, att_g, after=()):
    s = att.shape[0]
    tm = 256

    def body(d_ref, att_ref, l0, l1, l2, g_ref, do0, do1, do2, cc0, cc1, cc2, dg_ref):
        @pl.when(pl.program_id(0) == 0)
        def _():
            dg_ref[...] = jnp.zeros_like(dg_ref)

        att = att_ref[...]
        ahat, rstd = _rms(att)
        d = d_ref[...]
        dg_ref[...] += _rowsum(d * ahat)
        datt = _rms_bwd(d * g_ref[...], ahat, rstd)
        hi = lax.broadcasted_iota(jnp.int32, (ATT_WIDTH, ATT_WIDTH), 0) // ATT_HEAD_DIM
        hj = lax.broadcasted_iota(jnp.int32, (ATT_WIDTH, ATT_WIDTH), 1) // ATT_HEAD_DIM
        same_head = (hi == hj).astype(BF16)
        prod = datt * att
        prod_hi = prod.astype(BF16)
        prod_lo = (prod - prod_hi.astype(F32)).astype(BF16)
        head_sum = _dot(prod_hi, same_head) + _dot(prod_lo, same_head)
        ws = _branch_weights([l0[...], l1[...], l2[...]])
        for w, do_ref, cc_ref in zip(ws, (do0, do1, do2), (cc0, cc1, cc2)):
            do_ref[...] = w * datt
            cc_ref[...] = -w * head_sum

    tile = _rows(tm, ATT_WIDTH)
    return pl.pallas_call(
        _behind(body, after), name="att_combine_bwd", grid=(s // tm,),
        out_shape=[jax.ShapeDtypeStruct((s, ATT_WIDTH), F32)] * 6 + [jax.ShapeDtypeStruct((1, ATT_WIDTH), F32)],
        in_specs=_ANY * len(after) + [tile] * 5 + [_whole((1, ATT_WIDTH))], out_specs=[tile] * 6 + [_whole((1, ATT_WIDTH))],
        compiler_params=_params("arbitrary"),
    )(*after, datt_out, att, *lses, att_g)


def _out_fwd(x, hg, at, mod, w_out):
    s = x.shape[0]
    tm = 512

    def body(x_ref, hg_ref, at_ref, mod_ref, w_ref, x1_ref):
        mix = _dot(hg_ref[...], w_ref[0:512, :]) + _dot(at_ref[...], w_ref[512:1024, :])
        x1_ref[...] = x_ref[...] + mod_ref[:, 2 * D_MODEL:3 * D_MODEL] * mix

    return pl.pallas_call(
        body, name="out_fwd", grid=(s // tm,), out_shape=jax.ShapeDtypeStruct((s, D_MODEL), F32),
        in_specs=[_rows(tm, D_MODEL), _rows(tm, 512), _rows(tm, 512), _whole((1, 6 * D_MODEL)), _whole((D_MODEL, D_MODEL))],
        out_specs=_rows(tm, D_MODEL), compiler_params=_params("parallel"),
    )(x, hg, at, mod, w_out)


def _out_bwd(dx1, hg, at, mod, w_out):
    s = dx1.shape[0]
    tm = 512
    n_steps = s // tm

    def body(dx_ref, hg_ref, at_ref, mod_ref, w_ref, dhg_ref, dat_ref, dw_ref, dwb_ref, dgate_ref):
        @pl.when(pl.program_id(0) == 0)
        def _():
            dw_ref[...] = jnp.zeros_like(dw_ref)
            dgate_ref[...] = jnp.zeros_like(dgate_ref)

        hg, at, dx = hg_ref[...], at_ref[...], dx_ref[...]
        mix = _dot(hg, w_ref[0:512, :]) + _dot(at, w_ref[512:1024, :])
        dgate_ref[...] += _rowsum(dx * mix)
        dmix = (mod_ref[:, 2 * D_MODEL:3 * D_MODEL] * dx).astype(BF16)
        dhg_ref[...] = _dot_nt(dmix, w_ref[0:512, :])
        dat_ref[...] = _dot_nt(dmix, w_ref[512:1024, :])
        dw_ref[0:512, :] += _dot_tn(hg, dmix)
        dw_ref[512:1024, :] += _dot_tn(at, dmix)

        @pl.when(pl.program_id(0) == n_steps - 1)
        def _():
            dwb_ref[...] = dw_ref[...].astype(BF16)

    return pl.pallas_call(
        body, name="out_bwd", grid=(n_steps,),
        out_shape=[jax.ShapeDtypeStruct((s, 512), F32)] * 2
        + [jax.ShapeDtypeStruct((D_MODEL, D_MODEL), F32), jax.ShapeDtypeStruct((D_MODEL, D_MODEL), BF16),
           jax.ShapeDtypeStruct((1, D_MODEL), F32)],
        in_specs=[_rows(tm, D_MODEL), _rows(tm, 512), _rows(tm, 512), _whole((1, 6 * D_MODEL)), _whole((D_MODEL, D_MODEL))],
        out_specs=[_rows(tm, 512), _rows(tm, 512), _whole((D_MODEL, D_MODEL)), _whole((D_MODEL, D_MODEL)), _whole((1, D_MODEL))],
        compiler_params=_params("arbitrary"),
    )(dx1, hg, at, mod, w_out)


def _ffn(x1, target, mod, g2, gf, w_gu, w_down):
    s = x1.shape[0]
    tm = 256

    def body(x_ref, t_ref, mod_ref, g2_ref, gf_ref, wgu_hbm, wd_hbm,
             dx_ref, h2_ref, act_ref, dau_ref, dff_ref, sums_ref, loss_ref, wgu, wd, au_s, sem):
        @pl.when(pl.program_id(0) == 0)
        def _():
            c1 = pltpu.make_async_copy(wgu_hbm, wgu, sem.at[0])
            c2 = pltpu.make_async_copy(wd_hbm, wd, sem.at[1])
            c1.start()
            c2.start()
            c1.wait()
            c2.wait()
            sums_ref[...] = jnp.zeros_like(sums_ref)
            loss_ref[...] = jnp.zeros_like(loss_ref)

        x1v = x_ref[...]
        xhat, rstd = _rms(x1v)
        g2 = g2_ref[...]
        n2 = xhat * g2
        scale2 = 1.0 + mod_ref[:, 4 * D_MODEL:5 * D_MODEL]
        gate2 = mod_ref[:, 5 * D_MODEL:6 * D_MODEL]
        hb = (n2 * scale2 + mod_ref[:, 3 * D_MODEL:4 * D_MODEL]).astype(BF16)
        h2_ref[...] = hb
        au_s[...] = _dot_nt(hb, wgu[...])
        a = au_s[:, 0:D_FF]
        act = (_silu(a) * au_s[:, D_FF:2 * D_FF]).astype(BF16)
        act_ref[...] = act
        ff = _dot(act, wd[...])
        x2 = x1v + gate2 * ff
        nf, rstd_f = _rms(x2)
        gfv = gf_ref[...]
        err = nf * gfv - t_ref[...]
        loss_ref[...] += 0.5 * jnp.sum(_rowsum(err * err), axis=-1, keepdims=True) * (1.0 / D_MODEL)
        dy = err * (1.0 / D_MODEL)
        dx2 = _rms_bwd(dy * gfv, nf, rstd_f)
        dffb = (gate2 * dx2).astype(BF16)
        dff_ref[...] = dffb
        dact = _dot_nt(dffb, wd[...])
        a = au_s[:, 0:D_FF]
        dau_ref[:, 0:D_FF] = (dact * au_s[:, D_FF:2 * D_FF] * _dsilu(a)).astype(BF16)
        dau_ref[:, D_FF:2 * D_FF] = (dact * _silu(a)).astype(BF16)
        dh = _dot(dau_ref[...], wgu[...])
        dn = dh * scale2
        sums_ref[0:1, :] += _rowsum(dh)
        sums_ref[1:2, :] += _rowsum(dh * n2)
        sums_ref[2:3, :] += _rowsum(dx2 * ff)
        sums_ref[3:4, :] += _rowsum(dn * xhat)
        sums_ref[4:5, :] += _rowsum(dy * nf)
        dx_ref[...] = dx2 + _rms_bwd(dn * g2, xhat, rstd)

    vec = _whole((1, D_MODEL))
    hbm = pl.BlockSpec(memory_space=pl.ANY)
    return pl.pallas_call(
        body, name="ffn", grid=(s // tm,),
        out_shape=[jax.ShapeDtypeStruct((s, D_MODEL), F32), jax.ShapeDtypeStruct((s, D_MODEL), BF16),
                   jax.ShapeDtypeStruct((s, D_FF), BF16), jax.ShapeDtypeStruct((s, 2 * D_FF), BF16),
                   jax.ShapeDtypeStruct((s, D_MODEL), BF16), jax.ShapeDtypeStruct((8, D_MODEL), F32),
                   jax.ShapeDtypeStruct((1, 128), F32)],
        in_specs=[_rows(tm, D_MODEL), _rows(tm, D_MODEL), _whole((1, 6 * D_MODEL)), vec, vec, hbm, hbm],
        out_specs=[_rows(tm, D_MODEL), _rows(tm, D_MODEL), _rows(tm, D_FF), _rows(tm, 2 * D_FF), _rows(tm, D_MODEL),
                   _whole((8, D_MODEL)), _whole((1, 128))],
        scratch_shapes=[pltpu.VMEM((2 * D_FF, D_MODEL), BF16), pltpu.VMEM((D_FF, D_MODEL), BF16),
                        pltpu.VMEM((tm, 2 * D_FF), F32), pltpu.SemaphoreType.DMA((2,))],
        compiler_params=_params("arbitrary"),
    )(x1, target, mod, g2, gf, w_gu, w_down)


def _weight_grad(a, b, name, rounded=False):
    s, m = a.shape
    n = b.shape[1]
    ts = min(s, 2048)
    n_steps = s // ts
    tm = max(t for t in range(128, m + 1, 128) if m % t == 0 and t * n * 4 <= 6 * 1024 * 1024)

    def body(a_ref, b_ref, o_ref, *ob_ref):
        @pl.when(pl.program_id(1) == 0)
        def _():
            o_ref[...] = jnp.zeros_like(o_ref)

        o_ref[...] += _dot_tn(a_ref[...], b_ref[...])
        if rounded:
            @pl.when(pl.program_id(1) == n_steps - 1)
            def _():
                ob_ref[0][...] = o_ref[...].astype(BF16)

    tile = pl.BlockSpec((tm, n), lambda j, i: (j, 0))
    return pl.pallas_call(
        body, name=name, grid=(m // tm, n_steps),
        out_shape=[jax.ShapeDtypeStruct((m, n), F32)] + [jax.ShapeDtypeStruct((m, n), BF16)] * rounded,
        in_specs=[pl.BlockSpec((ts, tm), lambda j, i: (i, j)), pl.BlockSpec((ts, n), lambda j, i: (i, 0))],
        out_specs=[tile] + [tile] * rounded,
        compiler_params=_params("parallel", "arbitrary"),
    )(a, b)


def _adamw_math(w, g, m, v):
    m = ADAM_B1 * m + (1.0 - ADAM_B1) * g
    v = ADAM_B2 * v + (1.0 - ADAM_B2) * (g * g)
    m_hat = m / (1.0 - ADAM_B1 ** ADAM_STEP)
    v_hat = v / (1.0 - ADAM_B2 ** ADAM_STEP)
    delta = -ADAM_LR * (m_hat / (jnp.sqrt(v_hat) + ADAM_EPS) + ADAM_WD * w)
    return delta, m, v


def _adamw_shard(where, w, m, v, partial, got, name):
    r, c = w.shape
    tr = _shard_rows(r)
    lead = partial.ndim - 2
    n_got = got.shape[0]

    def body(where_ref, w_ref, m_ref, v_ref, own_ref, *rest):
        got_refs, (grad_ref, d_ref, nm_ref, nv_ref) = rest[:n_got], rest[n_got:]
        g = own_ref[...]
        for g_ref in got_refs:
            g = g + g_ref[...].astype(F32)
        grad_ref[...] = g
        d_ref[...], nm_ref[...], nv_ref[...] = _adamw_math(w_ref[...], g, m_ref[...], v_ref[...])

    tile = pl.BlockSpec((tr, c), lambda i, where_ref: (i, 0))
    own = pl.BlockSpec((None,) * lead + (tr, c), lambda i, where_ref: (*[where_ref[d] for d in range(lead)], i, 0))
    part = [pl.BlockSpec((None, tr, c), functools.partial(lambda j, i, where_ref: (j, i, 0), j)) for j in range(n_got)]
    return pl.pallas_call(
        body, name=name,
        grid_spec=pltpu.PrefetchScalarGridSpec(num_scalar_prefetch=1, grid=(r // tr,), in_specs=[tile] * 3 + [own] + part,
                                               out_specs=[tile] * 4),
        out_shape=[jax.ShapeDtypeStruct((r, c), F32)] * 4, compiler_params=_params("parallel"),
    )(where, w, m, v, partial, *[got] * n_got)


def _small_update(small_all, dmod_blocks, c_all, logits, w_ada, m_ada, v_ada, smalls, after=()):
    def body(sm_ref, dm_ref, c_ref, lg_ref, wa_ref, ma_ref, va_ref, *rest):
        ins, outs = rest[:21], rest[21:]
        _, me = _flip(0)
        tot = sm_ref[0:1, :]
        for i in range(1, N_DEV):
            tot = tot + sm_ref[i:i + 1, :]
        loss_ref = outs[0]
        loss_ref[...] = tot[:, SM_LOSS:SM_LOSS + 128]
        g_ada = lax.dot_general(_silu(c_ref[...]), dm_ref[me], (((0,), (0,)), ((), ())),
                                preferred_element_type=F32, precision=HIGHEST)
        outs[1][...] = g_ada
        outs[2][...], outs[3][...], outs[4][...] = _adamw_math(wa_ref[...], g_ada, ma_ref[...], va_ref[...])
        p0 = _lower_bound(lg_ref)
        dl0 = tot[:, SM_LB:SM_LB + 512] * p0 * (1.0 - p0)
        grads = [tot[:, SM_MOD:SM_MOD + 6 * D_MODEL], tot[:, SM_G1:SM_G1 + D_MODEL], tot[:, SM_G2:SM_G2 + D_MODEL],
                 tot[:, SM_GF:SM_GF + D_MODEL], tot[:, SM_ATT:SM_ATT + 512], tot[:, SM_HG:SM_HG + 128],
                 jnp.where(lax.broadcasted_iota(jnp.int32, (2, 512), 0) == 0, dl0, -dl0)]
        for i, g in enumerate(grads):
            w_ref, m_ref, v_ref = ins[3 * i:3 * i + 3]
            o = outs[5 + 4 * i:9 + 4 * i]
            o[0][...] = g
            o[1][...], o[2][...], o[3][...] = _adamw_math(w_ref[...], g, m_ref[...], v_ref[...])

    flat = [t for trio in smalls for t in trio]
    vm = pl.BlockSpec(memory_space=pltpu.VMEM)
    out_shape = [jax.ShapeDtypeStruct((1, 128), F32)] + [jax.ShapeDtypeStruct(w_ada.shape, F32)] * 4
    for trio in smalls:
        out_shape += [jax.ShapeDtypeStruct(trio[0].shape, F32)] * 4
    return pl.pallas_call(
        _behind(body, after), name="small_update", out_shape=out_shape,
        in_specs=_ANY * len(after) + [vm] * (7 + len(flat)), out_specs=[vm] * len(out_shape),
        compiler_params=pltpu.CompilerParams(vmem_limit_bytes=V7X_VMEM_LIMIT),
    )(*after, small_all, dmod_blocks, c_all, logits, w_ada, m_ada, v_ada, *flat)


def kernel(x, c, w_ada, b_ada, norm1_g, w_in, hg_lb_logits, hg_onorm_g, att_onorm_g, w_out, norm2_g, w_gate_up, w_down, final_g, loss_target, m_w_ada, m_b_ada, m_norm1_g, m_w_in, m_hg_lb_logits, m_hg_onorm_g, m_att_onorm_g, m_w_out, m_norm2_g, m_w_gate_up, m_w_down, m_final_g, v_w_ada, v_b_ada, v_norm1_g, v_w_in, v_hg_lb_logits, v_hg_onorm_g, v_att_onorm_g, v_w_out, v_norm2_g, v_w_gate_up, v_w_down, v_final_g):
    x2d, target = x[0], loss_target[0]
    seq = x2d.shape[0]
    assert seq % (ATT_BLOCK * max(DILATIONS)) == 0 and seq % HG_TILE == 0
    gf = final_g.reshape(1, D_MODEL)

    c_all = _exchange_small(c.reshape(8, D_MODEL // 8), None, "gather_c").reshape(N_DEV, D_MODEL)
    ada = _ada_rows(c_all, w_ada[0], b_ada)
    mod = _exchange_small(ada, 1, "scatter_mod").reshape(1, 6 * D_MODEL)

    core = lax.axis_index("c").astype(jnp.int32).reshape(1)
    chip = (2 * lax.axis_index("x") + lax.axis_index("y")).astype(jnp.int32).reshape(1)
    me = 4 * lax.axis_index("x") + 2 * lax.axis_index("y") + lax.axis_index("c")

    g_in, = _gather_weights([w_in[0].T.astype(BF16)])
    w_in_b = g_in.reshape(IN_WIDTH, D_MODEL)
    rest_shards = [w_out[0].astype(BF16), w_gate_up[0].T.astype(BF16), w_down[0].astype(BF16)]
    lands = [lax.empty((N_DEV,) + s.shape, BF16) for s in rest_shards]
    g_send, g_recv, g_srcs, g_lands, tok = _copies_start("gather_rest_start", _plan_gather_own, 12, rest_shards, lands, [w_in_b, mod])
    flight = {}

    def stage(name, *vals):
        if name == "attention_begun":
            flight["shards"], got = _copies_wait("gather_rest_wait", _plan_gather_own, g_send, g_recv, g_srcs, g_lands, list(vals))
            flight["pass"] = _copies_start("gather_pass_start", _plan_gather_pass, 9, [], got, [])
            return [flight["pass"][4]]
        if name == "mixer_weights_done":
            shapes = [(4, 2, D_MODEL // N_DEV, D_MODEL), (4, 2, 2 * D_FF // N_DEV, D_MODEL), (4, 2, D_FF // N_DEV, D_MODEL)]
            flight["grads"] = [g32.reshape(sh) for (g32, _), sh in zip(vals, shapes)]
            rounded = [g16.reshape(sh) for (_, g16), sh in zip(vals, shapes)]
            direct_lands = [lax.empty((N_DEV - 1,) + sh[2:], BF16) for sh in shapes]
            flight["direct"] = _copies_start("reduce_rest_start", _plan_reduce_direct, 21, rounded, direct_lands, [])
            return [flight["direct"][4]]
        raise ValueError(name)

    def rest_weights(after):
        s, r, _, p_lands, _ = flight["pass"]
        _, got = _copies_wait("gather_pass_wait", _plan_gather_pass, s, r, [], p_lands, [after])
        full = [lax.dynamic_update_index_in_dim(g, shard, me, 0) for g, shard in zip(got, flight["shards"])]
        return full[0].reshape(D_MODEL, D_MODEL), full[1].reshape(2 * D_FF, D_MODEL), full[2].reshape(D_FF, D_MODEL)

    grad_x, dw_in, small = _block_step(x2d, target, mod, norm1_g, hg_lb_logits, hg_onorm_g, att_onorm_g, norm2_g, gf,
                                       w_in_b, rest_weights, stage, [tok])

    g_in8 = dw_in.reshape(4, 2, IN_WIDTH // N_DEV, D_MODEL)
    in_pairs = _copies_start("reduce_pairs_in_start", _plan_reduce_pairs, 4, [g_in8], [lax.empty((4,) + g_in8.shape[2:], F32)], [])
    s, r, srcs, d_lands, _ = flight["direct"]
    _, recv_rest = _copies_wait("reduce_rest_wait", _plan_reduce_direct, s, r, srcs, d_lands, [in_pairs[4]])
    small_rows = jnp.pad(small, ((0, 0), (0, SM_PADDED - SM_WIDTH))).reshape(SM_PADDED // 128, 128)
    small_all = _exchange_small(small_rows, None, "gather_small", [in_pairs[4]]).reshape(N_DEV, SM_PADDED)[:, :SM_WIDTH]
    in_grads, got_in = _copies_wait("reduce_pairs_in_wait", _plan_reduce_pairs, in_pairs[0], in_pairs[1], in_pairs[2], in_pairs[3],
                                    [small_all])
    in_s32, in_s16 = _pair_sum(core, in_grads[0], got_in[0], "pair_sum_in")
    in_chips = _copies_start("reduce_chips_in_start", _plan_reduce_chips, 3, [in_s16], [lax.empty((3,) + in_s16.shape[1:], BF16)], [])
    big, updated = {}, []
    rest_params = [("w_out", w_out, m_w_out, v_w_out), ("w_gate_up", w_gate_up, m_w_gate_up, v_w_gate_up), ("w_down", w_down, m_w_down, v_w_down)]
    mine = jnp.concatenate([chip, core])
    for (n, w, m, v), g32, got in zip(rest_params, flight["grads"], recv_rest):
        if n == "w_gate_up":
            outs4 = _adamw_shard(mine, w[0].T, m[0].T, v[0].T, g32, got, f"adamw_{n}")
            big[n] = [t.T[None] for t in outs4]
        else:
            outs4 = _adamw_shard(mine, w[0], m[0], v[0], g32, got, f"adamw_{n}")
            big[n] = [t[None] for t in outs4]
        updated.append(outs4[3])
    smalls = [(b_ada, m_b_ada, v_b_ada), (norm1_g, m_norm1_g, v_norm1_g), (norm2_g, m_norm2_g, v_norm2_g),
              (gf, m_final_g.reshape(1, D_MODEL), v_final_g.reshape(1, D_MODEL)),
              (att_onorm_g, m_att_onorm_g, v_att_onorm_g), (hg_onorm_g, m_hg_onorm_g, v_hg_onorm_g),
              (hg_lb_logits, m_hg_lb_logits, v_hg_lb_logits)]
    dmod_blocks = small_all[:, :6 * D_MODEL].reshape(N_DEV, N_DEV, 6 * D_MODEL // N_DEV).transpose(1, 0, 2)
    res = _small_update(small_all, dmod_blocks, c_all, hg_lb_logits, w_ada[0], m_w_ada[0], v_w_ada[0], smalls, [in_chips[4]])
    _, recv_in = _copies_wait("reduce_chips_in_wait", _plan_reduce_chips, in_chips[0], in_chips[1], in_chips[2], in_chips[3],
                              [res[0]] + updated)
    big["w_in"] = [t.T[None] for t in _adamw_shard(chip, w_in[0].T, m_w_in[0].T, v_w_in[0].T, in_s32, recv_in[0], "adamw_w_in")]
    loss = res[0][0, 0]
    ada4 = [t[None] for t in res[1:5]]
    sm4 = {n: list(res[5 + 4 * i:9 + 4 * i]) for i, n in enumerate(["b_ada", "norm1_g", "norm2_g", "final_g", "att", "hg", "lb"])}
    sm4["final_g"] = [t.reshape(D_MODEL) for t in sm4["final_g"]]

    order = [ada4, sm4["b_ada"], sm4["norm1_g"], big["w_in"], sm4["lb"], sm4["hg"], sm4["att"], big["w_out"], sm4["norm2_g"],
             big["w_gate_up"], big["w_down"], sm4["final_g"]]
    return (loss, grad_x[None], *[o[0] for o in order], *[o[1] for o in order], *[o[2] for o in order], *[o[3] for o in order])


def _block_step(x2d, target, mod, norm1_g, hg_lb_logits, hg_onorm_g, att_onorm_g, norm2_g, gf, w_in_b, rest_weights, stage,
                after=()):
    h1, hq, hf, hi, hgt, aq, ak, av = _in_fwd(x2d, mod, norm1_g, w_in_b, after)
    hg_out, hg_o, hg_states = _hg_fwd(hq, hf, hi, hgt, hg_lb_logits, hg_onorm_g)
    branch = [_att_fwd(aq, ak, av, d) for d in DILATIONS[:2]]
    behind = stage("attention_begun", branch[0][0], branch[1][0])
    branch += [_att_fwd(aq, ak, av, d) for d in DILATIONS[2:]]
    outs = [b[0] for b in branch]
    lses = [b[1] for b in branch]
    att, att_out = _att_combine(outs, lses, att_onorm_g, behind)
    w_out_b, w_gu_b, w_down_b = rest_weights(att_out)
    x1 = _out_fwd(x2d, hg_out, att_out, mod, w_out_b)

    dx1, h2, act, dau, dff, ffn_sums, loss_part = _ffn(x1, target, mod, norm2_g, gf, w_gu_b, w_down_b)
    dw_gu = _weight_grad(dau, h2, "dw_gate_up", rounded=True)
    dw_down = _weight_grad(act, dff, "dw_down", rounded=True)

    dhg, dat, dw_out, dw_out_b, dgate1 = _out_bwd(dx1, hg_out, att_out, mod, w_out_b)
    behind = stage("mixer_weights_done", (dw_out, dw_out_b), dw_gu, dw_down)
    comb = _att_combine_bwd(dat, att, lses, att_onorm_g, behind)
    dos, ccs, d_att_g = comb[0:3], comb[3:6], comb[6]
    datt = []
    for i, d in enumerate(DILATIONS):
        datt.append(_att_bwd(aq, ak, av, dos[i], ccs[i], lses[i], d))
    dhq, dhf, dhi, dhgt, d_hg_g, d_lb = _hg_bwd(hq, hf, hi, hgt, hg_lb_logits, hg_onorm_g, hg_o, hg_states, dhg)
    dps = [dhq, dhf, dhi, dhgt] + [datt[i][j] for j in range(3) for i in range(3)]
    grad_x, dp_b, dshift1, dscale1, d_g1 = _in_bwd(x2d, dx1, mod, norm1_g, w_in_b, dps)
    dw_in, = _weight_grad(dp_b, h1, "dw_in")
    small = jnp.concatenate([dshift1, dscale1, dgate1, ffn_sums[0:1], ffn_sums[1:2], ffn_sums[2:3], d_g1, ffn_sums[3:4],
                             ffn_sums[4:5], d_att_g, d_lb, d_hg_g, loss_part], axis=1)
    return grad_x, dw_in, small
```

```python
import functools

import jax
import jax.numpy as jnp
from jax import lax
from jax.experimental import pallas as pl
from jax.experimental.pallas import tpu as pltpu

F32 = jnp.float32
BF16 = jnp.bfloat16
HIGHEST = lax.Precision.HIGHEST
MESH = pl.DeviceIdType.MESH

D_MODEL = 1024
N_DEV = 8
HG_HEADS = 4
HG_DIM = 128
HG_WIDTH = HG_HEADS * HG_DIM
HG_CHUNK = 128
ATT_WIDTH = 512
ATT_HEAD_DIM = 64
ATT_BLOCK = 128
DILATIONS = (1, 4, 16)
ATT_SCALE = ATT_HEAD_DIM ** -0.5
D_FF = 2816
IN_WIDTH = 7 * 512
RMS_EPS = 1e-6
NEG = -1e30

ADAM_LR = 0.001
ADAM_B1 = 0.9
ADAM_B2 = 0.999
ADAM_EPS = 1e-08
ADAM_WD = 0.01
ADAM_STEP = 10

V7X_VMEM_LIMIT = 56 * 1024 * 1024

SM_MOD = 0
SM_G1 = 6 * D_MODEL
SM_G2 = 7 * D_MODEL
SM_GF = 8 * D_MODEL
SM_ATT = 9 * D_MODEL
SM_LB = 9 * D_MODEL + 512
SM_HG = 10 * D_MODEL
SM_LOSS = 10 * D_MODEL + 128
SM_WIDTH = 10 * D_MODEL + 256
SM_PADDED = 88 * 128


def _params(*sem, vmem=V7X_VMEM_LIMIT):
    return pltpu.CompilerParams(dimension_semantics=sem, vmem_limit_bytes=vmem)


def _dot(a, b):
    return jnp.dot(a, b, preferred_element_type=F32)


def _dot_nt(a, b):
    return lax.dot_general(a, b, (((1,), (1,)), ((), ())), preferred_element_type=F32)


def _dot_tn(a, b):
    return lax.dot_general(a, b, (((0,), (0,)), ((), ())), preferred_element_type=F32)


def _dot_f32(a, b):
    return jnp.dot(a, b, preferred_element_type=F32, precision=HIGHEST)


def _sigmoid(x):
    return 1.0 / (1.0 + jnp.exp(-x))


def _silu(x):
    return x * _sigmoid(x)


def _dsilu(x):
    s = _sigmoid(x)
    return s * (1.0 + x * (1.0 - s))


def _rms(x):
    rstd = lax.rsqrt(jnp.mean(x * x, axis=-1, keepdims=True) + RMS_EPS)
    return x * rstd, rstd


def _rms_bwd(dn, xhat, rstd):
    return rstd * (dn - xhat * jnp.mean(dn * xhat, axis=-1, keepdims=True))


def _rowsum(x):
    return jnp.sum(x, axis=0, keepdims=True)


def _rows(tm, n):
    return pl.BlockSpec((tm, n), lambda i: (i, 0))


def _whole(shape):
    return pl.BlockSpec(shape, lambda i: (0,) * len(shape))


def _mesh_pos():
    return lax.axis_index("x"), lax.axis_index("y"), lax.axis_index("c")


def _flip(k):
    x, y, c = _mesh_pos()
    px = 1 - x if k & 4 else x
    py = 1 - y if k & 2 else y
    pc = 1 - c if k & 1 else c
    return (px, py, pc), 4 * px + 2 * py + pc


_ANY = [pl.BlockSpec(memory_space=pl.ANY)]


def _behind(body, after):
    return lambda *refs: body(*refs[len(after):])


def _exchange_small(x, rows_per_peer, name, after=()):
    r_all, cols = x.shape
    r_out = r_all if rows_per_peer is None else rows_per_peer

    def body(x_ref, out_ref, send_sems, recv_sems):
        _, me = _flip(0)

        def src(pid):
            if rows_per_peer is None:
                return x_ref
            return x_ref.at[pl.ds(pl.multiple_of(pid * r_out, r_out), r_out), :]

        if rows_per_peer is None:
            out_ref[me] = x_ref[...]
        else:
            out_ref[me] = x_ref[pl.ds(pl.multiple_of(me * r_out, r_out), r_out), :]
        sends = []
        for k in range(1, N_DEV):
            dev, pid = _flip(k)
            cp = pltpu.make_async_remote_copy(src_ref=src(pid), dst_ref=out_ref.at[me], send_sem=send_sems.at[k - 1],
                                              recv_sem=recv_sems.at[k - 1], device_id=dev, device_id_type=MESH)
            cp.start()
            sends.append(cp)
        for k in range(1, N_DEV):
            dev, pid = _flip(k)
            pltpu.make_async_remote_copy(src_ref=src(pid), dst_ref=out_ref.at[pid], send_sem=send_sems.at[k - 1],
                                         recv_sem=recv_sems.at[k - 1], device_id=dev, device_id_type=MESH).wait_recv()
        for cp in sends:
            cp.wait_send()

    return pl.pallas_call(
        _behind(body, after), name=name,
        out_shape=jax.ShapeDtypeStruct((N_DEV, r_out, cols), x.dtype),
        in_specs=_ANY * len(after) + [pl.BlockSpec(memory_space=pltpu.VMEM)],
        out_specs=pl.BlockSpec(memory_space=pltpu.VMEM),
        scratch_shapes=[pltpu.SemaphoreType.DMA((N_DEV - 1,)), pltpu.SemaphoreType.DMA((N_DEV - 1,))],
    )(*after, x)


def _gather_weights(shards):
    n = len(shards)

    def body(*refs):
        xs, outs = refs[:n], refs[n:2 * n]
        send_sems, recv_sems, local_sems = refs[2 * n:]
        x, y, c = _mesh_pos()
        me, sibling = (x, y, c), (x, y, 1 - c)
        chips = [(1 - x, y), (x, 1 - y), (1 - x, 1 - y)]

        def blk(a, px, py, pc):
            return outs[a].at[4 * px + 2 * py + pc]

        def copy(a, k, block, to, src=None):
            return pltpu.make_async_remote_copy(
                src_ref=blk(a, *block) if src is None else src, dst_ref=blk(a, *block),
                send_sem=send_sems.at[a * 7 + k], recv_sem=recv_sems.at[a * 7 + k], device_id=to, device_id_type=MESH)

        mine = [pltpu.make_async_copy(xs[a], blk(a, *me), local_sems.at[a]) for a in range(n)]
        for cp in mine:
            cp.start()
        first = []
        for a in range(n):
            first.append(copy(a, 0, me, sibling, src=xs[a]))
            first += [copy(a, 1 + j, me, (*chip, c), src=xs[a]) for j, chip in enumerate(chips)]
        for cp in first:
            cp.start()
        passed = []
        for j, chip in enumerate(chips):
            for a in range(n):
                copy(a, 1 + j, (*chip, c), me).wait_recv()
                cp = copy(a, 4 + j, (*chip, c), sibling)
                cp.start()
                passed.append(cp)
        for a in range(n):
            copy(a, 0, sibling, me).wait_recv()
            for j, chip in enumerate(chips):
                copy(a, 4 + j, (*chip, 1 - c), me).wait_recv()
        for cp in first + passed:
            cp.wait_send()
        for cp in mine:
            cp.wait()

    hbm = pl.BlockSpec(memory_space=pl.ANY)
    return pl.pallas_call(
        body, name="gather_weights",
        out_shape=[jax.ShapeDtypeStruct((N_DEV,) + s.shape, s.dtype) for s in shards],
        in_specs=[hbm] * n, out_specs=[hbm] * n,
        scratch_shapes=[pltpu.SemaphoreType.DMA((7 * n,)), pltpu.SemaphoreType.DMA((7 * n,)), pltpu.SemaphoreType.DMA((n,))],
    )(*shards)


_HBM = pl.BlockSpec(memory_space=pltpu.HBM)
_SEM = pl.BlockSpec(memory_space=pltpu.SEMAPHORE)
_DATAFLOW = pltpu.SideEffectType.DATAFLOW_SIDE_EFFECTING


def _copies_start(name, plan, n_copies, srcs, lands, after):
    bufs = list(srcs) + list(lands)
    nb = len(bufs)

    def body(*refs):
        ins, send_sems, recv_sems, token = refs[:nb], refs[nb + len(after)], refs[nb + len(after) + 1], refs[-1]
        for i, (src, dst, dev) in enumerate(plan(ins[:len(srcs)], ins[len(srcs):])):
            pltpu.make_async_remote_copy(src_ref=src, dst_ref=dst, send_sem=send_sems.at[i], recv_sem=recv_sems.at[i],
                                         device_id=dev, device_id_type=MESH).start()
        token[...] = jnp.zeros_like(token)

    outs = pl.pallas_call(
        body, name=name,
        out_shape=(pltpu.SemaphoreType.DMA((n_copies,)), pltpu.SemaphoreType.DMA((n_copies,)),
                   *[pltpu.HBM(b.shape, b.dtype) for b in bufs], jax.ShapeDtypeStruct((8, 128), F32)),
        in_specs=[_HBM] * nb + [pl.BlockSpec(memory_space=pl.ANY)] * len(after),
        out_specs=(_SEM, _SEM, *[_HBM] * nb, pl.BlockSpec(memory_space=pltpu.VMEM)),
        input_output_aliases={i: 2 + i for i in range(nb)},
        compiler_params=pltpu.CompilerParams(has_side_effects=_DATAFLOW),
    )(*[pltpu.with_memory_space_constraint(b, pltpu.HBM) for b in bufs], *after)
    return outs[0], outs[1], list(outs[2:2 + len(srcs)]), list(outs[2 + len(srcs):2 + nb]), outs[-1]


def _copies_wait(name, plan, send_sems, recv_sems, srcs, lands, after):
    bufs = list(srcs) + list(lands)
    nb = len(bufs)

    def body(*refs):
        ins, send_ref, recv_ref = refs[:nb], refs[nb], refs[nb + 1]
        for i, (src, dst, dev) in enumerate(plan(ins[:len(srcs)], ins[len(srcs):])):
            cp = pltpu.make_async_remote_copy(src_ref=src, dst_ref=dst, send_sem=send_ref.at[i], recv_sem=recv_ref.at[i],
                                              device_id=dev, device_id_type=MESH)
            cp.wait_send()
            cp.wait_recv()

    outs = pl.pallas_call(
        body, name=name, out_shape=[pltpu.HBM(b.shape, b.dtype) for b in bufs],
        in_specs=[_HBM] * nb + [_SEM, _SEM] + [pl.BlockSpec(memory_space=pl.ANY)] * len(after), out_specs=[_HBM] * nb,
        input_output_aliases={i: i for i in range(nb)},
        compiler_params=pltpu.CompilerParams(has_side_effects=_DATAFLOW),
    )(*bufs, send_sems, recv_sems, *after)
    return list(outs[:len(srcs)]), list(outs[len(srcs):])


def _plan_gather_own(srcs, lands):
    _, me = _flip(0)
    return [(srcs[a], lands[a].at[me], _flip(k)[0]) for a in range(len(srcs)) for k in (1, 4, 2, 6)]


def _plan_gather_pass(srcs, lands):
    sibling = _flip(1)[0]
    plan = []
    for land in lands:
        for k in (4, 2, 6):
            block = land.at[_flip(k)[1]]
            plan.append((block, block, sibling))
    return plan


def _plan_reduce_pairs(srcs, lands):
    x, y, c = _mesh_pos()
    return [(srcs[a].at[chip, 1 - c], lands[a].at[chip], (x, y, 1 - c)) for a in range(len(srcs)) for chip in range(4)]


def _plan_reduce_chips(srcs, lands):
    plan = []
    for a in range(len(srcs)):
        for j, k in enumerate((4, 2, 6)):
            dev = _flip(k)[0]
            plan.append((srcs[a].at[2 * dev[0] + dev[1]], lands[a].at[j], dev))
    return plan


def _plan_reduce_direct(srcs, lands):
    plan = []
    for a in range(len(srcs)):
        for k in range(1, N_DEV):
            dev = _flip(k)[0]
            plan.append((srcs[a].at[2 * dev[0] + dev[1], dev[2]], lands[a].at[k - 1], dev))
    return plan


def _shard_rows(r):
    return r // 2 if r % 32 == 0 else r


def _pair_sum(core, grads, got, name):
    _, _, r, c = grads.shape
    tr = _shard_rows(r)

    def body(core_ref, a_ref, b_ref, o_ref, ob_ref):
        s = a_ref[...] + b_ref[...]
        o_ref[...] = s
        ob_ref[...] = s.astype(BF16)

    spec = pl.BlockSpec((None, tr, c), lambda i, j, core_ref: (i, j, 0))
    return pl.pallas_call(
        body, name=name,
        grid_spec=pltpu.PrefetchScalarGridSpec(
            num_scalar_prefetch=1, grid=(4, r // tr),
            in_specs=[pl.BlockSpec((None, None, tr, c), lambda i, j, core_ref: (i, core_ref[0], j, 0)), spec],
            out_specs=[spec, spec]),
        out_shape=[jax.ShapeDtypeStruct((4, r, c), F32), jax.ShapeDtypeStruct((4, r, c), BF16)],
        compiler_params=_params("parallel", "parallel"),
    )(core, grads, got)


def _ada_rows(c_all, w_ada, b_ada):
    n_cols = w_ada.shape[1]

    def body(c_ref, w_ref, b_ref, o_ref):
        _, me = _flip(0)
        bias = b_ref[:, pl.ds(pl.multiple_of(me * n_cols, 128), n_cols)]
        o_ref[...] = _dot_f32(_silu(c_ref[...]), w_ref[...]) + bias

    return pl.pallas_call(
        body, name="ada_rows", out_shape=jax.ShapeDtypeStruct((N_DEV, n_cols), F32),
        in_specs=[pl.BlockSpec(memory_space=pltpu.VMEM)] * 3, out_specs=pl.BlockSpec(memory_space=pltpu.VMEM),
    )(c_all, w_ada, b_ada)


def _in_fwd(x, mod, g1, w_in, after=()):
    s = x.shape[0]
    tm = 256

    def body(x_ref, mod_ref, g_ref, w_ref, h_ref, *outs):
        xhat, _ = _rms(x_ref[...])
        h = (xhat * g_ref[...]) * (1.0 + mod_ref[:, D_MODEL:2 * D_MODEL]) + mod_ref[:, 0:D_MODEL]
        hb = h.astype(BF16)
        h_ref[...] = hb
        for j, o_ref in enumerate(outs):
            o_ref[...] = _dot_nt(hb, w_ref[j * 512:(j + 1) * 512, :])

    return pl.pallas_call(
        _behind(body, after), name="in_fwd", grid=(s // tm,),
        out_shape=[jax.ShapeDtypeStruct((s, D_MODEL), BF16)] + [jax.ShapeDtypeStruct((s, 512), F32)] * 7,
        in_specs=_ANY * len(after)
        + [_rows(tm, D_MODEL), _whole((1, 6 * D_MODEL)), _whole((1, D_MODEL)), _whole((IN_WIDTH, D_MODEL))],
        out_specs=[_rows(tm, D_MODEL)] + [_rows(tm, 512)] * 7,
        compiler_params=_params("parallel"),
    )(*after, x, mod, g1, w_in)


def _in_bwd(x, dx1, mod, g1, w_in, dps):
    s = x.shape[0]
    tm = 256

    def body(x_ref, dx_ref, mod_ref, g_ref, w_ref, *rest):
        dp_refs, (gx_ref, dpb_ref, dsh_ref, dsc_ref, dg_ref) = rest[:13], rest[13:]
        pieces = [dp_refs[j][...] for j in range(4)]
        pieces += [dp_refs[4 + 3 * j][...] + dp_refs[5 + 3 * j][...] + dp_refs[6 + 3 * j][...] for j in range(3)]
        for j, p in enumerate(pieces):
            dpb_ref[:, j * 512:(j + 1) * 512] = p.astype(BF16)
        dh = _dot(dpb_ref[...], w_ref[...])
        xhat, rstd = _rms(x_ref[...])
        g = g_ref[...]
        scale1 = 1.0 + mod_ref[:, D_MODEL:2 * D_MODEL]
        n1 = xhat * g

        @pl.when(pl.program_id(0) == 0)
        def _():
            dsh_ref[...] = jnp.zeros_like(dsh_ref)
            dsc_ref[...] = jnp.zeros_like(dsc_ref)
            dg_ref[...] = jnp.zeros_like(dg_ref)

        dsh_ref[...] += _rowsum(dh)
        dsc_ref[...] += _rowsum(dh * n1)
        dn = dh * scale1
        dg_ref[...] += _rowsum(dn * xhat)
        gx_ref[...] = dx_ref[...] + _rms_bwd(dn * g, xhat, rstd)

    vec = _whole((1, D_MODEL))
    return pl.pallas_call(
        body, name="in_bwd", grid=(s // tm,),
        out_shape=[jax.ShapeDtypeStruct((s, D_MODEL), F32), jax.ShapeDtypeStruct((s, IN_WIDTH), BF16)]
        + [jax.ShapeDtypeStruct((1, D_MODEL), F32)] * 3,
        in_specs=[_rows(tm, D_MODEL), _rows(tm, D_MODEL), _whole((1, 6 * D_MODEL)), vec, _whole((IN_WIDTH, D_MODEL))]
        + [_rows(tm, 512)] * 13,
        out_specs=[_rows(tm, D_MODEL), _rows(tm, IN_WIDTH), vec, vec, vec],
        compiler_params=_params("arbitrary"),
    )(x, dx1, mod, g1, w_in, *dps)


HG_TILE = 512
HG_TILE_CHUNKS = HG_TILE // HG_CHUNK


def _lower_bound(lg_ref):
    return 1.0 / (1.0 + jnp.exp(lg_ref[1:2, :] - lg_ref[0:1, :]))


def _chunk_masks():
    r = lax.broadcasted_iota(jnp.int32, (HG_CHUNK, HG_CHUNK), 0)
    c = lax.broadcasted_iota(jnp.int32, (HG_CHUNK, HG_CHUNK), 1)
    return r >= c, c >= r, (r >= c).astype(F32), (c >= r).astype(F32)


def _hg_fwd(hq, hf, hi, hgt, logits, onorm_g):
    s = hq.shape[0]
    n_tiles = s // HG_TILE

    def body(q_ref, f_ref, i_ref, g_ref, lg_ref, og_ref, out_ref, o_ref, st_ref, state, qf_s, kk_s, lf_s):
        @pl.when(pl.program_id(0) == 0)
        def _():
            state[...] = jnp.zeros_like(state)

        lb = _lower_bound(lg_ref)
        f = lb + (1.0 - lb) * _sigmoid(f_ref[...])
        kk_s[...] = 1.0 - f
        lf_s[...] = jnp.log(f)
        qf_s[...] = _silu(q_ref[...])
        causal, _, tri, _ = _chunk_masks()

        def chunk(ci, carry):
            rows = pl.ds(pl.multiple_of(ci * HG_CHUNK, HG_CHUNK), HG_CHUNK)
            srows = pl.ds(pl.multiple_of(ci * HG_DIM, HG_DIM), HG_DIM)
            lf = lf_s[rows, :]
            b = _dot_f32(tri, lf)
            bl = _rowsum(lf)
            ref = 0.5 * bl
            qf, kk, v = qf_s[rows, :], kk_s[rows, :], i_ref[rows, :]
            a_in = (qf * jnp.exp(b)).astype(BF16)
            a_t = (qf * jnp.exp(b - ref)).astype(BF16)
            b_t = (kk * jnp.exp(ref - b)).astype(BF16)
            kd = kk * jnp.exp(bl - b)
            ebl = jnp.exp(bl)
            vb = v.astype(BF16)
            for h in range(HG_HEADS):
                c = slice(h * HG_DIM, (h + 1) * HG_DIM)
                st = state[h]
                st_ref[srows, c] = st
                p = jnp.where(causal, _dot_nt(a_t[:, c], b_t[:, c]), 0.0)
                o_ref[rows, c] = _dot(p.astype(BF16), vb[:, c]) + _dot_nt(a_in[:, c], st.astype(BF16))
                state[h] = st * ebl[:, c] + _dot_tn(vb[:, c], kd[:, c].astype(BF16))
            return carry

        lax.fori_loop(0, HG_TILE_CHUNKS, chunk, 0, unroll=True)
        for h in range(HG_HEADS):
            c = slice(h * HG_DIM, (h + 1) * HG_DIM)
            ohat, _ = _rms(o_ref[:, c])
            out_ref[:, c] = (ohat * og_ref[...] * _silu(g_ref[:, c])).astype(BF16)

    tile = _rows(HG_TILE, HG_WIDTH)
    return pl.pallas_call(
        body, name="hg_fwd", grid=(n_tiles,),
        out_shape=[jax.ShapeDtypeStruct((s, HG_WIDTH), BF16), jax.ShapeDtypeStruct((s, HG_WIDTH), F32),
                   jax.ShapeDtypeStruct((s // HG_CHUNK * HG_DIM, HG_WIDTH), F32)],
        in_specs=[tile] * 4 + [_whole((2, HG_WIDTH)), _whole((1, HG_DIM))],
        out_specs=[tile, tile, _rows(HG_TILE_CHUNKS * HG_DIM, HG_WIDTH)],
        scratch_shapes=[pltpu.VMEM((HG_HEADS, HG_DIM, HG_DIM), F32)] + [pltpu.VMEM((HG_TILE, HG_WIDTH), F32)] * 3,
        compiler_params=_params("arbitrary"),
    )(hq, hf, hi, hgt, logits, onorm_g)


def _hg_bwd(hq, hf, hi, hgt, logits, onorm_g, o, states, dout):
    s = hq.shape[0]
    n_tiles = s // HG_TILE

    def body(q_ref, f_ref, i_ref, g_ref, lg_ref, og_ref, o_ref, st_ref, d_ref,
             dq_ref, df_ref, di_ref, dg_ref, dog_ref, dlb_ref, dstate, qf_s, kk_s, lf_s, do_s):
        @pl.when(pl.program_id(0) == 0)
        def _():
            dstate[...] = jnp.zeros_like(dstate)
            dog_ref[...] = jnp.zeros_like(dog_ref)
            dlb_ref[...] = jnp.zeros_like(dlb_ref)

        og = og_ref[...]
        dog = jnp.zeros((1, HG_DIM), F32)
        for h in range(HG_HEADS):
            c = slice(h * HG_DIM, (h + 1) * HG_DIM)
            ohat, rstd = _rms(o_ref[:, c])
            gate = g_ref[:, c]
            d = d_ref[:, c]
            dg_ref[:, c] = (d * (ohat * og) * _dsilu(gate)).astype(BF16)
            dnormed = d * _silu(gate)
            dog += _rowsum(dnormed * ohat)
            do_s[:, c] = _rms_bwd(dnormed * og, ohat, rstd)
        dog_ref[...] += dog

        lb = _lower_bound(lg_ref)
        f = lb + (1.0 - lb) * _sigmoid(f_ref[...])
        kk_s[...] = 1.0 - f
        lf_s[...] = jnp.log(f)
        qf_s[...] = _silu(q_ref[...])
        causal, upper, tri, tri_t = _chunk_masks()

        def chunk(step, carry):
            ci = HG_TILE_CHUNKS - 1 - step
            rows = pl.ds(pl.multiple_of(ci * HG_CHUNK, HG_CHUNK), HG_CHUNK)
            srows = pl.ds(pl.multiple_of(ci * HG_DIM, HG_DIM), HG_DIM)
            lf = lf_s[rows, :]
            b = _dot_f32(tri, lf)
            bl = _rowsum(lf)
            ref = 0.5 * bl
            qf, kk, v, do = qf_s[rows, :], kk_s[rows, :], i_ref[rows, :], do_s[rows, :]
            eb, ebr, erb, ekd, ebl = jnp.exp(b), jnp.exp(b - ref), jnp.exp(ref - b), jnp.exp(bl - b), jnp.exp(bl)
            a_in, a_t, b_t, kd = qf * eb, qf * ebr, kk * erb, kk * ekd
            for h in range(HG_HEADS):
                c = slice(h * HG_DIM, (h + 1) * HG_DIM)
                st, dst = st_ref[srows, c], dstate[h]
                stb, dstb = st.astype(BF16), dst.astype(BF16)
                doh, vh = do[:, c], v[:, c]
                dob, vb = doh.astype(BF16), vh.astype(BF16)
                ain_h, at_h, bt_h, kd_h = a_in[:, c], a_t[:, c], b_t[:, c], kd[:, c]
                atb, btb = at_h.astype(BF16), bt_h.astype(BF16)
                d_ain = _dot(dob, stb)
                p_t = jnp.where(upper, _dot_nt(btb, atb), 0.0).astype(BF16)
                dp = jnp.where(causal, _dot_nt(dob, vb), 0.0).astype(BF16)
                dp_t = jnp.where(upper, _dot_nt(vb, dob), 0.0).astype(BF16)
                di_ref[rows, c] = (_dot(p_t, dob) + _dot_nt(kd_h.astype(BF16), dstb)).astype(BF16)
                d_at = _dot(dp, btb)
                d_bt = _dot(dp_t, atb)
                d_kd = _dot(vb, dstb)
                dqf = d_ain * eb[:, c] + d_at * ebr[:, c]
                dkk = d_bt * erb[:, c] + d_kd * ekd[:, c]
                db = d_ain * ain_h + d_at * atb.astype(F32) - d_bt * btb.astype(F32) - d_kd * kd_h
                dbl = _rowsum(d_kd * kd_h) + _rowsum(dst * st) * ebl[:, c]
                dstate[h] = _dot_tn(dob, ain_h.astype(BF16)) + dst * ebl[:, c]
                dlf = _dot_f32(tri_t, db) + dbl
                qv, fr = q_ref[rows, c], f_ref[rows, c]
                lbh = lb[:, c]
                sg = _sigmoid(fr)
                dfv = dlf / (lbh + (1.0 - lbh) * sg) - dkk
                df_ref[rows, c] = (dfv * (1.0 - lbh) * sg * (1.0 - sg)).astype(BF16)
                dlb_ref[:, c] += _rowsum(dfv * (1.0 - sg))
                dq_ref[rows, c] = (dqf * _dsilu(qv)).astype(BF16)
            return carry

        lax.fori_loop(0, HG_TILE_CHUNKS, chunk, 0, unroll=True)

    rev = pl.BlockSpec((HG_TILE, HG_WIDTH), lambda i: (n_tiles - 1 - i, 0))
    return pl.pallas_call(
        body, name="hg_bwd", grid=(n_tiles,),
        out_shape=[jax.ShapeDtypeStruct((s, HG_WIDTH), BF16)] * 4
        + [jax.ShapeDtypeStruct((1, HG_DIM), F32), jax.ShapeDtypeStruct((1, HG_WIDTH), F32)],
        in_specs=[rev] * 4 + [_whole((2, HG_WIDTH)), _whole((1, HG_DIM)), rev,
                              pl.BlockSpec((HG_TILE_CHUNKS * HG_DIM, HG_WIDTH), lambda i: (n_tiles - 1 - i, 0)), rev],
        out_specs=[rev] * 4 + [_whole((1, HG_DIM)), _whole((1, HG_WIDTH))],
        scratch_shapes=[pltpu.VMEM((HG_HEADS, HG_DIM, HG_DIM), F32)] + [pltpu.VMEM((HG_TILE, HG_WIDTH), F32)] * 4,
        compiler_params=_params("arbitrary"),
    )(hq, hf, hi, hgt, logits, onorm_g, o, states, dout)


TOKEN_GROUP = 16


def _att_geometry(dil, seq=0):
    per_group = TOKEN_GROUP // dil
    ub = ATT_BLOCK // per_group
    if dil == TOKEN_GROUP:
        n_blocks = 2 if seq % (2 * ub * TOKEN_GROUP) == 0 and seq > 0 else 1
    else:
        n_blocks = 4
    return per_group, ub, ATT_WIDTH if dil == 1 else 128, n_blocks


def _att_consts(dil):
    per_group, ub = _att_geometry(dil)[:2]

    def pos(i):
        return i if dil == 1 else (i % ub) * per_group + i // ub

    lane = lax.broadcasted_iota(jnp.int32, (ATT_BLOCK, 128), 1)
    qi = pos(lax.broadcasted_iota(jnp.int32, (2 * ATT_BLOCK, ATT_BLOCK), 0) % ATT_BLOCK)
    kj = pos(lax.broadcasted_iota(jnp.int32, (2 * ATT_BLOCK, ATT_BLOCK), 1))
    return lane < ATT_HEAD_DIM, kj <= qi, lambda off: kj >= qi + off


def _load_tile(ref, dil, r, c, base=0):
    per_group, ub = _att_geometry(dil)[:2]
    if dil == 1:
        return ref[base:base + ATT_BLOCK, c]
    return jnp.concatenate([ref[pl.ds(base + dil * w + r, ub, stride=TOKEN_GROUP), c] for w in range(per_group)], axis=0)


def _store_tile(ref, dil, r, c, val, base=0):
    per_group, ub = _att_geometry(dil)[:2]
    if dil == 1:
        ref[base:base + ATT_BLOCK, c] = val
        return
    for w in range(per_group):
        ref[pl.ds(base + dil * w + r, ub, stride=TOKEN_GROUP), c] = val[w * ub:(w + 1) * ub]


def _stack_heads(x2, first):
    return jnp.concatenate([jnp.where(first, x2, 0.0), jnp.where(first, 0.0, x2)], axis=0)


def _stack_bcast(x2, first):
    other = pltpu.roll(x2, ATT_HEAD_DIM, axis=1)
    return jnp.concatenate([jnp.where(first, x2, other), jnp.where(first, other, x2)], axis=0)


def _unstack_heads(st, first):
    return jnp.where(first, st[:ATT_BLOCK], st[ATT_BLOCK:])


def _att_fwd(q, k, v, dil):
    seq, width = q.shape
    _, ub, lanes, nbs = _att_geometry(dil, seq)
    rows = ub * TOKEN_GROUP
    n_steps = seq // (nbs * rows)

    def body(q_ref, k_ref, v_ref, kp_ref, vp_ref, o_ref, lse_ref):
        first, cur_ok, _band = _att_consts(dil)
        inner_ok = _band(0)
        edge_ok = _band(jnp.where(pl.program_id(0) > 0, 0, ATT_BLOCK))
        for r in range(dil):
            for j in range(lanes // 128):
                c = slice(j * 128, (j + 1) * 128)
                kc = vc = None
                for b in range(nbs):
                    base = b * rows
                    prev_ok = edge_ok if b == 0 else inner_ok
                    if b == 0:
                        kp, vp = _load_tile(kp_ref, dil, r, c).astype(BF16), _load_tile(vp_ref, dil, r, c).astype(BF16)
                    else:
                        kp, vp = kc, vc
                    qst = _stack_heads(_load_tile(q_ref, dil, r, c, base) * ATT_SCALE, first).astype(BF16)
                    kc = _load_tile(k_ref, dil, r, c, base).astype(BF16)
                    vc = _load_tile(v_ref, dil, r, c, base).astype(BF16)
                    sc = jnp.where(cur_ok, _dot_nt(qst, kc), NEG)
                    sp = jnp.where(prev_ok, _dot_nt(qst, kp), NEG)
                    mx = jnp.max(jnp.maximum(sc, sp), axis=-1, keepdims=True)
                    pc, pp = jnp.exp(sc - mx), jnp.exp(sp - mx)
                    den = jnp.sum(pc + pp, axis=-1, keepdims=True)
                    ost = (_dot(pc.astype(BF16), vc) + _dot(pp.astype(BF16), vp)) / den
                    lse = jnp.broadcast_to(mx + jnp.log(den), (2 * ATT_BLOCK, 128))
                    _store_tile(o_ref, dil, r, c, _unstack_heads(ost, first), base)
                    _store_tile(lse_ref, dil, r, c, _unstack_heads(lse, first), base)

    slab = pl.BlockSpec((nbs * rows, lanes), lambda n, j: (n, j))
    before = pl.BlockSpec((rows, lanes), lambda n, j: (jnp.maximum(n * nbs - 1, 0), j))
    return pl.pallas_call(
        body, name=f"att_fwd_d{dil}", grid=(n_steps, width // lanes),
        out_shape=[jax.ShapeDtypeStruct((seq, width), F32)] * 2,
        in_specs=[slab, slab, slab, before, before], out_specs=[slab, slab],
        compiler_params=_params("arbitrary", "arbitrary"),
    )(q, k, v, k, v)


def _att_bwd(q, k, v, do, cc, lse, dil):
    seq, width = q.shape
    _, ub, lanes, nbs = _att_geometry(dil, seq)
    rows = ub * TOKEN_GROUP
    n_blocks = seq // rows
    n_steps = n_blocks // nbs

    def body(q_ref, k_ref, v_ref, do_ref, cc_ref, lse_ref, qx_ref, dox_ref, ccx_ref, lsex_ref,
             dq_ref, dk_ref, dv_ref, carry):
        first, cur_ok, _band = _att_consts(dil)
        step = pl.program_id(1)
        inner_ok = _band(0)
        edge_ok = _band(jnp.where(step < n_steps - 1, 0, ATT_BLOCK))

        @pl.when(step == 0)
        def _():
            carry[...] = jnp.zeros_like(carry)

        def queries(refs, r, c, base):
            q_r, do_r, lse_r, cc_r = refs
            return (_stack_heads(_load_tile(q_r, dil, r, c, base) * ATT_SCALE, first).astype(BF16),
                    _stack_heads(_load_tile(do_r, dil, r, c, base), first).astype(BF16),
                    _stack_bcast(_load_tile(lse_r, dil, r, c, base), first),
                    _stack_bcast(_load_tile(cc_r, dil, r, c, base), first))

        for r in range(dil):
            for j in range(lanes // 128):
                c = slice(j * 128, (j + 1) * 128)
                own = queries((q_ref, do_ref, lse_ref, cc_ref), r, c, 0)
                left = _load_tile(carry, dil, r, c)
                for b in range(nbs):
                    base = b * rows
                    last = b == nbs - 1
                    next_ok = edge_ok if last else inner_ok
                    if last:
                        following = queries((qx_ref, dox_ref, lsex_ref, ccx_ref), r, c, 0)
                    else:
                        following = queries((q_ref, do_ref, lse_ref, cc_ref), r, c, base + rows)
                    (qst, dost, lse_n, cc_n), (qxst, doxst, lse_x, cc_x) = own, following
                    kb = _load_tile(k_ref, dil, r, c, base).astype(BF16)
                    vb = _load_tile(v_ref, dil, r, c, base).astype(BF16)
                    p_cur = jnp.exp(jnp.where(cur_ok, _dot_nt(qst, kb), NEG) - lse_n)
                    p_next = jnp.exp(jnp.where(next_ok, _dot_nt(qxst, kb), NEG) - lse_x)
                    ds_cur = (p_cur * (_dot_nt(dost, vb) + cc_n)).astype(BF16)
                    ds_next = (p_next * (_dot_nt(doxst, vb) + cc_x)).astype(BF16)
                    dq_own = left + _unstack_heads(_dot(ds_cur, kb), first)
                    _store_tile(dq_ref, dil, r, c, dq_own * ATT_SCALE, base)
                    _store_tile(dk_ref, dil, r, c, _dot_tn(ds_cur, qst) + _dot_tn(ds_next, qxst), base)
                    _store_tile(dv_ref, dil, r, c, _dot_tn(p_cur.astype(BF16), dost) + _dot_tn(p_next.astype(BF16), doxst), base)
                    left = _unstack_heads(_dot(ds_next, kb), first)
                    own = following
                _store_tile(carry, dil, r, c, left)

    slab = pl.BlockSpec((nbs * rows, lanes), lambda j, n: (n, j))
    after = pl.BlockSpec((rows, lanes), lambda j, n: (jnp.minimum((n + 1) * nbs, n_blocks - 1), j))
    return pl.pallas_call(
        body, name=f"att_bwd_d{dil}", grid=(width // lanes, n_steps),
        out_shape=[jax.ShapeDtypeStruct((seq, width), F32)] * 3,
        in_specs=[slab] * 6 + [after] * 4, out_specs=[slab] * 3,
        scratch_shapes=[pltpu.VMEM((rows, lanes), F32)],
        compiler_params=_params("arbitrary", "arbitrary"),
    )(q, k, v, do, cc, lse, q, do, cc, lse)


def _branch_weights(lses):
    mx = jnp.maximum(jnp.maximum(lses[0], lses[1]), lses[2])
    es = [jnp.exp(l - mx) for l in lses]
    inv = 1.0 / (es[0] + es[1] + es[2])
    return [e * inv for e in es]


def _att_combine(outs, lses, att_g, after=()):
    s = outs[0].shape[0]
    tm = 512

    def body(o0, o1, o2, l0, l1, l2, g_ref, att_ref, out_ref):
        ws = _branch_weights([l0[...], l1[...], l2[...]])
        att = ws[0] * o0[...] + ws[1] * o1[...] + ws[2] * o2[...]
        att_ref[...] = att
        ahat, _ = _rms(att)
        out_ref[...] = (ahat * g_ref[...]).astype(BF16)

    tile = _rows(tm, ATT_WIDTH)
    return pl.pallas_call(
        _behind(body, after), name="att_combine", grid=(s // tm,),
        out_shape=[jax.ShapeDtypeStruct((s, ATT_WIDTH), F32), jax.ShapeDtypeStruct((s, ATT_WIDTH), BF16)],
        in_specs=_ANY * len(after) + [tile] * 6 + [_whole((1, ATT_WIDTH))], out_specs=[tile, tile],
        compiler_params=_params("parallel"),
    )(*after, *outs, *lses, att_g)


def _att_combine_bwd(datt_out, att, lses, att_g, after=()):
    s = att.shape[0]
    tm = 256

    def body(d_ref, att_ref, l0, l1, l2, g_ref, do0, do1, do2, cc0, cc1, cc2, dg_ref):
        @pl.when(pl.program_id(0) == 0)
        def _():
            dg_ref[...] = jnp.zeros_like(dg_ref)

        att = att_ref[...]
        ahat, rstd = _rms(att)
        d = d_ref[...]
        dg_ref[...] += _rowsum(d * ahat)
        datt = _rms_bwd(d * g_ref[...], ahat, rstd)
        hi = lax.broadcasted_iota(jnp.int32, (ATT_WIDTH, ATT_WIDTH), 0) // ATT_HEAD_DIM
        hj = lax.broadcasted_iota(jnp.int32, (ATT_WIDTH, ATT_WIDTH), 1) // ATT_HEAD_DIM
        same_head = (hi == hj).astype(BF16)
        prod = datt * att
        prod_hi = prod.astype(BF16)
        prod_lo = (prod - prod_hi.astype(F32)).astype(BF16)
        head_sum = _dot(prod_hi, same_head) + _dot(prod_lo, same_head)
        ws = _branch_weights([l0[...], l1[...], l2[...]])
        for w, do_ref, cc_ref in zip(ws, (do0, do1, do2), (cc0, cc1, cc2)):
            do_ref[...] = w * datt
            cc_ref[...] = -w * head_sum

    tile = _rows(tm, ATT_WIDTH)
    return pl.pallas_call(
        _behind(body, after), name="att_combine_bwd", grid=(s // tm,),
        out_shape=[jax.ShapeDtypeStruct((s, ATT_WIDTH), F32)] * 6 + [jax.ShapeDtypeStruct((1, ATT_WIDTH), F32)],
        in_specs=_ANY * len(after) + [tile] * 5 + [_whole((1, ATT_WIDTH))], out_specs=[tile] * 6 + [_whole((1, ATT_WIDTH))],
        compiler_params=_params("arbitrary"),
    )(*after, datt_out, att, *lses, att_g)


def _out_bwd(dx1, hg, at, mod, w_out):
    s = dx1.shape[0]
    tm = 512
    n_steps = s // tm

    def body(dx_ref, hg_ref, at_ref, mod_ref, w_ref, dhg_ref, dat_ref, dw_ref, dwb_ref, dgate_ref):
        @pl.when(pl.program_id(0) == 0)
        def _():
            dw_ref[...] = jnp.zeros_like(dw_ref)
            dgate_ref[...] = jnp.zeros_like(dgate_ref)

        hg, at, dx = hg_ref[...], at_ref[...], dx_ref[...]
        mix = _dot(hg, w_ref[0:512, :]) + _dot(at, w_ref[512:1024, :])
        dgate_ref[...] += _rowsum(dx * mix)
        dmix = (mod_ref[:, 2 * D_MODEL:3 * D_MODEL] * dx).astype(BF16)
        dhg_ref[...] = _dot_nt(dmix, w_ref[0:512, :])
        dat_ref[...] = _dot_nt(dmix, w_ref[512:1024, :])
        dw_ref[0:512, :] += _dot_tn(hg, dmix)
        dw_ref[512:1024, :] += _dot_tn(at, dmix)

        @pl.when(pl.program_id(0) == n_steps - 1)
        def _():
            dwb_ref[...] = dw_ref[...].astype(BF16)

    return pl.pallas_call(
        body, name="out_bwd", grid=(n_steps,),
        out_shape=[jax.ShapeDtypeStruct((s, 512), F32)] * 2
        + [jax.ShapeDtypeStruct((D_MODEL, D_MODEL), F32), jax.ShapeDtypeStruct((D_MODEL, D_MODEL), BF16),
           jax.ShapeDtypeStruct((1, D_MODEL), F32)],
        in_specs=[_rows(tm, D_MODEL), _rows(tm, 512), _rows(tm, 512), _whole((1, 6 * D_MODEL)), _whole((D_MODEL, D_MODEL))],
        out_specs=[_rows(tm, 512), _rows(tm, 512), _whole((D_MODEL, D_MODEL)), _whole((D_MODEL, D_MODEL)), _whole((1, D_MODEL))],
        compiler_params=_params("arbitrary"),
    )(dx1, hg, at, mod, w_out)


def _ffn(x, hg, at, target, mod, g2, gf, w_out, w_gu, w_down):
    s = x.shape[0]
    tm = 256

    def body(x_ref, hg_ref, at_ref, t_ref, mod_ref, g2_ref, gf_ref, wo_ref, wgu_hbm, wd_hbm,
             dx_ref, h2_ref, act_ref, dau_ref, dff_ref, sums_ref, loss_ref, wgu, wd, au_s, sem):
        @pl.when(pl.program_id(0) == 0)
        def _():
            c1 = pltpu.make_async_copy(wgu_hbm, wgu, sem.at[0])
            c2 = pltpu.make_async_copy(wd_hbm, wd, sem.at[1])
            c1.start()
            c2.start()
            c1.wait()
            c2.wait()
            sums_ref[...] = jnp.zeros_like(sums_ref)
            loss_ref[...] = jnp.zeros_like(loss_ref)

        mix = _dot(hg_ref[...], wo_ref[0:512, :]) + _dot(at_ref[...], wo_ref[512:1024, :])
        x1v = x_ref[...] + mod_ref[:, 2 * D_MODEL:3 * D_MODEL] * mix
        xhat, rstd = _rms(x1v)
        g2 = g2_ref[...]
        n2 = xhat * g2
        scale2 = 1.0 + mod_ref[:, 4 * D_MODEL:5 * D_MODEL]
        gate2 = mod_ref[:, 5 * D_MODEL:6 * D_MODEL]
        hb = (n2 * scale2 + mod_ref[:, 3 * D_MODEL:4 * D_MODEL]).astype(BF16)
        h2_ref[...] = hb
        au_s[...] = _dot_nt(hb, wgu[...])
        a = au_s[:, 0:D_FF]
        act = (_silu(a) * au_s[:, D_FF:2 * D_FF]).astype(BF16)
        act_ref[...] = act
        ff = _dot(act, wd[...])
        x2 = x1v + gate2 * ff
        nf, rstd_f = _rms(x2)
        gfv = gf_ref[...]
        err = nf * gfv - t_ref[...]
        loss_ref[...] += 0.5 * jnp.sum(_rowsum(err * err), axis=-1, keepdims=True) * (1.0 / D_MODEL)
        dy = err * (1.0 / D_MODEL)
        dx2 = _rms_bwd(dy * gfv, nf, rstd_f)
        dffb = (gate2 * dx2).astype(BF16)
        dff_ref[...] = dffb
        dact = _dot_nt(dffb, wd[...])
        a = au_s[:, 0:D_FF]
        dau_ref[:, 0:D_FF] = (dact * au_s[:, D_FF:2 * D_FF] * _dsilu(a)).astype(BF16)
        dau_ref[:, D_FF:2 * D_FF] = (dact * _silu(a)).astype(BF16)
        dh = _dot(dau_ref[...], wgu[...])
        dn = dh * scale2
        sums_ref[0:1, :] += _rowsum(dh)
        sums_ref[1:2, :] += _rowsum(dh * n2)
        sums_ref[2:3, :] += _rowsum(dx2 * ff)
        sums_ref[3:4, :] += _rowsum(dn * xhat)
        sums_ref[4:5, :] += _rowsum(dy * nf)
        dx_ref[...] = dx2 + _rms_bwd(dn * g2, xhat, rstd)

    vec = _whole((1, D_MODEL))
    hbm = pl.BlockSpec(memory_space=pl.ANY)
    return pl.pallas_call(
        body, name="ffn", grid=(s // tm,),
        out_shape=[jax.ShapeDtypeStruct((s, D_MODEL), F32), jax.ShapeDtypeStruct((s, D_MODEL), BF16),
                   jax.ShapeDtypeStruct((s, D_FF), BF16), jax.ShapeDtypeStruct((s, 2 * D_FF), BF16),
                   jax.ShapeDtypeStruct((s, D_MODEL), BF16), jax.ShapeDtypeStruct((8, D_MODEL), F32),
                   jax.ShapeDtypeStruct((1, 128), F32)],
        in_specs=[_rows(tm, D_MODEL), _rows(tm, 512), _rows(tm, 512), _rows(tm, D_MODEL), _whole((1, 6 * D_MODEL)), vec, vec,
                  _whole((D_MODEL, D_MODEL)), hbm, hbm],
        out_specs=[_rows(tm, D_MODEL), _rows(tm, D_MODEL), _rows(tm, D_FF), _rows(tm, 2 * D_FF), _rows(tm, D_MODEL),
                   _whole((8, D_MODEL)), _whole((1, 128))],
        scratch_shapes=[pltpu.VMEM((2 * D_FF, D_MODEL), BF16), pltpu.VMEM((D_FF, D_MODEL), BF16),
                        pltpu.VMEM((tm, 2 * D_FF), F32), pltpu.SemaphoreType.DMA((2,))],
        compiler_params=_params("arbitrary"),
    )(x, hg, at, target, mod, g2, gf, w_out, w_gu, w_down)


def _weight_grad(a, b, name, rounded=False):
    s, m = a.shape
    n = b.shape[1]
    ts = min(s, 2048)
    n_steps = s // ts
    tm = max(t for t in range(128, m + 1, 128) if m % t == 0 and t * n * 4 <= 6 * 1024 * 1024)

    def body(a_ref, b_ref, o_ref, *ob_ref):
        @pl.when(pl.program_id(1) == 0)
        def _():
            o_ref[...] = jnp.zeros_like(o_ref)

        o_ref[...] += _dot_tn(a_ref[...], b_ref[...])
        if rounded:
            @pl.when(pl.program_id(1) == n_steps - 1)
            def _():
                ob_ref[0][...] = o_ref[...].astype(BF16)

    tile = pl.BlockSpec((tm, n), lambda j, i: (j, 0))
    return pl.pallas_call(
        body, name=name, grid=(m // tm, n_steps),
        out_shape=[jax.ShapeDtypeStruct((m, n), F32)] + [jax.ShapeDtypeStruct((m, n), BF16)] * rounded,
        in_specs=[pl.BlockSpec((ts, tm), lambda j, i: (i, j)), pl.BlockSpec((ts, n), lambda j, i: (i, 0))],
        out_specs=[tile] + [tile] * rounded,
        compiler_params=_params("parallel", "arbitrary"),
    )(a, b)


def _adamw_math(w, g, m, v):
    m = ADAM_B1 * m + (1.0 - ADAM_B1) * g
    v = ADAM_B2 * v + (1.0 - ADAM_B2) * (g * g)
    m_hat = m / (1.0 - ADAM_B1 ** ADAM_STEP)
    v_hat = v / (1.0 - ADAM_B2 ** ADAM_STEP)
    delta = -ADAM_LR * (m_hat / (jnp.sqrt(v_hat) + ADAM_EPS) + ADAM_WD * w)
    return delta, m, v


def _adamw_shard(where, w, m, v, partial, got, name):
    r, c = w.shape
    tr = _shard_rows(r)
    lead = partial.ndim - 2
    n_got = got.shape[0]

    def body(where_ref, w_ref, m_ref, v_ref, own_ref, *rest):
        got_refs, (grad_ref, d_ref, nm_ref, nv_ref) = rest[:n_got], rest[n_got:]
        g = own_ref[...]
        for g_ref in got_refs:
            g = g + g_ref[...].astype(F32)
        grad_ref[...] = g
        d_ref[...], nm_ref[...], nv_ref[...] = _adamw_math(w_ref[...], g, m_ref[...], v_ref[...])

    tile = pl.BlockSpec((tr, c), lambda i, where_ref: (i, 0))
    own = pl.BlockSpec((None,) * lead + (tr, c), lambda i, where_ref: (*[where_ref[d] for d in range(lead)], i, 0))
    part = [pl.BlockSpec((None, tr, c), functools.partial(lambda j, i, where_ref: (j, i, 0), j)) for j in range(n_got)]
    return pl.pallas_call(
        body, name=name,
        grid_spec=pltpu.PrefetchScalarGridSpec(num_scalar_prefetch=1, grid=(r // tr,), in_specs=[tile] * 3 + [own] + part,
                                               out_specs=[tile] * 4),
        out_shape=[jax.ShapeDtypeStruct((r, c), F32)] * 4, compiler_params=_params("parallel"),
    )(where, w, m, v, partial, *[got] * n_got)


def _small_update(small_all, dmod_blocks, c_all, logits, w_ada, m_ada, v_ada, smalls, after=()):
    def body(sm_ref, dm_ref, c_ref, lg_ref, wa_ref, ma_ref, va_ref, *rest):
        ins, outs = rest[:21], rest[21:]
        _, me = _flip(0)
        tot = sm_ref[0:1, :]
        for i in range(1, N_DEV):
            tot = tot + sm_ref[i:i + 1, :]
        loss_ref = outs[0]
        loss_ref[...] = tot[:, SM_LOSS:SM_LOSS + 128]
        g_ada = lax.dot_general(_silu(c_ref[...]), dm_ref[me], (((0,), (0,)), ((), ())),
                                preferred_element_type=F32, precision=HIGHEST)
        outs[1][...] = g_ada
        outs[2][...], outs[3][...], outs[4][...] = _adamw_math(wa_ref[...], g_ada, ma_ref[...], va_ref[...])
        p0 = _lower_bound(lg_ref)
        dl0 = tot[:, SM_LB:SM_LB + 512] * p0 * (1.0 - p0)
        grads = [tot[:, SM_MOD:SM_MOD + 6 * D_MODEL], tot[:, SM_G1:SM_G1 + D_MODEL], tot[:, SM_G2:SM_G2 + D_MODEL],
                 tot[:, SM_GF:SM_GF + D_MODEL], tot[:, SM_ATT:SM_ATT + 512], tot[:, SM_HG:SM_HG + 128],
                 jnp.where(lax.broadcasted_iota(jnp.int32, (2, 512), 0) == 0, dl0, -dl0)]
        for i, g in enumerate(grads):
            w_ref, m_ref, v_ref = ins[3 * i:3 * i + 3]
            o = outs[5 + 4 * i:9 + 4 * i]
            o[0][...] = g
            o[1][...], o[2][...], o[3][...] = _adamw_math(w_ref[...], g, m_ref[...], v_ref[...])

    flat = [t for trio in smalls for t in trio]
    vm = pl.BlockSpec(memory_space=pltpu.VMEM)
    out_shape = [jax.ShapeDtypeStruct((1, 128), F32)] + [jax.ShapeDtypeStruct(w_ada.shape, F32)] * 4
    for trio in smalls:
        out_shape += [jax.ShapeDtypeStruct(trio[0].shape, F32)] * 4
    return pl.pallas_call(
        _behind(body, after), name="small_update", out_shape=out_shape,
        in_specs=_ANY * len(after) + [vm] * (7 + len(flat)), out_specs=[vm] * len(out_shape),
        compiler_params=pltpu.CompilerParams(vmem_limit_bytes=V7X_VMEM_LIMIT),
    )(*after, small_all, dmod_blocks, c_all, logits, w_ada, m_ada, v_ada, *flat)


def kernel(x, c, w_ada, b_ada, norm1_g, w_in, hg_lb_logits, hg_onorm_g, att_onorm_g, w_out, norm2_g, w_gate_up, w_down, final_g, loss_target, m_w_ada, m_b_ada, m_norm1_g, m_w_in, m_hg_lb_logits, m_hg_onorm_g, m_att_onorm_g, m_w_out, m_norm2_g, m_w_gate_up, m_w_down, m_final_g, v_w_ada, v_b_ada, v_norm1_g, v_w_in, v_hg_lb_logits, v_hg_onorm_g, v_att_onorm_g, v_w_out, v_norm2_g, v_w_gate_up, v_w_down, v_final_g):
    x2d, target = x[0], loss_target[0]
    seq = x2d.shape[0]
    assert seq % (ATT_BLOCK * max(DILATIONS)) == 0 and seq % HG_TILE == 0
    gf = final_g.reshape(1, D_MODEL)

    c_all = _exchange_small(c.reshape(8, D_MODEL // 8), None, "gather_c").reshape(N_DEV, D_MODEL)
    ada = _ada_rows(c_all, w_ada[0], b_ada)
    mod = _exchange_small(ada, 1, "scatter_mod").reshape(1, 6 * D_MODEL)

    core = lax.axis_index("c").astype(jnp.int32).reshape(1)
    chip = (2 * lax.axis_index("x") + lax.axis_index("y")).astype(jnp.int32).reshape(1)
    me = 4 * lax.axis_index("x") + 2 * lax.axis_index("y") + lax.axis_index("c")

    g_in, = _gather_weights([w_in[0].T.astype(BF16)])
    w_in_b = g_in.reshape(IN_WIDTH, D_MODEL)
    rest_shards = [w_out[0].astype(BF16), w_gate_up[0].T.astype(BF16), w_down[0].astype(BF16)]
    lands = [lax.empty((N_DEV,) + s.shape, BF16) for s in rest_shards]
    g_send, g_recv, g_srcs, g_lands, tok = _copies_start("gather_rest_start", _plan_gather_own, 12, rest_shards, lands, [w_in_b, mod])
    flight = {}

    def stage(name, *vals):
        if name == "attention_begun":
            flight["shards"], got = _copies_wait("gather_rest_wait", _plan_gather_own, g_send, g_recv, g_srcs, g_lands, list(vals))
            flight["pass"] = _copies_start("gather_pass_start", _plan_gather_pass, 9, [], got, [])
            return [flight["pass"][4]]
        if name == "mixer_weights_done":
            shapes = [(4, 2, D_MODEL // N_DEV, D_MODEL), (4, 2, 2 * D_FF // N_DEV, D_MODEL), (4, 2, D_FF // N_DEV, D_MODEL)]
            flight["grads"] = [g32.reshape(sh) for (g32, _), sh in zip(vals, shapes)]
            rounded = [g16.reshape(sh) for (_, g16), sh in zip(vals, shapes)]
            direct_lands = [lax.empty((N_DEV - 1,) + sh[2:], BF16) for sh in shapes]
            flight["direct"] = _copies_start("reduce_rest_start", _plan_reduce_direct, 21, rounded, direct_lands, [])
            return [flight["direct"][4]]
        raise ValueError(name)

    def rest_weights(after):
        s, r, _, p_lands, _ = flight["pass"]
        _, got = _copies_wait("gather_pass_wait", _plan_gather_pass, s, r, [], p_lands, [after])
        full = [lax.dynamic_update_index_in_dim(g, shard, me, 0) for g, shard in zip(got, flight["shards"])]
        return full[0].reshape(D_MODEL, D_MODEL), full[1].reshape(2 * D_FF, D_MODEL), full[2].reshape(D_FF, D_MODEL)

    grad_x, dw_in, small = _block_step(x2d, target, mod, norm1_g, hg_lb_logits, hg_onorm_g, att_onorm_g, norm2_g, gf,
                                       w_in_b, rest_weights, stage, [tok])

    g_in8 = dw_in.reshape(4, 2, IN_WIDTH // N_DEV, D_MODEL)
    in_pairs = _copies_start("reduce_pairs_in_start", _plan_reduce_pairs, 4, [g_in8], [lax.empty((4,) + g_in8.shape[2:], F32)], [])
    s, r, srcs, d_lands, _ = flight["direct"]
    _, recv_rest = _copies_wait("reduce_rest_wait", _plan_reduce_direct, s, r, srcs, d_lands, [in_pairs[4]])
    small_rows = jnp.pad(small, ((0, 0), (0, SM_PADDED - SM_WIDTH))).reshape(SM_PADDED // 128, 128)
    small_all = _exchange_small(small_rows, None, "gather_small", [in_pairs[4]]).reshape(N_DEV, SM_PADDED)[:, :SM_WIDTH]
    in_grads, got_in = _copies_wait("reduce_pairs_in_wait", _plan_reduce_pairs, in_pairs[0], in_pairs[1], in_pairs[2], in_pairs[3],
                                    [small_all])
    in_s32, in_s16 = _pair_sum(core, in_grads[0], got_in[0], "pair_sum_in")
    in_chips = _copies_start("reduce_chips_in_start", _plan_reduce_chips, 3, [in_s16], [lax.empty((3,) + in_s16.shape[1:], BF16)], [])
    big, updated = {}, []
    rest_params = [("w_out", w_out, m_w_out, v_w_out), ("w_gate_up", w_gate_up, m_w_gate_up, v_w_gate_up), ("w_down", w_down, m_w_down, v_w_down)]
    mine = jnp.concatenate([chip, core])
    for (n, w, m, v), g32, got in zip(rest_params, flight["grads"], recv_rest):
        if n == "w_gate_up":
            outs4 = _adamw_shard(mine, w[0].T, m[0].T, v[0].T, g32, got, f"adamw_{n}")
            big[n] = [t.T[None] for t in outs4]
        else:
            outs4 = _adamw_shard(mine, w[0], m[0], v[0], g32, got, f"adamw_{n}")
            big[n] = [t[None] for t in outs4]
        updated.append(outs4[3])
    smalls = [(b_ada, m_b_ada, v_b_ada), (norm1_g, m_norm1_g, v_norm1_g), (norm2_g, m_norm2_g, v_norm2_g),
              (gf, m_final_g.reshape(1, D_MODEL), v_final_g.reshape(1, D_MODEL)),
              (att_onorm_g, m_att_onorm_g, v_att_onorm_g), (hg_onorm_g, m_hg_onorm_g, v_hg_onorm_g),
              (hg_lb_logits, m_hg_lb_logits, v_hg_lb_logits)]
    dmod_blocks = small_all[:, :6 * D_MODEL].reshape(N_DEV, N_DEV, 6 * D_MODEL // N_DEV).transpose(1, 0, 2)
    res = _small_update(small_all, dmod_blocks, c_all, hg_lb_logits, w_ada[0], m_w_ada[0], v_w_ada[0], smalls, [in_chips[4]])
    _, recv_in = _copies_wait("reduce_chips_in_wait", _plan_reduce_chips, in_chips[0], in_chips[1], in_chips[2], in_chips[3],
                              [res[0]] + updated)
    big["w_in"] = [t.T[None] for t in _adamw_shard(chip, w_in[0].T, m_w_in[0].T, v_w_in[0].T, in_s32, recv_in[0], "adamw_w_in")]
    loss = res[0][0, 0]
    ada4 = [t[None] for t in res[1:5]]
    sm4 = {n: list(res[5 + 4 * i:9 + 4 * i]) for i, n in enumerate(["b_ada", "norm1_g", "norm2_g", "final_g", "att", "hg", "lb"])}
    sm4["final_g"] = [t.reshape(D_MODEL) for t in sm4["final_g"]]

    order = [ada4, sm4["b_ada"], sm4["norm1_g"], big["w_in"], sm4["lb"], sm4["hg"], sm4["att"], big["w_out"], sm4["norm2_g"],
             big["w_gate_up"], big["w_down"], sm4["final_g"]]
    return (loss, grad_x[None], *[o[0] for o in order], *[o[1] for o in order], *[o[2] for o in order], *[o[3] for o in order])


def _block_step(x2d, target, mod, norm1_g, hg_lb_logits, hg_onorm_g, att_onorm_g, norm2_g, gf, w_in_b, rest_weights, stage,
                after=()):
    h1, hq, hf, hi, hgt, aq, ak, av = _in_fwd(x2d, mod, norm1_g, w_in_b, after)
    hg_out, hg_o, hg_states = _hg_fwd(hq, hf, hi, hgt, hg_lb_logits, hg_onorm_g)
    branch = [_att_fwd(aq, ak, av, d) for d in DILATIONS[:2]]
    behind = stage("attention_begun", branch[0][0], branch[1][0])
    branch += [_att_fwd(aq, ak, av, d) for d in DILATIONS[2:]]
    outs = [b[0] for b in branch]
    lses = [b[1] for b in branch]
    att, att_out = _att_combine(outs, lses, att_onorm_g, behind)
    w_out_b, w_gu_b, w_down_b = rest_weights(att_out)

    dx1, h2, act, dau, dff, ffn_sums, loss_part = _ffn(x2d, hg_out, att_out, target, mod, norm2_g, gf, w_out_b, w_gu_b, w_down_b)
    dw_gu = _weight_grad(dau, h2, "dw_gate_up", rounded=True)
    dw_down = _weight_grad(act, dff, "dw_down", rounded=True)

    dhg, dat, dw_out, dw_out_b, dgate1 = _out_bwd(dx1, hg_out, att_out, mod, w_out_b)
    behind = stage("mixer_weights_done", (dw_out, dw_out_b), dw_gu, dw_down)
    comb = _att_combine_bwd(dat, att, lses, att_onorm_g, behind)
    dos, ccs, d_att_g = comb[0:3], comb[3:6], comb[6]
    datt = []
    for i, d in enumerate(DILATIONS):
        datt.append(_att_bwd(aq, ak, av, dos[i], ccs[i], lses[i], d))
    dhq, dhf, dhi, dhgt, d_hg_g, d_lb = _hg_bwd(hq, hf, hi, hgt, hg_lb_logits, hg_onorm_g, hg_o, hg_states, dhg)
    dps = [dhq, dhf, dhi, dhgt] + [datt[i][j] for j in range(3) for i in range(3)]
    grad_x, dp_b, dshift1, dscale1, d_g1 = _in_bwd(x2d, dx1, mod, norm1_g, w_in_b, dps)
    dw_in, = _weight_grad(dp_b, h1, "dw_in")
    small = jnp.concatenate([dshift1, dscale1, dgate1, ffn_sums[0:1], ffn_sums[1:2], ffn_sums[2:3], d_g1, ffn_sums[3:4],
                             ffn_sums[4:5], d_att_g, d_lb, d_hg_g, loss_part], axis=1)
    return grad_x, dw_in, small
```

```python
import functools

import jax
import jax.numpy as jnp
from jax import lax
from jax.experimental import pallas as pl
from jax.experimental.pallas import tpu as pltpu

F32 = jnp.float32
BF16 = jnp.bfloat16
HIGHEST = lax.Precision.HIGHEST
MESH = pl.DeviceIdType.MESH

D_MODEL = 1024
N_DEV = 8
HG_HEADS = 4
HG_DIM = 128
HG_WIDTH = HG_HEADS * HG_DIM
HG_CHUNK = 128
ATT_WIDTH = 512
ATT_HEAD_DIM = 64
ATT_BLOCK = 128
DILATIONS = (1, 4, 16)
ATT_SCALE = ATT_HEAD_DIM ** -0.5
D_FF = 2816
IN_WIDTH = 7 * 512
RMS_EPS = 1e-6
NEG = -1e30

ADAM_LR = 0.001
ADAM_B1 = 0.9
ADAM_B2 = 0.999
ADAM_EPS = 1e-08
ADAM_WD = 0.01
ADAM_STEP = 10

V7X_VMEM_LIMIT = 56 * 1024 * 1024

SM_MOD = 0
SM_G1 = 6 * D_MODEL
SM_G2 = 7 * D_MODEL
SM_GF = 8 * D_MODEL
SM_ATT = 9 * D_MODEL
SM_LB = 9 * D_MODEL + 512
SM_HG = 10 * D_MODEL
SM_LOSS = 10 * D_MODEL + 128
SM_WIDTH = 10 * D_MODEL + 256
SM_PADDED = 88 * 128


def _params(*sem, vmem=V7X_VMEM_LIMIT):
    return pltpu.CompilerParams(dimension_semantics=sem, vmem_limit_bytes=vmem)


def _dot(a, b):
    return jnp.dot(a, b, preferred_element_type=F32)


def _dot_nt(a, b):
    return lax.dot_general(a, b, (((1,), (1,)), ((), ())), preferred_element_type=F32)


def _dot_tn(a, b):
    return lax.dot_general(a, b, (((0,), (0,)), ((), ())), preferred_element_type=F32)


def _dot_f32(a, b):
    return jnp.dot(a, b, preferred_element_type=F32, precision=HIGHEST)


def _sigmoid(x):
    return 1.0 / (1.0 + jnp.exp(-x))


def _silu(x):
    return x * _sigmoid(x)


def _dsilu(x):
    s = _sigmoid(x)
    return s * (1.0 + x * (1.0 - s))


def _rms(x):
    rstd = lax.rsqrt(jnp.mean(x * x, axis=-1, keepdims=True) + RMS_EPS)
    return x * rstd, rstd


def _rms_bwd(dn, xhat, rstd):
    return rstd * (dn - xhat * jnp.mean(dn * xhat, axis=-1, keepdims=True))


def _rowsum(x):
    return jnp.sum(x, axis=0, keepdims=True)


def _rows(tm, n):
    return pl.BlockSpec((tm, n), lambda i: (i, 0))


def _whole(shape):
    return pl.BlockSpec(shape, lambda i: (0,) * len(shape))


def _mesh_pos():
    return lax.axis_index("x"), lax.axis_index("y"), lax.axis_index("c")


def _flip(k):
    x, y, c = _mesh_pos()
    px = 1 - x if k & 4 else x
    py = 1 - y if k & 2 else y
    pc = 1 - c if k & 1 else c
    return (px, py, pc), 4 * px + 2 * py + pc


_ANY = [pl.BlockSpec(memory_space=pl.ANY)]


def _behind(body, after):
    return lambda *refs: body(*refs[len(after):])


def _exchange_small(x, rows_per_peer, name, after=()):
    r_all, cols = x.shape
    r_out = r_all if rows_per_peer is None else rows_per_peer

    def body(x_ref, out_ref, send_sems, recv_sems):
        _, me = _flip(0)

        def src(pid):
            if rows_per_peer is None:
                return x_ref
            return x_ref.at[pl.ds(pl.multiple_of(pid * r_out, r_out), r_out), :]

        if rows_per_peer is None:
            out_ref[me] = x_ref[...]
        else:
            out_ref[me] = x_ref[pl.ds(pl.multiple_of(me * r_out, r_out), r_out), :]
        sends = []
        for k in range(1, N_DEV):
            dev, pid = _flip(k)
            cp = pltpu.make_async_remote_copy(src_ref=src(pid), dst_ref=out_ref.at[me], send_sem=send_sems.at[k - 1],
                                              recv_sem=recv_sems.at[k - 1], device_id=dev, device_id_type=MESH)
            cp.start()
            sends.append(cp)
        for k in range(1, N_DEV):
            dev, pid = _flip(k)
            pltpu.make_async_remote_copy(src_ref=src(pid), dst_ref=out_ref.at[pid], send_sem=send_sems.at[k - 1],
                                         recv_sem=recv_sems.at[k - 1], device_id=dev, device_id_type=MESH).wait_recv()
        for cp in sends:
            cp.wait_send()

    return pl.pallas_call(
        _behind(body, after), name=name,
        out_shape=jax.ShapeDtypeStruct((N_DEV, r_out, cols), x.dtype),
        in_specs=_ANY * len(after) + [pl.BlockSpec(memory_space=pltpu.VMEM)],
        out_specs=pl.BlockSpec(memory_space=pltpu.VMEM),
        scratch_shapes=[pltpu.SemaphoreType.DMA((N_DEV - 1,)), pltpu.SemaphoreType.DMA((N_DEV - 1,))],
    )(*after, x)


def _gather_weights(shards):
    n = len(shards)

    def body(*refs):
        xs, outs = refs[:n], refs[n:2 * n]
        send_sems, recv_sems, local_sems = refs[2 * n:]
        x, y, c = _mesh_pos()
        me, sibling = (x, y, c), (x, y, 1 - c)
        chips = [(1 - x, y), (x, 1 - y), (1 - x, 1 - y)]

        def blk(a, px, py, pc):
            return outs[a].at[4 * px + 2 * py + pc]

        def copy(a, k, block, to, src=None):
            return pltpu.make_async_remote_copy(
                src_ref=blk(a, *block) if src is None else src, dst_ref=blk(a, *block),
                send_sem=send_sems.at[a * 7 + k], recv_sem=recv_sems.at[a * 7 + k], device_id=to, device_id_type=MESH)

        mine = [pltpu.make_async_copy(xs[a], blk(a, *me), local_sems.at[a]) for a in range(n)]
        for cp in mine:
            cp.start()
        first = []
        for a in range(n):
            first.append(copy(a, 0, me, sibling, src=xs[a]))
            first += [copy(a, 1 + j, me, (*chip, c), src=xs[a]) for j, chip in enumerate(chips)]
        for cp in first:
            cp.start()
        passed = []
        for j, chip in enumerate(chips):
            for a in range(n):
                copy(a, 1 + j, (*chip, c), me).wait_recv()
                cp = copy(a, 4 + j, (*chip, c), sibling)
                cp.start()
                passed.append(cp)
        for a in range(n):
            copy(a, 0, sibling, me).wait_recv()
            for j, chip in enumerate(chips):
                copy(a, 4 + j, (*chip, 1 - c), me).wait_recv()
        for cp in first + passed:
            cp.wait_send()
        for cp in mine:
            cp.wait()

    hbm = pl.BlockSpec(memory_space=pl.ANY)
    return pl.pallas_call(
        body, name="gather_weights",
        out_shape=[jax.ShapeDtypeStruct((N_DEV,) + s.shape, s.dtype) for s in shards],
        in_specs=[hbm] * n, out_specs=[hbm] * n,
        scratch_shapes=[pltpu.SemaphoreType.DMA((7 * n,)), pltpu.SemaphoreType.DMA((7 * n,)), pltpu.SemaphoreType.DMA((n,))],
    )(*shards)


_HBM = pl.BlockSpec(memory_space=pltpu.HBM)
_SEM = pl.BlockSpec(memory_space=pltpu.SEMAPHORE)
_DATAFLOW = pltpu.SideEffectType.DATAFLOW_SIDE_EFFECTING


def _copies_start(name, plan, n_copies, srcs, lands, after):
    bufs = list(srcs) + list(lands)
    nb = len(bufs)

    def body(*refs):
        ins, send_sems, recv_sems, token = refs[:nb], refs[nb + len(after)], refs[nb + len(after) + 1], refs[-1]
        for i, (src, dst, dev) in enumerate(plan(ins[:len(srcs)], ins[len(srcs):])):
            pltpu.make_async_remote_copy(src_ref=src, dst_ref=dst, send_sem=send_sems.at[i], recv_sem=recv_sems.at[i],
                                         device_id=dev, device_id_type=MESH).start()
        token[...] = jnp.zeros_like(token)

    outs = pl.pallas_call(
        body, name=name,
        out_shape=(pltpu.SemaphoreType.DMA((n_copies,)), pltpu.SemaphoreType.DMA((n_copies,)),
                   *[pltpu.HBM(b.shape, b.dtype) for b in bufs], jax.ShapeDtypeStruct((8, 128), F32)),
        in_specs=[_HBM] * nb + [pl.BlockSpec(memory_space=pl.ANY)] * len(after),
        out_specs=(_SEM, _SEM, *[_HBM] * nb, pl.BlockSpec(memory_space=pltpu.VMEM)),
        input_output_aliases={i: 2 + i for i in range(nb)},
        compiler_params=pltpu.CompilerParams(has_side_effects=_DATAFLOW),
    )(*[pltpu.with_memory_space_constraint(b, pltpu.HBM) for b in bufs], *after)
    return outs[0], outs[1], list(outs[2:2 + len(srcs)]), list(outs[2 + len(srcs):2 + nb]), outs[-1]


def _copies_wait(name, plan, send_sems, recv_sems, srcs, lands, after):
    bufs = list(srcs) + list(lands)
    nb = len(bufs)

    def body(*refs):
        ins, send_ref, recv_ref = refs[:nb], refs[nb], refs[nb + 1]
        for i, (src, dst, dev) in enumerate(plan(ins[:len(srcs)], ins[len(srcs):])):
            cp = pltpu.make_async_remote_copy(src_ref=src, dst_ref=dst, send_sem=send_ref.at[i], recv_sem=recv_ref.at[i],
                                              device_id=dev, device_id_type=MESH)
            cp.wait_send()
            cp.wait_recv()

    outs = pl.pallas_call(
        body, name=name, out_shape=[pltpu.HBM(b.shape, b.dtype) for b in bufs],
        in_specs=[_HBM] * nb + [_SEM, _SEM] + [pl.BlockSpec(memory_space=pl.ANY)] * len(after), out_specs=[_HBM] * nb,
        input_output_aliases={i: i for i in range(nb)},
        compiler_params=pltpu.CompilerParams(has_side_effects=_DATAFLOW),
    )(*bufs, send_sems, recv_sems, *after)
    return list(outs[:len(srcs)]), list(outs[len(srcs):])


def _plan_gather_own(srcs, lands):
    _, me = _flip(0)
    return [(srcs[a], lands[a].at[me], _flip(k)[0]) for a in range(len(srcs)) for k in (1, 4, 2, 6)]


def _plan_gather_pass(srcs, lands):
    sibling = _flip(1)[0]
    plan = []
    for land in lands:
        for k in (4, 2, 6):
            block = land.at[_flip(k)[1]]
            plan.append((block, block, sibling))
    return plan


def _plan_reduce_pairs(srcs, lands):
    x, y, c = _mesh_pos()
    return [(srcs[a].at[chip, 1 - c], lands[a].at[chip], (x, y, 1 - c)) for a in range(len(srcs)) for chip in range(4)]


def _plan_reduce_chips(srcs, lands):
    plan = []
    for a in range(len(srcs)):
        for j, k in enumerate((4, 2, 6)):
            dev = _flip(k)[0]
            plan.append((srcs[a].at[2 * dev[0] + dev[1]], lands[a].at[j], dev))
    return plan


def _plan_reduce_direct(srcs, lands):
    plan = []
    for a in range(len(srcs)):
        for k in range(1, N_DEV):
            dev = _flip(k)[0]
            plan.append((srcs[a].at[2 * dev[0] + dev[1], dev[2]], lands[a].at[k - 1], dev))
    return plan


def _shard_rows(r):
    return r // 2 if r % 32 == 0 else r


def _pair_sum(core, grads, got, name):
    _, _, r, c = grads.shape
    tr = _shard_rows(r)

    def body(core_ref, a_ref, b_ref, o_ref, ob_ref):
        s = a_ref[...] + b_ref[...]
        o_ref[...] = s
        ob_ref[...] = s.astype(BF16)

    spec = pl.BlockSpec((None, tr, c), lambda i, j, core_ref: (i, j, 0))
    return pl.pallas_call(
        body, name=name,
        grid_spec=pltpu.PrefetchScalarGridSpec(
            num_scalar_prefetch=1, grid=(4, r // tr),
            in_specs=[pl.BlockSpec((None, None, tr, c), lambda i, j, core_ref: (i, core_ref[0], j, 0)), spec],
            out_specs=[spec, spec]),
        out_shape=[jax.ShapeDtypeStruct((4, r, c), F32), jax.ShapeDtypeStruct((4, r, c), BF16)],
        compiler_params=_params("parallel", "parallel"),
    )(core, grads, got)


def _ada_rows(c_all, w_ada, b_ada):
    n_cols = w_ada.shape[1]

    def body(c_ref, w_ref, b_ref, o_ref):
        _, me = _flip(0)
        bias = b_ref[:, pl.ds(pl.multiple_of(me * n_cols, 128), n_cols)]
        o_ref[...] = _dot_f32(_silu(c_ref[...]), w_ref[...]) + bias

    return pl.pallas_call(
        body, name="ada_rows", out_shape=jax.ShapeDtypeStruct((N_DEV, n_cols), F32),
        in_specs=[pl.BlockSpec(memory_space=pltpu.VMEM)] * 3, out_specs=pl.BlockSpec(memory_space=pltpu.VMEM),
    )(c_all, w_ada, b_ada)


def _in_fwd(x, mod, g1, w_in, logits, onorm_g, after=()):
    s = x.shape[0]
    tm = HG_TILE

    def body(x_ref, mod_ref, g_ref, w_ref, lg_ref, og_ref, h_ref, *rest):
        groups, (out_ref, o_ref, st_ref, state, qf_s, kk_s, lf_s) = rest[:7], rest[7:]
        xhat, _ = _rms(x_ref[...])
        h = (xhat * g_ref[...]) * (1.0 + mod_ref[:, D_MODEL:2 * D_MODEL]) + mod_ref[:, 0:D_MODEL]
        hb = h.astype(BF16)
        h_ref[...] = hb
        for j, o_ref_j in enumerate(groups):
            o_ref_j[...] = _dot_nt(hb, w_ref[j * 512:(j + 1) * 512, :])
        _hg_fwd_tile(*groups[:4], lg_ref, og_ref, out_ref, o_ref, st_ref, state, qf_s, kk_s, lf_s)

    tile = _rows(tm, 512)
    return pl.pallas_call(
        _behind(body, after), name="in_fwd", grid=(s // tm,),
        out_shape=[jax.ShapeDtypeStruct((s, D_MODEL), BF16)] + [jax.ShapeDtypeStruct((s, 512), F32)] * 7
        + [jax.ShapeDtypeStruct((s, HG_WIDTH), BF16), jax.ShapeDtypeStruct((s, HG_WIDTH), F32),
           jax.ShapeDtypeStruct((s // HG_CHUNK * HG_DIM, HG_WIDTH), F32)],
        in_specs=_ANY * len(after)
        + [_rows(tm, D_MODEL), _whole((1, 6 * D_MODEL)), _whole((1, D_MODEL)), _whole((IN_WIDTH, D_MODEL)),
           _whole((2, HG_WIDTH)), _whole((1, HG_DIM))],
        out_specs=[_rows(tm, D_MODEL)] + [tile] * 7 + [tile, tile, _rows(HG_TILE_CHUNKS * HG_DIM, HG_WIDTH)],
        scratch_shapes=[pltpu.VMEM((HG_HEADS, HG_DIM, HG_DIM), F32)] + [pltpu.VMEM((HG_TILE, HG_WIDTH), F32)] * 3,
        compiler_params=_params("arbitrary"),
    )(*after, x, mod, g1, w_in, logits, onorm_g)


def _in_bwd(x, dx1, mod, g1, w_in, dps):
    s = x.shape[0]
    tm = 256

    def body(x_ref, dx_ref, mod_ref, g_ref, w_ref, *rest):
        dp_refs, (gx_ref, dpb_ref, dsh_ref, dsc_ref, dg_ref) = rest[:13], rest[13:]
        pieces = [dp_refs[j][...] for j in range(4)]
        pieces += [dp_refs[4 + 3 * j][...] + dp_refs[5 + 3 * j][...] + dp_refs[6 + 3 * j][...] for j in range(3)]
        for j, p in enumerate(pieces):
            dpb_ref[:, j * 512:(j + 1) * 512] = p.astype(BF16)
        dh = _dot(dpb_ref[...], w_ref[...])
        xhat, rstd = _rms(x_ref[...])
        g = g_ref[...]
        scale1 = 1.0 + mod_ref[:, D_MODEL:2 * D_MODEL]
        n1 = xhat * g

        @pl.when(pl.program_id(0) == 0)
        def _():
            dsh_ref[...] = jnp.zeros_like(dsh_ref)
            dsc_ref[...] = jnp.zeros_like(dsc_ref)
            dg_ref[...] = jnp.zeros_like(dg_ref)

        dsh_ref[...] += _rowsum(dh)
        dsc_ref[...] += _rowsum(dh * n1)
        dn = dh * scale1
        dg_ref[...] += _rowsum(dn * xhat)
        gx_ref[...] = dx_ref[...] + _rms_bwd(dn * g, xhat, rstd)

    vec = _whole((1, D_MODEL))
    return pl.pallas_call(
        body, name="in_bwd", grid=(s // tm,),
        out_shape=[jax.ShapeDtypeStruct((s, D_MODEL), F32), jax.ShapeDtypeStruct((s, IN_WIDTH), BF16)]
        + [jax.ShapeDtypeStruct((1, D_MODEL), F32)] * 3,
        in_specs=[_rows(tm, D_MODEL), _rows(tm, D_MODEL), _whole((1, 6 * D_MODEL)), vec, _whole((IN_WIDTH, D_MODEL))]
        + [_rows(tm, 512)] * 13,
        out_specs=[_rows(tm, D_MODEL), _rows(tm, IN_WIDTH), vec, vec, vec],
        compiler_params=_params("arbitrary"),
    )(x, dx1, mod, g1, w_in, *dps)


HG_TILE = 512
HG_TILE_CHUNKS = HG_TILE // HG_CHUNK


def _lower_bound(lg_ref):
    return 1.0 / (1.0 + jnp.exp(lg_ref[1:2, :] - lg_ref[0:1, :]))


def _chunk_masks():
    r = lax.broadcasted_iota(jnp.int32, (HG_CHUNK, HG_CHUNK), 0)
    c = lax.broadcasted_iota(jnp.int32, (HG_CHUNK, HG_CHUNK), 1)
    return r >= c, c >= r, (r >= c).astype(F32), (c >= r).astype(F32)


def _hg_fwd_tile(q_ref, f_ref, i_ref, g_ref, lg_ref, og_ref, out_ref, o_ref, st_ref, state, qf_s, kk_s, lf_s):
    @pl.when(pl.program_id(0) == 0)
    def _():
        state[...] = jnp.zeros_like(state)

    lb = _lower_bound(lg_ref)
    f = lb + (1.0 - lb) * _sigmoid(f_ref[...])
    kk_s[...] = 1.0 - f
    lf_s[...] = jnp.log(f)
    qf_s[...] = _silu(q_ref[...])
    causal, _, tri, _ = _chunk_masks()

    def chunk(ci, carry):
        rows = pl.ds(pl.multiple_of(ci * HG_CHUNK, HG_CHUNK), HG_CHUNK)
        srows = pl.ds(pl.multiple_of(ci * HG_DIM, HG_DIM), HG_DIM)
        lf = lf_s[rows, :]
        b = _dot_f32(tri, lf)
        bl = _rowsum(lf)
        ref = 0.5 * bl
        qf, kk, v = qf_s[rows, :], kk_s[rows, :], i_ref[rows, :]
        a_in = (qf * jnp.exp(b)).astype(BF16)
        a_t = (qf * jnp.exp(b - ref)).astype(BF16)
        b_t = (kk * jnp.exp(ref - b)).astype(BF16)
        kd = kk * jnp.exp(bl - b)
        ebl = jnp.exp(bl)
        vb = v.astype(BF16)
        for h in range(HG_HEADS):
            c = slice(h * HG_DIM, (h + 1) * HG_DIM)
            st = state[h]
            st_ref[srows, c] = st
            p = jnp.where(causal, _dot_nt(a_t[:, c], b_t[:, c]), 0.0)
            o_ref[rows, c] = _dot(p.astype(BF16), vb[:, c]) + _dot_nt(a_in[:, c], st.astype(BF16))
            state[h] = st * ebl[:, c] + _dot_tn(vb[:, c], kd[:, c].astype(BF16))
        return carry

    lax.fori_loop(0, HG_TILE_CHUNKS, chunk, 0, unroll=True)
    for h in range(HG_HEADS):
        c = slice(h * HG_DIM, (h + 1) * HG_DIM)
        ohat, _ = _rms(o_ref[:, c])
        out_ref[:, c] = (ohat * og_ref[...] * _silu(g_ref[:, c])).astype(BF16)


def _hg_bwd(hq, hf, hi, hgt, logits, onorm_g, o, states, dout):
    s = hq.shape[0]
    n_tiles = s // HG_TILE

    def body(q_ref, f_ref, i_ref, g_ref, lg_ref, og_ref, o_ref, st_ref, d_ref,
             dq_ref, df_ref, di_ref, dg_ref, dog_ref, dlb_ref, dstate, qf_s, kk_s, lf_s, do_s):
        @pl.when(pl.program_id(0) == 0)
        def _():
            dstate[...] = jnp.zeros_like(dstate)
            dog_ref[...] = jnp.zeros_like(dog_ref)
            dlb_ref[...] = jnp.zeros_like(dlb_ref)

        og = og_ref[...]
        dog = jnp.zeros((1, HG_DIM), F32)
        for h in range(HG_HEADS):
            c = slice(h * HG_DIM, (h + 1) * HG_DIM)
            ohat, rstd = _rms(o_ref[:, c])
            gate = g_ref[:, c]
            d = d_ref[:, c]
            dg_ref[:, c] = (d * (ohat * og) * _dsilu(gate)).astype(BF16)
            dnormed = d * _silu(gate)
            dog += _rowsum(dnormed * ohat)
            do_s[:, c] = _rms_bwd(dnormed * og, ohat, rstd)
        dog_ref[...] += dog

        lb = _lower_bound(lg_ref)
        f = lb + (1.0 - lb) * _sigmoid(f_ref[...])
        kk_s[...] = 1.0 - f
        lf_s[...] = jnp.log(f)
        qf_s[...] = _silu(q_ref[...])
        causal, upper, tri, tri_t = _chunk_masks()

        def chunk(step, carry):
            ci = HG_TILE_CHUNKS - 1 - step
            rows = pl.ds(pl.multiple_of(ci * HG_CHUNK, HG_CHUNK), HG_CHUNK)
            srows = pl.ds(pl.multiple_of(ci * HG_DIM, HG_DIM), HG_DIM)
            lf = lf_s[rows, :]
            b = _dot_f32(tri, lf)
            bl = _rowsum(lf)
            ref = 0.5 * bl
            qf, kk, v, do = qf_s[rows, :], kk_s[rows, :], i_ref[rows, :], do_s[rows, :]
            eb, ebr, erb, ekd, ebl = jnp.exp(b), jnp.exp(b - ref), jnp.exp(ref - b), jnp.exp(bl - b), jnp.exp(bl)
            a_in, a_t, b_t, kd = qf * eb, qf * ebr, kk * erb, kk * ekd
            for h in range(HG_HEADS):
                c = slice(h * HG_DIM, (h + 1) * HG_DIM)
                st, dst = st_ref[srows, c], dstate[h]
                stb, dstb = st.astype(BF16), dst.astype(BF16)
                doh, vh = do[:, c], v[:, c]
                dob, vb = doh.astype(BF16), vh.astype(BF16)
                ain_h, at_h, bt_h, kd_h = a_in[:, c], a_t[:, c], b_t[:, c], kd[:, c]
                atb, btb = at_h.astype(BF16), bt_h.astype(BF16)
                d_ain = _dot(dob, stb)
                p_t = jnp.where(upper, _dot_nt(btb, atb), 0.0).astype(BF16)
                dp = jnp.where(causal, _dot_nt(dob, vb), 0.0).astype(BF16)
                dp_t = jnp.where(upper, _dot_nt(vb, dob), 0.0).astype(BF16)
                di_ref[rows, c] = (_dot(p_t, dob) + _dot_nt(kd_h.astype(BF16), dstb)).astype(BF16)
                d_at = _dot(dp, btb)
                d_bt = _dot(dp_t, atb)
                d_kd = _dot(vb, dstb)
                dqf = d_ain * eb[:, c] + d_at * ebr[:, c]
                dkk = d_bt * erb[:, c] + d_kd * ekd[:, c]
                db = d_ain * ain_h + d_at * atb.astype(F32) - d_bt * btb.astype(F32) - d_kd * kd_h
                dbl = _rowsum(d_kd * kd_h) + _rowsum(dst * st) * ebl[:, c]
                dstate[h] = _dot_tn(dob, ain_h.astype(BF16)) + dst * ebl[:, c]
                dlf = _dot_f32(tri_t, db) + dbl
                qv, fr = q_ref[rows, c], f_ref[rows, c]
                lbh = lb[:, c]
                sg = _sigmoid(fr)
                dfv = dlf / (lbh + (1.0 - lbh) * sg) - dkk
                df_ref[rows, c] = (dfv * (1.0 - lbh) * sg * (1.0 - sg)).astype(BF16)
                dlb_ref[:, c] += _rowsum(dfv * (1.0 - sg))
                dq_ref[rows, c] = (dqf * _dsilu(qv)).astype(BF16)
            return carry

        lax.fori_loop(0, HG_TILE_CHUNKS, chunk, 0, unroll=True)

    rev = pl.BlockSpec((HG_TILE, HG_WIDTH), lambda i: (n_tiles - 1 - i, 0))
    return pl.pallas_call(
        body, name="hg_bwd", grid=(n_tiles,),
        out_shape=[jax.ShapeDtypeStruct((s, HG_WIDTH), BF16)] * 4
        + [jax.ShapeDtypeStruct((1, HG_DIM), F32), jax.ShapeDtypeStruct((1, HG_WIDTH), F32)],
        in_specs=[rev] * 4 + [_whole((2, HG_WIDTH)), _whole((1, HG_DIM)), rev,
                              pl.BlockSpec((HG_TILE_CHUNKS * HG_DIM, HG_WIDTH), lambda i: (n_tiles - 1 - i, 0)), rev],
        out_specs=[rev] * 4 + [_whole((1, HG_DIM)), _whole((1, HG_WIDTH))],
        scratch_shapes=[pltpu.VMEM((HG_HEADS, HG_DIM, HG_DIM), F32)] + [pltpu.VMEM((HG_TILE, HG_WIDTH), F32)] * 4,
        compiler_params=_params("arbitrary"),
    )(hq, hf, hi, hgt, logits, onorm_g, o, states, dout)


TOKEN_GROUP = 16


def _att_geometry(dil, seq=0):
    per_group = TOKEN_GROUP // dil
    ub = ATT_BLOCK // per_group
    if dil == TOKEN_GROUP:
        n_blocks = 2 if seq % (2 * ub * TOKEN_GROUP) == 0 and seq > 0 else 1
    else:
        n_blocks = 4
    return per_group, ub, ATT_WIDTH if dil == 1 else 128, n_blocks


def _att_consts(dil):
    per_group, ub = _att_geometry(dil)[:2]

    def pos(i):
        return i if dil == 1 else (i % ub) * per_group + i // ub

    lane = lax.broadcasted_iota(jnp.int32, (ATT_BLOCK, 128), 1)
    qi = pos(lax.broadcasted_iota(jnp.int32, (2 * ATT_BLOCK, ATT_BLOCK), 0) % ATT_BLOCK)
    kj = pos(lax.broadcasted_iota(jnp.int32, (2 * ATT_BLOCK, ATT_BLOCK), 1))
    return lane < ATT_HEAD_DIM, kj <= qi, lambda off: kj >= qi + off


def _load_tile(ref, dil, r, c, base=0):
    per_group, ub = _att_geometry(dil)[:2]
    if dil == 1:
        return ref[base:base + ATT_BLOCK, c]
    return jnp.concatenate([ref[pl.ds(base + dil * w + r, ub, stride=TOKEN_GROUP), c] for w in range(per_group)], axis=0)


def _store_tile(ref, dil, r, c, val, base=0):
    per_group, ub = _att_geometry(dil)[:2]
    if dil == 1:
        ref[base:base + ATT_BLOCK, c] = val
        return
    for w in range(per_group):
        ref[pl.ds(base + dil * w + r, ub, stride=TOKEN_GROUP), c] = val[w * ub:(w + 1) * ub]


def _stack_heads(x2, first):
    return jnp.concatenate([jnp.where(first, x2, 0.0), jnp.where(first, 0.0, x2)], axis=0)


def _stack_bcast(x2, first):
    other = pltpu.roll(x2, ATT_HEAD_DIM, axis=1)
    return jnp.concatenate([jnp.where(first, x2, other), jnp.where(first, other, x2)], axis=0)


def _unstack_heads(st, first):
    return jnp.where(first, st[:ATT_BLOCK], st[ATT_BLOCK:])


def _att_fwd(q, k, v, dil):
    seq, width = q.shape
    _, ub, lanes, nbs = _att_geometry(dil, seq)
    rows = ub * TOKEN_GROUP
    n_steps = seq // (nbs * rows)

    def body(q_ref, k_ref, v_ref, kp_ref, vp_ref, o_ref, lse_ref):
        first, cur_ok, _band = _att_consts(dil)
        inner_ok = _band(0)
        edge_ok = _band(jnp.where(pl.program_id(0) > 0, 0, ATT_BLOCK))
        for r in range(dil):
            for j in range(lanes // 128):
                c = slice(j * 128, (j + 1) * 128)
                kc = vc = None
                for b in range(nbs):
                    base = b * rows
                    prev_ok = edge_ok if b == 0 else inner_ok
                    if b == 0:
                        kp, vp = _load_tile(kp_ref, dil, r, c).astype(BF16), _load_tile(vp_ref, dil, r, c).astype(BF16)
                    else:
                        kp, vp = kc, vc
                    qst = _stack_heads(_load_tile(q_ref, dil, r, c, base) * ATT_SCALE, first).astype(BF16)
                    kc = _load_tile(k_ref, dil, r, c, base).astype(BF16)
                    vc = _load_tile(v_ref, dil, r, c, base).astype(BF16)
                    sc = jnp.where(cur_ok, _dot_nt(qst, kc), NEG)
                    sp = jnp.where(prev_ok, _dot_nt(qst, kp), NEG)
                    mx = jnp.max(jnp.maximum(sc, sp), axis=-1, keepdims=True)
                    pc, pp = jnp.exp(sc - mx), jnp.exp(sp - mx)
                    den = jnp.sum(pc + pp, axis=-1, keepdims=True)
                    ost = (_dot(pc.astype(BF16), vc) + _dot(pp.astype(BF16), vp)) / den
                    lse = jnp.broadcast_to(mx + jnp.log(den), (2 * ATT_BLOCK, 128))
                    _store_tile(o_ref, dil, r, c, _unstack_heads(ost, first), base)
                    _store_tile(lse_ref, dil, r, c, _unstack_heads(lse, first), base)

    slab = pl.BlockSpec((nbs * rows, lanes), lambda n, j: (n, j))
    before = pl.BlockSpec((rows, lanes), lambda n, j: (jnp.maximum(n * nbs - 1, 0), j))
    return pl.pallas_call(
        body, name=f"att_fwd_d{dil}", grid=(n_steps, width // lanes),
        out_shape=[jax.ShapeDtypeStruct((seq, width), F32)] * 2,
        in_specs=[slab, slab, slab, before, before], out_specs=[slab, slab],
        compiler_params=_params("arbitrary", "arbitrary"),
    )(q, k, v, k, v)


def _att_bwd(q, k, v, do, cc, lse, dil):
    seq, width = q.shape
    _, ub, lanes, nbs = _att_geometry(dil, seq)
    rows = ub * TOKEN_GROUP
    n_blocks = seq // rows
    n_steps = n_blocks // nbs

    def body(q_ref, k_ref, v_ref, do_ref, cc_ref, lse_ref, qx_ref, dox_ref, ccx_ref, lsex_ref,
             dq_ref, dk_ref, dv_ref, carry):
        first, cur_ok, _band = _att_consts(dil)
        step = pl.program_id(1)
        inner_ok = _band(0)
        edge_ok = _band(jnp.where(step < n_steps - 1, 0, ATT_BLOCK))

        @pl.when(step == 0)
        def _():
            carry[...] = jnp.zeros_like(carry)

        def queries(refs, r, c, base):
            q_r, do_r, lse_r, cc_r = refs
            return (_stack_heads(_load_tile(q_r, dil, r, c, base) * ATT_SCALE, first).astype(BF16),
                    _stack_heads(_load_tile(do_r, dil, r, c, base), first).astype(BF16),
                    _stack_bcast(_load_tile(lse_r, dil, r, c, base), first),
                    _stack_bcast(_load_tile(cc_r, dil, r, c, base), first))

        for r in range(dil):
            for j in range(lanes // 128):
                c = slice(j * 128, (j + 1) * 128)
                own = queries((q_ref, do_ref, lse_ref, cc_ref), r, c, 0)
                left = _load_tile(carry, dil, r, c)
                for b in range(nbs):
                    base = b * rows
                    last = b == nbs - 1
                    next_ok = edge_ok if last else inner_ok
                    if last:
                        following = queries((qx_ref, dox_ref, lsex_ref, ccx_ref), r, c, 0)
                    else:
                        following = queries((q_ref, do_ref, lse_ref, cc_ref), r, c, base + rows)
                    (qst, dost, lse_n, cc_n), (qxst, doxst, lse_x, cc_x) = own, following
                    kb = _load_tile(k_ref, dil, r, c, base).astype(BF16)
                    vb = _load_tile(v_ref, dil, r, c, base).astype(BF16)
                    p_cur = jnp.exp(jnp.where(cur_ok, _dot_nt(qst, kb), NEG) - lse_n)
                    p_next = jnp.exp(jnp.where(next_ok, _dot_nt(qxst, kb), NEG) - lse_x)
                    ds_cur = (p_cur * (_dot_nt(dost, vb) + cc_n)).astype(BF16)
                    ds_next = (p_next * (_dot_nt(doxst, vb) + cc_x)).astype(BF16)
                    dq_own = left + _unstack_heads(_dot(ds_cur, kb), first)
                    _store_tile(dq_ref, dil, r, c, dq_own * ATT_SCALE, base)
                    _store_tile(dk_ref, dil, r, c, _dot_tn(ds_cur, qst) + _dot_tn(ds_next, qxst), base)
                    _store_tile(dv_ref, dil, r, c, _dot_tn(p_cur.astype(BF16), dost) + _dot_tn(p_next.astype(BF16), doxst), base)
                    left = _unstack_heads(_dot(ds_next, kb), first)
                    own = following
                _store_tile(carry, dil, r, c, left)

    slab = pl.BlockSpec((nbs * rows, lanes), lambda j, n: (n, j))
    after = pl.BlockSpec((rows, lanes), lambda j, n: (jnp.minimum((n + 1) * nbs, n_blocks - 1), j))
    return pl.pallas_call(
        body, name=f"att_bwd_d{dil}", grid=(width // lanes, n_steps),
        out_shape=[jax.ShapeDtypeStruct((seq, width), F32)] * 3,
        in_specs=[slab] * 6 + [after] * 4, out_specs=[slab] * 3,
        scratch_shapes=[pltpu.VMEM((rows, lanes), F32)],
        compiler_params=_params("arbitrary", "arbitrary"),
    )(q, k, v, do, cc, lse, q, do, cc, lse)


def _branch_weights(lses):
    mx = jnp.maximum(jnp.maximum(lses[0], lses[1]), lses[2])
    es = [jnp.exp(l - mx) for l in lses]
    inv = 1.0 / (es[0] + es[1] + es[2])
    return [e * inv for e in es]


def _att_combine(outs, lses, att_g, after=()):
    s = outs[0].shape[0]
    tm = 512

    def body(o0, o1, o2, l0, l1, l2, g_ref, att_ref, out_ref):
        ws = _branch_weights([l0[...], l1[...], l2[...]])
        att = ws[0] * o0[...] + ws[1] * o1[...] + ws[2] * o2[...]
        att_ref[...] = att
        ahat, _ = _rms(att)
        out_ref[...] = (ahat * g_ref[...]).astype(BF16)

    tile = _rows(tm, ATT_WIDTH)
    return pl.pallas_call(
        _behind(body, after), name="att_combine", grid=(s // tm,),
        out_shape=[jax.ShapeDtypeStruct((s, ATT_WIDTH), F32), jax.ShapeDtypeStruct((s, ATT_WIDTH), BF16)],
        in_specs=_ANY * len(after) + [tile] * 6 + [_whole((1, ATT_WIDTH))], out_specs=[tile, tile],
        compiler_params=_params("parallel"),
    )(*after, *outs, *lses, att_g)


def _att_combine_bwd(datt_out, att, lses, att_g, after=()):
    s = att.shape[0]
    tm = 256

    def body(d_ref, att_ref, l0, l1, l2, g_ref, do0, do1, do2, cc0, cc1, cc2, dg_ref):
        @pl.when(pl.program_id(0) == 0)
        def _():
            dg_ref[...] = jnp.zeros_like(dg_ref)

        att = att_ref[...]
        ahat, rstd = _rms(att)
        d = d_ref[...]
        dg_ref[...] += _rowsum(d * ahat)
        datt = _rms_bwd(d * g_ref[...], ahat, rstd)
        hi = lax.broadcasted_iota(jnp.int32, (ATT_WIDTH, ATT_WIDTH), 0) // ATT_HEAD_DIM
        hj = lax.broadcasted_iota(jnp.int32, (ATT_WIDTH, ATT_WIDTH), 1) // ATT_HEAD_DIM
        same_head = (hi == hj).astype(BF16)
        prod = datt * att
        prod_hi = prod.astype(BF16)
        prod_lo = (prod - prod_hi.astype(F32)).astype(BF16)
        head_sum = _dot(prod_hi, same_head) + _dot(prod_lo, same_head)
        ws = _branch_weights([l0[...], l1[...], l2[...]])
        for w, do_ref, cc_ref in zip(ws, (do0, do1, do2), (cc0, cc1, cc2)):
            do_ref[...] = w * datt
            cc_ref[...] = -w * head_sum

    tile = _rows(tm, ATT_WIDTH)
    return pl.pallas_call(
        _behind(body, after), name="att_combine_bwd", grid=(s // tm,),
        out_shape=[jax.ShapeDtypeStruct((s, ATT_WIDTH), F32)] * 6 + [jax.ShapeDtypeStruct((1, ATT_WIDTH), F32)],
        in_specs=_ANY * len(after) + [tile] * 5 + [_whole((1, ATT_WIDTH))], out_specs=[tile] * 6 + [_whole((1, ATT_WIDTH))],
        compiler_params=_params("arbitrary"),
    )(*after, datt_out, att, *lses, att_g)


def _out_bwd(dx1, hg, at, mod, w_out):
    s = dx1.shape[0]
    tm = 512
    n_steps = s // tm

    def body(dx_ref, hg_ref, at_ref, mod_ref, w_ref, dhg_ref, dat_ref, dw_ref, dwb_ref, dgate_ref):
        @pl.when(pl.program_id(0) == 0)
        def _():
            dw_ref[...] = jnp.zeros_like(dw_ref)
            dgate_ref[...] = jnp.zeros_like(dgate_ref)

        hg, at, dx = hg_ref[...], at_ref[...], dx_ref[...]
        mix = _dot(hg, w_ref[0:512, :]) + _dot(at, w_ref[512:1024, :])
        dgate_ref[...] += _rowsum(dx * mix)
        dmix = (mod_ref[:, 2 * D_MODEL:3 * D_MODEL] * dx).astype(BF16)
        dhg_ref[...] = _dot_nt(dmix, w_ref[0:512, :])
        dat_ref[...] = _dot_nt(dmix, w_ref[512:1024, :])
        dw_ref[0:512, :] += _dot_tn(hg, dmix)
        dw_ref[512:1024, :] += _dot_tn(at, dmix)

        @pl.when(pl.program_id(0) == n_steps - 1)
        def _():
            dwb_ref[...] = dw_ref[...].astype(BF16)

    return pl.pallas_call(
        body, name="out_bwd", grid=(n_steps,),
        out_shape=[jax.ShapeDtypeStruct((s, 512), F32)] * 2
        + [jax.ShapeDtypeStruct((D_MODEL, D_MODEL), F32), jax.ShapeDtypeStruct((D_MODEL, D_MODEL), BF16),
           jax.ShapeDtypeStruct((1, D_MODEL), F32)],
        in_specs=[_rows(tm, D_MODEL), _rows(tm, 512), _rows(tm, 512), _whole((1, 6 * D_MODEL)), _whole((D_MODEL, D_MODEL))],
        out_specs=[_rows(tm, 512), _rows(tm, 512), _whole((D_MODEL, D_MODEL)), _whole((D_MODEL, D_MODEL)), _whole((1, D_MODEL))],
        compiler_params=_params("arbitrary"),
    )(dx1, hg, at, mod, w_out)


def _ffn(x, hg, at, target, mod, g2, gf, w_out, w_gu, w_down):
    s = x.shape[0]
    tm = 256

    def body(x_ref, hg_ref, at_ref, t_ref, mod_ref, g2_ref, gf_ref, wo_ref, wgu_hbm, wd_hbm,
             dx_ref, h2_ref, act_ref, dau_ref, dff_ref, sums_ref, loss_ref, wgu, wd, au_s, sem):
        @pl.when(pl.program_id(0) == 0)
        def _():
            c1 = pltpu.make_async_copy(wgu_hbm, wgu, sem.at[0])
            c2 = pltpu.make_async_copy(wd_hbm, wd, sem.at[1])
            c1.start()
            c2.start()
            c1.wait()
            c2.wait()
            sums_ref[...] = jnp.zeros_like(sums_ref)
            loss_ref[...] = jnp.zeros_like(loss_ref)

        mix = _dot(hg_ref[...], wo_ref[0:512, :]) + _dot(at_ref[...], wo_ref[512:1024, :])
        x1v = x_ref[...] + mod_ref[:, 2 * D_MODEL:3 * D_MODEL] * mix
        xhat, rstd = _rms(x1v)
        g2 = g2_ref[...]
        n2 = xhat * g2
        scale2 = 1.0 + mod_ref[:, 4 * D_MODEL:5 * D_MODEL]
        gate2 = mod_ref[:, 5 * D_MODEL:6 * D_MODEL]
        hb = (n2 * scale2 + mod_ref[:, 3 * D_MODEL:4 * D_MODEL]).astype(BF16)
        h2_ref[...] = hb
        au_s[...] = _dot_nt(hb, wgu[...])
        a = au_s[:, 0:D_FF]
        act = (_silu(a) * au_s[:, D_FF:2 * D_FF]).astype(BF16)
        act_ref[...] = act
        ff = _dot(act, wd[...])
        x2 = x1v + gate2 * ff
        nf, rstd_f = _rms(x2)
        gfv = gf_ref[...]
        err = nf * gfv - t_ref[...]
        loss_ref[...] += 0.5 * jnp.sum(_rowsum(err * err), axis=-1, keepdims=True) * (1.0 / D_MODEL)
        dy = err * (1.0 / D_MODEL)
        dx2 = _rms_bwd(dy * gfv, nf, rstd_f)
        dffb = (gate2 * dx2).astype(BF16)
        dff_ref[...] = dffb
        dact = _dot_nt(dffb, wd[...])
        a = au_s[:, 0:D_FF]
        dau_ref[:, 0:D_FF] = (dact * au_s[:, D_FF:2 * D_FF] * _dsilu(a)).astype(BF16)
        dau_ref[:, D_FF:2 * D_FF] = (dact * _silu(a)).astype(BF16)
        dh = _dot(dau_ref[...], wgu[...])
        dn = dh * scale2
        sums_ref[0:1, :] += _rowsum(dh)
        sums_ref[1:2, :] += _rowsum(dh * n2)
        sums_ref[2:3, :] += _rowsum(dx2 * ff)
        sums_ref[3:4, :] += _rowsum(dn * xhat)
        sums_ref[4:5, :] += _rowsum(dy * nf)
        dx_ref[...] = dx2 + _rms_bwd(dn * g2, xhat, rstd)

    vec = _whole((1, D_MODEL))
    hbm = pl.BlockSpec(memory_space=pl.ANY)
    return pl.pallas_call(
        body, name="ffn", grid=(s // tm,),
        out_shape=[jax.ShapeDtypeStruct((s, D_MODEL), F32), jax.ShapeDtypeStruct((s, D_MODEL), BF16),
                   jax.ShapeDtypeStruct((s, D_FF), BF16), jax.ShapeDtypeStruct((s, 2 * D_FF), BF16),
                   jax.ShapeDtypeStruct((s, D_MODEL), BF16), jax.ShapeDtypeStruct((8, D_MODEL), F32),
                   jax.ShapeDtypeStruct((1, 128), F32)],
        in_specs=[_rows(tm, D_MODEL), _rows(tm, 512), _rows(tm, 512), _rows(tm, D_MODEL), _whole((1, 6 * D_MODEL)), vec, vec,
                  _whole((D_MODEL, D_MODEL)), hbm, hbm],
        out_specs=[_rows(tm, D_MODEL), _rows(tm, D_MODEL), _rows(tm, D_FF), _rows(tm, 2 * D_FF), _rows(tm, D_MODEL),
                   _whole((8, D_MODEL)), _whole((1, 128))],
        scratch_shapes=[pltpu.VMEM((2 * D_FF, D_MODEL), BF16), pltpu.VMEM((D_FF, D_MODEL), BF16),
                        pltpu.VMEM((tm, 2 * D_FF), F32), pltpu.SemaphoreType.DMA((2,))],
        compiler_params=_params("arbitrary"),
    )(x, hg, at, target, mod, g2, gf, w_out, w_gu, w_down)


def _weight_grad(a, b, name, rounded=False):
    s, m = a.shape
    n = b.shape[1]
    ts = min(s, 2048)
    n_steps = s // ts
    tm = max(t for t in range(128, m + 1, 128) if m % t == 0 and t * n * 4 <= 6 * 1024 * 1024)

    def body(a_ref, b_ref, o_ref, *ob_ref):
        @pl.when(pl.program_id(1) == 0)
        def _():
            o_ref[...] = jnp.zeros_like(o_ref)

        o_ref[...] += _dot_tn(a_ref[...], b_ref[...])
        if rounded:
            @pl.when(pl.program_id(1) == n_steps - 1)
            def _():
                ob_ref[0][...] = o_ref[...].astype(BF16)

    tile = pl.BlockSpec((tm, n), lambda j, i: (j, 0))
    return pl.pallas_call(
        body, name=name, grid=(m // tm, n_steps),
        out_shape=[jax.ShapeDtypeStruct((m, n), F32)] + [jax.ShapeDtypeStruct((m, n), BF16)] * rounded,
        in_specs=[pl.BlockSpec((ts, tm), lambda j, i: (i, j)), pl.BlockSpec((ts, n), lambda j, i: (i, 0))],
        out_specs=[tile] + [tile] * rounded,
        compiler_params=_params("parallel", "arbitrary"),
    )(a, b)


def _adamw_math(w, g, m, v):
    m = ADAM_B1 * m + (1.0 - ADAM_B1) * g
    v = ADAM_B2 * v + (1.0 - ADAM_B2) * (g * g)
    m_hat = m / (1.0 - ADAM_B1 ** ADAM_STEP)
    v_hat = v / (1.0 - ADAM_B2 ** ADAM_STEP)
    delta = -ADAM_LR * (m_hat / (jnp.sqrt(v_hat) + ADAM_EPS) + ADAM_WD * w)
    return delta, m, v


def _adamw_shard(where, w, m, v, partial, got, name):
    r, c = w.shape
    tr = _shard_rows(r)
    lead = partial.ndim - 2
    n_got = got.shape[0]

    def body(where_ref, w_ref, m_ref, v_ref, own_ref, *rest):
        got_refs, (grad_ref, d_ref, nm_ref, nv_ref) = rest[:n_got], rest[n_got:]
        g = own_ref[...]
        for g_ref in got_refs:
            g = g + g_ref[...].astype(F32)
        grad_ref[...] = g
        d_ref[...], nm_ref[...], nv_ref[...] = _adamw_math(w_ref[...], g, m_ref[...], v_ref[...])

    tile = pl.BlockSpec((tr, c), lambda i, where_ref: (i, 0))
    own = pl.BlockSpec((None,) * lead + (tr, c), lambda i, where_ref: (*[where_ref[d] for d in range(lead)], i, 0))
    part = [pl.BlockSpec((None, tr, c), functools.partial(lambda j, i, where_ref: (j, i, 0), j)) for j in range(n_got)]
    return pl.pallas_call(
        body, name=name,
        grid_spec=pltpu.PrefetchScalarGridSpec(num_scalar_prefetch=1, grid=(r // tr,), in_specs=[tile] * 3 + [own] + part,
                                               out_specs=[tile] * 4),
        out_shape=[jax.ShapeDtypeStruct((r, c), F32)] * 4, compiler_params=_params("parallel"),
    )(where, w, m, v, partial, *[got] * n_got)


def _small_update(small_all, dmod_blocks, c_all, logits, w_ada, m_ada, v_ada, smalls, after=()):
    def body(sm_ref, dm_ref, c_ref, lg_ref, wa_ref, ma_ref, va_ref, *rest):
        ins, outs = rest[:21], rest[21:]
        _, me = _flip(0)
        tot = sm_ref[0:1, :]
        for i in range(1, N_DEV):
            tot = tot + sm_ref[i:i + 1, :]
        loss_ref = outs[0]
        loss_ref[...] = tot[:, SM_LOSS:SM_LOSS + 128]
        g_ada = lax.dot_general(_silu(c_ref[...]), dm_ref[me], (((0,), (0,)), ((), ())),
                                preferred_element_type=F32, precision=HIGHEST)
        outs[1][...] = g_ada
        outs[2][...], outs[3][...], outs[4][...] = _adamw_math(wa_ref[...], g_ada, ma_ref[...], va_ref[...])
        p0 = _lower_bound(lg_ref)
        dl0 = tot[:, SM_LB:SM_LB + 512] * p0 * (1.0 - p0)
        grads = [tot[:, SM_MOD:SM_MOD + 6 * D_MODEL], tot[:, SM_G1:SM_G1 + D_MODEL], tot[:, SM_G2:SM_G2 + D_MODEL],
                 tot[:, SM_GF:SM_GF + D_MODEL], tot[:, SM_ATT:SM_ATT + 512], tot[:, SM_HG:SM_HG + 128],
                 jnp.where(lax.broadcasted_iota(jnp.int32, (2, 512), 0) == 0, dl0, -dl0)]
        for i, g in enumerate(grads):
            w_ref, m_ref, v_ref = ins[3 * i:3 * i + 3]
            o = outs[5 + 4 * i:9 + 4 * i]
            o[0][...] = g
            o[1][...], o[2][...], o[3][...] = _adamw_math(w_ref[...], g, m_ref[...], v_ref[...])

    flat = [t for trio in smalls for t in trio]
    vm = pl.BlockSpec(memory_space=pltpu.VMEM)
    out_shape = [jax.ShapeDtypeStruct((1, 128), F32)] + [jax.ShapeDtypeStruct(w_ada.shape, F32)] * 4
    for trio in smalls:
        out_shape += [jax.ShapeDtypeStruct(trio[0].shape, F32)] * 4
    return pl.pallas_call(
        _behind(body, after), name="small_update", out_shape=out_shape,
        in_specs=_ANY * len(after) + [vm] * (7 + len(flat)), out_specs=[vm] * len(out_shape),
        compiler_params=pltpu.CompilerParams(vmem_limit_bytes=V7X_VMEM_LIMIT),
    )(*after, small_all, dmod_blocks, c_all, logits, w_ada, m_ada, v_ada, *flat)


def kernel(x, c, w_ada, b_ada, norm1_g, w_in, hg_lb_logits, hg_onorm_g, att_onorm_g, w_out, norm2_g, w_gate_up, w_down, final_g, loss_target, m_w_ada, m_b_ada, m_norm1_g, m_w_in, m_hg_lb_logits, m_hg_onorm_g, m_att_onorm_g, m_w_out, m_norm2_g, m_w_gate_up, m_w_down, m_final_g, v_w_ada, v_b_ada, v_norm1_g, v_w_in, v_hg_lb_logits, v_hg_onorm_g, v_att_onorm_g, v_w_out, v_norm2_g, v_w_gate_up, v_w_down, v_final_g):
    x2d, target = x[0], loss_target[0]
    seq = x2d.shape[0]
    assert seq % (ATT_BLOCK * max(DILATIONS)) == 0 and seq % HG_TILE == 0
    gf = final_g.reshape(1, D_MODEL)

    c_all = _exchange_small(c.reshape(8, D_MODEL // 8), None, "gather_c").reshape(N_DEV, D_MODEL)
    ada = _ada_rows(c_all, w_ada[0], b_ada)
    mod = _exchange_small(ada, 1, "scatter_mod").reshape(1, 6 * D_MODEL)

    core = lax.axis_index("c").astype(jnp.int32).reshape(1)
    chip = (2 * lax.axis_index("x") + lax.axis_index("y")).astype(jnp.int32).reshape(1)
    me = 4 * lax.axis_index("x") + 2 * lax.axis_index("y") + lax.axis_index("c")

    g_in, = _gather_weights([w_in[0].T.astype(BF16)])
    w_in_b = g_in.reshape(IN_WIDTH, D_MODEL)
    rest_shards = [w_out[0].astype(BF16), w_gate_up[0].T.astype(BF16), w_down[0].astype(BF16)]
    lands = [lax.empty((N_DEV,) + s.shape, BF16) for s in rest_shards]
    g_send, g_recv, g_srcs, g_lands, tok = _copies_start("gather_rest_start", _plan_gather_own, 12, rest_shards, lands, [w_in_b, mod])
    flight = {}

    def stage(name, *vals):
        if name == "attention_begun":
            flight["shards"], got = _copies_wait("gather_rest_wait", _plan_gather_own, g_send, g_recv, g_srcs, g_lands, list(vals))
            flight["pass"] = _copies_start("gather_pass_start", _plan_gather_pass, 9, [], got, [])
            return [flight["pass"][4]]
        if name == "mixer_weights_done":
            shapes = [(4, 2, D_MODEL // N_DEV, D_MODEL), (4, 2, 2 * D_FF // N_DEV, D_MODEL), (4, 2, D_FF // N_DEV, D_MODEL)]
            flight["grads"] = [g32.reshape(sh) for (g32, _), sh in zip(vals, shapes)]
            rounded = [g16.reshape(sh) for (_, g16), sh in zip(vals, shapes)]
            direct_lands = [lax.empty((N_DEV - 1,) + sh[2:], BF16) for sh in shapes]
            flight["direct"] = _copies_start("reduce_rest_start", _plan_reduce_direct, 21, rounded, direct_lands, [])
            return [flight["direct"][4]]
        raise ValueError(name)

    def rest_weights(after):
        s, r, _, p_lands, _ = flight["pass"]
        _, got = _copies_wait("gather_pass_wait", _plan_gather_pass, s, r, [], p_lands, [after])
        full = [lax.dynamic_update_index_in_dim(g, shard, me, 0) for g, shard in zip(got, flight["shards"])]
        return full[0].reshape(D_MODEL, D_MODEL), full[1].reshape(2 * D_FF, D_MODEL), full[2].reshape(D_FF, D_MODEL)

    grad_x, dw_in, small = _block_step(x2d, target, mod, norm1_g, hg_lb_logits, hg_onorm_g, att_onorm_g, norm2_g, gf,
                                       w_in_b, rest_weights, stage, [tok])

    g_in8 = dw_in.reshape(4, 2, IN_WIDTH // N_DEV, D_MODEL)
    in_pairs = _copies_start("reduce_pairs_in_start", _plan_reduce_pairs, 4, [g_in8], [lax.empty((4,) + g_in8.shape[2:], F32)], [])
    s, r, srcs, d_lands, _ = flight["direct"]
    _, recv_rest = _copies_wait("reduce_rest_wait", _plan_reduce_direct, s, r, srcs, d_lands, [in_pairs[4]])
    small_rows = jnp.pad(small, ((0, 0), (0, SM_PADDED - SM_WIDTH))).reshape(SM_PADDED // 128, 128)
    small_all = _exchange_small(small_rows, None, "gather_small", [in_pairs[4]]).reshape(N_DEV, SM_PADDED)[:, :SM_WIDTH]
    in_grads, got_in = _copies_wait("reduce_pairs_in_wait", _plan_reduce_pairs, in_pairs[0], in_pairs[1], in_pairs[2], in_pairs[3],
                                    [small_all])
    in_s32, in_s16 = _pair_sum(core, in_grads[0], got_in[0], "pair_sum_in")
    in_chips = _copies_start("reduce_chips_in_start", _plan_reduce_chips, 3, [in_s16], [lax.empty((3,) + in_s16.shape[1:], BF16)], [])
    big, updated = {}, []
    rest_params = [("w_out", w_out, m_w_out, v_w_out), ("w_gate_up", w_gate_up, m_w_gate_up, v_w_gate_up), ("w_down", w_down, m_w_down, v_w_down)]
    mine = jnp.concatenate([chip, core])
    for (n, w, m, v), g32, got in zip(rest_params, flight["grads"], recv_rest):
        if n == "w_gate_up":
            outs4 = _adamw_shard(mine, w[0].T, m[0].T, v[0].T, g32, got, f"adamw_{n}")
            big[n] = [t.T[None] for t in outs4]
        else:
            outs4 = _adamw_shard(mine, w[0], m[0], v[0], g32, got, f"adamw_{n}")
            big[n] = [t[None] for t in outs4]
        updated.append(outs4[3])
    smalls = [(b_ada, m_b_ada, v_b_ada), (norm1_g, m_norm1_g, v_norm1_g), (norm2_g, m_norm2_g, v_norm2_g),
              (gf, m_final_g.reshape(1, D_MODEL), v_final_g.reshape(1, D_MODEL)),
              (att_onorm_g, m_att_onorm_g, v_att_onorm_g), (hg_onorm_g, m_hg_onorm_g, v_hg_onorm_g),
              (hg_lb_logits, m_hg_lb_logits, v_hg_lb_logits)]
    dmod_blocks = small_all[:, :6 * D_MODEL].reshape(N_DEV, N_DEV, 6 * D_MODEL // N_DEV).transpose(1, 0, 2)
    res = _small_update(small_all, dmod_blocks, c_all, hg_lb_logits, w_ada[0], m_w_ada[0], v_w_ada[0], smalls, [in_chips[4]])
    _, recv_in = _copies_wait("reduce_chips_in_wait", _plan_reduce_chips, in_chips[0], in_chips[1], in_chips[2], in_chips[3],
                              [res[0]] + updated)
    big["w_in"] = [t.T[None] for t in _adamw_shard(chip, w_in[0].T, m_w_in[0].T, v_w_in[0].T, in_s32, recv_in[0], "adamw_w_in")]
    loss = res[0][0, 0]
    ada4 = [t[None] for t in res[1:5]]
    sm4 = {n: list(res[5 + 4 * i:9 + 4 * i]) for i, n in enumerate(["b_ada", "norm1_g", "norm2_g", "final_g", "att", "hg", "lb"])}
    sm4["final_g"] = [t.reshape(D_MODEL) for t in sm4["final_g"]]

    order = [ada4, sm4["b_ada"], sm4["norm1_g"], big["w_in"], sm4["lb"], sm4["hg"], sm4["att"], big["w_out"], sm4["norm2_g"],
             big["w_gate_up"], big["w_down"], sm4["final_g"]]
    return (loss, grad_x[None], *[o[0] for o in order], *[o[1] for o in order], *[o[2] for o in order], *[o[3] for o in order])


def _block_step(x2d, target, mod, norm1_g, hg_lb_logits, hg_onorm_g, att_onorm_g, norm2_g, gf, w_in_b, rest_weights, stage,
                after=()):
    h1, hq, hf, hi, hgt, aq, ak, av, hg_out, hg_o, hg_states = _in_fwd(x2d, mod, norm1_g, w_in_b, hg_lb_logits, hg_onorm_g, after)
    branch = [_att_fwd(aq, ak, av, d) for d in DILATIONS[:2]]
    behind = stage("attention_begun", branch[0][0], branch[1][0])
    branch += [_att_fwd(aq, ak, av, d) for d in DILATIONS[2:]]
    outs = [b[0] for b in branch]
    lses = [b[1] for b in branch]
    att, att_out = _att_combine(outs, lses, att_onorm_g, behind)
    w_out_b, w_gu_b, w_down_b = rest_weights(att_out)

    dx1, h2, act, dau, dff, ffn_sums, loss_part = _ffn(x2d, hg_out, att_out, target, mod, norm2_g, gf, w_out_b, w_gu_b, w_down_b)
    dw_gu = _weight_grad(dau, h2, "dw_gate_up", rounded=True)
    dw_down = _weight_grad(act, dff, "dw_down", rounded=True)

    dhg, dat, dw_out, dw_out_b, dgate1 = _out_bwd(dx1, hg_out, att_out, mod, w_out_b)
    behind = stage("mixer_weights_done", (dw_out, dw_out_b), dw_gu, dw_down)
    comb = _att_combine_bwd(dat, att, lses, att_onorm_g, behind)
    dos, ccs, d_att_g = comb[0:3], comb[3:6], comb[6]
    datt = []
    for i, d in enumerate(DILATIONS):
        datt.append(_att_bwd(aq, ak, av, dos[i], ccs[i], lses[i], d))
    dhq, dhf, dhi, dhgt, d_hg_g, d_lb = _hg_bwd(hq, hf, hi, hgt, hg_lb_logits, hg_onorm_g, hg_o, hg_states, dhg)
    dps = [dhq, dhf, dhi, dhgt] + [datt[i][j] for j in range(3) for i in range(3)]
    grad_x, dp_b, dshift1, dscale1, d_g1 = _in_bwd(x2d, dx1, mod, norm1_g, w_in_b, dps)
    dw_in, = _weight_grad(dp_b, h1, "dw_in")
    small = jnp.concatenate([dshift1, dscale1, dgate1, ffn_sums[0:1], ffn_sums[1:2], ffn_sums[2:3], d_g1, ffn_sums[3:4],
                             ffn_sums[4:5], d_att_g, d_lb, d_hg_g, loss_part], axis=1)
    return grad_x, dw_in, small
```

```python
import functools

import jax
import jax.numpy as jnp
from jax import lax
from jax.experimental import pallas as pl
from jax.experimental.pallas import tpu as pltpu

F32 = jnp.float32
BF16 = jnp.bfloat16
HIGHEST = lax.Precision.HIGHEST
MESH = pl.DeviceIdType.MESH

D_MODEL = 1024
N_DEV = 8
HG_HEADS = 4
HG_DIM = 128
HG_WIDTH = HG_HEADS * HG_DIM
HG_CHUNK = 128
ATT_WIDTH = 512
ATT_HEAD_DIM = 64
ATT_BLOCK = 128
DILATIONS = (1, 4, 16)
ATT_SCALE = ATT_HEAD_DIM ** -0.5
D_FF = 2816
IN_WIDTH = 7 * 512
RMS_EPS = 1e-6
NEG = -1e30

ADAM_LR = 0.001
ADAM_B1 = 0.9
ADAM_B2 = 0.999
ADAM_EPS = 1e-08
ADAM_WD = 0.01
ADAM_STEP = 10

V7X_VMEM_LIMIT = 56 * 1024 * 1024

SM_MOD = 0
SM_G1 = 6 * D_MODEL
SM_G2 = 7 * D_MODEL
SM_GF = 8 * D_MODEL
SM_ATT = 9 * D_MODEL
SM_LB = 9 * D_MODEL + 512
SM_HG = 10 * D_MODEL
SM_LOSS = 10 * D_MODEL + 128
SM_WIDTH = 10 * D_MODEL + 256
SM_PADDED = 88 * 128


def _params(*sem, vmem=V7X_VMEM_LIMIT):
    return pltpu.CompilerParams(dimension_semantics=sem, vmem_limit_bytes=vmem)


def _dot(a, b):
    return jnp.dot(a, b, preferred_element_type=F32)


def _dot_nt(a, b):
    return lax.dot_general(a, b, (((1,), (1,)), ((), ())), preferred_element_type=F32)


def _dot_tn(a, b):
    return lax.dot_general(a, b, (((0,), (0,)), ((), ())), preferred_element_type=F32)


def _dot_f32(a, b):
    return jnp.dot(a, b, preferred_element_type=F32, precision=HIGHEST)


def _sigmoid(x):
    return 1.0 / (1.0 + jnp.exp(-x))


def _silu(x):
    return x * _sigmoid(x)


def _dsilu(x):
    s = _sigmoid(x)
    return s * (1.0 + x * (1.0 - s))


def _rms(x):
    rstd = lax.rsqrt(jnp.mean(x * x, axis=-1, keepdims=True) + RMS_EPS)
    return x * rstd, rstd


def _rms_bwd(dn, xhat, rstd):
    return rstd * (dn - xhat * jnp.mean(dn * xhat, axis=-1, keepdims=True))


def _rowsum(x):
    return jnp.sum(x, axis=0, keepdims=True)


def _rows(tm, n):
    return pl.BlockSpec((tm, n), lambda i: (i, 0))


def _whole(shape):
    return pl.BlockSpec(shape, lambda i: (0,) * len(shape))


def _mesh_pos():
    return lax.axis_index("x"), lax.axis_index("y"), lax.axis_index("c")


def _flip(k):
    x, y, c = _mesh_pos()
    px = 1 - x if k & 4 else x
    py = 1 - y if k & 2 else y
    pc = 1 - c if k & 1 else c
    return (px, py, pc), 4 * px + 2 * py + pc


_ANY = [pl.BlockSpec(memory_space=pl.ANY)]


def _behind(body, after):
    return lambda *refs: body(*refs[len(after):])


def _exchange_small(x, rows_per_peer, name, after=()):
    r_all, cols = x.shape
    r_out = r_all if rows_per_peer is None else rows_per_peer

    def body(x_ref, out_ref, send_sems, recv_sems):
        _, me = _flip(0)

        def src(pid):
            if rows_per_peer is None:
                return x_ref
            return x_ref.at[pl.ds(pl.multiple_of(pid * r_out, r_out), r_out), :]

        if rows_per_peer is None:
            out_ref[me] = x_ref[...]
        else:
            out_ref[me] = x_ref[pl.ds(pl.multiple_of(me * r_out, r_out), r_out), :]
        sends = []
        for k in range(1, N_DEV):
            dev, pid = _flip(k)
            cp = pltpu.make_async_remote_copy(src_ref=src(pid), dst_ref=out_ref.at[me], send_sem=send_sems.at[k - 1],
                                              recv_sem=recv_sems.at[k - 1], device_id=dev, device_id_type=MESH)
            cp.start()
            sends.append(cp)
        for k in range(1, N_DEV):
            dev, pid = _flip(k)
            pltpu.make_async_remote_copy(src_ref=src(pid), dst_ref=out_ref.at[pid], send_sem=send_sems.at[k - 1],
                                         recv_sem=recv_sems.at[k - 1], device_id=dev, device_id_type=MESH).wait_recv()
        for cp in sends:
            cp.wait_send()

    return pl.pallas_call(
        _behind(body, after), name=name,
        out_shape=jax.ShapeDtypeStruct((N_DEV, r_out, cols), x.dtype),
        in_specs=_ANY * len(after) + [pl.BlockSpec(memory_space=pltpu.VMEM)],
        out_specs=pl.BlockSpec(memory_space=pltpu.VMEM),
        scratch_shapes=[pltpu.SemaphoreType.DMA((N_DEV - 1,)), pltpu.SemaphoreType.DMA((N_DEV - 1,))],
    )(*after, x)


def _gather_weights(shards):
    n = len(shards)

    def body(*refs):
        xs, outs = refs[:n], refs[n:2 * n]
        send_sems, recv_sems, local_sems = refs[2 * n:]
        x, y, c = _mesh_pos()
        me, sibling = (x, y, c), (x, y, 1 - c)
        chips = [(1 - x, y), (x, 1 - y), (1 - x, 1 - y)]

        def blk(a, px, py, pc):
            return outs[a].at[4 * px + 2 * py + pc]

        def copy(a, k, block, to, src=None):
            return pltpu.make_async_remote_copy(
                src_ref=blk(a, *block) if src is None else src, dst_ref=blk(a, *block),
                send_sem=send_sems.at[a * 7 + k], recv_sem=recv_sems.at[a * 7 + k], device_id=to, device_id_type=MESH)

        mine = [pltpu.make_async_copy(xs[a], blk(a, *me), local_sems.at[a]) for a in range(n)]
        for cp in mine:
            cp.start()
        first = []
        for a in range(n):
            first.append(copy(a, 0, me, sibling, src=xs[a]))
            first += [copy(a, 1 + j, me, (*chip, c), src=xs[a]) for j, chip in enumerate(chips)]
        for cp in first:
            cp.start()
        passed = []
        for j, chip in enumerate(chips):
            for a in range(n):
                copy(a, 1 + j, (*chip, c), me).wait_recv()
                cp = copy(a, 4 + j, (*chip, c), sibling)
                cp.start()
                passed.append(cp)
        for a in range(n):
            copy(a, 0, sibling, me).wait_recv()
            for j, chip in enumerate(chips):
                copy(a, 4 + j, (*chip, 1 - c), me).wait_recv()
        for cp in first + passed:
            cp.wait_send()
        for cp in mine:
            cp.wait()

    hbm = pl.BlockSpec(memory_space=pl.ANY)
    return pl.pallas_call(
        body, name="gather_weights",
        out_shape=[jax.ShapeDtypeStruct((N_DEV,) + s.shape, s.dtype) for s in shards],
        in_specs=[hbm] * n, out_specs=[hbm] * n,
        scratch_shapes=[pltpu.SemaphoreType.DMA((7 * n,)), pltpu.SemaphoreType.DMA((7 * n,)), pltpu.SemaphoreType.DMA((n,))],
    )(*shards)


_HBM = pl.BlockSpec(memory_space=pltpu.HBM)
_SEM = pl.BlockSpec(memory_space=pltpu.SEMAPHORE)
_DATAFLOW = pltpu.SideEffectType.DATAFLOW_SIDE_EFFECTING


def _copies_start(name, plan, n_copies, srcs, lands, after):
    bufs = list(srcs) + list(lands)
    nb = len(bufs)

    def body(*refs):
        ins, send_sems, recv_sems, token = refs[:nb], refs[nb + len(after)], refs[nb + len(after) + 1], refs[-1]
        for i, (src, dst, dev) in enumerate(plan(ins[:len(srcs)], ins[len(srcs):])):
            pltpu.make_async_remote_copy(src_ref=src, dst_ref=dst, send_sem=send_sems.at[i], recv_sem=recv_sems.at[i],
                                         device_id=dev, device_id_type=MESH).start()
        token[...] = jnp.zeros_like(token)

    outs = pl.pallas_call(
        body, name=name,
        out_shape=(pltpu.SemaphoreType.DMA((n_copies,)), pltpu.SemaphoreType.DMA((n_copies,)),
                   *[pltpu.HBM(b.shape, b.dtype) for b in bufs], jax.ShapeDtypeStruct((8, 128), F32)),
        in_specs=[_HBM] * nb + [pl.BlockSpec(memory_space=pl.ANY)] * len(after),
        out_specs=(_SEM, _SEM, *[_HBM] * nb, pl.BlockSpec(memory_space=pltpu.VMEM)),
        input_output_aliases={i: 2 + i for i in range(nb)},
        compiler_params=pltpu.CompilerParams(has_side_effects=_DATAFLOW),
    )(*[pltpu.with_memory_space_constraint(b, pltpu.HBM) for b in bufs], *after)
    return outs[0], outs[1], list(outs[2:2 + len(srcs)]), list(outs[2 + len(srcs):2 + nb]), outs[-1]


def _copies_wait(name, plan, send_sems, recv_sems, srcs, lands, after):
    bufs = list(srcs) + list(lands)
    nb = len(bufs)

    def body(*refs):
        ins, send_ref, recv_ref = refs[:nb], refs[nb], refs[nb + 1]
        for i, (src, dst, dev) in enumerate(plan(ins[:len(srcs)], ins[len(srcs):])):
            cp = pltpu.make_async_remote_copy(src_ref=src, dst_ref=dst, send_sem=send_ref.at[i], recv_sem=recv_ref.at[i],
                                              device_id=dev, device_id_type=MESH)
            cp.wait_send()
            cp.wait_recv()

    outs = pl.pallas_call(
        body, name=name, out_shape=[pltpu.HBM(b.shape, b.dtype) for b in bufs],
        in_specs=[_HBM] * nb + [_SEM, _SEM] + [pl.BlockSpec(memory_space=pl.ANY)] * len(after), out_specs=[_HBM] * nb,
        input_output_aliases={i: i for i in range(nb)},
        compiler_params=pltpu.CompilerParams(has_side_effects=_DATAFLOW),
    )(*bufs, send_sems, recv_sems, *after)
    return list(outs[:len(srcs)]), list(outs[len(srcs):])


def _plan_gather_own(srcs, lands):
    _, me = _flip(0)
    return [(srcs[a], lands[a].at[me], _flip(k)[0]) for a in range(len(srcs)) for k in (1, 4, 2, 6)]


def _plan_gather_pass(srcs, lands):
    sibling = _flip(1)[0]
    plan = []
    for land in lands:
        for k in (4, 2, 6):
            block = land.at[_flip(k)[1]]
            plan.append((block, block, sibling))
    return plan


def _plan_reduce_pairs(srcs, lands):
    x, y, c = _mesh_pos()
    return [(srcs[a].at[chip, 1 - c], lands[a].at[chip], (x, y, 1 - c)) for a in range(len(srcs)) for chip in range(4)]


def _plan_reduce_chips(srcs, lands):
    plan = []
    for a in range(len(srcs)):
        for j, k in enumerate((4, 2, 6)):
            dev = _flip(k)[0]
            plan.append((srcs[a].at[2 * dev[0] + dev[1]], lands[a].at[j], dev))
    return plan


def _plan_reduce_direct(srcs, lands):
    plan = []
    for a in range(len(srcs)):
        for k in range(1, N_DEV):
            dev = _flip(k)[0]
            plan.append((srcs[a].at[2 * dev[0] + dev[1], dev[2]], lands[a].at[k - 1], dev))
    return plan


def _shard_rows(r):
    return r // 2 if r % 32 == 0 else r


def _pair_sum(core, grads, got, name):
    _, _, r, c = grads.shape
    tr = _shard_rows(r)

    def body(core_ref, a_ref, b_ref, o_ref, ob_ref):
        s = a_ref[...] + b_ref[...]
        o_ref[...] = s
        ob_ref[...] = s.astype(BF16)

    spec = pl.BlockSpec((None, tr, c), lambda i, j, core_ref: (i, j, 0))
    return pl.pallas_call(
        body, name=name,
        grid_spec=pltpu.PrefetchScalarGridSpec(
            num_scalar_prefetch=1, grid=(4, r // tr),
            in_specs=[pl.BlockSpec((None, None, tr, c), lambda i, j, core_ref: (i, core_ref[0], j, 0)), spec],
            out_specs=[spec, spec]),
        out_shape=[jax.ShapeDtypeStruct((4, r, c), F32), jax.ShapeDtypeStruct((4, r, c), BF16)],
        compiler_params=_params("parallel", "parallel"),
    )(core, grads, got)


def _ada_rows(c_all, w_ada, b_ada):
    n_cols = w_ada.shape[1]

    def body(c_ref, w_ref, b_ref, o_ref):
        _, me = _flip(0)
        bias = b_ref[:, pl.ds(pl.multiple_of(me * n_cols, 128), n_cols)]
        o_ref[...] = _dot_f32(_silu(c_ref[...]), w_ref[...]) + bias

    return pl.pallas_call(
        body, name="ada_rows", out_shape=jax.ShapeDtypeStruct((N_DEV, n_cols), F32),
        in_specs=[pl.BlockSpec(memory_space=pltpu.VMEM)] * 3, out_specs=pl.BlockSpec(memory_space=pltpu.VMEM),
    )(c_all, w_ada, b_ada)


def _in_fwd(x, mod, g1, w_in, logits, onorm_g, after=()):
    s = x.shape[0]
    tm = HG_TILE

    def body(x_ref, mod_ref, g_ref, w_ref, lg_ref, og_ref, h_ref, *rest):
        groups, (out_ref, o_ref, st_ref, state, qf_s, kk_s, lf_s) = rest[:7], rest[7:]
        xhat, _ = _rms(x_ref[...])
        h = (xhat * g_ref[...]) * (1.0 + mod_ref[:, D_MODEL:2 * D_MODEL]) + mod_ref[:, 0:D_MODEL]
        hb = h.astype(BF16)
        h_ref[...] = hb
        for j, o_ref_j in enumerate(groups):
            o_ref_j[...] = _dot_nt(hb, w_ref[j * 512:(j + 1) * 512, :])
        _hg_fwd_tile(*groups[:4], lg_ref, og_ref, out_ref, o_ref, st_ref, state, qf_s, kk_s, lf_s)

    tile = _rows(tm, 512)
    return pl.pallas_call(
        _behind(body, after), name="in_fwd", grid=(s // tm,),
        out_shape=[jax.ShapeDtypeStruct((s, D_MODEL), BF16)] + [jax.ShapeDtypeStruct((s, 512), F32)] * 7
        + [jax.ShapeDtypeStruct((s, HG_WIDTH), BF16), jax.ShapeDtypeStruct((s, HG_WIDTH), F32),
           jax.ShapeDtypeStruct((s // HG_CHUNK * HG_DIM, HG_WIDTH), F32)],
        in_specs=_ANY * len(after)
        + [_rows(tm, D_MODEL), _whole((1, 6 * D_MODEL)), _whole((1, D_MODEL)), _whole((IN_WIDTH, D_MODEL)),
           _whole((2, HG_WIDTH)), _whole((1, HG_DIM))],
        out_specs=[_rows(tm, D_MODEL)] + [tile] * 7 + [tile, tile, _rows(HG_TILE_CHUNKS * HG_DIM, HG_WIDTH)],
        scratch_shapes=[pltpu.VMEM((HG_HEADS, HG_DIM, HG_DIM), F32)] + [pltpu.VMEM((HG_TILE, HG_WIDTH), F32)] * 3,
        compiler_params=_params("arbitrary"),
    )(*after, x, mod, g1, w_in, logits, onorm_g)


def _in_bwd(x, dx1, mod, g1, w_in, dps):
    s = x.shape[0]
    tm = 256

    def body(x_ref, dx_ref, mod_ref, g_ref, w_ref, *rest):
        dp_refs, (gx_ref, dpb_ref, dsh_ref, dsc_ref, dg_ref) = rest[:13], rest[13:]
        pieces = [dp_refs[j][...] for j in range(4)]
        pieces += [dp_refs[4 + 3 * j][...] + dp_refs[5 + 3 * j][...] + dp_refs[6 + 3 * j][...] for j in range(3)]
        for j, p in enumerate(pieces):
            dpb_ref[:, j * 512:(j + 1) * 512] = p.astype(BF16)
        dh = _dot(dpb_ref[...], w_ref[...])
        xhat, rstd = _rms(x_ref[...])
        g = g_ref[...]
        scale1 = 1.0 + mod_ref[:, D_MODEL:2 * D_MODEL]
        n1 = xhat * g

        @pl.when(pl.program_id(0) == 0)
        def _():
            dsh_ref[...] = jnp.zeros_like(dsh_ref)
            dsc_ref[...] = jnp.zeros_like(dsc_ref)
            dg_ref[...] = jnp.zeros_like(dg_ref)

        dsh_ref[...] += _rowsum(dh)
        dsc_ref[...] += _rowsum(dh * n1)
        dn = dh * scale1
        dg_ref[...] += _rowsum(dn * xhat)
        gx_ref[...] = dx_ref[...] + _rms_bwd(dn * g, xhat, rstd)

    vec = _whole((1, D_MODEL))
    return pl.pallas_call(
        body, name="in_bwd", grid=(s // tm,),
        out_shape=[jax.ShapeDtypeStruct((s, D_MODEL), F32), jax.ShapeDtypeStruct((s, IN_WIDTH), BF16)]
        + [jax.ShapeDtypeStruct((1, D_MODEL), F32)] * 3,
        in_specs=[_rows(tm, D_MODEL), _rows(tm, D_MODEL), _whole((1, 6 * D_MODEL)), vec, _whole((IN_WIDTH, D_MODEL))]
        + [_rows(tm, 512)] * 13,
        out_specs=[_rows(tm, D_MODEL), _rows(tm, IN_WIDTH), vec, vec, vec],
        compiler_params=_params("arbitrary"),
    )(x, dx1, mod, g1, w_in, *dps)


HG_TILE = 512
HG_TILE_CHUNKS = HG_TILE // HG_CHUNK


def _lower_bound(lg_ref):
    return 1.0 / (1.0 + jnp.exp(lg_ref[1:2, :] - lg_ref[0:1, :]))


def _chunk_masks():
    r = lax.broadcasted_iota(jnp.int32, (HG_CHUNK, HG_CHUNK), 0)
    c = lax.broadcasted_iota(jnp.int32, (HG_CHUNK, HG_CHUNK), 1)
    return r >= c, c >= r, (r >= c).astype(F32), (c >= r).astype(F32)


def _hg_fwd_tile(q_ref, f_ref, i_ref, g_ref, lg_ref, og_ref, out_ref, o_ref, st_ref, state, qf_s, kk_s, lf_s):
    @pl.when(pl.program_id(0) == 0)
    def _():
        state[...] = jnp.zeros_like(state)

    lb = _lower_bound(lg_ref)
    f = lb + (1.0 - lb) * _sigmoid(f_ref[...])
    kk_s[...] = 1.0 - f
    lf_s[...] = jnp.log(f)
    qf_s[...] = _silu(q_ref[...])
    causal, _, tri, _ = _chunk_masks()

    def chunk(ci, carry):
        rows = pl.ds(pl.multiple_of(ci * HG_CHUNK, HG_CHUNK), HG_CHUNK)
        srows = pl.ds(pl.multiple_of(ci * HG_DIM, HG_DIM), HG_DIM)
        lf = lf_s[rows, :]
        b = _dot_f32(tri, lf)
        bl = _rowsum(lf)
        ref = 0.5 * bl
        qf, kk, v = qf_s[rows, :], kk_s[rows, :], i_ref[rows, :]
        a_in = (qf * jnp.exp(b)).astype(BF16)
        a_t = (qf * jnp.exp(b - ref)).astype(BF16)
        b_t = (kk * jnp.exp(ref - b)).astype(BF16)
        kd = kk * jnp.exp(bl - b)
        ebl = jnp.exp(bl)
        vb = v.astype(BF16)
        for h in range(HG_HEADS):
            c = slice(h * HG_DIM, (h + 1) * HG_DIM)
            st = state[h]
            st_ref[srows, c] = st
            p = jnp.where(causal, _dot_nt(a_t[:, c], b_t[:, c]), 0.0)
            o_ref[rows, c] = _dot(p.astype(BF16), vb[:, c]) + _dot_nt(a_in[:, c], st.astype(BF16))
            state[h] = st * ebl[:, c] + _dot_tn(vb[:, c], kd[:, c].astype(BF16))
        return carry

    lax.fori_loop(0, HG_TILE_CHUNKS, chunk, 0, unroll=True)
    for h in range(HG_HEADS):
        c = slice(h * HG_DIM, (h + 1) * HG_DIM)
        ohat, _ = _rms(o_ref[:, c])
        out_ref[:, c] = (ohat * og_ref[...] * _silu(g_ref[:, c])).astype(BF16)


def _hg_bwd(hq, hf, hi, hgt, logits, onorm_g, o, states, dout):
    s = hq.shape[0]
    n_tiles = s // HG_TILE

    def body(q_ref, f_ref, i_ref, g_ref, lg_ref, og_ref, o_ref, st_ref, d_ref,
             dq_ref, df_ref, di_ref, dg_ref, dog_ref, dlb_ref, dstate, qf_s, kk_s, lf_s, do_s):
        @pl.when(pl.program_id(0) == 0)
        def _():
            dstate[...] = jnp.zeros_like(dstate)
            dog_ref[...] = jnp.zeros_like(dog_ref)
            dlb_ref[...] = jnp.zeros_like(dlb_ref)

        og = og_ref[...]
        dog = jnp.zeros((1, HG_DIM), F32)
        for h in range(HG_HEADS):
            c = slice(h * HG_DIM, (h + 1) * HG_DIM)
            ohat, rstd = _rms(o_ref[:, c])
            gate = g_ref[:, c]
            d = d_ref[:, c]
            dg_ref[:, c] = (d * (ohat * og) * _dsilu(gate)).astype(BF16)
            dnormed = d * _silu(gate)
            dog += _rowsum(dnormed * ohat)
            do_s[:, c] = _rms_bwd(dnormed * og, ohat, rstd)
        dog_ref[...] += dog

        lb = _lower_bound(lg_ref)
        f = lb + (1.0 - lb) * _sigmoid(f_ref[...])
        kk_s[...] = 1.0 - f
        lf_s[...] = jnp.log(f)
        qf_s[...] = _silu(q_ref[...])
        causal, upper, tri, tri_t = _chunk_masks()

        def chunk(step, carry):
            ci = HG_TILE_CHUNKS - 1 - step
            rows = pl.ds(pl.multiple_of(ci * HG_CHUNK, HG_CHUNK), HG_CHUNK)
            srows = pl.ds(pl.multiple_of(ci * HG_DIM, HG_DIM), HG_DIM)
            lf = lf_s[rows, :]
            b = _dot_f32(tri, lf)
            bl = _rowsum(lf)
            ref = 0.5 * bl
            qf, kk, v, do = qf_s[rows, :], kk_s[rows, :], i_ref[rows, :], do_s[rows, :]
            eb, ebr, erb, ekd, ebl = jnp.exp(b), jnp.exp(b - ref), jnp.exp(ref - b), jnp.exp(bl - b), jnp.exp(bl)
            a_in, a_t, b_t, kd = qf * eb, qf * ebr, kk * erb, kk * ekd
            for h in range(HG_HEADS):
                c = slice(h * HG_DIM, (h + 1) * HG_DIM)
                st, dst = st_ref[srows, c], dstate[h]
                stb, dstb = st.astype(BF16), dst.astype(BF16)
                doh, vh = do[:, c], v[:, c]
                dob, vb = doh.astype(BF16), vh.astype(BF16)
                ain_h, at_h, bt_h, kd_h = a_in[:, c], a_t[:, c], b_t[:, c], kd[:, c]
                atb, btb = at_h.astype(BF16), bt_h.astype(BF16)
                d_ain = _dot(dob, stb)
                p_t = jnp.where(upper, _dot_nt(btb, atb), 0.0).astype(BF16)
                dp = jnp.where(causal, _dot_nt(dob, vb), 0.0).astype(BF16)
                dp_t = jnp.where(upper, _dot_nt(vb, dob), 0.0).astype(BF16)
                di_ref[rows, c] = (_dot(p_t, dob) + _dot_nt(kd_h.astype(BF16), dstb)).astype(BF16)
                d_at = _dot(dp, btb)
                d_bt = _dot(dp_t, atb)
                d_kd = _dot(vb, dstb)
                dqf = d_ain * eb[:, c] + d_at * ebr[:, c]
                dkk = d_bt * erb[:, c] + d_kd * ekd[:, c]
                db = d_ain * ain_h + d_at * atb.astype(F32) - d_bt * btb.astype(F32) - d_kd * kd_h
                dbl = _rowsum(d_kd * kd_h) + _rowsum(dst * st) * ebl[:, c]
                dstate[h] = _dot_tn(dob, ain_h.astype(BF16)) + dst * ebl[:, c]
                dlf = _dot_f32(tri_t, db) + dbl
                qv, fr = q_ref[rows, c], f_ref[rows, c]
                lbh = lb[:, c]
                sg = _sigmoid(fr)
                dfv = dlf / (lbh + (1.0 - lbh) * sg) - dkk
                df_ref[rows, c] = (dfv * (1.0 - lbh) * sg * (1.0 - sg)).astype(BF16)
                dlb_ref[:, c] += _rowsum(dfv * (1.0 - sg))
                dq_ref[rows, c] = (dqf * _dsilu(qv)).astype(BF16)
            return carry

        lax.fori_loop(0, HG_TILE_CHUNKS, chunk, 0, unroll=True)

    rev = pl.BlockSpec((HG_TILE, HG_WIDTH), lambda i: (n_tiles - 1 - i, 0))
    return pl.pallas_call(
        body, name="hg_bwd", grid=(n_tiles,),
        out_shape=[jax.ShapeDtypeStruct((s, HG_WIDTH), BF16)] * 4
        + [jax.ShapeDtypeStruct((1, HG_DIM), F32), jax.ShapeDtypeStruct((1, HG_WIDTH), F32)],
        in_specs=[rev] * 4 + [_whole((2, HG_WIDTH)), _whole((1, HG_DIM)), rev,
                              pl.BlockSpec((HG_TILE_CHUNKS * HG_DIM, HG_WIDTH), lambda i: (n_tiles - 1 - i, 0)), rev],
        out_specs=[rev] * 4 + [_whole((1, HG_DIM)), _whole((1, HG_WIDTH))],
        scratch_shapes=[pltpu.VMEM((HG_HEADS, HG_DIM, HG_DIM), F32)] + [pltpu.VMEM((HG_TILE, HG_WIDTH), F32)] * 4,
        compiler_params=_params("arbitrary"),
    )(hq, hf, hi, hgt, logits, onorm_g, o, states, dout)


TOKEN_GROUP = 16


def _att_geometry(dil, seq=0):
    per_group = TOKEN_GROUP // dil
    ub = ATT_BLOCK // per_group
    if dil == TOKEN_GROUP:
        n_blocks = 2 if seq % (2 * ub * TOKEN_GROUP) == 0 and seq > 0 else 1
    else:
        n_blocks = 4
    return per_group, ub, ATT_WIDTH if dil == 1 else 128, n_blocks


def _att_consts(dil):
    per_group, ub = _att_geometry(dil)[:2]

    def pos(i):
        return i if dil == 1 else (i % ub) * per_group + i // ub

    lane = lax.broadcasted_iota(jnp.int32, (ATT_BLOCK, 128), 1)
    qi = pos(lax.broadcasted_iota(jnp.int32, (2 * ATT_BLOCK, ATT_BLOCK), 0) % ATT_BLOCK)
    kj = pos(lax.broadcasted_iota(jnp.int32, (2 * ATT_BLOCK, ATT_BLOCK), 1))
    return lane < ATT_HEAD_DIM, kj <= qi, lambda off: kj >= qi + off


def _load_tile(ref, dil, r, c, base=0):
    per_group, ub = _att_geometry(dil)[:2]
    if dil == 1:
        return ref[base:base + ATT_BLOCK, c]
    return jnp.concatenate([ref[pl.ds(base + dil * w + r, ub, stride=TOKEN_GROUP), c] for w in range(per_group)], axis=0)


def _store_tile(ref, dil, r, c, val, base=0):
    per_group, ub = _att_geometry(dil)[:2]
    if dil == 1:
        ref[base:base + ATT_BLOCK, c] = val
        return
    for w in range(per_group):
        ref[pl.ds(base + dil * w + r, ub, stride=TOKEN_GROUP), c] = val[w * ub:(w + 1) * ub]


def _stack_heads(x2, first):
    return jnp.concatenate([jnp.where(first, x2, 0.0), jnp.where(first, 0.0, x2)], axis=0)


def _stack_bcast(x2, first):
    other = pltpu.roll(x2, ATT_HEAD_DIM, axis=1)
    return jnp.concatenate([jnp.where(first, x2, other), jnp.where(first, other, x2)], axis=0)


def _unstack_heads(st, first):
    return jnp.where(first, st[:ATT_BLOCK], st[ATT_BLOCK:])


def _att_fwd(q, k, v, dil):
    seq, width = q.shape
    _, ub, lanes, nbs = _att_geometry(dil, seq)
    rows = ub * TOKEN_GROUP
    n_steps = seq // (nbs * rows)

    def body(q_ref, k_ref, v_ref, kp_ref, vp_ref, o_ref, lse_ref):
        first, cur_ok, _band = _att_consts(dil)
        inner_ok = _band(0)
        edge_ok = _band(jnp.where(pl.program_id(0) > 0, 0, ATT_BLOCK))
        for r in range(dil):
            for j in range(lanes // 128):
                c = slice(j * 128, (j + 1) * 128)
                kc = vc = None
                for b in range(nbs):
                    base = b * rows
                    prev_ok = edge_ok if b == 0 else inner_ok
                    if b == 0:
                        kp, vp = _load_tile(kp_ref, dil, r, c).astype(BF16), _load_tile(vp_ref, dil, r, c).astype(BF16)
                    else:
                        kp, vp = kc, vc
                    qst = _stack_heads(_load_tile(q_ref, dil, r, c, base) * ATT_SCALE, first).astype(BF16)
                    kc = _load_tile(k_ref, dil, r, c, base).astype(BF16)
                    vc = _load_tile(v_ref, dil, r, c, base).astype(BF16)
                    sc = jnp.where(cur_ok, _dot_nt(qst, kc), NEG)
                    sp = jnp.where(prev_ok, _dot_nt(qst, kp), NEG)
                    mx = jnp.max(jnp.maximum(sc, sp), axis=-1, keepdims=True)
                    pc, pp = jnp.exp(sc - mx), jnp.exp(sp - mx)
                    den = jnp.sum(pc + pp, axis=-1, keepdims=True)
                    ost = (_dot(pc.astype(BF16), vc) + _dot(pp.astype(BF16), vp)) / den
                    lse = jnp.broadcast_to(mx + jnp.log(den), (2 * ATT_BLOCK, 128))
                    _store_tile(o_ref, dil, r, c, _unstack_heads(ost, first), base)
                    _store_tile(lse_ref, dil, r, c, _unstack_heads(lse, first), base)

    slab = pl.BlockSpec((nbs * rows, lanes), lambda n, j: (n, j))
    before = pl.BlockSpec((rows, lanes), lambda n, j: (jnp.maximum(n * nbs - 1, 0), j))
    return pl.pallas_call(
        body, name=f"att_fwd_d{dil}", grid=(n_steps, width // lanes),
        out_shape=[jax.ShapeDtypeStruct((seq, width), F32)] * 2,
        in_specs=[slab, slab, slab, before, before], out_specs=[slab, slab],
        compiler_params=_params("arbitrary", "arbitrary"),
    )(q, k, v, k, v)


def _att_bwd(q, k, v, do, cc, lse, dil, behind=()):
    seq, width = q.shape
    _, ub, lanes, nbs = _att_geometry(dil, seq)
    rows = ub * TOKEN_GROUP
    n_blocks = seq // rows
    n_steps = n_blocks // nbs

    def body(q_ref, k_ref, v_ref, do_ref, cc_ref, lse_ref, qx_ref, dox_ref, ccx_ref, lsex_ref,
             dq_ref, dk_ref, dv_ref, carry):
        first, cur_ok, _band = _att_consts(dil)
        step = pl.program_id(1)
        inner_ok = _band(0)
        edge_ok = _band(jnp.where(step < n_steps - 1, 0, ATT_BLOCK))

        @pl.when(step == 0)
        def _():
            carry[...] = jnp.zeros_like(carry)

        def queries(refs, r, c, base):
            q_r, do_r, lse_r, cc_r = refs
            return (_stack_heads(_load_tile(q_r, dil, r, c, base) * ATT_SCALE, first).astype(BF16),
                    _stack_heads(_load_tile(do_r, dil, r, c, base), first).astype(BF16),
                    _stack_bcast(_load_tile(lse_r, dil, r, c, base), first),
                    _stack_bcast(_load_tile(cc_r, dil, r, c, base), first))

        for r in range(dil):
            for j in range(lanes // 128):
                c = slice(j * 128, (j + 1) * 128)
                own = queries((q_ref, do_ref, lse_ref, cc_ref), r, c, 0)
                left = _load_tile(carry, dil, r, c)
                for b in range(nbs):
                    base = b * rows
                    last = b == nbs - 1
                    next_ok = edge_ok if last else inner_ok
                    if last:
                        following = queries((qx_ref, dox_ref, lsex_ref, ccx_ref), r, c, 0)
                    else:
                        following = queries((q_ref, do_ref, lse_ref, cc_ref), r, c, base + rows)
                    (qst, dost, lse_n, cc_n), (qxst, doxst, lse_x, cc_x) = own, following
                    kb = _load_tile(k_ref, dil, r, c, base).astype(BF16)
                    vb = _load_tile(v_ref, dil, r, c, base).astype(BF16)
                    p_cur = jnp.exp(jnp.where(cur_ok, _dot_nt(qst, kb), NEG) - lse_n)
                    p_next = jnp.exp(jnp.where(next_ok, _dot_nt(qxst, kb), NEG) - lse_x)
                    ds_cur = (p_cur * (_dot_nt(dost, vb) + cc_n)).astype(BF16)
                    ds_next = (p_next * (_dot_nt(doxst, vb) + cc_x)).astype(BF16)
                    dq_own = left + _unstack_heads(_dot(ds_cur, kb), first)
                    _store_tile(dq_ref, dil, r, c, dq_own * ATT_SCALE, base)
                    _store_tile(dk_ref, dil, r, c, _dot_tn(ds_cur, qst) + _dot_tn(ds_next, qxst), base)
                    _store_tile(dv_ref, dil, r, c, _dot_tn(p_cur.astype(BF16), dost) + _dot_tn(p_next.astype(BF16), doxst), base)
                    left = _unstack_heads(_dot(ds_next, kb), first)
                    own = following
                _store_tile(carry, dil, r, c, left)

    slab = pl.BlockSpec((nbs * rows, lanes), lambda j, n: (n, j))
    after = pl.BlockSpec((rows, lanes), lambda j, n: (jnp.minimum((n + 1) * nbs, n_blocks - 1), j))
    return pl.pallas_call(
        _behind(body, behind), name=f"att_bwd_d{dil}", grid=(width // lanes, n_steps),
        out_shape=[jax.ShapeDtypeStruct((seq, width), F32)] * 3,
        in_specs=_ANY * len(behind) + [slab] * 6 + [after] * 4, out_specs=[slab] * 3,
        scratch_shapes=[pltpu.VMEM((rows, lanes), F32)],
        compiler_params=_params("arbitrary", "arbitrary"),
    )(*behind, q, k, v, do, cc, lse, q, do, cc, lse)


def _branch_weights(lses):
    mx = jnp.maximum(jnp.maximum(lses[0], lses[1]), lses[2])
    es = [jnp.exp(l - mx) for l in lses]
    inv = 1.0 / (es[0] + es[1] + es[2])
    return [e * inv for e in es]


def _att_combine(outs, lses, att_g, after=()):
    s = outs[0].shape[0]
    tm = 512

    def body(o0, o1, o2, l0, l1, l2, g_ref, att_ref, out_ref):
        ws = _branch_weights([l0[...], l1[...], l2[...]])
        att = ws[0] * o0[...] + ws[1] * o1[...] + ws[2] * o2[...]
        att_ref[...] = att
        ahat, _ = _rms(att)
        out_ref[...] = (ahat * g_ref[...]).astype(BF16)

    tile = _rows(tm, ATT_WIDTH)
    return pl.pallas_call(
        _behind(body, after), name="att_combine", grid=(s // tm,),
        out_shape=[jax.ShapeDtypeStruct((s, ATT_WIDTH), F32), jax.ShapeDtypeStruct((s, ATT_WIDTH), BF16)],
        in_specs=_ANY * len(after) + [tile] * 6 + [_whole((1, ATT_WIDTH))], out_specs=[tile, tile],
        compiler_params=_params("parallel"),
    )(*after, *outs, *lses, att_g)


def _att_combine_bwd_tile(d, att, lses, g):
    ahat, rstd = _rms(att)
    datt = _rms_bwd(d * g, ahat, rstd)
    hi = lax.broadcasted_iota(jnp.int32, (ATT_WIDTH, ATT_WIDTH), 0) // ATT_HEAD_DIM
    hj = lax.broadcasted_iota(jnp.int32, (ATT_WIDTH, ATT_WIDTH), 1) // ATT_HEAD_DIM
    same_head = (hi == hj).astype(BF16)
    prod = datt * att
    prod_hi = prod.astype(BF16)
    prod_lo = (prod - prod_hi.astype(F32)).astype(BF16)
    head_sum = _dot(prod_hi, same_head) + _dot(prod_lo, same_head)
    ws = _branch_weights(lses)
    return [w * datt for w in ws], [-w * head_sum for w in ws], _rowsum(d * ahat)


def _out_bwd(dx1, hg, at, mod, w_out, att, lses, att_g):
    s = dx1.shape[0]
    tm = 512
    n_steps = s // tm

    def body(dx_ref, hg_ref, at_ref, mod_ref, w_ref, att_ref, l0, l1, l2, g_ref,
             dhg_ref, do0, do1, do2, cc0, cc1, cc2, dw_ref, dwb_ref, dgate_ref, dg_ref):
        @pl.when(pl.program_id(0) == 0)
        def _():
            dw_ref[...] = jnp.zeros_like(dw_ref)
            dgate_ref[...] = jnp.zeros_like(dgate_ref)
            dg_ref[...] = jnp.zeros_like(dg_ref)

        hg, at, dx = hg_ref[...], at_ref[...], dx_ref[...]
        mix = _dot(hg, w_ref[0:512, :]) + _dot(at, w_ref[512:1024, :])
        dgate_ref[...] += _rowsum(dx * mix)
        dmix = (mod_ref[:, 2 * D_MODEL:3 * D_MODEL] * dx).astype(BF16)
        dhg_ref[...] = _dot_nt(dmix, w_ref[0:512, :])
        dos, ccs, dg_rows = _att_combine_bwd_tile(_dot_nt(dmix, w_ref[512:1024, :]), att_ref[...],
                                                  [l0[...], l1[...], l2[...]], g_ref[...])
        for val, ref in zip(dos + ccs, (do0, do1, do2, cc0, cc1, cc2)):
            ref[...] = val
        dg_ref[...] += dg_rows
        dw_ref[0:512, :] += _dot_tn(hg, dmix)
        dw_ref[512:1024, :] += _dot_tn(at, dmix)

        @pl.when(pl.program_id(0) == n_steps - 1)
        def _():
            dwb_ref[...] = dw_ref[...].astype(BF16)

    tile = _rows(tm, 512)
    square = _whole((D_MODEL, D_MODEL))
    return pl.pallas_call(
        body, name="out_bwd", grid=(n_steps,),
        out_shape=[jax.ShapeDtypeStruct((s, 512), F32)] * 7
        + [jax.ShapeDtypeStruct((D_MODEL, D_MODEL), F32), jax.ShapeDtypeStruct((D_MODEL, D_MODEL), BF16),
           jax.ShapeDtypeStruct((1, D_MODEL), F32), jax.ShapeDtypeStruct((1, ATT_WIDTH), F32)],
        in_specs=[_rows(tm, D_MODEL), tile, tile, _whole((1, 6 * D_MODEL)), square] + [tile] * 4 + [_whole((1, ATT_WIDTH))],
        out_specs=[tile] * 7 + [square, square, _whole((1, D_MODEL)), _whole((1, ATT_WIDTH))],
        compiler_params=_params("arbitrary"),
    )(dx1, hg, at, mod, w_out, att, *lses, att_g)


def _ffn(x, hg, at, target, mod, g2, gf, w_out, w_gu, w_down):
    s = x.shape[0]
    tm = 256

    def body(x_ref, hg_ref, at_ref, t_ref, mod_ref, g2_ref, gf_ref, wo_ref, wgu_hbm, wd_hbm,
             dx_ref, h2_ref, act_ref, dau_ref, dff_ref, sums_ref, loss_ref, wgu, wd, au_s, sem):
        @pl.when(pl.program_id(0) == 0)
        def _():
            c1 = pltpu.make_async_copy(wgu_hbm, wgu, sem.at[0])
            c2 = pltpu.make_async_copy(wd_hbm, wd, sem.at[1])
            c1.start()
            c2.start()
            c1.wait()
            c2.wait()
            sums_ref[...] = jnp.zeros_like(sums_ref)
            loss_ref[...] = jnp.zeros_like(loss_ref)

        mix = _dot(hg_ref[...], wo_ref[0:512, :]) + _dot(at_ref[...], wo_ref[512:1024, :])
        x1v = x_ref[...] + mod_ref[:, 2 * D_MODEL:3 * D_MODEL] * mix
        xhat, rstd = _rms(x1v)
        g2 = g2_ref[...]
        n2 = xhat * g2
        scale2 = 1.0 + mod_ref[:, 4 * D_MODEL:5 * D_MODEL]
        gate2 = mod_ref[:, 5 * D_MODEL:6 * D_MODEL]
        hb = (n2 * scale2 + mod_ref[:, 3 * D_MODEL:4 * D_MODEL]).astype(BF16)
        h2_ref[...] = hb
        au_s[...] = _dot_nt(hb, wgu[...])
        a = au_s[:, 0:D_FF]
        act = (_silu(a) * au_s[:, D_FF:2 * D_FF]).astype(BF16)
        act_ref[...] = act
        ff = _dot(act, wd[...])
        x2 = x1v + gate2 * ff
        nf, rstd_f = _rms(x2)
        gfv = gf_ref[...]
        err = nf * gfv - t_ref[...]
        loss_ref[...] += 0.5 * jnp.sum(_rowsum(err * err), axis=-1, keepdims=True) * (1.0 / D_MODEL)
        dy = err * (1.0 / D_MODEL)
        dx2 = _rms_bwd(dy * gfv, nf, rstd_f)
        dffb = (gate2 * dx2).astype(BF16)
        dff_ref[...] = dffb
        dact = _dot_nt(dffb, wd[...])
        a = au_s[:, 0:D_FF]
        dau_ref[:, 0:D_FF] = (dact * au_s[:, D_FF:2 * D_FF] * _dsilu(a)).astype(BF16)
        dau_ref[:, D_FF:2 * D_FF] = (dact * _silu(a)).astype(BF16)
        dh = _dot(dau_ref[...], wgu[...])
        dn = dh * scale2
        sums_ref[0:1, :] += _rowsum(dh)
        sums_ref[1:2, :] += _rowsum(dh * n2)
        sums_ref[2:3, :] += _rowsum(dx2 * ff)
        sums_ref[3:4, :] += _rowsum(dn * xhat)
        sums_ref[4:5, :] += _rowsum(dy * nf)
        dx_ref[...] = dx2 + _rms_bwd(dn * g2, xhat, rstd)

    vec = _whole((1, D_MODEL))
    hbm = pl.BlockSpec(memory_space=pl.ANY)
    return pl.pallas_call(
        body, name="ffn", grid=(s // tm,),
        out_shape=[jax.ShapeDtypeStruct((s, D_MODEL), F32), jax.ShapeDtypeStruct((s, D_MODEL), BF16),
                   jax.ShapeDtypeStruct((s, D_FF), BF16), jax.ShapeDtypeStruct((s, 2 * D_FF), BF16),
                   jax.ShapeDtypeStruct((s, D_MODEL), BF16), jax.ShapeDtypeStruct((8, D_MODEL), F32),
                   jax.ShapeDtypeStruct((1, 128), F32)],
        in_specs=[_rows(tm, D_MODEL), _rows(tm, 512), _rows(tm, 512), _rows(tm, D_MODEL), _whole((1, 6 * D_MODEL)), vec, vec,
                  _whole((D_MODEL, D_MODEL)), hbm, hbm],
        out_specs=[_rows(tm, D_MODEL), _rows(tm, D_MODEL), _rows(tm, D_FF), _rows(tm, 2 * D_FF), _rows(tm, D_MODEL),
                   _whole((8, D_MODEL)), _whole((1, 128))],
        scratch_shapes=[pltpu.VMEM((2 * D_FF, D_MODEL), BF16), pltpu.VMEM((D_FF, D_MODEL), BF16),
                        pltpu.VMEM((tm, 2 * D_FF), F32), pltpu.SemaphoreType.DMA((2,))],
        compiler_params=_params("arbitrary"),
    )(x, hg, at, target, mod, g2, gf, w_out, w_gu, w_down)


def _weight_grad(a, b, name, rounded=False):
    s, m = a.shape
    n = b.shape[1]
    ts = min(s, 2048)
    n_steps = s // ts
    tm = max(t for t in range(128, m + 1, 128) if m % t == 0 and t * n * 4 <= 6 * 1024 * 1024)

    def body(a_ref, b_ref, o_ref, *ob_ref):
        @pl.when(pl.program_id(1) == 0)
        def _():
            o_ref[...] = jnp.zeros_like(o_ref)

        o_ref[...] += _dot_tn(a_ref[...], b_ref[...])
        if rounded:
            @pl.when(pl.program_id(1) == n_steps - 1)
            def _():
                ob_ref[0][...] = o_ref[...].astype(BF16)

    tile = pl.BlockSpec((tm, n), lambda j, i: (j, 0))
    return pl.pallas_call(
        body, name=name, grid=(m // tm, n_steps),
        out_shape=[jax.ShapeDtypeStruct((m, n), F32)] + [jax.ShapeDtypeStruct((m, n), BF16)] * rounded,
        in_specs=[pl.BlockSpec((ts, tm), lambda j, i: (i, j)), pl.BlockSpec((ts, n), lambda j, i: (i, 0))],
        out_specs=[tile] + [tile] * rounded,
        compiler_params=_params("parallel", "arbitrary"),
    )(a, b)


def _adamw_math(w, g, m, v):
    m = ADAM_B1 * m + (1.0 - ADAM_B1) * g
    v = ADAM_B2 * v + (1.0 - ADAM_B2) * (g * g)
    m_hat = m / (1.0 - ADAM_B1 ** ADAM_STEP)
    v_hat = v / (1.0 - ADAM_B2 ** ADAM_STEP)
    delta = -ADAM_LR * (m_hat / (jnp.sqrt(v_hat) + ADAM_EPS) + ADAM_WD * w)
    return delta, m, v


def _adamw_shard(where, w, m, v, partial, got, name):
    r, c = w.shape
    tr = _shard_rows(r)
    lead = partial.ndim - 2
    n_got = got.shape[0]

    def body(where_ref, w_ref, m_ref, v_ref, own_ref, *rest):
        got_refs, (grad_ref, d_ref, nm_ref, nv_ref) = rest[:n_got], rest[n_got:]
        g = own_ref[...]
        for g_ref in got_refs:
            g = g + g_ref[...].astype(F32)
        grad_ref[...] = g
        d_ref[...], nm_ref[...], nv_ref[...] = _adamw_math(w_ref[...], g, m_ref[...], v_ref[...])

    tile = pl.BlockSpec((tr, c), lambda i, where_ref: (i, 0))
    own = pl.BlockSpec((None,) * lead + (tr, c), lambda i, where_ref: (*[where_ref[d] for d in range(lead)], i, 0))
    part = [pl.BlockSpec((None, tr, c), functools.partial(lambda j, i, where_ref: (j, i, 0), j)) for j in range(n_got)]
    return pl.pallas_call(
        body, name=name,
        grid_spec=pltpu.PrefetchScalarGridSpec(num_scalar_prefetch=1, grid=(r // tr,), in_specs=[tile] * 3 + [own] + part,
                                               out_specs=[tile] * 4),
        out_shape=[jax.ShapeDtypeStruct((r, c), F32)] * 4, compiler_params=_params("parallel"),
    )(where, w, m, v, partial, *[got] * n_got)


def _small_update(small_all, dmod_blocks, c_all, logits, w_ada, m_ada, v_ada, smalls, after=()):
    def body(sm_ref, dm_ref, c_ref, lg_ref, wa_ref, ma_ref, va_ref, *rest):
        ins, outs = rest[:21], rest[21:]
        _, me = _flip(0)
        tot = sm_ref[0:1, :]
        for i in range(1, N_DEV):
            tot = tot + sm_ref[i:i + 1, :]
        loss_ref = outs[0]
        loss_ref[...] = tot[:, SM_LOSS:SM_LOSS + 128]
        g_ada = lax.dot_general(_silu(c_ref[...]), dm_ref[me], (((0,), (0,)), ((), ())),
                                preferred_element_type=F32, precision=HIGHEST)
        outs[1][...] = g_ada
        outs[2][...], outs[3][...], outs[4][...] = _adamw_math(wa_ref[...], g_ada, ma_ref[...], va_ref[...])
        p0 = _lower_bound(lg_ref)
        dl0 = tot[:, SM_LB:SM_LB + 512] * p0 * (1.0 - p0)
        grads = [tot[:, SM_MOD:SM_MOD + 6 * D_MODEL], tot[:, SM_G1:SM_G1 + D_MODEL], tot[:, SM_G2:SM_G2 + D_MODEL],
                 tot[:, SM_GF:SM_GF + D_MODEL], tot[:, SM_ATT:SM_ATT + 512], tot[:, SM_HG:SM_HG + 128],
                 jnp.where(lax.broadcasted_iota(jnp.int32, (2, 512), 0) == 0, dl0, -dl0)]
        for i, g in enumerate(grads):
            w_ref, m_ref, v_ref = ins[3 * i:3 * i + 3]
            o = outs[5 + 4 * i:9 + 4 * i]
            o[0][...] = g
            o[1][...], o[2][...], o[3][...] = _adamw_math(w_ref[...], g, m_ref[...], v_ref[...])

    flat = [t for trio in smalls for t in trio]
    vm = pl.BlockSpec(memory_space=pltpu.VMEM)
    out_shape = [jax.ShapeDtypeStruct((1, 128), F32)] + [jax.ShapeDtypeStruct(w_ada.shape, F32)] * 4
    for trio in smalls:
        out_shape += [jax.ShapeDtypeStruct(trio[0].shape, F32)] * 4
    return pl.pallas_call(
        _behind(body, after), name="small_update", out_shape=out_shape,
        in_specs=_ANY * len(after) + [vm] * (7 + len(flat)), out_specs=[vm] * len(out_shape),
        compiler_params=pltpu.CompilerParams(vmem_limit_bytes=V7X_VMEM_LIMIT),
    )(*after, small_all, dmod_blocks, c_all, logits, w_ada, m_ada, v_ada, *flat)


def kernel(x, c, w_ada, b_ada, norm1_g, w_in, hg_lb_logits, hg_onorm_g, att_onorm_g, w_out, norm2_g, w_gate_up, w_down, final_g, loss_target, m_w_ada, m_b_ada, m_norm1_g, m_w_in, m_hg_lb_logits, m_hg_onorm_g, m_att_onorm_g, m_w_out, m_norm2_g, m_w_gate_up, m_w_down, m_final_g, v_w_ada, v_b_ada, v_norm1_g, v_w_in, v_hg_lb_logits, v_hg_onorm_g, v_att_onorm_g, v_w_out, v_norm2_g, v_w_gate_up, v_w_down, v_final_g):
    x2d, target = x[0], loss_target[0]
    seq = x2d.shape[0]
    assert seq % (ATT_BLOCK * max(DILATIONS)) == 0 and seq % HG_TILE == 0
    gf = final_g.reshape(1, D_MODEL)

    c_all = _exchange_small(c.reshape(8, D_MODEL // 8), None, "gather_c").reshape(N_DEV, D_MODEL)
    ada = _ada_rows(c_all, w_ada[0], b_ada)
    mod = _exchange_small(ada, 1, "scatter_mod").reshape(1, 6 * D_MODEL)

    core = lax.axis_index("c").astype(jnp.int32).reshape(1)
    chip = (2 * lax.axis_index("x") + lax.axis_index("y")).astype(jnp.int32).reshape(1)
    me = 4 * lax.axis_index("x") + 2 * lax.axis_index("y") + lax.axis_index("c")

    g_in, = _gather_weights([w_in[0].T.astype(BF16)])
    w_in_b = g_in.reshape(IN_WIDTH, D_MODEL)
    rest_shards = [w_out[0].astype(BF16), w_gate_up[0].T.astype(BF16), w_down[0].astype(BF16)]
    lands = [lax.empty((N_DEV,) + s.shape, BF16) for s in rest_shards]
    g_send, g_recv, g_srcs, g_lands, tok = _copies_start("gather_rest_start", _plan_gather_own, 12, rest_shards, lands, [w_in_b, mod])
    flight = {}

    def stage(name, *vals):
        if name == "attention_begun":
            flight["shards"], got = _copies_wait("gather_rest_wait", _plan_gather_own, g_send, g_recv, g_srcs, g_lands, list(vals))
            flight["pass"] = _copies_start("gather_pass_start", _plan_gather_pass, 9, [], got, [])
            return [flight["pass"][4]]
        if name == "mixer_weights_done":
            shapes = [(4, 2, D_MODEL // N_DEV, D_MODEL), (4, 2, 2 * D_FF // N_DEV, D_MODEL), (4, 2, D_FF // N_DEV, D_MODEL)]
            flight["grads"] = [g32.reshape(sh) for (g32, _), sh in zip(vals, shapes)]
            rounded = [g16.reshape(sh) for (_, g16), sh in zip(vals, shapes)]
            direct_lands = [lax.empty((N_DEV - 1,) + sh[2:], BF16) for sh in shapes]
            flight["direct"] = _copies_start("reduce_rest_start", _plan_reduce_direct, 21, rounded, direct_lands, [])
            return [flight["direct"][4]]
        raise ValueError(name)

    def rest_weights(after):
        s, r, _, p_lands, _ = flight["pass"]
        _, got = _copies_wait("gather_pass_wait", _plan_gather_pass, s, r, [], p_lands, [after])
        full = [lax.dynamic_update_index_in_dim(g, shard, me, 0) for g, shard in zip(got, flight["shards"])]
        return full[0].reshape(D_MODEL, D_MODEL), full[1].reshape(2 * D_FF, D_MODEL), full[2].reshape(D_FF, D_MODEL)

    grad_x, dw_in, small = _block_step(x2d, target, mod, norm1_g, hg_lb_logits, hg_onorm_g, att_onorm_g, norm2_g, gf,
                                       w_in_b, rest_weights, stage, [tok])

    g_in8 = dw_in.reshape(4, 2, IN_WIDTH // N_DEV, D_MODEL)
    in_pairs = _copies_start("reduce_pairs_in_start", _plan_reduce_pairs, 4, [g_in8], [lax.empty((4,) + g_in8.shape[2:], F32)], [])
    s, r, srcs, d_lands, _ = flight["direct"]
    _, recv_rest = _copies_wait("reduce_rest_wait", _plan_reduce_direct, s, r, srcs, d_lands, [in_pairs[4]])
    small_rows = jnp.pad(small, ((0, 0), (0, SM_PADDED - SM_WIDTH))).reshape(SM_PADDED // 128, 128)
    small_all = _exchange_small(small_rows, None, "gather_small", [in_pairs[4]]).reshape(N_DEV, SM_PADDED)[:, :SM_WIDTH]
    in_grads, got_in = _copies_wait("reduce_pairs_in_wait", _plan_reduce_pairs, in_pairs[0], in_pairs[1], in_pairs[2], in_pairs[3],
                                    [small_all])
    in_s32, in_s16 = _pair_sum(core, in_grads[0], got_in[0], "pair_sum_in")
    in_chips = _copies_start("reduce_chips_in_start", _plan_reduce_chips, 3, [in_s16], [lax.empty((3,) + in_s16.shape[1:], BF16)], [])
    big, updated = {}, []
    rest_params = [("w_out", w_out, m_w_out, v_w_out), ("w_gate_up", w_gate_up, m_w_gate_up, v_w_gate_up), ("w_down", w_down, m_w_down, v_w_down)]
    mine = jnp.concatenate([chip, core])
    for (n, w, m, v), g32, got in zip(rest_params, flight["grads"], recv_rest):
        if n == "w_gate_up":
            outs4 = _adamw_shard(mine, w[0].T, m[0].T, v[0].T, g32, got, f"adamw_{n}")
            big[n] = [t.T[None] for t in outs4]
        else:
            outs4 = _adamw_shard(mine, w[0], m[0], v[0], g32, got, f"adamw_{n}")
            big[n] = [t[None] for t in outs4]
        updated.append(outs4[3])
    smalls = [(b_ada, m_b_ada, v_b_ada), (norm1_g, m_norm1_g, v_norm1_g), (norm2_g, m_norm2_g, v_norm2_g),
              (gf, m_final_g.reshape(1, D_MODEL), v_final_g.reshape(1, D_MODEL)),
              (att_onorm_g, m_att_onorm_g, v_att_onorm_g), (hg_onorm_g, m_hg_onorm_g, v_hg_onorm_g),
              (hg_lb_logits, m_hg_lb_logits, v_hg_lb_logits)]
    dmod_blocks = small_all[:, :6 * D_MODEL].reshape(N_DEV, N_DEV, 6 * D_MODEL // N_DEV).transpose(1, 0, 2)
    res = _small_update(small_all, dmod_blocks, c_all, hg_lb_logits, w_ada[0], m_w_ada[0], v_w_ada[0], smalls, [in_chips[4]])
    _, recv_in = _copies_wait("reduce_chips_in_wait", _plan_reduce_chips, in_chips[0], in_chips[1], in_chips[2], in_chips[3],
                              [res[0]] + updated)
    big["w_in"] = [t.T[None] for t in _adamw_shard(chip, w_in[0].T, m_w_in[0].T, v_w_in[0].T, in_s32, recv_in[0], "adamw_w_in")]
    loss = res[0][0, 0]
    ada4 = [t[None] for t in res[1:5]]
    sm4 = {n: list(res[5 + 4 * i:9 + 4 * i]) for i, n in enumerate(["b_ada", "norm1_g", "norm2_g", "final_g", "att", "hg", "lb"])}
    sm4["final_g"] = [t.reshape(D_MODEL) for t in sm4["final_g"]]

    order = [ada4, sm4["b_ada"], sm4["norm1_g"], big["w_in"], sm4["lb"], sm4["hg"], sm4["att"], big["w_out"], sm4["norm2_g"],
             big["w_gate_up"], big["w_down"], sm4["final_g"]]
    return (loss, grad_x[None], *[o[0] for o in order], *[o[1] for o in order], *[o[2] for o in order], *[o[3] for o in order])


def _block_step(x2d, target, mod, norm1_g, hg_lb_logits, hg_onorm_g, att_onorm_g, norm2_g, gf, w_in_b, rest_weights, stage,
                after=()):
    h1, hq, hf, hi, hgt, aq, ak, av, hg_out, hg_o, hg_states = _in_fwd(x2d, mod, norm1_g, w_in_b, hg_lb_logits, hg_onorm_g, after)
    branch = [_att_fwd(aq, ak, av, d) for d in DILATIONS[:2]]
    behind = stage("attention_begun", branch[0][0], branch[1][0])
    branch += [_att_fwd(aq, ak, av, d) for d in DILATIONS[2:]]
    outs = [b[0] for b in branch]
    lses = [b[1] for b in branch]
    att, att_out = _att_combine(outs, lses, att_onorm_g, behind)
    w_out_b, w_gu_b, w_down_b = rest_weights(att_out)

    dx1, h2, act, dau, dff, ffn_sums, loss_part = _ffn(x2d, hg_out, att_out, target, mod, norm2_g, gf, w_out_b, w_gu_b, w_down_b)
    dw_gu = _weight_grad(dau, h2, "dw_gate_up", rounded=True)
    dw_down = _weight_grad(act, dff, "dw_down", rounded=True)

    back = _out_bwd(dx1, hg_out, att_out, mod, w_out_b, att, lses, att_onorm_g)
    dhg, dos, ccs = back[0], back[1:4], back[4:7]
    dw_out, dw_out_b, dgate1, d_att_g = back[7:11]
    behind = stage("mixer_weights_done", (dw_out, dw_out_b), dw_gu, dw_down)
    datt = []
    for i, d in enumerate(DILATIONS):
        datt.append(_att_bwd(aq, ak, av, dos[i], ccs[i], lses[i], d, behind))
    dhq, dhf, dhi, dhgt, d_hg_g, d_lb = _hg_bwd(hq, hf, hi, hgt, hg_lb_logits, hg_onorm_g, hg_o, hg_states, dhg)
    dps = [dhq, dhf, dhi, dhgt] + [datt[i][j] for j in range(3) for i in range(3)]
    grad_x, dp_b, dshift1, dscale1, d_g1 = _in_bwd(x2d, dx1, mod, norm1_g, w_in_b, dps)
    dw_in, = _weight_grad(dp_b, h1, "dw_in")
    small = jnp.concatenate([dshift1, dscale1, dgate1, ffn_sums[0:1], ffn_sums[1:2], ffn_sums[2:3], d_g1, ffn_sums[3:4],
                             ffn_sums[4:5], d_att_g, d_lb, d_hg_g, loss_part], axis=1)
    return grad_x, dw_in, small
```

```python
import functools

import jax
import jax.numpy as jnp
from jax import lax
from jax.experimental import pallas as pl
from jax.experimental.pallas import tpu as pltpu

F32 = jnp.float32
BF16 = jnp.bfloat16
HIGHEST = lax.Precision.HIGHEST
MESH = pl.DeviceIdType.MESH

D_MODEL = 1024
N_DEV = 8
HG_HEADS = 4
HG_DIM = 128
HG_WIDTH = HG_HEADS * HG_DIM
HG_CHUNK = 128
ATT_WIDTH = 512
ATT_HEAD_DIM = 64
ATT_BLOCK = 128
DILATIONS = (1, 4, 16)
ATT_SCALE = ATT_HEAD_DIM ** -0.5
D_FF = 2816
IN_WIDTH = 7 * 512
RMS_EPS = 1e-6
NEG = -1e30

ADAM_LR = 0.001
ADAM_B1 = 0.9
ADAM_B2 = 0.999
ADAM_EPS = 1e-08
ADAM_WD = 0.01
ADAM_STEP = 10

V7X_VMEM_LIMIT = 56 * 1024 * 1024

SM_MOD = 0
SM_G1 = 6 * D_MODEL
SM_G2 = 7 * D_MODEL
SM_GF = 8 * D_MODEL
SM_ATT = 9 * D_MODEL
SM_LB = 9 * D_MODEL + 512
SM_HG = 10 * D_MODEL
SM_LOSS = 10 * D_MODEL + 128
SM_WIDTH = 10 * D_MODEL + 256
SM_PADDED = 88 * 128


def _params(*sem, vmem=V7X_VMEM_LIMIT):
    return pltpu.CompilerParams(dimension_semantics=sem, vmem_limit_bytes=vmem)


def _dot(a, b):
    return jnp.dot(a, b, preferred_element_type=F32)


def _dot_nt(a, b):
    return lax.dot_general(a, b, (((1,), (1,)), ((), ())), preferred_element_type=F32)


def _dot_tn(a, b):
    return lax.dot_general(a, b, (((0,), (0,)), ((), ())), preferred_element_type=F32)


def _dot_f32(a, b):
    return jnp.dot(a, b, preferred_element_type=F32, precision=HIGHEST)


def _sigmoid(x):
    return 1.0 / (1.0 + jnp.exp(-x))


def _silu(x):
    return x * _sigmoid(x)


def _dsilu(x):
    s = _sigmoid(x)
    return s * (1.0 + x * (1.0 - s))


def _rms(x):
    rstd = lax.rsqrt(jnp.mean(x * x, axis=-1, keepdims=True) + RMS_EPS)
    return x * rstd, rstd


def _rms_bwd(dn, xhat, rstd):
    return rstd * (dn - xhat * jnp.mean(dn * xhat, axis=-1, keepdims=True))


def _rowsum(x):
    return jnp.sum(x, axis=0, keepdims=True)


def _rows(tm, n):
    return pl.BlockSpec((tm, n), lambda i: (i, 0))


def _whole(shape):
    return pl.BlockSpec(shape, lambda i: (0,) * len(shape))


def _mesh_pos():
    return lax.axis_index("x"), lax.axis_index("y"), lax.axis_index("c")


def _flip(k):
    x, y, c = _mesh_pos()
    px = 1 - x if k & 4 else x
    py = 1 - y if k & 2 else y
    pc = 1 - c if k & 1 else c
    return (px, py, pc), 4 * px + 2 * py + pc


_ANY = [pl.BlockSpec(memory_space=pl.ANY)]


def _behind(body, after):
    return lambda *refs: body(*refs[len(after):])


def _exchange_small(x, rows_per_peer, name, after=()):
    r_all, cols = x.shape
    r_out = r_all if rows_per_peer is None else rows_per_peer

    def body(x_ref, out_ref, send_sems, recv_sems):
        _, me = _flip(0)

        def src(pid):
            if rows_per_peer is None:
                return x_ref
            return x_ref.at[pl.ds(pl.multiple_of(pid * r_out, r_out), r_out), :]

        if rows_per_peer is None:
            out_ref[me] = x_ref[...]
        else:
            out_ref[me] = x_ref[pl.ds(pl.multiple_of(me * r_out, r_out), r_out), :]
        sends = []
        for k in range(1, N_DEV):
            dev, pid = _flip(k)
            cp = pltpu.make_async_remote_copy(src_ref=src(pid), dst_ref=out_ref.at[me], send_sem=send_sems.at[k - 1],
                                              recv_sem=recv_sems.at[k - 1], device_id=dev, device_id_type=MESH)
            cp.start()
            sends.append(cp)
        for k in range(1, N_DEV):
            dev, pid = _flip(k)
            pltpu.make_async_remote_copy(src_ref=src(pid), dst_ref=out_ref.at[pid], send_sem=send_sems.at[k - 1],
                                         recv_sem=recv_sems.at[k - 1], device_id=dev, device_id_type=MESH).wait_recv()
        for cp in sends:
            cp.wait_send()

    return pl.pallas_call(
        _behind(body, after), name=name,
        out_shape=jax.ShapeDtypeStruct((N_DEV, r_out, cols), x.dtype),
        in_specs=_ANY * len(after) + [pl.BlockSpec(memory_space=pltpu.VMEM)],
        out_specs=pl.BlockSpec(memory_space=pltpu.VMEM),
        scratch_shapes=[pltpu.SemaphoreType.DMA((N_DEV - 1,)), pltpu.SemaphoreType.DMA((N_DEV - 1,))],
    )(*after, x)


def _gather_weights(shards):
    n = len(shards)

    def body(*refs):
        xs, outs = refs[:n], refs[n:2 * n]
        send_sems, recv_sems, local_sems = refs[2 * n:]
        x, y, c = _mesh_pos()
        me, sibling = (x, y, c), (x, y, 1 - c)
        chips = [(1 - x, y), (x, 1 - y), (1 - x, 1 - y)]

        def blk(a, px, py, pc):
            return outs[a].at[4 * px + 2 * py + pc]

        def copy(a, k, block, to, src=None):
            return pltpu.make_async_remote_copy(
                src_ref=blk(a, *block) if src is None else src, dst_ref=blk(a, *block),
                send_sem=send_sems.at[a * 7 + k], recv_sem=recv_sems.at[a * 7 + k], device_id=to, device_id_type=MESH)

        mine = [pltpu.make_async_copy(xs[a], blk(a, *me), local_sems.at[a]) for a in range(n)]
        for cp in mine:
            cp.start()
        first = []
        for a in range(n):
            first.append(copy(a, 0, me, sibling, src=xs[a]))
            first += [copy(a, 1 + j, me, (*chip, c), src=xs[a]) for j, chip in enumerate(chips)]
        for cp in first:
            cp.start()
        passed = []
        for j, chip in enumerate(chips):
            for a in range(n):
                copy(a, 1 + j, (*chip, c), me).wait_recv()
                cp = copy(a, 4 + j, (*chip, c), sibling)
                cp.start()
                passed.append(cp)
        for a in range(n):
            copy(a, 0, sibling, me).wait_recv()
            for j, chip in enumerate(chips):
                copy(a, 4 + j, (*chip, 1 - c), me).wait_recv()
        for cp in first + passed:
            cp.wait_send()
        for cp in mine:
            cp.wait()

    hbm = pl.BlockSpec(memory_space=pl.ANY)
    return pl.pallas_call(
        body, name="gather_weights",
        out_shape=[jax.ShapeDtypeStruct((N_DEV,) + s.shape, s.dtype) for s in shards],
        in_specs=[hbm] * n, out_specs=[hbm] * n,
        scratch_shapes=[pltpu.SemaphoreType.DMA((7 * n,)), pltpu.SemaphoreType.DMA((7 * n,)), pltpu.SemaphoreType.DMA((n,))],
    )(*shards)


_HBM = pl.BlockSpec(memory_space=pltpu.HBM)
_SEM = pl.BlockSpec(memory_space=pltpu.SEMAPHORE)
_DATAFLOW = pltpu.SideEffectType.DATAFLOW_SIDE_EFFECTING


def _copies_start(name, plan, n_copies, srcs, lands, after):
    bufs = list(srcs) + list(lands)
    nb = len(bufs)

    def body(*refs):
        ins, send_sems, recv_sems, token = refs[:nb], refs[nb + len(after)], refs[nb + len(after) + 1], refs[-1]
        for i, (src, dst, dev) in enumerate(plan(ins[:len(srcs)], ins[len(srcs):])):
            pltpu.make_async_remote_copy(src_ref=src, dst_ref=dst, send_sem=send_sems.at[i], recv_sem=recv_sems.at[i],
                                         device_id=dev, device_id_type=MESH).start()
        token[...] = jnp.zeros_like(token)

    outs = pl.pallas_call(
        body, name=name,
        out_shape=(pltpu.SemaphoreType.DMA((n_copies,)), pltpu.SemaphoreType.DMA((n_copies,)),
                   *[pltpu.HBM(b.shape, b.dtype) for b in bufs], jax.ShapeDtypeStruct((8, 128), F32)),
        in_specs=[_HBM] * nb + [pl.BlockSpec(memory_space=pl.ANY)] * len(after),
        out_specs=(_SEM, _SEM, *[_HBM] * nb, pl.BlockSpec(memory_space=pltpu.VMEM)),
        input_output_aliases={i: 2 + i for i in range(nb)},
        compiler_params=pltpu.CompilerParams(has_side_effects=_DATAFLOW),
    )(*[pltpu.with_memory_space_constraint(b, pltpu.HBM) for b in bufs], *after)
    return outs[0], outs[1], list(outs[2:2 + len(srcs)]), list(outs[2 + len(srcs):2 + nb]), outs[-1]


def _copies_wait(name, plan, send_sems, recv_sems, srcs, lands, after):
    bufs = list(srcs) + list(lands)
    nb = len(bufs)

    def body(*refs):
        ins, send_ref, recv_ref = refs[:nb], refs[nb], refs[nb + 1]
        for i, (src, dst, dev) in enumerate(plan(ins[:len(srcs)], ins[len(srcs):])):
            cp = pltpu.make_async_remote_copy(src_ref=src, dst_ref=dst, send_sem=send_ref.at[i], recv_sem=recv_ref.at[i],
                                              device_id=dev, device_id_type=MESH)
            cp.wait_send()
            cp.wait_recv()

    outs = pl.pallas_call(
        body, name=name, out_shape=[pltpu.HBM(b.shape, b.dtype) for b in bufs],
        in_specs=[_HBM] * nb + [_SEM, _SEM] + [pl.BlockSpec(memory_space=pl.ANY)] * len(after), out_specs=[_HBM] * nb,
        input_output_aliases={i: i for i in range(nb)},
        compiler_params=pltpu.CompilerParams(has_side_effects=_DATAFLOW),
    )(*bufs, send_sems, recv_sems, *after)
    return list(outs[:len(srcs)]), list(outs[len(srcs):])


def _plan_gather_own(srcs, lands):
    _, me = _flip(0)
    return [(srcs[a], lands[a].at[me], _flip(k)[0]) for a in range(len(srcs)) for k in (1, 4, 2, 6)]


def _plan_gather_pass(srcs, lands):
    sibling = _flip(1)[0]
    plan = []
    for land in lands:
        for k in (4, 2, 6):
            block = land.at[_flip(k)[1]]
            plan.append((block, block, sibling))
    return plan


def _plan_reduce_pairs(srcs, lands):
    x, y, c = _mesh_pos()
    return [(srcs[a].at[chip, 1 - c], lands[a].at[chip], (x, y, 1 - c)) for a in range(len(srcs)) for chip in range(4)]


def _plan_reduce_chips(srcs, lands):
    plan = []
    for a in range(len(srcs)):
        for j, k in enumerate((4, 2, 6)):
            dev = _flip(k)[0]
            plan.append((srcs[a].at[2 * dev[0] + dev[1]], lands[a].at[j], dev))
    return plan


def _plan_reduce_direct(srcs, lands):
    plan = []
    for a in range(len(srcs)):
        for k in range(1, N_DEV):
            dev = _flip(k)[0]
            plan.append((srcs[a].at[2 * dev[0] + dev[1], dev[2]], lands[a].at[k - 1], dev))
    return plan


def _shard_rows(r):
    return r // 2 if r % 32 == 0 else r


def _pair_sum(core, grads, got, name):
    _, _, r, c = grads.shape
    tr = _shard_rows(r)

    def body(core_ref, a_ref, b_ref, o_ref, ob_ref):
        s = a_ref[...] + b_ref[...]
        o_ref[...] = s
        ob_ref[...] = s.astype(BF16)

    spec = pl.BlockSpec((None, tr, c), lambda i, j, core_ref: (i, j, 0))
    return pl.pallas_call(
        body, name=name,
        grid_spec=pltpu.PrefetchScalarGridSpec(
            num_scalar_prefetch=1, grid=(4, r // tr),
            in_specs=[pl.BlockSpec((None, None, tr, c), lambda i, j, core_ref: (i, core_ref[0], j, 0)), spec],
            out_specs=[spec, spec]),
        out_shape=[jax.ShapeDtypeStruct((4, r, c), F32), jax.ShapeDtypeStruct((4, r, c), BF16)],
        compiler_params=_params("parallel", "parallel"),
    )(core, grads, got)


def _ada_rows(c_all, w_ada, b_ada):
    n_cols = w_ada.shape[1]

    def body(c_ref, w_ref, b_ref, o_ref):
        _, me = _flip(0)
        bias = b_ref[:, pl.ds(pl.multiple_of(me * n_cols, 128), n_cols)]
        o_ref[...] = _dot_f32(_silu(c_ref[...]), w_ref[...]) + bias

    return pl.pallas_call(
        body, name="ada_rows", out_shape=jax.ShapeDtypeStruct((N_DEV, n_cols), F32),
        in_specs=[pl.BlockSpec(memory_space=pltpu.VMEM)] * 3, out_specs=pl.BlockSpec(memory_space=pltpu.VMEM),
    )(c_all, w_ada, b_ada)


def _in_fwd(x, mod, g1, w_in, logits, onorm_g, after=()):
    s = x.shape[0]
    tm = HG_TILE

    def body(x_ref, mod_ref, g_ref, w_ref, lg_ref, og_ref, h_ref, *rest):
        groups, (out_ref, o_ref, st_ref, state, qf_s, kk_s, lf_s) = rest[:7], rest[7:]
        xhat, _ = _rms(x_ref[...])
        h = (xhat * g_ref[...]) * (1.0 + mod_ref[:, D_MODEL:2 * D_MODEL]) + mod_ref[:, 0:D_MODEL]
        hb = h.astype(BF16)
        h_ref[...] = hb
        for j, o_ref_j in enumerate(groups):
            o_ref_j[...] = _dot_nt(hb, w_ref[j * 512:(j + 1) * 512, :])
        _hg_fwd_tile(*groups[:4], lg_ref, og_ref, out_ref, o_ref, st_ref, state, qf_s, kk_s, lf_s)

    tile = _rows(tm, 512)
    return pl.pallas_call(
        _behind(body, after), name="in_fwd", grid=(s // tm,),
        out_shape=[jax.ShapeDtypeStruct((s, D_MODEL), BF16)] + [jax.ShapeDtypeStruct((s, 512), F32)] * 7
        + [jax.ShapeDtypeStruct((s, HG_WIDTH), BF16), jax.ShapeDtypeStruct((s, HG_WIDTH), F32),
           jax.ShapeDtypeStruct((s // HG_CHUNK * HG_DIM, HG_WIDTH), F32)],
        in_specs=_ANY * len(after)
        + [_rows(tm, D_MODEL), _whole((1, 6 * D_MODEL)), _whole((1, D_MODEL)), _whole((IN_WIDTH, D_MODEL)),
           _whole((2, HG_WIDTH)), _whole((1, HG_DIM))],
        out_specs=[_rows(tm, D_MODEL)] + [tile] * 7 + [tile, tile, _rows(HG_TILE_CHUNKS * HG_DIM, HG_WIDTH)],
        scratch_shapes=[pltpu.VMEM((HG_HEADS, HG_DIM, HG_DIM), F32)] + [pltpu.VMEM((HG_TILE, HG_WIDTH), F32)] * 3,
        compiler_params=_params("arbitrary"),
    )(*after, x, mod, g1, w_in, logits, onorm_g)


def _in_bwd(x, dx1, mod, g1, w_in, dps):
    s = x.shape[0]
    tm = 256

    def body(x_ref, dx_ref, mod_ref, g_ref, w_ref, *rest):
        dp_refs, (gx_ref, dpb_ref, dsh_ref, dsc_ref, dg_ref) = rest[:13], rest[13:]
        pieces = [dp_refs[j][...] for j in range(4)]
        pieces += [dp_refs[4 + 3 * j][...] + dp_refs[5 + 3 * j][...] + dp_refs[6 + 3 * j][...] for j in range(3)]
        for j, p in enumerate(pieces):
            dpb_ref[:, j * 512:(j + 1) * 512] = p.astype(BF16)
        dh = _dot(dpb_ref[...], w_ref[...])
        xhat, rstd = _rms(x_ref[...])
        g = g_ref[...]
        scale1 = 1.0 + mod_ref[:, D_MODEL:2 * D_MODEL]
        n1 = xhat * g

        @pl.when(pl.program_id(0) == 0)
        def _():
            dsh_ref[...] = jnp.zeros_like(dsh_ref)
            dsc_ref[...] = jnp.zeros_like(dsc_ref)
            dg_ref[...] = jnp.zeros_like(dg_ref)

        dsh_ref[...] += _rowsum(dh)
        dsc_ref[...] += _rowsum(dh * n1)
        dn = dh * scale1
        dg_ref[...] += _rowsum(dn * xhat)
        gx_ref[...] = dx_ref[...] + _rms_bwd(dn * g, xhat, rstd)

    vec = _whole((1, D_MODEL))
    return pl.pallas_call(
        body, name="in_bwd", grid=(s // tm,),
        out_shape=[jax.ShapeDtypeStruct((s, D_MODEL), F32), jax.ShapeDtypeStruct((s, IN_WIDTH), BF16)]
        + [jax.ShapeDtypeStruct((1, D_MODEL), F32)] * 3,
        in_specs=[_rows(tm, D_MODEL), _rows(tm, D_MODEL), _whole((1, 6 * D_MODEL)), vec, _whole((IN_WIDTH, D_MODEL))]
        + [_rows(tm, 512)] * 13,
        out_specs=[_rows(tm, D_MODEL), _rows(tm, IN_WIDTH), vec, vec, vec],
        compiler_params=_params("arbitrary"),
    )(x, dx1, mod, g1, w_in, *dps)


HG_TILE = 512
HG_TILE_CHUNKS = HG_TILE // HG_CHUNK


def _lower_bound(lg_ref):
    return 1.0 / (1.0 + jnp.exp(lg_ref[1:2, :] - lg_ref[0:1, :]))


def _chunk_masks():
    r = lax.broadcasted_iota(jnp.int32, (HG_CHUNK, HG_CHUNK), 0)
    c = lax.broadcasted_iota(jnp.int32, (HG_CHUNK, HG_CHUNK), 1)
    return r >= c, c >= r, (r >= c).astype(F32), (c >= r).astype(F32)


def _hg_fwd_tile(q_ref, f_ref, i_ref, g_ref, lg_ref, og_ref, out_ref, o_ref, st_ref, state, qf_s, kk_s, lf_s):
    @pl.when(pl.program_id(0) == 0)
    def _():
        state[...] = jnp.zeros_like(state)

    lb = _lower_bound(lg_ref)
    f = lb + (1.0 - lb) * _sigmoid(f_ref[...])
    kk_s[...] = 1.0 - f
    lf_s[...] = jnp.log(f)
    qf_s[...] = _silu(q_ref[...])
    causal, _, tri, _ = _chunk_masks()

    def chunk(ci, carry):
        rows = pl.ds(pl.multiple_of(ci * HG_CHUNK, HG_CHUNK), HG_CHUNK)
        srows = pl.ds(pl.multiple_of(ci * HG_DIM, HG_DIM), HG_DIM)
        lf = lf_s[rows, :]
        b = _dot_f32(tri, lf)
        bl = _rowsum(lf)
        ref = 0.5 * bl
        qf, kk, v = qf_s[rows, :], kk_s[rows, :], i_ref[rows, :]
        a_in = (qf * jnp.exp(b)).astype(BF16)
        a_t = (qf * jnp.exp(b - ref)).astype(BF16)
        b_t = (kk * jnp.exp(ref - b)).astype(BF16)
        kd = kk * jnp.exp(bl - b)
        ebl = jnp.exp(bl)
        vb = v.astype(BF16)
        for h in range(HG_HEADS):
            c = slice(h * HG_DIM, (h + 1) * HG_DIM)
            st = state[h]
            st_ref[srows, c] = st
            p = jnp.where(causal, _dot_nt(a_t[:, c], b_t[:, c]), 0.0)
            o_ref[rows, c] = _dot(p.astype(BF16), vb[:, c]) + _dot_nt(a_in[:, c], st.astype(BF16))
            state[h] = st * ebl[:, c] + _dot_tn(vb[:, c], kd[:, c].astype(BF16))
        return carry

    lax.fori_loop(0, HG_TILE_CHUNKS, chunk, 0, unroll=True)
    for h in range(HG_HEADS):
        c = slice(h * HG_DIM, (h + 1) * HG_DIM)
        ohat, _ = _rms(o_ref[:, c])
        out_ref[:, c] = (ohat * og_ref[...] * _silu(g_ref[:, c])).astype(BF16)


def _hg_bwd(hq, hf, hi, hgt, logits, onorm_g, o, states, dout):
    s = hq.shape[0]
    n_tiles = s // HG_TILE

    def body(q_ref, f_ref, i_ref, g_ref, lg_ref, og_ref, o_ref, st_ref, d_ref,
             dq_ref, df_ref, di_ref, dg_ref, dog_ref, dlb_ref, dstate, qf_s, kk_s, lf_s, do_s):
        @pl.when(pl.program_id(0) == 0)
        def _():
            dstate[...] = jnp.zeros_like(dstate)
            dog_ref[...] = jnp.zeros_like(dog_ref)
            dlb_ref[...] = jnp.zeros_like(dlb_ref)

        og = og_ref[...]
        dog = jnp.zeros((1, HG_DIM), F32)
        for h in range(HG_HEADS):
            c = slice(h * HG_DIM, (h + 1) * HG_DIM)
            ohat, rstd = _rms(o_ref[:, c])
            gate = g_ref[:, c]
            d = d_ref[:, c]
            dg_ref[:, c] = (d * (ohat * og) * _dsilu(gate)).astype(BF16)
            dnormed = d * _silu(gate)
            dog += _rowsum(dnormed * ohat)
            do_s[:, c] = _rms_bwd(dnormed * og, ohat, rstd)
        dog_ref[...] += dog

        lb = _lower_bound(lg_ref)
        f = lb + (1.0 - lb) * _sigmoid(f_ref[...])
        kk_s[...] = 1.0 - f
        lf_s[...] = jnp.log(f)
        qf_s[...] = _silu(q_ref[...])
        causal, upper, tri, tri_t = _chunk_masks()

        def chunk(step, carry):
            ci = HG_TILE_CHUNKS - 1 - step
            rows = pl.ds(pl.multiple_of(ci * HG_CHUNK, HG_CHUNK), HG_CHUNK)
            srows = pl.ds(pl.multiple_of(ci * HG_DIM, HG_DIM), HG_DIM)
            lf = lf_s[rows, :]
            b = _dot_f32(tri, lf)
            bl = _rowsum(lf)
            ref = 0.5 * bl
            qf, kk, v, do = qf_s[rows, :], kk_s[rows, :], i_ref[rows, :], do_s[rows, :]
            eb, ebr, erb, ekd, ebl = jnp.exp(b), jnp.exp(b - ref), jnp.exp(ref - b), jnp.exp(bl - b), jnp.exp(bl)
            a_in, a_t, b_t, kd = qf * eb, qf * ebr, kk * erb, kk * ekd
            for h in range(HG_HEADS):
                c = slice(h * HG_DIM, (h + 1) * HG_DIM)
                st, dst = st_ref[srows, c], dstate[h]
                stb, dstb = st.astype(BF16), dst.astype(BF16)
                doh, vh = do[:, c], v[:, c]
                dob, vb = doh.astype(BF16), vh.astype(BF16)
                ain_h, at_h, bt_h, kd_h = a_in[:, c], a_t[:, c], b_t[:, c], kd[:, c]
                atb, btb = at_h.astype(BF16), bt_h.astype(BF16)
                d_ain = _dot(dob, stb)
                p_t = jnp.where(upper, _dot_nt(btb, atb), 0.0).astype(BF16)
                dp = jnp.where(causal, _dot_nt(dob, vb), 0.0).astype(BF16)
                dp_t = jnp.where(upper, _dot_nt(vb, dob), 0.0).astype(BF16)
                di_ref[rows, c] = (_dot(p_t, dob) + _dot_nt(kd_h.astype(BF16), dstb)).astype(BF16)
                d_at = _dot(dp, btb)
                d_bt = _dot(dp_t, atb)
                d_kd = _dot(vb, dstb)
                dqf = d_ain * eb[:, c] + d_at * ebr[:, c]
                dkk = d_bt * erb[:, c] + d_kd * ekd[:, c]
                db = d_ain * ain_h + d_at * atb.astype(F32) - d_bt * btb.astype(F32) - d_kd * kd_h
                dbl = _rowsum(d_kd * kd_h) + _rowsum(dst * st) * ebl[:, c]
                dstate[h] = _dot_tn(dob, ain_h.astype(BF16)) + dst * ebl[:, c]
                dlf = _dot_f32(tri_t, db) + dbl
                qv, fr = q_ref[rows, c], f_ref[rows, c]
                lbh = lb[:, c]
                sg = _sigmoid(fr)
                dfv = dlf / (lbh + (1.0 - lbh) * sg) - dkk
                df_ref[rows, c] = (dfv * (1.0 - lbh) * sg * (1.0 - sg)).astype(BF16)
                dlb_ref[:, c] += _rowsum(dfv * (1.0 - sg))
                dq_ref[rows, c] = (dqf * _dsilu(qv)).astype(BF16)
            return carry

        lax.fori_loop(0, HG_TILE_CHUNKS, chunk, 0, unroll=True)

    rev = pl.BlockSpec((HG_TILE, HG_WIDTH), lambda i: (n_tiles - 1 - i, 0))
    return pl.pallas_call(
        body, name="hg_bwd", grid=(n_tiles,),
        out_shape=[jax.ShapeDtypeStruct((s, HG_WIDTH), BF16)] * 4
        + [jax.ShapeDtypeStruct((1, HG_DIM), F32), jax.ShapeDtypeStruct((1, HG_WIDTH), F32)],
        in_specs=[rev] * 4 + [_whole((2, HG_WIDTH)), _whole((1, HG_DIM)), rev,
                              pl.BlockSpec((HG_TILE_CHUNKS * HG_DIM, HG_WIDTH), lambda i: (n_tiles - 1 - i, 0)), rev],
        out_specs=[rev] * 4 + [_whole((1, HG_DIM)), _whole((1, HG_WIDTH))],
        scratch_shapes=[pltpu.VMEM((HG_HEADS, HG_DIM, HG_DIM), F32)] + [pltpu.VMEM((HG_TILE, HG_WIDTH), F32)] * 4,
        compiler_params=_params("arbitrary"),
    )(hq, hf, hi, hgt, logits, onorm_g, o, states, dout)


TOKEN_GROUP = 16


def _att_geometry(dil, seq=0):
    per_group = TOKEN_GROUP // dil
    ub = ATT_BLOCK // per_group
    if dil == TOKEN_GROUP:
        n_blocks = 2 if seq % (2 * ub * TOKEN_GROUP) == 0 and seq > 0 else 1
    else:
        n_blocks = 4
    return per_group, ub, ATT_WIDTH if dil == 1 else 128, n_blocks


def _att_consts(dil):
    per_group, ub = _att_geometry(dil)[:2]

    def pos(i):
        return i if dil == 1 else (i % ub) * per_group + i // ub

    lane = lax.broadcasted_iota(jnp.int32, (ATT_BLOCK, 128), 1)
    qi = pos(lax.broadcasted_iota(jnp.int32, (2 * ATT_BLOCK, ATT_BLOCK), 0) % ATT_BLOCK)
    kj = pos(lax.broadcasted_iota(jnp.int32, (2 * ATT_BLOCK, ATT_BLOCK), 1))
    return lane < ATT_HEAD_DIM, kj <= qi, lambda off: kj >= qi + off


def _load_tile(ref, dil, r, c, base=0):
    per_group, ub = _att_geometry(dil)[:2]
    if dil == 1:
        return ref[base:base + ATT_BLOCK, c]
    return jnp.concatenate([ref[pl.ds(base + dil * w + r, ub, stride=TOKEN_GROUP), c] for w in range(per_group)], axis=0)


def _store_tile(ref, dil, r, c, val, base=0):
    per_group, ub = _att_geometry(dil)[:2]
    if dil == 1:
        ref[base:base + ATT_BLOCK, c] = val
        return
    for w in range(per_group):
        ref[pl.ds(base + dil * w + r, ub, stride=TOKEN_GROUP), c] = val[w * ub:(w + 1) * ub]


def _stack_heads(x2, first):
    return jnp.concatenate([jnp.where(first, x2, 0.0), jnp.where(first, 0.0, x2)], axis=0)


def _stack_bcast(x2, first):
    other = pltpu.roll(x2, ATT_HEAD_DIM, axis=1)
    return jnp.concatenate([jnp.where(first, x2, other), jnp.where(first, other, x2)], axis=0)


def _unstack_heads(st, first):
    return jnp.where(first, st[:ATT_BLOCK], st[ATT_BLOCK:])


def _att_fwd(q, k, v, dil):
    seq, width = q.shape
    _, ub, lanes, nbs = _att_geometry(dil, seq)
    rows = ub * TOKEN_GROUP
    n_steps = seq // (nbs * rows)

    def body(q_ref, k_ref, v_ref, kp_ref, vp_ref, o_ref, lse_ref):
        first, cur_ok, _band = _att_consts(dil)
        inner_ok = _band(0)
        edge_ok = _band(jnp.where(pl.program_id(0) > 0, 0, ATT_BLOCK))
        for r in range(dil):
            for j in range(lanes // 128):
                c = slice(j * 128, (j + 1) * 128)
                kc = vc = None
                for b in range(nbs):
                    base = b * rows
                    prev_ok = edge_ok if b == 0 else inner_ok
                    if b == 0:
                        kp, vp = _load_tile(kp_ref, dil, r, c).astype(BF16), _load_tile(vp_ref, dil, r, c).astype(BF16)
                    else:
                        kp, vp = kc, vc
                    qst = _stack_heads(_load_tile(q_ref, dil, r, c, base) * ATT_SCALE, first).astype(BF16)
                    kc = _load_tile(k_ref, dil, r, c, base).astype(BF16)
                    vc = _load_tile(v_ref, dil, r, c, base).astype(BF16)
                    sc = jnp.where(cur_ok, _dot_nt(qst, kc), NEG)
                    sp = jnp.where(prev_ok, _dot_nt(qst, kp), NEG)
                    mx = jnp.max(jnp.maximum(sc, sp), axis=-1, keepdims=True)
                    pc, pp = jnp.exp(sc - mx), jnp.exp(sp - mx)
                    den = jnp.sum(pc + pp, axis=-1, keepdims=True)
                    ost = (_dot(pc.astype(BF16), vc) + _dot(pp.astype(BF16), vp)) / den
                    lse = jnp.broadcast_to(mx + jnp.log(den), (2 * ATT_BLOCK, 128))
                    _store_tile(o_ref, dil, r, c, _unstack_heads(ost, first), base)
                    _store_tile(lse_ref, dil, r, c, _unstack_heads(lse, first), base)

    slab = pl.BlockSpec((nbs * rows, lanes), lambda n, j: (n, j))
    before = pl.BlockSpec((rows, lanes), lambda n, j: (jnp.maximum(n * nbs - 1, 0), j))
    return pl.pallas_call(
        body, name=f"att_fwd_d{dil}", grid=(n_steps, width // lanes),
        out_shape=[jax.ShapeDtypeStruct((seq, width), F32)] * 2,
        in_specs=[slab, slab, slab, before, before], out_specs=[slab, slab],
        compiler_params=_params("arbitrary", "arbitrary"),
    )(q, k, v, k, v)


def _att_bwd(q, k, v, do, cc, lse, dil, behind=()):
    seq, width = q.shape
    _, ub, lanes, nbs = _att_geometry(dil, seq)
    rows = ub * TOKEN_GROUP
    n_blocks = seq // rows
    n_steps = n_blocks // nbs

    def body(q_ref, k_ref, v_ref, do_ref, cc_ref, lse_ref, qx_ref, dox_ref, ccx_ref, lsex_ref,
             dq_ref, dk_ref, dv_ref, carry):
        first, cur_ok, _band = _att_consts(dil)
        step = pl.program_id(1)
        inner_ok = _band(0)
        edge_ok = _band(jnp.where(step < n_steps - 1, 0, ATT_BLOCK))

        @pl.when(step == 0)
        def _():
            carry[...] = jnp.zeros_like(carry)

        def queries(refs, r, c, base):
            q_r, do_r, lse_r, cc_r = refs
            return (_stack_heads(_load_tile(q_r, dil, r, c, base) * ATT_SCALE, first).astype(BF16),
                    _stack_heads(_load_tile(do_r, dil, r, c, base), first).astype(BF16),
                    _stack_bcast(_load_tile(lse_r, dil, r, c, base), first),
                    _stack_bcast(_load_tile(cc_r, dil, r, c, base), first))

        for r in range(dil):
            for j in range(lanes // 128):
                c = slice(j * 128, (j + 1) * 128)
                own = queries((q_ref, do_ref, lse_ref, cc_ref), r, c, 0)
                left = _load_tile(carry, dil, r, c)
                for b in range(nbs):
                    base = b * rows
                    last = b == nbs - 1
                    next_ok = edge_ok if last else inner_ok
                    if last:
                        following = queries((qx_ref, dox_ref, lsex_ref, ccx_ref), r, c, 0)
                    else:
                        following = queries((q_ref, do_ref, lse_ref, cc_ref), r, c, base + rows)
                    (qst, dost, lse_n, cc_n), (qxst, doxst, lse_x, cc_x) = own, following
                    kb = _load_tile(k_ref, dil, r, c, base).astype(BF16)
                    vb = _load_tile(v_ref, dil, r, c, base).astype(BF16)
                    p_cur = jnp.exp(jnp.where(cur_ok, _dot_nt(qst, kb), NEG) - lse_n)
                    p_next = jnp.exp(jnp.where(next_ok, _dot_nt(qxst, kb), NEG) - lse_x)
                    ds_cur = (p_cur * (_dot_nt(dost, vb) + cc_n)).astype(BF16)
                    ds_next = (p_next * (_dot_nt(doxst, vb) + cc_x)).astype(BF16)
                    dq_own = left + _unstack_heads(_dot(ds_cur, kb), first)
                    _store_tile(dq_ref, dil, r, c, dq_own * ATT_SCALE, base)
                    _store_tile(dk_ref, dil, r, c, _dot_tn(ds_cur, qst) + _dot_tn(ds_next, qxst), base)
                    _store_tile(dv_ref, dil, r, c, _dot_tn(p_cur.astype(BF16), dost) + _dot_tn(p_next.astype(BF16), doxst), base)
                    left = _unstack_heads(_dot(ds_next, kb), first)
                    own = following
                _store_tile(carry, dil, r, c, left)

    slab = pl.BlockSpec((nbs * rows, lanes), lambda j, n: (n, j))
    after = pl.BlockSpec((rows, lanes), lambda j, n: (jnp.minimum((n + 1) * nbs, n_blocks - 1), j))
    return pl.pallas_call(
        _behind(body, behind), name=f"att_bwd_d{dil}", grid=(width // lanes, n_steps),
        out_shape=[jax.ShapeDtypeStruct((seq, width), F32)] * 3,
        in_specs=_ANY * len(behind) + [slab] * 6 + [after] * 4, out_specs=[slab] * 3,
        scratch_shapes=[pltpu.VMEM((rows, lanes), F32)],
        compiler_params=_params("arbitrary", "arbitrary"),
    )(*behind, q, k, v, do, cc, lse, q, do, cc, lse)


def _branch_weights(lses):
    mx = jnp.maximum(jnp.maximum(lses[0], lses[1]), lses[2])
    es = [jnp.exp(l - mx) for l in lses]
    inv = 1.0 / (es[0] + es[1] + es[2])
    return [e * inv for e in es]


def _att_combine(outs, lses, att_g, after=()):
    s = outs[0].shape[0]
    tm = 512

    def body(o0, o1, o2, l0, l1, l2, g_ref, att_ref, out_ref):
        ws = _branch_weights([l0[...], l1[...], l2[...]])
        att = ws[0] * o0[...] + ws[1] * o1[...] + ws[2] * o2[...]
        att_ref[...] = att
        ahat, _ = _rms(att)
        out_ref[...] = (ahat * g_ref[...]).astype(BF16)

    tile = _rows(tm, ATT_WIDTH)
    return pl.pallas_call(
        _behind(body, after), name="att_combine", grid=(s // tm,),
        out_shape=[jax.ShapeDtypeStruct((s, ATT_WIDTH), F32), jax.ShapeDtypeStruct((s, ATT_WIDTH), BF16)],
        in_specs=_ANY * len(after) + [tile] * 6 + [_whole((1, ATT_WIDTH))], out_specs=[tile, tile],
        compiler_params=_params("parallel"),
    )(*after, *outs, *lses, att_g)


def _att_combine_bwd_tile(d, att, lses, g):
    ahat, rstd = _rms(att)
    datt = _rms_bwd(d * g, ahat, rstd)
    hi = lax.broadcasted_iota(jnp.int32, (ATT_WIDTH, ATT_WIDTH), 0) // ATT_HEAD_DIM
    hj = lax.broadcasted_iota(jnp.int32, (ATT_WIDTH, ATT_WIDTH), 1) // ATT_HEAD_DIM
    same_head = (hi == hj).astype(BF16)
    prod = datt * att
    prod_hi = prod.astype(BF16)
    prod_lo = (prod - prod_hi.astype(F32)).astype(BF16)
    head_sum = _dot(prod_hi, same_head) + _dot(prod_lo, same_head)
    ws = _branch_weights(lses)
    return [w * datt for w in ws], [-w * head_sum for w in ws], _rowsum(d * ahat)


def _out_bwd(dx1, hg, at, mod, w_out, att, lses, att_g):
    s = dx1.shape[0]
    tm = 512
    n_steps = s // tm

    def body(dx_ref, hg_ref, at_ref, mod_ref, w_ref, att_ref, l0, l1, l2, g_ref,
             dhg_ref, do0, do1, do2, cc0, cc1, cc2, dw_ref, dwb_ref, dgate_ref, dg_ref):
        @pl.when(pl.program_id(0) == 0)
        def _():
            dw_ref[...] = jnp.zeros_like(dw_ref)
            dgate_ref[...] = jnp.zeros_like(dgate_ref)
            dg_ref[...] = jnp.zeros_like(dg_ref)

        hg, at, dx = hg_ref[...], at_ref[...], dx_ref[...]
        mix = _dot(hg, w_ref[0:512, :]) + _dot(at, w_ref[512:1024, :])
        dgate_ref[...] += _rowsum(dx * mix)
        dmix = (mod_ref[:, 2 * D_MODEL:3 * D_MODEL] * dx).astype(BF16)
        dhg_ref[...] = _dot_nt(dmix, w_ref[0:512, :])
        dos, ccs, dg_rows = _att_combine_bwd_tile(_dot_nt(dmix, w_ref[512:1024, :]), att_ref[...],
                                                  [l0[...], l1[...], l2[...]], g_ref[...])
        for val, ref in zip(dos + ccs, (do0, do1, do2, cc0, cc1, cc2)):
            ref[...] = val
        dg_ref[...] += dg_rows
        dw_ref[0:512, :] += _dot_tn(hg, dmix)
        dw_ref[512:1024, :] += _dot_tn(at, dmix)

        @pl.when(pl.program_id(0) == n_steps - 1)
        def _():
            dwb_ref[...] = dw_ref[...].astype(BF16)

    tile = _rows(tm, 512)
    square = _whole((D_MODEL, D_MODEL))
    return pl.pallas_call(
        body, name="out_bwd", grid=(n_steps,),
        out_shape=[jax.ShapeDtypeStruct((s, 512), F32)] * 7
        + [jax.ShapeDtypeStruct((D_MODEL, D_MODEL), F32), jax.ShapeDtypeStruct((D_MODEL, D_MODEL), BF16),
           jax.ShapeDtypeStruct((1, D_MODEL), F32), jax.ShapeDtypeStruct((1, ATT_WIDTH), F32)],
        in_specs=[_rows(tm, D_MODEL), tile, tile, _whole((1, 6 * D_MODEL)), square] + [tile] * 4 + [_whole((1, ATT_WIDTH))],
        out_specs=[tile] * 7 + [square, square, _whole((1, D_MODEL)), _whole((1, ATT_WIDTH))],
        compiler_params=_params("arbitrary"),
    )(dx1, hg, at, mod, w_out, att, *lses, att_g)


def _ffn(x, hg, at, target, mod, g2, gf, w_out, w_gu, w_down):
    s = x.shape[0]
    tm = 256

    def body(x_ref, hg_ref, at_ref, t_ref, mod_ref, g2_ref, gf_ref, wo_ref, wgu_hbm, wd_hbm,
             dx_ref, h2_ref, act_ref, dau_ref, dff_ref, sums_ref, loss_ref, wgu, wd, au_s, sem):
        @pl.when(pl.program_id(0) == 0)
        def _():
            c1 = pltpu.make_async_copy(wgu_hbm, wgu, sem.at[0])
            c2 = pltpu.make_async_copy(wd_hbm, wd, sem.at[1])
            c1.start()
            c2.start()
            c1.wait()
            c2.wait()
            sums_ref[...] = jnp.zeros_like(sums_ref)
            loss_ref[...] = jnp.zeros_like(loss_ref)

        mix = _dot(hg_ref[...], wo_ref[0:512, :]) + _dot(at_ref[...], wo_ref[512:1024, :])
        x1v = x_ref[...] + mod_ref[:, 2 * D_MODEL:3 * D_MODEL] * mix
        xhat, rstd = _rms(x1v)
        g2 = g2_ref[...]
        n2 = xhat * g2
        scale2 = 1.0 + mod_ref[:, 4 * D_MODEL:5 * D_MODEL]
        gate2 = mod_ref[:, 5 * D_MODEL:6 * D_MODEL]
        hb = (n2 * scale2 + mod_ref[:, 3 * D_MODEL:4 * D_MODEL]).astype(BF16)
        h2_ref[...] = hb
        au_s[...] = _dot_nt(hb, wgu[...])
        a = au_s[:, 0:D_FF]
        act = (_silu(a) * au_s[:, D_FF:2 * D_FF]).astype(BF16)
        act_ref[...] = act
        ff = _dot(act, wd[...])
        x2 = x1v + gate2 * ff
        nf, rstd_f = _rms(x2)
        gfv = gf_ref[...]
        err = nf * gfv - t_ref[...]
        loss_ref[...] += 0.5 * jnp.sum(_rowsum(err * err), axis=-1, keepdims=True) * (1.0 / D_MODEL)
        dy = err * (1.0 / D_MODEL)
        dx2 = _rms_bwd(dy * gfv, nf, rstd_f)
        dffb = (gate2 * dx2).astype(BF16)
        dff_ref[...] = dffb
        dact = _dot_nt(dffb, wd[...])
        a = au_s[:, 0:D_FF]
        dau_ref[:, 0:D_FF] = (dact * au_s[:, D_FF:2 * D_FF] * _dsilu(a)).astype(BF16)
        dau_ref[:, D_FF:2 * D_FF] = (dact * _silu(a)).astype(BF16)
        dh = _dot(dau_ref[...], wgu[...])
        dn = dh * scale2
        sums_ref[0:1, :] += _rowsum(dh)
        sums_ref[1:2, :] += _rowsum(dh * n2)
        sums_ref[2:3, :] += _rowsum(dx2 * ff)
        sums_ref[3:4, :] += _rowsum(dn * xhat)
        sums_ref[4:5, :] += _rowsum(dy * nf)
        dx_ref[...] = dx2 + _rms_bwd(dn * g2, xhat, rstd)

    vec = _whole((1, D_MODEL))
    hbm = pl.BlockSpec(memory_space=pl.ANY)
    return pl.pallas_call(
        body, name="ffn", grid=(s // tm,),
        out_shape=[jax.ShapeDtypeStruct((s, D_MODEL), F32), jax.ShapeDtypeStruct((s, D_MODEL), BF16),
                   jax.ShapeDtypeStruct((s, D_FF), BF16), jax.ShapeDtypeStruct((s, 2 * D_FF), BF16),
                   jax.ShapeDtypeStruct((s, D_MODEL), BF16), jax.ShapeDtypeStruct((8, D_MODEL), F32),
                   jax.ShapeDtypeStruct((1, 128), F32)],
        in_specs=[_rows(tm, D_MODEL), _rows(tm, 512), _rows(tm, 512), _rows(tm, D_MODEL), _whole((1, 6 * D_MODEL)), vec, vec,
                  _whole((D_MODEL, D_MODEL)), hbm, hbm],
        out_specs=[_rows(tm, D_MODEL), _rows(tm, D_MODEL), _rows(tm, D_FF), _rows(tm, 2 * D_FF), _rows(tm, D_MODEL),
                   _whole((8, D_MODEL)), _whole((1, 128))],
        scratch_shapes=[pltpu.VMEM((2 * D_FF, D_MODEL), BF16), pltpu.VMEM((D_FF, D_MODEL), BF16),
                        pltpu.VMEM((tm, 2 * D_FF), F32), pltpu.SemaphoreType.DMA((2,))],
        compiler_params=_params("arbitrary"),
    )(x, hg, at, target, mod, g2, gf, w_out, w_gu, w_down)


def _weight_grad(a, b, name, rounded=False):
    s, m = a.shape
    n = b.shape[1]
    ts = min(s, 2048)
    n_steps = s // ts
    tm = max(t for t in range(128, m + 1, 128) if m % t == 0 and t * n * 4 <= 6 * 1024 * 1024)

    def body(a_ref, b_ref, o_ref, *ob_ref):
        @pl.when(pl.program_id(1) == 0)
        def _():
            o_ref[...] = jnp.zeros_like(o_ref)

        o_ref[...] += _dot_tn(a_ref[...], b_ref[...])
        if rounded:
            @pl.when(pl.program_id(1) == n_steps - 1)
            def _():
                ob_ref[0][...] = o_ref[...].astype(BF16)

    tile = pl.BlockSpec((tm, n), lambda j, i: (j, 0))
    return pl.pallas_call(
        body, name=name, grid=(m // tm, n_steps),
        out_shape=[jax.ShapeDtypeStruct((m, n), F32)] + [jax.ShapeDtypeStruct((m, n), BF16)] * rounded,
        in_specs=[pl.BlockSpec((ts, tm), lambda j, i: (i, j)), pl.BlockSpec((ts, n), lambda j, i: (i, 0))],
        out_specs=[tile] + [tile] * rounded,
        compiler_params=_params("parallel", "arbitrary"),
    )(a, b)


def _adamw_math(w, g, m, v):
    m = ADAM_B1 * m + (1.0 - ADAM_B1) * g
    v = ADAM_B2 * v + (1.0 - ADAM_B2) * (g * g)
    m_hat = m / (1.0 - ADAM_B1 ** ADAM_STEP)
    v_hat = v / (1.0 - ADAM_B2 ** ADAM_STEP)
    delta = -ADAM_LR * (m_hat / (jnp.sqrt(v_hat) + ADAM_EPS) + ADAM_WD * w)
    return delta, m, v


def _adamw_shard(where, w, m, v, partial, got, name, after=()):
    r, c = w.shape
    tr = _shard_rows(r)
    lead = partial.ndim - 2
    n_got = got.shape[0]

    def body(where_ref, *refs):
        w_ref, m_ref, v_ref, own_ref, *rest = refs[len(after):]
        got_refs, (grad_ref, d_ref, nm_ref, nv_ref) = rest[:n_got], rest[n_got:]
        g = own_ref[...]
        for g_ref in got_refs:
            g = g + g_ref[...].astype(F32)
        grad_ref[...] = g
        d_ref[...], nm_ref[...], nv_ref[...] = _adamw_math(w_ref[...], g, m_ref[...], v_ref[...])

    tile = pl.BlockSpec((tr, c), lambda i, where_ref: (i, 0))
    own = pl.BlockSpec((None,) * lead + (tr, c), lambda i, where_ref: (*[where_ref[d] for d in range(lead)], i, 0))
    part = [pl.BlockSpec((None, tr, c), functools.partial(lambda j, i, where_ref: (j, i, 0), j)) for j in range(n_got)]
    return pl.pallas_call(
        body, name=name,
        grid_spec=pltpu.PrefetchScalarGridSpec(num_scalar_prefetch=1, grid=(r // tr,),
                                               in_specs=_ANY * len(after) + [tile] * 3 + [own] + part, out_specs=[tile] * 4),
        out_shape=[jax.ShapeDtypeStruct((r, c), F32)] * 4, compiler_params=_params("parallel"),
    )(where, *after, w, m, v, partial, *[got] * n_got)


def _small_update(small_all, dmod_blocks, c_all, logits, w_ada, m_ada, v_ada, smalls, after=()):
    def body(sm_ref, dm_ref, c_ref, lg_ref, wa_ref, ma_ref, va_ref, *rest):
        ins, outs = rest[:21], rest[21:]
        _, me = _flip(0)
        tot = sm_ref[0:1, :]
        for i in range(1, N_DEV):
            tot = tot + sm_ref[i:i + 1, :]
        loss_ref = outs[0]
        loss_ref[...] = tot[:, SM_LOSS:SM_LOSS + 128]
        g_ada = lax.dot_general(_silu(c_ref[...]), dm_ref[me], (((0,), (0,)), ((), ())),
                                preferred_element_type=F32, precision=HIGHEST)
        outs[1][...] = g_ada
        outs[2][...], outs[3][...], outs[4][...] = _adamw_math(wa_ref[...], g_ada, ma_ref[...], va_ref[...])
        p0 = _lower_bound(lg_ref)
        dl0 = tot[:, SM_LB:SM_LB + 512] * p0 * (1.0 - p0)
        grads = [tot[:, SM_MOD:SM_MOD + 6 * D_MODEL], tot[:, SM_G1:SM_G1 + D_MODEL], tot[:, SM_G2:SM_G2 + D_MODEL],
                 tot[:, SM_GF:SM_GF + D_MODEL], tot[:, SM_ATT:SM_ATT + 512], tot[:, SM_HG:SM_HG + 128],
                 jnp.where(lax.broadcasted_iota(jnp.int32, (2, 512), 0) == 0, dl0, -dl0)]
        for i, g in enumerate(grads):
            w_ref, m_ref, v_ref = ins[3 * i:3 * i + 3]
            o = outs[5 + 4 * i:9 + 4 * i]
            o[0][...] = g
            o[1][...], o[2][...], o[3][...] = _adamw_math(w_ref[...], g, m_ref[...], v_ref[...])

    flat = [t for trio in smalls for t in trio]
    vm = pl.BlockSpec(memory_space=pltpu.VMEM)
    out_shape = [jax.ShapeDtypeStruct((1, 128), F32)] + [jax.ShapeDtypeStruct(w_ada.shape, F32)] * 4
    for trio in smalls:
        out_shape += [jax.ShapeDtypeStruct(trio[0].shape, F32)] * 4
    return pl.pallas_call(
        _behind(body, after), name="small_update", out_shape=out_shape,
        in_specs=_ANY * len(after) + [vm] * (7 + len(flat)), out_specs=[vm] * len(out_shape),
        compiler_params=pltpu.CompilerParams(vmem_limit_bytes=V7X_VMEM_LIMIT),
    )(*after, small_all, dmod_blocks, c_all, logits, w_ada, m_ada, v_ada, *flat)


def kernel(x, c, w_ada, b_ada, norm1_g, w_in, hg_lb_logits, hg_onorm_g, att_onorm_g, w_out, norm2_g, w_gate_up, w_down, final_g, loss_target, m_w_ada, m_b_ada, m_norm1_g, m_w_in, m_hg_lb_logits, m_hg_onorm_g, m_att_onorm_g, m_w_out, m_norm2_g, m_w_gate_up, m_w_down, m_final_g, v_w_ada, v_b_ada, v_norm1_g, v_w_in, v_hg_lb_logits, v_hg_onorm_g, v_att_onorm_g, v_w_out, v_norm2_g, v_w_gate_up, v_w_down, v_final_g):
    x2d, target = x[0], loss_target[0]
    seq = x2d.shape[0]
    assert seq % (ATT_BLOCK * max(DILATIONS)) == 0 and seq % HG_TILE == 0
    gf = final_g.reshape(1, D_MODEL)

    c_all = _exchange_small(c.reshape(8, D_MODEL // 8), None, "gather_c").reshape(N_DEV, D_MODEL)
    ada = _ada_rows(c_all, w_ada[0], b_ada)
    mod = _exchange_small(ada, 1, "scatter_mod").reshape(1, 6 * D_MODEL)

    core = lax.axis_index("c").astype(jnp.int32).reshape(1)
    chip = (2 * lax.axis_index("x") + lax.axis_index("y")).astype(jnp.int32).reshape(1)
    me = 4 * lax.axis_index("x") + 2 * lax.axis_index("y") + lax.axis_index("c")

    g_in, = _gather_weights([w_in[0].T.astype(BF16)])
    w_in_b = g_in.reshape(IN_WIDTH, D_MODEL)
    rest_shards = [w_out[0].astype(BF16), w_gate_up[0].T.astype(BF16), w_down[0].astype(BF16)]
    lands = [lax.empty((N_DEV,) + s.shape, BF16) for s in rest_shards]
    g_send, g_recv, g_srcs, g_lands, tok = _copies_start("gather_rest_start", _plan_gather_own, 12, rest_shards, lands, [w_in_b, mod])
    flight = {}

    def stage(name, *vals):
        if name == "attention_begun":
            flight["shards"], got = _copies_wait("gather_rest_wait", _plan_gather_own, g_send, g_recv, g_srcs, g_lands, list(vals))
            flight["pass"] = _copies_start("gather_pass_start", _plan_gather_pass, 9, [], got, [])
            return [flight["pass"][4]]
        if name == "mixer_weights_done":
            shapes = [(4, 2, D_MODEL // N_DEV, D_MODEL), (4, 2, 2 * D_FF // N_DEV, D_MODEL), (4, 2, D_FF // N_DEV, D_MODEL)]
            flight["grads"] = [g32.reshape(sh) for (g32, _), sh in zip(vals, shapes)]
            rounded = [g16.reshape(sh) for (_, g16), sh in zip(vals, shapes)]
            direct_lands = [lax.empty((N_DEV - 1,) + sh[2:], BF16) for sh in shapes]
            flight["direct"] = _copies_start("reduce_rest_start", _plan_reduce_direct, 21, rounded, direct_lands, [])
            return [flight["direct"][4]]
        raise ValueError(name)

    def rest_weights(after):
        s, r, _, p_lands, _ = flight["pass"]
        _, got = _copies_wait("gather_pass_wait", _plan_gather_pass, s, r, [], p_lands, [after])
        full = [lax.dynamic_update_index_in_dim(g, shard, me, 0) for g, shard in zip(got, flight["shards"])]
        return full[0].reshape(D_MODEL, D_MODEL), full[1].reshape(2 * D_FF, D_MODEL), full[2].reshape(D_FF, D_MODEL)

    grad_x, dw_in, small = _block_step(x2d, target, mod, norm1_g, hg_lb_logits, hg_onorm_g, att_onorm_g, norm2_g, gf,
                                       w_in_b, rest_weights, stage, [tok])

    g_in8 = dw_in.reshape(4, 2, IN_WIDTH // N_DEV, D_MODEL)
    in_pairs = _copies_start("reduce_pairs_in_start", _plan_reduce_pairs, 4, [g_in8], [lax.empty((4,) + g_in8.shape[2:], F32)], [])
    s, r, srcs, d_lands, _ = flight["direct"]
    _, recv_rest = _copies_wait("reduce_rest_wait", _plan_reduce_direct, s, r, srcs, d_lands, [in_pairs[4]])
    small_rows = jnp.pad(small, ((0, 0), (0, SM_PADDED - SM_WIDTH))).reshape(SM_PADDED // 128, 128)
    small_all = _exchange_small(small_rows, None, "gather_small", [in_pairs[4]]).reshape(N_DEV, SM_PADDED)[:, :SM_WIDTH]
    in_grads, got_in = _copies_wait("reduce_pairs_in_wait", _plan_reduce_pairs, in_pairs[0], in_pairs[1], in_pairs[2], in_pairs[3],
                                    [small_all])
    in_s32, in_s16 = _pair_sum(core, in_grads[0], got_in[0], "pair_sum_in")
    in_chips = _copies_start("reduce_chips_in_start", _plan_reduce_chips, 3, [in_s16], [lax.empty((3,) + in_s16.shape[1:], BF16)], [])
    big, updated = {}, []
    rest_params = [("w_out", w_out, m_w_out, v_w_out), ("w_gate_up", w_gate_up, m_w_gate_up, v_w_gate_up), ("w_down", w_down, m_w_down, v_w_down)]
    mine = jnp.concatenate([chip, core])
    for (n, w, m, v), g32, got in zip(rest_params, flight["grads"], recv_rest):
        if n == "w_gate_up":
            outs4 = _adamw_shard(mine, w[0].T, m[0].T, v[0].T, g32, got, f"adamw_{n}", [in_chips[4]])
            big[n] = [t.T[None] for t in outs4]
        else:
            outs4 = _adamw_shard(mine, w[0], m[0], v[0], g32, got, f"adamw_{n}", [in_chips[4]])
            big[n] = [t[None] for t in outs4]
        updated.append(outs4[3])
    smalls = [(b_ada, m_b_ada, v_b_ada), (norm1_g, m_norm1_g, v_norm1_g), (norm2_g, m_norm2_g, v_norm2_g),
              (gf, m_final_g.reshape(1, D_MODEL), v_final_g.reshape(1, D_MODEL)),
              (att_onorm_g, m_att_onorm_g, v_att_onorm_g), (hg_onorm_g, m_hg_onorm_g, v_hg_onorm_g),
              (hg_lb_logits, m_hg_lb_logits, v_hg_lb_logits)]
    dmod_blocks = small_all[:, :6 * D_MODEL].reshape(N_DEV, N_DEV, 6 * D_MODEL // N_DEV).transpose(1, 0, 2)
    res = _small_update(small_all, dmod_blocks, c_all, hg_lb_logits, w_ada[0], m_w_ada[0], v_w_ada[0], smalls, [in_chips[4]])
    _, recv_in = _copies_wait("reduce_chips_in_wait", _plan_reduce_chips, in_chips[0], in_chips[1], in_chips[2], in_chips[3],
                              [res[0]] + updated)
    big["w_in"] = [t.T[None] for t in _adamw_shard(chip, w_in[0].T, m_w_in[0].T, v_w_in[0].T, in_s32, recv_in[0], "adamw_w_in")]
    loss = res[0][0, 0]
    ada4 = [t[None] for t in res[1:5]]
    sm4 = {n: list(res[5 + 4 * i:9 + 4 * i]) for i, n in enumerate(["b_ada", "norm1_g", "norm2_g", "final_g", "att", "hg", "lb"])}
    sm4["final_g"] = [t.reshape(D_MODEL) for t in sm4["final_g"]]

    order = [ada4, sm4["b_ada"], sm4["norm1_g"], big["w_in"], sm4["lb"], sm4["hg"], sm4["att"], big["w_out"], sm4["norm2_g"],
             big["w_gate_up"], big["w_down"], sm4["final_g"]]
    return (loss, grad_x[None], *[o[0] for o in order], *[o[1] for o in order], *[o[2] for o in order], *[o[3] for o in order])


def _block_step(x2d, target, mod, norm1_g, hg_lb_logits, hg_onorm_g, att_onorm_g, norm2_g, gf, w_in_b, rest_weights, stage,
                after=()):
    h1, hq, hf, hi, hgt, aq, ak, av, hg_out, hg_o, hg_states = _in_fwd(x2d, mod, norm1_g, w_in_b, hg_lb_logits, hg_onorm_g, after)
    branch = [_att_fwd(aq, ak, av, d) for d in DILATIONS[:2]]
    behind = stage("attention_begun", branch[0][0], branch[1][0])
    branch += [_att_fwd(aq, ak, av, d) for d in DILATIONS[2:]]
    outs = [b[0] for b in branch]
    lses = [b[1] for b in branch]
    att, att_out = _att_combine(outs, lses, att_onorm_g, behind)
    w_out_b, w_gu_b, w_down_b = rest_weights(att_out)

    dx1, h2, act, dau, dff, ffn_sums, loss_part = _ffn(x2d, hg_out, att_out, target, mod, norm2_g, gf, w_out_b, w_gu_b, w_down_b)
    dw_gu = _weight_grad(dau, h2, "dw_gate_up", rounded=True)
    dw_down = _weight_grad(act, dff, "dw_down", rounded=True)

    back = _out_bwd(dx1, hg_out, att_out, mod, w_out_b, att, lses, att_onorm_g)
    dhg, dos, ccs = back[0], back[1:4], back[4:7]
    dw_out, dw_out_b, dgate1, d_att_g = back[7:11]
    behind = stage("mixer_weights_done", (dw_out, dw_out_b), dw_gu, dw_down)
    datt = []
    for i, d in enumerate(DILATIONS):
        datt.append(_att_bwd(aq, ak, av, dos[i], ccs[i], lses[i], d, behind))
    dhq, dhf, dhi, dhgt, d_hg_g, d_lb = _hg_bwd(hq, hf, hi, hgt, hg_lb_logits, hg_onorm_g, hg_o, hg_states, dhg)
    dps = [dhq, dhf, dhi, dhgt] + [datt[i][j] for j in range(3) for i in range(3)]
    grad_x, dp_b, dshift1, dscale1, d_g1 = _in_bwd(x2d, dx1, mod, norm1_g, w_in_b, dps)
    dw_in, = _weight_grad(dp_b, h1, "dw_in")
    small = jnp.concatenate([dshift1, dscale1, dgate1, ffn_sums[0:1], ffn_sums[1:2], ffn_sums[2:3], d_g1, ffn_sums[3:4],
                             ffn_sums[4:5], d_att_g, d_lb, d_hg_g, loss_part], axis=1)
    return grad_x, dw_in, small
```

```python
import functools

import jax
import jax.numpy as jnp
from jax import lax
from jax.experimental import pallas as pl
from jax.experimental.pallas import tpu as pltpu

F32 = jnp.float32
BF16 = jnp.bfloat16
HIGHEST = lax.Precision.HIGHEST
MESH = pl.DeviceIdType.MESH

D_MODEL = 1024
N_DEV = 8
HG_HEADS = 4
HG_DIM = 128
HG_WIDTH = HG_HEADS * HG_DIM
HG_CHUNK = 128
ATT_WIDTH = 512
ATT_HEAD_DIM = 64
ATT_BLOCK = 128
DILATIONS = (1, 4, 16)
ATT_SCALE = ATT_HEAD_DIM ** -0.5
D_FF = 2816
IN_WIDTH = 7 * 512
RMS_EPS = 1e-6
NEG = -1e30

ADAM_LR = 0.001
ADAM_B1 = 0.9
ADAM_B2 = 0.999
ADAM_EPS = 1e-08
ADAM_WD = 0.01
ADAM_STEP = 10

V7X_VMEM_LIMIT = 56 * 1024 * 1024
V7X_VMEM_MOST = 60 * 1024 * 1024

SM_MOD = 0
SM_G1 = 6 * D_MODEL
SM_G2 = 7 * D_MODEL
SM_GF = 8 * D_MODEL
SM_ATT = 9 * D_MODEL
SM_LB = 9 * D_MODEL + 512
SM_HG = 10 * D_MODEL
SM_LOSS = 10 * D_MODEL + 128
SM_WIDTH = 10 * D_MODEL + 256
SM_PADDED = 88 * 128


def _params(*sem, vmem=V7X_VMEM_LIMIT):
    return pltpu.CompilerParams(dimension_semantics=sem, vmem_limit_bytes=vmem)


def _dot(a, b):
    return jnp.dot(a, b, preferred_element_type=F32)


def _dot_nt(a, b):
    return lax.dot_general(a, b, (((1,), (1,)), ((), ())), preferred_element_type=F32)


def _dot_tn(a, b):
    return lax.dot_general(a, b, (((0,), (0,)), ((), ())), preferred_element_type=F32)


def _dot_f32(a, b):
    return jnp.dot(a, b, preferred_element_type=F32, precision=HIGHEST)


def _sigmoid(x):
    return 1.0 / (1.0 + jnp.exp(-x))


def _silu(x):
    return x * _sigmoid(x)


def _dsilu(x):
    s = _sigmoid(x)
    return s * (1.0 + x * (1.0 - s))


def _rms(x):
    rstd = lax.rsqrt(jnp.mean(x * x, axis=-1, keepdims=True) + RMS_EPS)
    return x * rstd, rstd


def _rms_bwd(dn, xhat, rstd):
    return rstd * (dn - xhat * jnp.mean(dn * xhat, axis=-1, keepdims=True))


def _rowsum(x):
    return jnp.sum(x, axis=0, keepdims=True)


def _rows(tm, n):
    return pl.BlockSpec((tm, n), lambda i: (i, 0))


def _whole(shape):
    return pl.BlockSpec(shape, lambda i: (0,) * len(shape))


def _mesh_pos():
    return lax.axis_index("x"), lax.axis_index("y"), lax.axis_index("c")


def _flip(k):
    x, y, c = _mesh_pos()
    px = 1 - x if k & 4 else x
    py = 1 - y if k & 2 else y
    pc = 1 - c if k & 1 else c
    return (px, py, pc), 4 * px + 2 * py + pc


_ANY = [pl.BlockSpec(memory_space=pl.ANY)]


def _behind(body, after):
    return lambda *refs: body(*refs[len(after):])


def _exchange_small(x, rows_per_peer, name, after=()):
    r_all, cols = x.shape
    r_out = r_all if rows_per_peer is None else rows_per_peer

    def body(x_ref, out_ref, send_sems, recv_sems):
        _, me = _flip(0)

        def src(pid):
            if rows_per_peer is None:
                return x_ref
            return x_ref.at[pl.ds(pl.multiple_of(pid * r_out, r_out), r_out), :]

        if rows_per_peer is None:
            out_ref[me] = x_ref[...]
        else:
            out_ref[me] = x_ref[pl.ds(pl.multiple_of(me * r_out, r_out), r_out), :]
        sends = []
        for k in range(1, N_DEV):
            dev, pid = _flip(k)
            cp = pltpu.make_async_remote_copy(src_ref=src(pid), dst_ref=out_ref.at[me], send_sem=send_sems.at[k - 1],
                                              recv_sem=recv_sems.at[k - 1], device_id=dev, device_id_type=MESH)
            cp.start()
            sends.append(cp)
        for k in range(1, N_DEV):
            dev, pid = _flip(k)
            pltpu.make_async_remote_copy(src_ref=src(pid), dst_ref=out_ref.at[pid], send_sem=send_sems.at[k - 1],
                                         recv_sem=recv_sems.at[k - 1], device_id=dev, device_id_type=MESH).wait_recv()
        for cp in sends:
            cp.wait_send()

    return pl.pallas_call(
        _behind(body, after), name=name,
        out_shape=jax.ShapeDtypeStruct((N_DEV, r_out, cols), x.dtype),
        in_specs=_ANY * len(after) + [pl.BlockSpec(memory_space=pltpu.VMEM)],
        out_specs=pl.BlockSpec(memory_space=pltpu.VMEM),
        scratch_shapes=[pltpu.SemaphoreType.DMA((N_DEV - 1,)), pltpu.SemaphoreType.DMA((N_DEV - 1,))],
    )(*after, x)


def _gather_weights(shards):
    n = len(shards)

    def body(*refs):
        xs, outs = refs[:n], refs[n:2 * n]
        send_sems, recv_sems, local_sems = refs[2 * n:]
        x, y, c = _mesh_pos()
        me, sibling = (x, y, c), (x, y, 1 - c)
        chips = [(1 - x, y), (x, 1 - y), (1 - x, 1 - y)]

        def blk(a, px, py, pc):
            return outs[a].at[4 * px + 2 * py + pc]

        def copy(a, k, block, to, src=None):
            return pltpu.make_async_remote_copy(
                src_ref=blk(a, *block) if src is None else src, dst_ref=blk(a, *block),
                send_sem=send_sems.at[a * 7 + k], recv_sem=recv_sems.at[a * 7 + k], device_id=to, device_id_type=MESH)

        mine = [pltpu.make_async_copy(xs[a], blk(a, *me), local_sems.at[a]) for a in range(n)]
        for cp in mine:
            cp.start()
        first = []
        for a in range(n):
            first.append(copy(a, 0, me, sibling, src=xs[a]))
            first += [copy(a, 1 + j, me, (*chip, c), src=xs[a]) for j, chip in enumerate(chips)]
        for cp in first:
            cp.start()
        passed = []
        for j, chip in enumerate(chips):
            for a in range(n):
                copy(a, 1 + j, (*chip, c), me).wait_recv()
                cp = copy(a, 4 + j, (*chip, c), sibling)
                cp.start()
                passed.append(cp)
        for a in range(n):
            copy(a, 0, sibling, me).wait_recv()
            for j, chip in enumerate(chips):
                copy(a, 4 + j, (*chip, 1 - c), me).wait_recv()
        for cp in first + passed:
            cp.wait_send()
        for cp in mine:
            cp.wait()

    hbm = pl.BlockSpec(memory_space=pl.ANY)
    return pl.pallas_call(
        body, name="gather_weights",
        out_shape=[jax.ShapeDtypeStruct((N_DEV,) + s.shape, s.dtype) for s in shards],
        in_specs=[hbm] * n, out_specs=[hbm] * n,
        scratch_shapes=[pltpu.SemaphoreType.DMA((7 * n,)), pltpu.SemaphoreType.DMA((7 * n,)), pltpu.SemaphoreType.DMA((n,))],
    )(*shards)


_HBM = pl.BlockSpec(memory_space=pltpu.HBM)
_SEM = pl.BlockSpec(memory_space=pltpu.SEMAPHORE)
_DATAFLOW = pltpu.SideEffectType.DATAFLOW_SIDE_EFFECTING


def _copies_start(name, plan, n_copies, srcs, lands, after):
    bufs = list(srcs) + list(lands)
    nb = len(bufs)

    def body(*refs):
        ins, send_sems, recv_sems, token = refs[:nb], refs[nb + len(after)], refs[nb + len(after) + 1], refs[-1]
        for i, (src, dst, dev) in enumerate(plan(ins[:len(srcs)], ins[len(srcs):])):
            pltpu.make_async_remote_copy(src_ref=src, dst_ref=dst, send_sem=send_sems.at[i], recv_sem=recv_sems.at[i],
                                         device_id=dev, device_id_type=MESH).start()
        token[...] = jnp.zeros_like(token)

    outs = pl.pallas_call(
        body, name=name,
        out_shape=(pltpu.SemaphoreType.DMA((n_copies,)), pltpu.SemaphoreType.DMA((n_copies,)),
                   *[pltpu.HBM(b.shape, b.dtype) for b in bufs], jax.ShapeDtypeStruct((8, 128), F32)),
        in_specs=[_HBM] * nb + [pl.BlockSpec(memory_space=pl.ANY)] * len(after),
        out_specs=(_SEM, _SEM, *[_HBM] * nb, pl.BlockSpec(memory_space=pltpu.VMEM)),
        input_output_aliases={i: 2 + i for i in range(nb)},
        compiler_params=pltpu.CompilerParams(has_side_effects=_DATAFLOW),
    )(*[pltpu.with_memory_space_constraint(b, pltpu.HBM) for b in bufs], *after)
    return outs[0], outs[1], list(outs[2:2 + len(srcs)]), list(outs[2 + len(srcs):2 + nb]), outs[-1]


def _copies_wait(name, plan, send_sems, recv_sems, srcs, lands, after):
    bufs = list(srcs) + list(lands)
    nb = len(bufs)

    def body(*refs):
        ins, send_ref, recv_ref = refs[:nb], refs[nb], refs[nb + 1]
        for i, (src, dst, dev) in enumerate(plan(ins[:len(srcs)], ins[len(srcs):])):
            cp = pltpu.make_async_remote_copy(src_ref=src, dst_ref=dst, send_sem=send_ref.at[i], recv_sem=recv_ref.at[i],
                                              device_id=dev, device_id_type=MESH)
            cp.wait_send()
            cp.wait_recv()

    outs = pl.pallas_call(
        body, name=name, out_shape=[pltpu.HBM(b.shape, b.dtype) for b in bufs],
        in_specs=[_HBM] * nb + [_SEM, _SEM] + [pl.BlockSpec(memory_space=pl.ANY)] * len(after), out_specs=[_HBM] * nb,
        input_output_aliases={i: i for i in range(nb)},
        compiler_params=pltpu.CompilerParams(has_side_effects=_DATAFLOW),
    )(*bufs, send_sems, recv_sems, *after)
    return list(outs[:len(srcs)]), list(outs[len(srcs):])


def _plan_gather_own(srcs, lands):
    _, me = _flip(0)
    return [(srcs[a], lands[a].at[me], _flip(k)[0]) for a in range(len(srcs)) for k in (1, 4, 2, 6)]


def _plan_gather_pass(srcs, lands):
    sibling = _flip(1)[0]
    plan = []
    for land in lands:
        for k in (4, 2, 6):
            block = land.at[_flip(k)[1]]
            plan.append((block, block, sibling))
    return plan


def _plan_reduce_pairs(srcs, lands):
    x, y, c = _mesh_pos()
    return [(srcs[a].at[chip, 1 - c], lands[a].at[chip], (x, y, 1 - c)) for a in range(len(srcs)) for chip in range(4)]


def _plan_reduce_chips(srcs, lands):
    plan = []
    for a in range(len(srcs)):
        for j, k in enumerate((4, 2, 6)):
            dev = _flip(k)[0]
            plan.append((srcs[a].at[2 * dev[0] + dev[1]], lands[a].at[j], dev))
    return plan


def _plan_reduce_direct(srcs, lands):
    plan = []
    for a in range(len(srcs)):
        for k in range(1, N_DEV):
            dev = _flip(k)[0]
            plan.append((srcs[a].at[2 * dev[0] + dev[1], dev[2]], lands[a].at[k - 1], dev))
    return plan


def _shard_rows(r):
    return r // 2 if r % 32 == 0 else r


def _pair_sum(core, grads, got, name):
    _, _, r, c = grads.shape
    tr = _shard_rows(r)

    def body(core_ref, a_ref, b_ref, o_ref, ob_ref):
        s = a_ref[...] + b_ref[...]
        o_ref[...] = s
        ob_ref[...] = s.astype(BF16)

    spec = pl.BlockSpec((None, tr, c), lambda i, j, core_ref: (i, j, 0))
    return pl.pallas_call(
        body, name=name,
        grid_spec=pltpu.PrefetchScalarGridSpec(
            num_scalar_prefetch=1, grid=(4, r // tr),
            in_specs=[pl.BlockSpec((None, None, tr, c), lambda i, j, core_ref: (i, core_ref[0], j, 0)), spec],
            out_specs=[spec, spec]),
        out_shape=[jax.ShapeDtypeStruct((4, r, c), F32), jax.ShapeDtypeStruct((4, r, c), BF16)],
        compiler_params=_params("parallel", "parallel"),
    )(core, grads, got)


def _ada_rows(c_all, w_ada, b_ada):
    n_cols = w_ada.shape[1]

    def body(c_ref, w_ref, b_ref, o_ref):
        _, me = _flip(0)
        bias = b_ref[:, pl.ds(pl.multiple_of(me * n_cols, 128), n_cols)]
        o_ref[...] = _dot_f32(_silu(c_ref[...]), w_ref[...]) + bias

    return pl.pallas_call(
        body, name="ada_rows", out_shape=jax.ShapeDtypeStruct((N_DEV, n_cols), F32),
        in_specs=[pl.BlockSpec(memory_space=pltpu.VMEM)] * 3, out_specs=pl.BlockSpec(memory_space=pltpu.VMEM),
    )(c_all, w_ada, b_ada)


def _in_fwd(x, mod, g1, w_in, logits, onorm_g, after=()):
    s = x.shape[0]
    tm = HG_TILE

    def body(x_ref, mod_ref, g_ref, w_ref, lg_ref, og_ref, h_ref, *rest):
        groups, (out_ref, o_ref, st_ref, state, qf_s, kk_s, lf_s) = rest[:7], rest[7:]
        xhat, _ = _rms(x_ref[...])
        h = (xhat * g_ref[...]) * (1.0 + mod_ref[:, D_MODEL:2 * D_MODEL]) + mod_ref[:, 0:D_MODEL]
        hb = h.astype(BF16)
        h_ref[...] = hb
        for j, o_ref_j in enumerate(groups):
            o_ref_j[...] = _dot_nt(hb, w_ref[j * 512:(j + 1) * 512, :])
        _hg_fwd_tile(*groups[:4], lg_ref, og_ref, out_ref, o_ref, st_ref, state, qf_s, kk_s, lf_s)

    tile = _rows(tm, 512)
    return pl.pallas_call(
        _behind(body, after), name="in_fwd", grid=(s // tm,),
        out_shape=[jax.ShapeDtypeStruct((s, D_MODEL), BF16)] + [jax.ShapeDtypeStruct((s, 512), F32)] * 7
        + [jax.ShapeDtypeStruct((s, HG_WIDTH), BF16), jax.ShapeDtypeStruct((s, HG_WIDTH), F32),
           jax.ShapeDtypeStruct((s // HG_CHUNK * HG_DIM, HG_WIDTH), F32)],
        in_specs=_ANY * len(after)
        + [_rows(tm, D_MODEL), _whole((1, 6 * D_MODEL)), _whole((1, D_MODEL)), _whole((IN_WIDTH, D_MODEL)),
           _whole((2, HG_WIDTH)), _whole((1, HG_DIM))],
        out_specs=[_rows(tm, D_MODEL)] + [tile] * 7 + [tile, tile, _rows(HG_TILE_CHUNKS * HG_DIM, HG_WIDTH)],
        scratch_shapes=[pltpu.VMEM((HG_HEADS, HG_DIM, HG_DIM), F32)] + [pltpu.VMEM((HG_TILE, HG_WIDTH), F32)] * 3,
        compiler_params=_params("arbitrary"),
    )(*after, x, mod, g1, w_in, logits, onorm_g)


def _in_bwd(x, dx1, mod, g1, w_in, dps):
    s = x.shape[0]
    tm = 256

    def body(x_ref, dx_ref, mod_ref, g_ref, w_ref, *rest):
        dp_refs, (gx_ref, dpb_ref, dsh_ref, dsc_ref, dg_ref) = rest[:13], rest[13:]
        pieces = [dp_refs[j][...] for j in range(4)]
        pieces += [dp_refs[4 + 3 * j][...] + dp_refs[5 + 3 * j][...] + dp_refs[6 + 3 * j][...] for j in range(3)]
        for j, p in enumerate(pieces):
            dpb_ref[:, j * 512:(j + 1) * 512] = p.astype(BF16)
        dh = _dot(dpb_ref[...], w_ref[...])
        xhat, rstd = _rms(x_ref[...])
        g = g_ref[...]
        scale1 = 1.0 + mod_ref[:, D_MODEL:2 * D_MODEL]
        n1 = xhat * g

        @pl.when(pl.program_id(0) == 0)
        def _():
            dsh_ref[...] = jnp.zeros_like(dsh_ref)
            dsc_ref[...] = jnp.zeros_like(dsc_ref)
            dg_ref[...] = jnp.zeros_like(dg_ref)

        dsh_ref[...] += _rowsum(dh)
        dsc_ref[...] += _rowsum(dh * n1)
        dn = dh * scale1
        dg_ref[...] += _rowsum(dn * xhat)
        gx_ref[...] = dx_ref[...] + _rms_bwd(dn * g, xhat, rstd)

    vec = _whole((1, D_MODEL))
    return pl.pallas_call(
        body, name="in_bwd", grid=(s // tm,),
        out_shape=[jax.ShapeDtypeStruct((s, D_MODEL), F32), jax.ShapeDtypeStruct((s, IN_WIDTH), BF16)]
        + [jax.ShapeDtypeStruct((1, D_MODEL), F32)] * 3,
        in_specs=[_rows(tm, D_MODEL), _rows(tm, D_MODEL), _whole((1, 6 * D_MODEL)), vec, _whole((IN_WIDTH, D_MODEL))]
        + [_rows(tm, 512)] * 13,
        out_specs=[_rows(tm, D_MODEL), _rows(tm, IN_WIDTH), vec, vec, vec],
        compiler_params=_params("arbitrary"),
    )(x, dx1, mod, g1, w_in, *dps)


HG_TILE = 512
HG_TILE_CHUNKS = HG_TILE // HG_CHUNK


def _lower_bound(lg_ref):
    return 1.0 / (1.0 + jnp.exp(lg_ref[1:2, :] - lg_ref[0:1, :]))


def _chunk_masks():
    r = lax.broadcasted_iota(jnp.int32, (HG_CHUNK, HG_CHUNK), 0)
    c = lax.broadcasted_iota(jnp.int32, (HG_CHUNK, HG_CHUNK), 1)
    return r >= c, c >= r, (r >= c).astype(F32), (c >= r).astype(F32)


def _hg_fwd_tile(q_ref, f_ref, i_ref, g_ref, lg_ref, og_ref, out_ref, o_ref, st_ref, state, qf_s, kk_s, lf_s):
    @pl.when(pl.program_id(0) == 0)
    def _():
        state[...] = jnp.zeros_like(state)

    lb = _lower_bound(lg_ref)
    f = lb + (1.0 - lb) * _sigmoid(f_ref[...])
    kk_s[...] = 1.0 - f
    lf_s[...] = jnp.log(f)
    qf_s[...] = _silu(q_ref[...])
    causal, _, tri, _ = _chunk_masks()

    def chunk(ci, carry):
        rows = pl.ds(pl.multiple_of(ci * HG_CHUNK, HG_CHUNK), HG_CHUNK)
        srows = pl.ds(pl.multiple_of(ci * HG_DIM, HG_DIM), HG_DIM)
        lf = lf_s[rows, :]
        b = _dot_f32(tri, lf)
        bl = _rowsum(lf)
        ref = 0.5 * bl
        qf, kk, v = qf_s[rows, :], kk_s[rows, :], i_ref[rows, :]
        a_in = (qf * jnp.exp(b)).astype(BF16)
        a_t = (qf * jnp.exp(b - ref)).astype(BF16)
        b_t = (kk * jnp.exp(ref - b)).astype(BF16)
        kd = kk * jnp.exp(bl - b)
        ebl = jnp.exp(bl)
        vb = v.astype(BF16)
        for h in range(HG_HEADS):
            c = slice(h * HG_DIM, (h + 1) * HG_DIM)
            st = state[h]
            st_ref[srows, c] = st
            p = jnp.where(causal, _dot_nt(a_t[:, c], b_t[:, c]), 0.0)
            o_ref[rows, c] = _dot(p.astype(BF16), vb[:, c]) + _dot_nt(a_in[:, c], st.astype(BF16))
            state[h] = st * ebl[:, c] + _dot_tn(vb[:, c], kd[:, c].astype(BF16))
        return carry

    lax.fori_loop(0, HG_TILE_CHUNKS, chunk, 0, unroll=True)
    for h in range(HG_HEADS):
        c = slice(h * HG_DIM, (h + 1) * HG_DIM)
        ohat, _ = _rms(o_ref[:, c])
        out_ref[:, c] = (ohat * og_ref[...] * _silu(g_ref[:, c])).astype(BF16)


def _hg_bwd(hq, hf, hi, hgt, logits, onorm_g, o, states, dout):
    s = hq.shape[0]
    n_tiles = s // HG_TILE

    def body(q_ref, f_ref, i_ref, g_ref, lg_ref, og_ref, o_ref, st_ref, d_ref,
             dq_ref, df_ref, di_ref, dg_ref, dog_ref, dlb_ref, dstate, qf_s, kk_s, lf_s, do_s):
        @pl.when(pl.program_id(0) == 0)
        def _():
            dstate[...] = jnp.zeros_like(dstate)
            dog_ref[...] = jnp.zeros_like(dog_ref)
            dlb_ref[...] = jnp.zeros_like(dlb_ref)

        og = og_ref[...]
        dog = jnp.zeros((1, HG_DIM), F32)
        for h in range(HG_HEADS):
            c = slice(h * HG_DIM, (h + 1) * HG_DIM)
            ohat, rstd = _rms(o_ref[:, c])
            gate = g_ref[:, c]
            d = d_ref[:, c]
            dg_ref[:, c] = (d * (ohat * og) * _dsilu(gate)).astype(BF16)
            dnormed = d * _silu(gate)
            dog += _rowsum(dnormed * ohat)
            do_s[:, c] = _rms_bwd(dnormed * og, ohat, rstd)
        dog_ref[...] += dog

        lb = _lower_bound(lg_ref)
        f = lb + (1.0 - lb) * _sigmoid(f_ref[...])
        kk_s[...] = 1.0 - f
        lf_s[...] = jnp.log(f)
        qf_s[...] = _silu(q_ref[...])
        causal, upper, tri, tri_t = _chunk_masks()

        def chunk(step, carry):
            ci = HG_TILE_CHUNKS - 1 - step
            rows = pl.ds(pl.multiple_of(ci * HG_CHUNK, HG_CHUNK), HG_CHUNK)
            srows = pl.ds(pl.multiple_of(ci * HG_DIM, HG_DIM), HG_DIM)
            lf = lf_s[rows, :]
            b = _dot_f32(tri, lf)
            bl = _rowsum(lf)
            ref = 0.5 * bl
            qf, kk, v, do = qf_s[rows, :], kk_s[rows, :], i_ref[rows, :], do_s[rows, :]
            eb, ebr, erb, ekd, ebl = jnp.exp(b), jnp.exp(b - ref), jnp.exp(ref - b), jnp.exp(bl - b), jnp.exp(bl)
            a_in, a_t, b_t, kd = qf * eb, qf * ebr, kk * erb, kk * ekd
            for h in range(HG_HEADS):
                c = slice(h * HG_DIM, (h + 1) * HG_DIM)
                st, dst = st_ref[srows, c], dstate[h]
                stb, dstb = st.astype(BF16), dst.astype(BF16)
                doh, vh = do[:, c], v[:, c]
                dob, vb = doh.astype(BF16), vh.astype(BF16)
                ain_h, at_h, bt_h, kd_h = a_in[:, c], a_t[:, c], b_t[:, c], kd[:, c]
                atb, btb = at_h.astype(BF16), bt_h.astype(BF16)
                d_ain = _dot(dob, stb)
                p_t = jnp.where(upper, _dot_nt(btb, atb), 0.0).astype(BF16)
                dp = jnp.where(causal, _dot_nt(dob, vb), 0.0).astype(BF16)
                dp_t = jnp.where(upper, _dot_nt(vb, dob), 0.0).astype(BF16)
                di_ref[rows, c] = (_dot(p_t, dob) + _dot_nt(kd_h.astype(BF16), dstb)).astype(BF16)
                d_at = _dot(dp, btb)
                d_bt = _dot(dp_t, atb)
                d_kd = _dot(vb, dstb)
                dqf = d_ain * eb[:, c] + d_at * ebr[:, c]
                dkk = d_bt * erb[:, c] + d_kd * ekd[:, c]
                db = d_ain * ain_h + d_at * atb.astype(F32) - d_bt * btb.astype(F32) - d_kd * kd_h
                dbl = _rowsum(d_kd * kd_h) + _rowsum(dst * st) * ebl[:, c]
                dstate[h] = _dot_tn(dob, ain_h.astype(BF16)) + dst * ebl[:, c]
                dlf = _dot_f32(tri_t, db) + dbl
                qv, fr = q_ref[rows, c], f_ref[rows, c]
                lbh = lb[:, c]
                sg = _sigmoid(fr)
                dfv = dlf / (lbh + (1.0 - lbh) * sg) - dkk
                df_ref[rows, c] = (dfv * (1.0 - lbh) * sg * (1.0 - sg)).astype(BF16)
                dlb_ref[:, c] += _rowsum(dfv * (1.0 - sg))
                dq_ref[rows, c] = (dqf * _dsilu(qv)).astype(BF16)
            return carry

        lax.fori_loop(0, HG_TILE_CHUNKS, chunk, 0, unroll=True)

    rev = pl.BlockSpec((HG_TILE, HG_WIDTH), lambda i: (n_tiles - 1 - i, 0))
    return pl.pallas_call(
        body, name="hg_bwd", grid=(n_tiles,),
        out_shape=[jax.ShapeDtypeStruct((s, HG_WIDTH), BF16)] * 4
        + [jax.ShapeDtypeStruct((1, HG_DIM), F32), jax.ShapeDtypeStruct((1, HG_WIDTH), F32)],
        in_specs=[rev] * 4 + [_whole((2, HG_WIDTH)), _whole((1, HG_DIM)), rev,
                              pl.BlockSpec((HG_TILE_CHUNKS * HG_DIM, HG_WIDTH), lambda i: (n_tiles - 1 - i, 0)), rev],
        out_specs=[rev] * 4 + [_whole((1, HG_DIM)), _whole((1, HG_WIDTH))],
        scratch_shapes=[pltpu.VMEM((HG_HEADS, HG_DIM, HG_DIM), F32)] + [pltpu.VMEM((HG_TILE, HG_WIDTH), F32)] * 4,
        compiler_params=_params("arbitrary"),
    )(hq, hf, hi, hgt, logits, onorm_g, o, states, dout)


TOKEN_GROUP = 16


def _att_geometry(dil, seq=0):
    per_group = TOKEN_GROUP // dil
    ub = ATT_BLOCK // per_group
    if dil == TOKEN_GROUP:
        n_blocks = 2 if seq % (2 * ub * TOKEN_GROUP) == 0 and seq > 0 else 1
    else:
        n_blocks = 4
    return per_group, ub, ATT_WIDTH if dil == 1 else 128, n_blocks


def _att_consts(dil):
    per_group, ub = _att_geometry(dil)[:2]

    def pos(i):
        return i if dil == 1 else (i % ub) * per_group + i // ub

    lane = lax.broadcasted_iota(jnp.int32, (ATT_BLOCK, 128), 1)
    qi = pos(lax.broadcasted_iota(jnp.int32, (2 * ATT_BLOCK, ATT_BLOCK), 0) % ATT_BLOCK)
    kj = pos(lax.broadcasted_iota(jnp.int32, (2 * ATT_BLOCK, ATT_BLOCK), 1))
    return lane < ATT_HEAD_DIM, kj <= qi, lambda off: kj >= qi + off


def _load_tile(ref, dil, r, c, base=0):
    per_group, ub = _att_geometry(dil)[:2]
    if dil == 1:
        return ref[base:base + ATT_BLOCK, c]
    return jnp.concatenate([ref[pl.ds(base + dil * w + r, ub, stride=TOKEN_GROUP), c] for w in range(per_group)], axis=0)


def _store_tile(ref, dil, r, c, val, base=0):
    per_group, ub = _att_geometry(dil)[:2]
    if dil == 1:
        ref[base:base + ATT_BLOCK, c] = val
        return
    for w in range(per_group):
        ref[pl.ds(base + dil * w + r, ub, stride=TOKEN_GROUP), c] = val[w * ub:(w + 1) * ub]


def _stack_heads(x2, first):
    return jnp.concatenate([jnp.where(first, x2, 0.0), jnp.where(first, 0.0, x2)], axis=0)


def _stack_bcast(x2, first):
    other = pltpu.roll(x2, ATT_HEAD_DIM, axis=1)
    return jnp.concatenate([jnp.where(first, x2, other), jnp.where(first, other, x2)], axis=0)


def _unstack_heads(st, first):
    return jnp.where(first, st[:ATT_BLOCK], st[ATT_BLOCK:])


def _att_fwd(q, k, v, dil, behind=()):
    seq, width = q.shape
    _, ub, lanes, nbs = _att_geometry(dil, seq)
    rows = ub * TOKEN_GROUP
    n_steps = seq // (nbs * rows)

    def body(q_ref, k_ref, v_ref, kp_ref, vp_ref, o_ref, lse_ref):
        first, cur_ok, _band = _att_consts(dil)
        inner_ok = _band(0)
        edge_ok = _band(jnp.where(pl.program_id(0) > 0, 0, ATT_BLOCK))
        for r in range(dil):
            for j in range(lanes // 128):
                c = slice(j * 128, (j + 1) * 128)
                kc = vc = None
                for b in range(nbs):
                    base = b * rows
                    prev_ok = edge_ok if b == 0 else inner_ok
                    if b == 0:
                        kp, vp = _load_tile(kp_ref, dil, r, c).astype(BF16), _load_tile(vp_ref, dil, r, c).astype(BF16)
                    else:
                        kp, vp = kc, vc
                    qst = _stack_heads(_load_tile(q_ref, dil, r, c, base) * ATT_SCALE, first).astype(BF16)
                    kc = _load_tile(k_ref, dil, r, c, base).astype(BF16)
                    vc = _load_tile(v_ref, dil, r, c, base).astype(BF16)
                    sc = jnp.where(cur_ok, _dot_nt(qst, kc), NEG)
                    sp = jnp.where(prev_ok, _dot_nt(qst, kp), NEG)
                    mx = jnp.max(jnp.maximum(sc, sp), axis=-1, keepdims=True)
                    pc, pp = jnp.exp(sc - mx), jnp.exp(sp - mx)
                    den = jnp.sum(pc + pp, axis=-1, keepdims=True)
                    ost = (_dot(pc.astype(BF16), vc) + _dot(pp.astype(BF16), vp)) / den
                    lse = jnp.broadcast_to(mx + jnp.log(den), (2 * ATT_BLOCK, 128))
                    _store_tile(o_ref, dil, r, c, _unstack_heads(ost, first), base)
                    _store_tile(lse_ref, dil, r, c, _unstack_heads(lse, first), base)

    slab = pl.BlockSpec((nbs * rows, lanes), lambda n, j: (n, j))
    before = pl.BlockSpec((rows, lanes), lambda n, j: (jnp.maximum(n * nbs - 1, 0), j))
    return pl.pallas_call(
        _behind(body, behind), name=f"att_fwd_d{dil}", grid=(n_steps, width // lanes),
        out_shape=[jax.ShapeDtypeStruct((seq, width), F32)] * 2,
        in_specs=_ANY * len(behind) + [slab, slab, slab, before, before], out_specs=[slab, slab],
        compiler_params=_params("arbitrary", "arbitrary"),
    )(*behind, q, k, v, k, v)


def _att_bwd(q, k, v, do, cc, lse, dil, behind=()):
    seq, width = q.shape
    _, ub, lanes, nbs = _att_geometry(dil, seq)
    rows = ub * TOKEN_GROUP
    n_blocks = seq // rows
    n_steps = n_blocks // nbs

    def body(q_ref, k_ref, v_ref, do_ref, cc_ref, lse_ref, qx_ref, dox_ref, ccx_ref, lsex_ref,
             dq_ref, dk_ref, dv_ref, carry):
        first, cur_ok, _band = _att_consts(dil)
        step = pl.program_id(1)
        inner_ok = _band(0)
        edge_ok = _band(jnp.where(step < n_steps - 1, 0, ATT_BLOCK))

        @pl.when(step == 0)
        def _():
            carry[...] = jnp.zeros_like(carry)

        def queries(refs, r, c, base):
            q_r, do_r, lse_r, cc_r = refs
            return (_stack_heads(_load_tile(q_r, dil, r, c, base) * ATT_SCALE, first).astype(BF16),
                    _stack_heads(_load_tile(do_r, dil, r, c, base), first).astype(BF16),
                    _stack_bcast(_load_tile(lse_r, dil, r, c, base), first),
                    _stack_bcast(_load_tile(cc_r, dil, r, c, base), first))

        for r in range(dil):
            for j in range(lanes // 128):
                c = slice(j * 128, (j + 1) * 128)
                own = queries((q_ref, do_ref, lse_ref, cc_ref), r, c, 0)
                left = _load_tile(carry, dil, r, c)
                for b in range(nbs):
                    base = b * rows
                    last = b == nbs - 1
                    next_ok = edge_ok if last else inner_ok
                    if last:
                        following = queries((qx_ref, dox_ref, lsex_ref, ccx_ref), r, c, 0)
                    else:
                        following = queries((q_ref, do_ref, lse_ref, cc_ref), r, c, base + rows)
                    (qst, dost, lse_n, cc_n), (qxst, doxst, lse_x, cc_x) = own, following
                    kb = _load_tile(k_ref, dil, r, c, base).astype(BF16)
                    vb = _load_tile(v_ref, dil, r, c, base).astype(BF16)
                    p_cur = jnp.exp(jnp.where(cur_ok, _dot_nt(qst, kb), NEG) - lse_n)
                    p_next = jnp.exp(jnp.where(next_ok, _dot_nt(qxst, kb), NEG) - lse_x)
                    ds_cur = (p_cur * (_dot_nt(dost, vb) + cc_n)).astype(BF16)
                    ds_next = (p_next * (_dot_nt(doxst, vb) + cc_x)).astype(BF16)
                    dq_own = left + _unstack_heads(_dot(ds_cur, kb), first)
                    _store_tile(dq_ref, dil, r, c, dq_own * ATT_SCALE, base)
                    _store_tile(dk_ref, dil, r, c, _dot_tn(ds_cur, qst) + _dot_tn(ds_next, qxst), base)
                    _store_tile(dv_ref, dil, r, c, _dot_tn(p_cur.astype(BF16), dost) + _dot_tn(p_next.astype(BF16), doxst), base)
                    left = _unstack_heads(_dot(ds_next, kb), first)
                    own = following
                _store_tile(carry, dil, r, c, left)

    slab = pl.BlockSpec((nbs * rows, lanes), lambda j, n: (n, j))
    after = pl.BlockSpec((rows, lanes), lambda j, n: (jnp.minimum((n + 1) * nbs, n_blocks - 1), j))
    return pl.pallas_call(
        _behind(body, behind), name=f"att_bwd_d{dil}", grid=(width // lanes, n_steps),
        out_shape=[jax.ShapeDtypeStruct((seq, width), F32)] * 3,
        in_specs=_ANY * len(behind) + [slab] * 6 + [after] * 4, out_specs=[slab] * 3,
        scratch_shapes=[pltpu.VMEM((rows, lanes), F32)],
        compiler_params=_params("arbitrary", "arbitrary"),
    )(*behind, q, k, v, do, cc, lse, q, do, cc, lse)


def _branch_weights(lses):
    mx = jnp.maximum(jnp.maximum(lses[0], lses[1]), lses[2])
    es = [jnp.exp(l - mx) for l in lses]
    inv = 1.0 / (es[0] + es[1] + es[2])
    return [e * inv for e in es]


def _att_combine(outs, lses, att_g, after=()):
    s = outs[0].shape[0]
    tm = 512

    def body(o0, o1, o2, l0, l1, l2, g_ref, att_ref, out_ref):
        ws = _branch_weights([l0[...], l1[...], l2[...]])
        att = ws[0] * o0[...] + ws[1] * o1[...] + ws[2] * o2[...]
        att_ref[...] = att
        ahat, _ = _rms(att)
        out_ref[...] = (ahat * g_ref[...]).astype(BF16)

    tile = _rows(tm, ATT_WIDTH)
    return pl.pallas_call(
        _behind(body, after), name="att_combine", grid=(s // tm,),
        out_shape=[jax.ShapeDtypeStruct((s, ATT_WIDTH), F32), jax.ShapeDtypeStruct((s, ATT_WIDTH), BF16)],
        in_specs=_ANY * len(after) + [tile] * 6 + [_whole((1, ATT_WIDTH))], out_specs=[tile, tile],
        compiler_params=_params("parallel"),
    )(*after, *outs, *lses, att_g)


def _att_combine_bwd_tile(d, att, lses, g):
    ahat, rstd = _rms(att)
    datt = _rms_bwd(d * g, ahat, rstd)
    hi = lax.broadcasted_iota(jnp.int32, (ATT_WIDTH, ATT_WIDTH), 0) // ATT_HEAD_DIM
    hj = lax.broadcasted_iota(jnp.int32, (ATT_WIDTH, ATT_WIDTH), 1) // ATT_HEAD_DIM
    same_head = (hi == hj).astype(BF16)
    prod = datt * att
    prod_hi = prod.astype(BF16)
    prod_lo = (prod - prod_hi.astype(F32)).astype(BF16)
    head_sum = _dot(prod_hi, same_head) + _dot(prod_lo, same_head)
    ws = _branch_weights(lses)
    return [w * datt for w in ws], [-w * head_sum for w in ws], _rowsum(d * ahat)


def _out_bwd(dx1, hg, at, mod, w_out, att, lses, att_g):
    s = dx1.shape[0]
    tm = 512
    n_steps = s // tm

    def body(dx_ref, hg_ref, at_ref, mod_ref, w_ref, att_ref, l0, l1, l2, g_ref,
             dhg_ref, do0, do1, do2, cc0, cc1, cc2, dw_ref, dwb_ref, dgate_ref, dg_ref):
        @pl.when(pl.program_id(0) == 0)
        def _():
            dw_ref[...] = jnp.zeros_like(dw_ref)
            dgate_ref[...] = jnp.zeros_like(dgate_ref)
            dg_ref[...] = jnp.zeros_like(dg_ref)

        hg, at, dx = hg_ref[...], at_ref[...], dx_ref[...]
        mix = _dot(hg, w_ref[0:512, :]) + _dot(at, w_ref[512:1024, :])
        dgate_ref[...] += _rowsum(dx * mix)
        dmix = (mod_ref[:, 2 * D_MODEL:3 * D_MODEL] * dx).astype(BF16)
        dhg_ref[...] = _dot_nt(dmix, w_ref[0:512, :])
        dos, ccs, dg_rows = _att_combine_bwd_tile(_dot_nt(dmix, w_ref[512:1024, :]), att_ref[...],
                                                  [l0[...], l1[...], l2[...]], g_ref[...])
        for val, ref in zip(dos + ccs, (do0, do1, do2, cc0, cc1, cc2)):
            ref[...] = val
        dg_ref[...] += dg_rows
        dw_ref[0:512, :] += _dot_tn(hg, dmix)
        dw_ref[512:1024, :] += _dot_tn(at, dmix)

        @pl.when(pl.program_id(0) == n_steps - 1)
        def _():
            dwb_ref[...] = dw_ref[...].astype(BF16)

    tile = _rows(tm, 512)
    square = _whole((D_MODEL, D_MODEL))
    return pl.pallas_call(
        body, name="out_bwd", grid=(n_steps,),
        out_shape=[jax.ShapeDtypeStruct((s, 512), F32)] * 7
        + [jax.ShapeDtypeStruct((D_MODEL, D_MODEL), F32), jax.ShapeDtypeStruct((D_MODEL, D_MODEL), BF16),
           jax.ShapeDtypeStruct((1, D_MODEL), F32), jax.ShapeDtypeStruct((1, ATT_WIDTH), F32)],
        in_specs=[_rows(tm, D_MODEL), tile, tile, _whole((1, 6 * D_MODEL)), square] + [tile] * 4 + [_whole((1, ATT_WIDTH))],
        out_specs=[tile] * 7 + [square, square, _whole((1, D_MODEL)), _whole((1, ATT_WIDTH))],
        compiler_params=_params("arbitrary"),
    )(dx1, hg, at, mod, w_out, att, *lses, att_g)


def _ffn(x, hg, outs, lses, att_g, target, mod, g2, gf, w_out, w_gu, w_down, after=()):
    s = x.shape[0]
    tm = 256

    def body(x_ref, hg_ref, o0, o1, o2, l0, l1, l2, ag_ref, t_ref, mod_ref, g2_ref, gf_ref, wo_ref, wgu_hbm, wd_hbm,
             dx_ref, h2_ref, act_ref, dau_ref, dff_ref, sums_ref, loss_ref, att_ref, at_ref, wgu, wd, au_s, sem):
        @pl.when(pl.program_id(0) == 0)
        def _():
            c1 = pltpu.make_async_copy(wgu_hbm, wgu, sem.at[0])
            c2 = pltpu.make_async_copy(wd_hbm, wd, sem.at[1])
            c1.start()
            c2.start()
            c1.wait()
            c2.wait()
            sums_ref[...] = jnp.zeros_like(sums_ref)
            loss_ref[...] = jnp.zeros_like(loss_ref)

        ws = _branch_weights([l0[...], l1[...], l2[...]])
        att = ws[0] * o0[...] + ws[1] * o1[...] + ws[2] * o2[...]
        att_ref[...] = att
        ahat, _ = _rms(att)
        at = (ahat * ag_ref[...]).astype(BF16)
        at_ref[...] = at
        mix = _dot(hg_ref[...], wo_ref[0:512, :]) + _dot(at, wo_ref[512:1024, :])
        x1v = x_ref[...] + mod_ref[:, 2 * D_MODEL:3 * D_MODEL] * mix
        xhat, rstd = _rms(x1v)
        g2 = g2_ref[...]
        n2 = xhat * g2
        scale2 = 1.0 + mod_ref[:, 4 * D_MODEL:5 * D_MODEL]
        gate2 = mod_ref[:, 5 * D_MODEL:6 * D_MODEL]
        hb = (n2 * scale2 + mod_ref[:, 3 * D_MODEL:4 * D_MODEL]).astype(BF16)
        h2_ref[...] = hb
        au_s[...] = _dot_nt(hb, wgu[...])
        a = au_s[:, 0:D_FF]
        act = (_silu(a) * au_s[:, D_FF:2 * D_FF]).astype(BF16)
        act_ref[...] = act
        ff = _dot(act, wd[...])
        x2 = x1v + gate2 * ff
        nf, rstd_f = _rms(x2)
        gfv = gf_ref[...]
        err = nf * gfv - t_ref[...]
        loss_ref[...] += 0.5 * jnp.sum(_rowsum(err * err), axis=-1, keepdims=True) * (1.0 / D_MODEL)
        dy = err * (1.0 / D_MODEL)
        dx2 = _rms_bwd(dy * gfv, nf, rstd_f)
        dffb = (gate2 * dx2).astype(BF16)
        dff_ref[...] = dffb
        dact = _dot_nt(dffb, wd[...])
        a = au_s[:, 0:D_FF]
        dau_ref[:, 0:D_FF] = (dact * au_s[:, D_FF:2 * D_FF] * _dsilu(a)).astype(BF16)
        dau_ref[:, D_FF:2 * D_FF] = (dact * _silu(a)).astype(BF16)
        dh = _dot(dau_ref[...], wgu[...])
        dn = dh * scale2
        sums_ref[0:1, :] += _rowsum(dh)
        sums_ref[1:2, :] += _rowsum(dh * n2)
        sums_ref[2:3, :] += _rowsum(dx2 * ff)
        sums_ref[3:4, :] += _rowsum(dn * xhat)
        sums_ref[4:5, :] += _rowsum(dy * nf)
        dx_ref[...] = dx2 + _rms_bwd(dn * g2, xhat, rstd)

    vec = _whole((1, D_MODEL))
    hbm = pl.BlockSpec(memory_space=pl.ANY)
    half = _rows(tm, 512)
    return pl.pallas_call(
        _behind(body, after), name="ffn", grid=(s // tm,),
        out_shape=[jax.ShapeDtypeStruct((s, D_MODEL), F32), jax.ShapeDtypeStruct((s, D_MODEL), BF16),
                   jax.ShapeDtypeStruct((s, D_FF), BF16), jax.ShapeDtypeStruct((s, 2 * D_FF), BF16),
                   jax.ShapeDtypeStruct((s, D_MODEL), BF16), jax.ShapeDtypeStruct((8, D_MODEL), F32),
                   jax.ShapeDtypeStruct((1, 128), F32), jax.ShapeDtypeStruct((s, ATT_WIDTH), F32),
                   jax.ShapeDtypeStruct((s, ATT_WIDTH), BF16)],
        in_specs=_ANY * len(after) + [_rows(tm, D_MODEL)] + [half] * 7 + [_whole((1, ATT_WIDTH)), _rows(tm, D_MODEL),
                                                                          _whole((1, 6 * D_MODEL)), vec, vec,
                                                                          _whole((D_MODEL, D_MODEL)), hbm, hbm],
        out_specs=[_rows(tm, D_MODEL), _rows(tm, D_MODEL), _rows(tm, D_FF), _rows(tm, 2 * D_FF), _rows(tm, D_MODEL),
                   _whole((8, D_MODEL)), _whole((1, 128)), half, half],
        scratch_shapes=[pltpu.VMEM((2 * D_FF, D_MODEL), BF16), pltpu.VMEM((D_FF, D_MODEL), BF16),
                        pltpu.VMEM((tm, 2 * D_FF), F32), pltpu.SemaphoreType.DMA((2,))],
        compiler_params=_params("arbitrary", vmem=V7X_VMEM_MOST),
    )(*after, x, hg, *outs, *lses, att_g, target, mod, g2, gf, w_out, w_gu, w_down)


def _weight_grad(a, b, name, rounded=False):
    s, m = a.shape
    n = b.shape[1]
    ts = min(s, 2048)
    n_steps = s // ts
    tm = max(t for t in range(128, m + 1, 128) if m % t == 0 and t * n * 4 <= 6 * 1024 * 1024)

    def body(a_ref, b_ref, o_ref, *ob_ref):
        @pl.when(pl.program_id(1) == 0)
        def _():
            o_ref[...] = jnp.zeros_like(o_ref)

        o_ref[...] += _dot_tn(a_ref[...], b_ref[...])
        if rounded:
            @pl.when(pl.program_id(1) == n_steps - 1)
            def _():
                ob_ref[0][...] = o_ref[...].astype(BF16)

    tile = pl.BlockSpec((tm, n), lambda j, i: (j, 0))
    return pl.pallas_call(
        body, name=name, grid=(m // tm, n_steps),
        out_shape=[jax.ShapeDtypeStruct((m, n), F32)] + [jax.ShapeDtypeStruct((m, n), BF16)] * rounded,
        in_specs=[pl.BlockSpec((ts, tm), lambda j, i: (i, j)), pl.BlockSpec((ts, n), lambda j, i: (i, 0))],
        out_specs=[tile] + [tile] * rounded,
        compiler_params=_params("parallel", "arbitrary"),
    )(a, b)


def _adamw_math(w, g, m, v):
    m = ADAM_B1 * m + (1.0 - ADAM_B1) * g
    v = ADAM_B2 * v + (1.0 - ADAM_B2) * (g * g)
    m_hat = m / (1.0 - ADAM_B1 ** ADAM_STEP)
    v_hat = v / (1.0 - ADAM_B2 ** ADAM_STEP)
    delta = -ADAM_LR * (m_hat / (jnp.sqrt(v_hat) + ADAM_EPS) + ADAM_WD * w)
    return delta, m, v


def _adamw_shard(where, w, m, v, partial, got, name, after=()):
    r, c = w.shape
    tr = _shard_rows(r)
    lead = partial.ndim - 2
    n_got = got.shape[0]

    def body(where_ref, *refs):
        w_ref, m_ref, v_ref, own_ref, *rest = refs[len(after):]
        got_refs, (grad_ref, d_ref, nm_ref, nv_ref) = rest[:n_got], rest[n_got:]
        g = own_ref[...]
        for g_ref in got_refs:
            g = g + g_ref[...].astype(F32)
        grad_ref[...] = g
        d_ref[...], nm_ref[...], nv_ref[...] = _adamw_math(w_ref[...], g, m_ref[...], v_ref[...])

    tile = pl.BlockSpec((tr, c), lambda i, where_ref: (i, 0))
    own = pl.BlockSpec((None,) * lead + (tr, c), lambda i, where_ref: (*[where_ref[d] for d in range(lead)], i, 0))
    part = [pl.BlockSpec((None, tr, c), functools.partial(lambda j, i, where_ref: (j, i, 0), j)) for j in range(n_got)]
    return pl.pallas_call(
        body, name=name,
        grid_spec=pltpu.PrefetchScalarGridSpec(num_scalar_prefetch=1, grid=(r // tr,),
                                               in_specs=_ANY * len(after) + [tile] * 3 + [own] + part, out_specs=[tile] * 4),
        out_shape=[jax.ShapeDtypeStruct((r, c), F32)] * 4, compiler_params=_params("parallel"),
    )(where, *after, w, m, v, partial, *[got] * n_got)


def _small_update(small_all, dmod_blocks, c_all, logits, w_ada, m_ada, v_ada, smalls, after=()):
    def body(sm_ref, dm_ref, c_ref, lg_ref, wa_ref, ma_ref, va_ref, *rest):
        ins, outs = rest[:21], rest[21:]
        _, me = _flip(0)
        tot = sm_ref[0:1, :]
        for i in range(1, N_DEV):
            tot = tot + sm_ref[i:i + 1, :]
        loss_ref = outs[0]
        loss_ref[...] = tot[:, SM_LOSS:SM_LOSS + 128]
        g_ada = lax.dot_general(_silu(c_ref[...]), dm_ref[me], (((0,), (0,)), ((), ())),
                                preferred_element_type=F32, precision=HIGHEST)
        outs[1][...] = g_ada
        outs[2][...], outs[3][...], outs[4][...] = _adamw_math(wa_ref[...], g_ada, ma_ref[...], va_ref[...])
        p0 = _lower_bound(lg_ref)
        dl0 = tot[:, SM_LB:SM_LB + 512] * p0 * (1.0 - p0)
        grads = [tot[:, SM_MOD:SM_MOD + 6 * D_MODEL], tot[:, SM_G1:SM_G1 + D_MODEL], tot[:, SM_G2:SM_G2 + D_MODEL],
                 tot[:, SM_GF:SM_GF + D_MODEL], tot[:, SM_ATT:SM_ATT + 512], tot[:, SM_HG:SM_HG + 128],
                 jnp.where(lax.broadcasted_iota(jnp.int32, (2, 512), 0) == 0, dl0, -dl0)]
        for i, g in enumerate(grads):
            w_ref, m_ref, v_ref = ins[3 * i:3 * i + 3]
            o = outs[5 + 4 * i:9 + 4 * i]
            o[0][...] = g
            o[1][...], o[2][...], o[3][...] = _adamw_math(w_ref[...], g, m_ref[...], v_ref[...])

    flat = [t for trio in smalls for t in trio]
    vm = pl.BlockSpec(memory_space=pltpu.VMEM)
    out_shape = [jax.ShapeDtypeStruct((1, 128), F32)] + [jax.ShapeDtypeStruct(w_ada.shape, F32)] * 4
    for trio in smalls:
        out_shape += [jax.ShapeDtypeStruct(trio[0].shape, F32)] * 4
    return pl.pallas_call(
        _behind(body, after), name="small_update", out_shape=out_shape,
        in_specs=_ANY * len(after) + [vm] * (7 + len(flat)), out_specs=[vm] * len(out_shape),
        compiler_params=pltpu.CompilerParams(vmem_limit_bytes=V7X_VMEM_LIMIT),
    )(*after, small_all, dmod_blocks, c_all, logits, w_ada, m_ada, v_ada, *flat)


def kernel(x, c, w_ada, b_ada, norm1_g, w_in, hg_lb_logits, hg_onorm_g, att_onorm_g, w_out, norm2_g, w_gate_up, w_down, final_g, loss_target, m_w_ada, m_b_ada, m_norm1_g, m_w_in, m_hg_lb_logits, m_hg_onorm_g, m_att_onorm_g, m_w_out, m_norm2_g, m_w_gate_up, m_w_down, m_final_g, v_w_ada, v_b_ada, v_norm1_g, v_w_in, v_hg_lb_logits, v_hg_onorm_g, v_att_onorm_g, v_w_out, v_norm2_g, v_w_gate_up, v_w_down, v_final_g):
    x2d, target = x[0], loss_target[0]
    seq = x2d.shape[0]
    assert seq % (ATT_BLOCK * max(DILATIONS)) == 0 and seq % HG_TILE == 0
    gf = final_g.reshape(1, D_MODEL)

    c_all = _exchange_small(c.reshape(8, D_MODEL // 8), None, "gather_c").reshape(N_DEV, D_MODEL)
    ada = _ada_rows(c_all, w_ada[0], b_ada)
    mod = _exchange_small(ada, 1, "scatter_mod").reshape(1, 6 * D_MODEL)

    core = lax.axis_index("c").astype(jnp.int32).reshape(1)
    chip = (2 * lax.axis_index("x") + lax.axis_index("y")).astype(jnp.int32).reshape(1)
    me = 4 * lax.axis_index("x") + 2 * lax.axis_index("y") + lax.axis_index("c")

    g_in, = _gather_weights([w_in[0].T.astype(BF16)])
    w_in_b = g_in.reshape(IN_WIDTH, D_MODEL)
    rest_shards = [w_out[0].astype(BF16), w_gate_up[0].T.astype(BF16), w_down[0].astype(BF16)]
    lands = [lax.empty((N_DEV,) + s.shape, BF16) for s in rest_shards]
    g_send, g_recv, g_srcs, g_lands, tok = _copies_start("gather_rest_start", _plan_gather_own, 12, rest_shards, lands, [w_in_b, mod])
    flight = {}

    def stage(name, *vals):
        if name == "attention_begun":
            flight["shards"], got = _copies_wait("gather_rest_wait", _plan_gather_own, g_send, g_recv, g_srcs, g_lands, list(vals))
            flight["pass"] = _copies_start("gather_pass_start", _plan_gather_pass, 9, [], got, [])
            return [flight["pass"][4]]
        if name == "mixer_weights_done":
            shapes = [(4, 2, D_MODEL // N_DEV, D_MODEL), (4, 2, 2 * D_FF // N_DEV, D_MODEL), (4, 2, D_FF // N_DEV, D_MODEL)]
            flight["grads"] = [g32.reshape(sh) for (g32, _), sh in zip(vals, shapes)]
            rounded = [g16.reshape(sh) for (_, g16), sh in zip(vals, shapes)]
            direct_lands = [lax.empty((N_DEV - 1,) + sh[2:], BF16) for sh in shapes]
            flight["direct"] = _copies_start("reduce_rest_start", _plan_reduce_direct, 21, rounded, direct_lands, [])
            return [flight["direct"][4]]
        raise ValueError(name)

    def rest_weights(after):
        s, r, _, p_lands, _ = flight["pass"]
        _, got = _copies_wait("gather_pass_wait", _plan_gather_pass, s, r, [], p_lands, [after])
        full = [lax.dynamic_update_index_in_dim(g, shard, me, 0) for g, shard in zip(got, flight["shards"])]
        return full[0].reshape(D_MODEL, D_MODEL), full[1].reshape(2 * D_FF, D_MODEL), full[2].reshape(D_FF, D_MODEL)

    grad_x, dw_in, small = _block_step(x2d, target, mod, norm1_g, hg_lb_logits, hg_onorm_g, att_onorm_g, norm2_g, gf,
                                       w_in_b, rest_weights, stage, [tok])

    g_in8 = dw_in.reshape(4, 2, IN_WIDTH // N_DEV, D_MODEL)
    in_pairs = _copies_start("reduce_pairs_in_start", _plan_reduce_pairs, 4, [g_in8], [lax.empty((4,) + g_in8.shape[2:], F32)], [])
    s, r, srcs, d_lands, _ = flight["direct"]
    _, recv_rest = _copies_wait("reduce_rest_wait", _plan_reduce_direct, s, r, srcs, d_lands, [in_pairs[4]])
    small_rows = jnp.pad(small, ((0, 0), (0, SM_PADDED - SM_WIDTH))).reshape(SM_PADDED // 128, 128)
    small_all = _exchange_small(small_rows, None, "gather_small", [in_pairs[4]]).reshape(N_DEV, SM_PADDED)[:, :SM_WIDTH]
    in_grads, got_in = _copies_wait("reduce_pairs_in_wait", _plan_reduce_pairs, in_pairs[0], in_pairs[1], in_pairs[2], in_pairs[3],
                                    [small_all])
    in_s32, in_s16 = _pair_sum(core, in_grads[0], got_in[0], "pair_sum_in")
    in_chips = _copies_start("reduce_chips_in_start", _plan_reduce_chips, 3, [in_s16], [lax.empty((3,) + in_s16.shape[1:], BF16)], [])
    big, updated = {}, []
    rest_params = [("w_out", w_out, m_w_out, v_w_out), ("w_gate_up", w_gate_up, m_w_gate_up, v_w_gate_up), ("w_down", w_down, m_w_down, v_w_down)]
    mine = jnp.concatenate([chip, core])
    for (n, w, m, v), g32, got in zip(rest_params, flight["grads"], recv_rest):
        if n == "w_gate_up":
            outs4 = _adamw_shard(mine, w[0].T, m[0].T, v[0].T, g32, got, f"adamw_{n}", [in_chips[4]])
            big[n] = [t.T[None] for t in outs4]
        else:
            outs4 = _adamw_shard(mine, w[0], m[0], v[0], g32, got, f"adamw_{n}", [in_chips[4]])
            big[n] = [t[None] for t in outs4]
        updated.append(outs4[3])
    smalls = [(b_ada, m_b_ada, v_b_ada), (norm1_g, m_norm1_g, v_norm1_g), (norm2_g, m_norm2_g, v_norm2_g),
              (gf, m_final_g.reshape(1, D_MODEL), v_final_g.reshape(1, D_MODEL)),
              (att_onorm_g, m_att_onorm_g, v_att_onorm_g), (hg_onorm_g, m_hg_onorm_g, v_hg_onorm_g),
              (hg_lb_logits, m_hg_lb_logits, v_hg_lb_logits)]
    dmod_blocks = small_all[:, :6 * D_MODEL].reshape(N_DEV, N_DEV, 6 * D_MODEL // N_DEV).transpose(1, 0, 2)
    res = _small_update(small_all, dmod_blocks, c_all, hg_lb_logits, w_ada[0], m_w_ada[0], v_w_ada[0], smalls, [in_chips[4]])
    _, recv_in = _copies_wait("reduce_chips_in_wait", _plan_reduce_chips, in_chips[0], in_chips[1], in_chips[2], in_chips[3],
                              [res[0]] + updated)
    big["w_in"] = [t.T[None] for t in _adamw_shard(chip, w_in[0].T, m_w_in[0].T, v_w_in[0].T, in_s32, recv_in[0], "adamw_w_in")]
    loss = res[0][0, 0]
    ada4 = [t[None] for t in res[1:5]]
    sm4 = {n: list(res[5 + 4 * i:9 + 4 * i]) for i, n in enumerate(["b_ada", "norm1_g", "norm2_g", "final_g", "att", "hg", "lb"])}
    sm4["final_g"] = [t.reshape(D_MODEL) for t in sm4["final_g"]]

    order = [ada4, sm4["b_ada"], sm4["norm1_g"], big["w_in"], sm4["lb"], sm4["hg"], sm4["att"], big["w_out"], sm4["norm2_g"],
             big["w_gate_up"], big["w_down"], sm4["final_g"]]
    return (loss, grad_x[None], *[o[0] for o in order], *[o[1] for o in order], *[o[2] for o in order], *[o[3] for o in order])


def _block_step(x2d, target, mod, norm1_g, hg_lb_logits, hg_onorm_g, att_onorm_g, norm2_g, gf, w_in_b, rest_weights, stage,
                after=()):
    h1, hq, hf, hi, hgt, aq, ak, av, hg_out, hg_o, hg_states = _in_fwd(x2d, mod, norm1_g, w_in_b, hg_lb_logits, hg_onorm_g, after)
    branch = [_att_fwd(aq, ak, av, d) for d in DILATIONS[:2]]
    behind = stage("attention_begun", branch[0][0], branch[1][0])
    branch += [_att_fwd(aq, ak, av, d, behind) for d in DILATIONS[2:]]
    outs = [b[0] for b in branch]
    lses = [b[1] for b in branch]
    w_out_b, w_gu_b, w_down_b = rest_weights(outs[-1])

    dx1, h2, act, dau, dff, ffn_sums, loss_part, att, att_out = _ffn(
        x2d, hg_out, outs, lses, att_onorm_g, target, mod, norm2_g, gf, w_out_b, w_gu_b, w_down_b)
    dw_gu = _weight_grad(dau, h2, "dw_gate_up", rounded=True)
    dw_down = _weight_grad(act, dff, "dw_down", rounded=True)

    back = _out_bwd(dx1, hg_out, att_out, mod, w_out_b, att, lses, att_onorm_g)
    dhg, dos, ccs = back[0], back[1:4], back[4:7]
    dw_out, dw_out_b, dgate1, d_att_g = back[7:11]
    behind = stage("mixer_weights_done", (dw_out, dw_out_b), dw_gu, dw_down)
    datt = []
    for i, d in enumerate(DILATIONS):
        datt.append(_att_bwd(aq, ak, av, dos[i], ccs[i], lses[i], d, behind))
    dhq, dhf, dhi, dhgt, d_hg_g, d_lb = _hg_bwd(hq, hf, hi, hgt, hg_lb_logits, hg_onorm_g, hg_o, hg_states, dhg)
    dps = [dhq, dhf, dhi, dhgt] + [datt[i][j] for j in range(3) for i in range(3)]
    grad_x, dp_b, dshift1, dscale1, d_g1 = _in_bwd(x2d, dx1, mod, norm1_g, w_in_b, dps)
    dw_in, = _weight_grad(dp_b, h1, "dw_in")
    small = jnp.concatenate([dshift1, dscale1, dgate1, ffn_sums[0:1], ffn_sums[1:2], ffn_sums[2:3], d_g1, ffn_sums[3:4],
                             ffn_sums[4:5], d_att_g, d_lb, d_hg_g, loss_part], axis=1)
    return grad_x, dw_in, small
```

```python
import functools

import jax
import jax.numpy as jnp
from jax import lax
from jax.experimental import pallas as pl
from jax.experimental.pallas import tpu as pltpu

F32 = jnp.float32
BF16 = jnp.bfloat16
HIGHEST = lax.Precision.HIGHEST
MESH = pl.DeviceIdType.MESH

D_MODEL = 1024
N_DEV = 8
HG_HEADS = 4
HG_DIM = 128
HG_WIDTH = HG_HEADS * HG_DIM
HG_CHUNK = 128
ATT_WIDTH = 512
ATT_HEAD_DIM = 64
ATT_BLOCK = 128
DILATIONS = (1, 4, 16)
ATT_SCALE = ATT_HEAD_DIM ** -0.5
D_FF = 2816
IN_WIDTH = 7 * 512
RMS_EPS = 1e-6
NEG = -1e30

ADAM_LR = 0.001
ADAM_B1 = 0.9
ADAM_B2 = 0.999
ADAM_EPS = 1e-08
ADAM_WD = 0.01
ADAM_STEP = 10

V7X_VMEM_LIMIT = 56 * 1024 * 1024
V7X_VMEM_MOST = 60 * 1024 * 1024

SM_MOD = 0
SM_G1 = 6 * D_MODEL
SM_G2 = 7 * D_MODEL
SM_GF = 8 * D_MODEL
SM_ATT = 9 * D_MODEL
SM_LB = 9 * D_MODEL + 512
SM_HG = 10 * D_MODEL
SM_LOSS = 10 * D_MODEL + 128
SM_WIDTH = 10 * D_MODEL + 256
SM_PADDED = 88 * 128


def _params(*sem, vmem=V7X_VMEM_LIMIT):
    return pltpu.CompilerParams(dimension_semantics=sem, vmem_limit_bytes=vmem)


def _dot(a, b):
    return jnp.dot(a, b, preferred_element_type=F32)


def _dot_nt(a, b):
    return lax.dot_general(a, b, (((1,), (1,)), ((), ())), preferred_element_type=F32)


def _dot_tn(a, b):
    return lax.dot_general(a, b, (((0,), (0,)), ((), ())), preferred_element_type=F32)


def _dot_f32(a, b):
    return jnp.dot(a, b, preferred_element_type=F32, precision=HIGHEST)


def _sigmoid(x):
    return 1.0 / (1.0 + jnp.exp(-x))


def _silu(x):
    return x * _sigmoid(x)


def _dsilu(x):
    s = _sigmoid(x)
    return s * (1.0 + x * (1.0 - s))


def _rms(x):
    rstd = lax.rsqrt(jnp.mean(x * x, axis=-1, keepdims=True) + RMS_EPS)
    return x * rstd, rstd


def _rms_bwd(dn, xhat, rstd):
    return rstd * (dn - xhat * jnp.mean(dn * xhat, axis=-1, keepdims=True))


def _rowsum(x):
    return jnp.sum(x, axis=0, keepdims=True)


def _rows(tm, n):
    return pl.BlockSpec((tm, n), lambda i: (i, 0))


def _whole(shape):
    return pl.BlockSpec(shape, lambda i: (0,) * len(shape))


def _mesh_pos():
    return lax.axis_index("x"), lax.axis_index("y"), lax.axis_index("c")


def _flip(k):
    x, y, c = _mesh_pos()
    px = 1 - x if k & 4 else x
    py = 1 - y if k & 2 else y
    pc = 1 - c if k & 1 else c
    return (px, py, pc), 4 * px + 2 * py + pc


_ANY = [pl.BlockSpec(memory_space=pl.ANY)]


def _behind(body, after):
    return lambda *refs: body(*refs[len(after):])


def _exchange_small(x, rows_per_peer, name, after=()):
    r_all, cols = x.shape
    r_out = r_all if rows_per_peer is None else rows_per_peer

    def body(x_ref, out_ref, send_sems, recv_sems):
        _, me = _flip(0)

        def src(pid):
            if rows_per_peer is None:
                return x_ref
            return x_ref.at[pl.ds(pl.multiple_of(pid * r_out, r_out), r_out), :]

        if rows_per_peer is None:
            out_ref[me] = x_ref[...]
        else:
            out_ref[me] = x_ref[pl.ds(pl.multiple_of(me * r_out, r_out), r_out), :]
        sends = []
        for k in range(1, N_DEV):
            dev, pid = _flip(k)
            cp = pltpu.make_async_remote_copy(src_ref=src(pid), dst_ref=out_ref.at[me], send_sem=send_sems.at[k - 1],
                                              recv_sem=recv_sems.at[k - 1], device_id=dev, device_id_type=MESH)
            cp.start()
            sends.append(cp)
        for k in range(1, N_DEV):
            dev, pid = _flip(k)
            pltpu.make_async_remote_copy(src_ref=src(pid), dst_ref=out_ref.at[pid], send_sem=send_sems.at[k - 1],
                                         recv_sem=recv_sems.at[k - 1], device_id=dev, device_id_type=MESH).wait_recv()
        for cp in sends:
            cp.wait_send()

    return pl.pallas_call(
        _behind(body, after), name=name,
        out_shape=jax.ShapeDtypeStruct((N_DEV, r_out, cols), x.dtype),
        in_specs=_ANY * len(after) + [pl.BlockSpec(memory_space=pltpu.VMEM)],
        out_specs=pl.BlockSpec(memory_space=pltpu.VMEM),
        scratch_shapes=[pltpu.SemaphoreType.DMA((N_DEV - 1,)), pltpu.SemaphoreType.DMA((N_DEV - 1,))],
    )(*after, x)


def _gather_weights(shards):
    n = len(shards)

    def body(*refs):
        xs, outs = refs[:n], refs[n:2 * n]
        send_sems, recv_sems, local_sems = refs[2 * n:]
        x, y, c = _mesh_pos()
        me, sibling = (x, y, c), (x, y, 1 - c)
        chips = [(1 - x, y), (x, 1 - y), (1 - x, 1 - y)]

        def blk(a, px, py, pc):
            return outs[a].at[4 * px + 2 * py + pc]

        def copy(a, k, block, to, src=None):
            return pltpu.make_async_remote_copy(
                src_ref=blk(a, *block) if src is None else src, dst_ref=blk(a, *block),
                send_sem=send_sems.at[a * 7 + k], recv_sem=recv_sems.at[a * 7 + k], device_id=to, device_id_type=MESH)

        mine = [pltpu.make_async_copy(xs[a], blk(a, *me), local_sems.at[a]) for a in range(n)]
        for cp in mine:
            cp.start()
        first = []
        for a in range(n):
            first.append(copy(a, 0, me, sibling, src=xs[a]))
            first += [copy(a, 1 + j, me, (*chip, c), src=xs[a]) for j, chip in enumerate(chips)]
        for cp in first:
            cp.start()
        passed = []
        for j, chip in enumerate(chips):
            for a in range(n):
                copy(a, 1 + j, (*chip, c), me).wait_recv()
                cp = copy(a, 4 + j, (*chip, c), sibling)
                cp.start()
                passed.append(cp)
        for a in range(n):
            copy(a, 0, sibling, me).wait_recv()
            for j, chip in enumerate(chips):
                copy(a, 4 + j, (*chip, 1 - c), me).wait_recv()
        for cp in first + passed:
            cp.wait_send()
        for cp in mine:
            cp.wait()

    hbm = pl.BlockSpec(memory_space=pl.ANY)
    return pl.pallas_call(
        body, name="gather_weights",
        out_shape=[jax.ShapeDtypeStruct((N_DEV,) + s.shape, s.dtype) for s in shards],
        in_specs=[hbm] * n, out_specs=[hbm] * n,
        scratch_shapes=[pltpu.SemaphoreType.DMA((7 * n,)), pltpu.SemaphoreType.DMA((7 * n,)), pltpu.SemaphoreType.DMA((n,))],
    )(*shards)


_HBM = pl.BlockSpec(memory_space=pltpu.HBM)
_SEM = pl.BlockSpec(memory_space=pltpu.SEMAPHORE)
_DATAFLOW = pltpu.SideEffectType.DATAFLOW_SIDE_EFFECTING


def _copies_start(name, plan, n_copies, srcs, lands, after):
    bufs = list(srcs) + list(lands)
    nb = len(bufs)

    def body(*refs):
        ins, send_sems, recv_sems, token = refs[:nb], refs[nb + len(after)], refs[nb + len(after) + 1], refs[-1]
        for i, (src, dst, dev) in enumerate(plan(ins[:len(srcs)], ins[len(srcs):])):
            pltpu.make_async_remote_copy(src_ref=src, dst_ref=dst, send_sem=send_sems.at[i], recv_sem=recv_sems.at[i],
                                         device_id=dev, device_id_type=MESH).start()
        token[...] = jnp.zeros_like(token)

    outs = pl.pallas_call(
        body, name=name,
        out_shape=(pltpu.SemaphoreType.DMA((n_copies,)), pltpu.SemaphoreType.DMA((n_copies,)),
                   *[pltpu.HBM(b.shape, b.dtype) for b in bufs], jax.ShapeDtypeStruct((8, 128), F32)),
        in_specs=[_HBM] * nb + [pl.BlockSpec(memory_space=pl.ANY)] * len(after),
        out_specs=(_SEM, _SEM, *[_HBM] * nb, pl.BlockSpec(memory_space=pltpu.VMEM)),
        input_output_aliases={i: 2 + i for i in range(nb)},
        compiler_params=pltpu.CompilerParams(has_side_effects=_DATAFLOW),
    )(*[pltpu.with_memory_space_constraint(b, pltpu.HBM) for b in bufs], *after)
    return outs[0], outs[1], list(outs[2:2 + len(srcs)]), list(outs[2 + len(srcs):2 + nb]), outs[-1]


def _copies_wait(name, plan, send_sems, recv_sems, srcs, lands, after):
    bufs = list(srcs) + list(lands)
    nb = len(bufs)

    def body(*refs):
        ins, send_ref, recv_ref = refs[:nb], refs[nb], refs[nb + 1]
        for i, (src, dst, dev) in enumerate(plan(ins[:len(srcs)], ins[len(srcs):])):
            cp = pltpu.make_async_remote_copy(src_ref=src, dst_ref=dst, send_sem=send_ref.at[i], recv_sem=recv_ref.at[i],
                                              device_id=dev, device_id_type=MESH)
            cp.wait_send()
            cp.wait_recv()

    outs = pl.pallas_call(
        body, name=name, out_shape=[pltpu.HBM(b.shape, b.dtype) for b in bufs],
        in_specs=[_HBM] * nb + [_SEM, _SEM] + [pl.BlockSpec(memory_space=pl.ANY)] * len(after), out_specs=[_HBM] * nb,
        input_output_aliases={i: i for i in range(nb)},
        compiler_params=pltpu.CompilerParams(has_side_effects=_DATAFLOW),
    )(*bufs, send_sems, recv_sems, *after)
    return list(outs[:len(srcs)]), list(outs[len(srcs):])


def _plan_gather_own(srcs, lands):
    _, me = _flip(0)
    return [(srcs[a], lands[a].at[me], _flip(k)[0]) for a in range(len(srcs)) for k in (1, 4, 2, 6)]


def _plan_gather_pass(srcs, lands):
    sibling = _flip(1)[0]
    plan = []
    for land in lands:
        for k in (4, 2, 6):
            block = land.at[_flip(k)[1]]
            plan.append((block, block, sibling))
    return plan


def _plan_reduce_pairs(srcs, lands):
    x, y, c = _mesh_pos()
    return [(srcs[a].at[chip, 1 - c], lands[a].at[chip], (x, y, 1 - c)) for a in range(len(srcs)) for chip in range(4)]


def _plan_reduce_chips(srcs, lands):
    plan = []
    for a in range(len(srcs)):
        for j, k in enumerate((4, 2, 6)):
            dev = _flip(k)[0]
            plan.append((srcs[a].at[2 * dev[0] + dev[1]], lands[a].at[j], dev))
    return plan


def _plan_reduce_direct(srcs, lands):
    plan = []
    for a in range(len(srcs)):
        for k in range(1, N_DEV):
            dev = _flip(k)[0]
            plan.append((srcs[a].at[2 * dev[0] + dev[1], dev[2]], lands[a].at[k - 1], dev))
    return plan


def _shard_rows(r):
    return r // 2 if r % 32 == 0 else r


def _pair_sum(core, grads, got, name):
    _, _, r, c = grads.shape
    tr = _shard_rows(r)

    def body(core_ref, a_ref, b_ref, o_ref, ob_ref):
        s = a_ref[...] + b_ref[...]
        o_ref[...] = s
        ob_ref[...] = s.astype(BF16)

    spec = pl.BlockSpec((None, tr, c), lambda i, j, core_ref: (i, j, 0))
    return pl.pallas_call(
        body, name=name,
        grid_spec=pltpu.PrefetchScalarGridSpec(
            num_scalar_prefetch=1, grid=(4, r // tr),
            in_specs=[pl.BlockSpec((None, None, tr, c), lambda i, j, core_ref: (i, core_ref[0], j, 0)), spec],
            out_specs=[spec, spec]),
        out_shape=[jax.ShapeDtypeStruct((4, r, c), F32), jax.ShapeDtypeStruct((4, r, c), BF16)],
        compiler_params=_params("parallel", "parallel"),
    )(core, grads, got)


def _ada_rows(c_all, w_ada, b_ada):
    n_cols = w_ada.shape[1]

    def body(c_ref, w_ref, b_ref, o_ref):
        _, me = _flip(0)
        bias = b_ref[:, pl.ds(pl.multiple_of(me * n_cols, 128), n_cols)]
        o_ref[...] = _dot_f32(_silu(c_ref[...]), w_ref[...]) + bias

    return pl.pallas_call(
        body, name="ada_rows", out_shape=jax.ShapeDtypeStruct((N_DEV, n_cols), F32),
        in_specs=[pl.BlockSpec(memory_space=pltpu.VMEM)] * 3, out_specs=pl.BlockSpec(memory_space=pltpu.VMEM),
    )(c_all, w_ada, b_ada)


def _in_fwd(x, mod, g1, w_in, logits, onorm_g, after=()):
    s = x.shape[0]
    tm = HG_TILE

    def body(x_ref, mod_ref, g_ref, w_ref, lg_ref, og_ref, h_ref, *rest):
        groups, (out_ref, o_ref, st_ref, state, qf_s, kk_s, lf_s) = rest[:7], rest[7:]
        xhat, _ = _rms(x_ref[...])
        h = (xhat * g_ref[...]) * (1.0 + mod_ref[:, D_MODEL:2 * D_MODEL]) + mod_ref[:, 0:D_MODEL]
        hb = h.astype(BF16)
        h_ref[...] = hb
        for j, o_ref_j in enumerate(groups):
            o_ref_j[...] = _dot_nt(hb, w_ref[j * 512:(j + 1) * 512, :])
        _hg_fwd_tile(*groups[:4], lg_ref, og_ref, out_ref, o_ref, st_ref, state, qf_s, kk_s, lf_s)

    tile = _rows(tm, 512)
    return pl.pallas_call(
        _behind(body, after), name="in_fwd", grid=(s // tm,),
        out_shape=[jax.ShapeDtypeStruct((s, D_MODEL), BF16)] + [jax.ShapeDtypeStruct((s, 512), F32)] * 7
        + [jax.ShapeDtypeStruct((s, HG_WIDTH), BF16), jax.ShapeDtypeStruct((s, HG_WIDTH), F32),
           jax.ShapeDtypeStruct((s // HG_CHUNK * HG_DIM, HG_WIDTH), F32)],
        in_specs=_ANY * len(after)
        + [_rows(tm, D_MODEL), _whole((1, 6 * D_MODEL)), _whole((1, D_MODEL)), _whole((IN_WIDTH, D_MODEL)),
           _whole((2, HG_WIDTH)), _whole((1, HG_DIM))],
        out_specs=[_rows(tm, D_MODEL)] + [tile] * 7 + [tile, tile, _rows(HG_TILE_CHUNKS * HG_DIM, HG_WIDTH)],
        scratch_shapes=[pltpu.VMEM((HG_HEADS, HG_DIM, HG_DIM), F32)] + [pltpu.VMEM((HG_TILE, HG_WIDTH), F32)] * 3,
        compiler_params=_params("arbitrary"),
    )(*after, x, mod, g1, w_in, logits, onorm_g)


def _in_bwd(x, dx1, mod, g1, w_in, dps):
    s = x.shape[0]
    tm = 256
    assert len(dps) == 10

    def body(x_ref, dx_ref, mod_ref, g_ref, w_ref, *rest):
        dp_refs, (gx_ref, dpb_ref, dsh_ref, dsc_ref, dg_ref) = rest[:10], rest[10:]
        pieces = [dp_refs[j][...] for j in range(4)]
        pieces += [dp_refs[4 + 2 * j][...] + dp_refs[5 + 2 * j][...] for j in range(3)]
        for j, p in enumerate(pieces):
            dpb_ref[:, j * 512:(j + 1) * 512] = p.astype(BF16)
        dh = _dot(dpb_ref[...], w_ref[...])
        xhat, rstd = _rms(x_ref[...])
        g = g_ref[...]
        scale1 = 1.0 + mod_ref[:, D_MODEL:2 * D_MODEL]
        n1 = xhat * g

        @pl.when(pl.program_id(0) == 0)
        def _():
            dsh_ref[...] = jnp.zeros_like(dsh_ref)
            dsc_ref[...] = jnp.zeros_like(dsc_ref)
            dg_ref[...] = jnp.zeros_like(dg_ref)

        dsh_ref[...] += _rowsum(dh)
        dsc_ref[...] += _rowsum(dh * n1)
        dn = dh * scale1
        dg_ref[...] += _rowsum(dn * xhat)
        gx_ref[...] = dx_ref[...] + _rms_bwd(dn * g, xhat, rstd)

    vec = _whole((1, D_MODEL))
    return pl.pallas_call(
        body, name="in_bwd", grid=(s // tm,),
        out_shape=[jax.ShapeDtypeStruct((s, D_MODEL), F32), jax.ShapeDtypeStruct((s, IN_WIDTH), BF16)]
        + [jax.ShapeDtypeStruct((1, D_MODEL), F32)] * 3,
        in_specs=[_rows(tm, D_MODEL), _rows(tm, D_MODEL), _whole((1, 6 * D_MODEL)), vec, _whole((IN_WIDTH, D_MODEL))]
        + [_rows(tm, 512)] * 10,
        out_specs=[_rows(tm, D_MODEL), _rows(tm, IN_WIDTH), vec, vec, vec],
        compiler_params=_params("arbitrary"),
    )(x, dx1, mod, g1, w_in, *dps)


HG_TILE = 512
HG_TILE_CHUNKS = HG_TILE // HG_CHUNK


def _lower_bound(lg_ref):
    return 1.0 / (1.0 + jnp.exp(lg_ref[1:2, :] - lg_ref[0:1, :]))


def _chunk_masks():
    r = lax.broadcasted_iota(jnp.int32, (HG_CHUNK, HG_CHUNK), 0)
    c = lax.broadcasted_iota(jnp.int32, (HG_CHUNK, HG_CHUNK), 1)
    return r >= c, c >= r, (r >= c).astype(F32), (c >= r).astype(F32)


def _hg_fwd_tile(q_ref, f_ref, i_ref, g_ref, lg_ref, og_ref, out_ref, o_ref, st_ref, state, qf_s, kk_s, lf_s):
    @pl.when(pl.program_id(0) == 0)
    def _():
        state[...] = jnp.zeros_like(state)

    lb = _lower_bound(lg_ref)
    f = lb + (1.0 - lb) * _sigmoid(f_ref[...])
    kk_s[...] = 1.0 - f
    lf_s[...] = jnp.log(f)
    qf_s[...] = _silu(q_ref[...])
    causal, _, tri, _ = _chunk_masks()

    def chunk(ci, carry):
        rows = pl.ds(pl.multiple_of(ci * HG_CHUNK, HG_CHUNK), HG_CHUNK)
        srows = pl.ds(pl.multiple_of(ci * HG_DIM, HG_DIM), HG_DIM)
        lf = lf_s[rows, :]
        b = _dot_f32(tri, lf)
        bl = _rowsum(lf)
        ref = 0.5 * bl
        qf, kk, v = qf_s[rows, :], kk_s[rows, :], i_ref[rows, :]
        a_in = (qf * jnp.exp(b)).astype(BF16)
        a_t = (qf * jnp.exp(b - ref)).astype(BF16)
        b_t = (kk * jnp.exp(ref - b)).astype(BF16)
        kd = kk * jnp.exp(bl - b)
        ebl = jnp.exp(bl)
        vb = v.astype(BF16)
        for h in range(HG_HEADS):
            c = slice(h * HG_DIM, (h + 1) * HG_DIM)
            st = state[h]
            st_ref[srows, c] = st
            p = jnp.where(causal, _dot_nt(a_t[:, c], b_t[:, c]), 0.0)
            o_ref[rows, c] = _dot(p.astype(BF16), vb[:, c]) + _dot_nt(a_in[:, c], st.astype(BF16))
            state[h] = st * ebl[:, c] + _dot_tn(vb[:, c], kd[:, c].astype(BF16))
        return carry

    lax.fori_loop(0, HG_TILE_CHUNKS, chunk, 0, unroll=True)
    for h in range(HG_HEADS):
        c = slice(h * HG_DIM, (h + 1) * HG_DIM)
        ohat, _ = _rms(o_ref[:, c])
        out_ref[:, c] = (ohat * og_ref[...] * _silu(g_ref[:, c])).astype(BF16)


def _hg_bwd(hq, hf, hi, hgt, logits, onorm_g, o, states, dout):
    s = hq.shape[0]
    n_tiles = s // HG_TILE

    def body(q_ref, f_ref, i_ref, g_ref, lg_ref, og_ref, o_ref, st_ref, d_ref,
             dq_ref, df_ref, di_ref, dg_ref, dog_ref, dlb_ref, dstate, qf_s, kk_s, lf_s, do_s):
        @pl.when(pl.program_id(0) == 0)
        def _():
            dstate[...] = jnp.zeros_like(dstate)
            dog_ref[...] = jnp.zeros_like(dog_ref)
            dlb_ref[...] = jnp.zeros_like(dlb_ref)

        og = og_ref[...]
        dog = jnp.zeros((1, HG_DIM), F32)
        for h in range(HG_HEADS):
            c = slice(h * HG_DIM, (h + 1) * HG_DIM)
            ohat, rstd = _rms(o_ref[:, c])
            gate = g_ref[:, c]
            d = d_ref[:, c]
            dg_ref[:, c] = (d * (ohat * og) * _dsilu(gate)).astype(BF16)
            dnormed = d * _silu(gate)
            dog += _rowsum(dnormed * ohat)
            do_s[:, c] = _rms_bwd(dnormed * og, ohat, rstd)
        dog_ref[...] += dog

        lb = _lower_bound(lg_ref)
        f = lb + (1.0 - lb) * _sigmoid(f_ref[...])
        kk_s[...] = 1.0 - f
        lf_s[...] = jnp.log(f)
        qf_s[...] = _silu(q_ref[...])
        causal, upper, tri, tri_t = _chunk_masks()

        def chunk(step, carry):
            ci = HG_TILE_CHUNKS - 1 - step
            rows = pl.ds(pl.multiple_of(ci * HG_CHUNK, HG_CHUNK), HG_CHUNK)
            srows = pl.ds(pl.multiple_of(ci * HG_DIM, HG_DIM), HG_DIM)
            lf = lf_s[rows, :]
            b = _dot_f32(tri, lf)
            bl = _rowsum(lf)
            ref = 0.5 * bl
            qf, kk, v, do = qf_s[rows, :], kk_s[rows, :], i_ref[rows, :], do_s[rows, :]
            eb, ebr, erb, ekd, ebl = jnp.exp(b), jnp.exp(b - ref), jnp.exp(ref - b), jnp.exp(bl - b), jnp.exp(bl)
            a_in, a_t, b_t, kd = qf * eb, qf * ebr, kk * erb, kk * ekd
            for h in range(HG_HEADS):
                c = slice(h * HG_DIM, (h + 1) * HG_DIM)
                st, dst = st_ref[srows, c], dstate[h]
                stb, dstb = st.astype(BF16), dst.astype(BF16)
                doh, vh = do[:, c], v[:, c]
                dob, vb = doh.astype(BF16), vh.astype(BF16)
                ain_h, at_h, bt_h, kd_h = a_in[:, c], a_t[:, c], b_t[:, c], kd[:, c]
                atb, btb = at_h.astype(BF16), bt_h.astype(BF16)
                d_ain = _dot(dob, stb)
                p_t = jnp.where(upper, _dot_nt(btb, atb), 0.0).astype(BF16)
                dp = jnp.where(causal, _dot_nt(dob, vb), 0.0).astype(BF16)
                dp_t = jnp.where(upper, _dot_nt(vb, dob), 0.0).astype(BF16)
                di_ref[rows, c] = (_dot(p_t, dob) + _dot_nt(kd_h.astype(BF16), dstb)).astype(BF16)
                d_at = _dot(dp, btb)
                d_bt = _dot(dp_t, atb)
                d_kd = _dot(vb, dstb)
                dqf = d_ain * eb[:, c] + d_at * ebr[:, c]
                dkk = d_bt * erb[:, c] + d_kd * ekd[:, c]
                db = d_ain * ain_h + d_at * atb.astype(F32) - d_bt * btb.astype(F32) - d_kd * kd_h
                dbl = _rowsum(d_kd * kd_h) + _rowsum(dst * st) * ebl[:, c]
                dstate[h] = _dot_tn(dob, ain_h.astype(BF16)) + dst * ebl[:, c]
                dlf = _dot_f32(tri_t, db) + dbl
                qv, fr = q_ref[rows, c], f_ref[rows, c]
                lbh = lb[:, c]
                sg = _sigmoid(fr)
                dfv = dlf / (lbh + (1.0 - lbh) * sg) - dkk
                df_ref[rows, c] = (dfv * (1.0 - lbh) * sg * (1.0 - sg)).astype(BF16)
                dlb_ref[:, c] += _rowsum(dfv * (1.0 - sg))
                dq_ref[rows, c] = (dqf * _dsilu(qv)).astype(BF16)
            return carry

        lax.fori_loop(0, HG_TILE_CHUNKS, chunk, 0, unroll=True)

    rev = pl.BlockSpec((HG_TILE, HG_WIDTH), lambda i: (n_tiles - 1 - i, 0))
    return pl.pallas_call(
        body, name="hg_bwd", grid=(n_tiles,),
        out_shape=[jax.ShapeDtypeStruct((s, HG_WIDTH), BF16)] * 4
        + [jax.ShapeDtypeStruct((1, HG_DIM), F32), jax.ShapeDtypeStruct((1, HG_WIDTH), F32)],
        in_specs=[rev] * 4 + [_whole((2, HG_WIDTH)), _whole((1, HG_DIM)), rev,
                              pl.BlockSpec((HG_TILE_CHUNKS * HG_DIM, HG_WIDTH), lambda i: (n_tiles - 1 - i, 0)), rev],
        out_specs=[rev] * 4 + [_whole((1, HG_DIM)), _whole((1, HG_WIDTH))],
        scratch_shapes=[pltpu.VMEM((HG_HEADS, HG_DIM, HG_DIM), F32)] + [pltpu.VMEM((HG_TILE, HG_WIDTH), F32)] * 4,
        compiler_params=_params("arbitrary"),
    )(hq, hf, hi, hgt, logits, onorm_g, o, states, dout)


TOKEN_GROUP = 16


def _att_geometry(dil, seq=0):
    per_group = TOKEN_GROUP // dil
    ub = ATT_BLOCK // per_group
    if dil == TOKEN_GROUP:
        n_blocks = 2 if seq % (2 * ub * TOKEN_GROUP) == 0 and seq > 0 else 1
    else:
        n_blocks = 4
    return per_group, ub, ATT_WIDTH if dil == 1 else 128, n_blocks


def _att_consts(dil):
    per_group, ub = _att_geometry(dil)[:2]

    def pos(i):
        return i if dil == 1 else (i % ub) * per_group + i // ub

    lane = lax.broadcasted_iota(jnp.int32, (ATT_BLOCK, 128), 1)
    qi = pos(lax.broadcasted_iota(jnp.int32, (2 * ATT_BLOCK, ATT_BLOCK), 0) % ATT_BLOCK)
    kj = pos(lax.broadcasted_iota(jnp.int32, (2 * ATT_BLOCK, ATT_BLOCK), 1))
    return lane < ATT_HEAD_DIM, kj <= qi, lambda off: kj >= qi + off


def _load_tile(ref, dil, r, c, base=0):
    per_group, ub = _att_geometry(dil)[:2]
    if dil == 1:
        return ref[base:base + ATT_BLOCK, c]
    return jnp.concatenate([ref[pl.ds(base + dil * w + r, ub, stride=TOKEN_GROUP), c] for w in range(per_group)], axis=0)


def _store_tile(ref, dil, r, c, val, base=0):
    per_group, ub = _att_geometry(dil)[:2]
    if dil == 1:
        ref[base:base + ATT_BLOCK, c] = val
        return
    for w in range(per_group):
        ref[pl.ds(base + dil * w + r, ub, stride=TOKEN_GROUP), c] = val[w * ub:(w + 1) * ub]


def _stack_heads(x2, first):
    return jnp.concatenate([jnp.where(first, x2, 0.0), jnp.where(first, 0.0, x2)], axis=0)


def _stack_bcast(x2, first):
    other = pltpu.roll(x2, ATT_HEAD_DIM, axis=1)
    return jnp.concatenate([jnp.where(first, x2, other), jnp.where(first, other, x2)], axis=0)


def _unstack_heads(st, first):
    return jnp.where(first, st[:ATT_BLOCK], st[ATT_BLOCK:])


def _att_fwd(q, k, v, dil, behind=()):
    seq, width = q.shape
    _, ub, lanes, nbs = _att_geometry(dil, seq)
    rows = ub * TOKEN_GROUP
    n_steps = seq // (nbs * rows)

    def body(q_ref, k_ref, v_ref, kp_ref, vp_ref, o_ref, lse_ref):
        first, cur_ok, _band = _att_consts(dil)
        inner_ok = _band(0)
        edge_ok = _band(jnp.where(pl.program_id(0) > 0, 0, ATT_BLOCK))
        for r in range(dil):
            for j in range(lanes // 128):
                c = slice(j * 128, (j + 1) * 128)
                kc = vc = None
                for b in range(nbs):
                    base = b * rows
                    prev_ok = edge_ok if b == 0 else inner_ok
                    if b == 0:
                        kp, vp = _load_tile(kp_ref, dil, r, c).astype(BF16), _load_tile(vp_ref, dil, r, c).astype(BF16)
                    else:
                        kp, vp = kc, vc
                    qst = _stack_heads(_load_tile(q_ref, dil, r, c, base) * ATT_SCALE, first).astype(BF16)
                    kc = _load_tile(k_ref, dil, r, c, base).astype(BF16)
                    vc = _load_tile(v_ref, dil, r, c, base).astype(BF16)
                    sc = jnp.where(cur_ok, _dot_nt(qst, kc), NEG)
                    sp = jnp.where(prev_ok, _dot_nt(qst, kp), NEG)
                    mx = jnp.max(jnp.maximum(sc, sp), axis=-1, keepdims=True)
                    pc, pp = jnp.exp(sc - mx), jnp.exp(sp - mx)
                    den = jnp.sum(pc + pp, axis=-1, keepdims=True)
                    ost = (_dot(pc.astype(BF16), vc) + _dot(pp.astype(BF16), vp)) / den
                    lse = jnp.broadcast_to(mx + jnp.log(den), (2 * ATT_BLOCK, 128))
                    _store_tile(o_ref, dil, r, c, _unstack_heads(ost, first), base)
                    _store_tile(lse_ref, dil, r, c, _unstack_heads(lse, first), base)

    slab = pl.BlockSpec((nbs * rows, lanes), lambda n, j: (n, j))
    before = pl.BlockSpec((rows, lanes), lambda n, j: (jnp.maximum(n * nbs - 1, 0), j))
    return pl.pallas_call(
        _behind(body, behind), name=f"att_fwd_d{dil}", grid=(n_steps, width // lanes),
        out_shape=[jax.ShapeDtypeStruct((seq, width), F32)] * 2,
        in_specs=_ANY * len(behind) + [slab, slab, slab, before, before], out_specs=[slab, slab],
        compiler_params=_params("arbitrary", "arbitrary"),
    )(*behind, q, k, v, k, v)


def _att_bwd(q, k, v, do, cc, lse, dil, behind=(), plus=()):
    seq, width = q.shape
    assert not plus or dil == 1
    _, ub, lanes, nbs = _att_geometry(dil, seq)
    rows = ub * TOKEN_GROUP
    n_blocks = seq // rows
    n_steps = n_blocks // nbs

    def body(q_ref, k_ref, v_ref, do_ref, cc_ref, lse_ref, qx_ref, dox_ref, ccx_ref, lsex_ref, *rest):
        plus_refs, (dq_ref, dk_ref, dv_ref, carry) = rest[:len(plus)], rest[len(plus):]
        first, cur_ok, _band = _att_consts(dil)
        step = pl.program_id(1)
        inner_ok = _band(0)
        edge_ok = _band(jnp.where(step < n_steps - 1, 0, ATT_BLOCK))

        @pl.when(step == 0)
        def _():
            carry[...] = jnp.zeros_like(carry)

        def queries(refs, r, c, base):
            q_r, do_r, lse_r, cc_r = refs
            return (_stack_heads(_load_tile(q_r, dil, r, c, base) * ATT_SCALE, first).astype(BF16),
                    _stack_heads(_load_tile(do_r, dil, r, c, base), first).astype(BF16),
                    _stack_bcast(_load_tile(lse_r, dil, r, c, base), first),
                    _stack_bcast(_load_tile(cc_r, dil, r, c, base), first))

        for r in range(dil):
            for j in range(lanes // 128):
                c = slice(j * 128, (j + 1) * 128)
                own = queries((q_ref, do_ref, lse_ref, cc_ref), r, c, 0)
                left = _load_tile(carry, dil, r, c)
                for b in range(nbs):
                    base = b * rows
                    last = b == nbs - 1
                    next_ok = edge_ok if last else inner_ok
                    if last:
                        following = queries((qx_ref, dox_ref, lsex_ref, ccx_ref), r, c, 0)
                    else:
                        following = queries((q_ref, do_ref, lse_ref, cc_ref), r, c, base + rows)
                    (qst, dost, lse_n, cc_n), (qxst, doxst, lse_x, cc_x) = own, following
                    kb = _load_tile(k_ref, dil, r, c, base).astype(BF16)
                    vb = _load_tile(v_ref, dil, r, c, base).astype(BF16)
                    p_cur = jnp.exp(jnp.where(cur_ok, _dot_nt(qst, kb), NEG) - lse_n)
                    p_next = jnp.exp(jnp.where(next_ok, _dot_nt(qxst, kb), NEG) - lse_x)
                    ds_cur = (p_cur * (_dot_nt(dost, vb) + cc_n)).astype(BF16)
                    ds_next = (p_next * (_dot_nt(doxst, vb) + cc_x)).astype(BF16)
                    dq_own = (left + _unstack_heads(_dot(ds_cur, kb), first)) * ATT_SCALE
                    dk_own = _dot_tn(ds_cur, qst) + _dot_tn(ds_next, qxst)
                    dv_own = _dot_tn(p_cur.astype(BF16), dost) + _dot_tn(p_next.astype(BF16), doxst)
                    for out_ref, val, other in zip((dq_ref, dk_ref, dv_ref), (dq_own, dk_own, dv_own),
                                                   plus_refs or (None,) * 3):
                        if other is not None:
                            val = val + _load_tile(other, dil, r, c, base)
                        _store_tile(out_ref, dil, r, c, val, base)
                    left = _unstack_heads(_dot(ds_next, kb), first)
                    own = following
                _store_tile(carry, dil, r, c, left)

    slab = pl.BlockSpec((nbs * rows, lanes), lambda j, n: (n, j))
    after = pl.BlockSpec((rows, lanes), lambda j, n: (jnp.minimum((n + 1) * nbs, n_blocks - 1), j))
    return pl.pallas_call(
        _behind(body, behind), name=f"att_bwd_d{dil}", grid=(width // lanes, n_steps),
        out_shape=[jax.ShapeDtypeStruct((seq, width), F32)] * 3,
        in_specs=_ANY * len(behind) + [slab] * 6 + [after] * 4 + [slab] * len(plus), out_specs=[slab] * 3,
        scratch_shapes=[pltpu.VMEM((rows, lanes), F32)],
        compiler_params=_params("arbitrary", "arbitrary"),
    )(*behind, q, k, v, do, cc, lse, q, do, cc, lse, *plus)


def _branch_weights(lses):
    mx = jnp.maximum(jnp.maximum(lses[0], lses[1]), lses[2])
    es = [jnp.exp(l - mx) for l in lses]
    inv = 1.0 / (es[0] + es[1] + es[2])
    return [e * inv for e in es]


def _att_combine_bwd_tile(d, att, lses, g):
    ahat, rstd = _rms(att)
    datt = _rms_bwd(d * g, ahat, rstd)
    hi = lax.broadcasted_iota(jnp.int32, (ATT_WIDTH, ATT_WIDTH), 0) // ATT_HEAD_DIM
    hj = lax.broadcasted_iota(jnp.int32, (ATT_WIDTH, ATT_WIDTH), 1) // ATT_HEAD_DIM
    same_head = (hi == hj).astype(BF16)
    prod = datt * att
    prod_hi = prod.astype(BF16)
    prod_lo = (prod - prod_hi.astype(F32)).astype(BF16)
    head_sum = _dot(prod_hi, same_head) + _dot(prod_lo, same_head)
    ws = _branch_weights(lses)
    return [w * datt for w in ws], [-w * head_sum for w in ws], _rowsum(d * ahat)


def _out_bwd(dx1, hg, at, mod, w_out, att, lses, att_g):
    s = dx1.shape[0]
    tm = 512
    n_steps = s // tm

    def body(dx_ref, hg_ref, at_ref, mod_ref, w_ref, att_ref, l0, l1, l2, g_ref,
             dhg_ref, do0, do1, do2, cc0, cc1, cc2, dw_ref, dwb_ref, dgate_ref, dg_ref):
        @pl.when(pl.program_id(0) == 0)
        def _():
            dw_ref[...] = jnp.zeros_like(dw_ref)
            dgate_ref[...] = jnp.zeros_like(dgate_ref)
            dg_ref[...] = jnp.zeros_like(dg_ref)

        hg, at, dx = hg_ref[...], at_ref[...], dx_ref[...]
        mix = _dot(hg, w_ref[0:512, :]) + _dot(at, w_ref[512:1024, :])
        dgate_ref[...] += _rowsum(dx * mix)
        dmix = (mod_ref[:, 2 * D_MODEL:3 * D_MODEL] * dx).astype(BF16)
        dhg_ref[...] = _dot_nt(dmix, w_ref[0:512, :])
        dos, ccs, dg_rows = _att_combine_bwd_tile(_dot_nt(dmix, w_ref[512:1024, :]), att_ref[...],
                                                  [l0[...], l1[...], l2[...]], g_ref[...])
        for val, ref in zip(dos + ccs, (do0, do1, do2, cc0, cc1, cc2)):
            ref[...] = val
        dg_ref[...] += dg_rows
        dw_ref[0:512, :] += _dot_tn(hg, dmix)
        dw_ref[512:1024, :] += _dot_tn(at, dmix)

        @pl.when(pl.program_id(0) == n_steps - 1)
        def _():
            dwb_ref[...] = dw_ref[...].astype(BF16)

    tile = _rows(tm, 512)
    square = _whole((D_MODEL, D_MODEL))
    return pl.pallas_call(
        body, name="out_bwd", grid=(n_steps,),
        out_shape=[jax.ShapeDtypeStruct((s, 512), F32)] * 7
        + [jax.ShapeDtypeStruct((D_MODEL, D_MODEL), F32), jax.ShapeDtypeStruct((D_MODEL, D_MODEL), BF16),
           jax.ShapeDtypeStruct((1, D_MODEL), F32), jax.ShapeDtypeStruct((1, ATT_WIDTH), F32)],
        in_specs=[_rows(tm, D_MODEL), tile, tile, _whole((1, 6 * D_MODEL)), square] + [tile] * 4 + [_whole((1, ATT_WIDTH))],
        out_specs=[tile] * 7 + [square, square, _whole((1, D_MODEL)), _whole((1, ATT_WIDTH))],
        compiler_params=_params("arbitrary"),
    )(dx1, hg, at, mod, w_out, att, *lses, att_g)


def _ffn(x, hg, outs, lses, att_g, target, mod, g2, gf, w_out, w_gu, w_down, after=()):
    s = x.shape[0]
    tm = 256

    def body(x_ref, hg_ref, o0, o1, o2, l0, l1, l2, ag_ref, t_ref, mod_ref, g2_ref, gf_ref, wo_ref, wgu_hbm, wd_hbm,
             dx_ref, h2_ref, act_ref, dau_ref, dff_ref, sums_ref, loss_ref, att_ref, at_ref, wgu, wd, au_s, sem):
        @pl.when(pl.program_id(0) == 0)
        def _():
            c1 = pltpu.make_async_copy(wgu_hbm, wgu, sem.at[0])
            c2 = pltpu.make_async_copy(wd_hbm, wd, sem.at[1])
            c1.start()
            c2.start()
            c1.wait()
            c2.wait()
            sums_ref[...] = jnp.zeros_like(sums_ref)
            loss_ref[...] = jnp.zeros_like(loss_ref)

        ws = _branch_weights([l0[...], l1[...], l2[...]])
        att = ws[0] * o0[...] + ws[1] * o1[...] + ws[2] * o2[...]
        att_ref[...] = att
        ahat, _ = _rms(att)
        at = (ahat * ag_ref[...]).astype(BF16)
        at_ref[...] = at
        mix = _dot(hg_ref[...], wo_ref[0:512, :]) + _dot(at, wo_ref[512:1024, :])
        x1v = x_ref[...] + mod_ref[:, 2 * D_MODEL:3 * D_MODEL] * mix
        xhat, rstd = _rms(x1v)
        g2 = g2_ref[...]
        n2 = xhat * g2
        scale2 = 1.0 + mod_ref[:, 4 * D_MODEL:5 * D_MODEL]
        gate2 = mod_ref[:, 5 * D_MODEL:6 * D_MODEL]
        hb = (n2 * scale2 + mod_ref[:, 3 * D_MODEL:4 * D_MODEL]).astype(BF16)
        h2_ref[...] = hb
        au_s[...] = _dot_nt(hb, wgu[...])
        a = au_s[:, 0:D_FF]
        act = (_silu(a) * au_s[:, D_FF:2 * D_FF]).astype(BF16)
        act_ref[...] = act
        ff = _dot(act, wd[...])
        x2 = x1v + gate2 * ff
        nf, rstd_f = _rms(x2)
        gfv = gf_ref[...]
        err = nf * gfv - t_ref[...]
        loss_ref[...] += 0.5 * jnp.sum(_rowsum(err * err), axis=-1, keepdims=True) * (1.0 / D_MODEL)
        dy = err * (1.0 / D_MODEL)
        dx2 = _rms_bwd(dy * gfv, nf, rstd_f)
        dffb = (gate2 * dx2).astype(BF16)
        dff_ref[...] = dffb
        dact = _dot_nt(dffb, wd[...])
        a = au_s[:, 0:D_FF]
        dau_ref[:, 0:D_FF] = (dact * au_s[:, D_FF:2 * D_FF] * _dsilu(a)).astype(BF16)
        dau_ref[:, D_FF:2 * D_FF] = (dact * _silu(a)).astype(BF16)
        dh = _dot(dau_ref[...], wgu[...])
        dn = dh * scale2
        sums_ref[0:1, :] += _rowsum(dh)
        sums_ref[1:2, :] += _rowsum(dh * n2)
        sums_ref[2:3, :] += _rowsum(dx2 * ff)
        sums_ref[3:4, :] += _rowsum(dn * xhat)
        sums_ref[4:5, :] += _rowsum(dy * nf)
        dx_ref[...] = dx2 + _rms_bwd(dn * g2, xhat, rstd)

    vec = _whole((1, D_MODEL))
    hbm = pl.BlockSpec(memory_space=pl.ANY)
    half = _rows(tm, 512)
    return pl.pallas_call(
        _behind(body, after), name="ffn", grid=(s // tm,),
        out_shape=[jax.ShapeDtypeStruct((s, D_MODEL), F32), jax.ShapeDtypeStruct((s, D_MODEL), BF16),
                   jax.ShapeDtypeStruct((s, D_FF), BF16), jax.ShapeDtypeStruct((s, 2 * D_FF), BF16),
                   jax.ShapeDtypeStruct((s, D_MODEL), BF16), jax.ShapeDtypeStruct((8, D_MODEL), F32),
                   jax.ShapeDtypeStruct((1, 128), F32), jax.ShapeDtypeStruct((s, ATT_WIDTH), F32),
                   jax.ShapeDtypeStruct((s, ATT_WIDTH), BF16)],
        in_specs=_ANY * len(after) + [_rows(tm, D_MODEL)] + [half] * 7 + [_whole((1, ATT_WIDTH)), _rows(tm, D_MODEL),
                                                                          _whole((1, 6 * D_MODEL)), vec, vec,
                                                                          _whole((D_MODEL, D_MODEL)), hbm, hbm],
        out_specs=[_rows(tm, D_MODEL), _rows(tm, D_MODEL), _rows(tm, D_FF), _rows(tm, 2 * D_FF), _rows(tm, D_MODEL),
                   _whole((8, D_MODEL)), _whole((1, 128)), half, half],
        scratch_shapes=[pltpu.VMEM((2 * D_FF, D_MODEL), BF16), pltpu.VMEM((D_FF, D_MODEL), BF16),
                        pltpu.VMEM((tm, 2 * D_FF), F32), pltpu.SemaphoreType.DMA((2,))],
        compiler_params=_params("arbitrary", vmem=V7X_VMEM_MOST),
    )(*after, x, hg, *outs, *lses, att_g, target, mod, g2, gf, w_out, w_gu, w_down)


def _weight_grad(a, b, name, rounded=False):
    s, m = a.shape
    n = b.shape[1]
    ts = min(s, 2048)
    n_steps = s // ts
    tm = max(t for t in range(128, m + 1, 128) if m % t == 0 and t * n * 4 <= 6 * 1024 * 1024)

    def body(a_ref, b_ref, o_ref, *ob_ref):
        @pl.when(pl.program_id(1) == 0)
        def _():
            o_ref[...] = jnp.zeros_like(o_ref)

        o_ref[...] += _dot_tn(a_ref[...], b_ref[...])
        if rounded:
            @pl.when(pl.program_id(1) == n_steps - 1)
            def _():
                ob_ref[0][...] = o_ref[...].astype(BF16)

    tile = pl.BlockSpec((tm, n), lambda j, i: (j, 0))
    return pl.pallas_call(
        body, name=name, grid=(m // tm, n_steps),
        out_shape=[jax.ShapeDtypeStruct((m, n), F32)] + [jax.ShapeDtypeStruct((m, n), BF16)] * rounded,
        in_specs=[pl.BlockSpec((ts, tm), lambda j, i: (i, j)), pl.BlockSpec((ts, n), lambda j, i: (i, 0))],
        out_specs=[tile] + [tile] * rounded,
        compiler_params=_params("parallel", "arbitrary"),
    )(a, b)


def _adamw_math(w, g, m, v):
    m = ADAM_B1 * m + (1.0 - ADAM_B1) * g
    v = ADAM_B2 * v + (1.0 - ADAM_B2) * (g * g)
    m_hat = m / (1.0 - ADAM_B1 ** ADAM_STEP)
    v_hat = v / (1.0 - ADAM_B2 ** ADAM_STEP)
    delta = -ADAM_LR * (m_hat / (jnp.sqrt(v_hat) + ADAM_EPS) + ADAM_WD * w)
    return delta, m, v


def _adamw_shard(where, w, m, v, partial, got, name, after=()):
    r, c = w.shape
    tr = _shard_rows(r)
    lead = partial.ndim - 2
    n_got = got.shape[0]

    def body(where_ref, *refs):
        w_ref, m_ref, v_ref, own_ref, *rest = refs[len(after):]
        got_refs, (grad_ref, d_ref, nm_ref, nv_ref) = rest[:n_got], rest[n_got:]
        g = own_ref[...]
        for g_ref in got_refs:
            g = g + g_ref[...].astype(F32)
        grad_ref[...] = g
        d_ref[...], nm_ref[...], nv_ref[...] = _adamw_math(w_ref[...], g, m_ref[...], v_ref[...])

    tile = pl.BlockSpec((tr, c), lambda i, where_ref: (i, 0))
    own = pl.BlockSpec((None,) * lead + (tr, c), lambda i, where_ref: (*[where_ref[d] for d in range(lead)], i, 0))
    part = [pl.BlockSpec((None, tr, c), functools.partial(lambda j, i, where_ref: (j, i, 0), j)) for j in range(n_got)]
    return pl.pallas_call(
        body, name=name,
        grid_spec=pltpu.PrefetchScalarGridSpec(num_scalar_prefetch=1, grid=(r // tr,),
                                               in_specs=_ANY * len(after) + [tile] * 3 + [own] + part, out_specs=[tile] * 4),
        out_shape=[jax.ShapeDtypeStruct((r, c), F32)] * 4, compiler_params=_params("parallel"),
    )(where, *after, w, m, v, partial, *[got] * n_got)


def _small_update(small_all, dmod_blocks, c_all, logits, w_ada, m_ada, v_ada, smalls, after=()):
    def body(sm_ref, dm_ref, c_ref, lg_ref, wa_ref, ma_ref, va_ref, *rest):
        ins, outs = rest[:21], rest[21:]
        _, me = _flip(0)
        tot = sm_ref[0:1, :]
        for i in range(1, N_DEV):
            tot = tot + sm_ref[i:i + 1, :]
        loss_ref = outs[0]
        loss_ref[...] = tot[:, SM_LOSS:SM_LOSS + 128]
        g_ada = lax.dot_general(_silu(c_ref[...]), dm_ref[me], (((0,), (0,)), ((), ())),
                                preferred_element_type=F32, precision=HIGHEST)
        outs[1][...] = g_ada
        outs[2][...], outs[3][...], outs[4][...] = _adamw_math(wa_ref[...], g_ada, ma_ref[...], va_ref[...])
        p0 = _lower_bound(lg_ref)
        dl0 = tot[:, SM_LB:SM_LB + 512] * p0 * (1.0 - p0)
        grads = [tot[:, SM_MOD:SM_MOD + 6 * D_MODEL], tot[:, SM_G1:SM_G1 + D_MODEL], tot[:, SM_G2:SM_G2 + D_MODEL],
                 tot[:, SM_GF:SM_GF + D_MODEL], tot[:, SM_ATT:SM_ATT + 512], tot[:, SM_HG:SM_HG + 128],
                 jnp.where(lax.broadcasted_iota(jnp.int32, (2, 512), 0) == 0, dl0, -dl0)]
        for i, g in enumerate(grads):
            w_ref, m_ref, v_ref = ins[3 * i:3 * i + 3]
            o = outs[5 + 4 * i:9 + 4 * i]
            o[0][...] = g
            o[1][...], o[2][...], o[3][...] = _adamw_math(w_ref[...], g, m_ref[...], v_ref[...])

    flat = [t for trio in smalls for t in trio]
    vm = pl.BlockSpec(memory_space=pltpu.VMEM)
    out_shape = [jax.ShapeDtypeStruct((1, 128), F32)] + [jax.ShapeDtypeStruct(w_ada.shape, F32)] * 4
    for trio in smalls:
        out_shape += [jax.ShapeDtypeStruct(trio[0].shape, F32)] * 4
    return pl.pallas_call(
        _behind(body, after), name="small_update", out_shape=out_shape,
        in_specs=_ANY * len(after) + [vm] * (7 + len(flat)), out_specs=[vm] * len(out_shape),
        compiler_params=pltpu.CompilerParams(vmem_limit_bytes=V7X_VMEM_LIMIT),
    )(*after, small_all, dmod_blocks, c_all, logits, w_ada, m_ada, v_ada, *flat)


def kernel(x, c, w_ada, b_ada, norm1_g, w_in, hg_lb_logits, hg_onorm_g, att_onorm_g, w_out, norm2_g, w_gate_up, w_down, final_g, loss_target, m_w_ada, m_b_ada, m_norm1_g, m_w_in, m_hg_lb_logits, m_hg_onorm_g, m_att_onorm_g, m_w_out, m_norm2_g, m_w_gate_up, m_w_down, m_final_g, v_w_ada, v_b_ada, v_norm1_g, v_w_in, v_hg_lb_logits, v_hg_onorm_g, v_att_onorm_g, v_w_out, v_norm2_g, v_w_gate_up, v_w_down, v_final_g):
    x2d, target = x[0], loss_target[0]
    seq = x2d.shape[0]
    assert seq % (ATT_BLOCK * max(DILATIONS)) == 0 and seq % HG_TILE == 0
    gf = final_g.reshape(1, D_MODEL)

    c_all = _exchange_small(c.reshape(8, D_MODEL // 8), None, "gather_c").reshape(N_DEV, D_MODEL)
    ada = _ada_rows(c_all, w_ada[0], b_ada)
    mod = _exchange_small(ada, 1, "scatter_mod").reshape(1, 6 * D_MODEL)

    core = lax.axis_index("c").astype(jnp.int32).reshape(1)
    chip = (2 * lax.axis_index("x") + lax.axis_index("y")).astype(jnp.int32).reshape(1)
    me = 4 * lax.axis_index("x") + 2 * lax.axis_index("y") + lax.axis_index("c")

    g_in, = _gather_weights([w_in[0].T.astype(BF16)])
    w_in_b = g_in.reshape(IN_WIDTH, D_MODEL)
    rest_shards = [w_out[0].astype(BF16), w_gate_up[0].T.astype(BF16), w_down[0].astype(BF16)]
    lands = [lax.empty((N_DEV,) + s.shape, BF16) for s in rest_shards]
    g_send, g_recv, g_srcs, g_lands, tok = _copies_start("gather_rest_start", _plan_gather_own, 12, rest_shards, lands, [w_in_b, mod])
    flight = {}

    def stage(name, *vals):
        if name == "attention_begun":
            flight["shards"], got = _copies_wait("gather_rest_wait", _plan_gather_own, g_send, g_recv, g_srcs, g_lands, list(vals))
            flight["pass"] = _copies_start("gather_pass_start", _plan_gather_pass, 9, [], got, [])
            return [flight["pass"][4]]
        if name == "mixer_weights_done":
            shapes = [(4, 2, D_MODEL // N_DEV, D_MODEL), (4, 2, 2 * D_FF // N_DEV, D_MODEL), (4, 2, D_FF // N_DEV, D_MODEL)]
            flight["grads"] = [g32.reshape(sh) for (g32, _), sh in zip(vals, shapes)]
            rounded = [g16.reshape(sh) for (_, g16), sh in zip(vals, shapes)]
            direct_lands = [lax.empty((N_DEV - 1,) + sh[2:], BF16) for sh in shapes]
            flight["direct"] = _copies_start("reduce_rest_start", _plan_reduce_direct, 21, rounded, direct_lands, [])
            return [flight["direct"][4]]
        raise ValueError(name)

    def rest_weights(after):
        s, r, _, p_lands, _ = flight["pass"]
        _, got = _copies_wait("gather_pass_wait", _plan_gather_pass, s, r, [], p_lands, [after])
        full = [lax.dynamic_update_index_in_dim(g, shard, me, 0) for g, shard in zip(got, flight["shards"])]
        return full[0].reshape(D_MODEL, D_MODEL), full[1].reshape(2 * D_FF, D_MODEL), full[2].reshape(D_FF, D_MODEL)

    grad_x, dw_in, small = _block_step(x2d, target, mod, norm1_g, hg_lb_logits, hg_onorm_g, att_onorm_g, norm2_g, gf,
                                       w_in_b, rest_weights, stage, [tok])

    g_in8 = dw_in.reshape(4, 2, IN_WIDTH // N_DEV, D_MODEL)
    in_pairs = _copies_start("reduce_pairs_in_start", _plan_reduce_pairs, 4, [g_in8], [lax.empty((4,) + g_in8.shape[2:], F32)], [])
    s, r, srcs, d_lands, _ = flight["direct"]
    _, recv_rest = _copies_wait("reduce_rest_wait", _plan_reduce_direct, s, r, srcs, d_lands, [in_pairs[4]])
    small_rows = jnp.pad(small, ((0, 0), (0, SM_PADDED - SM_WIDTH))).reshape(SM_PADDED // 128, 128)
    small_all = _exchange_small(small_rows, None, "gather_small", [in_pairs[4]]).reshape(N_DEV, SM_PADDED)[:, :SM_WIDTH]
    in_grads, got_in = _copies_wait("reduce_pairs_in_wait", _plan_reduce_pairs, in_pairs[0], in_pairs[1], in_pairs[2], in_pairs[3],
                                    [small_all])
    in_s32, in_s16 = _pair_sum(core, in_grads[0], got_in[0], "pair_sum_in")
    in_chips = _copies_start("reduce_chips_in_start", _plan_reduce_chips, 3, [in_s16], [lax.empty((3,) + in_s16.shape[1:], BF16)], [])
    big, updated = {}, []
    rest_params = [("w_out", w_out, m_w_out, v_w_out), ("w_gate_up", w_gate_up, m_w_gate_up, v_w_gate_up), ("w_down", w_down, m_w_down, v_w_down)]
    mine = jnp.concatenate([chip, core])
    for (n, w, m, v), g32, got in zip(rest_params, flight["grads"], recv_rest):
        if n == "w_gate_up":
            outs4 = _adamw_shard(mine, w[0].T, m[0].T, v[0].T, g32, got, f"adamw_{n}", [in_chips[4]])
            big[n] = [t.T[None] for t in outs4]
        else:
            outs4 = _adamw_shard(mine, w[0], m[0], v[0], g32, got, f"adamw_{n}", [in_chips[4]])
            big[n] = [t[None] for t in outs4]
        updated.append(outs4[3])
    smalls = [(b_ada, m_b_ada, v_b_ada), (norm1_g, m_norm1_g, v_norm1_g), (norm2_g, m_norm2_g, v_norm2_g),
              (gf, m_final_g.reshape(1, D_MODEL), v_final_g.reshape(1, D_MODEL)),
              (att_onorm_g, m_att_onorm_g, v_att_onorm_g), (hg_onorm_g, m_hg_onorm_g, v_hg_onorm_g),
              (hg_lb_logits, m_hg_lb_logits, v_hg_lb_logits)]
    dmod_blocks = small_all[:, :6 * D_MODEL].reshape(N_DEV, N_DEV, 6 * D_MODEL // N_DEV).transpose(1, 0, 2)
    res = _small_update(small_all, dmod_blocks, c_all, hg_lb_logits, w_ada[0], m_w_ada[0], v_w_ada[0], smalls, [in_chips[4]])
    _, recv_in = _copies_wait("reduce_chips_in_wait", _plan_reduce_chips, in_chips[0], in_chips[1], in_chips[2], in_chips[3],
                              [res[0]] + updated)
    big["w_in"] = [t.T[None] for t in _adamw_shard(chip, w_in[0].T, m_w_in[0].T, v_w_in[0].T, in_s32, recv_in[0], "adamw_w_in")]
    loss = res[0][0, 0]
    ada4 = [t[None] for t in res[1:5]]
    sm4 = {n: list(res[5 + 4 * i:9 + 4 * i]) for i, n in enumerate(["b_ada", "norm1_g", "norm2_g", "final_g", "att", "hg", "lb"])}
    sm4["final_g"] = [t.reshape(D_MODEL) for t in sm4["final_g"]]

    order = [ada4, sm4["b_ada"], sm4["norm1_g"], big["w_in"], sm4["lb"], sm4["hg"], sm4["att"], big["w_out"], sm4["norm2_g"],
             big["w_gate_up"], big["w_down"], sm4["final_g"]]
    return (loss, grad_x[None], *[o[0] for o in order], *[o[1] for o in order], *[o[2] for o in order], *[o[3] for o in order])


def _block_step(x2d, target, mod, norm1_g, hg_lb_logits, hg_onorm_g, att_onorm_g, norm2_g, gf, w_in_b, rest_weights, stage,
                after=()):
    h1, hq, hf, hi, hgt, aq, ak, av, hg_out, hg_o, hg_states = _in_fwd(x2d, mod, norm1_g, w_in_b, hg_lb_logits, hg_onorm_g, after)
    branch = [_att_fwd(aq, ak, av, d) for d in DILATIONS[:2]]
    behind = stage("attention_begun", branch[0][0], branch[1][0])
    branch += [_att_fwd(aq, ak, av, d, behind) for d in DILATIONS[2:]]
    outs = [b[0] for b in branch]
    lses = [b[1] for b in branch]
    w_out_b, w_gu_b, w_down_b = rest_weights(outs[-1])

    dx1, h2, act, dau, dff, ffn_sums, loss_part, att, att_out = _ffn(
        x2d, hg_out, outs, lses, att_onorm_g, target, mod, norm2_g, gf, w_out_b, w_gu_b, w_down_b)
    dw_gu = _weight_grad(dau, h2, "dw_gate_up", rounded=True)
    dw_down = _weight_grad(act, dff, "dw_down", rounded=True)

    back = _out_bwd(dx1, hg_out, att_out, mod, w_out_b, att, lses, att_onorm_g)
    dhg, dos, ccs = back[0], back[1:4], back[4:7]
    dw_out, dw_out_b, dgate1, d_att_g = back[7:11]
    behind = stage("mixer_weights_done", (dw_out, dw_out_b), dw_gu, dw_down)
    datt = {d: _att_bwd(aq, ak, av, dos[i], ccs[i], lses[i], d, behind) for i, d in enumerate(DILATIONS) if d > 1}
    datt[1] = _att_bwd(aq, ak, av, dos[0], ccs[0], lses[0], 1, behind, plus=datt[DILATIONS[2]])
    dhq, dhf, dhi, dhgt, d_hg_g, d_lb = _hg_bwd(hq, hf, hi, hgt, hg_lb_logits, hg_onorm_g, hg_o, hg_states, dhg)
    dps = [dhq, dhf, dhi, dhgt] + [datt[d][j] for j in range(3) for d in DILATIONS[:2]]
    grad_x, dp_b, dshift1, dscale1, d_g1 = _in_bwd(x2d, dx1, mod, norm1_g, w_in_b, dps)
    dw_in, = _weight_grad(dp_b, h1, "dw_in")
    small = jnp.concatenate([dshift1, dscale1, dgate1, ffn_sums[0:1], ffn_sums[1:2], ffn_sums[2:3], d_g1, ffn_sums[3:4],
                             ffn_sums[4:5], d_att_g, d_lb, d_hg_g, loss_part], axis=1)
    return grad_x, dw_in, small
```

```python
import functools

import jax
import jax.numpy as jnp
from jax import lax
from jax.experimental import pallas as pl
from jax.experimental.pallas import tpu as pltpu

F32 = jnp.float32
BF16 = jnp.bfloat16
HIGHEST = lax.Precision.HIGHEST
MESH = pl.DeviceIdType.MESH

D_MODEL = 1024
N_DEV = 8
HG_HEADS = 4
HG_DIM = 128
HG_WIDTH = HG_HEADS * HG_DIM
HG_CHUNK = 128
ATT_WIDTH = 512
ATT_HEAD_DIM = 64
ATT_BLOCK = 128
DILATIONS = (1, 4, 16)
ATT_SCALE = ATT_HEAD_DIM ** -0.5
D_FF = 2816
IN_WIDTH = 7 * 512
RMS_EPS = 1e-6
NEG = -1e30

ADAM_LR = 0.001
ADAM_B1 = 0.9
ADAM_B2 = 0.999
ADAM_EPS = 1e-08
ADAM_WD = 0.01
ADAM_STEP = 10

V7X_VMEM_LIMIT = 56 * 1024 * 1024
V7X_VMEM_MOST = 60 * 1024 * 1024

SM_MOD = 0
SM_G1 = 6 * D_MODEL
SM_G2 = 7 * D_MODEL
SM_GF = 8 * D_MODEL
SM_ATT = 9 * D_MODEL
SM_LB = 9 * D_MODEL + 512
SM_HG = 10 * D_MODEL
SM_LOSS = 10 * D_MODEL + 128
SM_WIDTH = 10 * D_MODEL + 256
SM_PADDED = 88 * 128


def _params(*sem, vmem=V7X_VMEM_LIMIT):
    return pltpu.CompilerParams(dimension_semantics=sem, vmem_limit_bytes=vmem)


def _dot(a, b):
    return jnp.dot(a, b, preferred_element_type=F32)


def _dot_nt(a, b):
    return lax.dot_general(a, b, (((1,), (1,)), ((), ())), preferred_element_type=F32)


def _dot_tn(a, b):
    return lax.dot_general(a, b, (((0,), (0,)), ((), ())), preferred_element_type=F32)


def _dot_f32(a, b):
    return jnp.dot(a, b, preferred_element_type=F32, precision=HIGHEST)


def _sigmoid(x):
    return 1.0 / (1.0 + jnp.exp(-x))


def _silu(x):
    return x * _sigmoid(x)


def _dsilu(x):
    s = _sigmoid(x)
    return s * (1.0 + x * (1.0 - s))


def _rms(x):
    rstd = lax.rsqrt(jnp.mean(x * x, axis=-1, keepdims=True) + RMS_EPS)
    return x * rstd, rstd


def _rms_bwd(dn, xhat, rstd):
    return rstd * (dn - xhat * jnp.mean(dn * xhat, axis=-1, keepdims=True))


def _rowsum(x):
    return jnp.sum(x, axis=0, keepdims=True)


def _rows(tm, n):
    return pl.BlockSpec((tm, n), lambda i: (i, 0))


def _whole(shape):
    return pl.BlockSpec(shape, lambda i: (0,) * len(shape))


def _mesh_pos():
    return lax.axis_index("x"), lax.axis_index("y"), lax.axis_index("c")


def _flip(k):
    x, y, c = _mesh_pos()
    px = 1 - x if k & 4 else x
    py = 1 - y if k & 2 else y
    pc = 1 - c if k & 1 else c
    return (px, py, pc), 4 * px + 2 * py + pc


_ANY = [pl.BlockSpec(memory_space=pl.ANY)]


def _behind(body, after):
    return lambda *refs: body(*refs[len(after):])


def _exchange_small(x, rows_per_peer, name, after=()):
    r_all, cols = x.shape
    r_out = r_all if rows_per_peer is None else rows_per_peer

    def body(x_ref, out_ref, send_sems, recv_sems):
        _, me = _flip(0)

        def src(pid):
            if rows_per_peer is None:
                return x_ref
            return x_ref.at[pl.ds(pl.multiple_of(pid * r_out, r_out), r_out), :]

        if rows_per_peer is None:
            out_ref[me] = x_ref[...]
        else:
            out_ref[me] = x_ref[pl.ds(pl.multiple_of(me * r_out, r_out), r_out), :]
        sends = []
        for k in range(1, N_DEV):
            dev, pid = _flip(k)
            cp = pltpu.make_async_remote_copy(src_ref=src(pid), dst_ref=out_ref.at[me], send_sem=send_sems.at[k - 1],
                                              recv_sem=recv_sems.at[k - 1], device_id=dev, device_id_type=MESH)
            cp.start()
            sends.append(cp)
        for k in range(1, N_DEV):
            dev, pid = _flip(k)
            pltpu.make_async_remote_copy(src_ref=src(pid), dst_ref=out_ref.at[pid], send_sem=send_sems.at[k - 1],
                                         recv_sem=recv_sems.at[k - 1], device_id=dev, device_id_type=MESH).wait_recv()
        for cp in sends:
            cp.wait_send()

    return pl.pallas_call(
        _behind(body, after), name=name,
        out_shape=jax.ShapeDtypeStruct((N_DEV, r_out, cols), x.dtype),
        in_specs=_ANY * len(after) + [pl.BlockSpec(memory_space=pltpu.VMEM)],
        out_specs=pl.BlockSpec(memory_space=pltpu.VMEM),
        scratch_shapes=[pltpu.SemaphoreType.DMA((N_DEV - 1,)), pltpu.SemaphoreType.DMA((N_DEV - 1,))],
    )(*after, x)


def _gather_weights(shards):
    n = len(shards)

    def body(*refs):
        xs, outs = refs[:n], refs[n:2 * n]
        send_sems, recv_sems, local_sems = refs[2 * n:]
        x, y, c = _mesh_pos()
        me, sibling = (x, y, c), (x, y, 1 - c)
        chips = [(1 - x, y), (x, 1 - y), (1 - x, 1 - y)]

        def blk(a, px, py, pc):
            return outs[a].at[4 * px + 2 * py + pc]

        def copy(a, k, block, to, src=None):
            return pltpu.make_async_remote_copy(
                src_ref=blk(a, *block) if src is None else src, dst_ref=blk(a, *block),
                send_sem=send_sems.at[a * 7 + k], recv_sem=recv_sems.at[a * 7 + k], device_id=to, device_id_type=MESH)

        mine = [pltpu.make_async_copy(xs[a], blk(a, *me), local_sems.at[a]) for a in range(n)]
        for cp in mine:
            cp.start()
        first = []
        for a in range(n):
            first.append(copy(a, 0, me, sibling, src=xs[a]))
            first += [copy(a, 1 + j, me, (*chip, c), src=xs[a]) for j, chip in enumerate(chips)]
        for cp in first:
            cp.start()
        passed = []
        for j, chip in enumerate(chips):
            for a in range(n):
                copy(a, 1 + j, (*chip, c), me).wait_recv()
                cp = copy(a, 4 + j, (*chip, c), sibling)
                cp.start()
                passed.append(cp)
        for a in range(n):
            copy(a, 0, sibling, me).wait_recv()
            for j, chip in enumerate(chips):
                copy(a, 4 + j, (*chip, 1 - c), me).wait_recv()
        for cp in first + passed:
            cp.wait_send()
        for cp in mine:
            cp.wait()

    hbm = pl.BlockSpec(memory_space=pl.ANY)
    return pl.pallas_call(
        body, name="gather_weights",
        out_shape=[jax.ShapeDtypeStruct((N_DEV,) + s.shape, s.dtype) for s in shards],
        in_specs=[hbm] * n, out_specs=[hbm] * n,
        scratch_shapes=[pltpu.SemaphoreType.DMA((7 * n,)), pltpu.SemaphoreType.DMA((7 * n,)), pltpu.SemaphoreType.DMA((n,))],
    )(*shards)


_HBM = pl.BlockSpec(memory_space=pltpu.HBM)
_SEM = pl.BlockSpec(memory_space=pltpu.SEMAPHORE)
_DATAFLOW = pltpu.SideEffectType.DATAFLOW_SIDE_EFFECTING


def _copies_start(name, plan, n_copies, srcs, lands, after):
    bufs = list(srcs) + list(lands)
    nb = len(bufs)

    def body(*refs):
        ins, send_sems, recv_sems, token = refs[:nb], refs[nb + len(after)], refs[nb + len(after) + 1], refs[-1]
        for i, (src, dst, dev) in enumerate(plan(ins[:len(srcs)], ins[len(srcs):])):
            pltpu.make_async_remote_copy(src_ref=src, dst_ref=dst, send_sem=send_sems.at[i], recv_sem=recv_sems.at[i],
                                         device_id=dev, device_id_type=MESH).start()
        token[...] = jnp.zeros_like(token)

    outs = pl.pallas_call(
        body, name=name,
        out_shape=(pltpu.SemaphoreType.DMA((n_copies,)), pltpu.SemaphoreType.DMA((n_copies,)),
                   *[pltpu.HBM(b.shape, b.dtype) for b in bufs], jax.ShapeDtypeStruct((8, 128), F32)),
        in_specs=[_HBM] * nb + [pl.BlockSpec(memory_space=pl.ANY)] * len(after),
        out_specs=(_SEM, _SEM, *[_HBM] * nb, pl.BlockSpec(memory_space=pltpu.VMEM)),
        input_output_aliases={i: 2 + i for i in range(nb)},
        compiler_params=pltpu.CompilerParams(has_side_effects=_DATAFLOW),
    )(*[pltpu.with_memory_space_constraint(b, pltpu.HBM) for b in bufs], *after)
    return outs[0], outs[1], list(outs[2:2 + len(srcs)]), list(outs[2 + len(srcs):2 + nb]), outs[-1]


def _copies_wait(name, plan, send_sems, recv_sems, srcs, lands, after):
    bufs = list(srcs) + list(lands)
    nb = len(bufs)

    def body(*refs):
        ins, send_ref, recv_ref = refs[:nb], refs[nb], refs[nb + 1]
        for i, (src, dst, dev) in enumerate(plan(ins[:len(srcs)], ins[len(srcs):])):
            cp = pltpu.make_async_remote_copy(src_ref=src, dst_ref=dst, send_sem=send_ref.at[i], recv_sem=recv_ref.at[i],
                                              device_id=dev, device_id_type=MESH)
            cp.wait_send()
            cp.wait_recv()

    outs = pl.pallas_call(
        body, name=name, out_shape=[pltpu.HBM(b.shape, b.dtype) for b in bufs],
        in_specs=[_HBM] * nb + [_SEM, _SEM] + [pl.BlockSpec(memory_space=pl.ANY)] * len(after), out_specs=[_HBM] * nb,
        input_output_aliases={i: i for i in range(nb)},
        compiler_params=pltpu.CompilerParams(has_side_effects=_DATAFLOW),
    )(*bufs, send_sems, recv_sems, *after)
    return list(outs[:len(srcs)]), list(outs[len(srcs):])


def _plan_gather_own(srcs, lands):
    _, me = _flip(0)
    return [(srcs[a], lands[a].at[me], _flip(k)[0]) for a in range(len(srcs)) for k in (1, 4, 2, 6)]


def _plan_gather_pass(srcs, lands):
    sibling = _flip(1)[0]
    plan = []
    for land in lands:
        for k in (4, 2, 6):
            block = land.at[_flip(k)[1]]
            plan.append((block, block, sibling))
    return plan


def _plan_reduce_pairs(srcs, lands):
    x, y, c = _mesh_pos()
    return [(srcs[a].at[chip, 1 - c], lands[a].at[chip], (x, y, 1 - c)) for a in range(len(srcs)) for chip in range(4)]


def _plan_reduce_chips(srcs, lands):
    plan = []
    for a in range(len(srcs)):
        for j, k in enumerate((4, 2, 6)):
            dev = _flip(k)[0]
            plan.append((srcs[a].at[2 * dev[0] + dev[1]], lands[a].at[j], dev))
    return plan


def _plan_reduce_direct(srcs, lands):
    plan = []
    for a in range(len(srcs)):
        for k in range(1, N_DEV):
            dev = _flip(k)[0]
            plan.append((srcs[a].at[2 * dev[0] + dev[1], dev[2]], lands[a].at[k - 1], dev))
    return plan


def _shard_rows(r):
    return r // 2 if r % 32 == 0 else r


def _pair_sum(core, grads, got, name):
    _, _, r, c = grads.shape
    tr = _shard_rows(r)

    def body(core_ref, a_ref, b_ref, o_ref, ob_ref):
        s = a_ref[...] + b_ref[...]
        o_ref[...] = s
        ob_ref[...] = s.astype(BF16)

    spec = pl.BlockSpec((None, tr, c), lambda i, j, core_ref: (i, j, 0))
    return pl.pallas_call(
        body, name=name,
        grid_spec=pltpu.PrefetchScalarGridSpec(
            num_scalar_prefetch=1, grid=(4, r // tr),
            in_specs=[pl.BlockSpec((None, None, tr, c), lambda i, j, core_ref: (i, core_ref[0], j, 0)), spec],
            out_specs=[spec, spec]),
        out_shape=[jax.ShapeDtypeStruct((4, r, c), F32), jax.ShapeDtypeStruct((4, r, c), BF16)],
        compiler_params=_params("parallel", "parallel"),
    )(core, grads, got)


def _ada_rows(c_all, w_ada, b_ada):
    n_cols = w_ada.shape[1]

    def body(c_ref, w_ref, b_ref, o_ref):
        _, me = _flip(0)
        bias = b_ref[:, pl.ds(pl.multiple_of(me * n_cols, 128), n_cols)]
        o_ref[...] = _dot_f32(_silu(c_ref[...]), w_ref[...]) + bias

    return pl.pallas_call(
        body, name="ada_rows", out_shape=jax.ShapeDtypeStruct((N_DEV, n_cols), F32),
        in_specs=[pl.BlockSpec(memory_space=pltpu.VMEM)] * 3, out_specs=pl.BlockSpec(memory_space=pltpu.VMEM),
    )(c_all, w_ada, b_ada)


def _in_fwd(x, mod, g1, w_in, logits, onorm_g, after=()):
    s = x.shape[0]
    tm = HG_TILE

    def body(x_ref, mod_ref, g_ref, w_ref, lg_ref, og_ref, h_ref, *rest):
        groups, (out_ref, o_ref, st_ref, state, qf_s, kk_s, lf_s) = rest[:7], rest[7:]
        xhat, _ = _rms(x_ref[...])
        h = (xhat * g_ref[...]) * (1.0 + mod_ref[:, D_MODEL:2 * D_MODEL]) + mod_ref[:, 0:D_MODEL]
        hb = h.astype(BF16)
        h_ref[...] = hb
        for j, o_ref_j in enumerate(groups):
            o_ref_j[...] = _dot_nt(hb, w_ref[j * 512:(j + 1) * 512, :])
        _hg_fwd_tile(*groups[:4], lg_ref, og_ref, out_ref, o_ref, st_ref, state, qf_s, kk_s, lf_s)

    tile = _rows(tm, 512)
    return pl.pallas_call(
        _behind(body, after), name="in_fwd", grid=(s // tm,),
        out_shape=[jax.ShapeDtypeStruct((s, D_MODEL), BF16)] + [jax.ShapeDtypeStruct((s, 512), F32)] * 7
        + [jax.ShapeDtypeStruct((s, HG_WIDTH), BF16), jax.ShapeDtypeStruct((s, HG_WIDTH), F32),
           jax.ShapeDtypeStruct((s // HG_CHUNK * HG_DIM, HG_WIDTH), F32)],
        in_specs=_ANY * len(after)
        + [_rows(tm, D_MODEL), _whole((1, 6 * D_MODEL)), _whole((1, D_MODEL)), _whole((IN_WIDTH, D_MODEL)),
           _whole((2, HG_WIDTH)), _whole((1, HG_DIM))],
        out_specs=[_rows(tm, D_MODEL)] + [tile] * 7 + [tile, tile, _rows(HG_TILE_CHUNKS * HG_DIM, HG_WIDTH)],
        scratch_shapes=[pltpu.VMEM((HG_HEADS, HG_DIM, HG_DIM), F32)] + [pltpu.VMEM((HG_TILE, HG_WIDTH), F32)] * 3,
        compiler_params=_params("arbitrary"),
    )(*after, x, mod, g1, w_in, logits, onorm_g)


def _in_bwd(x, dx1, mod, g1, w_in, dps):
    s = x.shape[0]
    tm = 256
    assert len(dps) == 7 and all(p.dtype == BF16 for p in dps)

    def body(x_ref, dx_ref, mod_ref, g_ref, w_ref, *rest):
        dp_refs, (gx_ref, dpb_ref, dsh_ref, dsc_ref, dg_ref) = rest[:7], rest[7:]
        for j, p_ref in enumerate(dp_refs):
            dpb_ref[:, j * 512:(j + 1) * 512] = p_ref[...]
        dh = _dot(dpb_ref[...], w_ref[...])
        xhat, rstd = _rms(x_ref[...])
        g = g_ref[...]
        scale1 = 1.0 + mod_ref[:, D_MODEL:2 * D_MODEL]
        n1 = xhat * g

        @pl.when(pl.program_id(0) == 0)
        def _():
            dsh_ref[...] = jnp.zeros_like(dsh_ref)
            dsc_ref[...] = jnp.zeros_like(dsc_ref)
            dg_ref[...] = jnp.zeros_like(dg_ref)

        dsh_ref[...] += _rowsum(dh)
        dsc_ref[...] += _rowsum(dh * n1)
        dn = dh * scale1
        dg_ref[...] += _rowsum(dn * xhat)
        gx_ref[...] = dx_ref[...] + _rms_bwd(dn * g, xhat, rstd)

    vec = _whole((1, D_MODEL))
    return pl.pallas_call(
        body, name="in_bwd", grid=(s // tm,),
        out_shape=[jax.ShapeDtypeStruct((s, D_MODEL), F32), jax.ShapeDtypeStruct((s, IN_WIDTH), BF16)]
        + [jax.ShapeDtypeStruct((1, D_MODEL), F32)] * 3,
        in_specs=[_rows(tm, D_MODEL), _rows(tm, D_MODEL), _whole((1, 6 * D_MODEL)), vec, _whole((IN_WIDTH, D_MODEL))]
        + [_rows(tm, 512)] * 7,
        out_specs=[_rows(tm, D_MODEL), _rows(tm, IN_WIDTH), vec, vec, vec],
        compiler_params=_params("arbitrary"),
    )(x, dx1, mod, g1, w_in, *dps)


HG_TILE = 512
HG_TILE_CHUNKS = HG_TILE // HG_CHUNK


def _lower_bound(lg_ref):
    return 1.0 / (1.0 + jnp.exp(lg_ref[1:2, :] - lg_ref[0:1, :]))


def _chunk_masks():
    r = lax.broadcasted_iota(jnp.int32, (HG_CHUNK, HG_CHUNK), 0)
    c = lax.broadcasted_iota(jnp.int32, (HG_CHUNK, HG_CHUNK), 1)
    return r >= c, c >= r, (r >= c).astype(F32), (c >= r).astype(F32)


def _hg_fwd_tile(q_ref, f_ref, i_ref, g_ref, lg_ref, og_ref, out_ref, o_ref, st_ref, state, qf_s, kk_s, lf_s):
    @pl.when(pl.program_id(0) == 0)
    def _():
        state[...] = jnp.zeros_like(state)

    lb = _lower_bound(lg_ref)
    f = lb + (1.0 - lb) * _sigmoid(f_ref[...])
    kk_s[...] = 1.0 - f
    lf_s[...] = jnp.log(f)
    qf_s[...] = _silu(q_ref[...])
    causal, _, tri, _ = _chunk_masks()

    def chunk(ci, carry):
        rows = pl.ds(pl.multiple_of(ci * HG_CHUNK, HG_CHUNK), HG_CHUNK)
        srows = pl.ds(pl.multiple_of(ci * HG_DIM, HG_DIM), HG_DIM)
        lf = lf_s[rows, :]
        b = _dot_f32(tri, lf)
        bl = _rowsum(lf)
        ref = 0.5 * bl
        qf, kk, v = qf_s[rows, :], kk_s[rows, :], i_ref[rows, :]
        a_in = (qf * jnp.exp(b)).astype(BF16)
        a_t = (qf * jnp.exp(b - ref)).astype(BF16)
        b_t = (kk * jnp.exp(ref - b)).astype(BF16)
        kd = kk * jnp.exp(bl - b)
        ebl = jnp.exp(bl)
        vb = v.astype(BF16)
        for h in range(HG_HEADS):
            c = slice(h * HG_DIM, (h + 1) * HG_DIM)
            st = state[h]
            st_ref[srows, c] = st
            p = jnp.where(causal, _dot_nt(a_t[:, c], b_t[:, c]), 0.0)
            o_ref[rows, c] = _dot(p.astype(BF16), vb[:, c]) + _dot_nt(a_in[:, c], st.astype(BF16))
            state[h] = st * ebl[:, c] + _dot_tn(vb[:, c], kd[:, c].astype(BF16))
        return carry

    lax.fori_loop(0, HG_TILE_CHUNKS, chunk, 0, unroll=True)
    for h in range(HG_HEADS):
        c = slice(h * HG_DIM, (h + 1) * HG_DIM)
        ohat, _ = _rms(o_ref[:, c])
        out_ref[:, c] = (ohat * og_ref[...] * _silu(g_ref[:, c])).astype(BF16)


def _hg_bwd(hq, hf, hi, hgt, logits, onorm_g, o, states, dout):
    s = hq.shape[0]
    n_tiles = s // HG_TILE

    def body(q_ref, f_ref, i_ref, g_ref, lg_ref, og_ref, o_ref, st_ref, d_ref,
             dq_ref, df_ref, di_ref, dg_ref, dog_ref, dlb_ref, dstate, qf_s, kk_s, lf_s, do_s):
        @pl.when(pl.program_id(0) == 0)
        def _():
            dstate[...] = jnp.zeros_like(dstate)
            dog_ref[...] = jnp.zeros_like(dog_ref)
            dlb_ref[...] = jnp.zeros_like(dlb_ref)

        og = og_ref[...]
        dog = jnp.zeros((1, HG_DIM), F32)
        for h in range(HG_HEADS):
            c = slice(h * HG_DIM, (h + 1) * HG_DIM)
            ohat, rstd = _rms(o_ref[:, c])
            gate = g_ref[:, c]
            d = d_ref[:, c]
            dg_ref[:, c] = (d * (ohat * og) * _dsilu(gate)).astype(BF16)
            dnormed = d * _silu(gate)
            dog += _rowsum(dnormed * ohat)
            do_s[:, c] = _rms_bwd(dnormed * og, ohat, rstd)
        dog_ref[...] += dog

        lb = _lower_bound(lg_ref)
        f = lb + (1.0 - lb) * _sigmoid(f_ref[...])
        kk_s[...] = 1.0 - f
        lf_s[...] = jnp.log(f)
        qf_s[...] = _silu(q_ref[...])
        causal, upper, tri, tri_t = _chunk_masks()

        def chunk(step, carry):
            ci = HG_TILE_CHUNKS - 1 - step
            rows = pl.ds(pl.multiple_of(ci * HG_CHUNK, HG_CHUNK), HG_CHUNK)
            srows = pl.ds(pl.multiple_of(ci * HG_DIM, HG_DIM), HG_DIM)
            lf = lf_s[rows, :]
            b = _dot_f32(tri, lf)
            bl = _rowsum(lf)
            ref = 0.5 * bl
            qf, kk, v, do = qf_s[rows, :], kk_s[rows, :], i_ref[rows, :], do_s[rows, :]
            eb, ebr, erb, ekd, ebl = jnp.exp(b), jnp.exp(b - ref), jnp.exp(ref - b), jnp.exp(bl - b), jnp.exp(bl)
            a_in, a_t, b_t, kd = qf * eb, qf * ebr, kk * erb, kk * ekd
            for h in range(HG_HEADS):
                c = slice(h * HG_DIM, (h + 1) * HG_DIM)
                st, dst = st_ref[srows, c], dstate[h]
                stb, dstb = st.astype(BF16), dst.astype(BF16)
                doh, vh = do[:, c], v[:, c]
                dob, vb = doh.astype(BF16), vh.astype(BF16)
                ain_h, at_h, bt_h, kd_h = a_in[:, c], a_t[:, c], b_t[:, c], kd[:, c]
                atb, btb = at_h.astype(BF16), bt_h.astype(BF16)
                d_ain = _dot(dob, stb)
                p_t = jnp.where(upper, _dot_nt(btb, atb), 0.0).astype(BF16)
                dp = jnp.where(causal, _dot_nt(dob, vb), 0.0).astype(BF16)
                dp_t = jnp.where(upper, _dot_nt(vb, dob), 0.0).astype(BF16)
                di_ref[rows, c] = (_dot(p_t, dob) + _dot_nt(kd_h.astype(BF16), dstb)).astype(BF16)
                d_at = _dot(dp, btb)
                d_bt = _dot(dp_t, atb)
                d_kd = _dot(vb, dstb)
                dqf = d_ain * eb[:, c] + d_at * ebr[:, c]
                dkk = d_bt * erb[:, c] + d_kd * ekd[:, c]
                db = d_ain * ain_h + d_at * atb.astype(F32) - d_bt * btb.astype(F32) - d_kd * kd_h
                dbl = _rowsum(d_kd * kd_h) + _rowsum(dst * st) * ebl[:, c]
                dstate[h] = _dot_tn(dob, ain_h.astype(BF16)) + dst * ebl[:, c]
                dlf = _dot_f32(tri_t, db) + dbl
                qv, fr = q_ref[rows, c], f_ref[rows, c]
                lbh = lb[:, c]
                sg = _sigmoid(fr)
                dfv = dlf / (lbh + (1.0 - lbh) * sg) - dkk
                df_ref[rows, c] = (dfv * (1.0 - lbh) * sg * (1.0 - sg)).astype(BF16)
                dlb_ref[:, c] += _rowsum(dfv * (1.0 - sg))
                dq_ref[rows, c] = (dqf * _dsilu(qv)).astype(BF16)
            return carry

        lax.fori_loop(0, HG_TILE_CHUNKS, chunk, 0, unroll=True)

    rev = pl.BlockSpec((HG_TILE, HG_WIDTH), lambda i: (n_tiles - 1 - i, 0))
    return pl.pallas_call(
        body, name="hg_bwd", grid=(n_tiles,),
        out_shape=[jax.ShapeDtypeStruct((s, HG_WIDTH), BF16)] * 4
        + [jax.ShapeDtypeStruct((1, HG_DIM), F32), jax.ShapeDtypeStruct((1, HG_WIDTH), F32)],
        in_specs=[rev] * 4 + [_whole((2, HG_WIDTH)), _whole((1, HG_DIM)), rev,
                              pl.BlockSpec((HG_TILE_CHUNKS * HG_DIM, HG_WIDTH), lambda i: (n_tiles - 1 - i, 0)), rev],
        out_specs=[rev] * 4 + [_whole((1, HG_DIM)), _whole((1, HG_WIDTH))],
        scratch_shapes=[pltpu.VMEM((HG_HEADS, HG_DIM, HG_DIM), F32)] + [pltpu.VMEM((HG_TILE, HG_WIDTH), F32)] * 4,
        compiler_params=_params("arbitrary"),
    )(hq, hf, hi, hgt, logits, onorm_g, o, states, dout)


TOKEN_GROUP = 16


def _att_geometry(dil, seq=0):
    per_group = TOKEN_GROUP // dil
    ub = ATT_BLOCK // per_group
    if dil == TOKEN_GROUP:
        n_blocks = 2 if seq % (2 * ub * TOKEN_GROUP) == 0 and seq > 0 else 1
    else:
        n_blocks = 4
    return per_group, ub, ATT_WIDTH if dil == 1 else 128, n_blocks


def _att_consts(dil):
    per_group, ub = _att_geometry(dil)[:2]

    def pos(i):
        return i if dil == 1 else (i % ub) * per_group + i // ub

    lane = lax.broadcasted_iota(jnp.int32, (ATT_BLOCK, 128), 1)
    qi = pos(lax.broadcasted_iota(jnp.int32, (2 * ATT_BLOCK, ATT_BLOCK), 0) % ATT_BLOCK)
    kj = pos(lax.broadcasted_iota(jnp.int32, (2 * ATT_BLOCK, ATT_BLOCK), 1))
    return lane < ATT_HEAD_DIM, kj <= qi, lambda off: kj >= qi + off


def _load_tile(ref, dil, r, c, base=0):
    per_group, ub = _att_geometry(dil)[:2]
    if dil == 1:
        return ref[base:base + ATT_BLOCK, c]
    return jnp.concatenate([ref[pl.ds(base + dil * w + r, ub, stride=TOKEN_GROUP), c] for w in range(per_group)], axis=0)


def _store_tile(ref, dil, r, c, val, base=0):
    per_group, ub = _att_geometry(dil)[:2]
    if dil == 1:
        ref[base:base + ATT_BLOCK, c] = val
        return
    for w in range(per_group):
        ref[pl.ds(base + dil * w + r, ub, stride=TOKEN_GROUP), c] = val[w * ub:(w + 1) * ub]


def _stack_heads(x2, first):
    return jnp.concatenate([jnp.where(first, x2, 0.0), jnp.where(first, 0.0, x2)], axis=0)


def _stack_bcast(x2, first):
    other = pltpu.roll(x2, ATT_HEAD_DIM, axis=1)
    return jnp.concatenate([jnp.where(first, x2, other), jnp.where(first, other, x2)], axis=0)


def _unstack_heads(st, first):
    return jnp.where(first, st[:ATT_BLOCK], st[ATT_BLOCK:])


def _att_fwd(q, k, v, dil, behind=()):
    seq, width = q.shape
    _, ub, lanes, nbs = _att_geometry(dil, seq)
    rows = ub * TOKEN_GROUP
    n_steps = seq // (nbs * rows)

    def body(q_ref, k_ref, v_ref, kp_ref, vp_ref, o_ref, lse_ref):
        first, cur_ok, _band = _att_consts(dil)
        inner_ok = _band(0)
        edge_ok = _band(jnp.where(pl.program_id(0) > 0, 0, ATT_BLOCK))
        for r in range(dil):
            for j in range(lanes // 128):
                c = slice(j * 128, (j + 1) * 128)
                kc = vc = None
                for b in range(nbs):
                    base = b * rows
                    prev_ok = edge_ok if b == 0 else inner_ok
                    if b == 0:
                        kp, vp = _load_tile(kp_ref, dil, r, c).astype(BF16), _load_tile(vp_ref, dil, r, c).astype(BF16)
                    else:
                        kp, vp = kc, vc
                    qst = _stack_heads(_load_tile(q_ref, dil, r, c, base) * ATT_SCALE, first).astype(BF16)
                    kc = _load_tile(k_ref, dil, r, c, base).astype(BF16)
                    vc = _load_tile(v_ref, dil, r, c, base).astype(BF16)
                    sc = jnp.where(cur_ok, _dot_nt(qst, kc), NEG)
                    sp = jnp.where(prev_ok, _dot_nt(qst, kp), NEG)
                    mx = jnp.max(jnp.maximum(sc, sp), axis=-1, keepdims=True)
                    pc, pp = jnp.exp(sc - mx), jnp.exp(sp - mx)
                    den = jnp.sum(pc + pp, axis=-1, keepdims=True)
                    ost = (_dot(pc.astype(BF16), vc) + _dot(pp.astype(BF16), vp)) / den
                    lse = jnp.broadcast_to(mx + jnp.log(den), (2 * ATT_BLOCK, 128))
                    _store_tile(o_ref, dil, r, c, _unstack_heads(ost, first), base)
                    _store_tile(lse_ref, dil, r, c, _unstack_heads(lse, first), base)

    slab = pl.BlockSpec((nbs * rows, lanes), lambda n, j: (n, j))
    before = pl.BlockSpec((rows, lanes), lambda n, j: (jnp.maximum(n * nbs - 1, 0), j))
    return pl.pallas_call(
        _behind(body, behind), name=f"att_fwd_d{dil}", grid=(n_steps, width // lanes),
        out_shape=[jax.ShapeDtypeStruct((seq, width), F32)] * 2,
        in_specs=_ANY * len(behind) + [slab, slab, slab, before, before], out_specs=[slab, slab],
        compiler_params=_params("arbitrary", "arbitrary"),
    )(*behind, q, k, v, k, v)


def _att_bwd(q, k, v, do, cc, lse, dil, behind=(), plus=()):
    seq, width = q.shape
    assert not plus or dil == 1
    plus = [a for triple in plus for a in triple]
    _, ub, lanes, nbs = _att_geometry(dil, seq)
    rows = ub * TOKEN_GROUP
    n_blocks = seq // rows
    n_steps = n_blocks // nbs

    def body(q_ref, k_ref, v_ref, do_ref, cc_ref, lse_ref, qx_ref, dox_ref, ccx_ref, lsex_ref, *rest):
        plus_refs, (dq_ref, dk_ref, dv_ref, carry) = rest[:len(plus)], rest[len(plus):]
        first, cur_ok, _band = _att_consts(dil)
        step = pl.program_id(1)
        inner_ok = _band(0)
        edge_ok = _band(jnp.where(step < n_steps - 1, 0, ATT_BLOCK))

        @pl.when(step == 0)
        def _():
            carry[...] = jnp.zeros_like(carry)

        def queries(refs, r, c, base):
            q_r, do_r, lse_r, cc_r = refs
            return (_stack_heads(_load_tile(q_r, dil, r, c, base) * ATT_SCALE, first).astype(BF16),
                    _stack_heads(_load_tile(do_r, dil, r, c, base), first).astype(BF16),
                    _stack_bcast(_load_tile(lse_r, dil, r, c, base), first),
                    _stack_bcast(_load_tile(cc_r, dil, r, c, base), first))

        for r in range(dil):
            for j in range(lanes // 128):
                c = slice(j * 128, (j + 1) * 128)
                own = queries((q_ref, do_ref, lse_ref, cc_ref), r, c, 0)
                left = _load_tile(carry, dil, r, c)
                for b in range(nbs):
                    base = b * rows
                    last = b == nbs - 1
                    next_ok = edge_ok if last else inner_ok
                    if last:
                        following = queries((qx_ref, dox_ref, lsex_ref, ccx_ref), r, c, 0)
                    else:
                        following = queries((q_ref, do_ref, lse_ref, cc_ref), r, c, base + rows)
                    (qst, dost, lse_n, cc_n), (qxst, doxst, lse_x, cc_x) = own, following
                    kb = _load_tile(k_ref, dil, r, c, base).astype(BF16)
                    vb = _load_tile(v_ref, dil, r, c, base).astype(BF16)
                    p_cur = jnp.exp(jnp.where(cur_ok, _dot_nt(qst, kb), NEG) - lse_n)
                    p_next = jnp.exp(jnp.where(next_ok, _dot_nt(qxst, kb), NEG) - lse_x)
                    ds_cur = (p_cur * (_dot_nt(dost, vb) + cc_n)).astype(BF16)
                    ds_next = (p_next * (_dot_nt(doxst, vb) + cc_x)).astype(BF16)
                    dq_own = (left + _unstack_heads(_dot(ds_cur, kb), first)) * ATT_SCALE
                    dk_own = _dot_tn(ds_cur, qst) + _dot_tn(ds_next, qxst)
                    dv_own = _dot_tn(p_cur.astype(BF16), dost) + _dot_tn(p_next.astype(BF16), doxst)
                    for i, (out_ref, val) in enumerate(zip((dq_ref, dk_ref, dv_ref), (dq_own, dk_own, dv_own))):
                        for other in plus_refs[i::3]:
                            val = val + _load_tile(other, dil, r, c, base)
                        _store_tile(out_ref, dil, r, c, val.astype(out_ref.dtype), base)
                    left = _unstack_heads(_dot(ds_next, kb), first)
                    own = following
                _store_tile(carry, dil, r, c, left)

    slab = pl.BlockSpec((nbs * rows, lanes), lambda j, n: (n, j))
    after = pl.BlockSpec((rows, lanes), lambda j, n: (jnp.minimum((n + 1) * nbs, n_blocks - 1), j))
    return pl.pallas_call(
        _behind(body, behind), name=f"att_bwd_d{dil}", grid=(width // lanes, n_steps),
        out_shape=[jax.ShapeDtypeStruct((seq, width), BF16 if plus else F32)] * 3,
        in_specs=_ANY * len(behind) + [slab] * 6 + [after] * 4 + [slab] * len(plus), out_specs=[slab] * 3,
        scratch_shapes=[pltpu.VMEM((rows, lanes), F32)],
        compiler_params=_params("arbitrary", "arbitrary"),
    )(*behind, q, k, v, do, cc, lse, q, do, cc, lse, *plus)


def _branch_weights(lses):
    mx = jnp.maximum(jnp.maximum(lses[0], lses[1]), lses[2])
    es = [jnp.exp(l - mx) for l in lses]
    inv = 1.0 / (es[0] + es[1] + es[2])
    return [e * inv for e in es]


def _att_combine_bwd_tile(d, att, lses, g):
    ahat, rstd = _rms(att)
    datt = _rms_bwd(d * g, ahat, rstd)
    hi = lax.broadcasted_iota(jnp.int32, (ATT_WIDTH, ATT_WIDTH), 0) // ATT_HEAD_DIM
    hj = lax.broadcasted_iota(jnp.int32, (ATT_WIDTH, ATT_WIDTH), 1) // ATT_HEAD_DIM
    same_head = (hi == hj).astype(BF16)
    prod = datt * att
    prod_hi = prod.astype(BF16)
    prod_lo = (prod - prod_hi.astype(F32)).astype(BF16)
    head_sum = _dot(prod_hi, same_head) + _dot(prod_lo, same_head)
    ws = _branch_weights(lses)
    return [w * datt for w in ws], [-w * head_sum for w in ws], _rowsum(d * ahat)


def _out_bwd(dx1, hg, at, mod, w_out, att, lses, att_g):
    s = dx1.shape[0]
    tm = 512
    n_steps = s // tm

    def body(dx_ref, hg_ref, at_ref, mod_ref, w_ref, att_ref, l0, l1, l2, g_ref,
             dhg_ref, do0, do1, do2, cc0, cc1, cc2, dw_ref, dwb_ref, dgate_ref, dg_ref):
        @pl.when(pl.program_id(0) == 0)
        def _():
            dw_ref[...] = jnp.zeros_like(dw_ref)
            dgate_ref[...] = jnp.zeros_like(dgate_ref)
            dg_ref[...] = jnp.zeros_like(dg_ref)

        hg, at, dx = hg_ref[...], at_ref[...], dx_ref[...]
        mix = _dot(hg, w_ref[0:512, :]) + _dot(at, w_ref[512:1024, :])
        dgate_ref[...] += _rowsum(dx * mix)
        dmix = (mod_ref[:, 2 * D_MODEL:3 * D_MODEL] * dx).astype(BF16)
        dhg_ref[...] = _dot_nt(dmix, w_ref[0:512, :])
        dos, ccs, dg_rows = _att_combine_bwd_tile(_dot_nt(dmix, w_ref[512:1024, :]), att_ref[...],
                                                  [l0[...], l1[...], l2[...]], g_ref[...])
        for val, ref in zip(dos + ccs, (do0, do1, do2, cc0, cc1, cc2)):
            ref[...] = val
        dg_ref[...] += dg_rows
        dw_ref[0:512, :] += _dot_tn(hg, dmix)
        dw_ref[512:1024, :] += _dot_tn(at, dmix)

        @pl.when(pl.program_id(0) == n_steps - 1)
        def _():
            dwb_ref[...] = dw_ref[...].astype(BF16)

    tile = _rows(tm, 512)
    square = _whole((D_MODEL, D_MODEL))
    return pl.pallas_call(
        body, name="out_bwd", grid=(n_steps,),
        out_shape=[jax.ShapeDtypeStruct((s, 512), F32)] * 7
        + [jax.ShapeDtypeStruct((D_MODEL, D_MODEL), F32), jax.ShapeDtypeStruct((D_MODEL, D_MODEL), BF16),
           jax.ShapeDtypeStruct((1, D_MODEL), F32), jax.ShapeDtypeStruct((1, ATT_WIDTH), F32)],
        in_specs=[_rows(tm, D_MODEL), tile, tile, _whole((1, 6 * D_MODEL)), square] + [tile] * 4 + [_whole((1, ATT_WIDTH))],
        out_specs=[tile] * 7 + [square, square, _whole((1, D_MODEL)), _whole((1, ATT_WIDTH))],
        compiler_params=_params("arbitrary"),
    )(dx1, hg, at, mod, w_out, att, *lses, att_g)


def _ffn(x, hg, outs, lses, att_g, target, mod, g2, gf, w_out, w_gu, w_down, after=()):
    s = x.shape[0]
    tm = 256

    def body(x_ref, hg_ref, o0, o1, o2, l0, l1, l2, ag_ref, t_ref, mod_ref, g2_ref, gf_ref, wo_ref, wgu_hbm, wd_hbm,
             dx_ref, h2_ref, act_ref, dau_ref, dff_ref, sums_ref, loss_ref, att_ref, at_ref, wgu, wd, au_s, sem):
        @pl.when(pl.program_id(0) == 0)
        def _():
            c1 = pltpu.make_async_copy(wgu_hbm, wgu, sem.at[0])
            c2 = pltpu.make_async_copy(wd_hbm, wd, sem.at[1])
            c1.start()
            c2.start()
            c1.wait()
            c2.wait()
            sums_ref[...] = jnp.zeros_like(sums_ref)
            loss_ref[...] = jnp.zeros_like(loss_ref)

        ws = _branch_weights([l0[...], l1[...], l2[...]])
        att = ws[0] * o0[...] + ws[1] * o1[...] + ws[2] * o2[...]
        att_ref[...] = att
        ahat, _ = _rms(att)
        at = (ahat * ag_ref[...]).astype(BF16)
        at_ref[...] = at
        mix = _dot(hg_ref[...], wo_ref[0:512, :]) + _dot(at, wo_ref[512:1024, :])
        x1v = x_ref[...] + mod_ref[:, 2 * D_MODEL:3 * D_MODEL] * mix
        xhat, rstd = _rms(x1v)
        g2 = g2_ref[...]
        n2 = xhat * g2
        scale2 = 1.0 + mod_ref[:, 4 * D_MODEL:5 * D_MODEL]
        gate2 = mod_ref[:, 5 * D_MODEL:6 * D_MODEL]
        hb = (n2 * scale2 + mod_ref[:, 3 * D_MODEL:4 * D_MODEL]).astype(BF16)
        h2_ref[...] = hb
        au_s[...] = _dot_nt(hb, wgu[...])
        a = au_s[:, 0:D_FF]
        act = (_silu(a) * au_s[:, D_FF:2 * D_FF]).astype(BF16)
        act_ref[...] = act
        ff = _dot(act, wd[...])
        x2 = x1v + gate2 * ff
        nf, rstd_f = _rms(x2)
        gfv = gf_ref[...]
        err = nf * gfv - t_ref[...]
        loss_ref[...] += 0.5 * jnp.sum(_rowsum(err * err), axis=-1, keepdims=True) * (1.0 / D_MODEL)
        dy = err * (1.0 / D_MODEL)
        dx2 = _rms_bwd(dy * gfv, nf, rstd_f)
        dffb = (gate2 * dx2).astype(BF16)
        dff_ref[...] = dffb
        dact = _dot_nt(dffb, wd[...])
        a = au_s[:, 0:D_FF]
        dau_ref[:, 0:D_FF] = (dact * au_s[:, D_FF:2 * D_FF] * _dsilu(a)).astype(BF16)
        dau_ref[:, D_FF:2 * D_FF] = (dact * _silu(a)).astype(BF16)
        dh = _dot(dau_ref[...], wgu[...])
        dn = dh * scale2
        sums_ref[0:1, :] += _rowsum(dh)
        sums_ref[1:2, :] += _rowsum(dh * n2)
        sums_ref[2:3, :] += _rowsum(dx2 * ff)
        sums_ref[3:4, :] += _rowsum(dn * xhat)
        sums_ref[4:5, :] += _rowsum(dy * nf)
        dx_ref[...] = dx2 + _rms_bwd(dn * g2, xhat, rstd)

    vec = _whole((1, D_MODEL))
    hbm = pl.BlockSpec(memory_space=pl.ANY)
    half = _rows(tm, 512)
    return pl.pallas_call(
        _behind(body, after), name="ffn", grid=(s // tm,),
        out_shape=[jax.ShapeDtypeStruct((s, D_MODEL), F32), jax.ShapeDtypeStruct((s, D_MODEL), BF16),
                   jax.ShapeDtypeStruct((s, D_FF), BF16), jax.ShapeDtypeStruct((s, 2 * D_FF), BF16),
                   jax.ShapeDtypeStruct((s, D_MODEL), BF16), jax.ShapeDtypeStruct((8, D_MODEL), F32),
                   jax.ShapeDtypeStruct((1, 128), F32), jax.ShapeDtypeStruct((s, ATT_WIDTH), F32),
                   jax.ShapeDtypeStruct((s, ATT_WIDTH), BF16)],
        in_specs=_ANY * len(after) + [_rows(tm, D_MODEL)] + [half] * 7 + [_whole((1, ATT_WIDTH)), _rows(tm, D_MODEL),
                                                                          _whole((1, 6 * D_MODEL)), vec, vec,
                                                                          _whole((D_MODEL, D_MODEL)), hbm, hbm],
        out_specs=[_rows(tm, D_MODEL), _rows(tm, D_MODEL), _rows(tm, D_FF), _rows(tm, 2 * D_FF), _rows(tm, D_MODEL),
                   _whole((8, D_MODEL)), _whole((1, 128)), half, half],
        scratch_shapes=[pltpu.VMEM((2 * D_FF, D_MODEL), BF16), pltpu.VMEM((D_FF, D_MODEL), BF16),
                        pltpu.VMEM((tm, 2 * D_FF), F32), pltpu.SemaphoreType.DMA((2,))],
        compiler_params=_params("arbitrary", vmem=V7X_VMEM_MOST),
    )(*after, x, hg, *outs, *lses, att_g, target, mod, g2, gf, w_out, w_gu, w_down)


def _weight_grad(a, b, name, rounded=False):
    s, m = a.shape
    n = b.shape[1]
    ts = min(s, 2048)
    n_steps = s // ts
    tm = max(t for t in range(128, m + 1, 128) if m % t == 0 and t * n * 4 <= 6 * 1024 * 1024)

    def body(a_ref, b_ref, o_ref, *ob_ref):
        @pl.when(pl.program_id(1) == 0)
        def _():
            o_ref[...] = jnp.zeros_like(o_ref)

        o_ref[...] += _dot_tn(a_ref[...], b_ref[...])
        if rounded:
            @pl.when(pl.program_id(1) == n_steps - 1)
            def _():
                ob_ref[0][...] = o_ref[...].astype(BF16)

    tile = pl.BlockSpec((tm, n), lambda j, i: (j, 0))
    return pl.pallas_call(
        body, name=name, grid=(m // tm, n_steps),
        out_shape=[jax.ShapeDtypeStruct((m, n), F32)] + [jax.ShapeDtypeStruct((m, n), BF16)] * rounded,
        in_specs=[pl.BlockSpec((ts, tm), lambda j, i: (i, j)), pl.BlockSpec((ts, n), lambda j, i: (i, 0))],
        out_specs=[tile] + [tile] * rounded,
        compiler_params=_params("parallel", "arbitrary"),
    )(a, b)


def _adamw_math(w, g, m, v):
    m = ADAM_B1 * m + (1.0 - ADAM_B1) * g
    v = ADAM_B2 * v + (1.0 - ADAM_B2) * (g * g)
    m_hat = m / (1.0 - ADAM_B1 ** ADAM_STEP)
    v_hat = v / (1.0 - ADAM_B2 ** ADAM_STEP)
    delta = -ADAM_LR * (m_hat / (jnp.sqrt(v_hat) + ADAM_EPS) + ADAM_WD * w)
    return delta, m, v


def _adamw_shard(where, w, m, v, partial, got, name, after=()):
    r, c = w.shape
    tr = _shard_rows(r)
    lead = partial.ndim - 2
    n_got = got.shape[0]

    def body(where_ref, *refs):
        w_ref, m_ref, v_ref, own_ref, *rest = refs[len(after):]
        got_refs, (grad_ref, d_ref, nm_ref, nv_ref) = rest[:n_got], rest[n_got:]
        g = own_ref[...]
        for g_ref in got_refs:
            g = g + g_ref[...].astype(F32)
        grad_ref[...] = g
        d_ref[...], nm_ref[...], nv_ref[...] = _adamw_math(w_ref[...], g, m_ref[...], v_ref[...])

    tile = pl.BlockSpec((tr, c), lambda i, where_ref: (i, 0))
    own = pl.BlockSpec((None,) * lead + (tr, c), lambda i, where_ref: (*[where_ref[d] for d in range(lead)], i, 0))
    part = [pl.BlockSpec((None, tr, c), functools.partial(lambda j, i, where_ref: (j, i, 0), j)) for j in range(n_got)]
    return pl.pallas_call(
        body, name=name,
        grid_spec=pltpu.PrefetchScalarGridSpec(num_scalar_prefetch=1, grid=(r // tr,),
                                               in_specs=_ANY * len(after) + [tile] * 3 + [own] + part, out_specs=[tile] * 4),
        out_shape=[jax.ShapeDtypeStruct((r, c), F32)] * 4, compiler_params=_params("parallel"),
    )(where, *after, w, m, v, partial, *[got] * n_got)


def _small_update(small_all, dmod_blocks, c_all, logits, w_ada, m_ada, v_ada, smalls, after=()):
    def body(sm_ref, dm_ref, c_ref, lg_ref, wa_ref, ma_ref, va_ref, *rest):
        ins, outs = rest[:21], rest[21:]
        _, me = _flip(0)
        tot = sm_ref[0:1, :]
        for i in range(1, N_DEV):
            tot = tot + sm_ref[i:i + 1, :]
        loss_ref = outs[0]
        loss_ref[...] = tot[:, SM_LOSS:SM_LOSS + 128]
        g_ada = lax.dot_general(_silu(c_ref[...]), dm_ref[me], (((0,), (0,)), ((), ())),
                                preferred_element_type=F32, precision=HIGHEST)
        outs[1][...] = g_ada
        outs[2][...], outs[3][...], outs[4][...] = _adamw_math(wa_ref[...], g_ada, ma_ref[...], va_ref[...])
        p0 = _lower_bound(lg_ref)
        dl0 = tot[:, SM_LB:SM_LB + 512] * p0 * (1.0 - p0)
        grads = [tot[:, SM_MOD:SM_MOD + 6 * D_MODEL], tot[:, SM_G1:SM_G1 + D_MODEL], tot[:, SM_G2:SM_G2 + D_MODEL],
                 tot[:, SM_GF:SM_GF + D_MODEL], tot[:, SM_ATT:SM_ATT + 512], tot[:, SM_HG:SM_HG + 128],
                 jnp.where(lax.broadcasted_iota(jnp.int32, (2, 512), 0) == 0, dl0, -dl0)]
        for i, g in enumerate(grads):
            w_ref, m_ref, v_ref = ins[3 * i:3 * i + 3]
            o = outs[5 + 4 * i:9 + 4 * i]
            o[0][...] = g
            o[1][...], o[2][...], o[3][...] = _adamw_math(w_ref[...], g, m_ref[...], v_ref[...])

    flat = [t for trio in smalls for t in trio]
    vm = pl.BlockSpec(memory_space=pltpu.VMEM)
    out_shape = [jax.ShapeDtypeStruct((1, 128), F32)] + [jax.ShapeDtypeStruct(w_ada.shape, F32)] * 4
    for trio in smalls:
        out_shape += [jax.ShapeDtypeStruct(trio[0].shape, F32)] * 4
    return pl.pallas_call(
        _behind(body, after), name="small_update", out_shape=out_shape,
        in_specs=_ANY * len(after) + [vm] * (7 + len(flat)), out_specs=[vm] * len(out_shape),
        compiler_params=pltpu.CompilerParams(vmem_limit_bytes=V7X_VMEM_LIMIT),
    )(*after, small_all, dmod_blocks, c_all, logits, w_ada, m_ada, v_ada, *flat)


def kernel(x, c, w_ada, b_ada, norm1_g, w_in, hg_lb_logits, hg_onorm_g, att_onorm_g, w_out, norm2_g, w_gate_up, w_down, final_g, loss_target, m_w_ada, m_b_ada, m_norm1_g, m_w_in, m_hg_lb_logits, m_hg_onorm_g, m_att_onorm_g, m_w_out, m_norm2_g, m_w_gate_up, m_w_down, m_final_g, v_w_ada, v_b_ada, v_norm1_g, v_w_in, v_hg_lb_logits, v_hg_onorm_g, v_att_onorm_g, v_w_out, v_norm2_g, v_w_gate_up, v_w_down, v_final_g):
    x2d, target = x[0], loss_target[0]
    seq = x2d.shape[0]
    assert seq % (ATT_BLOCK * max(DILATIONS)) == 0 and seq % HG_TILE == 0
    gf = final_g.reshape(1, D_MODEL)

    c_all = _exchange_small(c.reshape(8, D_MODEL // 8), None, "gather_c").reshape(N_DEV, D_MODEL)
    ada = _ada_rows(c_all, w_ada[0], b_ada)
    mod = _exchange_small(ada, 1, "scatter_mod").reshape(1, 6 * D_MODEL)

    core = lax.axis_index("c").astype(jnp.int32).reshape(1)
    chip = (2 * lax.axis_index("x") + lax.axis_index("y")).astype(jnp.int32).reshape(1)
    me = 4 * lax.axis_index("x") + 2 * lax.axis_index("y") + lax.axis_index("c")

    g_in, = _gather_weights([w_in[0].T.astype(BF16)])
    w_in_b = g_in.reshape(IN_WIDTH, D_MODEL)
    rest_shards = [w_out[0].astype(BF16), w_gate_up[0].T.astype(BF16), w_down[0].astype(BF16)]
    lands = [lax.empty((N_DEV,) + s.shape, BF16) for s in rest_shards]
    g_send, g_recv, g_srcs, g_lands, tok = _copies_start("gather_rest_start", _plan_gather_own, 12, rest_shards, lands, [w_in_b, mod])
    flight = {}

    def stage(name, *vals):
        if name == "attention_begun":
            flight["shards"], got = _copies_wait("gather_rest_wait", _plan_gather_own, g_send, g_recv, g_srcs, g_lands, list(vals))
            flight["pass"] = _copies_start("gather_pass_start", _plan_gather_pass, 9, [], got, [])
            return [flight["pass"][4]]
        if name == "mixer_weights_done":
            shapes = [(4, 2, D_MODEL // N_DEV, D_MODEL), (4, 2, 2 * D_FF // N_DEV, D_MODEL), (4, 2, D_FF // N_DEV, D_MODEL)]
            flight["grads"] = [g32.reshape(sh) for (g32, _), sh in zip(vals, shapes)]
            rounded = [g16.reshape(sh) for (_, g16), sh in zip(vals, shapes)]
            direct_lands = [lax.empty((N_DEV - 1,) + sh[2:], BF16) for sh in shapes]
            flight["direct"] = _copies_start("reduce_rest_start", _plan_reduce_direct, 21, rounded, direct_lands, [])
            return [flight["direct"][4]]
        raise ValueError(name)

    def rest_weights(after):
        s, r, _, p_lands, _ = flight["pass"]
        _, got = _copies_wait("gather_pass_wait", _plan_gather_pass, s, r, [], p_lands, [after])
        full = [lax.dynamic_update_index_in_dim(g, shard, me, 0) for g, shard in zip(got, flight["shards"])]
        return full[0].reshape(D_MODEL, D_MODEL), full[1].reshape(2 * D_FF, D_MODEL), full[2].reshape(D_FF, D_MODEL)

    grad_x, dw_in, small = _block_step(x2d, target, mod, norm1_g, hg_lb_logits, hg_onorm_g, att_onorm_g, norm2_g, gf,
                                       w_in_b, rest_weights, stage, [tok])

    g_in8 = dw_in.reshape(4, 2, IN_WIDTH // N_DEV, D_MODEL)
    in_pairs = _copies_start("reduce_pairs_in_start", _plan_reduce_pairs, 4, [g_in8], [lax.empty((4,) + g_in8.shape[2:], F32)], [])
    s, r, srcs, d_lands, _ = flight["direct"]
    _, recv_rest = _copies_wait("reduce_rest_wait", _plan_reduce_direct, s, r, srcs, d_lands, [in_pairs[4]])
    small_rows = jnp.pad(small, ((0, 0), (0, SM_PADDED - SM_WIDTH))).reshape(SM_PADDED // 128, 128)
    small_all = _exchange_small(small_rows, None, "gather_small", [in_pairs[4]]).reshape(N_DEV, SM_PADDED)[:, :SM_WIDTH]
    in_grads, got_in = _copies_wait("reduce_pairs_in_wait", _plan_reduce_pairs, in_pairs[0], in_pairs[1], in_pairs[2], in_pairs[3],
                                    [small_all])
    in_s32, in_s16 = _pair_sum(core, in_grads[0], got_in[0], "pair_sum_in")
    in_chips = _copies_start("reduce_chips_in_start", _plan_reduce_chips, 3, [in_s16], [lax.empty((3,) + in_s16.shape[1:], BF16)], [])
    big, updated = {}, []
    rest_params = [("w_out", w_out, m_w_out, v_w_out), ("w_gate_up", w_gate_up, m_w_gate_up, v_w_gate_up), ("w_down", w_down, m_w_down, v_w_down)]
    mine = jnp.concatenate([chip, core])
    for (n, w, m, v), g32, got in zip(rest_params, flight["grads"], recv_rest):
        if n == "w_gate_up":
            outs4 = _adamw_shard(mine, w[0].T, m[0].T, v[0].T, g32, got, f"adamw_{n}", [in_chips[4]])
            big[n] = [t.T[None] for t in outs4]
        else:
            outs4 = _adamw_shard(mine, w[0], m[0], v[0], g32, got, f"adamw_{n}", [in_chips[4]])
            big[n] = [t[None] for t in outs4]
        updated.append(outs4[3])
    smalls = [(b_ada, m_b_ada, v_b_ada), (norm1_g, m_norm1_g, v_norm1_g), (norm2_g, m_norm2_g, v_norm2_g),
              (gf, m_final_g.reshape(1, D_MODEL), v_final_g.reshape(1, D_MODEL)),
              (att_onorm_g, m_att_onorm_g, v_att_onorm_g), (hg_onorm_g, m_hg_onorm_g, v_hg_onorm_g),
              (hg_lb_logits, m_hg_lb_logits, v_hg_lb_logits)]
    dmod_blocks = small_all[:, :6 * D_MODEL].reshape(N_DEV, N_DEV, 6 * D_MODEL // N_DEV).transpose(1, 0, 2)
    res = _small_update(small_all, dmod_blocks, c_all, hg_lb_logits, w_ada[0], m_w_ada[0], v_w_ada[0], smalls, [in_chips[4]])
    _, recv_in = _copies_wait("reduce_chips_in_wait", _plan_reduce_chips, in_chips[0], in_chips[1], in_chips[2], in_chips[3],
                              [res[0]] + updated)
    big["w_in"] = [t.T[None] for t in _adamw_shard(chip, w_in[0].T, m_w_in[0].T, v_w_in[0].T, in_s32, recv_in[0], "adamw_w_in")]
    loss = res[0][0, 0]
    ada4 = [t[None] for t in res[1:5]]
    sm4 = {n: list(res[5 + 4 * i:9 + 4 * i]) for i, n in enumerate(["b_ada", "norm1_g", "norm2_g", "final_g", "att", "hg", "lb"])}
    sm4["final_g"] = [t.reshape(D_MODEL) for t in sm4["final_g"]]

    order = [ada4, sm4["b_ada"], sm4["norm1_g"], big["w_in"], sm4["lb"], sm4["hg"], sm4["att"], big["w_out"], sm4["norm2_g"],
             big["w_gate_up"], big["w_down"], sm4["final_g"]]
    return (loss, grad_x[None], *[o[0] for o in order], *[o[1] for o in order], *[o[2] for o in order], *[o[3] for o in order])


def _block_step(x2d, target, mod, norm1_g, hg_lb_logits, hg_onorm_g, att_onorm_g, norm2_g, gf, w_in_b, rest_weights, stage,
                after=()):
    h1, hq, hf, hi, hgt, aq, ak, av, hg_out, hg_o, hg_states = _in_fwd(x2d, mod, norm1_g, w_in_b, hg_lb_logits, hg_onorm_g, after)
    branch = [_att_fwd(aq, ak, av, d) for d in DILATIONS[:2]]
    behind = stage("attention_begun", branch[0][0], branch[1][0])
    branch += [_att_fwd(aq, ak, av, d, behind) for d in DILATIONS[2:]]
    outs = [b[0] for b in branch]
    lses = [b[1] for b in branch]
    w_out_b, w_gu_b, w_down_b = rest_weights(outs[-1])

    dx1, h2, act, dau, dff, ffn_sums, loss_part, att, att_out = _ffn(
        x2d, hg_out, outs, lses, att_onorm_g, target, mod, norm2_g, gf, w_out_b, w_gu_b, w_down_b)
    dw_gu = _weight_grad(dau, h2, "dw_gate_up", rounded=True)
    dw_down = _weight_grad(act, dff, "dw_down", rounded=True)

    back = _out_bwd(dx1, hg_out, att_out, mod, w_out_b, att, lses, att_onorm_g)
    dhg, dos, ccs = back[0], back[1:4], back[4:7]
    dw_out, dw_out_b, dgate1, d_att_g = back[7:11]
    behind = stage("mixer_weights_done", (dw_out, dw_out_b), dw_gu, dw_down)
    dilated = [_att_bwd(aq, ak, av, dos[i], ccs[i], lses[i], d, behind) for i, d in enumerate(DILATIONS) if d > 1]
    datt = _att_bwd(aq, ak, av, dos[0], ccs[0], lses[0], DILATIONS[0], behind, plus=dilated)
    dhq, dhf, dhi, dhgt, d_hg_g, d_lb = _hg_bwd(hq, hf, hi, hgt, hg_lb_logits, hg_onorm_g, hg_o, hg_states, dhg)
    dps = [dhq, dhf, dhi, dhgt] + list(datt)
    grad_x, dp_b, dshift1, dscale1, d_g1 = _in_bwd(x2d, dx1, mod, norm1_g, w_in_b, dps)
    dw_in, = _weight_grad(dp_b, h1, "dw_in")
    small = jnp.concatenate([dshift1, dscale1, dgate1, ffn_sums[0:1], ffn_sums[1:2], ffn_sums[2:3], d_g1, ffn_sums[3:4],
                             ffn_sums[4:5], d_att_g, d_lb, d_hg_g, loss_part], axis=1)
    return grad_x, dw_in, small
```

```python
import functools

import jax
import jax.numpy as jnp
from jax import lax
from jax.experimental import pallas as pl
from jax.experimental.pallas import tpu as pltpu

F32 = jnp.float32
BF16 = jnp.bfloat16
HIGHEST = lax.Precision.HIGHEST
MESH = pl.DeviceIdType.MESH

D_MODEL = 1024
N_DEV = 8
HG_HEADS = 4
HG_DIM = 128
HG_WIDTH = HG_HEADS * HG_DIM
HG_CHUNK = 128
ATT_WIDTH = 512
ATT_HEAD_DIM = 64
ATT_BLOCK = 128
DILATIONS = (1, 4, 16)
ATT_SCALE = ATT_HEAD_DIM ** -0.5
D_FF = 2816
IN_WIDTH = 7 * 512
RMS_EPS = 1e-6
NEG = -1e30

ADAM_LR = 0.001
ADAM_B1 = 0.9
ADAM_B2 = 0.999
ADAM_EPS = 1e-08
ADAM_WD = 0.01
ADAM_STEP = 10

V7X_VMEM_LIMIT = 56 * 1024 * 1024
V7X_VMEM_MOST = 60 * 1024 * 1024

SM_MOD = 0
SM_G1 = 6 * D_MODEL
SM_G2 = 7 * D_MODEL
SM_GF = 8 * D_MODEL
SM_ATT = 9 * D_MODEL
SM_LB = 9 * D_MODEL + 512
SM_HG = 10 * D_MODEL
SM_LOSS = 10 * D_MODEL + 128
SM_WIDTH = 10 * D_MODEL + 256
SM_PADDED = 88 * 128


def _params(*sem, vmem=V7X_VMEM_LIMIT):
    return pltpu.CompilerParams(dimension_semantics=sem, vmem_limit_bytes=vmem)


def _dot(a, b):
    return jnp.dot(a, b, preferred_element_type=F32)


def _dot_nt(a, b):
    return lax.dot_general(a, b, (((1,), (1,)), ((), ())), preferred_element_type=F32)


def _dot_tn(a, b):
    return lax.dot_general(a, b, (((0,), (0,)), ((), ())), preferred_element_type=F32)


def _dot_f32(a, b):
    return jnp.dot(a, b, preferred_element_type=F32, precision=HIGHEST)


def _sigmoid(x):
    return 1.0 / (1.0 + jnp.exp(-x))


def _silu(x):
    return x * _sigmoid(x)


def _dsilu(x):
    s = _sigmoid(x)
    return s * (1.0 + x * (1.0 - s))


def _rms(x):
    rstd = lax.rsqrt(jnp.mean(x * x, axis=-1, keepdims=True) + RMS_EPS)
    return x * rstd, rstd


def _rms_bwd(dn, xhat, rstd):
    return rstd * (dn - xhat * jnp.mean(dn * xhat, axis=-1, keepdims=True))


def _rowsum(x):
    return jnp.sum(x, axis=0, keepdims=True)


def _rows(tm, n):
    return pl.BlockSpec((tm, n), lambda i: (i, 0))


def _whole(shape):
    return pl.BlockSpec(shape, lambda i: (0,) * len(shape))


def _mesh_pos():
    return lax.axis_index("x"), lax.axis_index("y"), lax.axis_index("c")


def _flip(k):
    x, y, c = _mesh_pos()
    px = 1 - x if k & 4 else x
    py = 1 - y if k & 2 else y
    pc = 1 - c if k & 1 else c
    return (px, py, pc), 4 * px + 2 * py + pc


_ANY = [pl.BlockSpec(memory_space=pl.ANY)]


def _behind(body, after):
    return lambda *refs: body(*refs[len(after):])


def _exchange_small(x, rows_per_peer, name, after=()):
    r_all, cols = x.shape
    r_out = r_all if rows_per_peer is None else rows_per_peer

    def body(x_ref, out_ref, send_sems, recv_sems):
        _, me = _flip(0)

        def src(pid):
            if rows_per_peer is None:
                return x_ref
            return x_ref.at[pl.ds(pl.multiple_of(pid * r_out, r_out), r_out), :]

        if rows_per_peer is None:
            out_ref[me] = x_ref[...]
        else:
            out_ref[me] = x_ref[pl.ds(pl.multiple_of(me * r_out, r_out), r_out), :]
        sends = []
        for k in range(1, N_DEV):
            dev, pid = _flip(k)
            cp = pltpu.make_async_remote_copy(src_ref=src(pid), dst_ref=out_ref.at[me], send_sem=send_sems.at[k - 1],
                                              recv_sem=recv_sems.at[k - 1], device_id=dev, device_id_type=MESH)
            cp.start()
            sends.append(cp)
        for k in range(1, N_DEV):
            dev, pid = _flip(k)
            pltpu.make_async_remote_copy(src_ref=src(pid), dst_ref=out_ref.at[pid], send_sem=send_sems.at[k - 1],
                                         recv_sem=recv_sems.at[k - 1], device_id=dev, device_id_type=MESH).wait_recv()
        for cp in sends:
            cp.wait_send()

    return pl.pallas_call(
        _behind(body, after), name=name,
        out_shape=jax.ShapeDtypeStruct((N_DEV, r_out, cols), x.dtype),
        in_specs=_ANY * len(after) + [pl.BlockSpec(memory_space=pltpu.VMEM)],
        out_specs=pl.BlockSpec(memory_space=pltpu.VMEM),
        scratch_shapes=[pltpu.SemaphoreType.DMA((N_DEV - 1,)), pltpu.SemaphoreType.DMA((N_DEV - 1,))],
    )(*after, x)


def _gather_weights(shards):
    n = len(shards)

    def body(*refs):
        xs, outs = refs[:n], refs[n:2 * n]
        send_sems, recv_sems, local_sems = refs[2 * n:]
        x, y, c = _mesh_pos()
        me, sibling = (x, y, c), (x, y, 1 - c)
        chips = [(1 - x, y), (x, 1 - y), (1 - x, 1 - y)]

        def blk(a, px, py, pc):
            return outs[a].at[4 * px + 2 * py + pc]

        def copy(a, k, block, to, src=None):
            return pltpu.make_async_remote_copy(
                src_ref=blk(a, *block) if src is None else src, dst_ref=blk(a, *block),
                send_sem=send_sems.at[a * 7 + k], recv_sem=recv_sems.at[a * 7 + k], device_id=to, device_id_type=MESH)

        mine = [pltpu.make_async_copy(xs[a], blk(a, *me), local_sems.at[a]) for a in range(n)]
        for cp in mine:
            cp.start()
        first = []
        for a in range(n):
            first.append(copy(a, 0, me, sibling, src=xs[a]))
            first += [copy(a, 1 + j, me, (*chip, c), src=xs[a]) for j, chip in enumerate(chips)]
        for cp in first:
            cp.start()
        passed = []
        for j, chip in enumerate(chips):
            for a in range(n):
                copy(a, 1 + j, (*chip, c), me).wait_recv()
                cp = copy(a, 4 + j, (*chip, c), sibling)
                cp.start()
                passed.append(cp)
        for a in range(n):
            copy(a, 0, sibling, me).wait_recv()
            for j, chip in enumerate(chips):
                copy(a, 4 + j, (*chip, 1 - c), me).wait_recv()
        for cp in first + passed:
            cp.wait_send()
        for cp in mine:
            cp.wait()

    hbm = pl.BlockSpec(memory_space=pl.ANY)
    return pl.pallas_call(
        body, name="gather_weights",
        out_shape=[jax.ShapeDtypeStruct((N_DEV,) + s.shape, s.dtype) for s in shards],
        in_specs=[hbm] * n, out_specs=[hbm] * n,
        scratch_shapes=[pltpu.SemaphoreType.DMA((7 * n,)), pltpu.SemaphoreType.DMA((7 * n,)), pltpu.SemaphoreType.DMA((n,))],
    )(*shards)


_HBM = pl.BlockSpec(memory_space=pltpu.HBM)
_SEM = pl.BlockSpec(memory_space=pltpu.SEMAPHORE)
_DATAFLOW = pltpu.SideEffectType.DATAFLOW_SIDE_EFFECTING


def _copies_start(name, plan, n_copies, srcs, lands, after):
    bufs = list(srcs) + list(lands)
    nb = len(bufs)

    def body(*refs):
        ins, send_sems, recv_sems, token = refs[:nb], refs[nb + len(after)], refs[nb + len(after) + 1], refs[-1]
        for i, (src, dst, dev) in enumerate(plan(ins[:len(srcs)], ins[len(srcs):])):
            pltpu.make_async_remote_copy(src_ref=src, dst_ref=dst, send_sem=send_sems.at[i], recv_sem=recv_sems.at[i],
                                         device_id=dev, device_id_type=MESH).start()
        token[...] = jnp.zeros_like(token)

    outs = pl.pallas_call(
        body, name=name,
        out_shape=(pltpu.SemaphoreType.DMA((n_copies,)), pltpu.SemaphoreType.DMA((n_copies,)),
                   *[pltpu.HBM(b.shape, b.dtype) for b in bufs], jax.ShapeDtypeStruct((8, 128), F32)),
        in_specs=[_HBM] * nb + [pl.BlockSpec(memory_space=pl.ANY)] * len(after),
        out_specs=(_SEM, _SEM, *[_HBM] * nb, pl.BlockSpec(memory_space=pltpu.VMEM)),
        input_output_aliases={i: 2 + i for i in range(nb)},
        compiler_params=pltpu.CompilerParams(has_side_effects=_DATAFLOW),
    )(*[pltpu.with_memory_space_constraint(b, pltpu.HBM) for b in bufs], *after)
    return outs[0], outs[1], list(outs[2:2 + len(srcs)]), list(outs[2 + len(srcs):2 + nb]), outs[-1]


def _copies_wait(name, plan, send_sems, recv_sems, srcs, lands, after):
    bufs = list(srcs) + list(lands)
    nb = len(bufs)

    def body(*refs):
        ins, send_ref, recv_ref = refs[:nb], refs[nb], refs[nb + 1]
        for i, (src, dst, dev) in enumerate(plan(ins[:len(srcs)], ins[len(srcs):])):
            cp = pltpu.make_async_remote_copy(src_ref=src, dst_ref=dst, send_sem=send_ref.at[i], recv_sem=recv_ref.at[i],
                                              device_id=dev, device_id_type=MESH)
            cp.wait_send()
            cp.wait_recv()

    outs = pl.pallas_call(
        body, name=name, out_shape=[pltpu.HBM(b.shape, b.dtype) for b in bufs],
        in_specs=[_HBM] * nb + [_SEM, _SEM] + [pl.BlockSpec(memory_space=pl.ANY)] * len(after), out_specs=[_HBM] * nb,
        input_output_aliases={i: i for i in range(nb)},
        compiler_params=pltpu.CompilerParams(has_side_effects=_DATAFLOW),
    )(*bufs, send_sems, recv_sems, *after)
    return list(outs[:len(srcs)]), list(outs[len(srcs):])


def _plan_gather_own(srcs, lands):
    _, me = _flip(0)
    return [(srcs[a], lands[a].at[me], _flip(k)[0]) for a in range(len(srcs)) for k in (1, 4, 2, 6)]


def _plan_gather_pass(srcs, lands):
    sibling = _flip(1)[0]
    plan = []
    for land in lands:
        for k in (4, 2, 6):
            block = land.at[_flip(k)[1]]
            plan.append((block, block, sibling))
    return plan


def _plan_reduce_pairs(srcs, lands):
    x, y, c = _mesh_pos()
    return [(srcs[a].at[chip, 1 - c], lands[a].at[chip], (x, y, 1 - c)) for a in range(len(srcs)) for chip in range(4)]


def _plan_reduce_chips(srcs, lands):
    plan = []
    for a in range(len(srcs)):
        for j, k in enumerate((4, 2, 6)):
            dev = _flip(k)[0]
            plan.append((srcs[a].at[2 * dev[0] + dev[1]], lands[a].at[j], dev))
    return plan


def _plan_reduce_direct(srcs, lands):
    plan = []
    for a in range(len(srcs)):
        for k in range(1, N_DEV):
            dev = _flip(k)[0]
            plan.append((srcs[a].at[2 * dev[0] + dev[1], dev[2]], lands[a].at[k - 1], dev))
    return plan


def _shard_rows(r):
    return r // 2 if r % 32 == 0 else r


def _pair_sum(core, grads, got, name):
    _, _, r, c = grads.shape
    tr = _shard_rows(r)

    def body(core_ref, a_ref, b_ref, o_ref, ob_ref):
        s = a_ref[...] + b_ref[...]
        o_ref[...] = s
        ob_ref[...] = s.astype(BF16)

    spec = pl.BlockSpec((None, tr, c), lambda i, j, core_ref: (i, j, 0))
    return pl.pallas_call(
        body, name=name,
        grid_spec=pltpu.PrefetchScalarGridSpec(
            num_scalar_prefetch=1, grid=(4, r // tr),
            in_specs=[pl.BlockSpec((None, None, tr, c), lambda i, j, core_ref: (i, core_ref[0], j, 0)), spec],
            out_specs=[spec, spec]),
        out_shape=[jax.ShapeDtypeStruct((4, r, c), F32), jax.ShapeDtypeStruct((4, r, c), BF16)],
        compiler_params=_params("parallel", "parallel"),
    )(core, grads, got)


def _ada_rows(c_all, w_ada, b_ada):
    n_cols = w_ada.shape[1]

    def body(c_ref, w_ref, b_ref, o_ref):
        _, me = _flip(0)
        bias = b_ref[:, pl.ds(pl.multiple_of(me * n_cols, 128), n_cols)]
        o_ref[...] = _dot_f32(_silu(c_ref[...]), w_ref[...]) + bias

    return pl.pallas_call(
        body, name="ada_rows", out_shape=jax.ShapeDtypeStruct((N_DEV, n_cols), F32),
        in_specs=[pl.BlockSpec(memory_space=pltpu.VMEM)] * 3, out_specs=pl.BlockSpec(memory_space=pltpu.VMEM),
    )(c_all, w_ada, b_ada)


def _in_fwd(x, mod, g1, w_in, logits, onorm_g, after=()):
    s = x.shape[0]
    tm = HG_TILE

    def body(x_ref, mod_ref, g_ref, w_ref, lg_ref, og_ref, h_ref, *rest):
        groups, (out_ref, o_ref, st_ref, state, qf_s, kk_s, lf_s) = rest[:7], rest[7:]
        xhat, _ = _rms(x_ref[...])
        h = (xhat * g_ref[...]) * (1.0 + mod_ref[:, D_MODEL:2 * D_MODEL]) + mod_ref[:, 0:D_MODEL]
        hb = h.astype(BF16)
        h_ref[...] = hb
        for j, o_ref_j in enumerate(groups):
            o_ref_j[...] = _dot_nt(hb, w_ref[j * 512:(j + 1) * 512, :])
        _hg_fwd_tile(*groups[:4], lg_ref, og_ref, out_ref, o_ref, st_ref, state, qf_s, kk_s, lf_s)

    tile = _rows(tm, 512)
    return pl.pallas_call(
        _behind(body, after), name="in_fwd", grid=(s // tm,),
        out_shape=[jax.ShapeDtypeStruct((s, D_MODEL), BF16)] + [jax.ShapeDtypeStruct((s, 512), F32)] * 7
        + [jax.ShapeDtypeStruct((s, HG_WIDTH), BF16), jax.ShapeDtypeStruct((s, HG_WIDTH), F32),
           jax.ShapeDtypeStruct((s // HG_CHUNK * HG_DIM, HG_WIDTH), F32)],
        in_specs=_ANY * len(after)
        + [_rows(tm, D_MODEL), _whole((1, 6 * D_MODEL)), _whole((1, D_MODEL)), _whole((IN_WIDTH, D_MODEL)),
           _whole((2, HG_WIDTH)), _whole((1, HG_DIM))],
        out_specs=[_rows(tm, D_MODEL)] + [tile] * 7 + [tile, tile, _rows(HG_TILE_CHUNKS * HG_DIM, HG_WIDTH)],
        scratch_shapes=[pltpu.VMEM((HG_HEADS, HG_DIM, HG_DIM), F32)] + [pltpu.VMEM((HG_TILE, HG_WIDTH), F32)] * 3,
        compiler_params=_params("arbitrary"),
    )(*after, x, mod, g1, w_in, logits, onorm_g)


def _in_bwd(x, dx1, mod, g1, w_in, dps):
    s = x.shape[0]
    tm = 512
    assert len(dps) == 7 and all(p.dtype == BF16 for p in dps)

    def body(x_ref, dx_ref, mod_ref, g_ref, w_ref, *rest):
        dp_refs, (gx_ref, dpb_ref, dsh_ref, dsc_ref, dg_ref) = rest[:7], rest[7:]
        for j, p_ref in enumerate(dp_refs):
            dpb_ref[:, j * 512:(j + 1) * 512] = p_ref[...]
        dh = _dot(dpb_ref[...], w_ref[...])
        xhat, rstd = _rms(x_ref[...])
        g = g_ref[...]
        scale1 = 1.0 + mod_ref[:, D_MODEL:2 * D_MODEL]
        n1 = xhat * g

        @pl.when(pl.program_id(0) == 0)
        def _():
            dsh_ref[...] = jnp.zeros_like(dsh_ref)
            dsc_ref[...] = jnp.zeros_like(dsc_ref)
            dg_ref[...] = jnp.zeros_like(dg_ref)

        dsh_ref[...] += _rowsum(dh)
        dsc_ref[...] += _rowsum(dh * n1)
        dn = dh * scale1
        dg_ref[...] += _rowsum(dn * xhat)
        gx_ref[...] = dx_ref[...] + _rms_bwd(dn * g, xhat, rstd)

    vec = _whole((1, D_MODEL))
    return pl.pallas_call(
        body, name="in_bwd", grid=(s // tm,),
        out_shape=[jax.ShapeDtypeStruct((s, D_MODEL), F32), jax.ShapeDtypeStruct((s, IN_WIDTH), BF16)]
        + [jax.ShapeDtypeStruct((1, D_MODEL), F32)] * 3,
        in_specs=[_rows(tm, D_MODEL), _rows(tm, D_MODEL), _whole((1, 6 * D_MODEL)), vec, _whole((IN_WIDTH, D_MODEL))]
        + [_rows(tm, 512)] * 7,
        out_specs=[_rows(tm, D_MODEL), _rows(tm, IN_WIDTH), vec, vec, vec],
        compiler_params=_params("arbitrary"),
    )(x, dx1, mod, g1, w_in, *dps)


HG_TILE = 512
HG_TILE_CHUNKS = HG_TILE // HG_CHUNK


def _lower_bound(lg_ref):
    return 1.0 / (1.0 + jnp.exp(lg_ref[1:2, :] - lg_ref[0:1, :]))


def _chunk_masks():
    r = lax.broadcasted_iota(jnp.int32, (HG_CHUNK, HG_CHUNK), 0)
    c = lax.broadcasted_iota(jnp.int32, (HG_CHUNK, HG_CHUNK), 1)
    return r >= c, c >= r, (r >= c).astype(F32), (c >= r).astype(F32)


def _hg_fwd_tile(q_ref, f_ref, i_ref, g_ref, lg_ref, og_ref, out_ref, o_ref, st_ref, state, qf_s, kk_s, lf_s):
    @pl.when(pl.program_id(0) == 0)
    def _():
        state[...] = jnp.zeros_like(state)

    lb = _lower_bound(lg_ref)
    f = lb + (1.0 - lb) * _sigmoid(f_ref[...])
    kk_s[...] = 1.0 - f
    lf_s[...] = jnp.log(f)
    qf_s[...] = _silu(q_ref[...])
    causal, _, tri, _ = _chunk_masks()

    def chunk(ci, carry):
        rows = pl.ds(pl.multiple_of(ci * HG_CHUNK, HG_CHUNK), HG_CHUNK)
        srows = pl.ds(pl.multiple_of(ci * HG_DIM, HG_DIM), HG_DIM)
        lf = lf_s[rows, :]
        b = _dot_f32(tri, lf)
        bl = _rowsum(lf)
        ref = 0.5 * bl
        qf, kk, v = qf_s[rows, :], kk_s[rows, :], i_ref[rows, :]
        a_in = (qf * jnp.exp(b)).astype(BF16)
        a_t = (qf * jnp.exp(b - ref)).astype(BF16)
        b_t = (kk * jnp.exp(ref - b)).astype(BF16)
        kd = kk * jnp.exp(bl - b)
        ebl = jnp.exp(bl)
        vb = v.astype(BF16)
        for h in range(HG_HEADS):
            c = slice(h * HG_DIM, (h + 1) * HG_DIM)
            st = state[h]
            st_ref[srows, c] = st
            p = jnp.where(causal, _dot_nt(a_t[:, c], b_t[:, c]), 0.0)
            o_ref[rows, c] = _dot(p.astype(BF16), vb[:, c]) + _dot_nt(a_in[:, c], st.astype(BF16))
            state[h] = st * ebl[:, c] + _dot_tn(vb[:, c], kd[:, c].astype(BF16))
        return carry

    lax.fori_loop(0, HG_TILE_CHUNKS, chunk, 0, unroll=True)
    for h in range(HG_HEADS):
        c = slice(h * HG_DIM, (h + 1) * HG_DIM)
        ohat, _ = _rms(o_ref[:, c])
        out_ref[:, c] = (ohat * og_ref[...] * _silu(g_ref[:, c])).astype(BF16)


def _hg_bwd(hq, hf, hi, hgt, logits, onorm_g, o, states, dout):
    s = hq.shape[0]
    n_tiles = s // HG_TILE

    def body(q_ref, f_ref, i_ref, g_ref, lg_ref, og_ref, o_ref, st_ref, d_ref,
             dq_ref, df_ref, di_ref, dg_ref, dog_ref, dlb_ref, dstate, qf_s, kk_s, lf_s, do_s):
        @pl.when(pl.program_id(0) == 0)
        def _():
            dstate[...] = jnp.zeros_like(dstate)
            dog_ref[...] = jnp.zeros_like(dog_ref)
            dlb_ref[...] = jnp.zeros_like(dlb_ref)

        og = og_ref[...]
        dog = jnp.zeros((1, HG_DIM), F32)
        for h in range(HG_HEADS):
            c = slice(h * HG_DIM, (h + 1) * HG_DIM)
            ohat, rstd = _rms(o_ref[:, c])
            gate = g_ref[:, c]
            d = d_ref[:, c]
            dg_ref[:, c] = (d * (ohat * og) * _dsilu(gate)).astype(BF16)
            dnormed = d * _silu(gate)
            dog += _rowsum(dnormed * ohat)
            do_s[:, c] = _rms_bwd(dnormed * og, ohat, rstd)
        dog_ref[...] += dog

        lb = _lower_bound(lg_ref)
        f = lb + (1.0 - lb) * _sigmoid(f_ref[...])
        kk_s[...] = 1.0 - f
        lf_s[...] = jnp.log(f)
        qf_s[...] = _silu(q_ref[...])
        causal, upper, tri, tri_t = _chunk_masks()

        def chunk(step, carry):
            ci = HG_TILE_CHUNKS - 1 - step
            rows = pl.ds(pl.multiple_of(ci * HG_CHUNK, HG_CHUNK), HG_CHUNK)
            srows = pl.ds(pl.multiple_of(ci * HG_DIM, HG_DIM), HG_DIM)
            lf = lf_s[rows, :]
            b = _dot_f32(tri, lf)
            bl = _rowsum(lf)
            ref = 0.5 * bl
            qf, kk, v, do = qf_s[rows, :], kk_s[rows, :], i_ref[rows, :], do_s[rows, :]
            eb, ebr, erb, ekd, ebl = jnp.exp(b), jnp.exp(b - ref), jnp.exp(ref - b), jnp.exp(bl - b), jnp.exp(bl)
            a_in, a_t, b_t, kd = qf * eb, qf * ebr, kk * erb, kk * ekd
            for h in range(HG_HEADS):
                c = slice(h * HG_DIM, (h + 1) * HG_DIM)
                st, dst = st_ref[srows, c], dstate[h]
                stb, dstb = st.astype(BF16), dst.astype(BF16)
                doh, vh = do[:, c], v[:, c]
                dob, vb = doh.astype(BF16), vh.astype(BF16)
                ain_h, at_h, bt_h, kd_h = a_in[:, c], a_t[:, c], b_t[:, c], kd[:, c]
                atb, btb = at_h.astype(BF16), bt_h.astype(BF16)
                d_ain = _dot(dob, stb)
                p_t = jnp.where(upper, _dot_nt(btb, atb), 0.0).astype(BF16)
                dp = jnp.where(causal, _dot_nt(dob, vb), 0.0).astype(BF16)
                dp_t = jnp.where(upper, _dot_nt(vb, dob), 0.0).astype(BF16)
                di_ref[rows, c] = (_dot(p_t, dob) + _dot_nt(kd_h.astype(BF16), dstb)).astype(BF16)
                d_at = _dot(dp, btb)
                d_bt = _dot(dp_t, atb)
                d_kd = _dot(vb, dstb)
                dqf = d_ain * eb[:, c] + d_at * ebr[:, c]
                dkk = d_bt * erb[:, c] + d_kd * ekd[:, c]
                db = d_ain * ain_h + d_at * atb.astype(F32) - d_bt * btb.astype(F32) - d_kd * kd_h
                dbl = _rowsum(d_kd * kd_h) + _rowsum(dst * st) * ebl[:, c]
                dstate[h] = _dot_tn(dob, ain_h.astype(BF16)) + dst * ebl[:, c]
                dlf = _dot_f32(tri_t, db) + dbl
                qv, fr = q_ref[rows, c], f_ref[rows, c]
                lbh = lb[:, c]
                sg = _sigmoid(fr)
                dfv = dlf / (lbh + (1.0 - lbh) * sg) - dkk
                df_ref[rows, c] = (dfv * (1.0 - lbh) * sg * (1.0 - sg)).astype(BF16)
                dlb_ref[:, c] += _rowsum(dfv * (1.0 - sg))
                dq_ref[rows, c] = (dqf * _dsilu(qv)).astype(BF16)
            return carry

        lax.fori_loop(0, HG_TILE_CHUNKS, chunk, 0, unroll=True)

    rev = pl.BlockSpec((HG_TILE, HG_WIDTH), lambda i: (n_tiles - 1 - i, 0))
    return pl.pallas_call(
        body, name="hg_bwd", grid=(n_tiles,),
        out_shape=[jax.ShapeDtypeStruct((s, HG_WIDTH), BF16)] * 4
        + [jax.ShapeDtypeStruct((1, HG_DIM), F32), jax.ShapeDtypeStruct((1, HG_WIDTH), F32)],
        in_specs=[rev] * 4 + [_whole((2, HG_WIDTH)), _whole((1, HG_DIM)), rev,
                              pl.BlockSpec((HG_TILE_CHUNKS * HG_DIM, HG_WIDTH), lambda i: (n_tiles - 1 - i, 0)), rev],
        out_specs=[rev] * 4 + [_whole((1, HG_DIM)), _whole((1, HG_WIDTH))],
        scratch_shapes=[pltpu.VMEM((HG_HEADS, HG_DIM, HG_DIM), F32)] + [pltpu.VMEM((HG_TILE, HG_WIDTH), F32)] * 4,
        compiler_params=_params("arbitrary"),
    )(hq, hf, hi, hgt, logits, onorm_g, o, states, dout)


TOKEN_GROUP = 16


def _att_geometry(dil, seq=0):
    per_group = TOKEN_GROUP // dil
    ub = ATT_BLOCK // per_group
    if dil == TOKEN_GROUP:
        n_blocks = 2 if seq % (2 * ub * TOKEN_GROUP) == 0 and seq > 0 else 1
    else:
        n_blocks = 4
    return per_group, ub, ATT_WIDTH if dil == 1 else 128, n_blocks


def _att_consts(dil):
    per_group, ub = _att_geometry(dil)[:2]

    def pos(i):
        return i if dil == 1 else (i % ub) * per_group + i // ub

    lane = lax.broadcasted_iota(jnp.int32, (ATT_BLOCK, 128), 1)
    qi = pos(lax.broadcasted_iota(jnp.int32, (2 * ATT_BLOCK, ATT_BLOCK), 0) % ATT_BLOCK)
    kj = pos(lax.broadcasted_iota(jnp.int32, (2 * ATT_BLOCK, ATT_BLOCK), 1))
    return lane < ATT_HEAD_DIM, kj <= qi, lambda off: kj >= qi + off


def _load_tile(ref, dil, r, c, base=0):
    per_group, ub = _att_geometry(dil)[:2]
    if dil == 1:
        return ref[base:base + ATT_BLOCK, c]
    return jnp.concatenate([ref[pl.ds(base + dil * w + r, ub, stride=TOKEN_GROUP), c] for w in range(per_group)], axis=0)


def _store_tile(ref, dil, r, c, val, base=0):
    per_group, ub = _att_geometry(dil)[:2]
    if dil == 1:
        ref[base:base + ATT_BLOCK, c] = val
        return
    for w in range(per_group):
        ref[pl.ds(base + dil * w + r, ub, stride=TOKEN_GROUP), c] = val[w * ub:(w + 1) * ub]


def _stack_heads(x2, first):
    return jnp.concatenate([jnp.where(first, x2, 0.0), jnp.where(first, 0.0, x2)], axis=0)


def _stack_bcast(x2, first):
    other = pltpu.roll(x2, ATT_HEAD_DIM, axis=1)
    return jnp.concatenate([jnp.where(first, x2, other), jnp.where(first, other, x2)], axis=0)


def _unstack_heads(st, first):
    return jnp.where(first, st[:ATT_BLOCK], st[ATT_BLOCK:])


def _att_fwd(q, k, v, dil, behind=()):
    seq, width = q.shape
    _, ub, lanes, nbs = _att_geometry(dil, seq)
    rows = ub * TOKEN_GROUP
    n_steps = seq // (nbs * rows)

    def body(q_ref, k_ref, v_ref, kp_ref, vp_ref, o_ref, lse_ref):
        first, cur_ok, _band = _att_consts(dil)
        inner_ok = _band(0)
        edge_ok = _band(jnp.where(pl.program_id(0) > 0, 0, ATT_BLOCK))
        for r in range(dil):
            for j in range(lanes // 128):
                c = slice(j * 128, (j + 1) * 128)
                kc = vc = None
                for b in range(nbs):
                    base = b * rows
                    prev_ok = edge_ok if b == 0 else inner_ok
                    if b == 0:
                        kp, vp = _load_tile(kp_ref, dil, r, c).astype(BF16), _load_tile(vp_ref, dil, r, c).astype(BF16)
                    else:
                        kp, vp = kc, vc
                    qst = _stack_heads(_load_tile(q_ref, dil, r, c, base) * ATT_SCALE, first).astype(BF16)
                    kc = _load_tile(k_ref, dil, r, c, base).astype(BF16)
                    vc = _load_tile(v_ref, dil, r, c, base).astype(BF16)
                    sc = jnp.where(cur_ok, _dot_nt(qst, kc), NEG)
                    sp = jnp.where(prev_ok, _dot_nt(qst, kp), NEG)
                    mx = jnp.max(jnp.maximum(sc, sp), axis=-1, keepdims=True)
                    pc, pp = jnp.exp(sc - mx), jnp.exp(sp - mx)
                    den = jnp.sum(pc + pp, axis=-1, keepdims=True)
                    ost = (_dot(pc.astype(BF16), vc) + _dot(pp.astype(BF16), vp)) / den
                    lse = jnp.broadcast_to(mx + jnp.log(den), (2 * ATT_BLOCK, 128))
                    _store_tile(o_ref, dil, r, c, _unstack_heads(ost, first), base)
                    _store_tile(lse_ref, dil, r, c, _unstack_heads(lse, first), base)

    slab = pl.BlockSpec((nbs * rows, lanes), lambda n, j: (n, j))
    before = pl.BlockSpec((rows, lanes), lambda n, j: (jnp.maximum(n * nbs - 1, 0), j))
    return pl.pallas_call(
        _behind(body, behind), name=f"att_fwd_d{dil}", grid=(n_steps, width // lanes),
        out_shape=[jax.ShapeDtypeStruct((seq, width), F32)] * 2,
        in_specs=_ANY * len(behind) + [slab, slab, slab, before, before], out_specs=[slab, slab],
        compiler_params=_params("arbitrary", "arbitrary"),
    )(*behind, q, k, v, k, v)


def _att_bwd(q, k, v, do, cc, lse, dil, behind=(), plus=()):
    seq, width = q.shape
    assert not plus or dil == 1
    plus = [a for triple in plus for a in triple]
    _, ub, lanes, nbs = _att_geometry(dil, seq)
    rows = ub * TOKEN_GROUP
    n_blocks = seq // rows
    n_steps = n_blocks // nbs

    def body(q_ref, k_ref, v_ref, do_ref, cc_ref, lse_ref, qx_ref, dox_ref, ccx_ref, lsex_ref, *rest):
        plus_refs, (dq_ref, dk_ref, dv_ref, carry) = rest[:len(plus)], rest[len(plus):]
        first, cur_ok, _band = _att_consts(dil)
        step = pl.program_id(1)
        inner_ok = _band(0)
        edge_ok = _band(jnp.where(step < n_steps - 1, 0, ATT_BLOCK))

        @pl.when(step == 0)
        def _():
            carry[...] = jnp.zeros_like(carry)

        def queries(refs, r, c, base):
            q_r, do_r, lse_r, cc_r = refs
            return (_stack_heads(_load_tile(q_r, dil, r, c, base) * ATT_SCALE, first).astype(BF16),
                    _stack_heads(_load_tile(do_r, dil, r, c, base), first).astype(BF16),
                    _stack_bcast(_load_tile(lse_r, dil, r, c, base), first),
                    _stack_bcast(_load_tile(cc_r, dil, r, c, base), first))

        for r in range(dil):
            for j in range(lanes // 128):
                c = slice(j * 128, (j + 1) * 128)
                own = queries((q_ref, do_ref, lse_ref, cc_ref), r, c, 0)
                left = _load_tile(carry, dil, r, c)
                for b in range(nbs):
                    base = b * rows
                    last = b == nbs - 1
                    next_ok = edge_ok if last else inner_ok
                    if last:
                        following = queries((qx_ref, dox_ref, lsex_ref, ccx_ref), r, c, 0)
                    else:
                        following = queries((q_ref, do_ref, lse_ref, cc_ref), r, c, base + rows)
                    (qst, dost, lse_n, cc_n), (qxst, doxst, lse_x, cc_x) = own, following
                    kb = _load_tile(k_ref, dil, r, c, base).astype(BF16)
                    vb = _load_tile(v_ref, dil, r, c, base).astype(BF16)
                    p_cur = jnp.exp(jnp.where(cur_ok, _dot_nt(qst, kb), NEG) - lse_n)
                    p_next = jnp.exp(jnp.where(next_ok, _dot_nt(qxst, kb), NEG) - lse_x)
                    ds_cur = (p_cur * (_dot_nt(dost, vb) + cc_n)).astype(BF16)
                    ds_next = (p_next * (_dot_nt(doxst, vb) + cc_x)).astype(BF16)
                    dq_own = (left + _unstack_heads(_dot(ds_cur, kb), first)) * ATT_SCALE
                    dk_own = _dot_tn(ds_cur, qst) + _dot_tn(ds_next, qxst)
                    dv_own = _dot_tn(p_cur.astype(BF16), dost) + _dot_tn(p_next.astype(BF16), doxst)
                    for i, (out_ref, val) in enumerate(zip((dq_ref, dk_ref, dv_ref), (dq_own, dk_own, dv_own))):
                        for other in plus_refs[i::3]:
                            val = val + _load_tile(other, dil, r, c, base)
                        _store_tile(out_ref, dil, r, c, val.astype(out_ref.dtype), base)
                    left = _unstack_heads(_dot(ds_next, kb), first)
                    own = following
                _store_tile(carry, dil, r, c, left)

    slab = pl.BlockSpec((nbs * rows, lanes), lambda j, n: (n, j))
    after = pl.BlockSpec((rows, lanes), lambda j, n: (jnp.minimum((n + 1) * nbs, n_blocks - 1), j))
    return pl.pallas_call(
        _behind(body, behind), name=f"att_bwd_d{dil}", grid=(width // lanes, n_steps),
        out_shape=[jax.ShapeDtypeStruct((seq, width), BF16 if plus else F32)] * 3,
        in_specs=_ANY * len(behind) + [slab] * 6 + [after] * 4 + [slab] * len(plus), out_specs=[slab] * 3,
        scratch_shapes=[pltpu.VMEM((rows, lanes), F32)],
        compiler_params=_params("arbitrary", "arbitrary"),
    )(*behind, q, k, v, do, cc, lse, q, do, cc, lse, *plus)


def _branch_weights(lses):
    mx = jnp.maximum(jnp.maximum(lses[0], lses[1]), lses[2])
    es = [jnp.exp(l - mx) for l in lses]
    inv = 1.0 / (es[0] + es[1] + es[2])
    return [e * inv for e in es]


def _att_combine_bwd_tile(d, att, lses, g):
    ahat, rstd = _rms(att)
    datt = _rms_bwd(d * g, ahat, rstd)
    hi = lax.broadcasted_iota(jnp.int32, (ATT_WIDTH, ATT_WIDTH), 0) // ATT_HEAD_DIM
    hj = lax.broadcasted_iota(jnp.int32, (ATT_WIDTH, ATT_WIDTH), 1) // ATT_HEAD_DIM
    same_head = (hi == hj).astype(BF16)
    prod = datt * att
    prod_hi = prod.astype(BF16)
    prod_lo = (prod - prod_hi.astype(F32)).astype(BF16)
    head_sum = _dot(prod_hi, same_head) + _dot(prod_lo, same_head)
    ws = _branch_weights(lses)
    return [w * datt for w in ws], [-w * head_sum for w in ws], _rowsum(d * ahat)


def _out_bwd(dx1, hg, at, mod, w_out, att, lses, att_g):
    s = dx1.shape[0]
    tm = 512
    n_steps = s // tm

    def body(dx_ref, hg_ref, at_ref, mod_ref, w_ref, att_ref, l0, l1, l2, g_ref,
             dhg_ref, do0, do1, do2, cc0, cc1, cc2, dw_ref, dwb_ref, dgate_ref, dg_ref):
        @pl.when(pl.program_id(0) == 0)
        def _():
            dw_ref[...] = jnp.zeros_like(dw_ref)
            dgate_ref[...] = jnp.zeros_like(dgate_ref)
            dg_ref[...] = jnp.zeros_like(dg_ref)

        hg, at, dx = hg_ref[...], at_ref[...], dx_ref[...]
        mix = _dot(hg, w_ref[0:512, :]) + _dot(at, w_ref[512:1024, :])
        dgate_ref[...] += _rowsum(dx * mix)
        dmix = (mod_ref[:, 2 * D_MODEL:3 * D_MODEL] * dx).astype(BF16)
        dhg_ref[...] = _dot_nt(dmix, w_ref[0:512, :])
        dos, ccs, dg_rows = _att_combine_bwd_tile(_dot_nt(dmix, w_ref[512:1024, :]), att_ref[...],
                                                  [l0[...], l1[...], l2[...]], g_ref[...])
        for val, ref in zip(dos + ccs, (do0, do1, do2, cc0, cc1, cc2)):
            ref[...] = val
        dg_ref[...] += dg_rows
        dw_ref[0:512, :] += _dot_tn(hg, dmix)
        dw_ref[512:1024, :] += _dot_tn(at, dmix)

        @pl.when(pl.program_id(0) == n_steps - 1)
        def _():
            dwb_ref[...] = dw_ref[...].astype(BF16)

    tile = _rows(tm, 512)
    square = _whole((D_MODEL, D_MODEL))
    return pl.pallas_call(
        body, name="out_bwd", grid=(n_steps,),
        out_shape=[jax.ShapeDtypeStruct((s, 512), F32)] * 7
        + [jax.ShapeDtypeStruct((D_MODEL, D_MODEL), F32), jax.ShapeDtypeStruct((D_MODEL, D_MODEL), BF16),
           jax.ShapeDtypeStruct((1, D_MODEL), F32), jax.ShapeDtypeStruct((1, ATT_WIDTH), F32)],
        in_specs=[_rows(tm, D_MODEL), tile, tile, _whole((1, 6 * D_MODEL)), square] + [tile] * 4 + [_whole((1, ATT_WIDTH))],
        out_specs=[tile] * 7 + [square, square, _whole((1, D_MODEL)), _whole((1, ATT_WIDTH))],
        compiler_params=_params("arbitrary"),
    )(dx1, hg, at, mod, w_out, att, *lses, att_g)


def _ffn(x, hg, outs, lses, att_g, target, mod, g2, gf, w_out, w_gu, w_down, after=()):
    s = x.shape[0]
    tm = 256

    def body(x_ref, hg_ref, o0, o1, o2, l0, l1, l2, ag_ref, t_ref, mod_ref, g2_ref, gf_ref, wo_ref, wgu_hbm, wd_hbm,
             dx_ref, h2_ref, act_ref, dau_ref, dff_ref, sums_ref, loss_ref, att_ref, at_ref, wgu, wd, au_s, sem):
        @pl.when(pl.program_id(0) == 0)
        def _():
            c1 = pltpu.make_async_copy(wgu_hbm, wgu, sem.at[0])
            c2 = pltpu.make_async_copy(wd_hbm, wd, sem.at[1])
            c1.start()
            c2.start()
            c1.wait()
            c2.wait()
            sums_ref[...] = jnp.zeros_like(sums_ref)
            loss_ref[...] = jnp.zeros_like(loss_ref)

        ws = _branch_weights([l0[...], l1[...], l2[...]])
        att = ws[0] * o0[...] + ws[1] * o1[...] + ws[2] * o2[...]
        att_ref[...] = att
        ahat, _ = _rms(att)
        at = (ahat * ag_ref[...]).astype(BF16)
        at_ref[...] = at
        mix = _dot(hg_ref[...], wo_ref[0:512, :]) + _dot(at, wo_ref[512:1024, :])
        x1v = x_ref[...] + mod_ref[:, 2 * D_MODEL:3 * D_MODEL] * mix
        xhat, rstd = _rms(x1v)
        g2 = g2_ref[...]
        n2 = xhat * g2
        scale2 = 1.0 + mod_ref[:, 4 * D_MODEL:5 * D_MODEL]
        gate2 = mod_ref[:, 5 * D_MODEL:6 * D_MODEL]
        hb = (n2 * scale2 + mod_ref[:, 3 * D_MODEL:4 * D_MODEL]).astype(BF16)
        h2_ref[...] = hb
        au_s[...] = _dot_nt(hb, wgu[...])
        a = au_s[:, 0:D_FF]
        act = (_silu(a) * au_s[:, D_FF:2 * D_FF]).astype(BF16)
        act_ref[...] = act
        ff = _dot(act, wd[...])
        x2 = x1v + gate2 * ff
        nf, rstd_f = _rms(x2)
        gfv = gf_ref[...]
        err = nf * gfv - t_ref[...]
        loss_ref[...] += 0.5 * jnp.sum(_rowsum(err * err), axis=-1, keepdims=True) * (1.0 / D_MODEL)
        dy = err * (1.0 / D_MODEL)
        dx2 = _rms_bwd(dy * gfv, nf, rstd_f)
        dffb = (gate2 * dx2).astype(BF16)
        dff_ref[...] = dffb
        dact = _dot_nt(dffb, wd[...])
        a = au_s[:, 0:D_FF]
        dau_ref[:, 0:D_FF] = (dact * au_s[:, D_FF:2 * D_FF] * _dsilu(a)).astype(BF16)
        dau_ref[:, D_FF:2 * D_FF] = (dact * _silu(a)).astype(BF16)
        dh = _dot(dau_ref[...], wgu[...])
        dn = dh * scale2
        sums_ref[0:1, :] += _rowsum(dh)
        sums_ref[1:2, :] += _rowsum(dh * n2)
        sums_ref[2:3, :] += _rowsum(dx2 * ff)
        sums_ref[3:4, :] += _rowsum(dn * xhat)
        sums_ref[4:5, :] += _rowsum(dy * nf)
        dx_ref[...] = dx2 + _rms_bwd(dn * g2, xhat, rstd)

    vec = _whole((1, D_MODEL))
    hbm = pl.BlockSpec(memory_space=pl.ANY)
    half = _rows(tm, 512)
    return pl.pallas_call(
        _behind(body, after), name="ffn", grid=(s // tm,),
        out_shape=[jax.ShapeDtypeStruct((s, D_MODEL), F32), jax.ShapeDtypeStruct((s, D_MODEL), BF16),
                   jax.ShapeDtypeStruct((s, D_FF), BF16), jax.ShapeDtypeStruct((s, 2 * D_FF), BF16),
                   jax.ShapeDtypeStruct((s, D_MODEL), BF16), jax.ShapeDtypeStruct((8, D_MODEL), F32),
                   jax.ShapeDtypeStruct((1, 128), F32), jax.ShapeDtypeStruct((s, ATT_WIDTH), F32),
                   jax.ShapeDtypeStruct((s, ATT_WIDTH), BF16)],
        in_specs=_ANY * len(after) + [_rows(tm, D_MODEL)] + [half] * 7 + [_whole((1, ATT_WIDTH)), _rows(tm, D_MODEL),
                                                                          _whole((1, 6 * D_MODEL)), vec, vec,
                                                                          _whole((D_MODEL, D_MODEL)), hbm, hbm],
        out_specs=[_rows(tm, D_MODEL), _rows(tm, D_MODEL), _rows(tm, D_FF), _rows(tm, 2 * D_FF), _rows(tm, D_MODEL),
                   _whole((8, D_MODEL)), _whole((1, 128)), half, half],
        scratch_shapes=[pltpu.VMEM((2 * D_FF, D_MODEL), BF16), pltpu.VMEM((D_FF, D_MODEL), BF16),
                        pltpu.VMEM((tm, 2 * D_FF), F32), pltpu.SemaphoreType.DMA((2,))],
        compiler_params=_params("arbitrary", vmem=V7X_VMEM_MOST),
    )(*after, x, hg, *outs, *lses, att_g, target, mod, g2, gf, w_out, w_gu, w_down)


def _weight_grad(a, b, name, rounded=False):
    s, m = a.shape
    n = b.shape[1]
    ts = min(s, 2048)
    n_steps = s // ts
    tm = max(t for t in range(128, m + 1, 128) if m % t == 0 and t * n * 4 <= 6 * 1024 * 1024)

    def body(a_ref, b_ref, o_ref, *ob_ref):
        @pl.when(pl.program_id(1) == 0)
        def _():
            o_ref[...] = jnp.zeros_like(o_ref)

        o_ref[...] += _dot_tn(a_ref[...], b_ref[...])
        if rounded:
            @pl.when(pl.program_id(1) == n_steps - 1)
            def _():
                ob_ref[0][...] = o_ref[...].astype(BF16)

    tile = pl.BlockSpec((tm, n), lambda j, i: (j, 0))
    return pl.pallas_call(
        body, name=name, grid=(m // tm, n_steps),
        out_shape=[jax.ShapeDtypeStruct((m, n), F32)] + [jax.ShapeDtypeStruct((m, n), BF16)] * rounded,
        in_specs=[pl.BlockSpec((ts, tm), lambda j, i: (i, j)), pl.BlockSpec((ts, n), lambda j, i: (i, 0))],
        out_specs=[tile] + [tile] * rounded,
        compiler_params=_params("parallel", "arbitrary"),
    )(a, b)


def _adamw_math(w, g, m, v):
    m = ADAM_B1 * m + (1.0 - ADAM_B1) * g
    v = ADAM_B2 * v + (1.0 - ADAM_B2) * (g * g)
    m_hat = m / (1.0 - ADAM_B1 ** ADAM_STEP)
    v_hat = v / (1.0 - ADAM_B2 ** ADAM_STEP)
    delta = -ADAM_LR * (m_hat / (jnp.sqrt(v_hat) + ADAM_EPS) + ADAM_WD * w)
    return delta, m, v


def _adamw_shard(where, w, m, v, partial, got, name, after=()):
    r, c = w.shape
    tr = _shard_rows(r)
    lead = partial.ndim - 2
    n_got = got.shape[0]

    def body(where_ref, *refs):
        w_ref, m_ref, v_ref, own_ref, *rest = refs[len(after):]
        got_refs, (grad_ref, d_ref, nm_ref, nv_ref) = rest[:n_got], rest[n_got:]
        g = own_ref[...]
        for g_ref in got_refs:
            g = g + g_ref[...].astype(F32)
        grad_ref[...] = g
        d_ref[...], nm_ref[...], nv_ref[...] = _adamw_math(w_ref[...], g, m_ref[...], v_ref[...])

    tile = pl.BlockSpec((tr, c), lambda i, where_ref: (i, 0))
    own = pl.BlockSpec((None,) * lead + (tr, c), lambda i, where_ref: (*[where_ref[d] for d in range(lead)], i, 0))
    part = [pl.BlockSpec((None, tr, c), functools.partial(lambda j, i, where_ref: (j, i, 0), j)) for j in range(n_got)]
    return pl.pallas_call(
        body, name=name,
        grid_spec=pltpu.PrefetchScalarGridSpec(num_scalar_prefetch=1, grid=(r // tr,),
                                               in_specs=_ANY * len(after) + [tile] * 3 + [own] + part, out_specs=[tile] * 4),
        out_shape=[jax.ShapeDtypeStruct((r, c), F32)] * 4, compiler_params=_params("parallel"),
    )(where, *after, w, m, v, partial, *[got] * n_got)


def _small_update(small_all, dmod_blocks, c_all, logits, w_ada, m_ada, v_ada, smalls, after=()):
    def body(sm_ref, dm_ref, c_ref, lg_ref, wa_ref, ma_ref, va_ref, *rest):
        ins, outs = rest[:21], rest[21:]
        _, me = _flip(0)
        tot = sm_ref[0:1, :]
        for i in range(1, N_DEV):
            tot = tot + sm_ref[i:i + 1, :]
        loss_ref = outs[0]
        loss_ref[...] = tot[:, SM_LOSS:SM_LOSS + 128]
        g_ada = lax.dot_general(_silu(c_ref[...]), dm_ref[me], (((0,), (0,)), ((), ())),
                                preferred_element_type=F32, precision=HIGHEST)
        outs[1][...] = g_ada
        outs[2][...], outs[3][...], outs[4][...] = _adamw_math(wa_ref[...], g_ada, ma_ref[...], va_ref[...])
        p0 = _lower_bound(lg_ref)
        dl0 = tot[:, SM_LB:SM_LB + 512] * p0 * (1.0 - p0)
        grads = [tot[:, SM_MOD:SM_MOD + 6 * D_MODEL], tot[:, SM_G1:SM_G1 + D_MODEL], tot[:, SM_G2:SM_G2 + D_MODEL],
                 tot[:, SM_GF:SM_GF + D_MODEL], tot[:, SM_ATT:SM_ATT + 512], tot[:, SM_HG:SM_HG + 128],
                 jnp.where(lax.broadcasted_iota(jnp.int32, (2, 512), 0) == 0, dl0, -dl0)]
        for i, g in enumerate(grads):
            w_ref, m_ref, v_ref = ins[3 * i:3 * i + 3]
            o = outs[5 + 4 * i:9 + 4 * i]
            o[0][...] = g
            o[1][...], o[2][...], o[3][...] = _adamw_math(w_ref[...], g, m_ref[...], v_ref[...])

    flat = [t for trio in smalls for t in trio]
    vm = pl.BlockSpec(memory_space=pltpu.VMEM)
    out_shape = [jax.ShapeDtypeStruct((1, 128), F32)] + [jax.ShapeDtypeStruct(w_ada.shape, F32)] * 4
    for trio in smalls:
        out_shape += [jax.ShapeDtypeStruct(trio[0].shape, F32)] * 4
    return pl.pallas_call(
        _behind(body, after), name="small_update", out_shape=out_shape,
        in_specs=_ANY * len(after) + [vm] * (7 + len(flat)), out_specs=[vm] * len(out_shape),
        compiler_params=pltpu.CompilerParams(vmem_limit_bytes=V7X_VMEM_LIMIT),
    )(*after, small_all, dmod_blocks, c_all, logits, w_ada, m_ada, v_ada, *flat)


def kernel(x, c, w_ada, b_ada, norm1_g, w_in, hg_lb_logits, hg_onorm_g, att_onorm_g, w_out, norm2_g, w_gate_up, w_down, final_g, loss_target, m_w_ada, m_b_ada, m_norm1_g, m_w_in, m_hg_lb_logits, m_hg_onorm_g, m_att_onorm_g, m_w_out, m_norm2_g, m_w_gate_up, m_w_down, m_final_g, v_w_ada, v_b_ada, v_norm1_g, v_w_in, v_hg_lb_logits, v_hg_onorm_g, v_att_onorm_g, v_w_out, v_norm2_g, v_w_gate_up, v_w_down, v_final_g):
    x2d, target = x[0], loss_target[0]
    seq = x2d.shape[0]
    assert seq % (ATT_BLOCK * max(DILATIONS)) == 0 and seq % HG_TILE == 0
    gf = final_g.reshape(1, D_MODEL)

    c_all = _exchange_small(c.reshape(8, D_MODEL // 8), None, "gather_c").reshape(N_DEV, D_MODEL)
    ada = _ada_rows(c_all, w_ada[0], b_ada)
    mod = _exchange_small(ada, 1, "scatter_mod").reshape(1, 6 * D_MODEL)

    core = lax.axis_index("c").astype(jnp.int32).reshape(1)
    chip = (2 * lax.axis_index("x") + lax.axis_index("y")).astype(jnp.int32).reshape(1)
    me = 4 * lax.axis_index("x") + 2 * lax.axis_index("y") + lax.axis_index("c")

    g_in, = _gather_weights([w_in[0].T.astype(BF16)])
    w_in_b = g_in.reshape(IN_WIDTH, D_MODEL)
    rest_shards = [w_out[0].astype(BF16), w_gate_up[0].T.astype(BF16), w_down[0].astype(BF16)]
    lands = [lax.empty((N_DEV,) + s.shape, BF16) for s in rest_shards]
    g_send, g_recv, g_srcs, g_lands, tok = _copies_start("gather_rest_start", _plan_gather_own, 12, rest_shards, lands, [w_in_b, mod])
    flight = {}

    def stage(name, *vals):
        if name == "attention_begun":
            flight["shards"], got = _copies_wait("gather_rest_wait", _plan_gather_own, g_send, g_recv, g_srcs, g_lands, list(vals))
            flight["pass"] = _copies_start("gather_pass_start", _plan_gather_pass, 9, [], got, [])
            return [flight["pass"][4]]
        if name == "mixer_weights_done":
            shapes = [(4, 2, D_MODEL // N_DEV, D_MODEL), (4, 2, 2 * D_FF // N_DEV, D_MODEL), (4, 2, D_FF // N_DEV, D_MODEL)]
            flight["grads"] = [g32.reshape(sh) for (g32, _), sh in zip(vals, shapes)]
            rounded = [g16.reshape(sh) for (_, g16), sh in zip(vals, shapes)]
            direct_lands = [lax.empty((N_DEV - 1,) + sh[2:], BF16) for sh in shapes]
            flight["direct"] = _copies_start("reduce_rest_start", _plan_reduce_direct, 21, rounded, direct_lands, [])
            return [flight["direct"][4]]
        raise ValueError(name)

    def rest_weights(after):
        s, r, _, p_lands, _ = flight["pass"]
        _, got = _copies_wait("gather_pass_wait", _plan_gather_pass, s, r, [], p_lands, [after])
        full = [lax.dynamic_update_index_in_dim(g, shard, me, 0) for g, shard in zip(got, flight["shards"])]
        return full[0].reshape(D_MODEL, D_MODEL), full[1].reshape(2 * D_FF, D_MODEL), full[2].reshape(D_FF, D_MODEL)

    grad_x, dw_in, small = _block_step(x2d, target, mod, norm1_g, hg_lb_logits, hg_onorm_g, att_onorm_g, norm2_g, gf,
                                       w_in_b, rest_weights, stage, [tok])

    g_in8 = dw_in.reshape(4, 2, IN_WIDTH // N_DEV, D_MODEL)
    in_pairs = _copies_start("reduce_pairs_in_start", _plan_reduce_pairs, 4, [g_in8], [lax.empty((4,) + g_in8.shape[2:], F32)], [])
    s, r, srcs, d_lands, _ = flight["direct"]
    _, recv_rest = _copies_wait("reduce_rest_wait", _plan_reduce_direct, s, r, srcs, d_lands, [in_pairs[4]])
    small_rows = jnp.pad(small, ((0, 0), (0, SM_PADDED - SM_WIDTH))).reshape(SM_PADDED // 128, 128)
    small_all = _exchange_small(small_rows, None, "gather_small", [in_pairs[4]]).reshape(N_DEV, SM_PADDED)[:, :SM_WIDTH]
    in_grads, got_in = _copies_wait("reduce_pairs_in_wait", _plan_reduce_pairs, in_pairs[0], in_pairs[1], in_pairs[2], in_pairs[3],
                                    [small_all])
    in_s32, in_s16 = _pair_sum(core, in_grads[0], got_in[0], "pair_sum_in")
    in_chips = _copies_start("reduce_chips_in_start", _plan_reduce_chips, 3, [in_s16], [lax.empty((3,) + in_s16.shape[1:], BF16)], [])
    big, updated = {}, []
    rest_params = [("w_out", w_out, m_w_out, v_w_out), ("w_gate_up", w_gate_up, m_w_gate_up, v_w_gate_up), ("w_down", w_down, m_w_down, v_w_down)]
    mine = jnp.concatenate([chip, core])
    for (n, w, m, v), g32, got in zip(rest_params, flight["grads"], recv_rest):
        if n == "w_gate_up":
            outs4 = _adamw_shard(mine, w[0].T, m[0].T, v[0].T, g32, got, f"adamw_{n}", [in_chips[4]])
            big[n] = [t.T[None] for t in outs4]
        else:
            outs4 = _adamw_shard(mine, w[0], m[0], v[0], g32, got, f"adamw_{n}", [in_chips[4]])
            big[n] = [t[None] for t in outs4]
        updated.append(outs4[3])
    smalls = [(b_ada, m_b_ada, v_b_ada), (norm1_g, m_norm1_g, v_norm1_g), (norm2_g, m_norm2_g, v_norm2_g),
              (gf, m_final_g.reshape(1, D_MODEL), v_final_g.reshape(1, D_MODEL)),
              (att_onorm_g, m_att_onorm_g, v_att_onorm_g), (hg_onorm_g, m_hg_onorm_g, v_hg_onorm_g),
              (hg_lb_logits, m_hg_lb_logits, v_hg_lb_logits)]
    dmod_blocks = small_all[:, :6 * D_MODEL].reshape(N_DEV, N_DEV, 6 * D_MODEL // N_DEV).transpose(1, 0, 2)
    res = _small_update(small_all, dmod_blocks, c_all, hg_lb_logits, w_ada[0], m_w_ada[0], v_w_ada[0], smalls, [in_chips[4]])
    _, recv_in = _copies_wait("reduce_chips_in_wait", _plan_reduce_chips, in_chips[0], in_chips[1], in_chips[2], in_chips[3],
                              [res[0]] + updated)
    big["w_in"] = [t.T[None] for t in _adamw_shard(chip, w_in[0].T, m_w_in[0].T, v_w_in[0].T, in_s32, recv_in[0], "adamw_w_in")]
    loss = res[0][0, 0]
    ada4 = [t[None] for t in res[1:5]]
    sm4 = {n: list(res[5 + 4 * i:9 + 4 * i]) for i, n in enumerate(["b_ada", "norm1_g", "norm2_g", "final_g", "att", "hg", "lb"])}
    sm4["final_g"] = [t.reshape(D_MODEL) for t in sm4["final_g"]]

    order = [ada4, sm4["b_ada"], sm4["norm1_g"], big["w_in"], sm4["lb"], sm4["hg"], sm4["att"], big["w_out"], sm4["norm2_g"],
             big["w_gate_up"], big["w_down"], sm4["final_g"]]
    return (loss, grad_x[None], *[o[0] for o in order], *[o[1] for o in order], *[o[2] for o in order], *[o[3] for o in order])


def _block_step(x2d, target, mod, norm1_g, hg_lb_logits, hg_onorm_g, att_onorm_g, norm2_g, gf, w_in_b, rest_weights, stage,
                after=()):
    h1, hq, hf, hi, hgt, aq, ak, av, hg_out, hg_o, hg_states = _in_fwd(x2d, mod, norm1_g, w_in_b, hg_lb_logits, hg_onorm_g, after)
    branch = [_att_fwd(aq, ak, av, d) for d in DILATIONS[:2]]
    behind = stage("attention_begun", branch[0][0], branch[1][0])
    branch += [_att_fwd(aq, ak, av, d, behind) for d in DILATIONS[2:]]
    outs = [b[0] for b in branch]
    lses = [b[1] for b in branch]
    w_out_b, w_gu_b, w_down_b = rest_weights(outs[-1])

    dx1, h2, act, dau, dff, ffn_sums, loss_part, att, att_out = _ffn(
        x2d, hg_out, outs, lses, att_onorm_g, target, mod, norm2_g, gf, w_out_b, w_gu_b, w_down_b)
    dw_gu = _weight_grad(dau, h2, "dw_gate_up", rounded=True)
    dw_down = _weight_grad(act, dff, "dw_down", rounded=True)

    back = _out_bwd(dx1, hg_out, att_out, mod, w_out_b, att, lses, att_onorm_g)
    dhg, dos, ccs = back[0], back[1:4], back[4:7]
    dw_out, dw_out_b, dgate1, d_att_g = back[7:11]
    behind = stage("mixer_weights_done", (dw_out, dw_out_b), dw_gu, dw_down)
    dilated = [_att_bwd(aq, ak, av, dos[i], ccs[i], lses[i], d, behind) for i, d in enumerate(DILATIONS) if d > 1]
    datt = _att_bwd(aq, ak, av, dos[0], ccs[0], lses[0], DILATIONS[0], behind, plus=dilated)
    dhq, dhf, dhi, dhgt, d_hg_g, d_lb = _hg_bwd(hq, hf, hi, hgt, hg_lb_logits, hg_onorm_g, hg_o, hg_states, dhg)
    dps = [dhq, dhf, dhi, dhgt] + list(datt)
    grad_x, dp_b, dshift1, dscale1, d_g1 = _in_bwd(x2d, dx1, mod, norm1_g, w_in_b, dps)
    dw_in, = _weight_grad(dp_b, h1, "dw_in")
    small = jnp.concatenate([dshift1, dscale1, dgate1, ffn_sums[0:1], ffn_sums[1:2], ffn_sums[2:3], d_g1, ffn_sums[3:4],
                             ffn_sums[4:5], d_att_g, d_lb, d_hg_g, loss_part], axis=1)
    return grad_x, dw_in, small
```

```python
import functools

import jax
import jax.numpy as jnp
from jax import lax
from jax.experimental import pallas as pl
from jax.experimental.pallas import tpu as pltpu

F32 = jnp.float32
BF16 = jnp.bfloat16
HIGHEST = lax.Precision.HIGHEST
MESH = pl.DeviceIdType.MESH

D_MODEL = 1024
N_DEV = 8
HG_HEADS = 4
HG_DIM = 128
HG_WIDTH = HG_HEADS * HG_DIM
HG_CHUNK = 128
ATT_WIDTH = 512
ATT_HEAD_DIM = 64
ATT_BLOCK = 128
DILATIONS = (1, 4, 16)
ATT_SCALE = ATT_HEAD_DIM ** -0.5
D_FF = 2816
IN_WIDTH = 7 * 512
RMS_EPS = 1e-6
NEG = -1e30

ADAM_LR = 0.001
ADAM_B1 = 0.9
ADAM_B2 = 0.999
ADAM_EPS = 1e-08
ADAM_WD = 0.01
ADAM_STEP = 10

V7X_VMEM_LIMIT = 56 * 1024 * 1024
V7X_VMEM_MOST = 60 * 1024 * 1024

SM_MOD = 0
SM_G1 = 6 * D_MODEL
SM_G2 = 7 * D_MODEL
SM_GF = 8 * D_MODEL
SM_ATT = 9 * D_MODEL
SM_LB = 9 * D_MODEL + 512
SM_HG = 10 * D_MODEL
SM_LOSS = 10 * D_MODEL + 128
SM_WIDTH = 10 * D_MODEL + 256
SM_PADDED = 88 * 128


def _params(*sem, vmem=V7X_VMEM_LIMIT):
    return pltpu.CompilerParams(dimension_semantics=sem, vmem_limit_bytes=vmem)


def _dot(a, b):
    return jnp.dot(a, b, preferred_element_type=F32)


def _dot_nt(a, b):
    return lax.dot_general(a, b, (((1,), (1,)), ((), ())), preferred_element_type=F32)


def _dot_tn(a, b):
    return lax.dot_general(a, b, (((0,), (0,)), ((), ())), preferred_element_type=F32)


def _dot_f32(a, b):
    return jnp.dot(a, b, preferred_element_type=F32, precision=HIGHEST)


def _sigmoid(x):
    return 1.0 / (1.0 + jnp.exp(-x))


def _silu(x):
    return x * _sigmoid(x)


def _dsilu(x):
    s = _sigmoid(x)
    return s * (1.0 + x * (1.0 - s))


def _rms(x):
    rstd = lax.rsqrt(jnp.mean(x * x, axis=-1, keepdims=True) + RMS_EPS)
    return x * rstd, rstd


def _rms_bwd(dn, xhat, rstd):
    return rstd * (dn - xhat * jnp.mean(dn * xhat, axis=-1, keepdims=True))


def _rowsum(x):
    return jnp.sum(x, axis=0, keepdims=True)


def _rows(tm, n):
    return pl.BlockSpec((tm, n), lambda i: (i, 0))


def _whole(shape):
    return pl.BlockSpec(shape, lambda i: (0,) * len(shape))


def _mesh_pos():
    return lax.axis_index("x"), lax.axis_index("y"), lax.axis_index("c")


def _flip(k):
    x, y, c = _mesh_pos()
    px = 1 - x if k & 4 else x
    py = 1 - y if k & 2 else y
    pc = 1 - c if k & 1 else c
    return (px, py, pc), 4 * px + 2 * py + pc


_ANY = [pl.BlockSpec(memory_space=pl.ANY)]


def _behind(body, after):
    return lambda *refs: body(*refs[len(after):])


def _exchange_small(x, rows_per_peer, name, after=()):
    r_all, cols = x.shape
    r_out = r_all if rows_per_peer is None else rows_per_peer

    def body(x_ref, out_ref, send_sems, recv_sems):
        _, me = _flip(0)

        def src(pid):
            if rows_per_peer is None:
                return x_ref
            return x_ref.at[pl.ds(pl.multiple_of(pid * r_out, r_out), r_out), :]

        if rows_per_peer is None:
            out_ref[me] = x_ref[...]
        else:
            out_ref[me] = x_ref[pl.ds(pl.multiple_of(me * r_out, r_out), r_out), :]
        sends = []
        for k in range(1, N_DEV):
            dev, pid = _flip(k)
            cp = pltpu.make_async_remote_copy(src_ref=src(pid), dst_ref=out_ref.at[me], send_sem=send_sems.at[k - 1],
                                              recv_sem=recv_sems.at[k - 1], device_id=dev, device_id_type=MESH)
            cp.start()
            sends.append(cp)
        for k in range(1, N_DEV):
            dev, pid = _flip(k)
            pltpu.make_async_remote_copy(src_ref=src(pid), dst_ref=out_ref.at[pid], send_sem=send_sems.at[k - 1],
                                         recv_sem=recv_sems.at[k - 1], device_id=dev, device_id_type=MESH).wait_recv()
        for cp in sends:
            cp.wait_send()

    return pl.pallas_call(
        _behind(body, after), name=name,
        out_shape=jax.ShapeDtypeStruct((N_DEV, r_out, cols), x.dtype),
        in_specs=_ANY * len(after) + [pl.BlockSpec(memory_space=pltpu.VMEM)],
        out_specs=pl.BlockSpec(memory_space=pltpu.VMEM),
        scratch_shapes=[pltpu.SemaphoreType.DMA((N_DEV - 1,)), pltpu.SemaphoreType.DMA((N_DEV - 1,))],
    )(*after, x)


def _gather_weights(shards):
    n = len(shards)

    def body(*refs):
        xs, outs = refs[:n], refs[n:2 * n]
        send_sems, recv_sems, local_sems = refs[2 * n:]
        x, y, c = _mesh_pos()
        me, sibling = (x, y, c), (x, y, 1 - c)
        chips = [(1 - x, y), (x, 1 - y), (1 - x, 1 - y)]

        def blk(a, px, py, pc):
            return outs[a].at[4 * px + 2 * py + pc]

        def copy(a, k, block, to, src=None):
            return pltpu.make_async_remote_copy(
                src_ref=blk(a, *block) if src is None else src, dst_ref=blk(a, *block),
                send_sem=send_sems.at[a * 7 + k], recv_sem=recv_sems.at[a * 7 + k], device_id=to, device_id_type=MESH)

        mine = [pltpu.make_async_copy(xs[a], blk(a, *me), local_sems.at[a]) for a in range(n)]
        for cp in mine:
            cp.start()
        first = []
        for a in range(n):
            first.append(copy(a, 0, me, sibling, src=xs[a]))
            first += [copy(a, 1 + j, me, (*chip, c), src=xs[a]) for j, chip in enumerate(chips)]
        for cp in first:
            cp.start()
        passed = []
        for j, chip in enumerate(chips):
            for a in range(n):
                copy(a, 1 + j, (*chip, c), me).wait_recv()
                cp = copy(a, 4 + j, (*chip, c), sibling)
                cp.start()
                passed.append(cp)
        for a in range(n):
            copy(a, 0, sibling, me).wait_recv()
            for j, chip in enumerate(chips):
                copy(a, 4 + j, (*chip, 1 - c), me).wait_recv()
        for cp in first + passed:
            cp.wait_send()
        for cp in mine:
            cp.wait()

    hbm = pl.BlockSpec(memory_space=pl.ANY)
    return pl.pallas_call(
        body, name="gather_weights",
        out_shape=[jax.ShapeDtypeStruct((N_DEV,) + s.shape, s.dtype) for s in shards],
        in_specs=[hbm] * n, out_specs=[hbm] * n,
        scratch_shapes=[pltpu.SemaphoreType.DMA((7 * n,)), pltpu.SemaphoreType.DMA((7 * n,)), pltpu.SemaphoreType.DMA((n,))],
    )(*shards)


_HBM = pl.BlockSpec(memory_space=pltpu.HBM)
_SEM = pl.BlockSpec(memory_space=pltpu.SEMAPHORE)
_DATAFLOW = pltpu.SideEffectType.DATAFLOW_SIDE_EFFECTING


def _copies_start(name, plan, n_copies, srcs, lands, after):
    bufs = list(srcs) + list(lands)
    nb = len(bufs)

    def body(*refs):
        ins, send_sems, recv_sems, token = refs[:nb], refs[nb + len(after)], refs[nb + len(after) + 1], refs[-1]
        for i, (src, dst, dev) in enumerate(plan(ins[:len(srcs)], ins[len(srcs):])):
            pltpu.make_async_remote_copy(src_ref=src, dst_ref=dst, send_sem=send_sems.at[i], recv_sem=recv_sems.at[i],
                                         device_id=dev, device_id_type=MESH).start()
        token[...] = jnp.zeros_like(token)

    outs = pl.pallas_call(
        body, name=name,
        out_shape=(pltpu.SemaphoreType.DMA((n_copies,)), pltpu.SemaphoreType.DMA((n_copies,)),
                   *[pltpu.HBM(b.shape, b.dtype) for b in bufs], jax.ShapeDtypeStruct((8, 128), F32)),
        in_specs=[_HBM] * nb + [pl.BlockSpec(memory_space=pl.ANY)] * len(after),
        out_specs=(_SEM, _SEM, *[_HBM] * nb, pl.BlockSpec(memory_space=pltpu.VMEM)),
        input_output_aliases={i: 2 + i for i in range(nb)},
        compiler_params=pltpu.CompilerParams(has_side_effects=_DATAFLOW),
    )(*[pltpu.with_memory_space_constraint(b, pltpu.HBM) for b in bufs], *after)
    return outs[0], outs[1], list(outs[2:2 + len(srcs)]), list(outs[2 + len(srcs):2 + nb]), outs[-1]


def _copies_wait(name, plan, send_sems, recv_sems, srcs, lands, after):
    bufs = list(srcs) + list(lands)
    nb = len(bufs)

    def body(*refs):
        ins, send_ref, recv_ref = refs[:nb], refs[nb], refs[nb + 1]
        for i, (src, dst, dev) in enumerate(plan(ins[:len(srcs)], ins[len(srcs):])):
            cp = pltpu.make_async_remote_copy(src_ref=src, dst_ref=dst, send_sem=send_ref.at[i], recv_sem=recv_ref.at[i],
                                              device_id=dev, device_id_type=MESH)
            cp.wait_send()
            cp.wait_recv()

    outs = pl.pallas_call(
        body, name=name, out_shape=[pltpu.HBM(b.shape, b.dtype) for b in bufs],
        in_specs=[_HBM] * nb + [_SEM, _SEM] + [pl.BlockSpec(memory_space=pl.ANY)] * len(after), out_specs=[_HBM] * nb,
        input_output_aliases={i: i for i in range(nb)},
        compiler_params=pltpu.CompilerParams(has_side_effects=_DATAFLOW),
    )(*bufs, send_sems, recv_sems, *after)
    return list(outs[:len(srcs)]), list(outs[len(srcs):])


def _plan_gather_own(srcs, lands):
    _, me = _flip(0)
    return [(srcs[a], lands[a].at[me], _flip(k)[0]) for a in range(len(srcs)) for k in (1, 4, 2, 6)]


def _plan_gather_pass(srcs, lands):
    sibling = _flip(1)[0]
    plan = []
    for land in lands:
        for k in (4, 2, 6):
            block = land.at[_flip(k)[1]]
            plan.append((block, block, sibling))
    return plan


def _plan_reduce_pairs(srcs, lands):
    x, y, c = _mesh_pos()
    return [(srcs[a].at[chip, 1 - c], lands[a].at[chip], (x, y, 1 - c)) for a in range(len(srcs)) for chip in range(4)]


def _plan_reduce_chips(srcs, lands):
    plan = []
    for a in range(len(srcs)):
        for j, k in enumerate((4, 2, 6)):
            dev = _flip(k)[0]
            plan.append((srcs[a].at[2 * dev[0] + dev[1]], lands[a].at[j], dev))
    return plan


def _plan_reduce_direct(srcs, lands):
    plan = []
    for a in range(len(srcs)):
        for k in range(1, N_DEV):
            dev = _flip(k)[0]
            plan.append((srcs[a].at[2 * dev[0] + dev[1], dev[2]], lands[a].at[k - 1], dev))
    return plan


def _shard_rows(r):
    return r // 2 if r % 32 == 0 else r


def _pair_sum(core, grads, got, name):
    _, _, r, c = grads.shape
    tr = _shard_rows(r)

    def body(core_ref, a_ref, b_ref, o_ref, ob_ref):
        s = a_ref[...] + b_ref[...]
        o_ref[...] = s
        ob_ref[...] = s.astype(BF16)

    spec = pl.BlockSpec((None, tr, c), lambda i, j, core_ref: (i, j, 0))
    return pl.pallas_call(
        body, name=name,
        grid_spec=pltpu.PrefetchScalarGridSpec(
            num_scalar_prefetch=1, grid=(4, r // tr),
            in_specs=[pl.BlockSpec((None, None, tr, c), lambda i, j, core_ref: (i, core_ref[0], j, 0)), spec],
            out_specs=[spec, spec]),
        out_shape=[jax.ShapeDtypeStruct((4, r, c), F32), jax.ShapeDtypeStruct((4, r, c), BF16)],
        compiler_params=_params("parallel", "parallel"),
    )(core, grads, got)


def _ada_rows(c_all, w_ada, b_ada):
    n_cols = w_ada.shape[1]

    def body(c_ref, w_ref, b_ref, o_ref):
        _, me = _flip(0)
        bias = b_ref[:, pl.ds(pl.multiple_of(me * n_cols, 128), n_cols)]
        o_ref[...] = _dot_f32(_silu(c_ref[...]), w_ref[...]) + bias

    return pl.pallas_call(
        body, name="ada_rows", out_shape=jax.ShapeDtypeStruct((N_DEV, n_cols), F32),
        in_specs=[pl.BlockSpec(memory_space=pltpu.VMEM)] * 3, out_specs=pl.BlockSpec(memory_space=pltpu.VMEM),
    )(c_all, w_ada, b_ada)


def _in_fwd(x, mod, g1, w_in, logits, onorm_g, after=()):
    s = x.shape[0]
    tm = HG_TILE

    def body(x_ref, mod_ref, g_ref, w_ref, lg_ref, og_ref, h_ref, *rest):
        groups, (out_ref, o_ref, st_ref, state, qf_s, kk_s, lf_s) = rest[:7], rest[7:]
        xhat, _ = _rms(x_ref[...])
        h = (xhat * g_ref[...]) * (1.0 + mod_ref[:, D_MODEL:2 * D_MODEL]) + mod_ref[:, 0:D_MODEL]
        hb = h.astype(BF16)
        h_ref[...] = hb
        for j, o_ref_j in enumerate(groups):
            o_ref_j[...] = _dot_nt(hb, w_ref[j * 512:(j + 1) * 512, :])
        _hg_fwd_tile(*groups[:4], lg_ref, og_ref, out_ref, o_ref, st_ref, state, qf_s, kk_s, lf_s)

    tile = _rows(tm, 512)
    return pl.pallas_call(
        _behind(body, after), name="in_fwd", grid=(s // tm,),
        out_shape=[jax.ShapeDtypeStruct((s, D_MODEL), BF16)] + [jax.ShapeDtypeStruct((s, 512), F32)] * 7
        + [jax.ShapeDtypeStruct((s, HG_WIDTH), BF16), jax.ShapeDtypeStruct((s, HG_WIDTH), F32),
           jax.ShapeDtypeStruct((s // HG_CHUNK * HG_DIM, HG_WIDTH), F32)],
        in_specs=_ANY * len(after)
        + [_rows(tm, D_MODEL), _whole((1, 6 * D_MODEL)), _whole((1, D_MODEL)), _whole((IN_WIDTH, D_MODEL)),
           _whole((2, HG_WIDTH)), _whole((1, HG_DIM))],
        out_specs=[_rows(tm, D_MODEL)] + [tile] * 7 + [tile, tile, _rows(HG_TILE_CHUNKS * HG_DIM, HG_WIDTH)],
        scratch_shapes=[pltpu.VMEM((HG_HEADS, HG_DIM, HG_DIM), F32)] + [pltpu.VMEM((HG_TILE, HG_WIDTH), F32)] * 3,
        compiler_params=_params("arbitrary"),
    )(*after, x, mod, g1, w_in, logits, onorm_g)


def _in_bwd(x, dx1, mod, g1, w_in, h1, dps):
    s = x.shape[0]
    tm = 256
    n_steps = s // tm
    assert len(dps) == 7 and all(p.dtype == BF16 for p in dps)

    def body(x_ref, dx_ref, mod_ref, g_ref, w_ref, h_ref, *rest):
        dp_refs, (gx_ref, dw_ref, dsh_ref, dsc_ref, dg_ref, dpb_ref, acc, sem) = rest[:7], rest[7:]

        @pl.when(pl.program_id(0) == 0)
        def _():
            acc[...] = jnp.zeros_like(acc)

        h = h_ref[...]
        for j, p_ref in enumerate(dp_refs):
            p = p_ref[...]
            dpb_ref[:, j * 512:(j + 1) * 512] = p
            acc[j * 512:(j + 1) * 512, :] += _dot_tn(p, h)

        @pl.when(pl.program_id(0) == n_steps - 1)
        def _():
            done = pltpu.make_async_copy(acc, dw_ref, sem)
            done.start()
            done.wait()

        dh = _dot(dpb_ref[...], w_ref[...])
        xhat, rstd = _rms(x_ref[...])
        g = g_ref[...]
        scale1 = 1.0 + mod_ref[:, D_MODEL:2 * D_MODEL]
        n1 = xhat * g

        @pl.when(pl.program_id(0) == 0)
        def _():
            dsh_ref[...] = jnp.zeros_like(dsh_ref)
            dsc_ref[...] = jnp.zeros_like(dsc_ref)
            dg_ref[...] = jnp.zeros_like(dg_ref)

        dsh_ref[...] += _rowsum(dh)
        dsc_ref[...] += _rowsum(dh * n1)
        dn = dh * scale1
        dg_ref[...] += _rowsum(dn * xhat)
        gx_ref[...] = dx_ref[...] + _rms_bwd(dn * g, xhat, rstd)

    vec = _whole((1, D_MODEL))
    return pl.pallas_call(
        body, name="in_bwd", grid=(n_steps,),
        out_shape=[jax.ShapeDtypeStruct((s, D_MODEL), F32), jax.ShapeDtypeStruct((IN_WIDTH, D_MODEL), F32)]
        + [jax.ShapeDtypeStruct((1, D_MODEL), F32)] * 3,
        in_specs=[_rows(tm, D_MODEL), _rows(tm, D_MODEL), _whole((1, 6 * D_MODEL)), vec, _whole((IN_WIDTH, D_MODEL)),
                  _rows(tm, D_MODEL)] + [_rows(tm, 512)] * 7,
        out_specs=[_rows(tm, D_MODEL)] + _ANY + [vec, vec, vec],
        scratch_shapes=[pltpu.VMEM((tm, IN_WIDTH), BF16), pltpu.VMEM((IN_WIDTH, D_MODEL), F32),
                        pltpu.SemaphoreType.DMA(())],
        compiler_params=_params("arbitrary"),
    )(x, dx1, mod, g1, w_in, h1, *dps)


HG_TILE = 512
HG_TILE_CHUNKS = HG_TILE // HG_CHUNK


def _lower_bound(lg_ref):
    return 1.0 / (1.0 + jnp.exp(lg_ref[1:2, :] - lg_ref[0:1, :]))


def _chunk_masks():
    r = lax.broadcasted_iota(jnp.int32, (HG_CHUNK, HG_CHUNK), 0)
    c = lax.broadcasted_iota(jnp.int32, (HG_CHUNK, HG_CHUNK), 1)
    return r >= c, c >= r, (r >= c).astype(F32), (c >= r).astype(F32)


def _hg_fwd_tile(q_ref, f_ref, i_ref, g_ref, lg_ref, og_ref, out_ref, o_ref, st_ref, state, qf_s, kk_s, lf_s):
    @pl.when(pl.program_id(0) == 0)
    def _():
        state[...] = jnp.zeros_like(state)

    lb = _lower_bound(lg_ref)
    f = lb + (1.0 - lb) * _sigmoid(f_ref[...])
    kk_s[...] = 1.0 - f
    lf_s[...] = jnp.log(f)
    qf_s[...] = _silu(q_ref[...])
    causal, _, tri, _ = _chunk_masks()

    def chunk(ci, carry):
        rows = pl.ds(pl.multiple_of(ci * HG_CHUNK, HG_CHUNK), HG_CHUNK)
        srows = pl.ds(pl.multiple_of(ci * HG_DIM, HG_DIM), HG_DIM)
        lf = lf_s[rows, :]
        b = _dot_f32(tri, lf)
        bl = _rowsum(lf)
        ref = 0.5 * bl
        qf, kk, v = qf_s[rows, :], kk_s[rows, :], i_ref[rows, :]
        a_in = (qf * jnp.exp(b)).astype(BF16)
        a_t = (qf * jnp.exp(b - ref)).astype(BF16)
        b_t = (kk * jnp.exp(ref - b)).astype(BF16)
        kd = kk * jnp.exp(bl - b)
        ebl = jnp.exp(bl)
        vb = v.astype(BF16)
        for h in range(HG_HEADS):
            c = slice(h * HG_DIM, (h + 1) * HG_DIM)
            st = state[h]
            st_ref[srows, c] = st
            p = jnp.where(causal, _dot_nt(a_t[:, c], b_t[:, c]), 0.0)
            o_ref[rows, c] = _dot(p.astype(BF16), vb[:, c]) + _dot_nt(a_in[:, c], st.astype(BF16))
            state[h] = st * ebl[:, c] + _dot_tn(vb[:, c], kd[:, c].astype(BF16))
        return carry

    lax.fori_loop(0, HG_TILE_CHUNKS, chunk, 0, unroll=True)
    for h in range(HG_HEADS):
        c = slice(h * HG_DIM, (h + 1) * HG_DIM)
        ohat, _ = _rms(o_ref[:, c])
        out_ref[:, c] = (ohat * og_ref[...] * _silu(g_ref[:, c])).astype(BF16)


def _hg_bwd(hq, hf, hi, hgt, logits, onorm_g, o, states, dout):
    s = hq.shape[0]
    n_tiles = s // HG_TILE

    def body(q_ref, f_ref, i_ref, g_ref, lg_ref, og_ref, o_ref, st_ref, d_ref,
             dq_ref, df_ref, di_ref, dg_ref, dog_ref, dlb_ref, dstate, qf_s, kk_s, lf_s, do_s):
        @pl.when(pl.program_id(0) == 0)
        def _():
            dstate[...] = jnp.zeros_like(dstate)
            dog_ref[...] = jnp.zeros_like(dog_ref)
            dlb_ref[...] = jnp.zeros_like(dlb_ref)

        og = og_ref[...]
        dog = jnp.zeros((1, HG_DIM), F32)
        for h in range(HG_HEADS):
            c = slice(h * HG_DIM, (h + 1) * HG_DIM)
            ohat, rstd = _rms(o_ref[:, c])
            gate = g_ref[:, c]
            d = d_ref[:, c]
            dg_ref[:, c] = (d * (ohat * og) * _dsilu(gate)).astype(BF16)
            dnormed = d * _silu(gate)
            dog += _rowsum(dnormed * ohat)
            do_s[:, c] = _rms_bwd(dnormed * og, ohat, rstd)
        dog_ref[...] += dog

        lb = _lower_bound(lg_ref)
        f = lb + (1.0 - lb) * _sigmoid(f_ref[...])
        kk_s[...] = 1.0 - f
        lf_s[...] = jnp.log(f)
        qf_s[...] = _silu(q_ref[...])
        causal, upper, tri, tri_t = _chunk_masks()

        def chunk(step, carry):
            ci = HG_TILE_CHUNKS - 1 - step
            rows = pl.ds(pl.multiple_of(ci * HG_CHUNK, HG_CHUNK), HG_CHUNK)
            srows = pl.ds(pl.multiple_of(ci * HG_DIM, HG_DIM), HG_DIM)
            lf = lf_s[rows, :]
            b = _dot_f32(tri, lf)
            bl = _rowsum(lf)
            ref = 0.5 * bl
            qf, kk, v, do = qf_s[rows, :], kk_s[rows, :], i_ref[rows, :], do_s[rows, :]
            eb, ebr, erb, ekd, ebl = jnp.exp(b), jnp.exp(b - ref), jnp.exp(ref - b), jnp.exp(bl - b), jnp.exp(bl)
            a_in, a_t, b_t, kd = qf * eb, qf * ebr, kk * erb, kk * ekd
            for h in range(HG_HEADS):
                c = slice(h * HG_DIM, (h + 1) * HG_DIM)
                st, dst = st_ref[srows, c], dstate[h]
                stb, dstb = st.astype(BF16), dst.astype(BF16)
                doh, vh = do[:, c], v[:, c]
                dob, vb = doh.astype(BF16), vh.astype(BF16)
                ain_h, at_h, bt_h, kd_h = a_in[:, c], a_t[:, c], b_t[:, c], kd[:, c]
                atb, btb = at_h.astype(BF16), bt_h.astype(BF16)
                d_ain = _dot(dob, stb)
                p_t = jnp.where(upper, _dot_nt(btb, atb), 0.0).astype(BF16)
                dp = jnp.where(causal, _dot_nt(dob, vb), 0.0).astype(BF16)
                dp_t = jnp.where(upper, _dot_nt(vb, dob), 0.0).astype(BF16)
                di_ref[rows, c] = (_dot(p_t, dob) + _dot_nt(kd_h.astype(BF16), dstb)).astype(BF16)
                d_at = _dot(dp, btb)
                d_bt = _dot(dp_t, atb)
                d_kd = _dot(vb, dstb)
                dqf = d_ain * eb[:, c] + d_at * ebr[:, c]
                dkk = d_bt * erb[:, c] + d_kd * ekd[:, c]
                db = d_ain * ain_h + d_at * atb.astype(F32) - d_bt * btb.astype(F32) - d_kd * kd_h
                dbl = _rowsum(d_kd * kd_h) + _rowsum(dst * st) * ebl[:, c]
                dstate[h] = _dot_tn(dob, ain_h.astype(BF16)) + dst * ebl[:, c]
                dlf = _dot_f32(tri_t, db) + dbl
                qv, fr = q_ref[rows, c], f_ref[rows, c]
                lbh = lb[:, c]
                sg = _sigmoid(fr)
                dfv = dlf / (lbh + (1.0 - lbh) * sg) - dkk
                df_ref[rows, c] = (dfv * (1.0 - lbh) * sg * (1.0 - sg)).astype(BF16)
                dlb_ref[:, c] += _rowsum(dfv * (1.0 - sg))
                dq_ref[rows, c] = (dqf * _dsilu(qv)).astype(BF16)
            return carry

        lax.fori_loop(0, HG_TILE_CHUNKS, chunk, 0, unroll=True)

    rev = pl.BlockSpec((HG_TILE, HG_WIDTH), lambda i: (n_tiles - 1 - i, 0))
    return pl.pallas_call(
        body, name="hg_bwd", grid=(n_tiles,),
        out_shape=[jax.ShapeDtypeStruct((s, HG_WIDTH), BF16)] * 4
        + [jax.ShapeDtypeStruct((1, HG_DIM), F32), jax.ShapeDtypeStruct((1, HG_WIDTH), F32)],
        in_specs=[rev] * 4 + [_whole((2, HG_WIDTH)), _whole((1, HG_DIM)), rev,
                              pl.BlockSpec((HG_TILE_CHUNKS * HG_DIM, HG_WIDTH), lambda i: (n_tiles - 1 - i, 0)), rev],
        out_specs=[rev] * 4 + [_whole((1, HG_DIM)), _whole((1, HG_WIDTH))],
        scratch_shapes=[pltpu.VMEM((HG_HEADS, HG_DIM, HG_DIM), F32)] + [pltpu.VMEM((HG_TILE, HG_WIDTH), F32)] * 4,
        compiler_params=_params("arbitrary"),
    )(hq, hf, hi, hgt, logits, onorm_g, o, states, dout)


TOKEN_GROUP = 16


def _att_geometry(dil, seq=0):
    per_group = TOKEN_GROUP // dil
    ub = ATT_BLOCK // per_group
    if dil == TOKEN_GROUP:
        n_blocks = 2 if seq % (2 * ub * TOKEN_GROUP) == 0 and seq > 0 else 1
    else:
        n_blocks = 4
    return per_group, ub, ATT_WIDTH if dil == 1 else 128, n_blocks


def _att_consts(dil):
    per_group, ub = _att_geometry(dil)[:2]

    def pos(i):
        return i if dil == 1 else (i % ub) * per_group + i // ub

    lane = lax.broadcasted_iota(jnp.int32, (ATT_BLOCK, 128), 1)
    qi = pos(lax.broadcasted_iota(jnp.int32, (2 * ATT_BLOCK, ATT_BLOCK), 0) % ATT_BLOCK)
    kj = pos(lax.broadcasted_iota(jnp.int32, (2 * ATT_BLOCK, ATT_BLOCK), 1))
    return lane < ATT_HEAD_DIM, kj <= qi, lambda off: kj >= qi + off


def _load_tile(ref, dil, r, c, base=0):
    per_group, ub = _att_geometry(dil)[:2]
    if dil == 1:
        return ref[base:base + ATT_BLOCK, c]
    return jnp.concatenate([ref[pl.ds(base + dil * w + r, ub, stride=TOKEN_GROUP), c] for w in range(per_group)], axis=0)


def _store_tile(ref, dil, r, c, val, base=0):
    per_group, ub = _att_geometry(dil)[:2]
    if dil == 1:
        ref[base:base + ATT_BLOCK, c] = val
        return
    for w in range(per_group):
        ref[pl.ds(base + dil * w + r, ub, stride=TOKEN_GROUP), c] = val[w * ub:(w + 1) * ub]


def _stack_heads(x2, first):
    return jnp.concatenate([jnp.where(first, x2, 0.0), jnp.where(first, 0.0, x2)], axis=0)


def _stack_bcast(x2, first):
    other = pltpu.roll(x2, ATT_HEAD_DIM, axis=1)
    return jnp.concatenate([jnp.where(first, x2, other), jnp.where(first, other, x2)], axis=0)


def _unstack_heads(st, first):
    return jnp.where(first, st[:ATT_BLOCK], st[ATT_BLOCK:])


def _att_fwd(q, k, v, dil, behind=()):
    seq, width = q.shape
    _, ub, lanes, nbs = _att_geometry(dil, seq)
    rows = ub * TOKEN_GROUP
    n_steps = seq // (nbs * rows)

    def body(q_ref, k_ref, v_ref, kp_ref, vp_ref, o_ref, lse_ref):
        first, cur_ok, _band = _att_consts(dil)
        inner_ok = _band(0)
        edge_ok = _band(jnp.where(pl.program_id(0) > 0, 0, ATT_BLOCK))
        for r in range(dil):
            for j in range(lanes // 128):
                c = slice(j * 128, (j + 1) * 128)
                kc = vc = None
                for b in range(nbs):
                    base = b * rows
                    prev_ok = edge_ok if b == 0 else inner_ok
                    if b == 0:
                        kp, vp = _load_tile(kp_ref, dil, r, c).astype(BF16), _load_tile(vp_ref, dil, r, c).astype(BF16)
                    else:
                        kp, vp = kc, vc
                    qst = _stack_heads(_load_tile(q_ref, dil, r, c, base) * ATT_SCALE, first).astype(BF16)
                    kc = _load_tile(k_ref, dil, r, c, base).astype(BF16)
                    vc = _load_tile(v_ref, dil, r, c, base).astype(BF16)
                    sc = jnp.where(cur_ok, _dot_nt(qst, kc), NEG)
                    sp = jnp.where(prev_ok, _dot_nt(qst, kp), NEG)
                    mx = jnp.max(jnp.maximum(sc, sp), axis=-1, keepdims=True)
                    pc, pp = jnp.exp(sc - mx), jnp.exp(sp - mx)
                    den = jnp.sum(pc + pp, axis=-1, keepdims=True)
                    ost = (_dot(pc.astype(BF16), vc) + _dot(pp.astype(BF16), vp)) / den
                    lse = jnp.broadcast_to(mx + jnp.log(den), (2 * ATT_BLOCK, 128))
                    _store_tile(o_ref, dil, r, c, _unstack_heads(ost, first), base)
                    _store_tile(lse_ref, dil, r, c, _unstack_heads(lse, first), base)

    slab = pl.BlockSpec((nbs * rows, lanes), lambda n, j: (n, j))
    before = pl.BlockSpec((rows, lanes), lambda n, j: (jnp.maximum(n * nbs - 1, 0), j))
    return pl.pallas_call(
        _behind(body, behind), name=f"att_fwd_d{dil}", grid=(n_steps, width // lanes),
        out_shape=[jax.ShapeDtypeStruct((seq, width), F32)] * 2,
        in_specs=_ANY * len(behind) + [slab, slab, slab, before, before], out_specs=[slab, slab],
        compiler_params=_params("arbitrary", "arbitrary"),
    )(*behind, q, k, v, k, v)


def _att_bwd(q, k, v, do, cc, lse, dil, behind=(), plus=()):
    seq, width = q.shape
    assert not plus or dil == 1
    plus = [a for triple in plus for a in triple]
    _, ub, lanes, nbs = _att_geometry(dil, seq)
    rows = ub * TOKEN_GROUP
    n_blocks = seq // rows
    n_steps = n_blocks // nbs

    def body(q_ref, k_ref, v_ref, do_ref, cc_ref, lse_ref, qx_ref, dox_ref, ccx_ref, lsex_ref, *rest):
        plus_refs, (dq_ref, dk_ref, dv_ref, carry) = rest[:len(plus)], rest[len(plus):]
        first, cur_ok, _band = _att_consts(dil)
        step = pl.program_id(1)
        inner_ok = _band(0)
        edge_ok = _band(jnp.where(step < n_steps - 1, 0, ATT_BLOCK))

        @pl.when(step == 0)
        def _():
            carry[...] = jnp.zeros_like(carry)

        def queries(refs, r, c, base):
            q_r, do_r, lse_r, cc_r = refs
            return (_stack_heads(_load_tile(q_r, dil, r, c, base) * ATT_SCALE, first).astype(BF16),
                    _stack_heads(_load_tile(do_r, dil, r, c, base), first).astype(BF16),
                    _stack_bcast(_load_tile(lse_r, dil, r, c, base), first),
                    _stack_bcast(_load_tile(cc_r, dil, r, c, base), first))

        for r in range(dil):
            for j in range(lanes // 128):
                c = slice(j * 128, (j + 1) * 128)
                own = queries((q_ref, do_ref, lse_ref, cc_ref), r, c, 0)
                left = _load_tile(carry, dil, r, c)
                for b in range(nbs):
                    base = b * rows
                    last = b == nbs - 1
                    next_ok = edge_ok if last else inner_ok
                    if last:
                        following = queries((qx_ref, dox_ref, lsex_ref, ccx_ref), r, c, 0)
                    else:
                        following = queries((q_ref, do_ref, lse_ref, cc_ref), r, c, base + rows)
                    (qst, dost, lse_n, cc_n), (qxst, doxst, lse_x, cc_x) = own, following
                    kb = _load_tile(k_ref, dil, r, c, base).astype(BF16)
                    vb = _load_tile(v_ref, dil, r, c, base).astype(BF16)
                    p_cur = jnp.exp(jnp.where(cur_ok, _dot_nt(qst, kb), NEG) - lse_n)
                    p_next = jnp.exp(jnp.where(next_ok, _dot_nt(qxst, kb), NEG) - lse_x)
                    ds_cur = (p_cur * (_dot_nt(dost, vb) + cc_n)).astype(BF16)
                    ds_next = (p_next * (_dot_nt(doxst, vb) + cc_x)).astype(BF16)
                    dq_own = (left + _unstack_heads(_dot(ds_cur, kb), first)) * ATT_SCALE
                    dk_own = _dot_tn(ds_cur, qst) + _dot_tn(ds_next, qxst)
                    dv_own = _dot_tn(p_cur.astype(BF16), dost) + _dot_tn(p_next.astype(BF16), doxst)
                    for i, (out_ref, val) in enumerate(zip((dq_ref, dk_ref, dv_ref), (dq_own, dk_own, dv_own))):
                        for other in plus_refs[i::3]:
                            val = val + _load_tile(other, dil, r, c, base)
                        _store_tile(out_ref, dil, r, c, val.astype(out_ref.dtype), base)
                    left = _unstack_heads(_dot(ds_next, kb), first)
                    own = following
                _store_tile(carry, dil, r, c, left)

    slab = pl.BlockSpec((nbs * rows, lanes), lambda j, n: (n, j))
    after = pl.BlockSpec((rows, lanes), lambda j, n: (jnp.minimum((n + 1) * nbs, n_blocks - 1), j))
    return pl.pallas_call(
        _behind(body, behind), name=f"att_bwd_d{dil}", grid=(width // lanes, n_steps),
        out_shape=[jax.ShapeDtypeStruct((seq, width), BF16 if plus else F32)] * 3,
        in_specs=_ANY * len(behind) + [slab] * 6 + [after] * 4 + [slab] * len(plus), out_specs=[slab] * 3,
        scratch_shapes=[pltpu.VMEM((rows, lanes), F32)],
        compiler_params=_params("arbitrary", "arbitrary"),
    )(*behind, q, k, v, do, cc, lse, q, do, cc, lse, *plus)


def _branch_weights(lses):
    mx = jnp.maximum(jnp.maximum(lses[0], lses[1]), lses[2])
    es = [jnp.exp(l - mx) for l in lses]
    inv = 1.0 / (es[0] + es[1] + es[2])
    return [e * inv for e in es]


def _att_combine_bwd_tile(d, att, lses, g):
    ahat, rstd = _rms(att)
    datt = _rms_bwd(d * g, ahat, rstd)
    hi = lax.broadcasted_iota(jnp.int32, (ATT_WIDTH, ATT_WIDTH), 0) // ATT_HEAD_DIM
    hj = lax.broadcasted_iota(jnp.int32, (ATT_WIDTH, ATT_WIDTH), 1) // ATT_HEAD_DIM
    same_head = (hi == hj).astype(BF16)
    prod = datt * att
    prod_hi = prod.astype(BF16)
    prod_lo = (prod - prod_hi.astype(F32)).astype(BF16)
    head_sum = _dot(prod_hi, same_head) + _dot(prod_lo, same_head)
    ws = _branch_weights(lses)
    return [w * datt for w in ws], [-w * head_sum for w in ws], _rowsum(d * ahat)


def _out_bwd(dx1, hg, at, mod, w_out, att, lses, att_g):
    s = dx1.shape[0]
    tm = 512
    n_steps = s // tm

    def body(dx_ref, hg_ref, at_ref, mod_ref, w_ref, att_ref, l0, l1, l2, g_ref,
             dhg_ref, do0, do1, do2, cc0, cc1, cc2, dw_ref, dwb_ref, dgate_ref, dg_ref):
        @pl.when(pl.program_id(0) == 0)
        def _():
            dw_ref[...] = jnp.zeros_like(dw_ref)
            dgate_ref[...] = jnp.zeros_like(dgate_ref)
            dg_ref[...] = jnp.zeros_like(dg_ref)

        hg, at, dx = hg_ref[...], at_ref[...], dx_ref[...]
        mix = _dot(hg, w_ref[0:512, :]) + _dot(at, w_ref[512:1024, :])
        dgate_ref[...] += _rowsum(dx * mix)
        dmix = (mod_ref[:, 2 * D_MODEL:3 * D_MODEL] * dx).astype(BF16)
        dhg_ref[...] = _dot_nt(dmix, w_ref[0:512, :])
        dos, ccs, dg_rows = _att_combine_bwd_tile(_dot_nt(dmix, w_ref[512:1024, :]), att_ref[...],
                                                  [l0[...], l1[...], l2[...]], g_ref[...])
        for val, ref in zip(dos + ccs, (do0, do1, do2, cc0, cc1, cc2)):
            ref[...] = val
        dg_ref[...] += dg_rows
        dw_ref[0:512, :] += _dot_tn(hg, dmix)
        dw_ref[512:1024, :] += _dot_tn(at, dmix)

        @pl.when(pl.program_id(0) == n_steps - 1)
        def _():
            dwb_ref[...] = dw_ref[...].astype(BF16)

    tile = _rows(tm, 512)
    square = _whole((D_MODEL, D_MODEL))
    return pl.pallas_call(
        body, name="out_bwd", grid=(n_steps,),
        out_shape=[jax.ShapeDtypeStruct((s, 512), F32)] * 7
        + [jax.ShapeDtypeStruct((D_MODEL, D_MODEL), F32), jax.ShapeDtypeStruct((D_MODEL, D_MODEL), BF16),
           jax.ShapeDtypeStruct((1, D_MODEL), F32), jax.ShapeDtypeStruct((1, ATT_WIDTH), F32)],
        in_specs=[_rows(tm, D_MODEL), tile, tile, _whole((1, 6 * D_MODEL)), square] + [tile] * 4 + [_whole((1, ATT_WIDTH))],
        out_specs=[tile] * 7 + [square, square, _whole((1, D_MODEL)), _whole((1, ATT_WIDTH))],
        compiler_params=_params("arbitrary"),
    )(dx1, hg, at, mod, w_out, att, *lses, att_g)


def _ffn(x, hg, outs, lses, att_g, target, mod, g2, gf, w_out, w_gu, w_down, after=()):
    s = x.shape[0]
    tm = 256

    def body(x_ref, hg_ref, o0, o1, o2, l0, l1, l2, ag_ref, t_ref, mod_ref, g2_ref, gf_ref, wo_ref, wgu_hbm, wd_hbm,
             dx_ref, h2_ref, act_ref, dau_ref, dff_ref, sums_ref, loss_ref, att_ref, at_ref, wgu, wd, au_s, sem):
        @pl.when(pl.program_id(0) == 0)
        def _():
            c1 = pltpu.make_async_copy(wgu_hbm, wgu, sem.at[0])
            c2 = pltpu.make_async_copy(wd_hbm, wd, sem.at[1])
            c1.start()
            c2.start()
            c1.wait()
            c2.wait()
            sums_ref[...] = jnp.zeros_like(sums_ref)
            loss_ref[...] = jnp.zeros_like(loss_ref)

        ws = _branch_weights([l0[...], l1[...], l2[...]])
        att = ws[0] * o0[...] + ws[1] * o1[...] + ws[2] * o2[...]
        att_ref[...] = att
        ahat, _ = _rms(att)
        at = (ahat * ag_ref[...]).astype(BF16)
        at_ref[...] = at
        mix = _dot(hg_ref[...], wo_ref[0:512, :]) + _dot(at, wo_ref[512:1024, :])
        x1v = x_ref[...] + mod_ref[:, 2 * D_MODEL:3 * D_MODEL] * mix
        xhat, rstd = _rms(x1v)
        g2 = g2_ref[...]
        n2 = xhat * g2
        scale2 = 1.0 + mod_ref[:, 4 * D_MODEL:5 * D_MODEL]
        gate2 = mod_ref[:, 5 * D_MODEL:6 * D_MODEL]
        hb = (n2 * scale2 + mod_ref[:, 3 * D_MODEL:4 * D_MODEL]).astype(BF16)
        h2_ref[...] = hb
        au_s[...] = _dot_nt(hb, wgu[...])
        a = au_s[:, 0:D_FF]
        act = (_silu(a) * au_s[:, D_FF:2 * D_FF]).astype(BF16)
        act_ref[...] = act
        ff = _dot(act, wd[...])
        x2 = x1v + gate2 * ff
        nf, rstd_f = _rms(x2)
        gfv = gf_ref[...]
        err = nf * gfv - t_ref[...]
        loss_ref[...] += 0.5 * jnp.sum(_rowsum(err * err), axis=-1, keepdims=True) * (1.0 / D_MODEL)
        dy = err * (1.0 / D_MODEL)
        dx2 = _rms_bwd(dy * gfv, nf, rstd_f)
        dffb = (gate2 * dx2).astype(BF16)
        dff_ref[...] = dffb
        dact = _dot_nt(dffb, wd[...])
        a = au_s[:, 0:D_FF]
        dau_ref[:, 0:D_FF] = (dact * au_s[:, D_FF:2 * D_FF] * _dsilu(a)).astype(BF16)
        dau_ref[:, D_FF:2 * D_FF] = (dact * _silu(a)).astype(BF16)
        dh = _dot(dau_ref[...], wgu[...])
        dn = dh * scale2
        sums_ref[0:1, :] += _rowsum(dh)
        sums_ref[1:2, :] += _rowsum(dh * n2)
        sums_ref[2:3, :] += _rowsum(dx2 * ff)
        sums_ref[3:4, :] += _rowsum(dn * xhat)
        sums_ref[4:5, :] += _rowsum(dy * nf)
        dx_ref[...] = dx2 + _rms_bwd(dn * g2, xhat, rstd)

    vec = _whole((1, D_MODEL))
    hbm = pl.BlockSpec(memory_space=pl.ANY)
    half = _rows(tm, 512)
    return pl.pallas_call(
        _behind(body, after), name="ffn", grid=(s // tm,),
        out_shape=[jax.ShapeDtypeStruct((s, D_MODEL), F32), jax.ShapeDtypeStruct((s, D_MODEL), BF16),
                   jax.ShapeDtypeStruct((s, D_FF), BF16), jax.ShapeDtypeStruct((s, 2 * D_FF), BF16),
                   jax.ShapeDtypeStruct((s, D_MODEL), BF16), jax.ShapeDtypeStruct((8, D_MODEL), F32),
                   jax.ShapeDtypeStruct((1, 128), F32), jax.ShapeDtypeStruct((s, ATT_WIDTH), F32),
                   jax.ShapeDtypeStruct((s, ATT_WIDTH), BF16)],
        in_specs=_ANY * len(after) + [_rows(tm, D_MODEL)] + [half] * 7 + [_whole((1, ATT_WIDTH)), _rows(tm, D_MODEL),
                                                                          _whole((1, 6 * D_MODEL)), vec, vec,
                                                                          _whole((D_MODEL, D_MODEL)), hbm, hbm],
        out_specs=[_rows(tm, D_MODEL), _rows(tm, D_MODEL), _rows(tm, D_FF), _rows(tm, 2 * D_FF), _rows(tm, D_MODEL),
                   _whole((8, D_MODEL)), _whole((1, 128)), half, half],
        scratch_shapes=[pltpu.VMEM((2 * D_FF, D_MODEL), BF16), pltpu.VMEM((D_FF, D_MODEL), BF16),
                        pltpu.VMEM((tm, 2 * D_FF), F32), pltpu.SemaphoreType.DMA((2,))],
        compiler_params=_params("arbitrary", vmem=V7X_VMEM_MOST),
    )(*after, x, hg, *outs, *lses, att_g, target, mod, g2, gf, w_out, w_gu, w_down)


def _weight_grad(a, b, name, rounded=False):
    s, m = a.shape
    n = b.shape[1]
    ts = min(s, 2048)
    n_steps = s // ts
    tm = max(t for t in range(128, m + 1, 128) if m % t == 0 and t * n * 4 <= 6 * 1024 * 1024)

    def body(a_ref, b_ref, o_ref, *ob_ref):
        @pl.when(pl.program_id(1) == 0)
        def _():
            o_ref[...] = jnp.zeros_like(o_ref)

        o_ref[...] += _dot_tn(a_ref[...], b_ref[...])
        if rounded:
            @pl.when(pl.program_id(1) == n_steps - 1)
            def _():
                ob_ref[0][...] = o_ref[...].astype(BF16)

    tile = pl.BlockSpec((tm, n), lambda j, i: (j, 0))
    return pl.pallas_call(
        body, name=name, grid=(m // tm, n_steps),
        out_shape=[jax.ShapeDtypeStruct((m, n), F32)] + [jax.ShapeDtypeStruct((m, n), BF16)] * rounded,
        in_specs=[pl.BlockSpec((ts, tm), lambda j, i: (i, j)), pl.BlockSpec((ts, n), lambda j, i: (i, 0))],
        out_specs=[tile] + [tile] * rounded,
        compiler_params=_params("parallel", "arbitrary"),
    )(a, b)


def _adamw_math(w, g, m, v):
    m = ADAM_B1 * m + (1.0 - ADAM_B1) * g
    v = ADAM_B2 * v + (1.0 - ADAM_B2) * (g * g)
    m_hat = m / (1.0 - ADAM_B1 ** ADAM_STEP)
    v_hat = v / (1.0 - ADAM_B2 ** ADAM_STEP)
    delta = -ADAM_LR * (m_hat / (jnp.sqrt(v_hat) + ADAM_EPS) + ADAM_WD * w)
    return delta, m, v


def _adamw_shard(where, w, m, v, partial, got, name, after=()):
    r, c = w.shape
    tr = _shard_rows(r)
    lead = partial.ndim - 2
    n_got = got.shape[0]

    def body(where_ref, *refs):
        w_ref, m_ref, v_ref, own_ref, *rest = refs[len(after):]
        got_refs, (grad_ref, d_ref, nm_ref, nv_ref) = rest[:n_got], rest[n_got:]
        g = own_ref[...]
        for g_ref in got_refs:
            g = g + g_ref[...].astype(F32)
        grad_ref[...] = g
        d_ref[...], nm_ref[...], nv_ref[...] = _adamw_math(w_ref[...], g, m_ref[...], v_ref[...])

    tile = pl.BlockSpec((tr, c), lambda i, where_ref: (i, 0))
    own = pl.BlockSpec((None,) * lead + (tr, c), lambda i, where_ref: (*[where_ref[d] for d in range(lead)], i, 0))
    part = [pl.BlockSpec((None, tr, c), functools.partial(lambda j, i, where_ref: (j, i, 0), j)) for j in range(n_got)]
    return pl.pallas_call(
        body, name=name,
        grid_spec=pltpu.PrefetchScalarGridSpec(num_scalar_prefetch=1, grid=(r // tr,),
                                               in_specs=_ANY * len(after) + [tile] * 3 + [own] + part, out_specs=[tile] * 4),
        out_shape=[jax.ShapeDtypeStruct((r, c), F32)] * 4, compiler_params=_params("parallel"),
    )(where, *after, w, m, v, partial, *[got] * n_got)


def _small_update(small_all, dmod_blocks, c_all, logits, w_ada, m_ada, v_ada, smalls, after=()):
    def body(sm_ref, dm_ref, c_ref, lg_ref, wa_ref, ma_ref, va_ref, *rest):
        ins, outs = rest[:21], rest[21:]
        _, me = _flip(0)
        tot = sm_ref[0:1, :]
        for i in range(1, N_DEV):
            tot = tot + sm_ref[i:i + 1, :]
        loss_ref = outs[0]
        loss_ref[...] = tot[:, SM_LOSS:SM_LOSS + 128]
        g_ada = lax.dot_general(_silu(c_ref[...]), dm_ref[me], (((0,), (0,)), ((), ())),
                                preferred_element_type=F32, precision=HIGHEST)
        outs[1][...] = g_ada
        outs[2][...], outs[3][...], outs[4][...] = _adamw_math(wa_ref[...], g_ada, ma_ref[...], va_ref[...])
        p0 = _lower_bound(lg_ref)
        dl0 = tot[:, SM_LB:SM_LB + 512] * p0 * (1.0 - p0)
        grads = [tot[:, SM_MOD:SM_MOD + 6 * D_MODEL], tot[:, SM_G1:SM_G1 + D_MODEL], tot[:, SM_G2:SM_G2 + D_MODEL],
                 tot[:, SM_GF:SM_GF + D_MODEL], tot[:, SM_ATT:SM_ATT + 512], tot[:, SM_HG:SM_HG + 128],
                 jnp.where(lax.broadcasted_iota(jnp.int32, (2, 512), 0) == 0, dl0, -dl0)]
        for i, g in enumerate(grads):
            w_ref, m_ref, v_ref = ins[3 * i:3 * i + 3]
            o = outs[5 + 4 * i:9 + 4 * i]
            o[0][...] = g
            o[1][...], o[2][...], o[3][...] = _adamw_math(w_ref[...], g, m_ref[...], v_ref[...])

    flat = [t for trio in smalls for t in trio]
    vm = pl.BlockSpec(memory_space=pltpu.VMEM)
    out_shape = [jax.ShapeDtypeStruct((1, 128), F32)] + [jax.ShapeDtypeStruct(w_ada.shape, F32)] * 4
    for trio in smalls:
        out_shape += [jax.ShapeDtypeStruct(trio[0].shape, F32)] * 4
    return pl.pallas_call(
        _behind(body, after), name="small_update", out_shape=out_shape,
        in_specs=_ANY * len(after) + [vm] * (7 + len(flat)), out_specs=[vm] * len(out_shape),
        compiler_params=pltpu.CompilerParams(vmem_limit_bytes=V7X_VMEM_LIMIT),
    )(*after, small_all, dmod_blocks, c_all, logits, w_ada, m_ada, v_ada, *flat)


def kernel(x, c, w_ada, b_ada, norm1_g, w_in, hg_lb_logits, hg_onorm_g, att_onorm_g, w_out, norm2_g, w_gate_up, w_down, final_g, loss_target, m_w_ada, m_b_ada, m_norm1_g, m_w_in, m_hg_lb_logits, m_hg_onorm_g, m_att_onorm_g, m_w_out, m_norm2_g, m_w_gate_up, m_w_down, m_final_g, v_w_ada, v_b_ada, v_norm1_g, v_w_in, v_hg_lb_logits, v_hg_onorm_g, v_att_onorm_g, v_w_out, v_norm2_g, v_w_gate_up, v_w_down, v_final_g):
    x2d, target = x[0], loss_target[0]
    seq = x2d.shape[0]
    assert seq % (ATT_BLOCK * max(DILATIONS)) == 0 and seq % HG_TILE == 0
    gf = final_g.reshape(1, D_MODEL)

    c_all = _exchange_small(c.reshape(8, D_MODEL // 8), None, "gather_c").reshape(N_DEV, D_MODEL)
    ada = _ada_rows(c_all, w_ada[0], b_ada)
    mod = _exchange_small(ada, 1, "scatter_mod").reshape(1, 6 * D_MODEL)

    core = lax.axis_index("c").astype(jnp.int32).reshape(1)
    chip = (2 * lax.axis_index("x") + lax.axis_index("y")).astype(jnp.int32).reshape(1)
    me = 4 * lax.axis_index("x") + 2 * lax.axis_index("y") + lax.axis_index("c")

    g_in, = _gather_weights([w_in[0].T.astype(BF16)])
    w_in_b = g_in.reshape(IN_WIDTH, D_MODEL)
    rest_shards = [w_out[0].astype(BF16), w_gate_up[0].T.astype(BF16), w_down[0].astype(BF16)]
    lands = [lax.empty((N_DEV,) + s.shape, BF16) for s in rest_shards]
    g_send, g_recv, g_srcs, g_lands, tok = _copies_start("gather_rest_start", _plan_gather_own, 12, rest_shards, lands, [w_in_b, mod])
    flight = {}

    def stage(name, *vals):
        if name == "attention_begun":
            flight["shards"], got = _copies_wait("gather_rest_wait", _plan_gather_own, g_send, g_recv, g_srcs, g_lands, list(vals))
            flight["pass"] = _copies_start("gather_pass_start", _plan_gather_pass, 9, [], got, [])
            return [flight["pass"][4]]
        if name == "mixer_weights_done":
            shapes = [(4, 2, D_MODEL // N_DEV, D_MODEL), (4, 2, 2 * D_FF // N_DEV, D_MODEL), (4, 2, D_FF // N_DEV, D_MODEL)]
            flight["grads"] = [g32.reshape(sh) for (g32, _), sh in zip(vals, shapes)]
            rounded = [g16.reshape(sh) for (_, g16), sh in zip(vals, shapes)]
            direct_lands = [lax.empty((N_DEV - 1,) + sh[2:], BF16) for sh in shapes]
            flight["direct"] = _copies_start("reduce_rest_start", _plan_reduce_direct, 21, rounded, direct_lands, [])
            return [flight["direct"][4]]
        raise ValueError(name)

    def rest_weights(after):
        s, r, _, p_lands, _ = flight["pass"]
        _, got = _copies_wait("gather_pass_wait", _plan_gather_pass, s, r, [], p_lands, [after])
        full = [lax.dynamic_update_index_in_dim(g, shard, me, 0) for g, shard in zip(got, flight["shards"])]
        return full[0].reshape(D_MODEL, D_MODEL), full[1].reshape(2 * D_FF, D_MODEL), full[2].reshape(D_FF, D_MODEL)

    grad_x, dw_in, small = _block_step(x2d, target, mod, norm1_g, hg_lb_logits, hg_onorm_g, att_onorm_g, norm2_g, gf,
                                       w_in_b, rest_weights, stage, [tok])

    g_in8 = dw_in.reshape(4, 2, IN_WIDTH // N_DEV, D_MODEL)
    in_pairs = _copies_start("reduce_pairs_in_start", _plan_reduce_pairs, 4, [g_in8], [lax.empty((4,) + g_in8.shape[2:], F32)], [])
    s, r, srcs, d_lands, _ = flight["direct"]
    _, recv_rest = _copies_wait("reduce_rest_wait", _plan_reduce_direct, s, r, srcs, d_lands, [in_pairs[4]])
    small_rows = jnp.pad(small, ((0, 0), (0, SM_PADDED - SM_WIDTH))).reshape(SM_PADDED // 128, 128)
    small_all = _exchange_small(small_rows, None, "gather_small", [in_pairs[4]]).reshape(N_DEV, SM_PADDED)[:, :SM_WIDTH]
    in_grads, got_in = _copies_wait("reduce_pairs_in_wait", _plan_reduce_pairs, in_pairs[0], in_pairs[1], in_pairs[2], in_pairs[3],
                                    [small_all])
    in_s32, in_s16 = _pair_sum(core, in_grads[0], got_in[0], "pair_sum_in")
    in_chips = _copies_start("reduce_chips_in_start", _plan_reduce_chips, 3, [in_s16], [lax.empty((3,) + in_s16.shape[1:], BF16)], [])
    big, updated = {}, []
    rest_params = [("w_out", w_out, m_w_out, v_w_out), ("w_gate_up", w_gate_up, m_w_gate_up, v_w_gate_up), ("w_down", w_down, m_w_down, v_w_down)]
    mine = jnp.concatenate([chip, core])
    for (n, w, m, v), g32, got in zip(rest_params, flight["grads"], recv_rest):
        if n == "w_gate_up":
            outs4 = _adamw_shard(mine, w[0].T, m[0].T, v[0].T, g32, got, f"adamw_{n}", [in_chips[4]])
            big[n] = [t.T[None] for t in outs4]
        else:
            outs4 = _adamw_shard(mine, w[0], m[0], v[0], g32, got, f"adamw_{n}", [in_chips[4]])
            big[n] = [t[None] for t in outs4]
        updated.append(outs4[3])
    smalls = [(b_ada, m_b_ada, v_b_ada), (norm1_g, m_norm1_g, v_norm1_g), (norm2_g, m_norm2_g, v_norm2_g),
              (gf, m_final_g.reshape(1, D_MODEL), v_final_g.reshape(1, D_MODEL)),
              (att_onorm_g, m_att_onorm_g, v_att_onorm_g), (hg_onorm_g, m_hg_onorm_g, v_hg_onorm_g),
              (hg_lb_logits, m_hg_lb_logits, v_hg_lb_logits)]
    dmod_blocks = small_all[:, :6 * D_MODEL].reshape(N_DEV, N_DEV, 6 * D_MODEL // N_DEV).transpose(1, 0, 2)
    res = _small_update(small_all, dmod_blocks, c_all, hg_lb_logits, w_ada[0], m_w_ada[0], v_w_ada[0], smalls, [in_chips[4]])
    _, recv_in = _copies_wait("reduce_chips_in_wait", _plan_reduce_chips, in_chips[0], in_chips[1], in_chips[2], in_chips[3],
                              [res[0]] + updated)
    big["w_in"] = [t.T[None] for t in _adamw_shard(chip, w_in[0].T, m_w_in[0].T, v_w_in[0].T, in_s32, recv_in[0], "adamw_w_in")]
    loss = res[0][0, 0]
    ada4 = [t[None] for t in res[1:5]]
    sm4 = {n: list(res[5 + 4 * i:9 + 4 * i]) for i, n in enumerate(["b_ada", "norm1_g", "norm2_g", "final_g", "att", "hg", "lb"])}
    sm4["final_g"] = [t.reshape(D_MODEL) for t in sm4["final_g"]]

    order = [ada4, sm4["b_ada"], sm4["norm1_g"], big["w_in"], sm4["lb"], sm4["hg"], sm4["att"], big["w_out"], sm4["norm2_g"],
             big["w_gate_up"], big["w_down"], sm4["final_g"]]
    return (loss, grad_x[None], *[o[0] for o in order], *[o[1] for o in order], *[o[2] for o in order], *[o[3] for o in order])


def _block_step(x2d, target, mod, norm1_g, hg_lb_logits, hg_onorm_g, att_onorm_g, norm2_g, gf, w_in_b, rest_weights, stage,
                after=()):
    h1, hq, hf, hi, hgt, aq, ak, av, hg_out, hg_o, hg_states = _in_fwd(x2d, mod, norm1_g, w_in_b, hg_lb_logits, hg_onorm_g, after)
    branch = [_att_fwd(aq, ak, av, d) for d in DILATIONS[:2]]
    behind = stage("attention_begun", branch[0][0], branch[1][0])
    branch += [_att_fwd(aq, ak, av, d, behind) for d in DILATIONS[2:]]
    outs = [b[0] for b in branch]
    lses = [b[1] for b in branch]
    w_out_b, w_gu_b, w_down_b = rest_weights(outs[-1])

    dx1, h2, act, dau, dff, ffn_sums, loss_part, att, att_out = _ffn(
        x2d, hg_out, outs, lses, att_onorm_g, target, mod, norm2_g, gf, w_out_b, w_gu_b, w_down_b)
    dw_gu = _weight_grad(dau, h2, "dw_gate_up", rounded=True)
    dw_down = _weight_grad(act, dff, "dw_down", rounded=True)

    back = _out_bwd(dx1, hg_out, att_out, mod, w_out_b, att, lses, att_onorm_g)
    dhg, dos, ccs = back[0], back[1:4], back[4:7]
    dw_out, dw_out_b, dgate1, d_att_g = back[7:11]
    behind = stage("mixer_weights_done", (dw_out, dw_out_b), dw_gu, dw_down)
    dilated = [_att_bwd(aq, ak, av, dos[i], ccs[i], lses[i], d, behind) for i, d in enumerate(DILATIONS) if d > 1]
    datt = _att_bwd(aq, ak, av, dos[0], ccs[0], lses[0], DILATIONS[0], behind, plus=dilated)
    dhq, dhf, dhi, dhgt, d_hg_g, d_lb = _hg_bwd(hq, hf, hi, hgt, hg_lb_logits, hg_onorm_g, hg_o, hg_states, dhg)
    dps = [dhq, dhf, dhi, dhgt] + list(datt)
    grad_x, dw_in, dshift1, dscale1, d_g1 = _in_bwd(x2d, dx1, mod, norm1_g, w_in_b, h1, dps)
    small = jnp.concatenate([dshift1, dscale1, dgate1, ffn_sums[0:1], ffn_sums[1:2], ffn_sums[2:3], d_g1, ffn_sums[3:4],
                             ffn_sums[4:5], d_att_g, d_lb, d_hg_g, loss_part], axis=1)
    return grad_x, dw_in, small
```

```python
import functools

import jax
import jax.numpy as jnp
from jax import lax
from jax.experimental import pallas as pl
from jax.experimental.pallas import tpu as pltpu

F32 = jnp.float32
BF16 = jnp.bfloat16
HIGHEST = lax.Precision.HIGHEST
MESH = pl.DeviceIdType.MESH

D_MODEL = 1024
N_DEV = 8
HG_HEADS = 4
HG_DIM = 128
HG_WIDTH = HG_HEADS * HG_DIM
HG_CHUNK = 128
ATT_WIDTH = 512
ATT_HEAD_DIM = 64
ATT_BLOCK = 128
DILATIONS = (1, 4, 16)
ATT_SCALE = ATT_HEAD_DIM ** -0.5
D_FF = 2816
IN_WIDTH = 7 * 512
RMS_EPS = 1e-6
NEG = -1e30

ADAM_LR = 0.001
ADAM_B1 = 0.9
ADAM_B2 = 0.999
ADAM_EPS = 1e-08
ADAM_WD = 0.01
ADAM_STEP = 10

V7X_VMEM_LIMIT = 56 * 1024 * 1024
V7X_VMEM_MOST = 60 * 1024 * 1024

SM_MOD = 0
SM_G1 = 6 * D_MODEL
SM_G2 = 7 * D_MODEL
SM_GF = 8 * D_MODEL
SM_ATT = 9 * D_MODEL
SM_LB = 9 * D_MODEL + 512
SM_HG = 10 * D_MODEL
SM_LOSS = 10 * D_MODEL + 128
SM_WIDTH = 10 * D_MODEL + 256
SM_PADDED = 88 * 128


def _params(*sem, vmem=V7X_VMEM_LIMIT):
    return pltpu.CompilerParams(dimension_semantics=sem, vmem_limit_bytes=vmem)


def _dot(a, b):
    return jnp.dot(a, b, preferred_element_type=F32)


def _dot_nt(a, b):
    return lax.dot_general(a, b, (((1,), (1,)), ((), ())), preferred_element_type=F32)


def _dot_tn(a, b):
    return lax.dot_general(a, b, (((0,), (0,)), ((), ())), preferred_element_type=F32)


def _dot_f32(a, b):
    return jnp.dot(a, b, preferred_element_type=F32, precision=HIGHEST)


def _sigmoid(x):
    return 1.0 / (1.0 + jnp.exp(-x))


def _silu(x):
    return x * _sigmoid(x)


def _dsilu(x):
    s = _sigmoid(x)
    return s * (1.0 + x * (1.0 - s))


def _rms(x):
    rstd = lax.rsqrt(jnp.mean(x * x, axis=-1, keepdims=True) + RMS_EPS)
    return x * rstd, rstd


def _rms_bwd(dn, xhat, rstd):
    return rstd * (dn - xhat * jnp.mean(dn * xhat, axis=-1, keepdims=True))


def _rowsum(x):
    return jnp.sum(x, axis=0, keepdims=True)


def _rows(tm, n):
    return pl.BlockSpec((tm, n), lambda i: (i, 0))


def _whole(shape):
    return pl.BlockSpec(shape, lambda i: (0,) * len(shape))


def _mesh_pos():
    return lax.axis_index("x"), lax.axis_index("y"), lax.axis_index("c")


def _flip(k):
    x, y, c = _mesh_pos()
    px = 1 - x if k & 4 else x
    py = 1 - y if k & 2 else y
    pc = 1 - c if k & 1 else c
    return (px, py, pc), 4 * px + 2 * py + pc


_ANY = [pl.BlockSpec(memory_space=pl.ANY)]


def _behind(body, after):
    return lambda *refs: body(*refs[len(after):])


def _exchange_small(x, rows_per_peer, name, after=()):
    r_all, cols = x.shape
    r_out = r_all if rows_per_peer is None else rows_per_peer

    def body(x_ref, out_ref, send_sems, recv_sems):
        _, me = _flip(0)

        def src(pid):
            if rows_per_peer is None:
                return x_ref
            return x_ref.at[pl.ds(pl.multiple_of(pid * r_out, r_out), r_out), :]

        if rows_per_peer is None:
            out_ref[me] = x_ref[...]
        else:
            out_ref[me] = x_ref[pl.ds(pl.multiple_of(me * r_out, r_out), r_out), :]
        sends = []
        for k in range(1, N_DEV):
            dev, pid = _flip(k)
            cp = pltpu.make_async_remote_copy(src_ref=src(pid), dst_ref=out_ref.at[me], send_sem=send_sems.at[k - 1],
                                              recv_sem=recv_sems.at[k - 1], device_id=dev, device_id_type=MESH)
            cp.start()
            sends.append(cp)
        for k in range(1, N_DEV):
            dev, pid = _flip(k)
            pltpu.make_async_remote_copy(src_ref=src(pid), dst_ref=out_ref.at[pid], send_sem=send_sems.at[k - 1],
                                         recv_sem=recv_sems.at[k - 1], device_id=dev, device_id_type=MESH).wait_recv()
        for cp in sends:
            cp.wait_send()

    return pl.pallas_call(
        _behind(body, after), name=name,
        out_shape=jax.ShapeDtypeStruct((N_DEV, r_out, cols), x.dtype),
        in_specs=_ANY * len(after) + [pl.BlockSpec(memory_space=pltpu.VMEM)],
        out_specs=pl.BlockSpec(memory_space=pltpu.VMEM),
        scratch_shapes=[pltpu.SemaphoreType.DMA((N_DEV - 1,)), pltpu.SemaphoreType.DMA((N_DEV - 1,))],
    )(*after, x)


def _gather_weights(shards):
    n = len(shards)

    def body(*refs):
        xs, outs = refs[:n], refs[n:2 * n]
        send_sems, recv_sems, local_sems = refs[2 * n:]
        x, y, c = _mesh_pos()
        me, sibling = (x, y, c), (x, y, 1 - c)
        chips = [(1 - x, y), (x, 1 - y), (1 - x, 1 - y)]

        def blk(a, px, py, pc):
            return outs[a].at[4 * px + 2 * py + pc]

        def copy(a, k, block, to, src=None):
            return pltpu.make_async_remote_copy(
                src_ref=blk(a, *block) if src is None else src, dst_ref=blk(a, *block),
                send_sem=send_sems.at[a * 7 + k], recv_sem=recv_sems.at[a * 7 + k], device_id=to, device_id_type=MESH)

        mine = [pltpu.make_async_copy(xs[a], blk(a, *me), local_sems.at[a]) for a in range(n)]
        for cp in mine:
            cp.start()
        first = []
        for a in range(n):
            first.append(copy(a, 0, me, sibling, src=xs[a]))
            first += [copy(a, 1 + j, me, (*chip, c), src=xs[a]) for j, chip in enumerate(chips)]
        for cp in first:
            cp.start()
        passed = []
        for j, chip in enumerate(chips):
            for a in range(n):
                copy(a, 1 + j, (*chip, c), me).wait_recv()
                cp = copy(a, 4 + j, (*chip, c), sibling)
                cp.start()
                passed.append(cp)
        for a in range(n):
            copy(a, 0, sibling, me).wait_recv()
            for j, chip in enumerate(chips):
                copy(a, 4 + j, (*chip, 1 - c), me).wait_recv()
        for cp in first + passed:
            cp.wait_send()
        for cp in mine:
            cp.wait()

    hbm = pl.BlockSpec(memory_space=pl.ANY)
    return pl.pallas_call(
        body, name="gather_weights",
        out_shape=[jax.ShapeDtypeStruct((N_DEV,) + s.shape, s.dtype) for s in shards],
        in_specs=[hbm] * n, out_specs=[hbm] * n,
        scratch_shapes=[pltpu.SemaphoreType.DMA((7 * n,)), pltpu.SemaphoreType.DMA((7 * n,)), pltpu.SemaphoreType.DMA((n,))],
    )(*shards)


_HBM = pl.BlockSpec(memory_space=pltpu.HBM)
_SEM = pl.BlockSpec(memory_space=pltpu.SEMAPHORE)
_DATAFLOW = pltpu.SideEffectType.DATAFLOW_SIDE_EFFECTING


def _copies_start(name, plan, n_copies, srcs, lands, after):
    bufs = list(srcs) + list(lands)
    nb = len(bufs)

    def body(*refs):
        ins, send_sems, recv_sems, token = refs[:nb], refs[nb + len(after)], refs[nb + len(after) + 1], refs[-1]
        for i, (src, dst, dev) in enumerate(plan(ins[:len(srcs)], ins[len(srcs):])):
            pltpu.make_async_remote_copy(src_ref=src, dst_ref=dst, send_sem=send_sems.at[i], recv_sem=recv_sems.at[i],
                                         device_id=dev, device_id_type=MESH).start()
        token[...] = jnp.zeros_like(token)

    outs = pl.pallas_call(
        body, name=name,
        out_shape=(pltpu.SemaphoreType.DMA((n_copies,)), pltpu.SemaphoreType.DMA((n_copies,)),
                   *[pltpu.HBM(b.shape, b.dtype) for b in bufs], jax.ShapeDtypeStruct((8, 128), F32)),
        in_specs=[_HBM] * nb + [pl.BlockSpec(memory_space=pl.ANY)] * len(after),
        out_specs=(_SEM, _SEM, *[_HBM] * nb, pl.BlockSpec(memory_space=pltpu.VMEM)),
        input_output_aliases={i: 2 + i for i in range(nb)},
        compiler_params=pltpu.CompilerParams(has_side_effects=_DATAFLOW),
    )(*[pltpu.with_memory_space_constraint(b, pltpu.HBM) for b in bufs], *after)
    return outs[0], outs[1], list(outs[2:2 + len(srcs)]), list(outs[2 + len(srcs):2 + nb]), outs[-1]


def _copies_wait(name, plan, send_sems, recv_sems, srcs, lands, after):
    bufs = list(srcs) + list(lands)
    nb = len(bufs)

    def body(*refs):
        ins, send_ref, recv_ref = refs[:nb], refs[nb], refs[nb + 1]
        for i, (src, dst, dev) in enumerate(plan(ins[:len(srcs)], ins[len(srcs):])):
            cp = pltpu.make_async_remote_copy(src_ref=src, dst_ref=dst, send_sem=send_ref.at[i], recv_sem=recv_ref.at[i],
                                              device_id=dev, device_id_type=MESH)
            cp.wait_send()
            cp.wait_recv()

    outs = pl.pallas_call(
        body, name=name, out_shape=[pltpu.HBM(b.shape, b.dtype) for b in bufs],
        in_specs=[_HBM] * nb + [_SEM, _SEM] + [pl.BlockSpec(memory_space=pl.ANY)] * len(after), out_specs=[_HBM] * nb,
        input_output_aliases={i: i for i in range(nb)},
        compiler_params=pltpu.CompilerParams(has_side_effects=_DATAFLOW),
    )(*bufs, send_sems, recv_sems, *after)
    return list(outs[:len(srcs)]), list(outs[len(srcs):])


def _plan_gather_own(srcs, lands):
    _, me = _flip(0)
    return [(srcs[a], lands[a].at[me], _flip(k)[0]) for a in range(len(srcs)) for k in (1, 4, 2, 6)]


def _plan_gather_pass(srcs, lands):
    sibling = _flip(1)[0]
    plan = []
    for land in lands:
        for k in (4, 2, 6):
            block = land.at[_flip(k)[1]]
            plan.append((block, block, sibling))
    return plan


def _plan_reduce_pairs(srcs, lands):
    x, y, c = _mesh_pos()
    return [(srcs[a].at[chip, 1 - c], lands[a].at[chip], (x, y, 1 - c)) for a in range(len(srcs)) for chip in range(4)]


def _plan_reduce_chips(srcs, lands):
    plan = []
    for a in range(len(srcs)):
        for j, k in enumerate((4, 2, 6)):
            dev = _flip(k)[0]
            plan.append((srcs[a].at[2 * dev[0] + dev[1]], lands[a].at[j], dev))
    return plan


def _plan_reduce_direct(srcs, lands):
    plan = []
    for a in range(len(srcs)):
        for k in range(1, N_DEV):
            dev = _flip(k)[0]
            plan.append((srcs[a].at[2 * dev[0] + dev[1], dev[2]], lands[a].at[k - 1], dev))
    return plan


def _shard_rows(r):
    return r // 2 if r % 32 == 0 else r


def _pair_sum(core, grads, got, name):
    _, _, r, c = grads.shape
    tr = _shard_rows(r)

    def body(core_ref, a_ref, b_ref, o_ref, ob_ref):
        s = a_ref[...] + b_ref[...]
        o_ref[...] = s
        ob_ref[...] = s.astype(BF16)

    spec = pl.BlockSpec((None, tr, c), lambda i, j, core_ref: (i, j, 0))
    return pl.pallas_call(
        body, name=name,
        grid_spec=pltpu.PrefetchScalarGridSpec(
            num_scalar_prefetch=1, grid=(4, r // tr),
            in_specs=[pl.BlockSpec((None, None, tr, c), lambda i, j, core_ref: (i, core_ref[0], j, 0)), spec],
            out_specs=[spec, spec]),
        out_shape=[jax.ShapeDtypeStruct((4, r, c), F32), jax.ShapeDtypeStruct((4, r, c), BF16)],
        compiler_params=_params("parallel", "parallel"),
    )(core, grads, got)


def _ada_rows(c_all, w_ada, b_ada):
    n_cols = w_ada.shape[1]

    def body(c_ref, w_ref, b_ref, o_ref):
        _, me = _flip(0)
        bias = b_ref[:, pl.ds(pl.multiple_of(me * n_cols, 128), n_cols)]
        o_ref[...] = _dot_f32(_silu(c_ref[...]), w_ref[...]) + bias

    return pl.pallas_call(
        body, name="ada_rows", out_shape=jax.ShapeDtypeStruct((N_DEV, n_cols), F32),
        in_specs=[pl.BlockSpec(memory_space=pltpu.VMEM)] * 3, out_specs=pl.BlockSpec(memory_space=pltpu.VMEM),
    )(c_all, w_ada, b_ada)


def _in_fwd(x, mod, g1, w_in, logits, onorm_g, after=()):
    s = x.shape[0]
    tm = HG_TILE

    def body(x_ref, mod_ref, g_ref, w_ref, lg_ref, og_ref, h_ref, *rest):
        groups, (out_ref, o_ref, st_ref, state, qf_s, kk_s, lf_s) = rest[:7], rest[7:]
        xhat, _ = _rms(x_ref[...])
        h = (xhat * g_ref[...]) * (1.0 + mod_ref[:, D_MODEL:2 * D_MODEL]) + mod_ref[:, 0:D_MODEL]
        hb = h.astype(BF16)
        h_ref[...] = hb
        for j, o_ref_j in enumerate(groups):
            o_ref_j[...] = _dot_nt(hb, w_ref[j * 512:(j + 1) * 512, :])
        _hg_fwd_tile(*groups[:4], lg_ref, og_ref, out_ref, o_ref, st_ref, state, qf_s, kk_s, lf_s)

    tile = _rows(tm, 512)
    return pl.pallas_call(
        _behind(body, after), name="in_fwd", grid=(s // tm,),
        out_shape=[jax.ShapeDtypeStruct((s, D_MODEL), BF16)] + [jax.ShapeDtypeStruct((s, 512), F32)] * 7
        + [jax.ShapeDtypeStruct((s, HG_WIDTH), BF16), jax.ShapeDtypeStruct((s, HG_WIDTH), F32),
           jax.ShapeDtypeStruct((s // HG_CHUNK * HG_DIM, HG_WIDTH), F32)],
        in_specs=_ANY * len(after)
        + [_rows(tm, D_MODEL), _whole((1, 6 * D_MODEL)), _whole((1, D_MODEL)), _whole((IN_WIDTH, D_MODEL)),
           _whole((2, HG_WIDTH)), _whole((1, HG_DIM))],
        out_specs=[_rows(tm, D_MODEL)] + [tile] * 7 + [tile, tile, _rows(HG_TILE_CHUNKS * HG_DIM, HG_WIDTH)],
        scratch_shapes=[pltpu.VMEM((HG_HEADS, HG_DIM, HG_DIM), F32)] + [pltpu.VMEM((HG_TILE, HG_WIDTH), F32)] * 3,
        compiler_params=_params("arbitrary"),
    )(*after, x, mod, g1, w_in, logits, onorm_g)


def _in_bwd(x, dx1, mod, g1, w_in, h1, dps):
    s = x.shape[0]
    tm = 512
    n_steps = s // tm
    assert len(dps) == 7 and all(p.dtype == BF16 for p in dps)

    def body(x_ref, dx_ref, mod_ref, g_ref, w_ref, h_ref, *rest):
        dp_refs, (gx_ref, dw_ref, dsh_ref, dsc_ref, dg_ref, dpb_ref, acc, sem) = rest[:7], rest[7:]

        @pl.when(pl.program_id(0) == 0)
        def _():
            acc[...] = jnp.zeros_like(acc)

        h = h_ref[...]
        for j, p_ref in enumerate(dp_refs):
            p = p_ref[...]
            dpb_ref[:, j * 512:(j + 1) * 512] = p
            acc[j * 512:(j + 1) * 512, :] += _dot_tn(p, h)

        @pl.when(pl.program_id(0) == n_steps - 1)
        def _():
            done = pltpu.make_async_copy(acc, dw_ref, sem)
            done.start()
            done.wait()

        dh = _dot(dpb_ref[...], w_ref[...])
        xhat, rstd = _rms(x_ref[...])
        g = g_ref[...]
        scale1 = 1.0 + mod_ref[:, D_MODEL:2 * D_MODEL]
        n1 = xhat * g

        @pl.when(pl.program_id(0) == 0)
        def _():
            dsh_ref[...] = jnp.zeros_like(dsh_ref)
            dsc_ref[...] = jnp.zeros_like(dsc_ref)
            dg_ref[...] = jnp.zeros_like(dg_ref)

        dsh_ref[...] += _rowsum(dh)
        dsc_ref[...] += _rowsum(dh * n1)
        dn = dh * scale1
        dg_ref[...] += _rowsum(dn * xhat)
        gx_ref[...] = dx_ref[...] + _rms_bwd(dn * g, xhat, rstd)

    vec = _whole((1, D_MODEL))
    return pl.pallas_call(
        body, name="in_bwd", grid=(n_steps,),
        out_shape=[jax.ShapeDtypeStruct((s, D_MODEL), F32), jax.ShapeDtypeStruct((IN_WIDTH, D_MODEL), F32)]
        + [jax.ShapeDtypeStruct((1, D_MODEL), F32)] * 3,
        in_specs=[_rows(tm, D_MODEL), _rows(tm, D_MODEL), _whole((1, 6 * D_MODEL)), vec,
                  pl.BlockSpec((IN_WIDTH, D_MODEL), lambda i: (0, 0), pipeline_mode=pl.Buffered(1)),
                  _rows(tm, D_MODEL)] + [_rows(tm, 512)] * 7,
        out_specs=[_rows(tm, D_MODEL)] + _ANY + [vec, vec, vec],
        scratch_shapes=[pltpu.VMEM((tm, IN_WIDTH), BF16), pltpu.VMEM((IN_WIDTH, D_MODEL), F32),
                        pltpu.SemaphoreType.DMA(())],
        compiler_params=_params("arbitrary", vmem=V7X_VMEM_MOST),
    )(x, dx1, mod, g1, w_in, h1, *dps)


HG_TILE = 512
HG_TILE_CHUNKS = HG_TILE // HG_CHUNK


def _lower_bound(lg_ref):
    return 1.0 / (1.0 + jnp.exp(lg_ref[1:2, :] - lg_ref[0:1, :]))


def _chunk_masks():
    r = lax.broadcasted_iota(jnp.int32, (HG_CHUNK, HG_CHUNK), 0)
    c = lax.broadcasted_iota(jnp.int32, (HG_CHUNK, HG_CHUNK), 1)
    return r >= c, c >= r, (r >= c).astype(F32), (c >= r).astype(F32)


def _hg_fwd_tile(q_ref, f_ref, i_ref, g_ref, lg_ref, og_ref, out_ref, o_ref, st_ref, state, qf_s, kk_s, lf_s):
    @pl.when(pl.program_id(0) == 0)
    def _():
        state[...] = jnp.zeros_like(state)

    lb = _lower_bound(lg_ref)
    f = lb + (1.0 - lb) * _sigmoid(f_ref[...])
    kk_s[...] = 1.0 - f
    lf_s[...] = jnp.log(f)
    qf_s[...] = _silu(q_ref[...])
    causal, _, tri, _ = _chunk_masks()

    def chunk(ci, carry):
        rows = pl.ds(pl.multiple_of(ci * HG_CHUNK, HG_CHUNK), HG_CHUNK)
        srows = pl.ds(pl.multiple_of(ci * HG_DIM, HG_DIM), HG_DIM)
        lf = lf_s[rows, :]
        b = _dot_f32(tri, lf)
        bl = _rowsum(lf)
        ref = 0.5 * bl
        qf, kk, v = qf_s[rows, :], kk_s[rows, :], i_ref[rows, :]
        a_in = (qf * jnp.exp(b)).astype(BF16)
        a_t = (qf * jnp.exp(b - ref)).astype(BF16)
        b_t = (kk * jnp.exp(ref - b)).astype(BF16)
        kd = kk * jnp.exp(bl - b)
        ebl = jnp.exp(bl)
        vb = v.astype(BF16)
        for h in range(HG_HEADS):
            c = slice(h * HG_DIM, (h + 1) * HG_DIM)
            st = state[h]
            st_ref[srows, c] = st
            p = jnp.where(causal, _dot_nt(a_t[:, c], b_t[:, c]), 0.0)
            o_ref[rows, c] = _dot(p.astype(BF16), vb[:, c]) + _dot_nt(a_in[:, c], st.astype(BF16))
            state[h] = st * ebl[:, c] + _dot_tn(vb[:, c], kd[:, c].astype(BF16))
        return carry

    lax.fori_loop(0, HG_TILE_CHUNKS, chunk, 0, unroll=True)
    for h in range(HG_HEADS):
        c = slice(h * HG_DIM, (h + 1) * HG_DIM)
        ohat, _ = _rms(o_ref[:, c])
        out_ref[:, c] = (ohat * og_ref[...] * _silu(g_ref[:, c])).astype(BF16)


def _hg_bwd(hq, hf, hi, hgt, logits, onorm_g, o, states, dout):
    s = hq.shape[0]
    n_tiles = s // HG_TILE

    def body(q_ref, f_ref, i_ref, g_ref, lg_ref, og_ref, o_ref, st_ref, d_ref,
             dq_ref, df_ref, di_ref, dg_ref, dog_ref, dlb_ref, dstate, qf_s, kk_s, lf_s, do_s):
        @pl.when(pl.program_id(0) == 0)
        def _():
            dstate[...] = jnp.zeros_like(dstate)
            dog_ref[...] = jnp.zeros_like(dog_ref)
            dlb_ref[...] = jnp.zeros_like(dlb_ref)

        og = og_ref[...]
        dog = jnp.zeros((1, HG_DIM), F32)
        for h in range(HG_HEADS):
            c = slice(h * HG_DIM, (h + 1) * HG_DIM)
            ohat, rstd = _rms(o_ref[:, c])
            gate = g_ref[:, c]
            d = d_ref[:, c]
            dg_ref[:, c] = (d * (ohat * og) * _dsilu(gate)).astype(BF16)
            dnormed = d * _silu(gate)
            dog += _rowsum(dnormed * ohat)
            do_s[:, c] = _rms_bwd(dnormed * og, ohat, rstd)
        dog_ref[...] += dog

        lb = _lower_bound(lg_ref)
        f = lb + (1.0 - lb) * _sigmoid(f_ref[...])
        kk_s[...] = 1.0 - f
        lf_s[...] = jnp.log(f)
        qf_s[...] = _silu(q_ref[...])
        causal, upper, tri, tri_t = _chunk_masks()

        def chunk(step, carry):
            ci = HG_TILE_CHUNKS - 1 - step
            rows = pl.ds(pl.multiple_of(ci * HG_CHUNK, HG_CHUNK), HG_CHUNK)
            srows = pl.ds(pl.multiple_of(ci * HG_DIM, HG_DIM), HG_DIM)
            lf = lf_s[rows, :]
            b = _dot_f32(tri, lf)
            bl = _rowsum(lf)
            ref = 0.5 * bl
            qf, kk, v, do = qf_s[rows, :], kk_s[rows, :], i_ref[rows, :], do_s[rows, :]
            eb, ebr, erb, ekd, ebl = jnp.exp(b), jnp.exp(b - ref), jnp.exp(ref - b), jnp.exp(bl - b), jnp.exp(bl)
            a_in, a_t, b_t, kd = qf * eb, qf * ebr, kk * erb, kk * ekd
            for h in range(HG_HEADS):
                c = slice(h * HG_DIM, (h + 1) * HG_DIM)
                st, dst = st_ref[srows, c], dstate[h]
                stb, dstb = st.astype(BF16), dst.astype(BF16)
                doh, vh = do[:, c], v[:, c]
                dob, vb = doh.astype(BF16), vh.astype(BF16)
                ain_h, at_h, bt_h, kd_h = a_in[:, c], a_t[:, c], b_t[:, c], kd[:, c]
                atb, btb = at_h.astype(BF16), bt_h.astype(BF16)
                d_ain = _dot(dob, stb)
                p_t = jnp.where(upper, _dot_nt(btb, atb), 0.0).astype(BF16)
                dp = jnp.where(causal, _dot_nt(dob, vb), 0.0).astype(BF16)
                dp_t = jnp.where(upper, _dot_nt(vb, dob), 0.0).astype(BF16)
                di_ref[rows, c] = (_dot(p_t, dob) + _dot_nt(kd_h.astype(BF16), dstb)).astype(BF16)
                d_at = _dot(dp, btb)
                d_bt = _dot(dp_t, atb)
                d_kd = _dot(vb, dstb)
                dqf = d_ain * eb[:, c] + d_at * ebr[:, c]
                dkk = d_bt * erb[:, c] + d_kd * ekd[:, c]
                db = d_ain * ain_h + d_at * atb.astype(F32) - d_bt * btb.astype(F32) - d_kd * kd_h
                dbl = _rowsum(d_kd * kd_h) + _rowsum(dst * st) * ebl[:, c]
                dstate[h] = _dot_tn(dob, ain_h.astype(BF16)) + dst * ebl[:, c]
                dlf = _dot_f32(tri_t, db) + dbl
                qv, fr = q_ref[rows, c], f_ref[rows, c]
                lbh = lb[:, c]
                sg = _sigmoid(fr)
                dfv = dlf / (lbh + (1.0 - lbh) * sg) - dkk
                df_ref[rows, c] = (dfv * (1.0 - lbh) * sg * (1.0 - sg)).astype(BF16)
                dlb_ref[:, c] += _rowsum(dfv * (1.0 - sg))
                dq_ref[rows, c] = (dqf * _dsilu(qv)).astype(BF16)
            return carry

        lax.fori_loop(0, HG_TILE_CHUNKS, chunk, 0, unroll=True)

    rev = pl.BlockSpec((HG_TILE, HG_WIDTH), lambda i: (n_tiles - 1 - i, 0))
    return pl.pallas_call(
        body, name="hg_bwd", grid=(n_tiles,),
        out_shape=[jax.ShapeDtypeStruct((s, HG_WIDTH), BF16)] * 4
        + [jax.ShapeDtypeStruct((1, HG_DIM), F32), jax.ShapeDtypeStruct((1, HG_WIDTH), F32)],
        in_specs=[rev] * 4 + [_whole((2, HG_WIDTH)), _whole((1, HG_DIM)), rev,
                              pl.BlockSpec((HG_TILE_CHUNKS * HG_DIM, HG_WIDTH), lambda i: (n_tiles - 1 - i, 0)), rev],
        out_specs=[rev] * 4 + [_whole((1, HG_DIM)), _whole((1, HG_WIDTH))],
        scratch_shapes=[pltpu.VMEM((HG_HEADS, HG_DIM, HG_DIM), F32)] + [pltpu.VMEM((HG_TILE, HG_WIDTH), F32)] * 4,
        compiler_params=_params("arbitrary"),
    )(hq, hf, hi, hgt, logits, onorm_g, o, states, dout)


TOKEN_GROUP = 16


def _att_geometry(dil, seq=0):
    per_group = TOKEN_GROUP // dil
    ub = ATT_BLOCK // per_group
    if dil == TOKEN_GROUP:
        n_blocks = 2 if seq % (2 * ub * TOKEN_GROUP) == 0 and seq > 0 else 1
    else:
        n_blocks = 4
    return per_group, ub, ATT_WIDTH if dil == 1 else 128, n_blocks


def _att_consts(dil):
    per_group, ub = _att_geometry(dil)[:2]

    def pos(i):
        return i if dil == 1 else (i % ub) * per_group + i // ub

    lane = lax.broadcasted_iota(jnp.int32, (ATT_BLOCK, 128), 1)
    qi = pos(lax.broadcasted_iota(jnp.int32, (2 * ATT_BLOCK, ATT_BLOCK), 0) % ATT_BLOCK)
    kj = pos(lax.broadcasted_iota(jnp.int32, (2 * ATT_BLOCK, ATT_BLOCK), 1))
    return lane < ATT_HEAD_DIM, kj <= qi, lambda off: kj >= qi + off


def _load_tile(ref, dil, r, c, base=0):
    per_group, ub = _att_geometry(dil)[:2]
    if dil == 1:
        return ref[base:base + ATT_BLOCK, c]
    return jnp.concatenate([ref[pl.ds(base + dil * w + r, ub, stride=TOKEN_GROUP), c] for w in range(per_group)], axis=0)


def _store_tile(ref, dil, r, c, val, base=0):
    per_group, ub = _att_geometry(dil)[:2]
    if dil == 1:
        ref[base:base + ATT_BLOCK, c] = val
        return
    for w in range(per_group):
        ref[pl.ds(base + dil * w + r, ub, stride=TOKEN_GROUP), c] = val[w * ub:(w + 1) * ub]


def _stack_heads(x2, first):
    return jnp.concatenate([jnp.where(first, x2, 0.0), jnp.where(first, 0.0, x2)], axis=0)


def _stack_bcast(x2, first):
    other = pltpu.roll(x2, ATT_HEAD_DIM, axis=1)
    return jnp.concatenate([jnp.where(first, x2, other), jnp.where(first, other, x2)], axis=0)


def _unstack_heads(st, first):
    return jnp.where(first, st[:ATT_BLOCK], st[ATT_BLOCK:])


def _att_fwd(q, k, v, dil, behind=()):
    seq, width = q.shape
    _, ub, lanes, nbs = _att_geometry(dil, seq)
    rows = ub * TOKEN_GROUP
    n_steps = seq // (nbs * rows)

    def body(q_ref, k_ref, v_ref, kp_ref, vp_ref, o_ref, lse_ref):
        first, cur_ok, _band = _att_consts(dil)
        inner_ok = _band(0)
        edge_ok = _band(jnp.where(pl.program_id(0) > 0, 0, ATT_BLOCK))
        for r in range(dil):
            for j in range(lanes // 128):
                c = slice(j * 128, (j + 1) * 128)
                kc = vc = None
                for b in range(nbs):
                    base = b * rows
                    prev_ok = edge_ok if b == 0 else inner_ok
                    if b == 0:
                        kp, vp = _load_tile(kp_ref, dil, r, c).astype(BF16), _load_tile(vp_ref, dil, r, c).astype(BF16)
                    else:
                        kp, vp = kc, vc
                    qst = _stack_heads(_load_tile(q_ref, dil, r, c, base) * ATT_SCALE, first).astype(BF16)
                    kc = _load_tile(k_ref, dil, r, c, base).astype(BF16)
                    vc = _load_tile(v_ref, dil, r, c, base).astype(BF16)
                    sc = jnp.where(cur_ok, _dot_nt(qst, kc), NEG)
                    sp = jnp.where(prev_ok, _dot_nt(qst, kp), NEG)
                    mx = jnp.max(jnp.maximum(sc, sp), axis=-1, keepdims=True)
                    pc, pp = jnp.exp(sc - mx), jnp.exp(sp - mx)
                    den = jnp.sum(pc + pp, axis=-1, keepdims=True)
                    ost = (_dot(pc.astype(BF16), vc) + _dot(pp.astype(BF16), vp)) / den
                    lse = jnp.broadcast_to(mx + jnp.log(den), (2 * ATT_BLOCK, 128))
                    _store_tile(o_ref, dil, r, c, _unstack_heads(ost, first), base)
                    _store_tile(lse_ref, dil, r, c, _unstack_heads(lse, first), base)

    slab = pl.BlockSpec((nbs * rows, lanes), lambda n, j: (n, j))
    before = pl.BlockSpec((rows, lanes), lambda n, j: (jnp.maximum(n * nbs - 1, 0), j))
    return pl.pallas_call(
        _behind(body, behind), name=f"att_fwd_d{dil}", grid=(n_steps, width // lanes),
        out_shape=[jax.ShapeDtypeStruct((seq, width), F32)] * 2,
        in_specs=_ANY * len(behind) + [slab, slab, slab, before, before], out_specs=[slab, slab],
        compiler_params=_params("arbitrary", "arbitrary"),
    )(*behind, q, k, v, k, v)


def _att_bwd(q, k, v, do, cc, lse, dil, behind=(), plus=()):
    seq, width = q.shape
    assert not plus or dil == 1
    plus = [a for triple in plus for a in triple]
    _, ub, lanes, nbs = _att_geometry(dil, seq)
    rows = ub * TOKEN_GROUP
    n_blocks = seq // rows
    n_steps = n_blocks // nbs

    def body(q_ref, k_ref, v_ref, do_ref, cc_ref, lse_ref, qx_ref, dox_ref, ccx_ref, lsex_ref, *rest):
        plus_refs, (dq_ref, dk_ref, dv_ref, carry) = rest[:len(plus)], rest[len(plus):]
        first, cur_ok, _band = _att_consts(dil)
        step = pl.program_id(1)
        inner_ok = _band(0)
        edge_ok = _band(jnp.where(step < n_steps - 1, 0, ATT_BLOCK))

        @pl.when(step == 0)
        def _():
            carry[...] = jnp.zeros_like(carry)

        def queries(refs, r, c, base):
            q_r, do_r, lse_r, cc_r = refs
            return (_stack_heads(_load_tile(q_r, dil, r, c, base) * ATT_SCALE, first).astype(BF16),
                    _stack_heads(_load_tile(do_r, dil, r, c, base), first).astype(BF16),
                    _stack_bcast(_load_tile(lse_r, dil, r, c, base), first),
                    _stack_bcast(_load_tile(cc_r, dil, r, c, base), first))

        for r in range(dil):
            for j in range(lanes // 128):
                c = slice(j * 128, (j + 1) * 128)
                own = queries((q_ref, do_ref, lse_ref, cc_ref), r, c, 0)
                left = _load_tile(carry, dil, r, c)
                for b in range(nbs):
                    base = b * rows
                    last = b == nbs - 1
                    next_ok = edge_ok if last else inner_ok
                    if last:
                        following = queries((qx_ref, dox_ref, lsex_ref, ccx_ref), r, c, 0)
                    else:
                        following = queries((q_ref, do_ref, lse_ref, cc_ref), r, c, base + rows)
                    (qst, dost, lse_n, cc_n), (qxst, doxst, lse_x, cc_x) = own, following
                    kb = _load_tile(k_ref, dil, r, c, base).astype(BF16)
                    vb = _load_tile(v_ref, dil, r, c, base).astype(BF16)
                    p_cur = jnp.exp(jnp.where(cur_ok, _dot_nt(qst, kb), NEG) - lse_n)
                    p_next = jnp.exp(jnp.where(next_ok, _dot_nt(qxst, kb), NEG) - lse_x)
                    ds_cur = (p_cur * (_dot_nt(dost, vb) + cc_n)).astype(BF16)
                    ds_next = (p_next * (_dot_nt(doxst, vb) + cc_x)).astype(BF16)
                    dq_own = (left + _unstack_heads(_dot(ds_cur, kb), first)) * ATT_SCALE
                    dk_own = _dot_tn(ds_cur, qst) + _dot_tn(ds_next, qxst)
                    dv_own = _dot_tn(p_cur.astype(BF16), dost) + _dot_tn(p_next.astype(BF16), doxst)
                    for i, (out_ref, val) in enumerate(zip((dq_ref, dk_ref, dv_ref), (dq_own, dk_own, dv_own))):
                        for other in plus_refs[i::3]:
                            val = val + _load_tile(other, dil, r, c, base)
                        _store_tile(out_ref, dil, r, c, val.astype(out_ref.dtype), base)
                    left = _unstack_heads(_dot(ds_next, kb), first)
                    own = following
                _store_tile(carry, dil, r, c, left)

    slab = pl.BlockSpec((nbs * rows, lanes), lambda j, n: (n, j))
    after = pl.BlockSpec((rows, lanes), lambda j, n: (jnp.minimum((n + 1) * nbs, n_blocks - 1), j))
    return pl.pallas_call(
        _behind(body, behind), name=f"att_bwd_d{dil}", grid=(width // lanes, n_steps),
        out_shape=[jax.ShapeDtypeStruct((seq, width), BF16 if plus else F32)] * 3,
        in_specs=_ANY * len(behind) + [slab] * 6 + [after] * 4 + [slab] * len(plus), out_specs=[slab] * 3,
        scratch_shapes=[pltpu.VMEM((rows, lanes), F32)],
        compiler_params=_params("arbitrary", "arbitrary"),
    )(*behind, q, k, v, do, cc, lse, q, do, cc, lse, *plus)


def _branch_weights(lses):
    mx = jnp.maximum(jnp.maximum(lses[0], lses[1]), lses[2])
    es = [jnp.exp(l - mx) for l in lses]
    inv = 1.0 / (es[0] + es[1] + es[2])
    return [e * inv for e in es]


def _att_combine_bwd_tile(d, att, lses, g):
    ahat, rstd = _rms(att)
    datt = _rms_bwd(d * g, ahat, rstd)
    hi = lax.broadcasted_iota(jnp.int32, (ATT_WIDTH, ATT_WIDTH), 0) // ATT_HEAD_DIM
    hj = lax.broadcasted_iota(jnp.int32, (ATT_WIDTH, ATT_WIDTH), 1) // ATT_HEAD_DIM
    same_head = (hi == hj).astype(BF16)
    prod = datt * att
    prod_hi = prod.astype(BF16)
    prod_lo = (prod - prod_hi.astype(F32)).astype(BF16)
    head_sum = _dot(prod_hi, same_head) + _dot(prod_lo, same_head)
    ws = _branch_weights(lses)
    return [w * datt for w in ws], [-w * head_sum for w in ws], _rowsum(d * ahat)


def _out_bwd(dx1, hg, at, mod, w_out, att, lses, att_g):
    s = dx1.shape[0]
    tm = 512
    n_steps = s // tm

    def body(dx_ref, hg_ref, at_ref, mod_ref, w_ref, att_ref, l0, l1, l2, g_ref,
             dhg_ref, do0, do1, do2, cc0, cc1, cc2, dw_ref, dwb_ref, dgate_ref, dg_ref):
        @pl.when(pl.program_id(0) == 0)
        def _():
            dw_ref[...] = jnp.zeros_like(dw_ref)
            dgate_ref[...] = jnp.zeros_like(dgate_ref)
            dg_ref[...] = jnp.zeros_like(dg_ref)

        hg, at, dx = hg_ref[...], at_ref[...], dx_ref[...]
        mix = _dot(hg, w_ref[0:512, :]) + _dot(at, w_ref[512:1024, :])
        dgate_ref[...] += _rowsum(dx * mix)
        dmix = (mod_ref[:, 2 * D_MODEL:3 * D_MODEL] * dx).astype(BF16)
        dhg_ref[...] = _dot_nt(dmix, w_ref[0:512, :])
        dos, ccs, dg_rows = _att_combine_bwd_tile(_dot_nt(dmix, w_ref[512:1024, :]), att_ref[...],
                                                  [l0[...], l1[...], l2[...]], g_ref[...])
        for val, ref in zip(dos + ccs, (do0, do1, do2, cc0, cc1, cc2)):
            ref[...] = val
        dg_ref[...] += dg_rows
        dw_ref[0:512, :] += _dot_tn(hg, dmix)
        dw_ref[512:1024, :] += _dot_tn(at, dmix)

        @pl.when(pl.program_id(0) == n_steps - 1)
        def _():
            dwb_ref[...] = dw_ref[...].astype(BF16)

    tile = _rows(tm, 512)
    square = _whole((D_MODEL, D_MODEL))
    return pl.pallas_call(
        body, name="out_bwd", grid=(n_steps,),
        out_shape=[jax.ShapeDtypeStruct((s, 512), F32)] * 7
        + [jax.ShapeDtypeStruct((D_MODEL, D_MODEL), F32), jax.ShapeDtypeStruct((D_MODEL, D_MODEL), BF16),
           jax.ShapeDtypeStruct((1, D_MODEL), F32), jax.ShapeDtypeStruct((1, ATT_WIDTH), F32)],
        in_specs=[_rows(tm, D_MODEL), tile, tile, _whole((1, 6 * D_MODEL)), square] + [tile] * 4 + [_whole((1, ATT_WIDTH))],
        out_specs=[tile] * 7 + [square, square, _whole((1, D_MODEL)), _whole((1, ATT_WIDTH))],
        compiler_params=_params("arbitrary"),
    )(dx1, hg, at, mod, w_out, att, *lses, att_g)


def _ffn(x, hg, outs, lses, att_g, target, mod, g2, gf, w_out, w_gu, w_down, after=()):
    s = x.shape[0]
    tm = 256

    def body(x_ref, hg_ref, o0, o1, o2, l0, l1, l2, ag_ref, t_ref, mod_ref, g2_ref, gf_ref, wo_ref, wgu_hbm, wd_hbm,
             dx_ref, h2_ref, act_ref, dau_ref, dff_ref, sums_ref, loss_ref, att_ref, at_ref, wgu, wd, au_s, sem):
        @pl.when(pl.program_id(0) == 0)
        def _():
            c1 = pltpu.make_async_copy(wgu_hbm, wgu, sem.at[0])
            c2 = pltpu.make_async_copy(wd_hbm, wd, sem.at[1])
            c1.start()
            c2.start()
            c1.wait()
            c2.wait()
            sums_ref[...] = jnp.zeros_like(sums_ref)
            loss_ref[...] = jnp.zeros_like(loss_ref)

        ws = _branch_weights([l0[...], l1[...], l2[...]])
        att = ws[0] * o0[...] + ws[1] * o1[...] + ws[2] * o2[...]
        att_ref[...] = att
        ahat, _ = _rms(att)
        at = (ahat * ag_ref[...]).astype(BF16)
        at_ref[...] = at
        mix = _dot(hg_ref[...], wo_ref[0:512, :]) + _dot(at, wo_ref[512:1024, :])
        x1v = x_ref[...] + mod_ref[:, 2 * D_MODEL:3 * D_MODEL] * mix
        xhat, rstd = _rms(x1v)
        g2 = g2_ref[...]
        n2 = xhat * g2
        scale2 = 1.0 + mod_ref[:, 4 * D_MODEL:5 * D_MODEL]
        gate2 = mod_ref[:, 5 * D_MODEL:6 * D_MODEL]
        hb = (n2 * scale2 + mod_ref[:, 3 * D_MODEL:4 * D_MODEL]).astype(BF16)
        h2_ref[...] = hb
        au_s[...] = _dot_nt(hb, wgu[...])
        a = au_s[:, 0:D_FF]
        act = (_silu(a) * au_s[:, D_FF:2 * D_FF]).astype(BF16)
        act_ref[...] = act
        ff = _dot(act, wd[...])
        x2 = x1v + gate2 * ff
        nf, rstd_f = _rms(x2)
        gfv = gf_ref[...]
        err = nf * gfv - t_ref[...]
        loss_ref[...] += 0.5 * jnp.sum(_rowsum(err * err), axis=-1, keepdims=True) * (1.0 / D_MODEL)
        dy = err * (1.0 / D_MODEL)
        dx2 = _rms_bwd(dy * gfv, nf, rstd_f)
        dffb = (gate2 * dx2).astype(BF16)
        dff_ref[...] = dffb
        dact = _dot_nt(dffb, wd[...])
        a = au_s[:, 0:D_FF]
        dau_ref[:, 0:D_FF] = (dact * au_s[:, D_FF:2 * D_FF] * _dsilu(a)).astype(BF16)
        dau_ref[:, D_FF:2 * D_FF] = (dact * _silu(a)).astype(BF16)
        dh = _dot(dau_ref[...], wgu[...])
        dn = dh * scale2
        sums_ref[0:1, :] += _rowsum(dh)
        sums_ref[1:2, :] += _rowsum(dh * n2)
        sums_ref[2:3, :] += _rowsum(dx2 * ff)
        sums_ref[3:4, :] += _rowsum(dn * xhat)
        sums_ref[4:5, :] += _rowsum(dy * nf)
        dx_ref[...] = dx2 + _rms_bwd(dn * g2, xhat, rstd)

    vec = _whole((1, D_MODEL))
    hbm = pl.BlockSpec(memory_space=pl.ANY)
    half = _rows(tm, 512)
    return pl.pallas_call(
        _behind(body, after), name="ffn", grid=(s // tm,),
        out_shape=[jax.ShapeDtypeStruct((s, D_MODEL), F32), jax.ShapeDtypeStruct((s, D_MODEL), BF16),
                   jax.ShapeDtypeStruct((s, D_FF), BF16), jax.ShapeDtypeStruct((s, 2 * D_FF), BF16),
                   jax.ShapeDtypeStruct((s, D_MODEL), BF16), jax.ShapeDtypeStruct((8, D_MODEL), F32),
                   jax.ShapeDtypeStruct((1, 128), F32), jax.ShapeDtypeStruct((s, ATT_WIDTH), F32),
                   jax.ShapeDtypeStruct((s, ATT_WIDTH), BF16)],
        in_specs=_ANY * len(after) + [_rows(tm, D_MODEL)] + [half] * 7 + [_whole((1, ATT_WIDTH)), _rows(tm, D_MODEL),
                                                                          _whole((1, 6 * D_MODEL)), vec, vec,
                                                                          _whole((D_MODEL, D_MODEL)), hbm, hbm],
        out_specs=[_rows(tm, D_MODEL), _rows(tm, D_MODEL), _rows(tm, D_FF), _rows(tm, 2 * D_FF), _rows(tm, D_MODEL),
                   _whole((8, D_MODEL)), _whole((1, 128)), half, half],
        scratch_shapes=[pltpu.VMEM((2 * D_FF, D_MODEL), BF16), pltpu.VMEM((D_FF, D_MODEL), BF16),
                        pltpu.VMEM((tm, 2 * D_FF), F32), pltpu.SemaphoreType.DMA((2,))],
        compiler_params=_params("arbitrary", vmem=V7X_VMEM_MOST),
    )(*after, x, hg, *outs, *lses, att_g, target, mod, g2, gf, w_out, w_gu, w_down)


def _weight_grad(a, b, name, rounded=False):
    s, m = a.shape
    n = b.shape[1]
    ts = min(s, 2048)
    n_steps = s // ts
    tm = max(t for t in range(128, m + 1, 128) if m % t == 0 and t * n * 4 <= 6 * 1024 * 1024)

    def body(a_ref, b_ref, o_ref, *ob_ref):
        @pl.when(pl.program_id(1) == 0)
        def _():
            o_ref[...] = jnp.zeros_like(o_ref)

        o_ref[...] += _dot_tn(a_ref[...], b_ref[...])
        if rounded:
            @pl.when(pl.program_id(1) == n_steps - 1)
            def _():
                ob_ref[0][...] = o_ref[...].astype(BF16)

    tile = pl.BlockSpec((tm, n), lambda j, i: (j, 0))
    return pl.pallas_call(
        body, name=name, grid=(m // tm, n_steps),
        out_shape=[jax.ShapeDtypeStruct((m, n), F32)] + [jax.ShapeDtypeStruct((m, n), BF16)] * rounded,
        in_specs=[pl.BlockSpec((ts, tm), lambda j, i: (i, j)), pl.BlockSpec((ts, n), lambda j, i: (i, 0))],
        out_specs=[tile] + [tile] * rounded,
        compiler_params=_params("parallel", "arbitrary"),
    )(a, b)


def _adamw_math(w, g, m, v):
    m = ADAM_B1 * m + (1.0 - ADAM_B1) * g
    v = ADAM_B2 * v + (1.0 - ADAM_B2) * (g * g)
    m_hat = m / (1.0 - ADAM_B1 ** ADAM_STEP)
    v_hat = v / (1.0 - ADAM_B2 ** ADAM_STEP)
    delta = -ADAM_LR * (m_hat / (jnp.sqrt(v_hat) + ADAM_EPS) + ADAM_WD * w)
    return delta, m, v


def _adamw_shard(where, w, m, v, partial, got, name, after=()):
    r, c = w.shape
    tr = _shard_rows(r)
    lead = partial.ndim - 2
    n_got = got.shape[0]

    def body(where_ref, *refs):
        w_ref, m_ref, v_ref, own_ref, *rest = refs[len(after):]
        got_refs, (grad_ref, d_ref, nm_ref, nv_ref) = rest[:n_got], rest[n_got:]
        g = own_ref[...]
        for g_ref in got_refs:
            g = g + g_ref[...].astype(F32)
        grad_ref[...] = g
        d_ref[...], nm_ref[...], nv_ref[...] = _adamw_math(w_ref[...], g, m_ref[...], v_ref[...])

    tile = pl.BlockSpec((tr, c), lambda i, where_ref: (i, 0))
    own = pl.BlockSpec((None,) * lead + (tr, c), lambda i, where_ref: (*[where_ref[d] for d in range(lead)], i, 0))
    part = [pl.BlockSpec((None, tr, c), functools.partial(lambda j, i, where_ref: (j, i, 0), j)) for j in range(n_got)]
    return pl.pallas_call(
        body, name=name,
        grid_spec=pltpu.PrefetchScalarGridSpec(num_scalar_prefetch=1, grid=(r // tr,),
                                               in_specs=_ANY * len(after) + [tile] * 3 + [own] + part, out_specs=[tile] * 4),
        out_shape=[jax.ShapeDtypeStruct((r, c), F32)] * 4, compiler_params=_params("parallel"),
    )(where, *after, w, m, v, partial, *[got] * n_got)


def _small_update(small_all, dmod_blocks, c_all, logits, w_ada, m_ada, v_ada, smalls, after=()):
    def body(sm_ref, dm_ref, c_ref, lg_ref, wa_ref, ma_ref, va_ref, *rest):
        ins, outs = rest[:21], rest[21:]
        _, me = _flip(0)
        tot = sm_ref[0:1, :]
        for i in range(1, N_DEV):
            tot = tot + sm_ref[i:i + 1, :]
        loss_ref = outs[0]
        loss_ref[...] = tot[:, SM_LOSS:SM_LOSS + 128]
        g_ada = lax.dot_general(_silu(c_ref[...]), dm_ref[me], (((0,), (0,)), ((), ())),
                                preferred_element_type=F32, precision=HIGHEST)
        outs[1][...] = g_ada
        outs[2][...], outs[3][...], outs[4][...] = _adamw_math(wa_ref[...], g_ada, ma_ref[...], va_ref[...])
        p0 = _lower_bound(lg_ref)
        dl0 = tot[:, SM_LB:SM_LB + 512] * p0 * (1.0 - p0)
        grads = [tot[:, SM_MOD:SM_MOD + 6 * D_MODEL], tot[:, SM_G1:SM_G1 + D_MODEL], tot[:, SM_G2:SM_G2 + D_MODEL],
                 tot[:, SM_GF:SM_GF + D_MODEL], tot[:, SM_ATT:SM_ATT + 512], tot[:, SM_HG:SM_HG + 128],
                 jnp.where(lax.broadcasted_iota(jnp.int32, (2, 512), 0) == 0, dl0, -dl0)]
        for i, g in enumerate(grads):
            w_ref, m_ref, v_ref = ins[3 * i:3 * i + 3]
            o = outs[5 + 4 * i:9 + 4 * i]
            o[0][...] = g
            o[1][...], o[2][...], o[3][...] = _adamw_math(w_ref[...], g, m_ref[...], v_ref[...])

    flat = [t for trio in smalls for t in trio]
    vm = pl.BlockSpec(memory_space=pltpu.VMEM)
    out_shape = [jax.ShapeDtypeStruct((1, 128), F32)] + [jax.ShapeDtypeStruct(w_ada.shape, F32)] * 4
    for trio in smalls:
        out_shape += [jax.ShapeDtypeStruct(trio[0].shape, F32)] * 4
    return pl.pallas_call(
        _behind(body, after), name="small_update", out_shape=out_shape,
        in_specs=_ANY * len(after) + [vm] * (7 + len(flat)), out_specs=[vm] * len(out_shape),
        compiler_params=pltpu.CompilerParams(vmem_limit_bytes=V7X_VMEM_LIMIT),
    )(*after, small_all, dmod_blocks, c_all, logits, w_ada, m_ada, v_ada, *flat)


def kernel(x, c, w_ada, b_ada, norm1_g, w_in, hg_lb_logits, hg_onorm_g, att_onorm_g, w_out, norm2_g, w_gate_up, w_down, final_g, loss_target, m_w_ada, m_b_ada, m_norm1_g, m_w_in, m_hg_lb_logits, m_hg_onorm_g, m_att_onorm_g, m_w_out, m_norm2_g, m_w_gate_up, m_w_down, m_final_g, v_w_ada, v_b_ada, v_norm1_g, v_w_in, v_hg_lb_logits, v_hg_onorm_g, v_att_onorm_g, v_w_out, v_norm2_g, v_w_gate_up, v_w_down, v_final_g):
    x2d, target = x[0], loss_target[0]
    seq = x2d.shape[0]
    assert seq % (ATT_BLOCK * max(DILATIONS)) == 0 and seq % HG_TILE == 0
    gf = final_g.reshape(1, D_MODEL)

    c_all = _exchange_small(c.reshape(8, D_MODEL // 8), None, "gather_c").reshape(N_DEV, D_MODEL)
    ada = _ada_rows(c_all, w_ada[0], b_ada)
    mod = _exchange_small(ada, 1, "scatter_mod").reshape(1, 6 * D_MODEL)

    core = lax.axis_index("c").astype(jnp.int32).reshape(1)
    chip = (2 * lax.axis_index("x") + lax.axis_index("y")).astype(jnp.int32).reshape(1)
    me = 4 * lax.axis_index("x") + 2 * lax.axis_index("y") + lax.axis_index("c")

    g_in, = _gather_weights([w_in[0].T.astype(BF16)])
    w_in_b = g_in.reshape(IN_WIDTH, D_MODEL)
    rest_shards = [w_out[0].astype(BF16), w_gate_up[0].T.astype(BF16), w_down[0].astype(BF16)]
    lands = [lax.empty((N_DEV,) + s.shape, BF16) for s in rest_shards]
    g_send, g_recv, g_srcs, g_lands, tok = _copies_start("gather_rest_start", _plan_gather_own, 12, rest_shards, lands, [w_in_b, mod])
    flight = {}

    def stage(name, *vals):
        if name == "attention_begun":
            flight["shards"], got = _copies_wait("gather_rest_wait", _plan_gather_own, g_send, g_recv, g_srcs, g_lands, list(vals))
            flight["pass"] = _copies_start("gather_pass_start", _plan_gather_pass, 9, [], got, [])
            return [flight["pass"][4]]
        if name == "mixer_weights_done":
            shapes = [(4, 2, D_MODEL // N_DEV, D_MODEL), (4, 2, 2 * D_FF // N_DEV, D_MODEL), (4, 2, D_FF // N_DEV, D_MODEL)]
            flight["grads"] = [g32.reshape(sh) for (g32, _), sh in zip(vals, shapes)]
            rounded = [g16.reshape(sh) for (_, g16), sh in zip(vals, shapes)]
            direct_lands = [lax.empty((N_DEV - 1,) + sh[2:], BF16) for sh in shapes]
            flight["direct"] = _copies_start("reduce_rest_start", _plan_reduce_direct, 21, rounded, direct_lands, [])
            return [flight["direct"][4]]
        raise ValueError(name)

    def rest_weights(after):
        s, r, _, p_lands, _ = flight["pass"]
        _, got = _copies_wait("gather_pass_wait", _plan_gather_pass, s, r, [], p_lands, [after])
        full = [lax.dynamic_update_index_in_dim(g, shard, me, 0) for g, shard in zip(got, flight["shards"])]
        return full[0].reshape(D_MODEL, D_MODEL), full[1].reshape(2 * D_FF, D_MODEL), full[2].reshape(D_FF, D_MODEL)

    grad_x, dw_in, small = _block_step(x2d, target, mod, norm1_g, hg_lb_logits, hg_onorm_g, att_onorm_g, norm2_g, gf,
                                       w_in_b, rest_weights, stage, [tok])

    g_in8 = dw_in.reshape(4, 2, IN_WIDTH // N_DEV, D_MODEL)
    in_pairs = _copies_start("reduce_pairs_in_start", _plan_reduce_pairs, 4, [g_in8], [lax.empty((4,) + g_in8.shape[2:], F32)], [])
    s, r, srcs, d_lands, _ = flight["direct"]
    _, recv_rest = _copies_wait("reduce_rest_wait", _plan_reduce_direct, s, r, srcs, d_lands, [in_pairs[4]])
    small_rows = jnp.pad(small, ((0, 0), (0, SM_PADDED - SM_WIDTH))).reshape(SM_PADDED // 128, 128)
    small_all = _exchange_small(small_rows, None, "gather_small", [in_pairs[4]]).reshape(N_DEV, SM_PADDED)[:, :SM_WIDTH]
    in_grads, got_in = _copies_wait("reduce_pairs_in_wait", _plan_reduce_pairs, in_pairs[0], in_pairs[1], in_pairs[2], in_pairs[3],
                                    [small_all])
    in_s32, in_s16 = _pair_sum(core, in_grads[0], got_in[0], "pair_sum_in")
    in_chips = _copies_start("reduce_chips_in_start", _plan_reduce_chips, 3, [in_s16], [lax.empty((3,) + in_s16.shape[1:], BF16)], [])
    big, updated = {}, []
    rest_params = [("w_out", w_out, m_w_out, v_w_out), ("w_gate_up", w_gate_up, m_w_gate_up, v_w_gate_up), ("w_down", w_down, m_w_down, v_w_down)]
    mine = jnp.concatenate([chip, core])
    for (n, w, m, v), g32, got in zip(rest_params, flight["grads"], recv_rest):
        if n == "w_gate_up":
            outs4 = _adamw_shard(mine, w[0].T, m[0].T, v[0].T, g32, got, f"adamw_{n}", [in_chips[4]])
            big[n] = [t.T[None] for t in outs4]
        else:
            outs4 = _adamw_shard(mine, w[0], m[0], v[0], g32, got, f"adamw_{n}", [in_chips[4]])
            big[n] = [t[None] for t in outs4]
        updated.append(outs4[3])
    smalls = [(b_ada, m_b_ada, v_b_ada), (norm1_g, m_norm1_g, v_norm1_g), (norm2_g, m_norm2_g, v_norm2_g),
              (gf, m_final_g.reshape(1, D_MODEL), v_final_g.reshape(1, D_MODEL)),
              (att_onorm_g, m_att_onorm_g, v_att_onorm_g), (hg_onorm_g, m_hg_onorm_g, v_hg_onorm_g),
              (hg_lb_logits, m_hg_lb_logits, v_hg_lb_logits)]
    dmod_blocks = small_all[:, :6 * D_MODEL].reshape(N_DEV, N_DEV, 6 * D_MODEL // N_DEV).transpose(1, 0, 2)
    res = _small_update(small_all, dmod_blocks, c_all, hg_lb_logits, w_ada[0], m_w_ada[0], v_w_ada[0], smalls, [in_chips[4]])
    _, recv_in = _copies_wait("reduce_chips_in_wait", _plan_reduce_chips, in_chips[0], in_chips[1], in_chips[2], in_chips[3],
                              [res[0]] + updated)
    big["w_in"] = [t.T[None] for t in _adamw_shard(chip, w_in[0].T, m_w_in[0].T, v_w_in[0].T, in_s32, recv_in[0], "adamw_w_in")]
    loss = res[0][0, 0]
    ada4 = [t[None] for t in res[1:5]]
    sm4 = {n: list(res[5 + 4 * i:9 + 4 * i]) for i, n in enumerate(["b_ada", "norm1_g", "norm2_g", "final_g", "att", "hg", "lb"])}
    sm4["final_g"] = [t.reshape(D_MODEL) for t in sm4["final_g"]]

    order = [ada4, sm4["b_ada"], sm4["norm1_g"], big["w_in"], sm4["lb"], sm4["hg"], sm4["att"], big["w_out"], sm4["norm2_g"],
             big["w_gate_up"], big["w_down"], sm4["final_g"]]
    return (loss, grad_x[None], *[o[0] for o in order], *[o[1] for o in order], *[o[2] for o in order], *[o[3] for o in order])


def _block_step(x2d, target, mod, norm1_g, hg_lb_logits, hg_onorm_g, att_onorm_g, norm2_g, gf, w_in_b, rest_weights, stage,
                after=()):
    h1, hq, hf, hi, hgt, aq, ak, av, hg_out, hg_o, hg_states = _in_fwd(x2d, mod, norm1_g, w_in_b, hg_lb_logits, hg_onorm_g, after)
    branch = [_att_fwd(aq, ak, av, d) for d in DILATIONS[:2]]
    behind = stage("attention_begun", branch[0][0], branch[1][0])
    branch += [_att_fwd(aq, ak, av, d, behind) for d in DILATIONS[2:]]
    outs = [b[0] for b in branch]
    lses = [b[1] for b in branch]
    w_out_b, w_gu_b, w_down_b = rest_weights(outs[-1])

    dx1, h2, act, dau, dff, ffn_sums, loss_part, att, att_out = _ffn(
        x2d, hg_out, outs, lses, att_onorm_g, target, mod, norm2_g, gf, w_out_b, w_gu_b, w_down_b)
    dw_gu = _weight_grad(dau, h2, "dw_gate_up", rounded=True)
    dw_down = _weight_grad(act, dff, "dw_down", rounded=True)

    back = _out_bwd(dx1, hg_out, att_out, mod, w_out_b, att, lses, att_onorm_g)
    dhg, dos, ccs = back[0], back[1:4], back[4:7]
    dw_out, dw_out_b, dgate1, d_att_g = back[7:11]
    behind = stage("mixer_weights_done", (dw_out, dw_out_b), dw_gu, dw_down)
    dilated = [_att_bwd(aq, ak, av, dos[i], ccs[i], lses[i], d, behind) for i, d in enumerate(DILATIONS) if d > 1]
    datt = _att_bwd(aq, ak, av, dos[0], ccs[0], lses[0], DILATIONS[0], behind, plus=dilated)
    dhq, dhf, dhi, dhgt, d_hg_g, d_lb = _hg_bwd(hq, hf, hi, hgt, hg_lb_logits, hg_onorm_g, hg_o, hg_states, dhg)
    dps = [dhq, dhf, dhi, dhgt] + list(datt)
    grad_x, dw_in, dshift1, dscale1, d_g1 = _in_bwd(x2d, dx1, mod, norm1_g, w_in_b, h1, dps)
    small = jnp.concatenate([dshift1, dscale1, dgate1, ffn_sums[0:1], ffn_sums[1:2], ffn_sums[2:3], d_g1, ffn_sums[3:4],
                             ffn_sums[4:5], d_att_g, d_lb, d_hg_g, loss_part], axis=1)
    return grad_x, dw_in, small
```
